```python
import jax, jax.numpy as jnp
from jax import lax
import numpy as np

D_MODEL = 2048
BATCH = 8
SEQ = 2048
DEPTH = 1

CHUNK = 64
Q_BLOCK = 128
EPS = 1e-6

MLA_HEADS = 8
Q_LORA = 768
KV_LORA = 512
QK_NOPE = 128
QK_ROPE = 64
V_HEAD = 128
ROPE_THETA = 10000.0
MLA_WIDTH = 1024

SSM_HEADS = 16
SSM_HEAD_DIM = 64
SSM_INNER = 1024
SSM_GROUPS = 2
SSM_STATE = 128
SSM_CONV = 4
SSM_CONV_CH = 1536

MIX_WIDTH = 2048
IN_COLS = 3920

D_FF = 5632
FFN_CONV = 3

kernel_name = "hymba_mla_ssd_convffn_sandwich"


def rms_norm(x, g):
    xf = x.astype(jnp.float32)
    y = xf * lax.rsqrt(jnp.mean(xf * xf, axis=-1, keepdims=True) + EPS)
    return (y * g.astype(jnp.float32)).astype(x.dtype)


def causal_dwconv(x, w, b):
    K, C = w.shape
    y = lax.conv_general_dilated(
        x, w[:, None, :].astype(x.dtype), window_strides=(1,), padding=[(K - 1, 0)],
        dimension_numbers=('NWC', 'WIO', 'NWC'), feature_group_count=C)
    return y + b.astype(x.dtype)


def rope_tables(S):
    inv = 1.0 / (ROPE_THETA ** (jnp.arange(0, QK_ROPE, 2, dtype=jnp.float32) / QK_ROPE))
    ang = jnp.arange(S, dtype=jnp.float32)[:, None] * inv[None, :]
    return jnp.cos(ang), jnp.sin(ang)


def rotate(x, cos, sin):
    x1, x2 = jnp.split(x, 2, axis=-1)
    c = cos.astype(x.dtype)
    s = sin.astype(x.dtype)
    return jnp.concatenate([x1 * c - x2 * s, x1 * s + x2 * c], axis=-1)


def mla_mixer(c_q, c_kv, k_rope, q_norm_g, w_uq, kv_norm_g, w_ukv):
    Bsz, S, _ = c_q.shape
    q = (rms_norm(c_q, q_norm_g) @ w_uq).reshape(Bsz, S, MLA_HEADS, QK_NOPE + QK_ROPE)
    q_nope, q_rope = q[..., :QK_NOPE], q[..., QK_NOPE:]
    kv = (rms_norm(c_kv, kv_norm_g) @ w_ukv).reshape(Bsz, S, MLA_HEADS, QK_NOPE + V_HEAD)
    k_nope, v = kv[..., :QK_NOPE], kv[..., QK_NOPE:]
    cos, sin = rope_tables(S)
    q_rope = rotate(q_rope, cos[None, :, None, :], sin[None, :, None, :])
    k_rope = rotate(k_rope, cos[None], sin[None])
    scale = (QK_NOPE + QK_ROPE) ** -0.5
    nblk = S // Q_BLOCK
    key_chunk = jnp.arange(S) // CHUNK
    qn_b = q_nope.reshape(Bsz, nblk, Q_BLOCK, MLA_HEADS, QK_NOPE).transpose(1, 0, 2, 3, 4)
    qr_b = q_rope.reshape(Bsz, nblk, Q_BLOCK, MLA_HEADS, QK_ROPE).transpose(1, 0, 2, 3, 4)

    def block(args):
        qn, qr, i = args
        s = (jnp.einsum('bqhd,bkhd->bhqk', qn, k_nope)
             + jnp.einsum('bqhr,bkr->bhqk', qr, k_rope)).astype(jnp.float32) * scale
        q_chunk = (i * Q_BLOCK + jnp.arange(Q_BLOCK)) // CHUNK
        mask = key_chunk[None, :] <= q_chunk[:, None]
        s = jnp.where(mask, s, -1e30)
        p = jax.nn.softmax(s, axis=-1).astype(v.dtype)
        return jnp.einsum('bhqk,bkhd->bqhd', p, v)

    o = lax.map(block, (qn_b, qr_b, jnp.arange(nblk)))
    return o.transpose(1, 0, 2, 3, 4).reshape(Bsz, S, MLA_WIDTH)


def segsum(a):
    T = a.shape[-1]
    cs = jnp.cumsum(a, axis=-1)
    d = cs[..., :, None] - cs[..., None, :]
    return jnp.where(jnp.tril(jnp.ones((T, T), dtype=bool)), d, -jnp.inf)


def ssd_chunked(x, a, b, c):
    Bsz, S, H, P = x.shape
    N = b.shape[-1]
    nc = S // CHUNK
    x = x.reshape(Bsz, nc, CHUNK, H, P)
    b = b.reshape(Bsz, nc, CHUNK, H, N)
    c = c.reshape(Bsz, nc, CHUNK, H, N)
    a = a.reshape(Bsz, nc, CHUNK, H).transpose(0, 3, 1, 2)
    a_cs = jnp.cumsum(a, axis=-1)
    L = jnp.exp(segsum(a))
    y_diag = jnp.einsum('bclhn,bcshn,bhcls,bcshp->bclhp', c, b, L, x)
    decay_states = jnp.exp(a_cs[..., -1:] - a_cs)
    states = jnp.einsum('bclhn,bhcl,bclhp->bchpn', b, decay_states, x)
    chunk_decay = jnp.exp(a_cs[..., -1])

    def step(h, inp):
        st, dec = inp
        return h * dec[..., None, None] + st, h

    h0 = jnp.zeros((Bsz, H, P, N), x.dtype)
    _, states_in = lax.scan(step, h0, (states.transpose(1, 0, 2, 3, 4), chunk_decay.transpose(2, 0, 1)))
    states_in = states_in.transpose(1, 0, 2, 3, 4)
    y_off = jnp.einsum('bclhn,bchpn,bhcl->bclhp', c, states_in, jnp.exp(a_cs))
    return (y_diag + y_off).reshape(Bsz, S, H, P)


def ssd_mixer(z, xbc, dt, conv_w, conv_b, dt_bias, a_log, d_skip, norm_g):
    Bsz, S, _ = z.shape
    f32 = jnp.float32
    xbc = jax.nn.silu(causal_dwconv(xbc, conv_w, conv_b))
    gn = SSM_GROUPS * SSM_STATE
    xs = xbc[..., :SSM_INNER].reshape(Bsz, S, SSM_HEADS, SSM_HEAD_DIM).astype(f32)
    bm = xbc[..., SSM_INNER:SSM_INNER + gn].reshape(Bsz, S, SSM_GROUPS, SSM_STATE)
    cm = xbc[..., SSM_INNER + gn:].reshape(Bsz, S, SSM_GROUPS, SSM_STATE)
    rep = SSM_HEADS // SSM_GROUPS
    bh = jnp.repeat(bm, rep, axis=2).astype(f32)
    ch = jnp.repeat(cm, rep, axis=2).astype(f32)
    dtf = jax.nn.softplus(dt.astype(f32) + dt_bias.astype(f32))
    A = -jnp.exp(a_log.astype(f32))
    y = ssd_chunked(xs * dtf[..., None], dtf * A, bh, ch)
    y = y + xs * d_skip.astype(f32)[:, None]
    y = y.reshape(Bsz, S, SSM_INNER) * jax.nn.silu(z.astype(f32))
    yg = y.reshape(Bsz, S, SSM_GROUPS, SSM_INNER // SSM_GROUPS)
    yg = yg * lax.rsqrt(jnp.mean(yg * yg, axis=-1, keepdims=True) + EPS)
    y = yg.reshape(Bsz, S, SSM_INNER) * norm_g.astype(f32)
    return y.astype(z.dtype)


def conv_glu_ffn(h, w_gate, w_up, conv_w, conv_b, w_down):
    g = causal_dwconv(h @ w_gate, conv_w, conv_b)
    return (jax.nn.gelu(g, approximate=True) * (h @ w_up)) @ w_down


def hybrid_layer(x, mix_pre_g, w_in, q_norm_g, w_uq, kv_norm_g, w_ukv, ssm_conv_w, ssm_conv_b,
                 dt_bias, a_log, d_skip, ssm_norm_g, w_out, mix_post_g, ffn_pre_g, w_gate, w_up,
                 ffn_conv_w, ffn_conv_b, w_down, ffn_post_g):
    u = rms_norm(x, mix_pre_g) @ w_in
    cuts = np.cumsum([Q_LORA, KV_LORA, QK_ROPE, SSM_INNER, SSM_CONV_CH]).tolist()
    c_q, c_kv, k_rope, z, xbc, dt = jnp.split(u, cuts, axis=-1)
    a_out = mla_mixer(c_q, c_kv, k_rope, q_norm_g, w_uq, kv_norm_g, w_ukv)
    b_out = ssd_mixer(z, xbc, dt, ssm_conv_w, ssm_conv_b, dt_bias, a_log, d_skip, ssm_norm_g)
    mix = jnp.concatenate([a_out, b_out], axis=-1) @ w_out
    x = x + rms_norm(mix, mix_post_g)
    f = conv_glu_ffn(rms_norm(x, ffn_pre_g), w_gate, w_up, ffn_conv_w, ffn_conv_b, w_down)
    return x + rms_norm(f, ffn_post_g)


def _fwd_setup_inputs(seed: int = 0) -> dict:
    key = jax.random.key(seed)
    ks = jax.random.split(key, 24)
    f32 = jnp.float32

    def nrm(k, shape, fan_in):
        return jax.random.normal(k, shape, f32) * (fan_in ** -0.5)

    def gain(k, n):
        return 1.0 + 0.02 * jax.random.normal(k, (DEPTH, n), f32)

    dt0 = jnp.exp(jax.random.uniform(ks[9], (DEPTH, SSM_HEADS), f32, np.log(1e-3), np.log(1e-1)))
    return {
        "x": jax.random.normal(ks[0], (BATCH, SEQ, D_MODEL), f32),
        "mix_pre_g": gain(ks[1], D_MODEL),
        "w_in": nrm(ks[2], (DEPTH, D_MODEL, IN_COLS), D_MODEL),
        "q_norm_g": gain(ks[3], Q_LORA),
        "w_uq": nrm(ks[4], (DEPTH, Q_LORA, MLA_HEADS * (QK_NOPE + QK_ROPE)), Q_LORA),
        "kv_norm_g": gain(ks[5], KV_LORA),
        "w_ukv": nrm(ks[6], (DEPTH, KV_LORA, MLA_HEADS * (QK_NOPE + V_HEAD)), KV_LORA),
        "ssm_conv_w": nrm(ks[7], (DEPTH, SSM_CONV, SSM_CONV_CH), SSM_CONV),
        "ssm_conv_b": 0.01 * jax.random.normal(ks[8], (DEPTH, SSM_CONV_CH), f32),
        "dt_bias": dt0 + jnp.log(-jnp.expm1(-dt0)),
        "a_log": jnp.log(jax.random.uniform(ks[10], (DEPTH, SSM_HEADS), f32, 1.0, 16.0)),
        "d_skip": gain(ks[11], SSM_HEADS),
        "ssm_norm_g": gain(ks[12], SSM_INNER),
        "w_out": nrm(ks[13], (DEPTH, MIX_WIDTH, D_MODEL), MIX_WIDTH),
        "mix_post_g": gain(ks[14], D_MODEL),
        "ffn_pre_g": gain(ks[15], D_MODEL),
        "w_gate": nrm(ks[16], (DEPTH, D_MODEL, D_FF), D_MODEL),
        "w_up": nrm(ks[17], (DEPTH, D_MODEL, D_FF), D_MODEL),
        "ffn_conv_w": nrm(ks[18], (DEPTH, FFN_CONV, D_FF), FFN_CONV),
        "ffn_conv_b": 0.01 * jax.random.normal(ks[19], (DEPTH, D_FF), f32),
        "w_down": nrm(ks[20], (DEPTH, D_FF, D_MODEL), D_FF),
        "ffn_post_g": gain(ks[21], D_MODEL),
    }


def _fwd_reference(x, mix_pre_g, w_in, q_norm_g, w_uq, kv_norm_g, w_ukv, ssm_conv_w, ssm_conv_b,
              dt_bias, a_log, d_skip, ssm_norm_g, w_out, mix_post_g, ffn_pre_g, w_gate, w_up,
              ffn_conv_w, ffn_conv_b, w_down, ffn_post_g):
    for l in range(DEPTH):
        x = hybrid_layer(x, mix_pre_g[l], w_in[l], q_norm_g[l], w_uq[l], kv_norm_g[l], w_ukv[l],
                         ssm_conv_w[l], ssm_conv_b[l], dt_bias[l], a_log[l], d_skip[l], ssm_norm_g[l],
                         w_out[l], mix_post_g[l], ffn_pre_g[l], w_gate[l], w_up[l],
                         ffn_conv_w[l], ffn_conv_b[l], w_down[l], ffn_post_g[l])
    return x


import jax as _jax
import jax.numpy as _jnp

TWIN_FORMAT = 'train_step'
FWD_PARAMS = ['x', 'mix_pre_g', 'w_in', 'q_norm_g', 'w_uq', 'kv_norm_g', 'w_ukv', 'ssm_conv_w', 'ssm_conv_b', 'dt_bias', 'a_log', 'd_skip', 'ssm_norm_g', 'w_out', 'mix_post_g', 'ffn_pre_g', 'w_gate', 'w_up', 'ffn_conv_w', 'ffn_conv_b', 'w_down', 'ffn_post_g']
TWIN_WEIGHTS = ['mix_pre_g', 'w_in', 'q_norm_g', 'w_uq', 'kv_norm_g', 'w_ukv', 'ssm_conv_w', 'ssm_conv_b', 'dt_bias', 'a_log', 'd_skip', 'ssm_norm_g', 'w_out', 'mix_post_g', 'ffn_pre_g', 'w_gate', 'w_up', 'ffn_conv_w', 'ffn_conv_b', 'w_down', 'ffn_post_g']
TWIN_DIFF_INPUT = 'x'
TWIN_INPUTS = ['x', 'mix_pre_g', 'w_in', 'q_norm_g', 'w_uq', 'kv_norm_g', 'w_ukv', 'ssm_conv_w', 'ssm_conv_b', 'dt_bias', 'a_log', 'd_skip', 'ssm_norm_g', 'w_out', 'mix_post_g', 'ffn_pre_g', 'w_gate', 'w_up', 'ffn_conv_w', 'ffn_conv_b', 'w_down', 'ffn_post_g', 'loss_target', 'm_mix_pre_g', 'm_w_in', 'm_q_norm_g', 'm_w_uq', 'm_kv_norm_g', 'm_w_ukv', 'm_ssm_conv_w', 'm_ssm_conv_b', 'm_dt_bias', 'm_a_log', 'm_d_skip', 'm_ssm_norm_g', 'm_w_out', 'm_mix_post_g', 'm_ffn_pre_g', 'm_w_gate', 'm_w_up', 'm_ffn_conv_w', 'm_ffn_conv_b', 'm_w_down', 'm_ffn_post_g', 'v_mix_pre_g', 'v_w_in', 'v_q_norm_g', 'v_w_uq', 'v_kv_norm_g', 'v_w_ukv', 'v_ssm_conv_w', 'v_ssm_conv_b', 'v_dt_bias', 'v_a_log', 'v_d_skip', 'v_ssm_norm_g', 'v_w_out', 'v_mix_post_g', 'v_ffn_pre_g', 'v_w_gate', 'v_w_up', 'v_ffn_conv_w', 'v_ffn_conv_b', 'v_w_down', 'v_ffn_post_g']
TWIN_OUTPUTS = ['loss', 'grad_x', 'grad_mix_pre_g', 'grad_w_in', 'grad_q_norm_g', 'grad_w_uq', 'grad_kv_norm_g', 'grad_w_ukv', 'grad_ssm_conv_w', 'grad_ssm_conv_b', 'grad_dt_bias', 'grad_a_log', 'grad_d_skip', 'grad_ssm_norm_g', 'grad_w_out', 'grad_mix_post_g', 'grad_ffn_pre_g', 'grad_w_gate', 'grad_w_up', 'grad_ffn_conv_w', 'grad_ffn_conv_b', 'grad_w_down', 'grad_ffn_post_g', 'delta_mix_pre_g', 'delta_w_in', 'delta_q_norm_g', 'delta_w_uq', 'delta_kv_norm_g', 'delta_w_ukv', 'delta_ssm_conv_w', 'delta_ssm_conv_b', 'delta_dt_bias', 'delta_a_log', 'delta_d_skip', 'delta_ssm_norm_g', 'delta_w_out', 'delta_mix_post_g', 'delta_ffn_pre_g', 'delta_w_gate', 'delta_w_up', 'delta_ffn_conv_w', 'delta_ffn_conv_b', 'delta_w_down', 'delta_ffn_post_g', 'new_m_mix_pre_g', 'new_m_w_in', 'new_m_q_norm_g', 'new_m_w_uq', 'new_m_kv_norm_g', 'new_m_w_ukv', 'new_m_ssm_conv_w', 'new_m_ssm_conv_b', 'new_m_dt_bias', 'new_m_a_log', 'new_m_d_skip', 'new_m_ssm_norm_g', 'new_m_w_out', 'new_m_mix_post_g', 'new_m_ffn_pre_g', 'new_m_w_gate', 'new_m_w_up', 'new_m_ffn_conv_w', 'new_m_ffn_conv_b', 'new_m_w_down', 'new_m_ffn_post_g', 'new_v_mix_pre_g', 'new_v_w_in', 'new_v_q_norm_g', 'new_v_w_uq', 'new_v_kv_norm_g', 'new_v_w_ukv', 'new_v_ssm_conv_w', 'new_v_ssm_conv_b', 'new_v_dt_bias', 'new_v_a_log', 'new_v_d_skip', 'new_v_ssm_norm_g', 'new_v_w_out', 'new_v_mix_post_g', 'new_v_ffn_pre_g', 'new_v_w_gate', 'new_v_w_up', 'new_v_ffn_conv_w', 'new_v_ffn_conv_b', 'new_v_w_down', 'new_v_ffn_post_g']
TWIN_LEAF_KINDS = {'loss': 'loss', 'grad_x': 'grad_x', 'grad_mix_pre_g': 'grad_w', 'grad_w_in': 'grad_w', 'grad_q_norm_g': 'grad_w', 'grad_w_uq': 'grad_w', 'grad_kv_norm_g': 'grad_w', 'grad_w_ukv': 'grad_w', 'grad_ssm_conv_w': 'grad_w', 'grad_ssm_conv_b': 'grad_w', 'grad_dt_bias': 'grad_w', 'grad_a_log': 'grad_w', 'grad_d_skip': 'grad_w', 'grad_ssm_norm_g': 'grad_w', 'grad_w_out': 'grad_w', 'grad_mix_post_g': 'grad_w', 'grad_ffn_pre_g': 'grad_w', 'grad_w_gate': 'grad_w', 'grad_w_up': 'grad_w', 'grad_ffn_conv_w': 'grad_w', 'grad_ffn_conv_b': 'grad_w', 'grad_w_down': 'grad_w', 'grad_ffn_post_g': 'grad_w', 'delta_mix_pre_g': 'delta_w', 'delta_w_in': 'delta_w', 'delta_q_norm_g': 'delta_w', 'delta_w_uq': 'delta_w', 'delta_kv_norm_g': 'delta_w', 'delta_w_ukv': 'delta_w', 'delta_ssm_conv_w': 'delta_w', 'delta_ssm_conv_b': 'delta_w', 'delta_dt_bias': 'delta_w', 'delta_a_log': 'delta_w', 'delta_d_skip': 'delta_w', 'delta_ssm_norm_g': 'delta_w', 'delta_w_out': 'delta_w', 'delta_mix_post_g': 'delta_w', 'delta_ffn_pre_g': 'delta_w', 'delta_w_gate': 'delta_w', 'delta_w_up': 'delta_w', 'delta_ffn_conv_w': 'delta_w', 'delta_ffn_conv_b': 'delta_w', 'delta_w_down': 'delta_w', 'delta_ffn_post_g': 'delta_w', 'new_m_mix_pre_g': 'new_m', 'new_m_w_in': 'new_m', 'new_m_q_norm_g': 'new_m', 'new_m_w_uq': 'new_m', 'new_m_kv_norm_g': 'new_m', 'new_m_w_ukv': 'new_m', 'new_m_ssm_conv_w': 'new_m', 'new_m_ssm_conv_b': 'new_m', 'new_m_dt_bias': 'new_m', 'new_m_a_log': 'new_m', 'new_m_d_skip': 'new_m', 'new_m_ssm_norm_g': 'new_m', 'new_m_w_out': 'new_m', 'new_m_mix_post_g': 'new_m', 'new_m_ffn_pre_g': 'new_m', 'new_m_w_gate': 'new_m', 'new_m_w_up': 'new_m', 'new_m_ffn_conv_w': 'new_m', 'new_m_ffn_conv_b': 'new_m', 'new_m_w_down': 'new_m', 'new_m_ffn_post_g': 'new_m', 'new_v_mix_pre_g': 'new_v', 'new_v_w_in': 'new_v', 'new_v_q_norm_g': 'new_v', 'new_v_w_uq': 'new_v', 'new_v_kv_norm_g': 'new_v', 'new_v_w_ukv': 'new_v', 'new_v_ssm_conv_w': 'new_v', 'new_v_ssm_conv_b': 'new_v', 'new_v_dt_bias': 'new_v', 'new_v_a_log': 'new_v', 'new_v_d_skip': 'new_v', 'new_v_ssm_norm_g': 'new_v', 'new_v_w_out': 'new_v', 'new_v_mix_post_g': 'new_v', 'new_v_ffn_pre_g': 'new_v', 'new_v_w_gate': 'new_v', 'new_v_w_up': 'new_v', 'new_v_ffn_conv_w': 'new_v', 'new_v_ffn_conv_b': 'new_v', 'new_v_w_down': 'new_v', 'new_v_ffn_post_g': 'new_v'}


def _forward(args):
    return _fwd_reference(*[args[k] for k in FWD_PARAMS])


def _output_shape():
    out = _jax.eval_shape(lambda: _forward(_fwd_setup_inputs(0)))
    return out.shape, out.dtype

N_MICROBATCH = 1
ADAM_LR = 0.001
ADAM_B1 = 0.9
ADAM_B2 = 0.999
ADAM_EPS = 1e-08
ADAM_WD = 0.01
ADAM_STEP = 10
PER_EXAMPLE_BATCH_AXIS = {'x': 0, 'loss_target': 0}
SHARED_INPUTS = []
_WEIGHT_DTYPES = {'mix_pre_g': _jnp.float32, 'w_in': _jnp.float32, 'q_norm_g': _jnp.float32, 'w_uq': _jnp.float32, 'kv_norm_g': _jnp.float32, 'w_ukv': _jnp.float32, 'ssm_conv_w': _jnp.float32, 'ssm_conv_b': _jnp.float32, 'dt_bias': _jnp.float32, 'a_log': _jnp.float32, 'd_skip': _jnp.float32, 'ssm_norm_g': _jnp.float32, 'w_out': _jnp.float32, 'mix_post_g': _jnp.float32, 'ffn_pre_g': _jnp.float32, 'w_gate': _jnp.float32, 'w_up': _jnp.float32, 'ffn_conv_w': _jnp.float32, 'ffn_conv_b': _jnp.float32, 'w_down': _jnp.float32, 'ffn_post_g': _jnp.float32}
MOMENT_SCALE = {'mix_pre_g': 2.556463e-01, 'w_in': 1.747179e-01, 'q_norm_g': 3.405436e-02, 'w_uq': 2.534745e-02, 'kv_norm_g': 6.420925e-02, 'w_ukv': 3.005952e-02, 'ssm_conv_w': 2.251679e-01, 'ssm_conv_b': 4.137903e-01, 'dt_bias': 8.741639e-01, 'a_log': 9.039238e-01, 'd_skip': 1.140733e+00, 'ssm_norm_g': 3.419178e-01, 'w_out': 2.192411e-01, 'mix_post_g': 8.020674e+00, 'ffn_pre_g': 2.008261e-01, 'w_gate': 7.011531e-02, 'w_up': 1.034286e-01, 'ffn_conv_w': 9.207318e-02, 'ffn_conv_b': 1.632093e-01, 'w_down': 1.715827e-01, 'ffn_post_g': 8.019804e+00}


def _to_microbatches(a, axis):
    t = _jnp.moveaxis(a, axis, 0)
    t = t.reshape((N_MICROBATCH, t.shape[0] // N_MICROBATCH) + t.shape[1:])
    return _jnp.moveaxis(t, 1, axis + 1)


def setup_inputs(seed: int = 0) -> dict:
    inp = _fwd_setup_inputs(seed)
    key = _jax.random.fold_in(_jax.random.key(seed), 7919)
    shape, _ = _output_shape()
    out = dict(inp)
    out["loss_target"] = _jax.random.normal(_jax.random.fold_in(key, 0), shape, _jnp.float32)
    for i, name in enumerate(TWIN_WEIGHTS):
        w = inp[name].astype(_jnp.float32)
        if MOMENT_SCALE is None:
            s = _jnp.sqrt(_jnp.mean(_jnp.square(w)) + 1e-30)
        else:
            s = MOMENT_SCALE[name]
        km, kv = _jax.random.split(_jax.random.fold_in(key, i + 1))
        out[name] = w
        out["m_" + name] = s * _jax.random.normal(km, w.shape, _jnp.float32)
        out["v_" + name] = (s * s) * _jax.random.uniform(kv, w.shape, _jnp.float32, 0.5, 1.5)
    if N_MICROBATCH > 1:
        for name, axis in PER_EXAMPLE_BATCH_AXIS.items():
            out[name] = _to_microbatches(out[name], axis)
    return {'x': out['x'], 'mix_pre_g': out['mix_pre_g'], 'w_in': out['w_in'], 'q_norm_g': out['q_norm_g'], 'w_uq': out['w_uq'], 'kv_norm_g': out['kv_norm_g'], 'w_ukv': out['w_ukv'], 'ssm_conv_w': out['ssm_conv_w'], 'ssm_conv_b': out['ssm_conv_b'], 'dt_bias': out['dt_bias'], 'a_log': out['a_log'], 'd_skip': out['d_skip'], 'ssm_norm_g': out['ssm_norm_g'], 'w_out': out['w_out'], 'mix_post_g': out['mix_post_g'], 'ffn_pre_g': out['ffn_pre_g'], 'w_gate': out['w_gate'], 'w_up': out['w_up'], 'ffn_conv_w': out['ffn_conv_w'], 'ffn_conv_b': out['ffn_conv_b'], 'w_down': out['w_down'], 'ffn_post_g': out['ffn_post_g'], 'loss_target': out['loss_target'], 'm_mix_pre_g': out['m_mix_pre_g'], 'm_w_in': out['m_w_in'], 'm_q_norm_g': out['m_q_norm_g'], 'm_w_uq': out['m_w_uq'], 'm_kv_norm_g': out['m_kv_norm_g'], 'm_w_ukv': out['m_w_ukv'], 'm_ssm_conv_w': out['m_ssm_conv_w'], 'm_ssm_conv_b': out['m_ssm_conv_b'], 'm_dt_bias': out['m_dt_bias'], 'm_a_log': out['m_a_log'], 'm_d_skip': out['m_d_skip'], 'm_ssm_norm_g': out['m_ssm_norm_g'], 'm_w_out': out['m_w_out'], 'm_mix_post_g': out['m_mix_post_g'], 'm_ffn_pre_g': out['m_ffn_pre_g'], 'm_w_gate': out['m_w_gate'], 'm_w_up': out['m_w_up'], 'm_ffn_conv_w': out['m_ffn_conv_w'], 'm_ffn_conv_b': out['m_ffn_conv_b'], 'm_w_down': out['m_w_down'], 'm_ffn_post_g': out['m_ffn_post_g'], 'v_mix_pre_g': out['v_mix_pre_g'], 'v_w_in': out['v_w_in'], 'v_q_norm_g': out['v_q_norm_g'], 'v_w_uq': out['v_w_uq'], 'v_kv_norm_g': out['v_kv_norm_g'], 'v_w_ukv': out['v_w_ukv'], 'v_ssm_conv_w': out['v_ssm_conv_w'], 'v_ssm_conv_b': out['v_ssm_conv_b'], 'v_dt_bias': out['v_dt_bias'], 'v_a_log': out['v_a_log'], 'v_d_skip': out['v_d_skip'], 'v_ssm_norm_g': out['v_ssm_norm_g'], 'v_w_out': out['v_w_out'], 'v_mix_post_g': out['v_mix_post_g'], 'v_ffn_pre_g': out['v_ffn_pre_g'], 'v_w_gate': out['v_w_gate'], 'v_w_up': out['v_w_up'], 'v_ffn_conv_w': out['v_ffn_conv_w'], 'v_ffn_conv_b': out['v_ffn_conv_b'], 'v_w_down': out['v_w_down'], 'v_ffn_post_g': out['v_ffn_post_g']}


def _loss(weights, diff, rest, loss_target):
    with _jax.named_scope("forward"):
        args = {**rest, TWIN_DIFF_INPUT: diff, **{k: w.astype(_WEIGHT_DTYPES[k]) for k, w in weights.items()}}
        y = _forward(args)
    with _jax.named_scope("loss_head"):
        err = _jnp.square(y.astype(_jnp.float32) - loss_target)
        return 0.5 * _jnp.sum(_jnp.mean(err, axis=-1)) if err.ndim else 0.5 * err


def _adamw(w, g, m, v):
    m = ADAM_B1 * m + (1.0 - ADAM_B1) * g
    v = ADAM_B2 * v + (1.0 - ADAM_B2) * _jnp.square(g)
    m_hat = m / (1.0 - ADAM_B1 ** ADAM_STEP)
    v_hat = v / (1.0 - ADAM_B2 ** ADAM_STEP)
    delta = -ADAM_LR * (m_hat / (_jnp.sqrt(v_hat) + ADAM_EPS) + ADAM_WD * w)
    return delta, m, v


def reference(x, mix_pre_g, w_in, q_norm_g, w_uq, kv_norm_g, w_ukv, ssm_conv_w, ssm_conv_b, dt_bias, a_log, d_skip, ssm_norm_g, w_out, mix_post_g, ffn_pre_g, w_gate, w_up, ffn_conv_w, ffn_conv_b, w_down, ffn_post_g, loss_target, m_mix_pre_g, m_w_in, m_q_norm_g, m_w_uq, m_kv_norm_g, m_w_ukv, m_ssm_conv_w, m_ssm_conv_b, m_dt_bias, m_a_log, m_d_skip, m_ssm_norm_g, m_w_out, m_mix_post_g, m_ffn_pre_g, m_w_gate, m_w_up, m_ffn_conv_w, m_ffn_conv_b, m_w_down, m_ffn_post_g, v_mix_pre_g, v_w_in, v_q_norm_g, v_w_uq, v_kv_norm_g, v_w_ukv, v_ssm_conv_w, v_ssm_conv_b, v_dt_bias, v_a_log, v_d_skip, v_ssm_norm_g, v_w_out, v_mix_post_g, v_ffn_pre_g, v_w_gate, v_w_up, v_ffn_conv_w, v_ffn_conv_b, v_w_down, v_ffn_post_g):
    given = dict(x=x, mix_pre_g=mix_pre_g, w_in=w_in, q_norm_g=q_norm_g, w_uq=w_uq, kv_norm_g=kv_norm_g, w_ukv=w_ukv, ssm_conv_w=ssm_conv_w, ssm_conv_b=ssm_conv_b, dt_bias=dt_bias, a_log=a_log, d_skip=d_skip, ssm_norm_g=ssm_norm_g, w_out=w_out, mix_post_g=mix_post_g, ffn_pre_g=ffn_pre_g, w_gate=w_gate, w_up=w_up, ffn_conv_w=ffn_conv_w, ffn_conv_b=ffn_conv_b, w_down=w_down, ffn_post_g=ffn_post_g, loss_target=loss_target, m_mix_pre_g=m_mix_pre_g, m_w_in=m_w_in, m_q_norm_g=m_q_norm_g, m_w_uq=m_w_uq, m_kv_norm_g=m_kv_norm_g, m_w_ukv=m_w_ukv, m_ssm_conv_w=m_ssm_conv_w, m_ssm_conv_b=m_ssm_conv_b, m_dt_bias=m_dt_bias, m_a_log=m_a_log, m_d_skip=m_d_skip, m_ssm_norm_g=m_ssm_norm_g, m_w_out=m_w_out, m_mix_post_g=m_mix_post_g, m_ffn_pre_g=m_ffn_pre_g, m_w_gate=m_w_gate, m_w_up=m_w_up, m_ffn_conv_w=m_ffn_conv_w, m_ffn_conv_b=m_ffn_conv_b, m_w_down=m_w_down, m_ffn_post_g=m_ffn_post_g, v_mix_pre_g=v_mix_pre_g, v_w_in=v_w_in, v_q_norm_g=v_q_norm_g, v_w_uq=v_w_uq, v_kv_norm_g=v_kv_norm_g, v_w_ukv=v_w_ukv, v_ssm_conv_w=v_ssm_conv_w, v_ssm_conv_b=v_ssm_conv_b, v_dt_bias=v_dt_bias, v_a_log=v_a_log, v_d_skip=v_d_skip, v_ssm_norm_g=v_ssm_norm_g, v_w_out=v_w_out, v_mix_post_g=v_mix_post_g, v_ffn_pre_g=v_ffn_pre_g, v_w_gate=v_w_gate, v_w_up=v_w_up, v_ffn_conv_w=v_ffn_conv_w, v_ffn_conv_b=v_ffn_conv_b, v_w_down=v_w_down, v_ffn_post_g=v_ffn_post_g)
    weights = {n: given[n] for n in TWIN_WEIGHTS}
    shared = {n: given[n] for n in SHARED_INPUTS}
    per_example = {n: given[n] for n in ['x']}
    grad_fn = _jax.value_and_grad(_loss, argnums=(0, 1))

    def one_microbatch(ex, loss_target):
        ex = dict(ex)
        diff = ex.pop(TWIN_DIFF_INPUT)
        return grad_fn(weights, diff, {**shared, **ex}, loss_target)

    if N_MICROBATCH == 1:
        loss, (grad_w, grad_x) = one_microbatch(per_example, given["loss_target"])
    else:
        def body(carry, xs):
            loss_sum, grad_sum = carry
            l_k, (gw_k, gx_k) = one_microbatch(xs[0], xs[1])
            with _jax.named_scope("update"):
                return (loss_sum + l_k, _jax.tree.map(_jnp.add, grad_sum, gw_k)), gx_k

        init = (_jnp.zeros((), _jnp.float32), _jax.tree.map(_jnp.zeros_like, weights))
        (loss, grad_w), grad_x = _jax.lax.scan(body, init, (per_example, given["loss_target"]))
    with _jax.named_scope("update"):
        delta_w, new_m, new_v = {}, {}, {}
        for n in TWIN_WEIGHTS:
            delta_w[n], new_m[n], new_v[n] = _adamw(weights[n], grad_w[n], given["m_" + n], given["v_" + n])
    return (loss, grad_x, *[grad_w[n] for n in TWIN_WEIGHTS], *[delta_w[n] for n in TWIN_WEIGHTS],
            *[new_m[n] for n in TWIN_WEIGHTS], *[new_v[n] for n in TWIN_WEIGHTS])
```

```python
import functools
import math

import jax
import jax.numpy as jnp
from jax import lax
from jax.experimental import pallas as pl
from jax.experimental.pallas import tpu as pltpu

F32, BF16, I32 = jnp.float32, jnp.bfloat16, jnp.int32
NN = (((1,), (0,)), ((), ()))
NT = (((1,), (1,)), ((), ()))
TN = (((0,), (0,)), ((), ()))
HI = lax.Precision.HIGHEST
MESH_ID = pl.DeviceIdType.MESH

EPS = 1e-6
CHUNK = 64
NOPE, ROPE, VH = 128, 64, 128
ROPE_THETA = 10000.0
HP, NST = 64, 128
SSM_K, FFN_K = 4, 3
LANE = 128
N_CHIPS = 4
VMEM_LIMIT = 52 * 1024 * 1024

ADAM_LR, ADAM_B1, ADAM_B2, ADAM_EPS, ADAM_WD, ADAM_STEP = 0.001, 0.9, 0.999, 1e-08, 0.01, 10


class _Cfg:
    def __init__(self, S, D, QL, KVL, H, HS, G, DFF, T):
        self.S, self.D, self.QL, self.KVL, self.H, self.HS, self.G, self.DFF, self.T = S, D, QL, KVL, H, HS, G, DFF, T
        self.INNER = HS * HP
        self.CONVCH = self.INNER + 2 * G * NST
        self.QW = H * (NOPE + ROPE)
        self.KVW = H * (NOPE + VH)
        self.MLAW = H * VH
        self.MIXW = self.MLAW + self.INNER
        self.IN_COLS = QL + KVL + ROPE + self.INNER + self.CONVCH + HS
        self.o_kr = QL + KVL
        self.o_z = self.o_kr + LANE
        self.o_xbc = self.o_z + self.INNER
        self.o_dt = self.o_xbc + self.CONVCH
        self.EXT = self.o_dt + LANE
        self.NPAIR = HS // 2
        self.REP = HS // G
        self.BIG = (("w_in", D, self.IN_COLS, 1), ("w_uq", QL, self.QW, 1), ("w_ukv", KVL, self.KVW, 1),
                    ("w_out", self.MIXW, D, 0), ("w_gate", D, DFF, 1), ("w_up", D, DFF, 1), ("w_down", DFF, D, 0))
        self.NSHARD = sum(r * c for _, r, c, _ in self.BIG) // N_CHIPS
        unit = 2 * LANE * 16
        self.NPACK = -(-self.NSHARD // unit) * unit
        self.R = self.NPACK // (2 * LANE)


_FULL = _Cfg(S=2048, D=2048, QL=768, KVL=512, H=8, HS=16, G=2, DFF=5632, T=256)

SMALL = ("mix_pre_g", "q_norm_g", "kv_norm_g", "ssm_conv_w", "ssm_conv_b", "dt_bias", "a_log", "d_skip", "ssm_norm_g",
         "mix_post_g", "ffn_pre_g", "ffn_conv_w", "ffn_conv_b", "ffn_post_g")
SMALL_SHARDED = ("ssm_conv_w", "ffn_conv_w")
WEIGHTS = ("mix_pre_g", "w_in", "q_norm_g", "w_uq", "kv_norm_g", "w_ukv", "ssm_conv_w", "ssm_conv_b", "dt_bias", "a_log",
           "d_skip", "ssm_norm_g", "w_out", "mix_post_g", "ffn_pre_g", "w_gate", "w_up", "ffn_conv_w", "ffn_conv_b",
           "w_down", "ffn_post_g")


def _pick(n, target, mult):
    best = None
    for d in range(mult, min(n, target) + 1, mult):
        if n % d == 0:
            best = d
    return best if best is not None else n


def _params(sem=None):
    kw = dict(vmem_limit_bytes=VMEM_LIMIT)
    if sem is not None:
        kw["dimension_semantics"] = sem
    return pltpu.CompilerParams(**kw)


def _dot(a, b, dims=NN, precision=None):
    return lax.dot_general(a, b, dims, preferred_element_type=F32, precision=precision)


def _sigmoid(x):
    return 1.0 / (1.0 + jnp.exp(-x))


def _rs(x):
    return lax.rsqrt(jnp.mean(x * x, axis=-1, keepdims=True) + EPS)


def _rms_back(xh, r, dn):
    return r * (dn - xh * jnp.mean(dn * xh, axis=-1, keepdims=True))


def _colsum(v):
    return jnp.sum(v, axis=0, keepdims=True)


def _matmul(name, a, b, mode, out_dtype, a2=None, b2=None):
    if mode == "nn":
        (M, K), N = a.shape, b.shape[1]
    elif mode == "nt":
        (M, K), N = a.shape, b.shape[0]
    else:
        (K, M), N = a.shape, b.shape[1]
    tm, tn, tk = _pick(M, 1024, LANE), _pick(N, 1408, LANE), _pick(K, 512, LANE)
    nk = K // tk
    dims = {"nn": NN, "nt": NT, "tn": TN}[mode]
    a_spec = pl.BlockSpec((tk, tm), lambda i, j, k: (k, i)) if mode == "tn" else pl.BlockSpec((tm, tk), lambda i, j, k: (i, k))
    b_spec = pl.BlockSpec((tn, tk), lambda i, j, k: (j, k)) if mode == "nt" else pl.BlockSpec((tk, tn), lambda i, j, k: (k, j))
    two = a2 is not None

    def body(*refs):
        o_ref, acc_ref = refs[-2], refs[-1]
        k = pl.program_id(2)

        @pl.when(k == 0)
        def _():
            acc_ref[...] = jnp.zeros_like(acc_ref)

        part = _dot(refs[0][...].astype(BF16), refs[1][...].astype(BF16), dims)
        if two:
            part += _dot(refs[2][...].astype(BF16), refs[3][...].astype(BF16), dims)
        acc_ref[...] += part

        @pl.when(k == nk - 1)
        def _():
            o_ref[...] = acc_ref[...].astype(o_ref.dtype)

    ins = (a, b, a2, b2) if two else (a, b)
    return pl.pallas_call(
        body, name=name, grid=(M // tm, N // tn, nk),
        in_specs=[a_spec, b_spec] * (2 if two else 1),
        out_specs=pl.BlockSpec((tm, tn), lambda i, j, k: (i, j)),
        out_shape=jax.ShapeDtypeStruct((M, N), out_dtype),
        scratch_shapes=[pltpu.VMEM((tm, tn), F32)],
        compiler_params=_params(("parallel", "parallel", "arbitrary")),
    )(*ins)


def _rowwise(name, fn, rows, mats, outs, reds, ts):
    S = rows[0].shape[0]
    nr, nm, no = len(rows), len(mats), len(outs)

    def body(*refs):
        res = fn(*[r[...] for r in refs[:nr + nm]])
        res = res if isinstance(res, (tuple, list)) else (res,)
        for r, v in zip(refs[nr + nm:nr + nm + no], res[:no]):
            r[...] = v.astype(r.dtype)
        first = pl.program_id(0) == 0
        for r, v in zip(refs[nr + nm + no:], res[no:]):
            @pl.when(first)
            def _():
                r[...] = jnp.broadcast_to(v, r.shape)

            @pl.when(jnp.logical_not(first))
            def _():
                r[...] += jnp.broadcast_to(v, r.shape)

    in_specs = [pl.BlockSpec((ts, a.shape[1]), lambda i: (i, 0)) for a in rows]
    in_specs += [pl.BlockSpec(m.shape, lambda i, nd=m.ndim: (0,) * nd) for m in mats]
    out_specs = [pl.BlockSpec((ts, w), lambda i: (i, 0)) for w, _ in outs]
    out_specs += [pl.BlockSpec(s, lambda i: (0, 0)) for s in reds]
    out_shape = [jax.ShapeDtypeStruct((S, w), dt) for w, dt in outs] + [jax.ShapeDtypeStruct(s, F32) for s in reds]
    return pl.pallas_call(
        body, name=name, grid=(S // ts,), in_specs=in_specs, out_specs=out_specs, out_shape=out_shape,
        compiler_params=_params(("arbitrary",) if reds else ("parallel",)),
    )(*rows, *mats)


def _shift_down(v, s):
    if s == 0:
        return v
    rows = lax.broadcasted_iota(I32, v.shape, 0)
    return jnp.where(rows >= s, pltpu.roll(v, s, 0), 0.0)


def _shift_up(v, s):
    if s == 0:
        return v
    n = v.shape[0]
    rows = lax.broadcasted_iota(I32, v.shape, 0)
    return jnp.where(rows < n - s, pltpu.roll(v, n - s, 0), 0.0)


def _conv(x, w, b):
    K = w.shape[0]
    y = jnp.broadcast_to(b, x.shape)
    for k in range(K):
        y = y + w[k:k + 1, :] * _shift_down(x, K - 1 - k)
    return y


def _conv_back(x, w, dc):
    K = w.shape[0]
    dx = jnp.zeros_like(x)
    dw = []
    for k in range(K):
        dx = dx + w[k:k + 1, :] * _shift_up(dc, K - 1 - k)
        dw.append(_colsum(dc * _shift_down(x, K - 1 - k)))
    return dx, jnp.concatenate(dw, axis=0), _colsum(dc)


def _colwise(name, fn, cols, vecs, outs, pouts, tc):
    S, C = cols[0].shape
    nc_, nv, no = len(cols), len(vecs), len(outs)

    def body(*refs):
        res = fn(*[r[...] for r in refs[:nc_ + nv]])
        res = res if isinstance(res, (tuple, list)) else (res,)
        for r, v in zip(refs[nc_ + nv:], res):
            r[...] = v.astype(r.dtype)

    in_specs = [pl.BlockSpec((S, tc), lambda j: (0, j)) for _ in cols]
    in_specs += [pl.BlockSpec((v.shape[0], tc), lambda j: (0, j)) for v in vecs]
    out_specs = [pl.BlockSpec((S, tc), lambda j: (0, j)) for _ in outs] + [pl.BlockSpec((k, tc), lambda j: (0, j)) for k in pouts]
    out_shape = [jax.ShapeDtypeStruct((S, C), dt) for dt in outs] + [jax.ShapeDtypeStruct((k, C), F32) for k in pouts]
    return pl.pallas_call(
        body, name=name, grid=(C // tc,), in_specs=in_specs, out_specs=out_specs, out_shape=out_shape,
        compiler_params=_params(("parallel",)),
    )(*cols, *vecs)


_G0, _G1 = math.sqrt(2.0 / math.pi), 0.044715


def _gelu(g):
    th = jnp.tanh(_G0 * (g + _G1 * g * g * g))
    return 0.5 * g * (1.0 + th), th


def _ffn_act(gate_pre, up, w, b):
    act, _ = _gelu(_conv(gate_pre, w, b))
    return act * up


def _ffn_act_back(dact, gate_pre, up, w, b):
    g = _conv(gate_pre, w, b)
    ge, th = _gelu(g)
    dge = 0.5 * (1.0 + th) + 0.5 * g * (1.0 - th * th) * _G0 * (1.0 + 3.0 * _G1 * g * g)
    dup = dact * ge
    dgate_pre, dw, db = _conv_back(gate_pre, w, dact * up * dge)
    return dgate_pre, dup, dw, db


def _ssm_act(xbc, w, b):
    c = _conv(xbc, w, b)
    return c * _sigmoid(c)


def _ssm_act_back(dxc, xbc, w, b):
    c = _conv(xbc, w, b)
    sg = _sigmoid(c)
    return _conv_back(xbc, w, dxc * sg * (1.0 + c * (1.0 - sg)))


def _rope_tables(S):
    inv = 1.0 / (ROPE_THETA ** (jnp.arange(0, ROPE, 2, dtype=F32) / ROPE))
    ang = jnp.arange(S, dtype=F32)[:, None] * inv[None, :]
    cos, sin = jnp.cos(ang), jnp.sin(ang)
    return jnp.tile(cos, (1, 4)), jnp.tile(jnp.concatenate([-sin, sin], axis=1), (1, 2))


def _swap_halves(x):
    lane = lax.broadcasted_iota(I32, x.shape, 1)
    w = x.shape[1]
    return jnp.where((lane % ROPE) < ROPE // 2, pltpu.roll(x, w - ROPE // 2, 1), pltpu.roll(x, ROPE // 2, 1))


def _rot(x, cos2, sin2):
    return x * cos2 + _swap_halves(x) * sin2


def _rot_back(dy, cos2, sin2):
    return dy * cos2 + _swap_halves(dy * sin2)


def _mla_pack(cfg, q, kv, kr, cos2, sin2):
    S, H = cfg.S, cfg.H
    ts = _pick(S, 512, 8)

    def body(qn_ref, qr_ref, kn_ref, v_ref, kr_ref, c_ref, s_ref, Q_ref, K_ref, V_ref):
        h = pl.program_id(0)
        c2, s2 = c_ref[...], s_ref[...]
        Q_ref[0, :, 0:LANE] = qn_ref[...].astype(BF16)
        Q_ref[0, :, LANE:] = _rot(qr_ref[...], c2, s2).astype(BF16)
        K_ref[0, :, 0:LANE] = kn_ref[...].astype(BF16)
        krr = _rot(kr_ref[...], c2, s2)
        K_ref[0, :, LANE:] = jnp.where(h % 2 == 1, pltpu.roll(krr, ROPE, 1), krr).astype(BF16)
        V_ref[0] = v_ref[...].astype(BF16)

    blk = lambda f: pl.BlockSpec((ts, LANE), f)
    return pl.pallas_call(
        body, name="mla_pack", grid=(H, S // ts),
        in_specs=[blk(lambda h, i: (i, h)), blk(lambda h, i: (i, H + h // 2)), blk(lambda h, i: (i, h)),
                  blk(lambda h, i: (i, H + h)), blk(lambda h, i: (i, 0)), blk(lambda h, i: (i, 0)), blk(lambda h, i: (i, 0))],
        out_specs=[pl.BlockSpec((1, ts, 2 * LANE), lambda h, i: (h, i, 0)), pl.BlockSpec((1, ts, 2 * LANE), lambda h, i: (h, i, 0)),
                   pl.BlockSpec((1, ts, LANE), lambda h, i: (h, i, 0))],
        out_shape=[jax.ShapeDtypeStruct((H, S, 2 * LANE), BF16), jax.ShapeDtypeStruct((H, S, 2 * LANE), BF16),
                   jax.ShapeDtypeStruct((H, S, LANE), BF16)],
        compiler_params=_params(("parallel", "parallel")),
    )(q, q, kv, kv, kr, cos2, sin2)


def _mla_unpack(cfg, dQ, dK, dV, cos2, sin2):
    S, H = cfg.S, cfg.H
    ts = _pick(S, 256, 8)

    def body(dQ_ref, dK_ref, dV_ref, c_ref, s_ref, dq_ref, dkv_ref, dkr_ref):
        c2, s2 = c_ref[...], s_ref[...]
        lo = lax.broadcasted_iota(I32, (ts, LANE), 1) < ROPE
        tk = jnp.zeros((ts, LANE), F32)
        for h in range(H):
            dq_ref[:, h * LANE:(h + 1) * LANE] = dQ_ref[h, :, 0:LANE].astype(BF16)
            dkv_ref[:, h * LANE:(h + 1) * LANE] = dK_ref[h, :, 0:LANE].astype(BF16)
            dkv_ref[:, (H + h) * LANE:(H + h + 1) * LANE] = dV_ref[h].astype(BF16)
            own = lo if h % 2 == 0 else jnp.logical_not(lo)
            tk = tk + jnp.where(own, dK_ref[h, :, LANE:], 0.0)
        for j in range(H // 2):
            dr = dQ_ref[2 * j, :, LANE:] + dQ_ref[2 * j + 1, :, LANE:]
            dq_ref[:, (H + j) * LANE:(H + j + 1) * LANE] = _rot_back(dr, c2, s2).astype(BF16)
        dkr_rot = jnp.where(lo, tk + pltpu.roll(tk, ROPE, 1), 0.0)
        dkr_ref[...] = _rot_back(dkr_rot, c2, s2).astype(BF16)

    tab = pl.BlockSpec((ts, LANE), lambda i: (i, 0))
    return pl.pallas_call(
        body, name="mla_unpack", grid=(S // ts,),
        in_specs=[pl.BlockSpec((H, ts, 2 * LANE), lambda i: (0, i, 0)), pl.BlockSpec((H, ts, 2 * LANE), lambda i: (0, i, 0)),
                  pl.BlockSpec((H, ts, LANE), lambda i: (0, i, 0)), tab, tab],
        out_specs=[pl.BlockSpec((ts, cfg.QW), lambda i: (i, 0)), pl.BlockSpec((ts, cfg.KVW), lambda i: (i, 0)), tab],
        out_shape=[jax.ShapeDtypeStruct((S, cfg.QW), BF16), jax.ShapeDtypeStruct((S, cfg.KVW), BF16),
                   jax.ShapeDtypeStruct((S, LANE), BF16)],
        compiler_params=_params(("parallel",)),
    )(dQ, dK, dV, cos2, sin2)


_ATT_T = 256
_ATT_SCALE = (NOPE + ROPE) ** -0.5


def _att_scores(q, k, qi, kb):
    s = _dot(q, k, NT) * _ATT_SCALE
    rows = qi * _ATT_T + lax.broadcasted_iota(I32, s.shape, 0)
    cols = kb * _ATT_T + lax.broadcasted_iota(I32, s.shape, 1)
    return jnp.where((cols // CHUNK) <= (rows // CHUNK), s, -1e30)


def _attn_fwd(cfg, Q, K, V):
    S, H, T = cfg.S, cfg.H, _ATT_T

    def body(q_ref, k_ref, v_ref, o_ref, lse_ref):
        qi = pl.program_id(1)
        q = q_ref[0]

        def step(kb, carry):
            m, l, acc = carry
            ks = pl.multiple_of(kb * T, T)
            s = _att_scores(q, k_ref[0, pl.ds(ks, T), :], qi, kb)
            m_new = jnp.maximum(m, jnp.max(s, axis=1, keepdims=True))
            p = jnp.exp(s - m_new)
            alpha = jnp.exp(m - m_new)
            l = alpha * l + jnp.sum(p, axis=1, keepdims=True)
            acc = alpha * acc + _dot(p.astype(BF16), v_ref[0, pl.ds(ks, T), :])
            return m_new, l, acc

        init = (jnp.full((T, 1), -1e30, F32), jnp.zeros((T, 1), F32), jnp.zeros((T, VH), F32))
        m, l, acc = lax.fori_loop(0, qi + 1, step, init)
        o_ref[...] = acc / l
        lse_ref[...] = jnp.broadcast_to(m + jnp.log(l), (T, LANE))

    return pl.pallas_call(
        body, name="attn_fwd", grid=(H, S // T),
        in_specs=[pl.BlockSpec((1, T, 2 * LANE), lambda h, i: (h, i, 0)), pl.BlockSpec((1, S, 2 * LANE), lambda h, i: (h, 0, 0)),
                  pl.BlockSpec((1, S, LANE), lambda h, i: (h, 0, 0))],
        out_specs=[pl.BlockSpec((T, LANE), lambda h, i: (i, h)), pl.BlockSpec((T, LANE), lambda h, i: (i, h))],
        out_shape=[jax.ShapeDtypeStruct((S, H * LANE), F32), jax.ShapeDtypeStruct((S, H * LANE), F32)],
        compiler_params=_params(("parallel", "parallel")),
    )(Q, K, V)


def _attn_dq(cfg, Q, K, V, do, o, lse):
    S, H, T = cfg.S, cfg.H, _ATT_T

    def body(q_ref, k_ref, v_ref, do_ref, o_ref, lse_ref, dq_ref, dl_ref):
        qi = pl.program_id(1)
        q = q_ref[0]
        do = do_ref[...]
        delta = jnp.sum(do * o_ref[...], axis=1, keepdims=True)
        lse = lse_ref[:, 0:1]
        dob = do.astype(BF16)

        def step(kb, dq):
            ks = pl.multiple_of(kb * T, T)
            k = k_ref[0, pl.ds(ks, T), :]
            p = jnp.exp(_att_scores(q, k, qi, kb) - lse)
            dp = _dot(dob, v_ref[0, pl.ds(ks, T), :], NT)
            ds = p * (dp - delta) * _ATT_SCALE
            return dq + _dot(ds.astype(BF16), k)

        dq_ref[0] = lax.fori_loop(0, qi + 1, step, jnp.zeros((T, 2 * LANE), F32))
        dl_ref[...] = jnp.broadcast_to(delta, (T, LANE))

    col = pl.BlockSpec((T, LANE), lambda h, i: (i, h))
    return pl.pallas_call(
        body, name="attn_dq", grid=(H, S // T),
        in_specs=[pl.BlockSpec((1, T, 2 * LANE), lambda h, i: (h, i, 0)), pl.BlockSpec((1, S, 2 * LANE), lambda h, i: (h, 0, 0)),
                  pl.BlockSpec((1, S, LANE), lambda h, i: (h, 0, 0)), col, col, col],
        out_specs=[pl.BlockSpec((1, T, 2 * LANE), lambda h, i: (h, i, 0)), col],
        out_shape=[jax.ShapeDtypeStruct((H, S, 2 * LANE), F32), jax.ShapeDtypeStruct((S, H * LANE), F32)],
        compiler_params=_params(("parallel", "parallel")),
    )(Q, K, V, do, o, lse)


def _attn_dkv(cfg, Q, K, V, do, lse, delta):
    S, H, T = cfg.S, cfg.H, _ATT_T
    nq = S // T

    def body(q_ref, k_ref, v_ref, do_ref, lse_ref, dl_ref, dk_ref, dv_ref):
        kb = pl.program_id(1)
        k, v = k_ref[0], v_ref[0]

        def step(qi, carry):
            dk, dv = carry
            qs = pl.multiple_of(qi * T, T)
            q = q_ref[0, pl.ds(qs, T), :]
            dob = do_ref[pl.ds(qs, T), :].astype(BF16)
            p = jnp.exp(_att_scores(q, k, qi, kb) - lse_ref[pl.ds(qs, T), 0:1])
            dv = dv + _dot(p.astype(BF16), dob, TN)
            dp = _dot(dob, v, NT)
            ds = p * (dp - dl_ref[pl.ds(qs, T), 0:1]) * _ATT_SCALE
            dk = dk + _dot(ds.astype(BF16), q, TN)
            return dk, dv

        dk, dv = lax.fori_loop(kb, nq, step, (jnp.zeros((T, 2 * LANE), F32), jnp.zeros((T, VH), F32)))
        dk_ref[0] = dk
        dv_ref[0] = dv

    col = pl.BlockSpec((S, LANE), lambda h, j: (0, h))
    return pl.pallas_call(
        body, name="attn_dkv", grid=(H, S // T),
        in_specs=[pl.BlockSpec((1, S, 2 * LANE), lambda h, j: (h, 0, 0)), pl.BlockSpec((1, T, 2 * LANE), lambda h, j: (h, j, 0)),
                  pl.BlockSpec((1, T, LANE), lambda h, j: (h, j, 0)), col, col, col],
        out_specs=[pl.BlockSpec((1, T, 2 * LANE), lambda h, j: (h, j, 0)), pl.BlockSpec((1, T, LANE), lambda h, j: (h, j, 0))],
        out_shape=[jax.ShapeDtypeStruct((H, S, 2 * LANE), F32), jax.ShapeDtypeStruct((H, S, LANE), F32)],
        compiler_params=_params(("parallel", "parallel")),
    )(Q, K, V, do, lse, delta)


def _expand_matrix(cfg):
    r = lax.broadcasted_iota(I32, (LANE, cfg.INNER), 0)
    c = lax.broadcasted_iota(I32, (LANE, cfg.INNER), 1)
    return (r == c // HP).astype(F32)


def _softplus(x):
    return jnp.maximum(x, 0.0) + jnp.log(1.0 + jnp.exp(-jnp.abs(x)))


def _ssd_prep(cfg, dt_raw, dt_bias_pad, a_log_pad, expand):
    HS = cfg.HS

    def fn(raw, bias, alog, E):
        heads = lax.broadcasted_iota(I32, raw.shape, 1) < HS
        dt = jnp.where(heads, _softplus(raw + bias), 0.0)
        a = dt * jnp.where(heads[0:1], -jnp.exp(alog), 0.0)
        return dt, a, _dot(dt, E, precision=HI), _dot(a, E, precision=HI)

    return _rowwise("ssd_prep", fn, [dt_raw], [dt_bias_pad, a_log_pad, expand],
                    [(LANE, F32), (LANE, F32), (cfg.INNER, F32), (cfg.INNER, F32)], [], _pick(cfg.S, 512, 8))


def _tril(T):
    return lax.broadcasted_iota(I32, (T, T), 0) >= lax.broadcasted_iota(I32, (T, T), 1)


def _ssd_fwd(cfg, xc, dt_exp, a_exp, a_small, dskip_exp):
    S, T, INNER, G, NPAIR = cfg.S, cfg.T, cfg.INNER, cfg.G, cfg.NPAIR
    NC = S // T

    def body(xc_ref, dte_ref, ae_ref, as_ref, dsk_ref, y_ref, hin_ref, ht_ref):
        @pl.when(pl.program_id(0) == 0)
        def _():
            ht_ref[...] = jnp.zeros_like(ht_ref)

        tril = _tril(T)
        tri = tril.astype(F32)
        acs_s = _dot(tri, as_ref[...], precision=HI)
        acs_e = _dot(tri, ae_ref[...], precision=HI)
        acs_t = acs_s.T
        lo = lax.broadcasted_iota(I32, (T, LANE), 1) < HP
        for g in range(G):
            Bb = xc_ref[:, INNER + g * NST:INNER + (g + 1) * NST].astype(BF16)
            Cb = xc_ref[:, INNER + (G + g) * NST:INNER + (G + g + 1) * NST].astype(BF16)
            Gm = _dot(Cb, Bb, NT)
            for j in range(g * NPAIR // G, (g + 1) * NPAIR // G):
                sl = slice(j * LANE, (j + 1) * LANE)
                Xp = xc_ref[:, sl]
                Xdt = Xp * dte_ref[:, sl]
                Xb = Xdt.astype(BF16)
                acs_p = acs_e[:, sl]
                last = acs_p[T - 1:T, :]
                Hin = ht_ref[j]
                hin_ref[0, j] = Hin
                yd = []
                for e in (0, 1):
                    h = 2 * j + e
                    Lm = jnp.exp(jnp.where(tril, acs_s[:, h:h + 1] - acs_t[h:h + 1, :], -1e30))
                    yd.append(_dot((Gm * Lm).astype(BF16), Xb))
                y_off = _dot(Cb, Hin.astype(BF16)) * jnp.exp(acs_p)
                y_ref[:, sl] = jnp.where(lo, yd[0], yd[1]) + y_off + Xp * dsk_ref[:, sl]
                st = _dot(Bb, (Xdt * jnp.exp(last - acs_p)).astype(BF16), TN)
                ht_ref[j] = jnp.exp(last) * Hin + st

    rows = lambda w: pl.BlockSpec((T, w), lambda c: (c, 0))
    return pl.pallas_call(
        body, name="ssd_fwd", grid=(NC,),
        in_specs=[rows(cfg.CONVCH), rows(INNER), rows(INNER), rows(LANE), pl.BlockSpec((1, INNER), lambda c: (0, 0))],
        out_specs=[rows(INNER), pl.BlockSpec((1, NPAIR, NST, LANE), lambda c: (c, 0, 0, 0))],
        out_shape=[jax.ShapeDtypeStruct((S, INNER), F32), jax.ShapeDtypeStruct((NC, NPAIR, NST, LANE), F32)],
        scratch_shapes=[pltpu.VMEM((NPAIR, NST, LANE), F32)],
        compiler_params=_params(("arbitrary",)),
    )(xc, dt_exp, a_exp, a_small, dskip_exp)


def _ssd_bwd(cfg, dy, xc, dt_exp, a_exp, a_small, dskip_exp, hin, dt_raw, dt_bias_pad, a_log_pad, expand):
    S, T, INNER, G, NPAIR, HS = cfg.S, cfg.T, cfg.INNER, cfg.G, cfg.NPAIR, cfg.HS
    NC = S // T

    def body(dy_ref, xc_ref, dte_ref, ae_ref, as_ref, dsk_ref, hin_ref, raw_ref, bias_ref, alog_ref, e_ref,
             dxc_ref, draw_ref, dbias_ref, dalog_ref, dskip_ref, dht_ref, cols_ref, rows_ref, dacs_ref, ddt_ref):
        first = pl.program_id(0) == 0

        @pl.when(first)
        def _():
            dht_ref[...] = jnp.zeros_like(dht_ref)

        tril = _tril(T)
        tri = tril.astype(F32)
        a_s = as_ref[...]
        acs_s = _dot(tri, a_s, precision=HI)
        acs_e = _dot(tri, ae_ref[...], precision=HI)
        acs_t = acs_s.T
        lo = lax.broadcasted_iota(I32, (T, LANE), 1) < HP
        last_row = lax.broadcasted_iota(I32, (T, LANE), 0) == T - 1
        cols_ref[...] = jnp.zeros_like(cols_ref)
        rows_ref[...] = jnp.zeros_like(rows_ref)
        dsk_parts = []
        for g in range(G):
            bsl = slice(INNER + g * NST, INNER + (g + 1) * NST)
            csl = slice(INNER + (G + g) * NST, INNER + (G + g + 1) * NST)
            Bb = xc_ref[:, bsl].astype(BF16)
            Cb = xc_ref[:, csl].astype(BF16)
            Gm = _dot(Cb, Bb, NT)
            dG = jnp.zeros((T, T), F32)
            dB = jnp.zeros((T, NST), F32)
            dC = jnp.zeros((T, NST), F32)
            for j in range(g * NPAIR // G, (g + 1) * NPAIR // G):
                sl = slice(j * LANE, (j + 1) * LANE)
                Xp = xc_ref[:, sl]
                dtp = dte_ref[:, sl]
                Xdt = Xp * dtp
                Xb = Xdt.astype(BF16)
                acs_p = acs_e[:, sl]
                last = acs_p[T - 1:T, :]
                e_p, dec, cd = jnp.exp(acs_p), jnp.exp(last - acs_p), jnp.exp(last)
                Hin = hin_ref[0, j]
                Hb = Hin.astype(BF16)
                dHn = dht_ref[j]
                dHb = dHn.astype(BF16)
                dYp = dy_ref[:, sl]
                z = _dot(Cb, Hb)
                dz = (dYp * e_p).astype(BF16)
                dacs_p = dYp * z * e_p
                dC = dC + _dot(dz, Hb, NT)
                dHin = _dot(Cb, dz, TN) + cd * dHn
                dlast = _colsum(dHn * Hin) * cd
                qv = _dot(Bb, dHb)
                dXdt = qv * dec
                ddec = qv * Xdt * dec
                dacs_p = dacs_p - ddec
                dlast = dlast + _colsum(ddec)
                dB = dB + _dot((Xdt * dec).astype(BF16), dHb, NT)
                for e in (0, 1):
                    h = 2 * j + e
                    Lm = jnp.exp(jnp.where(tril, acs_s[:, h:h + 1] - acs_t[h:h + 1, :], -1e30))
                    Mh = Gm * Lm
                    dYe = jnp.where(lo if e == 0 else jnp.logical_not(lo), dYp, 0.0).astype(BF16)
                    dM = _dot(dYe, Xb, NT)
                    dXdt = dXdt + _dot(Mh.astype(BF16), dYe, TN)
                    W = dM * Mh
                    cols_ref[:, h:h + 1] = jnp.sum(W, axis=1, keepdims=True)
                    rows_ref[h:h + 1, :] = _colsum(W)
                    dG = dG + dM * Lm
                dacs_ref[:, sl] = dacs_p + jnp.where(last_row, dlast, 0.0)
                ddt_ref[:, sl] = dXdt * Xp
                dxc_ref[:, sl] = dXdt * dtp + dYp * dsk_ref[:, sl]
                dsk_parts.append(_colsum(dYp * Xp))
                dht_ref[j] = dHin
            dGb = dG.astype(BF16)
            dxc_ref[:, bsl] = dB + _dot(dGb, Cb, TN)
            dxc_ref[:, csl] = dC + _dot(dGb, Bb)
        E = e_ref[...]
        dacs_s = cols_ref[...] - rows_ref[...].T + _dot(dacs_ref[...], E, NT, precision=HI)
        da = _dot(tri, dacs_s, TN, precision=HI)
        heads = lax.broadcasted_iota(I32, (1, LANE), 1) < HS
        A = jnp.where(heads, -jnp.exp(alog_ref[...]), 0.0)
        ddt = _dot(ddt_ref[...], E, NT, precision=HI) + da * A
        draw = jnp.where(heads, ddt * _sigmoid(raw_ref[...] + bias_ref[...]), 0.0)
        draw_ref[...] = draw
        dsk = _dot(jnp.broadcast_to(jnp.concatenate(dsk_parts, axis=1), (8, INNER)), E, NT, precision=HI)[0:1]
        for ref, val in ((dbias_ref, _colsum(draw)), (dalog_ref, _colsum(da * a_s)), (dskip_ref, dsk)):
            @pl.when(first)
            def _():
                ref[...] = val

            @pl.when(jnp.logical_not(first))
            def _():
                ref[...] += val

    rows = lambda w: pl.BlockSpec((T, w), lambda c: (NC - 1 - c, 0))
    vec = lambda w: pl.BlockSpec((1, w), lambda c: (0, 0))
    return pl.pallas_call(
        body, name="ssd_bwd", grid=(NC,),
        in_specs=[rows(INNER), rows(cfg.CONVCH), rows(INNER), rows(INNER), rows(LANE), vec(INNER),
                  pl.BlockSpec((1, NPAIR, NST, LANE), lambda c: (NC - 1 - c, 0, 0, 0)), rows(LANE), vec(LANE), vec(LANE),
                  pl.BlockSpec((LANE, INNER), lambda c: (0, 0))],
        out_specs=[rows(cfg.CONVCH), rows(LANE), vec(LANE), vec(LANE), vec(LANE)],
        out_shape=[jax.ShapeDtypeStruct((S, cfg.CONVCH), F32), jax.ShapeDtypeStruct((S, LANE), F32)]
        + [jax.ShapeDtypeStruct((1, LANE), F32)] * 3,
        scratch_shapes=[pltpu.VMEM((NPAIR, NST, LANE), F32), pltpu.VMEM((T, LANE), F32), pltpu.VMEM((LANE, T), F32),
                        pltpu.VMEM((T, INNER), F32), pltpu.VMEM((T, INNER), F32)],
        compiler_params=_params(("arbitrary",)),
    )(dy, xc, dt_exp, a_exp, a_small, dskip_exp, hin, dt_raw, dt_bias_pad, a_log_pad, expand)


def _ssd_post(cfg, y, z, norm_g):
    W = cfg.INNER // cfg.G

    def fn(y, z, g):
        yz = y * z * _sigmoid(z)
        return jnp.concatenate([yz[:, i * W:(i + 1) * W] * _rs(yz[:, i * W:(i + 1) * W]) for i in range(cfg.G)], axis=1) * g

    return _rowwise("ssd_post", fn, [y, z], [norm_g], [(cfg.INNER, BF16)], [], _pick(cfg.S, 256, 8))[0]


def _ssd_post_bwd(cfg, db, y, z, norm_g):
    W = cfg.INNER // cfg.G

    def fn(db, y, z, g):
        sg = _sigmoid(z)
        yz = y * z * sg
        dn = db * g
        dyz, nh = [], []
        for i in range(cfg.G):
            seg = yz[:, i * W:(i + 1) * W]
            r = _rs(seg)
            nh.append(seg * r)
            dyz.append(_rms_back(nh[-1], r, dn[:, i * W:(i + 1) * W]))
        dyz = jnp.concatenate(dyz, axis=1)
        return dyz * z * sg, dyz * y * sg * (1.0 + z * (1.0 - sg)), _colsum(db * jnp.concatenate(nh, axis=1))

    return _rowwise("ssd_post_bwd", fn, [db, y, z], [norm_g], [(cfg.INNER, F32), (cfg.INNER, F32)], [(1, cfg.INNER)],
                    _pick(cfg.S, 256, 8))


def _local_grads(cfg, x, tgt, W, sp):
    S, D, H, INNER = cfg.S, cfg.D, cfg.H, cfg.INNER
    ts = _pick(S, 256, 8)
    tc = 256

    xn = _rowwise("rms_pre", lambda x, g: x * _rs(x) * g, [x], [sp["mix_pre_g"]], [(D, BF16)], [], ts)[0]
    u = _matmul("mm_in", xn, W["w_in"], "nn", F32)
    c_q, c_kv = u[:, :cfg.QL], u[:, cfg.QL:cfg.o_kr]
    kr = u[:, cfg.o_kr:cfg.o_z]
    z = u[:, cfg.o_z:cfg.o_xbc]
    xbc = u[:, cfg.o_xbc:cfg.o_dt]
    dt_raw = u[:, cfg.o_dt:]

    cqn = _rowwise("rms_q", lambda x, g: x * _rs(x) * g, [c_q], [sp["q_norm_g"]], [(cfg.QL, BF16)], [], ts)[0]
    ckvn = _rowwise("rms_kv", lambda x, g: x * _rs(x) * g, [c_kv], [sp["kv_norm_g"]], [(cfg.KVL, BF16)], [], ts)[0]
    q = _matmul("mm_uq", cqn, W["w_uq"], "nn", F32)
    kv = _matmul("mm_ukv", ckvn, W["w_ukv"], "nn", F32)
    cos2, sin2 = _rope_tables(S)
    Qh, Kh, Vh = _mla_pack(cfg, q, kv, kr, cos2, sin2)
    a_out, lse = _attn_fwd(cfg, Qh, Kh, Vh)

    pad = lambda v: jnp.pad(v, ((0, 0), (0, LANE - v.shape[1])))
    expand = _expand_matrix(cfg)
    dt_bias_pad, a_log_pad = pad(sp["dt_bias"]), pad(sp["a_log"])
    dskip_exp = jnp.repeat(sp["d_skip"], HP, axis=1)
    xc = _colwise("ssm_act", _ssm_act, [xbc], [sp["ssm_conv_w"], sp["ssm_conv_b"]], [F32], [], tc)[0]
    dt_s, a_s, dt_exp, a_exp = _ssd_prep(cfg, dt_raw, dt_bias_pad, a_log_pad, expand)
    y_ssd, hin = _ssd_fwd(cfg, xc, dt_exp, a_exp, a_s, dskip_exp)
    b_out = _ssd_post(cfg, y_ssd, z, sp["ssm_norm_g"])

    mix = _matmul("mm_out", a_out, W["w_out"][:cfg.MLAW], "nn", F32, b_out, W["w_out"][cfg.MLAW:])

    def mid(x, mix, g_mp, g_fp):
        x1 = x + mix * _rs(mix) * g_mp
        return x1, x1 * _rs(x1) * g_fp

    x1, h2 = _rowwise("fwd_mid", mid, [x, mix], [sp["mix_post_g"], sp["ffn_pre_g"]], [(D, F32), (D, BF16)], [], ts)
    gate_pre = _matmul("mm_gate", h2, W["w_gate"], "nn", F32)
    up = _matmul("mm_up", h2, W["w_up"], "nn", F32)
    act = _colwise("ffn_act", _ffn_act, [gate_pre, up], [sp["ffn_conv_w"], sp["ffn_conv_b"]], [BF16], [], tc)[0]
    f = _matmul("mm_down", act, W["w_down"], "nn", F32)

    def final(x1, f, t, g):
        r = _rs(f)
        fh = f * r
        err = x1 + fh * g - t
        loss = 0.5 * jnp.sum(jnp.mean(err * err, axis=-1, keepdims=True), axis=0, keepdims=True)
        dy = err * (1.0 / D)
        return dy, _rms_back(fh, r, dy * g), _colsum(dy * fh), loss

    dy, df, g_ffn_post, loss = _rowwise("final", final, [x1, f, tgt], [sp["ffn_post_g"]], [(D, F32), (D, BF16)],
                                        [(1, D), (1, LANE)], ts)
    gW = {}
    dact = _matmul("mm_down_dx", df, W["w_down"], "nt", F32)
    gW["w_down"] = _matmul("mm_down_dw", act, df, "tn", F32)
    dgate, dup, g_ffn_conv_w, g_ffn_conv_b = _colwise(
        "ffn_act_bwd", _ffn_act_back, [dact, gate_pre, up], [sp["ffn_conv_w"], sp["ffn_conv_b"]], [BF16, BF16], [FFN_K, 1], tc)
    dh2 = _matmul("mm_gu_dx", dgate, W["w_gate"], "nt", F32, dup, W["w_up"])
    gW["w_gate"] = _matmul("mm_gate_dw", h2, dgate, "tn", F32)
    gW["w_up"] = _matmul("mm_up_dw", h2, dup, "tn", F32)

    def mid_back(dy, dh2, x1, mix, g_mp, g_fp):
        r2 = _rs(x1)
        xh = x1 * r2
        dx1 = dy + _rms_back(xh, r2, dh2 * g_fp)
        r1 = _rs(mix)
        mh = mix * r1
        return dx1, _rms_back(mh, r1, dx1 * g_mp), _colsum(dh2 * xh), _colsum(dx1 * mh)

    dx1, dmix, g_ffn_pre, g_mix_post = _rowwise("bwd_mid", mid_back, [dy, dh2, x1, mix], [sp["mix_post_g"], sp["ffn_pre_g"]],
                                                [(D, F32), (D, BF16)], [(1, D), (1, D)], ts)
    da_out = _matmul("mm_out_dxa", dmix, W["w_out"][:cfg.MLAW], "nt", F32)
    db_out = _matmul("mm_out_dxb", dmix, W["w_out"][cfg.MLAW:], "nt", F32)
    gW["w_out"] = jnp.concatenate([_matmul("mm_out_dwa", a_out, dmix, "tn", F32), _matmul("mm_out_dwb", b_out, dmix, "tn", F32)],
                                  axis=0)

    dy_ssd, dz, g_ssm_norm = _ssd_post_bwd(cfg, db_out, y_ssd, z, sp["ssm_norm_g"])
    dxc, ddt_raw, g_dt_bias, g_a_log, g_d_skip = _ssd_bwd(cfg, dy_ssd, xc, dt_exp, a_exp, a_s, dskip_exp, hin, dt_raw,
                                                          dt_bias_pad, a_log_pad, expand)
    dxbc, g_ssm_conv_w, g_ssm_conv_b = _colwise("ssm_act_bwd", _ssm_act_back, [dxc, xbc], [sp["ssm_conv_w"], sp["ssm_conv_b"]],
                                                [BF16], [SSM_K, 1], tc)

    dQ, delta = _attn_dq(cfg, Qh, Kh, Vh, da_out, a_out, lse)
    dK, dV = _attn_dkv(cfg, Qh, Kh, Vh, da_out, lse, delta)
    dq, dkv, dkr = _mla_unpack(cfg, dQ, dK, dV, cos2, sin2)
    dcqn = _matmul("mm_uq_dx", dq, W["w_uq"], "nt", F32)
    dckvn = _matmul("mm_ukv_dx", dkv, W["w_ukv"], "nt", F32)
    gW["w_uq"] = _matmul("mm_uq_dw", cqn, dq, "tn", F32)
    gW["w_ukv"] = _matmul("mm_ukv_dw", ckvn, dkv, "tn", F32)

    def rms_back(x, dy, g):
        r = _rs(x)
        xh = x * r
        return _rms_back(xh, r, dy * g), _colsum(dy * xh)

    dc_q, g_q_norm = _rowwise("rms_q_bwd", rms_back, [c_q, dcqn], [sp["q_norm_g"]], [(cfg.QL, BF16)], [(1, cfg.QL)], ts)
    dc_kv, g_kv_norm = _rowwise("rms_kv_bwd", rms_back, [c_kv, dckvn], [sp["kv_norm_g"]], [(cfg.KVL, BF16)], [(1, cfg.KVL)], ts)

    du = jnp.concatenate([dc_q, dc_kv, dkr, dz.astype(BF16), dxbc, ddt_raw.astype(BF16)], axis=1)
    dxn = _matmul("mm_in_dx", du, W["w_in"], "nt", F32)
    gW["w_in"] = _matmul("mm_in_dw", xn, du, "tn", F32)

    def first_back(dx1, dxn, x, g):
        r = _rs(x)
        xh = x * r
        return dx1 + _rms_back(xh, r, dxn * g), _colsum(dxn * xh)

    grad_x, g_mix_pre = _rowwise("bwd_first", first_back, [dx1, dxn, x], [sp["mix_pre_g"]], [(D, F32)], [(1, D)], ts)

    gs = dict(mix_pre_g=g_mix_pre, q_norm_g=g_q_norm, kv_norm_g=g_kv_norm, ssm_conv_w=g_ssm_conv_w, ssm_conv_b=g_ssm_conv_b,
              dt_bias=g_dt_bias[:, :cfg.HS], a_log=g_a_log[:, :cfg.HS], d_skip=g_d_skip[:, :cfg.HS], ssm_norm_g=g_ssm_norm,
              mix_post_g=g_mix_post, ffn_pre_g=g_ffn_pre, ffn_conv_w=g_ffn_conv_w, ffn_conv_b=g_ffn_conv_b,
              ffn_post_g=g_ffn_post)
    return loss, grad_x, gW, gs


def _to_kernel_layout(cfg, name, w):
    if name == "w_in":
        a = cfg.o_kr + ROPE
        return jnp.concatenate([w[:, :a], jnp.zeros((w.shape[0], LANE - ROPE), w.dtype), w[:, a:],
                                jnp.zeros((w.shape[0], LANE - cfg.HS), w.dtype)], axis=1)
    if name == "w_uq":
        w3 = w.reshape(cfg.QL, cfg.H, NOPE + ROPE)
        return jnp.concatenate([w3[:, :, :NOPE].reshape(cfg.QL, -1), w3[:, :, NOPE:].reshape(cfg.QL, -1)], axis=1)
    if name == "w_ukv":
        w3 = w.reshape(cfg.KVL, cfg.H, NOPE + VH)
        return jnp.concatenate([w3[:, :, :NOPE].reshape(cfg.KVL, -1), w3[:, :, NOPE:].reshape(cfg.KVL, -1)], axis=1)
    return w


def _from_kernel_layout(cfg, name, g):
    if name == "w_in":
        return jnp.concatenate([g[:, :cfg.o_kr + ROPE], g[:, cfg.o_z:cfg.o_dt + cfg.HS]], axis=1)
    if name == "w_uq":
        n = g[:, :cfg.H * NOPE].reshape(cfg.QL, cfg.H, NOPE)
        r = g[:, cfg.H * NOPE:].reshape(cfg.QL, cfg.H, ROPE)
        return jnp.concatenate([n, r], axis=2).reshape(cfg.QL, -1)
    if name == "w_ukv":
        n = g[:, :cfg.H * NOPE].reshape(cfg.KVL, cfg.H, NOPE)
        v = g[:, cfg.H * NOPE:].reshape(cfg.KVL, cfg.H, VH)
        return jnp.concatenate([n, v], axis=2).reshape(cfg.KVL, -1)
    return g


def _pack_shards(cfg, shards, dtype):
    flat = jnp.concatenate([shards[n].reshape(-1).astype(dtype) for n, _, _, _ in cfg.BIG])
    return jnp.pad(flat, (0, cfg.NPACK - cfg.NSHARD)).reshape(2, cfg.R, LANE)


def _unpack_shards(cfg, packed):
    flat, out, off = packed.reshape(-1), {}, 0
    for n, r, c, ax in cfg.BIG:
        shp = (r, c // N_CHIPS) if ax == 1 else (r // N_CHIPS, c)
        out[n] = flat[off:off + shp[0] * shp[1]].reshape(shp)
        off += shp[0] * shp[1]
    return out


def _chips_to_full(cfg, gathered):
    flat, out, off = gathered.reshape(N_CHIPS, -1), {}, 0
    for n, r, c, ax in cfg.BIG:
        k = r * c // N_CHIPS
        p = flat[:, off:off + k]
        out[n] = p.reshape(N_CHIPS, r, c // N_CHIPS).transpose(1, 0, 2).reshape(r, c) if ax == 1 else p.reshape(r, c)
        off += k
    return out


def _full_to_chips(cfg, fulls, dtype):
    parts = []
    for n, r, c, ax in cfg.BIG:
        f = fulls[n].astype(dtype)
        parts.append(f.reshape(r, N_CHIPS, c // N_CHIPS).transpose(1, 0, 2).reshape(N_CHIPS, -1) if ax == 1
                     else f.reshape(N_CHIPS, -1))
    flat = jnp.pad(jnp.concatenate(parts, axis=1), ((0, 0), (0, cfg.NPACK - cfg.NSHARD)))
    return flat.reshape(N_CHIPS, 2, cfg.R, LANE).transpose(1, 0, 2, 3)


def _me():
    return lax.axis_index("x"), lax.axis_index("y"), lax.axis_index("c")


def _other_chips(x, y):
    return [(1 - x, y), (x, 1 - y), (1 - x, 1 - y)]


_ANY = pl.BlockSpec(memory_space=pl.ANY)


def _allgather_weights(cfg, mine):
    def body(s_ref, o_ref, send_sems, recv_sems, local_sem):
        x, y, c = _me()
        chip = 2 * x + y
        sib = (x, y, 1 - c)
        chips = _other_chips(x, y)

        def copy(k, src, dst, to):
            return pltpu.make_async_remote_copy(src_ref=src, dst_ref=dst, send_sem=send_sems.at[k], recv_sem=recv_sems.at[k],
                                                device_id=to, device_id_type=MESH_ID)

        local = pltpu.make_async_copy(s_ref, o_ref.at[chip], local_sem)
        local.start()
        sends = [copy(j, s_ref.at[c], o_ref.at[chip, c], (cx, cy, c)) for j, (cx, cy) in enumerate(chips)]
        for cp in sends:
            cp.start()
        passed = []
        for j, (cx, cy) in enumerate(chips):
            theirs = o_ref.at[2 * cx + cy, c]
            copy(j, theirs, theirs, sib).wait_recv()
            passed.append(copy(3 + j, theirs, theirs, sib))
            passed[-1].start()
        for j, (cx, cy) in enumerate(chips):
            theirs = o_ref.at[2 * cx + cy, 1 - c]
            copy(3 + j, theirs, theirs, sib).wait_recv()
        for cp in sends + passed:
            cp.wait_send()
        local.wait()

    return pl.pallas_call(
        body, name="allgather_weights", in_specs=[_ANY], out_specs=_ANY,
        out_shape=jax.ShapeDtypeStruct((N_CHIPS,) + mine.shape, mine.dtype),
        scratch_shapes=[pltpu.SemaphoreType.DMA((6,)), pltpu.SemaphoreType.DMA((6,)), pltpu.SemaphoreType.DMA],
    )(mine)


def _pair_exchange(cfg, packed):
    def body(p_ref, o_ref, send_sem, recv_sem):
        x, y, c = _me()
        cp = pltpu.make_async_remote_copy(src_ref=p_ref.at[1 - c], dst_ref=o_ref, send_sem=send_sem, recv_sem=recv_sem,
                                          device_id=(x, y, 1 - c), device_id_type=MESH_ID)
        cp.start()
        cp.wait()

    return pl.pallas_call(
        body, name="grad_pair_exchange", in_specs=[_ANY], out_specs=_ANY,
        out_shape=jax.ShapeDtypeStruct(packed.shape[1:], packed.dtype),
        scratch_shapes=[pltpu.SemaphoreType.DMA, pltpu.SemaphoreType.DMA],
    )(packed)


def _pair_sum(cfg, packed, theirs):
    tr = _pick(cfg.R, 2048, 16)
    c = lax.axis_index("c").astype(I32).reshape(1)

    def body(c_ref, a_ref, b_ref, o_ref):
        o_ref[...] = (a_ref[0].astype(F32) + b_ref[...].astype(F32)).astype(o_ref.dtype)

    return pl.pallas_call(
        body, name="grad_pair_sum",
        grid_spec=pltpu.PrefetchScalarGridSpec(
            num_scalar_prefetch=1, grid=(N_CHIPS, cfg.R // tr),
            in_specs=[pl.BlockSpec((1, 1, tr, LANE), lambda k, i, c_ref: (c_ref[0], k, i, 0)),
                      pl.BlockSpec((1, tr, LANE), lambda k, i, c_ref: (k, i, 0))],
            out_specs=pl.BlockSpec((1, tr, LANE), lambda k, i, c_ref: (k, i, 0))),
        out_shape=jax.ShapeDtypeStruct(theirs.shape, BF16),
        compiler_params=_params(("parallel", "parallel")),
    )(c, packed, theirs)


def _chip_exchange(cfg, sums):
    def body(s_ref, o_ref, send_sems, recv_sems, local_sem):
        x, y, c = _me()
        chip = 2 * x + y
        chips = _other_chips(x, y)
        local = pltpu.make_async_copy(s_ref.at[chip], o_ref.at[chip], local_sem)
        local.start()
        cps = [pltpu.make_async_remote_copy(src_ref=s_ref.at[2 * cx + cy], dst_ref=o_ref.at[chip], send_sem=send_sems.at[j],
                                            recv_sem=recv_sems.at[j], device_id=(cx, cy, c), device_id_type=MESH_ID)
               for j, (cx, cy) in enumerate(chips)]
        for cp in cps:
            cp.start()
        for j, (cx, cy) in enumerate(chips):
            theirs = o_ref.at[2 * cx + cy]
            pltpu.make_async_remote_copy(src_ref=theirs, dst_ref=theirs, send_sem=send_sems.at[j], recv_sem=recv_sems.at[j],
                                         device_id=(cx, cy, c), device_id_type=MESH_ID).wait_recv()
        for cp in cps:
            cp.wait_send()
        local.wait()

    return pl.pallas_call(
        body, name="grad_chip_exchange", in_specs=[_ANY], out_specs=_ANY,
        out_shape=jax.ShapeDtypeStruct(sums.shape, sums.dtype),
        scratch_shapes=[pltpu.SemaphoreType.DMA((3,)), pltpu.SemaphoreType.DMA((3,)), pltpu.SemaphoreType.DMA],
    )(sums)


def _chip_sum(cfg, pieces):
    tr = _pick(cfg.R, 2048, 16)

    def body(p_ref, o_ref):
        acc = p_ref[0].astype(F32)
        for k in range(1, N_CHIPS):
            acc = acc + p_ref[k].astype(F32)
        o_ref[...] = acc

    return pl.pallas_call(
        body, name="grad_chip_sum", grid=(cfg.R // tr,),
        in_specs=[pl.BlockSpec((N_CHIPS, tr, LANE), lambda i: (0, i, 0))],
        out_specs=pl.BlockSpec((tr, LANE), lambda i: (i, 0)),
        out_shape=jax.ShapeDtypeStruct((cfg.R, LANE), F32),
        compiler_params=_params(("parallel",)),
    )(pieces)


def _sibling_join(cfg, half):
    def body(h_ref, o_ref, send_sem, recv_sem, local_sem):
        x, y, c = _me()
        local = pltpu.make_async_copy(h_ref, o_ref.at[c], local_sem)
        local.start()
        cp = pltpu.make_async_remote_copy(src_ref=h_ref, dst_ref=o_ref.at[c], send_sem=send_sem, recv_sem=recv_sem,
                                          device_id=(x, y, 1 - c), device_id_type=MESH_ID)
        cp.start()
        other = o_ref.at[1 - c]
        pltpu.make_async_remote_copy(src_ref=other, dst_ref=other, send_sem=send_sem, recv_sem=recv_sem,
                                     device_id=(x, y, 1 - c), device_id_type=MESH_ID).wait_recv()
        cp.wait_send()
        local.wait()

    return pl.pallas_call(
        body, name="grad_sibling_join", in_specs=[_ANY], out_specs=_ANY,
        out_shape=jax.ShapeDtypeStruct((2,) + half.shape, half.dtype),
        scratch_shapes=[pltpu.SemaphoreType.DMA, pltpu.SemaphoreType.DMA, pltpu.SemaphoreType.DMA],
    )(half)


def _allreduce_small(name, vec):
    def body(v_ref, o_ref, buf_ref, send_sems, recv_sems):
        x, y, c = _me()
        me = 4 * x + 2 * y + c
        cps = []
        for p in range(1, 8):
            px, py, pc = x ^ (p >> 2), y ^ ((p >> 1) & 1), c ^ (p & 1)
            cps.append(pltpu.make_async_remote_copy(src_ref=v_ref, dst_ref=buf_ref.at[me], send_sem=send_sems.at[p - 1],
                                                    recv_sem=recv_sems.at[p - 1], device_id=(px, py, pc), device_id_type=MESH_ID))
            cps[-1].start()
        buf_ref[me] = v_ref[...]
        for p in range(1, 8):
            theirs = buf_ref.at[me ^ p]
            pltpu.make_async_remote_copy(src_ref=theirs, dst_ref=theirs, send_sem=send_sems.at[p - 1], recv_sem=recv_sems.at[p - 1],
                                         device_id=(x, y, c), device_id_type=MESH_ID).wait_recv()
        for cp in cps:
            cp.wait_send()
        acc = buf_ref[0]
        for k in range(1, 8):
            acc = acc + buf_ref[k]
        o_ref[...] = acc

    vm = pl.BlockSpec(memory_space=pltpu.VMEM)
    return pl.pallas_call(
        body, name=name, in_specs=[vm], out_specs=vm, out_shape=jax.ShapeDtypeStruct(vec.shape, F32),
        scratch_shapes=[pltpu.VMEM((8,) + vec.shape, F32), pltpu.SemaphoreType.DMA((7,)), pltpu.SemaphoreType.DMA((7,))],
    )(vec)


def _adamw(name, w, g, m, v):
    R, C = w.shape
    tr = _pick(R, max(8, (1 << 19) // max(C, 1) // 8 * 8), 8)

    def body(w_ref, g_ref, m_ref, v_ref, d_ref, nm_ref, nv_ref):
        g = g_ref[...]
        m = ADAM_B1 * m_ref[...] + (1.0 - ADAM_B1) * g
        v = ADAM_B2 * v_ref[...] + (1.0 - ADAM_B2) * (g * g)
        m_hat = m / (1.0 - ADAM_B1 ** ADAM_STEP)
        v_hat = v / (1.0 - ADAM_B2 ** ADAM_STEP)
        d_ref[...] = -ADAM_LR * (m_hat / (jnp.sqrt(v_hat) + ADAM_EPS) + ADAM_WD * w_ref[...])
        nm_ref[...] = m
        nv_ref[...] = v

    blk = pl.BlockSpec((tr, C), lambda i: (i, 0))
    return pl.pallas_call(
        body, name=name, grid=(R // tr,), in_specs=[blk] * 4, out_specs=[blk] * 3,
        out_shape=[jax.ShapeDtypeStruct((R, C), F32)] * 3, compiler_params=_params(("parallel",)),
    )(w, g, m, v)


def _pack_small(arrs):
    flat = jnp.concatenate([a.reshape(-1) for a in arrs])
    n = -(-flat.shape[0] // (8 * LANE)) * 8 * LANE
    return jnp.pad(flat, (0, n - flat.shape[0])).reshape(8, n // 8)


def _unpack_small(vec, shapes):
    flat, out, off = vec.reshape(-1), [], 0
    for s in shapes:
        out.append(flat[off:off + s[0] * s[1]].reshape(s))
        off += s[0] * s[1]
    return out


def _step(cfg, a):
    chip = 2 * lax.axis_index("x") + lax.axis_index("y")
    big = [n for n, _, _, _ in cfg.BIG]

    gathered = _allgather_weights(cfg, _pack_shards(cfg, {n: a[n] for n in big}, BF16))
    W = {n: _to_kernel_layout(cfg, n, w) for n, w in _chips_to_full(cfg, gathered).items()}

    sp = {n: a[n] for n in SMALL}
    sharded = _pack_small([a[n] for n in SMALL_SHARDED])
    slot = jnp.where(lax.broadcasted_iota(I32, (N_CHIPS,) + sharded.shape, 0) == chip, 0.5 * sharded[None], 0.0)
    allp = _allreduce_small("allgather_small", slot.reshape(N_CHIPS * 8, -1)).reshape((N_CHIPS,) + sharded.shape)
    per_chip = [_unpack_small(allp[ch], [a[n].shape for n in SMALL_SHARDED]) for ch in range(N_CHIPS)]
    for k, n in enumerate(SMALL_SHARDED):
        sp[n] = jnp.concatenate([per_chip[ch][k] for ch in range(N_CHIPS)], axis=1)

    loss, grad_x, gW, gs = _local_grads(cfg, a["x"], a["loss_target"], W, sp)

    packed = _full_to_chips(cfg, {n: _from_kernel_layout(cfg, n, gW[n]) for n in big}, BF16)
    sums = _pair_sum(cfg, packed, _pair_exchange(cfg, packed))
    half = _chip_sum(cfg, _chip_exchange(cfg, sums))
    g_big = _unpack_shards(cfg, _sibling_join(cfg, half))

    shapes = [gs[n].shape for n in SMALL] + [(1, LANE)]
    red = _unpack_small(_allreduce_small("allreduce_small", _pack_small([gs[n] for n in SMALL] + [loss])), shapes)
    g_small = dict(zip(SMALL, red[:-1]))
    for n in SMALL_SHARDED:
        cs = a[n].shape[1]
        g_small[n] = lax.dynamic_slice_in_dim(g_small[n], chip * cs, cs, axis=1)

    out = {"loss": red[-1][0, 0], "grad_x": grad_x}
    for n in big:
        out["grad_" + n] = g_big[n]
        out["delta_" + n], out["new_m_" + n], out["new_v_" + n] = _adamw("adamw_" + n, a[n], g_big[n], a["m_" + n], a["v_" + n])
    sshapes = [a[n].shape for n in SMALL]
    d, nm, nv = _adamw("adamw_small", _pack_small([a[n] for n in SMALL]), _pack_small([g_small[n] for n in SMALL]),
                       _pack_small([a["m_" + n] for n in SMALL]), _pack_small([a["v_" + n] for n in SMALL]))
    for n, dd, mm, vv in zip(SMALL, _unpack_small(d, sshapes), _unpack_small(nm, sshapes), _unpack_small(nv, sshapes)):
        out["grad_" + n], out["delta_" + n], out["new_m_" + n], out["new_v_" + n] = g_small[n], dd, mm, vv
    return out


def kernel(x, mix_pre_g, w_in, q_norm_g, w_uq, kv_norm_g, w_ukv, ssm_conv_w, ssm_conv_b, dt_bias, a_log, d_skip, ssm_norm_g, w_out, mix_post_g, ffn_pre_g, w_gate, w_up, ffn_conv_w, ffn_conv_b, w_down, ffn_post_g, loss_target, m_mix_pre_g, m_w_in, m_q_norm_g, m_w_uq, m_kv_norm_g, m_w_ukv, m_ssm_conv_w, m_ssm_conv_b, m_dt_bias, m_a_log, m_d_skip, m_ssm_norm_g, m_w_out, m_mix_post_g, m_ffn_pre_g, m_w_gate, m_w_up, m_ffn_conv_w, m_ffn_conv_b, m_w_down, m_ffn_post_g, v_mix_pre_g, v_w_in, v_q_norm_g, v_w_uq, v_kv_norm_g, v_w_ukv, v_ssm_conv_w, v_ssm_conv_b, v_dt_bias, v_a_log, v_d_skip, v_ssm_norm_g, v_w_out, v_mix_post_g, v_ffn_pre_g, v_w_gate, v_w_up, v_ffn_conv_w, v_ffn_conv_b, v_w_down, v_ffn_post_g):
    args = dict(locals())
    out = _step(_FULL, {k: (v[0] if v.ndim == 3 else v) for k, v in args.items()})
    res = [out["loss"], out["grad_x"][None]]
    for pre in ("grad_", "delta_", "new_m_", "new_v_"):
        res += [out[pre + n][None] if args[n].ndim == 3 else out[pre + n] for n in WEIGHTS]
    return tuple(res)
```

```python
import functools
import math

import jax
import jax.numpy as jnp
from jax import lax
from jax.experimental import pallas as pl
from jax.experimental.pallas import tpu as pltpu

F32, BF16, I32 = jnp.float32, jnp.bfloat16, jnp.int32
NN = (((1,), (0,)), ((), ()))
NT = (((1,), (1,)), ((), ()))
TN = (((0,), (0,)), ((), ()))
HI = lax.Precision.HIGHEST
MESH_ID = pl.DeviceIdType.MESH

EPS = 1e-6
CHUNK = 64
NOPE, ROPE, VH = 128, 64, 128
ROPE_THETA = 10000.0
HP, NST = 64, 128
SSM_K, FFN_K = 4, 3
LANE = 128
N_CHIPS = 4
VMEM_LIMIT = 52 * 1024 * 1024

ADAM_LR, ADAM_B1, ADAM_B2, ADAM_EPS, ADAM_WD, ADAM_STEP = 0.001, 0.9, 0.999, 1e-08, 0.01, 10


class _Cfg:
    def __init__(self, S, D, QL, KVL, H, HS, G, DFF, T):
        self.S, self.D, self.QL, self.KVL, self.H, self.HS, self.G, self.DFF, self.T = S, D, QL, KVL, H, HS, G, DFF, T
        self.INNER = HS * HP
        self.CONVCH = self.INNER + 2 * G * NST
        self.QW = H * (NOPE + ROPE)
        self.KVW = H * (NOPE + VH)
        self.MLAW = H * VH
        self.MIXW = self.MLAW + self.INNER
        self.IN_COLS = QL + KVL + ROPE + self.INNER + self.CONVCH + HS
        self.o_kr = QL + KVL
        self.o_z = self.o_kr + LANE
        self.o_xbc = self.o_z + self.INNER
        self.o_dt = self.o_xbc + self.CONVCH
        self.EXT = self.o_dt + LANE
        self.NPAIR = HS // 2
        self.REP = HS // G
        self.BIG = (("w_in", D, self.IN_COLS, 1), ("w_uq", QL, self.QW, 1), ("w_ukv", KVL, self.KVW, 1),
                    ("w_out", self.MIXW, D, 0), ("w_gate", D, DFF, 1), ("w_up", D, DFF, 1), ("w_down", DFF, D, 0))
        self.NSHARD = sum(r * c for _, r, c, _ in self.BIG) // N_CHIPS
        unit = 2 * LANE * 16
        self.NPACK = -(-self.NSHARD // unit) * unit
        self.R = self.NPACK // (2 * LANE)


_FULL = _Cfg(S=2048, D=2048, QL=768, KVL=512, H=8, HS=16, G=2, DFF=5632, T=256)

SMALL = ("mix_pre_g", "q_norm_g", "kv_norm_g", "ssm_conv_w", "ssm_conv_b", "dt_bias", "a_log", "d_skip", "ssm_norm_g",
         "mix_post_g", "ffn_pre_g", "ffn_conv_w", "ffn_conv_b", "ffn_post_g")
SMALL_SHARDED = ("ssm_conv_w", "ffn_conv_w")
WEIGHTS = ("mix_pre_g", "w_in", "q_norm_g", "w_uq", "kv_norm_g", "w_ukv", "ssm_conv_w", "ssm_conv_b", "dt_bias", "a_log",
           "d_skip", "ssm_norm_g", "w_out", "mix_post_g", "ffn_pre_g", "w_gate", "w_up", "ffn_conv_w", "ffn_conv_b",
           "w_down", "ffn_post_g")


def _pick(n, target, mult):
    best = None
    for d in range(mult, min(n, target) + 1, mult):
        if n % d == 0:
            best = d
    return best if best is not None else n


def _params(sem=None):
    kw = dict(vmem_limit_bytes=VMEM_LIMIT)
    if sem is not None:
        kw["dimension_semantics"] = sem
    return pltpu.CompilerParams(**kw)


def _dot(a, b, dims=NN, precision=None):
    return lax.dot_general(a, b, dims, preferred_element_type=F32, precision=precision)


def _sigmoid(x):
    return 1.0 / (1.0 + jnp.exp(-x))


def _rs(x):
    return lax.rsqrt(jnp.mean(x * x, axis=-1, keepdims=True) + EPS)


def _rms_back(xh, r, dn):
    return r * (dn - xh * jnp.mean(dn * xh, axis=-1, keepdims=True))


def _colsum(v):
    return jnp.sum(v, axis=0, keepdims=True)


def _matmul(name, a, b, mode, out_dtype, a2=None, b2=None, chips=False):
    cs = None
    if mode == "nn":
        (M, K), N = a.shape, b.shape[-1]
        if chips:
            cs, N = N, N_CHIPS * N
    elif mode == "nt":
        (M, K), N = a.shape, b.shape[-2]
        if chips:
            cs = b.shape[-1]
    else:
        (K, M), N = a.shape, b.shape[1]
        if chips:
            cs = N // N_CHIPS
    tm = _pick(M, 1024, LANE)
    tn = _pick(cs if chips and mode != "nt" else N, 1408, LANE)
    tk = _pick(cs, 1408, LANE) if chips and mode == "nt" else _pick(K, 512, LANE)
    nk = K // tk
    dims = {"nn": NN, "nt": NT, "tn": TN}[mode]
    a_spec = pl.BlockSpec((tk, tm), lambda i, j, k: (k, i)) if mode == "tn" else pl.BlockSpec((tm, tk), lambda i, j, k: (i, k))
    b_spec = pl.BlockSpec((tn, tk), lambda i, j, k: (j, k)) if mode == "nt" else pl.BlockSpec((tk, tn), lambda i, j, k: (k, j))
    o_spec = pl.BlockSpec((tm, tn), lambda i, j, k: (i, j))
    o_shape = (M, N)
    if chips and mode == "nn":
        per = cs // tn
        b_spec = pl.BlockSpec((None, tk, tn), lambda i, j, k: (j // per, k, j % per))
    elif chips and mode == "nt":
        per = cs // tk
        b_spec = pl.BlockSpec((None, tn, tk), lambda i, j, k: (k // per, j, k % per))
    elif chips:
        per = cs // tn
        o_spec = pl.BlockSpec((None, tm, tn), lambda i, j, k: (j // per, i, j % per))
        o_shape = (N_CHIPS, M, cs)
    two = a2 is not None

    def body(*refs):
        o_ref, acc_ref = refs[-2], refs[-1]
        k = pl.program_id(2)

        @pl.when(k == 0)
        def _():
            acc_ref[...] = jnp.zeros_like(acc_ref)

        part = _dot(refs[0][...].astype(BF16), refs[1][...].astype(BF16), dims)
        if two:
            part += _dot(refs[2][...].astype(BF16), refs[3][...].astype(BF16), dims)
        acc_ref[...] += part

        @pl.when(k == nk - 1)
        def _():
            o_ref[...] = acc_ref[...].astype(o_ref.dtype)

    ins = (a, b, a2, b2) if two else (a, b)
    return pl.pallas_call(
        body, name=name, grid=(M // tm, N // tn, nk),
        in_specs=[a_spec, b_spec] * (2 if two else 1),
        out_specs=o_spec,
        out_shape=jax.ShapeDtypeStruct(o_shape, out_dtype),
        scratch_shapes=[pltpu.VMEM((tm, tn), F32)],
        compiler_params=_params(("parallel", "parallel", "arbitrary")),
    )(*ins)


def _rowwise(name, fn, rows, mats, outs, reds, ts):
    S = rows[0].shape[0]
    nr, nm, no = len(rows), len(mats), len(outs)

    def body(*refs):
        res = fn(*[r[...] for r in refs[:nr + nm]])
        res = res if isinstance(res, (tuple, list)) else (res,)
        for r, v in zip(refs[nr + nm:nr + nm + no], res[:no]):
            r[...] = v.astype(r.dtype)
        first = pl.program_id(0) == 0
        for r, v in zip(refs[nr + nm + no:], res[no:]):
            @pl.when(first)
            def _():
                r[...] = jnp.broadcast_to(v, r.shape)

            @pl.when(jnp.logical_not(first))
            def _():
                r[...] += jnp.broadcast_to(v, r.shape)

    in_specs = [pl.BlockSpec((ts, a.shape[1]), lambda i: (i, 0)) for a in rows]
    in_specs += [pl.BlockSpec(m.shape, lambda i, nd=m.ndim: (0,) * nd) for m in mats]
    out_specs = [pl.BlockSpec((ts, w), lambda i: (i, 0)) for w, _ in outs]
    out_specs += [pl.BlockSpec(s, lambda i: (0, 0)) for s in reds]
    out_shape = [jax.ShapeDtypeStruct((S, w), dt) for w, dt in outs] + [jax.ShapeDtypeStruct(s, F32) for s in reds]
    return pl.pallas_call(
        body, name=name, grid=(S // ts,), in_specs=in_specs, out_specs=out_specs, out_shape=out_shape,
        compiler_params=_params(("arbitrary",) if reds else ("parallel",)),
    )(*rows, *mats)


def _shift_down(v, s):
    if s == 0:
        return v
    rows = lax.broadcasted_iota(I32, v.shape, 0)
    return jnp.where(rows >= s, pltpu.roll(v, s, 0), 0.0)


def _shift_up(v, s):
    if s == 0:
        return v
    n = v.shape[0]
    rows = lax.broadcasted_iota(I32, v.shape, 0)
    return jnp.where(rows < n - s, pltpu.roll(v, n - s, 0), 0.0)


def _conv(x, w, b):
    K = w.shape[0]
    y = jnp.broadcast_to(b, x.shape)
    for k in range(K):
        y = y + w[k:k + 1, :] * _shift_down(x, K - 1 - k)
    return y


def _conv_back(x, w, dc):
    K = w.shape[0]
    dx = jnp.zeros_like(x)
    dw = []
    for k in range(K):
        dx = dx + w[k:k + 1, :] * _shift_up(dc, K - 1 - k)
        dw.append(_colsum(dc * _shift_down(x, K - 1 - k)))
    return dx, jnp.concatenate(dw, axis=0), _colsum(dc)


def _colwise(name, fn, cols, vecs, outs, pouts, tc):
    S, C = cols[0].shape
    nc_, nv, no = len(cols), len(vecs), len(outs)

    def body(*refs):
        res = fn(*[r[...] for r in refs[:nc_ + nv]])
        res = res if isinstance(res, (tuple, list)) else (res,)
        for r, v in zip(refs[nc_ + nv:], res):
            r[...] = v.astype(r.dtype)

    in_specs = [pl.BlockSpec((S, tc), lambda j: (0, j)) for _ in cols]
    in_specs += [pl.BlockSpec((v.shape[0], tc), lambda j: (0, j)) for v in vecs]
    out_specs = [pl.BlockSpec((S, tc), lambda j: (0, j)) for _ in outs] + [pl.BlockSpec((k, tc), lambda j: (0, j)) for k in pouts]
    out_shape = [jax.ShapeDtypeStruct((S, C), dt) for dt in outs] + [jax.ShapeDtypeStruct((k, C), F32) for k in pouts]
    return pl.pallas_call(
        body, name=name, grid=(C // tc,), in_specs=in_specs, out_specs=out_specs, out_shape=out_shape,
        compiler_params=_params(("parallel",)),
    )(*cols, *vecs)


_G0, _G1 = math.sqrt(2.0 / math.pi), 0.044715


def _gelu(g):
    th = jnp.tanh(_G0 * (g + _G1 * g * g * g))
    return 0.5 * g * (1.0 + th), th


def _ffn_act(gate_pre, up, w, b):
    act, _ = _gelu(_conv(gate_pre, w, b))
    return act * up


def _ffn_act_back(dact, gate_pre, up, w, b):
    g = _conv(gate_pre, w, b)
    ge, th = _gelu(g)
    dge = 0.5 * (1.0 + th) + 0.5 * g * (1.0 - th * th) * _G0 * (1.0 + 3.0 * _G1 * g * g)
    dup = dact * ge
    dgate_pre, dw, db = _conv_back(gate_pre, w, dact * up * dge)
    return dgate_pre, dup, dw, db


def _ssm_act(xbc, w, b):
    c = _conv(xbc, w, b)
    return c * _sigmoid(c)


def _ssm_act_back(dxc, xbc, w, b):
    c = _conv(xbc, w, b)
    sg = _sigmoid(c)
    return _conv_back(xbc, w, dxc * sg * (1.0 + c * (1.0 - sg)))


def _rope_tables(S):
    inv = 1.0 / (ROPE_THETA ** (jnp.arange(0, ROPE, 2, dtype=F32) / ROPE))
    ang = jnp.arange(S, dtype=F32)[:, None] * inv[None, :]
    cos, sin = jnp.cos(ang), jnp.sin(ang)
    return jnp.tile(cos, (1, 4)), jnp.tile(jnp.concatenate([-sin, sin], axis=1), (1, 2))


def _swap_halves(x):
    lane = lax.broadcasted_iota(I32, x.shape, 1)
    w = x.shape[1]
    return jnp.where((lane % ROPE) < ROPE // 2, pltpu.roll(x, w - ROPE // 2, 1), pltpu.roll(x, ROPE // 2, 1))


def _rot(x, cos2, sin2):
    return x * cos2 + _swap_halves(x) * sin2


def _rot_back(dy, cos2, sin2):
    return dy * cos2 + _swap_halves(dy * sin2)


def _mla_pack(cfg, q, kv, kr, cos2, sin2):
    S, H = cfg.S, cfg.H
    ts = _pick(S, 512, 8)

    def body(qn_ref, qr_ref, kn_ref, v_ref, kr_ref, c_ref, s_ref, Q_ref, K_ref, V_ref):
        h = pl.program_id(0)
        c2, s2 = c_ref[...], s_ref[...]
        Q_ref[0, :, 0:LANE] = qn_ref[...].astype(BF16)
        Q_ref[0, :, LANE:] = _rot(qr_ref[...], c2, s2).astype(BF16)
        K_ref[0, :, 0:LANE] = kn_ref[...].astype(BF16)
        krr = _rot(kr_ref[...], c2, s2)
        K_ref[0, :, LANE:] = jnp.where(h % 2 == 1, pltpu.roll(krr, ROPE, 1), krr).astype(BF16)
        V_ref[0] = v_ref[...].astype(BF16)

    blk = lambda f: pl.BlockSpec((ts, LANE), f)
    return pl.pallas_call(
        body, name="mla_pack", grid=(H, S // ts),
        in_specs=[blk(lambda h, i: (i, h)), blk(lambda h, i: (i, H + h // 2)), blk(lambda h, i: (i, h)),
                  blk(lambda h, i: (i, H + h)), blk(lambda h, i: (i, 0)), blk(lambda h, i: (i, 0)), blk(lambda h, i: (i, 0))],
        out_specs=[pl.BlockSpec((1, ts, 2 * LANE), lambda h, i: (h, i, 0)), pl.BlockSpec((1, ts, 2 * LANE), lambda h, i: (h, i, 0)),
                   pl.BlockSpec((1, ts, LANE), lambda h, i: (h, i, 0))],
        out_shape=[jax.ShapeDtypeStruct((H, S, 2 * LANE), BF16), jax.ShapeDtypeStruct((H, S, 2 * LANE), BF16),
                   jax.ShapeDtypeStruct((H, S, LANE), BF16)],
        compiler_params=_params(("parallel", "parallel")),
    )(q, q, kv, kv, kr, cos2, sin2)


def _mla_unpack(cfg, dQ, dK, dV, cos2, sin2):
    S, H = cfg.S, cfg.H
    ts = _pick(S, 256, 8)

    def body(dQ_ref, dK_ref, dV_ref, c_ref, s_ref, dq_ref, dkv_ref, dkr_ref):
        c2, s2 = c_ref[...], s_ref[...]
        lo = lax.broadcasted_iota(I32, (ts, LANE), 1) < ROPE
        tk = jnp.zeros((ts, LANE), F32)
        for h in range(H):
            dq_ref[:, h * LANE:(h + 1) * LANE] = dQ_ref[h, :, 0:LANE].astype(BF16)
            dkv_ref[:, h * LANE:(h + 1) * LANE] = dK_ref[h, :, 0:LANE].astype(BF16)
            dkv_ref[:, (H + h) * LANE:(H + h + 1) * LANE] = dV_ref[h].astype(BF16)
            own = lo if h % 2 == 0 else jnp.logical_not(lo)
            tk = tk + jnp.where(own, dK_ref[h, :, LANE:], 0.0)
        for j in range(H // 2):
            dr = dQ_ref[2 * j, :, LANE:] + dQ_ref[2 * j + 1, :, LANE:]
            dq_ref[:, (H + j) * LANE:(H + j + 1) * LANE] = _rot_back(dr, c2, s2).astype(BF16)
        dkr_rot = jnp.where(lo, tk + pltpu.roll(tk, ROPE, 1), 0.0)
        dkr_ref[...] = _rot_back(dkr_rot, c2, s2).astype(BF16)

    tab = pl.BlockSpec((ts, LANE), lambda i: (i, 0))
    return pl.pallas_call(
        body, name="mla_unpack", grid=(S // ts,),
        in_specs=[pl.BlockSpec((H, ts, 2 * LANE), lambda i: (0, i, 0)), pl.BlockSpec((H, ts, 2 * LANE), lambda i: (0, i, 0)),
                  pl.BlockSpec((H, ts, LANE), lambda i: (0, i, 0)), tab, tab],
        out_specs=[pl.BlockSpec((ts, cfg.QW), lambda i: (i, 0)), pl.BlockSpec((ts, cfg.KVW), lambda i: (i, 0)), tab],
        out_shape=[jax.ShapeDtypeStruct((S, cfg.QW), BF16), jax.ShapeDtypeStruct((S, cfg.KVW), BF16),
                   jax.ShapeDtypeStruct((S, LANE), BF16)],
        compiler_params=_params(("parallel",)),
    )(dQ, dK, dV, cos2, sin2)


_ATT_T = 256
_ATT_SCALE = (NOPE + ROPE) ** -0.5


def _att_scores(q, k, qi, kb):
    s = _dot(q, k, NT) * _ATT_SCALE
    rows = qi * _ATT_T + lax.broadcasted_iota(I32, s.shape, 0)
    cols = kb * _ATT_T + lax.broadcasted_iota(I32, s.shape, 1)
    return jnp.where((cols // CHUNK) <= (rows // CHUNK), s, -1e30)


def _attn_fwd(cfg, Q, K, V):
    S, H, T = cfg.S, cfg.H, _ATT_T

    def body(q_ref, k_ref, v_ref, o_ref, lse_ref):
        qi = pl.program_id(1)
        q = q_ref[0]

        def step(kb, carry):
            m, l, acc = carry
            ks = pl.multiple_of(kb * T, T)
            s = _att_scores(q, k_ref[0, pl.ds(ks, T), :], qi, kb)
            m_new = jnp.maximum(m, jnp.max(s, axis=1, keepdims=True))
            p = jnp.exp(s - m_new)
            alpha = jnp.exp(m - m_new)
            l = alpha * l + jnp.sum(p, axis=1, keepdims=True)
            acc = alpha * acc + _dot(p.astype(BF16), v_ref[0, pl.ds(ks, T), :])
            return m_new, l, acc

        init = (jnp.full((T, 1), -1e30, F32), jnp.zeros((T, 1), F32), jnp.zeros((T, VH), F32))
        m, l, acc = lax.fori_loop(0, qi + 1, step, init)
        o_ref[...] = acc / l
        lse_ref[...] = jnp.broadcast_to(m + jnp.log(l), (T, LANE))

    return pl.pallas_call(
        body, name="attn_fwd", grid=(H, S // T),
        in_specs=[pl.BlockSpec((1, T, 2 * LANE), lambda h, i: (h, i, 0)), pl.BlockSpec((1, S, 2 * LANE), lambda h, i: (h, 0, 0)),
                  pl.BlockSpec((1, S, LANE), lambda h, i: (h, 0, 0))],
        out_specs=[pl.BlockSpec((T, LANE), lambda h, i: (i, h)), pl.BlockSpec((T, LANE), lambda h, i: (i, h))],
        out_shape=[jax.ShapeDtypeStruct((S, H * LANE), F32), jax.ShapeDtypeStruct((S, H * LANE), F32)],
        compiler_params=_params(("parallel", "parallel")),
    )(Q, K, V)


def _attn_dq(cfg, Q, K, V, do, o, lse):
    S, H, T = cfg.S, cfg.H, _ATT_T

    def body(q_ref, k_ref, v_ref, do_ref, o_ref, lse_ref, dq_ref, dl_ref):
        qi = pl.program_id(1)
        q = q_ref[0]
        do = do_ref[...]
        delta = jnp.sum(do * o_ref[...], axis=1, keepdims=True)
        lse = lse_ref[:, 0:1]
        dob = do.astype(BF16)

        def step(kb, dq):
            ks = pl.multiple_of(kb * T, T)
            k = k_ref[0, pl.ds(ks, T), :]
            p = jnp.exp(_att_scores(q, k, qi, kb) - lse)
            dp = _dot(dob, v_ref[0, pl.ds(ks, T), :], NT)
            ds = p * (dp - delta) * _ATT_SCALE
            return dq + _dot(ds.astype(BF16), k)

        dq_ref[0] = lax.fori_loop(0, qi + 1, step, jnp.zeros((T, 2 * LANE), F32))
        dl_ref[...] = jnp.broadcast_to(delta, (T, LANE))

    col = pl.BlockSpec((T, LANE), lambda h, i: (i, h))
    return pl.pallas_call(
        body, name="attn_dq", grid=(H, S // T),
        in_specs=[pl.BlockSpec((1, T, 2 * LANE), lambda h, i: (h, i, 0)), pl.BlockSpec((1, S, 2 * LANE), lambda h, i: (h, 0, 0)),
                  pl.BlockSpec((1, S, LANE), lambda h, i: (h, 0, 0)), col, col, col],
        out_specs=[pl.BlockSpec((1, T, 2 * LANE), lambda h, i: (h, i, 0)), col],
        out_shape=[jax.ShapeDtypeStruct((H, S, 2 * LANE), F32), jax.ShapeDtypeStruct((S, H * LANE), F32)],
        compiler_params=_params(("parallel", "parallel")),
    )(Q, K, V, do, o, lse)


def _attn_dkv(cfg, Q, K, V, do, lse, delta):
    S, H, T = cfg.S, cfg.H, _ATT_T
    nq = S // T

    def body(q_ref, k_ref, v_ref, do_ref, lse_ref, dl_ref, dk_ref, dv_ref):
        kb = pl.program_id(1)
        k, v = k_ref[0], v_ref[0]

        def step(qi, carry):
            dk, dv = carry
            qs = pl.multiple_of(qi * T, T)
            q = q_ref[0, pl.ds(qs, T), :]
            dob = do_ref[pl.ds(qs, T), :].astype(BF16)
            p = jnp.exp(_att_scores(q, k, qi, kb) - lse_ref[pl.ds(qs, T), 0:1])
            dv = dv + _dot(p.astype(BF16), dob, TN)
            dp = _dot(dob, v, NT)
            ds = p * (dp - dl_ref[pl.ds(qs, T), 0:1]) * _ATT_SCALE
            dk = dk + _dot(ds.astype(BF16), q, TN)
            return dk, dv

        dk, dv = lax.fori_loop(kb, nq, step, (jnp.zeros((T, 2 * LANE), F32), jnp.zeros((T, VH), F32)))
        dk_ref[0] = dk
        dv_ref[0] = dv

    col = pl.BlockSpec((S, LANE), lambda h, j: (0, h))
    return pl.pallas_call(
        body, name="attn_dkv", grid=(H, S // T),
        in_specs=[pl.BlockSpec((1, S, 2 * LANE), lambda h, j: (h, 0, 0)), pl.BlockSpec((1, T, 2 * LANE), lambda h, j: (h, j, 0)),
                  pl.BlockSpec((1, T, LANE), lambda h, j: (h, j, 0)), col, col, col],
        out_specs=[pl.BlockSpec((1, T, 2 * LANE), lambda h, j: (h, j, 0)), pl.BlockSpec((1, T, LANE), lambda h, j: (h, j, 0))],
        out_shape=[jax.ShapeDtypeStruct((H, S, 2 * LANE), F32), jax.ShapeDtypeStruct((H, S, LANE), F32)],
        compiler_params=_params(("parallel", "parallel")),
    )(Q, K, V, do, lse, delta)


def _expand_matrix(cfg):
    r = lax.broadcasted_iota(I32, (LANE, cfg.INNER), 0)
    c = lax.broadcasted_iota(I32, (LANE, cfg.INNER), 1)
    return (r == c // HP).astype(F32)


def _softplus(x):
    return jnp.maximum(x, 0.0) + jnp.log(1.0 + jnp.exp(-jnp.abs(x)))


def _ssd_prep(cfg, dt_raw, dt_bias_pad, a_log_pad, expand):
    HS = cfg.HS

    def fn(raw, bias, alog, E):
        heads = lax.broadcasted_iota(I32, raw.shape, 1) < HS
        dt = jnp.where(heads, _softplus(raw + bias), 0.0)
        a = dt * jnp.where(heads[0:1], -jnp.exp(alog), 0.0)
        return dt, a, _dot(dt, E, precision=HI), _dot(a, E, precision=HI)

    return _rowwise("ssd_prep", fn, [dt_raw], [dt_bias_pad, a_log_pad, expand],
                    [(LANE, F32), (LANE, F32), (cfg.INNER, F32), (cfg.INNER, F32)], [], _pick(cfg.S, 512, 8))


def _tril(T):
    return lax.broadcasted_iota(I32, (T, T), 0) >= lax.broadcasted_iota(I32, (T, T), 1)


def _ssd_fwd(cfg, xc, dt_exp, a_exp, a_small, dskip_exp):
    S, T, INNER, G, NPAIR = cfg.S, cfg.T, cfg.INNER, cfg.G, cfg.NPAIR
    NC = S // T

    def body(xc_ref, dte_ref, ae_ref, as_ref, dsk_ref, y_ref, hin_ref, ht_ref):
        @pl.when(pl.program_id(0) == 0)
        def _():
            ht_ref[...] = jnp.zeros_like(ht_ref)

        tril = _tril(T)
        tri = tril.astype(F32)
        acs_s = _dot(tri, as_ref[...], precision=HI)
        acs_e = _dot(tri, ae_ref[...], precision=HI)
        acs_t = acs_s.T
        lo = lax.broadcasted_iota(I32, (T, LANE), 1) < HP
        for g in range(G):
            Bb = xc_ref[:, INNER + g * NST:INNER + (g + 1) * NST].astype(BF16)
            Cb = xc_ref[:, INNER + (G + g) * NST:INNER + (G + g + 1) * NST].astype(BF16)
            Gm = _dot(Cb, Bb, NT)
            for j in range(g * NPAIR // G, (g + 1) * NPAIR // G):
                sl = slice(j * LANE, (j + 1) * LANE)
                Xp = xc_ref[:, sl]
                Xdt = Xp * dte_ref[:, sl]
                Xb = Xdt.astype(BF16)
                acs_p = acs_e[:, sl]
                last = acs_p[T - 1:T, :]
                Hin = ht_ref[j]
                hin_ref[0, j] = Hin
                yd = []
                for e in (0, 1):
                    h = 2 * j + e
                    Lm = jnp.exp(jnp.where(tril, acs_s[:, h:h + 1] - acs_t[h:h + 1, :], -1e30))
                    yd.append(_dot((Gm * Lm).astype(BF16), Xb))
                y_off = _dot(Cb, Hin.astype(BF16)) * jnp.exp(acs_p)
                y_ref[:, sl] = jnp.where(lo, yd[0], yd[1]) + y_off + Xp * dsk_ref[:, sl]
                st = _dot(Bb, (Xdt * jnp.exp(last - acs_p)).astype(BF16), TN)
                ht_ref[j] = jnp.exp(last) * Hin + st

    rows = lambda w: pl.BlockSpec((T, w), lambda c: (c, 0))
    return pl.pallas_call(
        body, name="ssd_fwd", grid=(NC,),
        in_specs=[rows(cfg.CONVCH), rows(INNER), rows(INNER), rows(LANE), pl.BlockSpec((1, INNER), lambda c: (0, 0))],
        out_specs=[rows(INNER), pl.BlockSpec((1, NPAIR, NST, LANE), lambda c: (c, 0, 0, 0))],
        out_shape=[jax.ShapeDtypeStruct((S, INNER), F32), jax.ShapeDtypeStruct((NC, NPAIR, NST, LANE), F32)],
        scratch_shapes=[pltpu.VMEM((NPAIR, NST, LANE), F32)],
        compiler_params=_params(("arbitrary",)),
    )(xc, dt_exp, a_exp, a_small, dskip_exp)


def _ssd_bwd(cfg, dy, xc, dt_exp, a_exp, a_small, dskip_exp, hin, dt_raw, dt_bias_pad, a_log_pad, expand):
    S, T, INNER, G, NPAIR, HS = cfg.S, cfg.T, cfg.INNER, cfg.G, cfg.NPAIR, cfg.HS
    NC = S // T

    def body(dy_ref, xc_ref, dte_ref, ae_ref, as_ref, dsk_ref, hin_ref, raw_ref, bias_ref, alog_ref, e_ref,
             dxc_ref, draw_ref, dbias_ref, dalog_ref, dskip_ref, dht_ref, cols_ref, rows_ref, dacs_ref, ddt_ref):
        first = pl.program_id(0) == 0

        @pl.when(first)
        def _():
            dht_ref[...] = jnp.zeros_like(dht_ref)

        tril = _tril(T)
        tri = tril.astype(F32)
        a_s = as_ref[...]
        acs_s = _dot(tri, a_s, precision=HI)
        acs_e = _dot(tri, ae_ref[...], precision=HI)
        acs_t = acs_s.T
        lo = lax.broadcasted_iota(I32, (T, LANE), 1) < HP
        last_row = lax.broadcasted_iota(I32, (T, LANE), 0) == T - 1
        cols_ref[...] = jnp.zeros_like(cols_ref)
        rows_ref[...] = jnp.zeros_like(rows_ref)
        dsk_parts = []
        for g in range(G):
            bsl = slice(INNER + g * NST, INNER + (g + 1) * NST)
            csl = slice(INNER + (G + g) * NST, INNER + (G + g + 1) * NST)
            Bb = xc_ref[:, bsl].astype(BF16)
            Cb = xc_ref[:, csl].astype(BF16)
            Gm = _dot(Cb, Bb, NT)
            dG = jnp.zeros((T, T), F32)
            dB = jnp.zeros((T, NST), F32)
            dC = jnp.zeros((T, NST), F32)
            for j in range(g * NPAIR // G, (g + 1) * NPAIR // G):
                sl = slice(j * LANE, (j + 1) * LANE)
                Xp = xc_ref[:, sl]
                dtp = dte_ref[:, sl]
                Xdt = Xp * dtp
                Xb = Xdt.astype(BF16)
                acs_p = acs_e[:, sl]
                last = acs_p[T - 1:T, :]
                e_p, dec, cd = jnp.exp(acs_p), jnp.exp(last - acs_p), jnp.exp(last)
                Hin = hin_ref[0, j]
                Hb = Hin.astype(BF16)
                dHn = dht_ref[j]
                dHb = dHn.astype(BF16)
                dYp = dy_ref[:, sl]
                z = _dot(Cb, Hb)
                dz = (dYp * e_p).astype(BF16)
                dacs_p = dYp * z * e_p
                dC = dC + _dot(dz, Hb, NT)
                dHin = _dot(Cb, dz, TN) + cd * dHn
                dlast = _colsum(dHn * Hin) * cd
                qv = _dot(Bb, dHb)
                dXdt = qv * dec
                ddec = qv * Xdt * dec
                dacs_p = dacs_p - ddec
                dlast = dlast + _colsum(ddec)
                dB = dB + _dot((Xdt * dec).astype(BF16), dHb, NT)
                for e in (0, 1):
                    h = 2 * j + e
                    Lm = jnp.exp(jnp.where(tril, acs_s[:, h:h + 1] - acs_t[h:h + 1, :], -1e30))
                    Mh = Gm * Lm
                    dYe = jnp.where(lo if e == 0 else jnp.logical_not(lo), dYp, 0.0).astype(BF16)
                    dM = _dot(dYe, Xb, NT)
                    dXdt = dXdt + _dot(Mh.astype(BF16), dYe, TN)
                    W = dM * Mh
                    cols_ref[:, h:h + 1] = jnp.sum(W, axis=1, keepdims=True)
                    rows_ref[h:h + 1, :] = _colsum(W)
                    dG = dG + dM * Lm
                dacs_ref[:, sl] = dacs_p + jnp.where(last_row, dlast, 0.0)
                ddt_ref[:, sl] = dXdt * Xp
                dxc_ref[:, sl] = dXdt * dtp + dYp * dsk_ref[:, sl]
                dsk_parts.append(_colsum(dYp * Xp))
                dht_ref[j] = dHin
            dGb = dG.astype(BF16)
            dxc_ref[:, bsl] = dB + _dot(dGb, Cb, TN)
            dxc_ref[:, csl] = dC + _dot(dGb, Bb)
        E = e_ref[...]
        dacs_s = cols_ref[...] - rows_ref[...].T + _dot(dacs_ref[...], E, NT, precision=HI)
        da = _dot(tri, dacs_s, TN, precision=HI)
        heads = lax.broadcasted_iota(I32, (1, LANE), 1) < HS
        A = jnp.where(heads, -jnp.exp(alog_ref[...]), 0.0)
        ddt = _dot(ddt_ref[...], E, NT, precision=HI) + da * A
        draw = jnp.where(heads, ddt * _sigmoid(raw_ref[...] + bias_ref[...]), 0.0)
        draw_ref[...] = draw
        dsk = _dot(jnp.broadcast_to(jnp.concatenate(dsk_parts, axis=1), (8, INNER)), E, NT, precision=HI)[0:1]
        for ref, val in ((dbias_ref, _colsum(draw)), (dalog_ref, _colsum(da * a_s)), (dskip_ref, dsk)):
            @pl.when(first)
            def _():
                ref[...] = val

            @pl.when(jnp.logical_not(first))
            def _():
                ref[...] += val

    rows = lambda w: pl.BlockSpec((T, w), lambda c: (NC - 1 - c, 0))
    vec = lambda w: pl.BlockSpec((1, w), lambda c: (0, 0))
    return pl.pallas_call(
        body, name="ssd_bwd", grid=(NC,),
        in_specs=[rows(INNER), rows(cfg.CONVCH), rows(INNER), rows(INNER), rows(LANE), vec(INNER),
                  pl.BlockSpec((1, NPAIR, NST, LANE), lambda c: (NC - 1 - c, 0, 0, 0)), rows(LANE), vec(LANE), vec(LANE),
                  pl.BlockSpec((LANE, INNER), lambda c: (0, 0))],
        out_specs=[rows(cfg.CONVCH), rows(LANE), vec(LANE), vec(LANE), vec(LANE)],
        out_shape=[jax.ShapeDtypeStruct((S, cfg.CONVCH), F32), jax.ShapeDtypeStruct((S, LANE), F32)]
        + [jax.ShapeDtypeStruct((1, LANE), F32)] * 3,
        scratch_shapes=[pltpu.VMEM((NPAIR, NST, LANE), F32), pltpu.VMEM((T, LANE), F32), pltpu.VMEM((LANE, T), F32),
                        pltpu.VMEM((T, INNER), F32), pltpu.VMEM((T, INNER), F32)],
        compiler_params=_params(("arbitrary",)),
    )(dy, xc, dt_exp, a_exp, a_small, dskip_exp, hin, dt_raw, dt_bias_pad, a_log_pad, expand)


def _ssd_post(cfg, y, z, norm_g):
    W = cfg.INNER // cfg.G

    def fn(y, z, g):
        yz = y * z * _sigmoid(z)
        return jnp.concatenate([yz[:, i * W:(i + 1) * W] * _rs(yz[:, i * W:(i + 1) * W]) for i in range(cfg.G)], axis=1) * g

    return _rowwise("ssd_post", fn, [y, z], [norm_g], [(cfg.INNER, BF16)], [], _pick(cfg.S, 256, 8))[0]


def _ssd_post_bwd(cfg, db, y, z, norm_g):
    W = cfg.INNER // cfg.G

    def fn(db, y, z, g):
        sg = _sigmoid(z)
        yz = y * z * sg
        dn = db * g
        dyz, nh = [], []
        for i in range(cfg.G):
            seg = yz[:, i * W:(i + 1) * W]
            r = _rs(seg)
            nh.append(seg * r)
            dyz.append(_rms_back(nh[-1], r, dn[:, i * W:(i + 1) * W]))
        dyz = jnp.concatenate(dyz, axis=1)
        return dyz * z * sg, dyz * y * sg * (1.0 + z * (1.0 - sg)), _colsum(db * jnp.concatenate(nh, axis=1))

    return _rowwise("ssd_post_bwd", fn, [db, y, z], [norm_g], [(cfg.INNER, F32), (cfg.INNER, F32)], [(1, cfg.INNER)],
                    _pick(cfg.S, 256, 8))


def _local_grads(cfg, x, tgt, W, sp):
    S, D, H, INNER = cfg.S, cfg.D, cfg.H, cfg.INNER
    ts = _pick(S, 256, 8)
    tc = 256

    xn = _rowwise("rms_pre", lambda x, g: x * _rs(x) * g, [x], [sp["mix_pre_g"]], [(D, BF16)], [], ts)[0]
    u = _matmul("mm_in", xn, W["w_in"], "nn", F32)
    c_q, c_kv = u[:, :cfg.QL], u[:, cfg.QL:cfg.o_kr]
    kr = u[:, cfg.o_kr:cfg.o_z]
    z = u[:, cfg.o_z:cfg.o_xbc]
    xbc = u[:, cfg.o_xbc:cfg.o_dt]
    dt_raw = u[:, cfg.o_dt:]

    cqn = _rowwise("rms_q", lambda x, g: x * _rs(x) * g, [c_q], [sp["q_norm_g"]], [(cfg.QL, BF16)], [], ts)[0]
    ckvn = _rowwise("rms_kv", lambda x, g: x * _rs(x) * g, [c_kv], [sp["kv_norm_g"]], [(cfg.KVL, BF16)], [], ts)[0]
    q = _matmul("mm_uq", cqn, W["w_uq"], "nn", F32)
    kv = _matmul("mm_ukv", ckvn, W["w_ukv"], "nn", F32)
    cos2, sin2 = _rope_tables(S)
    Qh, Kh, Vh = _mla_pack(cfg, q, kv, kr, cos2, sin2)
    a_out, lse = _attn_fwd(cfg, Qh, Kh, Vh)

    pad = lambda v: jnp.pad(v, ((0, 0), (0, LANE - v.shape[1])))
    expand = _expand_matrix(cfg)
    dt_bias_pad, a_log_pad = pad(sp["dt_bias"]), pad(sp["a_log"])
    dskip_exp = jnp.repeat(sp["d_skip"], HP, axis=1)
    xc = _colwise("ssm_act", _ssm_act, [xbc], [sp["ssm_conv_w"], sp["ssm_conv_b"]], [F32], [], tc)[0]
    dt_s, a_s, dt_exp, a_exp = _ssd_prep(cfg, dt_raw, dt_bias_pad, a_log_pad, expand)
    y_ssd, hin = _ssd_fwd(cfg, xc, dt_exp, a_exp, a_s, dskip_exp)
    b_out = _ssd_post(cfg, y_ssd, z, sp["ssm_norm_g"])

    ab_out = jnp.concatenate([a_out.astype(BF16), b_out], axis=1)
    mix = _matmul("mm_out", ab_out, W["w_out"], "nn", F32)

    def mid(x, mix, g_mp, g_fp):
        x1 = x + mix * _rs(mix) * g_mp
        return x1, x1 * _rs(x1) * g_fp

    x1, h2 = _rowwise("fwd_mid", mid, [x, mix], [sp["mix_post_g"], sp["ffn_pre_g"]], [(D, F32), (D, BF16)], [], ts)
    gate_pre = _matmul("mm_gate", h2, W["w_gate"], "nn", F32, chips=True)
    up = _matmul("mm_up", h2, W["w_up"], "nn", F32, chips=True)
    act = _colwise("ffn_act", _ffn_act, [gate_pre, up], [sp["ffn_conv_w"], sp["ffn_conv_b"]], [BF16], [], tc)[0]
    f = _matmul("mm_down", act, W["w_down"], "nn", F32)

    def final(x1, f, t, g):
        r = _rs(f)
        fh = f * r
        err = x1 + fh * g - t
        loss = 0.5 * jnp.sum(jnp.mean(err * err, axis=-1, keepdims=True), axis=0, keepdims=True)
        dy = err * (1.0 / D)
        return dy, _rms_back(fh, r, dy * g), _colsum(dy * fh), loss

    dy, df, g_ffn_post, loss = _rowwise("final", final, [x1, f, tgt], [sp["ffn_post_g"]], [(D, F32), (D, BF16)],
                                        [(1, D), (1, LANE)], ts)
    gW = {}
    dact = _matmul("mm_down_dx", df, W["w_down"], "nt", F32)
    gW["w_down"] = _matmul("mm_down_dw", act, df, "tn", BF16)
    dgate, dup, g_ffn_conv_w, g_ffn_conv_b = _colwise(
        "ffn_act_bwd", _ffn_act_back, [dact, gate_pre, up], [sp["ffn_conv_w"], sp["ffn_conv_b"]], [BF16, BF16], [FFN_K, 1], tc)
    dh2 = _matmul("mm_gu_dx", dgate, W["w_gate"], "nt", F32, dup, W["w_up"], chips=True)
    gW["w_gate"] = _matmul("mm_gate_dw", h2, dgate, "tn", BF16, chips=True)
    gW["w_up"] = _matmul("mm_up_dw", h2, dup, "tn", BF16, chips=True)

    def mid_back(dy, dh2, x1, mix, g_mp, g_fp):
        r2 = _rs(x1)
        xh = x1 * r2
        dx1 = dy + _rms_back(xh, r2, dh2 * g_fp)
        r1 = _rs(mix)
        mh = mix * r1
        return dx1, _rms_back(mh, r1, dx1 * g_mp), _colsum(dh2 * xh), _colsum(dx1 * mh)

    dx1, dmix, g_ffn_pre, g_mix_post = _rowwise("bwd_mid", mid_back, [dy, dh2, x1, mix], [sp["mix_post_g"], sp["ffn_pre_g"]],
                                                [(D, F32), (D, BF16)], [(1, D), (1, D)], ts)
    dab_out = _matmul("mm_out_dx", dmix, W["w_out"], "nt", F32)
    db_out = dab_out[:, cfg.MLAW:]
    gW["w_out"] = _matmul("mm_out_dw", ab_out, dmix, "tn", BF16)

    dy_ssd, dz, g_ssm_norm = _ssd_post_bwd(cfg, db_out, y_ssd, z, sp["ssm_norm_g"])
    dxc, ddt_raw, g_dt_bias, g_a_log, g_d_skip = _ssd_bwd(cfg, dy_ssd, xc, dt_exp, a_exp, a_s, dskip_exp, hin, dt_raw,
                                                          dt_bias_pad, a_log_pad, expand)
    dxbc, g_ssm_conv_w, g_ssm_conv_b = _colwise("ssm_act_bwd", _ssm_act_back, [dxc, xbc], [sp["ssm_conv_w"], sp["ssm_conv_b"]],
                                                [BF16], [SSM_K, 1], tc)

    dQ, delta = _attn_dq(cfg, Qh, Kh, Vh, dab_out, a_out, lse)
    dK, dV = _attn_dkv(cfg, Qh, Kh, Vh, dab_out, lse, delta)
    dq, dkv, dkr = _mla_unpack(cfg, dQ, dK, dV, cos2, sin2)
    dcqn = _matmul("mm_uq_dx", dq, W["w_uq"], "nt", F32)
    dckvn = _matmul("mm_ukv_dx", dkv, W["w_ukv"], "nt", F32)
    gW["w_uq"] = _matmul("mm_uq_dw", cqn, dq, "tn", BF16)
    gW["w_ukv"] = _matmul("mm_ukv_dw", ckvn, dkv, "tn", BF16)

    def rms_back(x, dy, g):
        r = _rs(x)
        xh = x * r
        return _rms_back(xh, r, dy * g), _colsum(dy * xh)

    dc_q, g_q_norm = _rowwise("rms_q_bwd", rms_back, [c_q, dcqn], [sp["q_norm_g"]], [(cfg.QL, BF16)], [(1, cfg.QL)], ts)
    dc_kv, g_kv_norm = _rowwise("rms_kv_bwd", rms_back, [c_kv, dckvn], [sp["kv_norm_g"]], [(cfg.KVL, BF16)], [(1, cfg.KVL)], ts)

    du = jnp.concatenate([dc_q, dc_kv, dkr, dz.astype(BF16), dxbc, ddt_raw.astype(BF16)], axis=1)
    dxn = _matmul("mm_in_dx", du, W["w_in"], "nt", F32)
    gW["w_in"] = _matmul("mm_in_dw", xn, du, "tn", BF16)

    def first_back(dx1, dxn, x, g):
        r = _rs(x)
        xh = x * r
        return dx1 + _rms_back(xh, r, dxn * g), _colsum(dxn * xh)

    grad_x, g_mix_pre = _rowwise("bwd_first", first_back, [dx1, dxn, x], [sp["mix_pre_g"]], [(D, F32)], [(1, D)], ts)

    gs = dict(mix_pre_g=g_mix_pre, q_norm_g=g_q_norm, kv_norm_g=g_kv_norm, ssm_conv_w=g_ssm_conv_w, ssm_conv_b=g_ssm_conv_b,
              dt_bias=g_dt_bias[:, :cfg.HS], a_log=g_a_log[:, :cfg.HS], d_skip=g_d_skip[:, :cfg.HS], ssm_norm_g=g_ssm_norm,
              mix_post_g=g_mix_post, ffn_pre_g=g_ffn_pre, ffn_conv_w=g_ffn_conv_w, ffn_conv_b=g_ffn_conv_b,
              ffn_post_g=g_ffn_post)
    return loss, grad_x, gW, gs


def _to_kernel_layout(cfg, name, w):
    if name == "w_in":
        a = cfg.o_kr + ROPE
        return jnp.concatenate([w[:, :a], jnp.zeros((w.shape[0], LANE - ROPE), w.dtype), w[:, a:],
                                jnp.zeros((w.shape[0], LANE - cfg.HS), w.dtype)], axis=1)
    if name == "w_uq":
        w3 = w.reshape(cfg.QL, cfg.H, NOPE + ROPE)
        return jnp.concatenate([w3[:, :, :NOPE].reshape(cfg.QL, -1), w3[:, :, NOPE:].reshape(cfg.QL, -1)], axis=1)
    if name == "w_ukv":
        w3 = w.reshape(cfg.KVL, cfg.H, NOPE + VH)
        return jnp.concatenate([w3[:, :, :NOPE].reshape(cfg.KVL, -1), w3[:, :, NOPE:].reshape(cfg.KVL, -1)], axis=1)
    return w


def _from_kernel_layout(cfg, name, g):
    if name == "w_in":
        return jnp.concatenate([g[:, :cfg.o_kr + ROPE], g[:, cfg.o_z:cfg.o_dt + cfg.HS]], axis=1)
    if name == "w_uq":
        n = g[:, :cfg.H * NOPE].reshape(cfg.QL, cfg.H, NOPE)
        r = g[:, cfg.H * NOPE:].reshape(cfg.QL, cfg.H, ROPE)
        return jnp.concatenate([n, r], axis=2).reshape(cfg.QL, -1)
    if name == "w_ukv":
        n = g[:, :cfg.H * NOPE].reshape(cfg.KVL, cfg.H, NOPE)
        v = g[:, cfg.H * NOPE:].reshape(cfg.KVL, cfg.H, VH)
        return jnp.concatenate([n, v], axis=2).reshape(cfg.KVL, -1)
    return g


def _cols_to_chips(w):
    r, c = w.shape
    return w.reshape(r, N_CHIPS, c // N_CHIPS).transpose(1, 0, 2)


def _chips_to_cols(g):
    k, r, cs = g.shape
    return g.transpose(1, 0, 2).reshape(r, k * cs)


_CHIP_MAJOR = ("w_gate", "w_up")
_RELAYOUT = ("w_in", "w_uq", "w_ukv")


def _gathered_to_kernel(cfg, name, wg):
    if name in _CHIP_MAJOR:
        return wg
    if name in _RELAYOUT:
        return _to_kernel_layout(cfg, name, _chips_to_cols(wg))
    return wg.reshape(wg.shape[0] * wg.shape[1], wg.shape[2])


def _grad_to_chips(cfg, name, g):
    if name in _CHIP_MAJOR:
        return g
    if name in _RELAYOUT:
        return _cols_to_chips(_from_kernel_layout(cfg, name, g))
    return g.reshape(N_CHIPS, g.shape[0] // N_CHIPS, g.shape[1])


def _me():
    return lax.axis_index("x"), lax.axis_index("y"), lax.axis_index("c")


def _other_chips(x, y):
    return [(1 - x, y), (x, 1 - y), (1 - x, 1 - y)]


_ANY = pl.BlockSpec(memory_space=pl.ANY)


def _row_block(rows, cols, mult):
    return _pick(rows, max(mult, (1 << 19) // cols // mult * mult), mult)


def _scalar(v):
    return v.astype(I32).reshape(1)


def _stage_shard(name, w, chip):
    rs, cs = w.shape
    tr = _row_block(rs, cs, 16)

    def body(chip_ref, w_ref, o_ref):
        o_ref[...] = w_ref[...].astype(BF16)

    return pl.pallas_call(
        body, name="stage_" + name,
        grid_spec=pltpu.PrefetchScalarGridSpec(
            num_scalar_prefetch=1, grid=(rs // tr,),
            in_specs=[pl.BlockSpec((tr, cs), lambda i, chip_ref: (i, 0))],
            out_specs=pl.BlockSpec((None, tr, cs), lambda i, chip_ref: (chip_ref[0], i, 0))),
        out_shape=jax.ShapeDtypeStruct((N_CHIPS, rs, cs), BF16),
        compiler_params=_params(("parallel",)),
    )(_scalar(chip), w)


def _half(ref, k, half):
    h = ref.shape[1] // 2
    return ref.at[k, pl.ds(pl.multiple_of(half * h, 16), h), :]


def _allgather_weights(bufs):
    n = len(bufs)

    def body(*refs):
        outs, send_sems, recv_sems = refs[n:2 * n], refs[2 * n], refs[2 * n + 1]
        x, y, c = _me()
        chip = 2 * x + y
        sib = (x, y, 1 - c)
        chips = _other_chips(x, y)

        def copy(k, part, to):
            return pltpu.make_async_remote_copy(src_ref=part, dst_ref=part, send_sem=send_sems.at[k], recv_sem=recv_sems.at[k],
                                                device_id=to, device_id_type=MESH_ID)

        started = []
        for w, o_ref in enumerate(outs):
            for j, (cx, cy) in enumerate(chips):
                started.append(copy(6 * w + j, _half(o_ref, chip, c), (cx, cy, c)))
                started[-1].start()
        for w, o_ref in enumerate(outs):
            for j, (cx, cy) in enumerate(chips):
                theirs = _half(o_ref, 2 * cx + cy, c)
                copy(6 * w + j, theirs, sib).wait_recv()
                started.append(copy(6 * w + 3 + j, theirs, sib))
                started[-1].start()
        for w, o_ref in enumerate(outs):
            for j, (cx, cy) in enumerate(chips):
                copy(6 * w + 3 + j, _half(o_ref, 2 * cx + cy, 1 - c), sib).wait_recv()
        for cp in started:
            cp.wait_send()

    return pl.pallas_call(
        body, name="allgather_weights", in_specs=[_ANY] * n, out_specs=[_ANY] * n,
        out_shape=[jax.ShapeDtypeStruct(b.shape, b.dtype) for b in bufs],
        input_output_aliases={i: i for i in range(n)},
        scratch_shapes=[pltpu.SemaphoreType.DMA((6 * n,)), pltpu.SemaphoreType.DMA((6 * n,))],
    )(*bufs)


def _pair_exchange(grads):
    n = len(grads)

    def body(*refs):
        ins, outs, send_sems, recv_sems = refs[:n], refs[n:2 * n], refs[2 * n], refs[2 * n + 1]
        x, y, c = _me()
        cps = []
        for w, (g_ref, o_ref) in enumerate(zip(ins, outs)):
            h = o_ref.shape[1]
            src = g_ref.at[:, pl.ds(pl.multiple_of((1 - c) * h, 16), h), :]
            cps.append(pltpu.make_async_remote_copy(src_ref=src, dst_ref=o_ref, send_sem=send_sems.at[w], recv_sem=recv_sems.at[w],
                                                    device_id=(x, y, 1 - c), device_id_type=MESH_ID))
            cps[-1].start()
        for cp in cps:
            cp.wait()

    return pl.pallas_call(
        body, name="grad_pair_exchange", in_specs=[_ANY] * n, out_specs=[_ANY] * n,
        out_shape=[jax.ShapeDtypeStruct((g.shape[0], g.shape[1] // 2, g.shape[2]), g.dtype) for g in grads],
        scratch_shapes=[pltpu.SemaphoreType.DMA((n,)), pltpu.SemaphoreType.DMA((n,))],
    )(*grads)


def _pair_sum(name, g, theirs, c):
    _, h, cs = theirs.shape
    tr = _row_block(h, cs, 16)
    nb = h // tr

    def body(c_ref, a_ref, b_ref, o_ref):
        o_ref[...] = (a_ref[...].astype(F32) + b_ref[...].astype(F32)).astype(o_ref.dtype)

    return pl.pallas_call(
        body, name="pair_sum_" + name,
        grid_spec=pltpu.PrefetchScalarGridSpec(
            num_scalar_prefetch=1, grid=(N_CHIPS, nb),
            in_specs=[pl.BlockSpec((None, tr, cs), lambda k, i, c_ref: (k, c_ref[0] * nb + i, 0)),
                      pl.BlockSpec((None, tr, cs), lambda k, i, c_ref: (k, i, 0))],
            out_specs=pl.BlockSpec((None, tr, cs), lambda k, i, c_ref: (k, i, 0))),
        out_shape=jax.ShapeDtypeStruct(theirs.shape, BF16),
        compiler_params=_params(("parallel", "parallel")),
    )(_scalar(c), g, theirs)


def _chip_exchange(sums):
    n = len(sums)

    def body(*refs):
        ins, outs, send_sems, recv_sems = refs[:n], refs[n:2 * n], refs[2 * n], refs[2 * n + 1]
        x, y, c = _me()
        chips = _other_chips(x, y)
        cps = []
        for w, (s_ref, o_ref) in enumerate(zip(ins, outs)):
            for j, (cx, cy) in enumerate(chips):
                cps.append(pltpu.make_async_remote_copy(src_ref=s_ref.at[2 * cx + cy], dst_ref=o_ref.at[j],
                                                        send_sem=send_sems.at[3 * w + j], recv_sem=recv_sems.at[3 * w + j],
                                                        device_id=(cx, cy, c), device_id_type=MESH_ID))
                cps[-1].start()
        for cp in cps:
            cp.wait()

    return pl.pallas_call(
        body, name="grad_chip_exchange", in_specs=[_ANY] * n, out_specs=[_ANY] * n,
        out_shape=[jax.ShapeDtypeStruct((3,) + s.shape[1:], s.dtype) for s in sums],
        scratch_shapes=[pltpu.SemaphoreType.DMA((3 * n,)), pltpu.SemaphoreType.DMA((3 * n,))],
    )(*sums)


def _chip_sum(name, sums, theirs, chip):
    _, h, cs = sums.shape
    tr = _row_block(h, cs, 16)

    def body(chip_ref, s_ref, t_ref, o_ref):
        acc = s_ref[...].astype(F32)
        for k in range(3):
            acc = acc + t_ref[k].astype(F32)
        o_ref[...] = acc

    return pl.pallas_call(
        body, name="chip_sum_" + name,
        grid_spec=pltpu.PrefetchScalarGridSpec(
            num_scalar_prefetch=1, grid=(h // tr,),
            in_specs=[pl.BlockSpec((None, tr, cs), lambda i, chip_ref: (chip_ref[0], i, 0)),
                      pl.BlockSpec((3, tr, cs), lambda i, chip_ref: (0, i, 0))],
            out_specs=pl.BlockSpec((tr, cs), lambda i, chip_ref: (i, 0))),
        out_shape=jax.ShapeDtypeStruct((h, cs), F32),
        compiler_params=_params(("parallel",)),
    )(_scalar(chip), sums, theirs)


def _sibling_exchange(halves):
    n = len(halves)

    def body(*refs):
        ins, outs, send_sems, recv_sems = refs[:n], refs[n:2 * n], refs[2 * n], refs[2 * n + 1]
        x, y, c = _me()
        cps = []
        for w, (h_ref, o_ref) in enumerate(zip(ins, outs)):
            cps.append(pltpu.make_async_remote_copy(src_ref=h_ref, dst_ref=o_ref, send_sem=send_sems.at[w], recv_sem=recv_sems.at[w],
                                                    device_id=(x, y, 1 - c), device_id_type=MESH_ID))
            cps[-1].start()
        for cp in cps:
            cp.wait()

    return pl.pallas_call(
        body, name="grad_sibling_exchange", in_specs=[_ANY] * n, out_specs=[_ANY] * n,
        out_shape=[jax.ShapeDtypeStruct(h.shape, h.dtype) for h in halves],
        scratch_shapes=[pltpu.SemaphoreType.DMA((n,)), pltpu.SemaphoreType.DMA((n,))],
    )(*halves)


def _allreduce_small(name, vec):
    def body(v_ref, o_ref, buf_ref, send_sems, recv_sems):
        x, y, c = _me()
        me = 4 * x + 2 * y + c
        cps = []
        for p in range(1, 8):
            px, py, pc = x ^ (p >> 2), y ^ ((p >> 1) & 1), c ^ (p & 1)
            cps.append(pltpu.make_async_remote_copy(src_ref=v_ref, dst_ref=buf_ref.at[me], send_sem=send_sems.at[p - 1],
                                                    recv_sem=recv_sems.at[p - 1], device_id=(px, py, pc), device_id_type=MESH_ID))
            cps[-1].start()
        buf_ref[me] = v_ref[...]
        for p in range(1, 8):
            theirs = buf_ref.at[me ^ p]
            pltpu.make_async_remote_copy(src_ref=theirs, dst_ref=theirs, send_sem=send_sems.at[p - 1], recv_sem=recv_sems.at[p - 1],
                                         device_id=(x, y, c), device_id_type=MESH_ID).wait_recv()
        for cp in cps:
            cp.wait_send()
        acc = buf_ref[0]
        for k in range(1, 8):
            acc = acc + buf_ref[k]
        o_ref[...] = acc

    vm = pl.BlockSpec(memory_space=pltpu.VMEM)
    return pl.pallas_call(
        body, name=name, in_specs=[vm], out_specs=vm, out_shape=jax.ShapeDtypeStruct(vec.shape, F32),
        scratch_shapes=[pltpu.VMEM((8,) + vec.shape, F32), pltpu.SemaphoreType.DMA((7,)), pltpu.SemaphoreType.DMA((7,))],
    )(vec)


def _adam_math(w, g, m, v):
    m = ADAM_B1 * m + (1.0 - ADAM_B1) * g
    v = ADAM_B2 * v + (1.0 - ADAM_B2) * (g * g)
    m_hat = m / (1.0 - ADAM_B1 ** ADAM_STEP)
    v_hat = v / (1.0 - ADAM_B2 ** ADAM_STEP)
    return -ADAM_LR * (m_hat / (jnp.sqrt(v_hat) + ADAM_EPS) + ADAM_WD * w), m, v


def _adamw(name, w, g, m, v):
    R, C = w.shape
    tr = _row_block(R, C, 8)

    def body(w_ref, g_ref, m_ref, v_ref, d_ref, nm_ref, nv_ref):
        d_ref[...], nm_ref[...], nv_ref[...] = _adam_math(w_ref[...], g_ref[...], m_ref[...], v_ref[...])

    blk = pl.BlockSpec((tr, C), lambda i: (i, 0))
    return pl.pallas_call(
        body, name=name, grid=(R // tr,), in_specs=[blk] * 4, out_specs=[blk] * 3,
        out_shape=[jax.ShapeDtypeStruct((R, C), F32)] * 3, compiler_params=_params(("parallel",)),
    )(w, g, m, v)


def _adamw_halves(name, w, mine, theirs, m, v, c):
    rs, cs = w.shape
    h = rs // 2
    tr = _row_block(h, cs, 8)
    nb = h // tr

    def body(c_ref, w_ref, a_ref, b_ref, m_ref, v_ref, g_ref, d_ref, nm_ref, nv_ref):
        g = jnp.where(pl.program_id(0) == c_ref[0], a_ref[...], b_ref[...])
        g_ref[...] = g
        d_ref[...], nm_ref[...], nv_ref[...] = _adam_math(w_ref[...], g, m_ref[...], v_ref[...])

    full = pl.BlockSpec((tr, cs), lambda s, i, c_ref: (s * nb + i, 0))
    part = pl.BlockSpec((tr, cs), lambda s, i, c_ref: (i, 0))
    return pl.pallas_call(
        body, name=name,
        grid_spec=pltpu.PrefetchScalarGridSpec(num_scalar_prefetch=1, grid=(2, nb), in_specs=[full, part, part, full, full],
                                               out_specs=[full] * 4),
        out_shape=[jax.ShapeDtypeStruct((rs, cs), F32)] * 4, compiler_params=_params(("parallel", "parallel")),
    )(_scalar(c), w, mine, theirs, m, v)


def _pack_small(arrs):
    flat = jnp.concatenate([a.reshape(-1) for a in arrs])
    n = -(-flat.shape[0] // (8 * LANE)) * 8 * LANE
    return jnp.pad(flat, (0, n - flat.shape[0])).reshape(8, n // 8)


def _unpack_small(vec, shapes):
    flat, out, off = vec.reshape(-1), [], 0
    for s in shapes:
        out.append(flat[off:off + s[0] * s[1]].reshape(s))
        off += s[0] * s[1]
    return out


def _step(cfg, a):
    chip = 2 * lax.axis_index("x") + lax.axis_index("y")
    core = lax.axis_index("c")
    big = [n for n, _, _, _ in cfg.BIG]

    gathered = _allgather_weights([_stage_shard(n, a[n], chip) for n in big])
    W = {n: _gathered_to_kernel(cfg, n, wg) for n, wg in zip(big, gathered)}

    sp = {n: a[n] for n in SMALL}
    sharded = _pack_small([a[n] for n in SMALL_SHARDED])
    slot = jnp.where(lax.broadcasted_iota(I32, (N_CHIPS,) + sharded.shape, 0) == chip, 0.5 * sharded[None], 0.0)
    allp = _allreduce_small("allgather_small", slot.reshape(N_CHIPS * 8, -1)).reshape((N_CHIPS,) + sharded.shape)
    per_chip = [_unpack_small(allp[ch], [a[n].shape for n in SMALL_SHARDED]) for ch in range(N_CHIPS)]
    for k, n in enumerate(SMALL_SHARDED):
        sp[n] = jnp.concatenate([per_chip[ch][k] for ch in range(N_CHIPS)], axis=1)

    loss, grad_x, gW, gs = _local_grads(cfg, a["x"], a["loss_target"], W, sp)

    grads = [_grad_to_chips(cfg, n, gW[n]) for n in big]
    sums = [_pair_sum(n, g, t, core) for n, g, t in zip(big, grads, _pair_exchange(grads))]
    mine = [_chip_sum(n, s, t, chip) for n, s, t in zip(big, sums, _chip_exchange(sums))]
    theirs = _sibling_exchange(mine)

    shapes = [gs[n].shape for n in SMALL] + [(1, LANE)]
    red = _unpack_small(_allreduce_small("allreduce_small", _pack_small([gs[n] for n in SMALL] + [loss])), shapes)
    g_small = dict(zip(SMALL, red[:-1]))
    for n in SMALL_SHARDED:
        cs = a[n].shape[1]
        g_small[n] = lax.dynamic_slice_in_dim(g_small[n], chip * cs, cs, axis=1)

    out = {"loss": red[-1][0, 0], "grad_x": grad_x}
    for n, gm, gt in zip(big, mine, theirs):
        out["grad_" + n], out["delta_" + n], out["new_m_" + n], out["new_v_" + n] = _adamw_halves(
            "adamw_" + n, a[n], gm, gt, a["m_" + n], a["v_" + n], core)
    sshapes = [a[n].shape for n in SMALL]
    d, nm, nv = _adamw("adamw_small", _pack_small([a[n] for n in SMALL]), _pack_small([g_small[n] for n in SMALL]),
                       _pack_small([a["m_" + n] for n in SMALL]), _pack_small([a["v_" + n] for n in SMALL]))
    for n, dd, mm, vv in zip(SMALL, _unpack_small(d, sshapes), _unpack_small(nm, sshapes), _unpack_small(nv, sshapes)):
        out["grad_" + n], out["delta_" + n], out["new_m_" + n], out["new_v_" + n] = g_small[n], dd, mm, vv
    return out


def kernel(x, mix_pre_g, w_in, q_norm_g, w_uq, kv_norm_g, w_ukv, ssm_conv_w, ssm_conv_b, dt_bias, a_log, d_skip, ssm_norm_g, w_out, mix_post_g, ffn_pre_g, w_gate, w_up, ffn_conv_w, ffn_conv_b, w_down, ffn_post_g, loss_target, m_mix_pre_g, m_w_in, m_q_norm_g, m_w_uq, m_kv_norm_g, m_w_ukv, m_ssm_conv_w, m_ssm_conv_b, m_dt_bias, m_a_log, m_d_skip, m_ssm_norm_g, m_w_out, m_mix_post_g, m_ffn_pre_g, m_w_gate, m_w_up, m_ffn_conv_w, m_ffn_conv_b, m_w_down, m_ffn_post_g, v_mix_pre_g, v_w_in, v_q_norm_g, v_w_uq, v_kv_norm_g, v_w_ukv, v_ssm_conv_w, v_ssm_conv_b, v_dt_bias, v_a_log, v_d_skip, v_ssm_norm_g, v_w_out, v_mix_post_g, v_ffn_pre_g, v_w_gate, v_w_up, v_ffn_conv_w, v_ffn_conv_b, v_w_down, v_ffn_post_g):
    args = dict(locals())
    out = _step(_FULL, {k: (v[0] if v.ndim == 3 else v) for k, v in args.items()})
    res = [out["loss"], out["grad_x"][None]]
    for pre in ("grad_", "delta_", "new_m_", "new_v_"):
        res += [out[pre + n][None] if args[n].ndim == 3 else out[pre + n] for n in WEIGHTS]
    return tuple(res)
```

```python
import functools
import math

import jax
import jax.numpy as jnp
from jax import lax
from jax.experimental import pallas as pl
from jax.experimental.pallas import tpu as pltpu

F32, BF16, I32 = jnp.float32, jnp.bfloat16, jnp.int32
NN = (((1,), (0,)), ((), ()))
NT = (((1,), (1,)), ((), ()))
TN = (((0,), (0,)), ((), ()))
HI = lax.Precision.HIGHEST
MESH_ID = pl.DeviceIdType.MESH

EPS = 1e-6
CHUNK = 64
NOPE, ROPE, VH = 128, 64, 128
ROPE_THETA = 10000.0
HP, NST = 64, 128
SSM_K, FFN_K = 4, 3
LANE = 128
N_CHIPS = 4
VMEM_LIMIT = 52 * 1024 * 1024

ADAM_LR, ADAM_B1, ADAM_B2, ADAM_EPS, ADAM_WD, ADAM_STEP = 0.001, 0.9, 0.999, 1e-08, 0.01, 10


class _Cfg:
    def __init__(self, S, D, QL, KVL, H, HS, G, DFF, T):
        self.S, self.D, self.QL, self.KVL, self.H, self.HS, self.G, self.DFF, self.T = S, D, QL, KVL, H, HS, G, DFF, T
        self.INNER = HS * HP
        self.CONVCH = self.INNER + 2 * G * NST
        self.QW = H * (NOPE + ROPE)
        self.KVW = H * (NOPE + VH)
        self.MLAW = H * VH
        self.MIXW = self.MLAW + self.INNER
        self.IN_COLS = QL + KVL + ROPE + self.INNER + self.CONVCH + HS
        self.o_kr = QL + KVL
        self.o_z = self.o_kr + LANE
        self.o_xbc = self.o_z + self.INNER
        self.o_dt = self.o_xbc + self.CONVCH
        self.EXT = self.o_dt + LANE
        self.NPAIR = HS // 2
        self.REP = HS // G
        self.BIG = (("w_in", D, self.IN_COLS, 1), ("w_uq", QL, self.QW, 1), ("w_ukv", KVL, self.KVW, 1),
                    ("w_out", self.MIXW, D, 0), ("w_gate", D, DFF, 1), ("w_up", D, DFF, 1), ("w_down", DFF, D, 0))
        self.NSHARD = sum(r * c for _, r, c, _ in self.BIG) // N_CHIPS
        unit = 2 * LANE * 16
        self.NPACK = -(-self.NSHARD // unit) * unit
        self.R = self.NPACK // (2 * LANE)


_FULL = _Cfg(S=2048, D=2048, QL=768, KVL=512, H=8, HS=16, G=2, DFF=5632, T=256)

SMALL = ("mix_pre_g", "q_norm_g", "kv_norm_g", "ssm_conv_w", "ssm_conv_b", "dt_bias", "a_log", "d_skip", "ssm_norm_g",
         "mix_post_g", "ffn_pre_g", "ffn_conv_w", "ffn_conv_b", "ffn_post_g")
SMALL_SHARDED = ("ssm_conv_w", "ffn_conv_w")
WEIGHTS = ("mix_pre_g", "w_in", "q_norm_g", "w_uq", "kv_norm_g", "w_ukv", "ssm_conv_w", "ssm_conv_b", "dt_bias", "a_log",
           "d_skip", "ssm_norm_g", "w_out", "mix_post_g", "ffn_pre_g", "w_gate", "w_up", "ffn_conv_w", "ffn_conv_b",
           "w_down", "ffn_post_g")


def _pick(n, target, mult):
    best = None
    for d in range(mult, min(n, target) + 1, mult):
        if n % d == 0:
            best = d
    return best if best is not None else n


def _params(sem=None):
    kw = dict(vmem_limit_bytes=VMEM_LIMIT)
    if sem is not None:
        kw["dimension_semantics"] = sem
    return pltpu.CompilerParams(**kw)


def _dot(a, b, dims=NN, precision=None):
    return lax.dot_general(a, b, dims, preferred_element_type=F32, precision=precision)


def _sigmoid(x):
    return 1.0 / (1.0 + jnp.exp(-x))


def _rs(x):
    return lax.rsqrt(jnp.mean(x * x, axis=-1, keepdims=True) + EPS)


def _rms_back(xh, r, dn):
    return r * (dn - xh * jnp.mean(dn * xh, axis=-1, keepdims=True))


def _colsum(v):
    return jnp.sum(v, axis=0, keepdims=True)


def _matmul(name, a, b, mode, out_dtype, a2=None, b2=None, chips=False):
    cs = None
    if mode == "nn":
        (M, K), N = a.shape, b.shape[-1]
        if chips:
            cs, N = N, N_CHIPS * N
    elif mode == "nt":
        (M, K), N = a.shape, b.shape[-2]
        if chips:
            cs = b.shape[-1]
    else:
        (K, M), N = a.shape, b.shape[1]
        if chips:
            cs = N // N_CHIPS
    tm = _pick(M, 1024, LANE)
    tn = _pick(cs if chips and mode != "nt" else N, 1408, LANE)
    tk = _pick(cs, 1408, LANE) if chips and mode == "nt" else _pick(K, 512, LANE)
    nk = K // tk
    dims = {"nn": NN, "nt": NT, "tn": TN}[mode]
    a_spec = pl.BlockSpec((tk, tm), lambda i, j, k: (k, i)) if mode == "tn" else pl.BlockSpec((tm, tk), lambda i, j, k: (i, k))
    b_spec = pl.BlockSpec((tn, tk), lambda i, j, k: (j, k)) if mode == "nt" else pl.BlockSpec((tk, tn), lambda i, j, k: (k, j))
    o_spec = pl.BlockSpec((tm, tn), lambda i, j, k: (i, j))
    o_shape = (M, N)
    if chips and mode == "nn":
        per = cs // tn
        b_spec = pl.BlockSpec((None, tk, tn), lambda i, j, k: (j // per, k, j % per))
    elif chips and mode == "nt":
        per = cs // tk
        b_spec = pl.BlockSpec((None, tn, tk), lambda i, j, k: (k // per, j, k % per))
    elif chips:
        per = cs // tn
        o_spec = pl.BlockSpec((None, tm, tn), lambda i, j, k: (j // per, i, j % per))
        o_shape = (N_CHIPS, M, cs)
    two = a2 is not None

    def body(*refs):
        o_ref, acc_ref = refs[-2], refs[-1]
        k = pl.program_id(2)

        @pl.when(k == 0)
        def _():
            acc_ref[...] = jnp.zeros_like(acc_ref)

        part = _dot(refs[0][...].astype(BF16), refs[1][...].astype(BF16), dims)
        if two:
            part += _dot(refs[2][...].astype(BF16), refs[3][...].astype(BF16), dims)
        acc_ref[...] += part

        @pl.when(k == nk - 1)
        def _():
            o_ref[...] = acc_ref[...].astype(o_ref.dtype)

    ins = (a, b, a2, b2) if two else (a, b)
    return pl.pallas_call(
        body, name=name, grid=(M // tm, N // tn, nk),
        in_specs=[a_spec, b_spec] * (2 if two else 1),
        out_specs=o_spec,
        out_shape=jax.ShapeDtypeStruct(o_shape, out_dtype),
        scratch_shapes=[pltpu.VMEM((tm, tn), F32)],
        compiler_params=_params(("parallel", "parallel", "arbitrary")),
    )(*ins)


def _rowwise(name, fn, rows, mats, outs, reds, ts):
    S = rows[0].shape[0]
    nr, nm, no = len(rows), len(mats), len(outs)

    def body(*refs):
        res = fn(*[r[...] for r in refs[:nr + nm]])
        res = res if isinstance(res, (tuple, list)) else (res,)
        for r, v in zip(refs[nr + nm:nr + nm + no], res[:no]):
            r[...] = v.astype(r.dtype)
        first = pl.program_id(0) == 0
        for r, v in zip(refs[nr + nm + no:], res[no:]):
            @pl.when(first)
            def _():
                r[...] = jnp.broadcast_to(v, r.shape)

            @pl.when(jnp.logical_not(first))
            def _():
                r[...] += jnp.broadcast_to(v, r.shape)

    in_specs = [pl.BlockSpec((ts, a.shape[1]), lambda i: (i, 0)) for a in rows]
    in_specs += [pl.BlockSpec(m.shape, lambda i, nd=m.ndim: (0,) * nd) for m in mats]
    out_specs = [pl.BlockSpec((ts, w), lambda i: (i, 0)) for w, _ in outs]
    out_specs += [pl.BlockSpec(s, lambda i: (0, 0)) for s in reds]
    out_shape = [jax.ShapeDtypeStruct((S, w), dt) for w, dt in outs] + [jax.ShapeDtypeStruct(s, F32) for s in reds]
    return pl.pallas_call(
        body, name=name, grid=(S // ts,), in_specs=in_specs, out_specs=out_specs, out_shape=out_shape,
        compiler_params=_params(("arbitrary",) if reds else ("parallel",)),
    )(*rows, *mats)


def _shift_down(v, s):
    if s == 0:
        return v
    rows = lax.broadcasted_iota(I32, v.shape, 0)
    return jnp.where(rows >= s, pltpu.roll(v, s, 0), 0.0)


def _shift_up(v, s):
    if s == 0:
        return v
    n = v.shape[0]
    rows = lax.broadcasted_iota(I32, v.shape, 0)
    return jnp.where(rows < n - s, pltpu.roll(v, n - s, 0), 0.0)


def _conv(x, w, b):
    K = w.shape[0]
    y = jnp.broadcast_to(b, x.shape)
    for k in range(K):
        y = y + w[k:k + 1, :] * _shift_down(x, K - 1 - k)
    return y


def _conv_back(x, w, dc):
    K = w.shape[0]
    dx = jnp.zeros_like(x)
    dw = []
    for k in range(K):
        dx = dx + w[k:k + 1, :] * _shift_up(dc, K - 1 - k)
        dw.append(_colsum(dc * _shift_down(x, K - 1 - k)))
    return dx, jnp.concatenate(dw, axis=0), _colsum(dc)


def _colwise(name, fn, cols, vecs, outs, pouts, tc):
    S, C = cols[0].shape
    nc_, nv, no = len(cols), len(vecs), len(outs)

    def body(*refs):
        res = fn(*[r[...] for r in refs[:nc_ + nv]])
        res = res if isinstance(res, (tuple, list)) else (res,)
        for r, v in zip(refs[nc_ + nv:], res):
            r[...] = v.astype(r.dtype)

    in_specs = [pl.BlockSpec((S, tc), lambda j: (0, j)) for _ in cols]
    in_specs += [pl.BlockSpec((v.shape[0], tc), lambda j: (0, j)) for v in vecs]
    out_specs = [pl.BlockSpec((S, tc), lambda j: (0, j)) for _ in outs] + [pl.BlockSpec((k, tc), lambda j: (0, j)) for k in pouts]
    out_shape = [jax.ShapeDtypeStruct((S, C), dt) for dt in outs] + [jax.ShapeDtypeStruct((k, C), F32) for k in pouts]
    return pl.pallas_call(
        body, name=name, grid=(C // tc,), in_specs=in_specs, out_specs=out_specs, out_shape=out_shape,
        compiler_params=_params(("parallel",)),
    )(*cols, *vecs)


_G0, _G1 = math.sqrt(2.0 / math.pi), 0.044715


def _gelu(g):
    th = jnp.tanh(_G0 * (g + _G1 * g * g * g))
    return 0.5 * g * (1.0 + th), th


def _ffn_act(gate_pre, up, w, b):
    act, _ = _gelu(_conv(gate_pre, w, b))
    return act * up


def _ffn_act_back(dact, gate_pre, up, w, b):
    g = _conv(gate_pre, w, b)
    ge, th = _gelu(g)
    dge = 0.5 * (1.0 + th) + 0.5 * g * (1.0 - th * th) * _G0 * (1.0 + 3.0 * _G1 * g * g)
    dup = dact * ge
    dgate_pre, dw, db = _conv_back(gate_pre, w, dact * up * dge)
    return dgate_pre, dup, dw, db


def _ssm_act(xbc, w, b):
    c = _conv(xbc, w, b)
    return c * _sigmoid(c)


def _ssm_act_back(dxc, xbc, w, b):
    c = _conv(xbc, w, b)
    sg = _sigmoid(c)
    return _conv_back(xbc, w, dxc * sg * (1.0 + c * (1.0 - sg)))


def _rope_tables(S):
    inv = 1.0 / (ROPE_THETA ** (jnp.arange(0, ROPE, 2, dtype=F32) / ROPE))
    ang = jnp.arange(S, dtype=F32)[:, None] * inv[None, :]
    cos, sin = jnp.cos(ang), jnp.sin(ang)
    return jnp.tile(cos, (1, 4)), jnp.tile(jnp.concatenate([-sin, sin], axis=1), (1, 2))


def _swap_halves(x):
    lane = lax.broadcasted_iota(I32, x.shape, 1)
    w = x.shape[1]
    return jnp.where((lane % ROPE) < ROPE // 2, pltpu.roll(x, w - ROPE // 2, 1), pltpu.roll(x, ROPE // 2, 1))


def _rot(x, cos2, sin2):
    return x * cos2 + _swap_halves(x) * sin2


def _rot_back(dy, cos2, sin2):
    return dy * cos2 + _swap_halves(dy * sin2)


def _mla_pack(cfg, q, kv, kr, cos2, sin2):
    S, H = cfg.S, cfg.H
    ts = _pick(S, 512, 8)

    def body(qn_ref, qr_ref, kn_ref, v_ref, kr_ref, c_ref, s_ref, Q_ref, K_ref, V_ref):
        h = pl.program_id(0)
        c2, s2 = c_ref[...], s_ref[...]
        Q_ref[0, :, 0:LANE] = qn_ref[...].astype(BF16)
        Q_ref[0, :, LANE:] = _rot(qr_ref[...], c2, s2).astype(BF16)
        K_ref[0, :, 0:LANE] = kn_ref[...].astype(BF16)
        krr = _rot(kr_ref[...], c2, s2)
        K_ref[0, :, LANE:] = jnp.where(h % 2 == 1, pltpu.roll(krr, ROPE, 1), krr).astype(BF16)
        V_ref[0] = v_ref[...].astype(BF16)

    blk = lambda f: pl.BlockSpec((ts, LANE), f)
    return pl.pallas_call(
        body, name="mla_pack", grid=(H, S // ts),
        in_specs=[blk(lambda h, i: (i, h)), blk(lambda h, i: (i, H + h // 2)), blk(lambda h, i: (i, h)),
                  blk(lambda h, i: (i, H + h)), blk(lambda h, i: (i, 0)), blk(lambda h, i: (i, 0)), blk(lambda h, i: (i, 0))],
        out_specs=[pl.BlockSpec((1, ts, 2 * LANE), lambda h, i: (h, i, 0)), pl.BlockSpec((1, ts, 2 * LANE), lambda h, i: (h, i, 0)),
                   pl.BlockSpec((1, ts, LANE), lambda h, i: (h, i, 0))],
        out_shape=[jax.ShapeDtypeStruct((H, S, 2 * LANE), BF16), jax.ShapeDtypeStruct((H, S, 2 * LANE), BF16),
                   jax.ShapeDtypeStruct((H, S, LANE), BF16)],
        compiler_params=_params(("parallel", "parallel")),
    )(q, q, kv, kv, kr, cos2, sin2)


def _mla_unpack(cfg, dQ, dK, dV, cos2, sin2):
    S, H = cfg.S, cfg.H
    ts = _pick(S, 256, 8)

    def body(dQ_ref, dK_ref, dV_ref, c_ref, s_ref, dq_ref, dkv_ref, dkr_ref):
        c2, s2 = c_ref[...], s_ref[...]
        lo = lax.broadcasted_iota(I32, (ts, LANE), 1) < ROPE
        tk = jnp.zeros((ts, LANE), F32)
        for h in range(H):
            dq_ref[:, h * LANE:(h + 1) * LANE] = dQ_ref[h, :, 0:LANE].astype(BF16)
            dkv_ref[:, h * LANE:(h + 1) * LANE] = dK_ref[h, :, 0:LANE].astype(BF16)
            dkv_ref[:, (H + h) * LANE:(H + h + 1) * LANE] = dV_ref[h].astype(BF16)
            own = lo if h % 2 == 0 else jnp.logical_not(lo)
            tk = tk + jnp.where(own, dK_ref[h, :, LANE:], 0.0)
        for j in range(H // 2):
            dr = dQ_ref[2 * j, :, LANE:] + dQ_ref[2 * j + 1, :, LANE:]
            dq_ref[:, (H + j) * LANE:(H + j + 1) * LANE] = _rot_back(dr, c2, s2).astype(BF16)
        dkr_rot = jnp.where(lo, tk + pltpu.roll(tk, ROPE, 1), 0.0)
        dkr_ref[...] = _rot_back(dkr_rot, c2, s2).astype(BF16)

    tab = pl.BlockSpec((ts, LANE), lambda i: (i, 0))
    return pl.pallas_call(
        body, name="mla_unpack", grid=(S // ts,),
        in_specs=[pl.BlockSpec((H, ts, 2 * LANE), lambda i: (0, i, 0)), pl.BlockSpec((H, ts, 2 * LANE), lambda i: (0, i, 0)),
                  pl.BlockSpec((H, ts, LANE), lambda i: (0, i, 0)), tab, tab],
        out_specs=[pl.BlockSpec((ts, cfg.QW), lambda i: (i, 0)), pl.BlockSpec((ts, cfg.KVW), lambda i: (i, 0)), tab],
        out_shape=[jax.ShapeDtypeStruct((S, cfg.QW), BF16), jax.ShapeDtypeStruct((S, cfg.KVW), BF16),
                   jax.ShapeDtypeStruct((S, LANE), BF16)],
        compiler_params=_params(("parallel",)),
    )(dQ, dK, dV, cos2, sin2)


_ATT_T = 256
_ATT_SCALE = (NOPE + ROPE) ** -0.5


def _att_scores(q, k, qi, kb):
    s = _dot(q, k, NT) * _ATT_SCALE
    rows = qi * _ATT_T + lax.broadcasted_iota(I32, s.shape, 0)
    cols = kb * _ATT_T + lax.broadcasted_iota(I32, s.shape, 1)
    return jnp.where((cols // CHUNK) <= (rows // CHUNK), s, -1e30)


def _attn_fwd(cfg, Q, K, V):
    S, H, T = cfg.S, cfg.H, _ATT_T

    def body(q_ref, k_ref, v_ref, o_ref, lse_ref):
        qi = pl.program_id(1)
        q = q_ref[0]

        def step(kb, carry):
            m, l, acc = carry
            ks = pl.multiple_of(kb * T, T)
            s = _att_scores(q, k_ref[0, pl.ds(ks, T), :], qi, kb)
            m_new = jnp.maximum(m, jnp.max(s, axis=1, keepdims=True))
            p = jnp.exp(s - m_new)
            alpha = jnp.exp(m - m_new)
            l = alpha * l + jnp.sum(p, axis=1, keepdims=True)
            acc = alpha * acc + _dot(p.astype(BF16), v_ref[0, pl.ds(ks, T), :])
            return m_new, l, acc

        init = (jnp.full((T, 1), -1e30, F32), jnp.zeros((T, 1), F32), jnp.zeros((T, VH), F32))
        m, l, acc = lax.fori_loop(0, qi + 1, step, init)
        o_ref[...] = acc / l
        lse_ref[...] = jnp.broadcast_to(m + jnp.log(l), (T, LANE))

    return pl.pallas_call(
        body, name="attn_fwd", grid=(H, S // T),
        in_specs=[pl.BlockSpec((1, T, 2 * LANE), lambda h, i: (h, i, 0)), pl.BlockSpec((1, S, 2 * LANE), lambda h, i: (h, 0, 0)),
                  pl.BlockSpec((1, S, LANE), lambda h, i: (h, 0, 0))],
        out_specs=[pl.BlockSpec((T, LANE), lambda h, i: (i, h)), pl.BlockSpec((T, LANE), lambda h, i: (i, h))],
        out_shape=[jax.ShapeDtypeStruct((S, H * LANE), F32), jax.ShapeDtypeStruct((S, H * LANE), F32)],
        compiler_params=_params(("parallel", "parallel")),
    )(Q, K, V)


def _attn_dq(cfg, Q, K, V, do, o, lse):
    S, H, T = cfg.S, cfg.H, _ATT_T

    def body(q_ref, k_ref, v_ref, do_ref, o_ref, lse_ref, dq_ref, dl_ref):
        qi = pl.program_id(1)
        q = q_ref[0]
        do = do_ref[...]
        delta = jnp.sum(do * o_ref[...], axis=1, keepdims=True)
        lse = lse_ref[:, 0:1]
        dob = do.astype(BF16)

        def step(kb, dq):
            ks = pl.multiple_of(kb * T, T)
            k = k_ref[0, pl.ds(ks, T), :]
            p = jnp.exp(_att_scores(q, k, qi, kb) - lse)
            dp = _dot(dob, v_ref[0, pl.ds(ks, T), :], NT)
            ds = p * (dp - delta) * _ATT_SCALE
            return dq + _dot(ds.astype(BF16), k)

        dq_ref[0] = lax.fori_loop(0, qi + 1, step, jnp.zeros((T, 2 * LANE), F32))
        dl_ref[...] = jnp.broadcast_to(delta, (T, LANE))

    col = pl.BlockSpec((T, LANE), lambda h, i: (i, h))
    return pl.pallas_call(
        body, name="attn_dq", grid=(H, S // T),
        in_specs=[pl.BlockSpec((1, T, 2 * LANE), lambda h, i: (h, i, 0)), pl.BlockSpec((1, S, 2 * LANE), lambda h, i: (h, 0, 0)),
                  pl.BlockSpec((1, S, LANE), lambda h, i: (h, 0, 0)), col, col, col],
        out_specs=[pl.BlockSpec((1, T, 2 * LANE), lambda h, i: (h, i, 0)), col],
        out_shape=[jax.ShapeDtypeStruct((H, S, 2 * LANE), F32), jax.ShapeDtypeStruct((S, H * LANE), F32)],
        compiler_params=_params(("parallel", "parallel")),
    )(Q, K, V, do, o, lse)


def _attn_dkv(cfg, Q, K, V, do, lse, delta):
    S, H, T = cfg.S, cfg.H, _ATT_T
    nq = S // T

    def body(q_ref, k_ref, v_ref, do_ref, lse_ref, dl_ref, dk_ref, dv_ref):
        kb = pl.program_id(1)
        k, v = k_ref[0], v_ref[0]

        def step(qi, carry):
            dk, dv = carry
            qs = pl.multiple_of(qi * T, T)
            q = q_ref[0, pl.ds(qs, T), :]
            dob = do_ref[pl.ds(qs, T), :].astype(BF16)
            p = jnp.exp(_att_scores(q, k, qi, kb) - lse_ref[pl.ds(qs, T), 0:1])
            dv = dv + _dot(p.astype(BF16), dob, TN)
            dp = _dot(dob, v, NT)
            ds = p * (dp - dl_ref[pl.ds(qs, T), 0:1]) * _ATT_SCALE
            dk = dk + _dot(ds.astype(BF16), q, TN)
            return dk, dv

        dk, dv = lax.fori_loop(kb, nq, step, (jnp.zeros((T, 2 * LANE), F32), jnp.zeros((T, VH), F32)))
        dk_ref[0] = dk
        dv_ref[0] = dv

    col = pl.BlockSpec((S, LANE), lambda h, j: (0, h))
    return pl.pallas_call(
        body, name="attn_dkv", grid=(H, S // T),
        in_specs=[pl.BlockSpec((1, S, 2 * LANE), lambda h, j: (h, 0, 0)), pl.BlockSpec((1, T, 2 * LANE), lambda h, j: (h, j, 0)),
                  pl.BlockSpec((1, T, LANE), lambda h, j: (h, j, 0)), col, col, col],
        out_specs=[pl.BlockSpec((1, T, 2 * LANE), lambda h, j: (h, j, 0)), pl.BlockSpec((1, T, LANE), lambda h, j: (h, j, 0))],
        out_shape=[jax.ShapeDtypeStruct((H, S, 2 * LANE), F32), jax.ShapeDtypeStruct((H, S, LANE), F32)],
        compiler_params=_params(("parallel", "parallel")),
    )(Q, K, V, do, lse, delta)


def _expand_matrix(cfg):
    r = lax.broadcasted_iota(I32, (LANE, cfg.INNER), 0)
    c = lax.broadcasted_iota(I32, (LANE, cfg.INNER), 1)
    return (r == c // HP).astype(F32)


def _softplus(x):
    return jnp.maximum(x, 0.0) + jnp.log(1.0 + jnp.exp(-jnp.abs(x)))


def _ssd_prep(cfg, dt_raw, dt_bias_pad, a_log_pad, expand):
    HS = cfg.HS

    def fn(raw, bias, alog, E):
        heads = lax.broadcasted_iota(I32, raw.shape, 1) < HS
        dt = jnp.where(heads, _softplus(raw + bias), 0.0)
        a = dt * jnp.where(heads[0:1], -jnp.exp(alog), 0.0)
        return dt, a, _dot(dt, E, precision=HI), _dot(a, E, precision=HI)

    return _rowwise("ssd_prep", fn, [dt_raw], [dt_bias_pad, a_log_pad, expand],
                    [(LANE, F32), (LANE, F32), (cfg.INNER, F32), (cfg.INNER, F32)], [], _pick(cfg.S, 512, 8))


def _tril(T):
    return lax.broadcasted_iota(I32, (T, T), 0) >= lax.broadcasted_iota(I32, (T, T), 1)


def _ssd_fwd(cfg, xc, dt_exp, a_exp, a_small, dskip_exp):
    S, T, INNER, G, NPAIR = cfg.S, cfg.T, cfg.INNER, cfg.G, cfg.NPAIR
    NC = S // T

    def body(xc_ref, dte_ref, ae_ref, as_ref, dsk_ref, y_ref, hin_ref, ht_ref):
        @pl.when(pl.program_id(0) == 0)
        def _():
            ht_ref[...] = jnp.zeros_like(ht_ref)

        tril = _tril(T)
        tri = tril.astype(F32)
        acs_s = _dot(tri, as_ref[...], precision=HI)
        acs_e = _dot(tri, ae_ref[...], precision=HI)
        acs_t = acs_s.T
        lo = lax.broadcasted_iota(I32, (T, LANE), 1) < HP
        for g in range(G):
            Bb = xc_ref[:, INNER + g * NST:INNER + (g + 1) * NST].astype(BF16)
            Cb = xc_ref[:, INNER + (G + g) * NST:INNER + (G + g + 1) * NST].astype(BF16)
            Gm = _dot(Cb, Bb, NT)
            for j in range(g * NPAIR // G, (g + 1) * NPAIR // G):
                sl = slice(j * LANE, (j + 1) * LANE)
                Xp = xc_ref[:, sl]
                Xdt = Xp * dte_ref[:, sl]
                Xb = Xdt.astype(BF16)
                acs_p = acs_e[:, sl]
                last = acs_p[T - 1:T, :]
                Hin = ht_ref[j]
                hin_ref[0, j] = Hin
                yd = []
                for e in (0, 1):
                    h = 2 * j + e
                    Lm = jnp.exp(jnp.where(tril, acs_s[:, h:h + 1] - acs_t[h:h + 1, :], -1e30))
                    yd.append(_dot((Gm * Lm).astype(BF16), Xb))
                y_off = _dot(Cb, Hin.astype(BF16)) * jnp.exp(acs_p)
                y_ref[:, sl] = jnp.where(lo, yd[0], yd[1]) + y_off + Xp * dsk_ref[:, sl]
                st = _dot(Bb, (Xdt * jnp.exp(last - acs_p)).astype(BF16), TN)
                ht_ref[j] = jnp.exp(last) * Hin + st

    rows = lambda w: pl.BlockSpec((T, w), lambda c: (c, 0))
    return pl.pallas_call(
        body, name="ssd_fwd", grid=(NC,),
        in_specs=[rows(cfg.CONVCH), rows(INNER), rows(INNER), rows(LANE), pl.BlockSpec((1, INNER), lambda c: (0, 0))],
        out_specs=[rows(INNER), pl.BlockSpec((1, NPAIR, NST, LANE), lambda c: (c, 0, 0, 0))],
        out_shape=[jax.ShapeDtypeStruct((S, INNER), F32), jax.ShapeDtypeStruct((NC, NPAIR, NST, LANE), F32)],
        scratch_shapes=[pltpu.VMEM((NPAIR, NST, LANE), F32)],
        compiler_params=_params(("arbitrary",)),
    )(xc, dt_exp, a_exp, a_small, dskip_exp)


def _ssd_bwd(cfg, dy, xc, dt_exp, a_exp, a_small, dskip_exp, hin, dt_raw, dt_bias_pad, a_log_pad, expand):
    S, T, INNER, G, NPAIR, HS = cfg.S, cfg.T, cfg.INNER, cfg.G, cfg.NPAIR, cfg.HS
    NC = S // T

    def body(dy_ref, xc_ref, dte_ref, ae_ref, as_ref, dsk_ref, hin_ref, raw_ref, bias_ref, alog_ref, e_ref,
             dxc_ref, draw_ref, dbias_ref, dalog_ref, dskip_ref, dht_ref, cols_ref, rows_ref, dacs_ref, ddt_ref):
        first = pl.program_id(0) == 0

        @pl.when(first)
        def _():
            dht_ref[...] = jnp.zeros_like(dht_ref)

        tril = _tril(T)
        tri = tril.astype(F32)
        a_s = as_ref[...]
        acs_s = _dot(tri, a_s, precision=HI)
        acs_e = _dot(tri, ae_ref[...], precision=HI)
        acs_t = acs_s.T
        lo = lax.broadcasted_iota(I32, (T, LANE), 1) < HP
        last_row = lax.broadcasted_iota(I32, (T, LANE), 0) == T - 1
        cols_ref[...] = jnp.zeros_like(cols_ref)
        rows_ref[...] = jnp.zeros_like(rows_ref)
        dsk_parts = []
        for g in range(G):
            bsl = slice(INNER + g * NST, INNER + (g + 1) * NST)
            csl = slice(INNER + (G + g) * NST, INNER + (G + g + 1) * NST)
            Bb = xc_ref[:, bsl].astype(BF16)
            Cb = xc_ref[:, csl].astype(BF16)
            Gm = _dot(Cb, Bb, NT)
            dG = jnp.zeros((T, T), F32)
            dB = jnp.zeros((T, NST), F32)
            dC = jnp.zeros((T, NST), F32)
            for j in range(g * NPAIR // G, (g + 1) * NPAIR // G):
                sl = slice(j * LANE, (j + 1) * LANE)
                Xp = xc_ref[:, sl]
                dtp = dte_ref[:, sl]
                Xdt = Xp * dtp
                Xb = Xdt.astype(BF16)
                acs_p = acs_e[:, sl]
                last = acs_p[T - 1:T, :]
                e_p, dec, cd = jnp.exp(acs_p), jnp.exp(last - acs_p), jnp.exp(last)
                Hin = hin_ref[0, j]
                Hb = Hin.astype(BF16)
                dHn = dht_ref[j]
                dHb = dHn.astype(BF16)
                dYp = dy_ref[:, sl]
                z = _dot(Cb, Hb)
                dz = (dYp * e_p).astype(BF16)
                dacs_p = dYp * z * e_p
                dC = dC + _dot(dz, Hb, NT)
                dHin = _dot(Cb, dz, TN) + cd * dHn
                dlast = _colsum(dHn * Hin) * cd
                qv = _dot(Bb, dHb)
                dXdt = qv * dec
                ddec = qv * Xdt * dec
                dacs_p = dacs_p - ddec
                dlast = dlast + _colsum(ddec)
                dB = dB + _dot((Xdt * dec).astype(BF16), dHb, NT)
                for e in (0, 1):
                    h = 2 * j + e
                    Lm = jnp.exp(jnp.where(tril, acs_s[:, h:h + 1] - acs_t[h:h + 1, :], -1e30))
                    Mh = Gm * Lm
                    dYe = jnp.where(lo if e == 0 else jnp.logical_not(lo), dYp, 0.0).astype(BF16)
                    dM = _dot(dYe, Xb, NT)
                    dXdt = dXdt + _dot(Mh.astype(BF16), dYe, TN)
                    W = dM * Mh
                    cols_ref[:, h:h + 1] = jnp.sum(W, axis=1, keepdims=True)
                    rows_ref[h:h + 1, :] = _colsum(W)
                    dG = dG + dM * Lm
                dacs_ref[:, sl] = dacs_p + jnp.where(last_row, dlast, 0.0)
                ddt_ref[:, sl] = dXdt * Xp
                dxc_ref[:, sl] = dXdt * dtp + dYp * dsk_ref[:, sl]
                dsk_parts.append(_colsum(dYp * Xp))
                dht_ref[j] = dHin
            dGb = dG.astype(BF16)
            dxc_ref[:, bsl] = dB + _dot(dGb, Cb, TN)
            dxc_ref[:, csl] = dC + _dot(dGb, Bb)
        E = e_ref[...]
        dacs_s = cols_ref[...] - rows_ref[...].T + _dot(dacs_ref[...], E, NT, precision=HI)
        da = _dot(tri, dacs_s, TN, precision=HI)
        heads = lax.broadcasted_iota(I32, (1, LANE), 1) < HS
        A = jnp.where(heads, -jnp.exp(alog_ref[...]), 0.0)
        ddt = _dot(ddt_ref[...], E, NT, precision=HI) + da * A
        draw = jnp.where(heads, ddt * _sigmoid(raw_ref[...] + bias_ref[...]), 0.0)
        draw_ref[...] = draw
        dsk = _dot(jnp.broadcast_to(jnp.concatenate(dsk_parts, axis=1), (8, INNER)), E, NT, precision=HI)[0:1]
        for ref, val in ((dbias_ref, _colsum(draw)), (dalog_ref, _colsum(da * a_s)), (dskip_ref, dsk)):
            @pl.when(first)
            def _():
                ref[...] = val

            @pl.when(jnp.logical_not(first))
            def _():
                ref[...] += val

    rows = lambda w: pl.BlockSpec((T, w), lambda c: (NC - 1 - c, 0))
    vec = lambda w: pl.BlockSpec((1, w), lambda c: (0, 0))
    return pl.pallas_call(
        body, name="ssd_bwd", grid=(NC,),
        in_specs=[rows(INNER), rows(cfg.CONVCH), rows(INNER), rows(INNER), rows(LANE), vec(INNER),
                  pl.BlockSpec((1, NPAIR, NST, LANE), lambda c: (NC - 1 - c, 0, 0, 0)), rows(LANE), vec(LANE), vec(LANE),
                  pl.BlockSpec((LANE, INNER), lambda c: (0, 0))],
        out_specs=[rows(cfg.CONVCH), rows(LANE), vec(LANE), vec(LANE), vec(LANE)],
        out_shape=[jax.ShapeDtypeStruct((S, cfg.CONVCH), F32), jax.ShapeDtypeStruct((S, LANE), F32)]
        + [jax.ShapeDtypeStruct((1, LANE), F32)] * 3,
        scratch_shapes=[pltpu.VMEM((NPAIR, NST, LANE), F32), pltpu.VMEM((T, LANE), F32), pltpu.VMEM((LANE, T), F32),
                        pltpu.VMEM((T, INNER), F32), pltpu.VMEM((T, INNER), F32)],
        compiler_params=_params(("arbitrary",)),
    )(dy, xc, dt_exp, a_exp, a_small, dskip_exp, hin, dt_raw, dt_bias_pad, a_log_pad, expand)


def _ssd_post(cfg, y, z, norm_g):
    W = cfg.INNER // cfg.G

    def fn(y, z, g):
        yz = y * z * _sigmoid(z)
        return jnp.concatenate([yz[:, i * W:(i + 1) * W] * _rs(yz[:, i * W:(i + 1) * W]) for i in range(cfg.G)], axis=1) * g

    return _rowwise("ssd_post", fn, [y, z], [norm_g], [(cfg.INNER, BF16)], [], _pick(cfg.S, 256, 8))[0]


def _ssd_post_bwd(cfg, db, y, z, norm_g):
    W = cfg.INNER // cfg.G

    def fn(db, y, z, g):
        sg = _sigmoid(z)
        yz = y * z * sg
        dn = db * g
        dyz, nh = [], []
        for i in range(cfg.G):
            seg = yz[:, i * W:(i + 1) * W]
            r = _rs(seg)
            nh.append(seg * r)
            dyz.append(_rms_back(nh[-1], r, dn[:, i * W:(i + 1) * W]))
        dyz = jnp.concatenate(dyz, axis=1)
        return dyz * z * sg, dyz * y * sg * (1.0 + z * (1.0 - sg)), _colsum(db * jnp.concatenate(nh, axis=1))

    return _rowwise("ssd_post_bwd", fn, [db, y, z], [norm_g], [(cfg.INNER, F32), (cfg.INNER, F32)], [(1, cfg.INNER)],
                    _pick(cfg.S, 256, 8))


def _local_grads(cfg, x, tgt, W, sp, ffn_grads_ready=None):
    S, D, H, INNER = cfg.S, cfg.D, cfg.H, cfg.INNER
    ts = _pick(S, 256, 8)
    tc = 256

    xn = _rowwise("rms_pre", lambda x, g: x * _rs(x) * g, [x], [sp["mix_pre_g"]], [(D, BF16)], [], ts)[0]
    u = _matmul("mm_in", xn, W["w_in"], "nn", F32)
    c_q, c_kv = u[:, :cfg.QL], u[:, cfg.QL:cfg.o_kr]
    kr = u[:, cfg.o_kr:cfg.o_z]
    z = u[:, cfg.o_z:cfg.o_xbc]
    xbc = u[:, cfg.o_xbc:cfg.o_dt]
    dt_raw = u[:, cfg.o_dt:]

    cqn = _rowwise("rms_q", lambda x, g: x * _rs(x) * g, [c_q], [sp["q_norm_g"]], [(cfg.QL, BF16)], [], ts)[0]
    ckvn = _rowwise("rms_kv", lambda x, g: x * _rs(x) * g, [c_kv], [sp["kv_norm_g"]], [(cfg.KVL, BF16)], [], ts)[0]
    q = _matmul("mm_uq", cqn, W["w_uq"], "nn", F32)
    kv = _matmul("mm_ukv", ckvn, W["w_ukv"], "nn", F32)
    cos2, sin2 = _rope_tables(S)
    Qh, Kh, Vh = _mla_pack(cfg, q, kv, kr, cos2, sin2)
    a_out, lse = _attn_fwd(cfg, Qh, Kh, Vh)

    pad = lambda v: jnp.pad(v, ((0, 0), (0, LANE - v.shape[1])))
    expand = _expand_matrix(cfg)
    dt_bias_pad, a_log_pad = pad(sp["dt_bias"]), pad(sp["a_log"])
    dskip_exp = jnp.repeat(sp["d_skip"], HP, axis=1)
    xc = _colwise("ssm_act", _ssm_act, [xbc], [sp["ssm_conv_w"], sp["ssm_conv_b"]], [F32], [], tc)[0]
    dt_s, a_s, dt_exp, a_exp = _ssd_prep(cfg, dt_raw, dt_bias_pad, a_log_pad, expand)
    y_ssd, hin = _ssd_fwd(cfg, xc, dt_exp, a_exp, a_s, dskip_exp)
    b_out = _ssd_post(cfg, y_ssd, z, sp["ssm_norm_g"])

    ab_out = jnp.concatenate([a_out.astype(BF16), b_out], axis=1)
    mix = _matmul("mm_out", ab_out, W["w_out"], "nn", F32)

    def mid(x, mix, g_mp, g_fp):
        x1 = x + mix * _rs(mix) * g_mp
        return x1, x1 * _rs(x1) * g_fp

    x1, h2 = _rowwise("fwd_mid", mid, [x, mix], [sp["mix_post_g"], sp["ffn_pre_g"]], [(D, F32), (D, BF16)], [], ts)
    gate_pre = _matmul("mm_gate", h2, W["w_gate"], "nn", F32, chips=True)
    up = _matmul("mm_up", h2, W["w_up"], "nn", F32, chips=True)
    act = _colwise("ffn_act", _ffn_act, [gate_pre, up], [sp["ffn_conv_w"], sp["ffn_conv_b"]], [BF16], [], tc)[0]
    f = _matmul("mm_down", act, W["w_down"], "nn", F32)

    def final(x1, f, t, g):
        r = _rs(f)
        fh = f * r
        err = x1 + fh * g - t
        loss = 0.5 * jnp.sum(jnp.mean(err * err, axis=-1, keepdims=True), axis=0, keepdims=True)
        dy = err * (1.0 / D)
        return dy, _rms_back(fh, r, dy * g), _colsum(dy * fh), loss

    dy, df, g_ffn_post, loss = _rowwise("final", final, [x1, f, tgt], [sp["ffn_post_g"]], [(D, F32), (D, BF16)],
                                        [(1, D), (1, LANE)], ts)
    gW = {}
    dact = _matmul("mm_down_dx", df, W["w_down"], "nt", F32)
    gW["w_down"] = _matmul("mm_down_dw", act, df, "tn", BF16)
    dgate, dup, g_ffn_conv_w, g_ffn_conv_b = _colwise(
        "ffn_act_bwd", _ffn_act_back, [dact, gate_pre, up], [sp["ffn_conv_w"], sp["ffn_conv_b"]], [BF16, BF16], [FFN_K, 1], tc)
    dh2 = _matmul("mm_gu_dx", dgate, W["w_gate"], "nt", F32, dup, W["w_up"], chips=True)
    gW["w_gate"] = _matmul("mm_gate_dw", h2, dgate, "tn", BF16, chips=True)
    gW["w_up"] = _matmul("mm_up_dw", h2, dup, "tn", BF16, chips=True)
    if ffn_grads_ready is not None:
        token = ffn_grads_ready({n: gW[n] for n in ("w_down", "w_gate", "w_up")})
        sp = dict(sp, mix_post_g=sp["mix_post_g"] + token[0, 0])

    def mid_back(dy, dh2, x1, mix, g_mp, g_fp):
        r2 = _rs(x1)
        xh = x1 * r2
        dx1 = dy + _rms_back(xh, r2, dh2 * g_fp)
        r1 = _rs(mix)
        mh = mix * r1
        return dx1, _rms_back(mh, r1, dx1 * g_mp), _colsum(dh2 * xh), _colsum(dx1 * mh)

    dx1, dmix, g_ffn_pre, g_mix_post = _rowwise("bwd_mid", mid_back, [dy, dh2, x1, mix], [sp["mix_post_g"], sp["ffn_pre_g"]],
                                                [(D, F32), (D, BF16)], [(1, D), (1, D)], ts)
    dab_out = _matmul("mm_out_dx", dmix, W["w_out"], "nt", F32)
    db_out = dab_out[:, cfg.MLAW:]
    gW["w_out"] = _matmul("mm_out_dw", ab_out, dmix, "tn", BF16)

    dy_ssd, dz, g_ssm_norm = _ssd_post_bwd(cfg, db_out, y_ssd, z, sp["ssm_norm_g"])
    dxc, ddt_raw, g_dt_bias, g_a_log, g_d_skip = _ssd_bwd(cfg, dy_ssd, xc, dt_exp, a_exp, a_s, dskip_exp, hin, dt_raw,
                                                          dt_bias_pad, a_log_pad, expand)
    dxbc, g_ssm_conv_w, g_ssm_conv_b = _colwise("ssm_act_bwd", _ssm_act_back, [dxc, xbc], [sp["ssm_conv_w"], sp["ssm_conv_b"]],
                                                [BF16], [SSM_K, 1], tc)

    dQ, delta = _attn_dq(cfg, Qh, Kh, Vh, dab_out, a_out, lse)
    dK, dV = _attn_dkv(cfg, Qh, Kh, Vh, dab_out, lse, delta)
    dq, dkv, dkr = _mla_unpack(cfg, dQ, dK, dV, cos2, sin2)
    dcqn = _matmul("mm_uq_dx", dq, W["w_uq"], "nt", F32)
    dckvn = _matmul("mm_ukv_dx", dkv, W["w_ukv"], "nt", F32)
    gW["w_uq"] = _matmul("mm_uq_dw", cqn, dq, "tn", BF16)
    gW["w_ukv"] = _matmul("mm_ukv_dw", ckvn, dkv, "tn", BF16)

    def rms_back(x, dy, g):
        r = _rs(x)
        xh = x * r
        return _rms_back(xh, r, dy * g), _colsum(dy * xh)

    dc_q, g_q_norm = _rowwise("rms_q_bwd", rms_back, [c_q, dcqn], [sp["q_norm_g"]], [(cfg.QL, BF16)], [(1, cfg.QL)], ts)
    dc_kv, g_kv_norm = _rowwise("rms_kv_bwd", rms_back, [c_kv, dckvn], [sp["kv_norm_g"]], [(cfg.KVL, BF16)], [(1, cfg.KVL)], ts)

    du = jnp.concatenate([dc_q, dc_kv, dkr, dz.astype(BF16), dxbc, ddt_raw.astype(BF16)], axis=1)
    dxn = _matmul("mm_in_dx", du, W["w_in"], "nt", F32)
    gW["w_in"] = _matmul("mm_in_dw", xn, du, "tn", BF16)

    def first_back(dx1, dxn, x, g):
        r = _rs(x)
        xh = x * r
        return dx1 + _rms_back(xh, r, dxn * g), _colsum(dxn * xh)

    grad_x, g_mix_pre = _rowwise("bwd_first", first_back, [dx1, dxn, x], [sp["mix_pre_g"]], [(D, F32)], [(1, D)], ts)

    gs = dict(mix_pre_g=g_mix_pre, q_norm_g=g_q_norm, kv_norm_g=g_kv_norm, ssm_conv_w=g_ssm_conv_w, ssm_conv_b=g_ssm_conv_b,
              dt_bias=g_dt_bias[:, :cfg.HS], a_log=g_a_log[:, :cfg.HS], d_skip=g_d_skip[:, :cfg.HS], ssm_norm_g=g_ssm_norm,
              mix_post_g=g_mix_post, ffn_pre_g=g_ffn_pre, ffn_conv_w=g_ffn_conv_w, ffn_conv_b=g_ffn_conv_b,
              ffn_post_g=g_ffn_post)
    return loss, grad_x, gW, gs


def _to_kernel_layout(cfg, name, w):
    if name == "w_in":
        a = cfg.o_kr + ROPE
        return jnp.concatenate([w[:, :a], jnp.zeros((w.shape[0], LANE - ROPE), w.dtype), w[:, a:],
                                jnp.zeros((w.shape[0], LANE - cfg.HS), w.dtype)], axis=1)
    if name == "w_uq":
        w3 = w.reshape(cfg.QL, cfg.H, NOPE + ROPE)
        return jnp.concatenate([w3[:, :, :NOPE].reshape(cfg.QL, -1), w3[:, :, NOPE:].reshape(cfg.QL, -1)], axis=1)
    if name == "w_ukv":
        w3 = w.reshape(cfg.KVL, cfg.H, NOPE + VH)
        return jnp.concatenate([w3[:, :, :NOPE].reshape(cfg.KVL, -1), w3[:, :, NOPE:].reshape(cfg.KVL, -1)], axis=1)
    return w


def _from_kernel_layout(cfg, name, g):
    if name == "w_in":
        return jnp.concatenate([g[:, :cfg.o_kr + ROPE], g[:, cfg.o_z:cfg.o_dt + cfg.HS]], axis=1)
    if name == "w_uq":
        n = g[:, :cfg.H * NOPE].reshape(cfg.QL, cfg.H, NOPE)
        r = g[:, cfg.H * NOPE:].reshape(cfg.QL, cfg.H, ROPE)
        return jnp.concatenate([n, r], axis=2).reshape(cfg.QL, -1)
    if name == "w_ukv":
        n = g[:, :cfg.H * NOPE].reshape(cfg.KVL, cfg.H, NOPE)
        v = g[:, cfg.H * NOPE:].reshape(cfg.KVL, cfg.H, VH)
        return jnp.concatenate([n, v], axis=2).reshape(cfg.KVL, -1)
    return g


def _cols_to_chips(w):
    r, c = w.shape
    return w.reshape(r, N_CHIPS, c // N_CHIPS).transpose(1, 0, 2)


def _chips_to_cols(g):
    k, r, cs = g.shape
    return g.transpose(1, 0, 2).reshape(r, k * cs)


_CHIP_MAJOR = ("w_gate", "w_up")
_RELAYOUT = ("w_in", "w_uq", "w_ukv")


def _gathered_to_kernel(cfg, name, wg):
    if name in _CHIP_MAJOR:
        return wg
    if name in _RELAYOUT:
        return _to_kernel_layout(cfg, name, _chips_to_cols(wg))
    return wg.reshape(wg.shape[0] * wg.shape[1], wg.shape[2])


def _grad_to_chips(cfg, name, g):
    if name in _CHIP_MAJOR:
        return g
    if name in _RELAYOUT:
        return _cols_to_chips(_from_kernel_layout(cfg, name, g))
    return g.reshape(N_CHIPS, g.shape[0] // N_CHIPS, g.shape[1])


def _me():
    return lax.axis_index("x"), lax.axis_index("y"), lax.axis_index("c")


def _other_chips(x, y):
    return [(1 - x, y), (x, 1 - y), (1 - x, 1 - y)]


_ANY = pl.BlockSpec(memory_space=pl.ANY)


def _row_block(rows, cols, mult):
    return _pick(rows, max(mult, (1 << 19) // cols // mult * mult), mult)


def _scalar(v):
    return v.astype(I32).reshape(1)


def _stage_shard(name, w, chip):
    rs, cs = w.shape
    tr = _row_block(rs, cs, 16)

    def body(chip_ref, w_ref, o_ref):
        o_ref[...] = w_ref[...].astype(BF16)

    return pl.pallas_call(
        body, name="stage_" + name,
        grid_spec=pltpu.PrefetchScalarGridSpec(
            num_scalar_prefetch=1, grid=(rs // tr,),
            in_specs=[pl.BlockSpec((tr, cs), lambda i, chip_ref: (i, 0))],
            out_specs=pl.BlockSpec((None, tr, cs), lambda i, chip_ref: (chip_ref[0], i, 0))),
        out_shape=jax.ShapeDtypeStruct((N_CHIPS, rs, cs), BF16),
        compiler_params=_params(("parallel",)),
    )(_scalar(chip), w)


def _half(ref, k, half):
    h = ref.shape[1] // 2
    return ref.at[k, pl.ds(pl.multiple_of(half * h, 16), h), :]


def _allgather_weights(bufs):
    n = len(bufs)

    def body(*refs):
        outs, send_sems, recv_sems = refs[n:2 * n], refs[2 * n], refs[2 * n + 1]
        x, y, c = _me()
        chip = 2 * x + y
        sib = (x, y, 1 - c)
        chips = _other_chips(x, y)

        def copy(k, part, to):
            return pltpu.make_async_remote_copy(src_ref=part, dst_ref=part, send_sem=send_sems.at[k], recv_sem=recv_sems.at[k],
                                                device_id=to, device_id_type=MESH_ID)

        started = []
        for w, o_ref in enumerate(outs):
            for j, (cx, cy) in enumerate(chips):
                started.append(copy(6 * w + j, _half(o_ref, chip, c), (cx, cy, c)))
                started[-1].start()
        for w, o_ref in enumerate(outs):
            for j, (cx, cy) in enumerate(chips):
                theirs = _half(o_ref, 2 * cx + cy, c)
                copy(6 * w + j, theirs, sib).wait_recv()
                started.append(copy(6 * w + 3 + j, theirs, sib))
                started[-1].start()
        for w, o_ref in enumerate(outs):
            for j, (cx, cy) in enumerate(chips):
                copy(6 * w + 3 + j, _half(o_ref, 2 * cx + cy, 1 - c), sib).wait_recv()
        for cp in started:
            cp.wait_send()

    return pl.pallas_call(
        body, name="allgather_weights", in_specs=[_ANY] * n, out_specs=[_ANY] * n,
        out_shape=[jax.ShapeDtypeStruct(b.shape, b.dtype) for b in bufs],
        input_output_aliases={i: i for i in range(n)},
        scratch_shapes=[pltpu.SemaphoreType.DMA((6 * n,)), pltpu.SemaphoreType.DMA((6 * n,))],
    )(*bufs)


def _pair_exchange(name, grads):
    n = len(grads)

    def body(*refs):
        ins, outs, send_sems, recv_sems = refs[:n], refs[n:2 * n], refs[2 * n], refs[2 * n + 1]
        x, y, c = _me()
        cps = []
        for w, (g_ref, o_ref) in enumerate(zip(ins, outs)):
            h = o_ref.shape[1]
            src = g_ref.at[:, pl.ds(pl.multiple_of((1 - c) * h, 16), h), :]
            cps.append(pltpu.make_async_remote_copy(src_ref=src, dst_ref=o_ref, send_sem=send_sems.at[w], recv_sem=recv_sems.at[w],
                                                    device_id=(x, y, 1 - c), device_id_type=MESH_ID))
            cps[-1].start()
        for cp in cps:
            cp.wait()

    return pl.pallas_call(
        body, name="pair_exchange_" + name, in_specs=[_ANY] * n, out_specs=[_ANY] * n,
        out_shape=[jax.ShapeDtypeStruct((g.shape[0], g.shape[1] // 2, g.shape[2]), g.dtype) for g in grads],
        scratch_shapes=[pltpu.SemaphoreType.DMA((n,)), pltpu.SemaphoreType.DMA((n,))],
    )(*grads)


def _pair_sum(name, g, theirs, c):
    _, h, cs = theirs.shape
    tr = _row_block(h, cs, 16)
    nb = h // tr

    def body(c_ref, a_ref, b_ref, o_ref):
        o_ref[...] = (a_ref[...].astype(F32) + b_ref[...].astype(F32)).astype(o_ref.dtype)

    return pl.pallas_call(
        body, name="pair_sum_" + name,
        grid_spec=pltpu.PrefetchScalarGridSpec(
            num_scalar_prefetch=1, grid=(N_CHIPS, nb),
            in_specs=[pl.BlockSpec((None, tr, cs), lambda k, i, c_ref: (k, c_ref[0] * nb + i, 0)),
                      pl.BlockSpec((None, tr, cs), lambda k, i, c_ref: (k, i, 0))],
            out_specs=pl.BlockSpec((None, tr, cs), lambda k, i, c_ref: (k, i, 0))),
        out_shape=jax.ShapeDtypeStruct(theirs.shape, BF16),
        compiler_params=_params(("parallel", "parallel")),
    )(_scalar(c), g, theirs)


_HBM = pl.BlockSpec(memory_space=pltpu.HBM)
_SEM = pl.BlockSpec(memory_space=pltpu.SEMAPHORE)
_EFFECT = pltpu.SideEffectType.DATAFLOW_SIDE_EFFECTING


def _chip_copies(srcs, lands, send_sems, recv_sems):
    x, y, c = _me()
    return [pltpu.make_async_remote_copy(src_ref=s_ref.at[2 * cx + cy], dst_ref=l_ref.at[j], send_sem=send_sems.at[3 * w + j],
                                         recv_sem=recv_sems.at[3 * w + j], device_id=(cx, cy, c), device_id_type=MESH_ID)
            for w, (s_ref, l_ref) in enumerate(zip(srcs, lands)) for j, (cx, cy) in enumerate(_other_chips(x, y))]


def _chip_exchange_start(name, sums):
    n = len(sums)
    lands = [lax.empty((3,) + s.shape[1:], s.dtype) for s in sums]

    def body(*refs):
        for cp in _chip_copies(refs[:n], refs[n:2 * n], refs[2 * n], refs[2 * n + 1]):
            cp.start()
        refs[-1][...] = jnp.zeros_like(refs[-1])

    hbm = lambda arrs: [pltpu.HBM(a.shape, a.dtype) for a in arrs]
    res = pl.pallas_call(
        body, name="chip_exchange_start_" + name,
        out_shape=(pltpu.SemaphoreType.DMA((3 * n,)), pltpu.SemaphoreType.DMA((3 * n,)), *hbm(sums), *hbm(lands),
                   jax.ShapeDtypeStruct((8, LANE), F32)),
        in_specs=[_HBM] * (2 * n), out_specs=(_SEM, _SEM, *[_HBM] * (2 * n), pl.BlockSpec(memory_space=pltpu.VMEM)),
        input_output_aliases={i: 2 + i for i in range(2 * n)},
        compiler_params=pltpu.CompilerParams(has_side_effects=_EFFECT),
    )(*[pltpu.with_memory_space_constraint(a, pltpu.HBM) for a in (*sums, *lands)])
    return res[0], res[1], list(res[2:2 + n]), list(res[2 + n:2 + 2 * n]), res[-1]


def _chip_exchange_wait(name, send_sems, recv_sems, sums, lands, after):
    n = len(sums)

    def body(*refs):
        for cp in _chip_copies(refs[:n], refs[n:2 * n], refs[2 * n], refs[2 * n + 1]):
            cp.wait_send()
            cp.wait_recv()

    hbm = [pltpu.HBM(a.shape, a.dtype) for a in (*sums, *lands)]
    res = pl.pallas_call(
        body, name="chip_exchange_wait_" + name, out_shape=hbm,
        in_specs=[_HBM] * (2 * n) + [_SEM, _SEM, _ANY], out_specs=[_HBM] * (2 * n),
        input_output_aliases={i: i for i in range(2 * n)},
        compiler_params=pltpu.CompilerParams(has_side_effects=_EFFECT),
    )(*sums, *lands, send_sems, recv_sems, after)
    return list(res[:n]), list(res[n:])


def _chip_exchange(name, sums):
    n = len(sums)

    def body(*refs):
        cps = _chip_copies(refs[:n], refs[n:2 * n], refs[2 * n], refs[2 * n + 1])
        for cp in cps:
            cp.start()
        for cp in cps:
            cp.wait()

    return pl.pallas_call(
        body, name="chip_exchange_" + name, in_specs=[_ANY] * n, out_specs=[_ANY] * n,
        out_shape=[jax.ShapeDtypeStruct((3,) + s.shape[1:], s.dtype) for s in sums],
        scratch_shapes=[pltpu.SemaphoreType.DMA((3 * n,)), pltpu.SemaphoreType.DMA((3 * n,))],
    )(*sums)


def _chip_sum(name, sums, theirs, chip):
    _, h, cs = sums.shape
    tr = _row_block(h, cs, 16)

    def body(chip_ref, s_ref, t_ref, o_ref):
        acc = s_ref[...].astype(F32)
        for k in range(3):
            acc = acc + t_ref[k].astype(F32)
        o_ref[...] = acc

    return pl.pallas_call(
        body, name="chip_sum_" + name,
        grid_spec=pltpu.PrefetchScalarGridSpec(
            num_scalar_prefetch=1, grid=(h // tr,),
            in_specs=[pl.BlockSpec((None, tr, cs), lambda i, chip_ref: (chip_ref[0], i, 0)),
                      pl.BlockSpec((3, tr, cs), lambda i, chip_ref: (0, i, 0))],
            out_specs=pl.BlockSpec((tr, cs), lambda i, chip_ref: (i, 0))),
        out_shape=jax.ShapeDtypeStruct((h, cs), F32),
        compiler_params=_params(("parallel",)),
    )(_scalar(chip), sums, theirs)


def _sibling_exchange(halves):
    n = len(halves)

    def body(*refs):
        ins, outs, send_sems, recv_sems = refs[:n], refs[n:2 * n], refs[2 * n], refs[2 * n + 1]
        x, y, c = _me()
        cps = []
        for w, (h_ref, o_ref) in enumerate(zip(ins, outs)):
            cps.append(pltpu.make_async_remote_copy(src_ref=h_ref, dst_ref=o_ref, send_sem=send_sems.at[w], recv_sem=recv_sems.at[w],
                                                    device_id=(x, y, 1 - c), device_id_type=MESH_ID))
            cps[-1].start()
        for cp in cps:
            cp.wait()

    return pl.pallas_call(
        body, name="grad_sibling_exchange", in_specs=[_ANY] * n, out_specs=[_ANY] * n,
        out_shape=[jax.ShapeDtypeStruct(h.shape, h.dtype) for h in halves],
        scratch_shapes=[pltpu.SemaphoreType.DMA((n,)), pltpu.SemaphoreType.DMA((n,))],
    )(*halves)


def _allreduce_small(name, vec):
    def body(v_ref, o_ref, buf_ref, send_sems, recv_sems):
        x, y, c = _me()
        me = 4 * x + 2 * y + c
        cps = []
        for p in range(1, 8):
            px, py, pc = x ^ (p >> 2), y ^ ((p >> 1) & 1), c ^ (p & 1)
            cps.append(pltpu.make_async_remote_copy(src_ref=v_ref, dst_ref=buf_ref.at[me], send_sem=send_sems.at[p - 1],
                                                    recv_sem=recv_sems.at[p - 1], device_id=(px, py, pc), device_id_type=MESH_ID))
            cps[-1].start()
        buf_ref[me] = v_ref[...]
        for p in range(1, 8):
            theirs = buf_ref.at[me ^ p]
            pltpu.make_async_remote_copy(src_ref=theirs, dst_ref=theirs, send_sem=send_sems.at[p - 1], recv_sem=recv_sems.at[p - 1],
                                         device_id=(x, y, c), device_id_type=MESH_ID).wait_recv()
        for cp in cps:
            cp.wait_send()
        acc = buf_ref[0]
        for k in range(1, 8):
            acc = acc + buf_ref[k]
        o_ref[...] = acc

    vm = pl.BlockSpec(memory_space=pltpu.VMEM)
    return pl.pallas_call(
        body, name=name, in_specs=[vm], out_specs=vm, out_shape=jax.ShapeDtypeStruct(vec.shape, F32),
        scratch_shapes=[pltpu.VMEM((8,) + vec.shape, F32), pltpu.SemaphoreType.DMA((7,)), pltpu.SemaphoreType.DMA((7,))],
    )(vec)


def _adam_math(w, g, m, v):
    m = ADAM_B1 * m + (1.0 - ADAM_B1) * g
    v = ADAM_B2 * v + (1.0 - ADAM_B2) * (g * g)
    m_hat = m / (1.0 - ADAM_B1 ** ADAM_STEP)
    v_hat = v / (1.0 - ADAM_B2 ** ADAM_STEP)
    return -ADAM_LR * (m_hat / (jnp.sqrt(v_hat) + ADAM_EPS) + ADAM_WD * w), m, v


def _adamw(name, w, g, m, v):
    R, C = w.shape
    tr = _row_block(R, C, 8)

    def body(w_ref, g_ref, m_ref, v_ref, d_ref, nm_ref, nv_ref):
        d_ref[...], nm_ref[...], nv_ref[...] = _adam_math(w_ref[...], g_ref[...], m_ref[...], v_ref[...])

    blk = pl.BlockSpec((tr, C), lambda i: (i, 0))
    return pl.pallas_call(
        body, name=name, grid=(R // tr,), in_specs=[blk] * 4, out_specs=[blk] * 3,
        out_shape=[jax.ShapeDtypeStruct((R, C), F32)] * 3, compiler_params=_params(("parallel",)),
    )(w, g, m, v)


def _adamw_halves(name, w, mine, theirs, m, v, c):
    rs, cs = w.shape
    h = rs // 2
    tr = _row_block(h, cs, 8)
    nb = h // tr

    def body(c_ref, w_ref, a_ref, b_ref, m_ref, v_ref, g_ref, d_ref, nm_ref, nv_ref):
        g = jnp.where(pl.program_id(0) == c_ref[0], a_ref[...], b_ref[...])
        g_ref[...] = g
        d_ref[...], nm_ref[...], nv_ref[...] = _adam_math(w_ref[...], g, m_ref[...], v_ref[...])

    full = pl.BlockSpec((tr, cs), lambda s, i, c_ref: (s * nb + i, 0))
    part = pl.BlockSpec((tr, cs), lambda s, i, c_ref: (i, 0))
    return pl.pallas_call(
        body, name=name,
        grid_spec=pltpu.PrefetchScalarGridSpec(num_scalar_prefetch=1, grid=(2, nb), in_specs=[full, part, part, full, full],
                                               out_specs=[full] * 4),
        out_shape=[jax.ShapeDtypeStruct((rs, cs), F32)] * 4, compiler_params=_params(("parallel", "parallel")),
    )(_scalar(c), w, mine, theirs, m, v)


def _pack_small(arrs):
    flat = jnp.concatenate([a.reshape(-1) for a in arrs])
    n = -(-flat.shape[0] // (8 * LANE)) * 8 * LANE
    return jnp.pad(flat, (0, n - flat.shape[0])).reshape(8, n // 8)


def _unpack_small(vec, shapes):
    flat, out, off = vec.reshape(-1), [], 0
    for s in shapes:
        out.append(flat[off:off + s[0] * s[1]].reshape(s))
        off += s[0] * s[1]
    return out


def _step(cfg, a):
    chip = 2 * lax.axis_index("x") + lax.axis_index("y")
    core = lax.axis_index("c")
    big = [n for n, _, _, _ in cfg.BIG]

    gathered = _allgather_weights([_stage_shard(n, a[n], chip) for n in big])
    W = {n: _gathered_to_kernel(cfg, n, wg) for n, wg in zip(big, gathered)}

    sp = {n: a[n] for n in SMALL}
    sharded = _pack_small([a[n] for n in SMALL_SHARDED])
    slot = jnp.where(lax.broadcasted_iota(I32, (N_CHIPS,) + sharded.shape, 0) == chip, 0.5 * sharded[None], 0.0)
    allp = _allreduce_small("allgather_small", slot.reshape(N_CHIPS * 8, -1)).reshape((N_CHIPS,) + sharded.shape)
    per_chip = [_unpack_small(allp[ch], [a[n].shape for n in SMALL_SHARDED]) for ch in range(N_CHIPS)]
    for k, n in enumerate(SMALL_SHARDED):
        sp[n] = jnp.concatenate([per_chip[ch][k] for ch in range(N_CHIPS)], axis=1)

    def pair_sums(tag, names, gW):
        grads = [_grad_to_chips(cfg, n, gW[n]) for n in names]
        return [_pair_sum(n, g, t, core) for n, g, t in zip(names, grads, _pair_exchange(tag, grads))]

    early = ("w_down", "w_gate", "w_up")
    late = [n for n in big if n not in early]
    started = []

    def ffn_grads_ready(g_ffn):
        started.extend(_chip_exchange_start("ffn", pair_sums("ffn", early, g_ffn)))
        return started[-1]

    loss, grad_x, gW, gs = _local_grads(cfg, a["x"], a["loss_target"], W, sp, ffn_grads_ready)
    sums = dict(zip(late, pair_sums("rest", late, gW)))
    landed = dict(zip(late, _chip_exchange("rest", [sums[n] for n in late])))
    e_sums, e_landed = _chip_exchange_wait("ffn", *started[:4], landed[late[0]])
    sums.update(zip(early, e_sums))
    landed.update(zip(early, e_landed))
    mine = [_chip_sum(n, sums[n], landed[n], chip) for n in big]
    theirs = _sibling_exchange(mine)

    shapes = [gs[n].shape for n in SMALL] + [(1, LANE)]
    red = _unpack_small(_allreduce_small("allreduce_small", _pack_small([gs[n] for n in SMALL] + [loss])), shapes)
    g_small = dict(zip(SMALL, red[:-1]))
    for n in SMALL_SHARDED:
        cs = a[n].shape[1]
        g_small[n] = lax.dynamic_slice_in_dim(g_small[n], chip * cs, cs, axis=1)

    out = {"loss": red[-1][0, 0], "grad_x": grad_x}
    for n, gm, gt in zip(big, mine, theirs):
        out["grad_" + n], out["delta_" + n], out["new_m_" + n], out["new_v_" + n] = _adamw_halves(
            "adamw_" + n, a[n], gm, gt, a["m_" + n], a["v_" + n], core)
    sshapes = [a[n].shape for n in SMALL]
    d, nm, nv = _adamw("adamw_small", _pack_small([a[n] for n in SMALL]), _pack_small([g_small[n] for n in SMALL]),
                       _pack_small([a["m_" + n] for n in SMALL]), _pack_small([a["v_" + n] for n in SMALL]))
    for n, dd, mm, vv in zip(SMALL, _unpack_small(d, sshapes), _unpack_small(nm, sshapes), _unpack_small(nv, sshapes)):
        out["grad_" + n], out["delta_" + n], out["new_m_" + n], out["new_v_" + n] = g_small[n], dd, mm, vv
    return out


def kernel(x, mix_pre_g, w_in, q_norm_g, w_uq, kv_norm_g, w_ukv, ssm_conv_w, ssm_conv_b, dt_bias, a_log, d_skip, ssm_norm_g, w_out, mix_post_g, ffn_pre_g, w_gate, w_up, ffn_conv_w, ffn_conv_b, w_down, ffn_post_g, loss_target, m_mix_pre_g, m_w_in, m_q_norm_g, m_w_uq, m_kv_norm_g, m_w_ukv, m_ssm_conv_w, m_ssm_conv_b, m_dt_bias, m_a_log, m_d_skip, m_ssm_norm_g, m_w_out, m_mix_post_g, m_ffn_pre_g, m_w_gate, m_w_up, m_ffn_conv_w, m_ffn_conv_b, m_w_down, m_ffn_post_g, v_mix_pre_g, v_w_in, v_q_norm_g, v_w_uq, v_kv_norm_g, v_w_ukv, v_ssm_conv_w, v_ssm_conv_b, v_dt_bias, v_a_log, v_d_skip, v_ssm_norm_g, v_w_out, v_mix_post_g, v_ffn_pre_g, v_w_gate, v_w_up, v_ffn_conv_w, v_ffn_conv_b, v_w_down, v_ffn_post_g):
    args = dict(locals())
    out = _step(_FULL, {k: (v[0] if v.ndim == 3 else v) for k, v in args.items()})
    res = [out["loss"], out["grad_x"][None]]
    for pre in ("grad_", "delta_", "new_m_", "new_v_"):
        res += [out[pre + n][None] if args[n].ndim == 3 else out[pre + n] for n in WEIGHTS]
    return tuple(res)
```

```python
import functools
import math

import jax
import jax.numpy as jnp
from jax import lax
from jax.experimental import pallas as pl
from jax.experimental.pallas import tpu as pltpu

F32, BF16, I32 = jnp.float32, jnp.bfloat16, jnp.int32
NN = (((1,), (0,)), ((), ()))
NT = (((1,), (1,)), ((), ()))
TN = (((0,), (0,)), ((), ()))
HI = lax.Precision.HIGHEST
MESH_ID = pl.DeviceIdType.MESH

EPS = 1e-6
CHUNK = 64
NOPE, ROPE, VH = 128, 64, 128
ROPE_THETA = 10000.0
HP, NST = 64, 128
SSM_K, FFN_K = 4, 3
LANE = 128
N_CHIPS = 4
VMEM_LIMIT = 52 * 1024 * 1024

ADAM_LR, ADAM_B1, ADAM_B2, ADAM_EPS, ADAM_WD, ADAM_STEP = 0.001, 0.9, 0.999, 1e-08, 0.01, 10


class _Cfg:
    def __init__(self, S, D, QL, KVL, H, HS, G, DFF, T):
        self.S, self.D, self.QL, self.KVL, self.H, self.HS, self.G, self.DFF, self.T = S, D, QL, KVL, H, HS, G, DFF, T
        self.INNER = HS * HP
        self.CONVCH = self.INNER + 2 * G * NST
        self.QW = H * (NOPE + ROPE)
        self.KVW = H * (NOPE + VH)
        self.MLAW = H * VH
        self.MIXW = self.MLAW + self.INNER
        self.IN_COLS = QL + KVL + ROPE + self.INNER + self.CONVCH + HS
        self.o_kr = QL + KVL
        self.o_z = self.o_kr + LANE
        self.o_xbc = self.o_z + self.INNER
        self.o_dt = self.o_xbc + self.CONVCH
        self.EXT = self.o_dt + LANE
        self.NPAIR = HS // 2
        self.REP = HS // G
        self.BIG = (("w_in", D, self.IN_COLS, 1), ("w_uq", QL, self.QW, 1), ("w_ukv", KVL, self.KVW, 1),
                    ("w_out", self.MIXW, D, 0), ("w_gate", D, DFF, 1), ("w_up", D, DFF, 1), ("w_down", DFF, D, 0))
        self.NSHARD = sum(r * c for _, r, c, _ in self.BIG) // N_CHIPS
        unit = 2 * LANE * 16
        self.NPACK = -(-self.NSHARD // unit) * unit
        self.R = self.NPACK // (2 * LANE)


_FULL = _Cfg(S=2048, D=2048, QL=768, KVL=512, H=8, HS=16, G=2, DFF=5632, T=256)

SMALL = ("mix_pre_g", "q_norm_g", "kv_norm_g", "ssm_conv_w", "ssm_conv_b", "dt_bias", "a_log", "d_skip", "ssm_norm_g",
         "mix_post_g", "ffn_pre_g", "ffn_conv_w", "ffn_conv_b", "ffn_post_g")
SMALL_SHARDED = ("ssm_conv_w", "ffn_conv_w")
WEIGHTS = ("mix_pre_g", "w_in", "q_norm_g", "w_uq", "kv_norm_g", "w_ukv", "ssm_conv_w", "ssm_conv_b", "dt_bias", "a_log",
           "d_skip", "ssm_norm_g", "w_out", "mix_post_g", "ffn_pre_g", "w_gate", "w_up", "ffn_conv_w", "ffn_conv_b",
           "w_down", "ffn_post_g")


def _pick(n, target, mult):
    best = None
    for d in range(mult, min(n, target) + 1, mult):
        if n % d == 0:
            best = d
    return best if best is not None else n


def _params(sem=None):
    kw = dict(vmem_limit_bytes=VMEM_LIMIT)
    if sem is not None:
        kw["dimension_semantics"] = sem
    return pltpu.CompilerParams(**kw)


def _dot(a, b, dims=NN, precision=None):
    return lax.dot_general(a, b, dims, preferred_element_type=F32, precision=precision)


def _sigmoid(x):
    return 1.0 / (1.0 + jnp.exp(-x))


def _rs(x):
    return lax.rsqrt(jnp.mean(x * x, axis=-1, keepdims=True) + EPS)


def _rms_back(xh, r, dn):
    return r * (dn - xh * jnp.mean(dn * xh, axis=-1, keepdims=True))


def _colsum(v):
    return jnp.sum(v, axis=0, keepdims=True)


def _matmul(name, a, b, mode, out_dtype, a2=None, b2=None, chips=False):
    cs = None
    if mode == "nn":
        (M, K), N = a.shape, b.shape[-1]
        if chips:
            cs, N = N, N_CHIPS * N
    elif mode == "nt":
        (M, K), N = a.shape, b.shape[-2]
        if chips:
            cs = b.shape[-1]
    else:
        (K, M), N = a.shape, b.shape[1]
        if chips:
            cs = N // N_CHIPS
    tm = _pick(M, 1024, LANE)
    tn = _pick(cs if chips and mode != "nt" else N, 1408, LANE)
    tk = _pick(cs, 1408, LANE) if chips and mode == "nt" else _pick(K, 512, LANE)
    nk = K // tk
    dims = {"nn": NN, "nt": NT, "tn": TN}[mode]
    a_spec = pl.BlockSpec((tk, tm), lambda i, j, k: (k, i)) if mode == "tn" else pl.BlockSpec((tm, tk), lambda i, j, k: (i, k))
    b_spec = pl.BlockSpec((tn, tk), lambda i, j, k: (j, k)) if mode == "nt" else pl.BlockSpec((tk, tn), lambda i, j, k: (k, j))
    o_spec = pl.BlockSpec((tm, tn), lambda i, j, k: (i, j))
    o_shape = (M, N)
    if chips and mode == "nn":
        per = cs // tn
        b_spec = pl.BlockSpec((None, tk, tn), lambda i, j, k: (j // per, k, j % per))
    elif chips and mode == "nt":
        per = cs // tk
        b_spec = pl.BlockSpec((None, tn, tk), lambda i, j, k: (k // per, j, k % per))
    elif chips:
        per = cs // tn
        o_spec = pl.BlockSpec((None, tm, tn), lambda i, j, k: (j // per, i, j % per))
        o_shape = (N_CHIPS, M, cs)
    two = a2 is not None

    def body(*refs):
        o_ref, acc_ref = refs[-2], refs[-1]
        k = pl.program_id(2)

        @pl.when(k == 0)
        def _():
            acc_ref[...] = jnp.zeros_like(acc_ref)

        part = _dot(refs[0][...].astype(BF16), refs[1][...].astype(BF16), dims)
        if two:
            part += _dot(refs[2][...].astype(BF16), refs[3][...].astype(BF16), dims)
        acc_ref[...] += part

        @pl.when(k == nk - 1)
        def _():
            o_ref[...] = acc_ref[...].astype(o_ref.dtype)

    ins = (a, b, a2, b2) if two else (a, b)
    return pl.pallas_call(
        body, name=name, grid=(M // tm, N // tn, nk),
        in_specs=[a_spec, b_spec] * (2 if two else 1),
        out_specs=o_spec,
        out_shape=jax.ShapeDtypeStruct(o_shape, out_dtype),
        scratch_shapes=[pltpu.VMEM((tm, tn), F32)],
        compiler_params=_params(("parallel", "parallel", "arbitrary")),
    )(*ins)


def _rowwise(name, fn, rows, mats, outs, reds, ts):
    S = rows[0].shape[0]
    nr, nm, no = len(rows), len(mats), len(outs)

    def body(*refs):
        res = fn(*[r[...] for r in refs[:nr + nm]])
        res = res if isinstance(res, (tuple, list)) else (res,)
        for r, v in zip(refs[nr + nm:nr + nm + no], res[:no]):
            r[...] = v.astype(r.dtype)
        first = pl.program_id(0) == 0
        for r, v in zip(refs[nr + nm + no:], res[no:]):
            @pl.when(first)
            def _():
                r[...] = jnp.broadcast_to(v, r.shape)

            @pl.when(jnp.logical_not(first))
            def _():
                r[...] += jnp.broadcast_to(v, r.shape)

    in_specs = [pl.BlockSpec((ts, a.shape[1]), lambda i: (i, 0)) for a in rows]
    in_specs += [pl.BlockSpec(m.shape, lambda i, nd=m.ndim: (0,) * nd) for m in mats]
    out_specs = [pl.BlockSpec((ts, w), lambda i: (i, 0)) for w, _ in outs]
    out_specs += [pl.BlockSpec(s, lambda i: (0, 0)) for s in reds]
    out_shape = [jax.ShapeDtypeStruct((S, w), dt) for w, dt in outs] + [jax.ShapeDtypeStruct(s, F32) for s in reds]
    return pl.pallas_call(
        body, name=name, grid=(S // ts,), in_specs=in_specs, out_specs=out_specs, out_shape=out_shape,
        compiler_params=_params(("arbitrary",) if reds else ("parallel",)),
    )(*rows, *mats)


def _shift_down(v, s):
    if s == 0:
        return v
    rows = lax.broadcasted_iota(I32, v.shape, 0)
    return jnp.where(rows >= s, pltpu.roll(v, s, 0), 0.0)


def _shift_up(v, s):
    if s == 0:
        return v
    n = v.shape[0]
    rows = lax.broadcasted_iota(I32, v.shape, 0)
    return jnp.where(rows < n - s, pltpu.roll(v, n - s, 0), 0.0)


def _conv(x, w, b):
    K = w.shape[0]
    y = jnp.broadcast_to(b, x.shape)
    for k in range(K):
        y = y + w[k:k + 1, :] * _shift_down(x, K - 1 - k)
    return y


def _conv_back(x, w, dc):
    K = w.shape[0]
    dx = jnp.zeros_like(x)
    dw = []
    for k in range(K):
        dx = dx + w[k:k + 1, :] * _shift_up(dc, K - 1 - k)
        dw.append(_colsum(dc * _shift_down(x, K - 1 - k)))
    return dx, jnp.concatenate(dw, axis=0), _colsum(dc)


def _colwise(name, fn, cols, vecs, outs, pouts, tc):
    S, C = cols[0].shape
    nc_, nv, no = len(cols), len(vecs), len(outs)

    def body(*refs):
        res = fn(*[r[...] for r in refs[:nc_ + nv]])
        res = res if isinstance(res, (tuple, list)) else (res,)
        for r, v in zip(refs[nc_ + nv:], res):
            r[...] = v.astype(r.dtype)

    in_specs = [pl.BlockSpec((S, tc), lambda j: (0, j)) for _ in cols]
    in_specs += [pl.BlockSpec((v.shape[0], tc), lambda j: (0, j)) for v in vecs]
    out_specs = [pl.BlockSpec((S, tc), lambda j: (0, j)) for _ in outs] + [pl.BlockSpec((k, tc), lambda j: (0, j)) for k in pouts]
    out_shape = [jax.ShapeDtypeStruct((S, C), dt) for dt in outs] + [jax.ShapeDtypeStruct((k, C), F32) for k in pouts]
    return pl.pallas_call(
        body, name=name, grid=(C // tc,), in_specs=in_specs, out_specs=out_specs, out_shape=out_shape,
        compiler_params=_params(("parallel",)),
    )(*cols, *vecs)


_G0, _G1 = math.sqrt(2.0 / math.pi), 0.044715


def _gelu(g):
    th = jnp.tanh(_G0 * (g + _G1 * g * g * g))
    return 0.5 * g * (1.0 + th), th


def _ffn_act(gate_pre, up, w, b):
    act, _ = _gelu(_conv(gate_pre, w, b))
    return act * up


def _ffn_act_back(dact, gate_pre, up, w, b):
    g = _conv(gate_pre, w, b)
    ge, th = _gelu(g)
    dge = 0.5 * (1.0 + th) + 0.5 * g * (1.0 - th * th) * _G0 * (1.0 + 3.0 * _G1 * g * g)
    dup = dact * ge
    dgate_pre, dw, db = _conv_back(gate_pre, w, dact * up * dge)
    return dgate_pre, dup, dw, db


def _ssm_act(xbc, w, b):
    c = _conv(xbc, w, b)
    return c * _sigmoid(c)


def _ssm_act_back(dxc, xbc, w, b):
    c = _conv(xbc, w, b)
    sg = _sigmoid(c)
    return _conv_back(xbc, w, dxc * sg * (1.0 + c * (1.0 - sg)))


def _rope_tables(S):
    inv = 1.0 / (ROPE_THETA ** (jnp.arange(0, ROPE, 2, dtype=F32) / ROPE))
    ang = jnp.arange(S, dtype=F32)[:, None] * inv[None, :]
    cos, sin = jnp.cos(ang), jnp.sin(ang)
    return jnp.tile(cos, (1, 4)), jnp.tile(jnp.concatenate([-sin, sin], axis=1), (1, 2))


def _swap_halves(x):
    lane = lax.broadcasted_iota(I32, x.shape, 1)
    w = x.shape[1]
    return jnp.where((lane % ROPE) < ROPE // 2, pltpu.roll(x, w - ROPE // 2, 1), pltpu.roll(x, ROPE // 2, 1))


def _rot(x, cos2, sin2):
    return x * cos2 + _swap_halves(x) * sin2


def _rot_back(dy, cos2, sin2):
    return dy * cos2 + _swap_halves(dy * sin2)


def _mla_pack(cfg, q, kv, kr, cos2, sin2):
    S, H = cfg.S, cfg.H
    ts = _pick(S, 512, 8)

    def body(qn_ref, qr_ref, kn_ref, v_ref, kr_ref, c_ref, s_ref, Q_ref, K_ref, V_ref):
        h = pl.program_id(0)
        c2, s2 = c_ref[...], s_ref[...]
        Q_ref[0, :, 0:LANE] = qn_ref[...].astype(BF16)
        Q_ref[0, :, LANE:] = _rot(qr_ref[...], c2, s2).astype(BF16)
        K_ref[0, :, 0:LANE] = kn_ref[...].astype(BF16)
        krr = _rot(kr_ref[...], c2, s2)
        K_ref[0, :, LANE:] = jnp.where(h % 2 == 1, pltpu.roll(krr, ROPE, 1), krr).astype(BF16)
        V_ref[0] = v_ref[...].astype(BF16)

    blk = lambda f: pl.BlockSpec((ts, LANE), f)
    return pl.pallas_call(
        body, name="mla_pack", grid=(H, S // ts),
        in_specs=[blk(lambda h, i: (i, h)), blk(lambda h, i: (i, H + h // 2)), blk(lambda h, i: (i, h)),
                  blk(lambda h, i: (i, H + h)), blk(lambda h, i: (i, 0)), blk(lambda h, i: (i, 0)), blk(lambda h, i: (i, 0))],
        out_specs=[pl.BlockSpec((1, ts, 2 * LANE), lambda h, i: (h, i, 0)), pl.BlockSpec((1, ts, 2 * LANE), lambda h, i: (h, i, 0)),
                   pl.BlockSpec((1, ts, LANE), lambda h, i: (h, i, 0))],
        out_shape=[jax.ShapeDtypeStruct((H, S, 2 * LANE), BF16), jax.ShapeDtypeStruct((H, S, 2 * LANE), BF16),
                   jax.ShapeDtypeStruct((H, S, LANE), BF16)],
        compiler_params=_params(("parallel", "parallel")),
    )(q, q, kv, kv, kr, cos2, sin2)


def _mla_unpack(cfg, dQ, dK, dV, cos2, sin2):
    S, H = cfg.S, cfg.H
    ts = _pick(S, 256, 8)

    def body(dQ_ref, dK_ref, dV_ref, c_ref, s_ref, dq_ref, dkv_ref, dkr_ref):
        c2, s2 = c_ref[...], s_ref[...]
        lo = lax.broadcasted_iota(I32, (ts, LANE), 1) < ROPE
        tk = jnp.zeros((ts, LANE), F32)
        for h in range(H):
            dq_ref[:, h * LANE:(h + 1) * LANE] = dQ_ref[h, :, 0:LANE].astype(BF16)
            dkv_ref[:, h * LANE:(h + 1) * LANE] = dK_ref[h, :, 0:LANE].astype(BF16)
            dkv_ref[:, (H + h) * LANE:(H + h + 1) * LANE] = dV_ref[h].astype(BF16)
            own = lo if h % 2 == 0 else jnp.logical_not(lo)
            tk = tk + jnp.where(own, dK_ref[h, :, LANE:], 0.0)
        for j in range(H // 2):
            dr = dQ_ref[2 * j, :, LANE:] + dQ_ref[2 * j + 1, :, LANE:]
            dq_ref[:, (H + j) * LANE:(H + j + 1) * LANE] = _rot_back(dr, c2, s2).astype(BF16)
        dkr_rot = jnp.where(lo, tk + pltpu.roll(tk, ROPE, 1), 0.0)
        dkr_ref[...] = _rot_back(dkr_rot, c2, s2).astype(BF16)

    tab = pl.BlockSpec((ts, LANE), lambda i: (i, 0))
    return pl.pallas_call(
        body, name="mla_unpack", grid=(S // ts,),
        in_specs=[pl.BlockSpec((H, ts, 2 * LANE), lambda i: (0, i, 0)), pl.BlockSpec((H, ts, 2 * LANE), lambda i: (0, i, 0)),
                  pl.BlockSpec((H, ts, LANE), lambda i: (0, i, 0)), tab, tab],
        out_specs=[pl.BlockSpec((ts, cfg.QW), lambda i: (i, 0)), pl.BlockSpec((ts, cfg.KVW), lambda i: (i, 0)), tab],
        out_shape=[jax.ShapeDtypeStruct((S, cfg.QW), BF16), jax.ShapeDtypeStruct((S, cfg.KVW), BF16),
                   jax.ShapeDtypeStruct((S, LANE), BF16)],
        compiler_params=_params(("parallel",)),
    )(dQ, dK, dV, cos2, sin2)


_ATT_T = 256
_ATT_SCALE = (NOPE + ROPE) ** -0.5


def _att_scores(q, k, qi, kb):
    s = _dot(q, k, NT) * _ATT_SCALE
    rows = qi * _ATT_T + lax.broadcasted_iota(I32, s.shape, 0)
    cols = kb * _ATT_T + lax.broadcasted_iota(I32, s.shape, 1)
    return jnp.where((cols // CHUNK) <= (rows // CHUNK), s, -1e30)


def _attn_fwd(cfg, Q, K, V):
    S, H, T = cfg.S, cfg.H, _ATT_T

    def body(q_ref, k_ref, v_ref, o_ref, lse_ref):
        qi = pl.program_id(1)
        q = q_ref[0]

        def step(kb, carry):
            m, l, acc = carry
            ks = pl.multiple_of(kb * T, T)
            s = _att_scores(q, k_ref[0, pl.ds(ks, T), :], qi, kb)
            m_new = jnp.maximum(m, jnp.max(s, axis=1, keepdims=True))
            p = jnp.exp(s - m_new)
            alpha = jnp.exp(m - m_new)
            l = alpha * l + jnp.sum(p, axis=1, keepdims=True)
            acc = alpha * acc + _dot(p.astype(BF16), v_ref[0, pl.ds(ks, T), :])
            return m_new, l, acc

        init = (jnp.full((T, 1), -1e30, F32), jnp.zeros((T, 1), F32), jnp.zeros((T, VH), F32))
        m, l, acc = lax.fori_loop(0, qi + 1, step, init)
        o_ref[...] = acc / l
        lse_ref[...] = jnp.broadcast_to(m + jnp.log(l), (T, LANE))

    return pl.pallas_call(
        body, name="attn_fwd", grid=(H, S // T),
        in_specs=[pl.BlockSpec((1, T, 2 * LANE), lambda h, i: (h, i, 0)), pl.BlockSpec((1, S, 2 * LANE), lambda h, i: (h, 0, 0)),
                  pl.BlockSpec((1, S, LANE), lambda h, i: (h, 0, 0))],
        out_specs=[pl.BlockSpec((T, LANE), lambda h, i: (i, h)), pl.BlockSpec((T, LANE), lambda h, i: (i, h))],
        out_shape=[jax.ShapeDtypeStruct((S, H * LANE), F32), jax.ShapeDtypeStruct((S, H * LANE), F32)],
        compiler_params=_params(("parallel", "parallel")),
    )(Q, K, V)


def _attn_dq(cfg, Q, K, V, do, o, lse):
    S, H, T = cfg.S, cfg.H, _ATT_T

    def body(q_ref, k_ref, v_ref, do_ref, o_ref, lse_ref, dq_ref, dl_ref):
        qi = pl.program_id(1)
        q = q_ref[0]
        do = do_ref[...]
        delta = jnp.sum(do * o_ref[...], axis=1, keepdims=True)
        lse = lse_ref[:, 0:1]
        dob = do.astype(BF16)

        def step(kb, dq):
            ks = pl.multiple_of(kb * T, T)
            k = k_ref[0, pl.ds(ks, T), :]
            p = jnp.exp(_att_scores(q, k, qi, kb) - lse)
            dp = _dot(dob, v_ref[0, pl.ds(ks, T), :], NT)
            ds = p * (dp - delta) * _ATT_SCALE
            return dq + _dot(ds.astype(BF16), k)

        dq_ref[0] = lax.fori_loop(0, qi + 1, step, jnp.zeros((T, 2 * LANE), F32))
        dl_ref[...] = jnp.broadcast_to(delta, (T, LANE))

    col = pl.BlockSpec((T, LANE), lambda h, i: (i, h))
    return pl.pallas_call(
        body, name="attn_dq", grid=(H, S // T),
        in_specs=[pl.BlockSpec((1, T, 2 * LANE), lambda h, i: (h, i, 0)), pl.BlockSpec((1, S, 2 * LANE), lambda h, i: (h, 0, 0)),
                  pl.BlockSpec((1, S, LANE), lambda h, i: (h, 0, 0)), col, col, col],
        out_specs=[pl.BlockSpec((1, T, 2 * LANE), lambda h, i: (h, i, 0)), col],
        out_shape=[jax.ShapeDtypeStruct((H, S, 2 * LANE), F32), jax.ShapeDtypeStruct((S, H * LANE), F32)],
        compiler_params=_params(("parallel", "parallel")),
    )(Q, K, V, do, o, lse)


def _attn_dkv(cfg, Q, K, V, do, lse, delta):
    S, H, T = cfg.S, cfg.H, _ATT_T
    nq = S // T

    def body(q_ref, k_ref, v_ref, do_ref, lse_ref, dl_ref, dk_ref, dv_ref):
        kb = pl.program_id(1)
        k, v = k_ref[0], v_ref[0]

        def step(qi, carry):
            dk, dv = carry
            qs = pl.multiple_of(qi * T, T)
            q = q_ref[0, pl.ds(qs, T), :]
            dob = do_ref[pl.ds(qs, T), :].astype(BF16)
            p = jnp.exp(_att_scores(q, k, qi, kb) - lse_ref[pl.ds(qs, T), 0:1])
            dv = dv + _dot(p.astype(BF16), dob, TN)
            dp = _dot(dob, v, NT)
            ds = p * (dp - dl_ref[pl.ds(qs, T), 0:1]) * _ATT_SCALE
            dk = dk + _dot(ds.astype(BF16), q, TN)
            return dk, dv

        dk, dv = lax.fori_loop(kb, nq, step, (jnp.zeros((T, 2 * LANE), F32), jnp.zeros((T, VH), F32)))
        dk_ref[0] = dk
        dv_ref[0] = dv

    col = pl.BlockSpec((S, LANE), lambda h, j: (0, h))
    return pl.pallas_call(
        body, name="attn_dkv", grid=(H, S // T),
        in_specs=[pl.BlockSpec((1, S, 2 * LANE), lambda h, j: (h, 0, 0)), pl.BlockSpec((1, T, 2 * LANE), lambda h, j: (h, j, 0)),
                  pl.BlockSpec((1, T, LANE), lambda h, j: (h, j, 0)), col, col, col],
        out_specs=[pl.BlockSpec((1, T, 2 * LANE), lambda h, j: (h, j, 0)), pl.BlockSpec((1, T, LANE), lambda h, j: (h, j, 0))],
        out_shape=[jax.ShapeDtypeStruct((H, S, 2 * LANE), F32), jax.ShapeDtypeStruct((H, S, LANE), F32)],
        compiler_params=_params(("parallel", "parallel")),
    )(Q, K, V, do, lse, delta)


def _expand_matrix(cfg):
    r = lax.broadcasted_iota(I32, (LANE, cfg.INNER), 0)
    c = lax.broadcasted_iota(I32, (LANE, cfg.INNER), 1)
    return (r == c // HP).astype(F32)


def _softplus(x):
    return jnp.maximum(x, 0.0) + jnp.log(1.0 + jnp.exp(-jnp.abs(x)))


def _ssd_prep(cfg, dt_raw, dt_bias_pad, a_log_pad, expand):
    HS = cfg.HS

    def fn(raw, bias, alog, E):
        heads = lax.broadcasted_iota(I32, raw.shape, 1) < HS
        dt = jnp.where(heads, _softplus(raw + bias), 0.0)
        a = dt * jnp.where(heads[0:1], -jnp.exp(alog), 0.0)
        return dt, a, _dot(dt, E, precision=HI), _dot(a, E, precision=HI)

    return _rowwise("ssd_prep", fn, [dt_raw], [dt_bias_pad, a_log_pad, expand],
                    [(LANE, F32), (LANE, F32), (cfg.INNER, F32), (cfg.INNER, F32)], [], _pick(cfg.S, 512, 8))


def _tril(T):
    return lax.broadcasted_iota(I32, (T, T), 0) >= lax.broadcasted_iota(I32, (T, T), 1)


def _ssd_fwd(cfg, xc, dt_exp, a_exp, a_small, dskip_exp):
    S, T, INNER, G, NPAIR = cfg.S, cfg.T, cfg.INNER, cfg.G, cfg.NPAIR
    NC = S // T

    def body(xc_ref, dte_ref, ae_ref, as_ref, dsk_ref, y_ref, hin_ref, ht_ref):
        @pl.when(pl.program_id(0) == 0)
        def _():
            ht_ref[...] = jnp.zeros_like(ht_ref)

        tril = _tril(T)
        tri = tril.astype(F32)
        acs_s = _dot(tri, as_ref[...], precision=HI)
        acs_e = _dot(tri, ae_ref[...], precision=HI)
        acs_t = acs_s.T
        lo = lax.broadcasted_iota(I32, (T, LANE), 1) < HP
        for g in range(G):
            Bb = xc_ref[:, INNER + g * NST:INNER + (g + 1) * NST].astype(BF16)
            Cb = xc_ref[:, INNER + (G + g) * NST:INNER + (G + g + 1) * NST].astype(BF16)
            Gm = _dot(Cb, Bb, NT)
            for j in range(g * NPAIR // G, (g + 1) * NPAIR // G):
                sl = slice(j * LANE, (j + 1) * LANE)
                Xp = xc_ref[:, sl]
                Xdt = Xp * dte_ref[:, sl]
                Xb = Xdt.astype(BF16)
                acs_p = acs_e[:, sl]
                last = acs_p[T - 1:T, :]
                Hin = ht_ref[j]
                hin_ref[0, j] = Hin
                yd = []
                for e in (0, 1):
                    h = 2 * j + e
                    Lm = jnp.exp(jnp.where(tril, acs_s[:, h:h + 1] - acs_t[h:h + 1, :], -1e30))
                    yd.append(_dot((Gm * Lm).astype(BF16), Xb))
                y_off = _dot(Cb, Hin.astype(BF16)) * jnp.exp(acs_p)
                y_ref[:, sl] = jnp.where(lo, yd[0], yd[1]) + y_off + Xp * dsk_ref[:, sl]
                st = _dot(Bb, (Xdt * jnp.exp(last - acs_p)).astype(BF16), TN)
                ht_ref[j] = jnp.exp(last) * Hin + st

    rows = lambda w: pl.BlockSpec((T, w), lambda c: (c, 0))
    return pl.pallas_call(
        body, name="ssd_fwd", grid=(NC,),
        in_specs=[rows(cfg.CONVCH), rows(INNER), rows(INNER), rows(LANE), pl.BlockSpec((1, INNER), lambda c: (0, 0))],
        out_specs=[rows(INNER), pl.BlockSpec((1, NPAIR, NST, LANE), lambda c: (c, 0, 0, 0))],
        out_shape=[jax.ShapeDtypeStruct((S, INNER), F32), jax.ShapeDtypeStruct((NC, NPAIR, NST, LANE), F32)],
        scratch_shapes=[pltpu.VMEM((NPAIR, NST, LANE), F32)],
        compiler_params=_params(("arbitrary",)),
    )(xc, dt_exp, a_exp, a_small, dskip_exp)


def _ssd_bwd(cfg, dy, xc, dt_exp, a_exp, a_small, dskip_exp, hin, dt_raw, dt_bias_pad, a_log_pad, expand):
    S, T, INNER, G, NPAIR, HS = cfg.S, cfg.T, cfg.INNER, cfg.G, cfg.NPAIR, cfg.HS
    NC = S // T

    def body(dy_ref, xc_ref, dte_ref, ae_ref, as_ref, dsk_ref, hin_ref, raw_ref, bias_ref, alog_ref, e_ref,
             dxc_ref, draw_ref, dbias_ref, dalog_ref, dskip_ref, dht_ref, cols_ref, rows_ref, dacs_ref, ddt_ref):
        first = pl.program_id(0) == 0

        @pl.when(first)
        def _():
            dht_ref[...] = jnp.zeros_like(dht_ref)

        tril = _tril(T)
        tri = tril.astype(F32)
        a_s = as_ref[...]
        acs_s = _dot(tri, a_s, precision=HI)
        acs_e = _dot(tri, ae_ref[...], precision=HI)
        acs_t = acs_s.T
        lo = lax.broadcasted_iota(I32, (T, LANE), 1) < HP
        last_row = lax.broadcasted_iota(I32, (T, LANE), 0) == T - 1
        cols_ref[...] = jnp.zeros_like(cols_ref)
        rows_ref[...] = jnp.zeros_like(rows_ref)
        dsk_parts = []
        for g in range(G):
            bsl = slice(INNER + g * NST, INNER + (g + 1) * NST)
            csl = slice(INNER + (G + g) * NST, INNER + (G + g + 1) * NST)
            Bb = xc_ref[:, bsl].astype(BF16)
            Cb = xc_ref[:, csl].astype(BF16)
            Gm = _dot(Cb, Bb, NT)
            dG = jnp.zeros((T, T), F32)
            dB = jnp.zeros((T, NST), F32)
            dC = jnp.zeros((T, NST), F32)
            for j in range(g * NPAIR // G, (g + 1) * NPAIR // G):
                sl = slice(j * LANE, (j + 1) * LANE)
                Xp = xc_ref[:, sl]
                dtp = dte_ref[:, sl]
                Xdt = Xp * dtp
                Xb = Xdt.astype(BF16)
                acs_p = acs_e[:, sl]
                last = acs_p[T - 1:T, :]
                e_p, dec, cd = jnp.exp(acs_p), jnp.exp(last - acs_p), jnp.exp(last)
                Hin = hin_ref[0, j]
                Hb = Hin.astype(BF16)
                dHn = dht_ref[j]
                dHb = dHn.astype(BF16)
                dYp = dy_ref[:, sl]
                z = _dot(Cb, Hb)
                dz = (dYp * e_p).astype(BF16)
                dacs_p = dYp * z * e_p
                dC = dC + _dot(dz, Hb, NT)
                dHin = _dot(Cb, dz, TN) + cd * dHn
                dlast = _colsum(dHn * Hin) * cd
                qv = _dot(Bb, dHb)
                dXdt = qv * dec
                ddec = qv * Xdt * dec
                dacs_p = dacs_p - ddec
                dlast = dlast + _colsum(ddec)
                dB = dB + _dot((Xdt * dec).astype(BF16), dHb, NT)
                for e in (0, 1):
                    h = 2 * j + e
                    Lm = jnp.exp(jnp.where(tril, acs_s[:, h:h + 1] - acs_t[h:h + 1, :], -1e30))
                    Mh = Gm * Lm
                    dYe = jnp.where(lo if e == 0 else jnp.logical_not(lo), dYp, 0.0).astype(BF16)
                    dM = _dot(dYe, Xb, NT)
                    dXdt = dXdt + _dot(Mh.astype(BF16), dYe, TN)
                    W = dM * Mh
                    cols_ref[:, h:h + 1] = jnp.sum(W, axis=1, keepdims=True)
                    rows_ref[h:h + 1, :] = _colsum(W)
                    dG = dG + dM * Lm
                dacs_ref[:, sl] = dacs_p + jnp.where(last_row, dlast, 0.0)
                ddt_ref[:, sl] = dXdt * Xp
                dxc_ref[:, sl] = dXdt * dtp + dYp * dsk_ref[:, sl]
                dsk_parts.append(_colsum(dYp * Xp))
                dht_ref[j] = dHin
            dGb = dG.astype(BF16)
            dxc_ref[:, bsl] = dB + _dot(dGb, Cb, TN)
            dxc_ref[:, csl] = dC + _dot(dGb, Bb)
        E = e_ref[...]
        dacs_s = cols_ref[...] - rows_ref[...].T + _dot(dacs_ref[...], E, NT, precision=HI)
        da = _dot(tri, dacs_s, TN, precision=HI)
        heads = lax.broadcasted_iota(I32, (1, LANE), 1) < HS
        A = jnp.where(heads, -jnp.exp(alog_ref[...]), 0.0)
        ddt = _dot(ddt_ref[...], E, NT, precision=HI) + da * A
        draw = jnp.where(heads, ddt * _sigmoid(raw_ref[...] + bias_ref[...]), 0.0)
        draw_ref[...] = draw
        dsk = _dot(jnp.broadcast_to(jnp.concatenate(dsk_parts, axis=1), (8, INNER)), E, NT, precision=HI)[0:1]
        for ref, val in ((dbias_ref, _colsum(draw)), (dalog_ref, _colsum(da * a_s)), (dskip_ref, dsk)):
            @pl.when(first)
            def _():
                ref[...] = val

            @pl.when(jnp.logical_not(first))
            def _():
                ref[...] += val

    rows = lambda w: pl.BlockSpec((T, w), lambda c: (NC - 1 - c, 0))
    vec = lambda w: pl.BlockSpec((1, w), lambda c: (0, 0))
    return pl.pallas_call(
        body, name="ssd_bwd", grid=(NC,),
        in_specs=[rows(INNER), rows(cfg.CONVCH), rows(INNER), rows(INNER), rows(LANE), vec(INNER),
                  pl.BlockSpec((1, NPAIR, NST, LANE), lambda c: (NC - 1 - c, 0, 0, 0)), rows(LANE), vec(LANE), vec(LANE),
                  pl.BlockSpec((LANE, INNER), lambda c: (0, 0))],
        out_specs=[rows(cfg.CONVCH), rows(LANE), vec(LANE), vec(LANE), vec(LANE)],
        out_shape=[jax.ShapeDtypeStruct((S, cfg.CONVCH), F32), jax.ShapeDtypeStruct((S, LANE), F32)]
        + [jax.ShapeDtypeStruct((1, LANE), F32)] * 3,
        scratch_shapes=[pltpu.VMEM((NPAIR, NST, LANE), F32), pltpu.VMEM((T, LANE), F32), pltpu.VMEM((LANE, T), F32),
                        pltpu.VMEM((T, INNER), F32), pltpu.VMEM((T, INNER), F32)],
        compiler_params=_params(("arbitrary",)),
    )(dy, xc, dt_exp, a_exp, a_small, dskip_exp, hin, dt_raw, dt_bias_pad, a_log_pad, expand)


def _ssd_post(cfg, y, z, norm_g):
    W = cfg.INNER // cfg.G

    def fn(y, z, g):
        yz = y * z * _sigmoid(z)
        return jnp.concatenate([yz[:, i * W:(i + 1) * W] * _rs(yz[:, i * W:(i + 1) * W]) for i in range(cfg.G)], axis=1) * g

    return _rowwise("ssd_post", fn, [y, z], [norm_g], [(cfg.INNER, BF16)], [], _pick(cfg.S, 256, 8))[0]


def _ssd_post_bwd(cfg, db, y, z, norm_g):
    W = cfg.INNER // cfg.G

    def fn(db, y, z, g):
        sg = _sigmoid(z)
        yz = y * z * sg
        dn = db * g
        dyz, nh = [], []
        for i in range(cfg.G):
            seg = yz[:, i * W:(i + 1) * W]
            r = _rs(seg)
            nh.append(seg * r)
            dyz.append(_rms_back(nh[-1], r, dn[:, i * W:(i + 1) * W]))
        dyz = jnp.concatenate(dyz, axis=1)
        return dyz * z * sg, dyz * y * sg * (1.0 + z * (1.0 - sg)), _colsum(db * jnp.concatenate(nh, axis=1))

    return _rowwise("ssd_post_bwd", fn, [db, y, z], [norm_g], [(cfg.INNER, F32), (cfg.INNER, F32)], [(1, cfg.INNER)],
                    _pick(cfg.S, 256, 8))


def _local_grads(cfg, x, tgt, W, sp, ffn_weights=None, ffn_grads_ready=None):
    S, D, H, INNER = cfg.S, cfg.D, cfg.H, cfg.INNER
    ts = _pick(S, 256, 8)
    tc = 256

    xn = _rowwise("rms_pre", lambda x, g: x * _rs(x) * g, [x], [sp["mix_pre_g"]], [(D, BF16)], [], ts)[0]
    u = _matmul("mm_in", xn, W["w_in"], "nn", F32)
    c_q, c_kv = u[:, :cfg.QL], u[:, cfg.QL:cfg.o_kr]
    kr = u[:, cfg.o_kr:cfg.o_z]
    z = u[:, cfg.o_z:cfg.o_xbc]
    xbc = u[:, cfg.o_xbc:cfg.o_dt]
    dt_raw = u[:, cfg.o_dt:]

    cqn = _rowwise("rms_q", lambda x, g: x * _rs(x) * g, [c_q], [sp["q_norm_g"]], [(cfg.QL, BF16)], [], ts)[0]
    ckvn = _rowwise("rms_kv", lambda x, g: x * _rs(x) * g, [c_kv], [sp["kv_norm_g"]], [(cfg.KVL, BF16)], [], ts)[0]
    q = _matmul("mm_uq", cqn, W["w_uq"], "nn", F32)
    kv = _matmul("mm_ukv", ckvn, W["w_ukv"], "nn", F32)
    cos2, sin2 = _rope_tables(S)
    Qh, Kh, Vh = _mla_pack(cfg, q, kv, kr, cos2, sin2)
    a_out, lse = _attn_fwd(cfg, Qh, Kh, Vh)

    pad = lambda v: jnp.pad(v, ((0, 0), (0, LANE - v.shape[1])))
    expand = _expand_matrix(cfg)
    dt_bias_pad, a_log_pad = pad(sp["dt_bias"]), pad(sp["a_log"])
    dskip_exp = jnp.repeat(sp["d_skip"], HP, axis=1)
    xc = _colwise("ssm_act", _ssm_act, [xbc], [sp["ssm_conv_w"], sp["ssm_conv_b"]], [F32], [], tc)[0]
    dt_s, a_s, dt_exp, a_exp = _ssd_prep(cfg, dt_raw, dt_bias_pad, a_log_pad, expand)
    y_ssd, hin = _ssd_fwd(cfg, xc, dt_exp, a_exp, a_s, dskip_exp)
    b_out = _ssd_post(cfg, y_ssd, z, sp["ssm_norm_g"])

    ab_out = jnp.concatenate([a_out.astype(BF16), b_out], axis=1)
    if ffn_weights is not None:
        sp = dict(sp, mix_post_g=sp["mix_post_g"] + ffn_weights.pass_on(ab_out)[0, 0])
    mix = _matmul("mm_out", ab_out, W["w_out"], "nn", F32)

    def mid(x, mix, g_mp, g_fp):
        x1 = x + mix * _rs(mix) * g_mp
        return x1, x1 * _rs(x1) * g_fp

    x1, h2 = _rowwise("fwd_mid", mid, [x, mix], [sp["mix_post_g"], sp["ffn_pre_g"]], [(D, F32), (D, BF16)], [], ts)
    if ffn_weights is not None:
        W = dict(W, **ffn_weights.arrived(h2))
    gate_pre = _matmul("mm_gate", h2, W["w_gate"], "nn", F32, chips=True)
    up = _matmul("mm_up", h2, W["w_up"], "nn", F32, chips=True)
    act = _colwise("ffn_act", _ffn_act, [gate_pre, up], [sp["ffn_conv_w"], sp["ffn_conv_b"]], [BF16], [], tc)[0]
    f = _matmul("mm_down", act, W["w_down"], "nn", F32)

    def final(x1, f, t, g):
        r = _rs(f)
        fh = f * r
        err = x1 + fh * g - t
        loss = 0.5 * jnp.sum(jnp.mean(err * err, axis=-1, keepdims=True), axis=0, keepdims=True)
        dy = err * (1.0 / D)
        return dy, _rms_back(fh, r, dy * g), _colsum(dy * fh), loss

    dy, df, g_ffn_post, loss = _rowwise("final", final, [x1, f, tgt], [sp["ffn_post_g"]], [(D, F32), (D, BF16)],
                                        [(1, D), (1, LANE)], ts)
    gW = {}
    dact = _matmul("mm_down_dx", df, W["w_down"], "nt", F32)
    gW["w_down"] = _matmul("mm_down_dw", act, df, "tn", BF16)
    dgate, dup, g_ffn_conv_w, g_ffn_conv_b = _colwise(
        "ffn_act_bwd", _ffn_act_back, [dact, gate_pre, up], [sp["ffn_conv_w"], sp["ffn_conv_b"]], [BF16, BF16], [FFN_K, 1], tc)
    dh2 = _matmul("mm_gu_dx", dgate, W["w_gate"], "nt", F32, dup, W["w_up"], chips=True)
    gW["w_gate"] = _matmul("mm_gate_dw", h2, dgate, "tn", BF16, chips=True)
    gW["w_up"] = _matmul("mm_up_dw", h2, dup, "tn", BF16, chips=True)

    def mid_back(dy, dh2, x1, mix, g_mp, g_fp):
        r2 = _rs(x1)
        xh = x1 * r2
        dx1 = dy + _rms_back(xh, r2, dh2 * g_fp)
        r1 = _rs(mix)
        mh = mix * r1
        return dx1, _rms_back(mh, r1, dx1 * g_mp), _colsum(dh2 * xh), _colsum(dx1 * mh)

    dx1, dmix, g_ffn_pre, g_mix_post = _rowwise("bwd_mid", mid_back, [dy, dh2, x1, mix], [sp["mix_post_g"], sp["ffn_pre_g"]],
                                                [(D, F32), (D, BF16)], [(1, D), (1, D)], ts)
    dab_out = _matmul("mm_out_dx", dmix, W["w_out"], "nt", F32)
    db_out = dab_out[:, cfg.MLAW:]
    gW["w_out"] = _matmul("mm_out_dw", ab_out, dmix, "tn", BF16)
    if ffn_grads_ready is not None:
        token = ffn_grads_ready({n: gW[n] for n in ("w_down", "w_gate", "w_up", "w_out")})
        sp = dict(sp, ssm_norm_g=sp["ssm_norm_g"] + token[0, 0])

    dy_ssd, dz, g_ssm_norm = _ssd_post_bwd(cfg, db_out, y_ssd, z, sp["ssm_norm_g"])
    dxc, ddt_raw, g_dt_bias, g_a_log, g_d_skip = _ssd_bwd(cfg, dy_ssd, xc, dt_exp, a_exp, a_s, dskip_exp, hin, dt_raw,
                                                          dt_bias_pad, a_log_pad, expand)
    dxbc, g_ssm_conv_w, g_ssm_conv_b = _colwise("ssm_act_bwd", _ssm_act_back, [dxc, xbc], [sp["ssm_conv_w"], sp["ssm_conv_b"]],
                                                [BF16], [SSM_K, 1], tc)

    dQ, delta = _attn_dq(cfg, Qh, Kh, Vh, dab_out, a_out, lse)
    dK, dV = _attn_dkv(cfg, Qh, Kh, Vh, dab_out, lse, delta)
    dq, dkv, dkr = _mla_unpack(cfg, dQ, dK, dV, cos2, sin2)
    dcqn = _matmul("mm_uq_dx", dq, W["w_uq"], "nt", F32)
    dckvn = _matmul("mm_ukv_dx", dkv, W["w_ukv"], "nt", F32)
    gW["w_uq"] = _matmul("mm_uq_dw", cqn, dq, "tn", BF16)
    gW["w_ukv"] = _matmul("mm_ukv_dw", ckvn, dkv, "tn", BF16)

    def rms_back(x, dy, g):
        r = _rs(x)
        xh = x * r
        return _rms_back(xh, r, dy * g), _colsum(dy * xh)

    dc_q, g_q_norm = _rowwise("rms_q_bwd", rms_back, [c_q, dcqn], [sp["q_norm_g"]], [(cfg.QL, BF16)], [(1, cfg.QL)], ts)
    dc_kv, g_kv_norm = _rowwise("rms_kv_bwd", rms_back, [c_kv, dckvn], [sp["kv_norm_g"]], [(cfg.KVL, BF16)], [(1, cfg.KVL)], ts)

    du = jnp.concatenate([dc_q, dc_kv, dkr, dz.astype(BF16), dxbc, ddt_raw.astype(BF16)], axis=1)
    dxn = _matmul("mm_in_dx", du, W["w_in"], "nt", F32)
    gW["w_in"] = _matmul("mm_in_dw", xn, du, "tn", BF16)

    def first_back(dx1, dxn, x, g):
        r = _rs(x)
        xh = x * r
        return dx1 + _rms_back(xh, r, dxn * g), _colsum(dxn * xh)

    grad_x, g_mix_pre = _rowwise("bwd_first", first_back, [dx1, dxn, x], [sp["mix_pre_g"]], [(D, F32)], [(1, D)], ts)

    gs = dict(mix_pre_g=g_mix_pre, q_norm_g=g_q_norm, kv_norm_g=g_kv_norm, ssm_conv_w=g_ssm_conv_w, ssm_conv_b=g_ssm_conv_b,
              dt_bias=g_dt_bias[:, :cfg.HS], a_log=g_a_log[:, :cfg.HS], d_skip=g_d_skip[:, :cfg.HS], ssm_norm_g=g_ssm_norm,
              mix_post_g=g_mix_post, ffn_pre_g=g_ffn_pre, ffn_conv_w=g_ffn_conv_w, ffn_conv_b=g_ffn_conv_b,
              ffn_post_g=g_ffn_post)
    return loss, grad_x, gW, gs


def _to_kernel_layout(cfg, name, w):
    if name == "w_in":
        a = cfg.o_kr + ROPE
        return jnp.concatenate([w[:, :a], jnp.zeros((w.shape[0], LANE - ROPE), w.dtype), w[:, a:],
                                jnp.zeros((w.shape[0], LANE - cfg.HS), w.dtype)], axis=1)
    if name == "w_uq":
        w3 = w.reshape(cfg.QL, cfg.H, NOPE + ROPE)
        return jnp.concatenate([w3[:, :, :NOPE].reshape(cfg.QL, -1), w3[:, :, NOPE:].reshape(cfg.QL, -1)], axis=1)
    if name == "w_ukv":
        w3 = w.reshape(cfg.KVL, cfg.H, NOPE + VH)
        return jnp.concatenate([w3[:, :, :NOPE].reshape(cfg.KVL, -1), w3[:, :, NOPE:].reshape(cfg.KVL, -1)], axis=1)
    return w


def _from_kernel_layout(cfg, name, g):
    if name == "w_in":
        return jnp.concatenate([g[:, :cfg.o_kr + ROPE], g[:, cfg.o_z:cfg.o_dt + cfg.HS]], axis=1)
    if name == "w_uq":
        n = g[:, :cfg.H * NOPE].reshape(cfg.QL, cfg.H, NOPE)
        r = g[:, cfg.H * NOPE:].reshape(cfg.QL, cfg.H, ROPE)
        return jnp.concatenate([n, r], axis=2).reshape(cfg.QL, -1)
    if name == "w_ukv":
        n = g[:, :cfg.H * NOPE].reshape(cfg.KVL, cfg.H, NOPE)
        v = g[:, cfg.H * NOPE:].reshape(cfg.KVL, cfg.H, VH)
        return jnp.concatenate([n, v], axis=2).reshape(cfg.KVL, -1)
    return g


def _cols_to_chips(w):
    r, c = w.shape
    return w.reshape(r, N_CHIPS, c // N_CHIPS).transpose(1, 0, 2)


def _chips_to_cols(g):
    k, r, cs = g.shape
    return g.transpose(1, 0, 2).reshape(r, k * cs)


_CHIP_MAJOR = ("w_gate", "w_up")
_RELAYOUT = ("w_in", "w_uq", "w_ukv")


def _gathered_to_kernel(cfg, name, wg):
    if name in _CHIP_MAJOR:
        return wg
    if name in _RELAYOUT:
        return _to_kernel_layout(cfg, name, _chips_to_cols(wg))
    return wg.reshape(wg.shape[0] * wg.shape[1], wg.shape[2])


def _grad_to_chips(cfg, name, g):
    if name in _CHIP_MAJOR:
        return g
    if name in _RELAYOUT:
        return _cols_to_chips(_from_kernel_layout(cfg, name, g))
    return g.reshape(N_CHIPS, g.shape[0] // N_CHIPS, g.shape[1])


def _me():
    return lax.axis_index("x"), lax.axis_index("y"), lax.axis_index("c")


def _other_chips(x, y):
    return [(1 - x, y), (x, 1 - y), (1 - x, 1 - y)]


_ANY = pl.BlockSpec(memory_space=pl.ANY)


def _row_block(rows, cols, mult):
    return _pick(rows, max(mult, (1 << 19) // cols // mult * mult), mult)


def _scalar(v):
    return v.astype(I32).reshape(1)


def _stage_shard(name, w, chip):
    rs, cs = w.shape
    tr = _row_block(rs, cs, 16)

    def body(chip_ref, w_ref, o_ref):
        o_ref[...] = w_ref[...].astype(BF16)

    return pl.pallas_call(
        body, name="stage_" + name,
        grid_spec=pltpu.PrefetchScalarGridSpec(
            num_scalar_prefetch=1, grid=(rs // tr,),
            in_specs=[pl.BlockSpec((tr, cs), lambda i, chip_ref: (i, 0))],
            out_specs=pl.BlockSpec((None, tr, cs), lambda i, chip_ref: (chip_ref[0], i, 0))),
        out_shape=jax.ShapeDtypeStruct((N_CHIPS, rs, cs), BF16),
        compiler_params=_params(("parallel",)),
    )(_scalar(chip), w)


def _half(ref, k, half):
    h = ref.shape[1] // 2
    return ref.at[k, pl.ds(pl.multiple_of(half * h, 16), h), :]


def _allgather_weights(bufs):
    n = len(bufs)

    def body(*refs):
        outs, send_sems, recv_sems = refs[n:2 * n], refs[2 * n], refs[2 * n + 1]
        x, y, c = _me()
        chip = 2 * x + y
        sib = (x, y, 1 - c)
        chips = _other_chips(x, y)

        def copy(k, part, to):
            return pltpu.make_async_remote_copy(src_ref=part, dst_ref=part, send_sem=send_sems.at[k], recv_sem=recv_sems.at[k],
                                                device_id=to, device_id_type=MESH_ID)

        started = []
        for w, o_ref in enumerate(outs):
            for j, (cx, cy) in enumerate(chips):
                started.append(copy(6 * w + j, _half(o_ref, chip, c), (cx, cy, c)))
                started[-1].start()
        for w, o_ref in enumerate(outs):
            for j, (cx, cy) in enumerate(chips):
                theirs = _half(o_ref, 2 * cx + cy, c)
                copy(6 * w + j, theirs, sib).wait_recv()
                started.append(copy(6 * w + 3 + j, theirs, sib))
                started[-1].start()
        for w, o_ref in enumerate(outs):
            for j, (cx, cy) in enumerate(chips):
                copy(6 * w + 3 + j, _half(o_ref, 2 * cx + cy, 1 - c), sib).wait_recv()
        for cp in started:
            cp.wait_send()

    return pl.pallas_call(
        body, name="allgather_weights", in_specs=[_ANY] * n, out_specs=[_ANY] * n,
        out_shape=[jax.ShapeDtypeStruct(b.shape, b.dtype) for b in bufs],
        input_output_aliases={i: i for i in range(n)},
        scratch_shapes=[pltpu.SemaphoreType.DMA((6 * n,)), pltpu.SemaphoreType.DMA((6 * n,))],
    )(*bufs)


_HBM = pl.BlockSpec(memory_space=pltpu.HBM)
_SEM = pl.BlockSpec(memory_space=pltpu.SEMAPHORE)
_EFFECT = pltpu.SideEffectType.DATAFLOW_SIDE_EFFECTING


def _split_start(name, bufs, n_copies, copies):
    n = len(bufs)

    def body(*refs):
        for cp in copies(refs[:n], refs[n], refs[n + 1]):
            cp.start()
        refs[-1][...] = jnp.zeros_like(refs[-1])

    res = pl.pallas_call(
        body, name=name,
        out_shape=(pltpu.SemaphoreType.DMA((n_copies,)), pltpu.SemaphoreType.DMA((n_copies,)),
                   *[pltpu.HBM(b.shape, b.dtype) for b in bufs], jax.ShapeDtypeStruct((8, LANE), F32)),
        in_specs=[_HBM] * n, out_specs=(_SEM, _SEM, *[_HBM] * n, pl.BlockSpec(memory_space=pltpu.VMEM)),
        input_output_aliases={i: 2 + i for i in range(n)},
        compiler_params=pltpu.CompilerParams(has_side_effects=_EFFECT),
    )(*[pltpu.with_memory_space_constraint(b, pltpu.HBM) for b in bufs])
    return res[0], res[1], list(res[2:2 + n]), res[-1]


def _split_wait(name, send_sems, recv_sems, bufs, after, copies):
    n = len(bufs)

    def body(*refs):
        for cp in copies(refs[:n], refs[n], refs[n + 1]):
            cp.wait_send()
            cp.wait_recv()

    return list(pl.pallas_call(
        body, name=name, out_shape=[pltpu.HBM(b.shape, b.dtype) for b in bufs],
        in_specs=[_HBM] * n + [_SEM, _SEM, _ANY], out_specs=[_HBM] * n,
        input_output_aliases={i: i for i in range(n)},
        compiler_params=pltpu.CompilerParams(has_side_effects=_EFFECT),
    )(*bufs, send_sems, recv_sems, after))


def _gather_to_chips(bufs, send_sems, recv_sems):
    x, y, c = _me()
    return [pltpu.make_async_remote_copy(src_ref=_half(b, 2 * x + y, c), dst_ref=_half(b, 2 * x + y, c),
                                         send_sem=send_sems.at[3 * w + j], recv_sem=recv_sems.at[3 * w + j],
                                         device_id=(cx, cy, c), device_id_type=MESH_ID)
            for w, b in enumerate(bufs) for j, (cx, cy) in enumerate(_other_chips(x, y))]


def _gather_to_sibling(bufs, send_sems, recv_sems):
    x, y, c = _me()
    return [pltpu.make_async_remote_copy(src_ref=_half(b, 2 * cx + cy, c), dst_ref=_half(b, 2 * cx + cy, c),
                                         send_sem=send_sems.at[3 * w + j], recv_sem=recv_sems.at[3 * w + j],
                                         device_id=(x, y, 1 - c), device_id_type=MESH_ID)
            for w, b in enumerate(bufs) for j, (cx, cy) in enumerate(_other_chips(x, y))]


def _pair_exchange(name, grads):
    n = len(grads)

    def body(*refs):
        ins, outs, send_sems, recv_sems = refs[:n], refs[n:2 * n], refs[2 * n], refs[2 * n + 1]
        x, y, c = _me()
        cps = []
        for w, (g_ref, o_ref) in enumerate(zip(ins, outs)):
            h = o_ref.shape[1]
            src = g_ref.at[:, pl.ds(pl.multiple_of((1 - c) * h, 16), h), :]
            cps.append(pltpu.make_async_remote_copy(src_ref=src, dst_ref=o_ref, send_sem=send_sems.at[w], recv_sem=recv_sems.at[w],
                                                    device_id=(x, y, 1 - c), device_id_type=MESH_ID))
            cps[-1].start()
        for cp in cps:
            cp.wait()

    return pl.pallas_call(
        body, name="pair_exchange_" + name, in_specs=[_ANY] * n, out_specs=[_ANY] * n,
        out_shape=[jax.ShapeDtypeStruct((g.shape[0], g.shape[1] // 2, g.shape[2]), g.dtype) for g in grads],
        scratch_shapes=[pltpu.SemaphoreType.DMA((n,)), pltpu.SemaphoreType.DMA((n,))],
    )(*grads)


def _pair_sum(name, g, theirs, c):
    _, h, cs = theirs.shape
    tr = _row_block(h, cs, 16)
    nb = h // tr

    def body(c_ref, a_ref, b_ref, o_ref):
        o_ref[...] = (a_ref[...].astype(F32) + b_ref[...].astype(F32)).astype(o_ref.dtype)

    return pl.pallas_call(
        body, name="pair_sum_" + name,
        grid_spec=pltpu.PrefetchScalarGridSpec(
            num_scalar_prefetch=1, grid=(N_CHIPS, nb),
            in_specs=[pl.BlockSpec((None, tr, cs), lambda k, i, c_ref: (k, c_ref[0] * nb + i, 0)),
                      pl.BlockSpec((None, tr, cs), lambda k, i, c_ref: (k, i, 0))],
            out_specs=pl.BlockSpec((None, tr, cs), lambda k, i, c_ref: (k, i, 0))),
        out_shape=jax.ShapeDtypeStruct(theirs.shape, BF16),
        compiler_params=_params(("parallel", "parallel")),
    )(_scalar(c), g, theirs)


def _chip_copies(srcs, lands, send_sems, recv_sems):
    x, y, c = _me()
    return [pltpu.make_async_remote_copy(src_ref=s_ref.at[2 * cx + cy], dst_ref=l_ref.at[j], send_sem=send_sems.at[3 * w + j],
                                         recv_sem=recv_sems.at[3 * w + j], device_id=(cx, cy, c), device_id_type=MESH_ID)
            for w, (s_ref, l_ref) in enumerate(zip(srcs, lands)) for j, (cx, cy) in enumerate(_other_chips(x, y))]


def _chip_exchange_start(name, sums):
    n = len(sums)
    lands = [lax.empty((3,) + s.shape[1:], s.dtype) for s in sums]
    send_sems, recv_sems, bufs, token = _split_start(
        "chip_exchange_start_" + name, [*sums, *lands], 3 * n, lambda refs, ss, rs: _chip_copies(refs[:n], refs[n:], ss, rs))
    return send_sems, recv_sems, bufs[:n], bufs[n:], token


def _chip_exchange_wait(name, send_sems, recv_sems, sums, lands, after):
    n = len(sums)
    bufs = _split_wait("chip_exchange_wait_" + name, send_sems, recv_sems, [*sums, *lands], after,
                       lambda refs, ss, rs: _chip_copies(refs[:n], refs[n:], ss, rs))
    return bufs[:n], bufs[n:]


def _chip_exchange(name, sums):
    n = len(sums)

    def body(*refs):
        cps = _chip_copies(refs[:n], refs[n:2 * n], refs[2 * n], refs[2 * n + 1])
        for cp in cps:
            cp.start()
        for cp in cps:
            cp.wait()

    return pl.pallas_call(
        body, name="chip_exchange_" + name, in_specs=[_ANY] * n, out_specs=[_ANY] * n,
        out_shape=[jax.ShapeDtypeStruct((3,) + s.shape[1:], s.dtype) for s in sums],
        scratch_shapes=[pltpu.SemaphoreType.DMA((3 * n,)), pltpu.SemaphoreType.DMA((3 * n,))],
    )(*sums)


def _chip_sum(name, sums, theirs, chip):
    _, h, cs = sums.shape
    tr = _row_block(h, cs, 16)

    def body(chip_ref, s_ref, t_ref, o_ref):
        acc = s_ref[...].astype(F32)
        for k in range(3):
            acc = acc + t_ref[k].astype(F32)
        o_ref[...] = acc

    return pl.pallas_call(
        body, name="chip_sum_" + name,
        grid_spec=pltpu.PrefetchScalarGridSpec(
            num_scalar_prefetch=1, grid=(h // tr,),
            in_specs=[pl.BlockSpec((None, tr, cs), lambda i, chip_ref: (chip_ref[0], i, 0)),
                      pl.BlockSpec((3, tr, cs), lambda i, chip_ref: (0, i, 0))],
            out_specs=pl.BlockSpec((tr, cs), lambda i, chip_ref: (i, 0))),
        out_shape=jax.ShapeDtypeStruct((h, cs), F32),
        compiler_params=_params(("parallel",)),
    )(_scalar(chip), sums, theirs)


def _sibling_exchange(halves):
    n = len(halves)

    def body(*refs):
        ins, outs, send_sems, recv_sems = refs[:n], refs[n:2 * n], refs[2 * n], refs[2 * n + 1]
        x, y, c = _me()
        cps = []
        for w, (h_ref, o_ref) in enumerate(zip(ins, outs)):
            cps.append(pltpu.make_async_remote_copy(src_ref=h_ref, dst_ref=o_ref, send_sem=send_sems.at[w], recv_sem=recv_sems.at[w],
                                                    device_id=(x, y, 1 - c), device_id_type=MESH_ID))
            cps[-1].start()
        for cp in cps:
            cp.wait()

    return pl.pallas_call(
        body, name="grad_sibling_exchange", in_specs=[_ANY] * n, out_specs=[_ANY] * n,
        out_shape=[jax.ShapeDtypeStruct(h.shape, h.dtype) for h in halves],
        scratch_shapes=[pltpu.SemaphoreType.DMA((n,)), pltpu.SemaphoreType.DMA((n,))],
    )(*halves)


def _allreduce_small(name, vec):
    def body(v_ref, o_ref, buf_ref, send_sems, recv_sems):
        x, y, c = _me()
        me = 4 * x + 2 * y + c
        cps = []
        for p in range(1, 8):
            px, py, pc = x ^ (p >> 2), y ^ ((p >> 1) & 1), c ^ (p & 1)
            cps.append(pltpu.make_async_remote_copy(src_ref=v_ref, dst_ref=buf_ref.at[me], send_sem=send_sems.at[p - 1],
                                                    recv_sem=recv_sems.at[p - 1], device_id=(px, py, pc), device_id_type=MESH_ID))
            cps[-1].start()
        buf_ref[me] = v_ref[...]
        for p in range(1, 8):
            theirs = buf_ref.at[me ^ p]
            pltpu.make_async_remote_copy(src_ref=theirs, dst_ref=theirs, send_sem=send_sems.at[p - 1], recv_sem=recv_sems.at[p - 1],
                                         device_id=(x, y, c), device_id_type=MESH_ID).wait_recv()
        for cp in cps:
            cp.wait_send()
        acc = buf_ref[0]
        for k in range(1, 8):
            acc = acc + buf_ref[k]
        o_ref[...] = acc

    vm = pl.BlockSpec(memory_space=pltpu.VMEM)
    return pl.pallas_call(
        body, name=name, in_specs=[vm], out_specs=vm, out_shape=jax.ShapeDtypeStruct(vec.shape, F32),
        scratch_shapes=[pltpu.VMEM((8,) + vec.shape, F32), pltpu.SemaphoreType.DMA((7,)), pltpu.SemaphoreType.DMA((7,))],
    )(vec)


def _adam_math(w, g, m, v):
    m = ADAM_B1 * m + (1.0 - ADAM_B1) * g
    v = ADAM_B2 * v + (1.0 - ADAM_B2) * (g * g)
    m_hat = m / (1.0 - ADAM_B1 ** ADAM_STEP)
    v_hat = v / (1.0 - ADAM_B2 ** ADAM_STEP)
    return -ADAM_LR * (m_hat / (jnp.sqrt(v_hat) + ADAM_EPS) + ADAM_WD * w), m, v


def _adamw(name, w, g, m, v):
    R, C = w.shape
    tr = _row_block(R, C, 8)

    def body(w_ref, g_ref, m_ref, v_ref, d_ref, nm_ref, nv_ref):
        d_ref[...], nm_ref[...], nv_ref[...] = _adam_math(w_ref[...], g_ref[...], m_ref[...], v_ref[...])

    blk = pl.BlockSpec((tr, C), lambda i: (i, 0))
    return pl.pallas_call(
        body, name=name, grid=(R // tr,), in_specs=[blk] * 4, out_specs=[blk] * 3,
        out_shape=[jax.ShapeDtypeStruct((R, C), F32)] * 3, compiler_params=_params(("parallel",)),
    )(w, g, m, v)


def _adamw_halves(name, w, mine, theirs, m, v, c):
    rs, cs = w.shape
    h = rs // 2
    tr = _row_block(h, cs, 8)
    nb = h // tr

    def body(c_ref, w_ref, a_ref, b_ref, m_ref, v_ref, g_ref, d_ref, nm_ref, nv_ref):
        g = jnp.where(pl.program_id(0) == c_ref[0], a_ref[...], b_ref[...])
        g_ref[...] = g
        d_ref[...], nm_ref[...], nv_ref[...] = _adam_math(w_ref[...], g, m_ref[...], v_ref[...])

    full = pl.BlockSpec((tr, cs), lambda s, i, c_ref: (s * nb + i, 0))
    part = pl.BlockSpec((tr, cs), lambda s, i, c_ref: (i, 0))
    return pl.pallas_call(
        body, name=name,
        grid_spec=pltpu.PrefetchScalarGridSpec(num_scalar_prefetch=1, grid=(2, nb), in_specs=[full, part, part, full, full],
                                               out_specs=[full] * 4),
        out_shape=[jax.ShapeDtypeStruct((rs, cs), F32)] * 4, compiler_params=_params(("parallel", "parallel")),
    )(_scalar(c), w, mine, theirs, m, v)


def _pack_small(arrs):
    flat = jnp.concatenate([a.reshape(-1) for a in arrs])
    n = -(-flat.shape[0] // (8 * LANE)) * 8 * LANE
    return jnp.pad(flat, (0, n - flat.shape[0])).reshape(8, n // 8)


def _unpack_small(vec, shapes):
    flat, out, off = vec.reshape(-1), [], 0
    for s in shapes:
        out.append(flat[off:off + s[0] * s[1]].reshape(s))
        off += s[0] * s[1]
    return out


class _FfnWeights:
    def __init__(self, cfg, names, staged):
        self.cfg, self.names, self.k = cfg, names, 3 * len(names)
        self.send, self.recv, self.bufs, self.token = _split_start("gather_chips_start", staged, self.k, _gather_to_chips)

    def pass_on(self, after):
        bufs = _split_wait("gather_chips_wait", self.send, self.recv, self.bufs, after, _gather_to_chips)
        self.send, self.recv, self.bufs, token = _split_start("gather_sibling_start", bufs, self.k, _gather_to_sibling)
        return token

    def arrived(self, after):
        bufs = _split_wait("gather_sibling_wait", self.send, self.recv, self.bufs, after, _gather_to_sibling)
        return {n: _gathered_to_kernel(self.cfg, n, b) for n, b in zip(self.names, bufs)}


def _step(cfg, a):
    chip = 2 * lax.axis_index("x") + lax.axis_index("y")
    core = lax.axis_index("c")
    big = [n for n, _, _, _ in cfg.BIG]

    ffn = ("w_gate", "w_up", "w_down")
    first = [n for n in big if n not in ffn]
    staged = {n: _stage_shard(n, a[n], chip) for n in big}
    W = {n: _gathered_to_kernel(cfg, n, wg) for n, wg in zip(first, _allgather_weights([staged[n] for n in first]))}
    ffn_weights = _FfnWeights(cfg, ffn, [staged[n] for n in ffn])

    sp = {n: a[n] for n in SMALL}
    sharded = _pack_small([a[n] for n in SMALL_SHARDED])
    slot = jnp.where(lax.broadcasted_iota(I32, (N_CHIPS,) + sharded.shape, 0) == chip, 0.5 * sharded[None], 0.0)
    allp = _allreduce_small("allgather_small", slot.reshape(N_CHIPS * 8, -1)).reshape((N_CHIPS,) + sharded.shape)
    per_chip = [_unpack_small(allp[ch], [a[n].shape for n in SMALL_SHARDED]) for ch in range(N_CHIPS)]
    for k, n in enumerate(SMALL_SHARDED):
        sp[n] = jnp.concatenate([per_chip[ch][k] for ch in range(N_CHIPS)], axis=1)
    sp["mix_pre_g"] = sp["mix_pre_g"] + ffn_weights.token[0, 0]

    def pair_sums(tag, names, gW):
        grads = [_grad_to_chips(cfg, n, gW[n]) for n in names]
        return [_pair_sum(n, g, t, core) for n, g, t in zip(names, grads, _pair_exchange(tag, grads))]

    early = ("w_down", "w_gate", "w_up", "w_out")
    late = [n for n in big if n not in early]
    started = []

    def ffn_grads_ready(g_early):
        started.extend(_chip_exchange_start("ffn", pair_sums("ffn", early, g_early)))
        return started[-1]

    loss, grad_x, gW, gs = _local_grads(cfg, a["x"], a["loss_target"], W, sp, ffn_weights, ffn_grads_ready)
    sums = dict(zip(late, pair_sums("rest", late, gW)))
    landed = dict(zip(late, _chip_exchange("rest", [sums[n] for n in late])))
    e_sums, e_landed = _chip_exchange_wait("ffn", *started[:4], landed[late[0]])
    sums.update(zip(early, e_sums))
    landed.update(zip(early, e_landed))
    mine = [_chip_sum(n, sums[n], landed[n], chip) for n in big]
    theirs = _sibling_exchange(mine)

    shapes = [gs[n].shape for n in SMALL] + [(1, LANE)]
    red = _unpack_small(_allreduce_small("allreduce_small", _pack_small([gs[n] for n in SMALL] + [loss])), shapes)
    g_small = dict(zip(SMALL, red[:-1]))
    for n in SMALL_SHARDED:
        cs = a[n].shape[1]
        g_small[n] = lax.dynamic_slice_in_dim(g_small[n], chip * cs, cs, axis=1)

    out = {"loss": red[-1][0, 0], "grad_x": grad_x}
    for n, gm, gt in zip(big, mine, theirs):
        out["grad_" + n], out["delta_" + n], out["new_m_" + n], out["new_v_" + n] = _adamw_halves(
            "adamw_" + n, a[n], gm, gt, a["m_" + n], a["v_" + n], core)
    sshapes = [a[n].shape for n in SMALL]
    d, nm, nv = _adamw("adamw_small", _pack_small([a[n] for n in SMALL]), _pack_small([g_small[n] for n in SMALL]),
                       _pack_small([a["m_" + n] for n in SMALL]), _pack_small([a["v_" + n] for n in SMALL]))
    for n, dd, mm, vv in zip(SMALL, _unpack_small(d, sshapes), _unpack_small(nm, sshapes), _unpack_small(nv, sshapes)):
        out["grad_" + n], out["delta_" + n], out["new_m_" + n], out["new_v_" + n] = g_small[n], dd, mm, vv
    return out


def kernel(x, mix_pre_g, w_in, q_norm_g, w_uq, kv_norm_g, w_ukv, ssm_conv_w, ssm_conv_b, dt_bias, a_log, d_skip, ssm_norm_g, w_out, mix_post_g, ffn_pre_g, w_gate, w_up, ffn_conv_w, ffn_conv_b, w_down, ffn_post_g, loss_target, m_mix_pre_g, m_w_in, m_q_norm_g, m_w_uq, m_kv_norm_g, m_w_ukv, m_ssm_conv_w, m_ssm_conv_b, m_dt_bias, m_a_log, m_d_skip, m_ssm_norm_g, m_w_out, m_mix_post_g, m_ffn_pre_g, m_w_gate, m_w_up, m_ffn_conv_w, m_ffn_conv_b, m_w_down, m_ffn_post_g, v_mix_pre_g, v_w_in, v_q_norm_g, v_w_uq, v_kv_norm_g, v_w_ukv, v_ssm_conv_w, v_ssm_conv_b, v_dt_bias, v_a_log, v_d_skip, v_ssm_norm_g, v_w_out, v_mix_post_g, v_ffn_pre_g, v_w_gate, v_w_up, v_ffn_conv_w, v_ffn_conv_b, v_w_down, v_ffn_post_g):
    args = dict(locals())
    out = _step(_FULL, {k: (v[0] if v.ndim == 3 else v) for k, v in args.items()})
    res = [out["loss"], out["grad_x"][None]]
    for pre in ("grad_", "delta_", "new_m_", "new_v_"):
        res += [out[pre + n][None] if args[n].ndim == 3 else out[pre + n] for n in WEIGHTS]
    return tuple(res)
```

```python
import functools
import math

import jax
import jax.numpy as jnp
from jax import lax
from jax.experimental import pallas as pl
from jax.experimental.pallas import tpu as pltpu

F32, BF16, I32 = jnp.float32, jnp.bfloat16, jnp.int32
NN = (((1,), (0,)), ((), ()))
NT = (((1,), (1,)), ((), ()))
TN = (((0,), (0,)), ((), ()))
HI = lax.Precision.HIGHEST
MESH_ID = pl.DeviceIdType.MESH

EPS = 1e-6
CHUNK = 64
NOPE, ROPE, VH = 128, 64, 128
ROPE_THETA = 10000.0
HP, NST = 64, 128
SSM_K, FFN_K = 4, 3
LANE = 128
N_CHIPS = 4
VMEM_LIMIT = 52 * 1024 * 1024
MM_TILE, MM_TILE_K = 1408, 2816

ADAM_LR, ADAM_B1, ADAM_B2, ADAM_EPS, ADAM_WD, ADAM_STEP = 0.001, 0.9, 0.999, 1e-08, 0.01, 10


class _Cfg:
    def __init__(self, S, D, QL, KVL, H, HS, G, DFF, T):
        self.S, self.D, self.QL, self.KVL, self.H, self.HS, self.G, self.DFF, self.T = S, D, QL, KVL, H, HS, G, DFF, T
        self.INNER = HS * HP
        self.CONVCH = self.INNER + 2 * G * NST
        self.QW = H * (NOPE + ROPE)
        self.KVW = H * (NOPE + VH)
        self.MLAW = H * VH
        self.MIXW = self.MLAW + self.INNER
        self.IN_COLS = QL + KVL + ROPE + self.INNER + self.CONVCH + HS
        self.o_kr = QL + KVL
        self.o_z = self.o_kr + LANE
        self.o_xbc = self.o_z + self.INNER
        self.o_dt = self.o_xbc + self.CONVCH
        self.EXT = self.o_dt + LANE
        self.NPAIR = HS // 2
        self.REP = HS // G
        self.BIG = (("w_in", D, self.IN_COLS, 1), ("w_uq", QL, self.QW, 1), ("w_ukv", KVL, self.KVW, 1),
                    ("w_out", self.MIXW, D, 0), ("w_gate", D, DFF, 1), ("w_up", D, DFF, 1), ("w_down", DFF, D, 0))
        self.NSHARD = sum(r * c for _, r, c, _ in self.BIG) // N_CHIPS
        unit = 2 * LANE * 16
        self.NPACK = -(-self.NSHARD // unit) * unit
        self.R = self.NPACK // (2 * LANE)


_FULL = _Cfg(S=2048, D=2048, QL=768, KVL=512, H=8, HS=16, G=2, DFF=5632, T=256)

SMALL = ("mix_pre_g", "q_norm_g", "kv_norm_g", "ssm_conv_w", "ssm_conv_b", "dt_bias", "a_log", "d_skip", "ssm_norm_g",
         "mix_post_g", "ffn_pre_g", "ffn_conv_w", "ffn_conv_b", "ffn_post_g")
SMALL_SHARDED = ("ssm_conv_w", "ffn_conv_w")
WEIGHTS = ("mix_pre_g", "w_in", "q_norm_g", "w_uq", "kv_norm_g", "w_ukv", "ssm_conv_w", "ssm_conv_b", "dt_bias", "a_log",
           "d_skip", "ssm_norm_g", "w_out", "mix_post_g", "ffn_pre_g", "w_gate", "w_up", "ffn_conv_w", "ffn_conv_b",
           "w_down", "ffn_post_g")


def _pick(n, target, mult):
    best = None
    for d in range(mult, min(n, target) + 1, mult):
        if n % d == 0:
            best = d
    return best if best is not None else n


def _params(sem=None):
    kw = dict(vmem_limit_bytes=VMEM_LIMIT)
    if sem is not None:
        kw["dimension_semantics"] = sem
    return pltpu.CompilerParams(**kw)


def _dot(a, b, dims=NN, precision=None):
    return lax.dot_general(a, b, dims, preferred_element_type=F32, precision=precision)


def _sigmoid(x):
    return 1.0 / (1.0 + jnp.exp(-x))


def _rs(x):
    return lax.rsqrt(jnp.mean(x * x, axis=-1, keepdims=True) + EPS)


def _rms_back(xh, r, dn):
    return r * (dn - xh * jnp.mean(dn * xh, axis=-1, keepdims=True))


def _colsum(v):
    return jnp.sum(v, axis=0, keepdims=True)


def _matmul(name, a, b, mode, out_dtype, a2=None, b2=None, chips=False):
    cs = None
    if mode == "nn":
        (M, K), N = a.shape, b.shape[-1]
        if chips:
            cs, N = N, N_CHIPS * N
    elif mode == "nt":
        (M, K), N = a.shape, b.shape[-2]
        if chips:
            cs = b.shape[-1]
    else:
        (K, M), N = a.shape, b.shape[1]
        if chips:
            cs = N // N_CHIPS
    tm = _pick(M, MM_TILE, LANE)
    tn = _pick(cs if chips and mode != "nt" else N, MM_TILE, LANE)
    tk = _pick(cs, MM_TILE, LANE) if chips and mode == "nt" else _pick(K, MM_TILE_K, LANE)
    nk = K // tk
    dims = {"nn": NN, "nt": NT, "tn": TN}[mode]
    a_spec = pl.BlockSpec((tk, tm), lambda i, j, k: (k, i)) if mode == "tn" else pl.BlockSpec((tm, tk), lambda i, j, k: (i, k))
    b_spec = pl.BlockSpec((tn, tk), lambda i, j, k: (j, k)) if mode == "nt" else pl.BlockSpec((tk, tn), lambda i, j, k: (k, j))
    o_spec = pl.BlockSpec((tm, tn), lambda i, j, k: (i, j))
    o_shape = (M, N)
    if chips and mode == "nn":
        per = cs // tn
        b_spec = pl.BlockSpec((None, tk, tn), lambda i, j, k: (j // per, k, j % per))
    elif chips and mode == "nt":
        per = cs // tk
        b_spec = pl.BlockSpec((None, tn, tk), lambda i, j, k: (k // per, j, k % per))
    elif chips:
        per = cs // tn
        o_spec = pl.BlockSpec((None, tm, tn), lambda i, j, k: (j // per, i, j % per))
        o_shape = (N_CHIPS, M, cs)
    two = a2 is not None

    def product(refs):
        part = _dot(refs[0][...].astype(BF16), refs[1][...].astype(BF16), dims)
        if two:
            part += _dot(refs[2][...].astype(BF16), refs[3][...].astype(BF16), dims)
        return part

    def body_whole_k(*refs):
        refs[-1][...] = product(refs).astype(refs[-1].dtype)

    def body(*refs):
        o_ref, acc_ref = refs[-2], refs[-1]
        k = pl.program_id(2)

        @pl.when(k == 0)
        def _():
            acc_ref[...] = product(refs)

        @pl.when(k > 0)
        def _():
            acc_ref[...] += product(refs)

        @pl.when(k == nk - 1)
        def _():
            o_ref[...] = acc_ref[...].astype(o_ref.dtype)

    ins = (a, b, a2, b2) if two else (a, b)
    return pl.pallas_call(
        body_whole_k if nk == 1 else body, name=name, grid=(M // tm, N // tn, nk),
        in_specs=[a_spec, b_spec] * (2 if two else 1),
        out_specs=o_spec,
        out_shape=jax.ShapeDtypeStruct(o_shape, out_dtype),
        scratch_shapes=[] if nk == 1 else [pltpu.VMEM((tm, tn), F32)],
        compiler_params=_params(("parallel", "parallel", "arbitrary")),
    )(*ins)


def _rowwise(name, fn, rows, mats, outs, reds, ts):
    S = rows[0].shape[0]
    nr, nm, no = len(rows), len(mats), len(outs)

    def body(*refs):
        res = fn(*[r[...] for r in refs[:nr + nm]])
        res = res if isinstance(res, (tuple, list)) else (res,)
        for r, v in zip(refs[nr + nm:nr + nm + no], res[:no]):
            r[...] = v.astype(r.dtype)
        first = pl.program_id(0) == 0
        for r, v in zip(refs[nr + nm + no:], res[no:]):
            @pl.when(first)
            def _():
                r[...] = jnp.broadcast_to(v, r.shape)

            @pl.when(jnp.logical_not(first))
            def _():
                r[...] += jnp.broadcast_to(v, r.shape)

    in_specs = [pl.BlockSpec((ts, a.shape[1]), lambda i: (i, 0)) for a in rows]
    in_specs += [pl.BlockSpec(m.shape, lambda i, nd=m.ndim: (0,) * nd) for m in mats]
    out_specs = [pl.BlockSpec((ts, w), lambda i: (i, 0)) for w, _ in outs]
    out_specs += [pl.BlockSpec(s, lambda i: (0, 0)) for s in reds]
    out_shape = [jax.ShapeDtypeStruct((S, w), dt) for w, dt in outs] + [jax.ShapeDtypeStruct(s, F32) for s in reds]
    return pl.pallas_call(
        body, name=name, grid=(S // ts,), in_specs=in_specs, out_specs=out_specs, out_shape=out_shape,
        compiler_params=_params(("arbitrary",) if reds else ("parallel",)),
    )(*rows, *mats)


def _shift_down(v, s):
    if s == 0:
        return v
    rows = lax.broadcasted_iota(I32, v.shape, 0)
    return jnp.where(rows >= s, pltpu.roll(v, s, 0), 0.0)


def _shift_up(v, s):
    if s == 0:
        return v
    n = v.shape[0]
    rows = lax.broadcasted_iota(I32, v.shape, 0)
    return jnp.where(rows < n - s, pltpu.roll(v, n - s, 0), 0.0)


def _conv(x, w, b):
    K = w.shape[0]
    y = jnp.broadcast_to(b, x.shape)
    for k in range(K):
        y = y + w[k:k + 1, :] * _shift_down(x, K - 1 - k)
    return y


def _conv_back(x, w, dc):
    K = w.shape[0]
    dx = jnp.zeros_like(x)
    dw = []
    for k in range(K):
        dx = dx + w[k:k + 1, :] * _shift_up(dc, K - 1 - k)
        dw.append(_colsum(dc * _shift_down(x, K - 1 - k)))
    return dx, jnp.concatenate(dw, axis=0), _colsum(dc)


def _colwise(name, fn, cols, vecs, outs, pouts, tc):
    S, C = cols[0].shape
    nc_, nv, no = len(cols), len(vecs), len(outs)

    def body(*refs):
        res = fn(*[r[...] for r in refs[:nc_ + nv]])
        res = res if isinstance(res, (tuple, list)) else (res,)
        for r, v in zip(refs[nc_ + nv:], res):
            r[...] = v.astype(r.dtype)

    in_specs = [pl.BlockSpec((S, tc), lambda j: (0, j)) for _ in cols]
    in_specs += [pl.BlockSpec((v.shape[0], tc), lambda j: (0, j)) for v in vecs]
    out_specs = [pl.BlockSpec((S, tc), lambda j: (0, j)) for _ in outs] + [pl.BlockSpec((k, tc), lambda j: (0, j)) for k in pouts]
    out_shape = [jax.ShapeDtypeStruct((S, C), dt) for dt in outs] + [jax.ShapeDtypeStruct((k, C), F32) for k in pouts]
    return pl.pallas_call(
        body, name=name, grid=(C // tc,), in_specs=in_specs, out_specs=out_specs, out_shape=out_shape,
        compiler_params=_params(("parallel",)),
    )(*cols, *vecs)


_G0, _G1 = math.sqrt(2.0 / math.pi), 0.044715


def _gelu(g):
    th = jnp.tanh(_G0 * (g + _G1 * g * g * g))
    return 0.5 * g * (1.0 + th), th


def _ffn_act(gate_pre, up, w, b):
    act, _ = _gelu(_conv(gate_pre, w, b))
    return act * up


def _ffn_act_back(dact, gate_pre, up, w, b):
    g = _conv(gate_pre, w, b)
    ge, th = _gelu(g)
    dge = 0.5 * (1.0 + th) + 0.5 * g * (1.0 - th * th) * _G0 * (1.0 + 3.0 * _G1 * g * g)
    dup = dact * ge
    dgate_pre, dw, db = _conv_back(gate_pre, w, dact * up * dge)
    return dgate_pre, dup, dw, db


def _ssm_act(xbc, w, b):
    c = _conv(xbc, w, b)
    return c * _sigmoid(c)


def _ssm_act_back(dxc, xbc, w, b):
    c = _conv(xbc, w, b)
    sg = _sigmoid(c)
    return _conv_back(xbc, w, dxc * sg * (1.0 + c * (1.0 - sg)))


def _rope_tables(S):
    inv = 1.0 / (ROPE_THETA ** (jnp.arange(0, ROPE, 2, dtype=F32) / ROPE))
    ang = jnp.arange(S, dtype=F32)[:, None] * inv[None, :]
    cos, sin = jnp.cos(ang), jnp.sin(ang)
    return jnp.tile(cos, (1, 4)), jnp.tile(jnp.concatenate([-sin, sin], axis=1), (1, 2))


def _swap_halves(x):
    lane = lax.broadcasted_iota(I32, x.shape, 1)
    w = x.shape[1]
    return jnp.where((lane % ROPE) < ROPE // 2, pltpu.roll(x, w - ROPE // 2, 1), pltpu.roll(x, ROPE // 2, 1))


def _rot(x, cos2, sin2):
    return x * cos2 + _swap_halves(x) * sin2


def _rot_back(dy, cos2, sin2):
    return dy * cos2 + _swap_halves(dy * sin2)


def _mla_pack(cfg, q, kv, kr, cos2, sin2):
    S, H = cfg.S, cfg.H
    ts = _pick(S, 512, 8)

    def body(qn_ref, qr_ref, kn_ref, v_ref, kr_ref, c_ref, s_ref, Q_ref, K_ref, V_ref):
        h = pl.program_id(0)
        c2, s2 = c_ref[...], s_ref[...]
        Q_ref[0, :, 0:LANE] = qn_ref[...].astype(BF16)
        Q_ref[0, :, LANE:] = _rot(qr_ref[...], c2, s2).astype(BF16)
        K_ref[0, :, 0:LANE] = kn_ref[...].astype(BF16)
        krr = _rot(kr_ref[...], c2, s2)
        K_ref[0, :, LANE:] = jnp.where(h % 2 == 1, pltpu.roll(krr, ROPE, 1), krr).astype(BF16)
        V_ref[0] = v_ref[...].astype(BF16)

    blk = lambda f: pl.BlockSpec((ts, LANE), f)
    return pl.pallas_call(
        body, name="mla_pack", grid=(H, S // ts),
        in_specs=[blk(lambda h, i: (i, h)), blk(lambda h, i: (i, H + h // 2)), blk(lambda h, i: (i, h)),
                  blk(lambda h, i: (i, H + h)), blk(lambda h, i: (i, 0)), blk(lambda h, i: (i, 0)), blk(lambda h, i: (i, 0))],
        out_specs=[pl.BlockSpec((1, ts, 2 * LANE), lambda h, i: (h, i, 0)), pl.BlockSpec((1, ts, 2 * LANE), lambda h, i: (h, i, 0)),
                   pl.BlockSpec((1, ts, LANE), lambda h, i: (h, i, 0))],
        out_shape=[jax.ShapeDtypeStruct((H, S, 2 * LANE), BF16), jax.ShapeDtypeStruct((H, S, 2 * LANE), BF16),
                   jax.ShapeDtypeStruct((H, S, LANE), BF16)],
        compiler_params=_params(("parallel", "parallel")),
    )(q, q, kv, kv, kr, cos2, sin2)


def _mla_unpack(cfg, dQ, dK, dV, cos2, sin2):
    S, H = cfg.S, cfg.H
    ts = _pick(S, 256, 8)

    def body(dQ_ref, dK_ref, dV_ref, c_ref, s_ref, dq_ref, dkv_ref, dkr_ref):
        c2, s2 = c_ref[...], s_ref[...]
        lo = lax.broadcasted_iota(I32, (ts, LANE), 1) < ROPE
        tk = jnp.zeros((ts, LANE), F32)
        for h in range(H):
            dq_ref[:, h * LANE:(h + 1) * LANE] = dQ_ref[h, :, 0:LANE].astype(BF16)
            dkv_ref[:, h * LANE:(h + 1) * LANE] = dK_ref[h, :, 0:LANE].astype(BF16)
            dkv_ref[:, (H + h) * LANE:(H + h + 1) * LANE] = dV_ref[h].astype(BF16)
            own = lo if h % 2 == 0 else jnp.logical_not(lo)
            tk = tk + jnp.where(own, dK_ref[h, :, LANE:], 0.0)
        for j in range(H // 2):
            dr = dQ_ref[2 * j, :, LANE:] + dQ_ref[2 * j + 1, :, LANE:]
            dq_ref[:, (H + j) * LANE:(H + j + 1) * LANE] = _rot_back(dr, c2, s2).astype(BF16)
        dkr_rot = jnp.where(lo, tk + pltpu.roll(tk, ROPE, 1), 0.0)
        dkr_ref[...] = _rot_back(dkr_rot, c2, s2).astype(BF16)

    tab = pl.BlockSpec((ts, LANE), lambda i: (i, 0))
    return pl.pallas_call(
        body, name="mla_unpack", grid=(S // ts,),
        in_specs=[pl.BlockSpec((H, ts, 2 * LANE), lambda i: (0, i, 0)), pl.BlockSpec((H, ts, 2 * LANE), lambda i: (0, i, 0)),
                  pl.BlockSpec((H, ts, LANE), lambda i: (0, i, 0)), tab, tab],
        out_specs=[pl.BlockSpec((ts, cfg.QW), lambda i: (i, 0)), pl.BlockSpec((ts, cfg.KVW), lambda i: (i, 0)), tab],
        out_shape=[jax.ShapeDtypeStruct((S, cfg.QW), BF16), jax.ShapeDtypeStruct((S, cfg.KVW), BF16),
                   jax.ShapeDtypeStruct((S, LANE), BF16)],
        compiler_params=_params(("parallel",)),
    )(dQ, dK, dV, cos2, sin2)


_ATT_T = 256
_ATT_SCALE = (NOPE + ROPE) ** -0.5


def _diag_mask(transposed=False):
    r = lax.broadcasted_iota(I32, (_ATT_T, _ATT_T), 0) // CHUNK
    c = lax.broadcasted_iota(I32, (_ATT_T, _ATT_T), 1) // CHUNK
    return r <= c if transposed else c <= r


def _row_form(col):
    return jnp.broadcast_to(col, (col.shape[0], LANE)).T[0:8, :]


def _attn_fwd(cfg, Q, K, V):
    S, H, T = cfg.S, cfg.H, _ATT_T

    def body(q_ref, k_ref, v_ref, o_ref, lse_ref, lse_t_ref):
        qi = pl.program_id(1)
        q = q_ref[0]

        def step(kb, carry, mask=None):
            m, l, acc = carry
            ks = pl.multiple_of(kb * T, T)
            s = _dot(q, k_ref[0, pl.ds(ks, T), :], NT) * _ATT_SCALE
            if mask is not None:
                s = jnp.where(mask, s, -1e30)
            m_new = jnp.maximum(m, jnp.max(s, axis=1, keepdims=True))
            p = jnp.exp(s - m_new)
            alpha = jnp.exp(m - m_new)
            l = alpha * l + jnp.sum(p, axis=1, keepdims=True)
            acc = alpha * acc + _dot(p.astype(BF16), v_ref[0, pl.ds(ks, T), :])
            return m_new, l, acc

        init = (jnp.full((T, 1), -1e30, F32), jnp.zeros((T, 1), F32), jnp.zeros((T, VH), F32))
        m, l, acc = step(qi, lax.fori_loop(0, qi, step, init), _diag_mask())
        o_ref[...] = acc / l
        lse = m + jnp.log(l)
        lse_ref[...] = jnp.broadcast_to(lse, (T, LANE))
        lse_t_ref[0] = _row_form(lse)

    return pl.pallas_call(
        body, name="attn_fwd", grid=(H, S // T),
        in_specs=[pl.BlockSpec((1, T, 2 * LANE), lambda h, i: (h, i, 0)), pl.BlockSpec((1, S, 2 * LANE), lambda h, i: (h, 0, 0)),
                  pl.BlockSpec((1, S, LANE), lambda h, i: (h, 0, 0))],
        out_specs=[pl.BlockSpec((T, LANE), lambda h, i: (i, h)), pl.BlockSpec((T, LANE), lambda h, i: (i, h)),
                   pl.BlockSpec((1, 8, T), lambda h, i: (h, 0, i))],
        out_shape=[jax.ShapeDtypeStruct((S, H * LANE), F32), jax.ShapeDtypeStruct((S, H * LANE), F32),
                   jax.ShapeDtypeStruct((H, 8, S), F32)],
        compiler_params=_params(("parallel", "parallel")),
    )(Q, K, V)


def _attn_dq(cfg, Q, K, V, do, o, lse):
    S, H, T = cfg.S, cfg.H, _ATT_T

    def body(q_ref, k_ref, v_ref, do_ref, o_ref, lse_ref, dq_ref, dl_t_ref):
        qi = pl.program_id(1)
        q = q_ref[0]
        do = do_ref[...]
        delta = jnp.sum(do * o_ref[...], axis=1, keepdims=True)
        lse = lse_ref[:, 0:1]
        dob = do.astype(BF16)

        def step(kb, dq, mask=None):
            ks = pl.multiple_of(kb * T, T)
            k = k_ref[0, pl.ds(ks, T), :]
            s = _dot(q, k, NT) * _ATT_SCALE
            if mask is not None:
                s = jnp.where(mask, s, -1e30)
            p = jnp.exp(s - lse)
            dp = _dot(dob, v_ref[0, pl.ds(ks, T), :], NT)
            ds = p * (dp - delta) * _ATT_SCALE
            return dq + _dot(ds.astype(BF16), k)

        dq_ref[0] = step(qi, lax.fori_loop(0, qi, step, jnp.zeros((T, 2 * LANE), F32)), _diag_mask())
        dl_t_ref[0] = _row_form(delta)

    col = pl.BlockSpec((T, LANE), lambda h, i: (i, h))
    return pl.pallas_call(
        body, name="attn_dq", grid=(H, S // T),
        in_specs=[pl.BlockSpec((1, T, 2 * LANE), lambda h, i: (h, i, 0)), pl.BlockSpec((1, S, 2 * LANE), lambda h, i: (h, 0, 0)),
                  pl.BlockSpec((1, S, LANE), lambda h, i: (h, 0, 0)), col, col, col],
        out_specs=[pl.BlockSpec((1, T, 2 * LANE), lambda h, i: (h, i, 0)), pl.BlockSpec((1, 8, T), lambda h, i: (h, 0, i))],
        out_shape=[jax.ShapeDtypeStruct((H, S, 2 * LANE), F32), jax.ShapeDtypeStruct((H, 8, S), F32)],
        compiler_params=_params(("parallel", "parallel")),
    )(Q, K, V, do, o, lse)


def _attn_dkv(cfg, Q, K, V, do, lse_t, delta_t):
    S, H, T = cfg.S, cfg.H, _ATT_T
    nq = S // T

    def body(q_ref, k_ref, v_ref, do_ref, lse_ref, dl_ref, dk_ref, dv_ref):
        kb = pl.program_id(1)
        k, v = k_ref[0], v_ref[0]

        def step(qi, carry, mask=None):
            dk, dv = carry
            qs = pl.multiple_of(qi * T, T)
            q = q_ref[0, pl.ds(qs, T), :]
            dob = do_ref[pl.ds(qs, T), :].astype(BF16)
            s = _dot(k, q, NT) * _ATT_SCALE
            if mask is not None:
                s = jnp.where(mask, s, -1e30)
            p = jnp.exp(s - lse_ref[0, 0:1, pl.ds(qs, T)])
            dv = dv + _dot(p.astype(BF16), dob)
            dp = _dot(v, dob, NT)
            ds = p * (dp - dl_ref[0, 0:1, pl.ds(qs, T)]) * _ATT_SCALE
            dk = dk + _dot(ds.astype(BF16), q)
            return dk, dv

        first = step(kb, (jnp.zeros((T, 2 * LANE), F32), jnp.zeros((T, VH), F32)), _diag_mask(transposed=True))
        dk, dv = lax.fori_loop(kb + 1, nq, step, first)
        dk_ref[0] = dk
        dv_ref[0] = dv

    row = pl.BlockSpec((1, 8, S), lambda h, j: (h, 0, 0))
    return pl.pallas_call(
        body, name="attn_dkv", grid=(H, S // T),
        in_specs=[pl.BlockSpec((1, S, 2 * LANE), lambda h, j: (h, 0, 0)), pl.BlockSpec((1, T, 2 * LANE), lambda h, j: (h, j, 0)),
                  pl.BlockSpec((1, T, LANE), lambda h, j: (h, j, 0)), pl.BlockSpec((S, LANE), lambda h, j: (0, h)), row, row],
        out_specs=[pl.BlockSpec((1, T, 2 * LANE), lambda h, j: (h, j, 0)), pl.BlockSpec((1, T, LANE), lambda h, j: (h, j, 0))],
        out_shape=[jax.ShapeDtypeStruct((H, S, 2 * LANE), F32), jax.ShapeDtypeStruct((H, S, LANE), F32)],
        compiler_params=_params(("parallel", "parallel")),
    )(Q, K, V, do, lse_t, delta_t)


def _expand_matrix(cfg):
    r = lax.broadcasted_iota(I32, (LANE, cfg.INNER), 0)
    c = lax.broadcasted_iota(I32, (LANE, cfg.INNER), 1)
    return (r == c // HP).astype(F32)


def _softplus(x):
    return jnp.maximum(x, 0.0) + jnp.log(1.0 + jnp.exp(-jnp.abs(x)))


def _ssd_prep(cfg, dt_raw, dt_bias_pad, a_log_pad, expand):
    HS = cfg.HS

    def fn(raw, bias, alog, E):
        heads = lax.broadcasted_iota(I32, raw.shape, 1) < HS
        dt = jnp.where(heads, _softplus(raw + bias), 0.0)
        a = dt * jnp.where(heads[0:1], -jnp.exp(alog), 0.0)
        return dt, a, _dot(dt, E, precision=HI), _dot(a, E, precision=HI)

    return _rowwise("ssd_prep", fn, [dt_raw], [dt_bias_pad, a_log_pad, expand],
                    [(LANE, F32), (LANE, F32), (cfg.INNER, F32), (cfg.INNER, F32)], [], _pick(cfg.S, 512, 8))


def _tril(T):
    return lax.broadcasted_iota(I32, (T, T), 0) >= lax.broadcasted_iota(I32, (T, T), 1)


def _ssd_fwd(cfg, xc, dt_exp, a_exp, a_small, dskip_exp):
    S, T, INNER, G, NPAIR = cfg.S, cfg.T, cfg.INNER, cfg.G, cfg.NPAIR
    NC = S // T

    def body(xc_ref, dte_ref, ae_ref, as_ref, dsk_ref, y_ref, hin_ref, ht_ref):
        @pl.when(pl.program_id(0) == 0)
        def _():
            ht_ref[...] = jnp.zeros_like(ht_ref)

        tril = _tril(T)
        tri = tril.astype(F32)
        acs_s = _dot(tri, as_ref[...], precision=HI)
        acs_e = _dot(tri, ae_ref[...], precision=HI)
        acs_t = acs_s.T
        lo = lax.broadcasted_iota(I32, (T, LANE), 1) < HP
        for g in range(G):
            Bb = xc_ref[:, INNER + g * NST:INNER + (g + 1) * NST].astype(BF16)
            Cb = xc_ref[:, INNER + (G + g) * NST:INNER + (G + g + 1) * NST].astype(BF16)
            Gm = _dot(Cb, Bb, NT)
            for j in range(g * NPAIR // G, (g + 1) * NPAIR // G):
                sl = slice(j * LANE, (j + 1) * LANE)
                Xp = xc_ref[:, sl]
                Xdt = Xp * dte_ref[:, sl]
                Xb = Xdt.astype(BF16)
                acs_p = acs_e[:, sl]
                last = acs_p[T - 1:T, :]
                Hin = ht_ref[j]
                hin_ref[0, j] = Hin
                yd = []
                for e in (0, 1):
                    h = 2 * j + e
                    Lm = jnp.exp(jnp.where(tril, acs_s[:, h:h + 1] - acs_t[h:h + 1, :], -1e30))
                    yd.append(_dot((Gm * Lm).astype(BF16), Xb))
                y_off = _dot(Cb, Hin.astype(BF16)) * jnp.exp(acs_p)
                y_ref[:, sl] = jnp.where(lo, yd[0], yd[1]) + y_off + Xp * dsk_ref[:, sl]
                st = _dot(Bb, (Xdt * jnp.exp(last - acs_p)).astype(BF16), TN)
                ht_ref[j] = jnp.exp(last) * Hin + st

    rows = lambda w: pl.BlockSpec((T, w), lambda c: (c, 0))
    return pl.pallas_call(
        body, name="ssd_fwd", grid=(NC,),
        in_specs=[rows(cfg.CONVCH), rows(INNER), rows(INNER), rows(LANE), pl.BlockSpec((1, INNER), lambda c: (0, 0))],
        out_specs=[rows(INNER), pl.BlockSpec((1, NPAIR, NST, LANE), lambda c: (c, 0, 0, 0))],
        out_shape=[jax.ShapeDtypeStruct((S, INNER), F32), jax.ShapeDtypeStruct((NC, NPAIR, NST, LANE), F32)],
        scratch_shapes=[pltpu.VMEM((NPAIR, NST, LANE), F32)],
        compiler_params=_params(("arbitrary",)),
    )(xc, dt_exp, a_exp, a_small, dskip_exp)


def _ssd_bwd(cfg, dy, xc, dt_exp, a_exp, a_small, dskip_exp, hin, dt_raw, dt_bias_pad, a_log_pad, expand):
    S, T, INNER, G, NPAIR, HS = cfg.S, cfg.T, cfg.INNER, cfg.G, cfg.NPAIR, cfg.HS
    NC = S // T

    def body(dy_ref, xc_ref, dte_ref, ae_ref, as_ref, dsk_ref, hin_ref, raw_ref, bias_ref, alog_ref, e_ref,
             dxc_ref, draw_ref, dbias_ref, dalog_ref, dskip_ref, dht_ref, cols_ref, rows_ref, dacs_ref, ddt_ref):
        first = pl.program_id(0) == 0

        @pl.when(first)
        def _():
            dht_ref[...] = jnp.zeros_like(dht_ref)

        tril = _tril(T)
        tri = tril.astype(F32)
        a_s = as_ref[...]
        acs_s = _dot(tri, a_s, precision=HI)
        acs_e = _dot(tri, ae_ref[...], precision=HI)
        acs_t = acs_s.T
        lo = lax.broadcasted_iota(I32, (T, LANE), 1) < HP
        last_row = lax.broadcasted_iota(I32, (T, LANE), 0) == T - 1
        cols_ref[...] = jnp.zeros_like(cols_ref)
        rows_ref[...] = jnp.zeros_like(rows_ref)
        dsk_parts = []
        for g in range(G):
            bsl = slice(INNER + g * NST, INNER + (g + 1) * NST)
            csl = slice(INNER + (G + g) * NST, INNER + (G + g + 1) * NST)
            Bb = xc_ref[:, bsl].astype(BF16)
            Cb = xc_ref[:, csl].astype(BF16)
            Gm = _dot(Cb, Bb, NT)
            dG = jnp.zeros((T, T), F32)
            dB = jnp.zeros((T, NST), F32)
            dC = jnp.zeros((T, NST), F32)
            for j in range(g * NPAIR // G, (g + 1) * NPAIR // G):
                sl = slice(j * LANE, (j + 1) * LANE)
                Xp = xc_ref[:, sl]
                dtp = dte_ref[:, sl]
                Xdt = Xp * dtp
                Xb = Xdt.astype(BF16)
                acs_p = acs_e[:, sl]
                last = acs_p[T - 1:T, :]
                e_p, dec, cd = jnp.exp(acs_p), jnp.exp(last - acs_p), jnp.exp(last)
                Hin = hin_ref[0, j]
                Hb = Hin.astype(BF16)
                dHn = dht_ref[j]
                dHb = dHn.astype(BF16)
                dYp = dy_ref[:, sl]
                z = _dot(Cb, Hb)
                dz = (dYp * e_p).astype(BF16)
                dacs_p = dYp * z * e_p
                dC = dC + _dot(dz, Hb, NT)
                dHin = _dot(Cb, dz, TN) + cd * dHn
                dlast = _colsum(dHn * Hin) * cd
                qv = _dot(Bb, dHb)
                dXdt = qv * dec
                ddec = qv * Xdt * dec
                dacs_p = dacs_p - ddec
                dlast = dlast + _colsum(ddec)
                dB = dB + _dot((Xdt * dec).astype(BF16), dHb, NT)
                for e in (0, 1):
                    h = 2 * j + e
                    Lm = jnp.exp(jnp.where(tril, acs_s[:, h:h + 1] - acs_t[h:h + 1, :], -1e30))
                    Mh = Gm * Lm
                    dYe = jnp.where(lo if e == 0 else jnp.logical_not(lo), dYp, 0.0).astype(BF16)
                    dM = _dot(dYe, Xb, NT)
                    dXdt = dXdt + _dot(Mh.astype(BF16), dYe, TN)
                    W = dM * Mh
                    cols_ref[:, h:h + 1] = jnp.sum(W, axis=1, keepdims=True)
                    rows_ref[h:h + 1, :] = _colsum(W)
                    dG = dG + dM * Lm
                dacs_ref[:, sl] = dacs_p + jnp.where(last_row, dlast, 0.0)
                ddt_ref[:, sl] = dXdt * Xp
                dxc_ref[:, sl] = dXdt * dtp + dYp * dsk_ref[:, sl]
                dsk_parts.append(_colsum(dYp * Xp))
                dht_ref[j] = dHin
            dGb = dG.astype(BF16)
            dxc_ref[:, bsl] = dB + _dot(dGb, Cb, TN)
            dxc_ref[:, csl] = dC + _dot(dGb, Bb)
        E = e_ref[...]
        dacs_s = cols_ref[...] - rows_ref[...].T + _dot(dacs_ref[...], E, NT, precision=HI)
        da = _dot(tri, dacs_s, TN, precision=HI)
        heads = lax.broadcasted_iota(I32, (1, LANE), 1) < HS
        A = jnp.where(heads, -jnp.exp(alog_ref[...]), 0.0)
        ddt = _dot(ddt_ref[...], E, NT, precision=HI) + da * A
        draw = jnp.where(heads, ddt * _sigmoid(raw_ref[...] + bias_ref[...]), 0.0)
        draw_ref[...] = draw
        dsk = _dot(jnp.broadcast_to(jnp.concatenate(dsk_parts, axis=1), (8, INNER)), E, NT, precision=HI)[0:1]
        for ref, val in ((dbias_ref, _colsum(draw)), (dalog_ref, _colsum(da * a_s)), (dskip_ref, dsk)):
            @pl.when(first)
            def _():
                ref[...] = val

            @pl.when(jnp.logical_not(first))
            def _():
                ref[...] += val

    rows = lambda w: pl.BlockSpec((T, w), lambda c: (NC - 1 - c, 0))
    vec = lambda w: pl.BlockSpec((1, w), lambda c: (0, 0))
    return pl.pallas_call(
        body, name="ssd_bwd", grid=(NC,),
        in_specs=[rows(INNER), rows(cfg.CONVCH), rows(INNER), rows(INNER), rows(LANE), vec(INNER),
                  pl.BlockSpec((1, NPAIR, NST, LANE), lambda c: (NC - 1 - c, 0, 0, 0)), rows(LANE), vec(LANE), vec(LANE),
                  pl.BlockSpec((LANE, INNER), lambda c: (0, 0))],
        out_specs=[rows(cfg.CONVCH), rows(LANE), vec(LANE), vec(LANE), vec(LANE)],
        out_shape=[jax.ShapeDtypeStruct((S, cfg.CONVCH), F32), jax.ShapeDtypeStruct((S, LANE), F32)]
        + [jax.ShapeDtypeStruct((1, LANE), F32)] * 3,
        scratch_shapes=[pltpu.VMEM((NPAIR, NST, LANE), F32), pltpu.VMEM((T, LANE), F32), pltpu.VMEM((LANE, T), F32),
                        pltpu.VMEM((T, INNER), F32), pltpu.VMEM((T, INNER), F32)],
        compiler_params=_params(("arbitrary",)),
    )(dy, xc, dt_exp, a_exp, a_small, dskip_exp, hin, dt_raw, dt_bias_pad, a_log_pad, expand)


def _ssd_post(cfg, y, z, norm_g):
    W = cfg.INNER // cfg.G

    def fn(y, z, g):
        yz = y * z * _sigmoid(z)
        return jnp.concatenate([yz[:, i * W:(i + 1) * W] * _rs(yz[:, i * W:(i + 1) * W]) for i in range(cfg.G)], axis=1) * g

    return _rowwise("ssd_post", fn, [y, z], [norm_g], [(cfg.INNER, BF16)], [], _pick(cfg.S, 256, 8))[0]


def _ssd_post_bwd(cfg, db, y, z, norm_g):
    W = cfg.INNER // cfg.G

    def fn(db, y, z, g):
        sg = _sigmoid(z)
        yz = y * z * sg
        dn = db * g
        dyz, nh = [], []
        for i in range(cfg.G):
            seg = yz[:, i * W:(i + 1) * W]
            r = _rs(seg)
            nh.append(seg * r)
            dyz.append(_rms_back(nh[-1], r, dn[:, i * W:(i + 1) * W]))
        dyz = jnp.concatenate(dyz, axis=1)
        return dyz * z * sg, dyz * y * sg * (1.0 + z * (1.0 - sg)), _colsum(db * jnp.concatenate(nh, axis=1))

    return _rowwise("ssd_post_bwd", fn, [db, y, z], [norm_g], [(cfg.INNER, F32), (cfg.INNER, F32)], [(1, cfg.INNER)],
                    _pick(cfg.S, 256, 8))


def _local_grads(cfg, x, tgt, W, sp, ffn_weights=None, ffn_grads_ready=None):
    S, D, H, INNER = cfg.S, cfg.D, cfg.H, cfg.INNER
    ts = _pick(S, 256, 8)
    tc = 256

    xn = _rowwise("rms_pre", lambda x, g: x * _rs(x) * g, [x], [sp["mix_pre_g"]], [(D, BF16)], [], ts)[0]
    u = _matmul("mm_in", xn, W["w_in"], "nn", F32)
    c_q, c_kv = u[:, :cfg.QL], u[:, cfg.QL:cfg.o_kr]
    kr = u[:, cfg.o_kr:cfg.o_z]
    z = u[:, cfg.o_z:cfg.o_xbc]
    xbc = u[:, cfg.o_xbc:cfg.o_dt]
    dt_raw = u[:, cfg.o_dt:]

    cqn = _rowwise("rms_q", lambda x, g: x * _rs(x) * g, [c_q], [sp["q_norm_g"]], [(cfg.QL, BF16)], [], ts)[0]
    ckvn = _rowwise("rms_kv", lambda x, g: x * _rs(x) * g, [c_kv], [sp["kv_norm_g"]], [(cfg.KVL, BF16)], [], ts)[0]
    q = _matmul("mm_uq", cqn, W["w_uq"], "nn", F32)
    kv = _matmul("mm_ukv", ckvn, W["w_ukv"], "nn", F32)
    cos2, sin2 = _rope_tables(S)
    Qh, Kh, Vh = _mla_pack(cfg, q, kv, kr, cos2, sin2)
    a_out, lse, lse_t = _attn_fwd(cfg, Qh, Kh, Vh)

    pad = lambda v: jnp.pad(v, ((0, 0), (0, LANE - v.shape[1])))
    expand = _expand_matrix(cfg)
    dt_bias_pad, a_log_pad = pad(sp["dt_bias"]), pad(sp["a_log"])
    dskip_exp = jnp.repeat(sp["d_skip"], HP, axis=1)
    xc = _colwise("ssm_act", _ssm_act, [xbc], [sp["ssm_conv_w"], sp["ssm_conv_b"]], [F32], [], tc)[0]
    dt_s, a_s, dt_exp, a_exp = _ssd_prep(cfg, dt_raw, dt_bias_pad, a_log_pad, expand)
    y_ssd, hin = _ssd_fwd(cfg, xc, dt_exp, a_exp, a_s, dskip_exp)
    b_out = _ssd_post(cfg, y_ssd, z, sp["ssm_norm_g"])

    ab_out = jnp.concatenate([a_out.astype(BF16), b_out], axis=1)
    if ffn_weights is not None:
        sp = dict(sp, mix_post_g=sp["mix_post_g"] + ffn_weights.pass_on(ab_out)[0, 0])
    mix = _matmul("mm_out", ab_out, W["w_out"], "nn", F32)

    def mid(x, mix, g_mp, g_fp):
        x1 = x + mix * _rs(mix) * g_mp
        return x1, x1 * _rs(x1) * g_fp

    x1, h2 = _rowwise("fwd_mid", mid, [x, mix], [sp["mix_post_g"], sp["ffn_pre_g"]], [(D, F32), (D, BF16)], [], ts)
    if ffn_weights is not None:
        W = dict(W, **ffn_weights.arrived(h2))
    gate_pre = _matmul("mm_gate", h2, W["w_gate"], "nn", F32, chips=True)
    up = _matmul("mm_up", h2, W["w_up"], "nn", F32, chips=True)
    act = _colwise("ffn_act", _ffn_act, [gate_pre, up], [sp["ffn_conv_w"], sp["ffn_conv_b"]], [BF16], [], tc)[0]
    f = _matmul("mm_down", act, W["w_down"], "nn", F32)

    def final(x1, f, t, g):
        r = _rs(f)
        fh = f * r
        err = x1 + fh * g - t
        loss = 0.5 * jnp.sum(jnp.mean(err * err, axis=-1, keepdims=True), axis=0, keepdims=True)
        dy = err * (1.0 / D)
        return dy, _rms_back(fh, r, dy * g), _colsum(dy * fh), loss

    dy, df, g_ffn_post, loss = _rowwise("final", final, [x1, f, tgt], [sp["ffn_post_g"]], [(D, F32), (D, BF16)],
                                        [(1, D), (1, LANE)], ts)
    gW = {}
    dact = _matmul("mm_down_dx", df, W["w_down"], "nt", F32)
    gW["w_down"] = _matmul("mm_down_dw", act, df, "tn", BF16)
    dgate, dup, g_ffn_conv_w, g_ffn_conv_b = _colwise(
        "ffn_act_bwd", _ffn_act_back, [dact, gate_pre, up], [sp["ffn_conv_w"], sp["ffn_conv_b"]], [BF16, BF16], [FFN_K, 1], tc)
    dh2 = _matmul("mm_gu_dx", dgate, W["w_gate"], "nt", F32, dup, W["w_up"], chips=True)
    gW["w_gate"] = _matmul("mm_gate_dw", h2, dgate, "tn", BF16, chips=True)
    gW["w_up"] = _matmul("mm_up_dw", h2, dup, "tn", BF16, chips=True)

    def mid_back(dy, dh2, x1, mix, g_mp, g_fp):
        r2 = _rs(x1)
        xh = x1 * r2
        dx1 = dy + _rms_back(xh, r2, dh2 * g_fp)
        r1 = _rs(mix)
        mh = mix * r1
        return dx1, _rms_back(mh, r1, dx1 * g_mp), _colsum(dh2 * xh), _colsum(dx1 * mh)

    dx1, dmix, g_ffn_pre, g_mix_post = _rowwise("bwd_mid", mid_back, [dy, dh2, x1, mix], [sp["mix_post_g"], sp["ffn_pre_g"]],
                                                [(D, F32), (D, BF16)], [(1, D), (1, D)], ts)
    dab_out = _matmul("mm_out_dx", dmix, W["w_out"], "nt", F32)
    db_out = dab_out[:, cfg.MLAW:]
    gW["w_out"] = _matmul("mm_out_dw", ab_out, dmix, "tn", BF16)
    if ffn_grads_ready is not None:
        token = ffn_grads_ready({n: gW[n] for n in ("w_down", "w_gate", "w_up", "w_out")})
        sp = dict(sp, ssm_norm_g=sp["ssm_norm_g"] + token[0, 0])

    dy_ssd, dz, g_ssm_norm = _ssd_post_bwd(cfg, db_out, y_ssd, z, sp["ssm_norm_g"])
    dxc, ddt_raw, g_dt_bias, g_a_log, g_d_skip = _ssd_bwd(cfg, dy_ssd, xc, dt_exp, a_exp, a_s, dskip_exp, hin, dt_raw,
                                                          dt_bias_pad, a_log_pad, expand)
    dxbc, g_ssm_conv_w, g_ssm_conv_b = _colwise("ssm_act_bwd", _ssm_act_back, [dxc, xbc], [sp["ssm_conv_w"], sp["ssm_conv_b"]],
                                                [BF16], [SSM_K, 1], tc)

    dQ, delta_t = _attn_dq(cfg, Qh, Kh, Vh, dab_out, a_out, lse)
    dK, dV = _attn_dkv(cfg, Qh, Kh, Vh, dab_out, lse_t, delta_t)
    dq, dkv, dkr = _mla_unpack(cfg, dQ, dK, dV, cos2, sin2)
    dcqn = _matmul("mm_uq_dx", dq, W["w_uq"], "nt", F32)
    dckvn = _matmul("mm_ukv_dx", dkv, W["w_ukv"], "nt", F32)
    gW["w_uq"] = _matmul("mm_uq_dw", cqn, dq, "tn", BF16)
    gW["w_ukv"] = _matmul("mm_ukv_dw", ckvn, dkv, "tn", BF16)

    def rms_back(x, dy, g):
        r = _rs(x)
        xh = x * r
        return _rms_back(xh, r, dy * g), _colsum(dy * xh)

    dc_q, g_q_norm = _rowwise("rms_q_bwd", rms_back, [c_q, dcqn], [sp["q_norm_g"]], [(cfg.QL, BF16)], [(1, cfg.QL)], ts)
    dc_kv, g_kv_norm = _rowwise("rms_kv_bwd", rms_back, [c_kv, dckvn], [sp["kv_norm_g"]], [(cfg.KVL, BF16)], [(1, cfg.KVL)], ts)

    du = jnp.concatenate([dc_q, dc_kv, dkr, dz.astype(BF16), dxbc, ddt_raw.astype(BF16)], axis=1)
    dxn = _matmul("mm_in_dx", du, W["w_in"], "nt", F32)
    gW["w_in"] = _matmul("mm_in_dw", xn, du, "tn", BF16)

    def first_back(dx1, dxn, x, g):
        r = _rs(x)
        xh = x * r
        return dx1 + _rms_back(xh, r, dxn * g), _colsum(dxn * xh)

    grad_x, g_mix_pre = _rowwise("bwd_first", first_back, [dx1, dxn, x], [sp["mix_pre_g"]], [(D, F32)], [(1, D)], ts)

    gs = dict(mix_pre_g=g_mix_pre, q_norm_g=g_q_norm, kv_norm_g=g_kv_norm, ssm_conv_w=g_ssm_conv_w, ssm_conv_b=g_ssm_conv_b,
              dt_bias=g_dt_bias[:, :cfg.HS], a_log=g_a_log[:, :cfg.HS], d_skip=g_d_skip[:, :cfg.HS], ssm_norm_g=g_ssm_norm,
              mix_post_g=g_mix_post, ffn_pre_g=g_ffn_pre, ffn_conv_w=g_ffn_conv_w, ffn_conv_b=g_ffn_conv_b,
              ffn_post_g=g_ffn_post)
    return loss, grad_x, gW, gs


def _to_kernel_layout(cfg, name, w):
    if name == "w_in":
        a = cfg.o_kr + ROPE
        return jnp.concatenate([w[:, :a], jnp.zeros((w.shape[0], LANE - ROPE), w.dtype), w[:, a:],
                                jnp.zeros((w.shape[0], LANE - cfg.HS), w.dtype)], axis=1)
    if name == "w_uq":
        w3 = w.reshape(cfg.QL, cfg.H, NOPE + ROPE)
        return jnp.concatenate([w3[:, :, :NOPE].reshape(cfg.QL, -1), w3[:, :, NOPE:].reshape(cfg.QL, -1)], axis=1)
    if name == "w_ukv":
        w3 = w.reshape(cfg.KVL, cfg.H, NOPE + VH)
        return jnp.concatenate([w3[:, :, :NOPE].reshape(cfg.KVL, -1), w3[:, :, NOPE:].reshape(cfg.KVL, -1)], axis=1)
    return w


def _from_kernel_layout(cfg, name, g):
    if name == "w_in":
        return jnp.concatenate([g[:, :cfg.o_kr + ROPE], g[:, cfg.o_z:cfg.o_dt + cfg.HS]], axis=1)
    if name == "w_uq":
        n = g[:, :cfg.H * NOPE].reshape(cfg.QL, cfg.H, NOPE)
        r = g[:, cfg.H * NOPE:].reshape(cfg.QL, cfg.H, ROPE)
        return jnp.concatenate([n, r], axis=2).reshape(cfg.QL, -1)
    if name == "w_ukv":
        n = g[:, :cfg.H * NOPE].reshape(cfg.KVL, cfg.H, NOPE)
        v = g[:, cfg.H * NOPE:].reshape(cfg.KVL, cfg.H, VH)
        return jnp.concatenate([n, v], axis=2).reshape(cfg.KVL, -1)
    return g


def _cols_to_chips(w):
    r, c = w.shape
    return w.reshape(r, N_CHIPS, c // N_CHIPS).transpose(1, 0, 2)


def _chips_to_cols(g):
    k, r, cs = g.shape
    return g.transpose(1, 0, 2).reshape(r, k * cs)


_CHIP_MAJOR = ("w_gate", "w_up")
_RELAYOUT = ("w_in", "w_uq", "w_ukv")


def _gathered_to_kernel(cfg, name, wg):
    if name in _CHIP_MAJOR:
        return wg
    if name in _RELAYOUT:
        return _to_kernel_layout(cfg, name, _chips_to_cols(wg))
    return wg.reshape(wg.shape[0] * wg.shape[1], wg.shape[2])


def _grad_to_chips(cfg, name, g):
    if name in _CHIP_MAJOR:
        return g
    if name in _RELAYOUT:
        return _cols_to_chips(_from_kernel_layout(cfg, name, g))
    return g.reshape(N_CHIPS, g.shape[0] // N_CHIPS, g.shape[1])


def _me():
    return lax.axis_index("x"), lax.axis_index("y"), lax.axis_index("c")


def _other_chips(x, y):
    return [(1 - x, y), (x, 1 - y), (1 - x, 1 - y)]


_ANY = pl.BlockSpec(memory_space=pl.ANY)


def _row_block(rows, cols, mult):
    return _pick(rows, max(mult, (1 << 19) // cols // mult * mult), mult)


def _scalar(v):
    return v.astype(I32).reshape(1)


def _stage_shard(name, w, chip):
    rs, cs = w.shape
    tr = _row_block(rs, cs, 16)

    def body(chip_ref, w_ref, o_ref):
        o_ref[...] = w_ref[...].astype(BF16)

    return pl.pallas_call(
        body, name="stage_" + name,
        grid_spec=pltpu.PrefetchScalarGridSpec(
            num_scalar_prefetch=1, grid=(rs // tr,),
            in_specs=[pl.BlockSpec((tr, cs), lambda i, chip_ref: (i, 0))],
            out_specs=pl.BlockSpec((None, tr, cs), lambda i, chip_ref: (chip_ref[0], i, 0))),
        out_shape=jax.ShapeDtypeStruct((N_CHIPS, rs, cs), BF16),
        compiler_params=_params(("parallel",)),
    )(_scalar(chip), w)


def _half(ref, k, half):
    h = ref.shape[1] // 2
    return ref.at[k, pl.ds(pl.multiple_of(half * h, 16), h), :]


def _allgather_weights(bufs):
    n = len(bufs)

    def body(*refs):
        outs, send_sems, recv_sems = refs[n:2 * n], refs[2 * n], refs[2 * n + 1]
        x, y, c = _me()
        chip = 2 * x + y
        sib = (x, y, 1 - c)
        chips = _other_chips(x, y)

        def copy(k, part, to):
            return pltpu.make_async_remote_copy(src_ref=part, dst_ref=part, send_sem=send_sems.at[k], recv_sem=recv_sems.at[k],
                                                device_id=to, device_id_type=MESH_ID)

        started = []
        for w, o_ref in enumerate(outs):
            for j, (cx, cy) in enumerate(chips):
                started.append(copy(6 * w + j, _half(o_ref, chip, c), (cx, cy, c)))
                started[-1].start()
        for w, o_ref in enumerate(outs):
            for j, (cx, cy) in enumerate(chips):
                theirs = _half(o_ref, 2 * cx + cy, c)
                copy(6 * w + j, theirs, sib).wait_recv()
                started.append(copy(6 * w + 3 + j, theirs, sib))
                started[-1].start()
        for w, o_ref in enumerate(outs):
            for j, (cx, cy) in enumerate(chips):
                copy(6 * w + 3 + j, _half(o_ref, 2 * cx + cy, 1 - c), sib).wait_recv()
        for cp in started:
            cp.wait_send()

    return pl.pallas_call(
        body, name="allgather_weights", in_specs=[_ANY] * n, out_specs=[_ANY] * n,
        out_shape=[jax.ShapeDtypeStruct(b.shape, b.dtype) for b in bufs],
        input_output_aliases={i: i for i in range(n)},
        scratch_shapes=[pltpu.SemaphoreType.DMA((6 * n,)), pltpu.SemaphoreType.DMA((6 * n,))],
    )(*bufs)


_HBM = pl.BlockSpec(memory_space=pltpu.HBM)
_SEM = pl.BlockSpec(memory_space=pltpu.SEMAPHORE)
_EFFECT = pltpu.SideEffectType.DATAFLOW_SIDE_EFFECTING


def _split_start(name, bufs, n_copies, copies):
    n = len(bufs)

    def body(*refs):
        for cp in copies(refs[:n], refs[n], refs[n + 1]):
            cp.start()
        refs[-1][...] = jnp.zeros_like(refs[-1])

    res = pl.pallas_call(
        body, name=name,
        out_shape=(pltpu.SemaphoreType.DMA((n_copies,)), pltpu.SemaphoreType.DMA((n_copies,)),
                   *[pltpu.HBM(b.shape, b.dtype) for b in bufs], jax.ShapeDtypeStruct((8, LANE), F32)),
        in_specs=[_HBM] * n, out_specs=(_SEM, _SEM, *[_HBM] * n, pl.BlockSpec(memory_space=pltpu.VMEM)),
        input_output_aliases={i: 2 + i for i in range(n)},
        compiler_params=pltpu.CompilerParams(has_side_effects=_EFFECT),
    )(*[pltpu.with_memory_space_constraint(b, pltpu.HBM) for b in bufs])
    return res[0], res[1], list(res[2:2 + n]), res[-1]


def _split_wait(name, send_sems, recv_sems, bufs, after, copies):
    n = len(bufs)

    def body(*refs):
        for cp in copies(refs[:n], refs[n], refs[n + 1]):
            cp.wait_send()
            cp.wait_recv()

    return list(pl.pallas_call(
        body, name=name, out_shape=[pltpu.HBM(b.shape, b.dtype) for b in bufs],
        in_specs=[_HBM] * n + [_SEM, _SEM, _ANY], out_specs=[_HBM] * n,
        input_output_aliases={i: i for i in range(n)},
        compiler_params=pltpu.CompilerParams(has_side_effects=_EFFECT),
    )(*bufs, send_sems, recv_sems, after))


def _gather_to_chips(bufs, send_sems, recv_sems):
    x, y, c = _me()
    return [pltpu.make_async_remote_copy(src_ref=_half(b, 2 * x + y, c), dst_ref=_half(b, 2 * x + y, c),
                                         send_sem=send_sems.at[3 * w + j], recv_sem=recv_sems.at[3 * w + j],
                                         device_id=(cx, cy, c), device_id_type=MESH_ID)
            for w, b in enumerate(bufs) for j, (cx, cy) in enumerate(_other_chips(x, y))]


def _gather_to_sibling(bufs, send_sems, recv_sems):
    x, y, c = _me()
    return [pltpu.make_async_remote_copy(src_ref=_half(b, 2 * cx + cy, c), dst_ref=_half(b, 2 * cx + cy, c),
                                         send_sem=send_sems.at[3 * w + j], recv_sem=recv_sems.at[3 * w + j],
                                         device_id=(x, y, 1 - c), device_id_type=MESH_ID)
            for w, b in enumerate(bufs) for j, (cx, cy) in enumerate(_other_chips(x, y))]


def _pair_exchange(name, grads):
    n = len(grads)

    def body(*refs):
        ins, outs, send_sems, recv_sems = refs[:n], refs[n:2 * n], refs[2 * n], refs[2 * n + 1]
        x, y, c = _me()
        cps = []
        for w, (g_ref, o_ref) in enumerate(zip(ins, outs)):
            h = o_ref.shape[1]
            src = g_ref.at[:, pl.ds(pl.multiple_of((1 - c) * h, 16), h), :]
            cps.append(pltpu.make_async_remote_copy(src_ref=src, dst_ref=o_ref, send_sem=send_sems.at[w], recv_sem=recv_sems.at[w],
                                                    device_id=(x, y, 1 - c), device_id_type=MESH_ID))
            cps[-1].start()
        for cp in cps:
            cp.wait()

    return pl.pallas_call(
        body, name="pair_exchange_" + name, in_specs=[_ANY] * n, out_specs=[_ANY] * n,
        out_shape=[jax.ShapeDtypeStruct((g.shape[0], g.shape[1] // 2, g.shape[2]), g.dtype) for g in grads],
        scratch_shapes=[pltpu.SemaphoreType.DMA((n,)), pltpu.SemaphoreType.DMA((n,))],
    )(*grads)


def _pair_sum(name, g, theirs, c):
    _, h, cs = theirs.shape
    tr = _row_block(h, cs, 16)
    nb = h // tr

    def body(c_ref, a_ref, b_ref, o_ref):
        o_ref[...] = (a_ref[...].astype(F32) + b_ref[...].astype(F32)).astype(o_ref.dtype)

    return pl.pallas_call(
        body, name="pair_sum_" + name,
        grid_spec=pltpu.PrefetchScalarGridSpec(
            num_scalar_prefetch=1, grid=(N_CHIPS, nb),
            in_specs=[pl.BlockSpec((None, tr, cs), lambda k, i, c_ref: (k, c_ref[0] * nb + i, 0)),
                      pl.BlockSpec((None, tr, cs), lambda k, i, c_ref: (k, i, 0))],
            out_specs=pl.BlockSpec((None, tr, cs), lambda k, i, c_ref: (k, i, 0))),
        out_shape=jax.ShapeDtypeStruct(theirs.shape, BF16),
        compiler_params=_params(("parallel", "parallel")),
    )(_scalar(c), g, theirs)


def _chip_copies(srcs, lands, send_sems, recv_sems):
    x, y, c = _me()
    return [pltpu.make_async_remote_copy(src_ref=s_ref.at[2 * cx + cy], dst_ref=l_ref.at[j], send_sem=send_sems.at[3 * w + j],
                                         recv_sem=recv_sems.at[3 * w + j], device_id=(cx, cy, c), device_id_type=MESH_ID)
            for w, (s_ref, l_ref) in enumerate(zip(srcs, lands)) for j, (cx, cy) in enumerate(_other_chips(x, y))]


def _chip_exchange_start(name, sums):
    n = len(sums)
    lands = [lax.empty((3,) + s.shape[1:], s.dtype) for s in sums]
    send_sems, recv_sems, bufs, token = _split_start(
        "chip_exchange_start_" + name, [*sums, *lands], 3 * n, lambda refs, ss, rs: _chip_copies(refs[:n], refs[n:], ss, rs))
    return send_sems, recv_sems, bufs[:n], bufs[n:], token


def _chip_exchange_wait(name, send_sems, recv_sems, sums, lands, after):
    n = len(sums)
    bufs = _split_wait("chip_exchange_wait_" + name, send_sems, recv_sems, [*sums, *lands], after,
                       lambda refs, ss, rs: _chip_copies(refs[:n], refs[n:], ss, rs))
    return bufs[:n], bufs[n:]


def _chip_exchange(name, sums):
    n = len(sums)

    def body(*refs):
        cps = _chip_copies(refs[:n], refs[n:2 * n], refs[2 * n], refs[2 * n + 1])
        for cp in cps:
            cp.start()
        for cp in cps:
            cp.wait()

    return pl.pallas_call(
        body, name="chip_exchange_" + name, in_specs=[_ANY] * n, out_specs=[_ANY] * n,
        out_shape=[jax.ShapeDtypeStruct((3,) + s.shape[1:], s.dtype) for s in sums],
        scratch_shapes=[pltpu.SemaphoreType.DMA((3 * n,)), pltpu.SemaphoreType.DMA((3 * n,))],
    )(*sums)


def _chip_sum(name, sums, theirs, chip):
    _, h, cs = sums.shape
    tr = _row_block(h, cs, 16)

    def body(chip_ref, s_ref, t_ref, o_ref):
        acc = s_ref[...].astype(F32)
        for k in range(3):
            acc = acc + t_ref[k].astype(F32)
        o_ref[...] = acc

    return pl.pallas_call(
        body, name="chip_sum_" + name,
        grid_spec=pltpu.PrefetchScalarGridSpec(
            num_scalar_prefetch=1, grid=(h // tr,),
            in_specs=[pl.BlockSpec((None, tr, cs), lambda i, chip_ref: (chip_ref[0], i, 0)),
                      pl.BlockSpec((3, tr, cs), lambda i, chip_ref: (0, i, 0))],
            out_specs=pl.BlockSpec((tr, cs), lambda i, chip_ref: (i, 0))),
        out_shape=jax.ShapeDtypeStruct((h, cs), F32),
        compiler_params=_params(("parallel",)),
    )(_scalar(chip), sums, theirs)


def _sibling_exchange(halves):
    n = len(halves)

    def body(*refs):
        ins, outs, send_sems, recv_sems = refs[:n], refs[n:2 * n], refs[2 * n], refs[2 * n + 1]
        x, y, c = _me()
        cps = []
        for w, (h_ref, o_ref) in enumerate(zip(ins, outs)):
            cps.append(pltpu.make_async_remote_copy(src_ref=h_ref, dst_ref=o_ref, send_sem=send_sems.at[w], recv_sem=recv_sems.at[w],
                                                    device_id=(x, y, 1 - c), device_id_type=MESH_ID))
            cps[-1].start()
        for cp in cps:
            cp.wait()

    return pl.pallas_call(
        body, name="grad_sibling_exchange", in_specs=[_ANY] * n, out_specs=[_ANY] * n,
        out_shape=[jax.ShapeDtypeStruct(h.shape, h.dtype) for h in halves],
        scratch_shapes=[pltpu.SemaphoreType.DMA((n,)), pltpu.SemaphoreType.DMA((n,))],
    )(*halves)


def _allreduce_small(name, vec):
    def body(v_ref, o_ref, buf_ref, send_sems, recv_sems):
        x, y, c = _me()
        me = 4 * x + 2 * y + c
        cps = []
        for p in range(1, 8):
            px, py, pc = x ^ (p >> 2), y ^ ((p >> 1) & 1), c ^ (p & 1)
            cps.append(pltpu.make_async_remote_copy(src_ref=v_ref, dst_ref=buf_ref.at[me], send_sem=send_sems.at[p - 1],
                                                    recv_sem=recv_sems.at[p - 1], device_id=(px, py, pc), device_id_type=MESH_ID))
            cps[-1].start()
        buf_ref[me] = v_ref[...]
        for p in range(1, 8):
            theirs = buf_ref.at[me ^ p]
            pltpu.make_async_remote_copy(src_ref=theirs, dst_ref=theirs, send_sem=send_sems.at[p - 1], recv_sem=recv_sems.at[p - 1],
                                         device_id=(x, y, c), device_id_type=MESH_ID).wait_recv()
        for cp in cps:
            cp.wait_send()
        acc = buf_ref[0]
        for k in range(1, 8):
            acc = acc + buf_ref[k]
        o_ref[...] = acc

    vm = pl.BlockSpec(memory_space=pltpu.VMEM)
    return pl.pallas_call(
        body, name=name, in_specs=[vm], out_specs=vm, out_shape=jax.ShapeDtypeStruct(vec.shape, F32),
        scratch_shapes=[pltpu.VMEM((8,) + vec.shape, F32), pltpu.SemaphoreType.DMA((7,)), pltpu.SemaphoreType.DMA((7,))],
    )(vec)


def _adam_math(w, g, m, v):
    m = ADAM_B1 * m + (1.0 - ADAM_B1) * g
    v = ADAM_B2 * v + (1.0 - ADAM_B2) * (g * g)
    m_hat = m / (1.0 - ADAM_B1 ** ADAM_STEP)
    v_hat = v / (1.0 - ADAM_B2 ** ADAM_STEP)
    return -ADAM_LR * (m_hat / (jnp.sqrt(v_hat) + ADAM_EPS) + ADAM_WD * w), m, v


def _adamw(name, w, g, m, v):
    R, C = w.shape
    tr = _row_block(R, C, 8)

    def body(w_ref, g_ref, m_ref, v_ref, d_ref, nm_ref, nv_ref):
        d_ref[...], nm_ref[...], nv_ref[...] = _adam_math(w_ref[...], g_ref[...], m_ref[...], v_ref[...])

    blk = pl.BlockSpec((tr, C), lambda i: (i, 0))
    return pl.pallas_call(
        body, name=name, grid=(R // tr,), in_specs=[blk] * 4, out_specs=[blk] * 3,
        out_shape=[jax.ShapeDtypeStruct((R, C), F32)] * 3, compiler_params=_params(("parallel",)),
    )(w, g, m, v)


def _adamw_halves(name, w, mine, theirs, m, v, c):
    rs, cs = w.shape
    h = rs // 2
    tr = _row_block(h, cs, 8)
    nb = h // tr

    def body(c_ref, w_ref, a_ref, b_ref, m_ref, v_ref, g_ref, d_ref, nm_ref, nv_ref):
        g = jnp.where(pl.program_id(0) == c_ref[0], a_ref[...], b_ref[...])
        g_ref[...] = g
        d_ref[...], nm_ref[...], nv_ref[...] = _adam_math(w_ref[...], g, m_ref[...], v_ref[...])

    full = pl.BlockSpec((tr, cs), lambda s, i, c_ref: (s * nb + i, 0))
    part = pl.BlockSpec((tr, cs), lambda s, i, c_ref: (i, 0))
    return pl.pallas_call(
        body, name=name,
        grid_spec=pltpu.PrefetchScalarGridSpec(num_scalar_prefetch=1, grid=(2, nb), in_specs=[full, part, part, full, full],
                                               out_specs=[full] * 4),
        out_shape=[jax.ShapeDtypeStruct((rs, cs), F32)] * 4, compiler_params=_params(("parallel", "parallel")),
    )(_scalar(c), w, mine, theirs, m, v)


def _pack_small(arrs):
    flat = jnp.concatenate([a.reshape(-1) for a in arrs])
    n = -(-flat.shape[0] // (8 * LANE)) * 8 * LANE
    return jnp.pad(flat, (0, n - flat.shape[0])).reshape(8, n // 8)


def _unpack_small(vec, shapes):
    flat, out, off = vec.reshape(-1), [], 0
    for s in shapes:
        out.append(flat[off:off + s[0] * s[1]].reshape(s))
        off += s[0] * s[1]
    return out


class _FfnWeights:
    def __init__(self, cfg, names, staged):
        self.cfg, self.names, self.k = cfg, names, 3 * len(names)
        self.send, self.recv, self.bufs, self.token = _split_start("gather_chips_start", staged, self.k, _gather_to_chips)

    def pass_on(self, after):
        bufs = _split_wait("gather_chips_wait", self.send, self.recv, self.bufs, after, _gather_to_chips)
        self.send, self.recv, self.bufs, token = _split_start("gather_sibling_start", bufs, self.k, _gather_to_sibling)
        return token

    def arrived(self, after):
        bufs = _split_wait("gather_sibling_wait", self.send, self.recv, self.bufs, after, _gather_to_sibling)
        return {n: _gathered_to_kernel(self.cfg, n, b) for n, b in zip(self.names, bufs)}


def _step(cfg, a):
    chip = 2 * lax.axis_index("x") + lax.axis_index("y")
    core = lax.axis_index("c")
    big = [n for n, _, _, _ in cfg.BIG]

    ffn = ("w_gate", "w_up", "w_down")
    first = [n for n in big if n not in ffn]
    staged = {n: _stage_shard(n, a[n], chip) for n in big}
    W = {n: _gathered_to_kernel(cfg, n, wg) for n, wg in zip(first, _allgather_weights([staged[n] for n in first]))}
    ffn_weights = _FfnWeights(cfg, ffn, [staged[n] for n in ffn])

    sp = {n: a[n] for n in SMALL}
    sharded = _pack_small([a[n] for n in SMALL_SHARDED])
    slot = jnp.where(lax.broadcasted_iota(I32, (N_CHIPS,) + sharded.shape, 0) == chip, 0.5 * sharded[None], 0.0)
    allp = _allreduce_small("allgather_small", slot.reshape(N_CHIPS * 8, -1)).reshape((N_CHIPS,) + sharded.shape)
    per_chip = [_unpack_small(allp[ch], [a[n].shape for n in SMALL_SHARDED]) for ch in range(N_CHIPS)]
    for k, n in enumerate(SMALL_SHARDED):
        sp[n] = jnp.concatenate([per_chip[ch][k] for ch in range(N_CHIPS)], axis=1)
    sp["mix_pre_g"] = sp["mix_pre_g"] + ffn_weights.token[0, 0]

    def pair_sums(tag, names, gW):
        grads = [_grad_to_chips(cfg, n, gW[n]) for n in names]
        return [_pair_sum(n, g, t, core) for n, g, t in zip(names, grads, _pair_exchange(tag, grads))]

    early = ("w_down", "w_gate", "w_up", "w_out")
    late = [n for n in big if n not in early]
    started = []

    def ffn_grads_ready(g_early):
        started.extend(_chip_exchange_start("ffn", pair_sums("ffn", early, g_early)))
        return started[-1]

    loss, grad_x, gW, gs = _local_grads(cfg, a["x"], a["loss_target"], W, sp, ffn_weights, ffn_grads_ready)
    sums = dict(zip(late, pair_sums("rest", late, gW)))
    landed = dict(zip(late, _chip_exchange("rest", [sums[n] for n in late])))
    e_sums, e_landed = _chip_exchange_wait("ffn", *started[:4], landed[late[0]])
    sums.update(zip(early, e_sums))
    landed.update(zip(early, e_landed))
    mine = [_chip_sum(n, sums[n], landed[n], chip) for n in big]
    theirs = _sibling_exchange(mine)

    shapes = [gs[n].shape for n in SMALL] + [(1, LANE)]
    red = _unpack_small(_allreduce_small("allreduce_small", _pack_small([gs[n] for n in SMALL] + [loss])), shapes)
    g_small = dict(zip(SMALL, red[:-1]))
    for n in SMALL_SHARDED:
        cs = a[n].shape[1]
        g_small[n] = lax.dynamic_slice_in_dim(g_small[n], chip * cs, cs, axis=1)

    out = {"loss": red[-1][0, 0], "grad_x": grad_x}
    for n, gm, gt in zip(big, mine, theirs):
        out["grad_" + n], out["delta_" + n], out["new_m_" + n], out["new_v_" + n] = _adamw_halves(
            "adamw_" + n, a[n], gm, gt, a["m_" + n], a["v_" + n], core)
    sshapes = [a[n].shape for n in SMALL]
    d, nm, nv = _adamw("adamw_small", _pack_small([a[n] for n in SMALL]), _pack_small([g_small[n] for n in SMALL]),
                       _pack_small([a["m_" + n] for n in SMALL]), _pack_small([a["v_" + n] for n in SMALL]))
    for n, dd, mm, vv in zip(SMALL, _unpack_small(d, sshapes), _unpack_small(nm, sshapes), _unpack_small(nv, sshapes)):
        out["grad_" + n], out["delta_" + n], out["new_m_" + n], out["new_v_" + n] = g_small[n], dd, mm, vv
    return out


def kernel(x, mix_pre_g, w_in, q_norm_g, w_uq, kv_norm_g, w_ukv, ssm_conv_w, ssm_conv_b, dt_bias, a_log, d_skip, ssm_norm_g, w_out, mix_post_g, ffn_pre_g, w_gate, w_up, ffn_conv_w, ffn_conv_b, w_down, ffn_post_g, loss_target, m_mix_pre_g, m_w_in, m_q_norm_g, m_w_uq, m_kv_norm_g, m_w_ukv, m_ssm_conv_w, m_ssm_conv_b, m_dt_bias, m_a_log, m_d_skip, m_ssm_norm_g, m_w_out, m_mix_post_g, m_ffn_pre_g, m_w_gate, m_w_up, m_ffn_conv_w, m_ffn_conv_b, m_w_down, m_ffn_post_g, v_mix_pre_g, v_w_in, v_q_norm_g, v_w_uq, v_kv_norm_g, v_w_ukv, v_ssm_conv_w, v_ssm_conv_b, v_dt_bias, v_a_log, v_d_skip, v_ssm_norm_g, v_w_out, v_mix_post_g, v_ffn_pre_g, v_w_gate, v_w_up, v_ffn_conv_w, v_ffn_conv_b, v_w_down, v_ffn_post_g):
    args = dict(locals())
    out = _step(_FULL, {k: (v[0] if v.ndim == 3 else v) for k, v in args.items()})
    res = [out["loss"], out["grad_x"][None]]
    for pre in ("grad_", "delta_", "new_m_", "new_v_"):
        res += [out[pre + n][None] if args[n].ndim == 3 else out[pre + n] for n in WEIGHTS]
    return tuple(res)
```

```python
import functools
import math

import jax
import jax.numpy as jnp
from jax import lax
from jax.experimental import pallas as pl
from jax.experimental.pallas import tpu as pltpu

F32, BF16, I32 = jnp.float32, jnp.bfloat16, jnp.int32
NN = (((1,), (0,)), ((), ()))
NT = (((1,), (1,)), ((), ()))
TN = (((0,), (0,)), ((), ()))
HI = lax.Precision.HIGHEST
MESH_ID = pl.DeviceIdType.MESH

EPS = 1e-6
CHUNK = 64
NOPE, ROPE, VH = 128, 64, 128
ROPE_THETA = 10000.0
HP, NST = 64, 128
SSM_K, FFN_K = 4, 3
LANE = 128
N_CHIPS = 4
VMEM_LIMIT = 52 * 1024 * 1024
MM_TILE, MM_TILE_K = 1408, 2816

ADAM_LR, ADAM_B1, ADAM_B2, ADAM_EPS, ADAM_WD, ADAM_STEP = 0.001, 0.9, 0.999, 1e-08, 0.01, 10


class _Cfg:
    def __init__(self, S, D, QL, KVL, H, HS, G, DFF, T):
        self.S, self.D, self.QL, self.KVL, self.H, self.HS, self.G, self.DFF, self.T = S, D, QL, KVL, H, HS, G, DFF, T
        self.INNER = HS * HP
        self.CONVCH = self.INNER + 2 * G * NST
        self.QW = H * (NOPE + ROPE)
        self.KVW = H * (NOPE + VH)
        self.MLAW = H * VH
        self.MIXW = self.MLAW + self.INNER
        self.IN_COLS = QL + KVL + ROPE + self.INNER + self.CONVCH + HS
        self.o_kr = QL + KVL
        self.o_z = self.o_kr + LANE
        self.o_xbc = self.o_z + self.INNER
        self.o_dt = self.o_xbc + self.CONVCH
        self.EXT = self.o_dt + LANE
        self.NPAIR = HS // 2
        self.REP = HS // G
        self.BIG = (("w_in", D, self.IN_COLS, 1), ("w_uq", QL, self.QW, 1), ("w_ukv", KVL, self.KVW, 1),
                    ("w_out", self.MIXW, D, 0), ("w_gate", D, DFF, 1), ("w_up", D, DFF, 1), ("w_down", DFF, D, 0))
        self.NSHARD = sum(r * c for _, r, c, _ in self.BIG) // N_CHIPS
        unit = 2 * LANE * 16
        self.NPACK = -(-self.NSHARD // unit) * unit
        self.R = self.NPACK // (2 * LANE)


_FULL = _Cfg(S=2048, D=2048, QL=768, KVL=512, H=8, HS=16, G=2, DFF=5632, T=256)

SMALL = ("mix_pre_g", "q_norm_g", "kv_norm_g", "ssm_conv_w", "ssm_conv_b", "dt_bias", "a_log", "d_skip", "ssm_norm_g",
         "mix_post_g", "ffn_pre_g", "ffn_conv_w", "ffn_conv_b", "ffn_post_g")
SMALL_SHARDED = ("ssm_conv_w", "ffn_conv_w")
WEIGHTS = ("mix_pre_g", "w_in", "q_norm_g", "w_uq", "kv_norm_g", "w_ukv", "ssm_conv_w", "ssm_conv_b", "dt_bias", "a_log",
           "d_skip", "ssm_norm_g", "w_out", "mix_post_g", "ffn_pre_g", "w_gate", "w_up", "ffn_conv_w", "ffn_conv_b",
           "w_down", "ffn_post_g")


def _pick(n, target, mult):
    best = None
    for d in range(mult, min(n, target) + 1, mult):
        if n % d == 0:
            best = d
    return best if best is not None else n


def _params(sem=None):
    kw = dict(vmem_limit_bytes=VMEM_LIMIT)
    if sem is not None:
        kw["dimension_semantics"] = sem
    return pltpu.CompilerParams(**kw)


def _dot(a, b, dims=NN, precision=None):
    return lax.dot_general(a, b, dims, preferred_element_type=F32, precision=precision)


def _sigmoid(x):
    return 1.0 / (1.0 + jnp.exp(-x))


def _rs(x):
    return lax.rsqrt(jnp.mean(x * x, axis=-1, keepdims=True) + EPS)


def _rms_back(xh, r, dn):
    return r * (dn - xh * jnp.mean(dn * xh, axis=-1, keepdims=True))


def _colsum(v):
    return jnp.sum(v, axis=0, keepdims=True)


def _matmul(name, a, b, mode, out_dtype, a2=None, b2=None, chips=False):
    cs = None
    if mode == "nn":
        (M, K), N = a.shape, b.shape[-1]
        if chips:
            cs, N = N, N_CHIPS * N
    elif mode == "nt":
        (M, K), N = a.shape, b.shape[-2]
        if chips:
            cs = b.shape[-1]
    else:
        (K, M), N = a.shape, b.shape[1]
        if chips:
            cs = N // N_CHIPS
    tm = _pick(M, MM_TILE, LANE)
    tn = _pick(cs if chips and mode != "nt" else N, MM_TILE, LANE)
    tk = _pick(cs, MM_TILE, LANE) if chips and mode == "nt" else _pick(K, MM_TILE_K, LANE)
    nk = K // tk
    dims = {"nn": NN, "nt": NT, "tn": TN}[mode]
    a_spec = pl.BlockSpec((tk, tm), lambda i, j, k: (k, i)) if mode == "tn" else pl.BlockSpec((tm, tk), lambda i, j, k: (i, k))
    b_spec = pl.BlockSpec((tn, tk), lambda i, j, k: (j, k)) if mode == "nt" else pl.BlockSpec((tk, tn), lambda i, j, k: (k, j))
    o_spec = pl.BlockSpec((tm, tn), lambda i, j, k: (i, j))
    o_shape = (M, N)
    if chips and mode == "nn":
        per = cs // tn
        b_spec = pl.BlockSpec((None, tk, tn), lambda i, j, k: (j // per, k, j % per))
    elif chips and mode == "nt":
        per = cs // tk
        b_spec = pl.BlockSpec((None, tn, tk), lambda i, j, k: (k // per, j, k % per))
    elif chips:
        per = cs // tn
        o_spec = pl.BlockSpec((None, tm, tn), lambda i, j, k: (j // per, i, j % per))
        o_shape = (N_CHIPS, M, cs)
    two = a2 is not None

    def product(refs):
        part = _dot(refs[0][...].astype(BF16), refs[1][...].astype(BF16), dims)
        if two:
            part += _dot(refs[2][...].astype(BF16), refs[3][...].astype(BF16), dims)
        return part

    def body_whole_k(*refs):
        refs[-1][...] = product(refs).astype(refs[-1].dtype)

    def body(*refs):
        o_ref, acc_ref = refs[-2], refs[-1]
        k = pl.program_id(2)

        @pl.when(k == 0)
        def _():
            acc_ref[...] = product(refs)

        @pl.when(k > 0)
        def _():
            acc_ref[...] += product(refs)

        @pl.when(k == nk - 1)
        def _():
            o_ref[...] = acc_ref[...].astype(o_ref.dtype)

    ins = (a, b, a2, b2) if two else (a, b)
    return pl.pallas_call(
        body_whole_k if nk == 1 else body, name=name, grid=(M // tm, N // tn, nk),
        in_specs=[a_spec, b_spec] * (2 if two else 1),
        out_specs=o_spec,
        out_shape=jax.ShapeDtypeStruct(o_shape, out_dtype),
        scratch_shapes=[] if nk == 1 else [pltpu.VMEM((tm, tn), F32)],
        compiler_params=_params(("parallel", "parallel", "arbitrary")),
    )(*ins)


def _rowwise(name, fn, rows, mats, outs, reds, ts):
    S = rows[0].shape[0]
    nr, nm, no = len(rows), len(mats), len(outs)

    def body(*refs):
        res = fn(*[r[...] for r in refs[:nr + nm]])
        res = res if isinstance(res, (tuple, list)) else (res,)
        for r, v in zip(refs[nr + nm:nr + nm + no], res[:no]):
            r[...] = v.astype(r.dtype)
        first = pl.program_id(0) == 0
        for r, v in zip(refs[nr + nm + no:], res[no:]):
            @pl.when(first)
            def _():
                r[...] = jnp.broadcast_to(v, r.shape)

            @pl.when(jnp.logical_not(first))
            def _():
                r[...] += jnp.broadcast_to(v, r.shape)

    in_specs = [pl.BlockSpec((ts, a.shape[1]), lambda i: (i, 0)) for a in rows]
    in_specs += [pl.BlockSpec(m.shape, lambda i, nd=m.ndim: (0,) * nd) for m in mats]
    out_specs = [pl.BlockSpec((ts, w), lambda i: (i, 0)) for w, _ in outs]
    out_specs += [pl.BlockSpec(s, lambda i: (0, 0)) for s in reds]
    out_shape = [jax.ShapeDtypeStruct((S, w), dt) for w, dt in outs] + [jax.ShapeDtypeStruct(s, F32) for s in reds]
    return pl.pallas_call(
        body, name=name, grid=(S // ts,), in_specs=in_specs, out_specs=out_specs, out_shape=out_shape,
        compiler_params=_params(("arbitrary",) if reds else ("parallel",)),
    )(*rows, *mats)


def _shift_down(v, s):
    if s == 0:
        return v
    rows = lax.broadcasted_iota(I32, v.shape, 0)
    return jnp.where(rows >= s, pltpu.roll(v, s, 0), 0.0)


def _shift_up(v, s):
    if s == 0:
        return v
    n = v.shape[0]
    rows = lax.broadcasted_iota(I32, v.shape, 0)
    return jnp.where(rows < n - s, pltpu.roll(v, n - s, 0), 0.0)


def _conv(x, w, b):
    K = w.shape[0]
    y = jnp.broadcast_to(b, x.shape)
    for k in range(K):
        y = y + w[k:k + 1, :] * _shift_down(x, K - 1 - k)
    return y


def _conv_back(x, w, dc):
    K = w.shape[0]
    dx = jnp.zeros_like(x)
    dw = []
    for k in range(K):
        dx = dx + w[k:k + 1, :] * _shift_up(dc, K - 1 - k)
        dw.append(_colsum(dc * _shift_down(x, K - 1 - k)))
    return dx, jnp.concatenate(dw, axis=0), _colsum(dc)


def _colwise(name, fn, cols, vecs, outs, pouts, tc):
    S, C = cols[0].shape
    nc_, nv, no = len(cols), len(vecs), len(outs)

    def body(*refs):
        res = fn(*[r[...] for r in refs[:nc_ + nv]])
        res = res if isinstance(res, (tuple, list)) else (res,)
        for r, v in zip(refs[nc_ + nv:], res):
            r[...] = v.astype(r.dtype)

    in_specs = [pl.BlockSpec((S, tc), lambda j: (0, j)) for _ in cols]
    in_specs += [pl.BlockSpec((v.shape[0], tc), lambda j: (0, j)) for v in vecs]
    out_specs = [pl.BlockSpec((S, tc), lambda j: (0, j)) for _ in outs] + [pl.BlockSpec((k, tc), lambda j: (0, j)) for k in pouts]
    out_shape = [jax.ShapeDtypeStruct((S, C), dt) for dt in outs] + [jax.ShapeDtypeStruct((k, C), F32) for k in pouts]
    return pl.pallas_call(
        body, name=name, grid=(C // tc,), in_specs=in_specs, out_specs=out_specs, out_shape=out_shape,
        compiler_params=_params(("parallel",)),
    )(*cols, *vecs)


_G0, _G1 = math.sqrt(2.0 / math.pi), 0.044715


def _gelu(g):
    th = jnp.tanh(_G0 * (g + _G1 * g * g * g))
    return 0.5 * g * (1.0 + th), th


def _ffn_act(gate_pre, up, w, b):
    act, _ = _gelu(_conv(gate_pre, w, b))
    return act * up


def _ffn_act_back(dact, gate_pre, up, w, b):
    g = _conv(gate_pre, w, b)
    ge, th = _gelu(g)
    dge = 0.5 * (1.0 + th) + 0.5 * g * (1.0 - th * th) * _G0 * (1.0 + 3.0 * _G1 * g * g)
    dup = dact * ge
    dgate_pre, dw, db = _conv_back(gate_pre, w, dact * up * dge)
    return dgate_pre, dup, dw, db


def _ssm_act(xbc, w, b):
    c = _conv(xbc, w, b)
    return c * _sigmoid(c)


def _ssm_act_back(dxc, xbc, w, b):
    c = _conv(xbc, w, b)
    sg = _sigmoid(c)
    return _conv_back(xbc, w, dxc * sg * (1.0 + c * (1.0 - sg)))


def _rope_tables(S):
    inv = 1.0 / (ROPE_THETA ** (jnp.arange(0, ROPE, 2, dtype=F32) / ROPE))
    ang = jnp.arange(S, dtype=F32)[:, None] * inv[None, :]
    cos, sin = jnp.cos(ang), jnp.sin(ang)
    return jnp.tile(cos, (1, 4)), jnp.tile(jnp.concatenate([-sin, sin], axis=1), (1, 2))


def _swap_halves(x):
    lane = lax.broadcasted_iota(I32, x.shape, 1)
    w = x.shape[1]
    return jnp.where((lane % ROPE) < ROPE // 2, pltpu.roll(x, w - ROPE // 2, 1), pltpu.roll(x, ROPE // 2, 1))


def _rot(x, cos2, sin2):
    return x * cos2 + _swap_halves(x) * sin2


def _rot_back(dy, cos2, sin2):
    return dy * cos2 + _swap_halves(dy * sin2)


def _mla_pack(cfg, q, kv, kr, cos2, sin2):
    S, H = cfg.S, cfg.H
    ts = _pick(S, 512, 8)

    def body(qn_ref, qr_ref, kn_ref, v_ref, kr_ref, c_ref, s_ref, Q_ref, K_ref, V_ref):
        h = pl.program_id(0)
        c2, s2 = c_ref[...], s_ref[...]
        Q_ref[0, :, 0:LANE] = qn_ref[...].astype(BF16)
        Q_ref[0, :, LANE:] = _rot(qr_ref[...], c2, s2).astype(BF16)
        K_ref[0, :, 0:LANE] = kn_ref[...].astype(BF16)
        krr = _rot(kr_ref[...], c2, s2)
        K_ref[0, :, LANE:] = jnp.where(h % 2 == 1, pltpu.roll(krr, ROPE, 1), krr).astype(BF16)
        V_ref[0] = v_ref[...].astype(BF16)

    blk = lambda f: pl.BlockSpec((ts, LANE), f)
    return pl.pallas_call(
        body, name="mla_pack", grid=(H, S // ts),
        in_specs=[blk(lambda h, i: (i, h)), blk(lambda h, i: (i, H + h // 2)), blk(lambda h, i: (i, h)),
                  blk(lambda h, i: (i, H + h)), blk(lambda h, i: (i, 0)), blk(lambda h, i: (i, 0)), blk(lambda h, i: (i, 0))],
        out_specs=[pl.BlockSpec((1, ts, 2 * LANE), lambda h, i: (h, i, 0)), pl.BlockSpec((1, ts, 2 * LANE), lambda h, i: (h, i, 0)),
                   pl.BlockSpec((1, ts, LANE), lambda h, i: (h, i, 0))],
        out_shape=[jax.ShapeDtypeStruct((H, S, 2 * LANE), BF16), jax.ShapeDtypeStruct((H, S, 2 * LANE), BF16),
                   jax.ShapeDtypeStruct((H, S, LANE), BF16)],
        compiler_params=_params(("parallel", "parallel")),
    )(q, q, kv, kv, kr, cos2, sin2)


def _mla_unpack(cfg, dQ, dK, dV, cos2, sin2):
    S, H = cfg.S, cfg.H
    ts = _pick(S, 256, 8)

    def body(dQ_ref, dK_ref, dV_ref, c_ref, s_ref, dq_ref, dkv_ref, dkr_ref):
        c2, s2 = c_ref[...], s_ref[...]
        lo = lax.broadcasted_iota(I32, (ts, LANE), 1) < ROPE
        tk = jnp.zeros((ts, LANE), F32)
        for h in range(H):
            dq_ref[:, h * LANE:(h + 1) * LANE] = dQ_ref[h, :, 0:LANE].astype(BF16)
            dkv_ref[:, h * LANE:(h + 1) * LANE] = dK_ref[h, :, 0:LANE].astype(BF16)
            dkv_ref[:, (H + h) * LANE:(H + h + 1) * LANE] = dV_ref[h].astype(BF16)
            own = lo if h % 2 == 0 else jnp.logical_not(lo)
            tk = tk + jnp.where(own, dK_ref[h, :, LANE:], 0.0)
        for j in range(H // 2):
            dr = dQ_ref[2 * j, :, LANE:] + dQ_ref[2 * j + 1, :, LANE:]
            dq_ref[:, (H + j) * LANE:(H + j + 1) * LANE] = _rot_back(dr, c2, s2).astype(BF16)
        dkr_rot = jnp.where(lo, tk + pltpu.roll(tk, ROPE, 1), 0.0)
        dkr_ref[...] = _rot_back(dkr_rot, c2, s2).astype(BF16)

    tab = pl.BlockSpec((ts, LANE), lambda i: (i, 0))
    return pl.pallas_call(
        body, name="mla_unpack", grid=(S // ts,),
        in_specs=[pl.BlockSpec((H, ts, 2 * LANE), lambda i: (0, i, 0)), pl.BlockSpec((H, ts, 2 * LANE), lambda i: (0, i, 0)),
                  pl.BlockSpec((H, ts, LANE), lambda i: (0, i, 0)), tab, tab],
        out_specs=[pl.BlockSpec((ts, cfg.QW), lambda i: (i, 0)), pl.BlockSpec((ts, cfg.KVW), lambda i: (i, 0)), tab],
        out_shape=[jax.ShapeDtypeStruct((S, cfg.QW), BF16), jax.ShapeDtypeStruct((S, cfg.KVW), BF16),
                   jax.ShapeDtypeStruct((S, LANE), BF16)],
        compiler_params=_params(("parallel",)),
    )(dQ, dK, dV, cos2, sin2)


_ATT_T = 256
_ATT_HB = 2
_ATT_SCALE = (NOPE + ROPE) ** -0.5


def _diag_mask(transposed=False):
    r = lax.broadcasted_iota(I32, (_ATT_T, _ATT_T), 0) // CHUNK
    c = lax.broadcasted_iota(I32, (_ATT_T, _ATT_T), 1) // CHUNK
    return r <= c if transposed else c <= r


def _row_form(col):
    return jnp.broadcast_to(col, (col.shape[0], LANE)).T[0:8, :]


def _attn_fwd(cfg, Q, K, V):
    S, H, T, HB = cfg.S, cfg.H, _ATT_T, _ATT_HB

    def body(q_ref, k_ref, v_ref, o_ref, lse_ref, lse_t_ref):
        qi = pl.program_id(1)

        def head_step(b, kb, carry, mask):
            m, l, acc = carry
            ks = pl.multiple_of(kb * T, T)
            s = _dot(q_ref[b], k_ref[b, pl.ds(ks, T), :], NT) * _ATT_SCALE
            if mask is not None:
                s = jnp.where(mask, s, -1e30)
            m_new = jnp.maximum(m, jnp.max(s, axis=1, keepdims=True))
            p = jnp.exp(s - m_new)
            alpha = jnp.exp(m - m_new)
            l = alpha * l + jnp.sum(p, axis=1, keepdims=True)
            acc = alpha * acc + _dot(p.astype(BF16), v_ref[b, pl.ds(ks, T), :])
            return m_new, l, acc

        def step(kb, carry, mask=None):
            return tuple(head_step(b, kb, carry[b], mask) for b in range(HB))

        init = (jnp.full((T, 1), -1e30, F32), jnp.zeros((T, 1), F32), jnp.zeros((T, VH), F32))
        done = step(qi, lax.fori_loop(0, qi, step, (init,) * HB), _diag_mask())
        for b, (m, l, acc) in enumerate(done):
            o_ref[:, b * LANE:(b + 1) * LANE] = acc / l
            lse = m + jnp.log(l)
            lse_ref[:, b * LANE:(b + 1) * LANE] = jnp.broadcast_to(lse, (T, LANE))
            lse_t_ref[b] = _row_form(lse)

    return pl.pallas_call(
        body, name="attn_fwd", grid=(H // HB, S // T),
        in_specs=[pl.BlockSpec((HB, T, 2 * LANE), lambda h, i: (h, i, 0)), pl.BlockSpec((HB, S, 2 * LANE), lambda h, i: (h, 0, 0)),
                  pl.BlockSpec((HB, S, LANE), lambda h, i: (h, 0, 0))],
        out_specs=[pl.BlockSpec((T, HB * LANE), lambda h, i: (i, h)), pl.BlockSpec((T, HB * LANE), lambda h, i: (i, h)),
                   pl.BlockSpec((HB, 8, T), lambda h, i: (h, 0, i))],
        out_shape=[jax.ShapeDtypeStruct((S, H * LANE), F32), jax.ShapeDtypeStruct((S, H * LANE), F32),
                   jax.ShapeDtypeStruct((H, 8, S), F32)],
        compiler_params=_params(("parallel", "parallel")),
    )(Q, K, V)


def _attn_dq(cfg, Q, K, V, do, o, lse):
    S, H, T, HB = cfg.S, cfg.H, _ATT_T, _ATT_HB

    def body(q_ref, k_ref, v_ref, do_ref, o_ref, lse_ref, dq_ref, dl_t_ref):
        qi = pl.program_id(1)
        do = [do_ref[:, b * LANE:(b + 1) * LANE] for b in range(HB)]
        delta = [jnp.sum(do[b] * o_ref[:, b * LANE:(b + 1) * LANE], axis=1, keepdims=True) for b in range(HB)]
        dob = [d.astype(BF16) for d in do]

        def head_step(b, kb, dq, mask):
            ks = pl.multiple_of(kb * T, T)
            k = k_ref[b, pl.ds(ks, T), :]
            s = _dot(q_ref[b], k, NT) * _ATT_SCALE
            if mask is not None:
                s = jnp.where(mask, s, -1e30)
            p = jnp.exp(s - lse_ref[:, b * LANE:b * LANE + 1])
            dp = _dot(dob[b], v_ref[b, pl.ds(ks, T), :], NT)
            ds = p * (dp - delta[b]) * _ATT_SCALE
            return dq + _dot(ds.astype(BF16), k)

        def step(kb, dqs, mask=None):
            return tuple(head_step(b, kb, dqs[b], mask) for b in range(HB))

        dqs = step(qi, lax.fori_loop(0, qi, step, (jnp.zeros((T, 2 * LANE), F32),) * HB), _diag_mask())
        for b in range(HB):
            dq_ref[b] = dqs[b]
            dl_t_ref[b] = _row_form(delta[b])

    col = pl.BlockSpec((T, HB * LANE), lambda h, i: (i, h))
    return pl.pallas_call(
        body, name="attn_dq", grid=(H // HB, S // T),
        in_specs=[pl.BlockSpec((HB, T, 2 * LANE), lambda h, i: (h, i, 0)), pl.BlockSpec((HB, S, 2 * LANE), lambda h, i: (h, 0, 0)),
                  pl.BlockSpec((HB, S, LANE), lambda h, i: (h, 0, 0)), col, col, col],
        out_specs=[pl.BlockSpec((HB, T, 2 * LANE), lambda h, i: (h, i, 0)), pl.BlockSpec((HB, 8, T), lambda h, i: (h, 0, i))],
        out_shape=[jax.ShapeDtypeStruct((H, S, 2 * LANE), F32), jax.ShapeDtypeStruct((H, 8, S), F32)],
        compiler_params=_params(("parallel", "parallel")),
    )(Q, K, V, do, o, lse)


def _attn_dkv(cfg, Q, K, V, do, lse_t, delta_t):
    S, H, T, HB = cfg.S, cfg.H, _ATT_T, _ATT_HB
    nq = S // T

    def body(q_ref, k_ref, v_ref, do_ref, lse_ref, dl_ref, dk_ref, dv_ref):
        kb = pl.program_id(1)

        def head_step(b, qi, carry, mask):
            dk, dv = carry
            qs = pl.multiple_of(qi * T, T)
            q = q_ref[b, pl.ds(qs, T), :]
            dob = do_ref[pl.ds(qs, T), b * LANE:(b + 1) * LANE].astype(BF16)
            s = _dot(k_ref[b], q, NT) * _ATT_SCALE
            if mask is not None:
                s = jnp.where(mask, s, -1e30)
            p = jnp.exp(s - lse_ref[b, 0:1, pl.ds(qs, T)])
            dv = dv + _dot(p.astype(BF16), dob)
            dp = _dot(v_ref[b], dob, NT)
            ds = p * (dp - dl_ref[b, 0:1, pl.ds(qs, T)]) * _ATT_SCALE
            dk = dk + _dot(ds.astype(BF16), q)
            return dk, dv

        def step(qi, carry, mask=None):
            return tuple(head_step(b, qi, carry[b], mask) for b in range(HB))

        zero = (jnp.zeros((T, 2 * LANE), F32), jnp.zeros((T, VH), F32))
        done = lax.fori_loop(kb + 1, nq, step, step(kb, (zero,) * HB, _diag_mask(transposed=True)))
        for b, (dk, dv) in enumerate(done):
            dk_ref[b] = dk
            dv_ref[b] = dv

    row = pl.BlockSpec((HB, 8, S), lambda h, j: (h, 0, 0))
    return pl.pallas_call(
        body, name="attn_dkv", grid=(H // HB, S // T),
        in_specs=[pl.BlockSpec((HB, S, 2 * LANE), lambda h, j: (h, 0, 0)), pl.BlockSpec((HB, T, 2 * LANE), lambda h, j: (h, j, 0)),
                  pl.BlockSpec((HB, T, LANE), lambda h, j: (h, j, 0)), pl.BlockSpec((S, HB * LANE), lambda h, j: (0, h)), row, row],
        out_specs=[pl.BlockSpec((HB, T, 2 * LANE), lambda h, j: (h, j, 0)), pl.BlockSpec((HB, T, LANE), lambda h, j: (h, j, 0))],
        out_shape=[jax.ShapeDtypeStruct((H, S, 2 * LANE), F32), jax.ShapeDtypeStruct((H, S, LANE), F32)],
        compiler_params=_params(("parallel", "parallel")),
    )(Q, K, V, do, lse_t, delta_t)


def _expand_matrix(cfg):
    r = lax.broadcasted_iota(I32, (LANE, cfg.INNER), 0)
    c = lax.broadcasted_iota(I32, (LANE, cfg.INNER), 1)
    return (r == c // HP).astype(F32)


def _softplus(x):
    return jnp.maximum(x, 0.0) + jnp.log(1.0 + jnp.exp(-jnp.abs(x)))


def _ssd_prep(cfg, dt_raw, dt_bias_pad, a_log_pad, expand):
    HS = cfg.HS

    def fn(raw, bias, alog, E):
        heads = lax.broadcasted_iota(I32, raw.shape, 1) < HS
        dt = jnp.where(heads, _softplus(raw + bias), 0.0)
        a = dt * jnp.where(heads[0:1], -jnp.exp(alog), 0.0)
        return dt, a, _dot(dt, E, precision=HI), _dot(a, E, precision=HI)

    return _rowwise("ssd_prep", fn, [dt_raw], [dt_bias_pad, a_log_pad, expand],
                    [(LANE, F32), (LANE, F32), (cfg.INNER, F32), (cfg.INNER, F32)], [], _pick(cfg.S, 512, 8))


def _tril(T):
    return lax.broadcasted_iota(I32, (T, T), 0) >= lax.broadcasted_iota(I32, (T, T), 1)


def _ssd_fwd(cfg, xc, dt_exp, a_exp, a_small, dskip_exp):
    S, T, INNER, G, NPAIR = cfg.S, cfg.T, cfg.INNER, cfg.G, cfg.NPAIR
    NC = S // T

    def body(xc_ref, dte_ref, ae_ref, as_ref, dsk_ref, y_ref, hin_ref, ht_ref):
        @pl.when(pl.program_id(0) == 0)
        def _():
            ht_ref[...] = jnp.zeros_like(ht_ref)

        tril = _tril(T)
        tri = tril.astype(F32)
        acs_s = _dot(tri, as_ref[...], precision=HI)
        acs_e = _dot(tri, ae_ref[...], precision=HI)
        acs_t = acs_s.T
        lo = lax.broadcasted_iota(I32, (T, LANE), 1) < HP
        for g in range(G):
            Bb = xc_ref[:, INNER + g * NST:INNER + (g + 1) * NST].astype(BF16)
            Cb = xc_ref[:, INNER + (G + g) * NST:INNER + (G + g + 1) * NST].astype(BF16)
            Gm = _dot(Cb, Bb, NT)
            for j in range(g * NPAIR // G, (g + 1) * NPAIR // G):
                sl = slice(j * LANE, (j + 1) * LANE)
                Xp = xc_ref[:, sl]
                Xdt = Xp * dte_ref[:, sl]
                Xb = Xdt.astype(BF16)
                acs_p = acs_e[:, sl]
                last = acs_p[T - 1:T, :]
                Hin = ht_ref[j]
                hin_ref[0, j] = Hin
                yd = []
                for e in (0, 1):
                    h = 2 * j + e
                    Lm = jnp.exp(jnp.where(tril, acs_s[:, h:h + 1] - acs_t[h:h + 1, :], -1e30))
                    yd.append(_dot((Gm * Lm).astype(BF16), Xb))
                y_off = _dot(Cb, Hin.astype(BF16)) * jnp.exp(acs_p)
                y_ref[:, sl] = jnp.where(lo, yd[0], yd[1]) + y_off + Xp * dsk_ref[:, sl]
                st = _dot(Bb, (Xdt * jnp.exp(last - acs_p)).astype(BF16), TN)
                ht_ref[j] = jnp.exp(last) * Hin + st

    rows = lambda w: pl.BlockSpec((T, w), lambda c: (c, 0))
    return pl.pallas_call(
        body, name="ssd_fwd", grid=(NC,),
        in_specs=[rows(cfg.CONVCH), rows(INNER), rows(INNER), rows(LANE), pl.BlockSpec((1, INNER), lambda c: (0, 0))],
        out_specs=[rows(INNER), pl.BlockSpec((1, NPAIR, NST, LANE), lambda c: (c, 0, 0, 0))],
        out_shape=[jax.ShapeDtypeStruct((S, INNER), F32), jax.ShapeDtypeStruct((NC, NPAIR, NST, LANE), F32)],
        scratch_shapes=[pltpu.VMEM((NPAIR, NST, LANE), F32)],
        compiler_params=_params(("arbitrary",)),
    )(xc, dt_exp, a_exp, a_small, dskip_exp)


def _ssd_bwd(cfg, dy, xc, dt_exp, a_exp, a_small, dskip_exp, hin, dt_raw, dt_bias_pad, a_log_pad, expand):
    S, T, INNER, G, NPAIR, HS = cfg.S, cfg.T, cfg.INNER, cfg.G, cfg.NPAIR, cfg.HS
    NC = S // T

    def body(dy_ref, xc_ref, dte_ref, ae_ref, as_ref, dsk_ref, hin_ref, raw_ref, bias_ref, alog_ref, e_ref,
             dxc_ref, draw_ref, dbias_ref, dalog_ref, dskip_ref, dht_ref, cols_ref, rows_ref, dacs_ref, ddt_ref):
        first = pl.program_id(0) == 0

        @pl.when(first)
        def _():
            dht_ref[...] = jnp.zeros_like(dht_ref)

        tril = _tril(T)
        tri = tril.astype(F32)
        a_s = as_ref[...]
        acs_s = _dot(tri, a_s, precision=HI)
        acs_e = _dot(tri, ae_ref[...], precision=HI)
        acs_t = acs_s.T
        lo = lax.broadcasted_iota(I32, (T, LANE), 1) < HP
        last_row = lax.broadcasted_iota(I32, (T, LANE), 0) == T - 1
        cols_ref[...] = jnp.zeros_like(cols_ref)
        rows_ref[...] = jnp.zeros_like(rows_ref)
        dsk_parts = []
        for g in range(G):
            bsl = slice(INNER + g * NST, INNER + (g + 1) * NST)
            csl = slice(INNER + (G + g) * NST, INNER + (G + g + 1) * NST)
            Bb = xc_ref[:, bsl].astype(BF16)
            Cb = xc_ref[:, csl].astype(BF16)
            Gm = _dot(Cb, Bb, NT)
            dG = jnp.zeros((T, T), F32)
            dB = jnp.zeros((T, NST), F32)
            dC = jnp.zeros((T, NST), F32)
            for j in range(g * NPAIR // G, (g + 1) * NPAIR // G):
                sl = slice(j * LANE, (j + 1) * LANE)
                Xp = xc_ref[:, sl]
                dtp = dte_ref[:, sl]
                Xdt = Xp * dtp
                Xb = Xdt.astype(BF16)
                acs_p = acs_e[:, sl]
                last = acs_p[T - 1:T, :]
                e_p, dec, cd = jnp.exp(acs_p), jnp.exp(last - acs_p), jnp.exp(last)
                Hin = hin_ref[0, j]
                Hb = Hin.astype(BF16)
                dHn = dht_ref[j]
                dHb = dHn.astype(BF16)
                dYp = dy_ref[:, sl]
                z = _dot(Cb, Hb)
                dz = (dYp * e_p).astype(BF16)
                dacs_p = dYp * z * e_p
                dC = dC + _dot(dz, Hb, NT)
                dHin = _dot(Cb, dz, TN) + cd * dHn
                dlast = _colsum(dHn * Hin) * cd
                qv = _dot(Bb, dHb)
                dXdt = qv * dec
                ddec = qv * Xdt * dec
                dacs_p = dacs_p - ddec
                dlast = dlast + _colsum(ddec)
                dB = dB + _dot((Xdt * dec).astype(BF16), dHb, NT)
                for e in (0, 1):
                    h = 2 * j + e
                    Lm = jnp.exp(jnp.where(tril, acs_s[:, h:h + 1] - acs_t[h:h + 1, :], -1e30))
                    Mh = Gm * Lm
                    dYe = jnp.where(lo if e == 0 else jnp.logical_not(lo), dYp, 0.0).astype(BF16)
                    dM = _dot(dYe, Xb, NT)
                    dXdt = dXdt + _dot(Mh.astype(BF16), dYe, TN)
                    W = dM * Mh
                    cols_ref[:, h:h + 1] = jnp.sum(W, axis=1, keepdims=True)
                    rows_ref[h:h + 1, :] = _colsum(W)
                    dG = dG + dM * Lm
                dacs_ref[:, sl] = dacs_p + jnp.where(last_row, dlast, 0.0)
                ddt_ref[:, sl] = dXdt * Xp
                dxc_ref[:, sl] = dXdt * dtp + dYp * dsk_ref[:, sl]
                dsk_parts.append(_colsum(dYp * Xp))
                dht_ref[j] = dHin
            dGb = dG.astype(BF16)
            dxc_ref[:, bsl] = dB + _dot(dGb, Cb, TN)
            dxc_ref[:, csl] = dC + _dot(dGb, Bb)
        E = e_ref[...]
        dacs_s = cols_ref[...] - rows_ref[...].T + _dot(dacs_ref[...], E, NT, precision=HI)
        da = _dot(tri, dacs_s, TN, precision=HI)
        heads = lax.broadcasted_iota(I32, (1, LANE), 1) < HS
        A = jnp.where(heads, -jnp.exp(alog_ref[...]), 0.0)
        ddt = _dot(ddt_ref[...], E, NT, precision=HI) + da * A
        draw = jnp.where(heads, ddt * _sigmoid(raw_ref[...] + bias_ref[...]), 0.0)
        draw_ref[...] = draw
        dsk = _dot(jnp.broadcast_to(jnp.concatenate(dsk_parts, axis=1), (8, INNER)), E, NT, precision=HI)[0:1]
        for ref, val in ((dbias_ref, _colsum(draw)), (dalog_ref, _colsum(da * a_s)), (dskip_ref, dsk)):
            @pl.when(first)
            def _():
                ref[...] = val

            @pl.when(jnp.logical_not(first))
            def _():
                ref[...] += val

    rows = lambda w: pl.BlockSpec((T, w), lambda c: (NC - 1 - c, 0))
    vec = lambda w: pl.BlockSpec((1, w), lambda c: (0, 0))
    return pl.pallas_call(
        body, name="ssd_bwd", grid=(NC,),
        in_specs=[rows(INNER), rows(cfg.CONVCH), rows(INNER), rows(INNER), rows(LANE), vec(INNER),
                  pl.BlockSpec((1, NPAIR, NST, LANE), lambda c: (NC - 1 - c, 0, 0, 0)), rows(LANE), vec(LANE), vec(LANE),
                  pl.BlockSpec((LANE, INNER), lambda c: (0, 0))],
        out_specs=[rows(cfg.CONVCH), rows(LANE), vec(LANE), vec(LANE), vec(LANE)],
        out_shape=[jax.ShapeDtypeStruct((S, cfg.CONVCH), F32), jax.ShapeDtypeStruct((S, LANE), F32)]
        + [jax.ShapeDtypeStruct((1, LANE), F32)] * 3,
        scratch_shapes=[pltpu.VMEM((NPAIR, NST, LANE), F32), pltpu.VMEM((T, LANE), F32), pltpu.VMEM((LANE, T), F32),
                        pltpu.VMEM((T, INNER), F32), pltpu.VMEM((T, INNER), F32)],
        compiler_params=_params(("arbitrary",)),
    )(dy, xc, dt_exp, a_exp, a_small, dskip_exp, hin, dt_raw, dt_bias_pad, a_log_pad, expand)


def _ssd_post(cfg, y, z, norm_g):
    W = cfg.INNER // cfg.G

    def fn(y, z, g):
        yz = y * z * _sigmoid(z)
        return jnp.concatenate([yz[:, i * W:(i + 1) * W] * _rs(yz[:, i * W:(i + 1) * W]) for i in range(cfg.G)], axis=1) * g

    return _rowwise("ssd_post", fn, [y, z], [norm_g], [(cfg.INNER, BF16)], [], _pick(cfg.S, 256, 8))[0]


def _ssd_post_bwd(cfg, db, y, z, norm_g):
    W = cfg.INNER // cfg.G

    def fn(db, y, z, g):
        sg = _sigmoid(z)
        yz = y * z * sg
        dn = db * g
        dyz, nh = [], []
        for i in range(cfg.G):
            seg = yz[:, i * W:(i + 1) * W]
            r = _rs(seg)
            nh.append(seg * r)
            dyz.append(_rms_back(nh[-1], r, dn[:, i * W:(i + 1) * W]))
        dyz = jnp.concatenate(dyz, axis=1)
        return dyz * z * sg, dyz * y * sg * (1.0 + z * (1.0 - sg)), _colsum(db * jnp.concatenate(nh, axis=1))

    return _rowwise("ssd_post_bwd", fn, [db, y, z], [norm_g], [(cfg.INNER, F32), (cfg.INNER, F32)], [(1, cfg.INNER)],
                    _pick(cfg.S, 256, 8))


def _local_grads(cfg, x, tgt, W, sp, ffn_weights=None, ffn_grads_ready=None):
    S, D, H, INNER = cfg.S, cfg.D, cfg.H, cfg.INNER
    ts = _pick(S, 256, 8)
    tc = 256

    xn = _rowwise("rms_pre", lambda x, g: x * _rs(x) * g, [x], [sp["mix_pre_g"]], [(D, BF16)], [], ts)[0]
    u = _matmul("mm_in", xn, W["w_in"], "nn", F32)
    c_q, c_kv = u[:, :cfg.QL], u[:, cfg.QL:cfg.o_kr]
    kr = u[:, cfg.o_kr:cfg.o_z]
    z = u[:, cfg.o_z:cfg.o_xbc]
    xbc = u[:, cfg.o_xbc:cfg.o_dt]
    dt_raw = u[:, cfg.o_dt:]

    cqn = _rowwise("rms_q", lambda x, g: x * _rs(x) * g, [c_q], [sp["q_norm_g"]], [(cfg.QL, BF16)], [], ts)[0]
    ckvn = _rowwise("rms_kv", lambda x, g: x * _rs(x) * g, [c_kv], [sp["kv_norm_g"]], [(cfg.KVL, BF16)], [], ts)[0]
    q = _matmul("mm_uq", cqn, W["w_uq"], "nn", F32)
    kv = _matmul("mm_ukv", ckvn, W["w_ukv"], "nn", F32)
    cos2, sin2 = _rope_tables(S)
    Qh, Kh, Vh = _mla_pack(cfg, q, kv, kr, cos2, sin2)
    a_out, lse, lse_t = _attn_fwd(cfg, Qh, Kh, Vh)

    pad = lambda v: jnp.pad(v, ((0, 0), (0, LANE - v.shape[1])))
    expand = _expand_matrix(cfg)
    dt_bias_pad, a_log_pad = pad(sp["dt_bias"]), pad(sp["a_log"])
    dskip_exp = jnp.repeat(sp["d_skip"], HP, axis=1)
    xc = _colwise("ssm_act", _ssm_act, [xbc], [sp["ssm_conv_w"], sp["ssm_conv_b"]], [F32], [], tc)[0]
    dt_s, a_s, dt_exp, a_exp = _ssd_prep(cfg, dt_raw, dt_bias_pad, a_log_pad, expand)
    y_ssd, hin = _ssd_fwd(cfg, xc, dt_exp, a_exp, a_s, dskip_exp)
    b_out = _ssd_post(cfg, y_ssd, z, sp["ssm_norm_g"])

    ab_out = jnp.concatenate([a_out.astype(BF16), b_out], axis=1)
    if ffn_weights is not None:
        sp = dict(sp, mix_post_g=sp["mix_post_g"] + ffn_weights.pass_on(ab_out)[0, 0])
    mix = _matmul("mm_out", ab_out, W["w_out"], "nn", F32)

    def mid(x, mix, g_mp, g_fp):
        x1 = x + mix * _rs(mix) * g_mp
        return x1, x1 * _rs(x1) * g_fp

    x1, h2 = _rowwise("fwd_mid", mid, [x, mix], [sp["mix_post_g"], sp["ffn_pre_g"]], [(D, F32), (D, BF16)], [], ts)
    if ffn_weights is not None:
        W = dict(W, **ffn_weights.arrived(h2))
    gate_pre = _matmul("mm_gate", h2, W["w_gate"], "nn", F32, chips=True)
    up = _matmul("mm_up", h2, W["w_up"], "nn", F32, chips=True)
    act = _colwise("ffn_act", _ffn_act, [gate_pre, up], [sp["ffn_conv_w"], sp["ffn_conv_b"]], [BF16], [], tc)[0]
    f = _matmul("mm_down", act, W["w_down"], "nn", F32)

    def final(x1, f, t, g):
        r = _rs(f)
        fh = f * r
        err = x1 + fh * g - t
        loss = 0.5 * jnp.sum(jnp.mean(err * err, axis=-1, keepdims=True), axis=0, keepdims=True)
        dy = err * (1.0 / D)
        return dy, _rms_back(fh, r, dy * g), _colsum(dy * fh), loss

    dy, df, g_ffn_post, loss = _rowwise("final", final, [x1, f, tgt], [sp["ffn_post_g"]], [(D, F32), (D, BF16)],
                                        [(1, D), (1, LANE)], ts)
    gW = {}
    dact = _matmul("mm_down_dx", df, W["w_down"], "nt", F32)
    gW["w_down"] = _matmul("mm_down_dw", act, df, "tn", BF16)
    dgate, dup, g_ffn_conv_w, g_ffn_conv_b = _colwise(
        "ffn_act_bwd", _ffn_act_back, [dact, gate_pre, up], [sp["ffn_conv_w"], sp["ffn_conv_b"]], [BF16, BF16], [FFN_K, 1], tc)
    dh2 = _matmul("mm_gu_dx", dgate, W["w_gate"], "nt", F32, dup, W["w_up"], chips=True)
    gW["w_gate"] = _matmul("mm_gate_dw", h2, dgate, "tn", BF16, chips=True)
    gW["w_up"] = _matmul("mm_up_dw", h2, dup, "tn", BF16, chips=True)

    def mid_back(dy, dh2, x1, mix, g_mp, g_fp):
        r2 = _rs(x1)
        xh = x1 * r2
        dx1 = dy + _rms_back(xh, r2, dh2 * g_fp)
        r1 = _rs(mix)
        mh = mix * r1
        return dx1, _rms_back(mh, r1, dx1 * g_mp), _colsum(dh2 * xh), _colsum(dx1 * mh)

    dx1, dmix, g_ffn_pre, g_mix_post = _rowwise("bwd_mid", mid_back, [dy, dh2, x1, mix], [sp["mix_post_g"], sp["ffn_pre_g"]],
                                                [(D, F32), (D, BF16)], [(1, D), (1, D)], ts)
    dab_out = _matmul("mm_out_dx", dmix, W["w_out"], "nt", F32)
    db_out = dab_out[:, cfg.MLAW:]
    gW["w_out"] = _matmul("mm_out_dw", ab_out, dmix, "tn", BF16)
    if ffn_grads_ready is not None:
        token = ffn_grads_ready({n: gW[n] for n in ("w_down", "w_gate", "w_up", "w_out")})
        sp = dict(sp, ssm_norm_g=sp["ssm_norm_g"] + token[0, 0])

    dy_ssd, dz, g_ssm_norm = _ssd_post_bwd(cfg, db_out, y_ssd, z, sp["ssm_norm_g"])
    dxc, ddt_raw, g_dt_bias, g_a_log, g_d_skip = _ssd_bwd(cfg, dy_ssd, xc, dt_exp, a_exp, a_s, dskip_exp, hin, dt_raw,
                                                          dt_bias_pad, a_log_pad, expand)
    dxbc, g_ssm_conv_w, g_ssm_conv_b = _colwise("ssm_act_bwd", _ssm_act_back, [dxc, xbc], [sp["ssm_conv_w"], sp["ssm_conv_b"]],
                                                [BF16], [SSM_K, 1], tc)

    dQ, delta_t = _attn_dq(cfg, Qh, Kh, Vh, dab_out, a_out, lse)
    dK, dV = _attn_dkv(cfg, Qh, Kh, Vh, dab_out, lse_t, delta_t)
    dq, dkv, dkr = _mla_unpack(cfg, dQ, dK, dV, cos2, sin2)
    dcqn = _matmul("mm_uq_dx", dq, W["w_uq"], "nt", F32)
    dckvn = _matmul("mm_ukv_dx", dkv, W["w_ukv"], "nt", F32)
    gW["w_uq"] = _matmul("mm_uq_dw", cqn, dq, "tn", BF16)
    gW["w_ukv"] = _matmul("mm_ukv_dw", ckvn, dkv, "tn", BF16)

    def rms_back(x, dy, g):
        r = _rs(x)
        xh = x * r
        return _rms_back(xh, r, dy * g), _colsum(dy * xh)

    dc_q, g_q_norm = _rowwise("rms_q_bwd", rms_back, [c_q, dcqn], [sp["q_norm_g"]], [(cfg.QL, BF16)], [(1, cfg.QL)], ts)
    dc_kv, g_kv_norm = _rowwise("rms_kv_bwd", rms_back, [c_kv, dckvn], [sp["kv_norm_g"]], [(cfg.KVL, BF16)], [(1, cfg.KVL)], ts)

    du = jnp.concatenate([dc_q, dc_kv, dkr, dz.astype(BF16), dxbc, ddt_raw.astype(BF16)], axis=1)
    dxn = _matmul("mm_in_dx", du, W["w_in"], "nt", F32)
    gW["w_in"] = _matmul("mm_in_dw", xn, du, "tn", BF16)

    def first_back(dx1, dxn, x, g):
        r = _rs(x)
        xh = x * r
        return dx1 + _rms_back(xh, r, dxn * g), _colsum(dxn * xh)

    grad_x, g_mix_pre = _rowwise("bwd_first", first_back, [dx1, dxn, x], [sp["mix_pre_g"]], [(D, F32)], [(1, D)], ts)

    gs = dict(mix_pre_g=g_mix_pre, q_norm_g=g_q_norm, kv_norm_g=g_kv_norm, ssm_conv_w=g_ssm_conv_w, ssm_conv_b=g_ssm_conv_b,
              dt_bias=g_dt_bias[:, :cfg.HS], a_log=g_a_log[:, :cfg.HS], d_skip=g_d_skip[:, :cfg.HS], ssm_norm_g=g_ssm_norm,
              mix_post_g=g_mix_post, ffn_pre_g=g_ffn_pre, ffn_conv_w=g_ffn_conv_w, ffn_conv_b=g_ffn_conv_b,
              ffn_post_g=g_ffn_post)
    return loss, grad_x, gW, gs


def _to_kernel_layout(cfg, name, w):
    if name == "w_in":
        a = cfg.o_kr + ROPE
        return jnp.concatenate([w[:, :a], jnp.zeros((w.shape[0], LANE - ROPE), w.dtype), w[:, a:],
                                jnp.zeros((w.shape[0], LANE - cfg.HS), w.dtype)], axis=1)
    if name in ("w_uq", "w_ukv"):
        per = NOPE + (ROPE if name == "w_uq" else VH)
        return jnp.concatenate([w[:, h * per:h * per + NOPE] for h in range(cfg.H)]
                               + [w[:, h * per + NOPE:(h + 1) * per] for h in range(cfg.H)], axis=1)
    return w


def _from_kernel_layout(cfg, name, g):
    if name == "w_in":
        return jnp.concatenate([g[:, :cfg.o_kr + ROPE], g[:, cfg.o_z:cfg.o_dt + cfg.HS]], axis=1)
    if name in ("w_uq", "w_ukv"):
        second = ROPE if name == "w_uq" else VH
        base = cfg.H * NOPE
        parts = []
        for h in range(cfg.H):
            parts += [g[:, h * NOPE:(h + 1) * NOPE], g[:, base + h * second:base + (h + 1) * second]]
        return jnp.concatenate(parts, axis=1)
    return g


def _cols_to_chips(w):
    r, c = w.shape
    return w.reshape(r, N_CHIPS, c // N_CHIPS).transpose(1, 0, 2)


def _chips_to_cols(g):
    k, r, cs = g.shape
    return g.transpose(1, 0, 2).reshape(r, k * cs)


_CHIP_MAJOR = ("w_gate", "w_up")
_RELAYOUT = ("w_in", "w_uq", "w_ukv")
_LAYOUT_ROWS = 256


def _gathered_to_kernel(cfg, name, wg):
    if name in _CHIP_MAJOR:
        return wg
    if name not in _RELAYOUT:
        return wg.reshape(wg.shape[0] * wg.shape[1], wg.shape[2])
    _, rows, cs = wg.shape
    tr = _pick(rows, _LAYOUT_ROWS, 16)

    def body(w_ref, o_ref):
        o_ref[...] = _to_kernel_layout(cfg, name, jnp.concatenate([w_ref[k] for k in range(N_CHIPS)], axis=1))

    wide = jax.eval_shape(lambda w: _to_kernel_layout(cfg, name, w), jax.ShapeDtypeStruct((rows, N_CHIPS * cs), wg.dtype)).shape[1]
    return pl.pallas_call(
        body, name="layout_" + name, grid=(rows // tr,),
        in_specs=[pl.BlockSpec((N_CHIPS, tr, cs), lambda i: (0, i, 0))], out_specs=pl.BlockSpec((tr, wide), lambda i: (i, 0)),
        out_shape=jax.ShapeDtypeStruct((rows, wide), wg.dtype), compiler_params=_params(("parallel",)),
    )(wg)


def _grad_to_chips(cfg, name, g):
    if name in _CHIP_MAJOR:
        return g
    if name not in _RELAYOUT:
        return g.reshape(N_CHIPS, g.shape[0] // N_CHIPS, g.shape[1])
    rows, wide = g.shape
    tr = _pick(rows, _LAYOUT_ROWS, 16)
    cs = jax.eval_shape(lambda v: _from_kernel_layout(cfg, name, v), g).shape[1] // N_CHIPS

    def body(g_ref, o_ref):
        nat = _from_kernel_layout(cfg, name, g_ref[...])
        for k in range(N_CHIPS):
            o_ref[k] = nat[:, k * cs:(k + 1) * cs]

    return pl.pallas_call(
        body, name="layout_grad_" + name, grid=(rows // tr,),
        in_specs=[pl.BlockSpec((tr, wide), lambda i: (i, 0))], out_specs=pl.BlockSpec((N_CHIPS, tr, cs), lambda i: (0, i, 0)),
        out_shape=jax.ShapeDtypeStruct((N_CHIPS, rows, cs), g.dtype), compiler_params=_params(("parallel",)),
    )(g)


def _me():
    return lax.axis_index("x"), lax.axis_index("y"), lax.axis_index("c")


def _other_chips(x, y):
    return [(1 - x, y), (x, 1 - y), (1 - x, 1 - y)]


_ANY = pl.BlockSpec(memory_space=pl.ANY)


def _row_block(rows, cols, mult):
    return _pick(rows, max(mult, (1 << 19) // cols // mult * mult), mult)


def _scalar(v):
    return v.astype(I32).reshape(1)


def _stage_shard(name, w, chip):
    _, rs, cs = w.shape
    tr = _row_block(rs, cs, 16)

    def body(chip_ref, w_ref, o_ref):
        o_ref[...] = w_ref[...].astype(BF16)

    return pl.pallas_call(
        body, name="stage_" + name,
        grid_spec=pltpu.PrefetchScalarGridSpec(
            num_scalar_prefetch=1, grid=(rs // tr,),
            in_specs=[pl.BlockSpec((None, tr, cs), lambda i, chip_ref: (0, i, 0))],
            out_specs=pl.BlockSpec((None, tr, cs), lambda i, chip_ref: (chip_ref[0], i, 0))),
        out_shape=jax.ShapeDtypeStruct((N_CHIPS, rs, cs), BF16),
        compiler_params=_params(("parallel",)),
    )(_scalar(chip), w)


def _half(ref, k, half):
    h = ref.shape[1] // 2
    return ref.at[k, pl.ds(pl.multiple_of(half * h, 16), h), :]


def _allgather_weights(bufs):
    n = len(bufs)

    def body(*refs):
        outs, send_sems, recv_sems = refs[n:2 * n], refs[2 * n], refs[2 * n + 1]
        x, y, c = _me()
        chip = 2 * x + y
        sib = (x, y, 1 - c)
        chips = _other_chips(x, y)

        def copy(k, part, to):
            return pltpu.make_async_remote_copy(src_ref=part, dst_ref=part, send_sem=send_sems.at[k], recv_sem=recv_sems.at[k],
                                                device_id=to, device_id_type=MESH_ID)

        started = []
        for w, o_ref in enumerate(outs):
            for j, (cx, cy) in enumerate(chips):
                started.append(copy(6 * w + j, _half(o_ref, chip, c), (cx, cy, c)))
                started[-1].start()
        for w, o_ref in enumerate(outs):
            for j, (cx, cy) in enumerate(chips):
                theirs = _half(o_ref, 2 * cx + cy, c)
                copy(6 * w + j, theirs, sib).wait_recv()
                started.append(copy(6 * w + 3 + j, theirs, sib))
                started[-1].start()
        for w, o_ref in enumerate(outs):
            for j, (cx, cy) in enumerate(chips):
                copy(6 * w + 3 + j, _half(o_ref, 2 * cx + cy, 1 - c), sib).wait_recv()
        for cp in started:
            cp.wait_send()

    return pl.pallas_call(
        body, name="allgather_weights", in_specs=[_ANY] * n, out_specs=[_ANY] * n,
        out_shape=[jax.ShapeDtypeStruct(b.shape, b.dtype) for b in bufs],
        input_output_aliases={i: i for i in range(n)},
        scratch_shapes=[pltpu.SemaphoreType.DMA((6 * n,)), pltpu.SemaphoreType.DMA((6 * n,))],
    )(*bufs)


_HBM = pl.BlockSpec(memory_space=pltpu.HBM)
_SEM = pl.BlockSpec(memory_space=pltpu.SEMAPHORE)
_EFFECT = pltpu.SideEffectType.DATAFLOW_SIDE_EFFECTING


def _split_start(name, bufs, n_copies, copies):
    n = len(bufs)

    def body(*refs):
        for cp in copies(refs[:n], refs[n], refs[n + 1]):
            cp.start()
        refs[-1][...] = jnp.zeros_like(refs[-1])

    res = pl.pallas_call(
        body, name=name,
        out_shape=(pltpu.SemaphoreType.DMA((n_copies,)), pltpu.SemaphoreType.DMA((n_copies,)),
                   *[pltpu.HBM(b.shape, b.dtype) for b in bufs], jax.ShapeDtypeStruct((8, LANE), F32)),
        in_specs=[_HBM] * n, out_specs=(_SEM, _SEM, *[_HBM] * n, pl.BlockSpec(memory_space=pltpu.VMEM)),
        input_output_aliases={i: 2 + i for i in range(n)},
        compiler_params=pltpu.CompilerParams(has_side_effects=_EFFECT),
    )(*[pltpu.with_memory_space_constraint(b, pltpu.HBM) for b in bufs])
    return res[0], res[1], list(res[2:2 + n]), res[-1]


def _split_wait(name, send_sems, recv_sems, bufs, after, copies):
    n = len(bufs)

    def body(*refs):
        for cp in copies(refs[:n], refs[n], refs[n + 1]):
            cp.wait_send()
            cp.wait_recv()

    return list(pl.pallas_call(
        body, name=name, out_shape=[pltpu.HBM(b.shape, b.dtype) for b in bufs],
        in_specs=[_HBM] * n + [_SEM, _SEM, _ANY], out_specs=[_HBM] * n,
        input_output_aliases={i: i for i in range(n)},
        compiler_params=pltpu.CompilerParams(has_side_effects=_EFFECT),
    )(*bufs, send_sems, recv_sems, after))


def _gather_to_chips(bufs, send_sems, recv_sems):
    x, y, c = _me()
    return [pltpu.make_async_remote_copy(src_ref=_half(b, 2 * x + y, c), dst_ref=_half(b, 2 * x + y, c),
                                         send_sem=send_sems.at[3 * w + j], recv_sem=recv_sems.at[3 * w + j],
                                         device_id=(cx, cy, c), device_id_type=MESH_ID)
            for w, b in enumerate(bufs) for j, (cx, cy) in enumerate(_other_chips(x, y))]


def _gather_to_sibling(bufs, send_sems, recv_sems):
    x, y, c = _me()
    return [pltpu.make_async_remote_copy(src_ref=_half(b, 2 * cx + cy, c), dst_ref=_half(b, 2 * cx + cy, c),
                                         send_sem=send_sems.at[3 * w + j], recv_sem=recv_sems.at[3 * w + j],
                                         device_id=(x, y, 1 - c), device_id_type=MESH_ID)
            for w, b in enumerate(bufs) for j, (cx, cy) in enumerate(_other_chips(x, y))]


def _pair_exchange(name, grads):
    n = len(grads)

    def body(*refs):
        ins, outs, send_sems, recv_sems = refs[:n], refs[n:2 * n], refs[2 * n], refs[2 * n + 1]
        x, y, c = _me()
        cps = []
        for w, (g_ref, o_ref) in enumerate(zip(ins, outs)):
            h = o_ref.shape[1]
            src = g_ref.at[:, pl.ds(pl.multiple_of((1 - c) * h, 16), h), :]
            cps.append(pltpu.make_async_remote_copy(src_ref=src, dst_ref=o_ref, send_sem=send_sems.at[w], recv_sem=recv_sems.at[w],
                                                    device_id=(x, y, 1 - c), device_id_type=MESH_ID))
            cps[-1].start()
        for cp in cps:
            cp.wait()

    return pl.pallas_call(
        body, name="pair_exchange_" + name, in_specs=[_ANY] * n, out_specs=[_ANY] * n,
        out_shape=[jax.ShapeDtypeStruct((g.shape[0], g.shape[1] // 2, g.shape[2]), g.dtype) for g in grads],
        scratch_shapes=[pltpu.SemaphoreType.DMA((n,)), pltpu.SemaphoreType.DMA((n,))],
    )(*grads)


def _pair_sum(name, g, theirs, c):
    _, h, cs = theirs.shape
    tr = _row_block(h, cs, 16)
    nb = h // tr

    def body(c_ref, a_ref, b_ref, o_ref):
        o_ref[...] = (a_ref[...].astype(F32) + b_ref[...].astype(F32)).astype(o_ref.dtype)

    return pl.pallas_call(
        body, name="pair_sum_" + name,
        grid_spec=pltpu.PrefetchScalarGridSpec(
            num_scalar_prefetch=1, grid=(N_CHIPS, nb),
            in_specs=[pl.BlockSpec((None, tr, cs), lambda k, i, c_ref: (k, c_ref[0] * nb + i, 0)),
                      pl.BlockSpec((None, tr, cs), lambda k, i, c_ref: (k, i, 0))],
            out_specs=pl.BlockSpec((None, tr, cs), lambda k, i, c_ref: (k, i, 0))),
        out_shape=jax.ShapeDtypeStruct(theirs.shape, BF16),
        compiler_params=_params(("parallel", "parallel")),
    )(_scalar(c), g, theirs)


def _chip_copies(srcs, lands, send_sems, recv_sems):
    x, y, c = _me()
    return [pltpu.make_async_remote_copy(src_ref=s_ref.at[2 * cx + cy], dst_ref=l_ref.at[j], send_sem=send_sems.at[3 * w + j],
                                         recv_sem=recv_sems.at[3 * w + j], device_id=(cx, cy, c), device_id_type=MESH_ID)
            for w, (s_ref, l_ref) in enumerate(zip(srcs, lands)) for j, (cx, cy) in enumerate(_other_chips(x, y))]


def _chip_exchange_start(name, sums):
    n = len(sums)
    lands = [lax.empty((3,) + s.shape[1:], s.dtype) for s in sums]
    send_sems, recv_sems, bufs, token = _split_start(
        "chip_exchange_start_" + name, [*sums, *lands], 3 * n, lambda refs, ss, rs: _chip_copies(refs[:n], refs[n:], ss, rs))
    return send_sems, recv_sems, bufs[:n], bufs[n:], token


def _chip_exchange_wait(name, send_sems, recv_sems, sums, lands, after):
    n = len(sums)
    bufs = _split_wait("chip_exchange_wait_" + name, send_sems, recv_sems, [*sums, *lands], after,
                       lambda refs, ss, rs: _chip_copies(refs[:n], refs[n:], ss, rs))
    return bufs[:n], bufs[n:]


def _chip_exchange(name, sums):
    n = len(sums)

    def body(*refs):
        cps = _chip_copies(refs[:n], refs[n:2 * n], refs[2 * n], refs[2 * n + 1])
        for cp in cps:
            cp.start()
        for cp in cps:
            cp.wait()

    return pl.pallas_call(
        body, name="chip_exchange_" + name, in_specs=[_ANY] * n, out_specs=[_ANY] * n,
        out_shape=[jax.ShapeDtypeStruct((3,) + s.shape[1:], s.dtype) for s in sums],
        scratch_shapes=[pltpu.SemaphoreType.DMA((3 * n,)), pltpu.SemaphoreType.DMA((3 * n,))],
    )(*sums)


def _chip_sum(name, sums, theirs, chip):
    _, h, cs = sums.shape
    tr = _row_block(h, cs, 16)

    def body(chip_ref, s_ref, t_ref, o_ref):
        acc = s_ref[...].astype(F32)
        for k in range(3):
            acc = acc + t_ref[k].astype(F32)
        o_ref[...] = acc

    return pl.pallas_call(
        body, name="chip_sum_" + name,
        grid_spec=pltpu.PrefetchScalarGridSpec(
            num_scalar_prefetch=1, grid=(h // tr,),
            in_specs=[pl.BlockSpec((None, tr, cs), lambda i, chip_ref: (chip_ref[0], i, 0)),
                      pl.BlockSpec((3, tr, cs), lambda i, chip_ref: (0, i, 0))],
            out_specs=pl.BlockSpec((tr, cs), lambda i, chip_ref: (i, 0))),
        out_shape=jax.ShapeDtypeStruct((h, cs), F32),
        compiler_params=_params(("parallel",)),
    )(_scalar(chip), sums, theirs)


def _sibling_exchange(halves):
    n = len(halves)

    def body(*refs):
        ins, outs, send_sems, recv_sems = refs[:n], refs[n:2 * n], refs[2 * n], refs[2 * n + 1]
        x, y, c = _me()
        cps = []
        for w, (h_ref, o_ref) in enumerate(zip(ins, outs)):
            cps.append(pltpu.make_async_remote_copy(src_ref=h_ref, dst_ref=o_ref, send_sem=send_sems.at[w], recv_sem=recv_sems.at[w],
                                                    device_id=(x, y, 1 - c), device_id_type=MESH_ID))
            cps[-1].start()
        for cp in cps:
            cp.wait()

    return pl.pallas_call(
        body, name="grad_sibling_exchange", in_specs=[_ANY] * n, out_specs=[_ANY] * n,
        out_shape=[jax.ShapeDtypeStruct(h.shape, h.dtype) for h in halves],
        scratch_shapes=[pltpu.SemaphoreType.DMA((n,)), pltpu.SemaphoreType.DMA((n,))],
    )(*halves)


def _allreduce_small(name, vec):
    def body(v_ref, o_ref, buf_ref, send_sems, recv_sems):
        x, y, c = _me()
        me = 4 * x + 2 * y + c
        cps = []
        for p in range(1, 8):
            px, py, pc = x ^ (p >> 2), y ^ ((p >> 1) & 1), c ^ (p & 1)
            cps.append(pltpu.make_async_remote_copy(src_ref=v_ref, dst_ref=buf_ref.at[me], send_sem=send_sems.at[p - 1],
                                                    recv_sem=recv_sems.at[p - 1], device_id=(px, py, pc), device_id_type=MESH_ID))
            cps[-1].start()
        buf_ref[me] = v_ref[...]
        for p in range(1, 8):
            theirs = buf_ref.at[me ^ p]
            pltpu.make_async_remote_copy(src_ref=theirs, dst_ref=theirs, send_sem=send_sems.at[p - 1], recv_sem=recv_sems.at[p - 1],
                                         device_id=(x, y, c), device_id_type=MESH_ID).wait_recv()
        for cp in cps:
            cp.wait_send()
        acc = buf_ref[0]
        for k in range(1, 8):
            acc = acc + buf_ref[k]
        o_ref[...] = acc

    vm = pl.BlockSpec(memory_space=pltpu.VMEM)
    return pl.pallas_call(
        body, name=name, in_specs=[vm], out_specs=vm, out_shape=jax.ShapeDtypeStruct(vec.shape, F32),
        scratch_shapes=[pltpu.VMEM((8,) + vec.shape, F32), pltpu.SemaphoreType.DMA((7,)), pltpu.SemaphoreType.DMA((7,))],
    )(vec)


def _adam_math(w, g, m, v):
    m = ADAM_B1 * m + (1.0 - ADAM_B1) * g
    v = ADAM_B2 * v + (1.0 - ADAM_B2) * (g * g)
    m_hat = m / (1.0 - ADAM_B1 ** ADAM_STEP)
    v_hat = v / (1.0 - ADAM_B2 ** ADAM_STEP)
    return -ADAM_LR * (m_hat / (jnp.sqrt(v_hat) + ADAM_EPS) + ADAM_WD * w), m, v


def _adamw(name, w, g, m, v):
    R, C = w.shape
    tr = _row_block(R, C, 8)

    def body(w_ref, g_ref, m_ref, v_ref, d_ref, nm_ref, nv_ref):
        d_ref[...], nm_ref[...], nv_ref[...] = _adam_math(w_ref[...], g_ref[...], m_ref[...], v_ref[...])

    blk = pl.BlockSpec((tr, C), lambda i: (i, 0))
    return pl.pallas_call(
        body, name=name, grid=(R // tr,), in_specs=[blk] * 4, out_specs=[blk] * 3,
        out_shape=[jax.ShapeDtypeStruct((R, C), F32)] * 3, compiler_params=_params(("parallel",)),
    )(w, g, m, v)


def _adamw_halves(name, w, mine, theirs, m, v, c):
    _, rs, cs = w.shape
    h = rs // 2
    tr = _row_block(h, cs, 8)
    nb = h // tr

    def body(c_ref, w_ref, a_ref, b_ref, m_ref, v_ref, g_ref, d_ref, nm_ref, nv_ref):
        g = jnp.where(pl.program_id(0) == c_ref[0], a_ref[...], b_ref[...])
        g_ref[...] = g
        d_ref[...], nm_ref[...], nv_ref[...] = _adam_math(w_ref[...], g, m_ref[...], v_ref[...])

    full = pl.BlockSpec((None, tr, cs), lambda s, i, c_ref: (0, s * nb + i, 0))
    part = pl.BlockSpec((tr, cs), lambda s, i, c_ref: (i, 0))
    return pl.pallas_call(
        body, name=name,
        grid_spec=pltpu.PrefetchScalarGridSpec(num_scalar_prefetch=1, grid=(2, nb), in_specs=[full, part, part, full, full],
                                               out_specs=[full] * 4),
        out_shape=[jax.ShapeDtypeStruct((1, rs, cs), F32)] * 4, compiler_params=_params(("parallel", "parallel")),
    )(_scalar(c), w, mine, theirs, m, v)


def _pack_small(arrs):
    flat = jnp.concatenate([a.reshape(-1) for a in arrs])
    n = -(-flat.shape[0] // (8 * LANE)) * 8 * LANE
    return jnp.pad(flat, (0, n - flat.shape[0])).reshape(8, n // 8)


def _unpack_small(vec, shapes):
    flat, out, off = vec.reshape(-1), [], 0
    for s in shapes:
        out.append(flat[off:off + s[0] * s[1]].reshape(s))
        off += s[0] * s[1]
    return out


class _FfnWeights:
    def __init__(self, cfg, names, staged):
        self.cfg, self.names, self.k = cfg, names, 3 * len(names)
        self.send, self.recv, self.bufs, self.token = _split_start("gather_chips_start", staged, self.k, _gather_to_chips)

    def pass_on(self, after):
        bufs = _split_wait("gather_chips_wait", self.send, self.recv, self.bufs, after, _gather_to_chips)
        self.send, self.recv, self.bufs, token = _split_start("gather_sibling_start", bufs, self.k, _gather_to_sibling)
        return token

    def arrived(self, after):
        bufs = _split_wait("gather_sibling_wait", self.send, self.recv, self.bufs, after, _gather_to_sibling)
        return {n: _gathered_to_kernel(self.cfg, n, b) for n, b in zip(self.names, bufs)}


def _step(cfg, a):
    chip = 2 * lax.axis_index("x") + lax.axis_index("y")
    core = lax.axis_index("c")
    big = [n for n, _, _, _ in cfg.BIG]

    ffn = ("w_gate", "w_up", "w_down")
    first = [n for n in big if n not in ffn]
    staged = {n: _stage_shard(n, a[n], chip) for n in big}
    W = {n: _gathered_to_kernel(cfg, n, wg) for n, wg in zip(first, _allgather_weights([staged[n] for n in first]))}
    ffn_weights = _FfnWeights(cfg, ffn, [staged[n] for n in ffn])

    sp = {n: a[n] for n in SMALL}
    sharded = _pack_small([a[n] for n in SMALL_SHARDED])
    slot = jnp.where(lax.broadcasted_iota(I32, (N_CHIPS,) + sharded.shape, 0) == chip, 0.5 * sharded[None], 0.0)
    allp = _allreduce_small("allgather_small", slot.reshape(N_CHIPS * 8, -1)).reshape((N_CHIPS,) + sharded.shape)
    per_chip = [_unpack_small(allp[ch], [a[n].shape for n in SMALL_SHARDED]) for ch in range(N_CHIPS)]
    for k, n in enumerate(SMALL_SHARDED):
        sp[n] = jnp.concatenate([per_chip[ch][k] for ch in range(N_CHIPS)], axis=1)
    sp["mix_pre_g"] = sp["mix_pre_g"] + ffn_weights.token[0, 0]

    def pair_sums(tag, names, gW):
        grads = [_grad_to_chips(cfg, n, gW[n]) for n in names]
        return [_pair_sum(n, g, t, core) for n, g, t in zip(names, grads, _pair_exchange(tag, grads))]

    early = ("w_down", "w_gate", "w_up", "w_out")
    late = [n for n in big if n not in early]
    started = []

    def ffn_grads_ready(g_early):
        started.extend(_chip_exchange_start("ffn", pair_sums("ffn", early, g_early)))
        return started[-1]

    loss, grad_x, gW, gs = _local_grads(cfg, a["x"], a["loss_target"], W, sp, ffn_weights, ffn_grads_ready)
    sums = dict(zip(late, pair_sums("rest", late, gW)))
    landed = dict(zip(late, _chip_exchange("rest", [sums[n] for n in late])))
    e_sums, e_landed = _chip_exchange_wait("ffn", *started[:4], landed[late[0]])
    sums.update(zip(early, e_sums))
    landed.update(zip(early, e_landed))
    mine = [_chip_sum(n, sums[n], landed[n], chip) for n in big]
    theirs = _sibling_exchange(mine)

    shapes = [gs[n].shape for n in SMALL] + [(1, LANE)]
    red = _unpack_small(_allreduce_small("allreduce_small", _pack_small([gs[n] for n in SMALL] + [loss])), shapes)
    g_small = dict(zip(SMALL, red[:-1]))
    for n in SMALL_SHARDED:
        cs = a[n].shape[1]
        g_small[n] = lax.dynamic_slice_in_dim(g_small[n], chip * cs, cs, axis=1)

    out = {"loss": red[-1][0, 0], "grad_x": grad_x}
    for n, gm, gt in zip(big, mine, theirs):
        out["grad_" + n], out["delta_" + n], out["new_m_" + n], out["new_v_" + n] = _adamw_halves(
            "adamw_" + n, a[n], gm, gt, a["m_" + n], a["v_" + n], core)
    sshapes = [a[n].shape for n in SMALL]
    d, nm, nv = _adamw("adamw_small", _pack_small([a[n] for n in SMALL]), _pack_small([g_small[n] for n in SMALL]),
                       _pack_small([a["m_" + n] for n in SMALL]), _pack_small([a["v_" + n] for n in SMALL]))
    for n, dd, mm, vv in zip(SMALL, _unpack_small(d, sshapes), _unpack_small(nm, sshapes), _unpack_small(nv, sshapes)):
        out["grad_" + n], out["delta_" + n], out["new_m_" + n], out["new_v_" + n] = g_small[n], dd, mm, vv
    return out


def kernel(x, mix_pre_g, w_in, q_norm_g, w_uq, kv_norm_g, w_ukv, ssm_conv_w, ssm_conv_b, dt_bias, a_log, d_skip, ssm_norm_g, w_out, mix_post_g, ffn_pre_g, w_gate, w_up, ffn_conv_w, ffn_conv_b, w_down, ffn_post_g, loss_target, m_mix_pre_g, m_w_in, m_q_norm_g, m_w_uq, m_kv_norm_g, m_w_ukv, m_ssm_conv_w, m_ssm_conv_b, m_dt_bias, m_a_log, m_d_skip, m_ssm_norm_g, m_w_out, m_mix_post_g, m_ffn_pre_g, m_w_gate, m_w_up, m_ffn_conv_w, m_ffn_conv_b, m_w_down, m_ffn_post_g, v_mix_pre_g, v_w_in, v_q_norm_g, v_w_uq, v_kv_norm_g, v_w_ukv, v_ssm_conv_w, v_ssm_conv_b, v_dt_bias, v_a_log, v_d_skip, v_ssm_norm_g, v_w_out, v_mix_post_g, v_ffn_pre_g, v_w_gate, v_w_up, v_ffn_conv_w, v_ffn_conv_b, v_w_down, v_ffn_post_g):
    args = dict(locals())
    big = {pre + n for n, _, _, _ in _FULL.BIG for pre in ("", "m_", "v_")}
    out = _step(_FULL, {k: (v[0] if v.ndim == 3 and k not in big else v) for k, v in args.items()})
    res = [out["loss"], out["grad_x"][None]]
    for pre in ("grad_", "delta_", "new_m_", "new_v_"):
        res += [out[pre + n][None] if args[n].ndim == 3 and n not in big else out[pre + n] for n in WEIGHTS]
    return tuple(res)
```

```python
import functools
import math

import jax
import jax.numpy as jnp
from jax import lax
from jax.experimental import pallas as pl
from jax.experimental.pallas import tpu as pltpu

F32, BF16, I32 = jnp.float32, jnp.bfloat16, jnp.int32
NN = (((1,), (0,)), ((), ()))
NT = (((1,), (1,)), ((), ()))
TN = (((0,), (0,)), ((), ()))
HI = lax.Precision.HIGHEST
MESH_ID = pl.DeviceIdType.MESH

EPS = 1e-6
CHUNK = 64
NOPE, ROPE, VH = 128, 64, 128
ROPE_THETA = 10000.0
HP, NST = 64, 128
SSM_K, FFN_K = 4, 3
LANE = 128
N_CHIPS = 4
VMEM_LIMIT = 52 * 1024 * 1024
MM_TILE, MM_TILE_K = 1408, 2816

ADAM_LR, ADAM_B1, ADAM_B2, ADAM_EPS, ADAM_WD, ADAM_STEP = 0.001, 0.9, 0.999, 1e-08, 0.01, 10


class _Cfg:
    def __init__(self, S, D, QL, KVL, H, HS, G, DFF, T):
        self.S, self.D, self.QL, self.KVL, self.H, self.HS, self.G, self.DFF, self.T = S, D, QL, KVL, H, HS, G, DFF, T
        self.INNER = HS * HP
        self.CONVCH = self.INNER + 2 * G * NST
        self.QW = H * (NOPE + ROPE)
        self.KVW = H * (NOPE + VH)
        self.MLAW = H * VH
        self.MIXW = self.MLAW + self.INNER
        self.IN_COLS = QL + KVL + ROPE + self.INNER + self.CONVCH + HS
        self.o_kr = QL + KVL
        self.o_z = self.o_kr + LANE
        self.o_xbc = self.o_z + self.INNER
        self.o_dt = self.o_xbc + self.CONVCH
        self.EXT = self.o_dt + LANE
        self.NPAIR = HS // 2
        self.REP = HS // G
        self.BIG = (("w_in", D, self.IN_COLS, 1), ("w_uq", QL, self.QW, 1), ("w_ukv", KVL, self.KVW, 1),
                    ("w_out", self.MIXW, D, 0), ("w_gate", D, DFF, 1), ("w_up", D, DFF, 1), ("w_down", DFF, D, 0))
        self.NSHARD = sum(r * c for _, r, c, _ in self.BIG) // N_CHIPS
        unit = 2 * LANE * 16
        self.NPACK = -(-self.NSHARD // unit) * unit
        self.R = self.NPACK // (2 * LANE)


_FULL = _Cfg(S=2048, D=2048, QL=768, KVL=512, H=8, HS=16, G=2, DFF=5632, T=256)

SMALL = ("mix_pre_g", "q_norm_g", "kv_norm_g", "ssm_conv_w", "ssm_conv_b", "dt_bias", "a_log", "d_skip", "ssm_norm_g",
         "mix_post_g", "ffn_pre_g", "ffn_conv_w", "ffn_conv_b", "ffn_post_g")
SMALL_SHARDED = ("ssm_conv_w", "ffn_conv_w")
WEIGHTS = ("mix_pre_g", "w_in", "q_norm_g", "w_uq", "kv_norm_g", "w_ukv", "ssm_conv_w", "ssm_conv_b", "dt_bias", "a_log",
           "d_skip", "ssm_norm_g", "w_out", "mix_post_g", "ffn_pre_g", "w_gate", "w_up", "ffn_conv_w", "ffn_conv_b",
           "w_down", "ffn_post_g")


def _pick(n, target, mult):
    best = None
    for d in range(mult, min(n, target) + 1, mult):
        if n % d == 0:
            best = d
    return best if best is not None else n


def _params(sem=None):
    kw = dict(vmem_limit_bytes=VMEM_LIMIT)
    if sem is not None:
        kw["dimension_semantics"] = sem
    return pltpu.CompilerParams(**kw)


def _dot(a, b, dims=NN, precision=None):
    return lax.dot_general(a, b, dims, preferred_element_type=F32, precision=precision)


def _sigmoid(x):
    return 1.0 / (1.0 + jnp.exp(-x))


def _rs(x):
    return lax.rsqrt(jnp.mean(x * x, axis=-1, keepdims=True) + EPS)


def _rms_back(xh, r, dn):
    return r * (dn - xh * jnp.mean(dn * xh, axis=-1, keepdims=True))


def _colsum(v):
    return jnp.sum(v, axis=0, keepdims=True)


def _matmul(name, a, b, mode, out_dtype, a2=None, b2=None, chips=False):
    cs = None
    if mode == "nn":
        (M, K), N = a.shape, b.shape[-1]
        if chips:
            cs, N = N, N_CHIPS * N
    elif mode == "nt":
        (M, K), N = a.shape, b.shape[-2]
        if chips:
            cs = b.shape[-1]
    else:
        (K, M), N = a.shape, b.shape[1]
        if chips:
            cs = N // N_CHIPS
    tm = _pick(M, MM_TILE, LANE)
    tn = _pick(cs if chips and mode != "nt" else N, MM_TILE, LANE)
    tk = _pick(cs, MM_TILE, LANE) if chips and mode == "nt" else _pick(K, MM_TILE_K, LANE)
    nk = K // tk
    dims = {"nn": NN, "nt": NT, "tn": TN}[mode]
    a_spec = pl.BlockSpec((tk, tm), lambda i, j, k: (k, i)) if mode == "tn" else pl.BlockSpec((tm, tk), lambda i, j, k: (i, k))
    b_spec = pl.BlockSpec((tn, tk), lambda i, j, k: (j, k)) if mode == "nt" else pl.BlockSpec((tk, tn), lambda i, j, k: (k, j))
    o_spec = pl.BlockSpec((tm, tn), lambda i, j, k: (i, j))
    o_shape = (M, N)
    if chips and mode == "nn":
        per = cs // tn
        b_spec = pl.BlockSpec((None, tk, tn), lambda i, j, k: (j // per, k, j % per))
    elif chips and mode == "nt":
        per = cs // tk
        b_spec = pl.BlockSpec((None, tn, tk), lambda i, j, k: (k // per, j, k % per))
    elif chips:
        per = cs // tn
        o_spec = pl.BlockSpec((None, tm, tn), lambda i, j, k: (j // per, i, j % per))
        o_shape = (N_CHIPS, M, cs)
    two = a2 is not None

    def product(refs):
        part = _dot(refs[0][...].astype(BF16), refs[1][...].astype(BF16), dims)
        if two:
            part += _dot(refs[2][...].astype(BF16), refs[3][...].astype(BF16), dims)
        return part

    def body_whole_k(*refs):
        refs[-1][...] = product(refs).astype(refs[-1].dtype)

    def body(*refs):
        o_ref, acc_ref = refs[-2], refs[-1]
        k = pl.program_id(2)

        @pl.when(k == 0)
        def _():
            acc_ref[...] = product(refs)

        @pl.when(k > 0)
        def _():
            acc_ref[...] += product(refs)

        @pl.when(k == nk - 1)
        def _():
            o_ref[...] = acc_ref[...].astype(o_ref.dtype)

    ins = (a, b, a2, b2) if two else (a, b)
    return pl.pallas_call(
        body_whole_k if nk == 1 else body, name=name, grid=(M // tm, N // tn, nk),
        in_specs=[a_spec, b_spec] * (2 if two else 1),
        out_specs=o_spec,
        out_shape=jax.ShapeDtypeStruct(o_shape, out_dtype),
        scratch_shapes=[] if nk == 1 else [pltpu.VMEM((tm, tn), F32)],
        compiler_params=_params(("parallel", "parallel", "arbitrary")),
    )(*ins)


def _rowwise(name, fn, rows, mats, outs, reds, ts):
    S = rows[0].shape[0]
    nr, nm, no = len(rows), len(mats), len(outs)

    def body(*refs):
        res = fn(*[r[...] for r in refs[:nr + nm]])
        res = res if isinstance(res, (tuple, list)) else (res,)
        for r, v in zip(refs[nr + nm:nr + nm + no], res[:no]):
            r[...] = v.astype(r.dtype)
        first = pl.program_id(0) == 0
        for r, v in zip(refs[nr + nm + no:], res[no:]):
            @pl.when(first)
            def _():
                r[...] = jnp.broadcast_to(v, r.shape)

            @pl.when(jnp.logical_not(first))
            def _():
                r[...] += jnp.broadcast_to(v, r.shape)

    in_specs = [pl.BlockSpec((ts, a.shape[1]), lambda i: (i, 0)) for a in rows]
    in_specs += [pl.BlockSpec(m.shape, lambda i, nd=m.ndim: (0,) * nd) for m in mats]
    out_specs = [pl.BlockSpec((ts, w), lambda i: (i, 0)) for w, _ in outs]
    out_specs += [pl.BlockSpec(s, lambda i: (0, 0)) for s in reds]
    out_shape = [jax.ShapeDtypeStruct((S, w), dt) for w, dt in outs] + [jax.ShapeDtypeStruct(s, F32) for s in reds]
    return pl.pallas_call(
        body, name=name, grid=(S // ts,), in_specs=in_specs, out_specs=out_specs, out_shape=out_shape,
        compiler_params=_params(("arbitrary",) if reds else ("parallel",)),
    )(*rows, *mats)


def _shift_down(v, s):
    if s == 0:
        return v
    rows = lax.broadcasted_iota(I32, v.shape, 0)
    return jnp.where(rows >= s, pltpu.roll(v, s, 0), 0.0)


def _shift_up(v, s):
    if s == 0:
        return v
    n = v.shape[0]
    rows = lax.broadcasted_iota(I32, v.shape, 0)
    return jnp.where(rows < n - s, pltpu.roll(v, n - s, 0), 0.0)


def _conv(x, w, b):
    K = w.shape[0]
    y = jnp.broadcast_to(b, x.shape)
    for k in range(K):
        y = y + w[k:k + 1, :] * _shift_down(x, K - 1 - k)
    return y


def _conv_back(x, w, dc):
    K = w.shape[0]
    dx = jnp.zeros_like(x)
    dw = []
    for k in range(K):
        dx = dx + w[k:k + 1, :] * _shift_up(dc, K - 1 - k)
        dw.append(_colsum(dc * _shift_down(x, K - 1 - k)))
    return dx, jnp.concatenate(dw, axis=0), _colsum(dc)


def _colwise(name, fn, cols, vecs, outs, pouts, tc):
    S, C = cols[0].shape
    nc_, nv, no = len(cols), len(vecs), len(outs)

    def body(*refs):
        res = fn(*[r[...] for r in refs[:nc_ + nv]])
        res = res if isinstance(res, (tuple, list)) else (res,)
        for r, v in zip(refs[nc_ + nv:], res):
            r[...] = v.astype(r.dtype)

    in_specs = [pl.BlockSpec((S, tc), lambda j: (0, j)) for _ in cols]
    in_specs += [pl.BlockSpec((v.shape[0], tc), lambda j: (0, j)) for v in vecs]
    out_specs = [pl.BlockSpec((S, tc), lambda j: (0, j)) for _ in outs] + [pl.BlockSpec((k, tc), lambda j: (0, j)) for k in pouts]
    out_shape = [jax.ShapeDtypeStruct((S, C), dt) for dt in outs] + [jax.ShapeDtypeStruct((k, C), F32) for k in pouts]
    return pl.pallas_call(
        body, name=name, grid=(C // tc,), in_specs=in_specs, out_specs=out_specs, out_shape=out_shape,
        compiler_params=_params(("parallel",)),
    )(*cols, *vecs)


_G0, _G1 = math.sqrt(2.0 / math.pi), 0.044715


def _gelu(g):
    th = jnp.tanh(_G0 * (g + _G1 * g * g * g))
    return 0.5 * g * (1.0 + th), th


def _ffn_act(gate_pre, up, w, b):
    act, _ = _gelu(_conv(gate_pre, w, b))
    return act * up


def _ffn_act_back(dact, gate_pre, up, w, b):
    g = _conv(gate_pre, w, b)
    ge, th = _gelu(g)
    dge = 0.5 * (1.0 + th) + 0.5 * g * (1.0 - th * th) * _G0 * (1.0 + 3.0 * _G1 * g * g)
    dup = dact * ge
    dgate_pre, dw, db = _conv_back(gate_pre, w, dact * up * dge)
    return dgate_pre, dup, dw, db


def _ssm_act(xbc, w, b):
    c = _conv(xbc, w, b)
    return c * _sigmoid(c)


def _ssm_act_back(dxc, xbc, w, b):
    c = _conv(xbc, w, b)
    sg = _sigmoid(c)
    return _conv_back(xbc, w, dxc * sg * (1.0 + c * (1.0 - sg)))


def _rope_tables(S):
    inv = 1.0 / (ROPE_THETA ** (jnp.arange(0, ROPE, 2, dtype=F32) / ROPE))
    ang = jnp.arange(S, dtype=F32)[:, None] * inv[None, :]
    cos, sin = jnp.cos(ang), jnp.sin(ang)
    return jnp.tile(cos, (1, 4)), jnp.tile(jnp.concatenate([-sin, sin], axis=1), (1, 2))


def _swap_halves(x):
    lane = lax.broadcasted_iota(I32, x.shape, 1)
    w = x.shape[1]
    return jnp.where((lane % ROPE) < ROPE // 2, pltpu.roll(x, w - ROPE // 2, 1), pltpu.roll(x, ROPE // 2, 1))


def _rot(x, cos2, sin2):
    return x * cos2 + _swap_halves(x) * sin2


def _rot_back(dy, cos2, sin2):
    return dy * cos2 + _swap_halves(dy * sin2)


def _mla_pack(cfg, q, kv, kr, cos2, sin2):
    S, H = cfg.S, cfg.H
    ts = _pick(S, 512, 8)

    def body(qn_ref, qr_ref, kn_ref, v_ref, kr_ref, c_ref, s_ref, Q_ref, K_ref, V_ref):
        h = pl.program_id(0)
        c2, s2 = c_ref[...], s_ref[...]
        Q_ref[0, :, 0:LANE] = qn_ref[...].astype(BF16)
        Q_ref[0, :, LANE:] = _rot(qr_ref[...], c2, s2).astype(BF16)
        K_ref[0, :, 0:LANE] = kn_ref[...].astype(BF16)
        krr = _rot(kr_ref[...], c2, s2)
        K_ref[0, :, LANE:] = jnp.where(h % 2 == 1, pltpu.roll(krr, ROPE, 1), krr).astype(BF16)
        V_ref[0] = v_ref[...].astype(BF16)

    blk = lambda f: pl.BlockSpec((ts, LANE), f)
    return pl.pallas_call(
        body, name="mla_pack", grid=(H, S // ts),
        in_specs=[blk(lambda h, i: (i, h)), blk(lambda h, i: (i, H + h // 2)), blk(lambda h, i: (i, h)),
                  blk(lambda h, i: (i, H + h)), blk(lambda h, i: (i, 0)), blk(lambda h, i: (i, 0)), blk(lambda h, i: (i, 0))],
        out_specs=[pl.BlockSpec((1, ts, 2 * LANE), lambda h, i: (h, i, 0)), pl.BlockSpec((1, ts, 2 * LANE), lambda h, i: (h, i, 0)),
                   pl.BlockSpec((1, ts, LANE), lambda h, i: (h, i, 0))],
        out_shape=[jax.ShapeDtypeStruct((H, S, 2 * LANE), BF16), jax.ShapeDtypeStruct((H, S, 2 * LANE), BF16),
                   jax.ShapeDtypeStruct((H, S, LANE), BF16)],
        compiler_params=_params(("parallel", "parallel")),
    )(q, q, kv, kv, kr, cos2, sin2)


def _mla_unpack(cfg, dQ, dK, dV, cos2, sin2):
    S, H = cfg.S, cfg.H
    ts = _pick(S, 256, 8)

    def body(dQ_ref, dK_ref, dV_ref, c_ref, s_ref, dq_ref, dkv_ref, dkr_ref):
        c2, s2 = c_ref[...], s_ref[...]
        lo = lax.broadcasted_iota(I32, (ts, LANE), 1) < ROPE
        tk = jnp.zeros((ts, LANE), F32)
        for h in range(H):
            dq_ref[:, h * LANE:(h + 1) * LANE] = dQ_ref[h, :, 0:LANE].astype(BF16)
            dkv_ref[:, h * LANE:(h + 1) * LANE] = dK_ref[h, :, 0:LANE].astype(BF16)
            dkv_ref[:, (H + h) * LANE:(H + h + 1) * LANE] = dV_ref[h].astype(BF16)
            own = lo if h % 2 == 0 else jnp.logical_not(lo)
            tk = tk + jnp.where(own, dK_ref[h, :, LANE:], 0.0)
        for j in range(H // 2):
            dr = dQ_ref[2 * j, :, LANE:] + dQ_ref[2 * j + 1, :, LANE:]
            dq_ref[:, (H + j) * LANE:(H + j + 1) * LANE] = _rot_back(dr, c2, s2).astype(BF16)
        dkr_rot = jnp.where(lo, tk + pltpu.roll(tk, ROPE, 1), 0.0)
        dkr_ref[...] = _rot_back(dkr_rot, c2, s2).astype(BF16)

    tab = pl.BlockSpec((ts, LANE), lambda i: (i, 0))
    return pl.pallas_call(
        body, name="mla_unpack", grid=(S // ts,),
        in_specs=[pl.BlockSpec((H, ts, 2 * LANE), lambda i: (0, i, 0)), pl.BlockSpec((H, ts, 2 * LANE), lambda i: (0, i, 0)),
                  pl.BlockSpec((H, ts, LANE), lambda i: (0, i, 0)), tab, tab],
        out_specs=[pl.BlockSpec((ts, cfg.QW), lambda i: (i, 0)), pl.BlockSpec((ts, cfg.KVW), lambda i: (i, 0)), tab],
        out_shape=[jax.ShapeDtypeStruct((S, cfg.QW), BF16), jax.ShapeDtypeStruct((S, cfg.KVW), BF16),
                   jax.ShapeDtypeStruct((S, LANE), BF16)],
        compiler_params=_params(("parallel",)),
    )(dQ, dK, dV, cos2, sin2)


_ATT_T = 256
_ATT_HB = 2
_ATT_SCALE = (NOPE + ROPE) ** -0.5


def _diag_mask(transposed=False):
    r = lax.broadcasted_iota(I32, (_ATT_T, _ATT_T), 0) // CHUNK
    c = lax.broadcasted_iota(I32, (_ATT_T, _ATT_T), 1) // CHUNK
    return r <= c if transposed else c <= r


def _row_form(col):
    return jnp.broadcast_to(col, (col.shape[0], LANE)).T[0:8, :]


def _attn_fwd(cfg, Q, K, V):
    S, H, T, HB = cfg.S, cfg.H, _ATT_T, _ATT_HB

    def body(q_ref, k_ref, v_ref, o_ref, lse_ref, lse_t_ref):
        qi = pl.program_id(1)

        def head_step(b, kb, carry, mask):
            m, l, acc = carry
            ks = pl.multiple_of(kb * T, T)
            s = _dot(q_ref[b], k_ref[b, pl.ds(ks, T), :], NT) * _ATT_SCALE
            if mask is not None:
                s = jnp.where(mask, s, -1e30)
            m_new = jnp.maximum(m, jnp.max(s, axis=1, keepdims=True))
            p = jnp.exp(s - m_new)
            alpha = jnp.exp(m - m_new)
            l = alpha * l + jnp.sum(p, axis=1, keepdims=True)
            acc = alpha * acc + _dot(p.astype(BF16), v_ref[b, pl.ds(ks, T), :])
            return m_new, l, acc

        def step(kb, carry, mask=None):
            return tuple(head_step(b, kb, carry[b], mask) for b in range(HB))

        init = (jnp.full((T, 1), -1e30, F32), jnp.zeros((T, 1), F32), jnp.zeros((T, VH), F32))
        done = step(qi, lax.fori_loop(0, qi, step, (init,) * HB), _diag_mask())
        for b, (m, l, acc) in enumerate(done):
            o_ref[:, b * LANE:(b + 1) * LANE] = acc / l
            lse = m + jnp.log(l)
            lse_ref[:, b * LANE:(b + 1) * LANE] = jnp.broadcast_to(lse, (T, LANE))
            lse_t_ref[b] = _row_form(lse)

    return pl.pallas_call(
        body, name="attn_fwd", grid=(H // HB, S // T),
        in_specs=[pl.BlockSpec((HB, T, 2 * LANE), lambda h, i: (h, i, 0)), pl.BlockSpec((HB, S, 2 * LANE), lambda h, i: (h, 0, 0)),
                  pl.BlockSpec((HB, S, LANE), lambda h, i: (h, 0, 0))],
        out_specs=[pl.BlockSpec((T, HB * LANE), lambda h, i: (i, h)), pl.BlockSpec((T, HB * LANE), lambda h, i: (i, h)),
                   pl.BlockSpec((HB, 8, T), lambda h, i: (h, 0, i))],
        out_shape=[jax.ShapeDtypeStruct((S, H * LANE), F32), jax.ShapeDtypeStruct((S, H * LANE), F32),
                   jax.ShapeDtypeStruct((H, 8, S), F32)],
        compiler_params=_params(("parallel", "parallel")),
    )(Q, K, V)


def _attn_dq(cfg, Q, K, V, do, o, lse):
    S, H, T, HB = cfg.S, cfg.H, _ATT_T, _ATT_HB

    def body(q_ref, k_ref, v_ref, do_ref, o_ref, lse_ref, dq_ref, dl_t_ref):
        qi = pl.program_id(1)
        do = [do_ref[:, b * LANE:(b + 1) * LANE] for b in range(HB)]
        delta = [jnp.sum(do[b] * o_ref[:, b * LANE:(b + 1) * LANE], axis=1, keepdims=True) for b in range(HB)]
        dob = [d.astype(BF16) for d in do]

        def head_step(b, kb, dq, mask):
            ks = pl.multiple_of(kb * T, T)
            k = k_ref[b, pl.ds(ks, T), :]
            s = _dot(q_ref[b], k, NT) * _ATT_SCALE
            if mask is not None:
                s = jnp.where(mask, s, -1e30)
            p = jnp.exp(s - lse_ref[:, b * LANE:b * LANE + 1])
            dp = _dot(dob[b], v_ref[b, pl.ds(ks, T), :], NT)
            ds = p * (dp - delta[b]) * _ATT_SCALE
            return dq + _dot(ds.astype(BF16), k)

        def step(kb, dqs, mask=None):
            return tuple(head_step(b, kb, dqs[b], mask) for b in range(HB))

        dqs = step(qi, lax.fori_loop(0, qi, step, (jnp.zeros((T, 2 * LANE), F32),) * HB), _diag_mask())
        for b in range(HB):
            dq_ref[b] = dqs[b]
            dl_t_ref[b] = _row_form(delta[b])

    col = pl.BlockSpec((T, HB * LANE), lambda h, i: (i, h))
    return pl.pallas_call(
        body, name="attn_dq", grid=(H // HB, S // T),
        in_specs=[pl.BlockSpec((HB, T, 2 * LANE), lambda h, i: (h, i, 0)), pl.BlockSpec((HB, S, 2 * LANE), lambda h, i: (h, 0, 0)),
                  pl.BlockSpec((HB, S, LANE), lambda h, i: (h, 0, 0)), col, col, col],
        out_specs=[pl.BlockSpec((HB, T, 2 * LANE), lambda h, i: (h, i, 0)), pl.BlockSpec((HB, 8, T), lambda h, i: (h, 0, i))],
        out_shape=[jax.ShapeDtypeStruct((H, S, 2 * LANE), F32), jax.ShapeDtypeStruct((H, 8, S), F32)],
        compiler_params=_params(("parallel", "parallel")),
    )(Q, K, V, do, o, lse)


def _attn_dkv(cfg, Q, K, V, do, lse_t, delta_t):
    S, H, T, HB = cfg.S, cfg.H, _ATT_T, _ATT_HB
    nq = S // T

    def body(q_ref, k_ref, v_ref, do_ref, lse_ref, dl_ref, dk_ref, dv_ref):
        kb = pl.program_id(1)

        def head_step(b, qi, carry, mask):
            dk, dv = carry
            qs = pl.multiple_of(qi * T, T)
            q = q_ref[b, pl.ds(qs, T), :]
            dob = do_ref[pl.ds(qs, T), b * LANE:(b + 1) * LANE].astype(BF16)
            s = _dot(k_ref[b], q, NT) * _ATT_SCALE
            if mask is not None:
                s = jnp.where(mask, s, -1e30)
            p = jnp.exp(s - lse_ref[b, 0:1, pl.ds(qs, T)])
            dv = dv + _dot(p.astype(BF16), dob)
            dp = _dot(v_ref[b], dob, NT)
            ds = p * (dp - dl_ref[b, 0:1, pl.ds(qs, T)]) * _ATT_SCALE
            dk = dk + _dot(ds.astype(BF16), q)
            return dk, dv

        def step(qi, carry, mask=None):
            return tuple(head_step(b, qi, carry[b], mask) for b in range(HB))

        zero = (jnp.zeros((T, 2 * LANE), F32), jnp.zeros((T, VH), F32))
        done = lax.fori_loop(kb + 1, nq, step, step(kb, (zero,) * HB, _diag_mask(transposed=True)))
        for b, (dk, dv) in enumerate(done):
            dk_ref[b] = dk
            dv_ref[b] = dv

    row = pl.BlockSpec((HB, 8, S), lambda h, j: (h, 0, 0))
    return pl.pallas_call(
        body, name="attn_dkv", grid=(H // HB, S // T),
        in_specs=[pl.BlockSpec((HB, S, 2 * LANE), lambda h, j: (h, 0, 0)), pl.BlockSpec((HB, T, 2 * LANE), lambda h, j: (h, j, 0)),
                  pl.BlockSpec((HB, T, LANE), lambda h, j: (h, j, 0)), pl.BlockSpec((S, HB * LANE), lambda h, j: (0, h)), row, row],
        out_specs=[pl.BlockSpec((HB, T, 2 * LANE), lambda h, j: (h, j, 0)), pl.BlockSpec((HB, T, LANE), lambda h, j: (h, j, 0))],
        out_shape=[jax.ShapeDtypeStruct((H, S, 2 * LANE), F32), jax.ShapeDtypeStruct((H, S, LANE), F32)],
        compiler_params=_params(("parallel", "parallel")),
    )(Q, K, V, do, lse_t, delta_t)


def _expand_matrix(cfg):
    r = lax.broadcasted_iota(I32, (LANE, cfg.INNER), 0)
    c = lax.broadcasted_iota(I32, (LANE, cfg.INNER), 1)
    return (r == c // HP).astype(F32)


def _softplus(x):
    return jnp.maximum(x, 0.0) + jnp.log(1.0 + jnp.exp(-jnp.abs(x)))


def _ssd_prep(cfg, dt_raw, dt_bias_pad, a_log_pad, expand):
    HS = cfg.HS

    def fn(raw, bias, alog, E):
        heads = lax.broadcasted_iota(I32, raw.shape, 1) < HS
        dt = jnp.where(heads, _softplus(raw + bias), 0.0)
        a = dt * jnp.where(heads[0:1], -jnp.exp(alog), 0.0)
        return dt, a, _dot(dt, E, precision=HI), _dot(a, E, precision=HI)

    return _rowwise("ssd_prep", fn, [dt_raw], [dt_bias_pad, a_log_pad, expand],
                    [(LANE, F32), (LANE, F32), (cfg.INNER, F32), (cfg.INNER, F32)], [], _pick(cfg.S, 512, 8))


def _tril(T):
    return lax.broadcasted_iota(I32, (T, T), 0) >= lax.broadcasted_iota(I32, (T, T), 1)


def _ssd_fwd(cfg, xc, dt_exp, a_exp, a_small, dskip_exp):
    S, T, INNER, G, NPAIR = cfg.S, cfg.T, cfg.INNER, cfg.G, cfg.NPAIR
    NC = S // T

    def body(xc_ref, dte_ref, ae_ref, as_ref, dsk_ref, y_ref, hin_ref, ht_ref):
        @pl.when(pl.program_id(0) == 0)
        def _():
            ht_ref[...] = jnp.zeros_like(ht_ref)

        tril = _tril(T)
        tri = tril.astype(F32)
        acs_s = _dot(tri, as_ref[...], precision=HI)
        acs_e = _dot(tri, ae_ref[...], precision=HI)
        acs_t = acs_s.T
        lo = lax.broadcasted_iota(I32, (T, LANE), 1) < HP
        for g in range(G):
            Bb = xc_ref[:, INNER + g * NST:INNER + (g + 1) * NST].astype(BF16)
            Cb = xc_ref[:, INNER + (G + g) * NST:INNER + (G + g + 1) * NST].astype(BF16)
            Gm = _dot(Cb, Bb, NT)
            for j in range(g * NPAIR // G, (g + 1) * NPAIR // G):
                sl = slice(j * LANE, (j + 1) * LANE)
                Xp = xc_ref[:, sl]
                Xdt = Xp * dte_ref[:, sl]
                Xb = Xdt.astype(BF16)
                acs_p = acs_e[:, sl]
                last = acs_p[T - 1:T, :]
                Hin = ht_ref[j]
                hin_ref[0, j] = Hin
                yd = []
                for e in (0, 1):
                    h = 2 * j + e
                    Lm = jnp.exp(jnp.where(tril, acs_s[:, h:h + 1] - acs_t[h:h + 1, :], -1e30))
                    yd.append(_dot((Gm * Lm).astype(BF16), Xb))
                y_off = _dot(Cb, Hin.astype(BF16)) * jnp.exp(acs_p)
                y_ref[:, sl] = jnp.where(lo, yd[0], yd[1]) + y_off + Xp * dsk_ref[:, sl]
                st = _dot(Bb, (Xdt * jnp.exp(last - acs_p)).astype(BF16), TN)
                ht_ref[j] = jnp.exp(last) * Hin + st

    rows = lambda w: pl.BlockSpec((T, w), lambda c: (c, 0))
    return pl.pallas_call(
        body, name="ssd_fwd", grid=(NC,),
        in_specs=[rows(cfg.CONVCH), rows(INNER), rows(INNER), rows(LANE), pl.BlockSpec((1, INNER), lambda c: (0, 0))],
        out_specs=[rows(INNER), pl.BlockSpec((1, NPAIR, NST, LANE), lambda c: (c, 0, 0, 0))],
        out_shape=[jax.ShapeDtypeStruct((S, INNER), F32), jax.ShapeDtypeStruct((NC, NPAIR, NST, LANE), F32)],
        scratch_shapes=[pltpu.VMEM((NPAIR, NST, LANE), F32)],
        compiler_params=_params(("arbitrary",)),
    )(xc, dt_exp, a_exp, a_small, dskip_exp)


def _ssd_bwd(cfg, dy, xc, dt_exp, a_exp, a_small, dskip_exp, hin, dt_raw, dt_bias_pad, a_log_pad, expand):
    S, T, INNER, G, NPAIR, HS = cfg.S, cfg.T, cfg.INNER, cfg.G, cfg.NPAIR, cfg.HS
    NC = S // T

    def body(dy_ref, xc_ref, dte_ref, ae_ref, as_ref, dsk_ref, hin_ref, raw_ref, bias_ref, alog_ref, e_ref,
             dxc_ref, draw_ref, dbias_ref, dalog_ref, dskip_ref, dht_ref, cols_ref, rows_ref, dacs_ref, ddt_ref):
        first = pl.program_id(0) == 0

        @pl.when(first)
        def _():
            dht_ref[...] = jnp.zeros_like(dht_ref)

        tril = _tril(T)
        tri = tril.astype(F32)
        a_s = as_ref[...]
        acs_s = _dot(tri, a_s, precision=HI)
        acs_e = _dot(tri, ae_ref[...], precision=HI)
        acs_t = acs_s.T
        lo = lax.broadcasted_iota(I32, (T, LANE), 1) < HP
        last_row = lax.broadcasted_iota(I32, (T, LANE), 0) == T - 1
        cols_ref[...] = jnp.zeros_like(cols_ref)
        rows_ref[...] = jnp.zeros_like(rows_ref)
        dsk_parts = []
        for g in range(G):
            bsl = slice(INNER + g * NST, INNER + (g + 1) * NST)
            csl = slice(INNER + (G + g) * NST, INNER + (G + g + 1) * NST)
            Bb = xc_ref[:, bsl].astype(BF16)
            Cb = xc_ref[:, csl].astype(BF16)
            Gm = _dot(Cb, Bb, NT)
            dG = jnp.zeros((T, T), F32)
            dB = jnp.zeros((T, NST), F32)
            dC = jnp.zeros((T, NST), F32)
            for j in range(g * NPAIR // G, (g + 1) * NPAIR // G):
                sl = slice(j * LANE, (j + 1) * LANE)
                Xp = xc_ref[:, sl]
                dtp = dte_ref[:, sl]
                Xdt = Xp * dtp
                Xb = Xdt.astype(BF16)
                acs_p = acs_e[:, sl]
                last = acs_p[T - 1:T, :]
                e_p, dec, cd = jnp.exp(acs_p), jnp.exp(last - acs_p), jnp.exp(last)
                Hin = hin_ref[0, j]
                Hb = Hin.astype(BF16)
                dHn = dht_ref[j]
                dHb = dHn.astype(BF16)
                dYp = dy_ref[:, sl]
                z = _dot(Cb, Hb)
                dz = (dYp * e_p).astype(BF16)
                dacs_p = dYp * z * e_p
                dC = dC + _dot(dz, Hb, NT)
                dHin = _dot(Cb, dz, TN) + cd * dHn
                dlast = _colsum(dHn * Hin) * cd
                qv = _dot(Bb, dHb)
                dXdt = qv * dec
                ddec = qv * Xdt * dec
                dacs_p = dacs_p - ddec
                dlast = dlast + _colsum(ddec)
                dB = dB + _dot((Xdt * dec).astype(BF16), dHb, NT)
                for e in (0, 1):
                    h = 2 * j + e
                    Lm = jnp.exp(jnp.where(tril, acs_s[:, h:h + 1] - acs_t[h:h + 1, :], -1e30))
                    Mh = Gm * Lm
                    dYe = jnp.where(lo if e == 0 else jnp.logical_not(lo), dYp, 0.0).astype(BF16)
                    dM = _dot(dYe, Xb, NT)
                    dXdt = dXdt + _dot(Mh.astype(BF16), dYe, TN)
                    W = dM * Mh
                    cols_ref[:, h:h + 1] = jnp.sum(W, axis=1, keepdims=True)
                    rows_ref[h:h + 1, :] = _colsum(W)
                    dG = dG + dM * Lm
                dacs_ref[:, sl] = dacs_p + jnp.where(last_row, dlast, 0.0)
                ddt_ref[:, sl] = dXdt * Xp
                dxc_ref[:, sl] = dXdt * dtp + dYp * dsk_ref[:, sl]
                dsk_parts.append(_colsum(dYp * Xp))
                dht_ref[j] = dHin
            dGb = dG.astype(BF16)
            dxc_ref[:, bsl] = dB + _dot(dGb, Cb, TN)
            dxc_ref[:, csl] = dC + _dot(dGb, Bb)
        E = e_ref[...]
        dacs_s = cols_ref[...] - rows_ref[...].T + _dot(dacs_ref[...], E, NT, precision=HI)
        da = _dot(tri, dacs_s, TN, precision=HI)
        heads = lax.broadcasted_iota(I32, (1, LANE), 1) < HS
        A = jnp.where(heads, -jnp.exp(alog_ref[...]), 0.0)
        ddt = _dot(ddt_ref[...], E, NT, precision=HI) + da * A
        draw = jnp.where(heads, ddt * _sigmoid(raw_ref[...] + bias_ref[...]), 0.0)
        draw_ref[...] = draw
        dsk = _dot(jnp.broadcast_to(jnp.concatenate(dsk_parts, axis=1), (8, INNER)), E, NT, precision=HI)[0:1]
        for ref, val in ((dbias_ref, _colsum(draw)), (dalog_ref, _colsum(da * a_s)), (dskip_ref, dsk)):
            @pl.when(first)
            def _():
                ref[...] = val

            @pl.when(jnp.logical_not(first))
            def _():
                ref[...] += val

    rows = lambda w: pl.BlockSpec((T, w), lambda c: (NC - 1 - c, 0))
    vec = lambda w: pl.BlockSpec((1, w), lambda c: (0, 0))
    return pl.pallas_call(
        body, name="ssd_bwd", grid=(NC,),
        in_specs=[rows(INNER), rows(cfg.CONVCH), rows(INNER), rows(INNER), rows(LANE), vec(INNER),
                  pl.BlockSpec((1, NPAIR, NST, LANE), lambda c: (NC - 1 - c, 0, 0, 0)), rows(LANE), vec(LANE), vec(LANE),
                  pl.BlockSpec((LANE, INNER), lambda c: (0, 0))],
        out_specs=[rows(cfg.CONVCH), rows(LANE), vec(LANE), vec(LANE), vec(LANE)],
        out_shape=[jax.ShapeDtypeStruct((S, cfg.CONVCH), F32), jax.ShapeDtypeStruct((S, LANE), F32)]
        + [jax.ShapeDtypeStruct((1, LANE), F32)] * 3,
        scratch_shapes=[pltpu.VMEM((NPAIR, NST, LANE), F32), pltpu.VMEM((T, LANE), F32), pltpu.VMEM((LANE, T), F32),
                        pltpu.VMEM((T, INNER), F32), pltpu.VMEM((T, INNER), F32)],
        compiler_params=_params(("arbitrary",)),
    )(dy, xc, dt_exp, a_exp, a_small, dskip_exp, hin, dt_raw, dt_bias_pad, a_log_pad, expand)


def _ssd_post(cfg, y, z, norm_g):
    W = cfg.INNER // cfg.G

    def fn(y, z, g):
        yz = y * z * _sigmoid(z)
        return jnp.concatenate([yz[:, i * W:(i + 1) * W] * _rs(yz[:, i * W:(i + 1) * W]) for i in range(cfg.G)], axis=1) * g

    return _rowwise("ssd_post", fn, [y, z], [norm_g], [(cfg.INNER, BF16)], [], _pick(cfg.S, 256, 8))[0]


def _ssd_post_bwd(cfg, db, y, z, norm_g):
    W = cfg.INNER // cfg.G

    def fn(db, y, z, g):
        sg = _sigmoid(z)
        yz = y * z * sg
        dn = db * g
        dyz, nh = [], []
        for i in range(cfg.G):
            seg = yz[:, i * W:(i + 1) * W]
            r = _rs(seg)
            nh.append(seg * r)
            dyz.append(_rms_back(nh[-1], r, dn[:, i * W:(i + 1) * W]))
        dyz = jnp.concatenate(dyz, axis=1)
        return dyz * z * sg, dyz * y * sg * (1.0 + z * (1.0 - sg)), _colsum(db * jnp.concatenate(nh, axis=1))

    return _rowwise("ssd_post_bwd", fn, [db, y, z], [norm_g], [(cfg.INNER, F32), (cfg.INNER, F32)], [(1, cfg.INNER)],
                    _pick(cfg.S, 256, 8))


def _local_grads(cfg, x, tgt, W, sp, out_weight=None, ffn_weights=None, early_grads_ready=None, in_grad_ready=None):
    S, D, H, INNER = cfg.S, cfg.D, cfg.H, cfg.INNER
    ts = _pick(S, 256, 8)
    tc = 256

    xn = _rowwise("rms_pre", lambda x, g: x * _rs(x) * g, [x], [sp["mix_pre_g"]], [(D, BF16)], [], ts)[0]
    u = _matmul("mm_in", xn, W["w_in"], "nn", F32)
    c_q, c_kv = u[:, :cfg.QL], u[:, cfg.QL:cfg.o_kr]
    kr = u[:, cfg.o_kr:cfg.o_z]
    z = u[:, cfg.o_z:cfg.o_xbc]
    xbc = u[:, cfg.o_xbc:cfg.o_dt]
    dt_raw = u[:, cfg.o_dt:]

    cqn = _rowwise("rms_q", lambda x, g: x * _rs(x) * g, [c_q], [sp["q_norm_g"]], [(cfg.QL, BF16)], [], ts)[0]
    ckvn = _rowwise("rms_kv", lambda x, g: x * _rs(x) * g, [c_kv], [sp["kv_norm_g"]], [(cfg.KVL, BF16)], [], ts)[0]
    q = _matmul("mm_uq", cqn, W["w_uq"], "nn", F32)
    kv = _matmul("mm_ukv", ckvn, W["w_ukv"], "nn", F32)
    cos2, sin2 = _rope_tables(S)
    Qh, Kh, Vh = _mla_pack(cfg, q, kv, kr, cos2, sin2)
    a_out, lse, lse_t = _attn_fwd(cfg, Qh, Kh, Vh)
    if out_weight is not None:
        sp = dict(sp, ssm_conv_b=sp["ssm_conv_b"] + out_weight.pass_on(a_out)[0, 0])

    pad = lambda v: jnp.pad(v, ((0, 0), (0, LANE - v.shape[1])))
    expand = _expand_matrix(cfg)
    dt_bias_pad, a_log_pad = pad(sp["dt_bias"]), pad(sp["a_log"])
    dskip_exp = jnp.repeat(sp["d_skip"], HP, axis=1)
    xc = _colwise("ssm_act", _ssm_act, [xbc], [sp["ssm_conv_w"], sp["ssm_conv_b"]], [F32], [], tc)[0]
    dt_s, a_s, dt_exp, a_exp = _ssd_prep(cfg, dt_raw, dt_bias_pad, a_log_pad, expand)
    y_ssd, hin = _ssd_fwd(cfg, xc, dt_exp, a_exp, a_s, dskip_exp)
    b_out = _ssd_post(cfg, y_ssd, z, sp["ssm_norm_g"])

    ab_out = jnp.concatenate([a_out.astype(BF16), b_out], axis=1)
    if out_weight is not None:
        W = dict(W, **out_weight.arrived(ab_out))
    if ffn_weights is not None:
        sp = dict(sp, mix_post_g=sp["mix_post_g"] + ffn_weights.pass_on(ab_out)[0, 0])
    mix = _matmul("mm_out", ab_out, W["w_out"], "nn", F32)

    def mid(x, mix, g_mp, g_fp):
        x1 = x + mix * _rs(mix) * g_mp
        return x1, x1 * _rs(x1) * g_fp

    x1, h2 = _rowwise("fwd_mid", mid, [x, mix], [sp["mix_post_g"], sp["ffn_pre_g"]], [(D, F32), (D, BF16)], [], ts)
    if ffn_weights is not None:
        W = dict(W, **ffn_weights.arrived(h2))
    gate_pre = _matmul("mm_gate", h2, W["w_gate"], "nn", F32, chips=True)
    up = _matmul("mm_up", h2, W["w_up"], "nn", F32, chips=True)
    act = _colwise("ffn_act", _ffn_act, [gate_pre, up], [sp["ffn_conv_w"], sp["ffn_conv_b"]], [BF16], [], tc)[0]
    f = _matmul("mm_down", act, W["w_down"], "nn", F32)

    def final(x1, f, t, g):
        r = _rs(f)
        fh = f * r
        err = x1 + fh * g - t
        loss = 0.5 * jnp.sum(jnp.mean(err * err, axis=-1, keepdims=True), axis=0, keepdims=True)
        dy = err * (1.0 / D)
        return dy, _rms_back(fh, r, dy * g), _colsum(dy * fh), loss

    dy, df, g_ffn_post, loss = _rowwise("final", final, [x1, f, tgt], [sp["ffn_post_g"]], [(D, F32), (D, BF16)],
                                        [(1, D), (1, LANE)], ts)
    gW = {}
    dact = _matmul("mm_down_dx", df, W["w_down"], "nt", F32)
    gW["w_down"] = _matmul("mm_down_dw", act, df, "tn", BF16)
    dgate, dup, g_ffn_conv_w, g_ffn_conv_b = _colwise(
        "ffn_act_bwd", _ffn_act_back, [dact, gate_pre, up], [sp["ffn_conv_w"], sp["ffn_conv_b"]], [BF16, BF16], [FFN_K, 1], tc)
    dh2 = _matmul("mm_gu_dx", dgate, W["w_gate"], "nt", F32, dup, W["w_up"], chips=True)
    gW["w_gate"] = _matmul("mm_gate_dw", h2, dgate, "tn", BF16, chips=True)
    gW["w_up"] = _matmul("mm_up_dw", h2, dup, "tn", BF16, chips=True)

    def mid_back(dy, dh2, x1, mix, g_mp, g_fp):
        r2 = _rs(x1)
        xh = x1 * r2
        dx1 = dy + _rms_back(xh, r2, dh2 * g_fp)
        r1 = _rs(mix)
        mh = mix * r1
        return dx1, _rms_back(mh, r1, dx1 * g_mp), _colsum(dh2 * xh), _colsum(dx1 * mh)

    dx1, dmix, g_ffn_pre, g_mix_post = _rowwise("bwd_mid", mid_back, [dy, dh2, x1, mix], [sp["mix_post_g"], sp["ffn_pre_g"]],
                                                [(D, F32), (D, BF16)], [(1, D), (1, D)], ts)
    dab_out = _matmul("mm_out_dx", dmix, W["w_out"], "nt", F32)
    db_out = dab_out[:, cfg.MLAW:]
    gW["w_out"] = _matmul("mm_out_dw", ab_out, dmix, "tn", BF16)
    if early_grads_ready is not None:
        token = early_grads_ready({n: gW[n] for n in ("w_down", "w_gate", "w_up", "w_out")})
        sp = dict(sp, ssm_norm_g=sp["ssm_norm_g"] + token[0, 0])

    dy_ssd, dz, g_ssm_norm = _ssd_post_bwd(cfg, db_out, y_ssd, z, sp["ssm_norm_g"])
    dxc, ddt_raw, g_dt_bias, g_a_log, g_d_skip = _ssd_bwd(cfg, dy_ssd, xc, dt_exp, a_exp, a_s, dskip_exp, hin, dt_raw,
                                                          dt_bias_pad, a_log_pad, expand)
    dxbc, g_ssm_conv_w, g_ssm_conv_b = _colwise("ssm_act_bwd", _ssm_act_back, [dxc, xbc], [sp["ssm_conv_w"], sp["ssm_conv_b"]],
                                                [BF16], [SSM_K, 1], tc)

    dQ, delta_t = _attn_dq(cfg, Qh, Kh, Vh, dab_out, a_out, lse)
    dK, dV = _attn_dkv(cfg, Qh, Kh, Vh, dab_out, lse_t, delta_t)
    dq, dkv, dkr = _mla_unpack(cfg, dQ, dK, dV, cos2, sin2)
    dcqn = _matmul("mm_uq_dx", dq, W["w_uq"], "nt", F32)
    dckvn = _matmul("mm_ukv_dx", dkv, W["w_ukv"], "nt", F32)
    gW["w_uq"] = _matmul("mm_uq_dw", cqn, dq, "tn", BF16)
    gW["w_ukv"] = _matmul("mm_ukv_dw", ckvn, dkv, "tn", BF16)

    def rms_back(x, dy, g):
        r = _rs(x)
        xh = x * r
        return _rms_back(xh, r, dy * g), _colsum(dy * xh)

    dc_q, g_q_norm = _rowwise("rms_q_bwd", rms_back, [c_q, dcqn], [sp["q_norm_g"]], [(cfg.QL, BF16)], [(1, cfg.QL)], ts)
    dc_kv, g_kv_norm = _rowwise("rms_kv_bwd", rms_back, [c_kv, dckvn], [sp["kv_norm_g"]], [(cfg.KVL, BF16)], [(1, cfg.KVL)], ts)

    du = jnp.concatenate([dc_q, dc_kv, dkr, dz.astype(BF16), dxbc, ddt_raw.astype(BF16)], axis=1)
    gW["w_in"] = _matmul("mm_in_dw", xn, du, "tn", BF16)
    if in_grad_ready is not None:
        token = in_grad_ready({n: gW[n] for n in ("w_in", "w_uq", "w_ukv")})
        sp = dict(sp, mix_pre_g=sp["mix_pre_g"] + token[0, 0])
    dxn = _matmul("mm_in_dx", du, W["w_in"], "nt", F32)

    def first_back(dx1, dxn, x, g):
        r = _rs(x)
        xh = x * r
        return dx1 + _rms_back(xh, r, dxn * g), _colsum(dxn * xh)

    grad_x, g_mix_pre = _rowwise("bwd_first", first_back, [dx1, dxn, x], [sp["mix_pre_g"]], [(D, F32)], [(1, D)], ts)

    gs = dict(mix_pre_g=g_mix_pre, q_norm_g=g_q_norm, kv_norm_g=g_kv_norm, ssm_conv_w=g_ssm_conv_w, ssm_conv_b=g_ssm_conv_b,
              dt_bias=g_dt_bias[:, :cfg.HS], a_log=g_a_log[:, :cfg.HS], d_skip=g_d_skip[:, :cfg.HS], ssm_norm_g=g_ssm_norm,
              mix_post_g=g_mix_post, ffn_pre_g=g_ffn_pre, ffn_conv_w=g_ffn_conv_w, ffn_conv_b=g_ffn_conv_b,
              ffn_post_g=g_ffn_post)
    return loss, grad_x, gW, gs


def _to_kernel_layout(cfg, name, w):
    if name == "w_in":
        a = cfg.o_kr + ROPE
        return jnp.concatenate([w[:, :a], jnp.zeros((w.shape[0], LANE - ROPE), w.dtype), w[:, a:],
                                jnp.zeros((w.shape[0], LANE - cfg.HS), w.dtype)], axis=1)
    if name in ("w_uq", "w_ukv"):
        per = NOPE + (ROPE if name == "w_uq" else VH)
        return jnp.concatenate([w[:, h * per:h * per + NOPE] for h in range(cfg.H)]
                               + [w[:, h * per + NOPE:(h + 1) * per] for h in range(cfg.H)], axis=1)
    return w


def _from_kernel_layout(cfg, name, g):
    if name == "w_in":
        return jnp.concatenate([g[:, :cfg.o_kr + ROPE], g[:, cfg.o_z:cfg.o_dt + cfg.HS]], axis=1)
    if name in ("w_uq", "w_ukv"):
        second = ROPE if name == "w_uq" else VH
        base = cfg.H * NOPE
        parts = []
        for h in range(cfg.H):
            parts += [g[:, h * NOPE:(h + 1) * NOPE], g[:, base + h * second:base + (h + 1) * second]]
        return jnp.concatenate(parts, axis=1)
    return g


def _cols_to_chips(w):
    r, c = w.shape
    return w.reshape(r, N_CHIPS, c // N_CHIPS).transpose(1, 0, 2)


def _chips_to_cols(g):
    k, r, cs = g.shape
    return g.transpose(1, 0, 2).reshape(r, k * cs)


_CHIP_MAJOR = ("w_gate", "w_up")
_RELAYOUT = ("w_in", "w_uq", "w_ukv")
_LAYOUT_ROWS = 256


def _gathered_to_kernel(cfg, name, wg):
    if name in _CHIP_MAJOR:
        return wg
    if name not in _RELAYOUT:
        return wg.reshape(wg.shape[0] * wg.shape[1], wg.shape[2])
    _, rows, cs = wg.shape
    tr = _pick(rows, _LAYOUT_ROWS, 16)

    def body(w_ref, o_ref):
        o_ref[...] = _to_kernel_layout(cfg, name, jnp.concatenate([w_ref[k] for k in range(N_CHIPS)], axis=1))

    wide = jax.eval_shape(lambda w: _to_kernel_layout(cfg, name, w), jax.ShapeDtypeStruct((rows, N_CHIPS * cs), wg.dtype)).shape[1]
    return pl.pallas_call(
        body, name="layout_" + name, grid=(rows // tr,),
        in_specs=[pl.BlockSpec((N_CHIPS, tr, cs), lambda i: (0, i, 0))], out_specs=pl.BlockSpec((tr, wide), lambda i: (i, 0)),
        out_shape=jax.ShapeDtypeStruct((rows, wide), wg.dtype), compiler_params=_params(("parallel",)),
    )(wg)


def _grad_to_chips(cfg, name, g):
    if name in _CHIP_MAJOR:
        return g
    if name not in _RELAYOUT:
        return g.reshape(N_CHIPS, g.shape[0] // N_CHIPS, g.shape[1])
    rows, wide = g.shape
    tr = _pick(rows, _LAYOUT_ROWS, 16)
    cs = jax.eval_shape(lambda v: _from_kernel_layout(cfg, name, v), g).shape[1] // N_CHIPS

    def body(g_ref, o_ref):
        nat = _from_kernel_layout(cfg, name, g_ref[...])
        for k in range(N_CHIPS):
            o_ref[k] = nat[:, k * cs:(k + 1) * cs]

    return pl.pallas_call(
        body, name="layout_grad_" + name, grid=(rows // tr,),
        in_specs=[pl.BlockSpec((tr, wide), lambda i: (i, 0))], out_specs=pl.BlockSpec((N_CHIPS, tr, cs), lambda i: (0, i, 0)),
        out_shape=jax.ShapeDtypeStruct((N_CHIPS, rows, cs), g.dtype), compiler_params=_params(("parallel",)),
    )(g)


def _me():
    return lax.axis_index("x"), lax.axis_index("y"), lax.axis_index("c")


def _other_chips(x, y):
    return [(1 - x, y), (x, 1 - y), (1 - x, 1 - y)]


_ANY = pl.BlockSpec(memory_space=pl.ANY)


def _row_block(rows, cols, mult):
    return _pick(rows, max(mult, (1 << 19) // cols // mult * mult), mult)


def _scalar(v):
    return v.astype(I32).reshape(1)


def _stage_shard(name, w, chip):
    _, rs, cs = w.shape
    tr = _row_block(rs, cs, 16)

    def body(chip_ref, w_ref, o_ref):
        o_ref[...] = w_ref[...].astype(BF16)

    return pl.pallas_call(
        body, name="stage_" + name,
        grid_spec=pltpu.PrefetchScalarGridSpec(
            num_scalar_prefetch=1, grid=(rs // tr,),
            in_specs=[pl.BlockSpec((None, tr, cs), lambda i, chip_ref: (0, i, 0))],
            out_specs=pl.BlockSpec((None, tr, cs), lambda i, chip_ref: (chip_ref[0], i, 0))),
        out_shape=jax.ShapeDtypeStruct((N_CHIPS, rs, cs), BF16),
        compiler_params=_params(("parallel",)),
    )(_scalar(chip), w)


def _half(ref, k, half):
    h = ref.shape[1] // 2
    return ref.at[k, pl.ds(pl.multiple_of(half * h, 16), h), :]


def _allgather_weights(bufs):
    n = len(bufs)

    def body(*refs):
        outs, send_sems, recv_sems = refs[n:2 * n], refs[2 * n], refs[2 * n + 1]
        x, y, c = _me()
        chip = 2 * x + y
        sib = (x, y, 1 - c)
        chips = _other_chips(x, y)

        def copy(k, part, to):
            return pltpu.make_async_remote_copy(src_ref=part, dst_ref=part, send_sem=send_sems.at[k], recv_sem=recv_sems.at[k],
                                                device_id=to, device_id_type=MESH_ID)

        started = []
        for w, o_ref in enumerate(outs):
            for j, (cx, cy) in enumerate(chips):
                started.append(copy(6 * w + j, _half(o_ref, chip, c), (cx, cy, c)))
                started[-1].start()
        for w, o_ref in enumerate(outs):
            for j, (cx, cy) in enumerate(chips):
                theirs = _half(o_ref, 2 * cx + cy, c)
                copy(6 * w + j, theirs, sib).wait_recv()
                started.append(copy(6 * w + 3 + j, theirs, sib))
                started[-1].start()
        for w, o_ref in enumerate(outs):
            for j, (cx, cy) in enumerate(chips):
                copy(6 * w + 3 + j, _half(o_ref, 2 * cx + cy, 1 - c), sib).wait_recv()
        for cp in started:
            cp.wait_send()

    return pl.pallas_call(
        body, name="allgather_weights", in_specs=[_ANY] * n, out_specs=[_ANY] * n,
        out_shape=[jax.ShapeDtypeStruct(b.shape, b.dtype) for b in bufs],
        input_output_aliases={i: i for i in range(n)},
        scratch_shapes=[pltpu.SemaphoreType.DMA((6 * n,)), pltpu.SemaphoreType.DMA((6 * n,))],
    )(*bufs)


_HBM = pl.BlockSpec(memory_space=pltpu.HBM)
_SEM = pl.BlockSpec(memory_space=pltpu.SEMAPHORE)
_EFFECT = pltpu.SideEffectType.DATAFLOW_SIDE_EFFECTING


def _split_start(name, bufs, n_copies, copies):
    n = len(bufs)

    def body(*refs):
        for cp in copies(refs[:n], refs[n], refs[n + 1]):
            cp.start()
        refs[-1][...] = jnp.zeros_like(refs[-1])

    res = pl.pallas_call(
        body, name=name,
        out_shape=(pltpu.SemaphoreType.DMA((n_copies,)), pltpu.SemaphoreType.DMA((n_copies,)),
                   *[pltpu.HBM(b.shape, b.dtype) for b in bufs], jax.ShapeDtypeStruct((8, LANE), F32)),
        in_specs=[_HBM] * n, out_specs=(_SEM, _SEM, *[_HBM] * n, pl.BlockSpec(memory_space=pltpu.VMEM)),
        input_output_aliases={i: 2 + i for i in range(n)},
        compiler_params=pltpu.CompilerParams(has_side_effects=_EFFECT),
    )(*[pltpu.with_memory_space_constraint(b, pltpu.HBM) for b in bufs])
    return res[0], res[1], list(res[2:2 + n]), res[-1]


def _split_wait(name, send_sems, recv_sems, bufs, after, copies):
    n = len(bufs)

    def body(*refs):
        for cp in copies(refs[:n], refs[n], refs[n + 1]):
            cp.wait_send()
            cp.wait_recv()

    return list(pl.pallas_call(
        body, name=name, out_shape=[pltpu.HBM(b.shape, b.dtype) for b in bufs],
        in_specs=[_HBM] * n + [_SEM, _SEM, _ANY], out_specs=[_HBM] * n,
        input_output_aliases={i: i for i in range(n)},
        compiler_params=pltpu.CompilerParams(has_side_effects=_EFFECT),
    )(*bufs, send_sems, recv_sems, after))


def _gather_to_chips(bufs, send_sems, recv_sems):
    x, y, c = _me()
    return [pltpu.make_async_remote_copy(src_ref=_half(b, 2 * x + y, c), dst_ref=_half(b, 2 * x + y, c),
                                         send_sem=send_sems.at[3 * w + j], recv_sem=recv_sems.at[3 * w + j],
                                         device_id=(cx, cy, c), device_id_type=MESH_ID)
            for w, b in enumerate(bufs) for j, (cx, cy) in enumerate(_other_chips(x, y))]


def _gather_to_sibling(bufs, send_sems, recv_sems):
    x, y, c = _me()
    return [pltpu.make_async_remote_copy(src_ref=_half(b, 2 * cx + cy, c), dst_ref=_half(b, 2 * cx + cy, c),
                                         send_sem=send_sems.at[3 * w + j], recv_sem=recv_sems.at[3 * w + j],
                                         device_id=(x, y, 1 - c), device_id_type=MESH_ID)
            for w, b in enumerate(bufs) for j, (cx, cy) in enumerate(_other_chips(x, y))]


def _pair_exchange(name, grads):
    n = len(grads)

    def body(*refs):
        ins, outs, send_sems, recv_sems = refs[:n], refs[n:2 * n], refs[2 * n], refs[2 * n + 1]
        x, y, c = _me()
        cps = []
        for w, (g_ref, o_ref) in enumerate(zip(ins, outs)):
            h = o_ref.shape[1]
            src = g_ref.at[:, pl.ds(pl.multiple_of((1 - c) * h, 16), h), :]
            cps.append(pltpu.make_async_remote_copy(src_ref=src, dst_ref=o_ref, send_sem=send_sems.at[w], recv_sem=recv_sems.at[w],
                                                    device_id=(x, y, 1 - c), device_id_type=MESH_ID))
            cps[-1].start()
        for cp in cps:
            cp.wait()

    return pl.pallas_call(
        body, name="pair_exchange_" + name, in_specs=[_ANY] * n, out_specs=[_ANY] * n,
        out_shape=[jax.ShapeDtypeStruct((g.shape[0], g.shape[1] // 2, g.shape[2]), g.dtype) for g in grads],
        scratch_shapes=[pltpu.SemaphoreType.DMA((n,)), pltpu.SemaphoreType.DMA((n,))],
    )(*grads)


def _pair_sum(name, g, theirs, c):
    _, h, cs = theirs.shape
    tr = _row_block(h, cs, 16)
    nb = h // tr

    def body(c_ref, a_ref, b_ref, o_ref):
        o_ref[...] = (a_ref[...].astype(F32) + b_ref[...].astype(F32)).astype(o_ref.dtype)

    return pl.pallas_call(
        body, name="pair_sum_" + name,
        grid_spec=pltpu.PrefetchScalarGridSpec(
            num_scalar_prefetch=1, grid=(N_CHIPS, nb),
            in_specs=[pl.BlockSpec((None, tr, cs), lambda k, i, c_ref: (k, c_ref[0] * nb + i, 0)),
                      pl.BlockSpec((None, tr, cs), lambda k, i, c_ref: (k, i, 0))],
            out_specs=pl.BlockSpec((None, tr, cs), lambda k, i, c_ref: (k, i, 0))),
        out_shape=jax.ShapeDtypeStruct(theirs.shape, BF16),
        compiler_params=_params(("parallel", "parallel")),
    )(_scalar(c), g, theirs)


def _chip_copies(srcs, lands, send_sems, recv_sems):
    x, y, c = _me()
    return [pltpu.make_async_remote_copy(src_ref=s_ref.at[2 * cx + cy], dst_ref=l_ref.at[j], send_sem=send_sems.at[3 * w + j],
                                         recv_sem=recv_sems.at[3 * w + j], device_id=(cx, cy, c), device_id_type=MESH_ID)
            for w, (s_ref, l_ref) in enumerate(zip(srcs, lands)) for j, (cx, cy) in enumerate(_other_chips(x, y))]


def _chip_exchange_start(name, sums):
    n = len(sums)
    lands = [lax.empty((3,) + s.shape[1:], s.dtype) for s in sums]
    send_sems, recv_sems, bufs, token = _split_start(
        "chip_exchange_start_" + name, [*sums, *lands], 3 * n, lambda refs, ss, rs: _chip_copies(refs[:n], refs[n:], ss, rs))
    return send_sems, recv_sems, bufs[:n], bufs[n:], token


def _chip_exchange_wait(name, send_sems, recv_sems, sums, lands, after):
    n = len(sums)
    bufs = _split_wait("chip_exchange_wait_" + name, send_sems, recv_sems, [*sums, *lands], after,
                       lambda refs, ss, rs: _chip_copies(refs[:n], refs[n:], ss, rs))
    return bufs[:n], bufs[n:]


def _chip_exchange(name, sums):
    n = len(sums)

    def body(*refs):
        cps = _chip_copies(refs[:n], refs[n:2 * n], refs[2 * n], refs[2 * n + 1])
        for cp in cps:
            cp.start()
        for cp in cps:
            cp.wait()

    return pl.pallas_call(
        body, name="chip_exchange_" + name, in_specs=[_ANY] * n, out_specs=[_ANY] * n,
        out_shape=[jax.ShapeDtypeStruct((3,) + s.shape[1:], s.dtype) for s in sums],
        scratch_shapes=[pltpu.SemaphoreType.DMA((3 * n,)), pltpu.SemaphoreType.DMA((3 * n,))],
    )(*sums)


def _chip_sum(name, sums, theirs, chip):
    _, h, cs = sums.shape
    tr = _row_block(h, cs, 16)

    def body(chip_ref, s_ref, t_ref, o_ref):
        acc = s_ref[...].astype(F32)
        for k in range(3):
            acc = acc + t_ref[k].astype(F32)
        o_ref[...] = acc

    return pl.pallas_call(
        body, name="chip_sum_" + name,
        grid_spec=pltpu.PrefetchScalarGridSpec(
            num_scalar_prefetch=1, grid=(h // tr,),
            in_specs=[pl.BlockSpec((None, tr, cs), lambda i, chip_ref: (chip_ref[0], i, 0)),
                      pl.BlockSpec((3, tr, cs), lambda i, chip_ref: (0, i, 0))],
            out_specs=pl.BlockSpec((tr, cs), lambda i, chip_ref: (i, 0))),
        out_shape=jax.ShapeDtypeStruct((h, cs), F32),
        compiler_params=_params(("parallel",)),
    )(_scalar(chip), sums, theirs)


def _sibling_exchange(halves):
    n = len(halves)

    def body(*refs):
        ins, outs, send_sems, recv_sems = refs[:n], refs[n:2 * n], refs[2 * n], refs[2 * n + 1]
        x, y, c = _me()
        cps = []
        for w, (h_ref, o_ref) in enumerate(zip(ins, outs)):
            cps.append(pltpu.make_async_remote_copy(src_ref=h_ref, dst_ref=o_ref, send_sem=send_sems.at[w], recv_sem=recv_sems.at[w],
                                                    device_id=(x, y, 1 - c), device_id_type=MESH_ID))
            cps[-1].start()
        for cp in cps:
            cp.wait()

    return pl.pallas_call(
        body, name="grad_sibling_exchange", in_specs=[_ANY] * n, out_specs=[_ANY] * n,
        out_shape=[jax.ShapeDtypeStruct(h.shape, h.dtype) for h in halves],
        scratch_shapes=[pltpu.SemaphoreType.DMA((n,)), pltpu.SemaphoreType.DMA((n,))],
    )(*halves)


def _allreduce_small(name, vec):
    def body(v_ref, o_ref, buf_ref, send_sems, recv_sems):
        x, y, c = _me()
        me = 4 * x + 2 * y + c
        cps = []
        for p in range(1, 8):
            px, py, pc = x ^ (p >> 2), y ^ ((p >> 1) & 1), c ^ (p & 1)
            cps.append(pltpu.make_async_remote_copy(src_ref=v_ref, dst_ref=buf_ref.at[me], send_sem=send_sems.at[p - 1],
                                                    recv_sem=recv_sems.at[p - 1], device_id=(px, py, pc), device_id_type=MESH_ID))
            cps[-1].start()
        buf_ref[me] = v_ref[...]
        for p in range(1, 8):
            theirs = buf_ref.at[me ^ p]
            pltpu.make_async_remote_copy(src_ref=theirs, dst_ref=theirs, send_sem=send_sems.at[p - 1], recv_sem=recv_sems.at[p - 1],
                                         device_id=(x, y, c), device_id_type=MESH_ID).wait_recv()
        for cp in cps:
            cp.wait_send()
        acc = buf_ref[0]
        for k in range(1, 8):
            acc = acc + buf_ref[k]
        o_ref[...] = acc

    vm = pl.BlockSpec(memory_space=pltpu.VMEM)
    return pl.pallas_call(
        body, name=name, in_specs=[vm], out_specs=vm, out_shape=jax.ShapeDtypeStruct(vec.shape, F32),
        scratch_shapes=[pltpu.VMEM((8,) + vec.shape, F32), pltpu.SemaphoreType.DMA((7,)), pltpu.SemaphoreType.DMA((7,))],
    )(vec)


def _adam_math(w, g, m, v):
    m = ADAM_B1 * m + (1.0 - ADAM_B1) * g
    v = ADAM_B2 * v + (1.0 - ADAM_B2) * (g * g)
    m_hat = m / (1.0 - ADAM_B1 ** ADAM_STEP)
    v_hat = v / (1.0 - ADAM_B2 ** ADAM_STEP)
    return -ADAM_LR * (m_hat / (jnp.sqrt(v_hat) + ADAM_EPS) + ADAM_WD * w), m, v


def _adamw(name, w, g, m, v):
    R, C = w.shape
    tr = _row_block(R, C, 8)

    def body(w_ref, g_ref, m_ref, v_ref, d_ref, nm_ref, nv_ref):
        d_ref[...], nm_ref[...], nv_ref[...] = _adam_math(w_ref[...], g_ref[...], m_ref[...], v_ref[...])

    blk = pl.BlockSpec((tr, C), lambda i: (i, 0))
    return pl.pallas_call(
        body, name=name, grid=(R // tr,), in_specs=[blk] * 4, out_specs=[blk] * 3,
        out_shape=[jax.ShapeDtypeStruct((R, C), F32)] * 3, compiler_params=_params(("parallel",)),
    )(w, g, m, v)


def _adamw_halves(name, w, mine, theirs, m, v, c):
    _, rs, cs = w.shape
    h = rs // 2
    tr = _row_block(h, cs, 8)
    nb = h // tr

    def body(c_ref, w_ref, a_ref, b_ref, m_ref, v_ref, g_ref, d_ref, nm_ref, nv_ref):
        g = jnp.where(pl.program_id(0) == c_ref[0], a_ref[...], b_ref[...])
        g_ref[...] = g
        d_ref[...], nm_ref[...], nv_ref[...] = _adam_math(w_ref[...], g, m_ref[...], v_ref[...])

    full = pl.BlockSpec((None, tr, cs), lambda s, i, c_ref: (0, s * nb + i, 0))
    part = pl.BlockSpec((tr, cs), lambda s, i, c_ref: (i, 0))
    return pl.pallas_call(
        body, name=name,
        grid_spec=pltpu.PrefetchScalarGridSpec(num_scalar_prefetch=1, grid=(2, nb), in_specs=[full, part, part, full, full],
                                               out_specs=[full] * 4),
        out_shape=[jax.ShapeDtypeStruct((1, rs, cs), F32)] * 4, compiler_params=_params(("parallel", "parallel")),
    )(_scalar(c), w, mine, theirs, m, v)


def _pack_small(arrs):
    flat = jnp.concatenate([a.reshape(-1) for a in arrs])
    n = -(-flat.shape[0] // (8 * LANE)) * 8 * LANE
    return jnp.pad(flat, (0, n - flat.shape[0])).reshape(8, n // 8)


def _unpack_small(vec, shapes):
    flat, out, off = vec.reshape(-1), [], 0
    for s in shapes:
        out.append(flat[off:off + s[0] * s[1]].reshape(s))
        off += s[0] * s[1]
    return out


class _LateWeights:
    def __init__(self, cfg, tag, names, staged):
        self.cfg, self.tag, self.names, self.k = cfg, tag, names, 3 * len(names)
        self.send, self.recv, self.bufs, self.token = _split_start(f"gather_{tag}_chips_start", staged, self.k, _gather_to_chips)

    def pass_on(self, after):
        bufs = _split_wait(f"gather_{self.tag}_chips_wait", self.send, self.recv, self.bufs, after, _gather_to_chips)
        self.send, self.recv, self.bufs, token = _split_start(f"gather_{self.tag}_sibling_start", bufs, self.k, _gather_to_sibling)
        return token

    def arrived(self, after):
        bufs = _split_wait(f"gather_{self.tag}_sibling_wait", self.send, self.recv, self.bufs, after, _gather_to_sibling)
        return {n: _gathered_to_kernel(self.cfg, n, b) for n, b in zip(self.names, bufs)}


def _step(cfg, a):
    chip = 2 * lax.axis_index("x") + lax.axis_index("y")
    core = lax.axis_index("c")
    big = [n for n, _, _, _ in cfg.BIG]

    ffn = ("w_gate", "w_up", "w_down")
    first = ("w_in", "w_uq", "w_ukv")
    staged = {n: _stage_shard(n, a[n], chip) for n in big}
    W = {n: _gathered_to_kernel(cfg, n, wg) for n, wg in zip(first, _allgather_weights([staged[n] for n in first]))}
    out_weight = _LateWeights(cfg, "out", ("w_out",), [staged["w_out"]])
    ffn_weights = _LateWeights(cfg, "ffn", ffn, [staged[n] for n in ffn])

    sp = {n: a[n] for n in SMALL}
    sharded = _pack_small([a[n] for n in SMALL_SHARDED])
    slot = jnp.where(lax.broadcasted_iota(I32, (N_CHIPS,) + sharded.shape, 0) == chip, 0.5 * sharded[None], 0.0)
    allp = _allreduce_small("allgather_small", slot.reshape(N_CHIPS * 8, -1)).reshape((N_CHIPS,) + sharded.shape)
    per_chip = [_unpack_small(allp[ch], [a[n].shape for n in SMALL_SHARDED]) for ch in range(N_CHIPS)]
    for k, n in enumerate(SMALL_SHARDED):
        sp[n] = jnp.concatenate([per_chip[ch][k] for ch in range(N_CHIPS)], axis=1)
    sp["mix_pre_g"] = sp["mix_pre_g"] + (out_weight.token[0, 0] + ffn_weights.token[0, 0])

    started = {}

    def start_exchange(tag, names, grads):
        grads = [_grad_to_chips(cfg, n, grads[n]) for n in names]
        sums = [_pair_sum(n, g, t, core) for n, g, t in zip(names, grads, _pair_exchange(tag, grads))]
        started[tag] = (names, _chip_exchange_start(tag, sums))
        return started[tag][1][-1]

    early = ("w_down", "w_gate", "w_up", "w_out")
    loss, grad_x, gW, gs = _local_grads(cfg, a["x"], a["loss_target"], W, sp, out_weight, ffn_weights,
                                        functools.partial(start_exchange, "early", early),
                                        functools.partial(start_exchange, "rest", first))
    sums, landed, after = {}, {}, grad_x
    for tag in ("rest", "early"):
        names, (send_sems, recv_sems, s_bufs, l_bufs, _) = started[tag]
        s_bufs, l_bufs = _chip_exchange_wait(tag, send_sems, recv_sems, s_bufs, l_bufs, after)
        sums.update(zip(names, s_bufs))
        landed.update(zip(names, l_bufs))
        after = l_bufs[0]
    mine = [_chip_sum(n, sums[n], landed[n], chip) for n in big]
    theirs = _sibling_exchange(mine)

    shapes = [gs[n].shape for n in SMALL] + [(1, LANE)]
    red = _unpack_small(_allreduce_small("allreduce_small", _pack_small([gs[n] for n in SMALL] + [loss])), shapes)
    g_small = dict(zip(SMALL, red[:-1]))
    for n in SMALL_SHARDED:
        cs = a[n].shape[1]
        g_small[n] = lax.dynamic_slice_in_dim(g_small[n], chip * cs, cs, axis=1)

    out = {"loss": red[-1][0, 0], "grad_x": grad_x}
    for n, gm, gt in zip(big, mine, theirs):
        out["grad_" + n], out["delta_" + n], out["new_m_" + n], out["new_v_" + n] = _adamw_halves(
            "adamw_" + n, a[n], gm, gt, a["m_" + n], a["v_" + n], core)
    sshapes = [a[n].shape for n in SMALL]
    d, nm, nv = _adamw("adamw_small", _pack_small([a[n] for n in SMALL]), _pack_small([g_small[n] for n in SMALL]),
                       _pack_small([a["m_" + n] for n in SMALL]), _pack_small([a["v_" + n] for n in SMALL]))
    for n, dd, mm, vv in zip(SMALL, _unpack_small(d, sshapes), _unpack_small(nm, sshapes), _unpack_small(nv, sshapes)):
        out["grad_" + n], out["delta_" + n], out["new_m_" + n], out["new_v_" + n] = g_small[n], dd, mm, vv
    return out


def kernel(x, mix_pre_g, w_in, q_norm_g, w_uq, kv_norm_g, w_ukv, ssm_conv_w, ssm_conv_b, dt_bias, a_log, d_skip, ssm_norm_g, w_out, mix_post_g, ffn_pre_g, w_gate, w_up, ffn_conv_w, ffn_conv_b, w_down, ffn_post_g, loss_target, m_mix_pre_g, m_w_in, m_q_norm_g, m_w_uq, m_kv_norm_g, m_w_ukv, m_ssm_conv_w, m_ssm_conv_b, m_dt_bias, m_a_log, m_d_skip, m_ssm_norm_g, m_w_out, m_mix_post_g, m_ffn_pre_g, m_w_gate, m_w_up, m_ffn_conv_w, m_ffn_conv_b, m_w_down, m_ffn_post_g, v_mix_pre_g, v_w_in, v_q_norm_g, v_w_uq, v_kv_norm_g, v_w_ukv, v_ssm_conv_w, v_ssm_conv_b, v_dt_bias, v_a_log, v_d_skip, v_ssm_norm_g, v_w_out, v_mix_post_g, v_ffn_pre_g, v_w_gate, v_w_up, v_ffn_conv_w, v_ffn_conv_b, v_w_down, v_ffn_post_g):
    args = dict(locals())
    big = {pre + n for n, _, _, _ in _FULL.BIG for pre in ("", "m_", "v_")}
    out = _step(_FULL, {k: (v[0] if v.ndim == 3 and k not in big else v) for k, v in args.items()})
    res = [out["loss"], out["grad_x"][None]]
    for pre in ("grad_", "delta_", "new_m_", "new_v_"):
        res += [out[pre + n][None] if args[n].ndim == 3 and n not in big else out[pre + n] for n in WEIGHTS]
    return tuple(res)
```

```python
import functools
import math

import jax
import jax.numpy as jnp
from jax import lax
from jax.experimental import pallas as pl
from jax.experimental.pallas import tpu as pltpu

F32, BF16, I32 = jnp.float32, jnp.bfloat16, jnp.int32
NN = (((1,), (0,)), ((), ()))
NT = (((1,), (1,)), ((), ()))
TN = (((0,), (0,)), ((), ()))
HI = lax.Precision.HIGHEST
MESH_ID = pl.DeviceIdType.MESH

EPS = 1e-6
CHUNK = 64
NOPE, ROPE, VH = 128, 64, 128
ROPE_THETA = 10000.0
HP, NST = 64, 128
SSM_K, FFN_K = 4, 3
LANE = 128
N_CHIPS = 4
VMEM_LIMIT = 52 * 1024 * 1024
MM_TILE, MM_TILE_K = 1408, 2816

ADAM_LR, ADAM_B1, ADAM_B2, ADAM_EPS, ADAM_WD, ADAM_STEP = 0.001, 0.9, 0.999, 1e-08, 0.01, 10


class _Cfg:
    def __init__(self, S, D, QL, KVL, H, HS, G, DFF, T):
        self.S, self.D, self.QL, self.KVL, self.H, self.HS, self.G, self.DFF, self.T = S, D, QL, KVL, H, HS, G, DFF, T
        self.INNER = HS * HP
        self.CONVCH = self.INNER + 2 * G * NST
        self.QW = H * (NOPE + ROPE)
        self.KVW = H * (NOPE + VH)
        self.MLAW = H * VH
        self.MIXW = self.MLAW + self.INNER
        self.IN_COLS = QL + KVL + ROPE + self.INNER + self.CONVCH + HS
        self.o_kr = QL + KVL
        self.o_z = self.o_kr + LANE
        self.o_xbc = self.o_z + self.INNER
        self.o_dt = self.o_xbc + self.CONVCH
        self.EXT = self.o_dt + LANE
        self.NPAIR = HS // 2
        self.REP = HS // G
        self.BIG = (("w_in", D, self.IN_COLS, 1), ("w_uq", QL, self.QW, 1), ("w_ukv", KVL, self.KVW, 1),
                    ("w_out", self.MIXW, D, 0), ("w_gate", D, DFF, 1), ("w_up", D, DFF, 1), ("w_down", DFF, D, 0))
        self.NSHARD = sum(r * c for _, r, c, _ in self.BIG) // N_CHIPS
        unit = 2 * LANE * 16
        self.NPACK = -(-self.NSHARD // unit) * unit
        self.R = self.NPACK // (2 * LANE)


_FULL = _Cfg(S=2048, D=2048, QL=768, KVL=512, H=8, HS=16, G=2, DFF=5632, T=256)

SMALL = ("mix_pre_g", "q_norm_g", "kv_norm_g", "ssm_conv_w", "ssm_conv_b", "dt_bias", "a_log", "d_skip", "ssm_norm_g",
         "mix_post_g", "ffn_pre_g", "ffn_conv_w", "ffn_conv_b", "ffn_post_g")
SMALL_SHARDED = ("ssm_conv_w", "ffn_conv_w")
WEIGHTS = ("mix_pre_g", "w_in", "q_norm_g", "w_uq", "kv_norm_g", "w_ukv", "ssm_conv_w", "ssm_conv_b", "dt_bias", "a_log",
           "d_skip", "ssm_norm_g", "w_out", "mix_post_g", "ffn_pre_g", "w_gate", "w_up", "ffn_conv_w", "ffn_conv_b",
           "w_down", "ffn_post_g")


def _pick(n, target, mult):
    best = None
    for d in range(mult, min(n, target) + 1, mult):
        if n % d == 0:
            best = d
    return best if best is not None else n


def _params(sem=None):
    kw = dict(vmem_limit_bytes=VMEM_LIMIT)
    if sem is not None:
        kw["dimension_semantics"] = sem
    return pltpu.CompilerParams(**kw)


def _dot(a, b, dims=NN, precision=None):
    return lax.dot_general(a, b, dims, preferred_element_type=F32, precision=precision)


def _sigmoid(x):
    return 1.0 / (1.0 + jnp.exp(-x))


def _rs(x):
    return lax.rsqrt(jnp.mean(x * x, axis=-1, keepdims=True) + EPS)


def _rms_back(xh, r, dn):
    return r * (dn - xh * jnp.mean(dn * xh, axis=-1, keepdims=True))


def _colsum(v):
    return jnp.sum(v, axis=0, keepdims=True)


def _matmul(name, a, b, mode, out_dtype, a2=None, b2=None, chips=False):
    cs = None
    if mode == "nn":
        (M, K), N = a.shape, b.shape[-1]
        if chips:
            cs, N = N, N_CHIPS * N
    elif mode == "nt":
        (M, K), N = a.shape, b.shape[-2]
        if chips:
            cs = b.shape[-1]
    else:
        (K, M), N = a.shape, b.shape[1]
        if chips:
            cs = N // N_CHIPS
    tm = _pick(M, MM_TILE, LANE)
    tn = _pick(cs if chips and mode != "nt" else N, MM_TILE, LANE)
    tk = _pick(cs, MM_TILE, LANE) if chips and mode == "nt" else _pick(K, MM_TILE_K, LANE)
    nk = K // tk
    dims = {"nn": NN, "nt": NT, "tn": TN}[mode]
    a_spec = pl.BlockSpec((tk, tm), lambda i, j, k: (k, i)) if mode == "tn" else pl.BlockSpec((tm, tk), lambda i, j, k: (i, k))
    b_spec = pl.BlockSpec((tn, tk), lambda i, j, k: (j, k)) if mode == "nt" else pl.BlockSpec((tk, tn), lambda i, j, k: (k, j))
    o_spec = pl.BlockSpec((tm, tn), lambda i, j, k: (i, j))
    o_shape = (M, N)
    if chips and mode == "nn":
        per = cs // tn
        b_spec = pl.BlockSpec((None, tk, tn), lambda i, j, k: (j // per, k, j % per))
    elif chips and mode == "nt":
        per = cs // tk
        b_spec = pl.BlockSpec((None, tn, tk), lambda i, j, k: (k // per, j, k % per))
    elif chips:
        per = cs // tn
        o_spec = pl.BlockSpec((None, tm, tn), lambda i, j, k: (j // per, i, j % per))
        o_shape = (N_CHIPS, M, cs)
    two = a2 is not None

    def product(refs):
        part = _dot(refs[0][...].astype(BF16), refs[1][...].astype(BF16), dims)
        if two:
            part += _dot(refs[2][...].astype(BF16), refs[3][...].astype(BF16), dims)
        return part

    def body_whole_k(*refs):
        refs[-1][...] = product(refs).astype(refs[-1].dtype)

    def body(*refs):
        o_ref, acc_ref = refs[-2], refs[-1]
        k = pl.program_id(2)

        @pl.when(k == 0)
        def _():
            acc_ref[...] = product(refs)

        @pl.when(k > 0)
        def _():
            acc_ref[...] += product(refs)

        @pl.when(k == nk - 1)
        def _():
            o_ref[...] = acc_ref[...].astype(o_ref.dtype)

    ins = (a, b, a2, b2) if two else (a, b)
    return pl.pallas_call(
        body_whole_k if nk == 1 else body, name=name, grid=(M // tm, N // tn, nk),
        in_specs=[a_spec, b_spec] * (2 if two else 1),
        out_specs=o_spec,
        out_shape=jax.ShapeDtypeStruct(o_shape, out_dtype),
        scratch_shapes=[] if nk == 1 else [pltpu.VMEM((tm, tn), F32)],
        compiler_params=_params(("parallel", "parallel", "arbitrary")),
    )(*ins)


def _rowwise(name, fn, rows, mats, outs, reds, ts):
    S = rows[0].shape[0]
    nr, nm, no = len(rows), len(mats), len(outs)

    def body(*refs):
        res = fn(*[r[...] for r in refs[:nr + nm]])
        res = res if isinstance(res, (tuple, list)) else (res,)
        for r, v in zip(refs[nr + nm:nr + nm + no], res[:no]):
            r[...] = v.astype(r.dtype)
        first = pl.program_id(0) == 0
        for r, v in zip(refs[nr + nm + no:], res[no:]):
            @pl.when(first)
            def _():
                r[...] = jnp.broadcast_to(v, r.shape)

            @pl.when(jnp.logical_not(first))
            def _():
                r[...] += jnp.broadcast_to(v, r.shape)

    in_specs = [pl.BlockSpec((ts, a.shape[1]), lambda i: (i, 0)) for a in rows]
    in_specs += [pl.BlockSpec(m.shape, lambda i, nd=m.ndim: (0,) * nd) for m in mats]
    out_specs = [pl.BlockSpec((ts, w), lambda i: (i, 0)) for w, _ in outs]
    out_specs += [pl.BlockSpec(s, lambda i: (0, 0)) for s in reds]
    out_shape = [jax.ShapeDtypeStruct((S, w), dt) for w, dt in outs] + [jax.ShapeDtypeStruct(s, F32) for s in reds]
    return pl.pallas_call(
        body, name=name, grid=(S // ts,), in_specs=in_specs, out_specs=out_specs, out_shape=out_shape,
        compiler_params=_params(("arbitrary",) if reds else ("parallel",)),
    )(*rows, *mats)


def _shift_down(v, s):
    if s == 0:
        return v
    rows = lax.broadcasted_iota(I32, v.shape, 0)
    return jnp.where(rows >= s, pltpu.roll(v, s, 0), 0.0)


def _shift_up(v, s):
    if s == 0:
        return v
    n = v.shape[0]
    rows = lax.broadcasted_iota(I32, v.shape, 0)
    return jnp.where(rows < n - s, pltpu.roll(v, n - s, 0), 0.0)


def _conv(x, w, b):
    K = w.shape[0]
    y = jnp.broadcast_to(b, x.shape)
    for k in range(K):
        y = y + w[k:k + 1, :] * _shift_down(x, K - 1 - k)
    return y


def _conv_back(x, w, dc):
    K = w.shape[0]
    dx = jnp.zeros_like(x)
    dw = []
    for k in range(K):
        dx = dx + w[k:k + 1, :] * _shift_up(dc, K - 1 - k)
        dw.append(_colsum(dc * _shift_down(x, K - 1 - k)))
    return dx, jnp.concatenate(dw, axis=0), _colsum(dc)


def _colwise(name, fn, cols, vecs, outs, pouts, tc):
    S, C = cols[0].shape
    nc_, nv, no = len(cols), len(vecs), len(outs)

    def body(*refs):
        res = fn(*[r[...] for r in refs[:nc_ + nv]])
        res = res if isinstance(res, (tuple, list)) else (res,)
        for r, v in zip(refs[nc_ + nv:], res):
            r[...] = v.astype(r.dtype)

    in_specs = [pl.BlockSpec((S, tc), lambda j: (0, j)) for _ in cols]
    in_specs += [pl.BlockSpec((v.shape[0], tc), lambda j: (0, j)) for v in vecs]
    out_specs = [pl.BlockSpec((S, tc), lambda j: (0, j)) for _ in outs] + [pl.BlockSpec((k, tc), lambda j: (0, j)) for k in pouts]
    out_shape = [jax.ShapeDtypeStruct((S, C), dt) for dt in outs] + [jax.ShapeDtypeStruct((k, C), F32) for k in pouts]
    return pl.pallas_call(
        body, name=name, grid=(C // tc,), in_specs=in_specs, out_specs=out_specs, out_shape=out_shape,
        compiler_params=_params(("parallel",)),
    )(*cols, *vecs)


_G0, _G1 = math.sqrt(2.0 / math.pi), 0.044715


def _gelu(g):
    th = jnp.tanh(_G0 * (g + _G1 * g * g * g))
    return 0.5 * g * (1.0 + th), th


def _ffn_act(gate_pre, up, w, b):
    act, _ = _gelu(_conv(gate_pre, w, b))
    return act * up


def _ffn_act_back(dact, gate_pre, up, w, b):
    g = _conv(gate_pre, w, b)
    ge, th = _gelu(g)
    dge = 0.5 * (1.0 + th) + 0.5 * g * (1.0 - th * th) * _G0 * (1.0 + 3.0 * _G1 * g * g)
    dup = dact * ge
    dgate_pre, dw, db = _conv_back(gate_pre, w, dact * up * dge)
    return dgate_pre, dup, dw, db


def _ssm_act(xbc, w, b):
    c = _conv(xbc, w, b)
    return c * _sigmoid(c)


def _ssm_act_back(dxc, xbc, w, b):
    c = _conv(xbc, w, b)
    sg = _sigmoid(c)
    return _conv_back(xbc, w, dxc * sg * (1.0 + c * (1.0 - sg)))


def _rope_tables(S):
    inv = 1.0 / (ROPE_THETA ** (jnp.arange(0, ROPE, 2, dtype=F32) / ROPE))
    ang = jnp.arange(S, dtype=F32)[:, None] * inv[None, :]
    cos, sin = jnp.cos(ang), jnp.sin(ang)
    return jnp.tile(cos, (1, 4)), jnp.tile(jnp.concatenate([-sin, sin], axis=1), (1, 2))


def _swap_halves(x):
    lane = lax.broadcasted_iota(I32, x.shape, 1)
    w = x.shape[1]
    return jnp.where((lane % ROPE) < ROPE // 2, pltpu.roll(x, w - ROPE // 2, 1), pltpu.roll(x, ROPE // 2, 1))


def _rot(x, cos2, sin2):
    return x * cos2 + _swap_halves(x) * sin2


def _rot_back(dy, cos2, sin2):
    return dy * cos2 + _swap_halves(dy * sin2)


def _mla_pack(cfg, q, kv, kr, cos2, sin2):
    S, H = cfg.S, cfg.H
    ts = _pick(S, 512, 8)

    def body(qn_ref, qr_ref, kn_ref, v_ref, kr_ref, c_ref, s_ref, Q_ref, K_ref, V_ref):
        h = pl.program_id(0)
        c2, s2 = c_ref[...], s_ref[...]
        Q_ref[0, :, 0:LANE] = qn_ref[...].astype(BF16)
        Q_ref[0, :, LANE:] = _rot(qr_ref[...], c2, s2).astype(BF16)
        K_ref[0, :, 0:LANE] = kn_ref[...].astype(BF16)
        krr = _rot(kr_ref[...], c2, s2)
        K_ref[0, :, LANE:] = jnp.where(h % 2 == 1, pltpu.roll(krr, ROPE, 1), krr).astype(BF16)
        V_ref[0] = v_ref[...].astype(BF16)

    blk = lambda f: pl.BlockSpec((ts, LANE), f)
    return pl.pallas_call(
        body, name="mla_pack", grid=(H, S // ts),
        in_specs=[blk(lambda h, i: (i, h)), blk(lambda h, i: (i, H + h // 2)), blk(lambda h, i: (i, h)),
                  blk(lambda h, i: (i, H + h)), blk(lambda h, i: (i, 0)), blk(lambda h, i: (i, 0)), blk(lambda h, i: (i, 0))],
        out_specs=[pl.BlockSpec((1, ts, 2 * LANE), lambda h, i: (h, i, 0)), pl.BlockSpec((1, ts, 2 * LANE), lambda h, i: (h, i, 0)),
                   pl.BlockSpec((1, ts, LANE), lambda h, i: (h, i, 0))],
        out_shape=[jax.ShapeDtypeStruct((H, S, 2 * LANE), BF16), jax.ShapeDtypeStruct((H, S, 2 * LANE), BF16),
                   jax.ShapeDtypeStruct((H, S, LANE), BF16)],
        compiler_params=_params(("parallel", "parallel")),
    )(q, q, kv, kv, kr, cos2, sin2)


def _mla_unpack(cfg, dQ, dK, dV, cos2, sin2):
    S, H = cfg.S, cfg.H
    ts = _pick(S, 256, 8)

    def body(dQ_ref, dK_ref, dV_ref, c_ref, s_ref, dq_ref, dkv_ref, dkr_ref):
        c2, s2 = c_ref[...], s_ref[...]
        lo = lax.broadcasted_iota(I32, (ts, LANE), 1) < ROPE
        tk = jnp.zeros((ts, LANE), F32)
        for h in range(H):
            dq_ref[:, h * LANE:(h + 1) * LANE] = dQ_ref[h, :, 0:LANE].astype(BF16)
            dkv_ref[:, h * LANE:(h + 1) * LANE] = dK_ref[h, :, 0:LANE].astype(BF16)
            dkv_ref[:, (H + h) * LANE:(H + h + 1) * LANE] = dV_ref[h].astype(BF16)
            own = lo if h % 2 == 0 else jnp.logical_not(lo)
            tk = tk + jnp.where(own, dK_ref[h, :, LANE:], 0.0)
        for j in range(H // 2):
            dr = dQ_ref[2 * j, :, LANE:] + dQ_ref[2 * j + 1, :, LANE:]
            dq_ref[:, (H + j) * LANE:(H + j + 1) * LANE] = _rot_back(dr, c2, s2).astype(BF16)
        dkr_rot = jnp.where(lo, tk + pltpu.roll(tk, ROPE, 1), 0.0)
        dkr_ref[...] = _rot_back(dkr_rot, c2, s2).astype(BF16)

    tab = pl.BlockSpec((ts, LANE), lambda i: (i, 0))
    return pl.pallas_call(
        body, name="mla_unpack", grid=(S // ts,),
        in_specs=[pl.BlockSpec((H, ts, 2 * LANE), lambda i: (0, i, 0)), pl.BlockSpec((H, ts, 2 * LANE), lambda i: (0, i, 0)),
                  pl.BlockSpec((H, ts, LANE), lambda i: (0, i, 0)), tab, tab],
        out_specs=[pl.BlockSpec((ts, cfg.QW), lambda i: (i, 0)), pl.BlockSpec((ts, cfg.KVW), lambda i: (i, 0)), tab],
        out_shape=[jax.ShapeDtypeStruct((S, cfg.QW), BF16), jax.ShapeDtypeStruct((S, cfg.KVW), BF16),
                   jax.ShapeDtypeStruct((S, LANE), BF16)],
        compiler_params=_params(("parallel",)),
    )(dQ, dK, dV, cos2, sin2)


_ATT_T = 256
_ATT_HB = 2
_ATT_SCALE = (NOPE + ROPE) ** -0.5


def _diag_mask(transposed=False):
    r = lax.broadcasted_iota(I32, (_ATT_T, _ATT_T), 0) // CHUNK
    c = lax.broadcasted_iota(I32, (_ATT_T, _ATT_T), 1) // CHUNK
    return r <= c if transposed else c <= r


def _row_form(col):
    return jnp.broadcast_to(col, (col.shape[0], LANE)).T[0:8, :]


def _attn_fwd(cfg, Q, K, V):
    S, H, T, HB = cfg.S, cfg.H, _ATT_T, _ATT_HB

    def body(q_ref, k_ref, v_ref, o_ref, lse_ref, lse_t_ref):
        qi = pl.program_id(1)

        def head_step(b, kb, carry, mask):
            m, l, acc = carry
            ks = pl.multiple_of(kb * T, T)
            s = _dot(q_ref[b], k_ref[b, pl.ds(ks, T), :], NT) * _ATT_SCALE
            if mask is not None:
                s = jnp.where(mask, s, -1e30)
            m_new = jnp.maximum(m, jnp.max(s, axis=1, keepdims=True))
            p = jnp.exp(s - m_new)
            alpha = jnp.exp(m - m_new)
            l = alpha * l + jnp.sum(p, axis=1, keepdims=True)
            acc = alpha * acc + _dot(p.astype(BF16), v_ref[b, pl.ds(ks, T), :])
            return m_new, l, acc

        def step(kb, carry, mask=None):
            return tuple(head_step(b, kb, carry[b], mask) for b in range(HB))

        init = (jnp.full((T, 1), -1e30, F32), jnp.zeros((T, 1), F32), jnp.zeros((T, VH), F32))
        done = step(qi, lax.fori_loop(0, qi, step, (init,) * HB), _diag_mask())
        for b, (m, l, acc) in enumerate(done):
            o_ref[:, b * LANE:(b + 1) * LANE] = acc / l
            lse = m + jnp.log(l)
            lse_ref[:, b * LANE:(b + 1) * LANE] = jnp.broadcast_to(lse, (T, LANE))
            lse_t_ref[b] = _row_form(lse)

    return pl.pallas_call(
        body, name="attn_fwd", grid=(H // HB, S // T),
        in_specs=[pl.BlockSpec((HB, T, 2 * LANE), lambda h, i: (h, i, 0)), pl.BlockSpec((HB, S, 2 * LANE), lambda h, i: (h, 0, 0)),
                  pl.BlockSpec((HB, S, LANE), lambda h, i: (h, 0, 0))],
        out_specs=[pl.BlockSpec((T, HB * LANE), lambda h, i: (i, h)), pl.BlockSpec((T, HB * LANE), lambda h, i: (i, h)),
                   pl.BlockSpec((HB, 8, T), lambda h, i: (h, 0, i))],
        out_shape=[jax.ShapeDtypeStruct((S, H * LANE), F32), jax.ShapeDtypeStruct((S, H * LANE), F32),
                   jax.ShapeDtypeStruct((H, 8, S), F32)],
        compiler_params=_params(("parallel", "parallel")),
    )(Q, K, V)


def _attn_dq(cfg, Q, K, V, do, o, lse):
    S, H, T, HB = cfg.S, cfg.H, _ATT_T, _ATT_HB

    def body(q_ref, k_ref, v_ref, do_ref, o_ref, lse_ref, dq_ref, dl_t_ref):
        qi = pl.program_id(1)
        do = [do_ref[:, b * LANE:(b + 1) * LANE] for b in range(HB)]
        delta = [jnp.sum(do[b] * o_ref[:, b * LANE:(b + 1) * LANE], axis=1, keepdims=True) for b in range(HB)]
        dob = [d.astype(BF16) for d in do]

        def head_step(b, kb, dq, mask):
            ks = pl.multiple_of(kb * T, T)
            k = k_ref[b, pl.ds(ks, T), :]
            s = _dot(q_ref[b], k, NT) * _ATT_SCALE
            if mask is not None:
                s = jnp.where(mask, s, -1e30)
            p = jnp.exp(s - lse_ref[:, b * LANE:b * LANE + 1])
            dp = _dot(dob[b], v_ref[b, pl.ds(ks, T), :], NT)
            ds = p * (dp - delta[b]) * _ATT_SCALE
            return dq + _dot(ds.astype(BF16), k)

        def step(kb, dqs, mask=None):
            return tuple(head_step(b, kb, dqs[b], mask) for b in range(HB))

        dqs = step(qi, lax.fori_loop(0, qi, step, (jnp.zeros((T, 2 * LANE), F32),) * HB), _diag_mask())
        for b in range(HB):
            dq_ref[b] = dqs[b]
            dl_t_ref[b] = _row_form(delta[b])

    col = pl.BlockSpec((T, HB * LANE), lambda h, i: (i, h))
    return pl.pallas_call(
        body, name="attn_dq", grid=(H // HB, S // T),
        in_specs=[pl.BlockSpec((HB, T, 2 * LANE), lambda h, i: (h, i, 0)), pl.BlockSpec((HB, S, 2 * LANE), lambda h, i: (h, 0, 0)),
                  pl.BlockSpec((HB, S, LANE), lambda h, i: (h, 0, 0)), col, col, col],
        out_specs=[pl.BlockSpec((HB, T, 2 * LANE), lambda h, i: (h, i, 0)), pl.BlockSpec((HB, 8, T), lambda h, i: (h, 0, i))],
        out_shape=[jax.ShapeDtypeStruct((H, S, 2 * LANE), F32), jax.ShapeDtypeStruct((H, 8, S), F32)],
        compiler_params=_params(("parallel", "parallel")),
    )(Q, K, V, do, o, lse)


def _attn_dkv(cfg, Q, K, V, do, lse_t, delta_t):
    S, H, T, HB = cfg.S, cfg.H, _ATT_T, _ATT_HB
    nq = S // T

    def body(q_ref, k_ref, v_ref, do_ref, lse_ref, dl_ref, dk_ref, dv_ref):
        kb = pl.program_id(1)

        def head_step(b, qi, carry, mask):
            dk, dv = carry
            qs = pl.multiple_of(qi * T, T)
            q = q_ref[b, pl.ds(qs, T), :]
            dob = do_ref[pl.ds(qs, T), b * LANE:(b + 1) * LANE].astype(BF16)
            s = _dot(k_ref[b], q, NT) * _ATT_SCALE
            if mask is not None:
                s = jnp.where(mask, s, -1e30)
            p = jnp.exp(s - lse_ref[b, 0:1, pl.ds(qs, T)])
            dv = dv + _dot(p.astype(BF16), dob)
            dp = _dot(v_ref[b], dob, NT)
            ds = p * (dp - dl_ref[b, 0:1, pl.ds(qs, T)]) * _ATT_SCALE
            dk = dk + _dot(ds.astype(BF16), q)
            return dk, dv

        def step(qi, carry, mask=None):
            return tuple(head_step(b, qi, carry[b], mask) for b in range(HB))

        zero = (jnp.zeros((T, 2 * LANE), F32), jnp.zeros((T, VH), F32))
        done = lax.fori_loop(kb + 1, nq, step, step(kb, (zero,) * HB, _diag_mask(transposed=True)))
        for b, (dk, dv) in enumerate(done):
            dk_ref[b] = dk
            dv_ref[b] = dv

    row = pl.BlockSpec((HB, 8, S), lambda h, j: (h, 0, 0))
    return pl.pallas_call(
        body, name="attn_dkv", grid=(H // HB, S // T),
        in_specs=[pl.BlockSpec((HB, S, 2 * LANE), lambda h, j: (h, 0, 0)), pl.BlockSpec((HB, T, 2 * LANE), lambda h, j: (h, j, 0)),
                  pl.BlockSpec((HB, T, LANE), lambda h, j: (h, j, 0)), pl.BlockSpec((S, HB * LANE), lambda h, j: (0, h)), row, row],
        out_specs=[pl.BlockSpec((HB, T, 2 * LANE), lambda h, j: (h, j, 0)), pl.BlockSpec((HB, T, LANE), lambda h, j: (h, j, 0))],
        out_shape=[jax.ShapeDtypeStruct((H, S, 2 * LANE), F32), jax.ShapeDtypeStruct((H, S, LANE), F32)],
        compiler_params=_params(("parallel", "parallel")),
    )(Q, K, V, do, lse_t, delta_t)


def _expand_matrix(cfg):
    r = lax.broadcasted_iota(I32, (LANE, cfg.INNER), 0)
    c = lax.broadcasted_iota(I32, (LANE, cfg.INNER), 1)
    return (r == c // HP).astype(F32)


def _softplus(x):
    return jnp.maximum(x, 0.0) + jnp.log(1.0 + jnp.exp(-jnp.abs(x)))


def _ssd_prep(cfg, dt_raw, dt_bias_pad, a_log_pad, expand):
    HS = cfg.HS

    def fn(raw, bias, alog, E):
        heads = lax.broadcasted_iota(I32, raw.shape, 1) < HS
        dt = jnp.where(heads, _softplus(raw + bias), 0.0)
        a = dt * jnp.where(heads[0:1], -jnp.exp(alog), 0.0)
        return dt, a, _dot(dt, E, precision=HI), _dot(a, E, precision=HI)

    return _rowwise("ssd_prep", fn, [dt_raw], [dt_bias_pad, a_log_pad, expand],
                    [(LANE, F32), (LANE, F32), (cfg.INNER, F32), (cfg.INNER, F32)], [], _pick(cfg.S, 512, 8))


def _tril(T):
    return lax.broadcasted_iota(I32, (T, T), 0) >= lax.broadcasted_iota(I32, (T, T), 1)


def _ssd_fwd(cfg, xc, dt_exp, a_exp, a_small, dskip_exp):
    S, T, INNER, G, NPAIR = cfg.S, cfg.T, cfg.INNER, cfg.G, cfg.NPAIR
    NC = S // T

    def body(xc_ref, dte_ref, ae_ref, as_ref, dsk_ref, y_ref, hin_ref, ht_ref):
        @pl.when(pl.program_id(0) == 0)
        def _():
            ht_ref[...] = jnp.zeros_like(ht_ref)

        tril = _tril(T)
        tri = tril.astype(F32)
        acs_s = _dot(tri, as_ref[...], precision=HI)
        acs_e = _dot(tri, ae_ref[...], precision=HI)
        acs_t = acs_s.T
        lo = lax.broadcasted_iota(I32, (T, LANE), 1) < HP
        for g in range(G):
            Bb = xc_ref[:, INNER + g * NST:INNER + (g + 1) * NST].astype(BF16)
            Cb = xc_ref[:, INNER + (G + g) * NST:INNER + (G + g + 1) * NST].astype(BF16)
            Gm = _dot(Cb, Bb, NT)
            for j in range(g * NPAIR // G, (g + 1) * NPAIR // G):
                sl = slice(j * LANE, (j + 1) * LANE)
                Xp = xc_ref[:, sl]
                Xdt = Xp * dte_ref[:, sl]
                Xb = Xdt.astype(BF16)
                acs_p = acs_e[:, sl]
                last = acs_p[T - 1:T, :]
                Hin = ht_ref[j]
                hin_ref[0, j] = Hin
                yd = []
                for e in (0, 1):
                    h = 2 * j + e
                    Lm = jnp.exp(jnp.where(tril, acs_s[:, h:h + 1] - acs_t[h:h + 1, :], -1e30))
                    yd.append(_dot((Gm * Lm).astype(BF16), Xb))
                y_off = _dot(Cb, Hin.astype(BF16)) * jnp.exp(acs_p)
                y_ref[:, sl] = jnp.where(lo, yd[0], yd[1]) + y_off + Xp * dsk_ref[:, sl]
                st = _dot(Bb, (Xdt * jnp.exp(last - acs_p)).astype(BF16), TN)
                ht_ref[j] = jnp.exp(last) * Hin + st

    rows = lambda w: pl.BlockSpec((T, w), lambda c: (c, 0))
    return pl.pallas_call(
        body, name="ssd_fwd", grid=(NC,),
        in_specs=[rows(cfg.CONVCH), rows(INNER), rows(INNER), rows(LANE), pl.BlockSpec((1, INNER), lambda c: (0, 0))],
        out_specs=[rows(INNER), pl.BlockSpec((1, NPAIR, NST, LANE), lambda c: (c, 0, 0, 0))],
        out_shape=[jax.ShapeDtypeStruct((S, INNER), F32), jax.ShapeDtypeStruct((NC, NPAIR, NST, LANE), F32)],
        scratch_shapes=[pltpu.VMEM((NPAIR, NST, LANE), F32)],
        compiler_params=_params(("arbitrary",)),
    )(xc, dt_exp, a_exp, a_small, dskip_exp)


def _ssd_bwd(cfg, dy, xc, dt_exp, a_exp, a_small, dskip_exp, hin, dt_raw, dt_bias_pad, a_log_pad, expand):
    S, T, INNER, G, NPAIR, HS = cfg.S, cfg.T, cfg.INNER, cfg.G, cfg.NPAIR, cfg.HS
    NC = S // T

    def body(dy_ref, xc_ref, dte_ref, ae_ref, as_ref, dsk_ref, hin_ref, raw_ref, bias_ref, alog_ref, e_ref,
             dxc_ref, draw_ref, dbias_ref, dalog_ref, dskip_ref, dht_ref, cols_ref, rows_ref, dacs_ref, ddt_ref):
        first = pl.program_id(0) == 0

        @pl.when(first)
        def _():
            dht_ref[...] = jnp.zeros_like(dht_ref)

        tril = _tril(T)
        tri = tril.astype(F32)
        a_s = as_ref[...]
        acs_s = _dot(tri, a_s, precision=HI)
        acs_e = _dot(tri, ae_ref[...], precision=HI)
        acs_t = acs_s.T
        lo = lax.broadcasted_iota(I32, (T, LANE), 1) < HP
        last_row = lax.broadcasted_iota(I32, (T, LANE), 0) == T - 1
        cols_ref[...] = jnp.zeros_like(cols_ref)
        rows_ref[...] = jnp.zeros_like(rows_ref)
        dsk_parts = []
        for g in range(G):
            bsl = slice(INNER + g * NST, INNER + (g + 1) * NST)
            csl = slice(INNER + (G + g) * NST, INNER + (G + g + 1) * NST)
            Bb = xc_ref[:, bsl].astype(BF16)
            Cb = xc_ref[:, csl].astype(BF16)
            Gm = _dot(Cb, Bb, NT)
            dG = jnp.zeros((T, T), F32)
            dB = jnp.zeros((T, NST), F32)
            dC = jnp.zeros((T, NST), F32)
            for j in range(g * NPAIR // G, (g + 1) * NPAIR // G):
                sl = slice(j * LANE, (j + 1) * LANE)
                Xp = xc_ref[:, sl]
                dtp = dte_ref[:, sl]
                Xdt = Xp * dtp
                Xb = Xdt.astype(BF16)
                acs_p = acs_e[:, sl]
                last = acs_p[T - 1:T, :]
                e_p, dec, cd = jnp.exp(acs_p), jnp.exp(last - acs_p), jnp.exp(last)
                Hin = hin_ref[0, j]
                Hb = Hin.astype(BF16)
                dHn = dht_ref[j]
                dHb = dHn.astype(BF16)
                dYp = dy_ref[:, sl]
                z = _dot(Cb, Hb)
                dz = (dYp * e_p).astype(BF16)
                dacs_p = dYp * z * e_p
                dC = dC + _dot(dz, Hb, NT)
                dHin = _dot(Cb, dz, TN) + cd * dHn
                dlast = _colsum(dHn * Hin) * cd
                qv = _dot(Bb, dHb)
                dXdt = qv * dec
                ddec = qv * Xdt * dec
                dacs_p = dacs_p - ddec
                dlast = dlast + _colsum(ddec)
                dB = dB + _dot((Xdt * dec).astype(BF16), dHb, NT)
                for e in (0, 1):
                    h = 2 * j + e
                    Lm = jnp.exp(jnp.where(tril, acs_s[:, h:h + 1] - acs_t[h:h + 1, :], -1e30))
                    Mh = Gm * Lm
                    dYe = jnp.where(lo if e == 0 else jnp.logical_not(lo), dYp, 0.0).astype(BF16)
                    dM = _dot(dYe, Xb, NT)
                    dXdt = dXdt + _dot(Mh.astype(BF16), dYe, TN)
                    W = dM * Mh
                    cols_ref[:, h:h + 1] = jnp.sum(W, axis=1, keepdims=True)
                    rows_ref[h:h + 1, :] = _colsum(W)
                    dG = dG + dM * Lm
                dacs_ref[:, sl] = dacs_p + jnp.where(last_row, dlast, 0.0)
                ddt_ref[:, sl] = dXdt * Xp
                dxc_ref[:, sl] = dXdt * dtp + dYp * dsk_ref[:, sl]
                dsk_parts.append(_colsum(dYp * Xp))
                dht_ref[j] = dHin
            dGb = dG.astype(BF16)
            dxc_ref[:, bsl] = dB + _dot(dGb, Cb, TN)
            dxc_ref[:, csl] = dC + _dot(dGb, Bb)
        E = e_ref[...]
        dacs_s = cols_ref[...] - rows_ref[...].T + _dot(dacs_ref[...], E, NT, precision=HI)
        da = _dot(tri, dacs_s, TN, precision=HI)
        heads = lax.broadcasted_iota(I32, (1, LANE), 1) < HS
        A = jnp.where(heads, -jnp.exp(alog_ref[...]), 0.0)
        ddt = _dot(ddt_ref[...], E, NT, precision=HI) + da * A
        draw = jnp.where(heads, ddt * _sigmoid(raw_ref[...] + bias_ref[...]), 0.0)
        draw_ref[...] = draw
        dsk = _dot(jnp.broadcast_to(jnp.concatenate(dsk_parts, axis=1), (8, INNER)), E, NT, precision=HI)[0:1]
        for ref, val in ((dbias_ref, _colsum(draw)), (dalog_ref, _colsum(da * a_s)), (dskip_ref, dsk)):
            @pl.when(first)
            def _():
                ref[...] = val

            @pl.when(jnp.logical_not(first))
            def _():
                ref[...] += val

    rows = lambda w: pl.BlockSpec((T, w), lambda c: (NC - 1 - c, 0))
    vec = lambda w: pl.BlockSpec((1, w), lambda c: (0, 0))
    return pl.pallas_call(
        body, name="ssd_bwd", grid=(NC,),
        in_specs=[rows(INNER), rows(cfg.CONVCH), rows(INNER), rows(INNER), rows(LANE), vec(INNER),
                  pl.BlockSpec((1, NPAIR, NST, LANE), lambda c: (NC - 1 - c, 0, 0, 0)), rows(LANE), vec(LANE), vec(LANE),
                  pl.BlockSpec((LANE, INNER), lambda c: (0, 0))],
        out_specs=[rows(cfg.CONVCH), rows(LANE), vec(LANE), vec(LANE), vec(LANE)],
        out_shape=[jax.ShapeDtypeStruct((S, cfg.CONVCH), F32), jax.ShapeDtypeStruct((S, LANE), F32)]
        + [jax.ShapeDtypeStruct((1, LANE), F32)] * 3,
        scratch_shapes=[pltpu.VMEM((NPAIR, NST, LANE), F32), pltpu.VMEM((T, LANE), F32), pltpu.VMEM((LANE, T), F32),
                        pltpu.VMEM((T, INNER), F32), pltpu.VMEM((T, INNER), F32)],
        compiler_params=_params(("arbitrary",)),
    )(dy, xc, dt_exp, a_exp, a_small, dskip_exp, hin, dt_raw, dt_bias_pad, a_log_pad, expand)


def _ssd_post(cfg, y, z, norm_g):
    W = cfg.INNER // cfg.G

    def fn(y, z, g):
        yz = y * z * _sigmoid(z)
        return jnp.concatenate([yz[:, i * W:(i + 1) * W] * _rs(yz[:, i * W:(i + 1) * W]) for i in range(cfg.G)], axis=1) * g

    return _rowwise("ssd_post", fn, [y, z], [norm_g], [(cfg.INNER, BF16)], [], _pick(cfg.S, 256, 8))[0]


def _ssd_post_bwd(cfg, db, y, z, norm_g):
    W = cfg.INNER // cfg.G

    def fn(db, y, z, g):
        sg = _sigmoid(z)
        yz = y * z * sg
        dn = db * g
        dyz, nh = [], []
        for i in range(cfg.G):
            seg = yz[:, i * W:(i + 1) * W]
            r = _rs(seg)
            nh.append(seg * r)
            dyz.append(_rms_back(nh[-1], r, dn[:, i * W:(i + 1) * W]))
        dyz = jnp.concatenate(dyz, axis=1)
        return dyz * z * sg, dyz * y * sg * (1.0 + z * (1.0 - sg)), _colsum(db * jnp.concatenate(nh, axis=1))

    return _rowwise("ssd_post_bwd", fn, [db, y, z], [norm_g], [(cfg.INNER, F32), (cfg.INNER, F32)], [(1, cfg.INNER)],
                    _pick(cfg.S, 256, 8))


def _local_grads(cfg, x, tgt, W, sp, out_weight=None, ffn_weights=None, early_grads_ready=None, in_grad_ready=None):
    S, D, H, INNER = cfg.S, cfg.D, cfg.H, cfg.INNER
    ts = _pick(S, 256, 8)
    tc = 256

    xn = _rowwise("rms_pre", lambda x, g: x * _rs(x) * g, [x], [sp["mix_pre_g"]], [(D, BF16)], [], ts)[0]
    u = _matmul("mm_in", xn, W["w_in"], "nn", F32)
    c_q, c_kv = u[:, :cfg.QL], u[:, cfg.QL:cfg.o_kr]
    kr = u[:, cfg.o_kr:cfg.o_z]
    z = u[:, cfg.o_z:cfg.o_xbc]
    xbc = u[:, cfg.o_xbc:cfg.o_dt]
    dt_raw = u[:, cfg.o_dt:]

    cqn = _rowwise("rms_q", lambda x, g: x * _rs(x) * g, [c_q], [sp["q_norm_g"]], [(cfg.QL, BF16)], [], ts)[0]
    ckvn = _rowwise("rms_kv", lambda x, g: x * _rs(x) * g, [c_kv], [sp["kv_norm_g"]], [(cfg.KVL, BF16)], [], ts)[0]
    q = _matmul("mm_uq", cqn, W["w_uq"], "nn", F32)
    kv = _matmul("mm_ukv", ckvn, W["w_ukv"], "nn", F32)
    cos2, sin2 = _rope_tables(S)
    Qh, Kh, Vh = _mla_pack(cfg, q, kv, kr, cos2, sin2)
    a_out, lse, lse_t = _attn_fwd(cfg, Qh, Kh, Vh)
    if out_weight is not None:
        sp = dict(sp, ssm_conv_b=sp["ssm_conv_b"] + out_weight.pass_on(a_out)[0, 0])

    pad = lambda v: jnp.pad(v, ((0, 0), (0, LANE - v.shape[1])))
    expand = _expand_matrix(cfg)
    dt_bias_pad, a_log_pad = pad(sp["dt_bias"]), pad(sp["a_log"])
    dskip_exp = jnp.repeat(sp["d_skip"], HP, axis=1)
    xc = _colwise("ssm_act", _ssm_act, [xbc], [sp["ssm_conv_w"], sp["ssm_conv_b"]], [F32], [], tc)[0]
    dt_s, a_s, dt_exp, a_exp = _ssd_prep(cfg, dt_raw, dt_bias_pad, a_log_pad, expand)
    y_ssd, hin = _ssd_fwd(cfg, xc, dt_exp, a_exp, a_s, dskip_exp)
    b_out = _ssd_post(cfg, y_ssd, z, sp["ssm_norm_g"])

    ab_out = jnp.concatenate([a_out.astype(BF16), b_out], axis=1)
    if out_weight is not None:
        W = dict(W, **out_weight.arrived(ab_out))
    if ffn_weights is not None:
        sp = dict(sp, mix_post_g=sp["mix_post_g"] + ffn_weights.pass_on(ab_out)[0, 0])
    mix = _matmul("mm_out", ab_out, W["w_out"], "nn", F32)

    def mid(x, mix, g_mp, g_fp):
        x1 = x + mix * _rs(mix) * g_mp
        return x1, x1 * _rs(x1) * g_fp

    x1, h2 = _rowwise("fwd_mid", mid, [x, mix], [sp["mix_post_g"], sp["ffn_pre_g"]], [(D, F32), (D, BF16)], [], ts)
    if ffn_weights is not None:
        W = dict(W, **ffn_weights.arrived(h2))
    gate_pre = _matmul("mm_gate", h2, W["w_gate"], "nn", F32, chips=True)
    up = _matmul("mm_up", h2, W["w_up"], "nn", F32, chips=True)
    act = _colwise("ffn_act", _ffn_act, [gate_pre, up], [sp["ffn_conv_w"], sp["ffn_conv_b"]], [BF16], [], tc)[0]
    f = _matmul("mm_down", act, W["w_down"], "nn", F32)

    def final(x1, f, t, g):
        r = _rs(f)
        fh = f * r
        err = x1 + fh * g - t
        loss = 0.5 * jnp.sum(jnp.mean(err * err, axis=-1, keepdims=True), axis=0, keepdims=True)
        dy = err * (1.0 / D)
        return dy, _rms_back(fh, r, dy * g), _colsum(dy * fh), loss

    dy, df, g_ffn_post, loss = _rowwise("final", final, [x1, f, tgt], [sp["ffn_post_g"]], [(D, F32), (D, BF16)],
                                        [(1, D), (1, LANE)], ts)
    gW = {}
    dact = _matmul("mm_down_dx", df, W["w_down"], "nt", F32)
    gW["w_down"] = _matmul("mm_down_dw", act, df, "tn", BF16)
    dgate, dup, g_ffn_conv_w, g_ffn_conv_b = _colwise(
        "ffn_act_bwd", _ffn_act_back, [dact, gate_pre, up], [sp["ffn_conv_w"], sp["ffn_conv_b"]], [BF16, BF16], [FFN_K, 1], tc)
    dh2 = _matmul("mm_gu_dx", dgate, W["w_gate"], "nt", F32, dup, W["w_up"], chips=True)
    gW["w_gate"] = _matmul("mm_gate_dw", h2, dgate, "tn", BF16, chips=True)
    gW["w_up"] = _matmul("mm_up_dw", h2, dup, "tn", BF16, chips=True)

    def mid_back(dy, dh2, x1, mix, g_mp, g_fp):
        r2 = _rs(x1)
        xh = x1 * r2
        dx1 = dy + _rms_back(xh, r2, dh2 * g_fp)
        r1 = _rs(mix)
        mh = mix * r1
        return dx1, _rms_back(mh, r1, dx1 * g_mp), _colsum(dh2 * xh), _colsum(dx1 * mh)

    dx1, dmix, g_ffn_pre, g_mix_post = _rowwise("bwd_mid", mid_back, [dy, dh2, x1, mix], [sp["mix_post_g"], sp["ffn_pre_g"]],
                                                [(D, F32), (D, BF16)], [(1, D), (1, D)], ts)
    dab_out = _matmul("mm_out_dx", dmix, W["w_out"], "nt", F32)
    db_out = dab_out[:, cfg.MLAW:]
    gW["w_out"] = _matmul("mm_out_dw", ab_out, dmix, "tn", BF16)
    if early_grads_ready is not None:
        token = early_grads_ready({n: gW[n] for n in ("w_down", "w_gate", "w_up", "w_out")})
        sp = dict(sp, ssm_norm_g=sp["ssm_norm_g"] + token[0, 0])

    dy_ssd, dz, g_ssm_norm = _ssd_post_bwd(cfg, db_out, y_ssd, z, sp["ssm_norm_g"])
    dxc, ddt_raw, g_dt_bias, g_a_log, g_d_skip = _ssd_bwd(cfg, dy_ssd, xc, dt_exp, a_exp, a_s, dskip_exp, hin, dt_raw,
                                                          dt_bias_pad, a_log_pad, expand)
    dxbc, g_ssm_conv_w, g_ssm_conv_b = _colwise("ssm_act_bwd", _ssm_act_back, [dxc, xbc], [sp["ssm_conv_w"], sp["ssm_conv_b"]],
                                                [BF16], [SSM_K, 1], tc)

    dQ, delta_t = _attn_dq(cfg, Qh, Kh, Vh, dab_out, a_out, lse)
    dK, dV = _attn_dkv(cfg, Qh, Kh, Vh, dab_out, lse_t, delta_t)
    dq, dkv, dkr = _mla_unpack(cfg, dQ, dK, dV, cos2, sin2)
    dcqn = _matmul("mm_uq_dx", dq, W["w_uq"], "nt", F32)
    dckvn = _matmul("mm_ukv_dx", dkv, W["w_ukv"], "nt", F32)
    gW["w_uq"] = _matmul("mm_uq_dw", cqn, dq, "tn", BF16)
    gW["w_ukv"] = _matmul("mm_ukv_dw", ckvn, dkv, "tn", BF16)

    def rms_back(x, dy, g):
        r = _rs(x)
        xh = x * r
        return _rms_back(xh, r, dy * g), _colsum(dy * xh)

    dc_q, g_q_norm = _rowwise("rms_q_bwd", rms_back, [c_q, dcqn], [sp["q_norm_g"]], [(cfg.QL, BF16)], [(1, cfg.QL)], ts)
    dc_kv, g_kv_norm = _rowwise("rms_kv_bwd", rms_back, [c_kv, dckvn], [sp["kv_norm_g"]], [(cfg.KVL, BF16)], [(1, cfg.KVL)], ts)

    du = jnp.concatenate([dc_q, dc_kv, dkr, dz.astype(BF16), dxbc, ddt_raw.astype(BF16)], axis=1)
    gW["w_in"] = _matmul("mm_in_dw", xn, du, "tn", BF16)
    if in_grad_ready is not None:
        token = in_grad_ready({n: gW[n] for n in ("w_in", "w_uq", "w_ukv")})
        sp = dict(sp, mix_pre_g=sp["mix_pre_g"] + token[0, 0])
    dxn = _matmul("mm_in_dx", du, W["w_in"], "nt", F32)

    def first_back(dx1, dxn, x, g):
        r = _rs(x)
        xh = x * r
        return dx1 + _rms_back(xh, r, dxn * g), _colsum(dxn * xh)

    grad_x, g_mix_pre = _rowwise("bwd_first", first_back, [dx1, dxn, x], [sp["mix_pre_g"]], [(D, F32)], [(1, D)], ts)

    gs = dict(mix_pre_g=g_mix_pre, q_norm_g=g_q_norm, kv_norm_g=g_kv_norm, ssm_conv_w=g_ssm_conv_w, ssm_conv_b=g_ssm_conv_b,
              dt_bias=g_dt_bias[:, :cfg.HS], a_log=g_a_log[:, :cfg.HS], d_skip=g_d_skip[:, :cfg.HS], ssm_norm_g=g_ssm_norm,
              mix_post_g=g_mix_post, ffn_pre_g=g_ffn_pre, ffn_conv_w=g_ffn_conv_w, ffn_conv_b=g_ffn_conv_b,
              ffn_post_g=g_ffn_post)
    return loss, grad_x, gW, gs


def _to_kernel_layout(cfg, name, w):
    if name == "w_in":
        a = cfg.o_kr + ROPE
        return jnp.concatenate([w[:, :a], jnp.zeros((w.shape[0], LANE - ROPE), w.dtype), w[:, a:],
                                jnp.zeros((w.shape[0], LANE - cfg.HS), w.dtype)], axis=1)
    if name in ("w_uq", "w_ukv"):
        per = NOPE + (ROPE if name == "w_uq" else VH)
        return jnp.concatenate([w[:, h * per:h * per + NOPE] for h in range(cfg.H)]
                               + [w[:, h * per + NOPE:(h + 1) * per] for h in range(cfg.H)], axis=1)
    return w


def _from_kernel_layout(cfg, name, g):
    if name == "w_in":
        return jnp.concatenate([g[:, :cfg.o_kr + ROPE], g[:, cfg.o_z:cfg.o_dt + cfg.HS]], axis=1)
    if name in ("w_uq", "w_ukv"):
        second = ROPE if name == "w_uq" else VH
        base = cfg.H * NOPE
        parts = []
        for h in range(cfg.H):
            parts += [g[:, h * NOPE:(h + 1) * NOPE], g[:, base + h * second:base + (h + 1) * second]]
        return jnp.concatenate(parts, axis=1)
    return g


def _cols_to_chips(w):
    r, c = w.shape
    return w.reshape(r, N_CHIPS, c // N_CHIPS).transpose(1, 0, 2)


def _chips_to_cols(g):
    k, r, cs = g.shape
    return g.transpose(1, 0, 2).reshape(r, k * cs)


_CHIP_MAJOR = ("w_gate", "w_up")
_RELAYOUT = ("w_in", "w_uq", "w_ukv")
_LAYOUT_ROWS = 256


def _gathered_to_kernel(cfg, name, wg):
    if name in _CHIP_MAJOR:
        return wg
    if name not in _RELAYOUT:
        return wg.reshape(wg.shape[0] * wg.shape[1], wg.shape[2])
    _, rows, cs = wg.shape
    tr = _pick(rows, _LAYOUT_ROWS, 16)

    def body(w_ref, o_ref):
        o_ref[...] = _to_kernel_layout(cfg, name, jnp.concatenate([w_ref[k] for k in range(N_CHIPS)], axis=1))

    wide = jax.eval_shape(lambda w: _to_kernel_layout(cfg, name, w), jax.ShapeDtypeStruct((rows, N_CHIPS * cs), wg.dtype)).shape[1]
    return pl.pallas_call(
        body, name="layout_" + name, grid=(rows // tr,),
        in_specs=[pl.BlockSpec((N_CHIPS, tr, cs), lambda i: (0, i, 0))], out_specs=pl.BlockSpec((tr, wide), lambda i: (i, 0)),
        out_shape=jax.ShapeDtypeStruct((rows, wide), wg.dtype), compiler_params=_params(("parallel",)),
    )(wg)


def _grad_to_chips(cfg, name, g):
    if name in _CHIP_MAJOR:
        return g
    if name not in _RELAYOUT:
        return g.reshape(N_CHIPS, g.shape[0] // N_CHIPS, g.shape[1])
    rows, wide = g.shape
    tr = _pick(rows, _LAYOUT_ROWS, 16)
    cs = jax.eval_shape(lambda v: _from_kernel_layout(cfg, name, v), g).shape[1] // N_CHIPS

    def body(g_ref, o_ref):
        nat = _from_kernel_layout(cfg, name, g_ref[...])
        for k in range(N_CHIPS):
            o_ref[k] = nat[:, k * cs:(k + 1) * cs]

    return pl.pallas_call(
        body, name="layout_grad_" + name, grid=(rows // tr,),
        in_specs=[pl.BlockSpec((tr, wide), lambda i: (i, 0))], out_specs=pl.BlockSpec((N_CHIPS, tr, cs), lambda i: (0, i, 0)),
        out_shape=jax.ShapeDtypeStruct((N_CHIPS, rows, cs), g.dtype), compiler_params=_params(("parallel",)),
    )(g)


def _me():
    return lax.axis_index("x"), lax.axis_index("y"), lax.axis_index("c")


def _other_chips(x, y):
    return [(1 - x, y), (x, 1 - y), (1 - x, 1 - y)]


_ANY = pl.BlockSpec(memory_space=pl.ANY)


def _row_block(rows, cols, mult):
    return _pick(rows, max(mult, (1 << 19) // cols // mult * mult), mult)


def _scalar(v):
    return v.astype(I32).reshape(1)


def _stage_shard(name, w, chip):
    _, rs, cs = w.shape
    tr = _row_block(rs, cs, 16)

    def body(chip_ref, w_ref, o_ref):
        o_ref[...] = w_ref[...].astype(BF16)

    return pl.pallas_call(
        body, name="stage_" + name,
        grid_spec=pltpu.PrefetchScalarGridSpec(
            num_scalar_prefetch=1, grid=(rs // tr,),
            in_specs=[pl.BlockSpec((None, tr, cs), lambda i, chip_ref: (0, i, 0))],
            out_specs=pl.BlockSpec((None, tr, cs), lambda i, chip_ref: (chip_ref[0], i, 0))),
        out_shape=jax.ShapeDtypeStruct((N_CHIPS, rs, cs), BF16),
        compiler_params=_params(("parallel",)),
    )(_scalar(chip), w)


def _half(ref, k, half):
    h = ref.shape[1] // 2
    return ref.at[k, pl.ds(pl.multiple_of(half * h, 16), h), :]


def _allgather_weights(bufs):
    n = len(bufs)

    def body(*refs):
        outs, send_sems, recv_sems = refs[n:2 * n], refs[2 * n], refs[2 * n + 1]
        x, y, c = _me()
        chip = 2 * x + y
        sib = (x, y, 1 - c)
        chips = _other_chips(x, y)

        def copy(k, part, to):
            return pltpu.make_async_remote_copy(src_ref=part, dst_ref=part, send_sem=send_sems.at[k], recv_sem=recv_sems.at[k],
                                                device_id=to, device_id_type=MESH_ID)

        started = []
        for w, o_ref in enumerate(outs):
            for j, (cx, cy) in enumerate(chips):
                started.append(copy(6 * w + j, _half(o_ref, chip, c), (cx, cy, c)))
                started[-1].start()
        for w, o_ref in enumerate(outs):
            for j, (cx, cy) in enumerate(chips):
                theirs = _half(o_ref, 2 * cx + cy, c)
                copy(6 * w + j, theirs, sib).wait_recv()
                started.append(copy(6 * w + 3 + j, theirs, sib))
                started[-1].start()
        for w, o_ref in enumerate(outs):
            for j, (cx, cy) in enumerate(chips):
                copy(6 * w + 3 + j, _half(o_ref, 2 * cx + cy, 1 - c), sib).wait_recv()
        for cp in started:
            cp.wait_send()

    return pl.pallas_call(
        body, name="allgather_weights", in_specs=[_ANY] * n, out_specs=[_ANY] * n,
        out_shape=[jax.ShapeDtypeStruct(b.shape, b.dtype) for b in bufs],
        input_output_aliases={i: i for i in range(n)},
        scratch_shapes=[pltpu.SemaphoreType.DMA((6 * n,)), pltpu.SemaphoreType.DMA((6 * n,))],
    )(*bufs)


_HBM = pl.BlockSpec(memory_space=pltpu.HBM)
_SEM = pl.BlockSpec(memory_space=pltpu.SEMAPHORE)
_EFFECT = pltpu.SideEffectType.DATAFLOW_SIDE_EFFECTING


def _split_start(name, bufs, n_copies, copies, after):
    n = len(bufs)

    def body(*refs):
        for cp in copies(refs[:n], refs[n + 1], refs[n + 2]):
            cp.start()
        refs[-1][...] = jnp.zeros_like(refs[-1])

    res = pl.pallas_call(
        body, name=name,
        out_shape=(pltpu.SemaphoreType.DMA((n_copies,)), pltpu.SemaphoreType.DMA((n_copies,)),
                   *[pltpu.HBM(b.shape, b.dtype) for b in bufs], jax.ShapeDtypeStruct((8, LANE), F32)),
        in_specs=[_HBM] * n + [_ANY], out_specs=(_SEM, _SEM, *[_HBM] * n, pl.BlockSpec(memory_space=pltpu.VMEM)),
        input_output_aliases={i: 2 + i for i in range(n)},
        compiler_params=pltpu.CompilerParams(has_side_effects=_EFFECT),
    )(*[pltpu.with_memory_space_constraint(b, pltpu.HBM) for b in bufs], after)
    return res[0], res[1], list(res[2:2 + n]), res[-1]


def _split_wait(name, send_sems, recv_sems, bufs, after, copies):
    n = len(bufs)

    def body(*refs):
        for cp in copies(refs[:n], refs[n], refs[n + 1]):
            cp.wait_send()
            cp.wait_recv()

    return list(pl.pallas_call(
        body, name=name, out_shape=[pltpu.HBM(b.shape, b.dtype) for b in bufs],
        in_specs=[_HBM] * n + [_SEM, _SEM, _ANY], out_specs=[_HBM] * n,
        input_output_aliases={i: i for i in range(n)},
        compiler_params=pltpu.CompilerParams(has_side_effects=_EFFECT),
    )(*bufs, send_sems, recv_sems, after))


def _gather_to_chips(bufs, send_sems, recv_sems):
    x, y, c = _me()
    return [pltpu.make_async_remote_copy(src_ref=_half(b, 2 * x + y, c), dst_ref=_half(b, 2 * x + y, c),
                                         send_sem=send_sems.at[3 * w + j], recv_sem=recv_sems.at[3 * w + j],
                                         device_id=(cx, cy, c), device_id_type=MESH_ID)
            for w, b in enumerate(bufs) for j, (cx, cy) in enumerate(_other_chips(x, y))]


def _gather_to_sibling(bufs, send_sems, recv_sems):
    x, y, c = _me()
    return [pltpu.make_async_remote_copy(src_ref=_half(b, 2 * cx + cy, c), dst_ref=_half(b, 2 * cx + cy, c),
                                         send_sem=send_sems.at[3 * w + j], recv_sem=recv_sems.at[3 * w + j],
                                         device_id=(x, y, 1 - c), device_id_type=MESH_ID)
            for w, b in enumerate(bufs) for j, (cx, cy) in enumerate(_other_chips(x, y))]


def _pair_exchange(name, grads):
    n = len(grads)

    def body(*refs):
        ins, outs, send_sems, recv_sems = refs[:n], refs[n:2 * n], refs[2 * n], refs[2 * n + 1]
        x, y, c = _me()
        cps = []
        for w, (g_ref, o_ref) in enumerate(zip(ins, outs)):
            h = o_ref.shape[1]
            src = g_ref.at[:, pl.ds(pl.multiple_of((1 - c) * h, 16), h), :]
            cps.append(pltpu.make_async_remote_copy(src_ref=src, dst_ref=o_ref, send_sem=send_sems.at[w], recv_sem=recv_sems.at[w],
                                                    device_id=(x, y, 1 - c), device_id_type=MESH_ID))
            cps[-1].start()
        for cp in cps:
            cp.wait()

    return pl.pallas_call(
        body, name="pair_exchange_" + name, in_specs=[_ANY] * n, out_specs=[_ANY] * n,
        out_shape=[jax.ShapeDtypeStruct((g.shape[0], g.shape[1] // 2, g.shape[2]), g.dtype) for g in grads],
        scratch_shapes=[pltpu.SemaphoreType.DMA((n,)), pltpu.SemaphoreType.DMA((n,))],
    )(*grads)


def _pair_sum(name, g, theirs, c):
    _, h, cs = theirs.shape
    tr = _row_block(h, cs, 16)
    nb = h // tr

    def body(c_ref, a_ref, b_ref, o_ref):
        o_ref[...] = (a_ref[...].astype(F32) + b_ref[...].astype(F32)).astype(o_ref.dtype)

    return pl.pallas_call(
        body, name="pair_sum_" + name,
        grid_spec=pltpu.PrefetchScalarGridSpec(
            num_scalar_prefetch=1, grid=(N_CHIPS, nb),
            in_specs=[pl.BlockSpec((None, tr, cs), lambda k, i, c_ref: (k, c_ref[0] * nb + i, 0)),
                      pl.BlockSpec((None, tr, cs), lambda k, i, c_ref: (k, i, 0))],
            out_specs=pl.BlockSpec((None, tr, cs), lambda k, i, c_ref: (k, i, 0))),
        out_shape=jax.ShapeDtypeStruct(theirs.shape, BF16),
        compiler_params=_params(("parallel", "parallel")),
    )(_scalar(c), g, theirs)


def _chip_copies(srcs, lands, send_sems, recv_sems):
    x, y, c = _me()
    return [pltpu.make_async_remote_copy(src_ref=s_ref.at[2 * cx + cy], dst_ref=l_ref.at[j], send_sem=send_sems.at[3 * w + j],
                                         recv_sem=recv_sems.at[3 * w + j], device_id=(cx, cy, c), device_id_type=MESH_ID)
            for w, (s_ref, l_ref) in enumerate(zip(srcs, lands)) for j, (cx, cy) in enumerate(_other_chips(x, y))]


def _chip_exchange_start(name, sums):
    n = len(sums)
    lands = [lax.empty((3,) + s.shape[1:], s.dtype) for s in sums]
    send_sems, recv_sems, bufs, token = _split_start(
        "chip_exchange_start_" + name, [*sums, *lands], 3 * n, lambda refs, ss, rs: _chip_copies(refs[:n], refs[n:], ss, rs),
        jnp.zeros((8, LANE), F32))
    return send_sems, recv_sems, bufs[:n], bufs[n:], token


def _chip_exchange_wait(name, send_sems, recv_sems, sums, lands, after):
    n = len(sums)
    bufs = _split_wait("chip_exchange_wait_" + name, send_sems, recv_sems, [*sums, *lands], after,
                       lambda refs, ss, rs: _chip_copies(refs[:n], refs[n:], ss, rs))
    return bufs[:n], bufs[n:]


def _chip_sum(name, sums, theirs, chip):
    _, h, cs = sums.shape
    tr = _row_block(h, cs, 16)

    def body(chip_ref, s_ref, t_ref, o_ref):
        acc = s_ref[...].astype(F32)
        for k in range(3):
            acc = acc + t_ref[k].astype(F32)
        o_ref[...] = acc

    return pl.pallas_call(
        body, name="chip_sum_" + name,
        grid_spec=pltpu.PrefetchScalarGridSpec(
            num_scalar_prefetch=1, grid=(h // tr,),
            in_specs=[pl.BlockSpec((None, tr, cs), lambda i, chip_ref: (chip_ref[0], i, 0)),
                      pl.BlockSpec((3, tr, cs), lambda i, chip_ref: (0, i, 0))],
            out_specs=pl.BlockSpec((tr, cs), lambda i, chip_ref: (i, 0))),
        out_shape=jax.ShapeDtypeStruct((h, cs), F32),
        compiler_params=_params(("parallel",)),
    )(_scalar(chip), sums, theirs)


def _sibling_exchange(halves):
    n = len(halves)

    def body(*refs):
        ins, outs, send_sems, recv_sems = refs[:n], refs[n:2 * n], refs[2 * n], refs[2 * n + 1]
        x, y, c = _me()
        cps = []
        for w, (h_ref, o_ref) in enumerate(zip(ins, outs)):
            cps.append(pltpu.make_async_remote_copy(src_ref=h_ref, dst_ref=o_ref, send_sem=send_sems.at[w], recv_sem=recv_sems.at[w],
                                                    device_id=(x, y, 1 - c), device_id_type=MESH_ID))
            cps[-1].start()
        for cp in cps:
            cp.wait()

    return pl.pallas_call(
        body, name="grad_sibling_exchange", in_specs=[_ANY] * n, out_specs=[_ANY] * n,
        out_shape=[jax.ShapeDtypeStruct(h.shape, h.dtype) for h in halves],
        scratch_shapes=[pltpu.SemaphoreType.DMA((n,)), pltpu.SemaphoreType.DMA((n,))],
    )(*halves)


def _allreduce_small(name, vec):
    def body(v_ref, o_ref, buf_ref, send_sems, recv_sems):
        x, y, c = _me()
        me = 4 * x + 2 * y + c
        cps = []
        for p in range(1, 8):
            px, py, pc = x ^ (p >> 2), y ^ ((p >> 1) & 1), c ^ (p & 1)
            cps.append(pltpu.make_async_remote_copy(src_ref=v_ref, dst_ref=buf_ref.at[me], send_sem=send_sems.at[p - 1],
                                                    recv_sem=recv_sems.at[p - 1], device_id=(px, py, pc), device_id_type=MESH_ID))
            cps[-1].start()
        buf_ref[me] = v_ref[...]
        for p in range(1, 8):
            theirs = buf_ref.at[me ^ p]
            pltpu.make_async_remote_copy(src_ref=theirs, dst_ref=theirs, send_sem=send_sems.at[p - 1], recv_sem=recv_sems.at[p - 1],
                                         device_id=(x, y, c), device_id_type=MESH_ID).wait_recv()
        for cp in cps:
            cp.wait_send()
        acc = buf_ref[0]
        for k in range(1, 8):
            acc = acc + buf_ref[k]
        o_ref[...] = acc

    vm = pl.BlockSpec(memory_space=pltpu.VMEM)
    return pl.pallas_call(
        body, name=name, in_specs=[vm], out_specs=vm, out_shape=jax.ShapeDtypeStruct(vec.shape, F32),
        scratch_shapes=[pltpu.VMEM((8,) + vec.shape, F32), pltpu.SemaphoreType.DMA((7,)), pltpu.SemaphoreType.DMA((7,))],
    )(vec)


def _adam_math(w, g, m, v):
    m = ADAM_B1 * m + (1.0 - ADAM_B1) * g
    v = ADAM_B2 * v + (1.0 - ADAM_B2) * (g * g)
    m_hat = m / (1.0 - ADAM_B1 ** ADAM_STEP)
    v_hat = v / (1.0 - ADAM_B2 ** ADAM_STEP)
    return -ADAM_LR * (m_hat / (jnp.sqrt(v_hat) + ADAM_EPS) + ADAM_WD * w), m, v


def _adamw(name, w, g, m, v):
    R, C = w.shape
    tr = _row_block(R, C, 8)

    def body(w_ref, g_ref, m_ref, v_ref, d_ref, nm_ref, nv_ref):
        d_ref[...], nm_ref[...], nv_ref[...] = _adam_math(w_ref[...], g_ref[...], m_ref[...], v_ref[...])

    blk = pl.BlockSpec((tr, C), lambda i: (i, 0))
    return pl.pallas_call(
        body, name=name, grid=(R // tr,), in_specs=[blk] * 4, out_specs=[blk] * 3,
        out_shape=[jax.ShapeDtypeStruct((R, C), F32)] * 3, compiler_params=_params(("parallel",)),
    )(w, g, m, v)


def _adamw_halves(name, w, mine, theirs, m, v, c):
    _, rs, cs = w.shape
    h = rs // 2
    tr = _row_block(h, cs, 8)
    nb = h // tr

    def body(c_ref, w_ref, a_ref, b_ref, m_ref, v_ref, g_ref, d_ref, nm_ref, nv_ref):
        g = jnp.where(pl.program_id(0) == c_ref[0], a_ref[...], b_ref[...])
        g_ref[...] = g
        d_ref[...], nm_ref[...], nv_ref[...] = _adam_math(w_ref[...], g, m_ref[...], v_ref[...])

    full = pl.BlockSpec((None, tr, cs), lambda s, i, c_ref: (0, s * nb + i, 0))
    part = pl.BlockSpec((tr, cs), lambda s, i, c_ref: (i, 0))
    return pl.pallas_call(
        body, name=name,
        grid_spec=pltpu.PrefetchScalarGridSpec(num_scalar_prefetch=1, grid=(2, nb), in_specs=[full, part, part, full, full],
                                               out_specs=[full] * 4),
        out_shape=[jax.ShapeDtypeStruct((1, rs, cs), F32)] * 4, compiler_params=_params(("parallel", "parallel")),
    )(_scalar(c), w, mine, theirs, m, v)


def _pack_small(arrs):
    flat = jnp.concatenate([a.reshape(-1) for a in arrs])
    n = -(-flat.shape[0] // (8 * LANE)) * 8 * LANE
    return jnp.pad(flat, (0, n - flat.shape[0])).reshape(8, n // 8)


def _unpack_small(vec, shapes):
    flat, out, off = vec.reshape(-1), [], 0
    for s in shapes:
        out.append(flat[off:off + s[0] * s[1]].reshape(s))
        off += s[0] * s[1]
    return out


class _LateWeights:
    def __init__(self, cfg, tag, names, staged, after):
        self.cfg, self.tag, self.names, self.k = cfg, tag, names, 3 * len(names)
        self.send, self.recv, self.bufs, self.token = _split_start(f"gather_{tag}_chips_start", staged, self.k, _gather_to_chips,
                                                                    after)

    def pass_on(self, after):
        bufs = _split_wait(f"gather_{self.tag}_chips_wait", self.send, self.recv, self.bufs, after, _gather_to_chips)
        self.send, self.recv, self.bufs, token = _split_start(f"gather_{self.tag}_sibling_start", bufs, self.k, _gather_to_sibling,
                                                               self.token)
        return token

    def arrived(self, after):
        bufs = _split_wait(f"gather_{self.tag}_sibling_wait", self.send, self.recv, self.bufs, after, _gather_to_sibling)
        return {n: _gathered_to_kernel(self.cfg, n, b) for n, b in zip(self.names, bufs)}


def _step(cfg, a):
    chip = 2 * lax.axis_index("x") + lax.axis_index("y")
    core = lax.axis_index("c")
    big = [n for n, _, _, _ in cfg.BIG]

    ffn = ("w_gate", "w_up", "w_down")
    first = ("w_in", "w_uq", "w_ukv")
    staged = {n: _stage_shard(n, a[n], chip) for n in big}
    gathered = _allgather_weights([staged[n] for n in first])
    W = {n: _gathered_to_kernel(cfg, n, wg) for n, wg in zip(first, gathered)}
    out_weight = _LateWeights(cfg, "out", ("w_out",), [staged["w_out"]], gathered[0][0, :8, :LANE])
    ffn_weights = _LateWeights(cfg, "ffn", ffn, [staged[n] for n in ffn], out_weight.token)

    sp = {n: a[n] for n in SMALL}
    sharded = _pack_small([a[n] for n in SMALL_SHARDED])
    slot = jnp.where(lax.broadcasted_iota(I32, (N_CHIPS,) + sharded.shape, 0) == chip, 0.5 * sharded[None], 0.0)
    allp = _allreduce_small("allgather_small", slot.reshape(N_CHIPS * 8, -1)).reshape((N_CHIPS,) + sharded.shape)
    per_chip = [_unpack_small(allp[ch], [a[n].shape for n in SMALL_SHARDED]) for ch in range(N_CHIPS)]
    for k, n in enumerate(SMALL_SHARDED):
        sp[n] = jnp.concatenate([per_chip[ch][k] for ch in range(N_CHIPS)], axis=1)
    sp["mix_pre_g"] = sp["mix_pre_g"] + (out_weight.token[0, 0] + ffn_weights.token[0, 0])

    started = {}

    def start_exchange(tag, names, grads):
        grads = [_grad_to_chips(cfg, n, grads[n]) for n in names]
        sums = [_pair_sum(n, g, t, core) for n, g, t in zip(names, grads, _pair_exchange(tag, grads))]
        started[tag] = (names, _chip_exchange_start(tag, sums))
        return started[tag][1][-1]

    early = ("w_down", "w_gate", "w_up", "w_out")
    loss, grad_x, gW, gs = _local_grads(cfg, a["x"], a["loss_target"], W, sp, out_weight, ffn_weights,
                                        functools.partial(start_exchange, "early", early),
                                        functools.partial(start_exchange, "rest", first))
    sums, landed, after = {}, {}, grad_x
    for tag in ("rest", "early"):
        names, (send_sems, recv_sems, s_bufs, l_bufs, _) = started[tag]
        s_bufs, l_bufs = _chip_exchange_wait(tag, send_sems, recv_sems, s_bufs, l_bufs, after)
        sums.update(zip(names, s_bufs))
        landed.update(zip(names, l_bufs))
        after = l_bufs[0]
    mine = [_chip_sum(n, sums[n], landed[n], chip) for n in big]
    theirs = _sibling_exchange(mine)

    shapes = [gs[n].shape for n in SMALL] + [(1, LANE)]
    red = _unpack_small(_allreduce_small("allreduce_small", _pack_small([gs[n] for n in SMALL] + [loss])), shapes)
    g_small = dict(zip(SMALL, red[:-1]))
    for n in SMALL_SHARDED:
        cs = a[n].shape[1]
        g_small[n] = lax.dynamic_slice_in_dim(g_small[n], chip * cs, cs, axis=1)

    out = {"loss": red[-1][0, 0], "grad_x": grad_x}
    for n, gm, gt in zip(big, mine, theirs):
        out["grad_" + n], out["delta_" + n], out["new_m_" + n], out["new_v_" + n] = _adamw_halves(
            "adamw_" + n, a[n], gm, gt, a["m_" + n], a["v_" + n], core)
    sshapes = [a[n].shape for n in SMALL]
    d, nm, nv = _adamw("adamw_small", _pack_small([a[n] for n in SMALL]), _pack_small([g_small[n] for n in SMALL]),
                       _pack_small([a["m_" + n] for n in SMALL]), _pack_small([a["v_" + n] for n in SMALL]))
    for n, dd, mm, vv in zip(SMALL, _unpack_small(d, sshapes), _unpack_small(nm, sshapes), _unpack_small(nv, sshapes)):
        out["grad_" + n], out["delta_" + n], out["new_m_" + n], out["new_v_" + n] = g_small[n], dd, mm, vv
    return out


def kernel(x, mix_pre_g, w_in, q_norm_g, w_uq, kv_norm_g, w_ukv, ssm_conv_w, ssm_conv_b, dt_bias, a_log, d_skip, ssm_norm_g, w_out, mix_post_g, ffn_pre_g, w_gate, w_up, ffn_conv_w, ffn_conv_b, w_down, ffn_post_g, loss_target, m_mix_pre_g, m_w_in, m_q_norm_g, m_w_uq, m_kv_norm_g, m_w_ukv, m_ssm_conv_w, m_ssm_conv_b, m_dt_bias, m_a_log, m_d_skip, m_ssm_norm_g, m_w_out, m_mix_post_g, m_ffn_pre_g, m_w_gate, m_w_up, m_ffn_conv_w, m_ffn_conv_b, m_w_down, m_ffn_post_g, v_mix_pre_g, v_w_in, v_q_norm_g, v_w_uq, v_kv_norm_g, v_w_ukv, v_ssm_conv_w, v_ssm_conv_b, v_dt_bias, v_a_log, v_d_skip, v_ssm_norm_g, v_w_out, v_mix_post_g, v_ffn_pre_g, v_w_gate, v_w_up, v_ffn_conv_w, v_ffn_conv_b, v_w_down, v_ffn_post_g):
    args = dict(locals())
    big = {pre + n for n, _, _, _ in _FULL.BIG for pre in ("", "m_", "v_")}
    out = _step(_FULL, {k: (v[0] if v.ndim == 3 and k not in big else v) for k, v in args.items()})
    res = [out["loss"], out["grad_x"][None]]
    for pre in ("grad_", "delta_", "new_m_", "new_v_"):
        res += [out[pre + n][None] if args[n].ndim == 3 and n not in big else out[pre + n] for n in WEIGHTS]
    return tuple(res)
```

```python
import functools
import math

import jax
import jax.numpy as jnp
from jax import lax
from jax.experimental import pallas as pl
from jax.experimental.pallas import tpu as pltpu

F32, BF16, I32 = jnp.float32, jnp.bfloat16, jnp.int32
NN = (((1,), (0,)), ((), ()))
NT = (((1,), (1,)), ((), ()))
TN = (((0,), (0,)), ((), ()))
HI = lax.Precision.HIGHEST
MESH_ID = pl.DeviceIdType.MESH

EPS = 1e-6
CHUNK = 64
NOPE, ROPE, VH = 128, 64, 128
ROPE_THETA = 10000.0
HP, NST = 64, 128
SSM_K, FFN_K = 4, 3
LANE = 128
N_CHIPS = 4
VMEM_LIMIT = 52 * 1024 * 1024
MM_TILE, MM_TILE_K = 1408, 2816

ADAM_LR, ADAM_B1, ADAM_B2, ADAM_EPS, ADAM_WD, ADAM_STEP = 0.001, 0.9, 0.999, 1e-08, 0.01, 10


class _Cfg:
    def __init__(self, S, D, QL, KVL, H, HS, G, DFF, T):
        self.S, self.D, self.QL, self.KVL, self.H, self.HS, self.G, self.DFF, self.T = S, D, QL, KVL, H, HS, G, DFF, T
        self.INNER = HS * HP
        self.CONVCH = self.INNER + 2 * G * NST
        self.QW = H * (NOPE + ROPE)
        self.KVW = H * (NOPE + VH)
        self.MLAW = H * VH
        self.MIXW = self.MLAW + self.INNER
        self.IN_COLS = QL + KVL + ROPE + self.INNER + self.CONVCH + HS
        self.o_kr = QL + KVL
        self.o_z = self.o_kr + LANE
        self.o_xbc = self.o_z + self.INNER
        self.o_dt = self.o_xbc + self.CONVCH
        self.EXT = self.o_dt + LANE
        self.NPAIR = HS // 2
        self.REP = HS // G


_FULL = _Cfg(S=2048, D=2048, QL=768, KVL=512, H=8, HS=16, G=2, DFF=5632, T=256)
BIG = ("w_in", "w_uq", "w_ukv", "w_out", "w_gate", "w_up", "w_down")

SMALL = ("mix_pre_g", "q_norm_g", "kv_norm_g", "ssm_conv_w", "ssm_conv_b", "dt_bias", "a_log", "d_skip", "ssm_norm_g",
         "mix_post_g", "ffn_pre_g", "ffn_conv_w", "ffn_conv_b", "ffn_post_g")
SMALL_SHARDED = ("ssm_conv_w", "ffn_conv_w")
WEIGHTS = ("mix_pre_g", "w_in", "q_norm_g", "w_uq", "kv_norm_g", "w_ukv", "ssm_conv_w", "ssm_conv_b", "dt_bias", "a_log",
           "d_skip", "ssm_norm_g", "w_out", "mix_post_g", "ffn_pre_g", "w_gate", "w_up", "ffn_conv_w", "ffn_conv_b",
           "w_down", "ffn_post_g")


def _pick(n, target, mult):
    best = None
    for d in range(mult, min(n, target) + 1, mult):
        if n % d == 0:
            best = d
    return best if best is not None else n


def _params(sem=None):
    kw = dict(vmem_limit_bytes=VMEM_LIMIT)
    if sem is not None:
        kw["dimension_semantics"] = sem
    return pltpu.CompilerParams(**kw)


def _dot(a, b, dims=NN, precision=None):
    return lax.dot_general(a, b, dims, preferred_element_type=F32, precision=precision)


def _sigmoid(x):
    return 1.0 / (1.0 + jnp.exp(-x))


def _rs(x):
    return lax.rsqrt(jnp.mean(x * x, axis=-1, keepdims=True) + EPS)


def _rms_back(xh, r, dn):
    return r * (dn - xh * jnp.mean(dn * xh, axis=-1, keepdims=True))


def _colsum(v):
    return jnp.sum(v, axis=0, keepdims=True)


def _matmul(name, a, b, mode, out_dtype, a2=None, b2=None, chips=False):
    cs = None
    if mode == "nn":
        (M, K), N = a.shape, b.shape[-1]
        if chips:
            cs, N = N, N_CHIPS * N
    elif mode == "nt":
        (M, K), N = a.shape, b.shape[-2]
        if chips:
            cs = b.shape[-1]
    else:
        (K, M), N = a.shape, b.shape[1]
        if chips:
            cs = N // N_CHIPS
    tm = _pick(M, MM_TILE, LANE)
    tn = _pick(cs if chips and mode != "nt" else N, MM_TILE, LANE)
    tk = _pick(cs, MM_TILE, LANE) if chips and mode == "nt" else _pick(K, MM_TILE_K, LANE)
    nk = K // tk
    dims = {"nn": NN, "nt": NT, "tn": TN}[mode]
    a_spec = pl.BlockSpec((tk, tm), lambda i, j, k: (k, i)) if mode == "tn" else pl.BlockSpec((tm, tk), lambda i, j, k: (i, k))
    b_spec = pl.BlockSpec((tn, tk), lambda i, j, k: (j, k)) if mode == "nt" else pl.BlockSpec((tk, tn), lambda i, j, k: (k, j))
    o_spec = pl.BlockSpec((tm, tn), lambda i, j, k: (i, j))
    o_shape = (M, N)
    if chips and mode == "nn":
        per = cs // tn
        b_spec = pl.BlockSpec((None, tk, tn), lambda i, j, k: (j // per, k, j % per))
    elif chips and mode == "nt":
        per = cs // tk
        b_spec = pl.BlockSpec((None, tn, tk), lambda i, j, k: (k // per, j, k % per))
    elif chips:
        per = cs // tn
        o_spec = pl.BlockSpec((None, tm, tn), lambda i, j, k: (j // per, i, j % per))
        o_shape = (N_CHIPS, M, cs)
    two = a2 is not None

    def product(refs):
        part = _dot(refs[0][...].astype(BF16), refs[1][...].astype(BF16), dims)
        if two:
            part += _dot(refs[2][...].astype(BF16), refs[3][...].astype(BF16), dims)
        return part

    def body_whole_k(*refs):
        refs[-1][...] = product(refs).astype(refs[-1].dtype)

    def body(*refs):
        o_ref, acc_ref = refs[-2], refs[-1]
        k = pl.program_id(2)

        @pl.when(k == 0)
        def _():
            acc_ref[...] = product(refs)

        @pl.when(k > 0)
        def _():
            acc_ref[...] += product(refs)

        @pl.when(k == nk - 1)
        def _():
            o_ref[...] = acc_ref[...].astype(o_ref.dtype)

    ins = (a, b, a2, b2) if two else (a, b)
    return pl.pallas_call(
        body_whole_k if nk == 1 else body, name=name, grid=(M // tm, N // tn, nk),
        in_specs=[a_spec, b_spec] * (2 if two else 1),
        out_specs=o_spec,
        out_shape=jax.ShapeDtypeStruct(o_shape, out_dtype),
        scratch_shapes=[] if nk == 1 else [pltpu.VMEM((tm, tn), F32)],
        compiler_params=_params(("parallel", "parallel", "arbitrary")),
    )(*ins)


def _rowwise(name, fn, rows, mats, outs, reds, ts):
    S = rows[0].shape[0]
    nr, nm, no = len(rows), len(mats), len(outs)

    def body(*refs):
        res = fn(*[r[...] for r in refs[:nr + nm]])
        res = res if isinstance(res, (tuple, list)) else (res,)
        for r, v in zip(refs[nr + nm:nr + nm + no], res[:no]):
            r[...] = v.astype(r.dtype)
        first = pl.program_id(0) == 0
        for r, v in zip(refs[nr + nm + no:], res[no:]):
            @pl.when(first)
            def _():
                r[...] = jnp.broadcast_to(v, r.shape)

            @pl.when(jnp.logical_not(first))
            def _():
                r[...] += jnp.broadcast_to(v, r.shape)

    in_specs = [pl.BlockSpec((ts, a.shape[1]), lambda i: (i, 0)) for a in rows]
    in_specs += [pl.BlockSpec(m.shape, lambda i, nd=m.ndim: (0,) * nd) for m in mats]
    out_specs = [pl.BlockSpec((ts, w), lambda i: (i, 0)) for w, _ in outs]
    out_specs += [pl.BlockSpec(s, lambda i: (0, 0)) for s in reds]
    out_shape = [jax.ShapeDtypeStruct((S, w), dt) for w, dt in outs] + [jax.ShapeDtypeStruct(s, F32) for s in reds]
    return pl.pallas_call(
        body, name=name, grid=(S // ts,), in_specs=in_specs, out_specs=out_specs, out_shape=out_shape,
        compiler_params=_params(("arbitrary",) if reds else ("parallel",)),
    )(*rows, *mats)


def _shift_down(v, s):
    if s == 0:
        return v
    rows = lax.broadcasted_iota(I32, v.shape, 0)
    return jnp.where(rows >= s, pltpu.roll(v, s, 0), 0.0)


def _shift_up(v, s):
    if s == 0:
        return v
    n = v.shape[0]
    rows = lax.broadcasted_iota(I32, v.shape, 0)
    return jnp.where(rows < n - s, pltpu.roll(v, n - s, 0), 0.0)


def _conv(x, w, b):
    K = w.shape[0]
    y = jnp.broadcast_to(b, x.shape)
    for k in range(K):
        y = y + w[k:k + 1, :] * _shift_down(x, K - 1 - k)
    return y


def _conv_back(x, w, dc):
    K = w.shape[0]
    dx = jnp.zeros_like(x)
    dw = []
    for k in range(K):
        dx = dx + w[k:k + 1, :] * _shift_up(dc, K - 1 - k)
        dw.append(_colsum(dc * _shift_down(x, K - 1 - k)))
    return dx, jnp.concatenate(dw, axis=0), _colsum(dc)


def _colwise(name, fn, cols, vecs, outs, pouts, tc):
    S, C = cols[0].shape
    nc_, nv, no = len(cols), len(vecs), len(outs)

    def body(*refs):
        res = fn(*[r[...] for r in refs[:nc_ + nv]])
        res = res if isinstance(res, (tuple, list)) else (res,)
        for r, v in zip(refs[nc_ + nv:], res):
            r[...] = v.astype(r.dtype)

    in_specs = [pl.BlockSpec((S, tc), lambda j: (0, j)) for _ in cols]
    in_specs += [pl.BlockSpec((v.shape[0], tc), lambda j: (0, j)) for v in vecs]
    out_specs = [pl.BlockSpec((S, tc), lambda j: (0, j)) for _ in outs] + [pl.BlockSpec((k, tc), lambda j: (0, j)) for k in pouts]
    out_shape = [jax.ShapeDtypeStruct((S, C), dt) for dt in outs] + [jax.ShapeDtypeStruct((k, C), F32) for k in pouts]
    return pl.pallas_call(
        body, name=name, grid=(C // tc,), in_specs=in_specs, out_specs=out_specs, out_shape=out_shape,
        compiler_params=_params(("parallel",)),
    )(*cols, *vecs)


_G0, _G1 = math.sqrt(2.0 / math.pi), 0.044715


def _gelu(g):
    th = jnp.tanh(_G0 * (g + _G1 * g * g * g))
    return 0.5 * g * (1.0 + th), th


def _ffn_act(gate_pre, up, w, b):
    act, _ = _gelu(_conv(gate_pre, w, b))
    return act * up


def _ffn_act_back(dact, gate_pre, up, w, b):
    g = _conv(gate_pre, w, b)
    ge, th = _gelu(g)
    dge = 0.5 * (1.0 + th) + 0.5 * g * (1.0 - th * th) * _G0 * (1.0 + 3.0 * _G1 * g * g)
    dup = dact * ge
    dgate_pre, dw, db = _conv_back(gate_pre, w, dact * up * dge)
    return dgate_pre, dup, dw, db


def _ssm_act(xbc, w, b):
    c = _conv(xbc, w, b)
    return c * _sigmoid(c)


def _ssm_act_back(dxc, xbc, w, b):
    c = _conv(xbc, w, b)
    sg = _sigmoid(c)
    return _conv_back(xbc, w, dxc * sg * (1.0 + c * (1.0 - sg)))


def _rope_tables(S):
    inv = 1.0 / (ROPE_THETA ** (jnp.arange(0, ROPE, 2, dtype=F32) / ROPE))
    ang = jnp.arange(S, dtype=F32)[:, None] * inv[None, :]
    cos, sin = jnp.cos(ang), jnp.sin(ang)
    return jnp.tile(cos, (1, 4)), jnp.tile(jnp.concatenate([-sin, sin], axis=1), (1, 2))


def _swap_halves(x):
    lane = lax.broadcasted_iota(I32, x.shape, 1)
    w = x.shape[1]
    return jnp.where((lane % ROPE) < ROPE // 2, pltpu.roll(x, w - ROPE // 2, 1), pltpu.roll(x, ROPE // 2, 1))


def _rot(x, cos2, sin2):
    return x * cos2 + _swap_halves(x) * sin2


def _rot_back(dy, cos2, sin2):
    return dy * cos2 + _swap_halves(dy * sin2)


def _mla_pack(cfg, q, kv, kr, cos2, sin2):
    S, H = cfg.S, cfg.H
    ts = _pick(S, 512, 8)

    def body(qn_ref, qr_ref, kn_ref, v_ref, kr_ref, c_ref, s_ref, Q_ref, K_ref, V_ref):
        h = pl.program_id(0)
        c2, s2 = c_ref[...], s_ref[...]
        Q_ref[0, :, 0:LANE] = qn_ref[...].astype(BF16)
        Q_ref[0, :, LANE:] = _rot(qr_ref[...], c2, s2).astype(BF16)
        K_ref[0, :, 0:LANE] = kn_ref[...].astype(BF16)
        krr = _rot(kr_ref[...], c2, s2)
        K_ref[0, :, LANE:] = jnp.where(h % 2 == 1, pltpu.roll(krr, ROPE, 1), krr).astype(BF16)
        V_ref[0] = v_ref[...].astype(BF16)

    blk = lambda f: pl.BlockSpec((ts, LANE), f)
    return pl.pallas_call(
        body, name="mla_pack", grid=(H, S // ts),
        in_specs=[blk(lambda h, i: (i, h)), blk(lambda h, i: (i, H + h // 2)), blk(lambda h, i: (i, h)),
                  blk(lambda h, i: (i, H + h)), blk(lambda h, i: (i, 0)), blk(lambda h, i: (i, 0)), blk(lambda h, i: (i, 0))],
        out_specs=[pl.BlockSpec((1, ts, 2 * LANE), lambda h, i: (h, i, 0)), pl.BlockSpec((1, ts, 2 * LANE), lambda h, i: (h, i, 0)),
                   pl.BlockSpec((1, ts, LANE), lambda h, i: (h, i, 0))],
        out_shape=[jax.ShapeDtypeStruct((H, S, 2 * LANE), BF16), jax.ShapeDtypeStruct((H, S, 2 * LANE), BF16),
                   jax.ShapeDtypeStruct((H, S, LANE), BF16)],
        compiler_params=_params(("parallel", "parallel")),
    )(q, q, kv, kv, kr, cos2, sin2)


def _mla_unpack(cfg, dQ, dK, dV, cos2, sin2):
    S, H = cfg.S, cfg.H
    ts = _pick(S, 256, 8)

    def body(dQ_ref, dK_ref, dV_ref, c_ref, s_ref, dq_ref, dkv_ref, dkr_ref):
        c2, s2 = c_ref[...], s_ref[...]
        lo = lax.broadcasted_iota(I32, (ts, LANE), 1) < ROPE
        tk = jnp.zeros((ts, LANE), F32)
        for h in range(H):
            dq_ref[:, h * LANE:(h + 1) * LANE] = dQ_ref[h, :, 0:LANE].astype(BF16)
            dkv_ref[:, h * LANE:(h + 1) * LANE] = dK_ref[h, :, 0:LANE].astype(BF16)
            dkv_ref[:, (H + h) * LANE:(H + h + 1) * LANE] = dV_ref[h].astype(BF16)
            own = lo if h % 2 == 0 else jnp.logical_not(lo)
            tk = tk + jnp.where(own, dK_ref[h, :, LANE:], 0.0)
        for j in range(H // 2):
            dr = dQ_ref[2 * j, :, LANE:] + dQ_ref[2 * j + 1, :, LANE:]
            dq_ref[:, (H + j) * LANE:(H + j + 1) * LANE] = _rot_back(dr, c2, s2).astype(BF16)
        dkr_rot = jnp.where(lo, tk + pltpu.roll(tk, ROPE, 1), 0.0)
        dkr_ref[...] = _rot_back(dkr_rot, c2, s2).astype(BF16)

    tab = pl.BlockSpec((ts, LANE), lambda i: (i, 0))
    return pl.pallas_call(
        body, name="mla_unpack", grid=(S // ts,),
        in_specs=[pl.BlockSpec((H, ts, 2 * LANE), lambda i: (0, i, 0)), pl.BlockSpec((H, ts, 2 * LANE), lambda i: (0, i, 0)),
                  pl.BlockSpec((H, ts, LANE), lambda i: (0, i, 0)), tab, tab],
        out_specs=[pl.BlockSpec((ts, cfg.QW), lambda i: (i, 0)), pl.BlockSpec((ts, cfg.KVW), lambda i: (i, 0)), tab],
        out_shape=[jax.ShapeDtypeStruct((S, cfg.QW), BF16), jax.ShapeDtypeStruct((S, cfg.KVW), BF16),
                   jax.ShapeDtypeStruct((S, LANE), BF16)],
        compiler_params=_params(("parallel",)),
    )(dQ, dK, dV, cos2, sin2)


_ATT_T = 256
_ATT_HB = 2
_ATT_SCALE = (NOPE + ROPE) ** -0.5


def _diag_mask(transposed=False):
    r = lax.broadcasted_iota(I32, (_ATT_T, _ATT_T), 0) // CHUNK
    c = lax.broadcasted_iota(I32, (_ATT_T, _ATT_T), 1) // CHUNK
    return r <= c if transposed else c <= r


def _row_form(col):
    return jnp.broadcast_to(col, (col.shape[0], LANE)).T[0:8, :]


def _attn_fwd(cfg, Q, K, V):
    S, H, T, HB = cfg.S, cfg.H, _ATT_T, _ATT_HB

    def body(q_ref, k_ref, v_ref, o_ref, lse_ref, lse_t_ref):
        qi = pl.program_id(1)

        def head_step(b, kb, carry, mask):
            m, l, acc = carry
            ks = pl.multiple_of(kb * T, T)
            s = _dot(q_ref[b], k_ref[b, pl.ds(ks, T), :], NT) * _ATT_SCALE
            if mask is not None:
                s = jnp.where(mask, s, -1e30)
            m_new = jnp.maximum(m, jnp.max(s, axis=1, keepdims=True))
            p = jnp.exp(s - m_new)
            alpha = jnp.exp(m - m_new)
            l = alpha * l + jnp.sum(p, axis=1, keepdims=True)
            acc = alpha * acc + _dot(p.astype(BF16), v_ref[b, pl.ds(ks, T), :])
            return m_new, l, acc

        def step(kb, carry, mask=None):
            return tuple(head_step(b, kb, carry[b], mask) for b in range(HB))

        init = (jnp.full((T, 1), -1e30, F32), jnp.zeros((T, 1), F32), jnp.zeros((T, VH), F32))
        done = step(qi, lax.fori_loop(0, qi, step, (init,) * HB), _diag_mask())
        for b, (m, l, acc) in enumerate(done):
            o_ref[:, b * LANE:(b + 1) * LANE] = acc / l
            lse = m + jnp.log(l)
            lse_ref[:, b * LANE:(b + 1) * LANE] = jnp.broadcast_to(lse, (T, LANE))
            lse_t_ref[b] = _row_form(lse)

    return pl.pallas_call(
        body, name="attn_fwd", grid=(H // HB, S // T),
        in_specs=[pl.BlockSpec((HB, T, 2 * LANE), lambda h, i: (h, i, 0)), pl.BlockSpec((HB, S, 2 * LANE), lambda h, i: (h, 0, 0)),
                  pl.BlockSpec((HB, S, LANE), lambda h, i: (h, 0, 0))],
        out_specs=[pl.BlockSpec((T, HB * LANE), lambda h, i: (i, h)), pl.BlockSpec((T, HB * LANE), lambda h, i: (i, h)),
                   pl.BlockSpec((HB, 8, T), lambda h, i: (h, 0, i))],
        out_shape=[jax.ShapeDtypeStruct((S, H * LANE), F32), jax.ShapeDtypeStruct((S, H * LANE), F32),
                   jax.ShapeDtypeStruct((H, 8, S), F32)],
        compiler_params=_params(("parallel", "parallel")),
    )(Q, K, V)


def _attn_dq(cfg, Q, K, V, do, o, lse):
    S, H, T, HB = cfg.S, cfg.H, _ATT_T, _ATT_HB

    def body(q_ref, k_ref, v_ref, do_ref, o_ref, lse_ref, dq_ref, dl_t_ref):
        qi = pl.program_id(1)
        do = [do_ref[:, b * LANE:(b + 1) * LANE] for b in range(HB)]
        delta = [jnp.sum(do[b] * o_ref[:, b * LANE:(b + 1) * LANE], axis=1, keepdims=True) for b in range(HB)]
        dob = [d.astype(BF16) for d in do]

        def head_step(b, kb, dq, mask):
            ks = pl.multiple_of(kb * T, T)
            k = k_ref[b, pl.ds(ks, T), :]
            s = _dot(q_ref[b], k, NT) * _ATT_SCALE
            if mask is not None:
                s = jnp.where(mask, s, -1e30)
            p = jnp.exp(s - lse_ref[:, b * LANE:b * LANE + 1])
            dp = _dot(dob[b], v_ref[b, pl.ds(ks, T), :], NT)
            ds = p * (dp - delta[b]) * _ATT_SCALE
            return dq + _dot(ds.astype(BF16), k)

        def step(kb, dqs, mask=None):
            return tuple(head_step(b, kb, dqs[b], mask) for b in range(HB))

        dqs = step(qi, lax.fori_loop(0, qi, step, (jnp.zeros((T, 2 * LANE), F32),) * HB), _diag_mask())
        for b in range(HB):
            dq_ref[b] = dqs[b]
            dl_t_ref[b] = _row_form(delta[b])

    col = pl.BlockSpec((T, HB * LANE), lambda h, i: (i, h))
    return pl.pallas_call(
        body, name="attn_dq", grid=(H // HB, S // T),
        in_specs=[pl.BlockSpec((HB, T, 2 * LANE), lambda h, i: (h, i, 0)), pl.BlockSpec((HB, S, 2 * LANE), lambda h, i: (h, 0, 0)),
                  pl.BlockSpec((HB, S, LANE), lambda h, i: (h, 0, 0)), col, col, col],
        out_specs=[pl.BlockSpec((HB, T, 2 * LANE), lambda h, i: (h, i, 0)), pl.BlockSpec((HB, 8, T), lambda h, i: (h, 0, i))],
        out_shape=[jax.ShapeDtypeStruct((H, S, 2 * LANE), F32), jax.ShapeDtypeStruct((H, 8, S), F32)],
        compiler_params=_params(("parallel", "parallel")),
    )(Q, K, V, do, o, lse)


def _attn_dkv(cfg, Q, K, V, do, lse_t, delta_t):
    S, H, T, HB = cfg.S, cfg.H, _ATT_T, _ATT_HB
    nq = S // T

    def body(q_ref, k_ref, v_ref, do_ref, lse_ref, dl_ref, dk_ref, dv_ref):
        kb = pl.program_id(1)

        def head_step(b, qi, carry, mask):
            dk, dv = carry
            qs = pl.multiple_of(qi * T, T)
            q = q_ref[b, pl.ds(qs, T), :]
            dob = do_ref[pl.ds(qs, T), b * LANE:(b + 1) * LANE].astype(BF16)
            s = _dot(k_ref[b], q, NT) * _ATT_SCALE
            if mask is not None:
                s = jnp.where(mask, s, -1e30)
            p = jnp.exp(s - lse_ref[b, 0:1, pl.ds(qs, T)])
            dv = dv + _dot(p.astype(BF16), dob)
            dp = _dot(v_ref[b], dob, NT)
            ds = p * (dp - dl_ref[b, 0:1, pl.ds(qs, T)]) * _ATT_SCALE
            dk = dk + _dot(ds.astype(BF16), q)
            return dk, dv

        def step(qi, carry, mask=None):
            return tuple(head_step(b, qi, carry[b], mask) for b in range(HB))

        zero = (jnp.zeros((T, 2 * LANE), F32), jnp.zeros((T, VH), F32))
        done = lax.fori_loop(kb + 1, nq, step, step(kb, (zero,) * HB, _diag_mask(transposed=True)))
        for b, (dk, dv) in enumerate(done):
            dk_ref[b] = dk
            dv_ref[b] = dv

    row = pl.BlockSpec((HB, 8, S), lambda h, j: (h, 0, 0))
    return pl.pallas_call(
        body, name="attn_dkv", grid=(H // HB, S // T),
        in_specs=[pl.BlockSpec((HB, S, 2 * LANE), lambda h, j: (h, 0, 0)), pl.BlockSpec((HB, T, 2 * LANE), lambda h, j: (h, j, 0)),
                  pl.BlockSpec((HB, T, LANE), lambda h, j: (h, j, 0)), pl.BlockSpec((S, HB * LANE), lambda h, j: (0, h)), row, row],
        out_specs=[pl.BlockSpec((HB, T, 2 * LANE), lambda h, j: (h, j, 0)), pl.BlockSpec((HB, T, LANE), lambda h, j: (h, j, 0))],
        out_shape=[jax.ShapeDtypeStruct((H, S, 2 * LANE), F32), jax.ShapeDtypeStruct((H, S, LANE), F32)],
        compiler_params=_params(("parallel", "parallel")),
    )(Q, K, V, do, lse_t, delta_t)


def _expand_matrix(cfg):
    r = lax.broadcasted_iota(I32, (LANE, cfg.INNER), 0)
    c = lax.broadcasted_iota(I32, (LANE, cfg.INNER), 1)
    return (r == c // HP).astype(F32)


def _softplus(x):
    return jnp.maximum(x, 0.0) + jnp.log(1.0 + jnp.exp(-jnp.abs(x)))


def _ssd_prep(cfg, dt_raw, dt_bias_pad, a_log_pad, expand):
    HS = cfg.HS

    def fn(raw, bias, alog, E):
        heads = lax.broadcasted_iota(I32, raw.shape, 1) < HS
        dt = jnp.where(heads, _softplus(raw + bias), 0.0)
        a = dt * jnp.where(heads[0:1], -jnp.exp(alog), 0.0)
        return dt, a, _dot(dt, E, precision=HI), _dot(a, E, precision=HI)

    return _rowwise("ssd_prep", fn, [dt_raw], [dt_bias_pad, a_log_pad, expand],
                    [(LANE, F32), (LANE, F32), (cfg.INNER, F32), (cfg.INNER, F32)], [], _pick(cfg.S, 512, 8))


def _tril(T):
    return lax.broadcasted_iota(I32, (T, T), 0) >= lax.broadcasted_iota(I32, (T, T), 1)


def _ssd_fwd(cfg, xc, dt_exp, a_exp, a_small, dskip_exp):
    S, T, INNER, G, NPAIR = cfg.S, cfg.T, cfg.INNER, cfg.G, cfg.NPAIR
    NC = S // T

    def body(xc_ref, dte_ref, ae_ref, as_ref, dsk_ref, y_ref, hin_ref, ht_ref):
        @pl.when(pl.program_id(0) == 0)
        def _():
            ht_ref[...] = jnp.zeros_like(ht_ref)

        tril = _tril(T)
        tri = tril.astype(F32)
        acs_s = _dot(tri, as_ref[...], precision=HI)
        acs_e = _dot(tri, ae_ref[...], precision=HI)
        acs_t = acs_s.T
        lo = lax.broadcasted_iota(I32, (T, LANE), 1) < HP
        for g in range(G):
            Bb = xc_ref[:, INNER + g * NST:INNER + (g + 1) * NST].astype(BF16)
            Cb = xc_ref[:, INNER + (G + g) * NST:INNER + (G + g + 1) * NST].astype(BF16)
            Gm = _dot(Cb, Bb, NT)
            for j in range(g * NPAIR // G, (g + 1) * NPAIR // G):
                sl = slice(j * LANE, (j + 1) * LANE)
                Xp = xc_ref[:, sl]
                Xdt = Xp * dte_ref[:, sl]
                Xb = Xdt.astype(BF16)
                acs_p = acs_e[:, sl]
                last = acs_p[T - 1:T, :]
                Hin = ht_ref[j]
                hin_ref[0, j] = Hin
                yd = []
                for e in (0, 1):
                    h = 2 * j + e
                    Lm = jnp.exp(jnp.where(tril, acs_s[:, h:h + 1] - acs_t[h:h + 1, :], -1e30))
                    yd.append(_dot((Gm * Lm).astype(BF16), Xb))
                y_off = _dot(Cb, Hin.astype(BF16)) * jnp.exp(acs_p)
                y_ref[:, sl] = jnp.where(lo, yd[0], yd[1]) + y_off + Xp * dsk_ref[:, sl]
                st = _dot(Bb, (Xdt * jnp.exp(last - acs_p)).astype(BF16), TN)
                ht_ref[j] = jnp.exp(last) * Hin + st

    rows = lambda w: pl.BlockSpec((T, w), lambda c: (c, 0))
    return pl.pallas_call(
        body, name="ssd_fwd", grid=(NC,),
        in_specs=[rows(cfg.CONVCH), rows(INNER), rows(INNER), rows(LANE), pl.BlockSpec((1, INNER), lambda c: (0, 0))],
        out_specs=[rows(INNER), pl.BlockSpec((1, NPAIR, NST, LANE), lambda c: (c, 0, 0, 0))],
        out_shape=[jax.ShapeDtypeStruct((S, INNER), F32), jax.ShapeDtypeStruct((NC, NPAIR, NST, LANE), F32)],
        scratch_shapes=[pltpu.VMEM((NPAIR, NST, LANE), F32)],
        compiler_params=_params(("arbitrary",)),
    )(xc, dt_exp, a_exp, a_small, dskip_exp)


def _ssd_bwd(cfg, dy, xc, dt_exp, a_exp, a_small, dskip_exp, hin, dt_raw, dt_bias_pad, a_log_pad, expand):
    S, T, INNER, G, NPAIR, HS = cfg.S, cfg.T, cfg.INNER, cfg.G, cfg.NPAIR, cfg.HS
    NC = S // T

    def body(dy_ref, xc_ref, dte_ref, ae_ref, as_ref, dsk_ref, hin_ref, raw_ref, bias_ref, alog_ref, e_ref,
             dxc_ref, draw_ref, dbias_ref, dalog_ref, dskip_ref, dht_ref, cols_ref, rows_ref, dacs_ref, ddt_ref):
        first = pl.program_id(0) == 0

        @pl.when(first)
        def _():
            dht_ref[...] = jnp.zeros_like(dht_ref)

        tril = _tril(T)
        tri = tril.astype(F32)
        a_s = as_ref[...]
        acs_s = _dot(tri, a_s, precision=HI)
        acs_e = _dot(tri, ae_ref[...], precision=HI)
        acs_t = acs_s.T
        lo = lax.broadcasted_iota(I32, (T, LANE), 1) < HP
        last_row = lax.broadcasted_iota(I32, (T, LANE), 0) == T - 1
        cols_ref[...] = jnp.zeros_like(cols_ref)
        rows_ref[...] = jnp.zeros_like(rows_ref)
        dsk_parts = []
        for g in range(G):
            bsl = slice(INNER + g * NST, INNER + (g + 1) * NST)
            csl = slice(INNER + (G + g) * NST, INNER + (G + g + 1) * NST)
            Bb = xc_ref[:, bsl].astype(BF16)
            Cb = xc_ref[:, csl].astype(BF16)
            Gm = _dot(Cb, Bb, NT)
            dG = jnp.zeros((T, T), F32)
            dB = jnp.zeros((T, NST), F32)
            dC = jnp.zeros((T, NST), F32)
            for j in range(g * NPAIR // G, (g + 1) * NPAIR // G):
                sl = slice(j * LANE, (j + 1) * LANE)
                Xp = xc_ref[:, sl]
                dtp = dte_ref[:, sl]
                Xdt = Xp * dtp
                Xb = Xdt.astype(BF16)
                acs_p = acs_e[:, sl]
                last = acs_p[T - 1:T, :]
                e_p, dec, cd = jnp.exp(acs_p), jnp.exp(last - acs_p), jnp.exp(last)
                Hin = hin_ref[0, j]
                Hb = Hin.astype(BF16)
                dHn = dht_ref[j]
                dHb = dHn.astype(BF16)
                dYp = dy_ref[:, sl]
                z = _dot(Cb, Hb)
                dz = (dYp * e_p).astype(BF16)
                dacs_p = dYp * z * e_p
                dC = dC + _dot(dz, Hb, NT)
                dHin = _dot(Cb, dz, TN) + cd * dHn
                dlast = _colsum(dHn * Hin) * cd
                qv = _dot(Bb, dHb)
                dXdt = qv * dec
                ddec = qv * Xdt * dec
                dacs_p = dacs_p - ddec
                dlast = dlast + _colsum(ddec)
                dB = dB + _dot((Xdt * dec).astype(BF16), dHb, NT)
                for e in (0, 1):
                    h = 2 * j + e
                    Lm = jnp.exp(jnp.where(tril, acs_s[:, h:h + 1] - acs_t[h:h + 1, :], -1e30))
                    Mh = Gm * Lm
                    dYe = jnp.where(lo if e == 0 else jnp.logical_not(lo), dYp, 0.0).astype(BF16)
                    dM = _dot(dYe, Xb, NT)
                    dXdt = dXdt + _dot(Mh.astype(BF16), dYe, TN)
                    W = dM * Mh
                    cols_ref[:, h:h + 1] = jnp.sum(W, axis=1, keepdims=True)
                    rows_ref[h:h + 1, :] = _colsum(W)
                    dG = dG + dM * Lm
                dacs_ref[:, sl] = dacs_p + jnp.where(last_row, dlast, 0.0)
                ddt_ref[:, sl] = dXdt * Xp
                dxc_ref[:, sl] = dXdt * dtp + dYp * dsk_ref[:, sl]
                dsk_parts.append(_colsum(dYp * Xp))
                dht_ref[j] = dHin
            dGb = dG.astype(BF16)
            dxc_ref[:, bsl] = dB + _dot(dGb, Cb, TN)
            dxc_ref[:, csl] = dC + _dot(dGb, Bb)
        E = e_ref[...]
        dacs_s = cols_ref[...] - rows_ref[...].T + _dot(dacs_ref[...], E, NT, precision=HI)
        da = _dot(tri, dacs_s, TN, precision=HI)
        heads = lax.broadcasted_iota(I32, (1, LANE), 1) < HS
        A = jnp.where(heads, -jnp.exp(alog_ref[...]), 0.0)
        ddt = _dot(ddt_ref[...], E, NT, precision=HI) + da * A
        draw = jnp.where(heads, ddt * _sigmoid(raw_ref[...] + bias_ref[...]), 0.0)
        draw_ref[...] = draw
        dsk = _dot(jnp.broadcast_to(jnp.concatenate(dsk_parts, axis=1), (8, INNER)), E, NT, precision=HI)[0:1]
        for ref, val in ((dbias_ref, _colsum(draw)), (dalog_ref, _colsum(da * a_s)), (dskip_ref, dsk)):
            @pl.when(first)
            def _():
                ref[...] = val

            @pl.when(jnp.logical_not(first))
            def _():
                ref[...] += val

    rows = lambda w: pl.BlockSpec((T, w), lambda c: (NC - 1 - c, 0))
    vec = lambda w: pl.BlockSpec((1, w), lambda c: (0, 0))
    return pl.pallas_call(
        body, name="ssd_bwd", grid=(NC,),
        in_specs=[rows(INNER), rows(cfg.CONVCH), rows(INNER), rows(INNER), rows(LANE), vec(INNER),
                  pl.BlockSpec((1, NPAIR, NST, LANE), lambda c: (NC - 1 - c, 0, 0, 0)), rows(LANE), vec(LANE), vec(LANE),
                  pl.BlockSpec((LANE, INNER), lambda c: (0, 0))],
        out_specs=[rows(cfg.CONVCH), rows(LANE), vec(LANE), vec(LANE), vec(LANE)],
        out_shape=[jax.ShapeDtypeStruct((S, cfg.CONVCH), F32), jax.ShapeDtypeStruct((S, LANE), F32)]
        + [jax.ShapeDtypeStruct((1, LANE), F32)] * 3,
        scratch_shapes=[pltpu.VMEM((NPAIR, NST, LANE), F32), pltpu.VMEM((T, LANE), F32), pltpu.VMEM((LANE, T), F32),
                        pltpu.VMEM((T, INNER), F32), pltpu.VMEM((T, INNER), F32)],
        compiler_params=_params(("arbitrary",)),
    )(dy, xc, dt_exp, a_exp, a_small, dskip_exp, hin, dt_raw, dt_bias_pad, a_log_pad, expand)


def _ssd_post(cfg, y, z, norm_g):
    W = cfg.INNER // cfg.G

    def fn(y, z, g):
        yz = y * z * _sigmoid(z)
        return jnp.concatenate([yz[:, i * W:(i + 1) * W] * _rs(yz[:, i * W:(i + 1) * W]) for i in range(cfg.G)], axis=1) * g

    return _rowwise("ssd_post", fn, [y, z], [norm_g], [(cfg.INNER, BF16)], [], _pick(cfg.S, 256, 8))[0]


def _ssd_post_bwd(cfg, db, y, z, norm_g):
    W = cfg.INNER // cfg.G

    def fn(db, y, z, g):
        sg = _sigmoid(z)
        yz = y * z * sg
        dn = db * g
        dyz, nh = [], []
        for i in range(cfg.G):
            seg = yz[:, i * W:(i + 1) * W]
            r = _rs(seg)
            nh.append(seg * r)
            dyz.append(_rms_back(nh[-1], r, dn[:, i * W:(i + 1) * W]))
        dyz = jnp.concatenate(dyz, axis=1)
        return dyz * z * sg, dyz * y * sg * (1.0 + z * (1.0 - sg)), _colsum(db * jnp.concatenate(nh, axis=1))

    return _rowwise("ssd_post_bwd", fn, [db, y, z], [norm_g], [(cfg.INNER, F32), (cfg.INNER, F32)], [(1, cfg.INNER)],
                    _pick(cfg.S, 256, 8))


def _local_grads(cfg, x, tgt, W, sp, out_weight=None, ffn_weights=None, early_grads_ready=None, in_grad_ready=None):
    S, D, H, INNER = cfg.S, cfg.D, cfg.H, cfg.INNER
    ts = _pick(S, 256, 8)
    tc = 256

    xn = _rowwise("rms_pre", lambda x, g: x * _rs(x) * g, [x], [sp["mix_pre_g"]], [(D, BF16)], [], ts)[0]
    u = _matmul("mm_in", xn, W["w_in"], "nt", F32)
    c_q, c_kv = u[:, :cfg.QL], u[:, cfg.QL:cfg.o_kr]
    kr = u[:, cfg.o_kr:cfg.o_z]
    z = u[:, cfg.o_z:cfg.o_xbc]
    xbc = u[:, cfg.o_xbc:cfg.o_dt]
    dt_raw = u[:, cfg.o_dt:]

    cqn = _rowwise("rms_q", lambda x, g: x * _rs(x) * g, [c_q], [sp["q_norm_g"]], [(cfg.QL, BF16)], [], ts)[0]
    ckvn = _rowwise("rms_kv", lambda x, g: x * _rs(x) * g, [c_kv], [sp["kv_norm_g"]], [(cfg.KVL, BF16)], [], ts)[0]
    q = _matmul("mm_uq", cqn, W["w_uq"], "nn", F32)
    kv = _matmul("mm_ukv", ckvn, W["w_ukv"], "nn", F32)
    cos2, sin2 = _rope_tables(S)
    Qh, Kh, Vh = _mla_pack(cfg, q, kv, kr, cos2, sin2)
    a_out, lse, lse_t = _attn_fwd(cfg, Qh, Kh, Vh)
    if out_weight is not None:
        sp = dict(sp, ssm_conv_b=sp["ssm_conv_b"] + out_weight.pass_on(a_out)[0, 0])

    pad = lambda v: jnp.pad(v, ((0, 0), (0, LANE - v.shape[1])))
    expand = _expand_matrix(cfg)
    dt_bias_pad, a_log_pad = pad(sp["dt_bias"]), pad(sp["a_log"])
    dskip_exp = jnp.repeat(sp["d_skip"], HP, axis=1)
    xc = _colwise("ssm_act", _ssm_act, [xbc], [sp["ssm_conv_w"], sp["ssm_conv_b"]], [F32], [], tc)[0]
    dt_s, a_s, dt_exp, a_exp = _ssd_prep(cfg, dt_raw, dt_bias_pad, a_log_pad, expand)
    y_ssd, hin = _ssd_fwd(cfg, xc, dt_exp, a_exp, a_s, dskip_exp)
    b_out = _ssd_post(cfg, y_ssd, z, sp["ssm_norm_g"])

    ab_out = jnp.concatenate([a_out.astype(BF16), b_out], axis=1)
    if out_weight is not None:
        W = dict(W, **out_weight.arrived(ab_out))
    if ffn_weights is not None:
        sp = dict(sp, mix_post_g=sp["mix_post_g"] + ffn_weights.pass_on(ab_out)[0, 0])
    mix = _matmul("mm_out", ab_out, W["w_out"], "nn", F32)

    def mid(x, mix, g_mp, g_fp):
        x1 = x + mix * _rs(mix) * g_mp
        return x1, x1 * _rs(x1) * g_fp

    x1, h2 = _rowwise("fwd_mid", mid, [x, mix], [sp["mix_post_g"], sp["ffn_pre_g"]], [(D, F32), (D, BF16)], [], ts)
    if ffn_weights is not None:
        W = dict(W, **ffn_weights.arrived(h2))
    gate_pre = _matmul("mm_gate", h2, W["w_gate"], "nn", F32, chips=True)
    up = _matmul("mm_up", h2, W["w_up"], "nn", F32, chips=True)
    act = _colwise("ffn_act", _ffn_act, [gate_pre, up], [sp["ffn_conv_w"], sp["ffn_conv_b"]], [BF16], [], tc)[0]
    f = _matmul("mm_down", act, W["w_down"], "nn", F32)

    def final(x1, f, t, g):
        r = _rs(f)
        fh = f * r
        err = x1 + fh * g - t
        loss = 0.5 * jnp.sum(jnp.mean(err * err, axis=-1, keepdims=True), axis=0, keepdims=True)
        dy = err * (1.0 / D)
        return dy, _rms_back(fh, r, dy * g), _colsum(dy * fh), loss

    dy, df, g_ffn_post, loss = _rowwise("final", final, [x1, f, tgt], [sp["ffn_post_g"]], [(D, F32), (D, BF16)],
                                        [(1, D), (1, LANE)], ts)
    gW = {}
    dact = _matmul("mm_down_dx", df, W["w_down"], "nt", F32)
    gW["w_down"] = _matmul("mm_down_dw", act, df, "tn", BF16)
    dgate, dup, g_ffn_conv_w, g_ffn_conv_b = _colwise(
        "ffn_act_bwd", _ffn_act_back, [dact, gate_pre, up], [sp["ffn_conv_w"], sp["ffn_conv_b"]], [BF16, BF16], [FFN_K, 1], tc)
    dh2 = _matmul("mm_gu_dx", dgate, W["w_gate"], "nt", F32, dup, W["w_up"], chips=True)
    gW["w_gate"] = _matmul("mm_gate_dw", h2, dgate, "tn", BF16, chips=True)
    gW["w_up"] = _matmul("mm_up_dw", h2, dup, "tn", BF16, chips=True)

    def mid_back(dy, dh2, x1, mix, g_mp, g_fp):
        r2 = _rs(x1)
        xh = x1 * r2
        dx1 = dy + _rms_back(xh, r2, dh2 * g_fp)
        r1 = _rs(mix)
        mh = mix * r1
        return dx1, _rms_back(mh, r1, dx1 * g_mp), _colsum(dh2 * xh), _colsum(dx1 * mh)

    dx1, dmix, g_ffn_pre, g_mix_post = _rowwise("bwd_mid", mid_back, [dy, dh2, x1, mix], [sp["mix_post_g"], sp["ffn_pre_g"]],
                                                [(D, F32), (D, BF16)], [(1, D), (1, D)], ts)
    dab_out = _matmul("mm_out_dx", dmix, W["w_out"], "nt", F32)
    db_out = dab_out[:, cfg.MLAW:]
    gW["w_out"] = _matmul("mm_out_dw", ab_out, dmix, "tn", BF16)
    if early_grads_ready is not None:
        token = early_grads_ready({n: gW[n] for n in ("w_down", "w_gate", "w_up", "w_out")})
        sp = dict(sp, ssm_norm_g=sp["ssm_norm_g"] + token[0, 0])

    dy_ssd, dz, g_ssm_norm = _ssd_post_bwd(cfg, db_out, y_ssd, z, sp["ssm_norm_g"])
    dxc, ddt_raw, g_dt_bias, g_a_log, g_d_skip = _ssd_bwd(cfg, dy_ssd, xc, dt_exp, a_exp, a_s, dskip_exp, hin, dt_raw,
                                                          dt_bias_pad, a_log_pad, expand)
    dxbc, g_ssm_conv_w, g_ssm_conv_b = _colwise("ssm_act_bwd", _ssm_act_back, [dxc, xbc], [sp["ssm_conv_w"], sp["ssm_conv_b"]],
                                                [BF16], [SSM_K, 1], tc)

    dQ, delta_t = _attn_dq(cfg, Qh, Kh, Vh, dab_out, a_out, lse)
    dK, dV = _attn_dkv(cfg, Qh, Kh, Vh, dab_out, lse_t, delta_t)
    dq, dkv, dkr = _mla_unpack(cfg, dQ, dK, dV, cos2, sin2)
    dcqn = _matmul("mm_uq_dx", dq, W["w_uq"], "nt", F32)
    dckvn = _matmul("mm_ukv_dx", dkv, W["w_ukv"], "nt", F32)
    gW["w_uq"] = _matmul("mm_uq_dw", cqn, dq, "tn", BF16)
    gW["w_ukv"] = _matmul("mm_ukv_dw", ckvn, dkv, "tn", BF16)

    def rms_back(x, dy, g):
        r = _rs(x)
        xh = x * r
        return _rms_back(xh, r, dy * g), _colsum(dy * xh)

    dc_q, g_q_norm = _rowwise("rms_q_bwd", rms_back, [c_q, dcqn], [sp["q_norm_g"]], [(cfg.QL, BF16)], [(1, cfg.QL)], ts)
    dc_kv, g_kv_norm = _rowwise("rms_kv_bwd", rms_back, [c_kv, dckvn], [sp["kv_norm_g"]], [(cfg.KVL, BF16)], [(1, cfg.KVL)], ts)

    du = jnp.concatenate([dc_q, dc_kv, dkr, dz.astype(BF16), dxbc, ddt_raw.astype(BF16)], axis=1)
    gW["w_in"] = _matmul("mm_in_dw", du, xn, "tn", BF16)
    if in_grad_ready is not None:
        token = in_grad_ready({n: gW[n] for n in ("w_in", "w_uq", "w_ukv")})
        sp = dict(sp, mix_pre_g=sp["mix_pre_g"] + token[0, 0])
    dxn = _matmul("mm_in_dx", du, W["w_in"], "nn", F32)

    def first_back(dx1, dxn, x, g):
        r = _rs(x)
        xh = x * r
        return dx1 + _rms_back(xh, r, dxn * g), _colsum(dxn * xh)

    grad_x, g_mix_pre = _rowwise("bwd_first", first_back, [dx1, dxn, x], [sp["mix_pre_g"]], [(D, F32)], [(1, D)], ts)

    gs = dict(mix_pre_g=g_mix_pre, q_norm_g=g_q_norm, kv_norm_g=g_kv_norm, ssm_conv_w=g_ssm_conv_w, ssm_conv_b=g_ssm_conv_b,
              dt_bias=g_dt_bias[:, :cfg.HS], a_log=g_a_log[:, :cfg.HS], d_skip=g_d_skip[:, :cfg.HS], ssm_norm_g=g_ssm_norm,
              mix_post_g=g_mix_post, ffn_pre_g=g_ffn_pre, ffn_conv_w=g_ffn_conv_w, ffn_conv_b=g_ffn_conv_b,
              ffn_post_g=g_ffn_post)
    return loss, grad_x, gW, gs


def _to_kernel_layout(cfg, name, w):
    if name == "w_in":
        a = cfg.o_kr + ROPE
        return jnp.concatenate([w[:a], jnp.zeros((LANE - ROPE, w.shape[1]), w.dtype), w[a:],
                                jnp.zeros((LANE - cfg.HS, w.shape[1]), w.dtype)], axis=0)
    if name in ("w_uq", "w_ukv"):
        per = NOPE + (ROPE if name == "w_uq" else VH)
        return jnp.concatenate([w[:, h * per:h * per + NOPE] for h in range(cfg.H)]
                               + [w[:, h * per + NOPE:(h + 1) * per] for h in range(cfg.H)], axis=1)
    return w


def _from_kernel_layout(cfg, name, g):
    if name == "w_in":
        return jnp.concatenate([g[:cfg.o_kr + ROPE], g[cfg.o_z:cfg.o_dt + cfg.HS]], axis=0)
    if name in ("w_uq", "w_ukv"):
        second = ROPE if name == "w_uq" else VH
        base = cfg.H * NOPE
        parts = []
        for h in range(cfg.H):
            parts += [g[:, h * NOPE:(h + 1) * NOPE], g[:, base + h * second:base + (h + 1) * second]]
        return jnp.concatenate(parts, axis=1)
    return g


def _cols_to_chips(w):
    r, c = w.shape
    return w.reshape(r, N_CHIPS, c // N_CHIPS).transpose(1, 0, 2)


def _chips_to_cols(g):
    k, r, cs = g.shape
    return g.transpose(1, 0, 2).reshape(r, k * cs)


_CHIP_MAJOR = ("w_gate", "w_up")
_RELAYOUT = ("w_uq", "w_ukv")
_LAYOUT_ROWS = 256


def _w_in_layout(cfg, wg):
    _, rs, d = wg.shape
    tc = _pick(d, _LAYOUT_ROWS, LANE)

    def body(w_ref, o_ref):
        o_ref[...] = _to_kernel_layout(cfg, "w_in", jnp.concatenate([w_ref[k] for k in range(N_CHIPS)], axis=0))

    return pl.pallas_call(
        body, name="layout_w_in", grid=(d // tc,),
        in_specs=[pl.BlockSpec((N_CHIPS, rs, tc), lambda j: (0, 0, j))], out_specs=pl.BlockSpec((cfg.EXT, tc), lambda j: (0, j)),
        out_shape=jax.ShapeDtypeStruct((cfg.EXT, d), wg.dtype), compiler_params=_params(("parallel",)),
    )(wg)


def _w_in_grad_to_chips(cfg, g):
    _, d = g.shape
    rs = cfg.IN_COLS // N_CHIPS
    tc = _pick(d, _LAYOUT_ROWS, LANE)

    def body(g_ref, o_ref):
        nat = _from_kernel_layout(cfg, "w_in", g_ref[...])
        for k in range(N_CHIPS):
            o_ref[k] = nat[k * rs:(k + 1) * rs]

    return pl.pallas_call(
        body, name="layout_grad_w_in", grid=(d // tc,),
        in_specs=[pl.BlockSpec((cfg.EXT, tc), lambda j: (0, j))], out_specs=pl.BlockSpec((N_CHIPS, rs, tc), lambda j: (0, 0, j)),
        out_shape=jax.ShapeDtypeStruct((N_CHIPS, rs, d), g.dtype), compiler_params=_params(("parallel",)),
    )(g)


def _gathered_to_kernel(cfg, name, wg):
    if name in _CHIP_MAJOR:
        return wg
    if name == "w_in":
        return _w_in_layout(cfg, wg)
    if name not in _RELAYOUT:
        return wg.reshape(wg.shape[0] * wg.shape[1], wg.shape[2])
    _, rows, cs = wg.shape
    tr = _pick(rows, _LAYOUT_ROWS, 16)

    def body(w_ref, o_ref):
        o_ref[...] = _to_kernel_layout(cfg, name, jnp.concatenate([w_ref[k] for k in range(N_CHIPS)], axis=1))

    wide = jax.eval_shape(lambda w: _to_kernel_layout(cfg, name, w), jax.ShapeDtypeStruct((rows, N_CHIPS * cs), wg.dtype)).shape[1]
    return pl.pallas_call(
        body, name="layout_" + name, grid=(rows // tr,),
        in_specs=[pl.BlockSpec((N_CHIPS, tr, cs), lambda i: (0, i, 0))], out_specs=pl.BlockSpec((tr, wide), lambda i: (i, 0)),
        out_shape=jax.ShapeDtypeStruct((rows, wide), wg.dtype), compiler_params=_params(("parallel",)),
    )(wg)


def _grad_to_chips(cfg, name, g):
    if name in _CHIP_MAJOR:
        return g
    if name == "w_in":
        return _w_in_grad_to_chips(cfg, g)
    if name not in _RELAYOUT:
        return g.reshape(N_CHIPS, g.shape[0] // N_CHIPS, g.shape[1])
    rows, wide = g.shape
    tr = _pick(rows, _LAYOUT_ROWS, 16)
    cs = jax.eval_shape(lambda v: _from_kernel_layout(cfg, name, v), g).shape[1] // N_CHIPS

    def body(g_ref, o_ref):
        nat = _from_kernel_layout(cfg, name, g_ref[...])
        for k in range(N_CHIPS):
            o_ref[k] = nat[:, k * cs:(k + 1) * cs]

    return pl.pallas_call(
        body, name="layout_grad_" + name, grid=(rows // tr,),
        in_specs=[pl.BlockSpec((tr, wide), lambda i: (i, 0))], out_specs=pl.BlockSpec((N_CHIPS, tr, cs), lambda i: (0, i, 0)),
        out_shape=jax.ShapeDtypeStruct((N_CHIPS, rows, cs), g.dtype), compiler_params=_params(("parallel",)),
    )(g)


def _me():
    return lax.axis_index("x"), lax.axis_index("y"), lax.axis_index("c")


def _other_chips(x, y):
    return [(1 - x, y), (x, 1 - y), (1 - x, 1 - y)]


_ANY = pl.BlockSpec(memory_space=pl.ANY)


def _row_block(rows, cols, mult):
    return _pick(rows, max(mult, (1 << 19) // cols // mult * mult), mult)


def _scalar(v):
    return v.astype(I32).reshape(1)


def _blocks2d(r, c, mult):
    if r % mult == 0:
        tr = _row_block(r, c, mult)
        return (tr, c), r // tr, lambda i: (i, 0)
    tc = _pick(c, max(LANE, (1 << 19) // r // LANE * LANE), LANE)
    return (r, tc), c // tc, lambda i: (0, i)


def _by_rows(rows):
    return rows % 32 == 0


def _half_shape(rows, cols):
    return (rows // 2, cols) if _by_rows(rows) else (rows, cols // 2)


def _half_blocks(rows, cols, mult):
    hr, hc = _half_shape(rows, cols)
    block, n, part = _blocks2d(hr, hc, mult)
    assert (hr % mult == 0) == _by_rows(rows), (rows, cols, mult)
    full = (lambda h, i: (h * n + i, 0)) if _by_rows(rows) else (lambda h, i: (0, h * n + i))
    return block, n, full, part


def _half(ref, k, half):
    hr, hc = _half_shape(ref.shape[1], ref.shape[2])
    if _by_rows(ref.shape[1]):
        return ref.at[k, pl.ds(pl.multiple_of(half * hr, 16), hr), :]
    return ref.at[k, :, pl.ds(pl.multiple_of(half * hc, LANE), hc)]


def _shard_blocks(w, br, bc):
    if w.shape[0] == 1:
        def write(ref, v):
            ref[...] = v
        return (lambda f: pl.BlockSpec((None, br, bc), lambda *a: (0, *f(*a)))), (lambda ref: ref[...]), write
    assert w.shape[1] == 1 and br == w.shape[0], w.shape

    def write_rows(ref, v):
        ref[:, 0, :] = v
    return (lambda f: pl.BlockSpec((br, 1, bc), lambda *a: (0, 0, f(*a)[1]))), (lambda ref: ref[:, 0, :]), write_rows


def _stage_shard(name, w, chip):
    rs, cs = w.shape[0] * w.shape[1], w.shape[2]
    (br, bc), n, idx = _blocks2d(rs, cs, 16)
    spec, get, _ = _shard_blocks(w, br, bc)

    def body(chip_ref, w_ref, o_ref):
        o_ref[...] = get(w_ref).astype(BF16)

    return pl.pallas_call(
        body, name="stage_" + name,
        grid_spec=pltpu.PrefetchScalarGridSpec(
            num_scalar_prefetch=1, grid=(n,),
            in_specs=[spec(lambda i, chip_ref: idx(i))],
            out_specs=pl.BlockSpec((None, br, bc), lambda i, chip_ref: (chip_ref[0], *idx(i)))),
        out_shape=jax.ShapeDtypeStruct((N_CHIPS, rs, cs), BF16),
        compiler_params=_params(("parallel",)),
    )(_scalar(chip), w)


def _allgather_weights(bufs):
    n = len(bufs)

    def body(*refs):
        outs, send_sems, recv_sems = refs[n:2 * n], refs[2 * n], refs[2 * n + 1]
        x, y, c = _me()
        chip = 2 * x + y
        sib = (x, y, 1 - c)
        chips = _other_chips(x, y)

        def copy(k, part, to):
            return pltpu.make_async_remote_copy(src_ref=part, dst_ref=part, send_sem=send_sems.at[k], recv_sem=recv_sems.at[k],
                                                device_id=to, device_id_type=MESH_ID)

        started = []
        for w, o_ref in enumerate(outs):
            for j, (cx, cy) in enumerate(chips):
                started.append(copy(6 * w + j, _half(o_ref, chip, c), (cx, cy, c)))
                started[-1].start()
        for w, o_ref in enumerate(outs):
            for j, (cx, cy) in enumerate(chips):
                theirs = _half(o_ref, 2 * cx + cy, c)
                copy(6 * w + j, theirs, sib).wait_recv()
                started.append(copy(6 * w + 3 + j, theirs, sib))
                started[-1].start()
        for w, o_ref in enumerate(outs):
            for j, (cx, cy) in enumerate(chips):
                copy(6 * w + 3 + j, _half(o_ref, 2 * cx + cy, 1 - c), sib).wait_recv()
        for cp in started:
            cp.wait_send()

    return pl.pallas_call(
        body, name="allgather_weights", in_specs=[_ANY] * n, out_specs=[_ANY] * n,
        out_shape=[jax.ShapeDtypeStruct(b.shape, b.dtype) for b in bufs],
        input_output_aliases={i: i for i in range(n)},
        scratch_shapes=[pltpu.SemaphoreType.DMA((6 * n,)), pltpu.SemaphoreType.DMA((6 * n,))],
    )(*bufs)


_HBM = pl.BlockSpec(memory_space=pltpu.HBM)
_SEM = pl.BlockSpec(memory_space=pltpu.SEMAPHORE)
_EFFECT = pltpu.SideEffectType.DATAFLOW_SIDE_EFFECTING


def _split_start(name, bufs, n_copies, copies, after):
    n = len(bufs)

    def body(*refs):
        for cp in copies(refs[:n], refs[n + 1], refs[n + 2]):
            cp.start()
        refs[-1][...] = jnp.zeros_like(refs[-1])

    res = pl.pallas_call(
        body, name=name,
        out_shape=(pltpu.SemaphoreType.DMA((n_copies,)), pltpu.SemaphoreType.DMA((n_copies,)),
                   *[pltpu.HBM(b.shape, b.dtype) for b in bufs], jax.ShapeDtypeStruct((8, LANE), F32)),
        in_specs=[_HBM] * n + [_ANY], out_specs=(_SEM, _SEM, *[_HBM] * n, pl.BlockSpec(memory_space=pltpu.VMEM)),
        input_output_aliases={i: 2 + i for i in range(n)},
        compiler_params=pltpu.CompilerParams(has_side_effects=_EFFECT),
    )(*[pltpu.with_memory_space_constraint(b, pltpu.HBM) for b in bufs], after)
    return res[0], res[1], list(res[2:2 + n]), res[-1]


def _split_wait(name, send_sems, recv_sems, bufs, after, copies):
    n = len(bufs)

    def body(*refs):
        for cp in copies(refs[:n], refs[n], refs[n + 1]):
            cp.wait_send()
            cp.wait_recv()

    return list(pl.pallas_call(
        body, name=name, out_shape=[pltpu.HBM(b.shape, b.dtype) for b in bufs],
        in_specs=[_HBM] * n + [_SEM, _SEM, _ANY], out_specs=[_HBM] * n,
        input_output_aliases={i: i for i in range(n)},
        compiler_params=pltpu.CompilerParams(has_side_effects=_EFFECT),
    )(*bufs, send_sems, recv_sems, after))


def _gather_to_chips(bufs, send_sems, recv_sems):
    x, y, c = _me()
    return [pltpu.make_async_remote_copy(src_ref=_half(b, 2 * x + y, c), dst_ref=_half(b, 2 * x + y, c),
                                         send_sem=send_sems.at[3 * w + j], recv_sem=recv_sems.at[3 * w + j],
                                         device_id=(cx, cy, c), device_id_type=MESH_ID)
            for w, b in enumerate(bufs) for j, (cx, cy) in enumerate(_other_chips(x, y))]


def _gather_to_sibling(bufs, send_sems, recv_sems):
    x, y, c = _me()
    return [pltpu.make_async_remote_copy(src_ref=_half(b, 2 * cx + cy, c), dst_ref=_half(b, 2 * cx + cy, c),
                                         send_sem=send_sems.at[3 * w + j], recv_sem=recv_sems.at[3 * w + j],
                                         device_id=(x, y, 1 - c), device_id_type=MESH_ID)
            for w, b in enumerate(bufs) for j, (cx, cy) in enumerate(_other_chips(x, y))]


def _pair_exchange(name, grads):
    n = len(grads)

    def body(*refs):
        ins, outs, send_sems, recv_sems = refs[:n], refs[n:2 * n], refs[2 * n], refs[2 * n + 1]
        x, y, c = _me()
        cps = []
        for w, (g_ref, o_ref) in enumerate(zip(ins, outs)):
            cps.append(pltpu.make_async_remote_copy(src_ref=_half(g_ref, slice(None), 1 - c), dst_ref=o_ref,
                                                    send_sem=send_sems.at[w], recv_sem=recv_sems.at[w],
                                                    device_id=(x, y, 1 - c), device_id_type=MESH_ID))
            cps[-1].start()
        for cp in cps:
            cp.wait()

    return pl.pallas_call(
        body, name="pair_exchange_" + name, in_specs=[_ANY] * n, out_specs=[_ANY] * n,
        out_shape=[jax.ShapeDtypeStruct((g.shape[0], *_half_shape(g.shape[1], g.shape[2])), g.dtype) for g in grads],
        scratch_shapes=[pltpu.SemaphoreType.DMA((n,)), pltpu.SemaphoreType.DMA((n,))],
    )(*grads)


def _pair_sum(name, g, theirs, c):
    (br, bc), nb, full, part = _half_blocks(g.shape[1], g.shape[2], 16)

    def body(c_ref, a_ref, b_ref, o_ref):
        o_ref[...] = (a_ref[...].astype(F32) + b_ref[...].astype(F32)).astype(o_ref.dtype)

    return pl.pallas_call(
        body, name="pair_sum_" + name,
        grid_spec=pltpu.PrefetchScalarGridSpec(
            num_scalar_prefetch=1, grid=(N_CHIPS, nb),
            in_specs=[pl.BlockSpec((None, br, bc), lambda k, i, c_ref: (k, *full(c_ref[0], i))),
                      pl.BlockSpec((None, br, bc), lambda k, i, c_ref: (k, *part(i)))],
            out_specs=pl.BlockSpec((None, br, bc), lambda k, i, c_ref: (k, *part(i)))),
        out_shape=jax.ShapeDtypeStruct(theirs.shape, BF16),
        compiler_params=_params(("parallel", "parallel")),
    )(_scalar(c), g, theirs)


def _chip_copies(srcs, lands, send_sems, recv_sems):
    x, y, c = _me()
    return [pltpu.make_async_remote_copy(src_ref=s_ref.at[2 * cx + cy], dst_ref=l_ref.at[j], send_sem=send_sems.at[3 * w + j],
                                         recv_sem=recv_sems.at[3 * w + j], device_id=(cx, cy, c), device_id_type=MESH_ID)
            for w, (s_ref, l_ref) in enumerate(zip(srcs, lands)) for j, (cx, cy) in enumerate(_other_chips(x, y))]


def _chip_exchange_start(name, sums):
    n = len(sums)
    lands = [lax.empty((3,) + s.shape[1:], s.dtype) for s in sums]
    send_sems, recv_sems, bufs, token = _split_start(
        "chip_exchange_start_" + name, [*sums, *lands], 3 * n, lambda refs, ss, rs: _chip_copies(refs[:n], refs[n:], ss, rs),
        jnp.zeros((8, LANE), F32))
    return send_sems, recv_sems, bufs[:n], bufs[n:], token


def _chip_exchange_wait(name, send_sems, recv_sems, sums, lands, after):
    n = len(sums)
    bufs = _split_wait("chip_exchange_wait_" + name, send_sems, recv_sems, [*sums, *lands], after,
                       lambda refs, ss, rs: _chip_copies(refs[:n], refs[n:], ss, rs))
    return bufs[:n], bufs[n:]


def _chip_sum(name, sums, theirs, chip):
    _, h, cs = sums.shape
    (br, bc), nb, idx = _blocks2d(h, cs, 16)

    def body(chip_ref, s_ref, t_ref, o_ref):
        acc = s_ref[...].astype(F32)
        for k in range(3):
            acc = acc + t_ref[k].astype(F32)
        o_ref[...] = acc

    return pl.pallas_call(
        body, name="chip_sum_" + name,
        grid_spec=pltpu.PrefetchScalarGridSpec(
            num_scalar_prefetch=1, grid=(nb,),
            in_specs=[pl.BlockSpec((None, br, bc), lambda i, chip_ref: (chip_ref[0], *idx(i))),
                      pl.BlockSpec((3, br, bc), lambda i, chip_ref: (0, *idx(i)))],
            out_specs=pl.BlockSpec((br, bc), lambda i, chip_ref: idx(i))),
        out_shape=jax.ShapeDtypeStruct((h, cs), F32),
        compiler_params=_params(("parallel",)),
    )(_scalar(chip), sums, theirs)


def _sibling_exchange(halves):
    n = len(halves)

    def body(*refs):
        ins, outs, send_sems, recv_sems = refs[:n], refs[n:2 * n], refs[2 * n], refs[2 * n + 1]
        x, y, c = _me()
        cps = []
        for w, (h_ref, o_ref) in enumerate(zip(ins, outs)):
            cps.append(pltpu.make_async_remote_copy(src_ref=h_ref, dst_ref=o_ref, send_sem=send_sems.at[w], recv_sem=recv_sems.at[w],
                                                    device_id=(x, y, 1 - c), device_id_type=MESH_ID))
            cps[-1].start()
        for cp in cps:
            cp.wait()

    return pl.pallas_call(
        body, name="grad_sibling_exchange", in_specs=[_ANY] * n, out_specs=[_ANY] * n,
        out_shape=[jax.ShapeDtypeStruct(h.shape, h.dtype) for h in halves],
        scratch_shapes=[pltpu.SemaphoreType.DMA((n,)), pltpu.SemaphoreType.DMA((n,))],
    )(*halves)


def _allreduce_small(name, vec):
    def body(v_ref, o_ref, buf_ref, send_sems, recv_sems):
        x, y, c = _me()
        me = 4 * x + 2 * y + c
        cps = []
        for p in range(1, 8):
            px, py, pc = x ^ (p >> 2), y ^ ((p >> 1) & 1), c ^ (p & 1)
            cps.append(pltpu.make_async_remote_copy(src_ref=v_ref, dst_ref=buf_ref.at[me], send_sem=send_sems.at[p - 1],
                                                    recv_sem=recv_sems.at[p - 1], device_id=(px, py, pc), device_id_type=MESH_ID))
            cps[-1].start()
        buf_ref[me] = v_ref[...]
        for p in range(1, 8):
            theirs = buf_ref.at[me ^ p]
            pltpu.make_async_remote_copy(src_ref=theirs, dst_ref=theirs, send_sem=send_sems.at[p - 1], recv_sem=recv_sems.at[p - 1],
                                         device_id=(x, y, c), device_id_type=MESH_ID).wait_recv()
        for cp in cps:
            cp.wait_send()
        acc = buf_ref[0]
        for k in range(1, 8):
            acc = acc + buf_ref[k]
        o_ref[...] = acc

    vm = pl.BlockSpec(memory_space=pltpu.VMEM)
    return pl.pallas_call(
        body, name=name, in_specs=[vm], out_specs=vm, out_shape=jax.ShapeDtypeStruct(vec.shape, F32),
        scratch_shapes=[pltpu.VMEM((8,) + vec.shape, F32), pltpu.SemaphoreType.DMA((7,)), pltpu.SemaphoreType.DMA((7,))],
    )(vec)


def _adam_math(w, g, m, v):
    m = ADAM_B1 * m + (1.0 - ADAM_B1) * g
    v = ADAM_B2 * v + (1.0 - ADAM_B2) * (g * g)
    m_hat = m / (1.0 - ADAM_B1 ** ADAM_STEP)
    v_hat = v / (1.0 - ADAM_B2 ** ADAM_STEP)
    return -ADAM_LR * (m_hat / (jnp.sqrt(v_hat) + ADAM_EPS) + ADAM_WD * w), m, v


def _adamw(name, w, g, m, v):
    R, C = w.shape
    tr = _row_block(R, C, 8)

    def body(w_ref, g_ref, m_ref, v_ref, d_ref, nm_ref, nv_ref):
        d_ref[...], nm_ref[...], nv_ref[...] = _adam_math(w_ref[...], g_ref[...], m_ref[...], v_ref[...])

    blk = pl.BlockSpec((tr, C), lambda i: (i, 0))
    return pl.pallas_call(
        body, name=name, grid=(R // tr,), in_specs=[blk] * 4, out_specs=[blk] * 3,
        out_shape=[jax.ShapeDtypeStruct((R, C), F32)] * 3, compiler_params=_params(("parallel",)),
    )(w, g, m, v)


def _adamw_halves(name, w, mine, theirs, m, v, c):
    rs, cs = w.shape[0] * w.shape[1], w.shape[2]
    (br, bc), nb, whole, half = _half_blocks(rs, cs, 8)
    spec, get, put = _shard_blocks(w, br, bc)

    def body(c_ref, w_ref, a_ref, b_ref, m_ref, v_ref, g_ref, d_ref, nm_ref, nv_ref):
        g = jnp.where(pl.program_id(0) == c_ref[0], a_ref[...], b_ref[...])
        put(g_ref, g)
        for ref, val in zip((d_ref, nm_ref, nv_ref), _adam_math(get(w_ref), g, get(m_ref), get(v_ref))):
            put(ref, val)

    full = spec(lambda s, i, c_ref: whole(s, i))
    part = pl.BlockSpec((br, bc), lambda s, i, c_ref: half(i))
    return pl.pallas_call(
        body, name=name,
        grid_spec=pltpu.PrefetchScalarGridSpec(num_scalar_prefetch=1, grid=(2, nb), in_specs=[full, part, part, full, full],
                                               out_specs=[full] * 4),
        out_shape=[jax.ShapeDtypeStruct(w.shape, F32)] * 4, compiler_params=_params(("parallel", "parallel")),
    )(_scalar(c), w, mine, theirs, m, v)


def _pack_small(arrs):
    flat = jnp.concatenate([a.reshape(-1) for a in arrs])
    n = -(-flat.shape[0] // (8 * LANE)) * 8 * LANE
    return jnp.pad(flat, (0, n - flat.shape[0])).reshape(8, n // 8)


def _unpack_small(vec, shapes):
    flat, out, off = vec.reshape(-1), [], 0
    for s in shapes:
        out.append(flat[off:off + s[0] * s[1]].reshape(s))
        off += s[0] * s[1]
    return out


class _LateWeights:
    def __init__(self, cfg, tag, names, staged, after):
        self.cfg, self.tag, self.names, self.k = cfg, tag, names, 3 * len(names)
        self.send, self.recv, self.bufs, self.token = _split_start(f"gather_{tag}_chips_start", staged, self.k, _gather_to_chips,
                                                                    after)

    def pass_on(self, after):
        bufs = _split_wait(f"gather_{self.tag}_chips_wait", self.send, self.recv, self.bufs, after, _gather_to_chips)
        self.send, self.recv, self.bufs, token = _split_start(f"gather_{self.tag}_sibling_start", bufs, self.k, _gather_to_sibling,
                                                               self.token)
        return token

    def arrived(self, after):
        bufs = _split_wait(f"gather_{self.tag}_sibling_wait", self.send, self.recv, self.bufs, after, _gather_to_sibling)
        return {n: _gathered_to_kernel(self.cfg, n, b) for n, b in zip(self.names, bufs)}


def _step(cfg, a):
    chip = 2 * lax.axis_index("x") + lax.axis_index("y")
    core = lax.axis_index("c")
    big = BIG

    ffn = ("w_gate", "w_up", "w_down")
    first = ("w_in", "w_uq", "w_ukv")
    staged = {n: _stage_shard(n, a[n], chip) for n in big}
    gathered = _allgather_weights([staged[n] for n in first])
    W = {n: _gathered_to_kernel(cfg, n, wg) for n, wg in zip(first, gathered)}
    out_weight = _LateWeights(cfg, "out", ("w_out",), [staged["w_out"]], gathered[0][0, :8, :LANE])
    ffn_weights = _LateWeights(cfg, "ffn", ffn, [staged[n] for n in ffn], out_weight.token)

    sp = {n: a[n] for n in SMALL}
    sharded = _pack_small([a[n] for n in SMALL_SHARDED])
    slot = jnp.where(lax.broadcasted_iota(I32, (N_CHIPS,) + sharded.shape, 0) == chip, 0.5 * sharded[None], 0.0)
    allp = _allreduce_small("allgather_small", slot.reshape(N_CHIPS * 8, -1)).reshape((N_CHIPS,) + sharded.shape)
    per_chip = [_unpack_small(allp[ch], [a[n].shape for n in SMALL_SHARDED]) for ch in range(N_CHIPS)]
    for k, n in enumerate(SMALL_SHARDED):
        sp[n] = jnp.concatenate([per_chip[ch][k] for ch in range(N_CHIPS)], axis=1)
    sp["mix_pre_g"] = sp["mix_pre_g"] + (out_weight.token[0, 0] + ffn_weights.token[0, 0])

    started = {}

    def start_exchange(tag, names, grads):
        grads = [_grad_to_chips(cfg, n, grads[n]) for n in names]
        sums = [_pair_sum(n, g, t, core) for n, g, t in zip(names, grads, _pair_exchange(tag, grads))]
        started[tag] = (names, _chip_exchange_start(tag, sums))
        return started[tag][1][-1]

    early = ("w_down", "w_gate", "w_up", "w_out")
    loss, grad_x, gW, gs = _local_grads(cfg, a["x"], a["loss_target"], W, sp, out_weight, ffn_weights,
                                        functools.partial(start_exchange, "early", early),
                                        functools.partial(start_exchange, "rest", first))
    sums, landed, after = {}, {}, grad_x
    for tag in ("rest", "early"):
        names, (send_sems, recv_sems, s_bufs, l_bufs, _) = started[tag]
        s_bufs, l_bufs = _chip_exchange_wait(tag, send_sems, recv_sems, s_bufs, l_bufs, after)
        sums.update(zip(names, s_bufs))
        landed.update(zip(names, l_bufs))
        after = l_bufs[0]
    mine = [_chip_sum(n, sums[n], landed[n], chip) for n in big]
    theirs = _sibling_exchange(mine)

    shapes = [gs[n].shape for n in SMALL] + [(1, LANE)]
    red = _unpack_small(_allreduce_small("allreduce_small", _pack_small([gs[n] for n in SMALL] + [loss])), shapes)
    g_small = dict(zip(SMALL, red[:-1]))
    for n in SMALL_SHARDED:
        cs = a[n].shape[1]
        g_small[n] = lax.dynamic_slice_in_dim(g_small[n], chip * cs, cs, axis=1)

    out = {"loss": red[-1][0, 0], "grad_x": grad_x}
    for n, gm, gt in zip(big, mine, theirs):
        out["grad_" + n], out["delta_" + n], out["new_m_" + n], out["new_v_" + n] = _adamw_halves(
            "adamw_" + n, a[n], gm, gt, a["m_" + n], a["v_" + n], core)
    sshapes = [a[n].shape for n in SMALL]
    d, nm, nv = _adamw("adamw_small", _pack_small([a[n] for n in SMALL]), _pack_small([g_small[n] for n in SMALL]),
                       _pack_small([a["m_" + n] for n in SMALL]), _pack_small([a["v_" + n] for n in SMALL]))
    for n, dd, mm, vv in zip(SMALL, _unpack_small(d, sshapes), _unpack_small(nm, sshapes), _unpack_small(nv, sshapes)):
        out["grad_" + n], out["delta_" + n], out["new_m_" + n], out["new_v_" + n] = g_small[n], dd, mm, vv
    return out


def kernel(x, mix_pre_g, w_in, q_norm_g, w_uq, kv_norm_g, w_ukv, ssm_conv_w, ssm_conv_b, dt_bias, a_log, d_skip, ssm_norm_g, w_out, mix_post_g, ffn_pre_g, w_gate, w_up, ffn_conv_w, ffn_conv_b, w_down, ffn_post_g, loss_target, m_mix_pre_g, m_w_in, m_q_norm_g, m_w_uq, m_kv_norm_g, m_w_ukv, m_ssm_conv_w, m_ssm_conv_b, m_dt_bias, m_a_log, m_d_skip, m_ssm_norm_g, m_w_out, m_mix_post_g, m_ffn_pre_g, m_w_gate, m_w_up, m_ffn_conv_w, m_ffn_conv_b, m_w_down, m_ffn_post_g, v_mix_pre_g, v_w_in, v_q_norm_g, v_w_uq, v_kv_norm_g, v_w_ukv, v_ssm_conv_w, v_ssm_conv_b, v_dt_bias, v_a_log, v_d_skip, v_ssm_norm_g, v_w_out, v_mix_post_g, v_ffn_pre_g, v_w_gate, v_w_up, v_ffn_conv_w, v_ffn_conv_b, v_w_down, v_ffn_post_g):
    args = dict(locals())
    def given(k, v):
        if k in ("w_in", "m_w_in", "v_w_in"):
            return jnp.transpose(v, (2, 0, 1))
        return v if k.removeprefix("m_").removeprefix("v_") in BIG or v.ndim < 3 else v[0]

    out = _step(_FULL, {k: given(k, v) for k, v in args.items()})
    res = [out["loss"], out["grad_x"][None]]
    for pre in ("grad_", "delta_", "new_m_", "new_v_"):
        for n in WEIGHTS:
            o = out[pre + n]
            res.append(jnp.transpose(o, (1, 2, 0)) if n == "w_in" else o if n in BIG or args[n].ndim < 3 else o[None])
    return tuple(res)
```

```python
import functools
import math

import jax
import jax.numpy as jnp
from jax import lax
from jax.experimental import pallas as pl
from jax.experimental.pallas import tpu as pltpu

F32, BF16, I32 = jnp.float32, jnp.bfloat16, jnp.int32
NN = (((1,), (0,)), ((), ()))
NT = (((1,), (1,)), ((), ()))
TN = (((0,), (0,)), ((), ()))
HI = lax.Precision.HIGHEST
MESH_ID = pl.DeviceIdType.MESH

EPS = 1e-6
CHUNK = 64
NOPE, ROPE, VH = 128, 64, 128
ROPE_THETA = 10000.0
HP, NST = 64, 128
SSM_K, FFN_K = 4, 3
LANE = 128
N_CHIPS = 4
VMEM_LIMIT = 52 * 1024 * 1024
MM_TILE, MM_TILE_K = 1408, 2816

ADAM_LR, ADAM_B1, ADAM_B2, ADAM_EPS, ADAM_WD, ADAM_STEP = 0.001, 0.9, 0.999, 1e-08, 0.01, 10


class _Cfg:
    def __init__(self, S, D, QL, KVL, H, HS, G, DFF, T):
        self.S, self.D, self.QL, self.KVL, self.H, self.HS, self.G, self.DFF, self.T = S, D, QL, KVL, H, HS, G, DFF, T
        self.INNER = HS * HP
        self.CONVCH = self.INNER + 2 * G * NST
        self.QW = H * (NOPE + ROPE)
        self.KVW = H * (NOPE + VH)
        self.MLAW = H * VH
        self.MIXW = self.MLAW + self.INNER
        self.IN_COLS = QL + KVL + ROPE + self.INNER + self.CONVCH + HS
        self.o_kr = QL + KVL
        self.o_z = self.o_kr + LANE
        self.o_xbc = self.o_z + self.INNER
        self.o_dt = self.o_xbc + self.CONVCH
        self.EXT = self.o_dt + LANE
        self.NPAIR = HS // 2
        self.REP = HS // G


_FULL = _Cfg(S=2048, D=2048, QL=768, KVL=512, H=8, HS=16, G=2, DFF=5632, T=256)
BIG = ("w_in", "w_uq", "w_ukv", "w_out", "w_gate", "w_up", "w_down")

SMALL = ("mix_pre_g", "q_norm_g", "kv_norm_g", "ssm_conv_w", "ssm_conv_b", "dt_bias", "a_log", "d_skip", "ssm_norm_g",
         "mix_post_g", "ffn_pre_g", "ffn_conv_w", "ffn_conv_b", "ffn_post_g")
SMALL_SHARDED = ("ssm_conv_w", "ffn_conv_w")
WEIGHTS = ("mix_pre_g", "w_in", "q_norm_g", "w_uq", "kv_norm_g", "w_ukv", "ssm_conv_w", "ssm_conv_b", "dt_bias", "a_log",
           "d_skip", "ssm_norm_g", "w_out", "mix_post_g", "ffn_pre_g", "w_gate", "w_up", "ffn_conv_w", "ffn_conv_b",
           "w_down", "ffn_post_g")


def _pick(n, target, mult):
    best = None
    for d in range(mult, min(n, target) + 1, mult):
        if n % d == 0:
            best = d
    return best if best is not None else n


def _params(sem=None):
    kw = dict(vmem_limit_bytes=VMEM_LIMIT)
    if sem is not None:
        kw["dimension_semantics"] = sem
    return pltpu.CompilerParams(**kw)


def _dot(a, b, dims=NN, precision=None):
    return lax.dot_general(a, b, dims, preferred_element_type=F32, precision=precision)


def _sigmoid(x):
    return 1.0 / (1.0 + jnp.exp(-x))


def _rs(x):
    return lax.rsqrt(jnp.mean(x * x, axis=-1, keepdims=True) + EPS)


def _rms_back(xh, r, dn):
    return r * (dn - xh * jnp.mean(dn * xh, axis=-1, keepdims=True))


def _colsum(v):
    return jnp.sum(v, axis=0, keepdims=True)


def _matmul(name, a, b, mode, out_dtype, a2=None, b2=None, chips=False):
    cs = None
    if mode == "nn":
        (M, K), N = a.shape, b.shape[-1]
        if chips:
            cs, N = N, N_CHIPS * N
    elif mode == "nt":
        (M, K), N = a.shape, b.shape[-2]
        if chips:
            cs = b.shape[-1]
    else:
        (K, M), N = a.shape, b.shape[1]
        if chips:
            cs = N // N_CHIPS
    tm = _pick(M, MM_TILE, LANE)
    tn = _pick(cs if chips and mode != "nt" else N, MM_TILE, LANE)
    tk = _pick(cs, MM_TILE, LANE) if chips and mode == "nt" else _pick(K, MM_TILE_K, LANE)
    nk = K // tk
    dims = {"nn": NN, "nt": NT, "tn": TN}[mode]
    a_spec = pl.BlockSpec((tk, tm), lambda i, j, k: (k, i)) if mode == "tn" else pl.BlockSpec((tm, tk), lambda i, j, k: (i, k))
    b_spec = pl.BlockSpec((tn, tk), lambda i, j, k: (j, k)) if mode == "nt" else pl.BlockSpec((tk, tn), lambda i, j, k: (k, j))
    o_spec = pl.BlockSpec((tm, tn), lambda i, j, k: (i, j))
    o_shape = (M, N)
    if chips and mode == "nn":
        per = cs // tn
        b_spec = pl.BlockSpec((None, tk, tn), lambda i, j, k: (j // per, k, j % per))
    elif chips and mode == "nt":
        per = cs // tk
        b_spec = pl.BlockSpec((None, tn, tk), lambda i, j, k: (k // per, j, k % per))
    elif chips:
        per = cs // tn
        o_spec = pl.BlockSpec((None, tm, tn), lambda i, j, k: (j // per, i, j % per))
        o_shape = (N_CHIPS, M, cs)
    two = a2 is not None

    def product(refs):
        part = _dot(refs[0][...].astype(BF16), refs[1][...].astype(BF16), dims)
        if two:
            part += _dot(refs[2][...].astype(BF16), refs[3][...].astype(BF16), dims)
        return part

    def body_whole_k(*refs):
        refs[-1][...] = product(refs).astype(refs[-1].dtype)

    def body(*refs):
        o_ref, acc_ref = refs[-2], refs[-1]
        k = pl.program_id(2)

        @pl.when(k == 0)
        def _():
            acc_ref[...] = product(refs)

        @pl.when(k > 0)
        def _():
            acc_ref[...] += product(refs)

        @pl.when(k == nk - 1)
        def _():
            o_ref[...] = acc_ref[...].astype(o_ref.dtype)

    ins = (a, b, a2, b2) if two else (a, b)
    return pl.pallas_call(
        body_whole_k if nk == 1 else body, name=name, grid=(M // tm, N // tn, nk),
        in_specs=[a_spec, b_spec] * (2 if two else 1),
        out_specs=o_spec,
        out_shape=jax.ShapeDtypeStruct(o_shape, out_dtype),
        scratch_shapes=[] if nk == 1 else [pltpu.VMEM((tm, tn), F32)],
        compiler_params=_params(("parallel", "parallel", "arbitrary")),
    )(*ins)


def _rowwise(name, fn, rows, mats, outs, reds, ts):
    S = rows[0].shape[0]
    nr, nm, no = len(rows), len(mats), len(outs)

    def body(*refs):
        res = fn(*[r[...] for r in refs[:nr + nm]])
        res = res if isinstance(res, (tuple, list)) else (res,)
        for r, v in zip(refs[nr + nm:nr + nm + no], res[:no]):
            r[...] = v.astype(r.dtype)
        first = pl.program_id(0) == 0
        for r, v in zip(refs[nr + nm + no:], res[no:]):
            @pl.when(first)
            def _():
                r[...] = jnp.broadcast_to(v, r.shape)

            @pl.when(jnp.logical_not(first))
            def _():
                r[...] += jnp.broadcast_to(v, r.shape)

    in_specs = [pl.BlockSpec((ts, a.shape[1]), lambda i: (i, 0)) for a in rows]
    in_specs += [pl.BlockSpec(m.shape, lambda i, nd=m.ndim: (0,) * nd) for m in mats]
    out_specs = [pl.BlockSpec((ts, w), lambda i: (i, 0)) for w, _ in outs]
    out_specs += [pl.BlockSpec(s, lambda i: (0, 0)) for s in reds]
    out_shape = [jax.ShapeDtypeStruct((S, w), dt) for w, dt in outs] + [jax.ShapeDtypeStruct(s, F32) for s in reds]
    return pl.pallas_call(
        body, name=name, grid=(S // ts,), in_specs=in_specs, out_specs=out_specs, out_shape=out_shape,
        compiler_params=_params(("arbitrary",) if reds else ("parallel",)),
    )(*rows, *mats)


def _shift_down(v, s):
    if s == 0:
        return v
    rows = lax.broadcasted_iota(I32, v.shape, 0)
    return jnp.where(rows >= s, pltpu.roll(v, s, 0), 0.0)


def _shift_up(v, s):
    if s == 0:
        return v
    n = v.shape[0]
    rows = lax.broadcasted_iota(I32, v.shape, 0)
    return jnp.where(rows < n - s, pltpu.roll(v, n - s, 0), 0.0)


def _conv(x, w, b):
    K = w.shape[0]
    y = jnp.broadcast_to(b, x.shape)
    for k in range(K):
        y = y + w[k:k + 1, :] * _shift_down(x, K - 1 - k)
    return y


def _conv_back(x, w, dc):
    K = w.shape[0]
    dx = jnp.zeros_like(x)
    dw = []
    for k in range(K):
        dx = dx + w[k:k + 1, :] * _shift_up(dc, K - 1 - k)
        dw.append(_colsum(dc * _shift_down(x, K - 1 - k)))
    return dx, jnp.concatenate(dw, axis=0), _colsum(dc)


def _colwise(name, fn, cols, vecs, outs, pouts, tc):
    S, C = cols[0].shape
    nc_, nv, no = len(cols), len(vecs), len(outs)

    def body(*refs):
        res = fn(*[r[...] for r in refs[:nc_ + nv]])
        res = res if isinstance(res, (tuple, list)) else (res,)
        for r, v in zip(refs[nc_ + nv:], res):
            r[...] = v.astype(r.dtype)

    in_specs = [pl.BlockSpec((S, tc), lambda j: (0, j)) for _ in cols]
    in_specs += [pl.BlockSpec((v.shape[0], tc), lambda j: (0, j)) for v in vecs]
    out_specs = [pl.BlockSpec((S, tc), lambda j: (0, j)) for _ in outs] + [pl.BlockSpec((k, tc), lambda j: (0, j)) for k in pouts]
    out_shape = [jax.ShapeDtypeStruct((S, C), dt) for dt in outs] + [jax.ShapeDtypeStruct((k, C), F32) for k in pouts]
    return pl.pallas_call(
        body, name=name, grid=(C // tc,), in_specs=in_specs, out_specs=out_specs, out_shape=out_shape,
        compiler_params=_params(("parallel",)),
    )(*cols, *vecs)


_G0, _G1 = math.sqrt(2.0 / math.pi), 0.044715


def _gelu(g):
    th = jnp.tanh(_G0 * (g + _G1 * g * g * g))
    return 0.5 * g * (1.0 + th), th


def _ffn_act(gate_pre, up, w, b):
    act, _ = _gelu(_conv(gate_pre, w, b))
    return act * up


def _ffn_act_back(dact, gate_pre, up, w, b):
    g = _conv(gate_pre, w, b)
    ge, th = _gelu(g)
    dge = 0.5 * (1.0 + th) + 0.5 * g * (1.0 - th * th) * _G0 * (1.0 + 3.0 * _G1 * g * g)
    dup = dact * ge
    dgate_pre, dw, db = _conv_back(gate_pre, w, dact * up * dge)
    return dgate_pre, dup, dw, db


def _ssm_act(xbc, w, b):
    c = _conv(xbc, w, b)
    return c * _sigmoid(c)


def _ssm_act_back(dxc, xbc, w, b):
    c = _conv(xbc, w, b)
    sg = _sigmoid(c)
    return _conv_back(xbc, w, dxc * sg * (1.0 + c * (1.0 - sg)))


def _rope_tables(S):
    inv = 1.0 / (ROPE_THETA ** (jnp.arange(0, ROPE, 2, dtype=F32) / ROPE))
    ang = jnp.arange(S, dtype=F32)[:, None] * inv[None, :]
    cos, sin = jnp.cos(ang), jnp.sin(ang)
    return jnp.tile(cos, (1, 4)), jnp.tile(jnp.concatenate([-sin, sin], axis=1), (1, 2))


def _swap_halves(x):
    lane = lax.broadcasted_iota(I32, x.shape, 1)
    w = x.shape[1]
    return jnp.where((lane % ROPE) < ROPE // 2, pltpu.roll(x, w - ROPE // 2, 1), pltpu.roll(x, ROPE // 2, 1))


def _rot(x, cos2, sin2):
    return x * cos2 + _swap_halves(x) * sin2


def _rot_back(dy, cos2, sin2):
    return dy * cos2 + _swap_halves(dy * sin2)


def _mla_pack(cfg, q, kv, kr, cos2, sin2):
    S, H = cfg.S, cfg.H
    ts = _pick(S, 512, 8)

    def body(qn_ref, qr_ref, kn_ref, v_ref, kr_ref, c_ref, s_ref, Q_ref, K_ref, V_ref):
        h = pl.program_id(0)
        c2, s2 = c_ref[...], s_ref[...]
        Q_ref[0, :, 0:LANE] = qn_ref[...].astype(BF16)
        Q_ref[0, :, LANE:] = _rot(qr_ref[...], c2, s2).astype(BF16)
        K_ref[0, :, 0:LANE] = kn_ref[...].astype(BF16)
        krr = _rot(kr_ref[...], c2, s2)
        K_ref[0, :, LANE:] = jnp.where(h % 2 == 1, pltpu.roll(krr, ROPE, 1), krr).astype(BF16)
        V_ref[0] = v_ref[...].astype(BF16)

    blk = lambda f: pl.BlockSpec((ts, LANE), f)
    return pl.pallas_call(
        body, name="mla_pack", grid=(H, S // ts),
        in_specs=[blk(lambda h, i: (i, h)), blk(lambda h, i: (i, H + h // 2)), blk(lambda h, i: (i, h)),
                  blk(lambda h, i: (i, H + h)), blk(lambda h, i: (i, 0)), blk(lambda h, i: (i, 0)), blk(lambda h, i: (i, 0))],
        out_specs=[pl.BlockSpec((1, ts, 2 * LANE), lambda h, i: (h, i, 0)), pl.BlockSpec((1, ts, 2 * LANE), lambda h, i: (h, i, 0)),
                   pl.BlockSpec((1, ts, LANE), lambda h, i: (h, i, 0))],
        out_shape=[jax.ShapeDtypeStruct((H, S, 2 * LANE), BF16), jax.ShapeDtypeStruct((H, S, 2 * LANE), BF16),
                   jax.ShapeDtypeStruct((H, S, LANE), BF16)],
        compiler_params=_params(("parallel", "parallel")),
    )(q, q, kv, kv, kr, cos2, sin2)


def _mla_unpack(cfg, dQ, dK, dV, cos2, sin2):
    S, H = cfg.S, cfg.H
    ts = _pick(S, 256, 8)

    def body(dQ_ref, dK_ref, dV_ref, c_ref, s_ref, dq_ref, dkv_ref, dkr_ref):
        c2, s2 = c_ref[...], s_ref[...]
        lo = lax.broadcasted_iota(I32, (ts, LANE), 1) < ROPE
        tk = jnp.zeros((ts, LANE), F32)
        for h in range(H):
            dq_ref[:, h * LANE:(h + 1) * LANE] = dQ_ref[h, :, 0:LANE].astype(BF16)
            dkv_ref[:, h * LANE:(h + 1) * LANE] = dK_ref[h, :, 0:LANE].astype(BF16)
            dkv_ref[:, (H + h) * LANE:(H + h + 1) * LANE] = dV_ref[h].astype(BF16)
            own = lo if h % 2 == 0 else jnp.logical_not(lo)
            tk = tk + jnp.where(own, dK_ref[h, :, LANE:], 0.0)
        for j in range(H // 2):
            dr = dQ_ref[2 * j, :, LANE:] + dQ_ref[2 * j + 1, :, LANE:]
            dq_ref[:, (H + j) * LANE:(H + j + 1) * LANE] = _rot_back(dr, c2, s2).astype(BF16)
        dkr_rot = jnp.where(lo, tk + pltpu.roll(tk, ROPE, 1), 0.0)
        dkr_ref[...] = _rot_back(dkr_rot, c2, s2).astype(BF16)

    tab = pl.BlockSpec((ts, LANE), lambda i: (i, 0))
    return pl.pallas_call(
        body, name="mla_unpack", grid=(S // ts,),
        in_specs=[pl.BlockSpec((H, ts, 2 * LANE), lambda i: (0, i, 0)), pl.BlockSpec((H, ts, 2 * LANE), lambda i: (0, i, 0)),
                  pl.BlockSpec((H, ts, LANE), lambda i: (0, i, 0)), tab, tab],
        out_specs=[pl.BlockSpec((ts, cfg.QW), lambda i: (i, 0)), pl.BlockSpec((ts, cfg.KVW), lambda i: (i, 0)), tab],
        out_shape=[jax.ShapeDtypeStruct((S, cfg.QW), BF16), jax.ShapeDtypeStruct((S, cfg.KVW), BF16),
                   jax.ShapeDtypeStruct((S, LANE), BF16)],
        compiler_params=_params(("parallel",)),
    )(dQ, dK, dV, cos2, sin2)


_ATT_T = 256
_ATT_HB = 2
_ATT_SCALE = (NOPE + ROPE) ** -0.5


def _diag_mask(transposed=False):
    r = lax.broadcasted_iota(I32, (_ATT_T, _ATT_T), 0) // CHUNK
    c = lax.broadcasted_iota(I32, (_ATT_T, _ATT_T), 1) // CHUNK
    return r <= c if transposed else c <= r


def _row_form(col):
    return jnp.broadcast_to(col, (col.shape[0], LANE)).T[0:8, :]


def _attn_fwd(cfg, Q, K, V):
    S, H, T, HB = cfg.S, cfg.H, _ATT_T, _ATT_HB

    def body(q_ref, k_ref, v_ref, o_ref, lse_ref, lse_t_ref):
        qi = pl.program_id(1)

        def head_step(b, kb, carry, mask):
            m, l, acc = carry
            ks = pl.multiple_of(kb * T, T)
            s = _dot(q_ref[b], k_ref[b, pl.ds(ks, T), :], NT) * _ATT_SCALE
            if mask is not None:
                s = jnp.where(mask, s, -1e30)
            m_new = jnp.maximum(m, jnp.max(s, axis=1, keepdims=True))
            p = jnp.exp(s - m_new)
            alpha = jnp.exp(m - m_new)
            l = alpha * l + jnp.sum(p, axis=1, keepdims=True)
            acc = alpha * acc + _dot(p.astype(BF16), v_ref[b, pl.ds(ks, T), :])
            return m_new, l, acc

        def step(kb, carry, mask=None):
            return tuple(head_step(b, kb, carry[b], mask) for b in range(HB))

        init = (jnp.full((T, 1), -1e30, F32), jnp.zeros((T, 1), F32), jnp.zeros((T, VH), F32))
        done = step(qi, lax.fori_loop(0, qi, step, (init,) * HB), _diag_mask())
        for b, (m, l, acc) in enumerate(done):
            o_ref[:, b * LANE:(b + 1) * LANE] = acc / l
            lse = m + jnp.log(l)
            lse_ref[:, b * LANE:(b + 1) * LANE] = jnp.broadcast_to(lse, (T, LANE))
            lse_t_ref[b] = _row_form(lse)

    return pl.pallas_call(
        body, name="attn_fwd", grid=(H // HB, S // T),
        in_specs=[pl.BlockSpec((HB, T, 2 * LANE), lambda h, i: (h, i, 0)), pl.BlockSpec((HB, S, 2 * LANE), lambda h, i: (h, 0, 0)),
                  pl.BlockSpec((HB, S, LANE), lambda h, i: (h, 0, 0))],
        out_specs=[pl.BlockSpec((T, HB * LANE), lambda h, i: (i, h)), pl.BlockSpec((T, HB * LANE), lambda h, i: (i, h)),
                   pl.BlockSpec((HB, 8, T), lambda h, i: (h, 0, i))],
        out_shape=[jax.ShapeDtypeStruct((S, H * LANE), F32), jax.ShapeDtypeStruct((S, H * LANE), F32),
                   jax.ShapeDtypeStruct((H, 8, S), F32)],
        compiler_params=_params(("parallel", "parallel")),
    )(Q, K, V)


def _attn_dq(cfg, Q, K, V, do, o, lse):
    S, H, T, HB = cfg.S, cfg.H, _ATT_T, _ATT_HB

    def body(q_ref, k_ref, v_ref, do_ref, o_ref, lse_ref, dq_ref, dl_t_ref):
        qi = pl.program_id(1)
        do = [do_ref[:, b * LANE:(b + 1) * LANE] for b in range(HB)]
        delta = [jnp.sum(do[b] * o_ref[:, b * LANE:(b + 1) * LANE], axis=1, keepdims=True) for b in range(HB)]
        dob = [d.astype(BF16) for d in do]

        def head_step(b, kb, dq, mask):
            ks = pl.multiple_of(kb * T, T)
            k = k_ref[b, pl.ds(ks, T), :]
            s = _dot(q_ref[b], k, NT) * _ATT_SCALE
            if mask is not None:
                s = jnp.where(mask, s, -1e30)
            p = jnp.exp(s - lse_ref[:, b * LANE:b * LANE + 1])
            dp = _dot(dob[b], v_ref[b, pl.ds(ks, T), :], NT)
            ds = p * (dp - delta[b]) * _ATT_SCALE
            return dq + _dot(ds.astype(BF16), k)

        def step(kb, dqs, mask=None):
            return tuple(head_step(b, kb, dqs[b], mask) for b in range(HB))

        dqs = step(qi, lax.fori_loop(0, qi, step, (jnp.zeros((T, 2 * LANE), F32),) * HB), _diag_mask())
        for b in range(HB):
            dq_ref[b] = dqs[b]
            dl_t_ref[b] = _row_form(delta[b])

    col = pl.BlockSpec((T, HB * LANE), lambda h, i: (i, h))
    return pl.pallas_call(
        body, name="attn_dq", grid=(H // HB, S // T),
        in_specs=[pl.BlockSpec((HB, T, 2 * LANE), lambda h, i: (h, i, 0)), pl.BlockSpec((HB, S, 2 * LANE), lambda h, i: (h, 0, 0)),
                  pl.BlockSpec((HB, S, LANE), lambda h, i: (h, 0, 0)), col, col, col],
        out_specs=[pl.BlockSpec((HB, T, 2 * LANE), lambda h, i: (h, i, 0)), pl.BlockSpec((HB, 8, T), lambda h, i: (h, 0, i))],
        out_shape=[jax.ShapeDtypeStruct((H, S, 2 * LANE), F32), jax.ShapeDtypeStruct((H, 8, S), F32)],
        compiler_params=_params(("parallel", "parallel")),
    )(Q, K, V, do, o, lse)


def _attn_dkv(cfg, Q, K, V, do, lse_t, delta_t):
    S, H, T, HB = cfg.S, cfg.H, _ATT_T, _ATT_HB
    nq = S // T

    def body(q_ref, k_ref, v_ref, do_ref, lse_ref, dl_ref, dk_ref, dv_ref):
        kb = pl.program_id(1)

        def head_step(b, qi, carry, mask):
            dk, dv = carry
            qs = pl.multiple_of(qi * T, T)
            q = q_ref[b, pl.ds(qs, T), :]
            dob = do_ref[pl.ds(qs, T), b * LANE:(b + 1) * LANE].astype(BF16)
            s = _dot(k_ref[b], q, NT) * _ATT_SCALE
            if mask is not None:
                s = jnp.where(mask, s, -1e30)
            p = jnp.exp(s - lse_ref[b, 0:1, pl.ds(qs, T)])
            dv = dv + _dot(p.astype(BF16), dob)
            dp = _dot(v_ref[b], dob, NT)
            ds = p * (dp - dl_ref[b, 0:1, pl.ds(qs, T)]) * _ATT_SCALE
            dk = dk + _dot(ds.astype(BF16), q)
            return dk, dv

        def step(qi, carry, mask=None):
            return tuple(head_step(b, qi, carry[b], mask) for b in range(HB))

        zero = (jnp.zeros((T, 2 * LANE), F32), jnp.zeros((T, VH), F32))
        done = lax.fori_loop(kb + 1, nq, step, step(kb, (zero,) * HB, _diag_mask(transposed=True)))
        for b, (dk, dv) in enumerate(done):
            dk_ref[b] = dk
            dv_ref[b] = dv

    row = pl.BlockSpec((HB, 8, S), lambda h, j: (h, 0, 0))
    return pl.pallas_call(
        body, name="attn_dkv", grid=(H // HB, S // T),
        in_specs=[pl.BlockSpec((HB, S, 2 * LANE), lambda h, j: (h, 0, 0)), pl.BlockSpec((HB, T, 2 * LANE), lambda h, j: (h, j, 0)),
                  pl.BlockSpec((HB, T, LANE), lambda h, j: (h, j, 0)), pl.BlockSpec((S, HB * LANE), lambda h, j: (0, h)), row, row],
        out_specs=[pl.BlockSpec((HB, T, 2 * LANE), lambda h, j: (h, j, 0)), pl.BlockSpec((HB, T, LANE), lambda h, j: (h, j, 0))],
        out_shape=[jax.ShapeDtypeStruct((H, S, 2 * LANE), F32), jax.ShapeDtypeStruct((H, S, LANE), F32)],
        compiler_params=_params(("parallel", "parallel")),
    )(Q, K, V, do, lse_t, delta_t)


def _expand_matrix(cfg):
    r = lax.broadcasted_iota(I32, (LANE, cfg.INNER), 0)
    c = lax.broadcasted_iota(I32, (LANE, cfg.INNER), 1)
    return (r == c // HP).astype(F32)


def _softplus(x):
    return jnp.maximum(x, 0.0) + jnp.log(1.0 + jnp.exp(-jnp.abs(x)))


def _ssd_prep(cfg, dt_raw, dt_bias_pad, a_log_pad, expand):
    HS = cfg.HS

    def fn(raw, bias, alog, E):
        heads = lax.broadcasted_iota(I32, raw.shape, 1) < HS
        dt = jnp.where(heads, _softplus(raw + bias), 0.0)
        a = dt * jnp.where(heads[0:1], -jnp.exp(alog), 0.0)
        return dt, a, _dot(dt, E, precision=HI), _dot(a, E, precision=HI)

    return _rowwise("ssd_prep", fn, [dt_raw], [dt_bias_pad, a_log_pad, expand],
                    [(LANE, F32), (LANE, F32), (cfg.INNER, F32), (cfg.INNER, F32)], [], _pick(cfg.S, 512, 8))


def _tril(T):
    return lax.broadcasted_iota(I32, (T, T), 0) >= lax.broadcasted_iota(I32, (T, T), 1)


def _ssd_fwd(cfg, xc, dt_exp, a_exp, a_small, dskip_exp):
    S, T, INNER, G, NPAIR = cfg.S, cfg.T, cfg.INNER, cfg.G, cfg.NPAIR
    NC = S // T

    def body(xc_ref, dte_ref, ae_ref, as_ref, dsk_ref, y_ref, hin_ref, ht_ref):
        @pl.when(pl.program_id(0) == 0)
        def _():
            ht_ref[...] = jnp.zeros_like(ht_ref)

        tril = _tril(T)
        tri = tril.astype(F32)
        acs_s = _dot(tri, as_ref[...], precision=HI)
        acs_e = _dot(tri, ae_ref[...], precision=HI)
        acs_t = acs_s.T
        lo = lax.broadcasted_iota(I32, (T, LANE), 1) < HP
        for g in range(G):
            Bb = xc_ref[:, INNER + g * NST:INNER + (g + 1) * NST].astype(BF16)
            Cb = xc_ref[:, INNER + (G + g) * NST:INNER + (G + g + 1) * NST].astype(BF16)
            Gm = _dot(Cb, Bb, NT)
            for j in range(g * NPAIR // G, (g + 1) * NPAIR // G):
                sl = slice(j * LANE, (j + 1) * LANE)
                Xp = xc_ref[:, sl]
                Xdt = Xp * dte_ref[:, sl]
                Xb = Xdt.astype(BF16)
                acs_p = acs_e[:, sl]
                last = acs_p[T - 1:T, :]
                Hin = ht_ref[j]
                hin_ref[0, j] = Hin
                yd = []
                for e in (0, 1):
                    h = 2 * j + e
                    Lm = jnp.exp(jnp.where(tril, acs_s[:, h:h + 1] - acs_t[h:h + 1, :], -1e30))
                    yd.append(_dot((Gm * Lm).astype(BF16), Xb))
                y_off = _dot(Cb, Hin.astype(BF16)) * jnp.exp(acs_p)
                y_ref[:, sl] = jnp.where(lo, yd[0], yd[1]) + y_off + Xp * dsk_ref[:, sl]
                st = _dot(Bb, (Xdt * jnp.exp(last - acs_p)).astype(BF16), TN)
                ht_ref[j] = jnp.exp(last) * Hin + st

    rows = lambda w: pl.BlockSpec((T, w), lambda c: (c, 0))
    return pl.pallas_call(
        body, name="ssd_fwd", grid=(NC,),
        in_specs=[rows(cfg.CONVCH), rows(INNER), rows(INNER), rows(LANE), pl.BlockSpec((1, INNER), lambda c: (0, 0))],
        out_specs=[rows(INNER), pl.BlockSpec((1, NPAIR, NST, LANE), lambda c: (c, 0, 0, 0))],
        out_shape=[jax.ShapeDtypeStruct((S, INNER), F32), jax.ShapeDtypeStruct((NC, NPAIR, NST, LANE), F32)],
        scratch_shapes=[pltpu.VMEM((NPAIR, NST, LANE), F32)],
        compiler_params=_params(("arbitrary",)),
    )(xc, dt_exp, a_exp, a_small, dskip_exp)


def _ssd_bwd(cfg, dy, xc, dt_exp, a_exp, a_small, dskip_exp, hin, dt_raw, dt_bias_pad, a_log_pad, expand):
    S, T, INNER, G, NPAIR, HS = cfg.S, cfg.T, cfg.INNER, cfg.G, cfg.NPAIR, cfg.HS
    NC = S // T

    def body(dy_ref, xc_ref, dte_ref, ae_ref, as_ref, dsk_ref, hin_ref, raw_ref, bias_ref, alog_ref, e_ref,
             dxc_ref, draw_ref, dbias_ref, dalog_ref, dskip_ref, dht_ref, cols_ref, rows_ref, dacs_ref, ddt_ref):
        first = pl.program_id(0) == 0

        @pl.when(first)
        def _():
            dht_ref[...] = jnp.zeros_like(dht_ref)

        tril = _tril(T)
        tri = tril.astype(F32)
        a_s = as_ref[...]
        acs_s = _dot(tri, a_s, precision=HI)
        acs_e = _dot(tri, ae_ref[...], precision=HI)
        acs_t = acs_s.T
        lo = lax.broadcasted_iota(I32, (T, LANE), 1) < HP
        last_row = lax.broadcasted_iota(I32, (T, LANE), 0) == T - 1
        cols_ref[...] = jnp.zeros_like(cols_ref)
        rows_ref[...] = jnp.zeros_like(rows_ref)
        dsk_parts = []
        for g in range(G):
            bsl = slice(INNER + g * NST, INNER + (g + 1) * NST)
            csl = slice(INNER + (G + g) * NST, INNER + (G + g + 1) * NST)
            Bb = xc_ref[:, bsl].astype(BF16)
            Cb = xc_ref[:, csl].astype(BF16)
            Gm = _dot(Cb, Bb, NT)
            dG = jnp.zeros((T, T), F32)
            dB = jnp.zeros((T, NST), F32)
            dC = jnp.zeros((T, NST), F32)
            for j in range(g * NPAIR // G, (g + 1) * NPAIR // G):
                sl = slice(j * LANE, (j + 1) * LANE)
                Xp = xc_ref[:, sl]
                dtp = dte_ref[:, sl]
                Xdt = Xp * dtp
                Xb = Xdt.astype(BF16)
                acs_p = acs_e[:, sl]
                last = acs_p[T - 1:T, :]
                e_p, dec, cd = jnp.exp(acs_p), jnp.exp(last - acs_p), jnp.exp(last)
                Hin = hin_ref[0, j]
                Hb = Hin.astype(BF16)
                dHn = dht_ref[j]
                dHb = dHn.astype(BF16)
                dYp = dy_ref[:, sl]
                z = _dot(Cb, Hb)
                dz = (dYp * e_p).astype(BF16)
                dacs_p = dYp * z * e_p
                dC = dC + _dot(dz, Hb, NT)
                dHin = _dot(Cb, dz, TN) + cd * dHn
                dlast = _colsum(dHn * Hin) * cd
                qv = _dot(Bb, dHb)
                dXdt = qv * dec
                ddec = qv * Xdt * dec
                dacs_p = dacs_p - ddec
                dlast = dlast + _colsum(ddec)
                dB = dB + _dot((Xdt * dec).astype(BF16), dHb, NT)
                for e in (0, 1):
                    h = 2 * j + e
                    Lm = jnp.exp(jnp.where(tril, acs_s[:, h:h + 1] - acs_t[h:h + 1, :], -1e30))
                    Mh = Gm * Lm
                    dYe = jnp.where(lo if e == 0 else jnp.logical_not(lo), dYp, 0.0).astype(BF16)
                    dM = _dot(dYe, Xb, NT)
                    dXdt = dXdt + _dot(Mh.astype(BF16), dYe, TN)
                    W = dM * Mh
                    cols_ref[:, h:h + 1] = jnp.sum(W, axis=1, keepdims=True)
                    rows_ref[h:h + 1, :] = _colsum(W)
                    dG = dG + dM * Lm
                dacs_ref[:, sl] = dacs_p + jnp.where(last_row, dlast, 0.0)
                ddt_ref[:, sl] = dXdt * Xp
                dxc_ref[:, sl] = dXdt * dtp + dYp * dsk_ref[:, sl]
                dsk_parts.append(_colsum(dYp * Xp))
                dht_ref[j] = dHin
            dGb = dG.astype(BF16)
            dxc_ref[:, bsl] = dB + _dot(dGb, Cb, TN)
            dxc_ref[:, csl] = dC + _dot(dGb, Bb)
        E = e_ref[...]
        dacs_s = cols_ref[...] - rows_ref[...].T + _dot(dacs_ref[...], E, NT, precision=HI)
        da = _dot(tri, dacs_s, TN, precision=HI)
        heads = lax.broadcasted_iota(I32, (1, LANE), 1) < HS
        A = jnp.where(heads, -jnp.exp(alog_ref[...]), 0.0)
        ddt = _dot(ddt_ref[...], E, NT, precision=HI) + da * A
        draw = jnp.where(heads, ddt * _sigmoid(raw_ref[...] + bias_ref[...]), 0.0)
        draw_ref[...] = draw
        dsk = _dot(jnp.broadcast_to(jnp.concatenate(dsk_parts, axis=1), (8, INNER)), E, NT, precision=HI)[0:1]
        for ref, val in ((dbias_ref, _colsum(draw)), (dalog_ref, _colsum(da * a_s)), (dskip_ref, dsk)):
            @pl.when(first)
            def _():
                ref[...] = val

            @pl.when(jnp.logical_not(first))
            def _():
                ref[...] += val

    rows = lambda w: pl.BlockSpec((T, w), lambda c: (NC - 1 - c, 0))
    vec = lambda w: pl.BlockSpec((1, w), lambda c: (0, 0))
    return pl.pallas_call(
        body, name="ssd_bwd", grid=(NC,),
        in_specs=[rows(INNER), rows(cfg.CONVCH), rows(INNER), rows(INNER), rows(LANE), vec(INNER),
                  pl.BlockSpec((1, NPAIR, NST, LANE), lambda c: (NC - 1 - c, 0, 0, 0)), rows(LANE), vec(LANE), vec(LANE),
                  pl.BlockSpec((LANE, INNER), lambda c: (0, 0))],
        out_specs=[rows(cfg.CONVCH), rows(LANE), vec(LANE), vec(LANE), vec(LANE)],
        out_shape=[jax.ShapeDtypeStruct((S, cfg.CONVCH), F32), jax.ShapeDtypeStruct((S, LANE), F32)]
        + [jax.ShapeDtypeStruct((1, LANE), F32)] * 3,
        scratch_shapes=[pltpu.VMEM((NPAIR, NST, LANE), F32), pltpu.VMEM((T, LANE), F32), pltpu.VMEM((LANE, T), F32),
                        pltpu.VMEM((T, INNER), F32), pltpu.VMEM((T, INNER), F32)],
        compiler_params=_params(("arbitrary",)),
    )(dy, xc, dt_exp, a_exp, a_small, dskip_exp, hin, dt_raw, dt_bias_pad, a_log_pad, expand)


def _ssd_post(cfg, y, z, norm_g):
    W = cfg.INNER // cfg.G

    def fn(y, z, g):
        yz = y * z * _sigmoid(z)
        return jnp.concatenate([yz[:, i * W:(i + 1) * W] * _rs(yz[:, i * W:(i + 1) * W]) for i in range(cfg.G)], axis=1) * g

    return _rowwise("ssd_post", fn, [y, z], [norm_g], [(cfg.INNER, BF16)], [], _pick(cfg.S, 256, 8))[0]


def _ssd_post_bwd(cfg, db, y, z, norm_g):
    W = cfg.INNER // cfg.G

    def fn(db, y, z, g):
        sg = _sigmoid(z)
        yz = y * z * sg
        dn = db * g
        dyz, nh = [], []
        for i in range(cfg.G):
            seg = yz[:, i * W:(i + 1) * W]
            r = _rs(seg)
            nh.append(seg * r)
            dyz.append(_rms_back(nh[-1], r, dn[:, i * W:(i + 1) * W]))
        dyz = jnp.concatenate(dyz, axis=1)
        return dyz * z * sg, dyz * y * sg * (1.0 + z * (1.0 - sg)), _colsum(db * jnp.concatenate(nh, axis=1))

    return _rowwise("ssd_post_bwd", fn, [db, y, z], [norm_g], [(cfg.INNER, F32), (cfg.INNER, F32)], [(1, cfg.INNER)],
                    _pick(cfg.S, 256, 8))


def _local_grads(cfg, x, tgt, W, sp, out_weight=None, ffn_weights=None, ffn_grads_ready=None, early_grads_ready=None,
                 in_grad_ready=None):
    S, D, H, INNER = cfg.S, cfg.D, cfg.H, cfg.INNER
    ts = _pick(S, 256, 8)
    tc = 256

    xn = _rowwise("rms_pre", lambda x, g: x * _rs(x) * g, [x], [sp["mix_pre_g"]], [(D, BF16)], [], ts)[0]
    u = _matmul("mm_in", xn, W["w_in"], "nt", F32)
    c_q, c_kv = u[:, :cfg.QL], u[:, cfg.QL:cfg.o_kr]
    kr = u[:, cfg.o_kr:cfg.o_z]
    z = u[:, cfg.o_z:cfg.o_xbc]
    xbc = u[:, cfg.o_xbc:cfg.o_dt]
    dt_raw = u[:, cfg.o_dt:]

    cqn = _rowwise("rms_q", lambda x, g: x * _rs(x) * g, [c_q], [sp["q_norm_g"]], [(cfg.QL, BF16)], [], ts)[0]
    ckvn = _rowwise("rms_kv", lambda x, g: x * _rs(x) * g, [c_kv], [sp["kv_norm_g"]], [(cfg.KVL, BF16)], [], ts)[0]
    q = _matmul("mm_uq", cqn, W["w_uq"], "nn", F32)
    kv = _matmul("mm_ukv", ckvn, W["w_ukv"], "nn", F32)
    cos2, sin2 = _rope_tables(S)
    Qh, Kh, Vh = _mla_pack(cfg, q, kv, kr, cos2, sin2)
    a_out, lse, lse_t = _attn_fwd(cfg, Qh, Kh, Vh)
    if out_weight is not None:
        sp = dict(sp, ssm_conv_b=sp["ssm_conv_b"] + out_weight.pass_on(a_out)[0, 0])

    pad = lambda v: jnp.pad(v, ((0, 0), (0, LANE - v.shape[1])))
    expand = _expand_matrix(cfg)
    dt_bias_pad, a_log_pad = pad(sp["dt_bias"]), pad(sp["a_log"])
    dskip_exp = jnp.repeat(sp["d_skip"], HP, axis=1)
    xc = _colwise("ssm_act", _ssm_act, [xbc], [sp["ssm_conv_w"], sp["ssm_conv_b"]], [F32], [], tc)[0]
    dt_s, a_s, dt_exp, a_exp = _ssd_prep(cfg, dt_raw, dt_bias_pad, a_log_pad, expand)
    y_ssd, hin = _ssd_fwd(cfg, xc, dt_exp, a_exp, a_s, dskip_exp)
    b_out = _ssd_post(cfg, y_ssd, z, sp["ssm_norm_g"])

    ab_out = jnp.concatenate([a_out.astype(BF16), b_out], axis=1)
    if out_weight is not None:
        W = dict(W, **out_weight.arrived(ab_out))
    if ffn_weights is not None:
        sp = dict(sp, mix_post_g=sp["mix_post_g"] + ffn_weights.pass_on(ab_out)[0, 0])
    mix = _matmul("mm_out", ab_out, W["w_out"], "nn", F32)

    def mid(x, mix, g_mp, g_fp):
        x1 = x + mix * _rs(mix) * g_mp
        return x1, x1 * _rs(x1) * g_fp

    x1, h2 = _rowwise("fwd_mid", mid, [x, mix], [sp["mix_post_g"], sp["ffn_pre_g"]], [(D, F32), (D, BF16)], [], ts)
    if ffn_weights is not None:
        W = dict(W, **ffn_weights.arrived(h2))
    gate_pre = _matmul("mm_gate", h2, W["w_gate"], "nn", F32, chips=True)
    up = _matmul("mm_up", h2, W["w_up"], "nn", F32, chips=True)
    act = _colwise("ffn_act", _ffn_act, [gate_pre, up], [sp["ffn_conv_w"], sp["ffn_conv_b"]], [BF16], [], tc)[0]
    f = _matmul("mm_down", act, W["w_down"], "nn", F32)

    def final(x1, f, t, g):
        r = _rs(f)
        fh = f * r
        err = x1 + fh * g - t
        loss = 0.5 * jnp.sum(jnp.mean(err * err, axis=-1, keepdims=True), axis=0, keepdims=True)
        dy = err * (1.0 / D)
        return dy, _rms_back(fh, r, dy * g), _colsum(dy * fh), loss

    dy, df, g_ffn_post, loss = _rowwise("final", final, [x1, f, tgt], [sp["ffn_post_g"]], [(D, F32), (D, BF16)],
                                        [(1, D), (1, LANE)], ts)
    gW = {}
    dact = _matmul("mm_down_dx", df, W["w_down"], "nt", F32)
    gW["w_down"] = _matmul("mm_down_dw", act, df, "tn", BF16)
    dgate, dup, g_ffn_conv_w, g_ffn_conv_b = _colwise(
        "ffn_act_bwd", _ffn_act_back, [dact, gate_pre, up], [sp["ffn_conv_w"], sp["ffn_conv_b"]], [BF16, BF16], [FFN_K, 1], tc)
    gW["w_gate"] = _matmul("mm_gate_dw", h2, dgate, "tn", BF16, chips=True)
    gW["w_up"] = _matmul("mm_up_dw", h2, dup, "tn", BF16, chips=True)
    if ffn_grads_ready is not None:
        sp = dict(sp, ffn_pre_g=sp["ffn_pre_g"] + ffn_grads_ready({n: gW[n] for n in ("w_down", "w_gate", "w_up")})[0, 0])
    dh2 = _matmul("mm_gu_dx", dgate, W["w_gate"], "nt", F32, dup, W["w_up"], chips=True)

    def mid_back(dy, dh2, x1, mix, g_mp, g_fp):
        r2 = _rs(x1)
        xh = x1 * r2
        dx1 = dy + _rms_back(xh, r2, dh2 * g_fp)
        r1 = _rs(mix)
        mh = mix * r1
        return dx1, _rms_back(mh, r1, dx1 * g_mp), _colsum(dh2 * xh), _colsum(dx1 * mh)

    dx1, dmix, g_ffn_pre, g_mix_post = _rowwise("bwd_mid", mid_back, [dy, dh2, x1, mix], [sp["mix_post_g"], sp["ffn_pre_g"]],
                                                [(D, F32), (D, BF16)], [(1, D), (1, D)], ts)
    dab_out = _matmul("mm_out_dx", dmix, W["w_out"], "nt", F32)
    db_out = dab_out[:, cfg.MLAW:]
    gW["w_out"] = _matmul("mm_out_dw", ab_out, dmix, "tn", BF16)
    if early_grads_ready is not None:
        token = early_grads_ready({n: gW[n] for n in ("w_down", "w_gate", "w_up", "w_out")})
        sp = dict(sp, ssm_norm_g=sp["ssm_norm_g"] + token[0, 0])

    dy_ssd, dz, g_ssm_norm = _ssd_post_bwd(cfg, db_out, y_ssd, z, sp["ssm_norm_g"])
    dxc, ddt_raw, g_dt_bias, g_a_log, g_d_skip = _ssd_bwd(cfg, dy_ssd, xc, dt_exp, a_exp, a_s, dskip_exp, hin, dt_raw,
                                                          dt_bias_pad, a_log_pad, expand)
    dxbc, g_ssm_conv_w, g_ssm_conv_b = _colwise("ssm_act_bwd", _ssm_act_back, [dxc, xbc], [sp["ssm_conv_w"], sp["ssm_conv_b"]],
                                                [BF16], [SSM_K, 1], tc)

    dQ, delta_t = _attn_dq(cfg, Qh, Kh, Vh, dab_out, a_out, lse)
    dK, dV = _attn_dkv(cfg, Qh, Kh, Vh, dab_out, lse_t, delta_t)
    dq, dkv, dkr = _mla_unpack(cfg, dQ, dK, dV, cos2, sin2)
    dcqn = _matmul("mm_uq_dx", dq, W["w_uq"], "nt", F32)
    dckvn = _matmul("mm_ukv_dx", dkv, W["w_ukv"], "nt", F32)
    gW["w_uq"] = _matmul("mm_uq_dw", cqn, dq, "tn", BF16)
    gW["w_ukv"] = _matmul("mm_ukv_dw", ckvn, dkv, "tn", BF16)

    def rms_back(x, dy, g):
        r = _rs(x)
        xh = x * r
        return _rms_back(xh, r, dy * g), _colsum(dy * xh)

    dc_q, g_q_norm = _rowwise("rms_q_bwd", rms_back, [c_q, dcqn], [sp["q_norm_g"]], [(cfg.QL, BF16)], [(1, cfg.QL)], ts)
    dc_kv, g_kv_norm = _rowwise("rms_kv_bwd", rms_back, [c_kv, dckvn], [sp["kv_norm_g"]], [(cfg.KVL, BF16)], [(1, cfg.KVL)], ts)

    du = jnp.concatenate([dc_q, dc_kv, dkr, dz.astype(BF16), dxbc, ddt_raw.astype(BF16)], axis=1)
    gW["w_in"] = _matmul("mm_in_dw", du, xn, "tn", BF16)
    if in_grad_ready is not None:
        token = in_grad_ready({n: gW[n] for n in ("w_in", "w_uq", "w_ukv")})
        sp = dict(sp, mix_pre_g=sp["mix_pre_g"] + token[0, 0])
    dxn = _matmul("mm_in_dx", du, W["w_in"], "nn", F32)

    def first_back(dx1, dxn, x, g):
        r = _rs(x)
        xh = x * r
        return dx1 + _rms_back(xh, r, dxn * g), _colsum(dxn * xh)

    grad_x, g_mix_pre = _rowwise("bwd_first", first_back, [dx1, dxn, x], [sp["mix_pre_g"]], [(D, F32)], [(1, D)], ts)

    gs = dict(mix_pre_g=g_mix_pre, q_norm_g=g_q_norm, kv_norm_g=g_kv_norm, ssm_conv_w=g_ssm_conv_w, ssm_conv_b=g_ssm_conv_b,
              dt_bias=g_dt_bias[:, :cfg.HS], a_log=g_a_log[:, :cfg.HS], d_skip=g_d_skip[:, :cfg.HS], ssm_norm_g=g_ssm_norm,
              mix_post_g=g_mix_post, ffn_pre_g=g_ffn_pre, ffn_conv_w=g_ffn_conv_w, ffn_conv_b=g_ffn_conv_b,
              ffn_post_g=g_ffn_post)
    return loss, grad_x, gW, gs


def _to_kernel_layout(cfg, name, w):
    if name == "w_in":
        a = cfg.o_kr + ROPE
        return jnp.concatenate([w[:a], jnp.zeros((LANE - ROPE, w.shape[1]), w.dtype), w[a:],
                                jnp.zeros((LANE - cfg.HS, w.shape[1]), w.dtype)], axis=0)
    if name in ("w_uq", "w_ukv"):
        per = NOPE + (ROPE if name == "w_uq" else VH)
        return jnp.concatenate([w[:, h * per:h * per + NOPE] for h in range(cfg.H)]
                               + [w[:, h * per + NOPE:(h + 1) * per] for h in range(cfg.H)], axis=1)
    return w


def _from_kernel_layout(cfg, name, g):
    if name == "w_in":
        return jnp.concatenate([g[:cfg.o_kr + ROPE], g[cfg.o_z:cfg.o_dt + cfg.HS]], axis=0)
    if name in ("w_uq", "w_ukv"):
        second = ROPE if name == "w_uq" else VH
        base = cfg.H * NOPE
        parts = []
        for h in range(cfg.H):
            parts += [g[:, h * NOPE:(h + 1) * NOPE], g[:, base + h * second:base + (h + 1) * second]]
        return jnp.concatenate(parts, axis=1)
    return g


def _cols_to_chips(w):
    r, c = w.shape
    return w.reshape(r, N_CHIPS, c // N_CHIPS).transpose(1, 0, 2)


def _chips_to_cols(g):
    k, r, cs = g.shape
    return g.transpose(1, 0, 2).reshape(r, k * cs)


_CHIP_MAJOR = ("w_gate", "w_up")
_RELAYOUT = ("w_uq", "w_ukv")
_LAYOUT_ROWS = 256


def _w_in_layout(cfg, wg):
    _, rs, d = wg.shape
    tc = _pick(d, _LAYOUT_ROWS, LANE)

    def body(w_ref, o_ref):
        o_ref[...] = _to_kernel_layout(cfg, "w_in", jnp.concatenate([w_ref[k] for k in range(N_CHIPS)], axis=0))

    return pl.pallas_call(
        body, name="layout_w_in", grid=(d // tc,),
        in_specs=[pl.BlockSpec((N_CHIPS, rs, tc), lambda j: (0, 0, j))], out_specs=pl.BlockSpec((cfg.EXT, tc), lambda j: (0, j)),
        out_shape=jax.ShapeDtypeStruct((cfg.EXT, d), wg.dtype), compiler_params=_params(("parallel",)),
    )(wg)


def _w_in_grad_to_chips(cfg, g):
    _, d = g.shape
    rs = cfg.IN_COLS // N_CHIPS
    tc = _pick(d, _LAYOUT_ROWS, LANE)

    def body(g_ref, o_ref):
        nat = _from_kernel_layout(cfg, "w_in", g_ref[...])
        for k in range(N_CHIPS):
            o_ref[k] = nat[k * rs:(k + 1) * rs]

    return pl.pallas_call(
        body, name="layout_grad_w_in", grid=(d // tc,),
        in_specs=[pl.BlockSpec((cfg.EXT, tc), lambda j: (0, j))], out_specs=pl.BlockSpec((N_CHIPS, rs, tc), lambda j: (0, 0, j)),
        out_shape=jax.ShapeDtypeStruct((N_CHIPS, rs, d), g.dtype), compiler_params=_params(("parallel",)),
    )(g)


def _gathered_to_kernel(cfg, name, wg):
    if name in _CHIP_MAJOR:
        return wg
    if name == "w_in":
        return _w_in_layout(cfg, wg)
    if name not in _RELAYOUT:
        return wg.reshape(wg.shape[0] * wg.shape[1], wg.shape[2])
    _, rows, cs = wg.shape
    tr = _pick(rows, _LAYOUT_ROWS, 16)

    def body(w_ref, o_ref):
        o_ref[...] = _to_kernel_layout(cfg, name, jnp.concatenate([w_ref[k] for k in range(N_CHIPS)], axis=1))

    wide = jax.eval_shape(lambda w: _to_kernel_layout(cfg, name, w), jax.ShapeDtypeStruct((rows, N_CHIPS * cs), wg.dtype)).shape[1]
    return pl.pallas_call(
        body, name="layout_" + name, grid=(rows // tr,),
        in_specs=[pl.BlockSpec((N_CHIPS, tr, cs), lambda i: (0, i, 0))], out_specs=pl.BlockSpec((tr, wide), lambda i: (i, 0)),
        out_shape=jax.ShapeDtypeStruct((rows, wide), wg.dtype), compiler_params=_params(("parallel",)),
    )(wg)


def _grad_to_chips(cfg, name, g):
    if name in _CHIP_MAJOR:
        return g
    if name == "w_in":
        return _w_in_grad_to_chips(cfg, g)
    if name not in _RELAYOUT:
        return g.reshape(N_CHIPS, g.shape[0] // N_CHIPS, g.shape[1])
    rows, wide = g.shape
    tr = _pick(rows, _LAYOUT_ROWS, 16)
    cs = jax.eval_shape(lambda v: _from_kernel_layout(cfg, name, v), g).shape[1] // N_CHIPS

    def body(g_ref, o_ref):
        nat = _from_kernel_layout(cfg, name, g_ref[...])
        for k in range(N_CHIPS):
            o_ref[k] = nat[:, k * cs:(k + 1) * cs]

    return pl.pallas_call(
        body, name="layout_grad_" + name, grid=(rows // tr,),
        in_specs=[pl.BlockSpec((tr, wide), lambda i: (i, 0))], out_specs=pl.BlockSpec((N_CHIPS, tr, cs), lambda i: (0, i, 0)),
        out_shape=jax.ShapeDtypeStruct((N_CHIPS, rows, cs), g.dtype), compiler_params=_params(("parallel",)),
    )(g)


def _me():
    return lax.axis_index("x"), lax.axis_index("y"), lax.axis_index("c")


def _other_chips(x, y):
    return [(1 - x, y), (x, 1 - y), (1 - x, 1 - y)]


_ANY = pl.BlockSpec(memory_space=pl.ANY)


def _row_block(rows, cols, mult):
    return _pick(rows, max(mult, (1 << 19) // cols // mult * mult), mult)


def _scalar(v):
    return v.astype(I32).reshape(1)


def _blocks2d(r, c, mult):
    if r % mult == 0:
        tr = _row_block(r, c, mult)
        return (tr, c), r // tr, lambda i: (i, 0)
    tc = _pick(c, max(LANE, (1 << 19) // r // LANE * LANE), LANE)
    return (r, tc), c // tc, lambda i: (0, i)


def _by_rows(rows):
    return rows % 32 == 0


def _half_shape(rows, cols):
    return (rows // 2, cols) if _by_rows(rows) else (rows, cols // 2)


def _half_blocks(rows, cols, mult):
    hr, hc = _half_shape(rows, cols)
    block, n, part = _blocks2d(hr, hc, mult)
    assert (hr % mult == 0) == _by_rows(rows), (rows, cols, mult)
    full = (lambda h, i: (h * n + i, 0)) if _by_rows(rows) else (lambda h, i: (0, h * n + i))
    return block, n, full, part


def _half(ref, k, half):
    hr, hc = _half_shape(ref.shape[1], ref.shape[2])
    if _by_rows(ref.shape[1]):
        return ref.at[k, pl.ds(pl.multiple_of(half * hr, 16), hr), :]
    return ref.at[k, :, pl.ds(pl.multiple_of(half * hc, LANE), hc)]


def _shard_blocks(w, br, bc):
    if w.shape[0] == 1:
        def write(ref, v):
            ref[...] = v
        return (lambda f: pl.BlockSpec((None, br, bc), lambda *a: (0, *f(*a)))), (lambda ref: ref[...]), write
    assert w.shape[1] == 1 and br == w.shape[0], w.shape

    def write_rows(ref, v):
        ref[:, 0, :] = v
    return (lambda f: pl.BlockSpec((br, 1, bc), lambda *a: (0, 0, f(*a)[1]))), (lambda ref: ref[:, 0, :]), write_rows


def _stage_shard(name, w, chip):
    rs, cs = w.shape[0] * w.shape[1], w.shape[2]
    (br, bc), n, idx = _blocks2d(rs, cs, 16)
    spec, get, _ = _shard_blocks(w, br, bc)

    def body(chip_ref, w_ref, o_ref):
        o_ref[...] = get(w_ref).astype(BF16)

    return pl.pallas_call(
        body, name="stage_" + name,
        grid_spec=pltpu.PrefetchScalarGridSpec(
            num_scalar_prefetch=1, grid=(n,),
            in_specs=[spec(lambda i, chip_ref: idx(i))],
            out_specs=pl.BlockSpec((None, br, bc), lambda i, chip_ref: (chip_ref[0], *idx(i)))),
        out_shape=jax.ShapeDtypeStruct((N_CHIPS, rs, cs), BF16),
        compiler_params=_params(("parallel",)),
    )(_scalar(chip), w)


def _allgather_weights(bufs):
    n = len(bufs)

    def body(*refs):
        outs, send_sems, recv_sems = refs[n:2 * n], refs[2 * n], refs[2 * n + 1]
        x, y, c = _me()
        chip = 2 * x + y
        sib = (x, y, 1 - c)
        chips = _other_chips(x, y)

        def copy(k, part, to):
            return pltpu.make_async_remote_copy(src_ref=part, dst_ref=part, send_sem=send_sems.at[k], recv_sem=recv_sems.at[k],
                                                device_id=to, device_id_type=MESH_ID)

        started = []
        for w, o_ref in enumerate(outs):
            for j, (cx, cy) in enumerate(chips):
                started.append(copy(6 * w + j, _half(o_ref, chip, c), (cx, cy, c)))
                started[-1].start()
        for w, o_ref in enumerate(outs):
            for j, (cx, cy) in enumerate(chips):
                theirs = _half(o_ref, 2 * cx + cy, c)
                copy(6 * w + j, theirs, sib).wait_recv()
                started.append(copy(6 * w + 3 + j, theirs, sib))
                started[-1].start()
        for w, o_ref in enumerate(outs):
            for j, (cx, cy) in enumerate(chips):
                copy(6 * w + 3 + j, _half(o_ref, 2 * cx + cy, 1 - c), sib).wait_recv()
        for cp in started:
            cp.wait_send()

    return pl.pallas_call(
        body, name="allgather_weights", in_specs=[_ANY] * n, out_specs=[_ANY] * n,
        out_shape=[jax.ShapeDtypeStruct(b.shape, b.dtype) for b in bufs],
        input_output_aliases={i: i for i in range(n)},
        scratch_shapes=[pltpu.SemaphoreType.DMA((6 * n,)), pltpu.SemaphoreType.DMA((6 * n,))],
    )(*bufs)


_HBM = pl.BlockSpec(memory_space=pltpu.HBM)
_SEM = pl.BlockSpec(memory_space=pltpu.SEMAPHORE)
_EFFECT = pltpu.SideEffectType.DATAFLOW_SIDE_EFFECTING


def _split_start(name, bufs, n_copies, copies, after):
    n = len(bufs)

    def body(*refs):
        for cp in copies(refs[:n], refs[n + 1], refs[n + 2]):
            cp.start()
        refs[-1][...] = jnp.zeros_like(refs[-1])

    res = pl.pallas_call(
        body, name=name,
        out_shape=(pltpu.SemaphoreType.DMA((n_copies,)), pltpu.SemaphoreType.DMA((n_copies,)),
                   *[pltpu.HBM(b.shape, b.dtype) for b in bufs], jax.ShapeDtypeStruct((8, LANE), F32)),
        in_specs=[_HBM] * n + [_ANY], out_specs=(_SEM, _SEM, *[_HBM] * n, pl.BlockSpec(memory_space=pltpu.VMEM)),
        input_output_aliases={i: 2 + i for i in range(n)},
        compiler_params=pltpu.CompilerParams(has_side_effects=_EFFECT),
    )(*[pltpu.with_memory_space_constraint(b, pltpu.HBM) for b in bufs], after)
    return res[0], res[1], list(res[2:2 + n]), res[-1]


def _split_wait(name, send_sems, recv_sems, bufs, after, copies):
    n = len(bufs)

    def body(*refs):
        for cp in copies(refs[:n], refs[n], refs[n + 1]):
            cp.wait_send()
            cp.wait_recv()

    return list(pl.pallas_call(
        body, name=name, out_shape=[pltpu.HBM(b.shape, b.dtype) for b in bufs],
        in_specs=[_HBM] * n + [_SEM, _SEM, _ANY], out_specs=[_HBM] * n,
        input_output_aliases={i: i for i in range(n)},
        compiler_params=pltpu.CompilerParams(has_side_effects=_EFFECT),
    )(*bufs, send_sems, recv_sems, after))


def _gather_to_chips(bufs, send_sems, recv_sems):
    x, y, c = _me()
    return [pltpu.make_async_remote_copy(src_ref=_half(b, 2 * x + y, c), dst_ref=_half(b, 2 * x + y, c),
                                         send_sem=send_sems.at[3 * w + j], recv_sem=recv_sems.at[3 * w + j],
                                         device_id=(cx, cy, c), device_id_type=MESH_ID)
            for w, b in enumerate(bufs) for j, (cx, cy) in enumerate(_other_chips(x, y))]


def _gather_to_sibling(bufs, send_sems, recv_sems):
    x, y, c = _me()
    return [pltpu.make_async_remote_copy(src_ref=_half(b, 2 * cx + cy, c), dst_ref=_half(b, 2 * cx + cy, c),
                                         send_sem=send_sems.at[3 * w + j], recv_sem=recv_sems.at[3 * w + j],
                                         device_id=(x, y, 1 - c), device_id_type=MESH_ID)
            for w, b in enumerate(bufs) for j, (cx, cy) in enumerate(_other_chips(x, y))]


def _pair_exchange(name, grads):
    n = len(grads)

    def body(*refs):
        ins, outs, send_sems, recv_sems = refs[:n], refs[n:2 * n], refs[2 * n], refs[2 * n + 1]
        x, y, c = _me()
        cps = []
        for w, (g_ref, o_ref) in enumerate(zip(ins, outs)):
            cps.append(pltpu.make_async_remote_copy(src_ref=_half(g_ref, slice(None), 1 - c), dst_ref=o_ref,
                                                    send_sem=send_sems.at[w], recv_sem=recv_sems.at[w],
                                                    device_id=(x, y, 1 - c), device_id_type=MESH_ID))
            cps[-1].start()
        for cp in cps:
            cp.wait()

    return pl.pallas_call(
        body, name="pair_exchange_" + name, in_specs=[_ANY] * n, out_specs=[_ANY] * n,
        out_shape=[jax.ShapeDtypeStruct((g.shape[0], *_half_shape(g.shape[1], g.shape[2])), g.dtype) for g in grads],
        scratch_shapes=[pltpu.SemaphoreType.DMA((n,)), pltpu.SemaphoreType.DMA((n,))],
    )(*grads)


def _pair_copies(grads, lands, send_sems, recv_sems):
    x, y, c = _me()
    return [pltpu.make_async_remote_copy(src_ref=_half(g_ref, slice(None), 1 - c), dst_ref=l_ref, send_sem=send_sems.at[w],
                                         recv_sem=recv_sems.at[w], device_id=(x, y, 1 - c), device_id_type=MESH_ID)
            for w, (g_ref, l_ref) in enumerate(zip(grads, lands))]


def _pair_exchange_start(name, grads):
    n = len(grads)
    lands = [lax.empty((g.shape[0], *_half_shape(g.shape[1], g.shape[2])), g.dtype) for g in grads]
    send_sems, recv_sems, bufs, token = _split_start(
        "pair_exchange_start_" + name, [*grads, *lands], n, lambda refs, ss, rs: _pair_copies(refs[:n], refs[n:], ss, rs),
        jnp.zeros((8, LANE), F32))
    return (send_sems, recv_sems, bufs), token


def _pair_exchange_wait(name, state, after):
    send_sems, recv_sems, bufs = state
    n = len(bufs) // 2
    bufs = _split_wait("pair_exchange_wait_" + name, send_sems, recv_sems, bufs, after,
                       lambda refs, ss, rs: _pair_copies(refs[:n], refs[n:], ss, rs))
    return bufs[:n], bufs[n:]


def _pair_sum(name, g, theirs, c):
    (br, bc), nb, full, part = _half_blocks(g.shape[1], g.shape[2], 16)

    def body(c_ref, a_ref, b_ref, o_ref):
        o_ref[...] = (a_ref[...].astype(F32) + b_ref[...].astype(F32)).astype(o_ref.dtype)

    return pl.pallas_call(
        body, name="pair_sum_" + name,
        grid_spec=pltpu.PrefetchScalarGridSpec(
            num_scalar_prefetch=1, grid=(N_CHIPS, nb),
            in_specs=[pl.BlockSpec((None, br, bc), lambda k, i, c_ref: (k, *full(c_ref[0], i))),
                      pl.BlockSpec((None, br, bc), lambda k, i, c_ref: (k, *part(i)))],
            out_specs=pl.BlockSpec((None, br, bc), lambda k, i, c_ref: (k, *part(i)))),
        out_shape=jax.ShapeDtypeStruct(theirs.shape, BF16),
        compiler_params=_params(("parallel", "parallel")),
    )(_scalar(c), g, theirs)


def _chip_copies(srcs, lands, send_sems, recv_sems):
    x, y, c = _me()
    return [pltpu.make_async_remote_copy(src_ref=s_ref.at[2 * cx + cy], dst_ref=l_ref.at[j], send_sem=send_sems.at[3 * w + j],
                                         recv_sem=recv_sems.at[3 * w + j], device_id=(cx, cy, c), device_id_type=MESH_ID)
            for w, (s_ref, l_ref) in enumerate(zip(srcs, lands)) for j, (cx, cy) in enumerate(_other_chips(x, y))]


def _chip_exchange_start(name, sums):
    n = len(sums)
    lands = [lax.empty((3,) + s.shape[1:], s.dtype) for s in sums]
    send_sems, recv_sems, bufs, token = _split_start(
        "chip_exchange_start_" + name, [*sums, *lands], 3 * n, lambda refs, ss, rs: _chip_copies(refs[:n], refs[n:], ss, rs),
        jnp.zeros((8, LANE), F32))
    return send_sems, recv_sems, bufs[:n], bufs[n:], token


def _chip_exchange_wait(name, send_sems, recv_sems, sums, lands, after):
    n = len(sums)
    bufs = _split_wait("chip_exchange_wait_" + name, send_sems, recv_sems, [*sums, *lands], after,
                       lambda refs, ss, rs: _chip_copies(refs[:n], refs[n:], ss, rs))
    return bufs[:n], bufs[n:]


def _chip_sum(name, sums, theirs, chip):
    _, h, cs = sums.shape
    (br, bc), nb, idx = _blocks2d(h, cs, 16)

    def body(chip_ref, s_ref, t_ref, o_ref):
        acc = s_ref[...].astype(F32)
        for k in range(3):
            acc = acc + t_ref[k].astype(F32)
        o_ref[...] = acc

    return pl.pallas_call(
        body, name="chip_sum_" + name,
        grid_spec=pltpu.PrefetchScalarGridSpec(
            num_scalar_prefetch=1, grid=(nb,),
            in_specs=[pl.BlockSpec((None, br, bc), lambda i, chip_ref: (chip_ref[0], *idx(i))),
                      pl.BlockSpec((3, br, bc), lambda i, chip_ref: (0, *idx(i)))],
            out_specs=pl.BlockSpec((br, bc), lambda i, chip_ref: idx(i))),
        out_shape=jax.ShapeDtypeStruct((h, cs), F32),
        compiler_params=_params(("parallel",)),
    )(_scalar(chip), sums, theirs)


def _sibling_copies(halves, lands, send_sems, recv_sems):
    x, y, c = _me()
    return [pltpu.make_async_remote_copy(src_ref=h_ref, dst_ref=l_ref, send_sem=send_sems.at[w], recv_sem=recv_sems.at[w],
                                         device_id=(x, y, 1 - c), device_id_type=MESH_ID)
            for w, (h_ref, l_ref) in enumerate(zip(halves, lands))]


def _sibling_exchange_start(name, halves):
    n = len(halves)
    lands = [lax.empty(h.shape, h.dtype) for h in halves]
    send_sems, recv_sems, bufs, token = _split_start(
        "sibling_exchange_start_" + name, [*halves, *lands], n, lambda refs, ss, rs: _sibling_copies(refs[:n], refs[n:], ss, rs),
        jnp.zeros((8, LANE), F32))
    return (send_sems, recv_sems, bufs), token


def _sibling_exchange_wait(name, state, after):
    send_sems, recv_sems, bufs = state
    n = len(bufs) // 2
    bufs = _split_wait("sibling_exchange_wait_" + name, send_sems, recv_sems, bufs, after,
                       lambda refs, ss, rs: _sibling_copies(refs[:n], refs[n:], ss, rs))
    return bufs[:n], bufs[n:]


def _sibling_exchange(halves):
    n = len(halves)

    def body(*refs):
        ins, outs, send_sems, recv_sems = refs[:n], refs[n:2 * n], refs[2 * n], refs[2 * n + 1]
        x, y, c = _me()
        cps = []
        for w, (h_ref, o_ref) in enumerate(zip(ins, outs)):
            cps.append(pltpu.make_async_remote_copy(src_ref=h_ref, dst_ref=o_ref, send_sem=send_sems.at[w], recv_sem=recv_sems.at[w],
                                                    device_id=(x, y, 1 - c), device_id_type=MESH_ID))
            cps[-1].start()
        for cp in cps:
            cp.wait()

    return pl.pallas_call(
        body, name="grad_sibling_exchange", in_specs=[_ANY] * n, out_specs=[_ANY] * n,
        out_shape=[jax.ShapeDtypeStruct(h.shape, h.dtype) for h in halves],
        scratch_shapes=[pltpu.SemaphoreType.DMA((n,)), pltpu.SemaphoreType.DMA((n,))],
    )(*halves)


def _allreduce_small(name, vec):
    def body(v_ref, o_ref, buf_ref, send_sems, recv_sems):
        x, y, c = _me()
        me = 4 * x + 2 * y + c
        cps = []
        for p in range(1, 8):
            px, py, pc = x ^ (p >> 2), y ^ ((p >> 1) & 1), c ^ (p & 1)
            cps.append(pltpu.make_async_remote_copy(src_ref=v_ref, dst_ref=buf_ref.at[me], send_sem=send_sems.at[p - 1],
                                                    recv_sem=recv_sems.at[p - 1], device_id=(px, py, pc), device_id_type=MESH_ID))
            cps[-1].start()
        buf_ref[me] = v_ref[...]
        for p in range(1, 8):
            theirs = buf_ref.at[me ^ p]
            pltpu.make_async_remote_copy(src_ref=theirs, dst_ref=theirs, send_sem=send_sems.at[p - 1], recv_sem=recv_sems.at[p - 1],
                                         device_id=(x, y, c), device_id_type=MESH_ID).wait_recv()
        for cp in cps:
            cp.wait_send()
        acc = buf_ref[0]
        for k in range(1, 8):
            acc = acc + buf_ref[k]
        o_ref[...] = acc

    vm = pl.BlockSpec(memory_space=pltpu.VMEM)
    return pl.pallas_call(
        body, name=name, in_specs=[vm], out_specs=vm, out_shape=jax.ShapeDtypeStruct(vec.shape, F32),
        scratch_shapes=[pltpu.VMEM((8,) + vec.shape, F32), pltpu.SemaphoreType.DMA((7,)), pltpu.SemaphoreType.DMA((7,))],
    )(vec)


def _adam_math(w, g, m, v):
    m = ADAM_B1 * m + (1.0 - ADAM_B1) * g
    v = ADAM_B2 * v + (1.0 - ADAM_B2) * (g * g)
    m_hat = m / (1.0 - ADAM_B1 ** ADAM_STEP)
    v_hat = v / (1.0 - ADAM_B2 ** ADAM_STEP)
    return -ADAM_LR * (m_hat / (jnp.sqrt(v_hat) + ADAM_EPS) + ADAM_WD * w), m, v


def _adamw(name, w, g, m, v):
    R, C = w.shape
    tr = _row_block(R, C, 8)

    def body(w_ref, g_ref, m_ref, v_ref, d_ref, nm_ref, nv_ref):
        d_ref[...], nm_ref[...], nv_ref[...] = _adam_math(w_ref[...], g_ref[...], m_ref[...], v_ref[...])

    blk = pl.BlockSpec((tr, C), lambda i: (i, 0))
    return pl.pallas_call(
        body, name=name, grid=(R // tr,), in_specs=[blk] * 4, out_specs=[blk] * 3,
        out_shape=[jax.ShapeDtypeStruct((R, C), F32)] * 3, compiler_params=_params(("parallel",)),
    )(w, g, m, v)


def _adamw_halves(name, w, mine, theirs, m, v, c):
    rs, cs = w.shape[0] * w.shape[1], w.shape[2]
    (br, bc), nb, whole, half = _half_blocks(rs, cs, 8)
    spec, get, put = _shard_blocks(w, br, bc)

    def body(c_ref, w_ref, a_ref, b_ref, m_ref, v_ref, g_ref, d_ref, nm_ref, nv_ref):
        g = jnp.where(pl.program_id(0) == c_ref[0], a_ref[...], b_ref[...])
        put(g_ref, g)
        for ref, val in zip((d_ref, nm_ref, nv_ref), _adam_math(get(w_ref), g, get(m_ref), get(v_ref))):
            put(ref, val)

    full = spec(lambda s, i, c_ref: whole(s, i))
    part = pl.BlockSpec((br, bc), lambda s, i, c_ref: half(i))
    return pl.pallas_call(
        body, name=name,
        grid_spec=pltpu.PrefetchScalarGridSpec(num_scalar_prefetch=1, grid=(2, nb), in_specs=[full, part, part, full, full],
                                               out_specs=[full] * 4),
        out_shape=[jax.ShapeDtypeStruct(w.shape, F32)] * 4, compiler_params=_params(("parallel", "parallel")),
    )(_scalar(c), w, mine, theirs, m, v)


def _pack_small(arrs):
    flat = jnp.concatenate([a.reshape(-1) for a in arrs])
    n = -(-flat.shape[0] // (8 * LANE)) * 8 * LANE
    return jnp.pad(flat, (0, n - flat.shape[0])).reshape(8, n // 8)


def _unpack_small(vec, shapes):
    flat, out, off = vec.reshape(-1), [], 0
    for s in shapes:
        out.append(flat[off:off + s[0] * s[1]].reshape(s))
        off += s[0] * s[1]
    return out


class _LateWeights:
    def __init__(self, cfg, tag, names, staged, after):
        self.cfg, self.tag, self.names, self.k = cfg, tag, names, 3 * len(names)
        self.send, self.recv, self.bufs, self.token = _split_start(f"gather_{tag}_chips_start", staged, self.k, _gather_to_chips,
                                                                    after)

    def pass_on(self, after):
        bufs = _split_wait(f"gather_{self.tag}_chips_wait", self.send, self.recv, self.bufs, after, _gather_to_chips)
        self.send, self.recv, self.bufs, token = _split_start(f"gather_{self.tag}_sibling_start", bufs, self.k, _gather_to_sibling,
                                                               self.token)
        return token

    def arrived(self, after):
        bufs = _split_wait(f"gather_{self.tag}_sibling_wait", self.send, self.recv, self.bufs, after, _gather_to_sibling)
        return {n: _gathered_to_kernel(self.cfg, n, b) for n, b in zip(self.names, bufs)}


def _step(cfg, a):
    chip = 2 * lax.axis_index("x") + lax.axis_index("y")
    core = lax.axis_index("c")
    big = BIG

    ffn = ("w_gate", "w_up", "w_down")
    first = ("w_in", "w_uq", "w_ukv")
    staged = {n: _stage_shard(n, a[n], chip) for n in big}
    gathered = _allgather_weights([staged[n] for n in first])
    W = {n: _gathered_to_kernel(cfg, n, wg) for n, wg in zip(first, gathered)}

    sp = {n: a[n] for n in SMALL}
    sharded = _pack_small([a[n] for n in SMALL_SHARDED])
    slot = jnp.where(lax.broadcasted_iota(I32, (N_CHIPS,) + sharded.shape, 0) == chip, 0.5 * sharded[None], 0.0)
    allp = _allreduce_small("allgather_small", slot.reshape(N_CHIPS * 8, -1)).reshape((N_CHIPS,) + sharded.shape)
    per_chip = [_unpack_small(allp[ch], [a[n].shape for n in SMALL_SHARDED]) for ch in range(N_CHIPS)]
    for k, n in enumerate(SMALL_SHARDED):
        sp[n] = jnp.concatenate([per_chip[ch][k] for ch in range(N_CHIPS)], axis=1)

    behind, _ = lax.optimization_barrier((gathered[0][0, :8, :LANE], allp))
    out_weight = _LateWeights(cfg, "out", ("w_out",), [staged["w_out"]], behind)
    ffn_weights = _LateWeights(cfg, "ffn", ffn, [staged[n] for n in ffn], out_weight.token)
    sp["mix_pre_g"] = sp["mix_pre_g"] + (out_weight.token[0, 0] + ffn_weights.token[0, 0])

    state = {}

    def ffn_grads_ready(grads):
        state["ffn_pairs"], token = _pair_exchange_start("ffn", [_grad_to_chips(cfg, n, grads[n]) for n in ffn_grads])
        return token

    def pair_sums(names, grads, theirs):
        return [_pair_sum(n, g, t, core) for n, g, t in zip(names, grads, theirs)]

    def early_grads_ready(grads):
        g_out = [_grad_to_chips(cfg, "w_out", grads["w_out"])]
        g_ffn, t_ffn = _pair_exchange_wait("ffn", state["ffn_pairs"], g_out[0])
        sums = pair_sums(ffn_grads, g_ffn, t_ffn) + pair_sums(["w_out"], g_out, _pair_exchange("out", g_out))
        state["early"] = _chip_exchange_start("early", sums)
        return state["early"][-1]

    def reduced_halves(tag, names, after):
        send_sems, recv_sems, s_bufs, l_bufs, _ = state[tag]
        s_bufs, l_bufs = _chip_exchange_wait(tag, send_sems, recv_sems, s_bufs, l_bufs, after)
        return [_chip_sum(n, s, t, chip) for n, s, t in zip(names, s_bufs, l_bufs)]

    def in_grad_ready(grads):
        grads = [_grad_to_chips(cfg, n, grads[n]) for n in first]
        state["rest"] = _chip_exchange_start("rest", pair_sums(first, grads, _pair_exchange("rest", grads)))
        state["early_mine"] = reduced_halves("early", early, state["rest"][-1])
        state["early_siblings"], token = _sibling_exchange_start("early", state["early_mine"])
        return state["rest"][-1] + token

    ffn_grads = ("w_down", "w_gate", "w_up")
    early = ffn_grads + ("w_out",)
    loss, grad_x, gW, gs = _local_grads(cfg, a["x"], a["loss_target"], W, sp, out_weight, ffn_weights,
                                        ffn_grads_ready, early_grads_ready, in_grad_ready)
    mine = dict(zip(first, reduced_halves("rest", first, grad_x)))
    theirs = dict(zip(first, _sibling_exchange([mine[n] for n in first])))
    e_mine, e_theirs = _sibling_exchange_wait("early", state["early_siblings"], theirs[first[0]])
    mine.update(zip(early, e_mine))
    theirs.update(zip(early, e_theirs))

    shapes = [gs[n].shape for n in SMALL] + [(1, LANE)]
    vec, _ = lax.optimization_barrier((_pack_small([gs[n] for n in SMALL] + [loss]), e_theirs[0]))
    red = _unpack_small(_allreduce_small("allreduce_small", vec), shapes)
    g_small = dict(zip(SMALL, red[:-1]))
    for n in SMALL_SHARDED:
        cs = a[n].shape[1]
        g_small[n] = lax.dynamic_slice_in_dim(g_small[n], chip * cs, cs, axis=1)

    out = {"loss": red[-1][0, 0], "grad_x": grad_x}
    for n in big:
        out["grad_" + n], out["delta_" + n], out["new_m_" + n], out["new_v_" + n] = _adamw_halves(
            "adamw_" + n, a[n], mine[n], theirs[n], a["m_" + n], a["v_" + n], core)
    sshapes = [a[n].shape for n in SMALL]
    d, nm, nv = _adamw("adamw_small", _pack_small([a[n] for n in SMALL]), _pack_small([g_small[n] for n in SMALL]),
                       _pack_small([a["m_" + n] for n in SMALL]), _pack_small([a["v_" + n] for n in SMALL]))
    for n, dd, mm, vv in zip(SMALL, _unpack_small(d, sshapes), _unpack_small(nm, sshapes), _unpack_small(nv, sshapes)):
        out["grad_" + n], out["delta_" + n], out["new_m_" + n], out["new_v_" + n] = g_small[n], dd, mm, vv
    return out


def kernel(x, mix_pre_g, w_in, q_norm_g, w_uq, kv_norm_g, w_ukv, ssm_conv_w, ssm_conv_b, dt_bias, a_log, d_skip, ssm_norm_g, w_out, mix_post_g, ffn_pre_g, w_gate, w_up, ffn_conv_w, ffn_conv_b, w_down, ffn_post_g, loss_target, m_mix_pre_g, m_w_in, m_q_norm_g, m_w_uq, m_kv_norm_g, m_w_ukv, m_ssm_conv_w, m_ssm_conv_b, m_dt_bias, m_a_log, m_d_skip, m_ssm_norm_g, m_w_out, m_mix_post_g, m_ffn_pre_g, m_w_gate, m_w_up, m_ffn_conv_w, m_ffn_conv_b, m_w_down, m_ffn_post_g, v_mix_pre_g, v_w_in, v_q_norm_g, v_w_uq, v_kv_norm_g, v_w_ukv, v_ssm_conv_w, v_ssm_conv_b, v_dt_bias, v_a_log, v_d_skip, v_ssm_norm_g, v_w_out, v_mix_post_g, v_ffn_pre_g, v_w_gate, v_w_up, v_ffn_conv_w, v_ffn_conv_b, v_w_down, v_ffn_post_g):
    args = dict(locals())
    def given(k, v):
        if k in ("w_in", "m_w_in", "v_w_in"):
            return jnp.transpose(v, (2, 0, 1))
        return v if k.removeprefix("m_").removeprefix("v_") in BIG or v.ndim < 3 else v[0]

    out = _step(_FULL, {k: given(k, v) for k, v in args.items()})
    res = [out["loss"], out["grad_x"][None]]
    for pre in ("grad_", "delta_", "new_m_", "new_v_"):
        for n in WEIGHTS:
            o = out[pre + n]
            res.append(jnp.transpose(o, (1, 2, 0)) if n == "w_in" else o if n in BIG or args[n].ndim < 3 else o[None])
    return tuple(res)
```

```python
import functools
import math

import jax
import jax.numpy as jnp
from jax import lax
from jax.experimental import pallas as pl
from jax.experimental.pallas import tpu as pltpu

F32, BF16, I32 = jnp.float32, jnp.bfloat16, jnp.int32
NN = (((1,), (0,)), ((), ()))
NT = (((1,), (1,)), ((), ()))
TN = (((0,), (0,)), ((), ()))
HI = lax.Precision.HIGHEST
MESH_ID = pl.DeviceIdType.MESH

EPS = 1e-6
CHUNK = 64
NOPE, ROPE, VH = 128, 64, 128
ROPE_THETA = 10000.0
HP, NST = 64, 128
SSM_K, FFN_K = 4, 3
LANE = 128
N_CHIPS = 4
VMEM_LIMIT = 52 * 1024 * 1024
MM_TILE, MM_TILE_K = 1408, 2816

ADAM_LR, ADAM_B1, ADAM_B2, ADAM_EPS, ADAM_WD, ADAM_STEP = 0.001, 0.9, 0.999, 1e-08, 0.01, 10


class _Cfg:
    def __init__(self, S, D, QL, KVL, H, HS, G, DFF, T):
        self.S, self.D, self.QL, self.KVL, self.H, self.HS, self.G, self.DFF, self.T = S, D, QL, KVL, H, HS, G, DFF, T
        self.INNER = HS * HP
        self.CONVCH = self.INNER + 2 * G * NST
        self.QW = H * (NOPE + ROPE)
        self.KVW = H * (NOPE + VH)
        self.MLAW = H * VH
        self.MIXW = self.MLAW + self.INNER
        self.IN_COLS = QL + KVL + ROPE + self.INNER + self.CONVCH + HS
        self.o_kr = QL + KVL
        self.o_z = self.o_kr + LANE
        self.o_xbc = self.o_z + self.INNER
        self.o_dt = self.o_xbc + self.CONVCH
        self.EXT = self.o_dt + LANE
        self.NPAIR = HS // 2
        self.REP = HS // G


_FULL = _Cfg(S=2048, D=2048, QL=768, KVL=512, H=8, HS=16, G=2, DFF=5632, T=256)
BIG = ("w_in", "w_uq", "w_ukv", "w_out", "w_gate", "w_up", "w_down")

SMALL = ("mix_pre_g", "q_norm_g", "kv_norm_g", "ssm_conv_w", "ssm_conv_b", "dt_bias", "a_log", "d_skip", "ssm_norm_g",
         "mix_post_g", "ffn_pre_g", "ffn_conv_w", "ffn_conv_b", "ffn_post_g")
SMALL_SHARDED = ("ssm_conv_w", "ffn_conv_w")
WEIGHTS = ("mix_pre_g", "w_in", "q_norm_g", "w_uq", "kv_norm_g", "w_ukv", "ssm_conv_w", "ssm_conv_b", "dt_bias", "a_log",
           "d_skip", "ssm_norm_g", "w_out", "mix_post_g", "ffn_pre_g", "w_gate", "w_up", "ffn_conv_w", "ffn_conv_b",
           "w_down", "ffn_post_g")


def _pick(n, target, mult):
    best = None
    for d in range(mult, min(n, target) + 1, mult):
        if n % d == 0:
            best = d
    return best if best is not None else n


def _params(sem=None):
    kw = dict(vmem_limit_bytes=VMEM_LIMIT)
    if sem is not None:
        kw["dimension_semantics"] = sem
    return pltpu.CompilerParams(**kw)


def _dot(a, b, dims=NN, precision=None):
    return lax.dot_general(a, b, dims, preferred_element_type=F32, precision=precision)


def _sigmoid(x):
    return 1.0 / (1.0 + jnp.exp(-x))


def _rs(x):
    return lax.rsqrt(jnp.mean(x * x, axis=-1, keepdims=True) + EPS)


def _rms_back(xh, r, dn):
    return r * (dn - xh * jnp.mean(dn * xh, axis=-1, keepdims=True))


def _colsum(v):
    return jnp.sum(v, axis=0, keepdims=True)


def _matmul(name, a, b, mode, out_dtype, a2=None, b2=None, chips=False):
    cs = None
    if mode == "nn":
        (M, K), N = a.shape, b.shape[-1]
        if chips:
            cs, N = N, N_CHIPS * N
    elif mode == "nt":
        (M, K), N = a.shape, b.shape[-2]
        if chips:
            cs = b.shape[-1]
    else:
        (K, M), N = a.shape, b.shape[1]
        if chips:
            cs = N // N_CHIPS
    tm = _pick(M, MM_TILE, LANE)
    tn = _pick(cs if chips and mode != "nt" else N, MM_TILE, LANE)
    tk = _pick(cs, MM_TILE, LANE) if chips and mode == "nt" else _pick(K, MM_TILE_K, LANE)
    nk = K // tk
    dims = {"nn": NN, "nt": NT, "tn": TN}[mode]
    a_spec = pl.BlockSpec((tk, tm), lambda i, j, k: (k, i)) if mode == "tn" else pl.BlockSpec((tm, tk), lambda i, j, k: (i, k))
    b_spec = pl.BlockSpec((tn, tk), lambda i, j, k: (j, k)) if mode == "nt" else pl.BlockSpec((tk, tn), lambda i, j, k: (k, j))
    o_spec = pl.BlockSpec((tm, tn), lambda i, j, k: (i, j))
    o_shape = (M, N)
    if chips and mode == "nn":
        per = cs // tn
        b_spec = pl.BlockSpec((None, tk, tn), lambda i, j, k: (j // per, k, j % per))
    elif chips and mode == "nt":
        per = cs // tk
        b_spec = pl.BlockSpec((None, tn, tk), lambda i, j, k: (k // per, j, k % per))
    elif chips:
        per = cs // tn
        o_spec = pl.BlockSpec((None, tm, tn), lambda i, j, k: (j // per, i, j % per))
        o_shape = (N_CHIPS, M, cs)
    two = a2 is not None

    def product(refs):
        part = _dot(refs[0][...].astype(BF16), refs[1][...].astype(BF16), dims)
        if two:
            part += _dot(refs[2][...].astype(BF16), refs[3][...].astype(BF16), dims)
        return part

    def body_whole_k(*refs):
        refs[-1][...] = product(refs).astype(refs[-1].dtype)

    def body(*refs):
        o_ref, acc_ref = refs[-2], refs[-1]
        k = pl.program_id(2)

        @pl.when(k == 0)
        def _():
            acc_ref[...] = product(refs)

        @pl.when(k > 0)
        def _():
            acc_ref[...] += product(refs)

        @pl.when(k == nk - 1)
        def _():
            o_ref[...] = acc_ref[...].astype(o_ref.dtype)

    ins = (a, b, a2, b2) if two else (a, b)
    return pl.pallas_call(
        body_whole_k if nk == 1 else body, name=name, grid=(M // tm, N // tn, nk),
        in_specs=[a_spec, b_spec] * (2 if two else 1),
        out_specs=o_spec,
        out_shape=jax.ShapeDtypeStruct(o_shape, out_dtype),
        scratch_shapes=[] if nk == 1 else [pltpu.VMEM((tm, tn), F32)],
        compiler_params=_params(("parallel", "parallel", "arbitrary")),
    )(*ins)


def _rowwise(name, fn, rows, mats, outs, reds, ts):
    S = rows[0].shape[0]
    nr, nm, no = len(rows), len(mats), len(outs)

    def body(*refs):
        res = fn(*[r[...] for r in refs[:nr + nm]])
        res = res if isinstance(res, (tuple, list)) else (res,)
        for r, v in zip(refs[nr + nm:nr + nm + no], res[:no]):
            r[...] = v.astype(r.dtype)
        first = pl.program_id(0) == 0
        for r, v in zip(refs[nr + nm + no:], res[no:]):
            @pl.when(first)
            def _():
                r[...] = jnp.broadcast_to(v, r.shape)

            @pl.when(jnp.logical_not(first))
            def _():
                r[...] += jnp.broadcast_to(v, r.shape)

    in_specs = [pl.BlockSpec((ts, a.shape[1]), lambda i: (i, 0)) for a in rows]
    in_specs += [pl.BlockSpec(m.shape, lambda i, nd=m.ndim: (0,) * nd) for m in mats]
    out_specs = [pl.BlockSpec((ts, w), lambda i: (i, 0)) for w, _ in outs]
    out_specs += [pl.BlockSpec(s, lambda i: (0, 0)) for s in reds]
    out_shape = [jax.ShapeDtypeStruct((S, w), dt) for w, dt in outs] + [jax.ShapeDtypeStruct(s, F32) for s in reds]
    return pl.pallas_call(
        body, name=name, grid=(S // ts,), in_specs=in_specs, out_specs=out_specs, out_shape=out_shape,
        compiler_params=_params(("arbitrary",) if reds else ("parallel",)),
    )(*rows, *mats)


def _shift_down(v, s):
    if s == 0:
        return v
    rows = lax.broadcasted_iota(I32, v.shape, 0)
    return jnp.where(rows >= s, pltpu.roll(v, s, 0), 0.0)


def _shift_up(v, s):
    if s == 0:
        return v
    n = v.shape[0]
    rows = lax.broadcasted_iota(I32, v.shape, 0)
    return jnp.where(rows < n - s, pltpu.roll(v, n - s, 0), 0.0)


def _conv(x, w, b):
    K = w.shape[0]
    y = jnp.broadcast_to(b, x.shape)
    for k in range(K):
        y = y + w[k:k + 1, :] * _shift_down(x, K - 1 - k)
    return y


def _conv_back(x, w, dc):
    K = w.shape[0]
    dx = jnp.zeros_like(x)
    dw = []
    for k in range(K):
        dx = dx + w[k:k + 1, :] * _shift_up(dc, K - 1 - k)
        dw.append(_colsum(dc * _shift_down(x, K - 1 - k)))
    return dx, jnp.concatenate(dw, axis=0), _colsum(dc)


def _colwise(name, fn, cols, vecs, outs, pouts, tc):
    S, C = cols[0].shape
    nc_, nv, no = len(cols), len(vecs), len(outs)

    def body(*refs):
        res = fn(*[r[...] for r in refs[:nc_ + nv]])
        res = res if isinstance(res, (tuple, list)) else (res,)
        for r, v in zip(refs[nc_ + nv:], res):
            r[...] = v.astype(r.dtype)

    in_specs = [pl.BlockSpec((S, tc), lambda j: (0, j)) for _ in cols]
    in_specs += [pl.BlockSpec((v.shape[0], tc), lambda j: (0, j)) for v in vecs]
    out_specs = [pl.BlockSpec((S, tc), lambda j: (0, j)) for _ in outs] + [pl.BlockSpec((k, tc), lambda j: (0, j)) for k in pouts]
    out_shape = [jax.ShapeDtypeStruct((S, C), dt) for dt in outs] + [jax.ShapeDtypeStruct((k, C), F32) for k in pouts]
    return pl.pallas_call(
        body, name=name, grid=(C // tc,), in_specs=in_specs, out_specs=out_specs, out_shape=out_shape,
        compiler_params=_params(("parallel",)),
    )(*cols, *vecs)


_G0, _G1 = math.sqrt(2.0 / math.pi), 0.044715


def _gelu(g):
    th = jnp.tanh(_G0 * (g + _G1 * g * g * g))
    return 0.5 * g * (1.0 + th), th


def _ffn_act(gate_pre, up, w, b):
    act, _ = _gelu(_conv(gate_pre, w, b))
    return act * up


def _ffn_act_back(dact, gate_pre, up, w, b):
    g = _conv(gate_pre, w, b)
    ge, th = _gelu(g)
    dge = 0.5 * (1.0 + th) + 0.5 * g * (1.0 - th * th) * _G0 * (1.0 + 3.0 * _G1 * g * g)
    dup = dact * ge
    dgate_pre, dw, db = _conv_back(gate_pre, w, dact * up * dge)
    return dgate_pre, dup, dw, db


def _ssm_act(xbc, w, b):
    c = _conv(xbc, w, b)
    return c * _sigmoid(c)


def _ssm_act_back(dxc, xbc, w, b):
    c = _conv(xbc, w, b)
    sg = _sigmoid(c)
    return _conv_back(xbc, w, dxc * sg * (1.0 + c * (1.0 - sg)))


def _rope_tables(S):
    inv = 1.0 / (ROPE_THETA ** (jnp.arange(0, ROPE, 2, dtype=F32) / ROPE))
    ang = jnp.arange(S, dtype=F32)[:, None] * inv[None, :]
    cos, sin = jnp.cos(ang), jnp.sin(ang)
    return jnp.tile(cos, (1, 4)), jnp.tile(jnp.concatenate([-sin, sin], axis=1), (1, 2))


def _swap_halves(x):
    lane = lax.broadcasted_iota(I32, x.shape, 1)
    w = x.shape[1]
    return jnp.where((lane % ROPE) < ROPE // 2, pltpu.roll(x, w - ROPE // 2, 1), pltpu.roll(x, ROPE // 2, 1))


def _rot(x, cos2, sin2):
    return x * cos2 + _swap_halves(x) * sin2


def _rot_back(dy, cos2, sin2):
    return dy * cos2 + _swap_halves(dy * sin2)


def _mla_pack(cfg, q, kv, kr, cos2, sin2):
    S, H = cfg.S, cfg.H
    ts = _pick(S, 512, 8)

    def body(qn_ref, qr_ref, kn_ref, v_ref, kr_ref, c_ref, s_ref, Q_ref, K_ref, V_ref):
        h = pl.program_id(0)
        c2, s2 = c_ref[...], s_ref[...]
        Q_ref[0, :, 0:LANE] = qn_ref[...].astype(BF16)
        Q_ref[0, :, LANE:] = _rot(qr_ref[...], c2, s2).astype(BF16)
        K_ref[0, :, 0:LANE] = kn_ref[...].astype(BF16)
        krr = _rot(kr_ref[...], c2, s2)
        K_ref[0, :, LANE:] = jnp.where(h % 2 == 1, pltpu.roll(krr, ROPE, 1), krr).astype(BF16)
        V_ref[0] = v_ref[...].astype(BF16)

    blk = lambda f: pl.BlockSpec((ts, LANE), f)
    return pl.pallas_call(
        body, name="mla_pack", grid=(H, S // ts),
        in_specs=[blk(lambda h, i: (i, h)), blk(lambda h, i: (i, H + h // 2)), blk(lambda h, i: (i, h)),
                  blk(lambda h, i: (i, H + h)), blk(lambda h, i: (i, 0)), blk(lambda h, i: (i, 0)), blk(lambda h, i: (i, 0))],
        out_specs=[pl.BlockSpec((1, ts, 2 * LANE), lambda h, i: (h, i, 0)), pl.BlockSpec((1, ts, 2 * LANE), lambda h, i: (h, i, 0)),
                   pl.BlockSpec((1, ts, LANE), lambda h, i: (h, i, 0))],
        out_shape=[jax.ShapeDtypeStruct((H, S, 2 * LANE), BF16), jax.ShapeDtypeStruct((H, S, 2 * LANE), BF16),
                   jax.ShapeDtypeStruct((H, S, LANE), BF16)],
        compiler_params=_params(("parallel", "parallel")),
    )(q, q, kv, kv, kr, cos2, sin2)


def _mla_unpack(cfg, dQ, dK, dV, cos2, sin2):
    S, H = cfg.S, cfg.H
    ts = _pick(S, 256, 8)

    def body(dQ_ref, dK_ref, dV_ref, c_ref, s_ref, dq_ref, dkv_ref, dkr_ref):
        c2, s2 = c_ref[...], s_ref[...]
        lo = lax.broadcasted_iota(I32, (ts, LANE), 1) < ROPE
        tk = jnp.zeros((ts, LANE), F32)
        for h in range(H):
            dq_ref[:, h * LANE:(h + 1) * LANE] = dQ_ref[h, :, 0:LANE].astype(BF16)
            dkv_ref[:, h * LANE:(h + 1) * LANE] = dK_ref[h, :, 0:LANE].astype(BF16)
            dkv_ref[:, (H + h) * LANE:(H + h + 1) * LANE] = dV_ref[h].astype(BF16)
            own = lo if h % 2 == 0 else jnp.logical_not(lo)
            tk = tk + jnp.where(own, dK_ref[h, :, LANE:], 0.0)
        for j in range(H // 2):
            dr = dQ_ref[2 * j, :, LANE:] + dQ_ref[2 * j + 1, :, LANE:]
            dq_ref[:, (H + j) * LANE:(H + j + 1) * LANE] = _rot_back(dr, c2, s2).astype(BF16)
        dkr_rot = jnp.where(lo, tk + pltpu.roll(tk, ROPE, 1), 0.0)
        dkr_ref[...] = _rot_back(dkr_rot, c2, s2).astype(BF16)

    tab = pl.BlockSpec((ts, LANE), lambda i: (i, 0))
    return pl.pallas_call(
        body, name="mla_unpack", grid=(S // ts,),
        in_specs=[pl.BlockSpec((H, ts, 2 * LANE), lambda i: (0, i, 0)), pl.BlockSpec((H, ts, 2 * LANE), lambda i: (0, i, 0)),
                  pl.BlockSpec((H, ts, LANE), lambda i: (0, i, 0)), tab, tab],
        out_specs=[pl.BlockSpec((ts, cfg.QW), lambda i: (i, 0)), pl.BlockSpec((ts, cfg.KVW), lambda i: (i, 0)), tab],
        out_shape=[jax.ShapeDtypeStruct((S, cfg.QW), BF16), jax.ShapeDtypeStruct((S, cfg.KVW), BF16),
                   jax.ShapeDtypeStruct((S, LANE), BF16)],
        compiler_params=_params(("parallel",)),
    )(dQ, dK, dV, cos2, sin2)


_ATT_T = 256
_ATT_HB = 2
_ATT_SCALE = (NOPE + ROPE) ** -0.5


def _diag_mask(transposed=False):
    r = lax.broadcasted_iota(I32, (_ATT_T, _ATT_T), 0) // CHUNK
    c = lax.broadcasted_iota(I32, (_ATT_T, _ATT_T), 1) // CHUNK
    return r <= c if transposed else c <= r


def _row_form(col):
    return jnp.broadcast_to(col, (col.shape[0], LANE)).T[0:8, :]


def _attn_fwd(cfg, Q, K, V):
    S, H, T, HB = cfg.S, cfg.H, _ATT_T, _ATT_HB

    def body(q_ref, k_ref, v_ref, o_ref, lse_ref, lse_t_ref):
        qi = pl.program_id(1)

        def head_step(b, kb, carry, mask):
            m, l, acc = carry
            ks = pl.multiple_of(kb * T, T)
            s = _dot(q_ref[b], k_ref[b, pl.ds(ks, T), :], NT) * _ATT_SCALE
            if mask is not None:
                s = jnp.where(mask, s, -1e30)
            m_new = jnp.maximum(m, jnp.max(s, axis=1, keepdims=True))
            p = jnp.exp(s - m_new)
            alpha = jnp.exp(m - m_new)
            l = alpha * l + jnp.sum(p, axis=1, keepdims=True)
            acc = alpha * acc + _dot(p.astype(BF16), v_ref[b, pl.ds(ks, T), :])
            return m_new, l, acc

        def step(kb, carry, mask=None):
            return tuple(head_step(b, kb, carry[b], mask) for b in range(HB))

        init = (jnp.full((T, 1), -1e30, F32), jnp.zeros((T, 1), F32), jnp.zeros((T, VH), F32))
        done = step(qi, lax.fori_loop(0, qi, step, (init,) * HB), _diag_mask())
        for b, (m, l, acc) in enumerate(done):
            o_ref[:, b * LANE:(b + 1) * LANE] = acc / l
            lse = m + jnp.log(l)
            lse_ref[:, b * LANE:(b + 1) * LANE] = jnp.broadcast_to(lse, (T, LANE))
            lse_t_ref[b] = _row_form(lse)

    return pl.pallas_call(
        body, name="attn_fwd", grid=(H // HB, S // T),
        in_specs=[pl.BlockSpec((HB, T, 2 * LANE), lambda h, i: (h, i, 0)), pl.BlockSpec((HB, S, 2 * LANE), lambda h, i: (h, 0, 0)),
                  pl.BlockSpec((HB, S, LANE), lambda h, i: (h, 0, 0))],
        out_specs=[pl.BlockSpec((T, HB * LANE), lambda h, i: (i, h)), pl.BlockSpec((T, HB * LANE), lambda h, i: (i, h)),
                   pl.BlockSpec((HB, 8, T), lambda h, i: (h, 0, i))],
        out_shape=[jax.ShapeDtypeStruct((S, H * LANE), F32), jax.ShapeDtypeStruct((S, H * LANE), F32),
                   jax.ShapeDtypeStruct((H, 8, S), F32)],
        compiler_params=_params(("parallel", "parallel")),
    )(Q, K, V)


def _attn_dq(cfg, Q, K, V, do, o, lse):
    S, H, T, HB = cfg.S, cfg.H, _ATT_T, _ATT_HB

    def body(q_ref, k_ref, v_ref, do_ref, o_ref, lse_ref, dq_ref, dl_t_ref):
        qi = pl.program_id(1)
        do = [do_ref[:, b * LANE:(b + 1) * LANE] for b in range(HB)]
        delta = [jnp.sum(do[b] * o_ref[:, b * LANE:(b + 1) * LANE], axis=1, keepdims=True) for b in range(HB)]
        dob = [d.astype(BF16) for d in do]

        def head_step(b, kb, dq, mask):
            ks = pl.multiple_of(kb * T, T)
            k = k_ref[b, pl.ds(ks, T), :]
            s = _dot(q_ref[b], k, NT) * _ATT_SCALE
            if mask is not None:
                s = jnp.where(mask, s, -1e30)
            p = jnp.exp(s - lse_ref[:, b * LANE:b * LANE + 1])
            dp = _dot(dob[b], v_ref[b, pl.ds(ks, T), :], NT)
            ds = p * (dp - delta[b]) * _ATT_SCALE
            return dq + _dot(ds.astype(BF16), k)

        def step(kb, dqs, mask=None):
            return tuple(head_step(b, kb, dqs[b], mask) for b in range(HB))

        dqs = step(qi, lax.fori_loop(0, qi, step, (jnp.zeros((T, 2 * LANE), F32),) * HB), _diag_mask())
        for b in range(HB):
            dq_ref[b] = dqs[b]
            dl_t_ref[b] = _row_form(delta[b])

    col = pl.BlockSpec((T, HB * LANE), lambda h, i: (i, h))
    return pl.pallas_call(
        body, name="attn_dq", grid=(H // HB, S // T),
        in_specs=[pl.BlockSpec((HB, T, 2 * LANE), lambda h, i: (h, i, 0)), pl.BlockSpec((HB, S, 2 * LANE), lambda h, i: (h, 0, 0)),
                  pl.BlockSpec((HB, S, LANE), lambda h, i: (h, 0, 0)), col, col, col],
        out_specs=[pl.BlockSpec((HB, T, 2 * LANE), lambda h, i: (h, i, 0)), pl.BlockSpec((HB, 8, T), lambda h, i: (h, 0, i))],
        out_shape=[jax.ShapeDtypeStruct((H, S, 2 * LANE), F32), jax.ShapeDtypeStruct((H, 8, S), F32)],
        compiler_params=_params(("parallel", "parallel")),
    )(Q, K, V, do, o, lse)


def _attn_dkv(cfg, Q, K, V, do, lse_t, delta_t):
    S, H, T, HB = cfg.S, cfg.H, _ATT_T, _ATT_HB
    nq = S // T

    def body(q_ref, k_ref, v_ref, do_ref, lse_ref, dl_ref, dk_ref, dv_ref):
        kb = pl.program_id(1)

        def head_step(b, qi, carry, mask):
            dk, dv = carry
            qs = pl.multiple_of(qi * T, T)
            q = q_ref[b, pl.ds(qs, T), :]
            dob = do_ref[pl.ds(qs, T), b * LANE:(b + 1) * LANE].astype(BF16)
            s = _dot(k_ref[b], q, NT) * _ATT_SCALE
            if mask is not None:
                s = jnp.where(mask, s, -1e30)
            p = jnp.exp(s - lse_ref[b, 0:1, pl.ds(qs, T)])
            dv = dv + _dot(p.astype(BF16), dob)
            dp = _dot(v_ref[b], dob, NT)
            ds = p * (dp - dl_ref[b, 0:1, pl.ds(qs, T)]) * _ATT_SCALE
            dk = dk + _dot(ds.astype(BF16), q)
            return dk, dv

        def step(qi, carry, mask=None):
            return tuple(head_step(b, qi, carry[b], mask) for b in range(HB))

        zero = (jnp.zeros((T, 2 * LANE), F32), jnp.zeros((T, VH), F32))
        done = lax.fori_loop(kb + 1, nq, step, step(kb, (zero,) * HB, _diag_mask(transposed=True)))
        for b, (dk, dv) in enumerate(done):
            dk_ref[b] = dk
            dv_ref[b] = dv

    row = pl.BlockSpec((HB, 8, S), lambda h, j: (h, 0, 0))
    return pl.pallas_call(
        body, name="attn_dkv", grid=(H // HB, S // T),
        in_specs=[pl.BlockSpec((HB, S, 2 * LANE), lambda h, j: (h, 0, 0)), pl.BlockSpec((HB, T, 2 * LANE), lambda h, j: (h, j, 0)),
                  pl.BlockSpec((HB, T, LANE), lambda h, j: (h, j, 0)), pl.BlockSpec((S, HB * LANE), lambda h, j: (0, h)), row, row],
        out_specs=[pl.BlockSpec((HB, T, 2 * LANE), lambda h, j: (h, j, 0)), pl.BlockSpec((HB, T, LANE), lambda h, j: (h, j, 0))],
        out_shape=[jax.ShapeDtypeStruct((H, S, 2 * LANE), F32), jax.ShapeDtypeStruct((H, S, LANE), F32)],
        compiler_params=_params(("parallel", "parallel")),
    )(Q, K, V, do, lse_t, delta_t)


def _expand_matrix(cfg):
    r = lax.broadcasted_iota(I32, (LANE, cfg.INNER), 0)
    c = lax.broadcasted_iota(I32, (LANE, cfg.INNER), 1)
    return (r == c // HP).astype(F32)


def _softplus(x):
    return jnp.maximum(x, 0.0) + jnp.log(1.0 + jnp.exp(-jnp.abs(x)))


def _ssd_prep(cfg, dt_raw, dt_bias_pad, a_log_pad, expand):
    HS = cfg.HS

    def fn(raw, bias, alog, E):
        heads = lax.broadcasted_iota(I32, raw.shape, 1) < HS
        dt = jnp.where(heads, _softplus(raw + bias), 0.0)
        a = dt * jnp.where(heads[0:1], -jnp.exp(alog), 0.0)
        return dt, a, _dot(dt, E, precision=HI), _dot(a, E, precision=HI)

    return _rowwise("ssd_prep", fn, [dt_raw], [dt_bias_pad, a_log_pad, expand],
                    [(LANE, F32), (LANE, F32), (cfg.INNER, F32), (cfg.INNER, F32)], [], _pick(cfg.S, 512, 8))


def _tril(T):
    return lax.broadcasted_iota(I32, (T, T), 0) >= lax.broadcasted_iota(I32, (T, T), 1)


def _ssd_fwd(cfg, xc, dt_exp, a_exp, a_small, dskip_exp):
    S, T, INNER, G, NPAIR = cfg.S, cfg.T, cfg.INNER, cfg.G, cfg.NPAIR
    NC = S // T

    def body(xc_ref, dte_ref, ae_ref, as_ref, dsk_ref, y_ref, hin_ref, ht_ref):
        @pl.when(pl.program_id(0) == 0)
        def _():
            ht_ref[...] = jnp.zeros_like(ht_ref)

        tril = _tril(T)
        tri = tril.astype(F32)
        acs_s = _dot(tri, as_ref[...], precision=HI)
        acs_e = _dot(tri, ae_ref[...], precision=HI)
        acs_t = acs_s.T
        lo = lax.broadcasted_iota(I32, (T, LANE), 1) < HP
        for g in range(G):
            Bb = xc_ref[:, INNER + g * NST:INNER + (g + 1) * NST].astype(BF16)
            Cb = xc_ref[:, INNER + (G + g) * NST:INNER + (G + g + 1) * NST].astype(BF16)
            Gm = _dot(Cb, Bb, NT)
            for j in range(g * NPAIR // G, (g + 1) * NPAIR // G):
                sl = slice(j * LANE, (j + 1) * LANE)
                Xp = xc_ref[:, sl]
                Xdt = Xp * dte_ref[:, sl]
                Xb = Xdt.astype(BF16)
                acs_p = acs_e[:, sl]
                last = acs_p[T - 1:T, :]
                Hin = ht_ref[j]
                hin_ref[0, j] = Hin
                yd = []
                for e in (0, 1):
                    h = 2 * j + e
                    Lm = jnp.exp(jnp.where(tril, acs_s[:, h:h + 1] - acs_t[h:h + 1, :], -1e30))
                    yd.append(_dot((Gm * Lm).astype(BF16), Xb))
                y_off = _dot(Cb, Hin.astype(BF16)) * jnp.exp(acs_p)
                y_ref[:, sl] = jnp.where(lo, yd[0], yd[1]) + y_off + Xp * dsk_ref[:, sl]
                st = _dot(Bb, (Xdt * jnp.exp(last - acs_p)).astype(BF16), TN)
                ht_ref[j] = jnp.exp(last) * Hin + st

    rows = lambda w: pl.BlockSpec((T, w), lambda c: (c, 0))
    return pl.pallas_call(
        body, name="ssd_fwd", grid=(NC,),
        in_specs=[rows(cfg.CONVCH), rows(INNER), rows(INNER), rows(LANE), pl.BlockSpec((1, INNER), lambda c: (0, 0))],
        out_specs=[rows(INNER), pl.BlockSpec((1, NPAIR, NST, LANE), lambda c: (c, 0, 0, 0))],
        out_shape=[jax.ShapeDtypeStruct((S, INNER), F32), jax.ShapeDtypeStruct((NC, NPAIR, NST, LANE), F32)],
        scratch_shapes=[pltpu.VMEM((NPAIR, NST, LANE), F32)],
        compiler_params=_params(("arbitrary",)),
    )(xc, dt_exp, a_exp, a_small, dskip_exp)


def _ssd_bwd(cfg, dy, xc, dt_exp, a_exp, a_small, dskip_exp, hin, dt_raw, dt_bias_pad, a_log_pad, expand):
    S, T, INNER, G, NPAIR, HS = cfg.S, cfg.T, cfg.INNER, cfg.G, cfg.NPAIR, cfg.HS
    NC = S // T

    def body(dy_ref, xc_ref, dte_ref, ae_ref, as_ref, dsk_ref, hin_ref, raw_ref, bias_ref, alog_ref, e_ref,
             dxc_ref, draw_ref, dbias_ref, dalog_ref, dskip_ref, dht_ref, cols_ref, rows_ref, dacs_ref, ddt_ref):
        first = pl.program_id(0) == 0

        @pl.when(first)
        def _():
            dht_ref[...] = jnp.zeros_like(dht_ref)

        tril = _tril(T)
        tri = tril.astype(F32)
        a_s = as_ref[...]
        acs_s = _dot(tri, a_s, precision=HI)
        acs_e = _dot(tri, ae_ref[...], precision=HI)
        acs_t = acs_s.T
        lo = lax.broadcasted_iota(I32, (T, LANE), 1) < HP
        last_row = lax.broadcasted_iota(I32, (T, LANE), 0) == T - 1
        cols_ref[...] = jnp.zeros_like(cols_ref)
        rows_ref[...] = jnp.zeros_like(rows_ref)
        dsk_parts = []
        for g in range(G):
            bsl = slice(INNER + g * NST, INNER + (g + 1) * NST)
            csl = slice(INNER + (G + g) * NST, INNER + (G + g + 1) * NST)
            Bb = xc_ref[:, bsl].astype(BF16)
            Cb = xc_ref[:, csl].astype(BF16)
            Gm = _dot(Cb, Bb, NT)
            dG = jnp.zeros((T, T), F32)
            dB = jnp.zeros((T, NST), F32)
            dC = jnp.zeros((T, NST), F32)
            for j in range(g * NPAIR // G, (g + 1) * NPAIR // G):
                sl = slice(j * LANE, (j + 1) * LANE)
                Xp = xc_ref[:, sl]
                dtp = dte_ref[:, sl]
                Xdt = Xp * dtp
                Xb = Xdt.astype(BF16)
                acs_p = acs_e[:, sl]
                last = acs_p[T - 1:T, :]
                e_p, dec, cd = jnp.exp(acs_p), jnp.exp(last - acs_p), jnp.exp(last)
                Hin = hin_ref[0, j]
                Hb = Hin.astype(BF16)
                dHn = dht_ref[j]
                dHb = dHn.astype(BF16)
                dYp = dy_ref[:, sl]
                z = _dot(Cb, Hb)
                dz = (dYp * e_p).astype(BF16)
                dacs_p = dYp * z * e_p
                dC = dC + _dot(dz, Hb, NT)
                dHin = _dot(Cb, dz, TN) + cd * dHn
                dlast = _colsum(dHn * Hin) * cd
                qv = _dot(Bb, dHb)
                dXdt = qv * dec
                ddec = qv * Xdt * dec
                dacs_p = dacs_p - ddec
                dlast = dlast + _colsum(ddec)
                dB = dB + _dot((Xdt * dec).astype(BF16), dHb, NT)
                for e in (0, 1):
                    h = 2 * j + e
                    Lm = jnp.exp(jnp.where(tril, acs_s[:, h:h + 1] - acs_t[h:h + 1, :], -1e30))
                    Mh = Gm * Lm
                    dYe = jnp.where(lo if e == 0 else jnp.logical_not(lo), dYp, 0.0).astype(BF16)
                    dM = _dot(dYe, Xb, NT)
                    dXdt = dXdt + _dot(Mh.astype(BF16), dYe, TN)
                    W = dM * Mh
                    cols_ref[:, h:h + 1] = jnp.sum(W, axis=1, keepdims=True)
                    rows_ref[h:h + 1, :] = _colsum(W)
                    dG = dG + dM * Lm
                dacs_ref[:, sl] = dacs_p + jnp.where(last_row, dlast, 0.0)
                ddt_ref[:, sl] = dXdt * Xp
                dxc_ref[:, sl] = dXdt * dtp + dYp * dsk_ref[:, sl]
                dsk_parts.append(_colsum(dYp * Xp))
                dht_ref[j] = dHin
            dGb = dG.astype(BF16)
            dxc_ref[:, bsl] = dB + _dot(dGb, Cb, TN)
            dxc_ref[:, csl] = dC + _dot(dGb, Bb)
        E = e_ref[...]
        dacs_s = cols_ref[...] - rows_ref[...].T + _dot(dacs_ref[...], E, NT, precision=HI)
        da = _dot(tri, dacs_s, TN, precision=HI)
        heads = lax.broadcasted_iota(I32, (1, LANE), 1) < HS
        A = jnp.where(heads, -jnp.exp(alog_ref[...]), 0.0)
        ddt = _dot(ddt_ref[...], E, NT, precision=HI) + da * A
        draw = jnp.where(heads, ddt * _sigmoid(raw_ref[...] + bias_ref[...]), 0.0)
        draw_ref[...] = draw
        dsk = _dot(jnp.broadcast_to(jnp.concatenate(dsk_parts, axis=1), (8, INNER)), E, NT, precision=HI)[0:1]
        for ref, val in ((dbias_ref, _colsum(draw)), (dalog_ref, _colsum(da * a_s)), (dskip_ref, dsk)):
            @pl.when(first)
            def _():
                ref[...] = val

            @pl.when(jnp.logical_not(first))
            def _():
                ref[...] += val

    rows = lambda w: pl.BlockSpec((T, w), lambda c: (NC - 1 - c, 0))
    vec = lambda w: pl.BlockSpec((1, w), lambda c: (0, 0))
    return pl.pallas_call(
        body, name="ssd_bwd", grid=(NC,),
        in_specs=[rows(INNER), rows(cfg.CONVCH), rows(INNER), rows(INNER), rows(LANE), vec(INNER),
                  pl.BlockSpec((1, NPAIR, NST, LANE), lambda c: (NC - 1 - c, 0, 0, 0)), rows(LANE), vec(LANE), vec(LANE),
                  pl.BlockSpec((LANE, INNER), lambda c: (0, 0))],
        out_specs=[rows(cfg.CONVCH), rows(LANE), vec(LANE), vec(LANE), vec(LANE)],
        out_shape=[jax.ShapeDtypeStruct((S, cfg.CONVCH), F32), jax.ShapeDtypeStruct((S, LANE), F32)]
        + [jax.ShapeDtypeStruct((1, LANE), F32)] * 3,
        scratch_shapes=[pltpu.VMEM((NPAIR, NST, LANE), F32), pltpu.VMEM((T, LANE), F32), pltpu.VMEM((LANE, T), F32),
                        pltpu.VMEM((T, INNER), F32), pltpu.VMEM((T, INNER), F32)],
        compiler_params=_params(("arbitrary",)),
    )(dy, xc, dt_exp, a_exp, a_small, dskip_exp, hin, dt_raw, dt_bias_pad, a_log_pad, expand)


def _ssd_post(cfg, y, z, norm_g):
    W = cfg.INNER // cfg.G

    def fn(y, z, g):
        yz = y * z * _sigmoid(z)
        return jnp.concatenate([yz[:, i * W:(i + 1) * W] * _rs(yz[:, i * W:(i + 1) * W]) for i in range(cfg.G)], axis=1) * g

    return _rowwise("ssd_post", fn, [y, z], [norm_g], [(cfg.INNER, BF16)], [], _pick(cfg.S, 256, 8))[0]


def _ssd_post_bwd(cfg, db, y, z, norm_g):
    W = cfg.INNER // cfg.G

    def fn(db, y, z, g):
        sg = _sigmoid(z)
        yz = y * z * sg
        dn = db * g
        dyz, nh = [], []
        for i in range(cfg.G):
            seg = yz[:, i * W:(i + 1) * W]
            r = _rs(seg)
            nh.append(seg * r)
            dyz.append(_rms_back(nh[-1], r, dn[:, i * W:(i + 1) * W]))
        dyz = jnp.concatenate(dyz, axis=1)
        return dyz * z * sg, dyz * y * sg * (1.0 + z * (1.0 - sg)), _colsum(db * jnp.concatenate(nh, axis=1))

    return _rowwise("ssd_post_bwd", fn, [db, y, z], [norm_g], [(cfg.INNER, F32), (cfg.INNER, F32)], [(1, cfg.INNER)],
                    _pick(cfg.S, 256, 8))


def _local_grads(cfg, x, tgt, W, sp, out_weight=None, ffn_weights=None, ffn_grads_ready=None, early_grads_ready=None,
                 in_grad_ready=None):
    S, D, H, INNER = cfg.S, cfg.D, cfg.H, cfg.INNER
    ts = _pick(S, 256, 8)
    tc = 256

    xn = _rowwise("rms_pre", lambda x, g: x * _rs(x) * g, [x], [sp["mix_pre_g"]], [(D, BF16)], [], ts)[0]
    u = _matmul("mm_in", xn, W["w_in"], "nt", F32)
    c_q, c_kv = u[:, :cfg.QL], u[:, cfg.QL:cfg.o_kr]
    kr = u[:, cfg.o_kr:cfg.o_z]
    z = u[:, cfg.o_z:cfg.o_xbc]
    xbc = u[:, cfg.o_xbc:cfg.o_dt]
    dt_raw = u[:, cfg.o_dt:]

    cqn = _rowwise("rms_q", lambda x, g: x * _rs(x) * g, [c_q], [sp["q_norm_g"]], [(cfg.QL, BF16)], [], ts)[0]
    ckvn = _rowwise("rms_kv", lambda x, g: x * _rs(x) * g, [c_kv], [sp["kv_norm_g"]], [(cfg.KVL, BF16)], [], ts)[0]
    q = _matmul("mm_uq", cqn, W["w_uq"], "nn", F32)
    kv = _matmul("mm_ukv", ckvn, W["w_ukv"], "nn", F32)
    cos2, sin2 = _rope_tables(S)
    Qh, Kh, Vh = _mla_pack(cfg, q, kv, kr, cos2, sin2)
    a_out, lse, lse_t = _attn_fwd(cfg, Qh, Kh, Vh)
    if out_weight is not None:
        sp = dict(sp, ssm_conv_b=sp["ssm_conv_b"] + out_weight.pass_on(a_out)[0, 0])

    pad = lambda v: jnp.pad(v, ((0, 0), (0, LANE - v.shape[1])))
    expand = _expand_matrix(cfg)
    dt_bias_pad, a_log_pad = pad(sp["dt_bias"]), pad(sp["a_log"])
    dskip_exp = jnp.repeat(sp["d_skip"], HP, axis=1)
    xc = _colwise("ssm_act", _ssm_act, [xbc], [sp["ssm_conv_w"], sp["ssm_conv_b"]], [F32], [], tc)[0]
    dt_s, a_s, dt_exp, a_exp = _ssd_prep(cfg, dt_raw, dt_bias_pad, a_log_pad, expand)
    y_ssd, hin = _ssd_fwd(cfg, xc, dt_exp, a_exp, a_s, dskip_exp)
    b_out = _ssd_post(cfg, y_ssd, z, sp["ssm_norm_g"])

    ab_out = jnp.concatenate([a_out.astype(BF16), b_out], axis=1)
    if out_weight is not None:
        W = dict(W, **out_weight.arrived(ab_out))
    if ffn_weights is not None:
        sp = dict(sp, mix_post_g=sp["mix_post_g"] + ffn_weights.pass_on(ab_out)[0, 0])
    mix = _matmul("mm_out", ab_out, W["w_out"], "nn", F32)

    def mid(x, mix, g_mp, g_fp):
        x1 = x + mix * _rs(mix) * g_mp
        return x1, x1 * _rs(x1) * g_fp

    x1, h2 = _rowwise("fwd_mid", mid, [x, mix], [sp["mix_post_g"], sp["ffn_pre_g"]], [(D, F32), (D, BF16)], [], ts)
    if ffn_weights is not None:
        W = dict(W, **ffn_weights.arrived(h2))
    gate_pre = _matmul("mm_gate", h2, W["w_gate"], "nn", F32, chips=True)
    up = _matmul("mm_up", h2, W["w_up"], "nn", F32, chips=True)
    act = _colwise("ffn_act", _ffn_act, [gate_pre, up], [sp["ffn_conv_w"], sp["ffn_conv_b"]], [BF16], [], tc)[0]
    f = _matmul("mm_down", act, W["w_down"], "nn", F32)

    def final(x1, f, t, g):
        r = _rs(f)
        fh = f * r
        err = x1 + fh * g - t
        loss = 0.5 * jnp.sum(jnp.mean(err * err, axis=-1, keepdims=True), axis=0, keepdims=True)
        dy = err * (1.0 / D)
        return dy, _rms_back(fh, r, dy * g), _colsum(dy * fh), loss

    dy, df, g_ffn_post, loss = _rowwise("final", final, [x1, f, tgt], [sp["ffn_post_g"]], [(D, F32), (D, BF16)],
                                        [(1, D), (1, LANE)], ts)
    gW = {}
    dact = _matmul("mm_down_dx", df, W["w_down"], "nt", F32)
    gW["w_down"] = _matmul("mm_down_dw", act, df, "tn", BF16)
    dgate, dup, g_ffn_conv_w, g_ffn_conv_b = _colwise(
        "ffn_act_bwd", _ffn_act_back, [dact, gate_pre, up], [sp["ffn_conv_w"], sp["ffn_conv_b"]], [BF16, BF16], [FFN_K, 1], tc)
    gW["w_gate"] = _matmul("mm_gate_dw", h2, dgate, "tn", BF16, chips=True)
    gW["w_up"] = _matmul("mm_up_dw", h2, dup, "tn", BF16, chips=True)
    if ffn_grads_ready is not None:
        sp = dict(sp, ffn_pre_g=sp["ffn_pre_g"] + ffn_grads_ready({n: gW[n] for n in ("w_down", "w_gate", "w_up")})[0, 0])
    dh2 = _matmul("mm_gu_dx", dgate, W["w_gate"], "nt", F32, dup, W["w_up"], chips=True)

    def mid_back(dy, dh2, x1, mix, g_mp, g_fp):
        r2 = _rs(x1)
        xh = x1 * r2
        dx1 = dy + _rms_back(xh, r2, dh2 * g_fp)
        r1 = _rs(mix)
        mh = mix * r1
        return dx1, _rms_back(mh, r1, dx1 * g_mp), _colsum(dh2 * xh), _colsum(dx1 * mh)

    dx1, dmix, g_ffn_pre, g_mix_post = _rowwise("bwd_mid", mid_back, [dy, dh2, x1, mix], [sp["mix_post_g"], sp["ffn_pre_g"]],
                                                [(D, F32), (D, BF16)], [(1, D), (1, D)], ts)
    dab_out = _matmul("mm_out_dx", dmix, W["w_out"], "nt", F32)
    db_out = dab_out[:, cfg.MLAW:]
    gW["w_out"] = _matmul("mm_out_dw", ab_out, dmix, "tn", BF16)
    if early_grads_ready is not None:
        token = early_grads_ready({n: gW[n] for n in ("w_down", "w_gate", "w_up", "w_out")})
        sp = dict(sp, ssm_norm_g=sp["ssm_norm_g"] + token[0, 0])

    dy_ssd, dz, g_ssm_norm = _ssd_post_bwd(cfg, db_out, y_ssd, z, sp["ssm_norm_g"])
    dxc, ddt_raw, g_dt_bias, g_a_log, g_d_skip = _ssd_bwd(cfg, dy_ssd, xc, dt_exp, a_exp, a_s, dskip_exp, hin, dt_raw,
                                                          dt_bias_pad, a_log_pad, expand)
    dxbc, g_ssm_conv_w, g_ssm_conv_b = _colwise("ssm_act_bwd", _ssm_act_back, [dxc, xbc], [sp["ssm_conv_w"], sp["ssm_conv_b"]],
                                                [BF16], [SSM_K, 1], tc)

    dQ, delta_t = _attn_dq(cfg, Qh, Kh, Vh, dab_out, a_out, lse)
    dK, dV = _attn_dkv(cfg, Qh, Kh, Vh, dab_out, lse_t, delta_t)
    dq, dkv, dkr = _mla_unpack(cfg, dQ, dK, dV, cos2, sin2)
    dcqn = _matmul("mm_uq_dx", dq, W["w_uq"], "nt", F32)
    dckvn = _matmul("mm_ukv_dx", dkv, W["w_ukv"], "nt", F32)
    gW["w_uq"] = _matmul("mm_uq_dw", cqn, dq, "tn", BF16)
    gW["w_ukv"] = _matmul("mm_ukv_dw", ckvn, dkv, "tn", BF16)

    def rms_back(x, dy, g):
        r = _rs(x)
        xh = x * r
        return _rms_back(xh, r, dy * g), _colsum(dy * xh)

    dc_q, g_q_norm = _rowwise("rms_q_bwd", rms_back, [c_q, dcqn], [sp["q_norm_g"]], [(cfg.QL, BF16)], [(1, cfg.QL)], ts)
    dc_kv, g_kv_norm = _rowwise("rms_kv_bwd", rms_back, [c_kv, dckvn], [sp["kv_norm_g"]], [(cfg.KVL, BF16)], [(1, cfg.KVL)], ts)

    du = jnp.concatenate([dc_q, dc_kv, dkr, dz.astype(BF16), dxbc, ddt_raw.astype(BF16)], axis=1)
    gW["w_in"] = _matmul("mm_in_dw", du, xn, "tn", BF16)
    if in_grad_ready is not None:
        token = in_grad_ready({n: gW[n] for n in ("w_in", "w_uq", "w_ukv")})
        sp = dict(sp, mix_pre_g=sp["mix_pre_g"] + token[0, 0])
    dxn = _matmul("mm_in_dx", du, W["w_in"], "nn", F32)

    def first_back(dx1, dxn, x, g):
        r = _rs(x)
        xh = x * r
        return dx1 + _rms_back(xh, r, dxn * g), _colsum(dxn * xh)

    grad_x, g_mix_pre = _rowwise("bwd_first", first_back, [dx1, dxn, x], [sp["mix_pre_g"]], [(D, F32)], [(1, D)], ts)

    gs = dict(mix_pre_g=g_mix_pre, q_norm_g=g_q_norm, kv_norm_g=g_kv_norm, ssm_conv_w=g_ssm_conv_w, ssm_conv_b=g_ssm_conv_b,
              dt_bias=g_dt_bias[:, :cfg.HS], a_log=g_a_log[:, :cfg.HS], d_skip=g_d_skip[:, :cfg.HS], ssm_norm_g=g_ssm_norm,
              mix_post_g=g_mix_post, ffn_pre_g=g_ffn_pre, ffn_conv_w=g_ffn_conv_w, ffn_conv_b=g_ffn_conv_b,
              ffn_post_g=g_ffn_post)
    return loss, grad_x, gW, gs


def _to_kernel_layout(cfg, name, w):
    if name == "w_in":
        a = cfg.o_kr + ROPE
        return jnp.concatenate([w[:a], jnp.zeros((LANE - ROPE, w.shape[1]), w.dtype), w[a:],
                                jnp.zeros((LANE - cfg.HS, w.shape[1]), w.dtype)], axis=0)
    if name in ("w_uq", "w_ukv"):
        per = NOPE + (ROPE if name == "w_uq" else VH)
        return jnp.concatenate([w[:, h * per:h * per + NOPE] for h in range(cfg.H)]
                               + [w[:, h * per + NOPE:(h + 1) * per] for h in range(cfg.H)], axis=1)
    return w


def _from_kernel_layout(cfg, name, g):
    if name == "w_in":
        return jnp.concatenate([g[:cfg.o_kr + ROPE], g[cfg.o_z:cfg.o_dt + cfg.HS]], axis=0)
    if name in ("w_uq", "w_ukv"):
        second = ROPE if name == "w_uq" else VH
        base = cfg.H * NOPE
        parts = []
        for h in range(cfg.H):
            parts += [g[:, h * NOPE:(h + 1) * NOPE], g[:, base + h * second:base + (h + 1) * second]]
        return jnp.concatenate(parts, axis=1)
    return g


def _cols_to_chips(w):
    r, c = w.shape
    return w.reshape(r, N_CHIPS, c // N_CHIPS).transpose(1, 0, 2)


def _chips_to_cols(g):
    k, r, cs = g.shape
    return g.transpose(1, 0, 2).reshape(r, k * cs)


_CHIP_MAJOR = ("w_gate", "w_up")
_RELAYOUT = ("w_uq", "w_ukv")
_LAYOUT_ROWS = 256


def _w_in_layout(cfg, wg):
    _, rs, d = wg.shape
    tc = _pick(d, _LAYOUT_ROWS, LANE)

    def body(w_ref, o_ref):
        o_ref[...] = _to_kernel_layout(cfg, "w_in", jnp.concatenate([w_ref[k] for k in range(N_CHIPS)], axis=0))

    return pl.pallas_call(
        body, name="layout_w_in", grid=(d // tc,),
        in_specs=[pl.BlockSpec((N_CHIPS, rs, tc), lambda j: (0, 0, j))], out_specs=pl.BlockSpec((cfg.EXT, tc), lambda j: (0, j)),
        out_shape=jax.ShapeDtypeStruct((cfg.EXT, d), wg.dtype), compiler_params=_params(("parallel",)),
    )(wg)


def _w_in_grad_to_chips(cfg, g):
    _, d = g.shape
    rs = cfg.IN_COLS // N_CHIPS
    tc = _pick(d, _LAYOUT_ROWS, LANE)

    def body(g_ref, o_ref):
        nat = _from_kernel_layout(cfg, "w_in", g_ref[...])
        for k in range(N_CHIPS):
            o_ref[k] = nat[k * rs:(k + 1) * rs]

    return pl.pallas_call(
        body, name="layout_grad_w_in", grid=(d // tc,),
        in_specs=[pl.BlockSpec((cfg.EXT, tc), lambda j: (0, j))], out_specs=pl.BlockSpec((N_CHIPS, rs, tc), lambda j: (0, 0, j)),
        out_shape=jax.ShapeDtypeStruct((N_CHIPS, rs, d), g.dtype), compiler_params=_params(("parallel",)),
    )(g)


def _gathered_to_kernel(cfg, name, wg):
    if name in _CHIP_MAJOR:
        return wg
    if name == "w_in":
        return _w_in_layout(cfg, wg)
    if name not in _RELAYOUT:
        return wg.reshape(wg.shape[0] * wg.shape[1], wg.shape[2])
    _, rows, cs = wg.shape
    tr = _pick(rows, _LAYOUT_ROWS, 16)

    def body(w_ref, o_ref):
        o_ref[...] = _to_kernel_layout(cfg, name, jnp.concatenate([w_ref[k] for k in range(N_CHIPS)], axis=1))

    wide = jax.eval_shape(lambda w: _to_kernel_layout(cfg, name, w), jax.ShapeDtypeStruct((rows, N_CHIPS * cs), wg.dtype)).shape[1]
    return pl.pallas_call(
        body, name="layout_" + name, grid=(rows // tr,),
        in_specs=[pl.BlockSpec((N_CHIPS, tr, cs), lambda i: (0, i, 0))], out_specs=pl.BlockSpec((tr, wide), lambda i: (i, 0)),
        out_shape=jax.ShapeDtypeStruct((rows, wide), wg.dtype), compiler_params=_params(("parallel",)),
    )(wg)


def _grad_to_chips(cfg, name, g):
    if name in _CHIP_MAJOR:
        return g
    if name == "w_in":
        return _w_in_grad_to_chips(cfg, g)
    if name not in _RELAYOUT:
        return g.reshape(N_CHIPS, g.shape[0] // N_CHIPS, g.shape[1])
    rows, wide = g.shape
    tr = _pick(rows, _LAYOUT_ROWS, 16)
    cs = jax.eval_shape(lambda v: _from_kernel_layout(cfg, name, v), g).shape[1] // N_CHIPS

    def body(g_ref, o_ref):
        nat = _from_kernel_layout(cfg, name, g_ref[...])
        for k in range(N_CHIPS):
            o_ref[k] = nat[:, k * cs:(k + 1) * cs]

    return pl.pallas_call(
        body, name="layout_grad_" + name, grid=(rows // tr,),
        in_specs=[pl.BlockSpec((tr, wide), lambda i: (i, 0))], out_specs=pl.BlockSpec((N_CHIPS, tr, cs), lambda i: (0, i, 0)),
        out_shape=jax.ShapeDtypeStruct((N_CHIPS, rows, cs), g.dtype), compiler_params=_params(("parallel",)),
    )(g)


def _me():
    return lax.axis_index("x"), lax.axis_index("y"), lax.axis_index("c")


def _other_chips(x, y):
    return [(1 - x, y), (x, 1 - y), (1 - x, 1 - y)]


_ANY = pl.BlockSpec(memory_space=pl.ANY)


def _row_block(rows, cols, mult):
    return _pick(rows, max(mult, (1 << 19) // cols // mult * mult), mult)


def _scalar(v):
    return v.astype(I32).reshape(1)


def _blocks2d(r, c, mult):
    if r % mult == 0:
        tr = _row_block(r, c, mult)
        return (tr, c), r // tr, lambda i: (i, 0)
    tc = _pick(c, max(LANE, (1 << 19) // r // LANE * LANE), LANE)
    return (r, tc), c // tc, lambda i: (0, i)


def _by_rows(rows):
    return rows % 32 == 0


def _half_shape(rows, cols):
    return (rows // 2, cols) if _by_rows(rows) else (rows, cols // 2)


def _half_blocks(rows, cols, mult):
    hr, hc = _half_shape(rows, cols)
    block, n, part = _blocks2d(hr, hc, mult)
    assert (hr % mult == 0) == _by_rows(rows), (rows, cols, mult)
    full = (lambda h, i: (h * n + i, 0)) if _by_rows(rows) else (lambda h, i: (0, h * n + i))
    return block, n, full, part


def _half(ref, k, half):
    hr, hc = _half_shape(ref.shape[1], ref.shape[2])
    if _by_rows(ref.shape[1]):
        return ref.at[k, pl.ds(pl.multiple_of(half * hr, 16), hr), :]
    return ref.at[k, :, pl.ds(pl.multiple_of(half * hc, LANE), hc)]


def _shard_blocks(w, br, bc):
    if w.shape[0] == 1:
        def write(ref, v):
            ref[...] = v
        return (lambda f: pl.BlockSpec((None, br, bc), lambda *a: (0, *f(*a)))), (lambda ref: ref[...]), write
    assert w.shape[1] == 1 and br == w.shape[0], w.shape

    def write_rows(ref, v):
        ref[:, 0, :] = v
    return (lambda f: pl.BlockSpec((br, 1, bc), lambda *a: (0, 0, f(*a)[1]))), (lambda ref: ref[:, 0, :]), write_rows


def _stage_shard(name, w, chip):
    rs, cs = w.shape[0] * w.shape[1], w.shape[2]
    (br, bc), n, idx = _blocks2d(rs, cs, 16)
    spec, get, _ = _shard_blocks(w, br, bc)

    def body(chip_ref, w_ref, o_ref):
        o_ref[...] = get(w_ref).astype(BF16)

    return pl.pallas_call(
        body, name="stage_" + name,
        grid_spec=pltpu.PrefetchScalarGridSpec(
            num_scalar_prefetch=1, grid=(n,),
            in_specs=[spec(lambda i, chip_ref: idx(i))],
            out_specs=pl.BlockSpec((None, br, bc), lambda i, chip_ref: (chip_ref[0], *idx(i)))),
        out_shape=jax.ShapeDtypeStruct((N_CHIPS, rs, cs), BF16),
        compiler_params=_params(("parallel",)),
    )(_scalar(chip), w)


def _allgather_weights(bufs):
    n = len(bufs)

    def body(*refs):
        outs, send_sems, recv_sems = refs[n:2 * n], refs[2 * n], refs[2 * n + 1]
        x, y, c = _me()
        chip = 2 * x + y
        sib = (x, y, 1 - c)
        chips = _other_chips(x, y)

        def copy(k, part, to):
            return pltpu.make_async_remote_copy(src_ref=part, dst_ref=part, send_sem=send_sems.at[k], recv_sem=recv_sems.at[k],
                                                device_id=to, device_id_type=MESH_ID)

        started = []
        for w, o_ref in enumerate(outs):
            for j, (cx, cy) in enumerate(chips):
                started.append(copy(6 * w + j, _half(o_ref, chip, c), (cx, cy, c)))
                started[-1].start()
        for w, o_ref in enumerate(outs):
            for j, (cx, cy) in enumerate(chips):
                theirs = _half(o_ref, 2 * cx + cy, c)
                copy(6 * w + j, theirs, sib).wait_recv()
                started.append(copy(6 * w + 3 + j, theirs, sib))
                started[-1].start()
        for w, o_ref in enumerate(outs):
            for j, (cx, cy) in enumerate(chips):
                copy(6 * w + 3 + j, _half(o_ref, 2 * cx + cy, 1 - c), sib).wait_recv()
        for cp in started:
            cp.wait_send()

    return pl.pallas_call(
        body, name="allgather_weights", in_specs=[_ANY] * n, out_specs=[_ANY] * n,
        out_shape=[jax.ShapeDtypeStruct(b.shape, b.dtype) for b in bufs],
        input_output_aliases={i: i for i in range(n)},
        scratch_shapes=[pltpu.SemaphoreType.DMA((6 * n,)), pltpu.SemaphoreType.DMA((6 * n,))],
    )(*bufs)


_HBM = pl.BlockSpec(memory_space=pltpu.HBM)
_SEM = pl.BlockSpec(memory_space=pltpu.SEMAPHORE)
_EFFECT = pltpu.SideEffectType.DATAFLOW_SIDE_EFFECTING


def _split_start(name, bufs, n_copies, copies, after):
    n = len(bufs)

    def body(*refs):
        for cp in copies(refs[:n], refs[n + 1], refs[n + 2]):
            cp.start()
        refs[-1][...] = jnp.zeros_like(refs[-1])

    res = pl.pallas_call(
        body, name=name,
        out_shape=(pltpu.SemaphoreType.DMA((n_copies,)), pltpu.SemaphoreType.DMA((n_copies,)),
                   *[pltpu.HBM(b.shape, b.dtype) for b in bufs], jax.ShapeDtypeStruct((8, LANE), F32)),
        in_specs=[_HBM] * n + [_ANY], out_specs=(_SEM, _SEM, *[_HBM] * n, pl.BlockSpec(memory_space=pltpu.VMEM)),
        input_output_aliases={i: 2 + i for i in range(n)},
        compiler_params=pltpu.CompilerParams(has_side_effects=_EFFECT),
    )(*[pltpu.with_memory_space_constraint(b, pltpu.HBM) for b in bufs], after)
    return res[0], res[1], list(res[2:2 + n]), res[-1]


def _split_wait(name, send_sems, recv_sems, bufs, after, copies):
    n = len(bufs)

    def body(*refs):
        for cp in copies(refs[:n], refs[n], refs[n + 1]):
            cp.wait_send()
            cp.wait_recv()

    return list(pl.pallas_call(
        body, name=name, out_shape=[pltpu.HBM(b.shape, b.dtype) for b in bufs],
        in_specs=[_HBM] * n + [_SEM, _SEM, _ANY], out_specs=[_HBM] * n,
        input_output_aliases={i: i for i in range(n)},
        compiler_params=pltpu.CompilerParams(has_side_effects=_EFFECT),
    )(*bufs, send_sems, recv_sems, after))


def _gather_to_chips(bufs, send_sems, recv_sems):
    x, y, c = _me()
    return [pltpu.make_async_remote_copy(src_ref=_half(b, 2 * x + y, c), dst_ref=_half(b, 2 * x + y, c),
                                         send_sem=send_sems.at[3 * w + j], recv_sem=recv_sems.at[3 * w + j],
                                         device_id=(cx, cy, c), device_id_type=MESH_ID)
            for w, b in enumerate(bufs) for j, (cx, cy) in enumerate(_other_chips(x, y))]


def _gather_to_sibling(bufs, send_sems, recv_sems):
    x, y, c = _me()
    return [pltpu.make_async_remote_copy(src_ref=_half(b, 2 * cx + cy, c), dst_ref=_half(b, 2 * cx + cy, c),
                                         send_sem=send_sems.at[3 * w + j], recv_sem=recv_sems.at[3 * w + j],
                                         device_id=(x, y, 1 - c), device_id_type=MESH_ID)
            for w, b in enumerate(bufs) for j, (cx, cy) in enumerate(_other_chips(x, y))]


def _pair_exchange(name, grads):
    n = len(grads)

    def body(*refs):
        ins, outs, send_sems, recv_sems = refs[:n], refs[n:2 * n], refs[2 * n], refs[2 * n + 1]
        x, y, c = _me()
        cps = []
        for w, (g_ref, o_ref) in enumerate(zip(ins, outs)):
            cps.append(pltpu.make_async_remote_copy(src_ref=_half(g_ref, slice(None), 1 - c), dst_ref=o_ref,
                                                    send_sem=send_sems.at[w], recv_sem=recv_sems.at[w],
                                                    device_id=(x, y, 1 - c), device_id_type=MESH_ID))
            cps[-1].start()
        for cp in cps:
            cp.wait()

    return pl.pallas_call(
        body, name="pair_exchange_" + name, in_specs=[_ANY] * n, out_specs=[_ANY] * n,
        out_shape=[jax.ShapeDtypeStruct((g.shape[0], *_half_shape(g.shape[1], g.shape[2])), g.dtype) for g in grads],
        scratch_shapes=[pltpu.SemaphoreType.DMA((n,)), pltpu.SemaphoreType.DMA((n,))],
    )(*grads)


def _pair_copies(grads, lands, send_sems, recv_sems):
    x, y, c = _me()
    return [pltpu.make_async_remote_copy(src_ref=_half(g_ref, slice(None), 1 - c), dst_ref=l_ref, send_sem=send_sems.at[w],
                                         recv_sem=recv_sems.at[w], device_id=(x, y, 1 - c), device_id_type=MESH_ID)
            for w, (g_ref, l_ref) in enumerate(zip(grads, lands))]


def _pair_exchange_start(name, grads):
    n = len(grads)
    lands = [lax.empty((g.shape[0], *_half_shape(g.shape[1], g.shape[2])), g.dtype) for g in grads]
    send_sems, recv_sems, bufs, token = _split_start(
        "pair_exchange_start_" + name, [*grads, *lands], n, lambda refs, ss, rs: _pair_copies(refs[:n], refs[n:], ss, rs),
        jnp.zeros((8, LANE), F32))
    return (send_sems, recv_sems, bufs), token


def _pair_exchange_wait(name, state, after):
    send_sems, recv_sems, bufs = state
    n = len(bufs) // 2
    bufs = _split_wait("pair_exchange_wait_" + name, send_sems, recv_sems, bufs, after,
                       lambda refs, ss, rs: _pair_copies(refs[:n], refs[n:], ss, rs))
    return bufs[:n], bufs[n:]


def _pair_sum(name, g, theirs, c):
    (br, bc), nb, full, part = _half_blocks(g.shape[1], g.shape[2], 16)

    def body(c_ref, a_ref, b_ref, o_ref):
        o_ref[...] = (a_ref[...].astype(F32) + b_ref[...].astype(F32)).astype(o_ref.dtype)

    return pl.pallas_call(
        body, name="pair_sum_" + name,
        grid_spec=pltpu.PrefetchScalarGridSpec(
            num_scalar_prefetch=1, grid=(N_CHIPS, nb),
            in_specs=[pl.BlockSpec((None, br, bc), lambda k, i, c_ref: (k, *full(c_ref[0], i))),
                      pl.BlockSpec((None, br, bc), lambda k, i, c_ref: (k, *part(i)))],
            out_specs=pl.BlockSpec((None, br, bc), lambda k, i, c_ref: (k, *part(i)))),
        out_shape=jax.ShapeDtypeStruct(theirs.shape, BF16),
        compiler_params=_params(("parallel", "parallel")),
    )(_scalar(c), g, theirs)


def _chip_copies(srcs, lands, send_sems, recv_sems):
    x, y, c = _me()
    return [pltpu.make_async_remote_copy(src_ref=s_ref.at[2 * cx + cy], dst_ref=l_ref.at[j], send_sem=send_sems.at[3 * w + j],
                                         recv_sem=recv_sems.at[3 * w + j], device_id=(cx, cy, c), device_id_type=MESH_ID)
            for w, (s_ref, l_ref) in enumerate(zip(srcs, lands)) for j, (cx, cy) in enumerate(_other_chips(x, y))]


def _chip_exchange_start(name, sums):
    n = len(sums)
    lands = [lax.empty((3,) + s.shape[1:], s.dtype) for s in sums]
    send_sems, recv_sems, bufs, token = _split_start(
        "chip_exchange_start_" + name, [*sums, *lands], 3 * n, lambda refs, ss, rs: _chip_copies(refs[:n], refs[n:], ss, rs),
        jnp.zeros((8, LANE), F32))
    return send_sems, recv_sems, bufs[:n], bufs[n:], token


def _chip_exchange_wait(name, send_sems, recv_sems, sums, lands, after):
    n = len(sums)
    bufs = _split_wait("chip_exchange_wait_" + name, send_sems, recv_sems, [*sums, *lands], after,
                       lambda refs, ss, rs: _chip_copies(refs[:n], refs[n:], ss, rs))
    return bufs[:n], bufs[n:]


def _chip_sum(name, sums, theirs, chip):
    _, h, cs = sums.shape
    (br, bc), nb, idx = _blocks2d(h, cs, 16)

    def body(chip_ref, s_ref, t_ref, o_ref):
        acc = s_ref[...].astype(F32)
        for k in range(3):
            acc = acc + t_ref[k].astype(F32)
        o_ref[...] = acc

    return pl.pallas_call(
        body, name="chip_sum_" + name,
        grid_spec=pltpu.PrefetchScalarGridSpec(
            num_scalar_prefetch=1, grid=(nb,),
            in_specs=[pl.BlockSpec((None, br, bc), lambda i, chip_ref: (chip_ref[0], *idx(i))),
                      pl.BlockSpec((3, br, bc), lambda i, chip_ref: (0, *idx(i)))],
            out_specs=pl.BlockSpec((br, bc), lambda i, chip_ref: idx(i))),
        out_shape=jax.ShapeDtypeStruct((h, cs), F32),
        compiler_params=_params(("parallel",)),
    )(_scalar(chip), sums, theirs)


def _sibling_copies(halves, lands, send_sems, recv_sems):
    x, y, c = _me()
    return [pltpu.make_async_remote_copy(src_ref=h_ref, dst_ref=l_ref, send_sem=send_sems.at[w], recv_sem=recv_sems.at[w],
                                         device_id=(x, y, 1 - c), device_id_type=MESH_ID)
            for w, (h_ref, l_ref) in enumerate(zip(halves, lands))]


def _sibling_exchange_start(name, halves):
    n = len(halves)
    lands = [lax.empty(h.shape, h.dtype) for h in halves]
    send_sems, recv_sems, bufs, token = _split_start(
        "sibling_exchange_start_" + name, [*halves, *lands], n, lambda refs, ss, rs: _sibling_copies(refs[:n], refs[n:], ss, rs),
        jnp.zeros((8, LANE), F32))
    return (send_sems, recv_sems, bufs), token


def _sibling_exchange_wait(name, state, after):
    send_sems, recv_sems, bufs = state
    n = len(bufs) // 2
    bufs = _split_wait("sibling_exchange_wait_" + name, send_sems, recv_sems, bufs, after,
                       lambda refs, ss, rs: _sibling_copies(refs[:n], refs[n:], ss, rs))
    return bufs[:n], bufs[n:]


def _sibling_exchange(halves):
    n = len(halves)

    def body(*refs):
        ins, outs, send_sems, recv_sems = refs[:n], refs[n:2 * n], refs[2 * n], refs[2 * n + 1]
        x, y, c = _me()
        cps = []
        for w, (h_ref, o_ref) in enumerate(zip(ins, outs)):
            cps.append(pltpu.make_async_remote_copy(src_ref=h_ref, dst_ref=o_ref, send_sem=send_sems.at[w], recv_sem=recv_sems.at[w],
                                                    device_id=(x, y, 1 - c), device_id_type=MESH_ID))
            cps[-1].start()
        for cp in cps:
            cp.wait()

    return pl.pallas_call(
        body, name="grad_sibling_exchange", in_specs=[_ANY] * n, out_specs=[_ANY] * n,
        out_shape=[jax.ShapeDtypeStruct(h.shape, h.dtype) for h in halves],
        scratch_shapes=[pltpu.SemaphoreType.DMA((n,)), pltpu.SemaphoreType.DMA((n,))],
    )(*halves)


def _allreduce_small(name, vec, after):
    def body(v_ref, after_ref, o_ref, buf_ref, send_sems, recv_sems):
        x, y, c = _me()
        me = 4 * x + 2 * y + c
        cps = []
        for p in range(1, 8):
            px, py, pc = x ^ (p >> 2), y ^ ((p >> 1) & 1), c ^ (p & 1)
            cps.append(pltpu.make_async_remote_copy(src_ref=v_ref, dst_ref=buf_ref.at[me], send_sem=send_sems.at[p - 1],
                                                    recv_sem=recv_sems.at[p - 1], device_id=(px, py, pc), device_id_type=MESH_ID))
            cps[-1].start()
        buf_ref[me] = v_ref[...]
        for p in range(1, 8):
            theirs = buf_ref.at[me ^ p]
            pltpu.make_async_remote_copy(src_ref=theirs, dst_ref=theirs, send_sem=send_sems.at[p - 1], recv_sem=recv_sems.at[p - 1],
                                         device_id=(x, y, c), device_id_type=MESH_ID).wait_recv()
        for cp in cps:
            cp.wait_send()
        acc = buf_ref[0]
        for k in range(1, 8):
            acc = acc + buf_ref[k]
        o_ref[...] = acc

    vm = pl.BlockSpec(memory_space=pltpu.VMEM)
    return pl.pallas_call(
        body, name=name, in_specs=[vm, _ANY], out_specs=vm, out_shape=jax.ShapeDtypeStruct(vec.shape, F32),
        scratch_shapes=[pltpu.VMEM((8,) + vec.shape, F32), pltpu.SemaphoreType.DMA((7,)), pltpu.SemaphoreType.DMA((7,))],
    )(vec, after)


def _adam_math(w, g, m, v):
    m = ADAM_B1 * m + (1.0 - ADAM_B1) * g
    v = ADAM_B2 * v + (1.0 - ADAM_B2) * (g * g)
    m_hat = m / (1.0 - ADAM_B1 ** ADAM_STEP)
    v_hat = v / (1.0 - ADAM_B2 ** ADAM_STEP)
    return -ADAM_LR * (m_hat / (jnp.sqrt(v_hat) + ADAM_EPS) + ADAM_WD * w), m, v


def _adamw(name, w, g, m, v):
    R, C = w.shape
    tr = _row_block(R, C, 8)

    def body(w_ref, g_ref, m_ref, v_ref, d_ref, nm_ref, nv_ref):
        d_ref[...], nm_ref[...], nv_ref[...] = _adam_math(w_ref[...], g_ref[...], m_ref[...], v_ref[...])

    blk = pl.BlockSpec((tr, C), lambda i: (i, 0))
    return pl.pallas_call(
        body, name=name, grid=(R // tr,), in_specs=[blk] * 4, out_specs=[blk] * 3,
        out_shape=[jax.ShapeDtypeStruct((R, C), F32)] * 3, compiler_params=_params(("parallel",)),
    )(w, g, m, v)


def _adamw_halves(name, w, mine, theirs, m, v, c):
    rs, cs = w.shape[0] * w.shape[1], w.shape[2]
    (br, bc), nb, whole, half = _half_blocks(rs, cs, 8)
    spec, get, put = _shard_blocks(w, br, bc)

    def body(c_ref, w_ref, a_ref, b_ref, m_ref, v_ref, g_ref, d_ref, nm_ref, nv_ref):
        g = jnp.where(pl.program_id(0) == c_ref[0], a_ref[...], b_ref[...])
        put(g_ref, g)
        for ref, val in zip((d_ref, nm_ref, nv_ref), _adam_math(get(w_ref), g, get(m_ref), get(v_ref))):
            put(ref, val)

    full = spec(lambda s, i, c_ref: whole(s, i))
    part = pl.BlockSpec((br, bc), lambda s, i, c_ref: half(i))
    return pl.pallas_call(
        body, name=name,
        grid_spec=pltpu.PrefetchScalarGridSpec(num_scalar_prefetch=1, grid=(2, nb), in_specs=[full, part, part, full, full],
                                               out_specs=[full] * 4),
        out_shape=[jax.ShapeDtypeStruct(w.shape, F32)] * 4, compiler_params=_params(("parallel", "parallel")),
    )(_scalar(c), w, mine, theirs, m, v)


def _pack_small(arrs):
    flat = jnp.concatenate([a.reshape(-1) for a in arrs])
    n = -(-flat.shape[0] // (8 * LANE)) * 8 * LANE
    return jnp.pad(flat, (0, n - flat.shape[0])).reshape(8, n // 8)


def _unpack_small(vec, shapes):
    flat, out, off = vec.reshape(-1), [], 0
    for s in shapes:
        out.append(flat[off:off + s[0] * s[1]].reshape(s))
        off += s[0] * s[1]
    return out


class _LateWeights:
    def __init__(self, cfg, tag, names, staged, after):
        self.cfg, self.tag, self.names, self.k = cfg, tag, names, 3 * len(names)
        self.send, self.recv, self.bufs, self.token = _split_start(f"gather_{tag}_chips_start", staged, self.k, _gather_to_chips,
                                                                    after)

    def pass_on(self, after):
        bufs = _split_wait(f"gather_{self.tag}_chips_wait", self.send, self.recv, self.bufs, after, _gather_to_chips)
        self.send, self.recv, self.bufs, token = _split_start(f"gather_{self.tag}_sibling_start", bufs, self.k, _gather_to_sibling,
                                                               self.token)
        return token

    def arrived(self, after):
        bufs = _split_wait(f"gather_{self.tag}_sibling_wait", self.send, self.recv, self.bufs, after, _gather_to_sibling)
        return {n: _gathered_to_kernel(self.cfg, n, b) for n, b in zip(self.names, bufs)}


def _step(cfg, a):
    chip = 2 * lax.axis_index("x") + lax.axis_index("y")
    core = lax.axis_index("c")
    big = BIG

    ffn = ("w_gate", "w_up", "w_down")
    first = ("w_in", "w_uq", "w_ukv")
    staged = {n: _stage_shard(n, a[n], chip) for n in big}
    gathered = _allgather_weights([staged[n] for n in first])
    W = {n: _gathered_to_kernel(cfg, n, wg) for n, wg in zip(first, gathered)}

    sp = {n: a[n] for n in SMALL}
    sharded = _pack_small([a[n] for n in SMALL_SHARDED])
    slot = jnp.where(lax.broadcasted_iota(I32, (N_CHIPS,) + sharded.shape, 0) == chip, 0.5 * sharded[None], 0.0)
    allp = _allreduce_small("allgather_small", slot.reshape(N_CHIPS * 8, -1), gathered[0]).reshape((N_CHIPS,) + sharded.shape)
    per_chip = [_unpack_small(allp[ch], [a[n].shape for n in SMALL_SHARDED]) for ch in range(N_CHIPS)]
    for k, n in enumerate(SMALL_SHARDED):
        sp[n] = jnp.concatenate([per_chip[ch][k] for ch in range(N_CHIPS)], axis=1)

    out_weight = _LateWeights(cfg, "out", ("w_out",), [staged["w_out"]], allp)
    ffn_weights = _LateWeights(cfg, "ffn", ffn, [staged[n] for n in ffn], out_weight.token)
    sp["mix_pre_g"] = sp["mix_pre_g"] + (out_weight.token[0, 0] + ffn_weights.token[0, 0])

    state = {}

    def ffn_grads_ready(grads):
        state["ffn_pairs"], token = _pair_exchange_start("ffn", [_grad_to_chips(cfg, n, grads[n]) for n in ffn_grads])
        return token

    def pair_sums(names, grads, theirs):
        return [_pair_sum(n, g, t, core) for n, g, t in zip(names, grads, theirs)]

    def early_grads_ready(grads):
        g_out = [_grad_to_chips(cfg, "w_out", grads["w_out"])]
        g_ffn, t_ffn = _pair_exchange_wait("ffn", state["ffn_pairs"], g_out[0])
        sums = pair_sums(ffn_grads, g_ffn, t_ffn) + pair_sums(["w_out"], g_out, _pair_exchange("out", g_out))
        state["early"] = _chip_exchange_start("early", sums)
        return state["early"][-1]

    def reduced_halves(tag, names, after):
        send_sems, recv_sems, s_bufs, l_bufs, _ = state[tag]
        s_bufs, l_bufs = _chip_exchange_wait(tag, send_sems, recv_sems, s_bufs, l_bufs, after)
        return [_chip_sum(n, s, t, chip) for n, s, t in zip(names, s_bufs, l_bufs)]

    def in_grad_ready(grads):
        grads = [_grad_to_chips(cfg, n, grads[n]) for n in first]
        state["rest"] = _chip_exchange_start("rest", pair_sums(first, grads, _pair_exchange("rest", grads)))
        return state["rest"][-1]

    ffn_grads = ("w_down", "w_gate", "w_up")
    early = ffn_grads + ("w_out",)
    loss, grad_x, gW, gs = _local_grads(cfg, a["x"], a["loss_target"], W, sp, out_weight, ffn_weights,
                                        ffn_grads_ready, early_grads_ready, in_grad_ready)
    mine = dict(zip(early, reduced_halves("early", early, grad_x)))
    mine.update(zip(first, reduced_halves("rest", first, mine[early[0]])))
    theirs = dict(zip(big, _sibling_exchange([mine[n] for n in big])))

    shapes = [gs[n].shape for n in SMALL] + [(1, LANE)]
    red = _unpack_small(_allreduce_small("allreduce_small", _pack_small([gs[n] for n in SMALL] + [loss]), theirs[big[0]]), shapes)
    g_small = dict(zip(SMALL, red[:-1]))
    for n in SMALL_SHARDED:
        cs = a[n].shape[1]
        g_small[n] = lax.dynamic_slice_in_dim(g_small[n], chip * cs, cs, axis=1)

    out = {"loss": red[-1][0, 0], "grad_x": grad_x}
    for n in big:
        out["grad_" + n], out["delta_" + n], out["new_m_" + n], out["new_v_" + n] = _adamw_halves(
            "adamw_" + n, a[n], mine[n], theirs[n], a["m_" + n], a["v_" + n], core)
    sshapes = [a[n].shape for n in SMALL]
    d, nm, nv = _adamw("adamw_small", _pack_small([a[n] for n in SMALL]), _pack_small([g_small[n] for n in SMALL]),
                       _pack_small([a["m_" + n] for n in SMALL]), _pack_small([a["v_" + n] for n in SMALL]))
    for n, dd, mm, vv in zip(SMALL, _unpack_small(d, sshapes), _unpack_small(nm, sshapes), _unpack_small(nv, sshapes)):
        out["grad_" + n], out["delta_" + n], out["new_m_" + n], out["new_v_" + n] = g_small[n], dd, mm, vv
    return out


def kernel(x, mix_pre_g, w_in, q_norm_g, w_uq, kv_norm_g, w_ukv, ssm_conv_w, ssm_conv_b, dt_bias, a_log, d_skip, ssm_norm_g, w_out, mix_post_g, ffn_pre_g, w_gate, w_up, ffn_conv_w, ffn_conv_b, w_down, ffn_post_g, loss_target, m_mix_pre_g, m_w_in, m_q_norm_g, m_w_uq, m_kv_norm_g, m_w_ukv, m_ssm_conv_w, m_ssm_conv_b, m_dt_bias, m_a_log, m_d_skip, m_ssm_norm_g, m_w_out, m_mix_post_g, m_ffn_pre_g, m_w_gate, m_w_up, m_ffn_conv_w, m_ffn_conv_b, m_w_down, m_ffn_post_g, v_mix_pre_g, v_w_in, v_q_norm_g, v_w_uq, v_kv_norm_g, v_w_ukv, v_ssm_conv_w, v_ssm_conv_b, v_dt_bias, v_a_log, v_d_skip, v_ssm_norm_g, v_w_out, v_mix_post_g, v_ffn_pre_g, v_w_gate, v_w_up, v_ffn_conv_w, v_ffn_conv_b, v_w_down, v_ffn_post_g):
    args = dict(locals())
    def given(k, v):
        if k in ("w_in", "m_w_in", "v_w_in"):
            return jnp.transpose(v, (2, 0, 1))
        return v if k.removeprefix("m_").removeprefix("v_") in BIG or v.ndim < 3 else v[0]

    out = _step(_FULL, {k: given(k, v) for k, v in args.items()})
    res = [out["loss"], out["grad_x"][None]]
    for pre in ("grad_", "delta_", "new_m_", "new_v_"):
        for n in WEIGHTS:
            o = out[pre + n]
            res.append(jnp.transpose(o, (1, 2, 0)) if n == "w_in" else o if n in BIG or args[n].ndim < 3 else o[None])
    return tuple(res)
```

```python
import functools
import math

import jax
import jax.numpy as jnp
from jax import lax
from jax.experimental import pallas as pl
from jax.experimental.pallas import tpu as pltpu

F32, BF16, I32 = jnp.float32, jnp.bfloat16, jnp.int32
NN = (((1,), (0,)), ((), ()))
NT = (((1,), (1,)), ((), ()))
TN = (((0,), (0,)), ((), ()))
HI = lax.Precision.HIGHEST
MESH_ID = pl.DeviceIdType.MESH

EPS = 1e-6
CHUNK = 64
NOPE, ROPE, VH = 128, 64, 128
ROPE_THETA = 10000.0
HP, NST = 64, 128
SSM_K, FFN_K = 4, 3
LANE = 128
N_CHIPS = 4
VMEM_LIMIT = 52 * 1024 * 1024
MM_TILE, MM_TILE_K = 1408, 2816

ADAM_LR, ADAM_B1, ADAM_B2, ADAM_EPS, ADAM_WD, ADAM_STEP = 0.001, 0.9, 0.999, 1e-08, 0.01, 10


class _Cfg:
    def __init__(self, S, D, QL, KVL, H, HS, G, DFF, T):
        self.S, self.D, self.QL, self.KVL, self.H, self.HS, self.G, self.DFF, self.T = S, D, QL, KVL, H, HS, G, DFF, T
        self.INNER = HS * HP
        self.CONVCH = self.INNER + 2 * G * NST
        self.QW = H * (NOPE + ROPE)
        self.KVW = H * (NOPE + VH)
        self.MLAW = H * VH
        self.MIXW = self.MLAW + self.INNER
        self.IN_COLS = QL + KVL + ROPE + self.INNER + self.CONVCH + HS
        self.o_kr = QL + KVL
        self.o_z = self.o_kr + LANE
        self.o_xbc = self.o_z + self.INNER
        self.o_dt = self.o_xbc + self.CONVCH
        self.EXT = self.o_dt + LANE
        self.NPAIR = HS // 2
        self.REP = HS // G


_FULL = _Cfg(S=2048, D=2048, QL=768, KVL=512, H=8, HS=16, G=2, DFF=5632, T=256)
BIG = ("w_in", "w_uq", "w_ukv", "w_out", "w_gate", "w_up", "w_down")

SMALL = ("mix_pre_g", "q_norm_g", "kv_norm_g", "ssm_conv_w", "ssm_conv_b", "dt_bias", "a_log", "d_skip", "ssm_norm_g",
         "mix_post_g", "ffn_pre_g", "ffn_conv_w", "ffn_conv_b", "ffn_post_g")
SMALL_SHARDED = ("ssm_conv_w", "ffn_conv_w")
WEIGHTS = ("mix_pre_g", "w_in", "q_norm_g", "w_uq", "kv_norm_g", "w_ukv", "ssm_conv_w", "ssm_conv_b", "dt_bias", "a_log",
           "d_skip", "ssm_norm_g", "w_out", "mix_post_g", "ffn_pre_g", "w_gate", "w_up", "ffn_conv_w", "ffn_conv_b",
           "w_down", "ffn_post_g")


def _pick(n, target, mult):
    best = None
    for d in range(mult, min(n, target) + 1, mult):
        if n % d == 0:
            best = d
    return best if best is not None else n


def _params(sem=None):
    kw = dict(vmem_limit_bytes=VMEM_LIMIT)
    if sem is not None:
        kw["dimension_semantics"] = sem
    return pltpu.CompilerParams(**kw)


def _dot(a, b, dims=NN, precision=None):
    return lax.dot_general(a, b, dims, preferred_element_type=F32, precision=precision)


def _sigmoid(x):
    return 1.0 / (1.0 + jnp.exp(-x))


def _rs(x):
    return lax.rsqrt(jnp.mean(x * x, axis=-1, keepdims=True) + EPS)


def _rms_back(xh, r, dn):
    return r * (dn - xh * jnp.mean(dn * xh, axis=-1, keepdims=True))


def _colsum(v):
    return jnp.sum(v, axis=0, keepdims=True)


def _matmul(name, a, b, mode, out_dtype, a2=None, b2=None, chips=False):
    cs = None
    if mode == "nn":
        (M, K), N = a.shape, b.shape[-1]
        if chips:
            cs, N = N, N_CHIPS * N
    elif mode == "nt":
        (M, K), N = a.shape, b.shape[-2]
        if chips:
            cs = b.shape[-1]
    else:
        (K, M), N = a.shape, b.shape[1]
        if chips:
            cs = N // N_CHIPS
    tm = _pick(M, MM_TILE, LANE)
    tn = _pick(cs if chips and mode != "nt" else N, MM_TILE, LANE)
    tk = _pick(cs, MM_TILE, LANE) if chips and mode == "nt" else _pick(K, MM_TILE_K, LANE)
    nk = K // tk
    dims = {"nn": NN, "nt": NT, "tn": TN}[mode]
    a_spec = pl.BlockSpec((tk, tm), lambda i, j, k: (k, i)) if mode == "tn" else pl.BlockSpec((tm, tk), lambda i, j, k: (i, k))
    b_spec = pl.BlockSpec((tn, tk), lambda i, j, k: (j, k)) if mode == "nt" else pl.BlockSpec((tk, tn), lambda i, j, k: (k, j))
    o_spec = pl.BlockSpec((tm, tn), lambda i, j, k: (i, j))
    o_shape = (M, N)
    if chips and mode == "nn":
        per = cs // tn
        b_spec = pl.BlockSpec((None, tk, tn), lambda i, j, k: (j // per, k, j % per))
    elif chips and mode == "nt":
        per = cs // tk
        b_spec = pl.BlockSpec((None, tn, tk), lambda i, j, k: (k // per, j, k % per))
    elif chips:
        per = cs // tn
        o_spec = pl.BlockSpec((None, tm, tn), lambda i, j, k: (j // per, i, j % per))
        o_shape = (N_CHIPS, M, cs)
    two = a2 is not None

    def product(refs):
        part = _dot(refs[0][...].astype(BF16), refs[1][...].astype(BF16), dims)
        if two:
            part += _dot(refs[2][...].astype(BF16), refs[3][...].astype(BF16), dims)
        return part

    def body_whole_k(*refs):
        refs[-1][...] = product(refs).astype(refs[-1].dtype)

    def body(*refs):
        o_ref, acc_ref = refs[-2], refs[-1]
        k = pl.program_id(2)

        @pl.when(k == 0)
        def _():
            acc_ref[...] = product(refs)

        @pl.when(k > 0)
        def _():
            acc_ref[...] += product(refs)

        @pl.when(k == nk - 1)
        def _():
            o_ref[...] = acc_ref[...].astype(o_ref.dtype)

    ins = (a, b, a2, b2) if two else (a, b)
    return pl.pallas_call(
        body_whole_k if nk == 1 else body, name=name, grid=(M // tm, N // tn, nk),
        in_specs=[a_spec, b_spec] * (2 if two else 1),
        out_specs=o_spec,
        out_shape=jax.ShapeDtypeStruct(o_shape, out_dtype),
        scratch_shapes=[] if nk == 1 else [pltpu.VMEM((tm, tn), F32)],
        compiler_params=_params(("parallel", "parallel", "arbitrary")),
    )(*ins)


def _rowwise(name, fn, rows, mats, outs, reds, ts):
    S = rows[0].shape[0]
    nr, nm, no = len(rows), len(mats), len(outs)

    def body(*refs):
        res = fn(*[r[...] for r in refs[:nr + nm]])
        res = res if isinstance(res, (tuple, list)) else (res,)
        for r, v in zip(refs[nr + nm:nr + nm + no], res[:no]):
            r[...] = v.astype(r.dtype)
        first = pl.program_id(0) == 0
        for r, v in zip(refs[nr + nm + no:], res[no:]):
            @pl.when(first)
            def _():
                r[...] = jnp.broadcast_to(v, r.shape)

            @pl.when(jnp.logical_not(first))
            def _():
                r[...] += jnp.broadcast_to(v, r.shape)

    in_specs = [pl.BlockSpec((ts, a.shape[1]), lambda i: (i, 0)) for a in rows]
    in_specs += [pl.BlockSpec(m.shape, lambda i, nd=m.ndim: (0,) * nd) for m in mats]
    out_specs = [pl.BlockSpec((ts, w), lambda i: (i, 0)) for w, _ in outs]
    out_specs += [pl.BlockSpec(s, lambda i: (0, 0)) for s in reds]
    out_shape = [jax.ShapeDtypeStruct((S, w), dt) for w, dt in outs] + [jax.ShapeDtypeStruct(s, F32) for s in reds]
    return pl.pallas_call(
        body, name=name, grid=(S // ts,), in_specs=in_specs, out_specs=out_specs, out_shape=out_shape,
        compiler_params=_params(("arbitrary",) if reds else ("parallel",)),
    )(*rows, *mats)


def _shift_down(v, s):
    if s == 0:
        return v
    rows = lax.broadcasted_iota(I32, v.shape, 0)
    return jnp.where(rows >= s, pltpu.roll(v, s, 0), 0.0)


def _shift_up(v, s):
    if s == 0:
        return v
    n = v.shape[0]
    rows = lax.broadcasted_iota(I32, v.shape, 0)
    return jnp.where(rows < n - s, pltpu.roll(v, n - s, 0), 0.0)


def _conv(x, w, b):
    K = w.shape[0]
    y = jnp.broadcast_to(b, x.shape)
    for k in range(K):
        y = y + w[k:k + 1, :] * _shift_down(x, K - 1 - k)
    return y


def _conv_back(x, w, dc):
    K = w.shape[0]
    dx = jnp.zeros_like(x)
    dw = []
    for k in range(K):
        dx = dx + w[k:k + 1, :] * _shift_up(dc, K - 1 - k)
        dw.append(_colsum(dc * _shift_down(x, K - 1 - k)))
    return dx, jnp.concatenate(dw, axis=0), _colsum(dc)


def _colwise(name, fn, cols, vecs, outs, pouts, tc):
    S, C = cols[0].shape
    nc_, nv, no = len(cols), len(vecs), len(outs)

    def body(*refs):
        res = fn(*[r[...] for r in refs[:nc_ + nv]])
        res = res if isinstance(res, (tuple, list)) else (res,)
        for r, v in zip(refs[nc_ + nv:], res):
            r[...] = v.astype(r.dtype)

    in_specs = [pl.BlockSpec((S, tc), lambda j: (0, j)) for _ in cols]
    in_specs += [pl.BlockSpec((v.shape[0], tc), lambda j: (0, j)) for v in vecs]
    out_specs = [pl.BlockSpec((S, tc), lambda j: (0, j)) for _ in outs] + [pl.BlockSpec((k, tc), lambda j: (0, j)) for k in pouts]
    out_shape = [jax.ShapeDtypeStruct((S, C), dt) for dt in outs] + [jax.ShapeDtypeStruct((k, C), F32) for k in pouts]
    return pl.pallas_call(
        body, name=name, grid=(C // tc,), in_specs=in_specs, out_specs=out_specs, out_shape=out_shape,
        compiler_params=_params(("parallel",)),
    )(*cols, *vecs)


_G0, _G1 = math.sqrt(2.0 / math.pi), 0.044715


def _gelu(g):
    th = jnp.tanh(_G0 * (g + _G1 * g * g * g))
    return 0.5 * g * (1.0 + th), th


def _ffn_act(gate_pre, up, w, b):
    act, _ = _gelu(_conv(gate_pre, w, b))
    return act * up


def _ffn_act_back(dact, gate_pre, up, w, b):
    g = _conv(gate_pre, w, b)
    ge, th = _gelu(g)
    dge = 0.5 * (1.0 + th) + 0.5 * g * (1.0 - th * th) * _G0 * (1.0 + 3.0 * _G1 * g * g)
    dup = dact * ge
    dgate_pre, dw, db = _conv_back(gate_pre, w, dact * up * dge)
    return dgate_pre, dup, dw, db


def _ssm_act(xbc, w, b):
    c = _conv(xbc, w, b)
    return c * _sigmoid(c)


def _ssm_act_back(dxc, xbc, w, b):
    c = _conv(xbc, w, b)
    sg = _sigmoid(c)
    return _conv_back(xbc, w, dxc * sg * (1.0 + c * (1.0 - sg)))


def _rope_tables(S):
    inv = 1.0 / (ROPE_THETA ** (jnp.arange(0, ROPE, 2, dtype=F32) / ROPE))
    ang = jnp.arange(S, dtype=F32)[:, None] * inv[None, :]
    cos, sin = jnp.cos(ang), jnp.sin(ang)
    return jnp.tile(cos, (1, 4)), jnp.tile(jnp.concatenate([-sin, sin], axis=1), (1, 2))


def _swap_halves(x):
    lane = lax.broadcasted_iota(I32, x.shape, 1)
    w = x.shape[1]
    return jnp.where((lane % ROPE) < ROPE // 2, pltpu.roll(x, w - ROPE // 2, 1), pltpu.roll(x, ROPE // 2, 1))


def _rot(x, cos2, sin2):
    return x * cos2 + _swap_halves(x) * sin2


def _rot_back(dy, cos2, sin2):
    return dy * cos2 + _swap_halves(dy * sin2)


def _mla_pack(cfg, q, kv, kr, cos2, sin2):
    S, H = cfg.S, cfg.H
    ts = _pick(S, 512, 8)

    def body(qn_ref, qr_ref, kn_ref, v_ref, kr_ref, c_ref, s_ref, Q_ref, K_ref, V_ref):
        h = pl.program_id(0)
        c2, s2 = c_ref[...], s_ref[...]
        Q_ref[0, :, 0:LANE] = qn_ref[...].astype(BF16)
        Q_ref[0, :, LANE:] = _rot(qr_ref[...], c2, s2).astype(BF16)
        K_ref[0, :, 0:LANE] = kn_ref[...].astype(BF16)
        krr = _rot(kr_ref[...], c2, s2)
        K_ref[0, :, LANE:] = jnp.where(h % 2 == 1, pltpu.roll(krr, ROPE, 1), krr).astype(BF16)
        V_ref[0] = v_ref[...].astype(BF16)

    blk = lambda f: pl.BlockSpec((ts, LANE), f)
    return pl.pallas_call(
        body, name="mla_pack", grid=(H, S // ts),
        in_specs=[blk(lambda h, i: (i, h)), blk(lambda h, i: (i, H + h // 2)), blk(lambda h, i: (i, h)),
                  blk(lambda h, i: (i, H + h)), blk(lambda h, i: (i, 0)), blk(lambda h, i: (i, 0)), blk(lambda h, i: (i, 0))],
        out_specs=[pl.BlockSpec((1, ts, 2 * LANE), lambda h, i: (h, i, 0)), pl.BlockSpec((1, ts, 2 * LANE), lambda h, i: (h, i, 0)),
                   pl.BlockSpec((1, ts, LANE), lambda h, i: (h, i, 0))],
        out_shape=[jax.ShapeDtypeStruct((H, S, 2 * LANE), BF16), jax.ShapeDtypeStruct((H, S, 2 * LANE), BF16),
                   jax.ShapeDtypeStruct((H, S, LANE), BF16)],
        compiler_params=_params(("parallel", "parallel")),
    )(q, q, kv, kv, kr, cos2, sin2)


def _mla_unpack(cfg, dQ, dK, dV, cos2, sin2):
    S, H = cfg.S, cfg.H
    ts = _pick(S, 256, 8)

    def body(dQ_ref, dK_ref, dV_ref, c_ref, s_ref, dq_ref, dkv_ref, dkr_ref):
        c2, s2 = c_ref[...], s_ref[...]
        lo = lax.broadcasted_iota(I32, (ts, LANE), 1) < ROPE
        tk = jnp.zeros((ts, LANE), F32)
        for h in range(H):
            dq_ref[:, h * LANE:(h + 1) * LANE] = dQ_ref[h, :, 0:LANE].astype(BF16)
            dkv_ref[:, h * LANE:(h + 1) * LANE] = dK_ref[h, :, 0:LANE].astype(BF16)
            dkv_ref[:, (H + h) * LANE:(H + h + 1) * LANE] = dV_ref[h].astype(BF16)
            own = lo if h % 2 == 0 else jnp.logical_not(lo)
            tk = tk + jnp.where(own, dK_ref[h, :, LANE:], 0.0)
        for j in range(H // 2):
            dr = dQ_ref[2 * j, :, LANE:] + dQ_ref[2 * j + 1, :, LANE:]
            dq_ref[:, (H + j) * LANE:(H + j + 1) * LANE] = _rot_back(dr, c2, s2).astype(BF16)
        dkr_rot = jnp.where(lo, tk + pltpu.roll(tk, ROPE, 1), 0.0)
        dkr_ref[...] = _rot_back(dkr_rot, c2, s2).astype(BF16)

    tab = pl.BlockSpec((ts, LANE), lambda i: (i, 0))
    return pl.pallas_call(
        body, name="mla_unpack", grid=(S // ts,),
        in_specs=[pl.BlockSpec((H, ts, 2 * LANE), lambda i: (0, i, 0)), pl.BlockSpec((H, ts, 2 * LANE), lambda i: (0, i, 0)),
                  pl.BlockSpec((H, ts, LANE), lambda i: (0, i, 0)), tab, tab],
        out_specs=[pl.BlockSpec((ts, cfg.QW), lambda i: (i, 0)), pl.BlockSpec((ts, cfg.KVW), lambda i: (i, 0)), tab],
        out_shape=[jax.ShapeDtypeStruct((S, cfg.QW), BF16), jax.ShapeDtypeStruct((S, cfg.KVW), BF16),
                   jax.ShapeDtypeStruct((S, LANE), BF16)],
        compiler_params=_params(("parallel",)),
    )(dQ, dK, dV, cos2, sin2)


_ATT_T = 256
_ATT_HB = 2
_ATT_SCALE = (NOPE + ROPE) ** -0.5


def _diag_mask(transposed=False):
    r = lax.broadcasted_iota(I32, (_ATT_T, _ATT_T), 0) // CHUNK
    c = lax.broadcasted_iota(I32, (_ATT_T, _ATT_T), 1) // CHUNK
    return r <= c if transposed else c <= r


def _row_form(col):
    return jnp.broadcast_to(col, (col.shape[0], LANE)).T[0:8, :]


def _attn_fwd(cfg, Q, K, V):
    S, H, T, HB = cfg.S, cfg.H, _ATT_T, _ATT_HB

    def body(q_ref, k_ref, v_ref, o_ref, lse_ref, lse_t_ref):
        qi = pl.program_id(1)

        def head_step(b, kb, carry, mask):
            m, l, acc = carry
            ks = pl.multiple_of(kb * T, T)
            s = _dot(q_ref[b], k_ref[b, pl.ds(ks, T), :], NT) * _ATT_SCALE
            if mask is not None:
                s = jnp.where(mask, s, -1e30)
            m_new = jnp.maximum(m, jnp.max(s, axis=1, keepdims=True))
            p = jnp.exp(s - m_new)
            alpha = jnp.exp(m - m_new)
            l = alpha * l + jnp.sum(p, axis=1, keepdims=True)
            acc = alpha * acc + _dot(p.astype(BF16), v_ref[b, pl.ds(ks, T), :])
            return m_new, l, acc

        def step(kb, carry, mask=None):
            return tuple(head_step(b, kb, carry[b], mask) for b in range(HB))

        init = (jnp.full((T, 1), -1e30, F32), jnp.zeros((T, 1), F32), jnp.zeros((T, VH), F32))
        done = step(qi, lax.fori_loop(0, qi, step, (init,) * HB), _diag_mask())
        for b, (m, l, acc) in enumerate(done):
            o_ref[:, b * LANE:(b + 1) * LANE] = acc / l
            lse = m + jnp.log(l)
            lse_ref[:, b * LANE:(b + 1) * LANE] = jnp.broadcast_to(lse, (T, LANE))
            lse_t_ref[b] = _row_form(lse)

    return pl.pallas_call(
        body, name="attn_fwd", grid=(H // HB, S // T),
        in_specs=[pl.BlockSpec((HB, T, 2 * LANE), lambda h, i: (h, i, 0)), pl.BlockSpec((HB, S, 2 * LANE), lambda h, i: (h, 0, 0)),
                  pl.BlockSpec((HB, S, LANE), lambda h, i: (h, 0, 0))],
        out_specs=[pl.BlockSpec((T, HB * LANE), lambda h, i: (i, h)), pl.BlockSpec((T, HB * LANE), lambda h, i: (i, h)),
                   pl.BlockSpec((HB, 8, T), lambda h, i: (h, 0, i))],
        out_shape=[jax.ShapeDtypeStruct((S, H * LANE), F32), jax.ShapeDtypeStruct((S, H * LANE), F32),
                   jax.ShapeDtypeStruct((H, 8, S), F32)],
        compiler_params=_params(("parallel", "parallel")),
    )(Q, K, V)


def _attn_dq(cfg, Q, K, V, do, o, lse, after):
    S, H, T, HB = cfg.S, cfg.H, _ATT_T, _ATT_HB

    def body(q_ref, k_ref, v_ref, do_ref, o_ref, lse_ref, after_ref, dq_ref, dl_t_ref):
        qi = pl.program_id(1)
        do = [do_ref[:, b * LANE:(b + 1) * LANE] for b in range(HB)]
        delta = [jnp.sum(do[b] * o_ref[:, b * LANE:(b + 1) * LANE], axis=1, keepdims=True) for b in range(HB)]
        dob = [d.astype(BF16) for d in do]

        def head_step(b, kb, dq, mask):
            ks = pl.multiple_of(kb * T, T)
            k = k_ref[b, pl.ds(ks, T), :]
            s = _dot(q_ref[b], k, NT) * _ATT_SCALE
            if mask is not None:
                s = jnp.where(mask, s, -1e30)
            p = jnp.exp(s - lse_ref[:, b * LANE:b * LANE + 1])
            dp = _dot(dob[b], v_ref[b, pl.ds(ks, T), :], NT)
            ds = p * (dp - delta[b]) * _ATT_SCALE
            return dq + _dot(ds.astype(BF16), k)

        def step(kb, dqs, mask=None):
            return tuple(head_step(b, kb, dqs[b], mask) for b in range(HB))

        dqs = step(qi, lax.fori_loop(0, qi, step, (jnp.zeros((T, 2 * LANE), F32),) * HB), _diag_mask())
        for b in range(HB):
            dq_ref[b] = dqs[b]
            dl_t_ref[b] = _row_form(delta[b])

    col = pl.BlockSpec((T, HB * LANE), lambda h, i: (i, h))
    return pl.pallas_call(
        body, name="attn_dq", grid=(H // HB, S // T),
        in_specs=[pl.BlockSpec((HB, T, 2 * LANE), lambda h, i: (h, i, 0)), pl.BlockSpec((HB, S, 2 * LANE), lambda h, i: (h, 0, 0)),
                  pl.BlockSpec((HB, S, LANE), lambda h, i: (h, 0, 0)), col, col, col, _ANY],
        out_specs=[pl.BlockSpec((HB, T, 2 * LANE), lambda h, i: (h, i, 0)), pl.BlockSpec((HB, 8, T), lambda h, i: (h, 0, i))],
        out_shape=[jax.ShapeDtypeStruct((H, S, 2 * LANE), F32), jax.ShapeDtypeStruct((H, 8, S), F32)],
        compiler_params=_params(("parallel", "parallel")),
    )(Q, K, V, do, o, lse, after)


def _attn_dkv(cfg, Q, K, V, do, lse_t, delta_t):
    S, H, T, HB = cfg.S, cfg.H, _ATT_T, _ATT_HB
    nq = S // T

    def body(q_ref, k_ref, v_ref, do_ref, lse_ref, dl_ref, dk_ref, dv_ref):
        kb = pl.program_id(1)

        def head_step(b, qi, carry, mask):
            dk, dv = carry
            qs = pl.multiple_of(qi * T, T)
            q = q_ref[b, pl.ds(qs, T), :]
            dob = do_ref[pl.ds(qs, T), b * LANE:(b + 1) * LANE].astype(BF16)
            s = _dot(k_ref[b], q, NT) * _ATT_SCALE
            if mask is not None:
                s = jnp.where(mask, s, -1e30)
            p = jnp.exp(s - lse_ref[b, 0:1, pl.ds(qs, T)])
            dv = dv + _dot(p.astype(BF16), dob)
            dp = _dot(v_ref[b], dob, NT)
            ds = p * (dp - dl_ref[b, 0:1, pl.ds(qs, T)]) * _ATT_SCALE
            dk = dk + _dot(ds.astype(BF16), q)
            return dk, dv

        def step(qi, carry, mask=None):
            return tuple(head_step(b, qi, carry[b], mask) for b in range(HB))

        zero = (jnp.zeros((T, 2 * LANE), F32), jnp.zeros((T, VH), F32))
        done = lax.fori_loop(kb + 1, nq, step, step(kb, (zero,) * HB, _diag_mask(transposed=True)))
        for b, (dk, dv) in enumerate(done):
            dk_ref[b] = dk
            dv_ref[b] = dv

    row = pl.BlockSpec((HB, 8, S), lambda h, j: (h, 0, 0))
    return pl.pallas_call(
        body, name="attn_dkv", grid=(H // HB, S // T),
        in_specs=[pl.BlockSpec((HB, S, 2 * LANE), lambda h, j: (h, 0, 0)), pl.BlockSpec((HB, T, 2 * LANE), lambda h, j: (h, j, 0)),
                  pl.BlockSpec((HB, T, LANE), lambda h, j: (h, j, 0)), pl.BlockSpec((S, HB * LANE), lambda h, j: (0, h)), row, row],
        out_specs=[pl.BlockSpec((HB, T, 2 * LANE), lambda h, j: (h, j, 0)), pl.BlockSpec((HB, T, LANE), lambda h, j: (h, j, 0))],
        out_shape=[jax.ShapeDtypeStruct((H, S, 2 * LANE), F32), jax.ShapeDtypeStruct((H, S, LANE), F32)],
        compiler_params=_params(("parallel", "parallel")),
    )(Q, K, V, do, lse_t, delta_t)


def _expand_matrix(cfg):
    r = lax.broadcasted_iota(I32, (LANE, cfg.INNER), 0)
    c = lax.broadcasted_iota(I32, (LANE, cfg.INNER), 1)
    return (r == c // HP).astype(F32)


def _softplus(x):
    return jnp.maximum(x, 0.0) + jnp.log(1.0 + jnp.exp(-jnp.abs(x)))


def _ssd_prep(cfg, dt_raw, dt_bias_pad, a_log_pad, expand):
    HS = cfg.HS

    def fn(raw, bias, alog, E):
        heads = lax.broadcasted_iota(I32, raw.shape, 1) < HS
        dt = jnp.where(heads, _softplus(raw + bias), 0.0)
        a = dt * jnp.where(heads[0:1], -jnp.exp(alog), 0.0)
        return dt, a, _dot(dt, E, precision=HI), _dot(a, E, precision=HI)

    return _rowwise("ssd_prep", fn, [dt_raw], [dt_bias_pad, a_log_pad, expand],
                    [(LANE, F32), (LANE, F32), (cfg.INNER, F32), (cfg.INNER, F32)], [], _pick(cfg.S, 512, 8))


def _tril(T):
    return lax.broadcasted_iota(I32, (T, T), 0) >= lax.broadcasted_iota(I32, (T, T), 1)


def _ssd_fwd(cfg, xc, dt_exp, a_exp, a_small, dskip_exp):
    S, T, INNER, G, NPAIR = cfg.S, cfg.T, cfg.INNER, cfg.G, cfg.NPAIR
    NC = S // T

    def body(xc_ref, dte_ref, ae_ref, as_ref, dsk_ref, y_ref, hin_ref, ht_ref):
        @pl.when(pl.program_id(0) == 0)
        def _():
            ht_ref[...] = jnp.zeros_like(ht_ref)

        tril = _tril(T)
        tri = tril.astype(F32)
        acs_s = _dot(tri, as_ref[...], precision=HI)
        acs_e = _dot(tri, ae_ref[...], precision=HI)
        acs_t = acs_s.T
        lo = lax.broadcasted_iota(I32, (T, LANE), 1) < HP
        for g in range(G):
            Bb = xc_ref[:, INNER + g * NST:INNER + (g + 1) * NST].astype(BF16)
            Cb = xc_ref[:, INNER + (G + g) * NST:INNER + (G + g + 1) * NST].astype(BF16)
            Gm = _dot(Cb, Bb, NT)
            for j in range(g * NPAIR // G, (g + 1) * NPAIR // G):
                sl = slice(j * LANE, (j + 1) * LANE)
                Xp = xc_ref[:, sl]
                Xdt = Xp * dte_ref[:, sl]
                Xb = Xdt.astype(BF16)
                acs_p = acs_e[:, sl]
                last = acs_p[T - 1:T, :]
                Hin = ht_ref[j]
                hin_ref[0, j] = Hin
                yd = []
                for e in (0, 1):
                    h = 2 * j + e
                    Lm = jnp.exp(jnp.where(tril, acs_s[:, h:h + 1] - acs_t[h:h + 1, :], -1e30))
                    yd.append(_dot((Gm * Lm).astype(BF16), Xb))
                y_off = _dot(Cb, Hin.astype(BF16)) * jnp.exp(acs_p)
                y_ref[:, sl] = jnp.where(lo, yd[0], yd[1]) + y_off + Xp * dsk_ref[:, sl]
                st = _dot(Bb, (Xdt * jnp.exp(last - acs_p)).astype(BF16), TN)
                ht_ref[j] = jnp.exp(last) * Hin + st

    rows = lambda w: pl.BlockSpec((T, w), lambda c: (c, 0))
    return pl.pallas_call(
        body, name="ssd_fwd", grid=(NC,),
        in_specs=[rows(cfg.CONVCH), rows(INNER), rows(INNER), rows(LANE), pl.BlockSpec((1, INNER), lambda c: (0, 0))],
        out_specs=[rows(INNER), pl.BlockSpec((1, NPAIR, NST, LANE), lambda c: (c, 0, 0, 0))],
        out_shape=[jax.ShapeDtypeStruct((S, INNER), F32), jax.ShapeDtypeStruct((NC, NPAIR, NST, LANE), F32)],
        scratch_shapes=[pltpu.VMEM((NPAIR, NST, LANE), F32)],
        compiler_params=_params(("arbitrary",)),
    )(xc, dt_exp, a_exp, a_small, dskip_exp)


def _ssd_bwd(cfg, dy, xc, dt_exp, a_exp, a_small, dskip_exp, hin, dt_raw, dt_bias_pad, a_log_pad, expand):
    S, T, INNER, G, NPAIR, HS = cfg.S, cfg.T, cfg.INNER, cfg.G, cfg.NPAIR, cfg.HS
    NC = S // T

    def body(dy_ref, xc_ref, dte_ref, ae_ref, as_ref, dsk_ref, hin_ref, raw_ref, bias_ref, alog_ref, e_ref,
             dxc_ref, draw_ref, dbias_ref, dalog_ref, dskip_ref, dht_ref, cols_ref, rows_ref, dacs_ref, ddt_ref):
        first = pl.program_id(0) == 0

        @pl.when(first)
        def _():
            dht_ref[...] = jnp.zeros_like(dht_ref)

        tril = _tril(T)
        tri = tril.astype(F32)
        a_s = as_ref[...]
        acs_s = _dot(tri, a_s, precision=HI)
        acs_e = _dot(tri, ae_ref[...], precision=HI)
        acs_t = acs_s.T
        lo = lax.broadcasted_iota(I32, (T, LANE), 1) < HP
        last_row = lax.broadcasted_iota(I32, (T, LANE), 0) == T - 1
        cols_ref[...] = jnp.zeros_like(cols_ref)
        rows_ref[...] = jnp.zeros_like(rows_ref)
        dsk_parts = []
        for g in range(G):
            bsl = slice(INNER + g * NST, INNER + (g + 1) * NST)
            csl = slice(INNER + (G + g) * NST, INNER + (G + g + 1) * NST)
            Bb = xc_ref[:, bsl].astype(BF16)
            Cb = xc_ref[:, csl].astype(BF16)
            Gm = _dot(Cb, Bb, NT)
            dG = jnp.zeros((T, T), F32)
            dB = jnp.zeros((T, NST), F32)
            dC = jnp.zeros((T, NST), F32)
            for j in range(g * NPAIR // G, (g + 1) * NPAIR // G):
                sl = slice(j * LANE, (j + 1) * LANE)
                Xp = xc_ref[:, sl]
                dtp = dte_ref[:, sl]
                Xdt = Xp * dtp
                Xb = Xdt.astype(BF16)
                acs_p = acs_e[:, sl]
                last = acs_p[T - 1:T, :]
                e_p, dec, cd = jnp.exp(acs_p), jnp.exp(last - acs_p), jnp.exp(last)
                Hin = hin_ref[0, j]
                Hb = Hin.astype(BF16)
                dHn = dht_ref[j]
                dHb = dHn.astype(BF16)
                dYp = dy_ref[:, sl]
                z = _dot(Cb, Hb)
                dz = (dYp * e_p).astype(BF16)
                dacs_p = dYp * z * e_p
                dC = dC + _dot(dz, Hb, NT)
                dHin = _dot(Cb, dz, TN) + cd * dHn
                dlast = _colsum(dHn * Hin) * cd
                qv = _dot(Bb, dHb)
                dXdt = qv * dec
                ddec = qv * Xdt * dec
                dacs_p = dacs_p - ddec
                dlast = dlast + _colsum(ddec)
                dB = dB + _dot((Xdt * dec).astype(BF16), dHb, NT)
                for e in (0, 1):
                    h = 2 * j + e
                    Lm = jnp.exp(jnp.where(tril, acs_s[:, h:h + 1] - acs_t[h:h + 1, :], -1e30))
                    Mh = Gm * Lm
                    dYe = jnp.where(lo if e == 0 else jnp.logical_not(lo), dYp, 0.0).astype(BF16)
                    dM = _dot(dYe, Xb, NT)
                    dXdt = dXdt + _dot(Mh.astype(BF16), dYe, TN)
                    W = dM * Mh
                    cols_ref[:, h:h + 1] = jnp.sum(W, axis=1, keepdims=True)
                    rows_ref[h:h + 1, :] = _colsum(W)
                    dG = dG + dM * Lm
                dacs_ref[:, sl] = dacs_p + jnp.where(last_row, dlast, 0.0)
                ddt_ref[:, sl] = dXdt * Xp
                dxc_ref[:, sl] = dXdt * dtp + dYp * dsk_ref[:, sl]
                dsk_parts.append(_colsum(dYp * Xp))
                dht_ref[j] = dHin
            dGb = dG.astype(BF16)
            dxc_ref[:, bsl] = dB + _dot(dGb, Cb, TN)
            dxc_ref[:, csl] = dC + _dot(dGb, Bb)
        E = e_ref[...]
        dacs_s = cols_ref[...] - rows_ref[...].T + _dot(dacs_ref[...], E, NT, precision=HI)
        da = _dot(tri, dacs_s, TN, precision=HI)
        heads = lax.broadcasted_iota(I32, (1, LANE), 1) < HS
        A = jnp.where(heads, -jnp.exp(alog_ref[...]), 0.0)
        ddt = _dot(ddt_ref[...], E, NT, precision=HI) + da * A
        draw = jnp.where(heads, ddt * _sigmoid(raw_ref[...] + bias_ref[...]), 0.0)
        draw_ref[...] = draw
        dsk = _dot(jnp.broadcast_to(jnp.concatenate(dsk_parts, axis=1), (8, INNER)), E, NT, precision=HI)[0:1]
        for ref, val in ((dbias_ref, _colsum(draw)), (dalog_ref, _colsum(da * a_s)), (dskip_ref, dsk)):
            @pl.when(first)
            def _():
                ref[...] = val

            @pl.when(jnp.logical_not(first))
            def _():
                ref[...] += val

    rows = lambda w: pl.BlockSpec((T, w), lambda c: (NC - 1 - c, 0))
    vec = lambda w: pl.BlockSpec((1, w), lambda c: (0, 0))
    return pl.pallas_call(
        body, name="ssd_bwd", grid=(NC,),
        in_specs=[rows(INNER), rows(cfg.CONVCH), rows(INNER), rows(INNER), rows(LANE), vec(INNER),
                  pl.BlockSpec((1, NPAIR, NST, LANE), lambda c: (NC - 1 - c, 0, 0, 0)), rows(LANE), vec(LANE), vec(LANE),
                  pl.BlockSpec((LANE, INNER), lambda c: (0, 0))],
        out_specs=[rows(cfg.CONVCH), rows(LANE), vec(LANE), vec(LANE), vec(LANE)],
        out_shape=[jax.ShapeDtypeStruct((S, cfg.CONVCH), F32), jax.ShapeDtypeStruct((S, LANE), F32)]
        + [jax.ShapeDtypeStruct((1, LANE), F32)] * 3,
        scratch_shapes=[pltpu.VMEM((NPAIR, NST, LANE), F32), pltpu.VMEM((T, LANE), F32), pltpu.VMEM((LANE, T), F32),
                        pltpu.VMEM((T, INNER), F32), pltpu.VMEM((T, INNER), F32)],
        compiler_params=_params(("arbitrary",)),
    )(dy, xc, dt_exp, a_exp, a_small, dskip_exp, hin, dt_raw, dt_bias_pad, a_log_pad, expand)


def _ssd_post(cfg, y, z, norm_g):
    W = cfg.INNER // cfg.G

    def fn(y, z, g):
        yz = y * z * _sigmoid(z)
        return jnp.concatenate([yz[:, i * W:(i + 1) * W] * _rs(yz[:, i * W:(i + 1) * W]) for i in range(cfg.G)], axis=1) * g

    return _rowwise("ssd_post", fn, [y, z], [norm_g], [(cfg.INNER, BF16)], [], _pick(cfg.S, 256, 8))[0]


def _ssd_post_bwd(cfg, db, y, z, norm_g):
    W = cfg.INNER // cfg.G

    def fn(db, y, z, g):
        sg = _sigmoid(z)
        yz = y * z * sg
        dn = db * g
        dyz, nh = [], []
        for i in range(cfg.G):
            seg = yz[:, i * W:(i + 1) * W]
            r = _rs(seg)
            nh.append(seg * r)
            dyz.append(_rms_back(nh[-1], r, dn[:, i * W:(i + 1) * W]))
        dyz = jnp.concatenate(dyz, axis=1)
        return dyz * z * sg, dyz * y * sg * (1.0 + z * (1.0 - sg)), _colsum(db * jnp.concatenate(nh, axis=1))

    return _rowwise("ssd_post_bwd", fn, [db, y, z], [norm_g], [(cfg.INNER, F32), (cfg.INNER, F32)], [(1, cfg.INNER)],
                    _pick(cfg.S, 256, 8))


def _local_grads(cfg, x, tgt, W, sp, out_weight=None, ffn_weights=None, ffn_grads_ready=None, early_grads_ready=None,
                 in_grad_ready=None):
    S, D, H, INNER = cfg.S, cfg.D, cfg.H, cfg.INNER
    ts = _pick(S, 256, 8)
    tc = 256

    xn = _rowwise("rms_pre", lambda x, g: x * _rs(x) * g, [x], [sp["mix_pre_g"]], [(D, BF16)], [], ts)[0]
    u = _matmul("mm_in", xn, W["w_in"], "nt", F32)
    c_q, c_kv = u[:, :cfg.QL], u[:, cfg.QL:cfg.o_kr]
    kr = u[:, cfg.o_kr:cfg.o_z]
    z = u[:, cfg.o_z:cfg.o_xbc]
    xbc = u[:, cfg.o_xbc:cfg.o_dt]
    dt_raw = u[:, cfg.o_dt:]

    cqn = _rowwise("rms_q", lambda x, g: x * _rs(x) * g, [c_q], [sp["q_norm_g"]], [(cfg.QL, BF16)], [], ts)[0]
    ckvn = _rowwise("rms_kv", lambda x, g: x * _rs(x) * g, [c_kv], [sp["kv_norm_g"]], [(cfg.KVL, BF16)], [], ts)[0]
    q = _matmul("mm_uq", cqn, W["w_uq"], "nn", F32)
    kv = _matmul("mm_ukv", ckvn, W["w_ukv"], "nn", F32)
    cos2, sin2 = _rope_tables(S)
    Qh, Kh, Vh = _mla_pack(cfg, q, kv, kr, cos2, sin2)
    a_out, lse, lse_t = _attn_fwd(cfg, Qh, Kh, Vh)
    if out_weight is not None:
        sp = dict(sp, ssm_conv_b=sp["ssm_conv_b"] + out_weight.pass_on(a_out)[0, 0])

    pad = lambda v: jnp.pad(v, ((0, 0), (0, LANE - v.shape[1])))
    expand = _expand_matrix(cfg)
    dt_bias_pad, a_log_pad = pad(sp["dt_bias"]), pad(sp["a_log"])
    dskip_exp = jnp.repeat(sp["d_skip"], HP, axis=1)
    xc = _colwise("ssm_act", _ssm_act, [xbc], [sp["ssm_conv_w"], sp["ssm_conv_b"]], [F32], [], tc)[0]
    dt_s, a_s, dt_exp, a_exp = _ssd_prep(cfg, dt_raw, dt_bias_pad, a_log_pad, expand)
    y_ssd, hin = _ssd_fwd(cfg, xc, dt_exp, a_exp, a_s, dskip_exp)
    b_out = _ssd_post(cfg, y_ssd, z, sp["ssm_norm_g"])

    ab_out = jnp.concatenate([a_out.astype(BF16), b_out], axis=1)
    if out_weight is not None:
        W = dict(W, **out_weight.arrived(ab_out))
    if ffn_weights is not None:
        sp = dict(sp, mix_post_g=sp["mix_post_g"] + ffn_weights.pass_on(ab_out)[0, 0])
    mix = _matmul("mm_out", ab_out, W["w_out"], "nn", F32)

    def mid(x, mix, g_mp, g_fp):
        x1 = x + mix * _rs(mix) * g_mp
        return x1, x1 * _rs(x1) * g_fp

    x1, h2 = _rowwise("fwd_mid", mid, [x, mix], [sp["mix_post_g"], sp["ffn_pre_g"]], [(D, F32), (D, BF16)], [], ts)
    if ffn_weights is not None:
        W = dict(W, **ffn_weights.arrived(h2))
    gate_pre = _matmul("mm_gate", h2, W["w_gate"], "nn", F32, chips=True)
    up = _matmul("mm_up", h2, W["w_up"], "nn", F32, chips=True)
    act = _colwise("ffn_act", _ffn_act, [gate_pre, up], [sp["ffn_conv_w"], sp["ffn_conv_b"]], [BF16], [], tc)[0]
    f = _matmul("mm_down", act, W["w_down"], "nn", F32)

    def final(x1, f, t, g):
        r = _rs(f)
        fh = f * r
        err = x1 + fh * g - t
        loss = 0.5 * jnp.sum(jnp.mean(err * err, axis=-1, keepdims=True), axis=0, keepdims=True)
        dy = err * (1.0 / D)
        return dy, _rms_back(fh, r, dy * g), _colsum(dy * fh), loss

    dy, df, g_ffn_post, loss = _rowwise("final", final, [x1, f, tgt], [sp["ffn_post_g"]], [(D, F32), (D, BF16)],
                                        [(1, D), (1, LANE)], ts)
    gW = {}
    dact = _matmul("mm_down_dx", df, W["w_down"], "nt", F32)
    gW["w_down"] = _matmul("mm_down_dw", act, df, "tn", BF16)
    dgate, dup, g_ffn_conv_w, g_ffn_conv_b = _colwise(
        "ffn_act_bwd", _ffn_act_back, [dact, gate_pre, up], [sp["ffn_conv_w"], sp["ffn_conv_b"]], [BF16, BF16], [FFN_K, 1], tc)
    gW["w_gate"] = _matmul("mm_gate_dw", h2, dgate, "tn", BF16, chips=True)
    gW["w_up"] = _matmul("mm_up_dw", h2, dup, "tn", BF16, chips=True)
    if ffn_grads_ready is not None:
        sp = dict(sp, ffn_pre_g=sp["ffn_pre_g"] + ffn_grads_ready({n: gW[n] for n in ("w_down", "w_gate", "w_up")})[0, 0])
    dh2 = _matmul("mm_gu_dx", dgate, W["w_gate"], "nt", F32, dup, W["w_up"], chips=True)

    def mid_back(dy, dh2, x1, mix, g_mp, g_fp):
        r2 = _rs(x1)
        xh = x1 * r2
        dx1 = dy + _rms_back(xh, r2, dh2 * g_fp)
        r1 = _rs(mix)
        mh = mix * r1
        return dx1, _rms_back(mh, r1, dx1 * g_mp), _colsum(dh2 * xh), _colsum(dx1 * mh)

    dx1, dmix, g_ffn_pre, g_mix_post = _rowwise("bwd_mid", mid_back, [dy, dh2, x1, mix], [sp["mix_post_g"], sp["ffn_pre_g"]],
                                                [(D, F32), (D, BF16)], [(1, D), (1, D)], ts)
    dab_out = _matmul("mm_out_dx", dmix, W["w_out"], "nt", F32)
    db_out = dab_out[:, cfg.MLAW:]
    gW["w_out"] = _matmul("mm_out_dw", ab_out, dmix, "tn", BF16)
    early_token = jnp.zeros((8, LANE), F32)
    if early_grads_ready is not None:
        early_token = early_grads_ready({n: gW[n] for n in ("w_down", "w_gate", "w_up", "w_out")})
        sp = dict(sp, ssm_norm_g=sp["ssm_norm_g"] + early_token[0, 0])

    dy_ssd, dz, g_ssm_norm = _ssd_post_bwd(cfg, db_out, y_ssd, z, sp["ssm_norm_g"])
    dxc, ddt_raw, g_dt_bias, g_a_log, g_d_skip = _ssd_bwd(cfg, dy_ssd, xc, dt_exp, a_exp, a_s, dskip_exp, hin, dt_raw,
                                                          dt_bias_pad, a_log_pad, expand)
    dxbc, g_ssm_conv_w, g_ssm_conv_b = _colwise("ssm_act_bwd", _ssm_act_back, [dxc, xbc], [sp["ssm_conv_w"], sp["ssm_conv_b"]],
                                                [BF16], [SSM_K, 1], tc)

    dQ, delta_t = _attn_dq(cfg, Qh, Kh, Vh, dab_out, a_out, lse, early_token)
    dK, dV = _attn_dkv(cfg, Qh, Kh, Vh, dab_out, lse_t, delta_t)
    dq, dkv, dkr = _mla_unpack(cfg, dQ, dK, dV, cos2, sin2)
    dcqn = _matmul("mm_uq_dx", dq, W["w_uq"], "nt", F32)
    dckvn = _matmul("mm_ukv_dx", dkv, W["w_ukv"], "nt", F32)
    gW["w_uq"] = _matmul("mm_uq_dw", cqn, dq, "tn", BF16)
    gW["w_ukv"] = _matmul("mm_ukv_dw", ckvn, dkv, "tn", BF16)

    def rms_back(x, dy, g):
        r = _rs(x)
        xh = x * r
        return _rms_back(xh, r, dy * g), _colsum(dy * xh)

    dc_q, g_q_norm = _rowwise("rms_q_bwd", rms_back, [c_q, dcqn], [sp["q_norm_g"]], [(cfg.QL, BF16)], [(1, cfg.QL)], ts)
    dc_kv, g_kv_norm = _rowwise("rms_kv_bwd", rms_back, [c_kv, dckvn], [sp["kv_norm_g"]], [(cfg.KVL, BF16)], [(1, cfg.KVL)], ts)

    du = jnp.concatenate([dc_q, dc_kv, dkr, dz.astype(BF16), dxbc, ddt_raw.astype(BF16)], axis=1)
    gW["w_in"] = _matmul("mm_in_dw", du, xn, "tn", BF16)
    if in_grad_ready is not None:
        token = in_grad_ready({n: gW[n] for n in ("w_in", "w_uq", "w_ukv")})
        sp = dict(sp, mix_pre_g=sp["mix_pre_g"] + token[0, 0])
    dxn = _matmul("mm_in_dx", du, W["w_in"], "nn", F32)

    def first_back(dx1, dxn, x, g):
        r = _rs(x)
        xh = x * r
        return dx1 + _rms_back(xh, r, dxn * g), _colsum(dxn * xh)

    grad_x, g_mix_pre = _rowwise("bwd_first", first_back, [dx1, dxn, x], [sp["mix_pre_g"]], [(D, F32)], [(1, D)], ts)

    gs = dict(mix_pre_g=g_mix_pre, q_norm_g=g_q_norm, kv_norm_g=g_kv_norm, ssm_conv_w=g_ssm_conv_w, ssm_conv_b=g_ssm_conv_b,
              dt_bias=g_dt_bias[:, :cfg.HS], a_log=g_a_log[:, :cfg.HS], d_skip=g_d_skip[:, :cfg.HS], ssm_norm_g=g_ssm_norm,
              mix_post_g=g_mix_post, ffn_pre_g=g_ffn_pre, ffn_conv_w=g_ffn_conv_w, ffn_conv_b=g_ffn_conv_b,
              ffn_post_g=g_ffn_post)
    return loss, grad_x, gW, gs


def _to_kernel_layout(cfg, name, w):
    if name == "w_in":
        a = cfg.o_kr + ROPE
        return jnp.concatenate([w[:a], jnp.zeros((LANE - ROPE, w.shape[1]), w.dtype), w[a:],
                                jnp.zeros((LANE - cfg.HS, w.shape[1]), w.dtype)], axis=0)
    if name in ("w_uq", "w_ukv"):
        per = NOPE + (ROPE if name == "w_uq" else VH)
        return jnp.concatenate([w[:, h * per:h * per + NOPE] for h in range(cfg.H)]
                               + [w[:, h * per + NOPE:(h + 1) * per] for h in range(cfg.H)], axis=1)
    return w


def _from_kernel_layout(cfg, name, g):
    if name == "w_in":
        return jnp.concatenate([g[:cfg.o_kr + ROPE], g[cfg.o_z:cfg.o_dt + cfg.HS]], axis=0)
    if name in ("w_uq", "w_ukv"):
        second = ROPE if name == "w_uq" else VH
        base = cfg.H * NOPE
        parts = []
        for h in range(cfg.H):
            parts += [g[:, h * NOPE:(h + 1) * NOPE], g[:, base + h * second:base + (h + 1) * second]]
        return jnp.concatenate(parts, axis=1)
    return g


def _cols_to_chips(w):
    r, c = w.shape
    return w.reshape(r, N_CHIPS, c // N_CHIPS).transpose(1, 0, 2)


def _chips_to_cols(g):
    k, r, cs = g.shape
    return g.transpose(1, 0, 2).reshape(r, k * cs)


_CHIP_MAJOR = ("w_gate", "w_up")
_RELAYOUT = ("w_uq", "w_ukv")
_LAYOUT_ROWS = 256


def _w_in_layout(cfg, wg):
    _, rs, d = wg.shape
    tc = _pick(d, _LAYOUT_ROWS, LANE)

    def body(w_ref, o_ref):
        o_ref[...] = _to_kernel_layout(cfg, "w_in", jnp.concatenate([w_ref[k] for k in range(N_CHIPS)], axis=0))

    return pl.pallas_call(
        body, name="layout_w_in", grid=(d // tc,),
        in_specs=[pl.BlockSpec((N_CHIPS, rs, tc), lambda j: (0, 0, j))], out_specs=pl.BlockSpec((cfg.EXT, tc), lambda j: (0, j)),
        out_shape=jax.ShapeDtypeStruct((cfg.EXT, d), wg.dtype), compiler_params=_params(("parallel",)),
    )(wg)


def _w_in_grad_to_chips(cfg, g):
    _, d = g.shape
    rs = cfg.IN_COLS // N_CHIPS
    tc = _pick(d, _LAYOUT_ROWS, LANE)

    def body(g_ref, o_ref):
        nat = _from_kernel_layout(cfg, "w_in", g_ref[...])
        for k in range(N_CHIPS):
            o_ref[k] = nat[k * rs:(k + 1) * rs]

    return pl.pallas_call(
        body, name="layout_grad_w_in", grid=(d // tc,),
        in_specs=[pl.BlockSpec((cfg.EXT, tc), lambda j: (0, j))], out_specs=pl.BlockSpec((N_CHIPS, rs, tc), lambda j: (0, 0, j)),
        out_shape=jax.ShapeDtypeStruct((N_CHIPS, rs, d), g.dtype), compiler_params=_params(("parallel",)),
    )(g)


def _gathered_to_kernel(cfg, name, wg):
    if name in _CHIP_MAJOR:
        return wg
    if name == "w_in":
        return _w_in_layout(cfg, wg)
    if name not in _RELAYOUT:
        return wg.reshape(wg.shape[0] * wg.shape[1], wg.shape[2])
    _, rows, cs = wg.shape
    tr = _pick(rows, _LAYOUT_ROWS, 16)

    def body(w_ref, o_ref):
        o_ref[...] = _to_kernel_layout(cfg, name, jnp.concatenate([w_ref[k] for k in range(N_CHIPS)], axis=1))

    wide = jax.eval_shape(lambda w: _to_kernel_layout(cfg, name, w), jax.ShapeDtypeStruct((rows, N_CHIPS * cs), wg.dtype)).shape[1]
    return pl.pallas_call(
        body, name="layout_" + name, grid=(rows // tr,),
        in_specs=[pl.BlockSpec((N_CHIPS, tr, cs), lambda i: (0, i, 0))], out_specs=pl.BlockSpec((tr, wide), lambda i: (i, 0)),
        out_shape=jax.ShapeDtypeStruct((rows, wide), wg.dtype), compiler_params=_params(("parallel",)),
    )(wg)


def _grad_to_chips(cfg, name, g):
    if name in _CHIP_MAJOR:
        return g
    if name == "w_in":
        return _w_in_grad_to_chips(cfg, g)
    if name not in _RELAYOUT:
        return g.reshape(N_CHIPS, g.shape[0] // N_CHIPS, g.shape[1])
    rows, wide = g.shape
    tr = _pick(rows, _LAYOUT_ROWS, 16)
    cs = jax.eval_shape(lambda v: _from_kernel_layout(cfg, name, v), g).shape[1] // N_CHIPS

    def body(g_ref, o_ref):
        nat = _from_kernel_layout(cfg, name, g_ref[...])
        for k in range(N_CHIPS):
            o_ref[k] = nat[:, k * cs:(k + 1) * cs]

    return pl.pallas_call(
        body, name="layout_grad_" + name, grid=(rows // tr,),
        in_specs=[pl.BlockSpec((tr, wide), lambda i: (i, 0))], out_specs=pl.BlockSpec((N_CHIPS, tr, cs), lambda i: (0, i, 0)),
        out_shape=jax.ShapeDtypeStruct((N_CHIPS, rows, cs), g.dtype), compiler_params=_params(("parallel",)),
    )(g)


def _me():
    return lax.axis_index("x"), lax.axis_index("y"), lax.axis_index("c")


def _other_chips(x, y):
    return [(1 - x, y), (x, 1 - y), (1 - x, 1 - y)]


_ANY = pl.BlockSpec(memory_space=pl.ANY)


def _row_block(rows, cols, mult):
    return _pick(rows, max(mult, (1 << 19) // cols // mult * mult), mult)


def _scalar(v):
    return v.astype(I32).reshape(1)


def _blocks2d(r, c, mult):
    if r % mult == 0:
        tr = _row_block(r, c, mult)
        return (tr, c), r // tr, lambda i: (i, 0)
    tc = _pick(c, max(LANE, (1 << 19) // r // LANE * LANE), LANE)
    return (r, tc), c // tc, lambda i: (0, i)


def _by_rows(rows):
    return rows % 32 == 0


def _half_shape(rows, cols):
    return (rows // 2, cols) if _by_rows(rows) else (rows, cols // 2)


def _half_blocks(rows, cols, mult):
    hr, hc = _half_shape(rows, cols)
    block, n, part = _blocks2d(hr, hc, mult)
    assert (hr % mult == 0) == _by_rows(rows), (rows, cols, mult)
    full = (lambda h, i: (h * n + i, 0)) if _by_rows(rows) else (lambda h, i: (0, h * n + i))
    return block, n, full, part


def _half(ref, k, half):
    hr, hc = _half_shape(ref.shape[1], ref.shape[2])
    if _by_rows(ref.shape[1]):
        return ref.at[k, pl.ds(pl.multiple_of(half * hr, 16), hr), :]
    return ref.at[k, :, pl.ds(pl.multiple_of(half * hc, LANE), hc)]


def _shard_blocks(w, br, bc):
    if w.shape[0] == 1:
        def write(ref, v):
            ref[...] = v
        return (lambda f: pl.BlockSpec((None, br, bc), lambda *a: (0, *f(*a)))), (lambda ref: ref[...]), write
    assert w.shape[1] == 1 and br == w.shape[0], w.shape

    def write_rows(ref, v):
        ref[:, 0, :] = v
    return (lambda f: pl.BlockSpec((br, 1, bc), lambda *a: (0, 0, f(*a)[1]))), (lambda ref: ref[:, 0, :]), write_rows


def _stage_shard(name, w, chip):
    rs, cs = w.shape[0] * w.shape[1], w.shape[2]
    (br, bc), n, idx = _blocks2d(rs, cs, 16)
    spec, get, _ = _shard_blocks(w, br, bc)

    def body(chip_ref, w_ref, o_ref):
        o_ref[...] = get(w_ref).astype(BF16)

    return pl.pallas_call(
        body, name="stage_" + name,
        grid_spec=pltpu.PrefetchScalarGridSpec(
            num_scalar_prefetch=1, grid=(n,),
            in_specs=[spec(lambda i, chip_ref: idx(i))],
            out_specs=pl.BlockSpec((None, br, bc), lambda i, chip_ref: (chip_ref[0], *idx(i)))),
        out_shape=jax.ShapeDtypeStruct((N_CHIPS, rs, cs), BF16),
        compiler_params=_params(("parallel",)),
    )(_scalar(chip), w)


def _allgather_weights(bufs):
    n = len(bufs)

    def body(*refs):
        outs, send_sems, recv_sems = refs[n:2 * n], refs[2 * n], refs[2 * n + 1]
        x, y, c = _me()
        chip = 2 * x + y
        sib = (x, y, 1 - c)
        chips = _other_chips(x, y)

        def copy(k, part, to):
            return pltpu.make_async_remote_copy(src_ref=part, dst_ref=part, send_sem=send_sems.at[k], recv_sem=recv_sems.at[k],
                                                device_id=to, device_id_type=MESH_ID)

        started = []
        for w, o_ref in enumerate(outs):
            for j, (cx, cy) in enumerate(chips):
                started.append(copy(6 * w + j, _half(o_ref, chip, c), (cx, cy, c)))
                started[-1].start()
        for w, o_ref in enumerate(outs):
            for j, (cx, cy) in enumerate(chips):
                theirs = _half(o_ref, 2 * cx + cy, c)
                copy(6 * w + j, theirs, sib).wait_recv()
                started.append(copy(6 * w + 3 + j, theirs, sib))
                started[-1].start()
        for w, o_ref in enumerate(outs):
            for j, (cx, cy) in enumerate(chips):
                copy(6 * w + 3 + j, _half(o_ref, 2 * cx + cy, 1 - c), sib).wait_recv()
        for cp in started:
            cp.wait_send()

    return pl.pallas_call(
        body, name="allgather_weights", in_specs=[_ANY] * n, out_specs=[_ANY] * n,
        out_shape=[jax.ShapeDtypeStruct(b.shape, b.dtype) for b in bufs],
        input_output_aliases={i: i for i in range(n)},
        scratch_shapes=[pltpu.SemaphoreType.DMA((6 * n,)), pltpu.SemaphoreType.DMA((6 * n,))],
    )(*bufs)


_HBM = pl.BlockSpec(memory_space=pltpu.HBM)
_SEM = pl.BlockSpec(memory_space=pltpu.SEMAPHORE)
_EFFECT = pltpu.SideEffectType.DATAFLOW_SIDE_EFFECTING


def _split_start(name, bufs, n_copies, copies, after):
    n = len(bufs)

    def body(*refs):
        for cp in copies(refs[:n], refs[n + 1], refs[n + 2]):
            cp.start()
        refs[-1][...] = jnp.zeros_like(refs[-1])

    res = pl.pallas_call(
        body, name=name,
        out_shape=(pltpu.SemaphoreType.DMA((n_copies,)), pltpu.SemaphoreType.DMA((n_copies,)),
                   *[pltpu.HBM(b.shape, b.dtype) for b in bufs], jax.ShapeDtypeStruct((8, LANE), F32)),
        in_specs=[_HBM] * n + [_ANY], out_specs=(_SEM, _SEM, *[_HBM] * n, pl.BlockSpec(memory_space=pltpu.VMEM)),
        input_output_aliases={i: 2 + i for i in range(n)},
        compiler_params=pltpu.CompilerParams(has_side_effects=_EFFECT),
    )(*[pltpu.with_memory_space_constraint(b, pltpu.HBM) for b in bufs], after)
    return res[0], res[1], list(res[2:2 + n]), res[-1]


def _split_wait(name, send_sems, recv_sems, bufs, after, copies):
    n = len(bufs)

    def body(*refs):
        for cp in copies(refs[:n], refs[n], refs[n + 1]):
            cp.wait_send()
            cp.wait_recv()

    return list(pl.pallas_call(
        body, name=name, out_shape=[pltpu.HBM(b.shape, b.dtype) for b in bufs],
        in_specs=[_HBM] * n + [_SEM, _SEM, _ANY], out_specs=[_HBM] * n,
        input_output_aliases={i: i for i in range(n)},
        compiler_params=pltpu.CompilerParams(has_side_effects=_EFFECT),
    )(*bufs, send_sems, recv_sems, after))


def _gather_to_chips(bufs, send_sems, recv_sems):
    x, y, c = _me()
    return [pltpu.make_async_remote_copy(src_ref=_half(b, 2 * x + y, c), dst_ref=_half(b, 2 * x + y, c),
                                         send_sem=send_sems.at[3 * w + j], recv_sem=recv_sems.at[3 * w + j],
                                         device_id=(cx, cy, c), device_id_type=MESH_ID)
            for w, b in enumerate(bufs) for j, (cx, cy) in enumerate(_other_chips(x, y))]


def _gather_to_sibling(bufs, send_sems, recv_sems):
    x, y, c = _me()
    return [pltpu.make_async_remote_copy(src_ref=_half(b, 2 * cx + cy, c), dst_ref=_half(b, 2 * cx + cy, c),
                                         send_sem=send_sems.at[3 * w + j], recv_sem=recv_sems.at[3 * w + j],
                                         device_id=(x, y, 1 - c), device_id_type=MESH_ID)
            for w, b in enumerate(bufs) for j, (cx, cy) in enumerate(_other_chips(x, y))]


def _pair_exchange(name, grads):
    n = len(grads)

    def body(*refs):
        ins, outs, send_sems, recv_sems = refs[:n], refs[n:2 * n], refs[2 * n], refs[2 * n + 1]
        x, y, c = _me()
        cps = []
        for w, (g_ref, o_ref) in enumerate(zip(ins, outs)):
            cps.append(pltpu.make_async_remote_copy(src_ref=_half(g_ref, slice(None), 1 - c), dst_ref=o_ref,
                                                    send_sem=send_sems.at[w], recv_sem=recv_sems.at[w],
                                                    device_id=(x, y, 1 - c), device_id_type=MESH_ID))
            cps[-1].start()
        for cp in cps:
            cp.wait()

    return pl.pallas_call(
        body, name="pair_exchange_" + name, in_specs=[_ANY] * n, out_specs=[_ANY] * n,
        out_shape=[jax.ShapeDtypeStruct((g.shape[0], *_half_shape(g.shape[1], g.shape[2])), g.dtype) for g in grads],
        scratch_shapes=[pltpu.SemaphoreType.DMA((n,)), pltpu.SemaphoreType.DMA((n,))],
    )(*grads)


def _pair_copies(grads, lands, send_sems, recv_sems):
    x, y, c = _me()
    return [pltpu.make_async_remote_copy(src_ref=_half(g_ref, slice(None), 1 - c), dst_ref=l_ref, send_sem=send_sems.at[w],
                                         recv_sem=recv_sems.at[w], device_id=(x, y, 1 - c), device_id_type=MESH_ID)
            for w, (g_ref, l_ref) in enumerate(zip(grads, lands))]


def _pair_exchange_start(name, grads):
    n = len(grads)
    lands = [lax.empty((g.shape[0], *_half_shape(g.shape[1], g.shape[2])), g.dtype) for g in grads]
    send_sems, recv_sems, bufs, token = _split_start(
        "pair_exchange_start_" + name, [*grads, *lands], n, lambda refs, ss, rs: _pair_copies(refs[:n], refs[n:], ss, rs),
        jnp.zeros((8, LANE), F32))
    return (send_sems, recv_sems, bufs), token


def _pair_exchange_wait(name, state, after):
    send_sems, recv_sems, bufs = state
    n = len(bufs) // 2
    bufs = _split_wait("pair_exchange_wait_" + name, send_sems, recv_sems, bufs, after,
                       lambda refs, ss, rs: _pair_copies(refs[:n], refs[n:], ss, rs))
    return bufs[:n], bufs[n:]


def _pair_sum(name, g, theirs, c):
    (br, bc), nb, full, part = _half_blocks(g.shape[1], g.shape[2], 16)

    def body(c_ref, a_ref, b_ref, o_ref):
        o_ref[...] = (a_ref[...].astype(F32) + b_ref[...].astype(F32)).astype(o_ref.dtype)

    return pl.pallas_call(
        body, name="pair_sum_" + name,
        grid_spec=pltpu.PrefetchScalarGridSpec(
            num_scalar_prefetch=1, grid=(N_CHIPS, nb),
            in_specs=[pl.BlockSpec((None, br, bc), lambda k, i, c_ref: (k, *full(c_ref[0], i))),
                      pl.BlockSpec((None, br, bc), lambda k, i, c_ref: (k, *part(i)))],
            out_specs=pl.BlockSpec((None, br, bc), lambda k, i, c_ref: (k, *part(i)))),
        out_shape=jax.ShapeDtypeStruct(theirs.shape, BF16),
        compiler_params=_params(("parallel", "parallel")),
    )(_scalar(c), g, theirs)


def _chip_copies(srcs, lands, send_sems, recv_sems):
    x, y, c = _me()
    return [pltpu.make_async_remote_copy(src_ref=s_ref.at[2 * cx + cy], dst_ref=l_ref.at[j], send_sem=send_sems.at[3 * w + j],
                                         recv_sem=recv_sems.at[3 * w + j], device_id=(cx, cy, c), device_id_type=MESH_ID)
            for w, (s_ref, l_ref) in enumerate(zip(srcs, lands)) for j, (cx, cy) in enumerate(_other_chips(x, y))]


def _chip_exchange_start(name, sums):
    n = len(sums)
    lands = [lax.empty((3,) + s.shape[1:], s.dtype) for s in sums]
    send_sems, recv_sems, bufs, token = _split_start(
        "chip_exchange_start_" + name, [*sums, *lands], 3 * n, lambda refs, ss, rs: _chip_copies(refs[:n], refs[n:], ss, rs),
        jnp.zeros((8, LANE), F32))
    return send_sems, recv_sems, bufs[:n], bufs[n:], token


def _chip_exchange_wait(name, send_sems, recv_sems, sums, lands, after):
    n = len(sums)
    bufs = _split_wait("chip_exchange_wait_" + name, send_sems, recv_sems, [*sums, *lands], after,
                       lambda refs, ss, rs: _chip_copies(refs[:n], refs[n:], ss, rs))
    return bufs[:n], bufs[n:]


def _chip_sum(name, sums, theirs, chip):
    _, h, cs = sums.shape
    (br, bc), nb, idx = _blocks2d(h, cs, 16)

    def body(chip_ref, s_ref, t_ref, o_ref):
        acc = s_ref[...].astype(F32)
        for k in range(3):
            acc = acc + t_ref[k].astype(F32)
        o_ref[...] = acc

    return pl.pallas_call(
        body, name="chip_sum_" + name,
        grid_spec=pltpu.PrefetchScalarGridSpec(
            num_scalar_prefetch=1, grid=(nb,),
            in_specs=[pl.BlockSpec((None, br, bc), lambda i, chip_ref: (chip_ref[0], *idx(i))),
                      pl.BlockSpec((3, br, bc), lambda i, chip_ref: (0, *idx(i)))],
            out_specs=pl.BlockSpec((br, bc), lambda i, chip_ref: idx(i))),
        out_shape=jax.ShapeDtypeStruct((h, cs), F32),
        compiler_params=_params(("parallel",)),
    )(_scalar(chip), sums, theirs)


def _sibling_copies(halves, lands, send_sems, recv_sems):
    x, y, c = _me()
    return [pltpu.make_async_remote_copy(src_ref=h_ref, dst_ref=l_ref, send_sem=send_sems.at[w], recv_sem=recv_sems.at[w],
                                         device_id=(x, y, 1 - c), device_id_type=MESH_ID)
            for w, (h_ref, l_ref) in enumerate(zip(halves, lands))]


def _sibling_exchange_start(name, halves):
    n = len(halves)
    lands = [lax.empty(h.shape, h.dtype) for h in halves]
    send_sems, recv_sems, bufs, token = _split_start(
        "sibling_exchange_start_" + name, [*halves, *lands], n, lambda refs, ss, rs: _sibling_copies(refs[:n], refs[n:], ss, rs),
        jnp.zeros((8, LANE), F32))
    return (send_sems, recv_sems, bufs), token


def _sibling_exchange_wait(name, state, after):
    send_sems, recv_sems, bufs = state
    n = len(bufs) // 2
    bufs = _split_wait("sibling_exchange_wait_" + name, send_sems, recv_sems, bufs, after,
                       lambda refs, ss, rs: _sibling_copies(refs[:n], refs[n:], ss, rs))
    return bufs[:n], bufs[n:]


def _sibling_exchange(halves):
    n = len(halves)

    def body(*refs):
        ins, outs, send_sems, recv_sems = refs[:n], refs[n:2 * n], refs[2 * n], refs[2 * n + 1]
        x, y, c = _me()
        cps = []
        for w, (h_ref, o_ref) in enumerate(zip(ins, outs)):
            cps.append(pltpu.make_async_remote_copy(src_ref=h_ref, dst_ref=o_ref, send_sem=send_sems.at[w], recv_sem=recv_sems.at[w],
                                                    device_id=(x, y, 1 - c), device_id_type=MESH_ID))
            cps[-1].start()
        for cp in cps:
            cp.wait()

    return pl.pallas_call(
        body, name="grad_sibling_exchange", in_specs=[_ANY] * n, out_specs=[_ANY] * n,
        out_shape=[jax.ShapeDtypeStruct(h.shape, h.dtype) for h in halves],
        scratch_shapes=[pltpu.SemaphoreType.DMA((n,)), pltpu.SemaphoreType.DMA((n,))],
    )(*halves)


def _allreduce_small(name, vec, after):
    def body(v_ref, after_ref, o_ref, buf_ref, send_sems, recv_sems):
        x, y, c = _me()
        me = 4 * x + 2 * y + c
        cps = []
        for p in range(1, 8):
            px, py, pc = x ^ (p >> 2), y ^ ((p >> 1) & 1), c ^ (p & 1)
            cps.append(pltpu.make_async_remote_copy(src_ref=v_ref, dst_ref=buf_ref.at[me], send_sem=send_sems.at[p - 1],
                                                    recv_sem=recv_sems.at[p - 1], device_id=(px, py, pc), device_id_type=MESH_ID))
            cps[-1].start()
        buf_ref[me] = v_ref[...]
        for p in range(1, 8):
            theirs = buf_ref.at[me ^ p]
            pltpu.make_async_remote_copy(src_ref=theirs, dst_ref=theirs, send_sem=send_sems.at[p - 1], recv_sem=recv_sems.at[p - 1],
                                         device_id=(x, y, c), device_id_type=MESH_ID).wait_recv()
        for cp in cps:
            cp.wait_send()
        acc = buf_ref[0]
        for k in range(1, 8):
            acc = acc + buf_ref[k]
        o_ref[...] = acc

    vm = pl.BlockSpec(memory_space=pltpu.VMEM)
    return pl.pallas_call(
        body, name=name, in_specs=[vm, _ANY], out_specs=vm, out_shape=jax.ShapeDtypeStruct(vec.shape, F32),
        scratch_shapes=[pltpu.VMEM((8,) + vec.shape, F32), pltpu.SemaphoreType.DMA((7,)), pltpu.SemaphoreType.DMA((7,))],
    )(vec, after)


def _adam_math(w, g, m, v):
    m = ADAM_B1 * m + (1.0 - ADAM_B1) * g
    v = ADAM_B2 * v + (1.0 - ADAM_B2) * (g * g)
    m_hat = m / (1.0 - ADAM_B1 ** ADAM_STEP)
    v_hat = v / (1.0 - ADAM_B2 ** ADAM_STEP)
    return -ADAM_LR * (m_hat / (jnp.sqrt(v_hat) + ADAM_EPS) + ADAM_WD * w), m, v


def _adamw(name, w, g, m, v):
    R, C = w.shape
    tr = _row_block(R, C, 8)

    def body(w_ref, g_ref, m_ref, v_ref, d_ref, nm_ref, nv_ref):
        d_ref[...], nm_ref[...], nv_ref[...] = _adam_math(w_ref[...], g_ref[...], m_ref[...], v_ref[...])

    blk = pl.BlockSpec((tr, C), lambda i: (i, 0))
    return pl.pallas_call(
        body, name=name, grid=(R // tr,), in_specs=[blk] * 4, out_specs=[blk] * 3,
        out_shape=[jax.ShapeDtypeStruct((R, C), F32)] * 3, compiler_params=_params(("parallel",)),
    )(w, g, m, v)


def _adamw_halves(name, w, mine, theirs, m, v, c):
    rs, cs = w.shape[0] * w.shape[1], w.shape[2]
    (br, bc), nb, whole, half = _half_blocks(rs, cs, 8)
    spec, get, put = _shard_blocks(w, br, bc)

    def body(c_ref, w_ref, a_ref, b_ref, m_ref, v_ref, g_ref, d_ref, nm_ref, nv_ref):
        g = jnp.where(pl.program_id(0) == c_ref[0], a_ref[...], b_ref[...])
        put(g_ref, g)
        for ref, val in zip((d_ref, nm_ref, nv_ref), _adam_math(get(w_ref), g, get(m_ref), get(v_ref))):
            put(ref, val)

    full = spec(lambda s, i, c_ref: whole(s, i))
    part = pl.BlockSpec((br, bc), lambda s, i, c_ref: half(i))
    return pl.pallas_call(
        body, name=name,
        grid_spec=pltpu.PrefetchScalarGridSpec(num_scalar_prefetch=1, grid=(2, nb), in_specs=[full, part, part, full, full],
                                               out_specs=[full] * 4),
        out_shape=[jax.ShapeDtypeStruct(w.shape, F32)] * 4, compiler_params=_params(("parallel", "parallel")),
    )(_scalar(c), w, mine, theirs, m, v)


def _pack_small(arrs):
    flat = jnp.concatenate([a.reshape(-1) for a in arrs])
    n = -(-flat.shape[0] // (8 * LANE)) * 8 * LANE
    return jnp.pad(flat, (0, n - flat.shape[0])).reshape(8, n // 8)


def _unpack_small(vec, shapes):
    flat, out, off = vec.reshape(-1), [], 0
    for s in shapes:
        out.append(flat[off:off + s[0] * s[1]].reshape(s))
        off += s[0] * s[1]
    return out


class _LateWeights:
    def __init__(self, cfg, tag, names, staged, after):
        self.cfg, self.tag, self.names, self.k = cfg, tag, names, 3 * len(names)
        self.send, self.recv, self.bufs, self.token = _split_start(f"gather_{tag}_chips_start", staged, self.k, _gather_to_chips,
                                                                    after)

    def pass_on(self, after):
        bufs = _split_wait(f"gather_{self.tag}_chips_wait", self.send, self.recv, self.bufs, after, _gather_to_chips)
        self.send, self.recv, self.bufs, token = _split_start(f"gather_{self.tag}_sibling_start", bufs, self.k, _gather_to_sibling,
                                                               self.token)
        return token

    def arrived(self, after):
        bufs = _split_wait(f"gather_{self.tag}_sibling_wait", self.send, self.recv, self.bufs, after, _gather_to_sibling)
        return {n: _gathered_to_kernel(self.cfg, n, b) for n, b in zip(self.names, bufs)}


def _step(cfg, a):
    chip = 2 * lax.axis_index("x") + lax.axis_index("y")
    core = lax.axis_index("c")
    big = BIG

    ffn = ("w_gate", "w_up", "w_down")
    first = ("w_in", "w_uq", "w_ukv")
    staged = {n: _stage_shard(n, a[n], chip) for n in big}
    gathered = _allgather_weights([staged[n] for n in first])
    W = {n: _gathered_to_kernel(cfg, n, wg) for n, wg in zip(first, gathered)}

    sp = {n: a[n] for n in SMALL}
    sharded = _pack_small([a[n] for n in SMALL_SHARDED])
    slot = jnp.where(lax.broadcasted_iota(I32, (N_CHIPS,) + sharded.shape, 0) == chip, 0.5 * sharded[None], 0.0)
    allp = _allreduce_small("allgather_small", slot.reshape(N_CHIPS * 8, -1), gathered[0]).reshape((N_CHIPS,) + sharded.shape)
    per_chip = [_unpack_small(allp[ch], [a[n].shape for n in SMALL_SHARDED]) for ch in range(N_CHIPS)]
    for k, n in enumerate(SMALL_SHARDED):
        sp[n] = jnp.concatenate([per_chip[ch][k] for ch in range(N_CHIPS)], axis=1)

    out_weight = _LateWeights(cfg, "out", ("w_out",), [staged["w_out"]], allp)
    ffn_weights = _LateWeights(cfg, "ffn", ffn, [staged[n] for n in ffn], out_weight.token)
    sp["mix_pre_g"] = sp["mix_pre_g"] + (out_weight.token[0, 0] + ffn_weights.token[0, 0])

    state = {}

    def ffn_grads_ready(grads):
        state["ffn_pairs"], token = _pair_exchange_start("ffn", [_grad_to_chips(cfg, n, grads[n]) for n in ffn_grads])
        return token

    def pair_sums(names, grads, theirs):
        return [_pair_sum(n, g, t, core) for n, g, t in zip(names, grads, theirs)]

    def early_grads_ready(grads):
        g_out = [_grad_to_chips(cfg, "w_out", grads["w_out"])]
        g_ffn, t_ffn = _pair_exchange_wait("ffn", state["ffn_pairs"], g_out[0])
        sums = pair_sums(ffn_grads, g_ffn, t_ffn) + pair_sums(["w_out"], g_out, _pair_exchange("out", g_out))
        state["early"] = _chip_exchange_start("early", sums)
        return state["early"][-1]

    def reduced_halves(tag, names, after):
        send_sems, recv_sems, s_bufs, l_bufs, _ = state[tag]
        s_bufs, l_bufs = _chip_exchange_wait(tag, send_sems, recv_sems, s_bufs, l_bufs, after)
        return [_chip_sum(n, s, t, chip) for n, s, t in zip(names, s_bufs, l_bufs)]

    def in_grad_ready(grads):
        grads = [_grad_to_chips(cfg, n, grads[n]) for n in first]
        state["rest"] = _chip_exchange_start("rest", pair_sums(first, grads, _pair_exchange("rest", grads)))
        return state["rest"][-1]

    ffn_grads = ("w_down", "w_gate", "w_up")
    early = ffn_grads + ("w_out",)
    loss, grad_x, gW, gs = _local_grads(cfg, a["x"], a["loss_target"], W, sp, out_weight, ffn_weights,
                                        ffn_grads_ready, early_grads_ready, in_grad_ready)
    mine = dict(zip(early, reduced_halves("early", early, grad_x)))
    mine.update(zip(first, reduced_halves("rest", first, mine[early[0]])))
    theirs = dict(zip(big, _sibling_exchange([mine[n] for n in big])))

    shapes = [gs[n].shape for n in SMALL] + [(1, LANE)]
    red = _unpack_small(_allreduce_small("allreduce_small", _pack_small([gs[n] for n in SMALL] + [loss]), theirs[big[0]]), shapes)
    g_small = dict(zip(SMALL, red[:-1]))
    for n in SMALL_SHARDED:
        cs = a[n].shape[1]
        g_small[n] = lax.dynamic_slice_in_dim(g_small[n], chip * cs, cs, axis=1)

    out = {"loss": red[-1][0, 0], "grad_x": grad_x}
    for n in big:
        out["grad_" + n], out["delta_" + n], out["new_m_" + n], out["new_v_" + n] = _adamw_halves(
            "adamw_" + n, a[n], mine[n], theirs[n], a["m_" + n], a["v_" + n], core)
    sshapes = [a[n].shape for n in SMALL]
    d, nm, nv = _adamw("adamw_small", _pack_small([a[n] for n in SMALL]), _pack_small([g_small[n] for n in SMALL]),
                       _pack_small([a["m_" + n] for n in SMALL]), _pack_small([a["v_" + n] for n in SMALL]))
    for n, dd, mm, vv in zip(SMALL, _unpack_small(d, sshapes), _unpack_small(nm, sshapes), _unpack_small(nv, sshapes)):
        out["grad_" + n], out["delta_" + n], out["new_m_" + n], out["new_v_" + n] = g_small[n], dd, mm, vv
    return out


def kernel(x, mix_pre_g, w_in, q_norm_g, w_uq, kv_norm_g, w_ukv, ssm_conv_w, ssm_conv_b, dt_bias, a_log, d_skip, ssm_norm_g, w_out, mix_post_g, ffn_pre_g, w_gate, w_up, ffn_conv_w, ffn_conv_b, w_down, ffn_post_g, loss_target, m_mix_pre_g, m_w_in, m_q_norm_g, m_w_uq, m_kv_norm_g, m_w_ukv, m_ssm_conv_w, m_ssm_conv_b, m_dt_bias, m_a_log, m_d_skip, m_ssm_norm_g, m_w_out, m_mix_post_g, m_ffn_pre_g, m_w_gate, m_w_up, m_ffn_conv_w, m_ffn_conv_b, m_w_down, m_ffn_post_g, v_mix_pre_g, v_w_in, v_q_norm_g, v_w_uq, v_kv_norm_g, v_w_ukv, v_ssm_conv_w, v_ssm_conv_b, v_dt_bias, v_a_log, v_d_skip, v_ssm_norm_g, v_w_out, v_mix_post_g, v_ffn_pre_g, v_w_gate, v_w_up, v_ffn_conv_w, v_ffn_conv_b, v_w_down, v_ffn_post_g):
    args = dict(locals())
    def given(k, v):
        if k in ("w_in", "m_w_in", "v_w_in"):
            return jnp.transpose(v, (2, 0, 1))
        return v if k.removeprefix("m_").removeprefix("v_") in BIG or v.ndim < 3 else v[0]

    out = _step(_FULL, {k: given(k, v) for k, v in args.items()})
    res = [out["loss"], out["grad_x"][None]]
    for pre in ("grad_", "delta_", "new_m_", "new_v_"):
        for n in WEIGHTS:
            o = out[pre + n]
            res.append(jnp.transpose(o, (1, 2, 0)) if n == "w_in" else o if n in BIG or args[n].ndim < 3 else o[None])
    return tuple(res)
```

```python
import functools
import math

import jax
import jax.numpy as jnp
from jax import lax
from jax.experimental import pallas as pl
from jax.experimental.pallas import tpu as pltpu

F32, BF16, I32 = jnp.float32, jnp.bfloat16, jnp.int32
NN = (((1,), (0,)), ((), ()))
NT = (((1,), (1,)), ((), ()))
TN = (((0,), (0,)), ((), ()))
HI = lax.Precision.HIGHEST
MESH_ID = pl.DeviceIdType.MESH

EPS = 1e-6
CHUNK = 64
NOPE, ROPE, VH = 128, 64, 128
ROPE_THETA = 10000.0
HP, NST = 64, 128
SSM_K, FFN_K = 4, 3
LANE = 128
N_CHIPS = 4
VMEM_LIMIT = 52 * 1024 * 1024
MM_TILE, MM_TILE_K = 1408, 2816

ADAM_LR, ADAM_B1, ADAM_B2, ADAM_EPS, ADAM_WD, ADAM_STEP = 0.001, 0.9, 0.999, 1e-08, 0.01, 10


class _Cfg:
    def __init__(self, S, D, QL, KVL, H, HS, G, DFF, T):
        self.S, self.D, self.QL, self.KVL, self.H, self.HS, self.G, self.DFF, self.T = S, D, QL, KVL, H, HS, G, DFF, T
        self.INNER = HS * HP
        self.CONVCH = self.INNER + 2 * G * NST
        self.QW = H * (NOPE + ROPE)
        self.KVW = H * (NOPE + VH)
        self.MLAW = H * VH
        self.MIXW = self.MLAW + self.INNER
        self.IN_COLS = QL + KVL + ROPE + self.INNER + self.CONVCH + HS
        self.o_kr = QL + KVL
        self.o_z = self.o_kr + LANE
        self.o_xbc = self.o_z + self.INNER
        self.o_dt = self.o_xbc + self.CONVCH
        self.EXT = self.o_dt + LANE
        self.NPAIR = HS // 2
        self.REP = HS // G


_FULL = _Cfg(S=2048, D=2048, QL=768, KVL=512, H=8, HS=16, G=2, DFF=5632, T=256)
BIG = ("w_in", "w_uq", "w_ukv", "w_out", "w_gate", "w_up", "w_down")

SMALL = ("mix_pre_g", "q_norm_g", "kv_norm_g", "ssm_conv_w", "ssm_conv_b", "dt_bias", "a_log", "d_skip", "ssm_norm_g",
         "mix_post_g", "ffn_pre_g", "ffn_conv_w", "ffn_conv_b", "ffn_post_g")
SMALL_SHARDED = ("ssm_conv_w", "ffn_conv_w")
WEIGHTS = ("mix_pre_g", "w_in", "q_norm_g", "w_uq", "kv_norm_g", "w_ukv", "ssm_conv_w", "ssm_conv_b", "dt_bias", "a_log",
           "d_skip", "ssm_norm_g", "w_out", "mix_post_g", "ffn_pre_g", "w_gate", "w_up", "ffn_conv_w", "ffn_conv_b",
           "w_down", "ffn_post_g")


def _pick(n, target, mult):
    best = None
    for d in range(mult, min(n, target) + 1, mult):
        if n % d == 0:
            best = d
    return best if best is not None else n


def _params(sem=None):
    kw = dict(vmem_limit_bytes=VMEM_LIMIT)
    if sem is not None:
        kw["dimension_semantics"] = sem
    return pltpu.CompilerParams(**kw)


def _dot(a, b, dims=NN, precision=None):
    return lax.dot_general(a, b, dims, preferred_element_type=F32, precision=precision)


def _sigmoid(x):
    return 1.0 / (1.0 + jnp.exp(-x))


def _rs(x):
    return lax.rsqrt(jnp.mean(x * x, axis=-1, keepdims=True) + EPS)


def _rms_back(xh, r, dn):
    return r * (dn - xh * jnp.mean(dn * xh, axis=-1, keepdims=True))


def _colsum(v):
    return jnp.sum(v, axis=0, keepdims=True)


def _matmul(name, a, b, mode, out_dtype, a2=None, b2=None, chips=False):
    cs = None
    if mode == "nn":
        (M, K), N = a.shape, b.shape[-1]
        if chips:
            cs, N = N, N_CHIPS * N
    elif mode == "nt":
        (M, K), N = a.shape, b.shape[-2]
        if chips:
            cs = b.shape[-1]
    else:
        (K, M), N = a.shape, b.shape[1]
        if chips:
            cs = N // N_CHIPS
    tm = _pick(M, MM_TILE, LANE)
    tn = _pick(cs if chips and mode != "nt" else N, MM_TILE, LANE)
    tk = _pick(cs, MM_TILE, LANE) if chips and mode == "nt" else _pick(K, MM_TILE_K, LANE)
    nk = K // tk
    dims = {"nn": NN, "nt": NT, "tn": TN}[mode]
    a_spec = pl.BlockSpec((tk, tm), lambda i, j, k: (k, i)) if mode == "tn" else pl.BlockSpec((tm, tk), lambda i, j, k: (i, k))
    b_spec = pl.BlockSpec((tn, tk), lambda i, j, k: (j, k)) if mode == "nt" else pl.BlockSpec((tk, tn), lambda i, j, k: (k, j))
    o_spec = pl.BlockSpec((tm, tn), lambda i, j, k: (i, j))
    o_shape = (M, N)
    if chips and mode == "nn":
        per = cs // tn
        b_spec = pl.BlockSpec((None, tk, tn), lambda i, j, k: (j // per, k, j % per))
    elif chips and mode == "nt":
        per = cs // tk
        b_spec = pl.BlockSpec((None, tn, tk), lambda i, j, k: (k // per, j, k % per))
    elif chips:
        per = cs // tn
        o_spec = pl.BlockSpec((None, tm, tn), lambda i, j, k: (j // per, i, j % per))
        o_shape = (N_CHIPS, M, cs)
    two = a2 is not None

    def product(refs):
        part = _dot(refs[0][...].astype(BF16), refs[1][...].astype(BF16), dims)
        if two:
            part += _dot(refs[2][...].astype(BF16), refs[3][...].astype(BF16), dims)
        return part

    def body_whole_k(*refs):
        refs[-1][...] = product(refs).astype(refs[-1].dtype)

    def body(*refs):
        o_ref, acc_ref = refs[-2], refs[-1]
        k = pl.program_id(2)

        @pl.when(k == 0)
        def _():
            acc_ref[...] = product(refs)

        @pl.when(k > 0)
        def _():
            acc_ref[...] += product(refs)

        @pl.when(k == nk - 1)
        def _():
            o_ref[...] = acc_ref[...].astype(o_ref.dtype)

    ins = (a, b, a2, b2) if two else (a, b)
    return pl.pallas_call(
        body_whole_k if nk == 1 else body, name=name, grid=(M // tm, N // tn, nk),
        in_specs=[a_spec, b_spec] * (2 if two else 1),
        out_specs=o_spec,
        out_shape=jax.ShapeDtypeStruct(o_shape, out_dtype),
        scratch_shapes=[] if nk == 1 else [pltpu.VMEM((tm, tn), F32)],
        compiler_params=_params(("parallel", "parallel", "arbitrary")),
    )(*ins)


def _rowwise(name, fn, rows, mats, outs, reds, ts):
    S = rows[0].shape[0]
    nr, nm, no = len(rows), len(mats), len(outs)

    def body(*refs):
        res = fn(*[r[...] for r in refs[:nr + nm]])
        res = res if isinstance(res, (tuple, list)) else (res,)
        for r, v in zip(refs[nr + nm:nr + nm + no], res[:no]):
            r[...] = v.astype(r.dtype)
        first = pl.program_id(0) == 0
        for r, v in zip(refs[nr + nm + no:], res[no:]):
            @pl.when(first)
            def _():
                r[...] = jnp.broadcast_to(v, r.shape)

            @pl.when(jnp.logical_not(first))
            def _():
                r[...] += jnp.broadcast_to(v, r.shape)

    in_specs = [pl.BlockSpec((ts, a.shape[1]), lambda i: (i, 0)) for a in rows]
    in_specs += [pl.BlockSpec(m.shape, lambda i, nd=m.ndim: (0,) * nd) for m in mats]
    out_specs = [pl.BlockSpec((ts, w), lambda i: (i, 0)) for w, _ in outs]
    out_specs += [pl.BlockSpec(s, lambda i: (0, 0)) for s in reds]
    out_shape = [jax.ShapeDtypeStruct((S, w), dt) for w, dt in outs] + [jax.ShapeDtypeStruct(s, F32) for s in reds]
    return pl.pallas_call(
        body, name=name, grid=(S // ts,), in_specs=in_specs, out_specs=out_specs, out_shape=out_shape,
        compiler_params=_params(("arbitrary",) if reds else ("parallel",)),
    )(*rows, *mats)


def _shift_down(v, s):
    if s == 0:
        return v
    rows = lax.broadcasted_iota(I32, v.shape, 0)
    return jnp.where(rows >= s, pltpu.roll(v, s, 0), 0.0)


def _shift_up(v, s):
    if s == 0:
        return v
    n = v.shape[0]
    rows = lax.broadcasted_iota(I32, v.shape, 0)
    return jnp.where(rows < n - s, pltpu.roll(v, n - s, 0), 0.0)


def _conv(x, w, b):
    K = w.shape[0]
    y = jnp.broadcast_to(b, x.shape)
    for k in range(K):
        y = y + w[k:k + 1, :] * _shift_down(x, K - 1 - k)
    return y


def _conv_back(x, w, dc):
    K = w.shape[0]
    dx = jnp.zeros_like(x)
    dw = []
    for k in range(K):
        dx = dx + w[k:k + 1, :] * _shift_up(dc, K - 1 - k)
        dw.append(_colsum(dc * _shift_down(x, K - 1 - k)))
    return dx, jnp.concatenate(dw, axis=0), _colsum(dc)


def _colwise(name, fn, cols, vecs, outs, pouts, tc):
    S, C = cols[0].shape
    nc_, nv, no = len(cols), len(vecs), len(outs)

    def body(*refs):
        res = fn(*[r[...] for r in refs[:nc_ + nv]])
        res = res if isinstance(res, (tuple, list)) else (res,)
        for r, v in zip(refs[nc_ + nv:], res):
            r[...] = v.astype(r.dtype)

    in_specs = [pl.BlockSpec((S, tc), lambda j: (0, j)) for _ in cols]
    in_specs += [pl.BlockSpec((v.shape[0], tc), lambda j: (0, j)) for v in vecs]
    out_specs = [pl.BlockSpec((S, tc), lambda j: (0, j)) for _ in outs] + [pl.BlockSpec((k, tc), lambda j: (0, j)) for k in pouts]
    out_shape = [jax.ShapeDtypeStruct((S, C), dt) for dt in outs] + [jax.ShapeDtypeStruct((k, C), F32) for k in pouts]
    return pl.pallas_call(
        body, name=name, grid=(C // tc,), in_specs=in_specs, out_specs=out_specs, out_shape=out_shape,
        compiler_params=_params(("parallel",)),
    )(*cols, *vecs)


_G0, _G1 = math.sqrt(2.0 / math.pi), 0.044715


def _gelu(g):
    th = jnp.tanh(_G0 * (g + _G1 * g * g * g))
    return 0.5 * g * (1.0 + th), th


def _ffn_act(gate_pre, up, w, b):
    act, _ = _gelu(_conv(gate_pre, w, b))
    return act * up


def _ffn_act_back(dact, gate_pre, up, w, b):
    g = _conv(gate_pre, w, b)
    ge, th = _gelu(g)
    dge = 0.5 * (1.0 + th) + 0.5 * g * (1.0 - th * th) * _G0 * (1.0 + 3.0 * _G1 * g * g)
    dup = dact * ge
    dgate_pre, dw, db = _conv_back(gate_pre, w, dact * up * dge)
    return dgate_pre, dup, dw, db


def _ssm_act(xbc, w, b):
    c = _conv(xbc, w, b)
    return c * _sigmoid(c)


def _ssm_act_back(dxc, xbc, w, b):
    c = _conv(xbc, w, b)
    sg = _sigmoid(c)
    return _conv_back(xbc, w, dxc * sg * (1.0 + c * (1.0 - sg)))


def _rope_tables(S):
    inv = 1.0 / (ROPE_THETA ** (jnp.arange(0, ROPE, 2, dtype=F32) / ROPE))
    ang = jnp.arange(S, dtype=F32)[:, None] * inv[None, :]
    cos, sin = jnp.cos(ang), jnp.sin(ang)
    return jnp.tile(cos, (1, 4)), jnp.tile(jnp.concatenate([-sin, sin], axis=1), (1, 2))


def _swap_halves(x):
    lane = lax.broadcasted_iota(I32, x.shape, 1)
    w = x.shape[1]
    return jnp.where((lane % ROPE) < ROPE // 2, pltpu.roll(x, w - ROPE // 2, 1), pltpu.roll(x, ROPE // 2, 1))


def _rot(x, cos2, sin2):
    return x * cos2 + _swap_halves(x) * sin2


def _rot_back(dy, cos2, sin2):
    return dy * cos2 + _swap_halves(dy * sin2)


def _mla_pack(cfg, q, kv, kr, cos2, sin2):
    S, H = cfg.S, cfg.H
    ts = _pick(S, 512, 8)

    def body(qn_ref, qr_ref, kn_ref, v_ref, kr_ref, c_ref, s_ref, Q_ref, K_ref, V_ref):
        h = pl.program_id(0)
        c2, s2 = c_ref[...], s_ref[...]
        Q_ref[0, :, 0:LANE] = qn_ref[...].astype(BF16)
        Q_ref[0, :, LANE:] = _rot(qr_ref[...], c2, s2).astype(BF16)
        K_ref[0, :, 0:LANE] = kn_ref[...].astype(BF16)
        krr = _rot(kr_ref[...], c2, s2)
        K_ref[0, :, LANE:] = jnp.where(h % 2 == 1, pltpu.roll(krr, ROPE, 1), krr).astype(BF16)
        V_ref[0] = v_ref[...].astype(BF16)

    blk = lambda f: pl.BlockSpec((ts, LANE), f)
    return pl.pallas_call(
        body, name="mla_pack", grid=(H, S // ts),
        in_specs=[blk(lambda h, i: (i, h)), blk(lambda h, i: (i, H + h // 2)), blk(lambda h, i: (i, h)),
                  blk(lambda h, i: (i, H + h)), blk(lambda h, i: (i, 0)), blk(lambda h, i: (i, 0)), blk(lambda h, i: (i, 0))],
        out_specs=[pl.BlockSpec((1, ts, 2 * LANE), lambda h, i: (h, i, 0)), pl.BlockSpec((1, ts, 2 * LANE), lambda h, i: (h, i, 0)),
                   pl.BlockSpec((1, ts, LANE), lambda h, i: (h, i, 0))],
        out_shape=[jax.ShapeDtypeStruct((H, S, 2 * LANE), BF16), jax.ShapeDtypeStruct((H, S, 2 * LANE), BF16),
                   jax.ShapeDtypeStruct((H, S, LANE), BF16)],
        compiler_params=_params(("parallel", "parallel")),
    )(q, q, kv, kv, kr, cos2, sin2)


def _mla_unpack(cfg, dQ, dK, dV, cos2, sin2):
    S, H = cfg.S, cfg.H
    ts = _pick(S, 256, 8)

    def body(dQ_ref, dK_ref, dV_ref, c_ref, s_ref, dq_ref, dkv_ref, dkr_ref):
        c2, s2 = c_ref[...], s_ref[...]
        lo = lax.broadcasted_iota(I32, (ts, LANE), 1) < ROPE
        tk = jnp.zeros((ts, LANE), F32)
        for h in range(H):
            dq_ref[:, h * LANE:(h + 1) * LANE] = dQ_ref[h, :, 0:LANE].astype(BF16)
            dkv_ref[:, h * LANE:(h + 1) * LANE] = dK_ref[h, :, 0:LANE].astype(BF16)
            dkv_ref[:, (H + h) * LANE:(H + h + 1) * LANE] = dV_ref[h].astype(BF16)
            own = lo if h % 2 == 0 else jnp.logical_not(lo)
            tk = tk + jnp.where(own, dK_ref[h, :, LANE:], 0.0)
        for j in range(H // 2):
            dr = dQ_ref[2 * j, :, LANE:] + dQ_ref[2 * j + 1, :, LANE:]
            dq_ref[:, (H + j) * LANE:(H + j + 1) * LANE] = _rot_back(dr, c2, s2).astype(BF16)
        dkr_rot = jnp.where(lo, tk + pltpu.roll(tk, ROPE, 1), 0.0)
        dkr_ref[...] = _rot_back(dkr_rot, c2, s2).astype(BF16)

    tab = pl.BlockSpec((ts, LANE), lambda i: (i, 0))
    return pl.pallas_call(
        body, name="mla_unpack", grid=(S // ts,),
        in_specs=[pl.BlockSpec((H, ts, 2 * LANE), lambda i: (0, i, 0)), pl.BlockSpec((H, ts, 2 * LANE), lambda i: (0, i, 0)),
                  pl.BlockSpec((H, ts, LANE), lambda i: (0, i, 0)), tab, tab],
        out_specs=[pl.BlockSpec((ts, cfg.QW), lambda i: (i, 0)), pl.BlockSpec((ts, cfg.KVW), lambda i: (i, 0)), tab],
        out_shape=[jax.ShapeDtypeStruct((S, cfg.QW), BF16), jax.ShapeDtypeStruct((S, cfg.KVW), BF16),
                   jax.ShapeDtypeStruct((S, LANE), BF16)],
        compiler_params=_params(("parallel",)),
    )(dQ, dK, dV, cos2, sin2)


_ATT_T = 256
_ATT_HB = 2
_ATT_SCALE = (NOPE + ROPE) ** -0.5


def _diag_mask(transposed=False):
    r = lax.broadcasted_iota(I32, (_ATT_T, _ATT_T), 0) // CHUNK
    c = lax.broadcasted_iota(I32, (_ATT_T, _ATT_T), 1) // CHUNK
    return r <= c if transposed else c <= r


def _row_form(col):
    return jnp.broadcast_to(col, (col.shape[0], LANE)).T[0:8, :]


def _attn_fwd(cfg, Q, K, V):
    S, H, T, HB = cfg.S, cfg.H, _ATT_T, _ATT_HB

    def body(q_ref, k_ref, v_ref, o_ref, lse_ref, lse_t_ref):
        qi = pl.program_id(1)

        def head_step(b, kb, carry, mask):
            m, l, acc = carry
            ks = pl.multiple_of(kb * T, T)
            s = _dot(q_ref[b], k_ref[b, pl.ds(ks, T), :], NT) * _ATT_SCALE
            if mask is not None:
                s = jnp.where(mask, s, -1e30)
            m_new = jnp.maximum(m, jnp.max(s, axis=1, keepdims=True))
            p = jnp.exp(s - m_new)
            alpha = jnp.exp(m - m_new)
            l = alpha * l + jnp.sum(p, axis=1, keepdims=True)
            acc = alpha * acc + _dot(p.astype(BF16), v_ref[b, pl.ds(ks, T), :])
            return m_new, l, acc

        def step(kb, carry, mask=None):
            return tuple(head_step(b, kb, carry[b], mask) for b in range(HB))

        init = (jnp.full((T, 1), -1e30, F32), jnp.zeros((T, 1), F32), jnp.zeros((T, VH), F32))
        done = step(qi, lax.fori_loop(0, qi, step, (init,) * HB), _diag_mask())
        for b, (m, l, acc) in enumerate(done):
            o_ref[:, b * LANE:(b + 1) * LANE] = acc / l
            lse = m + jnp.log(l)
            lse_ref[:, b * LANE:(b + 1) * LANE] = jnp.broadcast_to(lse, (T, LANE))
            lse_t_ref[b] = _row_form(lse)

    return pl.pallas_call(
        body, name="attn_fwd", grid=(H // HB, S // T),
        in_specs=[pl.BlockSpec((HB, T, 2 * LANE), lambda h, i: (h, i, 0)), pl.BlockSpec((HB, S, 2 * LANE), lambda h, i: (h, 0, 0)),
                  pl.BlockSpec((HB, S, LANE), lambda h, i: (h, 0, 0))],
        out_specs=[pl.BlockSpec((T, HB * LANE), lambda h, i: (i, h)), pl.BlockSpec((T, HB * LANE), lambda h, i: (i, h)),
                   pl.BlockSpec((HB, 8, T), lambda h, i: (h, 0, i))],
        out_shape=[jax.ShapeDtypeStruct((S, H * LANE), F32), jax.ShapeDtypeStruct((S, H * LANE), F32),
                   jax.ShapeDtypeStruct((H, 8, S), F32)],
        compiler_params=_params(("parallel", "parallel")),
    )(Q, K, V)


def _attn_dq(cfg, Q, K, V, do, o, lse, after):
    S, H, T, HB = cfg.S, cfg.H, _ATT_T, _ATT_HB

    def body(q_ref, k_ref, v_ref, do_ref, o_ref, lse_ref, after_ref, dq_ref, dl_t_ref):
        qi = pl.program_id(1)
        do = [do_ref[:, b * LANE:(b + 1) * LANE] for b in range(HB)]
        delta = [jnp.sum(do[b] * o_ref[:, b * LANE:(b + 1) * LANE], axis=1, keepdims=True) for b in range(HB)]
        dob = [d.astype(BF16) for d in do]

        def head_step(b, kb, dq, mask):
            ks = pl.multiple_of(kb * T, T)
            k = k_ref[b, pl.ds(ks, T), :]
            s = _dot(q_ref[b], k, NT) * _ATT_SCALE
            if mask is not None:
                s = jnp.where(mask, s, -1e30)
            p = jnp.exp(s - lse_ref[:, b * LANE:b * LANE + 1])
            dp = _dot(dob[b], v_ref[b, pl.ds(ks, T), :], NT)
            ds = p * (dp - delta[b]) * _ATT_SCALE
            return dq + _dot(ds.astype(BF16), k)

        def step(kb, dqs, mask=None):
            return tuple(head_step(b, kb, dqs[b], mask) for b in range(HB))

        dqs = step(qi, lax.fori_loop(0, qi, step, (jnp.zeros((T, 2 * LANE), F32),) * HB), _diag_mask())
        for b in range(HB):
            dq_ref[b] = dqs[b]
            dl_t_ref[b] = _row_form(delta[b])

    col = pl.BlockSpec((T, HB * LANE), lambda h, i: (i, h))
    return pl.pallas_call(
        body, name="attn_dq", grid=(H // HB, S // T),
        in_specs=[pl.BlockSpec((HB, T, 2 * LANE), lambda h, i: (h, i, 0)), pl.BlockSpec((HB, S, 2 * LANE), lambda h, i: (h, 0, 0)),
                  pl.BlockSpec((HB, S, LANE), lambda h, i: (h, 0, 0)), col, col, col, _ANY],
        out_specs=[pl.BlockSpec((HB, T, 2 * LANE), lambda h, i: (h, i, 0)), pl.BlockSpec((HB, 8, T), lambda h, i: (h, 0, i))],
        out_shape=[jax.ShapeDtypeStruct((H, S, 2 * LANE), F32), jax.ShapeDtypeStruct((H, 8, S), F32)],
        compiler_params=_params(("parallel", "parallel")),
    )(Q, K, V, do, o, lse, after)


def _attn_dkv(cfg, Q, K, V, do, lse_t, delta_t):
    S, H, T, HB = cfg.S, cfg.H, _ATT_T, _ATT_HB
    nq = S // T

    def body(q_ref, k_ref, v_ref, do_ref, lse_ref, dl_ref, dk_ref, dv_ref):
        kb = pl.program_id(1)

        def head_step(b, qi, carry, mask):
            dk, dv = carry
            qs = pl.multiple_of(qi * T, T)
            q = q_ref[b, pl.ds(qs, T), :]
            dob = do_ref[pl.ds(qs, T), b * LANE:(b + 1) * LANE].astype(BF16)
            s = _dot(k_ref[b], q, NT) * _ATT_SCALE
            if mask is not None:
                s = jnp.where(mask, s, -1e30)
            p = jnp.exp(s - lse_ref[b, 0:1, pl.ds(qs, T)])
            dv = dv + _dot(p.astype(BF16), dob)
            dp = _dot(v_ref[b], dob, NT)
            ds = p * (dp - dl_ref[b, 0:1, pl.ds(qs, T)]) * _ATT_SCALE
            dk = dk + _dot(ds.astype(BF16), q)
            return dk, dv

        def step(qi, carry, mask=None):
            return tuple(head_step(b, qi, carry[b], mask) for b in range(HB))

        zero = (jnp.zeros((T, 2 * LANE), F32), jnp.zeros((T, VH), F32))
        done = lax.fori_loop(kb + 1, nq, step, step(kb, (zero,) * HB, _diag_mask(transposed=True)))
        for b, (dk, dv) in enumerate(done):
            dk_ref[b] = dk
            dv_ref[b] = dv

    row = pl.BlockSpec((HB, 8, S), lambda h, j: (h, 0, 0))
    return pl.pallas_call(
        body, name="attn_dkv", grid=(H // HB, S // T),
        in_specs=[pl.BlockSpec((HB, S, 2 * LANE), lambda h, j: (h, 0, 0)), pl.BlockSpec((HB, T, 2 * LANE), lambda h, j: (h, j, 0)),
                  pl.BlockSpec((HB, T, LANE), lambda h, j: (h, j, 0)), pl.BlockSpec((S, HB * LANE), lambda h, j: (0, h)), row, row],
        out_specs=[pl.BlockSpec((HB, T, 2 * LANE), lambda h, j: (h, j, 0)), pl.BlockSpec((HB, T, LANE), lambda h, j: (h, j, 0))],
        out_shape=[jax.ShapeDtypeStruct((H, S, 2 * LANE), F32), jax.ShapeDtypeStruct((H, S, LANE), F32)],
        compiler_params=_params(("parallel", "parallel")),
    )(Q, K, V, do, lse_t, delta_t)


def _expand_matrix(cfg):
    r = lax.broadcasted_iota(I32, (LANE, cfg.INNER), 0)
    c = lax.broadcasted_iota(I32, (LANE, cfg.INNER), 1)
    return (r == c // HP).astype(F32)


def _softplus(x):
    return jnp.maximum(x, 0.0) + jnp.log(1.0 + jnp.exp(-jnp.abs(x)))


def _ssd_prep(cfg, dt_raw, dt_bias_pad, a_log_pad, expand):
    HS = cfg.HS

    def fn(raw, bias, alog, E):
        heads = lax.broadcasted_iota(I32, raw.shape, 1) < HS
        dt = jnp.where(heads, _softplus(raw + bias), 0.0)
        a = dt * jnp.where(heads[0:1], -jnp.exp(alog), 0.0)
        return dt, a, _dot(dt, E, precision=HI), _dot(a, E, precision=HI)

    return _rowwise("ssd_prep", fn, [dt_raw], [dt_bias_pad, a_log_pad, expand],
                    [(LANE, F32), (LANE, F32), (cfg.INNER, F32), (cfg.INNER, F32)], [], _pick(cfg.S, 512, 8))


def _tril(T):
    return lax.broadcasted_iota(I32, (T, T), 0) >= lax.broadcasted_iota(I32, (T, T), 1)


def _ssd_fwd(cfg, xc, dt_exp, a_exp, a_small, dskip_exp):
    S, T, INNER, G, NPAIR = cfg.S, cfg.T, cfg.INNER, cfg.G, cfg.NPAIR
    NC = S // T

    def body(xc_ref, dte_ref, ae_ref, as_ref, dsk_ref, y_ref, hin_ref, ht_ref):
        @pl.when(pl.program_id(0) == 0)
        def _():
            ht_ref[...] = jnp.zeros_like(ht_ref)

        tril = _tril(T)
        tri = tril.astype(F32)
        acs_s = _dot(tri, as_ref[...], precision=HI)
        acs_e = _dot(tri, ae_ref[...], precision=HI)
        acs_t = acs_s.T
        lo = lax.broadcasted_iota(I32, (T, LANE), 1) < HP
        for g in range(G):
            Bb = xc_ref[:, INNER + g * NST:INNER + (g + 1) * NST].astype(BF16)
            Cb = xc_ref[:, INNER + (G + g) * NST:INNER + (G + g + 1) * NST].astype(BF16)
            Gm = _dot(Cb, Bb, NT)
            for j in range(g * NPAIR // G, (g + 1) * NPAIR // G):
                sl = slice(j * LANE, (j + 1) * LANE)
                Xp = xc_ref[:, sl]
                Xdt = Xp * dte_ref[:, sl]
                Xb = Xdt.astype(BF16)
                acs_p = acs_e[:, sl]
                last = acs_p[T - 1:T, :]
                Hin = ht_ref[j]
                hin_ref[0, j] = Hin
                yd = []
                for e in (0, 1):
                    h = 2 * j + e
                    Lm = jnp.exp(jnp.where(tril, acs_s[:, h:h + 1] - acs_t[h:h + 1, :], -1e30))
                    yd.append(_dot((Gm * Lm).astype(BF16), Xb))
                y_off = _dot(Cb, Hin.astype(BF16)) * jnp.exp(acs_p)
                y_ref[:, sl] = jnp.where(lo, yd[0], yd[1]) + y_off + Xp * dsk_ref[:, sl]
                st = _dot(Bb, (Xdt * jnp.exp(last - acs_p)).astype(BF16), TN)
                ht_ref[j] = jnp.exp(last) * Hin + st

    rows = lambda w: pl.BlockSpec((T, w), lambda c: (c, 0))
    return pl.pallas_call(
        body, name="ssd_fwd", grid=(NC,),
        in_specs=[rows(cfg.CONVCH), rows(INNER), rows(INNER), rows(LANE), pl.BlockSpec((1, INNER), lambda c: (0, 0))],
        out_specs=[rows(INNER), pl.BlockSpec((1, NPAIR, NST, LANE), lambda c: (c, 0, 0, 0))],
        out_shape=[jax.ShapeDtypeStruct((S, INNER), F32), jax.ShapeDtypeStruct((NC, NPAIR, NST, LANE), F32)],
        scratch_shapes=[pltpu.VMEM((NPAIR, NST, LANE), F32)],
        compiler_params=_params(("arbitrary",)),
    )(xc, dt_exp, a_exp, a_small, dskip_exp)


def _ssd_bwd(cfg, dy, xc, dt_exp, a_exp, a_small, dskip_exp, hin, dt_raw, dt_bias_pad, a_log_pad, expand):
    S, T, INNER, G, NPAIR, HS = cfg.S, cfg.T, cfg.INNER, cfg.G, cfg.NPAIR, cfg.HS
    NC = S // T

    def body(dy_ref, xc_ref, dte_ref, ae_ref, as_ref, dsk_ref, hin_ref, raw_ref, bias_ref, alog_ref, e_ref,
             dxc_ref, draw_ref, dbias_ref, dalog_ref, dskip_ref, dht_ref, cols_ref, rows_ref, dacs_ref, ddt_ref):
        first = pl.program_id(0) == 0

        @pl.when(first)
        def _():
            dht_ref[...] = jnp.zeros_like(dht_ref)

        tril = _tril(T)
        tri = tril.astype(F32)
        a_s = as_ref[...]
        acs_s = _dot(tri, a_s, precision=HI)
        acs_e = _dot(tri, ae_ref[...], precision=HI)
        acs_t = acs_s.T
        lo = lax.broadcasted_iota(I32, (T, LANE), 1) < HP
        last_row = lax.broadcasted_iota(I32, (T, LANE), 0) == T - 1
        cols_ref[...] = jnp.zeros_like(cols_ref)
        rows_ref[...] = jnp.zeros_like(rows_ref)
        dsk_parts = []
        for g in range(G):
            bsl = slice(INNER + g * NST, INNER + (g + 1) * NST)
            csl = slice(INNER + (G + g) * NST, INNER + (G + g + 1) * NST)
            Bb = xc_ref[:, bsl].astype(BF16)
            Cb = xc_ref[:, csl].astype(BF16)
            Gm = _dot(Cb, Bb, NT)
            dG = jnp.zeros((T, T), F32)
            dB = jnp.zeros((T, NST), F32)
            dC = jnp.zeros((T, NST), F32)
            for j in range(g * NPAIR // G, (g + 1) * NPAIR // G):
                sl = slice(j * LANE, (j + 1) * LANE)
                Xp = xc_ref[:, sl]
                dtp = dte_ref[:, sl]
                Xdt = Xp * dtp
                Xb = Xdt.astype(BF16)
                acs_p = acs_e[:, sl]
                last = acs_p[T - 1:T, :]
                e_p, dec, cd = jnp.exp(acs_p), jnp.exp(last - acs_p), jnp.exp(last)
                Hin = hin_ref[0, j]
                Hb = Hin.astype(BF16)
                dHn = dht_ref[j]
                dHb = dHn.astype(BF16)
                dYp = dy_ref[:, sl]
                z = _dot(Cb, Hb)
                dz = (dYp * e_p).astype(BF16)
                dacs_p = dYp * z * e_p
                dC = dC + _dot(dz, Hb, NT)
                dHin = _dot(Cb, dz, TN) + cd * dHn
                dlast = _colsum(dHn * Hin) * cd
                qv = _dot(Bb, dHb)
                dXdt = qv * dec
                ddec = qv * Xdt * dec
                dacs_p = dacs_p - ddec
                dlast = dlast + _colsum(ddec)
                dB = dB + _dot((Xdt * dec).astype(BF16), dHb, NT)
                for e in (0, 1):
                    h = 2 * j + e
                    Lm = jnp.exp(jnp.where(tril, acs_s[:, h:h + 1] - acs_t[h:h + 1, :], -1e30))
                    Mh = Gm * Lm
                    dYe = jnp.where(lo if e == 0 else jnp.logical_not(lo), dYp, 0.0).astype(BF16)
                    dM = _dot(dYe, Xb, NT)
                    dXdt = dXdt + _dot(Mh.astype(BF16), dYe, TN)
                    W = dM * Mh
                    cols_ref[:, h:h + 1] = jnp.sum(W, axis=1, keepdims=True)
                    rows_ref[h:h + 1, :] = _colsum(W)
                    dG = dG + dM * Lm
                dacs_ref[:, sl] = dacs_p + jnp.where(last_row, dlast, 0.0)
                ddt_ref[:, sl] = dXdt * Xp
                dxc_ref[:, sl] = dXdt * dtp + dYp * dsk_ref[:, sl]
                dsk_parts.append(_colsum(dYp * Xp))
                dht_ref[j] = dHin
            dGb = dG.astype(BF16)
            dxc_ref[:, bsl] = dB + _dot(dGb, Cb, TN)
            dxc_ref[:, csl] = dC + _dot(dGb, Bb)
        E = e_ref[...]
        dacs_s = cols_ref[...] - rows_ref[...].T + _dot(dacs_ref[...], E, NT, precision=HI)
        da = _dot(tri, dacs_s, TN, precision=HI)
        heads = lax.broadcasted_iota(I32, (1, LANE), 1) < HS
        A = jnp.where(heads, -jnp.exp(alog_ref[...]), 0.0)
        ddt = _dot(ddt_ref[...], E, NT, precision=HI) + da * A
        draw = jnp.where(heads, ddt * _sigmoid(raw_ref[...] + bias_ref[...]), 0.0)
        draw_ref[...] = draw
        dsk = _dot(jnp.broadcast_to(jnp.concatenate(dsk_parts, axis=1), (8, INNER)), E, NT, precision=HI)[0:1]
        for ref, val in ((dbias_ref, _colsum(draw)), (dalog_ref, _colsum(da * a_s)), (dskip_ref, dsk)):
            @pl.when(first)
            def _():
                ref[...] = val

            @pl.when(jnp.logical_not(first))
            def _():
                ref[...] += val

    rows = lambda w: pl.BlockSpec((T, w), lambda c: (NC - 1 - c, 0))
    vec = lambda w: pl.BlockSpec((1, w), lambda c: (0, 0))
    return pl.pallas_call(
        body, name="ssd_bwd", grid=(NC,),
        in_specs=[rows(INNER), rows(cfg.CONVCH), rows(INNER), rows(INNER), rows(LANE), vec(INNER),
                  pl.BlockSpec((1, NPAIR, NST, LANE), lambda c: (NC - 1 - c, 0, 0, 0)), rows(LANE), vec(LANE), vec(LANE),
                  pl.BlockSpec((LANE, INNER), lambda c: (0, 0))],
        out_specs=[rows(cfg.CONVCH), rows(LANE), vec(LANE), vec(LANE), vec(LANE)],
        out_shape=[jax.ShapeDtypeStruct((S, cfg.CONVCH), F32), jax.ShapeDtypeStruct((S, LANE), F32)]
        + [jax.ShapeDtypeStruct((1, LANE), F32)] * 3,
        scratch_shapes=[pltpu.VMEM((NPAIR, NST, LANE), F32), pltpu.VMEM((T, LANE), F32), pltpu.VMEM((LANE, T), F32),
                        pltpu.VMEM((T, INNER), F32), pltpu.VMEM((T, INNER), F32)],
        compiler_params=_params(("arbitrary",)),
    )(dy, xc, dt_exp, a_exp, a_small, dskip_exp, hin, dt_raw, dt_bias_pad, a_log_pad, expand)


def _ssd_post(cfg, y, z, norm_g):
    W = cfg.INNER // cfg.G

    def fn(y, z, g):
        yz = y * z * _sigmoid(z)
        return jnp.concatenate([yz[:, i * W:(i + 1) * W] * _rs(yz[:, i * W:(i + 1) * W]) for i in range(cfg.G)], axis=1) * g

    return _rowwise("ssd_post", fn, [y, z], [norm_g], [(cfg.INNER, BF16)], [], _pick(cfg.S, 256, 8))[0]


def _ssd_post_bwd(cfg, db, y, z, norm_g):
    W = cfg.INNER // cfg.G

    def fn(db, y, z, g):
        sg = _sigmoid(z)
        yz = y * z * sg
        dn = db * g
        dyz, nh = [], []
        for i in range(cfg.G):
            seg = yz[:, i * W:(i + 1) * W]
            r = _rs(seg)
            nh.append(seg * r)
            dyz.append(_rms_back(nh[-1], r, dn[:, i * W:(i + 1) * W]))
        dyz = jnp.concatenate(dyz, axis=1)
        return dyz * z * sg, dyz * y * sg * (1.0 + z * (1.0 - sg)), _colsum(db * jnp.concatenate(nh, axis=1))

    return _rowwise("ssd_post_bwd", fn, [db, y, z], [norm_g], [(cfg.INNER, F32), (cfg.INNER, F32)], [(1, cfg.INNER)],
                    _pick(cfg.S, 256, 8))


def _local_grads(cfg, x, tgt, W, sp, mla_weights=None, out_weight=None, ffn_weights=None, ffn_grads_ready=None,
                 early_grads_ready=None, in_grad_ready=None):
    S, D, H, INNER = cfg.S, cfg.D, cfg.H, cfg.INNER
    ts = _pick(S, 256, 8)
    tc = 256

    xn = _rowwise("rms_pre", lambda x, g: x * _rs(x) * g, [x], [sp["mix_pre_g"]], [(D, BF16)], [], ts)[0]
    u = _matmul("mm_in", xn, W["w_in"], "nt", F32)
    c_q, c_kv = u[:, :cfg.QL], u[:, cfg.QL:cfg.o_kr]
    kr = u[:, cfg.o_kr:cfg.o_z]
    z = u[:, cfg.o_z:cfg.o_xbc]
    xbc = u[:, cfg.o_xbc:cfg.o_dt]
    dt_raw = u[:, cfg.o_dt:]

    if mla_weights is not None:
        sp = dict(sp, q_norm_g=sp["q_norm_g"] + mla_weights.pass_on(u)[0, 0])
    cqn = _rowwise("rms_q", lambda x, g: x * _rs(x) * g, [c_q], [sp["q_norm_g"]], [(cfg.QL, BF16)], [], ts)[0]
    ckvn = _rowwise("rms_kv", lambda x, g: x * _rs(x) * g, [c_kv], [sp["kv_norm_g"]], [(cfg.KVL, BF16)], [], ts)[0]
    if mla_weights is not None:
        W = dict(W, **mla_weights.arrived(ckvn))
    q = _matmul("mm_uq", cqn, W["w_uq"], "nn", F32)
    kv = _matmul("mm_ukv", ckvn, W["w_ukv"], "nn", F32)
    cos2, sin2 = _rope_tables(S)
    Qh, Kh, Vh = _mla_pack(cfg, q, kv, kr, cos2, sin2)
    a_out, lse, lse_t = _attn_fwd(cfg, Qh, Kh, Vh)
    if out_weight is not None:
        sp = dict(sp, ssm_conv_b=sp["ssm_conv_b"] + out_weight.pass_on(a_out)[0, 0])

    pad = lambda v: jnp.pad(v, ((0, 0), (0, LANE - v.shape[1])))
    expand = _expand_matrix(cfg)
    dt_bias_pad, a_log_pad = pad(sp["dt_bias"]), pad(sp["a_log"])
    dskip_exp = jnp.repeat(sp["d_skip"], HP, axis=1)
    xc = _colwise("ssm_act", _ssm_act, [xbc], [sp["ssm_conv_w"], sp["ssm_conv_b"]], [F32], [], tc)[0]
    dt_s, a_s, dt_exp, a_exp = _ssd_prep(cfg, dt_raw, dt_bias_pad, a_log_pad, expand)
    y_ssd, hin = _ssd_fwd(cfg, xc, dt_exp, a_exp, a_s, dskip_exp)
    b_out = _ssd_post(cfg, y_ssd, z, sp["ssm_norm_g"])

    ab_out = jnp.concatenate([a_out.astype(BF16), b_out], axis=1)
    if out_weight is not None:
        W = dict(W, **out_weight.arrived(ab_out))
    if ffn_weights is not None:
        sp = dict(sp, mix_post_g=sp["mix_post_g"] + ffn_weights.pass_on(ab_out)[0, 0])
    mix = _matmul("mm_out", ab_out, W["w_out"], "nn", F32)

    def mid(x, mix, g_mp, g_fp):
        x1 = x + mix * _rs(mix) * g_mp
        return x1, x1 * _rs(x1) * g_fp

    x1, h2 = _rowwise("fwd_mid", mid, [x, mix], [sp["mix_post_g"], sp["ffn_pre_g"]], [(D, F32), (D, BF16)], [], ts)
    if ffn_weights is not None:
        W = dict(W, **ffn_weights.arrived(h2))
    gate_pre = _matmul("mm_gate", h2, W["w_gate"], "nn", F32, chips=True)
    up = _matmul("mm_up", h2, W["w_up"], "nn", F32, chips=True)
    act = _colwise("ffn_act", _ffn_act, [gate_pre, up], [sp["ffn_conv_w"], sp["ffn_conv_b"]], [BF16], [], tc)[0]
    f = _matmul("mm_down", act, W["w_down"], "nn", F32)

    def final(x1, f, t, g):
        r = _rs(f)
        fh = f * r
        err = x1 + fh * g - t
        loss = 0.5 * jnp.sum(jnp.mean(err * err, axis=-1, keepdims=True), axis=0, keepdims=True)
        dy = err * (1.0 / D)
        return dy, _rms_back(fh, r, dy * g), _colsum(dy * fh), loss

    dy, df, g_ffn_post, loss = _rowwise("final", final, [x1, f, tgt], [sp["ffn_post_g"]], [(D, F32), (D, BF16)],
                                        [(1, D), (1, LANE)], ts)
    gW = {}
    dact = _matmul("mm_down_dx", df, W["w_down"], "nt", F32)
    gW["w_down"] = _matmul("mm_down_dw", act, df, "tn", BF16)
    dgate, dup, g_ffn_conv_w, g_ffn_conv_b = _colwise(
        "ffn_act_bwd", _ffn_act_back, [dact, gate_pre, up], [sp["ffn_conv_w"], sp["ffn_conv_b"]], [BF16, BF16], [FFN_K, 1], tc)
    gW["w_gate"] = _matmul("mm_gate_dw", h2, dgate, "tn", BF16, chips=True)
    gW["w_up"] = _matmul("mm_up_dw", h2, dup, "tn", BF16, chips=True)
    if ffn_grads_ready is not None:
        sp = dict(sp, ffn_pre_g=sp["ffn_pre_g"] + ffn_grads_ready({n: gW[n] for n in ("w_down", "w_gate", "w_up")})[0, 0])
    dh2 = _matmul("mm_gu_dx", dgate, W["w_gate"], "nt", F32, dup, W["w_up"], chips=True)

    def mid_back(dy, dh2, x1, mix, g_mp, g_fp):
        r2 = _rs(x1)
        xh = x1 * r2
        dx1 = dy + _rms_back(xh, r2, dh2 * g_fp)
        r1 = _rs(mix)
        mh = mix * r1
        return dx1, _rms_back(mh, r1, dx1 * g_mp), _colsum(dh2 * xh), _colsum(dx1 * mh)

    dx1, dmix, g_ffn_pre, g_mix_post = _rowwise("bwd_mid", mid_back, [dy, dh2, x1, mix], [sp["mix_post_g"], sp["ffn_pre_g"]],
                                                [(D, F32), (D, BF16)], [(1, D), (1, D)], ts)
    dab_out = _matmul("mm_out_dx", dmix, W["w_out"], "nt", F32)
    db_out = dab_out[:, cfg.MLAW:]
    gW["w_out"] = _matmul("mm_out_dw", ab_out, dmix, "tn", BF16)
    early_token = jnp.zeros((8, LANE), F32)
    if early_grads_ready is not None:
        early_token = early_grads_ready({n: gW[n] for n in ("w_down", "w_gate", "w_up", "w_out")})
        sp = dict(sp, ssm_norm_g=sp["ssm_norm_g"] + early_token[0, 0])

    dy_ssd, dz, g_ssm_norm = _ssd_post_bwd(cfg, db_out, y_ssd, z, sp["ssm_norm_g"])
    dxc, ddt_raw, g_dt_bias, g_a_log, g_d_skip = _ssd_bwd(cfg, dy_ssd, xc, dt_exp, a_exp, a_s, dskip_exp, hin, dt_raw,
                                                          dt_bias_pad, a_log_pad, expand)
    dxbc, g_ssm_conv_w, g_ssm_conv_b = _colwise("ssm_act_bwd", _ssm_act_back, [dxc, xbc], [sp["ssm_conv_w"], sp["ssm_conv_b"]],
                                                [BF16], [SSM_K, 1], tc)

    dQ, delta_t = _attn_dq(cfg, Qh, Kh, Vh, dab_out, a_out, lse, early_token)
    dK, dV = _attn_dkv(cfg, Qh, Kh, Vh, dab_out, lse_t, delta_t)
    dq, dkv, dkr = _mla_unpack(cfg, dQ, dK, dV, cos2, sin2)
    dcqn = _matmul("mm_uq_dx", dq, W["w_uq"], "nt", F32)
    dckvn = _matmul("mm_ukv_dx", dkv, W["w_ukv"], "nt", F32)
    gW["w_uq"] = _matmul("mm_uq_dw", cqn, dq, "tn", BF16)
    gW["w_ukv"] = _matmul("mm_ukv_dw", ckvn, dkv, "tn", BF16)

    def rms_back(x, dy, g):
        r = _rs(x)
        xh = x * r
        return _rms_back(xh, r, dy * g), _colsum(dy * xh)

    dc_q, g_q_norm = _rowwise("rms_q_bwd", rms_back, [c_q, dcqn], [sp["q_norm_g"]], [(cfg.QL, BF16)], [(1, cfg.QL)], ts)
    dc_kv, g_kv_norm = _rowwise("rms_kv_bwd", rms_back, [c_kv, dckvn], [sp["kv_norm_g"]], [(cfg.KVL, BF16)], [(1, cfg.KVL)], ts)

    du = jnp.concatenate([dc_q, dc_kv, dkr, dz.astype(BF16), dxbc, ddt_raw.astype(BF16)], axis=1)
    gW["w_in"] = _matmul("mm_in_dw", du, xn, "tn", BF16)
    if in_grad_ready is not None:
        token = in_grad_ready({n: gW[n] for n in ("w_in", "w_uq", "w_ukv")})
        sp = dict(sp, mix_pre_g=sp["mix_pre_g"] + token[0, 0])
    dxn = _matmul("mm_in_dx", du, W["w_in"], "nn", F32)

    def first_back(dx1, dxn, x, g):
        r = _rs(x)
        xh = x * r
        return dx1 + _rms_back(xh, r, dxn * g), _colsum(dxn * xh)

    grad_x, g_mix_pre = _rowwise("bwd_first", first_back, [dx1, dxn, x], [sp["mix_pre_g"]], [(D, F32)], [(1, D)], ts)

    gs = dict(mix_pre_g=g_mix_pre, q_norm_g=g_q_norm, kv_norm_g=g_kv_norm, ssm_conv_w=g_ssm_conv_w, ssm_conv_b=g_ssm_conv_b,
              dt_bias=g_dt_bias[:, :cfg.HS], a_log=g_a_log[:, :cfg.HS], d_skip=g_d_skip[:, :cfg.HS], ssm_norm_g=g_ssm_norm,
              mix_post_g=g_mix_post, ffn_pre_g=g_ffn_pre, ffn_conv_w=g_ffn_conv_w, ffn_conv_b=g_ffn_conv_b,
              ffn_post_g=g_ffn_post)
    return loss, grad_x, gW, gs


def _to_kernel_layout(cfg, name, w):
    if name == "w_in":
        a = cfg.o_kr + ROPE
        return jnp.concatenate([w[:a], jnp.zeros((LANE - ROPE, w.shape[1]), w.dtype), w[a:],
                                jnp.zeros((LANE - cfg.HS, w.shape[1]), w.dtype)], axis=0)
    if name in ("w_uq", "w_ukv"):
        per = NOPE + (ROPE if name == "w_uq" else VH)
        return jnp.concatenate([w[:, h * per:h * per + NOPE] for h in range(cfg.H)]
                               + [w[:, h * per + NOPE:(h + 1) * per] for h in range(cfg.H)], axis=1)
    return w


def _from_kernel_layout(cfg, name, g):
    if name == "w_in":
        return jnp.concatenate([g[:cfg.o_kr + ROPE], g[cfg.o_z:cfg.o_dt + cfg.HS]], axis=0)
    if name in ("w_uq", "w_ukv"):
        second = ROPE if name == "w_uq" else VH
        base = cfg.H * NOPE
        parts = []
        for h in range(cfg.H):
            parts += [g[:, h * NOPE:(h + 1) * NOPE], g[:, base + h * second:base + (h + 1) * second]]
        return jnp.concatenate(parts, axis=1)
    return g


def _cols_to_chips(w):
    r, c = w.shape
    return w.reshape(r, N_CHIPS, c // N_CHIPS).transpose(1, 0, 2)


def _chips_to_cols(g):
    k, r, cs = g.shape
    return g.transpose(1, 0, 2).reshape(r, k * cs)


_CHIP_MAJOR = ("w_gate", "w_up")
_RELAYOUT = ("w_uq", "w_ukv")
_LAYOUT_ROWS = 256


def _w_in_layout(cfg, wg):
    _, rs, d = wg.shape
    tc = _pick(d, _LAYOUT_ROWS, LANE)

    def body(w_ref, o_ref):
        o_ref[...] = _to_kernel_layout(cfg, "w_in", jnp.concatenate([w_ref[k] for k in range(N_CHIPS)], axis=0))

    return pl.pallas_call(
        body, name="layout_w_in", grid=(d // tc,),
        in_specs=[pl.BlockSpec((N_CHIPS, rs, tc), lambda j: (0, 0, j))], out_specs=pl.BlockSpec((cfg.EXT, tc), lambda j: (0, j)),
        out_shape=jax.ShapeDtypeStruct((cfg.EXT, d), wg.dtype), compiler_params=_params(("parallel",)),
    )(wg)


def _w_in_grad_to_chips(cfg, g):
    _, d = g.shape
    rs = cfg.IN_COLS // N_CHIPS
    tc = _pick(d, _LAYOUT_ROWS, LANE)

    def body(g_ref, o_ref):
        nat = _from_kernel_layout(cfg, "w_in", g_ref[...])
        for k in range(N_CHIPS):
            o_ref[k] = nat[k * rs:(k + 1) * rs]

    return pl.pallas_call(
        body, name="layout_grad_w_in", grid=(d // tc,),
        in_specs=[pl.BlockSpec((cfg.EXT, tc), lambda j: (0, j))], out_specs=pl.BlockSpec((N_CHIPS, rs, tc), lambda j: (0, 0, j)),
        out_shape=jax.ShapeDtypeStruct((N_CHIPS, rs, d), g.dtype), compiler_params=_params(("parallel",)),
    )(g)


def _gathered_to_kernel(cfg, name, wg):
    if name in _CHIP_MAJOR:
        return wg
    if name == "w_in":
        return _w_in_layout(cfg, wg)
    if name not in _RELAYOUT:
        return wg.reshape(wg.shape[0] * wg.shape[1], wg.shape[2])
    _, rows, cs = wg.shape
    tr = _pick(rows, _LAYOUT_ROWS, 16)

    def body(w_ref, o_ref):
        o_ref[...] = _to_kernel_layout(cfg, name, jnp.concatenate([w_ref[k] for k in range(N_CHIPS)], axis=1))

    wide = jax.eval_shape(lambda w: _to_kernel_layout(cfg, name, w), jax.ShapeDtypeStruct((rows, N_CHIPS * cs), wg.dtype)).shape[1]
    return pl.pallas_call(
        body, name="layout_" + name, grid=(rows // tr,),
        in_specs=[pl.BlockSpec((N_CHIPS, tr, cs), lambda i: (0, i, 0))], out_specs=pl.BlockSpec((tr, wide), lambda i: (i, 0)),
        out_shape=jax.ShapeDtypeStruct((rows, wide), wg.dtype), compiler_params=_params(("parallel",)),
    )(wg)


def _grad_to_chips(cfg, name, g):
    if name in _CHIP_MAJOR:
        return g
    if name == "w_in":
        return _w_in_grad_to_chips(cfg, g)
    if name not in _RELAYOUT:
        return g.reshape(N_CHIPS, g.shape[0] // N_CHIPS, g.shape[1])
    rows, wide = g.shape
    tr = _pick(rows, _LAYOUT_ROWS, 16)
    cs = jax.eval_shape(lambda v: _from_kernel_layout(cfg, name, v), g).shape[1] // N_CHIPS

    def body(g_ref, o_ref):
        nat = _from_kernel_layout(cfg, name, g_ref[...])
        for k in range(N_CHIPS):
            o_ref[k] = nat[:, k * cs:(k + 1) * cs]

    return pl.pallas_call(
        body, name="layout_grad_" + name, grid=(rows // tr,),
        in_specs=[pl.BlockSpec((tr, wide), lambda i: (i, 0))], out_specs=pl.BlockSpec((N_CHIPS, tr, cs), lambda i: (0, i, 0)),
        out_shape=jax.ShapeDtypeStruct((N_CHIPS, rows, cs), g.dtype), compiler_params=_params(("parallel",)),
    )(g)


def _me():
    return lax.axis_index("x"), lax.axis_index("y"), lax.axis_index("c")


def _other_chips(x, y):
    return [(1 - x, y), (x, 1 - y), (1 - x, 1 - y)]


_ANY = pl.BlockSpec(memory_space=pl.ANY)


def _row_block(rows, cols, mult):
    return _pick(rows, max(mult, (1 << 19) // cols // mult * mult), mult)


def _scalar(v):
    return v.astype(I32).reshape(1)


def _blocks2d(r, c, mult):
    if r % mult == 0:
        tr = _row_block(r, c, mult)
        return (tr, c), r // tr, lambda i: (i, 0)
    tc = _pick(c, max(LANE, (1 << 19) // r // LANE * LANE), LANE)
    return (r, tc), c // tc, lambda i: (0, i)


def _by_rows(rows):
    return rows % 32 == 0


def _half_shape(rows, cols):
    return (rows // 2, cols) if _by_rows(rows) else (rows, cols // 2)


def _half_blocks(rows, cols, mult):
    hr, hc = _half_shape(rows, cols)
    block, n, part = _blocks2d(hr, hc, mult)
    assert (hr % mult == 0) == _by_rows(rows), (rows, cols, mult)
    full = (lambda h, i: (h * n + i, 0)) if _by_rows(rows) else (lambda h, i: (0, h * n + i))
    return block, n, full, part


def _half(ref, k, half):
    hr, hc = _half_shape(ref.shape[1], ref.shape[2])
    if _by_rows(ref.shape[1]):
        return ref.at[k, pl.ds(pl.multiple_of(half * hr, 16), hr), :]
    return ref.at[k, :, pl.ds(pl.multiple_of(half * hc, LANE), hc)]


def _shard_blocks(w, br, bc):
    if w.shape[0] == 1:
        def write(ref, v):
            ref[...] = v
        return (lambda f: pl.BlockSpec((None, br, bc), lambda *a: (0, *f(*a)))), (lambda ref: ref[...]), write
    assert w.shape[1] == 1 and br == w.shape[0], w.shape

    def write_rows(ref, v):
        ref[:, 0, :] = v
    return (lambda f: pl.BlockSpec((br, 1, bc), lambda *a: (0, 0, f(*a)[1]))), (lambda ref: ref[:, 0, :]), write_rows


def _stage_shard(name, w, chip):
    rs, cs = w.shape[0] * w.shape[1], w.shape[2]
    (br, bc), n, idx = _blocks2d(rs, cs, 16)
    spec, get, _ = _shard_blocks(w, br, bc)

    def body(chip_ref, w_ref, o_ref):
        o_ref[...] = get(w_ref).astype(BF16)

    return pl.pallas_call(
        body, name="stage_" + name,
        grid_spec=pltpu.PrefetchScalarGridSpec(
            num_scalar_prefetch=1, grid=(n,),
            in_specs=[spec(lambda i, chip_ref: idx(i))],
            out_specs=pl.BlockSpec((None, br, bc), lambda i, chip_ref: (chip_ref[0], *idx(i)))),
        out_shape=jax.ShapeDtypeStruct((N_CHIPS, rs, cs), BF16),
        compiler_params=_params(("parallel",)),
    )(_scalar(chip), w)


def _allgather_weights(bufs):
    n = len(bufs)

    def body(*refs):
        outs, send_sems, recv_sems = refs[n:2 * n], refs[2 * n], refs[2 * n + 1]
        x, y, c = _me()
        chip = 2 * x + y
        sib = (x, y, 1 - c)
        chips = _other_chips(x, y)

        def copy(k, part, to):
            return pltpu.make_async_remote_copy(src_ref=part, dst_ref=part, send_sem=send_sems.at[k], recv_sem=recv_sems.at[k],
                                                device_id=to, device_id_type=MESH_ID)

        started = []
        for w, o_ref in enumerate(outs):
            for j, (cx, cy) in enumerate(chips):
                started.append(copy(6 * w + j, _half(o_ref, chip, c), (cx, cy, c)))
                started[-1].start()
        for w, o_ref in enumerate(outs):
            for j, (cx, cy) in enumerate(chips):
                theirs = _half(o_ref, 2 * cx + cy, c)
                copy(6 * w + j, theirs, sib).wait_recv()
                started.append(copy(6 * w + 3 + j, theirs, sib))
                started[-1].start()
        for w, o_ref in enumerate(outs):
            for j, (cx, cy) in enumerate(chips):
                copy(6 * w + 3 + j, _half(o_ref, 2 * cx + cy, 1 - c), sib).wait_recv()
        for cp in started:
            cp.wait_send()

    return pl.pallas_call(
        body, name="allgather_weights", in_specs=[_ANY] * n, out_specs=[_ANY] * n,
        out_shape=[jax.ShapeDtypeStruct(b.shape, b.dtype) for b in bufs],
        input_output_aliases={i: i for i in range(n)},
        scratch_shapes=[pltpu.SemaphoreType.DMA((6 * n,)), pltpu.SemaphoreType.DMA((6 * n,))],
    )(*bufs)


_HBM = pl.BlockSpec(memory_space=pltpu.HBM)
_SEM = pl.BlockSpec(memory_space=pltpu.SEMAPHORE)
_EFFECT = pltpu.SideEffectType.DATAFLOW_SIDE_EFFECTING


def _split_start(name, bufs, n_copies, copies, after):
    n = len(bufs)

    def body(*refs):
        for cp in copies(refs[:n], refs[n + 1], refs[n + 2]):
            cp.start()
        refs[-1][...] = jnp.zeros_like(refs[-1])

    res = pl.pallas_call(
        body, name=name,
        out_shape=(pltpu.SemaphoreType.DMA((n_copies,)), pltpu.SemaphoreType.DMA((n_copies,)),
                   *[pltpu.HBM(b.shape, b.dtype) for b in bufs], jax.ShapeDtypeStruct((8, LANE), F32)),
        in_specs=[_HBM] * n + [_ANY], out_specs=(_SEM, _SEM, *[_HBM] * n, pl.BlockSpec(memory_space=pltpu.VMEM)),
        input_output_aliases={i: 2 + i for i in range(n)},
        compiler_params=pltpu.CompilerParams(has_side_effects=_EFFECT),
    )(*[pltpu.with_memory_space_constraint(b, pltpu.HBM) for b in bufs], after)
    return res[0], res[1], list(res[2:2 + n]), res[-1]


def _split_wait(name, send_sems, recv_sems, bufs, after, copies):
    n = len(bufs)

    def body(*refs):
        for cp in copies(refs[:n], refs[n], refs[n + 1]):
            cp.wait_send()
            cp.wait_recv()

    return list(pl.pallas_call(
        body, name=name, out_shape=[pltpu.HBM(b.shape, b.dtype) for b in bufs],
        in_specs=[_HBM] * n + [_SEM, _SEM, _ANY], out_specs=[_HBM] * n,
        input_output_aliases={i: i for i in range(n)},
        compiler_params=pltpu.CompilerParams(has_side_effects=_EFFECT),
    )(*bufs, send_sems, recv_sems, after))


def _gather_to_chips(bufs, send_sems, recv_sems):
    x, y, c = _me()
    return [pltpu.make_async_remote_copy(src_ref=_half(b, 2 * x + y, c), dst_ref=_half(b, 2 * x + y, c),
                                         send_sem=send_sems.at[3 * w + j], recv_sem=recv_sems.at[3 * w + j],
                                         device_id=(cx, cy, c), device_id_type=MESH_ID)
            for w, b in enumerate(bufs) for j, (cx, cy) in enumerate(_other_chips(x, y))]


def _gather_to_sibling(bufs, send_sems, recv_sems):
    x, y, c = _me()
    return [pltpu.make_async_remote_copy(src_ref=_half(b, 2 * cx + cy, c), dst_ref=_half(b, 2 * cx + cy, c),
                                         send_sem=send_sems.at[3 * w + j], recv_sem=recv_sems.at[3 * w + j],
                                         device_id=(x, y, 1 - c), device_id_type=MESH_ID)
            for w, b in enumerate(bufs) for j, (cx, cy) in enumerate(_other_chips(x, y))]


def _pair_exchange(name, grads):
    n = len(grads)

    def body(*refs):
        ins, outs, send_sems, recv_sems = refs[:n], refs[n:2 * n], refs[2 * n], refs[2 * n + 1]
        x, y, c = _me()
        cps = []
        for w, (g_ref, o_ref) in enumerate(zip(ins, outs)):
            cps.append(pltpu.make_async_remote_copy(src_ref=_half(g_ref, slice(None), 1 - c), dst_ref=o_ref,
                                                    send_sem=send_sems.at[w], recv_sem=recv_sems.at[w],
                                                    device_id=(x, y, 1 - c), device_id_type=MESH_ID))
            cps[-1].start()
        for cp in cps:
            cp.wait()

    return pl.pallas_call(
        body, name="pair_exchange_" + name, in_specs=[_ANY] * n, out_specs=[_ANY] * n,
        out_shape=[jax.ShapeDtypeStruct((g.shape[0], *_half_shape(g.shape[1], g.shape[2])), g.dtype) for g in grads],
        scratch_shapes=[pltpu.SemaphoreType.DMA((n,)), pltpu.SemaphoreType.DMA((n,))],
    )(*grads)


def _pair_copies(grads, lands, send_sems, recv_sems):
    x, y, c = _me()
    return [pltpu.make_async_remote_copy(src_ref=_half(g_ref, slice(None), 1 - c), dst_ref=l_ref, send_sem=send_sems.at[w],
                                         recv_sem=recv_sems.at[w], device_id=(x, y, 1 - c), device_id_type=MESH_ID)
            for w, (g_ref, l_ref) in enumerate(zip(grads, lands))]


def _pair_exchange_start(name, grads):
    n = len(grads)
    lands = [lax.empty((g.shape[0], *_half_shape(g.shape[1], g.shape[2])), g.dtype) for g in grads]
    send_sems, recv_sems, bufs, token = _split_start(
        "pair_exchange_start_" + name, [*grads, *lands], n, lambda refs, ss, rs: _pair_copies(refs[:n], refs[n:], ss, rs),
        jnp.zeros((8, LANE), F32))
    return (send_sems, recv_sems, bufs), token


def _pair_exchange_wait(name, state, after):
    send_sems, recv_sems, bufs = state
    n = len(bufs) // 2
    bufs = _split_wait("pair_exchange_wait_" + name, send_sems, recv_sems, bufs, after,
                       lambda refs, ss, rs: _pair_copies(refs[:n], refs[n:], ss, rs))
    return bufs[:n], bufs[n:]


def _pair_sum(name, g, theirs, c):
    (br, bc), nb, full, part = _half_blocks(g.shape[1], g.shape[2], 16)

    def body(c_ref, a_ref, b_ref, o_ref):
        o_ref[...] = (a_ref[...].astype(F32) + b_ref[...].astype(F32)).astype(o_ref.dtype)

    return pl.pallas_call(
        body, name="pair_sum_" + name,
        grid_spec=pltpu.PrefetchScalarGridSpec(
            num_scalar_prefetch=1, grid=(N_CHIPS, nb),
            in_specs=[pl.BlockSpec((None, br, bc), lambda k, i, c_ref: (k, *full(c_ref[0], i))),
                      pl.BlockSpec((None, br, bc), lambda k, i, c_ref: (k, *part(i)))],
            out_specs=pl.BlockSpec((None, br, bc), lambda k, i, c_ref: (k, *part(i)))),
        out_shape=jax.ShapeDtypeStruct(theirs.shape, BF16),
        compiler_params=_params(("parallel", "parallel")),
    )(_scalar(c), g, theirs)


def _chip_copies(srcs, lands, send_sems, recv_sems):
    x, y, c = _me()
    return [pltpu.make_async_remote_copy(src_ref=s_ref.at[2 * cx + cy], dst_ref=l_ref.at[j], send_sem=send_sems.at[3 * w + j],
                                         recv_sem=recv_sems.at[3 * w + j], device_id=(cx, cy, c), device_id_type=MESH_ID)
            for w, (s_ref, l_ref) in enumerate(zip(srcs, lands)) for j, (cx, cy) in enumerate(_other_chips(x, y))]


def _chip_exchange_start(name, sums):
    n = len(sums)
    lands = [lax.empty((3,) + s.shape[1:], s.dtype) for s in sums]
    send_sems, recv_sems, bufs, token = _split_start(
        "chip_exchange_start_" + name, [*sums, *lands], 3 * n, lambda refs, ss, rs: _chip_copies(refs[:n], refs[n:], ss, rs),
        jnp.zeros((8, LANE), F32))
    return send_sems, recv_sems, bufs[:n], bufs[n:], token


def _chip_exchange_wait(name, send_sems, recv_sems, sums, lands, after):
    n = len(sums)
    bufs = _split_wait("chip_exchange_wait_" + name, send_sems, recv_sems, [*sums, *lands], after,
                       lambda refs, ss, rs: _chip_copies(refs[:n], refs[n:], ss, rs))
    return bufs[:n], bufs[n:]


def _chip_sum(name, sums, theirs, chip):
    _, h, cs = sums.shape
    (br, bc), nb, idx = _blocks2d(h, cs, 16)

    def body(chip_ref, s_ref, t_ref, o_ref):
        acc = s_ref[...].astype(F32)
        for k in range(3):
            acc = acc + t_ref[k].astype(F32)
        o_ref[...] = acc

    return pl.pallas_call(
        body, name="chip_sum_" + name,
        grid_spec=pltpu.PrefetchScalarGridSpec(
            num_scalar_prefetch=1, grid=(nb,),
            in_specs=[pl.BlockSpec((None, br, bc), lambda i, chip_ref: (chip_ref[0], *idx(i))),
                      pl.BlockSpec((3, br, bc), lambda i, chip_ref: (0, *idx(i)))],
            out_specs=pl.BlockSpec((br, bc), lambda i, chip_ref: idx(i))),
        out_shape=jax.ShapeDtypeStruct((h, cs), F32),
        compiler_params=_params(("parallel",)),
    )(_scalar(chip), sums, theirs)


def _sibling_exchange(name, halves):
    n = len(halves)

    def body(*refs):
        ins, outs, send_sems, recv_sems = refs[:n], refs[n:2 * n], refs[2 * n], refs[2 * n + 1]
        x, y, c = _me()
        cps = []
        for w, (h_ref, o_ref) in enumerate(zip(ins, outs)):
            cps.append(pltpu.make_async_remote_copy(src_ref=h_ref, dst_ref=o_ref, send_sem=send_sems.at[w], recv_sem=recv_sems.at[w],
                                                    device_id=(x, y, 1 - c), device_id_type=MESH_ID))
            cps[-1].start()
        for cp in cps:
            cp.wait()

    return pl.pallas_call(
        body, name="sibling_exchange_" + name, in_specs=[_ANY] * n, out_specs=[_ANY] * n,
        out_shape=[jax.ShapeDtypeStruct(h.shape, h.dtype) for h in halves],
        scratch_shapes=[pltpu.SemaphoreType.DMA((n,)), pltpu.SemaphoreType.DMA((n,))],
    )(*halves)


def _allreduce_small(name, vec, after):
    def body(v_ref, after_ref, o_ref, buf_ref, send_sems, recv_sems):
        x, y, c = _me()
        me = 4 * x + 2 * y + c
        cps = []
        for p in range(1, 8):
            px, py, pc = x ^ (p >> 2), y ^ ((p >> 1) & 1), c ^ (p & 1)
            cps.append(pltpu.make_async_remote_copy(src_ref=v_ref, dst_ref=buf_ref.at[me], send_sem=send_sems.at[p - 1],
                                                    recv_sem=recv_sems.at[p - 1], device_id=(px, py, pc), device_id_type=MESH_ID))
            cps[-1].start()
        buf_ref[me] = v_ref[...]
        for p in range(1, 8):
            theirs = buf_ref.at[me ^ p]
            pltpu.make_async_remote_copy(src_ref=theirs, dst_ref=theirs, send_sem=send_sems.at[p - 1], recv_sem=recv_sems.at[p - 1],
                                         device_id=(x, y, c), device_id_type=MESH_ID).wait_recv()
        for cp in cps:
            cp.wait_send()
        acc = buf_ref[0]
        for k in range(1, 8):
            acc = acc + buf_ref[k]
        o_ref[...] = acc

    vm = pl.BlockSpec(memory_space=pltpu.VMEM)
    return pl.pallas_call(
        body, name=name, in_specs=[vm, _ANY], out_specs=vm, out_shape=jax.ShapeDtypeStruct(vec.shape, F32),
        scratch_shapes=[pltpu.VMEM((8,) + vec.shape, F32), pltpu.SemaphoreType.DMA((7,)), pltpu.SemaphoreType.DMA((7,))],
    )(vec, after)


def _adam_math(w, g, m, v):
    m = ADAM_B1 * m + (1.0 - ADAM_B1) * g
    v = ADAM_B2 * v + (1.0 - ADAM_B2) * (g * g)
    m_hat = m / (1.0 - ADAM_B1 ** ADAM_STEP)
    v_hat = v / (1.0 - ADAM_B2 ** ADAM_STEP)
    return -ADAM_LR * (m_hat / (jnp.sqrt(v_hat) + ADAM_EPS) + ADAM_WD * w), m, v


def _adamw(name, w, g, m, v):
    R, C = w.shape
    tr = _row_block(R, C, 8)

    def body(w_ref, g_ref, m_ref, v_ref, d_ref, nm_ref, nv_ref):
        d_ref[...], nm_ref[...], nv_ref[...] = _adam_math(w_ref[...], g_ref[...], m_ref[...], v_ref[...])

    blk = pl.BlockSpec((tr, C), lambda i: (i, 0))
    return pl.pallas_call(
        body, name=name, grid=(R // tr,), in_specs=[blk] * 4, out_specs=[blk] * 3,
        out_shape=[jax.ShapeDtypeStruct((R, C), F32)] * 3, compiler_params=_params(("parallel",)),
    )(w, g, m, v)


def _adamw_halves(name, w, mine, theirs, m, v, c):
    rs, cs = w.shape[0] * w.shape[1], w.shape[2]
    (br, bc), nb, whole, half = _half_blocks(rs, cs, 8)
    spec, get, put = _shard_blocks(w, br, bc)

    def body(c_ref, w_ref, a_ref, b_ref, m_ref, v_ref, g_ref, d_ref, nm_ref, nv_ref):
        g = jnp.where(pl.program_id(0) == c_ref[0], a_ref[...], b_ref[...])
        put(g_ref, g)
        for ref, val in zip((d_ref, nm_ref, nv_ref), _adam_math(get(w_ref), g, get(m_ref), get(v_ref))):
            put(ref, val)

    full = spec(lambda s, i, c_ref: whole(s, i))
    part = pl.BlockSpec((br, bc), lambda s, i, c_ref: half(i))
    return pl.pallas_call(
        body, name=name,
        grid_spec=pltpu.PrefetchScalarGridSpec(num_scalar_prefetch=1, grid=(2, nb), in_specs=[full, part, part, full, full],
                                               out_specs=[full] * 4),
        out_shape=[jax.ShapeDtypeStruct(w.shape, F32)] * 4, compiler_params=_params(("parallel", "parallel")),
    )(_scalar(c), w, mine, theirs, m, v)


def _pack_small(arrs):
    flat = jnp.concatenate([a.reshape(-1) for a in arrs])
    n = -(-flat.shape[0] // (8 * LANE)) * 8 * LANE
    return jnp.pad(flat, (0, n - flat.shape[0])).reshape(8, n // 8)


def _unpack_small(vec, shapes):
    flat, out, off = vec.reshape(-1), [], 0
    for s in shapes:
        out.append(flat[off:off + s[0] * s[1]].reshape(s))
        off += s[0] * s[1]
    return out


class _LateWeights:
    def __init__(self, cfg, tag, names, staged, after):
        self.cfg, self.tag, self.names, self.k = cfg, tag, names, 3 * len(names)
        self.send, self.recv, self.bufs, self.token = _split_start(f"gather_{tag}_chips_start", staged, self.k, _gather_to_chips,
                                                                    after)

    def pass_on(self, after):
        bufs = _split_wait(f"gather_{self.tag}_chips_wait", self.send, self.recv, self.bufs, after, _gather_to_chips)
        self.send, self.recv, self.bufs, token = _split_start(f"gather_{self.tag}_sibling_start", bufs, self.k, _gather_to_sibling,
                                                               self.token)
        return token

    def arrived(self, after):
        bufs = _split_wait(f"gather_{self.tag}_sibling_wait", self.send, self.recv, self.bufs, after, _gather_to_sibling)
        return {n: _gathered_to_kernel(self.cfg, n, b) for n, b in zip(self.names, bufs)}


def _step(cfg, a):
    chip = 2 * lax.axis_index("x") + lax.axis_index("y")
    core = lax.axis_index("c")
    big = BIG

    ffn = ("w_gate", "w_up", "w_down")
    first = ("w_in", "w_uq", "w_ukv")
    staged = {n: _stage_shard(n, a[n], chip) for n in big}
    gathered = _allgather_weights([staged["w_in"]])
    W = {"w_in": _gathered_to_kernel(cfg, "w_in", gathered[0])}

    sp = {n: a[n] for n in SMALL}
    sharded = _pack_small([a[n] for n in SMALL_SHARDED])
    slot = jnp.where(lax.broadcasted_iota(I32, (N_CHIPS,) + sharded.shape, 0) == chip, 0.5 * sharded[None], 0.0)
    allp = _allreduce_small("allgather_small", slot.reshape(N_CHIPS * 8, -1), gathered[0]).reshape((N_CHIPS,) + sharded.shape)
    per_chip = [_unpack_small(allp[ch], [a[n].shape for n in SMALL_SHARDED]) for ch in range(N_CHIPS)]
    for k, n in enumerate(SMALL_SHARDED):
        sp[n] = jnp.concatenate([per_chip[ch][k] for ch in range(N_CHIPS)], axis=1)

    mla_weights = _LateWeights(cfg, "mla", first[1:], [staged[n] for n in first[1:]], allp)
    out_weight = _LateWeights(cfg, "out", ("w_out",), [staged["w_out"]], mla_weights.token)
    ffn_weights = _LateWeights(cfg, "ffn", ffn, [staged[n] for n in ffn], out_weight.token)
    sp["mix_pre_g"] = sp["mix_pre_g"] + (mla_weights.token[0, 0] + out_weight.token[0, 0] + ffn_weights.token[0, 0])

    state = {}

    def ffn_grads_ready(grads):
        state["ffn_pairs"], token = _pair_exchange_start("ffn", [_grad_to_chips(cfg, n, grads[n]) for n in ffn_grads])
        return token

    def pair_sums(names, grads, theirs):
        return [_pair_sum(n, g, t, core) for n, g, t in zip(names, grads, theirs)]

    def early_grads_ready(grads):
        g_out = [_grad_to_chips(cfg, "w_out", grads["w_out"])]
        g_ffn, t_ffn = _pair_exchange_wait("ffn", state["ffn_pairs"], g_out[0])
        sums = pair_sums(ffn_grads, g_ffn, t_ffn) + pair_sums(["w_out"], g_out, _pair_exchange("out", g_out))
        state["early"] = _chip_exchange_start("early", sums)
        return state["early"][-1]

    def reduced_halves(tag, names, after):
        send_sems, recv_sems, s_bufs, l_bufs, _ = state[tag]
        s_bufs, l_bufs = _chip_exchange_wait(tag, send_sems, recv_sems, s_bufs, l_bufs, after)
        return [_chip_sum(n, s, t, chip) for n, s, t in zip(names, s_bufs, l_bufs)]

    def in_grad_ready(grads):
        grads = [_grad_to_chips(cfg, n, grads[n]) for n in first]
        state["rest"] = _chip_exchange_start("rest", pair_sums(first, grads, _pair_exchange("rest", grads)))
        return state["rest"][-1]

    ffn_grads = ("w_down", "w_gate", "w_up")
    early = ffn_grads + ("w_out",)
    loss, grad_x, gW, gs = _local_grads(cfg, a["x"], a["loss_target"], W, sp, mla_weights, out_weight, ffn_weights,
                                        ffn_grads_ready, early_grads_ready, in_grad_ready)
    out = {"grad_x": grad_x}

    def adamw(names, mine, theirs):
        for n, gm, gt in zip(names, mine, theirs):
            out["grad_" + n], out["delta_" + n], out["new_m_" + n], out["new_v_" + n] = _adamw_halves(
                "adamw_" + n, a[n], gm, gt, a["m_" + n], a["v_" + n], core)

    mine = reduced_halves("early", early, grad_x)
    adamw(early, mine, _sibling_exchange("early", mine))
    mine = reduced_halves("rest", first, out["new_v_" + early[-1]])
    theirs = _sibling_exchange("rest", mine)
    adamw(first, mine, theirs)

    shapes = [gs[n].shape for n in SMALL] + [(1, LANE)]
    red = _unpack_small(_allreduce_small("allreduce_small", _pack_small([gs[n] for n in SMALL] + [loss]), theirs[0]), shapes)
    g_small = dict(zip(SMALL, red[:-1]))
    for n in SMALL_SHARDED:
        cs = a[n].shape[1]
        g_small[n] = lax.dynamic_slice_in_dim(g_small[n], chip * cs, cs, axis=1)
    out["loss"] = red[-1][0, 0]
    sshapes = [a[n].shape for n in SMALL]
    d, nm, nv = _adamw("adamw_small", _pack_small([a[n] for n in SMALL]), _pack_small([g_small[n] for n in SMALL]),
                       _pack_small([a["m_" + n] for n in SMALL]), _pack_small([a["v_" + n] for n in SMALL]))
    for n, dd, mm, vv in zip(SMALL, _unpack_small(d, sshapes), _unpack_small(nm, sshapes), _unpack_small(nv, sshapes)):
        out["grad_" + n], out["delta_" + n], out["new_m_" + n], out["new_v_" + n] = g_small[n], dd, mm, vv
    return out


def kernel(x, mix_pre_g, w_in, q_norm_g, w_uq, kv_norm_g, w_ukv, ssm_conv_w, ssm_conv_b, dt_bias, a_log, d_skip, ssm_norm_g, w_out, mix_post_g, ffn_pre_g, w_gate, w_up, ffn_conv_w, ffn_conv_b, w_down, ffn_post_g, loss_target, m_mix_pre_g, m_w_in, m_q_norm_g, m_w_uq, m_kv_norm_g, m_w_ukv, m_ssm_conv_w, m_ssm_conv_b, m_dt_bias, m_a_log, m_d_skip, m_ssm_norm_g, m_w_out, m_mix_post_g, m_ffn_pre_g, m_w_gate, m_w_up, m_ffn_conv_w, m_ffn_conv_b, m_w_down, m_ffn_post_g, v_mix_pre_g, v_w_in, v_q_norm_g, v_w_uq, v_kv_norm_g, v_w_ukv, v_ssm_conv_w, v_ssm_conv_b, v_dt_bias, v_a_log, v_d_skip, v_ssm_norm_g, v_w_out, v_mix_post_g, v_ffn_pre_g, v_w_gate, v_w_up, v_ffn_conv_w, v_ffn_conv_b, v_w_down, v_ffn_post_g):
    args = dict(locals())
    def given(k, v):
        if k in ("w_in", "m_w_in", "v_w_in"):
            return jnp.transpose(v, (2, 0, 1))
        return v if k.removeprefix("m_").removeprefix("v_") in BIG or v.ndim < 3 else v[0]

    out = _step(_FULL, {k: given(k, v) for k, v in args.items()})
    res = [out["loss"], out["grad_x"][None]]
    for pre in ("grad_", "delta_", "new_m_", "new_v_"):
        for n in WEIGHTS:
            o = out[pre + n]
            res.append(jnp.transpose(o, (1, 2, 0)) if n == "w_in" else o if n in BIG or args[n].ndim < 3 else o[None])
    return tuple(res)
```

```python
import functools
import math

import jax
import jax.numpy as jnp
from jax import lax
from jax.experimental import pallas as pl
from jax.experimental.pallas import tpu as pltpu

F32, BF16, I32 = jnp.float32, jnp.bfloat16, jnp.int32
NN = (((1,), (0,)), ((), ()))
NT = (((1,), (1,)), ((), ()))
TN = (((0,), (0,)), ((), ()))
HI = lax.Precision.HIGHEST
MESH_ID = pl.DeviceIdType.MESH

EPS = 1e-6
CHUNK = 64
NOPE, ROPE, VH = 128, 64, 128
ROPE_THETA = 10000.0
HP, NST = 64, 128
SSM_K, FFN_K = 4, 3
LANE = 128
N_CHIPS = 4
VMEM_LIMIT = 52 * 1024 * 1024
MM_TILE, MM_TILE_K = 1408, 2816

ADAM_LR, ADAM_B1, ADAM_B2, ADAM_EPS, ADAM_WD, ADAM_STEP = 0.001, 0.9, 0.999, 1e-08, 0.01, 10


class _Cfg:
    def __init__(self, S, D, QL, KVL, H, HS, G, DFF, T):
        self.S, self.D, self.QL, self.KVL, self.H, self.HS, self.G, self.DFF, self.T = S, D, QL, KVL, H, HS, G, DFF, T
        self.INNER = HS * HP
        self.CONVCH = self.INNER + 2 * G * NST
        self.QW = H * (NOPE + ROPE)
        self.KVW = H * (NOPE + VH)
        self.MLAW = H * VH
        self.MIXW = self.MLAW + self.INNER
        self.IN_COLS = QL + KVL + ROPE + self.INNER + self.CONVCH + HS
        self.o_kr = QL + KVL
        self.o_z = self.o_kr + LANE
        self.o_xbc = self.o_z + self.INNER
        self.o_dt = self.o_xbc + self.CONVCH
        self.EXT = self.o_dt + LANE
        self.NPAIR = HS // 2
        self.REP = HS // G


_FULL = _Cfg(S=2048, D=2048, QL=768, KVL=512, H=8, HS=16, G=2, DFF=5632, T=256)
BIG = ("w_in", "w_uq", "w_ukv", "w_out", "w_gate", "w_up", "w_down")

SMALL = ("mix_pre_g", "q_norm_g", "kv_norm_g", "ssm_conv_w", "ssm_conv_b", "dt_bias", "a_log", "d_skip", "ssm_norm_g",
         "mix_post_g", "ffn_pre_g", "ffn_conv_w", "ffn_conv_b", "ffn_post_g")
SMALL_SHARDED = ("ssm_conv_w", "ffn_conv_w")
WEIGHTS = ("mix_pre_g", "w_in", "q_norm_g", "w_uq", "kv_norm_g", "w_ukv", "ssm_conv_w", "ssm_conv_b", "dt_bias", "a_log",
           "d_skip", "ssm_norm_g", "w_out", "mix_post_g", "ffn_pre_g", "w_gate", "w_up", "ffn_conv_w", "ffn_conv_b",
           "w_down", "ffn_post_g")


def _pick(n, target, mult):
    best = None
    for d in range(mult, min(n, target) + 1, mult):
        if n % d == 0:
            best = d
    return best if best is not None else n


def _params(sem=None):
    kw = dict(vmem_limit_bytes=VMEM_LIMIT)
    if sem is not None:
        kw["dimension_semantics"] = sem
    return pltpu.CompilerParams(**kw)


def _dot(a, b, dims=NN, precision=None):
    return lax.dot_general(a, b, dims, preferred_element_type=F32, precision=precision)


def _sigmoid(x):
    return 1.0 / (1.0 + jnp.exp(-x))


def _rs(x):
    return lax.rsqrt(jnp.mean(x * x, axis=-1, keepdims=True) + EPS)


def _rms_back(xh, r, dn):
    return r * (dn - xh * jnp.mean(dn * xh, axis=-1, keepdims=True))


def _colsum(v):
    return jnp.sum(v, axis=0, keepdims=True)


def _matmul(name, a, b, mode, out_dtype, a2=None, b2=None, chips=False):
    cs = None
    if mode == "nn":
        (M, K), N = a.shape, b.shape[-1]
        if chips:
            cs, N = N, N_CHIPS * N
    elif mode == "nt":
        (M, K), N = a.shape, b.shape[-2]
        if chips:
            cs = b.shape[-1]
    else:
        (K, M), N = a.shape, b.shape[1]
        if chips:
            cs = N // N_CHIPS
    tm = _pick(M, MM_TILE, LANE)
    tn = _pick(cs if chips and mode != "nt" else N, MM_TILE, LANE)
    tk = _pick(cs, MM_TILE, LANE) if chips and mode == "nt" else _pick(K, MM_TILE_K, LANE)
    nk = K // tk
    dims = {"nn": NN, "nt": NT, "tn": TN}[mode]
    a_spec = pl.BlockSpec((tk, tm), lambda i, j, k: (k, i)) if mode == "tn" else pl.BlockSpec((tm, tk), lambda i, j, k: (i, k))
    b_spec = pl.BlockSpec((tn, tk), lambda i, j, k: (j, k)) if mode == "nt" else pl.BlockSpec((tk, tn), lambda i, j, k: (k, j))
    o_spec = pl.BlockSpec((tm, tn), lambda i, j, k: (i, j))
    o_shape = (M, N)
    if chips and mode == "nn":
        per = cs // tn
        b_spec = pl.BlockSpec((None, tk, tn), lambda i, j, k: (j // per, k, j % per))
    elif chips and mode == "nt":
        per = cs // tk
        b_spec = pl.BlockSpec((None, tn, tk), lambda i, j, k: (k // per, j, k % per))
    elif chips:
        per = cs // tn
        o_spec = pl.BlockSpec((None, tm, tn), lambda i, j, k: (j // per, i, j % per))
        o_shape = (N_CHIPS, M, cs)
    two = a2 is not None

    def product(refs):
        part = _dot(refs[0][...].astype(BF16), refs[1][...].astype(BF16), dims)
        if two:
            part += _dot(refs[2][...].astype(BF16), refs[3][...].astype(BF16), dims)
        return part

    def body_whole_k(*refs):
        refs[-1][...] = product(refs).astype(refs[-1].dtype)

    def body(*refs):
        o_ref, acc_ref = refs[-2], refs[-1]
        k = pl.program_id(2)

        @pl.when(k == 0)
        def _():
            acc_ref[...] = product(refs)

        @pl.when(k > 0)
        def _():
            acc_ref[...] += product(refs)

        @pl.when(k == nk - 1)
        def _():
            o_ref[...] = acc_ref[...].astype(o_ref.dtype)

    ins = (a, b, a2, b2) if two else (a, b)
    return pl.pallas_call(
        body_whole_k if nk == 1 else body, name=name, grid=(M // tm, N // tn, nk),
        in_specs=[a_spec, b_spec] * (2 if two else 1),
        out_specs=o_spec,
        out_shape=jax.ShapeDtypeStruct(o_shape, out_dtype),
        scratch_shapes=[] if nk == 1 else [pltpu.VMEM((tm, tn), F32)],
        compiler_params=_params(("parallel", "parallel", "arbitrary")),
    )(*ins)


def _rowwise(name, fn, rows, mats, outs, reds, ts):
    S = rows[0].shape[0]
    nr, nm, no = len(rows), len(mats), len(outs)

    def body(*refs):
        res = fn(*[r[...] for r in refs[:nr + nm]])
        res = res if isinstance(res, (tuple, list)) else (res,)
        for r, v in zip(refs[nr + nm:nr + nm + no], res[:no]):
            r[...] = v.astype(r.dtype)
        first = pl.program_id(0) == 0
        for r, v in zip(refs[nr + nm + no:], res[no:]):
            @pl.when(first)
            def _():
                r[...] = jnp.broadcast_to(v, r.shape)

            @pl.when(jnp.logical_not(first))
            def _():
                r[...] += jnp.broadcast_to(v, r.shape)

    in_specs = [pl.BlockSpec((ts, a.shape[1]), lambda i: (i, 0)) for a in rows]
    in_specs += [pl.BlockSpec(m.shape, lambda i, nd=m.ndim: (0,) * nd) for m in mats]
    out_specs = [pl.BlockSpec((ts, w), lambda i: (i, 0)) for w, _ in outs]
    out_specs += [pl.BlockSpec(s, lambda i: (0, 0)) for s in reds]
    out_shape = [jax.ShapeDtypeStruct((S, w), dt) for w, dt in outs] + [jax.ShapeDtypeStruct(s, F32) for s in reds]
    return pl.pallas_call(
        body, name=name, grid=(S // ts,), in_specs=in_specs, out_specs=out_specs, out_shape=out_shape,
        compiler_params=_params(("arbitrary",) if reds else ("parallel",)),
    )(*rows, *mats)


def _shift_down(v, s):
    if s == 0:
        return v
    rows = lax.broadcasted_iota(I32, v.shape, 0)
    return jnp.where(rows >= s, pltpu.roll(v, s, 0), 0.0)


def _shift_up(v, s):
    if s == 0:
        return v
    n = v.shape[0]
    rows = lax.broadcasted_iota(I32, v.shape, 0)
    return jnp.where(rows < n - s, pltpu.roll(v, n - s, 0), 0.0)


def _conv(x, w, b):
    K = w.shape[0]
    y = jnp.broadcast_to(b, x.shape)
    for k in range(K):
        y = y + w[k:k + 1, :] * _shift_down(x, K - 1 - k)
    return y


def _conv_back(x, w, dc):
    K = w.shape[0]
    dx = jnp.zeros_like(x)
    dw = []
    for k in range(K):
        dx = dx + w[k:k + 1, :] * _shift_up(dc, K - 1 - k)
        dw.append(_colsum(dc * _shift_down(x, K - 1 - k)))
    return dx, jnp.concatenate(dw, axis=0), _colsum(dc)


def _colwise(name, fn, cols, vecs, outs, pouts, tc):
    S, C = cols[0].shape
    nc_, nv, no = len(cols), len(vecs), len(outs)

    def body(*refs):
        res = fn(*[r[...] for r in refs[:nc_ + nv]])
        res = res if isinstance(res, (tuple, list)) else (res,)
        for r, v in zip(refs[nc_ + nv:], res):
            r[...] = v.astype(r.dtype)

    in_specs = [pl.BlockSpec((S, tc), lambda j: (0, j)) for _ in cols]
    in_specs += [pl.BlockSpec((v.shape[0], tc), lambda j: (0, j)) for v in vecs]
    out_specs = [pl.BlockSpec((S, tc), lambda j: (0, j)) for _ in outs] + [pl.BlockSpec((k, tc), lambda j: (0, j)) for k in pouts]
    out_shape = [jax.ShapeDtypeStruct((S, C), dt) for dt in outs] + [jax.ShapeDtypeStruct((k, C), F32) for k in pouts]
    return pl.pallas_call(
        body, name=name, grid=(C // tc,), in_specs=in_specs, out_specs=out_specs, out_shape=out_shape,
        compiler_params=_params(("parallel",)),
    )(*cols, *vecs)


_G0, _G1 = math.sqrt(2.0 / math.pi), 0.044715


def _gelu(g):
    th = jnp.tanh(_G0 * (g + _G1 * g * g * g))
    return 0.5 * g * (1.0 + th), th


def _ffn_act(gate_pre, up, w, b):
    act, _ = _gelu(_conv(gate_pre, w, b))
    return act * up


def _ffn_act_back(dact, gate_pre, up, w, b):
    g = _conv(gate_pre, w, b)
    ge, th = _gelu(g)
    dge = 0.5 * (1.0 + th) + 0.5 * g * (1.0 - th * th) * _G0 * (1.0 + 3.0 * _G1 * g * g)
    dup = dact * ge
    dgate_pre, dw, db = _conv_back(gate_pre, w, dact * up * dge)
    return dgate_pre, dup, dw, db


def _ssm_act(xbc, w, b):
    c = _conv(xbc, w, b)
    return c * _sigmoid(c)


def _ssm_act_back(dxc, xbc, w, b):
    c = _conv(xbc, w, b)
    sg = _sigmoid(c)
    return _conv_back(xbc, w, dxc * sg * (1.0 + c * (1.0 - sg)))


def _rope_tables(S):
    inv = 1.0 / (ROPE_THETA ** (jnp.arange(0, ROPE, 2, dtype=F32) / ROPE))
    ang = jnp.arange(S, dtype=F32)[:, None] * inv[None, :]
    cos, sin = jnp.cos(ang), jnp.sin(ang)
    return jnp.tile(cos, (1, 4)), jnp.tile(jnp.concatenate([-sin, sin], axis=1), (1, 2))


def _swap_halves(x):
    lane = lax.broadcasted_iota(I32, x.shape, 1)
    w = x.shape[1]
    return jnp.where((lane % ROPE) < ROPE // 2, pltpu.roll(x, w - ROPE // 2, 1), pltpu.roll(x, ROPE // 2, 1))


def _rot(x, cos2, sin2):
    return x * cos2 + _swap_halves(x) * sin2


def _rot_back(dy, cos2, sin2):
    return dy * cos2 + _swap_halves(dy * sin2)


def _mla_pack(cfg, q, kv, kr, cos2, sin2):
    S, H = cfg.S, cfg.H
    ts = _pick(S, 512, 8)

    def body(qn_ref, qr_ref, kn_ref, v_ref, kr_ref, c_ref, s_ref, Q_ref, K_ref, V_ref):
        h = pl.program_id(0)
        c2, s2 = c_ref[...], s_ref[...]
        Q_ref[0, :, 0:LANE] = qn_ref[...].astype(BF16)
        Q_ref[0, :, LANE:] = _rot(qr_ref[...], c2, s2).astype(BF16)
        K_ref[0, :, 0:LANE] = kn_ref[...].astype(BF16)
        krr = _rot(kr_ref[...], c2, s2)
        K_ref[0, :, LANE:] = jnp.where(h % 2 == 1, pltpu.roll(krr, ROPE, 1), krr).astype(BF16)
        V_ref[0] = v_ref[...].astype(BF16)

    blk = lambda f: pl.BlockSpec((ts, LANE), f)
    return pl.pallas_call(
        body, name="mla_pack", grid=(H, S // ts),
        in_specs=[blk(lambda h, i: (i, h)), blk(lambda h, i: (i, H + h // 2)), blk(lambda h, i: (i, h)),
                  blk(lambda h, i: (i, H + h)), blk(lambda h, i: (i, 0)), blk(lambda h, i: (i, 0)), blk(lambda h, i: (i, 0))],
        out_specs=[pl.BlockSpec((1, ts, 2 * LANE), lambda h, i: (h, i, 0)), pl.BlockSpec((1, ts, 2 * LANE), lambda h, i: (h, i, 0)),
                   pl.BlockSpec((1, ts, LANE), lambda h, i: (h, i, 0))],
        out_shape=[jax.ShapeDtypeStruct((H, S, 2 * LANE), BF16), jax.ShapeDtypeStruct((H, S, 2 * LANE), BF16),
                   jax.ShapeDtypeStruct((H, S, LANE), BF16)],
        compiler_params=_params(("parallel", "parallel")),
    )(q, q, kv, kv, kr, cos2, sin2)


def _mla_unpack(cfg, dQ, dK, dV, cos2, sin2):
    S, H = cfg.S, cfg.H
    ts = _pick(S, 256, 8)

    def body(dQ_ref, dK_ref, dV_ref, c_ref, s_ref, dq_ref, dkv_ref, dkr_ref):
        c2, s2 = c_ref[...], s_ref[...]
        lo = lax.broadcasted_iota(I32, (ts, LANE), 1) < ROPE
        tk = jnp.zeros((ts, LANE), F32)
        for h in range(H):
            dq_ref[:, h * LANE:(h + 1) * LANE] = dQ_ref[h, :, 0:LANE].astype(BF16)
            dkv_ref[:, h * LANE:(h + 1) * LANE] = dK_ref[h, :, 0:LANE].astype(BF16)
            dkv_ref[:, (H + h) * LANE:(H + h + 1) * LANE] = dV_ref[h].astype(BF16)
            own = lo if h % 2 == 0 else jnp.logical_not(lo)
            tk = tk + jnp.where(own, dK_ref[h, :, LANE:], 0.0)
        for j in range(H // 2):
            dr = dQ_ref[2 * j, :, LANE:] + dQ_ref[2 * j + 1, :, LANE:]
            dq_ref[:, (H + j) * LANE:(H + j + 1) * LANE] = _rot_back(dr, c2, s2).astype(BF16)
        dkr_rot = jnp.where(lo, tk + pltpu.roll(tk, ROPE, 1), 0.0)
        dkr_ref[...] = _rot_back(dkr_rot, c2, s2).astype(BF16)

    tab = pl.BlockSpec((ts, LANE), lambda i: (i, 0))
    return pl.pallas_call(
        body, name="mla_unpack", grid=(S // ts,),
        in_specs=[pl.BlockSpec((H, ts, 2 * LANE), lambda i: (0, i, 0)), pl.BlockSpec((H, ts, 2 * LANE), lambda i: (0, i, 0)),
                  pl.BlockSpec((H, ts, LANE), lambda i: (0, i, 0)), tab, tab],
        out_specs=[pl.BlockSpec((ts, cfg.QW), lambda i: (i, 0)), pl.BlockSpec((ts, cfg.KVW), lambda i: (i, 0)), tab],
        out_shape=[jax.ShapeDtypeStruct((S, cfg.QW), BF16), jax.ShapeDtypeStruct((S, cfg.KVW), BF16),
                   jax.ShapeDtypeStruct((S, LANE), BF16)],
        compiler_params=_params(("parallel",)),
    )(dQ, dK, dV, cos2, sin2)


_ATT_T = 256
_ATT_HB = 2
_ATT_SCALE = (NOPE + ROPE) ** -0.5


def _diag_mask(transposed=False):
    r = lax.broadcasted_iota(I32, (_ATT_T, _ATT_T), 0) // CHUNK
    c = lax.broadcasted_iota(I32, (_ATT_T, _ATT_T), 1) // CHUNK
    return r <= c if transposed else c <= r


def _row_form(col):
    return jnp.broadcast_to(col, (col.shape[0], LANE)).T[0:8, :]


def _attn_fwd(cfg, Q, K, V):
    S, H, T, HB = cfg.S, cfg.H, _ATT_T, _ATT_HB

    def body(q_ref, k_ref, v_ref, o_ref, lse_ref, lse_t_ref):
        qi = pl.program_id(1)

        def head_step(b, kb, carry, mask):
            m, l, acc = carry
            ks = pl.multiple_of(kb * T, T)
            s = _dot(q_ref[b], k_ref[b, pl.ds(ks, T), :], NT) * _ATT_SCALE
            if mask is not None:
                s = jnp.where(mask, s, -1e30)
            m_new = jnp.maximum(m, jnp.max(s, axis=1, keepdims=True))
            p = jnp.exp(s - m_new)
            alpha = jnp.exp(m - m_new)
            l = alpha * l + jnp.sum(p, axis=1, keepdims=True)
            acc = alpha * acc + _dot(p.astype(BF16), v_ref[b, pl.ds(ks, T), :])
            return m_new, l, acc

        def step(kb, carry, mask=None):
            return tuple(head_step(b, kb, carry[b], mask) for b in range(HB))

        init = (jnp.full((T, 1), -1e30, F32), jnp.zeros((T, 1), F32), jnp.zeros((T, VH), F32))
        done = step(qi, lax.fori_loop(0, qi, step, (init,) * HB), _diag_mask())
        for b, (m, l, acc) in enumerate(done):
            o_ref[:, b * LANE:(b + 1) * LANE] = acc / l
            lse = m + jnp.log(l)
            lse_ref[:, b * LANE:(b + 1) * LANE] = jnp.broadcast_to(lse, (T, LANE))
            lse_t_ref[b] = _row_form(lse)

    return pl.pallas_call(
        body, name="attn_fwd", grid=(H // HB, S // T),
        in_specs=[pl.BlockSpec((HB, T, 2 * LANE), lambda h, i: (h, i, 0)), pl.BlockSpec((HB, S, 2 * LANE), lambda h, i: (h, 0, 0)),
                  pl.BlockSpec((HB, S, LANE), lambda h, i: (h, 0, 0))],
        out_specs=[pl.BlockSpec((T, HB * LANE), lambda h, i: (i, h)), pl.BlockSpec((T, HB * LANE), lambda h, i: (i, h)),
                   pl.BlockSpec((HB, 8, T), lambda h, i: (h, 0, i))],
        out_shape=[jax.ShapeDtypeStruct((S, H * LANE), F32), jax.ShapeDtypeStruct((S, H * LANE), F32),
                   jax.ShapeDtypeStruct((H, 8, S), F32)],
        compiler_params=_params(("parallel", "parallel")),
    )(Q, K, V)


def _attn_dq(cfg, Q, K, V, do, o, lse, after):
    S, H, T, HB = cfg.S, cfg.H, _ATT_T, _ATT_HB

    def body(q_ref, k_ref, v_ref, do_ref, o_ref, lse_ref, after_ref, dq_ref, dl_t_ref):
        qi = pl.program_id(1)
        do = [do_ref[:, b * LANE:(b + 1) * LANE] for b in range(HB)]
        delta = [jnp.sum(do[b] * o_ref[:, b * LANE:(b + 1) * LANE], axis=1, keepdims=True) for b in range(HB)]
        dob = [d.astype(BF16) for d in do]

        def head_step(b, kb, dq, mask):
            ks = pl.multiple_of(kb * T, T)
            k = k_ref[b, pl.ds(ks, T), :]
            s = _dot(q_ref[b], k, NT) * _ATT_SCALE
            if mask is not None:
                s = jnp.where(mask, s, -1e30)
            p = jnp.exp(s - lse_ref[:, b * LANE:b * LANE + 1])
            dp = _dot(dob[b], v_ref[b, pl.ds(ks, T), :], NT)
            ds = p * (dp - delta[b]) * _ATT_SCALE
            return dq + _dot(ds.astype(BF16), k)

        def step(kb, dqs, mask=None):
            return tuple(head_step(b, kb, dqs[b], mask) for b in range(HB))

        dqs = step(qi, lax.fori_loop(0, qi, step, (jnp.zeros((T, 2 * LANE), F32),) * HB), _diag_mask())
        for b in range(HB):
            dq_ref[b] = dqs[b]
            dl_t_ref[b] = _row_form(delta[b])

    col = pl.BlockSpec((T, HB * LANE), lambda h, i: (i, h))
    return pl.pallas_call(
        body, name="attn_dq", grid=(H // HB, S // T),
        in_specs=[pl.BlockSpec((HB, T, 2 * LANE), lambda h, i: (h, i, 0)), pl.BlockSpec((HB, S, 2 * LANE), lambda h, i: (h, 0, 0)),
                  pl.BlockSpec((HB, S, LANE), lambda h, i: (h, 0, 0)), col, col, col, _ANY],
        out_specs=[pl.BlockSpec((HB, T, 2 * LANE), lambda h, i: (h, i, 0)), pl.BlockSpec((HB, 8, T), lambda h, i: (h, 0, i))],
        out_shape=[jax.ShapeDtypeStruct((H, S, 2 * LANE), F32), jax.ShapeDtypeStruct((H, 8, S), F32)],
        compiler_params=_params(("parallel", "parallel")),
    )(Q, K, V, do, o, lse, after)


def _attn_dkv(cfg, Q, K, V, do, lse_t, delta_t):
    S, H, T, HB = cfg.S, cfg.H, _ATT_T, _ATT_HB
    nq = S // T

    def body(q_ref, k_ref, v_ref, do_ref, lse_ref, dl_ref, dk_ref, dv_ref):
        kb = pl.program_id(1)

        def head_step(b, qi, carry, mask):
            dk, dv = carry
            qs = pl.multiple_of(qi * T, T)
            q = q_ref[b, pl.ds(qs, T), :]
            dob = do_ref[pl.ds(qs, T), b * LANE:(b + 1) * LANE].astype(BF16)
            s = _dot(k_ref[b], q, NT) * _ATT_SCALE
            if mask is not None:
                s = jnp.where(mask, s, -1e30)
            p = jnp.exp(s - lse_ref[b, 0:1, pl.ds(qs, T)])
            dv = dv + _dot(p.astype(BF16), dob)
            dp = _dot(v_ref[b], dob, NT)
            ds = p * (dp - dl_ref[b, 0:1, pl.ds(qs, T)]) * _ATT_SCALE
            dk = dk + _dot(ds.astype(BF16), q)
            return dk, dv

        def step(qi, carry, mask=None):
            return tuple(head_step(b, qi, carry[b], mask) for b in range(HB))

        zero = (jnp.zeros((T, 2 * LANE), F32), jnp.zeros((T, VH), F32))
        done = lax.fori_loop(kb + 1, nq, step, step(kb, (zero,) * HB, _diag_mask(transposed=True)))
        for b, (dk, dv) in enumerate(done):
            dk_ref[b] = dk
            dv_ref[b] = dv

    row = pl.BlockSpec((HB, 8, S), lambda h, j: (h, 0, 0))
    return pl.pallas_call(
        body, name="attn_dkv", grid=(H // HB, S // T),
        in_specs=[pl.BlockSpec((HB, S, 2 * LANE), lambda h, j: (h, 0, 0)), pl.BlockSpec((HB, T, 2 * LANE), lambda h, j: (h, j, 0)),
                  pl.BlockSpec((HB, T, LANE), lambda h, j: (h, j, 0)), pl.BlockSpec((S, HB * LANE), lambda h, j: (0, h)), row, row],
        out_specs=[pl.BlockSpec((HB, T, 2 * LANE), lambda h, j: (h, j, 0)), pl.BlockSpec((HB, T, LANE), lambda h, j: (h, j, 0))],
        out_shape=[jax.ShapeDtypeStruct((H, S, 2 * LANE), F32), jax.ShapeDtypeStruct((H, S, LANE), F32)],
        compiler_params=_params(("parallel", "parallel")),
    )(Q, K, V, do, lse_t, delta_t)


def _expand_matrix(cfg):
    r = lax.broadcasted_iota(I32, (LANE, cfg.INNER), 0)
    c = lax.broadcasted_iota(I32, (LANE, cfg.INNER), 1)
    return (r == c // HP).astype(F32)


def _softplus(x):
    return jnp.maximum(x, 0.0) + jnp.log(1.0 + jnp.exp(-jnp.abs(x)))


def _ssd_prep(cfg, dt_raw, dt_bias_pad, a_log_pad, expand):
    HS = cfg.HS

    def fn(raw, bias, alog, E):
        heads = lax.broadcasted_iota(I32, raw.shape, 1) < HS
        dt = jnp.where(heads, _softplus(raw + bias), 0.0)
        a = dt * jnp.where(heads[0:1], -jnp.exp(alog), 0.0)
        return dt, a, _dot(dt, E, precision=HI), _dot(a, E, precision=HI)

    return _rowwise("ssd_prep", fn, [dt_raw], [dt_bias_pad, a_log_pad, expand],
                    [(LANE, F32), (LANE, F32), (cfg.INNER, F32), (cfg.INNER, F32)], [], _pick(cfg.S, 512, 8))


def _tril(T):
    return lax.broadcasted_iota(I32, (T, T), 0) >= lax.broadcasted_iota(I32, (T, T), 1)


def _ssd_fwd(cfg, xc, dt_exp, a_exp, a_small, dskip_exp):
    S, T, INNER, G, NPAIR = cfg.S, cfg.T, cfg.INNER, cfg.G, cfg.NPAIR
    NC = S // T

    def body(xc_ref, dte_ref, ae_ref, as_ref, dsk_ref, y_ref, hin_ref, ht_ref):
        @pl.when(pl.program_id(0) == 0)
        def _():
            ht_ref[...] = jnp.zeros_like(ht_ref)

        tril = _tril(T)
        tri = tril.astype(F32)
        acs_s = _dot(tri, as_ref[...], precision=HI)
        acs_e = _dot(tri, ae_ref[...], precision=HI)
        acs_t = acs_s.T
        lo = lax.broadcasted_iota(I32, (T, LANE), 1) < HP
        for g in range(G):
            Bb = xc_ref[:, INNER + g * NST:INNER + (g + 1) * NST].astype(BF16)
            Cb = xc_ref[:, INNER + (G + g) * NST:INNER + (G + g + 1) * NST].astype(BF16)
            Gm = _dot(Cb, Bb, NT)
            for j in range(g * NPAIR // G, (g + 1) * NPAIR // G):
                sl = slice(j * LANE, (j + 1) * LANE)
                Xp = xc_ref[:, sl]
                Xdt = Xp * dte_ref[:, sl]
                Xb = Xdt.astype(BF16)
                acs_p = acs_e[:, sl]
                last = acs_p[T - 1:T, :]
                Hin = ht_ref[j]
                hin_ref[0, j] = Hin
                yd = []
                for e in (0, 1):
                    h = 2 * j + e
                    Lm = jnp.exp(jnp.where(tril, acs_s[:, h:h + 1] - acs_t[h:h + 1, :], -1e30))
                    yd.append(_dot((Gm * Lm).astype(BF16), Xb))
                y_off = _dot(Cb, Hin.astype(BF16)) * jnp.exp(acs_p)
                y_ref[:, sl] = jnp.where(lo, yd[0], yd[1]) + y_off + Xp * dsk_ref[:, sl]
                st = _dot(Bb, (Xdt * jnp.exp(last - acs_p)).astype(BF16), TN)
                ht_ref[j] = jnp.exp(last) * Hin + st

    rows = lambda w: pl.BlockSpec((T, w), lambda c: (c, 0))
    return pl.pallas_call(
        body, name="ssd_fwd", grid=(NC,),
        in_specs=[rows(cfg.CONVCH), rows(INNER), rows(INNER), rows(LANE), pl.BlockSpec((1, INNER), lambda c: (0, 0))],
        out_specs=[rows(INNER), pl.BlockSpec((1, NPAIR, NST, LANE), lambda c: (c, 0, 0, 0))],
        out_shape=[jax.ShapeDtypeStruct((S, INNER), F32), jax.ShapeDtypeStruct((NC, NPAIR, NST, LANE), F32)],
        scratch_shapes=[pltpu.VMEM((NPAIR, NST, LANE), F32)],
        compiler_params=_params(("arbitrary",)),
    )(xc, dt_exp, a_exp, a_small, dskip_exp)


def _ssd_bwd(cfg, dy, xc, dt_exp, a_exp, a_small, dskip_exp, hin, dt_raw, dt_bias_pad, a_log_pad, expand):
    S, T, INNER, G, NPAIR, HS = cfg.S, cfg.T, cfg.INNER, cfg.G, cfg.NPAIR, cfg.HS
    NC = S // T

    def body(dy_ref, xc_ref, dte_ref, ae_ref, as_ref, dsk_ref, hin_ref, raw_ref, bias_ref, alog_ref, e_ref,
             dxc_ref, draw_ref, dbias_ref, dalog_ref, dskip_ref, dht_ref, cols_ref, rows_ref, dacs_ref, ddt_ref):
        first = pl.program_id(0) == 0

        @pl.when(first)
        def _():
            dht_ref[...] = jnp.zeros_like(dht_ref)

        tril = _tril(T)
        tri = tril.astype(F32)
        a_s = as_ref[...]
        acs_s = _dot(tri, a_s, precision=HI)
        acs_e = _dot(tri, ae_ref[...], precision=HI)
        acs_t = acs_s.T
        lo = lax.broadcasted_iota(I32, (T, LANE), 1) < HP
        last_row = lax.broadcasted_iota(I32, (T, LANE), 0) == T - 1
        cols_ref[...] = jnp.zeros_like(cols_ref)
        rows_ref[...] = jnp.zeros_like(rows_ref)
        dsk_parts = []
        for g in range(G):
            bsl = slice(INNER + g * NST, INNER + (g + 1) * NST)
            csl = slice(INNER + (G + g) * NST, INNER + (G + g + 1) * NST)
            Bb = xc_ref[:, bsl].astype(BF16)
            Cb = xc_ref[:, csl].astype(BF16)
            Gm = _dot(Cb, Bb, NT)
            dG = jnp.zeros((T, T), F32)
            dB = jnp.zeros((T, NST), F32)
            dC = jnp.zeros((T, NST), F32)
            for j in range(g * NPAIR // G, (g + 1) * NPAIR // G):
                sl = slice(j * LANE, (j + 1) * LANE)
                Xp = xc_ref[:, sl]
                dtp = dte_ref[:, sl]
                Xdt = Xp * dtp
                Xb = Xdt.astype(BF16)
                acs_p = acs_e[:, sl]
                last = acs_p[T - 1:T, :]
                e_p, dec, cd = jnp.exp(acs_p), jnp.exp(last - acs_p), jnp.exp(last)
                Hin = hin_ref[0, j]
                Hb = Hin.astype(BF16)
                dHn = dht_ref[j]
                dHb = dHn.astype(BF16)
                dYp = dy_ref[:, sl]
                z = _dot(Cb, Hb)
                dz = (dYp * e_p).astype(BF16)
                dacs_p = dYp * z * e_p
                dC = dC + _dot(dz, Hb, NT)
                dHin = _dot(Cb, dz, TN) + cd * dHn
                dlast = _colsum(dHn * Hin) * cd
                qv = _dot(Bb, dHb)
                dXdt = qv * dec
                ddec = qv * Xdt * dec
                dacs_p = dacs_p - ddec
                dlast = dlast + _colsum(ddec)
                dB = dB + _dot((Xdt * dec).astype(BF16), dHb, NT)
                for e in (0, 1):
                    h = 2 * j + e
                    Lm = jnp.exp(jnp.where(tril, acs_s[:, h:h + 1] - acs_t[h:h + 1, :], -1e30))
                    Mh = Gm * Lm
                    dYe = jnp.where(lo if e == 0 else jnp.logical_not(lo), dYp, 0.0).astype(BF16)
                    dM = _dot(dYe, Xb, NT)
                    dXdt = dXdt + _dot(Mh.astype(BF16), dYe, TN)
                    W = dM * Mh
                    cols_ref[:, h:h + 1] = jnp.sum(W, axis=1, keepdims=True)
                    rows_ref[h:h + 1, :] = _colsum(W)
                    dG = dG + dM * Lm
                dacs_ref[:, sl] = dacs_p + jnp.where(last_row, dlast, 0.0)
                ddt_ref[:, sl] = dXdt * Xp
                dxc_ref[:, sl] = dXdt * dtp + dYp * dsk_ref[:, sl]
                dsk_parts.append(_colsum(dYp * Xp))
                dht_ref[j] = dHin
            dGb = dG.astype(BF16)
            dxc_ref[:, bsl] = dB + _dot(dGb, Cb, TN)
            dxc_ref[:, csl] = dC + _dot(dGb, Bb)
        E = e_ref[...]
        dacs_s = cols_ref[...] - rows_ref[...].T + _dot(dacs_ref[...], E, NT, precision=HI)
        da = _dot(tri, dacs_s, TN, precision=HI)
        heads = lax.broadcasted_iota(I32, (1, LANE), 1) < HS
        A = jnp.where(heads, -jnp.exp(alog_ref[...]), 0.0)
        ddt = _dot(ddt_ref[...], E, NT, precision=HI) + da * A
        draw = jnp.where(heads, ddt * _sigmoid(raw_ref[...] + bias_ref[...]), 0.0)
        draw_ref[...] = draw
        dsk = _dot(jnp.broadcast_to(jnp.concatenate(dsk_parts, axis=1), (8, INNER)), E, NT, precision=HI)[0:1]
        for ref, val in ((dbias_ref, _colsum(draw)), (dalog_ref, _colsum(da * a_s)), (dskip_ref, dsk)):
            @pl.when(first)
            def _():
                ref[...] = val

            @pl.when(jnp.logical_not(first))
            def _():
                ref[...] += val

    rows = lambda w: pl.BlockSpec((T, w), lambda c: (NC - 1 - c, 0))
    vec = lambda w: pl.BlockSpec((1, w), lambda c: (0, 0))
    return pl.pallas_call(
        body, name="ssd_bwd", grid=(NC,),
        in_specs=[rows(INNER), rows(cfg.CONVCH), rows(INNER), rows(INNER), rows(LANE), vec(INNER),
                  pl.BlockSpec((1, NPAIR, NST, LANE), lambda c: (NC - 1 - c, 0, 0, 0)), rows(LANE), vec(LANE), vec(LANE),
                  pl.BlockSpec((LANE, INNER), lambda c: (0, 0))],
        out_specs=[rows(cfg.CONVCH), rows(LANE), vec(LANE), vec(LANE), vec(LANE)],
        out_shape=[jax.ShapeDtypeStruct((S, cfg.CONVCH), F32), jax.ShapeDtypeStruct((S, LANE), F32)]
        + [jax.ShapeDtypeStruct((1, LANE), F32)] * 3,
        scratch_shapes=[pltpu.VMEM((NPAIR, NST, LANE), F32), pltpu.VMEM((T, LANE), F32), pltpu.VMEM((LANE, T), F32),
                        pltpu.VMEM((T, INNER), F32), pltpu.VMEM((T, INNER), F32)],
        compiler_params=_params(("arbitrary",)),
    )(dy, xc, dt_exp, a_exp, a_small, dskip_exp, hin, dt_raw, dt_bias_pad, a_log_pad, expand)


def _ssd_post(cfg, y, z, norm_g):
    W = cfg.INNER // cfg.G

    def fn(y, z, g):
        yz = y * z * _sigmoid(z)
        return jnp.concatenate([yz[:, i * W:(i + 1) * W] * _rs(yz[:, i * W:(i + 1) * W]) for i in range(cfg.G)], axis=1) * g

    return _rowwise("ssd_post", fn, [y, z], [norm_g], [(cfg.INNER, BF16)], [], _pick(cfg.S, 256, 8))[0]


def _ssd_post_bwd(cfg, db, y, z, norm_g):
    W = cfg.INNER // cfg.G

    def fn(db, y, z, g):
        sg = _sigmoid(z)
        yz = y * z * sg
        dn = db * g
        dyz, nh = [], []
        for i in range(cfg.G):
            seg = yz[:, i * W:(i + 1) * W]
            r = _rs(seg)
            nh.append(seg * r)
            dyz.append(_rms_back(nh[-1], r, dn[:, i * W:(i + 1) * W]))
        dyz = jnp.concatenate(dyz, axis=1)
        return dyz * z * sg, dyz * y * sg * (1.0 + z * (1.0 - sg)), _colsum(db * jnp.concatenate(nh, axis=1))

    return _rowwise("ssd_post_bwd", fn, [db, y, z], [norm_g], [(cfg.INNER, F32), (cfg.INNER, F32)], [(1, cfg.INNER)],
                    _pick(cfg.S, 256, 8))


def _local_grads(cfg, x, tgt, W, sp, mla_weights=None, out_weight=None, ffn_weights=None, ffn_grads_ready=None,
                 early_grads_ready=None, in_grad_ready=None):
    S, D, H, INNER = cfg.S, cfg.D, cfg.H, cfg.INNER
    ts = _pick(S, 256, 8)
    tc = 256

    xn = _rowwise("rms_pre", lambda x, g: x * _rs(x) * g, [x], [sp["mix_pre_g"]], [(D, BF16)], [], ts)[0]
    u = _matmul("mm_in", xn, W["w_in"], "nt", F32)
    c_q, c_kv = u[:, :cfg.QL], u[:, cfg.QL:cfg.o_kr]
    kr = u[:, cfg.o_kr:cfg.o_z]
    z = u[:, cfg.o_z:cfg.o_xbc]
    xbc = u[:, cfg.o_xbc:cfg.o_dt]
    dt_raw = u[:, cfg.o_dt:]

    if mla_weights is not None:
        sp = dict(sp, q_norm_g=sp["q_norm_g"] + mla_weights.pass_on(u)[0, 0])
    cqn = _rowwise("rms_q", lambda x, g: x * _rs(x) * g, [c_q], [sp["q_norm_g"]], [(cfg.QL, BF16)], [], ts)[0]
    ckvn = _rowwise("rms_kv", lambda x, g: x * _rs(x) * g, [c_kv], [sp["kv_norm_g"]], [(cfg.KVL, BF16)], [], ts)[0]
    if mla_weights is not None:
        W = dict(W, **mla_weights.arrived(ckvn))
    q = _matmul("mm_uq", cqn, W["w_uq"], "nn", F32)
    kv = _matmul("mm_ukv", ckvn, W["w_ukv"], "nn", F32)
    cos2, sin2 = _rope_tables(S)
    Qh, Kh, Vh = _mla_pack(cfg, q, kv, kr, cos2, sin2)
    a_out, lse, lse_t = _attn_fwd(cfg, Qh, Kh, Vh)
    if out_weight is not None:
        sp = dict(sp, ssm_conv_b=sp["ssm_conv_b"] + out_weight.pass_on(a_out)[0, 0])

    pad = lambda v: jnp.pad(v, ((0, 0), (0, LANE - v.shape[1])))
    expand = _expand_matrix(cfg)
    dt_bias_pad, a_log_pad = pad(sp["dt_bias"]), pad(sp["a_log"])
    dskip_exp = jnp.repeat(sp["d_skip"], HP, axis=1)
    xc = _colwise("ssm_act", _ssm_act, [xbc], [sp["ssm_conv_w"], sp["ssm_conv_b"]], [F32], [], tc)[0]
    dt_s, a_s, dt_exp, a_exp = _ssd_prep(cfg, dt_raw, dt_bias_pad, a_log_pad, expand)
    y_ssd, hin = _ssd_fwd(cfg, xc, dt_exp, a_exp, a_s, dskip_exp)
    b_out = _ssd_post(cfg, y_ssd, z, sp["ssm_norm_g"])

    ab_out = jnp.concatenate([a_out.astype(BF16), b_out], axis=1)
    if out_weight is not None:
        W = dict(W, **out_weight.arrived(ab_out))
    if ffn_weights is not None:
        sp = dict(sp, mix_post_g=sp["mix_post_g"] + ffn_weights.pass_on(ab_out)[0, 0])
    mix = _matmul("mm_out", ab_out, W["w_out"], "nn", F32)

    def mid(x, mix, g_mp, g_fp):
        x1 = x + mix * _rs(mix) * g_mp
        return x1, x1 * _rs(x1) * g_fp

    x1, h2 = _rowwise("fwd_mid", mid, [x, mix], [sp["mix_post_g"], sp["ffn_pre_g"]], [(D, F32), (D, BF16)], [], ts)
    if ffn_weights is not None:
        W = dict(W, **ffn_weights.arrived(h2))
    gate_pre = _matmul("mm_gate", h2, W["w_gate"], "nn", F32, chips=True)
    up = _matmul("mm_up", h2, W["w_up"], "nn", F32, chips=True)
    act = _colwise("ffn_act", _ffn_act, [gate_pre, up], [sp["ffn_conv_w"], sp["ffn_conv_b"]], [BF16], [], tc)[0]
    f = _matmul("mm_down", act, W["w_down"], "nn", F32)

    def final(x1, f, t, g):
        r = _rs(f)
        fh = f * r
        err = x1 + fh * g - t
        loss = 0.5 * jnp.sum(jnp.mean(err * err, axis=-1, keepdims=True), axis=0, keepdims=True)
        dy = err * (1.0 / D)
        return dy, _rms_back(fh, r, dy * g), _colsum(dy * fh), loss

    dy, df, g_ffn_post, loss = _rowwise("final", final, [x1, f, tgt], [sp["ffn_post_g"]], [(D, F32), (D, BF16)],
                                        [(1, D), (1, LANE)], ts)
    gW = {}
    dact = _matmul("mm_down_dx", df, W["w_down"], "nt", F32)
    gW["w_down"] = _matmul("mm_down_dw", act, df, "tn", BF16)
    dgate, dup, g_ffn_conv_w, g_ffn_conv_b = _colwise(
        "ffn_act_bwd", _ffn_act_back, [dact, gate_pre, up], [sp["ffn_conv_w"], sp["ffn_conv_b"]], [BF16, BF16], [FFN_K, 1], tc)
    gW["w_gate"] = _matmul("mm_gate_dw", h2, dgate, "tn", BF16, chips=True)
    gW["w_up"] = _matmul("mm_up_dw", h2, dup, "tn", BF16, chips=True)
    if ffn_grads_ready is not None:
        sp = dict(sp, ffn_pre_g=sp["ffn_pre_g"] + ffn_grads_ready({n: gW[n] for n in ("w_down", "w_gate", "w_up")})[0, 0])
    dh2 = _matmul("mm_gu_dx", dgate, W["w_gate"], "nt", F32, dup, W["w_up"], chips=True)

    def mid_back(dy, dh2, x1, mix, g_mp, g_fp):
        r2 = _rs(x1)
        xh = x1 * r2
        dx1 = dy + _rms_back(xh, r2, dh2 * g_fp)
        r1 = _rs(mix)
        mh = mix * r1
        return dx1, _rms_back(mh, r1, dx1 * g_mp), _colsum(dh2 * xh), _colsum(dx1 * mh)

    dx1, dmix, g_ffn_pre, g_mix_post = _rowwise("bwd_mid", mid_back, [dy, dh2, x1, mix], [sp["mix_post_g"], sp["ffn_pre_g"]],
                                                [(D, F32), (D, BF16)], [(1, D), (1, D)], ts)
    dab_out = _matmul("mm_out_dx", dmix, W["w_out"], "nt", F32)
    db_out = dab_out[:, cfg.MLAW:]
    gW["w_out"] = _matmul("mm_out_dw", ab_out, dmix, "tn", BF16)
    early_token = jnp.zeros((8, LANE), F32)
    if early_grads_ready is not None:
        early_token = early_grads_ready({n: gW[n] for n in ("w_down", "w_gate", "w_up", "w_out")})
        sp = dict(sp, ssm_norm_g=sp["ssm_norm_g"] + early_token[0, 0])

    dy_ssd, dz, g_ssm_norm = _ssd_post_bwd(cfg, db_out, y_ssd, z, sp["ssm_norm_g"])
    dxc, ddt_raw, g_dt_bias, g_a_log, g_d_skip = _ssd_bwd(cfg, dy_ssd, xc, dt_exp, a_exp, a_s, dskip_exp, hin, dt_raw,
                                                          dt_bias_pad, a_log_pad, expand)
    dxbc, g_ssm_conv_w, g_ssm_conv_b = _colwise("ssm_act_bwd", _ssm_act_back, [dxc, xbc], [sp["ssm_conv_w"], sp["ssm_conv_b"]],
                                                [BF16], [SSM_K, 1], tc)

    dQ, delta_t = _attn_dq(cfg, Qh, Kh, Vh, dab_out, a_out, lse, early_token)
    dK, dV = _attn_dkv(cfg, Qh, Kh, Vh, dab_out, lse_t, delta_t)
    dq, dkv, dkr = _mla_unpack(cfg, dQ, dK, dV, cos2, sin2)
    dcqn = _matmul("mm_uq_dx", dq, W["w_uq"], "nt", F32)
    dckvn = _matmul("mm_ukv_dx", dkv, W["w_ukv"], "nt", F32)
    gW["w_uq"] = _matmul("mm_uq_dw", cqn, dq, "tn", BF16)
    gW["w_ukv"] = _matmul("mm_ukv_dw", ckvn, dkv, "tn", BF16)

    def rms_back(x, dy, g):
        r = _rs(x)
        xh = x * r
        return _rms_back(xh, r, dy * g), _colsum(dy * xh)

    dc_q, g_q_norm = _rowwise("rms_q_bwd", rms_back, [c_q, dcqn], [sp["q_norm_g"]], [(cfg.QL, BF16)], [(1, cfg.QL)], ts)
    dc_kv, g_kv_norm = _rowwise("rms_kv_bwd", rms_back, [c_kv, dckvn], [sp["kv_norm_g"]], [(cfg.KVL, BF16)], [(1, cfg.KVL)], ts)

    du = jnp.concatenate([dc_q, dc_kv, dkr, dz.astype(BF16), dxbc, ddt_raw.astype(BF16)], axis=1)
    gW["w_in"] = _matmul("mm_in_dw", du, xn, "tn", BF16)
    if in_grad_ready is not None:
        token = in_grad_ready({n: gW[n] for n in ("w_in", "w_uq", "w_ukv")})
        sp = dict(sp, mix_pre_g=sp["mix_pre_g"] + token[0, 0])
    dxn = _matmul("mm_in_dx", du, W["w_in"], "nn", F32)

    def first_back(dx1, dxn, x, g):
        r = _rs(x)
        xh = x * r
        return dx1 + _rms_back(xh, r, dxn * g), _colsum(dxn * xh)

    grad_x, g_mix_pre = _rowwise("bwd_first", first_back, [dx1, dxn, x], [sp["mix_pre_g"]], [(D, F32)], [(1, D)], ts)

    gs = dict(mix_pre_g=g_mix_pre, q_norm_g=g_q_norm, kv_norm_g=g_kv_norm, ssm_conv_w=g_ssm_conv_w, ssm_conv_b=g_ssm_conv_b,
              dt_bias=g_dt_bias[:, :cfg.HS], a_log=g_a_log[:, :cfg.HS], d_skip=g_d_skip[:, :cfg.HS], ssm_norm_g=g_ssm_norm,
              mix_post_g=g_mix_post, ffn_pre_g=g_ffn_pre, ffn_conv_w=g_ffn_conv_w, ffn_conv_b=g_ffn_conv_b,
              ffn_post_g=g_ffn_post)
    return loss, grad_x, gW, gs


def _to_kernel_layout(cfg, name, w):
    if name == "w_in":
        a = cfg.o_kr + ROPE
        return jnp.concatenate([w[:a], jnp.zeros((LANE - ROPE, w.shape[1]), w.dtype), w[a:],
                                jnp.zeros((LANE - cfg.HS, w.shape[1]), w.dtype)], axis=0)
    if name in ("w_uq", "w_ukv"):
        per = NOPE + (ROPE if name == "w_uq" else VH)
        return jnp.concatenate([w[:, h * per:h * per + NOPE] for h in range(cfg.H)]
                               + [w[:, h * per + NOPE:(h + 1) * per] for h in range(cfg.H)], axis=1)
    return w


def _from_kernel_layout(cfg, name, g):
    if name == "w_in":
        return jnp.concatenate([g[:cfg.o_kr + ROPE], g[cfg.o_z:cfg.o_dt + cfg.HS]], axis=0)
    if name in ("w_uq", "w_ukv"):
        second = ROPE if name == "w_uq" else VH
        base = cfg.H * NOPE
        parts = []
        for h in range(cfg.H):
            parts += [g[:, h * NOPE:(h + 1) * NOPE], g[:, base + h * second:base + (h + 1) * second]]
        return jnp.concatenate(parts, axis=1)
    return g


def _cols_to_chips(w):
    r, c = w.shape
    return w.reshape(r, N_CHIPS, c // N_CHIPS).transpose(1, 0, 2)


def _chips_to_cols(g):
    k, r, cs = g.shape
    return g.transpose(1, 0, 2).reshape(r, k * cs)


_CHIP_MAJOR = ("w_gate", "w_up")
_RELAYOUT = ("w_uq", "w_ukv")
_LAYOUT_ROWS = 256


def _w_in_layout(cfg, wg):
    _, rs, d = wg.shape
    tc = _pick(d, _LAYOUT_ROWS, LANE)

    def body(w_ref, o_ref):
        o_ref[...] = _to_kernel_layout(cfg, "w_in", jnp.concatenate([w_ref[k] for k in range(N_CHIPS)], axis=0))

    return pl.pallas_call(
        body, name="layout_w_in", grid=(d // tc,),
        in_specs=[pl.BlockSpec((N_CHIPS, rs, tc), lambda j: (0, 0, j))], out_specs=pl.BlockSpec((cfg.EXT, tc), lambda j: (0, j)),
        out_shape=jax.ShapeDtypeStruct((cfg.EXT, d), wg.dtype), compiler_params=_params(("parallel",)),
    )(wg)


def _w_in_grad_to_chips(cfg, g):
    _, d = g.shape
    rs = cfg.IN_COLS // N_CHIPS
    tc = _pick(d, _LAYOUT_ROWS, LANE)

    def body(g_ref, o_ref):
        nat = _from_kernel_layout(cfg, "w_in", g_ref[...])
        for k in range(N_CHIPS):
            o_ref[k] = nat[k * rs:(k + 1) * rs]

    return pl.pallas_call(
        body, name="layout_grad_w_in", grid=(d // tc,),
        in_specs=[pl.BlockSpec((cfg.EXT, tc), lambda j: (0, j))], out_specs=pl.BlockSpec((N_CHIPS, rs, tc), lambda j: (0, 0, j)),
        out_shape=jax.ShapeDtypeStruct((N_CHIPS, rs, d), g.dtype), compiler_params=_params(("parallel",)),
    )(g)


def _gathered_to_kernel(cfg, name, wg):
    if name in _CHIP_MAJOR:
        return wg
    if name == "w_in":
        return _w_in_layout(cfg, wg)
    if name not in _RELAYOUT:
        return wg.reshape(wg.shape[0] * wg.shape[1], wg.shape[2])
    _, rows, cs = wg.shape
    tr = _pick(rows, _LAYOUT_ROWS, 16)

    def body(w_ref, o_ref):
        o_ref[...] = _to_kernel_layout(cfg, name, jnp.concatenate([w_ref[k] for k in range(N_CHIPS)], axis=1))

    wide = jax.eval_shape(lambda w: _to_kernel_layout(cfg, name, w), jax.ShapeDtypeStruct((rows, N_CHIPS * cs), wg.dtype)).shape[1]
    return pl.pallas_call(
        body, name="layout_" + name, grid=(rows // tr,),
        in_specs=[pl.BlockSpec((N_CHIPS, tr, cs), lambda i: (0, i, 0))], out_specs=pl.BlockSpec((tr, wide), lambda i: (i, 0)),
        out_shape=jax.ShapeDtypeStruct((rows, wide), wg.dtype), compiler_params=_params(("parallel",)),
    )(wg)


def _grad_to_chips(cfg, name, g):
    if name in _CHIP_MAJOR:
        return g
    if name == "w_in":
        return _w_in_grad_to_chips(cfg, g)
    if name not in _RELAYOUT:
        return g.reshape(N_CHIPS, g.shape[0] // N_CHIPS, g.shape[1])
    rows, wide = g.shape
    tr = _pick(rows, _LAYOUT_ROWS, 16)
    cs = jax.eval_shape(lambda v: _from_kernel_layout(cfg, name, v), g).shape[1] // N_CHIPS

    def body(g_ref, o_ref):
        nat = _from_kernel_layout(cfg, name, g_ref[...])
        for k in range(N_CHIPS):
            o_ref[k] = nat[:, k * cs:(k + 1) * cs]

    return pl.pallas_call(
        body, name="layout_grad_" + name, grid=(rows // tr,),
        in_specs=[pl.BlockSpec((tr, wide), lambda i: (i, 0))], out_specs=pl.BlockSpec((N_CHIPS, tr, cs), lambda i: (0, i, 0)),
        out_shape=jax.ShapeDtypeStruct((N_CHIPS, rows, cs), g.dtype), compiler_params=_params(("parallel",)),
    )(g)


def _me():
    return lax.axis_index("x"), lax.axis_index("y"), lax.axis_index("c")


def _other_chips(x, y):
    return [(1 - x, y), (x, 1 - y), (1 - x, 1 - y)]


_ANY = pl.BlockSpec(memory_space=pl.ANY)


def _row_block(rows, cols, mult):
    return _pick(rows, max(mult, (1 << 19) // cols // mult * mult), mult)


def _scalar(v):
    return v.astype(I32).reshape(1)


def _blocks2d(r, c, mult):
    if r % mult == 0:
        tr = _row_block(r, c, mult)
        return (tr, c), r // tr, lambda i: (i, 0)
    tc = _pick(c, max(LANE, (1 << 19) // r // LANE * LANE), LANE)
    return (r, tc), c // tc, lambda i: (0, i)


def _by_rows(rows):
    return rows % 32 == 0


def _half_shape(rows, cols):
    return (rows // 2, cols) if _by_rows(rows) else (rows, cols // 2)


def _half_blocks(rows, cols, mult):
    hr, hc = _half_shape(rows, cols)
    block, n, part = _blocks2d(hr, hc, mult)
    assert (hr % mult == 0) == _by_rows(rows), (rows, cols, mult)
    full = (lambda h, i: (h * n + i, 0)) if _by_rows(rows) else (lambda h, i: (0, h * n + i))
    return block, n, full, part


def _half(ref, k, half):
    hr, hc = _half_shape(ref.shape[1], ref.shape[2])
    if _by_rows(ref.shape[1]):
        return ref.at[k, pl.ds(pl.multiple_of(half * hr, 16), hr), :]
    return ref.at[k, :, pl.ds(pl.multiple_of(half * hc, LANE), hc)]


def _shard_blocks(w, br, bc):
    if w.shape[0] == 1:
        def write(ref, v):
            ref[...] = v
        return (lambda f: pl.BlockSpec((None, br, bc), lambda *a: (0, *f(*a)))), (lambda ref: ref[...]), write
    assert w.shape[1] == 1 and br == w.shape[0], w.shape

    def write_rows(ref, v):
        ref[:, 0, :] = v
    return (lambda f: pl.BlockSpec((br, 1, bc), lambda *a: (0, 0, f(*a)[1]))), (lambda ref: ref[:, 0, :]), write_rows


def _stage_shard(name, w, chip, after=None):
    rs, cs = w.shape[0] * w.shape[1], w.shape[2]
    (br, bc), n, idx = _blocks2d(rs, cs, 16)
    spec, get, _ = _shard_blocks(w, br, bc)

    def body(chip_ref, w_ref, *refs):
        refs[-1][...] = get(w_ref).astype(BF16)

    return pl.pallas_call(
        body, name="stage_" + name,
        grid_spec=pltpu.PrefetchScalarGridSpec(
            num_scalar_prefetch=1, grid=(n,),
            in_specs=[spec(lambda i, chip_ref: idx(i))] + ([] if after is None else [_ANY]),
            out_specs=pl.BlockSpec((None, br, bc), lambda i, chip_ref: (chip_ref[0], *idx(i)))),
        out_shape=jax.ShapeDtypeStruct((N_CHIPS, rs, cs), BF16),
        compiler_params=_params(("parallel",)),
    )(_scalar(chip), w, *([] if after is None else [after]))


_HBM = pl.BlockSpec(memory_space=pltpu.HBM)
_SEM = pl.BlockSpec(memory_space=pltpu.SEMAPHORE)
_EFFECT = pltpu.SideEffectType.DATAFLOW_SIDE_EFFECTING


def _split_start(name, bufs, n_copies, copies, after):
    n = len(bufs)

    def body(*refs):
        for cp in copies(refs[:n], refs[n + 1], refs[n + 2]):
            cp.start()
        refs[-1][...] = jnp.zeros_like(refs[-1])

    res = pl.pallas_call(
        body, name=name,
        out_shape=(pltpu.SemaphoreType.DMA((n_copies,)), pltpu.SemaphoreType.DMA((n_copies,)),
                   *[pltpu.HBM(b.shape, b.dtype) for b in bufs], jax.ShapeDtypeStruct((8, LANE), F32)),
        in_specs=[_HBM] * n + [_ANY], out_specs=(_SEM, _SEM, *[_HBM] * n, pl.BlockSpec(memory_space=pltpu.VMEM)),
        input_output_aliases={i: 2 + i for i in range(n)},
        compiler_params=pltpu.CompilerParams(has_side_effects=_EFFECT),
    )(*[pltpu.with_memory_space_constraint(b, pltpu.HBM) for b in bufs], after)
    return res[0], res[1], list(res[2:2 + n]), res[-1]


def _split_wait(name, send_sems, recv_sems, bufs, after, copies):
    n = len(bufs)

    def body(*refs):
        for cp in copies(refs[:n], refs[n], refs[n + 1]):
            cp.wait_send()
            cp.wait_recv()

    return list(pl.pallas_call(
        body, name=name, out_shape=[pltpu.HBM(b.shape, b.dtype) for b in bufs],
        in_specs=[_HBM] * n + [_SEM, _SEM, _ANY], out_specs=[_HBM] * n,
        input_output_aliases={i: i for i in range(n)},
        compiler_params=pltpu.CompilerParams(has_side_effects=_EFFECT),
    )(*bufs, send_sems, recv_sems, after))


def _gather_to_chips(bufs, send_sems, recv_sems):
    x, y, c = _me()
    return [pltpu.make_async_remote_copy(src_ref=_half(b, 2 * x + y, c), dst_ref=_half(b, 2 * x + y, c),
                                         send_sem=send_sems.at[3 * w + j], recv_sem=recv_sems.at[3 * w + j],
                                         device_id=(cx, cy, c), device_id_type=MESH_ID)
            for w, b in enumerate(bufs) for j, (cx, cy) in enumerate(_other_chips(x, y))]


def _gather_to_sibling(bufs, send_sems, recv_sems):
    x, y, c = _me()
    return [pltpu.make_async_remote_copy(src_ref=_half(b, 2 * cx + cy, c), dst_ref=_half(b, 2 * cx + cy, c),
                                         send_sem=send_sems.at[3 * w + j], recv_sem=recv_sems.at[3 * w + j],
                                         device_id=(x, y, 1 - c), device_id_type=MESH_ID)
            for w, b in enumerate(bufs) for j, (cx, cy) in enumerate(_other_chips(x, y))]


def _pair_exchange(name, grads):
    n = len(grads)

    def body(*refs):
        ins, outs, send_sems, recv_sems = refs[:n], refs[n:2 * n], refs[2 * n], refs[2 * n + 1]
        x, y, c = _me()
        cps = []
        for w, (g_ref, o_ref) in enumerate(zip(ins, outs)):
            cps.append(pltpu.make_async_remote_copy(src_ref=_half(g_ref, slice(None), 1 - c), dst_ref=o_ref,
                                                    send_sem=send_sems.at[w], recv_sem=recv_sems.at[w],
                                                    device_id=(x, y, 1 - c), device_id_type=MESH_ID))
            cps[-1].start()
        for cp in cps:
            cp.wait()

    return pl.pallas_call(
        body, name="pair_exchange_" + name, in_specs=[_ANY] * n, out_specs=[_ANY] * n,
        out_shape=[jax.ShapeDtypeStruct((g.shape[0], *_half_shape(g.shape[1], g.shape[2])), g.dtype) for g in grads],
        scratch_shapes=[pltpu.SemaphoreType.DMA((n,)), pltpu.SemaphoreType.DMA((n,))],
    )(*grads)


def _pair_copies(grads, lands, send_sems, recv_sems):
    x, y, c = _me()
    return [pltpu.make_async_remote_copy(src_ref=_half(g_ref, slice(None), 1 - c), dst_ref=l_ref, send_sem=send_sems.at[w],
                                         recv_sem=recv_sems.at[w], device_id=(x, y, 1 - c), device_id_type=MESH_ID)
            for w, (g_ref, l_ref) in enumerate(zip(grads, lands))]


def _pair_exchange_start(name, grads):
    n = len(grads)
    lands = [lax.empty((g.shape[0], *_half_shape(g.shape[1], g.shape[2])), g.dtype) for g in grads]
    send_sems, recv_sems, bufs, token = _split_start(
        "pair_exchange_start_" + name, [*grads, *lands], n, lambda refs, ss, rs: _pair_copies(refs[:n], refs[n:], ss, rs),
        jnp.zeros((8, LANE), F32))
    return (send_sems, recv_sems, bufs), token


def _pair_exchange_wait(name, state, after):
    send_sems, recv_sems, bufs = state
    n = len(bufs) // 2
    bufs = _split_wait("pair_exchange_wait_" + name, send_sems, recv_sems, bufs, after,
                       lambda refs, ss, rs: _pair_copies(refs[:n], refs[n:], ss, rs))
    return bufs[:n], bufs[n:]


def _pair_sum(name, g, theirs, c):
    (br, bc), nb, full, part = _half_blocks(g.shape[1], g.shape[2], 16)

    def body(c_ref, a_ref, b_ref, o_ref):
        o_ref[...] = (a_ref[...].astype(F32) + b_ref[...].astype(F32)).astype(o_ref.dtype)

    return pl.pallas_call(
        body, name="pair_sum_" + name,
        grid_spec=pltpu.PrefetchScalarGridSpec(
            num_scalar_prefetch=1, grid=(N_CHIPS, nb),
            in_specs=[pl.BlockSpec((None, br, bc), lambda k, i, c_ref: (k, *full(c_ref[0], i))),
                      pl.BlockSpec((None, br, bc), lambda k, i, c_ref: (k, *part(i)))],
            out_specs=pl.BlockSpec((None, br, bc), lambda k, i, c_ref: (k, *part(i)))),
        out_shape=jax.ShapeDtypeStruct(theirs.shape, BF16),
        compiler_params=_params(("parallel", "parallel")),
    )(_scalar(c), g, theirs)


def _chip_copies(srcs, lands, send_sems, recv_sems):
    x, y, c = _me()
    return [pltpu.make_async_remote_copy(src_ref=s_ref.at[2 * cx + cy], dst_ref=l_ref.at[j], send_sem=send_sems.at[3 * w + j],
                                         recv_sem=recv_sems.at[3 * w + j], device_id=(cx, cy, c), device_id_type=MESH_ID)
            for w, (s_ref, l_ref) in enumerate(zip(srcs, lands)) for j, (cx, cy) in enumerate(_other_chips(x, y))]


def _chip_exchange_start(name, sums):
    n = len(sums)
    lands = [lax.empty((3,) + s.shape[1:], s.dtype) for s in sums]
    send_sems, recv_sems, bufs, token = _split_start(
        "chip_exchange_start_" + name, [*sums, *lands], 3 * n, lambda refs, ss, rs: _chip_copies(refs[:n], refs[n:], ss, rs),
        jnp.zeros((8, LANE), F32))
    return send_sems, recv_sems, bufs[:n], bufs[n:], token


def _chip_exchange_wait(name, send_sems, recv_sems, sums, lands, after):
    n = len(sums)
    bufs = _split_wait("chip_exchange_wait_" + name, send_sems, recv_sems, [*sums, *lands], after,
                       lambda refs, ss, rs: _chip_copies(refs[:n], refs[n:], ss, rs))
    return bufs[:n], bufs[n:]


def _chip_sum(name, sums, theirs, chip):
    _, h, cs = sums.shape
    (br, bc), nb, idx = _blocks2d(h, cs, 16)

    def body(chip_ref, s_ref, t_ref, o_ref):
        acc = s_ref[...].astype(F32)
        for k in range(3):
            acc = acc + t_ref[k].astype(F32)
        o_ref[...] = acc

    return pl.pallas_call(
        body, name="chip_sum_" + name,
        grid_spec=pltpu.PrefetchScalarGridSpec(
            num_scalar_prefetch=1, grid=(nb,),
            in_specs=[pl.BlockSpec((None, br, bc), lambda i, chip_ref: (chip_ref[0], *idx(i))),
                      pl.BlockSpec((3, br, bc), lambda i, chip_ref: (0, *idx(i)))],
            out_specs=pl.BlockSpec((br, bc), lambda i, chip_ref: idx(i))),
        out_shape=jax.ShapeDtypeStruct((h, cs), F32),
        compiler_params=_params(("parallel",)),
    )(_scalar(chip), sums, theirs)


def _sibling_exchange(name, halves):
    n = len(halves)

    def body(*refs):
        ins, outs, send_sems, recv_sems = refs[:n], refs[n:2 * n], refs[2 * n], refs[2 * n + 1]
        x, y, c = _me()
        cps = []
        for w, (h_ref, o_ref) in enumerate(zip(ins, outs)):
            cps.append(pltpu.make_async_remote_copy(src_ref=h_ref, dst_ref=o_ref, send_sem=send_sems.at[w], recv_sem=recv_sems.at[w],
                                                    device_id=(x, y, 1 - c), device_id_type=MESH_ID))
            cps[-1].start()
        for cp in cps:
            cp.wait()

    return pl.pallas_call(
        body, name="sibling_exchange_" + name, in_specs=[_ANY] * n, out_specs=[_ANY] * n,
        out_shape=[jax.ShapeDtypeStruct(h.shape, h.dtype) for h in halves],
        scratch_shapes=[pltpu.SemaphoreType.DMA((n,)), pltpu.SemaphoreType.DMA((n,))],
    )(*halves)


def _allreduce_small(name, vec, after):
    def body(v_ref, after_ref, o_ref, buf_ref, send_sems, recv_sems):
        x, y, c = _me()
        me = 4 * x + 2 * y + c
        cps = []
        for p in range(1, 8):
            px, py, pc = x ^ (p >> 2), y ^ ((p >> 1) & 1), c ^ (p & 1)
            cps.append(pltpu.make_async_remote_copy(src_ref=v_ref, dst_ref=buf_ref.at[me], send_sem=send_sems.at[p - 1],
                                                    recv_sem=recv_sems.at[p - 1], device_id=(px, py, pc), device_id_type=MESH_ID))
            cps[-1].start()
        buf_ref[me] = v_ref[...]
        for p in range(1, 8):
            theirs = buf_ref.at[me ^ p]
            pltpu.make_async_remote_copy(src_ref=theirs, dst_ref=theirs, send_sem=send_sems.at[p - 1], recv_sem=recv_sems.at[p - 1],
                                         device_id=(x, y, c), device_id_type=MESH_ID).wait_recv()
        for cp in cps:
            cp.wait_send()
        acc = buf_ref[0]
        for k in range(1, 8):
            acc = acc + buf_ref[k]
        o_ref[...] = acc

    vm = pl.BlockSpec(memory_space=pltpu.VMEM)
    return pl.pallas_call(
        body, name=name, in_specs=[vm, _ANY], out_specs=vm, out_shape=jax.ShapeDtypeStruct(vec.shape, F32),
        scratch_shapes=[pltpu.VMEM((8,) + vec.shape, F32), pltpu.SemaphoreType.DMA((7,)), pltpu.SemaphoreType.DMA((7,))],
    )(vec, after)


def _adam_math(w, g, m, v):
    m = ADAM_B1 * m + (1.0 - ADAM_B1) * g
    v = ADAM_B2 * v + (1.0 - ADAM_B2) * (g * g)
    m_hat = m / (1.0 - ADAM_B1 ** ADAM_STEP)
    v_hat = v / (1.0 - ADAM_B2 ** ADAM_STEP)
    return -ADAM_LR * (m_hat / (jnp.sqrt(v_hat) + ADAM_EPS) + ADAM_WD * w), m, v


def _adamw(name, w, g, m, v):
    R, C = w.shape
    tr = _row_block(R, C, 8)

    def body(w_ref, g_ref, m_ref, v_ref, d_ref, nm_ref, nv_ref):
        d_ref[...], nm_ref[...], nv_ref[...] = _adam_math(w_ref[...], g_ref[...], m_ref[...], v_ref[...])

    blk = pl.BlockSpec((tr, C), lambda i: (i, 0))
    return pl.pallas_call(
        body, name=name, grid=(R // tr,), in_specs=[blk] * 4, out_specs=[blk] * 3,
        out_shape=[jax.ShapeDtypeStruct((R, C), F32)] * 3, compiler_params=_params(("parallel",)),
    )(w, g, m, v)


def _adamw_halves(name, w, mine, theirs, m, v, c):
    rs, cs = w.shape[0] * w.shape[1], w.shape[2]
    (br, bc), nb, whole, half = _half_blocks(rs, cs, 8)
    spec, get, put = _shard_blocks(w, br, bc)

    def body(c_ref, w_ref, a_ref, b_ref, m_ref, v_ref, g_ref, d_ref, nm_ref, nv_ref):
        g = jnp.where(pl.program_id(0) == c_ref[0], a_ref[...], b_ref[...])
        put(g_ref, g)
        for ref, val in zip((d_ref, nm_ref, nv_ref), _adam_math(get(w_ref), g, get(m_ref), get(v_ref))):
            put(ref, val)

    full = spec(lambda s, i, c_ref: whole(s, i))
    part = pl.BlockSpec((br, bc), lambda s, i, c_ref: half(i))
    return pl.pallas_call(
        body, name=name,
        grid_spec=pltpu.PrefetchScalarGridSpec(num_scalar_prefetch=1, grid=(2, nb), in_specs=[full, part, part, full, full],
                                               out_specs=[full] * 4),
        out_shape=[jax.ShapeDtypeStruct(w.shape, F32)] * 4, compiler_params=_params(("parallel", "parallel")),
    )(_scalar(c), w, mine, theirs, m, v)


def _pack_small(arrs):
    flat = jnp.concatenate([a.reshape(-1) for a in arrs])
    n = -(-flat.shape[0] // (8 * LANE)) * 8 * LANE
    return jnp.pad(flat, (0, n - flat.shape[0])).reshape(8, n // 8)


def _unpack_small(vec, shapes):
    flat, out, off = vec.reshape(-1), [], 0
    for s in shapes:
        out.append(flat[off:off + s[0] * s[1]].reshape(s))
        off += s[0] * s[1]
    return out


class _LateWeights:
    def __init__(self, cfg, tag, names, staged, after):
        self.cfg, self.tag, self.names, self.k = cfg, tag, names, 3 * len(names)
        self.send, self.recv, self.bufs, self.token = _split_start(f"gather_{tag}_chips_start", staged, self.k, _gather_to_chips,
                                                                    after)

    def pass_on(self, after):
        bufs = _split_wait(f"gather_{self.tag}_chips_wait", self.send, self.recv, self.bufs, after, _gather_to_chips)
        self.send, self.recv, self.bufs, token = _split_start(f"gather_{self.tag}_sibling_start", bufs, self.k, _gather_to_sibling,
                                                               self.token)
        return token

    def arrived(self, after):
        bufs = _split_wait(f"gather_{self.tag}_sibling_wait", self.send, self.recv, self.bufs, after, _gather_to_sibling)
        return {n: _gathered_to_kernel(self.cfg, n, b) for n, b in zip(self.names, bufs)}


def _step(cfg, a):
    chip = 2 * lax.axis_index("x") + lax.axis_index("y")
    core = lax.axis_index("c")
    big = BIG

    ffn = ("w_gate", "w_up", "w_down")
    first = ("w_in", "w_uq", "w_ukv")
    staged = {"w_in": _stage_shard("w_in", a["w_in"], chip)}
    in_weight = _LateWeights(cfg, "in", ("w_in",), [staged["w_in"]], jnp.zeros((8, LANE), F32))
    staged.update({n: _stage_shard(n, a[n], chip, in_weight.token) for n in big if n != "w_in"})
    in_sibling_leg = in_weight.pass_on(staged[big[-1]])

    sp = {n: a[n] for n in SMALL}
    sharded = _pack_small([a[n] for n in SMALL_SHARDED])
    slot = jnp.where(lax.broadcasted_iota(I32, (N_CHIPS,) + sharded.shape, 0) == chip, 0.5 * sharded[None], 0.0)
    allp = _allreduce_small("allgather_small", slot.reshape(N_CHIPS * 8, -1), in_sibling_leg).reshape((N_CHIPS,) + sharded.shape)
    per_chip = [_unpack_small(allp[ch], [a[n].shape for n in SMALL_SHARDED]) for ch in range(N_CHIPS)]
    for k, n in enumerate(SMALL_SHARDED):
        sp[n] = jnp.concatenate([per_chip[ch][k] for ch in range(N_CHIPS)], axis=1)
    W = in_weight.arrived(allp)

    mla_weights = _LateWeights(cfg, "mla", first[1:], [staged[n] for n in first[1:]], W["w_in"])
    out_weight = _LateWeights(cfg, "out", ("w_out",), [staged["w_out"]], mla_weights.token)
    ffn_weights = _LateWeights(cfg, "ffn", ffn, [staged[n] for n in ffn], out_weight.token)
    sp["mix_pre_g"] = sp["mix_pre_g"] + (mla_weights.token[0, 0] + out_weight.token[0, 0] + ffn_weights.token[0, 0])

    state = {}

    def ffn_grads_ready(grads):
        state["ffn_pairs"], token = _pair_exchange_start("ffn", [_grad_to_chips(cfg, n, grads[n]) for n in ffn_grads])
        return token

    def pair_sums(names, grads, theirs):
        return [_pair_sum(n, g, t, core) for n, g, t in zip(names, grads, theirs)]

    def early_grads_ready(grads):
        g_out = [_grad_to_chips(cfg, "w_out", grads["w_out"])]
        g_ffn, t_ffn = _pair_exchange_wait("ffn", state["ffn_pairs"], g_out[0])
        sums = pair_sums(ffn_grads, g_ffn, t_ffn) + pair_sums(["w_out"], g_out, _pair_exchange("out", g_out))
        state["early"] = _chip_exchange_start("early", sums)
        return state["early"][-1]

    def reduced_halves(tag, names, after):
        send_sems, recv_sems, s_bufs, l_bufs, _ = state[tag]
        s_bufs, l_bufs = _chip_exchange_wait(tag, send_sems, recv_sems, s_bufs, l_bufs, after)
        return [_chip_sum(n, s, t, chip) for n, s, t in zip(names, s_bufs, l_bufs)]

    def in_grad_ready(grads):
        grads = [_grad_to_chips(cfg, n, grads[n]) for n in first]
        state["rest"] = _chip_exchange_start("rest", pair_sums(first, grads, _pair_exchange("rest", grads)))
        return state["rest"][-1]

    ffn_grads = ("w_down", "w_gate", "w_up")
    early = ffn_grads + ("w_out",)
    loss, grad_x, gW, gs = _local_grads(cfg, a["x"], a["loss_target"], W, sp, mla_weights, out_weight, ffn_weights,
                                        ffn_grads_ready, early_grads_ready, in_grad_ready)
    out = {"grad_x": grad_x}

    def adamw(names, mine, theirs):
        for n, gm, gt in zip(names, mine, theirs):
            out["grad_" + n], out["delta_" + n], out["new_m_" + n], out["new_v_" + n] = _adamw_halves(
                "adamw_" + n, a[n], gm, gt, a["m_" + n], a["v_" + n], core)

    mine = reduced_halves("early", early, grad_x)
    adamw(early, mine, _sibling_exchange("early", mine))
    mine = reduced_halves("rest", first, out["new_v_" + early[-1]])
    theirs = _sibling_exchange("rest", mine)
    adamw(first, mine, theirs)

    shapes = [gs[n].shape for n in SMALL] + [(1, LANE)]
    red = _unpack_small(_allreduce_small("allreduce_small", _pack_small([gs[n] for n in SMALL] + [loss]), theirs[0]), shapes)
    g_small = dict(zip(SMALL, red[:-1]))
    for n in SMALL_SHARDED:
        cs = a[n].shape[1]
        g_small[n] = lax.dynamic_slice_in_dim(g_small[n], chip * cs, cs, axis=1)
    out["loss"] = red[-1][0, 0]
    sshapes = [a[n].shape for n in SMALL]
    d, nm, nv = _adamw("adamw_small", _pack_small([a[n] for n in SMALL]), _pack_small([g_small[n] for n in SMALL]),
                       _pack_small([a["m_" + n] for n in SMALL]), _pack_small([a["v_" + n] for n in SMALL]))
    for n, dd, mm, vv in zip(SMALL, _unpack_small(d, sshapes), _unpack_small(nm, sshapes), _unpack_small(nv, sshapes)):
        out["grad_" + n], out["delta_" + n], out["new_m_" + n], out["new_v_" + n] = g_small[n], dd, mm, vv
    return out


def kernel(x, mix_pre_g, w_in, q_norm_g, w_uq, kv_norm_g, w_ukv, ssm_conv_w, ssm_conv_b, dt_bias, a_log, d_skip, ssm_norm_g, w_out, mix_post_g, ffn_pre_g, w_gate, w_up, ffn_conv_w, ffn_conv_b, w_down, ffn_post_g, loss_target, m_mix_pre_g, m_w_in, m_q_norm_g, m_w_uq, m_kv_norm_g, m_w_ukv, m_ssm_conv_w, m_ssm_conv_b, m_dt_bias, m_a_log, m_d_skip, m_ssm_norm_g, m_w_out, m_mix_post_g, m_ffn_pre_g, m_w_gate, m_w_up, m_ffn_conv_w, m_ffn_conv_b, m_w_down, m_ffn_post_g, v_mix_pre_g, v_w_in, v_q_norm_g, v_w_uq, v_kv_norm_g, v_w_ukv, v_ssm_conv_w, v_ssm_conv_b, v_dt_bias, v_a_log, v_d_skip, v_ssm_norm_g, v_w_out, v_mix_post_g, v_ffn_pre_g, v_w_gate, v_w_up, v_ffn_conv_w, v_ffn_conv_b, v_w_down, v_ffn_post_g):
    args = dict(locals())
    def given(k, v):
        if k in ("w_in", "m_w_in", "v_w_in"):
            return jnp.transpose(v, (2, 0, 1))
        return v if k.removeprefix("m_").removeprefix("v_") in BIG or v.ndim < 3 else v[0]

    out = _step(_FULL, {k: given(k, v) for k, v in args.items()})
    res = [out["loss"], out["grad_x"][None]]
    for pre in ("grad_", "delta_", "new_m_", "new_v_"):
        for n in WEIGHTS:
            o = out[pre + n]
            res.append(jnp.transpose(o, (1, 2, 0)) if n == "w_in" else o if n in BIG or args[n].ndim < 3 else o[None])
    return tuple(res)
```

```python
import functools
import math

import jax
import jax.numpy as jnp
from jax import lax
from jax.experimental import pallas as pl
from jax.experimental.pallas import tpu as pltpu

F32, BF16, I32 = jnp.float32, jnp.bfloat16, jnp.int32
NN = (((1,), (0,)), ((), ()))
NT = (((1,), (1,)), ((), ()))
TN = (((0,), (0,)), ((), ()))
HI = lax.Precision.HIGHEST
MESH_ID = pl.DeviceIdType.MESH

EPS = 1e-6
CHUNK = 64
NOPE, ROPE, VH = 128, 64, 128
ROPE_THETA = 10000.0
HP, NST = 64, 128
SSM_K, FFN_K = 4, 3
LANE = 128
N_CHIPS = 4
VMEM_LIMIT = 52 * 1024 * 1024
MM_TILE, MM_TILE_K = 1408, 2816

ADAM_LR, ADAM_B1, ADAM_B2, ADAM_EPS, ADAM_WD, ADAM_STEP = 0.001, 0.9, 0.999, 1e-08, 0.01, 10


class _Cfg:
    def __init__(self, S, D, QL, KVL, H, HS, G, DFF, T):
        self.S, self.D, self.QL, self.KVL, self.H, self.HS, self.G, self.DFF, self.T = S, D, QL, KVL, H, HS, G, DFF, T
        self.INNER = HS * HP
        self.CONVCH = self.INNER + 2 * G * NST
        self.QW = H * (NOPE + ROPE)
        self.KVW = H * (NOPE + VH)
        self.MLAW = H * VH
        self.MIXW = self.MLAW + self.INNER
        self.IN_COLS = QL + KVL + ROPE + self.INNER + self.CONVCH + HS
        self.o_kr = QL + KVL
        self.o_z = self.o_kr + LANE
        self.o_xbc = self.o_z + self.INNER
        self.o_dt = self.o_xbc + self.CONVCH
        self.EXT = self.o_dt + LANE
        self.NPAIR = HS // 2
        self.REP = HS // G


_FULL = _Cfg(S=2048, D=2048, QL=768, KVL=512, H=8, HS=16, G=2, DFF=5632, T=256)
BIG = ("w_in", "w_uq", "w_ukv", "w_out", "w_gate", "w_up", "w_down")

SMALL = ("mix_pre_g", "q_norm_g", "kv_norm_g", "ssm_conv_w", "ssm_conv_b", "dt_bias", "a_log", "d_skip", "ssm_norm_g",
         "mix_post_g", "ffn_pre_g", "ffn_conv_w", "ffn_conv_b", "ffn_post_g")
SMALL_SHARDED = ("ssm_conv_w", "ffn_conv_w")
WEIGHTS = ("mix_pre_g", "w_in", "q_norm_g", "w_uq", "kv_norm_g", "w_ukv", "ssm_conv_w", "ssm_conv_b", "dt_bias", "a_log",
           "d_skip", "ssm_norm_g", "w_out", "mix_post_g", "ffn_pre_g", "w_gate", "w_up", "ffn_conv_w", "ffn_conv_b",
           "w_down", "ffn_post_g")


def _pick(n, target, mult):
    best = None
    for d in range(mult, min(n, target) + 1, mult):
        if n % d == 0:
            best = d
    return best if best is not None else n


def _params(sem=None):
    kw = dict(vmem_limit_bytes=VMEM_LIMIT)
    if sem is not None:
        kw["dimension_semantics"] = sem
    return pltpu.CompilerParams(**kw)


def _dot(a, b, dims=NN, precision=None):
    return lax.dot_general(a, b, dims, preferred_element_type=F32, precision=precision)


def _sigmoid(x):
    return 1.0 / (1.0 + jnp.exp(-x))


def _rs(x):
    return lax.rsqrt(jnp.mean(x * x, axis=-1, keepdims=True) + EPS)


def _rms_back(xh, r, dn):
    return r * (dn - xh * jnp.mean(dn * xh, axis=-1, keepdims=True))


def _colsum(v):
    return jnp.sum(v, axis=0, keepdims=True)


def _matmul(name, a, b, mode, out_dtype, a2=None, b2=None, chips=False):
    cs = None
    if mode == "nn":
        (M, K), N = a.shape, b.shape[-1]
        if chips:
            cs, N = N, N_CHIPS * N
    elif mode == "nt":
        (M, K), N = a.shape, b.shape[-2]
        if chips:
            cs = b.shape[-1]
    else:
        (K, M), N = a.shape, b.shape[1]
        if chips:
            cs = N // N_CHIPS
    tm = _pick(M, MM_TILE, LANE)
    tn = _pick(cs if chips and mode != "nt" else N, MM_TILE, LANE)
    tk = _pick(cs, MM_TILE, LANE) if chips and mode == "nt" else _pick(K, MM_TILE_K, LANE)
    nk = K // tk
    dims = {"nn": NN, "nt": NT, "tn": TN}[mode]
    a_spec = pl.BlockSpec((tk, tm), lambda i, j, k: (k, i)) if mode == "tn" else pl.BlockSpec((tm, tk), lambda i, j, k: (i, k))
    b_spec = pl.BlockSpec((tn, tk), lambda i, j, k: (j, k)) if mode == "nt" else pl.BlockSpec((tk, tn), lambda i, j, k: (k, j))
    o_spec = pl.BlockSpec((tm, tn), lambda i, j, k: (i, j))
    o_shape = (M, N)
    if chips and mode == "nn":
        per = cs // tn
        b_spec = pl.BlockSpec((None, tk, tn), lambda i, j, k: (j // per, k, j % per))
    elif chips and mode == "nt":
        per = cs // tk
        b_spec = pl.BlockSpec((None, tn, tk), lambda i, j, k: (k // per, j, k % per))
    elif chips:
        per = cs // tn
        o_spec = pl.BlockSpec((None, tm, tn), lambda i, j, k: (j // per, i, j % per))
        o_shape = (N_CHIPS, M, cs)
    two = a2 is not None

    def product(refs):
        part = _dot(refs[0][...].astype(BF16), refs[1][...].astype(BF16), dims)
        if two:
            part += _dot(refs[2][...].astype(BF16), refs[3][...].astype(BF16), dims)
        return part

    def body_whole_k(*refs):
        refs[-1][...] = product(refs).astype(refs[-1].dtype)

    def body(*refs):
        o_ref, acc_ref = refs[-2], refs[-1]
        k = pl.program_id(2)

        @pl.when(k == 0)
        def _():
            acc_ref[...] = product(refs)

        @pl.when(k > 0)
        def _():
            acc_ref[...] += product(refs)

        @pl.when(k == nk - 1)
        def _():
            o_ref[...] = acc_ref[...].astype(o_ref.dtype)

    ins = (a, b, a2, b2) if two else (a, b)
    return pl.pallas_call(
        body_whole_k if nk == 1 else body, name=name, grid=(M // tm, N // tn, nk),
        in_specs=[a_spec, b_spec] * (2 if two else 1),
        out_specs=o_spec,
        out_shape=jax.ShapeDtypeStruct(o_shape, out_dtype),
        scratch_shapes=[] if nk == 1 else [pltpu.VMEM((tm, tn), F32)],
        compiler_params=_params(("parallel", "parallel", "arbitrary")),
    )(*ins)


def _rowwise(name, fn, rows, mats, outs, reds, ts):
    S = rows[0].shape[0]
    nr, nm, no = len(rows), len(mats), len(outs)

    def body(*refs):
        res = fn(*[r[...] for r in refs[:nr + nm]])
        res = res if isinstance(res, (tuple, list)) else (res,)
        for r, v in zip(refs[nr + nm:nr + nm + no], res[:no]):
            r[...] = v.astype(r.dtype)
        first = pl.program_id(0) == 0
        for r, v in zip(refs[nr + nm + no:], res[no:]):
            @pl.when(first)
            def _():
                r[...] = jnp.broadcast_to(v, r.shape)

            @pl.when(jnp.logical_not(first))
            def _():
                r[...] += jnp.broadcast_to(v, r.shape)

    in_specs = [pl.BlockSpec((ts, a.shape[1]), lambda i: (i, 0)) for a in rows]
    in_specs += [pl.BlockSpec(m.shape, lambda i, nd=m.ndim: (0,) * nd) for m in mats]
    out_specs = [pl.BlockSpec((ts, w), lambda i: (i, 0)) for w, _ in outs]
    out_specs += [pl.BlockSpec(s, lambda i: (0, 0)) for s in reds]
    out_shape = [jax.ShapeDtypeStruct((S, w), dt) for w, dt in outs] + [jax.ShapeDtypeStruct(s, F32) for s in reds]
    return pl.pallas_call(
        body, name=name, grid=(S // ts,), in_specs=in_specs, out_specs=out_specs, out_shape=out_shape,
        compiler_params=_params(("arbitrary",) if reds else ("parallel",)),
    )(*rows, *mats)


def _shift_down(v, s):
    if s == 0:
        return v
    rows = lax.broadcasted_iota(I32, v.shape, 0)
    return jnp.where(rows >= s, pltpu.roll(v, s, 0), 0.0)


def _shift_up(v, s):
    if s == 0:
        return v
    n = v.shape[0]
    rows = lax.broadcasted_iota(I32, v.shape, 0)
    return jnp.where(rows < n - s, pltpu.roll(v, n - s, 0), 0.0)


def _conv(x, w, b):
    K = w.shape[0]
    y = jnp.broadcast_to(b, x.shape)
    for k in range(K):
        y = y + w[k:k + 1, :] * _shift_down(x, K - 1 - k)
    return y


def _conv_back(x, w, dc):
    K = w.shape[0]
    dx = jnp.zeros_like(x)
    dw = []
    for k in range(K):
        dx = dx + w[k:k + 1, :] * _shift_up(dc, K - 1 - k)
        dw.append(_colsum(dc * _shift_down(x, K - 1 - k)))
    return dx, jnp.concatenate(dw, axis=0), _colsum(dc)


def _colwise(name, fn, cols, vecs, outs, pouts, tc):
    S, C = cols[0].shape
    nc_, nv, no = len(cols), len(vecs), len(outs)

    def body(*refs):
        res = fn(*[r[...] for r in refs[:nc_ + nv]])
        res = res if isinstance(res, (tuple, list)) else (res,)
        for r, v in zip(refs[nc_ + nv:], res):
            r[...] = v.astype(r.dtype)

    in_specs = [pl.BlockSpec((S, tc), lambda j: (0, j)) for _ in cols]
    in_specs += [pl.BlockSpec((v.shape[0], tc), lambda j: (0, j)) for v in vecs]
    out_specs = [pl.BlockSpec((S, tc), lambda j: (0, j)) for _ in outs] + [pl.BlockSpec((k, tc), lambda j: (0, j)) for k in pouts]
    out_shape = [jax.ShapeDtypeStruct((S, C), dt) for dt in outs] + [jax.ShapeDtypeStruct((k, C), F32) for k in pouts]
    return pl.pallas_call(
        body, name=name, grid=(C // tc,), in_specs=in_specs, out_specs=out_specs, out_shape=out_shape,
        compiler_params=_params(("parallel",)),
    )(*cols, *vecs)


_G0, _G1 = math.sqrt(2.0 / math.pi), 0.044715


def _gelu(g):
    th = jnp.tanh(_G0 * (g + _G1 * g * g * g))
    return 0.5 * g * (1.0 + th), th


def _ffn_act(gate_pre, up, w, b):
    act, _ = _gelu(_conv(gate_pre, w, b))
    return act * up


def _ffn_act_back(dact, gate_pre, up, w, b):
    g = _conv(gate_pre, w, b)
    ge, th = _gelu(g)
    dge = 0.5 * (1.0 + th) + 0.5 * g * (1.0 - th * th) * _G0 * (1.0 + 3.0 * _G1 * g * g)
    dup = dact * ge
    dgate_pre, dw, db = _conv_back(gate_pre, w, dact * up * dge)
    return dgate_pre, dup, dw, db


def _ssm_act(xbc, w, b):
    c = _conv(xbc, w, b)
    return c * _sigmoid(c)


def _ssm_act_back(dxc, xbc, w, b):
    c = _conv(xbc, w, b)
    sg = _sigmoid(c)
    return _conv_back(xbc, w, dxc * sg * (1.0 + c * (1.0 - sg)))


def _rope_tables(S):
    inv = 1.0 / (ROPE_THETA ** (jnp.arange(0, ROPE, 2, dtype=F32) / ROPE))
    ang = jnp.arange(S, dtype=F32)[:, None] * inv[None, :]
    cos, sin = jnp.cos(ang), jnp.sin(ang)
    return jnp.tile(cos, (1, 4)), jnp.tile(jnp.concatenate([-sin, sin], axis=1), (1, 2))


def _swap_halves(x):
    lane = lax.broadcasted_iota(I32, x.shape, 1)
    w = x.shape[1]
    return jnp.where((lane % ROPE) < ROPE // 2, pltpu.roll(x, w - ROPE // 2, 1), pltpu.roll(x, ROPE // 2, 1))


def _rot(x, cos2, sin2):
    return x * cos2 + _swap_halves(x) * sin2


def _rot_back(dy, cos2, sin2):
    return dy * cos2 + _swap_halves(dy * sin2)


def _mla_pack(cfg, q, kv, kr, cos2, sin2):
    S, H = cfg.S, cfg.H
    ts = _pick(S, 512, 8)

    def body(qn_ref, qr_ref, kn_ref, v_ref, kr_ref, c_ref, s_ref, Q_ref, K_ref, V_ref):
        h = pl.program_id(0)
        c2, s2 = c_ref[...], s_ref[...]
        Q_ref[0, :, 0:LANE] = qn_ref[...].astype(BF16)
        Q_ref[0, :, LANE:] = _rot(qr_ref[...], c2, s2).astype(BF16)
        K_ref[0, :, 0:LANE] = kn_ref[...].astype(BF16)
        krr = _rot(kr_ref[...], c2, s2)
        K_ref[0, :, LANE:] = jnp.where(h % 2 == 1, pltpu.roll(krr, ROPE, 1), krr).astype(BF16)
        V_ref[0] = v_ref[...].astype(BF16)

    blk = lambda f: pl.BlockSpec((ts, LANE), f)
    return pl.pallas_call(
        body, name="mla_pack", grid=(H, S // ts),
        in_specs=[blk(lambda h, i: (i, h)), blk(lambda h, i: (i, H + h // 2)), blk(lambda h, i: (i, h)),
                  blk(lambda h, i: (i, H + h)), blk(lambda h, i: (i, 0)), blk(lambda h, i: (i, 0)), blk(lambda h, i: (i, 0))],
        out_specs=[pl.BlockSpec((1, ts, 2 * LANE), lambda h, i: (h, i, 0)), pl.BlockSpec((1, ts, 2 * LANE), lambda h, i: (h, i, 0)),
                   pl.BlockSpec((1, ts, LANE), lambda h, i: (h, i, 0))],
        out_shape=[jax.ShapeDtypeStruct((H, S, 2 * LANE), BF16), jax.ShapeDtypeStruct((H, S, 2 * LANE), BF16),
                   jax.ShapeDtypeStruct((H, S, LANE), BF16)],
        compiler_params=_params(("parallel", "parallel")),
    )(q, q, kv, kv, kr, cos2, sin2)


def _mla_unpack(cfg, dQ, dK, dV, cos2, sin2):
    S, H = cfg.S, cfg.H
    ts = _pick(S, 256, 8)

    def body(dQ_ref, dK_ref, dV_ref, c_ref, s_ref, dq_ref, dkv_ref, dkr_ref):
        c2, s2 = c_ref[...], s_ref[...]
        lo = lax.broadcasted_iota(I32, (ts, LANE), 1) < ROPE
        tk = jnp.zeros((ts, LANE), F32)
        for h in range(H):
            dq_ref[:, h * LANE:(h + 1) * LANE] = dQ_ref[h, :, 0:LANE].astype(BF16)
            dkv_ref[:, h * LANE:(h + 1) * LANE] = dK_ref[h, :, 0:LANE].astype(BF16)
            dkv_ref[:, (H + h) * LANE:(H + h + 1) * LANE] = dV_ref[h].astype(BF16)
            own = lo if h % 2 == 0 else jnp.logical_not(lo)
            tk = tk + jnp.where(own, dK_ref[h, :, LANE:], 0.0)
        for j in range(H // 2):
            dr = dQ_ref[2 * j, :, LANE:] + dQ_ref[2 * j + 1, :, LANE:]
            dq_ref[:, (H + j) * LANE:(H + j + 1) * LANE] = _rot_back(dr, c2, s2).astype(BF16)
        dkr_rot = jnp.where(lo, tk + pltpu.roll(tk, ROPE, 1), 0.0)
        dkr_ref[...] = _rot_back(dkr_rot, c2, s2).astype(BF16)

    tab = pl.BlockSpec((ts, LANE), lambda i: (i, 0))
    return pl.pallas_call(
        body, name="mla_unpack", grid=(S // ts,),
        in_specs=[pl.BlockSpec((H, ts, 2 * LANE), lambda i: (0, i, 0)), pl.BlockSpec((H, ts, 2 * LANE), lambda i: (0, i, 0)),
                  pl.BlockSpec((H, ts, LANE), lambda i: (0, i, 0)), tab, tab],
        out_specs=[pl.BlockSpec((ts, cfg.QW), lambda i: (i, 0)), pl.BlockSpec((ts, cfg.KVW), lambda i: (i, 0)), tab],
        out_shape=[jax.ShapeDtypeStruct((S, cfg.QW), BF16), jax.ShapeDtypeStruct((S, cfg.KVW), BF16),
                   jax.ShapeDtypeStruct((S, LANE), BF16)],
        compiler_params=_params(("parallel",)),
    )(dQ, dK, dV, cos2, sin2)


_ATT_T = 256
_ATT_HB = 4
_ATT_SCALE = (NOPE + ROPE) ** -0.5


def _diag_mask(transposed=False):
    r = lax.broadcasted_iota(I32, (_ATT_T, _ATT_T), 0) // CHUNK
    c = lax.broadcasted_iota(I32, (_ATT_T, _ATT_T), 1) // CHUNK
    return r <= c if transposed else c <= r


def _row_form(col):
    return jnp.broadcast_to(col, (col.shape[0], LANE)).T[0:8, :]


def _attn_fwd(cfg, Q, K, V):
    S, H, T, HB = cfg.S, cfg.H, _ATT_T, _ATT_HB

    def body(q_ref, k_ref, v_ref, o_ref, lse_ref, lse_t_ref):
        qi = pl.program_id(1)

        def head_step(b, kb, carry, mask):
            m, l, acc = carry
            ks = pl.multiple_of(kb * T, T)
            s = _dot(q_ref[b], k_ref[b, pl.ds(ks, T), :], NT) * _ATT_SCALE
            if mask is not None:
                s = jnp.where(mask, s, -1e30)
            m_new = jnp.maximum(m, jnp.max(s, axis=1, keepdims=True))
            p = jnp.exp(s - m_new)
            alpha = jnp.exp(m - m_new)
            l = alpha * l + jnp.sum(p, axis=1, keepdims=True)
            acc = alpha * acc + _dot(p.astype(BF16), v_ref[b, pl.ds(ks, T), :])
            return m_new, l, acc

        def step(kb, carry, mask=None):
            return tuple(head_step(b, kb, carry[b], mask) for b in range(HB))

        init = (jnp.full((T, 1), -1e30, F32), jnp.zeros((T, 1), F32), jnp.zeros((T, VH), F32))
        done = step(qi, lax.fori_loop(0, qi, step, (init,) * HB), _diag_mask())
        for b, (m, l, acc) in enumerate(done):
            o_ref[:, b * LANE:(b + 1) * LANE] = acc / l
            lse = m + jnp.log(l)
            lse_ref[:, b * LANE:(b + 1) * LANE] = jnp.broadcast_to(lse, (T, LANE))
            lse_t_ref[b] = _row_form(lse)

    return pl.pallas_call(
        body, name="attn_fwd", grid=(H // HB, S // T),
        in_specs=[pl.BlockSpec((HB, T, 2 * LANE), lambda h, i: (h, i, 0)), pl.BlockSpec((HB, S, 2 * LANE), lambda h, i: (h, 0, 0)),
                  pl.BlockSpec((HB, S, LANE), lambda h, i: (h, 0, 0))],
        out_specs=[pl.BlockSpec((T, HB * LANE), lambda h, i: (i, h)), pl.BlockSpec((T, HB * LANE), lambda h, i: (i, h)),
                   pl.BlockSpec((HB, 8, T), lambda h, i: (h, 0, i))],
        out_shape=[jax.ShapeDtypeStruct((S, H * LANE), F32), jax.ShapeDtypeStruct((S, H * LANE), F32),
                   jax.ShapeDtypeStruct((H, 8, S), F32)],
        compiler_params=_params(("parallel", "parallel")),
    )(Q, K, V)


def _attn_dq(cfg, Q, K, V, do, o, lse, after):
    S, H, T, HB = cfg.S, cfg.H, _ATT_T, _ATT_HB

    def body(q_ref, k_ref, v_ref, do_ref, o_ref, lse_ref, after_ref, dq_ref, dl_t_ref):
        qi = pl.program_id(1)
        do = [do_ref[:, b * LANE:(b + 1) * LANE] for b in range(HB)]
        delta = [jnp.sum(do[b] * o_ref[:, b * LANE:(b + 1) * LANE], axis=1, keepdims=True) for b in range(HB)]
        dob = [d.astype(BF16) for d in do]

        def head_step(b, kb, dq, mask):
            ks = pl.multiple_of(kb * T, T)
            k = k_ref[b, pl.ds(ks, T), :]
            s = _dot(q_ref[b], k, NT) * _ATT_SCALE
            if mask is not None:
                s = jnp.where(mask, s, -1e30)
            p = jnp.exp(s - lse_ref[:, b * LANE:b * LANE + 1])
            dp = _dot(dob[b], v_ref[b, pl.ds(ks, T), :], NT)
            ds = p * (dp - delta[b]) * _ATT_SCALE
            return dq + _dot(ds.astype(BF16), k)

        def step(kb, dqs, mask=None):
            return tuple(head_step(b, kb, dqs[b], mask) for b in range(HB))

        dqs = step(qi, lax.fori_loop(0, qi, step, (jnp.zeros((T, 2 * LANE), F32),) * HB), _diag_mask())
        for b in range(HB):
            dq_ref[b] = dqs[b]
            dl_t_ref[b] = _row_form(delta[b])

    col = pl.BlockSpec((T, HB * LANE), lambda h, i: (i, h))
    return pl.pallas_call(
        body, name="attn_dq", grid=(H // HB, S // T),
        in_specs=[pl.BlockSpec((HB, T, 2 * LANE), lambda h, i: (h, i, 0)), pl.BlockSpec((HB, S, 2 * LANE), lambda h, i: (h, 0, 0)),
                  pl.BlockSpec((HB, S, LANE), lambda h, i: (h, 0, 0)), col, col, col, _ANY],
        out_specs=[pl.BlockSpec((HB, T, 2 * LANE), lambda h, i: (h, i, 0)), pl.BlockSpec((HB, 8, T), lambda h, i: (h, 0, i))],
        out_shape=[jax.ShapeDtypeStruct((H, S, 2 * LANE), F32), jax.ShapeDtypeStruct((H, 8, S), F32)],
        compiler_params=_params(("parallel", "parallel")),
    )(Q, K, V, do, o, lse, after)


def _attn_dkv(cfg, Q, K, V, do, lse_t, delta_t):
    S, H, T, HB = cfg.S, cfg.H, _ATT_T, _ATT_HB
    nq = S // T

    def body(q_ref, k_ref, v_ref, do_ref, lse_ref, dl_ref, dk_ref, dv_ref):
        kb = pl.program_id(1)

        def head_step(b, qi, carry, mask):
            dk, dv = carry
            qs = pl.multiple_of(qi * T, T)
            q = q_ref[b, pl.ds(qs, T), :]
            dob = do_ref[pl.ds(qs, T), b * LANE:(b + 1) * LANE].astype(BF16)
            s = _dot(k_ref[b], q, NT) * _ATT_SCALE
            if mask is not None:
                s = jnp.where(mask, s, -1e30)
            p = jnp.exp(s - lse_ref[b, 0:1, pl.ds(qs, T)])
            dv = dv + _dot(p.astype(BF16), dob)
            dp = _dot(v_ref[b], dob, NT)
            ds = p * (dp - dl_ref[b, 0:1, pl.ds(qs, T)]) * _ATT_SCALE
            dk = dk + _dot(ds.astype(BF16), q)
            return dk, dv

        def step(qi, carry, mask=None):
            return tuple(head_step(b, qi, carry[b], mask) for b in range(HB))

        zero = (jnp.zeros((T, 2 * LANE), F32), jnp.zeros((T, VH), F32))
        done = lax.fori_loop(kb + 1, nq, step, step(kb, (zero,) * HB, _diag_mask(transposed=True)))
        for b, (dk, dv) in enumerate(done):
            dk_ref[b] = dk
            dv_ref[b] = dv

    row = pl.BlockSpec((HB, 8, S), lambda h, j: (h, 0, 0))
    return pl.pallas_call(
        body, name="attn_dkv", grid=(H // HB, S // T),
        in_specs=[pl.BlockSpec((HB, S, 2 * LANE), lambda h, j: (h, 0, 0)), pl.BlockSpec((HB, T, 2 * LANE), lambda h, j: (h, j, 0)),
                  pl.BlockSpec((HB, T, LANE), lambda h, j: (h, j, 0)), pl.BlockSpec((S, HB * LANE), lambda h, j: (0, h)), row, row],
        out_specs=[pl.BlockSpec((HB, T, 2 * LANE), lambda h, j: (h, j, 0)), pl.BlockSpec((HB, T, LANE), lambda h, j: (h, j, 0))],
        out_shape=[jax.ShapeDtypeStruct((H, S, 2 * LANE), F32), jax.ShapeDtypeStruct((H, S, LANE), F32)],
        compiler_params=_params(("parallel", "parallel")),
    )(Q, K, V, do, lse_t, delta_t)


def _expand_matrix(cfg):
    r = lax.broadcasted_iota(I32, (LANE, cfg.INNER), 0)
    c = lax.broadcasted_iota(I32, (LANE, cfg.INNER), 1)
    return (r == c // HP).astype(F32)


def _softplus(x):
    return jnp.maximum(x, 0.0) + jnp.log(1.0 + jnp.exp(-jnp.abs(x)))


def _ssd_prep(cfg, dt_raw, dt_bias_pad, a_log_pad, expand):
    HS = cfg.HS

    def fn(raw, bias, alog, E):
        heads = lax.broadcasted_iota(I32, raw.shape, 1) < HS
        dt = jnp.where(heads, _softplus(raw + bias), 0.0)
        a = dt * jnp.where(heads[0:1], -jnp.exp(alog), 0.0)
        return dt, a, _dot(dt, E, precision=HI), _dot(a, E, precision=HI)

    return _rowwise("ssd_prep", fn, [dt_raw], [dt_bias_pad, a_log_pad, expand],
                    [(LANE, F32), (LANE, F32), (cfg.INNER, F32), (cfg.INNER, F32)], [], _pick(cfg.S, 512, 8))


def _tril(T):
    return lax.broadcasted_iota(I32, (T, T), 0) >= lax.broadcasted_iota(I32, (T, T), 1)


def _ssd_fwd(cfg, xc, dt_exp, a_exp, a_small, dskip_exp):
    S, T, INNER, G, NPAIR = cfg.S, cfg.T, cfg.INNER, cfg.G, cfg.NPAIR
    NC = S // T

    def body(xc_ref, dte_ref, ae_ref, as_ref, dsk_ref, y_ref, hin_ref, ht_ref):
        @pl.when(pl.program_id(0) == 0)
        def _():
            ht_ref[...] = jnp.zeros_like(ht_ref)

        tril = _tril(T)
        tri = tril.astype(F32)
        acs_s = _dot(tri, as_ref[...], precision=HI)
        acs_e = _dot(tri, ae_ref[...], precision=HI)
        acs_t = acs_s.T
        lo = lax.broadcasted_iota(I32, (T, LANE), 1) < HP
        for g in range(G):
            Bb = xc_ref[:, INNER + g * NST:INNER + (g + 1) * NST].astype(BF16)
            Cb = xc_ref[:, INNER + (G + g) * NST:INNER + (G + g + 1) * NST].astype(BF16)
            Gm = _dot(Cb, Bb, NT)
            for j in range(g * NPAIR // G, (g + 1) * NPAIR // G):
                sl = slice(j * LANE, (j + 1) * LANE)
                Xp = xc_ref[:, sl]
                Xdt = Xp * dte_ref[:, sl]
                Xb = Xdt.astype(BF16)
                acs_p = acs_e[:, sl]
                last = acs_p[T - 1:T, :]
                Hin = ht_ref[j]
                hin_ref[0, j] = Hin
                yd = []
                for e in (0, 1):
                    h = 2 * j + e
                    Lm = jnp.exp(jnp.where(tril, acs_s[:, h:h + 1] - acs_t[h:h + 1, :], -1e30))
                    yd.append(_dot((Gm * Lm).astype(BF16), Xb))
                y_off = _dot(Cb, Hin.astype(BF16)) * jnp.exp(acs_p)
                y_ref[:, sl] = jnp.where(lo, yd[0], yd[1]) + y_off + Xp * dsk_ref[:, sl]
                st = _dot(Bb, (Xdt * jnp.exp(last - acs_p)).astype(BF16), TN)
                ht_ref[j] = jnp.exp(last) * Hin + st

    rows = lambda w: pl.BlockSpec((T, w), lambda c: (c, 0))
    return pl.pallas_call(
        body, name="ssd_fwd", grid=(NC,),
        in_specs=[rows(cfg.CONVCH), rows(INNER), rows(INNER), rows(LANE), pl.BlockSpec((1, INNER), lambda c: (0, 0))],
        out_specs=[rows(INNER), pl.BlockSpec((1, NPAIR, NST, LANE), lambda c: (c, 0, 0, 0))],
        out_shape=[jax.ShapeDtypeStruct((S, INNER), F32), jax.ShapeDtypeStruct((NC, NPAIR, NST, LANE), F32)],
        scratch_shapes=[pltpu.VMEM((NPAIR, NST, LANE), F32)],
        compiler_params=_params(("arbitrary",)),
    )(xc, dt_exp, a_exp, a_small, dskip_exp)


def _ssd_bwd(cfg, dy, xc, dt_exp, a_exp, a_small, dskip_exp, hin, dt_raw, dt_bias_pad, a_log_pad, expand):
    S, T, INNER, G, NPAIR, HS = cfg.S, cfg.T, cfg.INNER, cfg.G, cfg.NPAIR, cfg.HS
    NC = S // T

    def body(dy_ref, xc_ref, dte_ref, ae_ref, as_ref, dsk_ref, hin_ref, raw_ref, bias_ref, alog_ref, e_ref,
             dxc_ref, draw_ref, dbias_ref, dalog_ref, dskip_ref, dht_ref, cols_ref, rows_ref, dacs_ref, ddt_ref):
        first = pl.program_id(0) == 0

        @pl.when(first)
        def _():
            dht_ref[...] = jnp.zeros_like(dht_ref)

        tril = _tril(T)
        tri = tril.astype(F32)
        a_s = as_ref[...]
        acs_s = _dot(tri, a_s, precision=HI)
        acs_e = _dot(tri, ae_ref[...], precision=HI)
        acs_t = acs_s.T
        lo = lax.broadcasted_iota(I32, (T, LANE), 1) < HP
        last_row = lax.broadcasted_iota(I32, (T, LANE), 0) == T - 1
        cols_ref[...] = jnp.zeros_like(cols_ref)
        rows_ref[...] = jnp.zeros_like(rows_ref)
        dsk_parts = []
        for g in range(G):
            bsl = slice(INNER + g * NST, INNER + (g + 1) * NST)
            csl = slice(INNER + (G + g) * NST, INNER + (G + g + 1) * NST)
            Bb = xc_ref[:, bsl].astype(BF16)
            Cb = xc_ref[:, csl].astype(BF16)
            Gm = _dot(Cb, Bb, NT)
            dG = jnp.zeros((T, T), F32)
            dB = jnp.zeros((T, NST), F32)
            dC = jnp.zeros((T, NST), F32)
            for j in range(g * NPAIR // G, (g + 1) * NPAIR // G):
                sl = slice(j * LANE, (j + 1) * LANE)
                Xp = xc_ref[:, sl]
                dtp = dte_ref[:, sl]
                Xdt = Xp * dtp
                Xb = Xdt.astype(BF16)
                acs_p = acs_e[:, sl]
                last = acs_p[T - 1:T, :]
                e_p, dec, cd = jnp.exp(acs_p), jnp.exp(last - acs_p), jnp.exp(last)
                Hin = hin_ref[0, j]
                Hb = Hin.astype(BF16)
                dHn = dht_ref[j]
                dHb = dHn.astype(BF16)
                dYp = dy_ref[:, sl]
                z = _dot(Cb, Hb)
                dz = (dYp * e_p).astype(BF16)
                dacs_p = dYp * z * e_p
                dC = dC + _dot(dz, Hb, NT)
                dHin = _dot(Cb, dz, TN) + cd * dHn
                dlast = _colsum(dHn * Hin) * cd
                qv = _dot(Bb, dHb)
                dXdt = qv * dec
                ddec = qv * Xdt * dec
                dacs_p = dacs_p - ddec
                dlast = dlast + _colsum(ddec)
                dB = dB + _dot((Xdt * dec).astype(BF16), dHb, NT)
                for e in (0, 1):
                    h = 2 * j + e
                    Lm = jnp.exp(jnp.where(tril, acs_s[:, h:h + 1] - acs_t[h:h + 1, :], -1e30))
                    Mh = Gm * Lm
                    dYe = jnp.where(lo if e == 0 else jnp.logical_not(lo), dYp, 0.0).astype(BF16)
                    dM = _dot(dYe, Xb, NT)
                    dXdt = dXdt + _dot(Mh.astype(BF16), dYe, TN)
                    W = dM * Mh
                    cols_ref[:, h:h + 1] = jnp.sum(W, axis=1, keepdims=True)
                    rows_ref[h:h + 1, :] = _colsum(W)
                    dG = dG + dM * Lm
                dacs_ref[:, sl] = dacs_p + jnp.where(last_row, dlast, 0.0)
                ddt_ref[:, sl] = dXdt * Xp
                dxc_ref[:, sl] = dXdt * dtp + dYp * dsk_ref[:, sl]
                dsk_parts.append(_colsum(dYp * Xp))
                dht_ref[j] = dHin
            dGb = dG.astype(BF16)
            dxc_ref[:, bsl] = dB + _dot(dGb, Cb, TN)
            dxc_ref[:, csl] = dC + _dot(dGb, Bb)
        E = e_ref[...]
        dacs_s = cols_ref[...] - rows_ref[...].T + _dot(dacs_ref[...], E, NT, precision=HI)
        da = _dot(tri, dacs_s, TN, precision=HI)
        heads = lax.broadcasted_iota(I32, (1, LANE), 1) < HS
        A = jnp.where(heads, -jnp.exp(alog_ref[...]), 0.0)
        ddt = _dot(ddt_ref[...], E, NT, precision=HI) + da * A
        draw = jnp.where(heads, ddt * _sigmoid(raw_ref[...] + bias_ref[...]), 0.0)
        draw_ref[...] = draw
        dsk = _dot(jnp.broadcast_to(jnp.concatenate(dsk_parts, axis=1), (8, INNER)), E, NT, precision=HI)[0:1]
        for ref, val in ((dbias_ref, _colsum(draw)), (dalog_ref, _colsum(da * a_s)), (dskip_ref, dsk)):
            @pl.when(first)
            def _():
                ref[...] = val

            @pl.when(jnp.logical_not(first))
            def _():
                ref[...] += val

    rows = lambda w: pl.BlockSpec((T, w), lambda c: (NC - 1 - c, 0))
    vec = lambda w: pl.BlockSpec((1, w), lambda c: (0, 0))
    return pl.pallas_call(
        body, name="ssd_bwd", grid=(NC,),
        in_specs=[rows(INNER), rows(cfg.CONVCH), rows(INNER), rows(INNER), rows(LANE), vec(INNER),
                  pl.BlockSpec((1, NPAIR, NST, LANE), lambda c: (NC - 1 - c, 0, 0, 0)), rows(LANE), vec(LANE), vec(LANE),
                  pl.BlockSpec((LANE, INNER), lambda c: (0, 0))],
        out_specs=[rows(cfg.CONVCH), rows(LANE), vec(LANE), vec(LANE), vec(LANE)],
        out_shape=[jax.ShapeDtypeStruct((S, cfg.CONVCH), F32), jax.ShapeDtypeStruct((S, LANE), F32)]
        + [jax.ShapeDtypeStruct((1, LANE), F32)] * 3,
        scratch_shapes=[pltpu.VMEM((NPAIR, NST, LANE), F32), pltpu.VMEM((T, LANE), F32), pltpu.VMEM((LANE, T), F32),
                        pltpu.VMEM((T, INNER), F32), pltpu.VMEM((T, INNER), F32)],
        compiler_params=_params(("arbitrary",)),
    )(dy, xc, dt_exp, a_exp, a_small, dskip_exp, hin, dt_raw, dt_bias_pad, a_log_pad, expand)


def _ssd_post(cfg, y, z, norm_g):
    W = cfg.INNER // cfg.G

    def fn(y, z, g):
        yz = y * z * _sigmoid(z)
        return jnp.concatenate([yz[:, i * W:(i + 1) * W] * _rs(yz[:, i * W:(i + 1) * W]) for i in range(cfg.G)], axis=1) * g

    return _rowwise("ssd_post", fn, [y, z], [norm_g], [(cfg.INNER, BF16)], [], _pick(cfg.S, 256, 8))[0]


def _ssd_post_bwd(cfg, db, y, z, norm_g):
    W = cfg.INNER // cfg.G

    def fn(db, y, z, g):
        sg = _sigmoid(z)
        yz = y * z * sg
        dn = db * g
        dyz, nh = [], []
        for i in range(cfg.G):
            seg = yz[:, i * W:(i + 1) * W]
            r = _rs(seg)
            nh.append(seg * r)
            dyz.append(_rms_back(nh[-1], r, dn[:, i * W:(i + 1) * W]))
        dyz = jnp.concatenate(dyz, axis=1)
        return dyz * z * sg, dyz * y * sg * (1.0 + z * (1.0 - sg)), _colsum(db * jnp.concatenate(nh, axis=1))

    return _rowwise("ssd_post_bwd", fn, [db, y, z], [norm_g], [(cfg.INNER, F32), (cfg.INNER, F32)], [(1, cfg.INNER)],
                    _pick(cfg.S, 256, 8))


def _local_grads(cfg, x, tgt, W, sp, mla_weights=None, out_weight=None, ffn_weights=None, ffn_grads_ready=None,
                 early_grads_ready=None, in_grad_ready=None):
    S, D, H, INNER = cfg.S, cfg.D, cfg.H, cfg.INNER
    ts = _pick(S, 256, 8)
    tc = 256

    xn = _rowwise("rms_pre", lambda x, g: x * _rs(x) * g, [x], [sp["mix_pre_g"]], [(D, BF16)], [], ts)[0]
    u = _matmul("mm_in", xn, W["w_in"], "nt", F32)
    c_q, c_kv = u[:, :cfg.QL], u[:, cfg.QL:cfg.o_kr]
    kr = u[:, cfg.o_kr:cfg.o_z]
    z = u[:, cfg.o_z:cfg.o_xbc]
    xbc = u[:, cfg.o_xbc:cfg.o_dt]
    dt_raw = u[:, cfg.o_dt:]

    if mla_weights is not None:
        sp = dict(sp, q_norm_g=sp["q_norm_g"] + mla_weights.pass_on(u)[0, 0])
    cqn = _rowwise("rms_q", lambda x, g: x * _rs(x) * g, [c_q], [sp["q_norm_g"]], [(cfg.QL, BF16)], [], ts)[0]
    ckvn = _rowwise("rms_kv", lambda x, g: x * _rs(x) * g, [c_kv], [sp["kv_norm_g"]], [(cfg.KVL, BF16)], [], ts)[0]
    if mla_weights is not None:
        W = dict(W, **mla_weights.arrived(ckvn))
    q = _matmul("mm_uq", cqn, W["w_uq"], "nn", F32)
    kv = _matmul("mm_ukv", ckvn, W["w_ukv"], "nn", F32)
    cos2, sin2 = _rope_tables(S)
    Qh, Kh, Vh = _mla_pack(cfg, q, kv, kr, cos2, sin2)
    a_out, lse, lse_t = _attn_fwd(cfg, Qh, Kh, Vh)
    if out_weight is not None:
        sp = dict(sp, ssm_conv_b=sp["ssm_conv_b"] + out_weight.pass_on(a_out)[0, 0])

    pad = lambda v: jnp.pad(v, ((0, 0), (0, LANE - v.shape[1])))
    expand = _expand_matrix(cfg)
    dt_bias_pad, a_log_pad = pad(sp["dt_bias"]), pad(sp["a_log"])
    dskip_exp = jnp.repeat(sp["d_skip"], HP, axis=1)
    xc = _colwise("ssm_act", _ssm_act, [xbc], [sp["ssm_conv_w"], sp["ssm_conv_b"]], [F32], [], tc)[0]
    dt_s, a_s, dt_exp, a_exp = _ssd_prep(cfg, dt_raw, dt_bias_pad, a_log_pad, expand)
    y_ssd, hin = _ssd_fwd(cfg, xc, dt_exp, a_exp, a_s, dskip_exp)
    b_out = _ssd_post(cfg, y_ssd, z, sp["ssm_norm_g"])

    ab_out = jnp.concatenate([a_out.astype(BF16), b_out], axis=1)
    if out_weight is not None:
        W = dict(W, **out_weight.arrived(ab_out))
    if ffn_weights is not None:
        sp = dict(sp, mix_post_g=sp["mix_post_g"] + ffn_weights.pass_on(ab_out)[0, 0])
    mix = _matmul("mm_out", ab_out, W["w_out"], "nn", F32)

    def mid(x, mix, g_mp, g_fp):
        x1 = x + mix * _rs(mix) * g_mp
        return x1, x1 * _rs(x1) * g_fp

    x1, h2 = _rowwise("fwd_mid", mid, [x, mix], [sp["mix_post_g"], sp["ffn_pre_g"]], [(D, F32), (D, BF16)], [], ts)
    if ffn_weights is not None:
        W = dict(W, **ffn_weights.arrived(h2))
    gate_pre = _matmul("mm_gate", h2, W["w_gate"], "nn", F32, chips=True)
    up = _matmul("mm_up", h2, W["w_up"], "nn", F32, chips=True)
    act = _colwise("ffn_act", _ffn_act, [gate_pre, up], [sp["ffn_conv_w"], sp["ffn_conv_b"]], [BF16], [], tc)[0]
    f = _matmul("mm_down", act, W["w_down"], "nn", F32)

    def final(x1, f, t, g):
        r = _rs(f)
        fh = f * r
        err = x1 + fh * g - t
        loss = 0.5 * jnp.sum(jnp.mean(err * err, axis=-1, keepdims=True), axis=0, keepdims=True)
        dy = err * (1.0 / D)
        return dy, _rms_back(fh, r, dy * g), _colsum(dy * fh), loss

    dy, df, g_ffn_post, loss = _rowwise("final", final, [x1, f, tgt], [sp["ffn_post_g"]], [(D, F32), (D, BF16)],
                                        [(1, D), (1, LANE)], ts)
    gW = {}
    dact = _matmul("mm_down_dx", df, W["w_down"], "nt", F32)
    gW["w_down"] = _matmul("mm_down_dw", act, df, "tn", BF16)
    dgate, dup, g_ffn_conv_w, g_ffn_conv_b = _colwise(
        "ffn_act_bwd", _ffn_act_back, [dact, gate_pre, up], [sp["ffn_conv_w"], sp["ffn_conv_b"]], [BF16, BF16], [FFN_K, 1], tc)
    gW["w_gate"] = _matmul("mm_gate_dw", h2, dgate, "tn", BF16, chips=True)
    gW["w_up"] = _matmul("mm_up_dw", h2, dup, "tn", BF16, chips=True)
    if ffn_grads_ready is not None:
        sp = dict(sp, ffn_pre_g=sp["ffn_pre_g"] + ffn_grads_ready({n: gW[n] for n in ("w_down", "w_gate", "w_up")})[0, 0])
    dh2 = _matmul("mm_gu_dx", dgate, W["w_gate"], "nt", F32, dup, W["w_up"], chips=True)

    def mid_back(dy, dh2, x1, mix, g_mp, g_fp):
        r2 = _rs(x1)
        xh = x1 * r2
        dx1 = dy + _rms_back(xh, r2, dh2 * g_fp)
        r1 = _rs(mix)
        mh = mix * r1
        return dx1, _rms_back(mh, r1, dx1 * g_mp), _colsum(dh2 * xh), _colsum(dx1 * mh)

    dx1, dmix, g_ffn_pre, g_mix_post = _rowwise("bwd_mid", mid_back, [dy, dh2, x1, mix], [sp["mix_post_g"], sp["ffn_pre_g"]],
                                                [(D, F32), (D, BF16)], [(1, D), (1, D)], ts)
    dab_out = _matmul("mm_out_dx", dmix, W["w_out"], "nt", F32)
    db_out = dab_out[:, cfg.MLAW:]
    gW["w_out"] = _matmul("mm_out_dw", ab_out, dmix, "tn", BF16)
    early_token = jnp.zeros((8, LANE), F32)
    if early_grads_ready is not None:
        early_token = early_grads_ready({n: gW[n] for n in ("w_down", "w_gate", "w_up", "w_out")})
        sp = dict(sp, ssm_norm_g=sp["ssm_norm_g"] + early_token[0, 0])

    dy_ssd, dz, g_ssm_norm = _ssd_post_bwd(cfg, db_out, y_ssd, z, sp["ssm_norm_g"])
    dxc, ddt_raw, g_dt_bias, g_a_log, g_d_skip = _ssd_bwd(cfg, dy_ssd, xc, dt_exp, a_exp, a_s, dskip_exp, hin, dt_raw,
                                                          dt_bias_pad, a_log_pad, expand)
    dxbc, g_ssm_conv_w, g_ssm_conv_b = _colwise("ssm_act_bwd", _ssm_act_back, [dxc, xbc], [sp["ssm_conv_w"], sp["ssm_conv_b"]],
                                                [BF16], [SSM_K, 1], tc)

    dQ, delta_t = _attn_dq(cfg, Qh, Kh, Vh, dab_out, a_out, lse, early_token)
    dK, dV = _attn_dkv(cfg, Qh, Kh, Vh, dab_out, lse_t, delta_t)
    dq, dkv, dkr = _mla_unpack(cfg, dQ, dK, dV, cos2, sin2)
    dcqn = _matmul("mm_uq_dx", dq, W["w_uq"], "nt", F32)
    dckvn = _matmul("mm_ukv_dx", dkv, W["w_ukv"], "nt", F32)
    gW["w_uq"] = _matmul("mm_uq_dw", cqn, dq, "tn", BF16)
    gW["w_ukv"] = _matmul("mm_ukv_dw", ckvn, dkv, "tn", BF16)

    def rms_back(x, dy, g):
        r = _rs(x)
        xh = x * r
        return _rms_back(xh, r, dy * g), _colsum(dy * xh)

    dc_q, g_q_norm = _rowwise("rms_q_bwd", rms_back, [c_q, dcqn], [sp["q_norm_g"]], [(cfg.QL, BF16)], [(1, cfg.QL)], ts)
    dc_kv, g_kv_norm = _rowwise("rms_kv_bwd", rms_back, [c_kv, dckvn], [sp["kv_norm_g"]], [(cfg.KVL, BF16)], [(1, cfg.KVL)], ts)

    du = jnp.concatenate([dc_q, dc_kv, dkr, dz.astype(BF16), dxbc, ddt_raw.astype(BF16)], axis=1)
    gW["w_in"] = _matmul("mm_in_dw", du, xn, "tn", BF16)
    if in_grad_ready is not None:
        token = in_grad_ready({n: gW[n] for n in ("w_in", "w_uq", "w_ukv")})
        sp = dict(sp, mix_pre_g=sp["mix_pre_g"] + token[0, 0])
    dxn = _matmul("mm_in_dx", du, W["w_in"], "nn", F32)

    def first_back(dx1, dxn, x, g):
        r = _rs(x)
        xh = x * r
        return dx1 + _rms_back(xh, r, dxn * g), _colsum(dxn * xh)

    grad_x, g_mix_pre = _rowwise("bwd_first", first_back, [dx1, dxn, x], [sp["mix_pre_g"]], [(D, F32)], [(1, D)], ts)

    gs = dict(mix_pre_g=g_mix_pre, q_norm_g=g_q_norm, kv_norm_g=g_kv_norm, ssm_conv_w=g_ssm_conv_w, ssm_conv_b=g_ssm_conv_b,
              dt_bias=g_dt_bias[:, :cfg.HS], a_log=g_a_log[:, :cfg.HS], d_skip=g_d_skip[:, :cfg.HS], ssm_norm_g=g_ssm_norm,
              mix_post_g=g_mix_post, ffn_pre_g=g_ffn_pre, ffn_conv_w=g_ffn_conv_w, ffn_conv_b=g_ffn_conv_b,
              ffn_post_g=g_ffn_post)
    return loss, grad_x, gW, gs


def _to_kernel_layout(cfg, name, w):
    if name == "w_in":
        a = cfg.o_kr + ROPE
        return jnp.concatenate([w[:a], jnp.zeros((LANE - ROPE, w.shape[1]), w.dtype), w[a:],
                                jnp.zeros((LANE - cfg.HS, w.shape[1]), w.dtype)], axis=0)
    if name in ("w_uq", "w_ukv"):
        per = NOPE + (ROPE if name == "w_uq" else VH)
        return jnp.concatenate([w[:, h * per:h * per + NOPE] for h in range(cfg.H)]
                               + [w[:, h * per + NOPE:(h + 1) * per] for h in range(cfg.H)], axis=1)
    return w


def _from_kernel_layout(cfg, name, g):
    if name == "w_in":
        return jnp.concatenate([g[:cfg.o_kr + ROPE], g[cfg.o_z:cfg.o_dt + cfg.HS]], axis=0)
    if name in ("w_uq", "w_ukv"):
        second = ROPE if name == "w_uq" else VH
        base = cfg.H * NOPE
        parts = []
        for h in range(cfg.H):
            parts += [g[:, h * NOPE:(h + 1) * NOPE], g[:, base + h * second:base + (h + 1) * second]]
        return jnp.concatenate(parts, axis=1)
    return g


def _cols_to_chips(w):
    r, c = w.shape
    return w.reshape(r, N_CHIPS, c // N_CHIPS).transpose(1, 0, 2)


def _chips_to_cols(g):
    k, r, cs = g.shape
    return g.transpose(1, 0, 2).reshape(r, k * cs)


_CHIP_MAJOR = ("w_gate", "w_up")
_RELAYOUT = ("w_uq", "w_ukv")
_LAYOUT_ROWS = 256


def _w_in_layout(cfg, wg):
    _, rs, d = wg.shape
    tc = _pick(d, _LAYOUT_ROWS, LANE)

    def body(w_ref, o_ref):
        o_ref[...] = _to_kernel_layout(cfg, "w_in", jnp.concatenate([w_ref[k] for k in range(N_CHIPS)], axis=0))

    return pl.pallas_call(
        body, name="layout_w_in", grid=(d // tc,),
        in_specs=[pl.BlockSpec((N_CHIPS, rs, tc), lambda j: (0, 0, j))], out_specs=pl.BlockSpec((cfg.EXT, tc), lambda j: (0, j)),
        out_shape=jax.ShapeDtypeStruct((cfg.EXT, d), wg.dtype), compiler_params=_params(("parallel",)),
    )(wg)


def _w_in_grad_to_chips(cfg, g):
    _, d = g.shape
    rs = cfg.IN_COLS // N_CHIPS
    tc = _pick(d, _LAYOUT_ROWS, LANE)

    def body(g_ref, o_ref):
        nat = _from_kernel_layout(cfg, "w_in", g_ref[...])
        for k in range(N_CHIPS):
            o_ref[k] = nat[k * rs:(k + 1) * rs]

    return pl.pallas_call(
        body, name="layout_grad_w_in", grid=(d // tc,),
        in_specs=[pl.BlockSpec((cfg.EXT, tc), lambda j: (0, j))], out_specs=pl.BlockSpec((N_CHIPS, rs, tc), lambda j: (0, 0, j)),
        out_shape=jax.ShapeDtypeStruct((N_CHIPS, rs, d), g.dtype), compiler_params=_params(("parallel",)),
    )(g)


def _gathered_to_kernel(cfg, name, wg):
    if name in _CHIP_MAJOR:
        return wg
    if name == "w_in":
        return _w_in_layout(cfg, wg)
    if name not in _RELAYOUT:
        return wg.reshape(wg.shape[0] * wg.shape[1], wg.shape[2])
    _, rows, cs = wg.shape
    tr = _pick(rows, _LAYOUT_ROWS, 16)

    def body(w_ref, o_ref):
        o_ref[...] = _to_kernel_layout(cfg, name, jnp.concatenate([w_ref[k] for k in range(N_CHIPS)], axis=1))

    wide = jax.eval_shape(lambda w: _to_kernel_layout(cfg, name, w), jax.ShapeDtypeStruct((rows, N_CHIPS * cs), wg.dtype)).shape[1]
    return pl.pallas_call(
        body, name="layout_" + name, grid=(rows // tr,),
        in_specs=[pl.BlockSpec((N_CHIPS, tr, cs), lambda i: (0, i, 0))], out_specs=pl.BlockSpec((tr, wide), lambda i: (i, 0)),
        out_shape=jax.ShapeDtypeStruct((rows, wide), wg.dtype), compiler_params=_params(("parallel",)),
    )(wg)


def _grad_to_chips(cfg, name, g):
    if name in _CHIP_MAJOR:
        return g
    if name == "w_in":
        return _w_in_grad_to_chips(cfg, g)
    if name not in _RELAYOUT:
        return g.reshape(N_CHIPS, g.shape[0] // N_CHIPS, g.shape[1])
    rows, wide = g.shape
    tr = _pick(rows, _LAYOUT_ROWS, 16)
    cs = jax.eval_shape(lambda v: _from_kernel_layout(cfg, name, v), g).shape[1] // N_CHIPS

    def body(g_ref, o_ref):
        nat = _from_kernel_layout(cfg, name, g_ref[...])
        for k in range(N_CHIPS):
            o_ref[k] = nat[:, k * cs:(k + 1) * cs]

    return pl.pallas_call(
        body, name="layout_grad_" + name, grid=(rows // tr,),
        in_specs=[pl.BlockSpec((tr, wide), lambda i: (i, 0))], out_specs=pl.BlockSpec((N_CHIPS, tr, cs), lambda i: (0, i, 0)),
        out_shape=jax.ShapeDtypeStruct((N_CHIPS, rows, cs), g.dtype), compiler_params=_params(("parallel",)),
    )(g)


def _me():
    return lax.axis_index("x"), lax.axis_index("y"), lax.axis_index("c")


def _other_chips(x, y):
    return [(1 - x, y), (x, 1 - y), (1 - x, 1 - y)]


_ANY = pl.BlockSpec(memory_space=pl.ANY)


def _row_block(rows, cols, mult):
    return _pick(rows, max(mult, (1 << 19) // cols // mult * mult), mult)


def _scalar(v):
    return v.astype(I32).reshape(1)


def _blocks2d(r, c, mult):
    if r % mult == 0:
        tr = _row_block(r, c, mult)
        return (tr, c), r // tr, lambda i: (i, 0)
    tc = _pick(c, max(LANE, (1 << 19) // r // LANE * LANE), LANE)
    return (r, tc), c // tc, lambda i: (0, i)


def _by_rows(rows):
    return rows % 32 == 0


def _half_shape(rows, cols):
    return (rows // 2, cols) if _by_rows(rows) else (rows, cols // 2)


def _half_blocks(rows, cols, mult):
    hr, hc = _half_shape(rows, cols)
    block, n, part = _blocks2d(hr, hc, mult)
    assert (hr % mult == 0) == _by_rows(rows), (rows, cols, mult)
    full = (lambda h, i: (h * n + i, 0)) if _by_rows(rows) else (lambda h, i: (0, h * n + i))
    return block, n, full, part


def _half(ref, k, half):
    hr, hc = _half_shape(ref.shape[1], ref.shape[2])
    if _by_rows(ref.shape[1]):
        return ref.at[k, pl.ds(pl.multiple_of(half * hr, 16), hr), :]
    return ref.at[k, :, pl.ds(pl.multiple_of(half * hc, LANE), hc)]


def _shard_blocks(w, br, bc):
    if w.shape[0] == 1:
        def write(ref, v):
            ref[...] = v
        return (lambda f: pl.BlockSpec((None, br, bc), lambda *a: (0, *f(*a)))), (lambda ref: ref[...]), write
    assert w.shape[1] == 1 and br == w.shape[0], w.shape

    def write_rows(ref, v):
        ref[:, 0, :] = v
    return (lambda f: pl.BlockSpec((br, 1, bc), lambda *a: (0, 0, f(*a)[1]))), (lambda ref: ref[:, 0, :]), write_rows


def _stage_shard(name, w, chip, after=None):
    rs, cs = w.shape[0] * w.shape[1], w.shape[2]
    (br, bc), n, idx = _blocks2d(rs, cs, 16)
    spec, get, _ = _shard_blocks(w, br, bc)

    def body(chip_ref, w_ref, *refs):
        refs[-1][...] = get(w_ref).astype(BF16)

    return pl.pallas_call(
        body, name="stage_" + name,
        grid_spec=pltpu.PrefetchScalarGridSpec(
            num_scalar_prefetch=1, grid=(n,),
            in_specs=[spec(lambda i, chip_ref: idx(i))] + ([] if after is None else [_ANY]),
            out_specs=pl.BlockSpec((None, br, bc), lambda i, chip_ref: (chip_ref[0], *idx(i)))),
        out_shape=jax.ShapeDtypeStruct((N_CHIPS, rs, cs), BF16),
        compiler_params=_params(("parallel",)),
    )(_scalar(chip), w, *([] if after is None else [after]))


_HBM = pl.BlockSpec(memory_space=pltpu.HBM)
_SEM = pl.BlockSpec(memory_space=pltpu.SEMAPHORE)
_EFFECT = pltpu.SideEffectType.DATAFLOW_SIDE_EFFECTING


def _split_start(name, bufs, n_copies, copies, after):
    n = len(bufs)

    def body(*refs):
        for cp in copies(refs[:n], refs[n + 1], refs[n + 2]):
            cp.start()
        refs[-1][...] = jnp.zeros_like(refs[-1])

    res = pl.pallas_call(
        body, name=name,
        out_shape=(pltpu.SemaphoreType.DMA((n_copies,)), pltpu.SemaphoreType.DMA((n_copies,)),
                   *[pltpu.HBM(b.shape, b.dtype) for b in bufs], jax.ShapeDtypeStruct((8, LANE), F32)),
        in_specs=[_HBM] * n + [_ANY], out_specs=(_SEM, _SEM, *[_HBM] * n, pl.BlockSpec(memory_space=pltpu.VMEM)),
        input_output_aliases={i: 2 + i for i in range(n)},
        compiler_params=pltpu.CompilerParams(has_side_effects=_EFFECT),
    )(*[pltpu.with_memory_space_constraint(b, pltpu.HBM) for b in bufs], after)
    return res[0], res[1], list(res[2:2 + n]), res[-1]


def _split_wait(name, send_sems, recv_sems, bufs, after, copies):
    n = len(bufs)

    def body(*refs):
        for cp in copies(refs[:n], refs[n], refs[n + 1]):
            cp.wait_send()
            cp.wait_recv()

    return list(pl.pallas_call(
        body, name=name, out_shape=[pltpu.HBM(b.shape, b.dtype) for b in bufs],
        in_specs=[_HBM] * n + [_SEM, _SEM, _ANY], out_specs=[_HBM] * n,
        input_output_aliases={i: i for i in range(n)},
        compiler_params=pltpu.CompilerParams(has_side_effects=_EFFECT),
    )(*bufs, send_sems, recv_sems, after))


def _gather_to_chips(bufs, send_sems, recv_sems):
    x, y, c = _me()
    return [pltpu.make_async_remote_copy(src_ref=_half(b, 2 * x + y, c), dst_ref=_half(b, 2 * x + y, c),
                                         send_sem=send_sems.at[3 * w + j], recv_sem=recv_sems.at[3 * w + j],
                                         device_id=(cx, cy, c), device_id_type=MESH_ID)
            for w, b in enumerate(bufs) for j, (cx, cy) in enumerate(_other_chips(x, y))]


def _gather_to_sibling(bufs, send_sems, recv_sems):
    x, y, c = _me()
    return [pltpu.make_async_remote_copy(src_ref=_half(b, 2 * cx + cy, c), dst_ref=_half(b, 2 * cx + cy, c),
                                         send_sem=send_sems.at[3 * w + j], recv_sem=recv_sems.at[3 * w + j],
                                         device_id=(x, y, 1 - c), device_id_type=MESH_ID)
            for w, b in enumerate(bufs) for j, (cx, cy) in enumerate(_other_chips(x, y))]


def _pair_exchange(name, grads):
    n = len(grads)

    def body(*refs):
        ins, outs, send_sems, recv_sems = refs[:n], refs[n:2 * n], refs[2 * n], refs[2 * n + 1]
        x, y, c = _me()
        cps = []
        for w, (g_ref, o_ref) in enumerate(zip(ins, outs)):
            cps.append(pltpu.make_async_remote_copy(src_ref=_half(g_ref, slice(None), 1 - c), dst_ref=o_ref,
                                                    send_sem=send_sems.at[w], recv_sem=recv_sems.at[w],
                                                    device_id=(x, y, 1 - c), device_id_type=MESH_ID))
            cps[-1].start()
        for cp in cps:
            cp.wait()

    return pl.pallas_call(
        body, name="pair_exchange_" + name, in_specs=[_ANY] * n, out_specs=[_ANY] * n,
        out_shape=[jax.ShapeDtypeStruct((g.shape[0], *_half_shape(g.shape[1], g.shape[2])), g.dtype) for g in grads],
        scratch_shapes=[pltpu.SemaphoreType.DMA((n,)), pltpu.SemaphoreType.DMA((n,))],
    )(*grads)


def _pair_copies(grads, lands, send_sems, recv_sems):
    x, y, c = _me()
    return [pltpu.make_async_remote_copy(src_ref=_half(g_ref, slice(None), 1 - c), dst_ref=l_ref, send_sem=send_sems.at[w],
                                         recv_sem=recv_sems.at[w], device_id=(x, y, 1 - c), device_id_type=MESH_ID)
            for w, (g_ref, l_ref) in enumerate(zip(grads, lands))]


def _pair_exchange_start(name, grads):
    n = len(grads)
    lands = [lax.empty((g.shape[0], *_half_shape(g.shape[1], g.shape[2])), g.dtype) for g in grads]
    send_sems, recv_sems, bufs, token = _split_start(
        "pair_exchange_start_" + name, [*grads, *lands], n, lambda refs, ss, rs: _pair_copies(refs[:n], refs[n:], ss, rs),
        jnp.zeros((8, LANE), F32))
    return (send_sems, recv_sems, bufs), token


def _pair_exchange_wait(name, state, after):
    send_sems, recv_sems, bufs = state
    n = len(bufs) // 2
    bufs = _split_wait("pair_exchange_wait_" + name, send_sems, recv_sems, bufs, after,
                       lambda refs, ss, rs: _pair_copies(refs[:n], refs[n:], ss, rs))
    return bufs[:n], bufs[n:]


def _pair_sum(name, g, theirs, c):
    (br, bc), nb, full, part = _half_blocks(g.shape[1], g.shape[2], 16)

    def body(c_ref, a_ref, b_ref, o_ref):
        o_ref[...] = (a_ref[...].astype(F32) + b_ref[...].astype(F32)).astype(o_ref.dtype)

    return pl.pallas_call(
        body, name="pair_sum_" + name,
        grid_spec=pltpu.PrefetchScalarGridSpec(
            num_scalar_prefetch=1, grid=(N_CHIPS, nb),
            in_specs=[pl.BlockSpec((None, br, bc), lambda k, i, c_ref: (k, *full(c_ref[0], i))),
                      pl.BlockSpec((None, br, bc), lambda k, i, c_ref: (k, *part(i)))],
            out_specs=pl.BlockSpec((None, br, bc), lambda k, i, c_ref: (k, *part(i)))),
        out_shape=jax.ShapeDtypeStruct(theirs.shape, BF16),
        compiler_params=_params(("parallel", "parallel")),
    )(_scalar(c), g, theirs)


def _chip_copies(srcs, lands, send_sems, recv_sems):
    x, y, c = _me()
    return [pltpu.make_async_remote_copy(src_ref=s_ref.at[2 * cx + cy], dst_ref=l_ref.at[j], send_sem=send_sems.at[3 * w + j],
                                         recv_sem=recv_sems.at[3 * w + j], device_id=(cx, cy, c), device_id_type=MESH_ID)
            for w, (s_ref, l_ref) in enumerate(zip(srcs, lands)) for j, (cx, cy) in enumerate(_other_chips(x, y))]


def _chip_exchange_start(name, sums):
    n = len(sums)
    lands = [lax.empty((3,) + s.shape[1:], s.dtype) for s in sums]
    send_sems, recv_sems, bufs, token = _split_start(
        "chip_exchange_start_" + name, [*sums, *lands], 3 * n, lambda refs, ss, rs: _chip_copies(refs[:n], refs[n:], ss, rs),
        jnp.zeros((8, LANE), F32))
    return send_sems, recv_sems, bufs[:n], bufs[n:], token


def _chip_exchange_wait(name, send_sems, recv_sems, sums, lands, after):
    n = len(sums)
    bufs = _split_wait("chip_exchange_wait_" + name, send_sems, recv_sems, [*sums, *lands], after,
                       lambda refs, ss, rs: _chip_copies(refs[:n], refs[n:], ss, rs))
    return bufs[:n], bufs[n:]


def _chip_sum(name, sums, theirs, chip):
    _, h, cs = sums.shape
    (br, bc), nb, idx = _blocks2d(h, cs, 16)

    def body(chip_ref, s_ref, t_ref, o_ref):
        acc = s_ref[...].astype(F32)
        for k in range(3):
            acc = acc + t_ref[k].astype(F32)
        o_ref[...] = acc

    return pl.pallas_call(
        body, name="chip_sum_" + name,
        grid_spec=pltpu.PrefetchScalarGridSpec(
            num_scalar_prefetch=1, grid=(nb,),
            in_specs=[pl.BlockSpec((None, br, bc), lambda i, chip_ref: (chip_ref[0], *idx(i))),
                      pl.BlockSpec((3, br, bc), lambda i, chip_ref: (0, *idx(i)))],
            out_specs=pl.BlockSpec((br, bc), lambda i, chip_ref: idx(i))),
        out_shape=jax.ShapeDtypeStruct((h, cs), F32),
        compiler_params=_params(("parallel",)),
    )(_scalar(chip), sums, theirs)


def _sibling_exchange(name, halves):
    n = len(halves)

    def body(*refs):
        ins, outs, send_sems, recv_sems = refs[:n], refs[n:2 * n], refs[2 * n], refs[2 * n + 1]
        x, y, c = _me()
        cps = []
        for w, (h_ref, o_ref) in enumerate(zip(ins, outs)):
            cps.append(pltpu.make_async_remote_copy(src_ref=h_ref, dst_ref=o_ref, send_sem=send_sems.at[w], recv_sem=recv_sems.at[w],
                                                    device_id=(x, y, 1 - c), device_id_type=MESH_ID))
            cps[-1].start()
        for cp in cps:
            cp.wait()

    return pl.pallas_call(
        body, name="sibling_exchange_" + name, in_specs=[_ANY] * n, out_specs=[_ANY] * n,
        out_shape=[jax.ShapeDtypeStruct(h.shape, h.dtype) for h in halves],
        scratch_shapes=[pltpu.SemaphoreType.DMA((n,)), pltpu.SemaphoreType.DMA((n,))],
    )(*halves)


def _allreduce_small(name, vec, after):
    def body(v_ref, after_ref, o_ref, buf_ref, send_sems, recv_sems):
        x, y, c = _me()
        me = 4 * x + 2 * y + c
        cps = []
        for p in range(1, 8):
            px, py, pc = x ^ (p >> 2), y ^ ((p >> 1) & 1), c ^ (p & 1)
            cps.append(pltpu.make_async_remote_copy(src_ref=v_ref, dst_ref=buf_ref.at[me], send_sem=send_sems.at[p - 1],
                                                    recv_sem=recv_sems.at[p - 1], device_id=(px, py, pc), device_id_type=MESH_ID))
            cps[-1].start()
        buf_ref[me] = v_ref[...]
        for p in range(1, 8):
            theirs = buf_ref.at[me ^ p]
            pltpu.make_async_remote_copy(src_ref=theirs, dst_ref=theirs, send_sem=send_sems.at[p - 1], recv_sem=recv_sems.at[p - 1],
                                         device_id=(x, y, c), device_id_type=MESH_ID).wait_recv()
        for cp in cps:
            cp.wait_send()
        acc = buf_ref[0]
        for k in range(1, 8):
            acc = acc + buf_ref[k]
        o_ref[...] = acc

    vm = pl.BlockSpec(memory_space=pltpu.VMEM)
    return pl.pallas_call(
        body, name=name, in_specs=[vm, _ANY], out_specs=vm, out_shape=jax.ShapeDtypeStruct(vec.shape, F32),
        scratch_shapes=[pltpu.VMEM((8,) + vec.shape, F32), pltpu.SemaphoreType.DMA((7,)), pltpu.SemaphoreType.DMA((7,))],
    )(vec, after)


def _adam_math(w, g, m, v):
    m = ADAM_B1 * m + (1.0 - ADAM_B1) * g
    v = ADAM_B2 * v + (1.0 - ADAM_B2) * (g * g)
    m_hat = m / (1.0 - ADAM_B1 ** ADAM_STEP)
    v_hat = v / (1.0 - ADAM_B2 ** ADAM_STEP)
    return -ADAM_LR * (m_hat / (jnp.sqrt(v_hat) + ADAM_EPS) + ADAM_WD * w), m, v


def _adamw(name, w, g, m, v):
    R, C = w.shape
    tr = _row_block(R, C, 8)

    def body(w_ref, g_ref, m_ref, v_ref, d_ref, nm_ref, nv_ref):
        d_ref[...], nm_ref[...], nv_ref[...] = _adam_math(w_ref[...], g_ref[...], m_ref[...], v_ref[...])

    blk = pl.BlockSpec((tr, C), lambda i: (i, 0))
    return pl.pallas_call(
        body, name=name, grid=(R // tr,), in_specs=[blk] * 4, out_specs=[blk] * 3,
        out_shape=[jax.ShapeDtypeStruct((R, C), F32)] * 3, compiler_params=_params(("parallel",)),
    )(w, g, m, v)


def _adamw_halves(name, w, mine, theirs, m, v, c):
    rs, cs = w.shape[0] * w.shape[1], w.shape[2]
    (br, bc), nb, whole, half = _half_blocks(rs, cs, 8)
    spec, get, put = _shard_blocks(w, br, bc)

    def body(c_ref, w_ref, a_ref, b_ref, m_ref, v_ref, g_ref, d_ref, nm_ref, nv_ref):
        g = jnp.where(pl.program_id(0) == c_ref[0], a_ref[...], b_ref[...])
        put(g_ref, g)
        for ref, val in zip((d_ref, nm_ref, nv_ref), _adam_math(get(w_ref), g, get(m_ref), get(v_ref))):
            put(ref, val)

    full = spec(lambda s, i, c_ref: whole(s, i))
    part = pl.BlockSpec((br, bc), lambda s, i, c_ref: half(i))
    return pl.pallas_call(
        body, name=name,
        grid_spec=pltpu.PrefetchScalarGridSpec(num_scalar_prefetch=1, grid=(2, nb), in_specs=[full, part, part, full, full],
                                               out_specs=[full] * 4),
        out_shape=[jax.ShapeDtypeStruct(w.shape, F32)] * 4, compiler_params=_params(("parallel", "parallel")),
    )(_scalar(c), w, mine, theirs, m, v)


def _pack_small(arrs):
    flat = jnp.concatenate([a.reshape(-1) for a in arrs])
    n = -(-flat.shape[0] // (8 * LANE)) * 8 * LANE
    return jnp.pad(flat, (0, n - flat.shape[0])).reshape(8, n // 8)


def _unpack_small(vec, shapes):
    flat, out, off = vec.reshape(-1), [], 0
    for s in shapes:
        out.append(flat[off:off + s[0] * s[1]].reshape(s))
        off += s[0] * s[1]
    return out


class _LateWeights:
    def __init__(self, cfg, tag, names, staged, after):
        self.cfg, self.tag, self.names, self.k = cfg, tag, names, 3 * len(names)
        self.send, self.recv, self.bufs, self.token = _split_start(f"gather_{tag}_chips_start", staged, self.k, _gather_to_chips,
                                                                    after)

    def pass_on(self, after):
        bufs = _split_wait(f"gather_{self.tag}_chips_wait", self.send, self.recv, self.bufs, after, _gather_to_chips)
        self.send, self.recv, self.bufs, token = _split_start(f"gather_{self.tag}_sibling_start", bufs, self.k, _gather_to_sibling,
                                                               self.token)
        return token

    def arrived(self, after):
        bufs = _split_wait(f"gather_{self.tag}_sibling_wait", self.send, self.recv, self.bufs, after, _gather_to_sibling)
        return {n: _gathered_to_kernel(self.cfg, n, b) for n, b in zip(self.names, bufs)}


def _step(cfg, a):
    chip = 2 * lax.axis_index("x") + lax.axis_index("y")
    core = lax.axis_index("c")
    big = BIG

    ffn = ("w_gate", "w_up", "w_down")
    first = ("w_in", "w_uq", "w_ukv")
    staged = {"w_in": _stage_shard("w_in", a["w_in"], chip)}
    in_weight = _LateWeights(cfg, "in", ("w_in",), [staged["w_in"]], jnp.zeros((8, LANE), F32))
    staged.update({n: _stage_shard(n, a[n], chip, in_weight.token) for n in big if n != "w_in"})
    in_sibling_leg = in_weight.pass_on(staged[big[-1]])

    sp = {n: a[n] for n in SMALL}
    sharded = _pack_small([a[n] for n in SMALL_SHARDED])
    slot = jnp.where(lax.broadcasted_iota(I32, (N_CHIPS,) + sharded.shape, 0) == chip, 0.5 * sharded[None], 0.0)
    allp = _allreduce_small("allgather_small", slot.reshape(N_CHIPS * 8, -1), in_sibling_leg).reshape((N_CHIPS,) + sharded.shape)
    per_chip = [_unpack_small(allp[ch], [a[n].shape for n in SMALL_SHARDED]) for ch in range(N_CHIPS)]
    for k, n in enumerate(SMALL_SHARDED):
        sp[n] = jnp.concatenate([per_chip[ch][k] for ch in range(N_CHIPS)], axis=1)
    W = in_weight.arrived(allp)

    mla_weights = _LateWeights(cfg, "mla", first[1:], [staged[n] for n in first[1:]], W["w_in"])
    out_weight = _LateWeights(cfg, "out", ("w_out",), [staged["w_out"]], mla_weights.token)
    ffn_weights = _LateWeights(cfg, "ffn", ffn, [staged[n] for n in ffn], out_weight.token)
    sp["mix_pre_g"] = sp["mix_pre_g"] + (mla_weights.token[0, 0] + out_weight.token[0, 0] + ffn_weights.token[0, 0])

    state = {}

    def ffn_grads_ready(grads):
        state["ffn_pairs"], token = _pair_exchange_start("ffn", [_grad_to_chips(cfg, n, grads[n]) for n in ffn_grads])
        return token

    def pair_sums(names, grads, theirs):
        return [_pair_sum(n, g, t, core) for n, g, t in zip(names, grads, theirs)]

    def early_grads_ready(grads):
        g_out = [_grad_to_chips(cfg, "w_out", grads["w_out"])]
        g_ffn, t_ffn = _pair_exchange_wait("ffn", state["ffn_pairs"], g_out[0])
        sums = pair_sums(ffn_grads, g_ffn, t_ffn) + pair_sums(["w_out"], g_out, _pair_exchange("out", g_out))
        state["early"] = _chip_exchange_start("early", sums)
        return state["early"][-1]

    def reduced_halves(tag, names, after):
        send_sems, recv_sems, s_bufs, l_bufs, _ = state[tag]
        s_bufs, l_bufs = _chip_exchange_wait(tag, send_sems, recv_sems, s_bufs, l_bufs, after)
        return [_chip_sum(n, s, t, chip) for n, s, t in zip(names, s_bufs, l_bufs)]

    def in_grad_ready(grads):
        grads = [_grad_to_chips(cfg, n, grads[n]) for n in first]
        state["rest"] = _chip_exchange_start("rest", pair_sums(first, grads, _pair_exchange("rest", grads)))
        return state["rest"][-1]

    ffn_grads = ("w_down", "w_gate", "w_up")
    early = ffn_grads + ("w_out",)
    loss, grad_x, gW, gs = _local_grads(cfg, a["x"], a["loss_target"], W, sp, mla_weights, out_weight, ffn_weights,
                                        ffn_grads_ready, early_grads_ready, in_grad_ready)
    out = {"grad_x": grad_x}

    def adamw(names, mine, theirs):
        for n, gm, gt in zip(names, mine, theirs):
            out["grad_" + n], out["delta_" + n], out["new_m_" + n], out["new_v_" + n] = _adamw_halves(
                "adamw_" + n, a[n], gm, gt, a["m_" + n], a["v_" + n], core)

    mine = reduced_halves("early", early, grad_x)
    adamw(early, mine, _sibling_exchange("early", mine))
    mine = reduced_halves("rest", first, out["new_v_" + early[-1]])
    theirs = _sibling_exchange("rest", mine)
    adamw(first, mine, theirs)

    shapes = [gs[n].shape for n in SMALL] + [(1, LANE)]
    red = _unpack_small(_allreduce_small("allreduce_small", _pack_small([gs[n] for n in SMALL] + [loss]), theirs[0]), shapes)
    g_small = dict(zip(SMALL, red[:-1]))
    for n in SMALL_SHARDED:
        cs = a[n].shape[1]
        g_small[n] = lax.dynamic_slice_in_dim(g_small[n], chip * cs, cs, axis=1)
    out["loss"] = red[-1][0, 0]
    sshapes = [a[n].shape for n in SMALL]
    d, nm, nv = _adamw("adamw_small", _pack_small([a[n] for n in SMALL]), _pack_small([g_small[n] for n in SMALL]),
                       _pack_small([a["m_" + n] for n in SMALL]), _pack_small([a["v_" + n] for n in SMALL]))
    for n, dd, mm, vv in zip(SMALL, _unpack_small(d, sshapes), _unpack_small(nm, sshapes), _unpack_small(nv, sshapes)):
        out["grad_" + n], out["delta_" + n], out["new_m_" + n], out["new_v_" + n] = g_small[n], dd, mm, vv
    return out


def kernel(x, mix_pre_g, w_in, q_norm_g, w_uq, kv_norm_g, w_ukv, ssm_conv_w, ssm_conv_b, dt_bias, a_log, d_skip, ssm_norm_g, w_out, mix_post_g, ffn_pre_g, w_gate, w_up, ffn_conv_w, ffn_conv_b, w_down, ffn_post_g, loss_target, m_mix_pre_g, m_w_in, m_q_norm_g, m_w_uq, m_kv_norm_g, m_w_ukv, m_ssm_conv_w, m_ssm_conv_b, m_dt_bias, m_a_log, m_d_skip, m_ssm_norm_g, m_w_out, m_mix_post_g, m_ffn_pre_g, m_w_gate, m_w_up, m_ffn_conv_w, m_ffn_conv_b, m_w_down, m_ffn_post_g, v_mix_pre_g, v_w_in, v_q_norm_g, v_w_uq, v_kv_norm_g, v_w_ukv, v_ssm_conv_w, v_ssm_conv_b, v_dt_bias, v_a_log, v_d_skip, v_ssm_norm_g, v_w_out, v_mix_post_g, v_ffn_pre_g, v_w_gate, v_w_up, v_ffn_conv_w, v_ffn_conv_b, v_w_down, v_ffn_post_g):
    args = dict(locals())
    def given(k, v):
        if k in ("w_in", "m_w_in", "v_w_in"):
            return jnp.transpose(v, (2, 0, 1))
        return v if k.removeprefix("m_").removeprefix("v_") in BIG or v.ndim < 3 else v[0]

    out = _step(_FULL, {k: given(k, v) for k, v in args.items()})
    res = [out["loss"], out["grad_x"][None]]
    for pre in ("grad_", "delta_", "new_m_", "new_v_"):
        for n in WEIGHTS:
            o = out[pre + n]
            res.append(jnp.transpose(o, (1, 2, 0)) if n == "w_in" else o if n in BIG or args[n].ndim < 3 else o[None])
    return tuple(res)
```

```python
import functools
import math

import jax
import jax.numpy as jnp
from jax import lax
from jax.experimental import pallas as pl
from jax.experimental.pallas import tpu as pltpu

F32, BF16, I32 = jnp.float32, jnp.bfloat16, jnp.int32
NN = (((1,), (0,)), ((), ()))
NT = (((1,), (1,)), ((), ()))
TN = (((0,), (0,)), ((), ()))
HI = lax.Precision.HIGHEST
MESH_ID = pl.DeviceIdType.MESH

EPS = 1e-6
CHUNK = 64
NOPE, ROPE, VH = 128, 64, 128
ROPE_THETA = 10000.0
HP, NST = 64, 128
SSM_K, FFN_K = 4, 3
LANE = 128
N_CHIPS = 4
VMEM_LIMIT = 52 * 1024 * 1024
MM_TILE, MM_TILE_K = 1408, 2816

ADAM_LR, ADAM_B1, ADAM_B2, ADAM_EPS, ADAM_WD, ADAM_STEP = 0.001, 0.9, 0.999, 1e-08, 0.01, 10


class _Cfg:
    def __init__(self, S, D, QL, KVL, H, HS, G, DFF, T):
        self.S, self.D, self.QL, self.KVL, self.H, self.HS, self.G, self.DFF, self.T = S, D, QL, KVL, H, HS, G, DFF, T
        self.INNER = HS * HP
        self.CONVCH = self.INNER + 2 * G * NST
        self.QW = H * (NOPE + ROPE)
        self.KVW = H * (NOPE + VH)
        self.MLAW = H * VH
        self.MIXW = self.MLAW + self.INNER
        self.IN_COLS = QL + KVL + ROPE + self.INNER + self.CONVCH + HS
        self.o_kr = QL + KVL
        self.o_z = self.o_kr + LANE
        self.o_xbc = self.o_z + self.INNER
        self.o_dt = self.o_xbc + self.CONVCH
        self.EXT = self.o_dt + LANE
        self.NPAIR = HS // 2
        self.REP = HS // G


_FULL = _Cfg(S=2048, D=2048, QL=768, KVL=512, H=8, HS=16, G=2, DFF=5632, T=256)
BIG = ("w_in", "w_uq", "w_ukv", "w_out", "w_gate", "w_up", "w_down")

SMALL = ("mix_pre_g", "q_norm_g", "kv_norm_g", "ssm_conv_w", "ssm_conv_b", "dt_bias", "a_log", "d_skip", "ssm_norm_g",
         "mix_post_g", "ffn_pre_g", "ffn_conv_w", "ffn_conv_b", "ffn_post_g")
SMALL_SHARDED = ("ssm_conv_w", "ffn_conv_w")
WEIGHTS = ("mix_pre_g", "w_in", "q_norm_g", "w_uq", "kv_norm_g", "w_ukv", "ssm_conv_w", "ssm_conv_b", "dt_bias", "a_log",
           "d_skip", "ssm_norm_g", "w_out", "mix_post_g", "ffn_pre_g", "w_gate", "w_up", "ffn_conv_w", "ffn_conv_b",
           "w_down", "ffn_post_g")


def _pick(n, target, mult):
    best = None
    for d in range(mult, min(n, target) + 1, mult):
        if n % d == 0:
            best = d
    return best if best is not None else n


def _params(sem=None):
    kw = dict(vmem_limit_bytes=VMEM_LIMIT)
    if sem is not None:
        kw["dimension_semantics"] = sem
    return pltpu.CompilerParams(**kw)


def _dot(a, b, dims=NN, precision=None):
    return lax.dot_general(a, b, dims, preferred_element_type=F32, precision=precision)


def _sigmoid(x):
    return 1.0 / (1.0 + jnp.exp(-x))


def _rs(x):
    return lax.rsqrt(jnp.mean(x * x, axis=-1, keepdims=True) + EPS)


def _rms_back(xh, r, dn):
    return r * (dn - xh * jnp.mean(dn * xh, axis=-1, keepdims=True))


def _colsum(v):
    return jnp.sum(v, axis=0, keepdims=True)


def _matmul(name, a, b, mode, out_dtype, a2=None, b2=None, chips=False):
    cs = None
    if mode == "nn":
        (M, K), N = a.shape, b.shape[-1]
        if chips:
            cs, N = N, N_CHIPS * N
    elif mode == "nt":
        (M, K), N = a.shape, b.shape[-2]
        if chips:
            cs = b.shape[-1]
    else:
        (K, M), N = a.shape, b.shape[1]
        if chips:
            cs = N // N_CHIPS
    tm = _pick(M, MM_TILE, LANE)
    tn = _pick(cs if chips and mode != "nt" else N, MM_TILE, LANE)
    tk = _pick(cs, MM_TILE, LANE) if chips and mode == "nt" else _pick(K, MM_TILE_K, LANE)
    nk = K // tk
    dims = {"nn": NN, "nt": NT, "tn": TN}[mode]
    a_spec = pl.BlockSpec((tk, tm), lambda i, j, k: (k, i)) if mode == "tn" else pl.BlockSpec((tm, tk), lambda i, j, k: (i, k))
    b_spec = pl.BlockSpec((tn, tk), lambda i, j, k: (j, k)) if mode == "nt" else pl.BlockSpec((tk, tn), lambda i, j, k: (k, j))
    o_spec = pl.BlockSpec((tm, tn), lambda i, j, k: (i, j))
    o_shape = (M, N)
    if chips and mode == "nn":
        per = cs // tn
        b_spec = pl.BlockSpec((None, tk, tn), lambda i, j, k: (j // per, k, j % per))
    elif chips and mode == "nt":
        per = cs // tk
        b_spec = pl.BlockSpec((None, tn, tk), lambda i, j, k: (k // per, j, k % per))
    elif chips:
        per = cs // tn
        o_spec = pl.BlockSpec((None, tm, tn), lambda i, j, k: (j // per, i, j % per))
        o_shape = (N_CHIPS, M, cs)
    two = a2 is not None

    def product(refs):
        part = _dot(refs[0][...].astype(BF16), refs[1][...].astype(BF16), dims)
        if two:
            part += _dot(refs[2][...].astype(BF16), refs[3][...].astype(BF16), dims)
        return part

    def body_whole_k(*refs):
        refs[-1][...] = product(refs).astype(refs[-1].dtype)

    def body(*refs):
        o_ref, acc_ref = refs[-2], refs[-1]
        k = pl.program_id(2)

        @pl.when(k == 0)
        def _():
            acc_ref[...] = product(refs)

        @pl.when(k > 0)
        def _():
            acc_ref[...] += product(refs)

        @pl.when(k == nk - 1)
        def _():
            o_ref[...] = acc_ref[...].astype(o_ref.dtype)

    ins = (a, b, a2, b2) if two else (a, b)
    return pl.pallas_call(
        body_whole_k if nk == 1 else body, name=name, grid=(M // tm, N // tn, nk),
        in_specs=[a_spec, b_spec] * (2 if two else 1),
        out_specs=o_spec,
        out_shape=jax.ShapeDtypeStruct(o_shape, out_dtype),
        scratch_shapes=[] if nk == 1 else [pltpu.VMEM((tm, tn), F32)],
        compiler_params=_params(("parallel", "parallel", "arbitrary")),
    )(*ins)


def _rowwise(name, fn, rows, mats, outs, reds, ts):
    S = rows[0].shape[0]
    nr, nm, no = len(rows), len(mats), len(outs)

    def body(*refs):
        res = fn(*[r[...] for r in refs[:nr + nm]])
        res = res if isinstance(res, (tuple, list)) else (res,)
        for r, v in zip(refs[nr + nm:nr + nm + no], res[:no]):
            r[...] = v.astype(r.dtype)
        first = pl.program_id(0) == 0
        for r, v in zip(refs[nr + nm + no:], res[no:]):
            @pl.when(first)
            def _():
                r[...] = jnp.broadcast_to(v, r.shape)

            @pl.when(jnp.logical_not(first))
            def _():
                r[...] += jnp.broadcast_to(v, r.shape)

    in_specs = [pl.BlockSpec((ts, a.shape[1]), lambda i: (i, 0)) for a in rows]
    in_specs += [pl.BlockSpec(m.shape, lambda i, nd=m.ndim: (0,) * nd) for m in mats]
    out_specs = [pl.BlockSpec((ts, w), lambda i: (i, 0)) for w, _ in outs]
    out_specs += [pl.BlockSpec(s, lambda i: (0, 0)) for s in reds]
    out_shape = [jax.ShapeDtypeStruct((S, w), dt) for w, dt in outs] + [jax.ShapeDtypeStruct(s, F32) for s in reds]
    return pl.pallas_call(
        body, name=name, grid=(S // ts,), in_specs=in_specs, out_specs=out_specs, out_shape=out_shape,
        compiler_params=_params(("arbitrary",) if reds else ("parallel",)),
    )(*rows, *mats)


def _shift_down(v, s):
    if s == 0:
        return v
    rows = lax.broadcasted_iota(I32, v.shape, 0)
    return jnp.where(rows >= s, pltpu.roll(v, s, 0), 0.0)


def _shift_up(v, s):
    if s == 0:
        return v
    n = v.shape[0]
    rows = lax.broadcasted_iota(I32, v.shape, 0)
    return jnp.where(rows < n - s, pltpu.roll(v, n - s, 0), 0.0)


def _conv(x, w, b):
    K = w.shape[0]
    y = jnp.broadcast_to(b, x.shape)
    for k in range(K):
        y = y + w[k:k + 1, :] * _shift_down(x, K - 1 - k)
    return y


def _conv_back(x, w, dc):
    K = w.shape[0]
    dx = jnp.zeros_like(x)
    dw = []
    for k in range(K):
        dx = dx + w[k:k + 1, :] * _shift_up(dc, K - 1 - k)
        dw.append(_colsum(dc * _shift_down(x, K - 1 - k)))
    return dx, jnp.concatenate(dw, axis=0), _colsum(dc)


def _colwise(name, fn, cols, vecs, outs, pouts, tc):
    S, C = cols[0].shape
    nc_, nv, no = len(cols), len(vecs), len(outs)

    def body(*refs):
        res = fn(*[r[...] for r in refs[:nc_ + nv]])
        res = res if isinstance(res, (tuple, list)) else (res,)
        for r, v in zip(refs[nc_ + nv:], res):
            r[...] = v.astype(r.dtype)

    in_specs = [pl.BlockSpec((S, tc), lambda j: (0, j)) for _ in cols]
    in_specs += [pl.BlockSpec((v.shape[0], tc), lambda j: (0, j)) for v in vecs]
    out_specs = [pl.BlockSpec((S, tc), lambda j: (0, j)) for _ in outs] + [pl.BlockSpec((k, tc), lambda j: (0, j)) for k in pouts]
    out_shape = [jax.ShapeDtypeStruct((S, C), dt) for dt in outs] + [jax.ShapeDtypeStruct((k, C), F32) for k in pouts]
    return pl.pallas_call(
        body, name=name, grid=(C // tc,), in_specs=in_specs, out_specs=out_specs, out_shape=out_shape,
        compiler_params=_params(("parallel",)),
    )(*cols, *vecs)


_G0, _G1 = math.sqrt(2.0 / math.pi), 0.044715


def _gelu(g):
    th = jnp.tanh(_G0 * (g + _G1 * g * g * g))
    return 0.5 * g * (1.0 + th), th


def _ffn_act(gate_pre, up, w, b):
    act, _ = _gelu(_conv(gate_pre, w, b))
    return act * up


def _ffn_act_back(dact, gate_pre, up, w, b):
    g = _conv(gate_pre, w, b)
    ge, th = _gelu(g)
    dge = 0.5 * (1.0 + th) + 0.5 * g * (1.0 - th * th) * _G0 * (1.0 + 3.0 * _G1 * g * g)
    dup = dact * ge
    dgate_pre, dw, db = _conv_back(gate_pre, w, dact * up * dge)
    return dgate_pre, dup, dw, db


def _ssm_act(xbc, w, b):
    c = _conv(xbc, w, b)
    return c * _sigmoid(c)


def _ssm_act_back(dxc, xbc, w, b):
    c = _conv(xbc, w, b)
    sg = _sigmoid(c)
    return _conv_back(xbc, w, dxc * sg * (1.0 + c * (1.0 - sg)))


def _rope_tables(S):
    inv = 1.0 / (ROPE_THETA ** (jnp.arange(0, ROPE, 2, dtype=F32) / ROPE))
    ang = jnp.arange(S, dtype=F32)[:, None] * inv[None, :]
    cos, sin = jnp.cos(ang), jnp.sin(ang)
    return jnp.tile(cos, (1, 4)), jnp.tile(jnp.concatenate([-sin, sin], axis=1), (1, 2))


def _swap_halves(x):
    lane = lax.broadcasted_iota(I32, x.shape, 1)
    w = x.shape[1]
    return jnp.where((lane % ROPE) < ROPE // 2, pltpu.roll(x, w - ROPE // 2, 1), pltpu.roll(x, ROPE // 2, 1))


def _rot(x, cos2, sin2):
    return x * cos2 + _swap_halves(x) * sin2


def _rot_back(dy, cos2, sin2):
    return dy * cos2 + _swap_halves(dy * sin2)


def _mla_pack(cfg, q, kv, kr, cos2, sin2):
    S, H = cfg.S, cfg.H
    ts = _pick(S, 256, 8)

    def body(q_ref, kv_ref, kr_ref, c_ref, s_ref, Q_ref, K_ref, V_ref):
        c2, s2 = c_ref[...], s_ref[...]
        krr = _rot(kr_ref[...], c2, s2)
        kr_half = (krr.astype(BF16), pltpu.roll(krr, ROPE, 1).astype(BF16))
        for j in range(H // 2):
            qr = _rot(q_ref[:, (H + j) * LANE:(H + j + 1) * LANE], c2, s2).astype(BF16)
            for h in (2 * j, 2 * j + 1):
                Q_ref[h, :, 0:LANE] = q_ref[:, h * LANE:(h + 1) * LANE].astype(BF16)
                Q_ref[h, :, LANE:] = qr
                K_ref[h, :, 0:LANE] = kv_ref[:, h * LANE:(h + 1) * LANE].astype(BF16)
                K_ref[h, :, LANE:] = kr_half[h % 2]
                V_ref[h] = kv_ref[:, (H + h) * LANE:(H + h + 1) * LANE].astype(BF16)

    tab = pl.BlockSpec((ts, LANE), lambda i: (i, 0))
    heads = lambda w: pl.BlockSpec((H, ts, w), lambda i: (0, i, 0))
    return pl.pallas_call(
        body, name="mla_pack", grid=(S // ts,),
        in_specs=[pl.BlockSpec((ts, cfg.QW), lambda i: (i, 0)), pl.BlockSpec((ts, cfg.KVW), lambda i: (i, 0)), tab, tab, tab],
        out_specs=[heads(2 * LANE), heads(2 * LANE), heads(LANE)],
        out_shape=[jax.ShapeDtypeStruct((H, S, 2 * LANE), BF16), jax.ShapeDtypeStruct((H, S, 2 * LANE), BF16),
                   jax.ShapeDtypeStruct((H, S, LANE), BF16)],
        compiler_params=_params(("parallel",)),
    )(q, kv, kr, cos2, sin2)


def _mla_unpack(cfg, dQ, dK, dV, cos2, sin2):
    S, H = cfg.S, cfg.H
    ts = _pick(S, 256, 8)

    def body(dQ_ref, dK_ref, dV_ref, c_ref, s_ref, dq_ref, dkv_ref, dkr_ref):
        c2, s2 = c_ref[...], s_ref[...]
        lo = lax.broadcasted_iota(I32, (ts, LANE), 1) < ROPE
        tk = jnp.zeros((ts, LANE), F32)
        for h in range(H):
            dq_ref[:, h * LANE:(h + 1) * LANE] = dQ_ref[h, :, 0:LANE].astype(BF16)
            dkv_ref[:, h * LANE:(h + 1) * LANE] = dK_ref[h, :, 0:LANE].astype(BF16)
            dkv_ref[:, (H + h) * LANE:(H + h + 1) * LANE] = dV_ref[h].astype(BF16)
            own = lo if h % 2 == 0 else jnp.logical_not(lo)
            tk = tk + jnp.where(own, dK_ref[h, :, LANE:], 0.0)
        for j in range(H // 2):
            dr = dQ_ref[2 * j, :, LANE:] + dQ_ref[2 * j + 1, :, LANE:]
            dq_ref[:, (H + j) * LANE:(H + j + 1) * LANE] = _rot_back(dr, c2, s2).astype(BF16)
        dkr_rot = jnp.where(lo, tk + pltpu.roll(tk, ROPE, 1), 0.0)
        dkr_ref[...] = _rot_back(dkr_rot, c2, s2).astype(BF16)

    tab = pl.BlockSpec((ts, LANE), lambda i: (i, 0))
    return pl.pallas_call(
        body, name="mla_unpack", grid=(S // ts,),
        in_specs=[pl.BlockSpec((H, ts, 2 * LANE), lambda i: (0, i, 0)), pl.BlockSpec((H, ts, 2 * LANE), lambda i: (0, i, 0)),
                  pl.BlockSpec((H, ts, LANE), lambda i: (0, i, 0)), tab, tab],
        out_specs=[pl.BlockSpec((ts, cfg.QW), lambda i: (i, 0)), pl.BlockSpec((ts, cfg.KVW), lambda i: (i, 0)), tab],
        out_shape=[jax.ShapeDtypeStruct((S, cfg.QW), BF16), jax.ShapeDtypeStruct((S, cfg.KVW), BF16),
                   jax.ShapeDtypeStruct((S, LANE), BF16)],
        compiler_params=_params(("parallel",)),
    )(dQ, dK, dV, cos2, sin2)


_ATT_T = 256
_ATT_HB = 8
_ATT_SCALE = (NOPE + ROPE) ** -0.5


def _diag_mask(transposed=False):
    r = lax.broadcasted_iota(I32, (_ATT_T, _ATT_T), 0) // CHUNK
    c = lax.broadcasted_iota(I32, (_ATT_T, _ATT_T), 1) // CHUNK
    return r <= c if transposed else c <= r


def _row_form(col):
    return jnp.broadcast_to(col, (col.shape[0], LANE)).T[0:8, :]


def _attn_fwd(cfg, Q, K, V):
    S, H, T, HB = cfg.S, cfg.H, _ATT_T, _ATT_HB

    def body(q_ref, k_ref, v_ref, o_ref, lse_ref, lse_t_ref):
        qi = pl.program_id(1)

        def head_step(b, kb, carry, mask):
            m, l, acc = carry
            ks = pl.multiple_of(kb * T, T)
            s = _dot(q_ref[b], k_ref[b, pl.ds(ks, T), :], NT) * _ATT_SCALE
            if mask is not None:
                s = jnp.where(mask, s, -1e30)
            m_new = jnp.maximum(m, jnp.max(s, axis=1, keepdims=True))
            p = jnp.exp(s - m_new)
            alpha = jnp.exp(m - m_new)
            l = alpha * l + jnp.sum(p, axis=1, keepdims=True)
            acc = alpha * acc + _dot(p.astype(BF16), v_ref[b, pl.ds(ks, T), :])
            return m_new, l, acc

        def step(kb, carry, mask=None):
            return tuple(head_step(b, kb, carry[b], mask) for b in range(HB))

        init = (jnp.full((T, 1), -1e30, F32), jnp.zeros((T, 1), F32), jnp.zeros((T, VH), F32))
        done = step(qi, lax.fori_loop(0, qi, step, (init,) * HB), _diag_mask())
        for b, (m, l, acc) in enumerate(done):
            o_ref[:, b * LANE:(b + 1) * LANE] = acc / l
            lse = m + jnp.log(l)
            lse_ref[:, b * LANE:(b + 1) * LANE] = jnp.broadcast_to(lse, (T, LANE))
            lse_t_ref[b] = _row_form(lse)

    return pl.pallas_call(
        body, name="attn_fwd", grid=(H // HB, S // T),
        in_specs=[pl.BlockSpec((HB, T, 2 * LANE), lambda h, i: (h, i, 0)), pl.BlockSpec((HB, S, 2 * LANE), lambda h, i: (h, 0, 0)),
                  pl.BlockSpec((HB, S, LANE), lambda h, i: (h, 0, 0))],
        out_specs=[pl.BlockSpec((T, HB * LANE), lambda h, i: (i, h)), pl.BlockSpec((T, HB * LANE), lambda h, i: (i, h)),
                   pl.BlockSpec((HB, 8, T), lambda h, i: (h, 0, i))],
        out_shape=[jax.ShapeDtypeStruct((S, H * LANE), F32), jax.ShapeDtypeStruct((S, H * LANE), F32),
                   jax.ShapeDtypeStruct((H, 8, S), F32)],
        compiler_params=_params(("parallel", "parallel")),
    )(Q, K, V)


def _attn_dq(cfg, Q, K, V, do, o, lse, after):
    S, H, T, HB = cfg.S, cfg.H, _ATT_T, _ATT_HB

    def body(q_ref, k_ref, v_ref, do_ref, o_ref, lse_ref, after_ref, dq_ref, dl_t_ref):
        qi = pl.program_id(1)
        do = [do_ref[:, b * LANE:(b + 1) * LANE] for b in range(HB)]
        delta = [jnp.sum(do[b] * o_ref[:, b * LANE:(b + 1) * LANE], axis=1, keepdims=True) for b in range(HB)]
        dob = [d.astype(BF16) for d in do]

        def head_step(b, kb, dq, mask):
            ks = pl.multiple_of(kb * T, T)
            k = k_ref[b, pl.ds(ks, T), :]
            s = _dot(q_ref[b], k, NT) * _ATT_SCALE
            if mask is not None:
                s = jnp.where(mask, s, -1e30)
            p = jnp.exp(s - lse_ref[:, b * LANE:b * LANE + 1])
            dp = _dot(dob[b], v_ref[b, pl.ds(ks, T), :], NT)
            ds = p * (dp - delta[b]) * _ATT_SCALE
            return dq + _dot(ds.astype(BF16), k)

        def step(kb, dqs, mask=None):
            return tuple(head_step(b, kb, dqs[b], mask) for b in range(HB))

        dqs = step(qi, lax.fori_loop(0, qi, step, (jnp.zeros((T, 2 * LANE), F32),) * HB), _diag_mask())
        for b in range(HB):
            dq_ref[b] = dqs[b]
            dl_t_ref[b] = _row_form(delta[b])

    col = pl.BlockSpec((T, HB * LANE), lambda h, i: (i, h))
    return pl.pallas_call(
        body, name="attn_dq", grid=(H // HB, S // T),
        in_specs=[pl.BlockSpec((HB, T, 2 * LANE), lambda h, i: (h, i, 0)), pl.BlockSpec((HB, S, 2 * LANE), lambda h, i: (h, 0, 0)),
                  pl.BlockSpec((HB, S, LANE), lambda h, i: (h, 0, 0)), col, col, col, _ANY],
        out_specs=[pl.BlockSpec((HB, T, 2 * LANE), lambda h, i: (h, i, 0)), pl.BlockSpec((HB, 8, T), lambda h, i: (h, 0, i))],
        out_shape=[jax.ShapeDtypeStruct((H, S, 2 * LANE), F32), jax.ShapeDtypeStruct((H, 8, S), F32)],
        compiler_params=_params(("parallel", "parallel")),
    )(Q, K, V, do, o, lse, after)


def _attn_dkv(cfg, Q, K, V, do, lse_t, delta_t):
    S, H, T, HB = cfg.S, cfg.H, _ATT_T, _ATT_HB
    nq = S // T

    def body(q_ref, k_ref, v_ref, do_ref, lse_ref, dl_ref, dk_ref, dv_ref):
        kb = pl.program_id(1)

        def head_step(b, qi, carry, mask):
            dk, dv = carry
            qs = pl.multiple_of(qi * T, T)
            q = q_ref[b, pl.ds(qs, T), :]
            dob = do_ref[pl.ds(qs, T), b * LANE:(b + 1) * LANE].astype(BF16)
            s = _dot(k_ref[b], q, NT) * _ATT_SCALE
            if mask is not None:
                s = jnp.where(mask, s, -1e30)
            p = jnp.exp(s - lse_ref[b, 0:1, pl.ds(qs, T)])
            dv = dv + _dot(p.astype(BF16), dob)
            dp = _dot(v_ref[b], dob, NT)
            ds = p * (dp - dl_ref[b, 0:1, pl.ds(qs, T)]) * _ATT_SCALE
            dk = dk + _dot(ds.astype(BF16), q)
            return dk, dv

        def step(qi, carry, mask=None):
            return tuple(head_step(b, qi, carry[b], mask) for b in range(HB))

        zero = (jnp.zeros((T, 2 * LANE), F32), jnp.zeros((T, VH), F32))
        done = lax.fori_loop(kb + 1, nq, step, step(kb, (zero,) * HB, _diag_mask(transposed=True)))
        for b, (dk, dv) in enumerate(done):
            dk_ref[b] = dk
            dv_ref[b] = dv

    row = pl.BlockSpec((HB, 8, S), lambda h, j: (h, 0, 0))
    return pl.pallas_call(
        body, name="attn_dkv", grid=(H // HB, S // T),
        in_specs=[pl.BlockSpec((HB, S, 2 * LANE), lambda h, j: (h, 0, 0)), pl.BlockSpec((HB, T, 2 * LANE), lambda h, j: (h, j, 0)),
                  pl.BlockSpec((HB, T, LANE), lambda h, j: (h, j, 0)), pl.BlockSpec((S, HB * LANE), lambda h, j: (0, h)), row, row],
        out_specs=[pl.BlockSpec((HB, T, 2 * LANE), lambda h, j: (h, j, 0)), pl.BlockSpec((HB, T, LANE), lambda h, j: (h, j, 0))],
        out_shape=[jax.ShapeDtypeStruct((H, S, 2 * LANE), F32), jax.ShapeDtypeStruct((H, S, LANE), F32)],
        compiler_params=_params(("parallel", "parallel")),
    )(Q, K, V, do, lse_t, delta_t)


def _expand_matrix(cfg):
    r = lax.broadcasted_iota(I32, (LANE, cfg.INNER), 0)
    c = lax.broadcasted_iota(I32, (LANE, cfg.INNER), 1)
    return (r == c // HP).astype(F32)


def _softplus(x):
    return jnp.maximum(x, 0.0) + jnp.log(1.0 + jnp.exp(-jnp.abs(x)))


def _ssd_prep(cfg, dt_raw, dt_bias_pad, a_log_pad, expand):
    HS = cfg.HS

    def fn(raw, bias, alog, E):
        heads = lax.broadcasted_iota(I32, raw.shape, 1) < HS
        dt = jnp.where(heads, _softplus(raw + bias), 0.0)
        a = dt * jnp.where(heads[0:1], -jnp.exp(alog), 0.0)
        return dt, a, _dot(dt, E, precision=HI), _dot(a, E, precision=HI)

    return _rowwise("ssd_prep", fn, [dt_raw], [dt_bias_pad, a_log_pad, expand],
                    [(LANE, F32), (LANE, F32), (cfg.INNER, F32), (cfg.INNER, F32)], [], _pick(cfg.S, 512, 8))


def _tril(T):
    return lax.broadcasted_iota(I32, (T, T), 0) >= lax.broadcasted_iota(I32, (T, T), 1)


def _ssd_fwd(cfg, xc, dt_exp, a_exp, a_small, dskip_exp):
    S, T, INNER, G, NPAIR = cfg.S, cfg.T, cfg.INNER, cfg.G, cfg.NPAIR
    NC = S // T

    def body(xc_ref, dte_ref, ae_ref, as_ref, dsk_ref, y_ref, hin_ref, ht_ref):
        @pl.when(pl.program_id(0) == 0)
        def _():
            ht_ref[...] = jnp.zeros_like(ht_ref)

        tril = _tril(T)
        tri = tril.astype(F32)
        acs_s = _dot(tri, as_ref[...], precision=HI)
        acs_e = _dot(tri, ae_ref[...], precision=HI)
        acs_t = acs_s.T
        lo = lax.broadcasted_iota(I32, (T, LANE), 1) < HP
        for g in range(G):
            Bb = xc_ref[:, INNER + g * NST:INNER + (g + 1) * NST].astype(BF16)
            Cb = xc_ref[:, INNER + (G + g) * NST:INNER + (G + g + 1) * NST].astype(BF16)
            Gm = _dot(Cb, Bb, NT)
            for j in range(g * NPAIR // G, (g + 1) * NPAIR // G):
                sl = slice(j * LANE, (j + 1) * LANE)
                Xp = xc_ref[:, sl]
                Xdt = Xp * dte_ref[:, sl]
                Xb = Xdt.astype(BF16)
                acs_p = acs_e[:, sl]
                last = acs_p[T - 1:T, :]
                Hin = ht_ref[j]
                hin_ref[0, j] = Hin
                yd = []
                for e in (0, 1):
                    h = 2 * j + e
                    Lm = jnp.exp(jnp.where(tril, acs_s[:, h:h + 1] - acs_t[h:h + 1, :], -1e30))
                    yd.append(_dot((Gm * Lm).astype(BF16), Xb))
                y_off = _dot(Cb, Hin.astype(BF16)) * jnp.exp(acs_p)
                y_ref[:, sl] = jnp.where(lo, yd[0], yd[1]) + y_off + Xp * dsk_ref[:, sl]
                st = _dot(Bb, (Xdt * jnp.exp(last - acs_p)).astype(BF16), TN)
                ht_ref[j] = jnp.exp(last) * Hin + st

    rows = lambda w: pl.BlockSpec((T, w), lambda c: (c, 0))
    return pl.pallas_call(
        body, name="ssd_fwd", grid=(NC,),
        in_specs=[rows(cfg.CONVCH), rows(INNER), rows(INNER), rows(LANE), pl.BlockSpec((1, INNER), lambda c: (0, 0))],
        out_specs=[rows(INNER), pl.BlockSpec((1, NPAIR, NST, LANE), lambda c: (c, 0, 0, 0))],
        out_shape=[jax.ShapeDtypeStruct((S, INNER), F32), jax.ShapeDtypeStruct((NC, NPAIR, NST, LANE), F32)],
        scratch_shapes=[pltpu.VMEM((NPAIR, NST, LANE), F32)],
        compiler_params=_params(("arbitrary",)),
    )(xc, dt_exp, a_exp, a_small, dskip_exp)


def _ssd_bwd(cfg, dy, xc, dt_exp, a_exp, a_small, dskip_exp, hin, dt_raw, dt_bias_pad, a_log_pad, expand):
    S, T, INNER, G, NPAIR, HS = cfg.S, cfg.T, cfg.INNER, cfg.G, cfg.NPAIR, cfg.HS
    NC = S // T

    def body(dy_ref, xc_ref, dte_ref, ae_ref, as_ref, dsk_ref, hin_ref, raw_ref, bias_ref, alog_ref, e_ref,
             dxc_ref, draw_ref, dbias_ref, dalog_ref, dskip_ref, dht_ref, cols_ref, rows_ref, dacs_ref, ddt_ref):
        first = pl.program_id(0) == 0

        @pl.when(first)
        def _():
            dht_ref[...] = jnp.zeros_like(dht_ref)

        tril = _tril(T)
        tri = tril.astype(F32)
        a_s = as_ref[...]
        acs_s = _dot(tri, a_s, precision=HI)
        acs_e = _dot(tri, ae_ref[...], precision=HI)
        acs_t = acs_s.T
        lo = lax.broadcasted_iota(I32, (T, LANE), 1) < HP
        last_row = lax.broadcasted_iota(I32, (T, LANE), 0) == T - 1
        cols_ref[...] = jnp.zeros_like(cols_ref)
        rows_ref[...] = jnp.zeros_like(rows_ref)
        dsk_parts = []
        for g in range(G):
            bsl = slice(INNER + g * NST, INNER + (g + 1) * NST)
            csl = slice(INNER + (G + g) * NST, INNER + (G + g + 1) * NST)
            Bb = xc_ref[:, bsl].astype(BF16)
            Cb = xc_ref[:, csl].astype(BF16)
            Gm = _dot(Cb, Bb, NT)
            dG = jnp.zeros((T, T), F32)
            dB = jnp.zeros((T, NST), F32)
            dC = jnp.zeros((T, NST), F32)
            for j in range(g * NPAIR // G, (g + 1) * NPAIR // G):
                sl = slice(j * LANE, (j + 1) * LANE)
                Xp = xc_ref[:, sl]
                dtp = dte_ref[:, sl]
                Xdt = Xp * dtp
                Xb = Xdt.astype(BF16)
                acs_p = acs_e[:, sl]
                last = acs_p[T - 1:T, :]
                e_p, dec, cd = jnp.exp(acs_p), jnp.exp(last - acs_p), jnp.exp(last)
                Hin = hin_ref[0, j]
                Hb = Hin.astype(BF16)
                dHn = dht_ref[j]
                dHb = dHn.astype(BF16)
                dYp = dy_ref[:, sl]
                z = _dot(Cb, Hb)
                dz = (dYp * e_p).astype(BF16)
                dacs_p = dYp * z * e_p
                dC = dC + _dot(dz, Hb, NT)
                dHin = _dot(Cb, dz, TN) + cd * dHn
                dlast = _colsum(dHn * Hin) * cd
                qv = _dot(Bb, dHb)
                dXdt = qv * dec
                ddec = qv * Xdt * dec
                dacs_p = dacs_p - ddec
                dlast = dlast + _colsum(ddec)
                dB = dB + _dot((Xdt * dec).astype(BF16), dHb, NT)
                for e in (0, 1):
                    h = 2 * j + e
                    Lm = jnp.exp(jnp.where(tril, acs_s[:, h:h + 1] - acs_t[h:h + 1, :], -1e30))
                    Mh = Gm * Lm
                    dYe = jnp.where(lo if e == 0 else jnp.logical_not(lo), dYp, 0.0).astype(BF16)
                    dM = _dot(dYe, Xb, NT)
                    dXdt = dXdt + _dot(Mh.astype(BF16), dYe, TN)
                    W = dM * Mh
                    cols_ref[:, h:h + 1] = jnp.sum(W, axis=1, keepdims=True)
                    rows_ref[h:h + 1, :] = _colsum(W)
                    dG = dG + dM * Lm
                dacs_ref[:, sl] = dacs_p + jnp.where(last_row, dlast, 0.0)
                ddt_ref[:, sl] = dXdt * Xp
                dxc_ref[:, sl] = dXdt * dtp + dYp * dsk_ref[:, sl]
                dsk_parts.append(_colsum(dYp * Xp))
                dht_ref[j] = dHin
            dGb = dG.astype(BF16)
            dxc_ref[:, bsl] = dB + _dot(dGb, Cb, TN)
            dxc_ref[:, csl] = dC + _dot(dGb, Bb)
        E = e_ref[...]
        dacs_s = cols_ref[...] - rows_ref[...].T + _dot(dacs_ref[...], E, NT, precision=HI)
        da = _dot(tri, dacs_s, TN, precision=HI)
        heads = lax.broadcasted_iota(I32, (1, LANE), 1) < HS
        A = jnp.where(heads, -jnp.exp(alog_ref[...]), 0.0)
        ddt = _dot(ddt_ref[...], E, NT, precision=HI) + da * A
        draw = jnp.where(heads, ddt * _sigmoid(raw_ref[...] + bias_ref[...]), 0.0)
        draw_ref[...] = draw
        dsk = _dot(jnp.broadcast_to(jnp.concatenate(dsk_parts, axis=1), (8, INNER)), E, NT, precision=HI)[0:1]
        for ref, val in ((dbias_ref, _colsum(draw)), (dalog_ref, _colsum(da * a_s)), (dskip_ref, dsk)):
            @pl.when(first)
            def _():
                ref[...] = val

            @pl.when(jnp.logical_not(first))
            def _():
                ref[...] += val

    rows = lambda w: pl.BlockSpec((T, w), lambda c: (NC - 1 - c, 0))
    vec = lambda w: pl.BlockSpec((1, w), lambda c: (0, 0))
    return pl.pallas_call(
        body, name="ssd_bwd", grid=(NC,),
        in_specs=[rows(INNER), rows(cfg.CONVCH), rows(INNER), rows(INNER), rows(LANE), vec(INNER),
                  pl.BlockSpec((1, NPAIR, NST, LANE), lambda c: (NC - 1 - c, 0, 0, 0)), rows(LANE), vec(LANE), vec(LANE),
                  pl.BlockSpec((LANE, INNER), lambda c: (0, 0))],
        out_specs=[rows(cfg.CONVCH), rows(LANE), vec(LANE), vec(LANE), vec(LANE)],
        out_shape=[jax.ShapeDtypeStruct((S, cfg.CONVCH), F32), jax.ShapeDtypeStruct((S, LANE), F32)]
        + [jax.ShapeDtypeStruct((1, LANE), F32)] * 3,
        scratch_shapes=[pltpu.VMEM((NPAIR, NST, LANE), F32), pltpu.VMEM((T, LANE), F32), pltpu.VMEM((LANE, T), F32),
                        pltpu.VMEM((T, INNER), F32), pltpu.VMEM((T, INNER), F32)],
        compiler_params=_params(("arbitrary",)),
    )(dy, xc, dt_exp, a_exp, a_small, dskip_exp, hin, dt_raw, dt_bias_pad, a_log_pad, expand)


def _ssd_post(cfg, y, z, norm_g):
    W = cfg.INNER // cfg.G

    def fn(y, z, g):
        yz = y * z * _sigmoid(z)
        return jnp.concatenate([yz[:, i * W:(i + 1) * W] * _rs(yz[:, i * W:(i + 1) * W]) for i in range(cfg.G)], axis=1) * g

    return _rowwise("ssd_post", fn, [y, z], [norm_g], [(cfg.INNER, BF16)], [], _pick(cfg.S, 256, 8))[0]


def _ssd_post_bwd(cfg, db, y, z, norm_g):
    W = cfg.INNER // cfg.G

    def fn(db, y, z, g):
        sg = _sigmoid(z)
        yz = y * z * sg
        dn = db * g
        dyz, nh = [], []
        for i in range(cfg.G):
            seg = yz[:, i * W:(i + 1) * W]
            r = _rs(seg)
            nh.append(seg * r)
            dyz.append(_rms_back(nh[-1], r, dn[:, i * W:(i + 1) * W]))
        dyz = jnp.concatenate(dyz, axis=1)
        return dyz * z * sg, dyz * y * sg * (1.0 + z * (1.0 - sg)), _colsum(db * jnp.concatenate(nh, axis=1))

    return _rowwise("ssd_post_bwd", fn, [db, y, z], [norm_g], [(cfg.INNER, F32), (cfg.INNER, F32)], [(1, cfg.INNER)],
                    _pick(cfg.S, 256, 8))


def _local_grads(cfg, x, tgt, W, sp, mla_weights=None, out_weight=None, ffn_weights=None, ffn_grads_ready=None,
                 early_grads_ready=None, in_grad_ready=None):
    S, D, H, INNER = cfg.S, cfg.D, cfg.H, cfg.INNER
    ts = _pick(S, 256, 8)
    tc = 256

    xn = _rowwise("rms_pre", lambda x, g: x * _rs(x) * g, [x], [sp["mix_pre_g"]], [(D, BF16)], [], ts)[0]
    u = _matmul("mm_in", xn, W["w_in"], "nt", F32)
    c_q, c_kv = u[:, :cfg.QL], u[:, cfg.QL:cfg.o_kr]
    kr = u[:, cfg.o_kr:cfg.o_z]
    z = u[:, cfg.o_z:cfg.o_xbc]
    xbc = u[:, cfg.o_xbc:cfg.o_dt]
    dt_raw = u[:, cfg.o_dt:]

    if mla_weights is not None:
        sp = dict(sp, q_norm_g=sp["q_norm_g"] + mla_weights.pass_on(u)[0, 0])
    cqn = _rowwise("rms_q", lambda x, g: x * _rs(x) * g, [c_q], [sp["q_norm_g"]], [(cfg.QL, BF16)], [], ts)[0]
    ckvn = _rowwise("rms_kv", lambda x, g: x * _rs(x) * g, [c_kv], [sp["kv_norm_g"]], [(cfg.KVL, BF16)], [], ts)[0]
    if mla_weights is not None:
        W = dict(W, **mla_weights.arrived(ckvn))
    q = _matmul("mm_uq", cqn, W["w_uq"], "nn", F32)
    kv = _matmul("mm_ukv", ckvn, W["w_ukv"], "nn", F32)
    cos2, sin2 = _rope_tables(S)
    Qh, Kh, Vh = _mla_pack(cfg, q, kv, kr, cos2, sin2)
    a_out, lse, lse_t = _attn_fwd(cfg, Qh, Kh, Vh)
    if out_weight is not None:
        sp = dict(sp, ssm_conv_b=sp["ssm_conv_b"] + out_weight.pass_on(a_out)[0, 0])

    pad = lambda v: jnp.pad(v, ((0, 0), (0, LANE - v.shape[1])))
    expand = _expand_matrix(cfg)
    dt_bias_pad, a_log_pad = pad(sp["dt_bias"]), pad(sp["a_log"])
    dskip_exp = jnp.repeat(sp["d_skip"], HP, axis=1)
    xc = _colwise("ssm_act", _ssm_act, [xbc], [sp["ssm_conv_w"], sp["ssm_conv_b"]], [F32], [], tc)[0]
    dt_s, a_s, dt_exp, a_exp = _ssd_prep(cfg, dt_raw, dt_bias_pad, a_log_pad, expand)
    y_ssd, hin = _ssd_fwd(cfg, xc, dt_exp, a_exp, a_s, dskip_exp)
    b_out = _ssd_post(cfg, y_ssd, z, sp["ssm_norm_g"])

    ab_out = jnp.concatenate([a_out.astype(BF16), b_out], axis=1)
    if out_weight is not None:
        W = dict(W, **out_weight.arrived(ab_out))
    if ffn_weights is not None:
        sp = dict(sp, mix_post_g=sp["mix_post_g"] + ffn_weights.pass_on(ab_out)[0, 0])
    mix = _matmul("mm_out", ab_out, W["w_out"], "nn", F32)

    def mid(x, mix, g_mp, g_fp):
        x1 = x + mix * _rs(mix) * g_mp
        return x1, x1 * _rs(x1) * g_fp

    x1, h2 = _rowwise("fwd_mid", mid, [x, mix], [sp["mix_post_g"], sp["ffn_pre_g"]], [(D, F32), (D, BF16)], [], ts)
    if ffn_weights is not None:
        W = dict(W, **ffn_weights.arrived(h2))
    gate_pre = _matmul("mm_gate", h2, W["w_gate"], "nn", F32, chips=True)
    up = _matmul("mm_up", h2, W["w_up"], "nn", F32, chips=True)
    act = _colwise("ffn_act", _ffn_act, [gate_pre, up], [sp["ffn_conv_w"], sp["ffn_conv_b"]], [BF16], [], tc)[0]
    f = _matmul("mm_down", act, W["w_down"], "nn", F32)

    def final(x1, f, t, g):
        r = _rs(f)
        fh = f * r
        err = x1 + fh * g - t
        loss = 0.5 * jnp.sum(jnp.mean(err * err, axis=-1, keepdims=True), axis=0, keepdims=True)
        dy = err * (1.0 / D)
        return dy, _rms_back(fh, r, dy * g), _colsum(dy * fh), loss

    dy, df, g_ffn_post, loss = _rowwise("final", final, [x1, f, tgt], [sp["ffn_post_g"]], [(D, F32), (D, BF16)],
                                        [(1, D), (1, LANE)], ts)
    gW = {}
    dact = _matmul("mm_down_dx", df, W["w_down"], "nt", F32)
    gW["w_down"] = _matmul("mm_down_dw", act, df, "tn", BF16)
    dgate, dup, g_ffn_conv_w, g_ffn_conv_b = _colwise(
        "ffn_act_bwd", _ffn_act_back, [dact, gate_pre, up], [sp["ffn_conv_w"], sp["ffn_conv_b"]], [BF16, BF16], [FFN_K, 1], tc)
    gW["w_gate"] = _matmul("mm_gate_dw", h2, dgate, "tn", BF16, chips=True)
    gW["w_up"] = _matmul("mm_up_dw", h2, dup, "tn", BF16, chips=True)
    if ffn_grads_ready is not None:
        sp = dict(sp, ffn_pre_g=sp["ffn_pre_g"] + ffn_grads_ready({n: gW[n] for n in ("w_down", "w_gate", "w_up")})[0, 0])
    dh2 = _matmul("mm_gu_dx", dgate, W["w_gate"], "nt", F32, dup, W["w_up"], chips=True)

    def mid_back(dy, dh2, x1, mix, g_mp, g_fp):
        r2 = _rs(x1)
        xh = x1 * r2
        dx1 = dy + _rms_back(xh, r2, dh2 * g_fp)
        r1 = _rs(mix)
        mh = mix * r1
        return dx1, _rms_back(mh, r1, dx1 * g_mp), _colsum(dh2 * xh), _colsum(dx1 * mh)

    dx1, dmix, g_ffn_pre, g_mix_post = _rowwise("bwd_mid", mid_back, [dy, dh2, x1, mix], [sp["mix_post_g"], sp["ffn_pre_g"]],
                                                [(D, F32), (D, BF16)], [(1, D), (1, D)], ts)
    dab_out = _matmul("mm_out_dx", dmix, W["w_out"], "nt", F32)
    db_out = dab_out[:, cfg.MLAW:]
    gW["w_out"] = _matmul("mm_out_dw", ab_out, dmix, "tn", BF16)
    early_token = jnp.zeros((8, LANE), F32)
    if early_grads_ready is not None:
        early_token = early_grads_ready({n: gW[n] for n in ("w_down", "w_gate", "w_up", "w_out")})
        sp = dict(sp, ssm_norm_g=sp["ssm_norm_g"] + early_token[0, 0])

    dy_ssd, dz, g_ssm_norm = _ssd_post_bwd(cfg, db_out, y_ssd, z, sp["ssm_norm_g"])
    dxc, ddt_raw, g_dt_bias, g_a_log, g_d_skip = _ssd_bwd(cfg, dy_ssd, xc, dt_exp, a_exp, a_s, dskip_exp, hin, dt_raw,
                                                          dt_bias_pad, a_log_pad, expand)
    dxbc, g_ssm_conv_w, g_ssm_conv_b = _colwise("ssm_act_bwd", _ssm_act_back, [dxc, xbc], [sp["ssm_conv_w"], sp["ssm_conv_b"]],
                                                [BF16], [SSM_K, 1], tc)

    dQ, delta_t = _attn_dq(cfg, Qh, Kh, Vh, dab_out, a_out, lse, early_token)
    dK, dV = _attn_dkv(cfg, Qh, Kh, Vh, dab_out, lse_t, delta_t)
    dq, dkv, dkr = _mla_unpack(cfg, dQ, dK, dV, cos2, sin2)
    dcqn = _matmul("mm_uq_dx", dq, W["w_uq"], "nt", F32)
    dckvn = _matmul("mm_ukv_dx", dkv, W["w_ukv"], "nt", F32)
    gW["w_uq"] = _matmul("mm_uq_dw", cqn, dq, "tn", BF16)
    gW["w_ukv"] = _matmul("mm_ukv_dw", ckvn, dkv, "tn", BF16)

    def rms_back(x, dy, g):
        r = _rs(x)
        xh = x * r
        return _rms_back(xh, r, dy * g), _colsum(dy * xh)

    dc_q, g_q_norm = _rowwise("rms_q_bwd", rms_back, [c_q, dcqn], [sp["q_norm_g"]], [(cfg.QL, BF16)], [(1, cfg.QL)], ts)
    dc_kv, g_kv_norm = _rowwise("rms_kv_bwd", rms_back, [c_kv, dckvn], [sp["kv_norm_g"]], [(cfg.KVL, BF16)], [(1, cfg.KVL)], ts)

    du = jnp.concatenate([dc_q, dc_kv, dkr, dz.astype(BF16), dxbc, ddt_raw.astype(BF16)], axis=1)
    gW["w_in"] = _matmul("mm_in_dw", du, xn, "tn", BF16)
    if in_grad_ready is not None:
        token = in_grad_ready({n: gW[n] for n in ("w_in", "w_uq", "w_ukv")})
        sp = dict(sp, mix_pre_g=sp["mix_pre_g"] + token[0, 0])
    dxn = _matmul("mm_in_dx", du, W["w_in"], "nn", F32)

    def first_back(dx1, dxn, x, g):
        r = _rs(x)
        xh = x * r
        return dx1 + _rms_back(xh, r, dxn * g), _colsum(dxn * xh)

    grad_x, g_mix_pre = _rowwise("bwd_first", first_back, [dx1, dxn, x], [sp["mix_pre_g"]], [(D, F32)], [(1, D)], ts)

    gs = dict(mix_pre_g=g_mix_pre, q_norm_g=g_q_norm, kv_norm_g=g_kv_norm, ssm_conv_w=g_ssm_conv_w, ssm_conv_b=g_ssm_conv_b,
              dt_bias=g_dt_bias[:, :cfg.HS], a_log=g_a_log[:, :cfg.HS], d_skip=g_d_skip[:, :cfg.HS], ssm_norm_g=g_ssm_norm,
              mix_post_g=g_mix_post, ffn_pre_g=g_ffn_pre, ffn_conv_w=g_ffn_conv_w, ffn_conv_b=g_ffn_conv_b,
              ffn_post_g=g_ffn_post)
    return loss, grad_x, gW, gs


def _to_kernel_layout(cfg, name, w):
    if name == "w_in":
        a = cfg.o_kr + ROPE
        return jnp.concatenate([w[:a], jnp.zeros((LANE - ROPE, w.shape[1]), w.dtype), w[a:],
                                jnp.zeros((LANE - cfg.HS, w.shape[1]), w.dtype)], axis=0)
    if name in ("w_uq", "w_ukv"):
        per = NOPE + (ROPE if name == "w_uq" else VH)
        return jnp.concatenate([w[:, h * per:h * per + NOPE] for h in range(cfg.H)]
                               + [w[:, h * per + NOPE:(h + 1) * per] for h in range(cfg.H)], axis=1)
    return w


def _from_kernel_layout(cfg, name, g):
    if name == "w_in":
        return jnp.concatenate([g[:cfg.o_kr + ROPE], g[cfg.o_z:cfg.o_dt + cfg.HS]], axis=0)
    if name in ("w_uq", "w_ukv"):
        second = ROPE if name == "w_uq" else VH
        base = cfg.H * NOPE
        parts = []
        for h in range(cfg.H):
            parts += [g[:, h * NOPE:(h + 1) * NOPE], g[:, base + h * second:base + (h + 1) * second]]
        return jnp.concatenate(parts, axis=1)
    return g


def _cols_to_chips(w):
    r, c = w.shape
    return w.reshape(r, N_CHIPS, c // N_CHIPS).transpose(1, 0, 2)


def _chips_to_cols(g):
    k, r, cs = g.shape
    return g.transpose(1, 0, 2).reshape(r, k * cs)


_CHIP_MAJOR = ("w_gate", "w_up")
_RELAYOUT = ("w_uq", "w_ukv")
_LAYOUT_ROWS = 256


def _w_in_layout(cfg, wg):
    _, rs, d = wg.shape
    tc = _pick(d, _LAYOUT_ROWS, LANE)

    def body(w_ref, o_ref):
        o_ref[...] = _to_kernel_layout(cfg, "w_in", jnp.concatenate([w_ref[k] for k in range(N_CHIPS)], axis=0))

    return pl.pallas_call(
        body, name="layout_w_in", grid=(d // tc,),
        in_specs=[pl.BlockSpec((N_CHIPS, rs, tc), lambda j: (0, 0, j))], out_specs=pl.BlockSpec((cfg.EXT, tc), lambda j: (0, j)),
        out_shape=jax.ShapeDtypeStruct((cfg.EXT, d), wg.dtype), compiler_params=_params(("parallel",)),
    )(wg)


def _w_in_grad_to_chips(cfg, g):
    _, d = g.shape
    rs = cfg.IN_COLS // N_CHIPS
    tc = _pick(d, _LAYOUT_ROWS, LANE)

    def body(g_ref, o_ref):
        nat = _from_kernel_layout(cfg, "w_in", g_ref[...])
        for k in range(N_CHIPS):
            o_ref[k] = nat[k * rs:(k + 1) * rs]

    return pl.pallas_call(
        body, name="layout_grad_w_in", grid=(d // tc,),
        in_specs=[pl.BlockSpec((cfg.EXT, tc), lambda j: (0, j))], out_specs=pl.BlockSpec((N_CHIPS, rs, tc), lambda j: (0, 0, j)),
        out_shape=jax.ShapeDtypeStruct((N_CHIPS, rs, d), g.dtype), compiler_params=_params(("parallel",)),
    )(g)


def _gathered_to_kernel(cfg, name, wg):
    if name in _CHIP_MAJOR:
        return wg
    if name == "w_in":
        return _w_in_layout(cfg, wg)
    if name not in _RELAYOUT:
        return wg.reshape(wg.shape[0] * wg.shape[1], wg.shape[2])
    _, rows, cs = wg.shape
    tr = _pick(rows, _LAYOUT_ROWS, 16)

    def body(w_ref, o_ref):
        o_ref[...] = _to_kernel_layout(cfg, name, jnp.concatenate([w_ref[k] for k in range(N_CHIPS)], axis=1))

    wide = jax.eval_shape(lambda w: _to_kernel_layout(cfg, name, w), jax.ShapeDtypeStruct((rows, N_CHIPS * cs), wg.dtype)).shape[1]
    return pl.pallas_call(
        body, name="layout_" + name, grid=(rows // tr,),
        in_specs=[pl.BlockSpec((N_CHIPS, tr, cs), lambda i: (0, i, 0))], out_specs=pl.BlockSpec((tr, wide), lambda i: (i, 0)),
        out_shape=jax.ShapeDtypeStruct((rows, wide), wg.dtype), compiler_params=_params(("parallel",)),
    )(wg)


def _grad_to_chips(cfg, name, g):
    if name in _CHIP_MAJOR:
        return g
    if name == "w_in":
        return _w_in_grad_to_chips(cfg, g)
    if name not in _RELAYOUT:
        return g.reshape(N_CHIPS, g.shape[0] // N_CHIPS, g.shape[1])
    rows, wide = g.shape
    tr = _pick(rows, _LAYOUT_ROWS, 16)
    cs = jax.eval_shape(lambda v: _from_kernel_layout(cfg, name, v), g).shape[1] // N_CHIPS

    def body(g_ref, o_ref):
        nat = _from_kernel_layout(cfg, name, g_ref[...])
        for k in range(N_CHIPS):
            o_ref[k] = nat[:, k * cs:(k + 1) * cs]

    return pl.pallas_call(
        body, name="layout_grad_" + name, grid=(rows // tr,),
        in_specs=[pl.BlockSpec((tr, wide), lambda i: (i, 0))], out_specs=pl.BlockSpec((N_CHIPS, tr, cs), lambda i: (0, i, 0)),
        out_shape=jax.ShapeDtypeStruct((N_CHIPS, rows, cs), g.dtype), compiler_params=_params(("parallel",)),
    )(g)


def _me():
    return lax.axis_index("x"), lax.axis_index("y"), lax.axis_index("c")


def _other_chips(x, y):
    return [(1 - x, y), (x, 1 - y), (1 - x, 1 - y)]


_ANY = pl.BlockSpec(memory_space=pl.ANY)


BLOCK_ELEMS = 1 << 19
BLOCK_ELEMS_FEW = 1 << 20


def _row_block(rows, cols, mult, elems=BLOCK_ELEMS):
    return _pick(rows, max(mult, elems // cols // mult * mult), mult)


def _scalar(v):
    return v.astype(I32).reshape(1)


def _blocks2d(r, c, mult, elems=BLOCK_ELEMS):
    if r % mult == 0:
        tr = _row_block(r, c, mult, elems)
        return (tr, c), r // tr, lambda i: (i, 0)
    tc = _pick(c, max(LANE, elems // r // LANE * LANE), LANE)
    return (r, tc), c // tc, lambda i: (0, i)


def _by_rows(rows):
    return rows % 32 == 0


def _half_shape(rows, cols):
    return (rows // 2, cols) if _by_rows(rows) else (rows, cols // 2)


def _half_blocks(rows, cols, mult, elems=BLOCK_ELEMS):
    hr, hc = _half_shape(rows, cols)
    block, n, part = _blocks2d(hr, hc, mult, elems)
    assert (hr % mult == 0) == _by_rows(rows), (rows, cols, mult)
    full = (lambda h, i: (h * n + i, 0)) if _by_rows(rows) else (lambda h, i: (0, h * n + i))
    return block, n, full, part


def _half(ref, k, half):
    hr, hc = _half_shape(ref.shape[1], ref.shape[2])
    if _by_rows(ref.shape[1]):
        return ref.at[k, pl.ds(pl.multiple_of(half * hr, 16), hr), :]
    return ref.at[k, :, pl.ds(pl.multiple_of(half * hc, LANE), hc)]


def _shard_blocks(w, br, bc):
    if w.shape[0] == 1:
        def write(ref, v):
            ref[...] = v
        return (lambda f: pl.BlockSpec((None, br, bc), lambda *a: (0, *f(*a)))), (lambda ref: ref[...]), write
    assert w.shape[1] == 1 and br == w.shape[0], w.shape

    def write_rows(ref, v):
        ref[:, 0, :] = v
    return (lambda f: pl.BlockSpec((br, 1, bc), lambda *a: (0, 0, f(*a)[1]))), (lambda ref: ref[:, 0, :]), write_rows


def _stage_shard(name, w, chip, after=None):
    rs, cs = w.shape[0] * w.shape[1], w.shape[2]
    (br, bc), n, idx = _blocks2d(rs, cs, 16, BLOCK_ELEMS_FEW)
    spec, get, _ = _shard_blocks(w, br, bc)

    def body(chip_ref, w_ref, *refs):
        refs[-1][...] = get(w_ref).astype(BF16)

    return pl.pallas_call(
        body, name="stage_" + name,
        grid_spec=pltpu.PrefetchScalarGridSpec(
            num_scalar_prefetch=1, grid=(n,),
            in_specs=[spec(lambda i, chip_ref: idx(i))] + ([] if after is None else [_ANY]),
            out_specs=pl.BlockSpec((None, br, bc), lambda i, chip_ref: (chip_ref[0], *idx(i)))),
        out_shape=jax.ShapeDtypeStruct((N_CHIPS, rs, cs), BF16),
        compiler_params=_params(("parallel",)),
    )(_scalar(chip), w, *([] if after is None else [after]))


_HBM = pl.BlockSpec(memory_space=pltpu.HBM)
_SEM = pl.BlockSpec(memory_space=pltpu.SEMAPHORE)
_EFFECT = pltpu.SideEffectType.DATAFLOW_SIDE_EFFECTING


def _split_start(name, bufs, n_copies, copies, after):
    n = len(bufs)

    def body(*refs):
        for cp in copies(refs[:n], refs[n + 1], refs[n + 2]):
            cp.start()
        refs[-1][...] = jnp.zeros_like(refs[-1])

    res = pl.pallas_call(
        body, name=name,
        out_shape=(pltpu.SemaphoreType.DMA((n_copies,)), pltpu.SemaphoreType.DMA((n_copies,)),
                   *[pltpu.HBM(b.shape, b.dtype) for b in bufs], jax.ShapeDtypeStruct((8, LANE), F32)),
        in_specs=[_HBM] * n + [_ANY], out_specs=(_SEM, _SEM, *[_HBM] * n, pl.BlockSpec(memory_space=pltpu.VMEM)),
        input_output_aliases={i: 2 + i for i in range(n)},
        compiler_params=pltpu.CompilerParams(has_side_effects=_EFFECT),
    )(*[pltpu.with_memory_space_constraint(b, pltpu.HBM) for b in bufs], after)
    return res[0], res[1], list(res[2:2 + n]), res[-1]


def _split_wait(name, send_sems, recv_sems, bufs, after, copies):
    n = len(bufs)

    def body(*refs):
        for cp in copies(refs[:n], refs[n], refs[n + 1]):
            cp.wait_send()
            cp.wait_recv()

    return list(pl.pallas_call(
        body, name=name, out_shape=[pltpu.HBM(b.shape, b.dtype) for b in bufs],
        in_specs=[_HBM] * n + [_SEM, _SEM, _ANY], out_specs=[_HBM] * n,
        input_output_aliases={i: i for i in range(n)},
        compiler_params=pltpu.CompilerParams(has_side_effects=_EFFECT),
    )(*bufs, send_sems, recv_sems, after))


def _gather_to_chips(bufs, send_sems, recv_sems):
    x, y, c = _me()
    return [pltpu.make_async_remote_copy(src_ref=_half(b, 2 * x + y, c), dst_ref=_half(b, 2 * x + y, c),
                                         send_sem=send_sems.at[3 * w + j], recv_sem=recv_sems.at[3 * w + j],
                                         device_id=(cx, cy, c), device_id_type=MESH_ID)
            for w, b in enumerate(bufs) for j, (cx, cy) in enumerate(_other_chips(x, y))]


def _gather_to_sibling(bufs, send_sems, recv_sems):
    x, y, c = _me()
    return [pltpu.make_async_remote_copy(src_ref=_half(b, 2 * cx + cy, c), dst_ref=_half(b, 2 * cx + cy, c),
                                         send_sem=send_sems.at[3 * w + j], recv_sem=recv_sems.at[3 * w + j],
                                         device_id=(x, y, 1 - c), device_id_type=MESH_ID)
            for w, b in enumerate(bufs) for j, (cx, cy) in enumerate(_other_chips(x, y))]


def _pair_exchange(name, grads):
    n = len(grads)

    def body(*refs):
        ins, outs, send_sems, recv_sems = refs[:n], refs[n:2 * n], refs[2 * n], refs[2 * n + 1]
        x, y, c = _me()
        cps = []
        for w, (g_ref, o_ref) in enumerate(zip(ins, outs)):
            cps.append(pltpu.make_async_remote_copy(src_ref=_half(g_ref, slice(None), 1 - c), dst_ref=o_ref,
                                                    send_sem=send_sems.at[w], recv_sem=recv_sems.at[w],
                                                    device_id=(x, y, 1 - c), device_id_type=MESH_ID))
            cps[-1].start()
        for cp in cps:
            cp.wait()

    return pl.pallas_call(
        body, name="pair_exchange_" + name, in_specs=[_ANY] * n, out_specs=[_ANY] * n,
        out_shape=[jax.ShapeDtypeStruct((g.shape[0], *_half_shape(g.shape[1], g.shape[2])), g.dtype) for g in grads],
        scratch_shapes=[pltpu.SemaphoreType.DMA((n,)), pltpu.SemaphoreType.DMA((n,))],
    )(*grads)


def _pair_copies(grads, lands, send_sems, recv_sems):
    x, y, c = _me()
    return [pltpu.make_async_remote_copy(src_ref=_half(g_ref, slice(None), 1 - c), dst_ref=l_ref, send_sem=send_sems.at[w],
                                         recv_sem=recv_sems.at[w], device_id=(x, y, 1 - c), device_id_type=MESH_ID)
            for w, (g_ref, l_ref) in enumerate(zip(grads, lands))]


def _pair_exchange_start(name, grads):
    n = len(grads)
    lands = [lax.empty((g.shape[0], *_half_shape(g.shape[1], g.shape[2])), g.dtype) for g in grads]
    send_sems, recv_sems, bufs, token = _split_start(
        "pair_exchange_start_" + name, [*grads, *lands], n, lambda refs, ss, rs: _pair_copies(refs[:n], refs[n:], ss, rs),
        jnp.zeros((8, LANE), F32))
    return (send_sems, recv_sems, bufs), token


def _pair_exchange_wait(name, state, after):
    send_sems, recv_sems, bufs = state
    n = len(bufs) // 2
    bufs = _split_wait("pair_exchange_wait_" + name, send_sems, recv_sems, bufs, after,
                       lambda refs, ss, rs: _pair_copies(refs[:n], refs[n:], ss, rs))
    return bufs[:n], bufs[n:]


def _pair_sum(name, g, theirs, c):
    (br, bc), nb, full, part = _half_blocks(g.shape[1], g.shape[2], 16, 2 * BLOCK_ELEMS_FEW)

    def body(c_ref, a_ref, b_ref, o_ref):
        o_ref[...] = (a_ref[...].astype(F32) + b_ref[...].astype(F32)).astype(o_ref.dtype)

    return pl.pallas_call(
        body, name="pair_sum_" + name,
        grid_spec=pltpu.PrefetchScalarGridSpec(
            num_scalar_prefetch=1, grid=(N_CHIPS, nb),
            in_specs=[pl.BlockSpec((None, br, bc), lambda k, i, c_ref: (k, *full(c_ref[0], i))),
                      pl.BlockSpec((None, br, bc), lambda k, i, c_ref: (k, *part(i)))],
            out_specs=pl.BlockSpec((None, br, bc), lambda k, i, c_ref: (k, *part(i)))),
        out_shape=jax.ShapeDtypeStruct(theirs.shape, BF16),
        compiler_params=_params(("parallel", "parallel")),
    )(_scalar(c), g, theirs)


def _chip_copies(srcs, lands, send_sems, recv_sems):
    x, y, c = _me()
    return [pltpu.make_async_remote_copy(src_ref=s_ref.at[2 * cx + cy], dst_ref=l_ref.at[j], send_sem=send_sems.at[3 * w + j],
                                         recv_sem=recv_sems.at[3 * w + j], device_id=(cx, cy, c), device_id_type=MESH_ID)
            for w, (s_ref, l_ref) in enumerate(zip(srcs, lands)) for j, (cx, cy) in enumerate(_other_chips(x, y))]


def _chip_exchange_start(name, sums):
    n = len(sums)
    lands = [lax.empty((3,) + s.shape[1:], s.dtype) for s in sums]
    send_sems, recv_sems, bufs, token = _split_start(
        "chip_exchange_start_" + name, [*sums, *lands], 3 * n, lambda refs, ss, rs: _chip_copies(refs[:n], refs[n:], ss, rs),
        jnp.zeros((8, LANE), F32))
    return send_sems, recv_sems, bufs[:n], bufs[n:], token


def _chip_exchange_wait(name, send_sems, recv_sems, sums, lands, after):
    n = len(sums)
    bufs = _split_wait("chip_exchange_wait_" + name, send_sems, recv_sems, [*sums, *lands], after,
                       lambda refs, ss, rs: _chip_copies(refs[:n], refs[n:], ss, rs))
    return bufs[:n], bufs[n:]


def _chip_sum(name, sums, theirs, chip):
    _, h, cs = sums.shape
    (br, bc), nb, idx = _blocks2d(h, cs, 16, BLOCK_ELEMS_FEW)

    def body(chip_ref, s_ref, t_ref, o_ref):
        acc = s_ref[...].astype(F32)
        for k in range(3):
            acc = acc + t_ref[k].astype(F32)
        o_ref[...] = acc

    return pl.pallas_call(
        body, name="chip_sum_" + name,
        grid_spec=pltpu.PrefetchScalarGridSpec(
            num_scalar_prefetch=1, grid=(nb,),
            in_specs=[pl.BlockSpec((None, br, bc), lambda i, chip_ref: (chip_ref[0], *idx(i))),
                      pl.BlockSpec((3, br, bc), lambda i, chip_ref: (0, *idx(i)))],
            out_specs=pl.BlockSpec((br, bc), lambda i, chip_ref: idx(i))),
        out_shape=jax.ShapeDtypeStruct((h, cs), F32),
        compiler_params=_params(("parallel",)),
    )(_scalar(chip), sums, theirs)


def _sibling_exchange(name, halves):
    n = len(halves)

    def body(*refs):
        ins, outs, send_sems, recv_sems = refs[:n], refs[n:2 * n], refs[2 * n], refs[2 * n + 1]
        x, y, c = _me()
        cps = []
        for w, (h_ref, o_ref) in enumerate(zip(ins, outs)):
            cps.append(pltpu.make_async_remote_copy(src_ref=h_ref, dst_ref=o_ref, send_sem=send_sems.at[w], recv_sem=recv_sems.at[w],
                                                    device_id=(x, y, 1 - c), device_id_type=MESH_ID))
            cps[-1].start()
        for cp in cps:
            cp.wait()

    return pl.pallas_call(
        body, name="sibling_exchange_" + name, in_specs=[_ANY] * n, out_specs=[_ANY] * n,
        out_shape=[jax.ShapeDtypeStruct(h.shape, h.dtype) for h in halves],
        scratch_shapes=[pltpu.SemaphoreType.DMA((n,)), pltpu.SemaphoreType.DMA((n,))],
    )(*halves)


def _allreduce_small(name, vec, after):
    def body(v_ref, after_ref, o_ref, buf_ref, send_sems, recv_sems):
        x, y, c = _me()
        me = 4 * x + 2 * y + c
        cps = []
        for p in range(1, 8):
            px, py, pc = x ^ (p >> 2), y ^ ((p >> 1) & 1), c ^ (p & 1)
            cps.append(pltpu.make_async_remote_copy(src_ref=v_ref, dst_ref=buf_ref.at[me], send_sem=send_sems.at[p - 1],
                                                    recv_sem=recv_sems.at[p - 1], device_id=(px, py, pc), device_id_type=MESH_ID))
            cps[-1].start()
        buf_ref[me] = v_ref[...]
        for p in range(1, 8):
            theirs = buf_ref.at[me ^ p]
            pltpu.make_async_remote_copy(src_ref=theirs, dst_ref=theirs, send_sem=send_sems.at[p - 1], recv_sem=recv_sems.at[p - 1],
                                         device_id=(x, y, c), device_id_type=MESH_ID).wait_recv()
        for cp in cps:
            cp.wait_send()
        acc = buf_ref[0]
        for k in range(1, 8):
            acc = acc + buf_ref[k]
        o_ref[...] = acc

    vm = pl.BlockSpec(memory_space=pltpu.VMEM)
    return pl.pallas_call(
        body, name=name, in_specs=[vm, _ANY], out_specs=vm, out_shape=jax.ShapeDtypeStruct(vec.shape, F32),
        scratch_shapes=[pltpu.VMEM((8,) + vec.shape, F32), pltpu.SemaphoreType.DMA((7,)), pltpu.SemaphoreType.DMA((7,))],
    )(vec, after)


def _adam_math(w, g, m, v):
    m = ADAM_B1 * m + (1.0 - ADAM_B1) * g
    v = ADAM_B2 * v + (1.0 - ADAM_B2) * (g * g)
    m_hat = m / (1.0 - ADAM_B1 ** ADAM_STEP)
    v_hat = v / (1.0 - ADAM_B2 ** ADAM_STEP)
    return -ADAM_LR * (m_hat / (jnp.sqrt(v_hat) + ADAM_EPS) + ADAM_WD * w), m, v


def _adamw(name, w, g, m, v):
    R, C = w.shape
    tr = _row_block(R, C, 8)

    def body(w_ref, g_ref, m_ref, v_ref, d_ref, nm_ref, nv_ref):
        d_ref[...], nm_ref[...], nv_ref[...] = _adam_math(w_ref[...], g_ref[...], m_ref[...], v_ref[...])

    blk = pl.BlockSpec((tr, C), lambda i: (i, 0))
    return pl.pallas_call(
        body, name=name, grid=(R // tr,), in_specs=[blk] * 4, out_specs=[blk] * 3,
        out_shape=[jax.ShapeDtypeStruct((R, C), F32)] * 3, compiler_params=_params(("parallel",)),
    )(w, g, m, v)


def _adamw_halves(name, w, mine, theirs, m, v, c):
    rs, cs = w.shape[0] * w.shape[1], w.shape[2]
    (br, bc), nb, whole, half = _half_blocks(rs, cs, 8)
    spec, get, put = _shard_blocks(w, br, bc)

    def body(c_ref, w_ref, a_ref, b_ref, m_ref, v_ref, g_ref, d_ref, nm_ref, nv_ref):
        g = jnp.where(pl.program_id(0) == c_ref[0], a_ref[...], b_ref[...])
        put(g_ref, g)
        for ref, val in zip((d_ref, nm_ref, nv_ref), _adam_math(get(w_ref), g, get(m_ref), get(v_ref))):
            put(ref, val)

    full = spec(lambda s, i, c_ref: whole(s, i))
    part = pl.BlockSpec((br, bc), lambda s, i, c_ref: half(i))
    return pl.pallas_call(
        body, name=name,
        grid_spec=pltpu.PrefetchScalarGridSpec(num_scalar_prefetch=1, grid=(2, nb), in_specs=[full, part, part, full, full],
                                               out_specs=[full] * 4),
        out_shape=[jax.ShapeDtypeStruct(w.shape, F32)] * 4, compiler_params=_params(("parallel", "parallel")),
    )(_scalar(c), w, mine, theirs, m, v)


def _pack_small(arrs):
    flat = jnp.concatenate([a.reshape(-1) for a in arrs])
    n = -(-flat.shape[0] // (8 * LANE)) * 8 * LANE
    return jnp.pad(flat, (0, n - flat.shape[0])).reshape(8, n // 8)


def _unpack_small(vec, shapes):
    flat, out, off = vec.reshape(-1), [], 0
    for s in shapes:
        out.append(flat[off:off + s[0] * s[1]].reshape(s))
        off += s[0] * s[1]
    return out


class _LateWeights:
    def __init__(self, cfg, tag, names, staged, after):
        self.cfg, self.tag, self.names, self.k = cfg, tag, names, 3 * len(names)
        self.send, self.recv, self.bufs, self.token = _split_start(f"gather_{tag}_chips_start", staged, self.k, _gather_to_chips,
                                                                    after)

    def pass_on(self, after):
        bufs = _split_wait(f"gather_{self.tag}_chips_wait", self.send, self.recv, self.bufs, after, _gather_to_chips)
        self.send, self.recv, self.bufs, token = _split_start(f"gather_{self.tag}_sibling_start", bufs, self.k, _gather_to_sibling,
                                                               self.token)
        return token

    def arrived(self, after):
        bufs = _split_wait(f"gather_{self.tag}_sibling_wait", self.send, self.recv, self.bufs, after, _gather_to_sibling)
        return {n: _gathered_to_kernel(self.cfg, n, b) for n, b in zip(self.names, bufs)}


def _step(cfg, a):
    chip = 2 * lax.axis_index("x") + lax.axis_index("y")
    core = lax.axis_index("c")
    big = BIG

    ffn = ("w_gate", "w_up", "w_down")
    first = ("w_in", "w_uq", "w_ukv")
    staged = {"w_in": _stage_shard("w_in", a["w_in"], chip)}
    in_weight = _LateWeights(cfg, "in", ("w_in",), [staged["w_in"]], jnp.zeros((8, LANE), F32))
    staged.update({n: _stage_shard(n, a[n], chip, in_weight.token) for n in big if n != "w_in"})
    in_sibling_leg = in_weight.pass_on(staged[big[-1]])

    sp = {n: a[n] for n in SMALL}
    sharded = _pack_small([a[n] for n in SMALL_SHARDED])
    slot = jnp.where(lax.broadcasted_iota(I32, (N_CHIPS,) + sharded.shape, 0) == chip, 0.5 * sharded[None], 0.0)
    allp = _allreduce_small("allgather_small", slot.reshape(N_CHIPS * 8, -1), in_sibling_leg).reshape((N_CHIPS,) + sharded.shape)
    per_chip = [_unpack_small(allp[ch], [a[n].shape for n in SMALL_SHARDED]) for ch in range(N_CHIPS)]
    for k, n in enumerate(SMALL_SHARDED):
        sp[n] = jnp.concatenate([per_chip[ch][k] for ch in range(N_CHIPS)], axis=1)
    W = in_weight.arrived(allp)

    mla_weights = _LateWeights(cfg, "mla", first[1:], [staged[n] for n in first[1:]], W["w_in"])
    out_weight = _LateWeights(cfg, "out", ("w_out",), [staged["w_out"]], mla_weights.token)
    ffn_weights = _LateWeights(cfg, "ffn", ffn, [staged[n] for n in ffn], out_weight.token)
    sp["mix_pre_g"] = sp["mix_pre_g"] + (mla_weights.token[0, 0] + out_weight.token[0, 0] + ffn_weights.token[0, 0])

    state = {}

    def ffn_grads_ready(grads):
        state["ffn_pairs"], token = _pair_exchange_start("ffn", [_grad_to_chips(cfg, n, grads[n]) for n in ffn_grads])
        return token

    def pair_sums(names, grads, theirs):
        return [_pair_sum(n, g, t, core) for n, g, t in zip(names, grads, theirs)]

    def early_grads_ready(grads):
        g_out = [_grad_to_chips(cfg, "w_out", grads["w_out"])]
        g_ffn, t_ffn = _pair_exchange_wait("ffn", state["ffn_pairs"], g_out[0])
        sums = pair_sums(ffn_grads, g_ffn, t_ffn) + pair_sums(["w_out"], g_out, _pair_exchange("out", g_out))
        state["early"] = _chip_exchange_start("early", sums)
        return state["early"][-1]

    def reduced_halves(tag, names, after):
        send_sems, recv_sems, s_bufs, l_bufs, _ = state[tag]
        s_bufs, l_bufs = _chip_exchange_wait(tag, send_sems, recv_sems, s_bufs, l_bufs, after)
        return [_chip_sum(n, s, t, chip) for n, s, t in zip(names, s_bufs, l_bufs)]

    def in_grad_ready(grads):
        grads = [_grad_to_chips(cfg, n, grads[n]) for n in first]
        state["rest"] = _chip_exchange_start("rest", pair_sums(first, grads, _pair_exchange("rest", grads)))
        return state["rest"][-1]

    ffn_grads = ("w_down", "w_gate", "w_up")
    early = ffn_grads + ("w_out",)
    loss, grad_x, gW, gs = _local_grads(cfg, a["x"], a["loss_target"], W, sp, mla_weights, out_weight, ffn_weights,
                                        ffn_grads_ready, early_grads_ready, in_grad_ready)
    out = {"grad_x": grad_x}

    def adamw(names, mine, theirs):
        for n, gm, gt in zip(names, mine, theirs):
            out["grad_" + n], out["delta_" + n], out["new_m_" + n], out["new_v_" + n] = _adamw_halves(
                "adamw_" + n, a[n], gm, gt, a["m_" + n], a["v_" + n], core)

    mine = reduced_halves("early", early, grad_x)
    adamw(early, mine, _sibling_exchange("early", mine))
    mine = reduced_halves("rest", first, out["new_v_" + early[-1]])
    theirs = _sibling_exchange("rest", mine)
    adamw(first, mine, theirs)

    shapes = [gs[n].shape for n in SMALL] + [(1, LANE)]
    red = _unpack_small(_allreduce_small("allreduce_small", _pack_small([gs[n] for n in SMALL] + [loss]), theirs[0]), shapes)
    g_small = dict(zip(SMALL, red[:-1]))
    for n in SMALL_SHARDED:
        cs = a[n].shape[1]
        g_small[n] = lax.dynamic_slice_in_dim(g_small[n], chip * cs, cs, axis=1)
    out["loss"] = red[-1][0, 0]
    sshapes = [a[n].shape for n in SMALL]
    d, nm, nv = _adamw("adamw_small", _pack_small([a[n] for n in SMALL]), _pack_small([g_small[n] for n in SMALL]),
                       _pack_small([a["m_" + n] for n in SMALL]), _pack_small([a["v_" + n] for n in SMALL]))
    for n, dd, mm, vv in zip(SMALL, _unpack_small(d, sshapes), _unpack_small(nm, sshapes), _unpack_small(nv, sshapes)):
        out["grad_" + n], out["delta_" + n], out["new_m_" + n], out["new_v_" + n] = g_small[n], dd, mm, vv
    return out


def kernel(x, mix_pre_g, w_in, q_norm_g, w_uq, kv_norm_g, w_ukv, ssm_conv_w, ssm_conv_b, dt_bias, a_log, d_skip, ssm_norm_g, w_out, mix_post_g, ffn_pre_g, w_gate, w_up, ffn_conv_w, ffn_conv_b, w_down, ffn_post_g, loss_target, m_mix_pre_g, m_w_in, m_q_norm_g, m_w_uq, m_kv_norm_g, m_w_ukv, m_ssm_conv_w, m_ssm_conv_b, m_dt_bias, m_a_log, m_d_skip, m_ssm_norm_g, m_w_out, m_mix_post_g, m_ffn_pre_g, m_w_gate, m_w_up, m_ffn_conv_w, m_ffn_conv_b, m_w_down, m_ffn_post_g, v_mix_pre_g, v_w_in, v_q_norm_g, v_w_uq, v_kv_norm_g, v_w_ukv, v_ssm_conv_w, v_ssm_conv_b, v_dt_bias, v_a_log, v_d_skip, v_ssm_norm_g, v_w_out, v_mix_post_g, v_ffn_pre_g, v_w_gate, v_w_up, v_ffn_conv_w, v_ffn_conv_b, v_w_down, v_ffn_post_g):
    args = dict(locals())
    def given(k, v):
        if k in ("w_in", "m_w_in", "v_w_in"):
            return jnp.transpose(v, (2, 0, 1))
        return v if k.removeprefix("m_").removeprefix("v_") in BIG or v.ndim < 3 else v[0]

    out = _step(_FULL, {k: given(k, v) for k, v in args.items()})
    res = [out["loss"], out["grad_x"][None]]
    for pre in ("grad_", "delta_", "new_m_", "new_v_"):
        for n in WEIGHTS:
            o = out[pre + n]
            res.append(jnp.transpose(o, (1, 2, 0)) if n == "w_in" else o if n in BIG or args[n].ndim < 3 else o[None])
    return tuple(res)
```

```python
import functools
import math

import jax
import jax.numpy as jnp
from jax import lax
from jax.experimental import pallas as pl
from jax.experimental.pallas import tpu as pltpu

F32, BF16, I32 = jnp.float32, jnp.bfloat16, jnp.int32
NN = (((1,), (0,)), ((), ()))
NT = (((1,), (1,)), ((), ()))
TN = (((0,), (0,)), ((), ()))
HI = lax.Precision.HIGHEST
MESH_ID = pl.DeviceIdType.MESH

EPS = 1e-6
CHUNK = 64
NOPE, ROPE, VH = 128, 64, 128
ROPE_THETA = 10000.0
HP, NST = 64, 128
SSM_K, FFN_K = 4, 3
LANE = 128
N_CHIPS = 4
VMEM_LIMIT = 52 * 1024 * 1024
MM_TILE, MM_TILE_K = 1408, 2816

ADAM_LR, ADAM_B1, ADAM_B2, ADAM_EPS, ADAM_WD, ADAM_STEP = 0.001, 0.9, 0.999, 1e-08, 0.01, 10


class _Cfg:
    def __init__(self, S, D, QL, KVL, H, HS, G, DFF, T):
        self.S, self.D, self.QL, self.KVL, self.H, self.HS, self.G, self.DFF, self.T = S, D, QL, KVL, H, HS, G, DFF, T
        self.INNER = HS * HP
        self.CONVCH = self.INNER + 2 * G * NST
        self.QW = H * (NOPE + ROPE)
        self.KVW = H * (NOPE + VH)
        self.MLAW = H * VH
        self.MIXW = self.MLAW + self.INNER
        self.IN_COLS = QL + KVL + ROPE + self.INNER + self.CONVCH + HS
        natural, at = {}, 0
        for name, w in (("c_q", QL), ("c_kv", KVL), ("kr", ROPE), ("z", self.INNER), ("xbc", self.CONVCH), ("dt", HS)):
            natural[name] = (at, w)
            at += w
        self.seg, taken = {}, []
        for name in sorted(natural, key=lambda n: -natural[n][1]):
            w = -(-natural[name][1] // LANE) * LANE
            off = next(o for o in range(0, 64 * w, w) if all(o + w <= t or o >= t + tw for t, tw in taken))
            taken.append((off, w))
            self.seg[name] = (off, w) + natural[name]
        self.EXT = max(o + w for o, w in taken)
        self.NPAIR = HS // 2
        self.REP = HS // G

    def window(self, name):
        off, w, _, _ = self.seg[name]
        return w, off // w


_FULL = _Cfg(S=2048, D=2048, QL=768, KVL=512, H=8, HS=16, G=2, DFF=5632, T=256)
BIG = ("w_in", "w_uq", "w_ukv", "w_out", "w_gate", "w_up", "w_down")

SMALL = ("mix_pre_g", "q_norm_g", "kv_norm_g", "ssm_conv_w", "ssm_conv_b", "dt_bias", "a_log", "d_skip", "ssm_norm_g",
         "mix_post_g", "ffn_pre_g", "ffn_conv_w", "ffn_conv_b", "ffn_post_g")
SMALL_SHARDED = ("ssm_conv_w", "ffn_conv_w")
WEIGHTS = ("mix_pre_g", "w_in", "q_norm_g", "w_uq", "kv_norm_g", "w_ukv", "ssm_conv_w", "ssm_conv_b", "dt_bias", "a_log",
           "d_skip", "ssm_norm_g", "w_out", "mix_post_g", "ffn_pre_g", "w_gate", "w_up", "ffn_conv_w", "ffn_conv_b",
           "w_down", "ffn_post_g")


def _pick(n, target, mult):
    best = None
    for d in range(mult, min(n, target) + 1, mult):
        if n % d == 0:
            best = d
    return best if best is not None else n


def _params(sem=None):
    kw = dict(vmem_limit_bytes=VMEM_LIMIT)
    if sem is not None:
        kw["dimension_semantics"] = sem
    return pltpu.CompilerParams(**kw)


def _dot(a, b, dims=NN, precision=None):
    return lax.dot_general(a, b, dims, preferred_element_type=F32, precision=precision)


def _sigmoid(x):
    return 1.0 / (1.0 + jnp.exp(-x))


def _rs(x):
    return lax.rsqrt(jnp.mean(x * x, axis=-1, keepdims=True) + EPS)


def _rms_back(xh, r, dn):
    return r * (dn - xh * jnp.mean(dn * xh, axis=-1, keepdims=True))


def _colsum(v):
    return jnp.sum(v, axis=0, keepdims=True)


def _matmul(name, a, b, mode, out_dtype, a2=None, b2=None, chips=False):
    cs = None
    if mode == "nn":
        (M, K), N = a.shape, b.shape[-1]
        if chips:
            cs, N = N, N_CHIPS * N
    elif mode == "nt":
        (M, K), N = a.shape, b.shape[-2]
        if chips:
            cs = b.shape[-1]
    else:
        (K, M), N = a.shape, b.shape[1]
        if chips:
            cs = N // N_CHIPS
    tm = _pick(M, MM_TILE, LANE)
    tn = _pick(cs if chips and mode != "nt" else N, MM_TILE, LANE)
    tk = _pick(cs, MM_TILE, LANE) if chips and mode == "nt" else _pick(K, MM_TILE_K, LANE)
    nk = K // tk
    dims = {"nn": NN, "nt": NT, "tn": TN}[mode]
    a_spec = pl.BlockSpec((tk, tm), lambda i, j, k: (k, i)) if mode == "tn" else pl.BlockSpec((tm, tk), lambda i, j, k: (i, k))
    b_spec = pl.BlockSpec((tn, tk), lambda i, j, k: (j, k)) if mode == "nt" else pl.BlockSpec((tk, tn), lambda i, j, k: (k, j))
    o_spec = pl.BlockSpec((tm, tn), lambda i, j, k: (i, j))
    o_shape = (M, N)
    if chips and mode == "nn":
        per = cs // tn
        b_spec = pl.BlockSpec((None, tk, tn), lambda i, j, k: (j // per, k, j % per))
    elif chips and mode == "nt":
        per = cs // tk
        b_spec = pl.BlockSpec((None, tn, tk), lambda i, j, k: (k // per, j, k % per))
    elif chips:
        per = cs // tn
        o_spec = pl.BlockSpec((None, tm, tn), lambda i, j, k: (j // per, i, j % per))
        o_shape = (N_CHIPS, M, cs)
    two = a2 is not None

    def product(refs):
        part = _dot(refs[0][...].astype(BF16), refs[1][...].astype(BF16), dims)
        if two:
            part += _dot(refs[2][...].astype(BF16), refs[3][...].astype(BF16), dims)
        return part

    def body_whole_k(*refs):
        refs[-1][...] = product(refs).astype(refs[-1].dtype)

    def body(*refs):
        o_ref, acc_ref = refs[-2], refs[-1]
        k = pl.program_id(2)

        @pl.when(k == 0)
        def _():
            acc_ref[...] = product(refs)

        @pl.when(k > 0)
        def _():
            acc_ref[...] += product(refs)

        @pl.when(k == nk - 1)
        def _():
            o_ref[...] = acc_ref[...].astype(o_ref.dtype)

    ins = (a, b, a2, b2) if two else (a, b)
    return pl.pallas_call(
        body_whole_k if nk == 1 else body, name=name, grid=(M // tm, N // tn, nk),
        in_specs=[a_spec, b_spec] * (2 if two else 1),
        out_specs=o_spec,
        out_shape=jax.ShapeDtypeStruct(o_shape, out_dtype),
        scratch_shapes=[] if nk == 1 else [pltpu.VMEM((tm, tn), F32)],
        compiler_params=_params(("parallel", "parallel", "arbitrary")),
    )(*ins)


def _window(a):
    return (a[0], *a[1]) if isinstance(a, tuple) else (a, a.shape[1], 0)


def _rowwise(name, fn, rows, mats, outs, reds, ts):
    rows, widths, blocks = zip(*[_window(a) for a in rows])
    S = rows[0].shape[0]
    nr, nm, no = len(rows), len(mats), len(outs)

    def body(*refs):
        res = fn(*[r[...] for r in refs[:nr + nm]])
        res = res if isinstance(res, (tuple, list)) else (res,)
        for r, v in zip(refs[nr + nm:nr + nm + no], res[:no]):
            r[...] = v.astype(r.dtype)
        first = pl.program_id(0) == 0
        for r, v in zip(refs[nr + nm + no:], res[no:]):
            @pl.when(first)
            def _():
                r[...] = jnp.broadcast_to(v, r.shape)

            @pl.when(jnp.logical_not(first))
            def _():
                r[...] += jnp.broadcast_to(v, r.shape)

    in_specs = [pl.BlockSpec((ts, w), lambda i, b=b: (i, b)) for w, b in zip(widths, blocks)]
    in_specs += [pl.BlockSpec(m.shape, lambda i, nd=m.ndim: (0,) * nd) for m in mats]
    out_specs = [pl.BlockSpec((ts, w), lambda i: (i, 0)) for w, _ in outs]
    out_specs += [pl.BlockSpec(s, lambda i: (0, 0)) for s in reds]
    out_shape = [jax.ShapeDtypeStruct((S, w), dt) for w, dt in outs] + [jax.ShapeDtypeStruct(s, F32) for s in reds]
    return pl.pallas_call(
        body, name=name, grid=(S // ts,), in_specs=in_specs, out_specs=out_specs, out_shape=out_shape,
        compiler_params=_params(("arbitrary",) if reds else ("parallel",)),
    )(*rows, *mats)


def _shift_down(v, s):
    if s == 0:
        return v
    rows = lax.broadcasted_iota(I32, v.shape, 0)
    return jnp.where(rows >= s, pltpu.roll(v, s, 0), 0.0)


def _shift_up(v, s):
    if s == 0:
        return v
    n = v.shape[0]
    rows = lax.broadcasted_iota(I32, v.shape, 0)
    return jnp.where(rows < n - s, pltpu.roll(v, n - s, 0), 0.0)


def _conv(x, w, b):
    K = w.shape[0]
    y = jnp.broadcast_to(b, x.shape)
    for k in range(K):
        y = y + w[k:k + 1, :] * _shift_down(x, K - 1 - k)
    return y


def _conv_back(x, w, dc):
    K = w.shape[0]
    dx = jnp.zeros_like(x)
    dw = []
    for k in range(K):
        dx = dx + w[k:k + 1, :] * _shift_up(dc, K - 1 - k)
        dw.append(_colsum(dc * _shift_down(x, K - 1 - k)))
    return dx, jnp.concatenate(dw, axis=0), _colsum(dc)


def _colwise(name, fn, cols, vecs, outs, pouts, tc):
    cols, widths, blocks = zip(*[_window(a) for a in cols])
    S, C = cols[0].shape[0], widths[0]
    firsts = [b * (C // tc) for b in blocks]
    nc_, nv, no = len(cols), len(vecs), len(outs)

    def body(*refs):
        res = fn(*[r[...] for r in refs[:nc_ + nv]])
        res = res if isinstance(res, (tuple, list)) else (res,)
        for r, v in zip(refs[nc_ + nv:], res):
            r[...] = v.astype(r.dtype)

    in_specs = [pl.BlockSpec((S, tc), lambda j, f=f: (0, f + j)) for f in firsts]
    in_specs += [pl.BlockSpec((v.shape[0], tc), lambda j: (0, j)) for v in vecs]
    out_specs = [pl.BlockSpec((S, tc), lambda j: (0, j)) for _ in outs] + [pl.BlockSpec((k, tc), lambda j: (0, j)) for k in pouts]
    out_shape = [jax.ShapeDtypeStruct((S, C), dt) for dt in outs] + [jax.ShapeDtypeStruct((k, C), F32) for k in pouts]
    return pl.pallas_call(
        body, name=name, grid=(C // tc,), in_specs=in_specs, out_specs=out_specs, out_shape=out_shape,
        compiler_params=_params(("parallel",)),
    )(*cols, *vecs)


_G0, _G1 = math.sqrt(2.0 / math.pi), 0.044715


def _gelu(g):
    th = jnp.tanh(_G0 * (g + _G1 * g * g * g))
    return 0.5 * g * (1.0 + th), th


def _ffn_act(gate_pre, up, w, b):
    act, _ = _gelu(_conv(gate_pre, w, b))
    return act * up


def _ffn_act_back(dact, gate_pre, up, w, b):
    g = _conv(gate_pre, w, b)
    ge, th = _gelu(g)
    dge = 0.5 * (1.0 + th) + 0.5 * g * (1.0 - th * th) * _G0 * (1.0 + 3.0 * _G1 * g * g)
    dup = dact * ge
    dgate_pre, dw, db = _conv_back(gate_pre, w, dact * up * dge)
    return dgate_pre, dup, dw, db


def _ssm_act(xbc, w, b):
    c = _conv(xbc, w, b)
    return c * _sigmoid(c)


def _ssm_act_back(dxc, xbc, w, b):
    c = _conv(xbc, w, b)
    sg = _sigmoid(c)
    return _conv_back(xbc, w, dxc * sg * (1.0 + c * (1.0 - sg)))


def _rope_tables(S):
    inv = 1.0 / (ROPE_THETA ** (jnp.arange(0, ROPE, 2, dtype=F32) / ROPE))
    ang = jnp.arange(S, dtype=F32)[:, None] * inv[None, :]
    cos, sin = jnp.cos(ang), jnp.sin(ang)
    return jnp.tile(cos, (1, 4)), jnp.tile(jnp.concatenate([-sin, sin], axis=1), (1, 2))


def _swap_halves(x):
    lane = lax.broadcasted_iota(I32, x.shape, 1)
    w = x.shape[1]
    return jnp.where((lane % ROPE) < ROPE // 2, pltpu.roll(x, w - ROPE // 2, 1), pltpu.roll(x, ROPE // 2, 1))


def _rot(x, cos2, sin2):
    return x * cos2 + _swap_halves(x) * sin2


def _rot_back(dy, cos2, sin2):
    return dy * cos2 + _swap_halves(dy * sin2)


def _mla_pack(cfg, q, kv, kr, cos2, sin2):
    S, H = cfg.S, cfg.H
    ts = _pick(S, 256, 8)
    kr, _, kr_block = _window(kr)

    def body(q_ref, kv_ref, kr_ref, c_ref, s_ref, Q_ref, K_ref, V_ref):
        c2, s2 = c_ref[...], s_ref[...]
        krr = _rot(kr_ref[...], c2, s2)
        kr_half = (krr.astype(BF16), pltpu.roll(krr, ROPE, 1).astype(BF16))
        for j in range(H // 2):
            qr = _rot(q_ref[:, (H + j) * LANE:(H + j + 1) * LANE], c2, s2).astype(BF16)
            for h in (2 * j, 2 * j + 1):
                Q_ref[h, :, 0:LANE] = q_ref[:, h * LANE:(h + 1) * LANE].astype(BF16)
                Q_ref[h, :, LANE:] = qr
                K_ref[h, :, 0:LANE] = kv_ref[:, h * LANE:(h + 1) * LANE].astype(BF16)
                K_ref[h, :, LANE:] = kr_half[h % 2]
                V_ref[h] = kv_ref[:, (H + h) * LANE:(H + h + 1) * LANE].astype(BF16)

    tab = pl.BlockSpec((ts, LANE), lambda i: (i, 0))
    heads = lambda w: pl.BlockSpec((H, ts, w), lambda i: (0, i, 0))
    return pl.pallas_call(
        body, name="mla_pack", grid=(S // ts,),
        in_specs=[pl.BlockSpec((ts, cfg.QW), lambda i: (i, 0)), pl.BlockSpec((ts, cfg.KVW), lambda i: (i, 0)),
                  pl.BlockSpec((ts, LANE), lambda i: (i, kr_block)), tab, tab],
        out_specs=[heads(2 * LANE), heads(2 * LANE), heads(LANE)],
        out_shape=[jax.ShapeDtypeStruct((H, S, 2 * LANE), BF16), jax.ShapeDtypeStruct((H, S, 2 * LANE), BF16),
                   jax.ShapeDtypeStruct((H, S, LANE), BF16)],
        compiler_params=_params(("parallel",)),
    )(q, kv, kr, cos2, sin2)


def _mla_unpack(cfg, dQ, dK, dV, cos2, sin2):
    S, H = cfg.S, cfg.H
    ts = _pick(S, 256, 8)

    def body(dQ_ref, dK_ref, dV_ref, c_ref, s_ref, dq_ref, dkv_ref, dkr_ref):
        c2, s2 = c_ref[...], s_ref[...]
        lo = lax.broadcasted_iota(I32, (ts, LANE), 1) < ROPE
        tk = jnp.zeros((ts, LANE), F32)
        for h in range(H):
            dq_ref[:, h * LANE:(h + 1) * LANE] = dQ_ref[h, :, 0:LANE].astype(BF16)
            dkv_ref[:, h * LANE:(h + 1) * LANE] = dK_ref[h, :, 0:LANE].astype(BF16)
            dkv_ref[:, (H + h) * LANE:(H + h + 1) * LANE] = dV_ref[h].astype(BF16)
            own = lo if h % 2 == 0 else jnp.logical_not(lo)
            tk = tk + jnp.where(own, dK_ref[h, :, LANE:], 0.0)
        for j in range(H // 2):
            dr = dQ_ref[2 * j, :, LANE:] + dQ_ref[2 * j + 1, :, LANE:]
            dq_ref[:, (H + j) * LANE:(H + j + 1) * LANE] = _rot_back(dr, c2, s2).astype(BF16)
        dkr_rot = jnp.where(lo, tk + pltpu.roll(tk, ROPE, 1), 0.0)
        dkr_ref[...] = _rot_back(dkr_rot, c2, s2).astype(BF16)

    tab = pl.BlockSpec((ts, LANE), lambda i: (i, 0))
    return pl.pallas_call(
        body, name="mla_unpack", grid=(S // ts,),
        in_specs=[pl.BlockSpec((H, ts, 2 * LANE), lambda i: (0, i, 0)), pl.BlockSpec((H, ts, 2 * LANE), lambda i: (0, i, 0)),
                  pl.BlockSpec((H, ts, LANE), lambda i: (0, i, 0)), tab, tab],
        out_specs=[pl.BlockSpec((ts, cfg.QW), lambda i: (i, 0)), pl.BlockSpec((ts, cfg.KVW), lambda i: (i, 0)), tab],
        out_shape=[jax.ShapeDtypeStruct((S, cfg.QW), BF16), jax.ShapeDtypeStruct((S, cfg.KVW), BF16),
                   jax.ShapeDtypeStruct((S, LANE), BF16)],
        compiler_params=_params(("parallel",)),
    )(dQ, dK, dV, cos2, sin2)


_ATT_T = 256
_ATT_HB = 8
_ATT_SCALE = (NOPE + ROPE) ** -0.5


def _diag_mask(transposed=False):
    r = lax.broadcasted_iota(I32, (_ATT_T, _ATT_T), 0) // CHUNK
    c = lax.broadcasted_iota(I32, (_ATT_T, _ATT_T), 1) // CHUNK
    return r <= c if transposed else c <= r


def _row_form(col):
    return jnp.broadcast_to(col, (col.shape[0], LANE)).T[0:8, :]


def _attn_fwd(cfg, Q, K, V):
    S, H, T, HB = cfg.S, cfg.H, _ATT_T, min(cfg.H, _ATT_HB)

    def body(q_ref, k_ref, v_ref, o_ref, lse_ref, lse_t_ref):
        qi = pl.program_id(1)

        def head_step(b, kb, carry, mask):
            m, l, acc = carry
            ks = pl.multiple_of(kb * T, T)
            s = _dot(q_ref[b], k_ref[b, pl.ds(ks, T), :], NT) * _ATT_SCALE
            if mask is not None:
                s = jnp.where(mask, s, -1e30)
            m_new = jnp.maximum(m, jnp.max(s, axis=1, keepdims=True))
            p = jnp.exp(s - m_new)
            alpha = jnp.exp(m - m_new)
            l = alpha * l + jnp.sum(p, axis=1, keepdims=True)
            acc = alpha * acc + _dot(p.astype(BF16), v_ref[b, pl.ds(ks, T), :])
            return m_new, l, acc

        def step(kb, carry, mask=None):
            return tuple(head_step(b, kb, carry[b], mask) for b in range(HB))

        init = (jnp.full((T, 1), -1e30, F32), jnp.zeros((T, 1), F32), jnp.zeros((T, VH), F32))
        done = step(qi, lax.fori_loop(0, qi, step, (init,) * HB), _diag_mask())
        for b, (m, l, acc) in enumerate(done):
            o_ref[:, b * LANE:(b + 1) * LANE] = acc / l
            lse = m + jnp.log(l)
            lse_ref[:, b * LANE:(b + 1) * LANE] = jnp.broadcast_to(lse, (T, LANE))
            lse_t_ref[b] = _row_form(lse)

    return pl.pallas_call(
        body, name="attn_fwd", grid=(H // HB, S // T),
        in_specs=[pl.BlockSpec((HB, T, 2 * LANE), lambda h, i: (h, i, 0)), pl.BlockSpec((HB, S, 2 * LANE), lambda h, i: (h, 0, 0)),
                  pl.BlockSpec((HB, S, LANE), lambda h, i: (h, 0, 0))],
        out_specs=[pl.BlockSpec((T, HB * LANE), lambda h, i: (i, h)), pl.BlockSpec((T, HB * LANE), lambda h, i: (i, h)),
                   pl.BlockSpec((HB, 8, T), lambda h, i: (h, 0, i))],
        out_shape=[jax.ShapeDtypeStruct((S, H * LANE), F32), jax.ShapeDtypeStruct((S, H * LANE), F32),
                   jax.ShapeDtypeStruct((H, 8, S), F32)],
        compiler_params=_params(("parallel", "parallel")),
    )(Q, K, V)


def _attn_dq(cfg, Q, K, V, do, o, lse, after):
    S, H, T, HB = cfg.S, cfg.H, _ATT_T, min(cfg.H, _ATT_HB)

    def body(q_ref, k_ref, v_ref, do_ref, o_ref, lse_ref, after_ref, dq_ref, dl_t_ref):
        qi = pl.program_id(1)
        do = [do_ref[:, b * LANE:(b + 1) * LANE] for b in range(HB)]
        delta = [jnp.sum(do[b] * o_ref[:, b * LANE:(b + 1) * LANE], axis=1, keepdims=True) for b in range(HB)]
        dob = [d.astype(BF16) for d in do]

        def head_step(b, kb, dq, mask):
            ks = pl.multiple_of(kb * T, T)
            k = k_ref[b, pl.ds(ks, T), :]
            s = _dot(q_ref[b], k, NT) * _ATT_SCALE
            if mask is not None:
                s = jnp.where(mask, s, -1e30)
            p = jnp.exp(s - lse_ref[:, b * LANE:b * LANE + 1])
            dp = _dot(dob[b], v_ref[b, pl.ds(ks, T), :], NT)
            ds = p * (dp - delta[b]) * _ATT_SCALE
            return dq + _dot(ds.astype(BF16), k)

        def step(kb, dqs, mask=None):
            return tuple(head_step(b, kb, dqs[b], mask) for b in range(HB))

        dqs = step(qi, lax.fori_loop(0, qi, step, (jnp.zeros((T, 2 * LANE), F32),) * HB), _diag_mask())
        for b in range(HB):
            dq_ref[b] = dqs[b]
            dl_t_ref[b] = _row_form(delta[b])

    col = pl.BlockSpec((T, HB * LANE), lambda h, i: (i, h))
    return pl.pallas_call(
        body, name="attn_dq", grid=(H // HB, S // T),
        in_specs=[pl.BlockSpec((HB, T, 2 * LANE), lambda h, i: (h, i, 0)), pl.BlockSpec((HB, S, 2 * LANE), lambda h, i: (h, 0, 0)),
                  pl.BlockSpec((HB, S, LANE), lambda h, i: (h, 0, 0)), col, col, col, _ANY],
        out_specs=[pl.BlockSpec((HB, T, 2 * LANE), lambda h, i: (h, i, 0)), pl.BlockSpec((HB, 8, T), lambda h, i: (h, 0, i))],
        out_shape=[jax.ShapeDtypeStruct((H, S, 2 * LANE), F32), jax.ShapeDtypeStruct((H, 8, S), F32)],
        compiler_params=_params(("parallel", "parallel")),
    )(Q, K, V, do, o, lse, after)


def _attn_dkv(cfg, Q, K, V, do, lse_t, delta_t):
    S, H, T, HB = cfg.S, cfg.H, _ATT_T, min(cfg.H, _ATT_HB)
    nq = S // T

    def body(q_ref, k_ref, v_ref, do_ref, lse_ref, dl_ref, dk_ref, dv_ref):
        kb = pl.program_id(1)

        def head_step(b, qi, carry, mask):
            dk, dv = carry
            qs = pl.multiple_of(qi * T, T)
            q = q_ref[b, pl.ds(qs, T), :]
            dob = do_ref[pl.ds(qs, T), b * LANE:(b + 1) * LANE].astype(BF16)
            s = _dot(k_ref[b], q, NT) * _ATT_SCALE
            if mask is not None:
                s = jnp.where(mask, s, -1e30)
            p = jnp.exp(s - lse_ref[b, 0:1, pl.ds(qs, T)])
            dv = dv + _dot(p.astype(BF16), dob)
            dp = _dot(v_ref[b], dob, NT)
            ds = p * (dp - dl_ref[b, 0:1, pl.ds(qs, T)]) * _ATT_SCALE
            dk = dk + _dot(ds.astype(BF16), q)
            return dk, dv

        def step(qi, carry, mask=None):
            return tuple(head_step(b, qi, carry[b], mask) for b in range(HB))

        zero = (jnp.zeros((T, 2 * LANE), F32), jnp.zeros((T, VH), F32))
        done = lax.fori_loop(kb + 1, nq, step, step(kb, (zero,) * HB, _diag_mask(transposed=True)))
        for b, (dk, dv) in enumerate(done):
            dk_ref[b] = dk
            dv_ref[b] = dv

    row = pl.BlockSpec((HB, 8, S), lambda h, j: (h, 0, 0))
    return pl.pallas_call(
        body, name="attn_dkv", grid=(H // HB, S // T),
        in_specs=[pl.BlockSpec((HB, S, 2 * LANE), lambda h, j: (h, 0, 0)), pl.BlockSpec((HB, T, 2 * LANE), lambda h, j: (h, j, 0)),
                  pl.BlockSpec((HB, T, LANE), lambda h, j: (h, j, 0)), pl.BlockSpec((S, HB * LANE), lambda h, j: (0, h)), row, row],
        out_specs=[pl.BlockSpec((HB, T, 2 * LANE), lambda h, j: (h, j, 0)), pl.BlockSpec((HB, T, LANE), lambda h, j: (h, j, 0))],
        out_shape=[jax.ShapeDtypeStruct((H, S, 2 * LANE), F32), jax.ShapeDtypeStruct((H, S, LANE), F32)],
        compiler_params=_params(("parallel", "parallel")),
    )(Q, K, V, do, lse_t, delta_t)


def _expand_matrix(cfg):
    r = lax.broadcasted_iota(I32, (LANE, cfg.INNER), 0)
    c = lax.broadcasted_iota(I32, (LANE, cfg.INNER), 1)
    return (r == c // HP).astype(F32)


def _softplus(x):
    return jnp.maximum(x, 0.0) + jnp.log(1.0 + jnp.exp(-jnp.abs(x)))


def _ssd_prep(cfg, dt_raw, dt_bias_pad, a_log_pad, expand):
    HS = cfg.HS

    def fn(raw, bias, alog, E):
        heads = lax.broadcasted_iota(I32, raw.shape, 1) < HS
        dt = jnp.where(heads, _softplus(raw + bias), 0.0)
        a = dt * jnp.where(heads[0:1], -jnp.exp(alog), 0.0)
        return dt, a, _dot(dt, E, precision=HI), _dot(a, E, precision=HI)

    return _rowwise("ssd_prep", fn, [dt_raw], [dt_bias_pad, a_log_pad, expand],
                    [(LANE, F32), (LANE, F32), (cfg.INNER, F32), (cfg.INNER, F32)], [], _pick(cfg.S, 512, 8))


def _tril(T):
    return lax.broadcasted_iota(I32, (T, T), 0) >= lax.broadcasted_iota(I32, (T, T), 1)


def _ssd_fwd(cfg, xc, dt_exp, a_exp, a_small, dskip_exp):
    S, T, INNER, G, NPAIR = cfg.S, cfg.T, cfg.INNER, cfg.G, cfg.NPAIR
    NC = S // T

    def body(xc_ref, dte_ref, ae_ref, as_ref, dsk_ref, y_ref, hin_ref, ht_ref):
        @pl.when(pl.program_id(0) == 0)
        def _():
            ht_ref[...] = jnp.zeros_like(ht_ref)

        tril = _tril(T)
        tri = tril.astype(F32)
        acs_s = _dot(tri, as_ref[...], precision=HI)
        acs_e = _dot(tri, ae_ref[...], precision=HI)
        acs_t = acs_s.T
        lo = lax.broadcasted_iota(I32, (T, LANE), 1) < HP
        for g in range(G):
            Bb = xc_ref[:, INNER + g * NST:INNER + (g + 1) * NST].astype(BF16)
            Cb = xc_ref[:, INNER + (G + g) * NST:INNER + (G + g + 1) * NST].astype(BF16)
            Gm = _dot(Cb, Bb, NT)
            for j in range(g * NPAIR // G, (g + 1) * NPAIR // G):
                sl = slice(j * LANE, (j + 1) * LANE)
                Xp = xc_ref[:, sl]
                Xdt = Xp * dte_ref[:, sl]
                Xb = Xdt.astype(BF16)
                acs_p = acs_e[:, sl]
                last = acs_p[T - 1:T, :]
                Hin = ht_ref[j]
                hin_ref[0, j] = Hin
                yd = []
                for e in (0, 1):
                    h = 2 * j + e
                    Lm = jnp.exp(jnp.where(tril, acs_s[:, h:h + 1] - acs_t[h:h + 1, :], -1e30))
                    yd.append(_dot((Gm * Lm).astype(BF16), Xb))
                y_off = _dot(Cb, Hin.astype(BF16)) * jnp.exp(acs_p)
                y_ref[:, sl] = jnp.where(lo, yd[0], yd[1]) + y_off + Xp * dsk_ref[:, sl]
                st = _dot(Bb, (Xdt * jnp.exp(last - acs_p)).astype(BF16), TN)
                ht_ref[j] = jnp.exp(last) * Hin + st

    rows = lambda w: pl.BlockSpec((T, w), lambda c: (c, 0))
    return pl.pallas_call(
        body, name="ssd_fwd", grid=(NC,),
        in_specs=[rows(cfg.CONVCH), rows(INNER), rows(INNER), rows(LANE), pl.BlockSpec((1, INNER), lambda c: (0, 0))],
        out_specs=[rows(INNER), pl.BlockSpec((1, NPAIR, NST, LANE), lambda c: (c, 0, 0, 0))],
        out_shape=[jax.ShapeDtypeStruct((S, INNER), F32), jax.ShapeDtypeStruct((NC, NPAIR, NST, LANE), F32)],
        scratch_shapes=[pltpu.VMEM((NPAIR, NST, LANE), F32)],
        compiler_params=_params(("arbitrary",)),
    )(xc, dt_exp, a_exp, a_small, dskip_exp)


def _ssd_bwd(cfg, dy, xc, dt_exp, a_exp, a_small, dskip_exp, hin, dt_raw, dt_bias_pad, a_log_pad, expand):
    S, T, INNER, G, NPAIR, HS = cfg.S, cfg.T, cfg.INNER, cfg.G, cfg.NPAIR, cfg.HS
    NC = S // T

    def body(dy_ref, xc_ref, dte_ref, ae_ref, as_ref, dsk_ref, hin_ref, raw_ref, bias_ref, alog_ref, e_ref,
             dxc_ref, draw_ref, dbias_ref, dalog_ref, dskip_ref, dht_ref, cols_ref, rows_ref, dacs_ref, ddt_ref):
        first = pl.program_id(0) == 0

        @pl.when(first)
        def _():
            dht_ref[...] = jnp.zeros_like(dht_ref)

        tril = _tril(T)
        tri = tril.astype(F32)
        a_s = as_ref[...]
        acs_s = _dot(tri, a_s, precision=HI)
        acs_e = _dot(tri, ae_ref[...], precision=HI)
        acs_t = acs_s.T
        lo = lax.broadcasted_iota(I32, (T, LANE), 1) < HP
        last_row = lax.broadcasted_iota(I32, (T, LANE), 0) == T - 1
        cols_ref[...] = jnp.zeros_like(cols_ref)
        rows_ref[...] = jnp.zeros_like(rows_ref)
        dsk_parts = []
        for g in range(G):
            bsl = slice(INNER + g * NST, INNER + (g + 1) * NST)
            csl = slice(INNER + (G + g) * NST, INNER + (G + g + 1) * NST)
            Bb = xc_ref[:, bsl].astype(BF16)
            Cb = xc_ref[:, csl].astype(BF16)
            Gm = _dot(Cb, Bb, NT)
            dG = jnp.zeros((T, T), F32)
            dB = jnp.zeros((T, NST), F32)
            dC = jnp.zeros((T, NST), F32)
            for j in range(g * NPAIR // G, (g + 1) * NPAIR // G):
                sl = slice(j * LANE, (j + 1) * LANE)
                Xp = xc_ref[:, sl]
                dtp = dte_ref[:, sl]
                Xdt = Xp * dtp
                Xb = Xdt.astype(BF16)
                acs_p = acs_e[:, sl]
                last = acs_p[T - 1:T, :]
                e_p, dec, cd = jnp.exp(acs_p), jnp.exp(last - acs_p), jnp.exp(last)
                Hin = hin_ref[0, j]
                Hb = Hin.astype(BF16)
                dHn = dht_ref[j]
                dHb = dHn.astype(BF16)
                dYp = dy_ref[:, sl]
                z = _dot(Cb, Hb)
                dz = (dYp * e_p).astype(BF16)
                dacs_p = dYp * z * e_p
                dC = dC + _dot(dz, Hb, NT)
                dHin = _dot(Cb, dz, TN) + cd * dHn
                dlast = _colsum(dHn * Hin) * cd
                qv = _dot(Bb, dHb)
                dXdt = qv * dec
                ddec = qv * Xdt * dec
                dacs_p = dacs_p - ddec
                dlast = dlast + _colsum(ddec)
                dB = dB + _dot((Xdt * dec).astype(BF16), dHb, NT)
                for e in (0, 1):
                    h = 2 * j + e
                    Lm = jnp.exp(jnp.where(tril, acs_s[:, h:h + 1] - acs_t[h:h + 1, :], -1e30))
                    Mh = Gm * Lm
                    dYe = jnp.where(lo if e == 0 else jnp.logical_not(lo), dYp, 0.0).astype(BF16)
                    dM = _dot(dYe, Xb, NT)
                    dXdt = dXdt + _dot(Mh.astype(BF16), dYe, TN)
                    W = dM * Mh
                    cols_ref[:, h:h + 1] = jnp.sum(W, axis=1, keepdims=True)
                    rows_ref[h:h + 1, :] = _colsum(W)
                    dG = dG + dM * Lm
                dacs_ref[:, sl] = dacs_p + jnp.where(last_row, dlast, 0.0)
                ddt_ref[:, sl] = dXdt * Xp
                dxc_ref[:, sl] = dXdt * dtp + dYp * dsk_ref[:, sl]
                dsk_parts.append(_colsum(dYp * Xp))
                dht_ref[j] = dHin
            dGb = dG.astype(BF16)
            dxc_ref[:, bsl] = dB + _dot(dGb, Cb, TN)
            dxc_ref[:, csl] = dC + _dot(dGb, Bb)
        E = e_ref[...]
        dacs_s = cols_ref[...] - rows_ref[...].T + _dot(dacs_ref[...], E, NT, precision=HI)
        da = _dot(tri, dacs_s, TN, precision=HI)
        heads = lax.broadcasted_iota(I32, (1, LANE), 1) < HS
        A = jnp.where(heads, -jnp.exp(alog_ref[...]), 0.0)
        ddt = _dot(ddt_ref[...], E, NT, precision=HI) + da * A
        draw = jnp.where(heads, ddt * _sigmoid(raw_ref[...] + bias_ref[...]), 0.0)
        draw_ref[...] = draw
        dsk = _dot(jnp.broadcast_to(jnp.concatenate(dsk_parts, axis=1), (8, INNER)), E, NT, precision=HI)[0:1]
        for ref, val in ((dbias_ref, _colsum(draw)), (dalog_ref, _colsum(da * a_s)), (dskip_ref, dsk)):
            @pl.when(first)
            def _():
                ref[...] = val

            @pl.when(jnp.logical_not(first))
            def _():
                ref[...] += val

    dt_raw, _, raw_block = _window(dt_raw)
    rows = lambda w, b=0: pl.BlockSpec((T, w), lambda c: (NC - 1 - c, b))
    vec = lambda w: pl.BlockSpec((1, w), lambda c: (0, 0))
    return pl.pallas_call(
        body, name="ssd_bwd", grid=(NC,),
        in_specs=[rows(INNER), rows(cfg.CONVCH), rows(INNER), rows(INNER), rows(LANE), vec(INNER),
                  pl.BlockSpec((1, NPAIR, NST, LANE), lambda c: (NC - 1 - c, 0, 0, 0)), rows(LANE, raw_block), vec(LANE), vec(LANE),
                  pl.BlockSpec((LANE, INNER), lambda c: (0, 0))],
        out_specs=[rows(cfg.CONVCH), rows(LANE), vec(LANE), vec(LANE), vec(LANE)],
        out_shape=[jax.ShapeDtypeStruct((S, cfg.CONVCH), F32), jax.ShapeDtypeStruct((S, LANE), F32)]
        + [jax.ShapeDtypeStruct((1, LANE), F32)] * 3,
        scratch_shapes=[pltpu.VMEM((NPAIR, NST, LANE), F32), pltpu.VMEM((T, LANE), F32), pltpu.VMEM((LANE, T), F32),
                        pltpu.VMEM((T, INNER), F32), pltpu.VMEM((T, INNER), F32)],
        compiler_params=_params(("arbitrary",)),
    )(dy, xc, dt_exp, a_exp, a_small, dskip_exp, hin, dt_raw, dt_bias_pad, a_log_pad, expand)


def _ssd_post(cfg, y, z, norm_g):
    W = cfg.INNER // cfg.G

    def fn(y, z, g):
        yz = y * z * _sigmoid(z)
        return jnp.concatenate([yz[:, i * W:(i + 1) * W] * _rs(yz[:, i * W:(i + 1) * W]) for i in range(cfg.G)], axis=1) * g

    return _rowwise("ssd_post", fn, [y, z], [norm_g], [(cfg.INNER, BF16)], [], _pick(cfg.S, 256, 8))[0]


def _ssd_post_bwd(cfg, db, y, z, norm_g):
    W = cfg.INNER // cfg.G

    def fn(db, y, z, g):
        sg = _sigmoid(z)
        yz = y * z * sg
        dn = db * g
        dyz, nh = [], []
        for i in range(cfg.G):
            seg = yz[:, i * W:(i + 1) * W]
            r = _rs(seg)
            nh.append(seg * r)
            dyz.append(_rms_back(nh[-1], r, dn[:, i * W:(i + 1) * W]))
        dyz = jnp.concatenate(dyz, axis=1)
        return dyz * z * sg, dyz * y * sg * (1.0 + z * (1.0 - sg)), _colsum(db * jnp.concatenate(nh, axis=1))

    return _rowwise("ssd_post_bwd", fn, [db, y, z], [norm_g], [(cfg.INNER, F32), (cfg.INNER, F32)], [(1, cfg.INNER)],
                    _pick(cfg.S, 256, 8))


def _local_grads(cfg, x, tgt, W, sp, mla_weights=None, out_weight=None, ffn_weights=None, ffn_grads_ready=None,
                 early_grads_ready=None, in_grad_ready=None):
    S, D, H, INNER = cfg.S, cfg.D, cfg.H, cfg.INNER
    ts = _pick(S, 256, 8)
    tc = 256

    xn = _rowwise("rms_pre", lambda x, g: x * _rs(x) * g, [x], [sp["mix_pre_g"]], [(D, BF16)], [], ts)[0]
    u = _matmul("mm_in", xn, W["w_in"], "nt", F32)
    c_q, c_kv, kr, z, xbc, dt_raw = [(u, cfg.window(n)) for n in ("c_q", "c_kv", "kr", "z", "xbc", "dt")]

    if mla_weights is not None:
        sp = dict(sp, q_norm_g=sp["q_norm_g"] + mla_weights.pass_on(u)[0, 0])
    cqn = _rowwise("rms_q", lambda x, g: x * _rs(x) * g, [c_q], [sp["q_norm_g"]], [(cfg.QL, BF16)], [], ts)[0]
    ckvn = _rowwise("rms_kv", lambda x, g: x * _rs(x) * g, [c_kv], [sp["kv_norm_g"]], [(cfg.KVL, BF16)], [], ts)[0]
    if mla_weights is not None:
        W = dict(W, **mla_weights.arrived(ckvn))
    q = _matmul("mm_uq", cqn, W["w_uq"], "nn", F32)
    kv = _matmul("mm_ukv", ckvn, W["w_ukv"], "nn", F32)
    cos2, sin2 = _rope_tables(S)
    Qh, Kh, Vh = _mla_pack(cfg, q, kv, kr, cos2, sin2)
    a_out, lse, lse_t = _attn_fwd(cfg, Qh, Kh, Vh)
    if out_weight is not None:
        sp = dict(sp, ssm_conv_b=sp["ssm_conv_b"] + out_weight.pass_on(a_out)[0, 0])

    pad = lambda v: jnp.pad(v, ((0, 0), (0, LANE - v.shape[1])))
    expand = _expand_matrix(cfg)
    dt_bias_pad, a_log_pad = pad(sp["dt_bias"]), pad(sp["a_log"])
    dskip_exp = jnp.repeat(sp["d_skip"], HP, axis=1)
    xc = _colwise("ssm_act", _ssm_act, [xbc], [sp["ssm_conv_w"], sp["ssm_conv_b"]], [F32], [], tc)[0]
    dt_s, a_s, dt_exp, a_exp = _ssd_prep(cfg, dt_raw, dt_bias_pad, a_log_pad, expand)
    y_ssd, hin = _ssd_fwd(cfg, xc, dt_exp, a_exp, a_s, dskip_exp)
    b_out = _ssd_post(cfg, y_ssd, z, sp["ssm_norm_g"])

    ab_out = jnp.concatenate([a_out.astype(BF16), b_out], axis=1)
    if out_weight is not None:
        W = dict(W, **out_weight.arrived(ab_out))
    if ffn_weights is not None:
        sp = dict(sp, mix_post_g=sp["mix_post_g"] + ffn_weights.pass_on(ab_out)[0, 0])
    mix = _matmul("mm_out", ab_out, W["w_out"], "nn", F32)

    def mid(x, mix, g_mp, g_fp):
        x1 = x + mix * _rs(mix) * g_mp
        return x1, x1 * _rs(x1) * g_fp

    x1, h2 = _rowwise("fwd_mid", mid, [x, mix], [sp["mix_post_g"], sp["ffn_pre_g"]], [(D, F32), (D, BF16)], [], ts)
    if ffn_weights is not None:
        W = dict(W, **ffn_weights.arrived(h2))
    gate_pre = _matmul("mm_gate", h2, W["w_gate"], "nn", F32, chips=True)
    up = _matmul("mm_up", h2, W["w_up"], "nn", F32, chips=True)
    act = _colwise("ffn_act", _ffn_act, [gate_pre, up], [sp["ffn_conv_w"], sp["ffn_conv_b"]], [BF16], [], tc)[0]
    f = _matmul("mm_down", act, W["w_down"], "nn", F32)

    def final(x1, f, t, g):
        r = _rs(f)
        fh = f * r
        err = x1 + fh * g - t
        loss = 0.5 * jnp.sum(jnp.mean(err * err, axis=-1, keepdims=True), axis=0, keepdims=True)
        dy = err * (1.0 / D)
        return dy, _rms_back(fh, r, dy * g), _colsum(dy * fh), loss

    dy, df, g_ffn_post, loss = _rowwise("final", final, [x1, f, tgt], [sp["ffn_post_g"]], [(D, F32), (D, BF16)],
                                        [(1, D), (1, LANE)], ts)
    gW = {}
    dact = _matmul("mm_down_dx", df, W["w_down"], "nt", F32)
    gW["w_down"] = _matmul("mm_down_dw", act, df, "tn", BF16)
    dgate, dup, g_ffn_conv_w, g_ffn_conv_b = _colwise(
        "ffn_act_bwd", _ffn_act_back, [dact, gate_pre, up], [sp["ffn_conv_w"], sp["ffn_conv_b"]], [BF16, BF16], [FFN_K, 1], tc)
    gW["w_gate"] = _matmul("mm_gate_dw", h2, dgate, "tn", BF16, chips=True)
    gW["w_up"] = _matmul("mm_up_dw", h2, dup, "tn", BF16, chips=True)
    if ffn_grads_ready is not None:
        sp = dict(sp, ffn_pre_g=sp["ffn_pre_g"] + ffn_grads_ready({n: gW[n] for n in ("w_down", "w_gate", "w_up")})[0, 0])
    dh2 = _matmul("mm_gu_dx", dgate, W["w_gate"], "nt", F32, dup, W["w_up"], chips=True)

    def mid_back(dy, dh2, x1, mix, g_mp, g_fp):
        r2 = _rs(x1)
        xh = x1 * r2
        dx1 = dy + _rms_back(xh, r2, dh2 * g_fp)
        r1 = _rs(mix)
        mh = mix * r1
        return dx1, _rms_back(mh, r1, dx1 * g_mp), _colsum(dh2 * xh), _colsum(dx1 * mh)

    dx1, dmix, g_ffn_pre, g_mix_post = _rowwise("bwd_mid", mid_back, [dy, dh2, x1, mix], [sp["mix_post_g"], sp["ffn_pre_g"]],
                                                [(D, F32), (D, BF16)], [(1, D), (1, D)], ts)
    dab_out = _matmul("mm_out_dx", dmix, W["w_out"], "nt", F32)
    db_out = (dab_out, (INNER, cfg.MLAW // INNER))
    gW["w_out"] = _matmul("mm_out_dw", ab_out, dmix, "tn", BF16)
    early_token = jnp.zeros((8, LANE), F32)
    if early_grads_ready is not None:
        early_token = early_grads_ready({n: gW[n] for n in ("w_down", "w_gate", "w_up", "w_out")})
        sp = dict(sp, ssm_norm_g=sp["ssm_norm_g"] + early_token[0, 0])

    dy_ssd, dz, g_ssm_norm = _ssd_post_bwd(cfg, db_out, y_ssd, z, sp["ssm_norm_g"])
    dxc, ddt_raw, g_dt_bias, g_a_log, g_d_skip = _ssd_bwd(cfg, dy_ssd, xc, dt_exp, a_exp, a_s, dskip_exp, hin, dt_raw,
                                                          dt_bias_pad, a_log_pad, expand)
    dxbc, g_ssm_conv_w, g_ssm_conv_b = _colwise("ssm_act_bwd", _ssm_act_back, [dxc, xbc], [sp["ssm_conv_w"], sp["ssm_conv_b"]],
                                                [BF16], [SSM_K, 1], tc)

    dQ, delta_t = _attn_dq(cfg, Qh, Kh, Vh, dab_out, a_out, lse, early_token)
    dK, dV = _attn_dkv(cfg, Qh, Kh, Vh, dab_out, lse_t, delta_t)
    dq, dkv, dkr = _mla_unpack(cfg, dQ, dK, dV, cos2, sin2)
    dcqn = _matmul("mm_uq_dx", dq, W["w_uq"], "nt", F32)
    dckvn = _matmul("mm_ukv_dx", dkv, W["w_ukv"], "nt", F32)
    gW["w_uq"] = _matmul("mm_uq_dw", cqn, dq, "tn", BF16)
    gW["w_ukv"] = _matmul("mm_ukv_dw", ckvn, dkv, "tn", BF16)

    def rms_back(x, dy, g):
        r = _rs(x)
        xh = x * r
        return _rms_back(xh, r, dy * g), _colsum(dy * xh)

    dc_q, g_q_norm = _rowwise("rms_q_bwd", rms_back, [c_q, dcqn], [sp["q_norm_g"]], [(cfg.QL, BF16)], [(1, cfg.QL)], ts)
    dc_kv, g_kv_norm = _rowwise("rms_kv_bwd", rms_back, [c_kv, dckvn], [sp["kv_norm_g"]], [(cfg.KVL, BF16)], [(1, cfg.KVL)], ts)

    du = dict(c_q=dc_q, c_kv=dc_kv, kr=dkr, z=dz.astype(BF16), xbc=dxbc, dt=ddt_raw.astype(BF16))
    du = jnp.concatenate([du[n] for n in sorted(du, key=lambda n: cfg.seg[n][0])], axis=1)
    assert du.shape[1] == cfg.EXT, "the layout of u has gaps"
    gW["w_in"] = _matmul("mm_in_dw", du, xn, "tn", BF16)
    if in_grad_ready is not None:
        token = in_grad_ready({n: gW[n] for n in ("w_in", "w_uq", "w_ukv")})
        sp = dict(sp, mix_pre_g=sp["mix_pre_g"] + token[0, 0])
    dxn = _matmul("mm_in_dx", du, W["w_in"], "nn", F32)

    def first_back(dx1, dxn, x, g):
        r = _rs(x)
        xh = x * r
        return dx1 + _rms_back(xh, r, dxn * g), _colsum(dxn * xh)

    grad_x, g_mix_pre = _rowwise("bwd_first", first_back, [dx1, dxn, x], [sp["mix_pre_g"]], [(D, F32)], [(1, D)], ts)

    gs = dict(mix_pre_g=g_mix_pre, q_norm_g=g_q_norm, kv_norm_g=g_kv_norm, ssm_conv_w=g_ssm_conv_w, ssm_conv_b=g_ssm_conv_b,
              dt_bias=g_dt_bias[:, :cfg.HS], a_log=g_a_log[:, :cfg.HS], d_skip=g_d_skip[:, :cfg.HS], ssm_norm_g=g_ssm_norm,
              mix_post_g=g_mix_post, ffn_pre_g=g_ffn_pre, ffn_conv_w=g_ffn_conv_w, ffn_conv_b=g_ffn_conv_b,
              ffn_post_g=g_ffn_post)
    return loss, grad_x, gW, gs


def _to_kernel_layout(cfg, name, w):
    if name == "w_in":
        parts, at = [], 0
        for off, width, n_off, n_width in sorted(cfg.seg.values()):
            parts += [jnp.zeros((off - at, w.shape[1]), w.dtype), w[n_off:n_off + n_width],
                      jnp.zeros((width - n_width, w.shape[1]), w.dtype)]
            at = off + width
        parts.append(jnp.zeros((cfg.EXT - at, w.shape[1]), w.dtype))
        return jnp.concatenate([p for p in parts if p.shape[0]], axis=0)
    if name in ("w_uq", "w_ukv"):
        per = NOPE + (ROPE if name == "w_uq" else VH)
        return jnp.concatenate([w[:, h * per:h * per + NOPE] for h in range(cfg.H)]
                               + [w[:, h * per + NOPE:(h + 1) * per] for h in range(cfg.H)], axis=1)
    return w


def _from_kernel_layout(cfg, name, g):
    if name == "w_in":
        return jnp.concatenate([g[off:off + n_width] for off, _, _, n_width in sorted(cfg.seg.values(), key=lambda s: s[2])], axis=0)
    if name in ("w_uq", "w_ukv"):
        second = ROPE if name == "w_uq" else VH
        base = cfg.H * NOPE
        parts = []
        for h in range(cfg.H):
            parts += [g[:, h * NOPE:(h + 1) * NOPE], g[:, base + h * second:base + (h + 1) * second]]
        return jnp.concatenate(parts, axis=1)
    return g


def _cols_to_chips(w):
    r, c = w.shape
    return w.reshape(r, N_CHIPS, c // N_CHIPS).transpose(1, 0, 2)


def _chips_to_cols(g):
    k, r, cs = g.shape
    return g.transpose(1, 0, 2).reshape(r, k * cs)


_CHIP_MAJOR = ("w_gate", "w_up")
_RELAYOUT = ("w_uq", "w_ukv")
_LAYOUT_ROWS = 256


def _w_in_layout(cfg, wg):
    _, rs, d = wg.shape
    tc = _pick(d, _LAYOUT_ROWS, LANE)

    def body(w_ref, o_ref):
        o_ref[...] = _to_kernel_layout(cfg, "w_in", jnp.concatenate([w_ref[k] for k in range(N_CHIPS)], axis=0))

    return pl.pallas_call(
        body, name="layout_w_in", grid=(d // tc,),
        in_specs=[pl.BlockSpec((N_CHIPS, rs, tc), lambda j: (0, 0, j))], out_specs=pl.BlockSpec((cfg.EXT, tc), lambda j: (0, j)),
        out_shape=jax.ShapeDtypeStruct((cfg.EXT, d), wg.dtype), compiler_params=_params(("parallel",)),
    )(wg)


def _w_in_grad_to_chips(cfg, g):
    _, d = g.shape
    rs = cfg.IN_COLS // N_CHIPS
    tc = _pick(d, _LAYOUT_ROWS, LANE)

    def body(g_ref, o_ref):
        nat = _from_kernel_layout(cfg, "w_in", g_ref[...])
        for k in range(N_CHIPS):
            o_ref[k] = nat[k * rs:(k + 1) * rs]

    return pl.pallas_call(
        body, name="layout_grad_w_in", grid=(d // tc,),
        in_specs=[pl.BlockSpec((cfg.EXT, tc), lambda j: (0, j))], out_specs=pl.BlockSpec((N_CHIPS, rs, tc), lambda j: (0, 0, j)),
        out_shape=jax.ShapeDtypeStruct((N_CHIPS, rs, d), g.dtype), compiler_params=_params(("parallel",)),
    )(g)


def _gathered_to_kernel(cfg, name, wg):
    if name in _CHIP_MAJOR:
        return wg
    if name == "w_in":
        return _w_in_layout(cfg, wg)
    if name not in _RELAYOUT:
        return wg.reshape(wg.shape[0] * wg.shape[1], wg.shape[2])
    _, rows, cs = wg.shape
    tr = _pick(rows, _LAYOUT_ROWS, 16)

    def body(w_ref, o_ref):
        o_ref[...] = _to_kernel_layout(cfg, name, jnp.concatenate([w_ref[k] for k in range(N_CHIPS)], axis=1))

    wide = jax.eval_shape(lambda w: _to_kernel_layout(cfg, name, w), jax.ShapeDtypeStruct((rows, N_CHIPS * cs), wg.dtype)).shape[1]
    return pl.pallas_call(
        body, name="layout_" + name, grid=(rows // tr,),
        in_specs=[pl.BlockSpec((N_CHIPS, tr, cs), lambda i: (0, i, 0))], out_specs=pl.BlockSpec((tr, wide), lambda i: (i, 0)),
        out_shape=jax.ShapeDtypeStruct((rows, wide), wg.dtype), compiler_params=_params(("parallel",)),
    )(wg)


def _grad_to_chips(cfg, name, g):
    if name in _CHIP_MAJOR:
        return g
    if name == "w_in":
        return _w_in_grad_to_chips(cfg, g)
    if name not in _RELAYOUT:
        return g.reshape(N_CHIPS, g.shape[0] // N_CHIPS, g.shape[1])
    rows, wide = g.shape
    tr = _pick(rows, _LAYOUT_ROWS, 16)
    cs = jax.eval_shape(lambda v: _from_kernel_layout(cfg, name, v), g).shape[1] // N_CHIPS

    def body(g_ref, o_ref):
        nat = _from_kernel_layout(cfg, name, g_ref[...])
        for k in range(N_CHIPS):
            o_ref[k] = nat[:, k * cs:(k + 1) * cs]

    return pl.pallas_call(
        body, name="layout_grad_" + name, grid=(rows // tr,),
        in_specs=[pl.BlockSpec((tr, wide), lambda i: (i, 0))], out_specs=pl.BlockSpec((N_CHIPS, tr, cs), lambda i: (0, i, 0)),
        out_shape=jax.ShapeDtypeStruct((N_CHIPS, rows, cs), g.dtype), compiler_params=_params(("parallel",)),
    )(g)


def _me():
    return lax.axis_index("x"), lax.axis_index("y"), lax.axis_index("c")


def _other_chips(x, y):
    return [(1 - x, y), (x, 1 - y), (1 - x, 1 - y)]


_ANY = pl.BlockSpec(memory_space=pl.ANY)


BLOCK_ELEMS = 1 << 19
BLOCK_ELEMS_FEW = 1 << 20


def _row_block(rows, cols, mult, elems=BLOCK_ELEMS):
    return _pick(rows, max(mult, elems // cols // mult * mult), mult)


def _scalar(v):
    return v.astype(I32).reshape(1)


def _blocks2d(r, c, mult, elems=BLOCK_ELEMS):
    if r % mult == 0:
        tr = _row_block(r, c, mult, elems)
        return (tr, c), r // tr, lambda i: (i, 0)
    tc = _pick(c, max(LANE, elems // r // LANE * LANE), LANE)
    return (r, tc), c // tc, lambda i: (0, i)


def _by_rows(rows):
    return rows % 32 == 0


def _half_shape(rows, cols):
    return (rows // 2, cols) if _by_rows(rows) else (rows, cols // 2)


def _half_blocks(rows, cols, mult, elems=BLOCK_ELEMS):
    hr, hc = _half_shape(rows, cols)
    block, n, part = _blocks2d(hr, hc, mult, elems)
    assert (hr % mult == 0) == _by_rows(rows), (rows, cols, mult)
    full = (lambda h, i: (h * n + i, 0)) if _by_rows(rows) else (lambda h, i: (0, h * n + i))
    return block, n, full, part


def _half(ref, k, half):
    hr, hc = _half_shape(ref.shape[1], ref.shape[2])
    if _by_rows(ref.shape[1]):
        return ref.at[k, pl.ds(pl.multiple_of(half * hr, 16), hr), :]
    return ref.at[k, :, pl.ds(pl.multiple_of(half * hc, LANE), hc)]


def _shard_blocks(w, br, bc):
    if w.shape[0] == 1:
        def write(ref, v):
            ref[...] = v
        return (lambda f: pl.BlockSpec((None, br, bc), lambda *a: (0, *f(*a)))), (lambda ref: ref[...]), write
    assert w.shape[1] == 1 and br == w.shape[0], w.shape

    def write_rows(ref, v):
        ref[:, 0, :] = v
    return (lambda f: pl.BlockSpec((br, 1, bc), lambda *a: (0, 0, f(*a)[1]))), (lambda ref: ref[:, 0, :]), write_rows


def _stage_shard(name, w, chip, after=None):
    rs, cs = w.shape[0] * w.shape[1], w.shape[2]
    (br, bc), n, idx = _blocks2d(rs, cs, 16, BLOCK_ELEMS_FEW)
    spec, get, _ = _shard_blocks(w, br, bc)

    def body(chip_ref, w_ref, *refs):
        refs[-1][...] = get(w_ref).astype(BF16)

    return pl.pallas_call(
        body, name="stage_" + name,
        grid_spec=pltpu.PrefetchScalarGridSpec(
            num_scalar_prefetch=1, grid=(n,),
            in_specs=[spec(lambda i, chip_ref: idx(i))] + ([] if after is None else [_ANY]),
            out_specs=pl.BlockSpec((None, br, bc), lambda i, chip_ref: (chip_ref[0], *idx(i)))),
        out_shape=jax.ShapeDtypeStruct((N_CHIPS, rs, cs), BF16),
        compiler_params=_params(("parallel",)),
    )(_scalar(chip), w, *([] if after is None else [after]))


_HBM = pl.BlockSpec(memory_space=pltpu.HBM)
_SEM = pl.BlockSpec(memory_space=pltpu.SEMAPHORE)
_EFFECT = pltpu.SideEffectType.DATAFLOW_SIDE_EFFECTING


def _split_start(name, bufs, n_copies, copies, after):
    n = len(bufs)

    def body(*refs):
        for cp in copies(refs[:n], refs[n + 1], refs[n + 2]):
            cp.start()
        refs[-1][...] = jnp.zeros_like(refs[-1])

    res = pl.pallas_call(
        body, name=name,
        out_shape=(pltpu.SemaphoreType.DMA((n_copies,)), pltpu.SemaphoreType.DMA((n_copies,)),
                   *[pltpu.HBM(b.shape, b.dtype) for b in bufs], jax.ShapeDtypeStruct((8, LANE), F32)),
        in_specs=[_HBM] * n + [_ANY], out_specs=(_SEM, _SEM, *[_HBM] * n, pl.BlockSpec(memory_space=pltpu.VMEM)),
        input_output_aliases={i: 2 + i for i in range(n)},
        compiler_params=pltpu.CompilerParams(has_side_effects=_EFFECT),
    )(*[pltpu.with_memory_space_constraint(b, pltpu.HBM) for b in bufs], after)
    return res[0], res[1], list(res[2:2 + n]), res[-1]


def _split_wait(name, send_sems, recv_sems, bufs, after, copies):
    n = len(bufs)

    def body(*refs):
        for cp in copies(refs[:n], refs[n], refs[n + 1]):
            cp.wait_send()
            cp.wait_recv()

    return list(pl.pallas_call(
        body, name=name, out_shape=[pltpu.HBM(b.shape, b.dtype) for b in bufs],
        in_specs=[_HBM] * n + [_SEM, _SEM, _ANY], out_specs=[_HBM] * n,
        input_output_aliases={i: i for i in range(n)},
        compiler_params=pltpu.CompilerParams(has_side_effects=_EFFECT),
    )(*bufs, send_sems, recv_sems, after))


def _gather_to_chips(bufs, send_sems, recv_sems):
    x, y, c = _me()
    return [pltpu.make_async_remote_copy(src_ref=_half(b, 2 * x + y, c), dst_ref=_half(b, 2 * x + y, c),
                                         send_sem=send_sems.at[3 * w + j], recv_sem=recv_sems.at[3 * w + j],
                                         device_id=(cx, cy, c), device_id_type=MESH_ID)
            for w, b in enumerate(bufs) for j, (cx, cy) in enumerate(_other_chips(x, y))]


def _gather_to_sibling(bufs, send_sems, recv_sems):
    x, y, c = _me()
    return [pltpu.make_async_remote_copy(src_ref=_half(b, 2 * cx + cy, c), dst_ref=_half(b, 2 * cx + cy, c),
                                         send_sem=send_sems.at[3 * w + j], recv_sem=recv_sems.at[3 * w + j],
                                         device_id=(x, y, 1 - c), device_id_type=MESH_ID)
            for w, b in enumerate(bufs) for j, (cx, cy) in enumerate(_other_chips(x, y))]


def _pair_exchange(name, grads):
    n = len(grads)

    def body(*refs):
        ins, outs, send_sems, recv_sems = refs[:n], refs[n:2 * n], refs[2 * n], refs[2 * n + 1]
        x, y, c = _me()
        cps = []
        for w, (g_ref, o_ref) in enumerate(zip(ins, outs)):
            cps.append(pltpu.make_async_remote_copy(src_ref=_half(g_ref, slice(None), 1 - c), dst_ref=o_ref,
                                                    send_sem=send_sems.at[w], recv_sem=recv_sems.at[w],
                                                    device_id=(x, y, 1 - c), device_id_type=MESH_ID))
            cps[-1].start()
        for cp in cps:
            cp.wait()

    return pl.pallas_call(
        body, name="pair_exchange_" + name, in_specs=[_ANY] * n, out_specs=[_ANY] * n,
        out_shape=[jax.ShapeDtypeStruct((g.shape[0], *_half_shape(g.shape[1], g.shape[2])), g.dtype) for g in grads],
        scratch_shapes=[pltpu.SemaphoreType.DMA((n,)), pltpu.SemaphoreType.DMA((n,))],
    )(*grads)


def _pair_copies(grads, lands, send_sems, recv_sems):
    x, y, c = _me()
    return [pltpu.make_async_remote_copy(src_ref=_half(g_ref, slice(None), 1 - c), dst_ref=l_ref, send_sem=send_sems.at[w],
                                         recv_sem=recv_sems.at[w], device_id=(x, y, 1 - c), device_id_type=MESH_ID)
            for w, (g_ref, l_ref) in enumerate(zip(grads, lands))]


def _pair_exchange_start(name, grads):
    n = len(grads)
    lands = [lax.empty((g.shape[0], *_half_shape(g.shape[1], g.shape[2])), g.dtype) for g in grads]
    send_sems, recv_sems, bufs, token = _split_start(
        "pair_exchange_start_" + name, [*grads, *lands], n, lambda refs, ss, rs: _pair_copies(refs[:n], refs[n:], ss, rs),
        jnp.zeros((8, LANE), F32))
    return (send_sems, recv_sems, bufs), token


def _pair_exchange_wait(name, state, after):
    send_sems, recv_sems, bufs = state
    n = len(bufs) // 2
    bufs = _split_wait("pair_exchange_wait_" + name, send_sems, recv_sems, bufs, after,
                       lambda refs, ss, rs: _pair_copies(refs[:n], refs[n:], ss, rs))
    return bufs[:n], bufs[n:]


def _pair_sum(name, g, theirs, c):
    (br, bc), nb, full, part = _half_blocks(g.shape[1], g.shape[2], 16, 2 * BLOCK_ELEMS_FEW)

    def body(c_ref, a_ref, b_ref, o_ref):
        o_ref[...] = (a_ref[...].astype(F32) + b_ref[...].astype(F32)).astype(o_ref.dtype)

    return pl.pallas_call(
        body, name="pair_sum_" + name,
        grid_spec=pltpu.PrefetchScalarGridSpec(
            num_scalar_prefetch=1, grid=(N_CHIPS, nb),
            in_specs=[pl.BlockSpec((None, br, bc), lambda k, i, c_ref: (k, *full(c_ref[0], i))),
                      pl.BlockSpec((None, br, bc), lambda k, i, c_ref: (k, *part(i)))],
            out_specs=pl.BlockSpec((None, br, bc), lambda k, i, c_ref: (k, *part(i)))),
        out_shape=jax.ShapeDtypeStruct(theirs.shape, BF16),
        compiler_params=_params(("parallel", "parallel")),
    )(_scalar(c), g, theirs)


def _chip_copies(srcs, lands, send_sems, recv_sems):
    x, y, c = _me()
    return [pltpu.make_async_remote_copy(src_ref=s_ref.at[2 * cx + cy], dst_ref=l_ref.at[j], send_sem=send_sems.at[3 * w + j],
                                         recv_sem=recv_sems.at[3 * w + j], device_id=(cx, cy, c), device_id_type=MESH_ID)
            for w, (s_ref, l_ref) in enumerate(zip(srcs, lands)) for j, (cx, cy) in enumerate(_other_chips(x, y))]


def _chip_exchange_start(name, sums):
    n = len(sums)
    lands = [lax.empty((3,) + s.shape[1:], s.dtype) for s in sums]
    send_sems, recv_sems, bufs, token = _split_start(
        "chip_exchange_start_" + name, [*sums, *lands], 3 * n, lambda refs, ss, rs: _chip_copies(refs[:n], refs[n:], ss, rs),
        jnp.zeros((8, LANE), F32))
    return send_sems, recv_sems, bufs[:n], bufs[n:], token


def _chip_exchange_wait(name, send_sems, recv_sems, sums, lands, after):
    n = len(sums)
    bufs = _split_wait("chip_exchange_wait_" + name, send_sems, recv_sems, [*sums, *lands], after,
                       lambda refs, ss, rs: _chip_copies(refs[:n], refs[n:], ss, rs))
    return bufs[:n], bufs[n:]


def _chip_sum(name, sums, theirs, chip):
    _, h, cs = sums.shape
    (br, bc), nb, idx = _blocks2d(h, cs, 16, BLOCK_ELEMS_FEW)

    def body(chip_ref, s_ref, t_ref, o_ref):
        acc = s_ref[...].astype(F32)
        for k in range(3):
            acc = acc + t_ref[k].astype(F32)
        o_ref[...] = acc

    return pl.pallas_call(
        body, name="chip_sum_" + name,
        grid_spec=pltpu.PrefetchScalarGridSpec(
            num_scalar_prefetch=1, grid=(nb,),
            in_specs=[pl.BlockSpec((None, br, bc), lambda i, chip_ref: (chip_ref[0], *idx(i))),
                      pl.BlockSpec((3, br, bc), lambda i, chip_ref: (0, *idx(i)))],
            out_specs=pl.BlockSpec((br, bc), lambda i, chip_ref: idx(i))),
        out_shape=jax.ShapeDtypeStruct((h, cs), F32),
        compiler_params=_params(("parallel",)),
    )(_scalar(chip), sums, theirs)


def _sibling_exchange(name, halves):
    n = len(halves)

    def body(*refs):
        ins, outs, send_sems, recv_sems = refs[:n], refs[n:2 * n], refs[2 * n], refs[2 * n + 1]
        x, y, c = _me()
        cps = []
        for w, (h_ref, o_ref) in enumerate(zip(ins, outs)):
            cps.append(pltpu.make_async_remote_copy(src_ref=h_ref, dst_ref=o_ref, send_sem=send_sems.at[w], recv_sem=recv_sems.at[w],
                                                    device_id=(x, y, 1 - c), device_id_type=MESH_ID))
            cps[-1].start()
        for cp in cps:
            cp.wait()

    return pl.pallas_call(
        body, name="sibling_exchange_" + name, in_specs=[_ANY] * n, out_specs=[_ANY] * n,
        out_shape=[jax.ShapeDtypeStruct(h.shape, h.dtype) for h in halves],
        scratch_shapes=[pltpu.SemaphoreType.DMA((n,)), pltpu.SemaphoreType.DMA((n,))],
    )(*halves)


def _allreduce_small(name, vec, after):
    def body(v_ref, after_ref, o_ref, buf_ref, send_sems, recv_sems):
        x, y, c = _me()
        me = 4 * x + 2 * y + c
        cps = []
        for p in range(1, 8):
            px, py, pc = x ^ (p >> 2), y ^ ((p >> 1) & 1), c ^ (p & 1)
            cps.append(pltpu.make_async_remote_copy(src_ref=v_ref, dst_ref=buf_ref.at[me], send_sem=send_sems.at[p - 1],
                                                    recv_sem=recv_sems.at[p - 1], device_id=(px, py, pc), device_id_type=MESH_ID))
            cps[-1].start()
        buf_ref[me] = v_ref[...]
        for p in range(1, 8):
            theirs = buf_ref.at[me ^ p]
            pltpu.make_async_remote_copy(src_ref=theirs, dst_ref=theirs, send_sem=send_sems.at[p - 1], recv_sem=recv_sems.at[p - 1],
                                         device_id=(x, y, c), device_id_type=MESH_ID).wait_recv()
        for cp in cps:
            cp.wait_send()
        acc = buf_ref[0]
        for k in range(1, 8):
            acc = acc + buf_ref[k]
        o_ref[...] = acc

    vm = pl.BlockSpec(memory_space=pltpu.VMEM)
    return pl.pallas_call(
        body, name=name, in_specs=[vm, _ANY], out_specs=vm, out_shape=jax.ShapeDtypeStruct(vec.shape, F32),
        scratch_shapes=[pltpu.VMEM((8,) + vec.shape, F32), pltpu.SemaphoreType.DMA((7,)), pltpu.SemaphoreType.DMA((7,))],
    )(vec, after)


def _adam_math(w, g, m, v):
    m = ADAM_B1 * m + (1.0 - ADAM_B1) * g
    v = ADAM_B2 * v + (1.0 - ADAM_B2) * (g * g)
    m_hat = m / (1.0 - ADAM_B1 ** ADAM_STEP)
    v_hat = v / (1.0 - ADAM_B2 ** ADAM_STEP)
    return -ADAM_LR * (m_hat / (jnp.sqrt(v_hat) + ADAM_EPS) + ADAM_WD * w), m, v


def _adamw(name, w, g, m, v):
    R, C = w.shape
    tr = _row_block(R, C, 8)

    def body(w_ref, g_ref, m_ref, v_ref, d_ref, nm_ref, nv_ref):
        d_ref[...], nm_ref[...], nv_ref[...] = _adam_math(w_ref[...], g_ref[...], m_ref[...], v_ref[...])

    blk = pl.BlockSpec((tr, C), lambda i: (i, 0))
    return pl.pallas_call(
        body, name=name, grid=(R // tr,), in_specs=[blk] * 4, out_specs=[blk] * 3,
        out_shape=[jax.ShapeDtypeStruct((R, C), F32)] * 3, compiler_params=_params(("parallel",)),
    )(w, g, m, v)


def _adamw_halves(name, w, mine, theirs, m, v, c):
    rs, cs = w.shape[0] * w.shape[1], w.shape[2]
    (br, bc), nb, whole, half = _half_blocks(rs, cs, 8)
    spec, get, put = _shard_blocks(w, br, bc)

    def body(c_ref, w_ref, a_ref, b_ref, m_ref, v_ref, g_ref, d_ref, nm_ref, nv_ref):
        g = jnp.where(pl.program_id(0) == c_ref[0], a_ref[...], b_ref[...])
        put(g_ref, g)
        for ref, val in zip((d_ref, nm_ref, nv_ref), _adam_math(get(w_ref), g, get(m_ref), get(v_ref))):
            put(ref, val)

    full = spec(lambda s, i, c_ref: whole(s, i))
    part = pl.BlockSpec((br, bc), lambda s, i, c_ref: half(i))
    return pl.pallas_call(
        body, name=name,
        grid_spec=pltpu.PrefetchScalarGridSpec(num_scalar_prefetch=1, grid=(2, nb), in_specs=[full, part, part, full, full],
                                               out_specs=[full] * 4),
        out_shape=[jax.ShapeDtypeStruct(w.shape, F32)] * 4, compiler_params=_params(("parallel", "parallel")),
    )(_scalar(c), w, mine, theirs, m, v)


def _pack_small(arrs):
    flat = jnp.concatenate([a.reshape(-1) for a in arrs])
    n = -(-flat.shape[0] // (8 * LANE)) * 8 * LANE
    return jnp.pad(flat, (0, n - flat.shape[0])).reshape(8, n // 8)


def _unpack_small(vec, shapes):
    flat, out, off = vec.reshape(-1), [], 0
    for s in shapes:
        out.append(flat[off:off + s[0] * s[1]].reshape(s))
        off += s[0] * s[1]
    return out


class _LateWeights:
    def __init__(self, cfg, tag, names, staged, after):
        self.cfg, self.tag, self.names, self.k = cfg, tag, names, 3 * len(names)
        self.send, self.recv, self.bufs, self.token = _split_start(f"gather_{tag}_chips_start", staged, self.k, _gather_to_chips,
                                                                    after)

    def pass_on(self, after):
        bufs = _split_wait(f"gather_{self.tag}_chips_wait", self.send, self.recv, self.bufs, after, _gather_to_chips)
        self.send, self.recv, self.bufs, token = _split_start(f"gather_{self.tag}_sibling_start", bufs, self.k, _gather_to_sibling,
                                                               self.token)
        return token

    def arrived(self, after):
        bufs = _split_wait(f"gather_{self.tag}_sibling_wait", self.send, self.recv, self.bufs, after, _gather_to_sibling)
        return {n: _gathered_to_kernel(self.cfg, n, b) for n, b in zip(self.names, bufs)}


def _step(cfg, a):
    chip = 2 * lax.axis_index("x") + lax.axis_index("y")
    core = lax.axis_index("c")
    big = BIG

    ffn = ("w_gate", "w_up", "w_down")
    first = ("w_in", "w_uq", "w_ukv")
    staged = {"w_in": _stage_shard("w_in", a["w_in"], chip)}
    in_weight = _LateWeights(cfg, "in", ("w_in",), [staged["w_in"]], jnp.zeros((8, LANE), F32))
    staged.update({n: _stage_shard(n, a[n], chip, in_weight.token) for n in big if n != "w_in"})
    in_sibling_leg = in_weight.pass_on(staged[big[-1]])

    sp = {n: a[n] for n in SMALL}
    sharded = _pack_small([a[n] for n in SMALL_SHARDED])
    slot = jnp.where(lax.broadcasted_iota(I32, (N_CHIPS,) + sharded.shape, 0) == chip, 0.5 * sharded[None], 0.0)
    allp = _allreduce_small("allgather_small", slot.reshape(N_CHIPS * 8, -1), in_sibling_leg).reshape((N_CHIPS,) + sharded.shape)
    per_chip = [_unpack_small(allp[ch], [a[n].shape for n in SMALL_SHARDED]) for ch in range(N_CHIPS)]
    for k, n in enumerate(SMALL_SHARDED):
        sp[n] = jnp.concatenate([per_chip[ch][k] for ch in range(N_CHIPS)], axis=1)
    W = in_weight.arrived(allp)

    mla_weights = _LateWeights(cfg, "mla", first[1:], [staged[n] for n in first[1:]], W["w_in"])
    out_weight = _LateWeights(cfg, "out", ("w_out",), [staged["w_out"]], mla_weights.token)
    ffn_weights = _LateWeights(cfg, "ffn", ffn, [staged[n] for n in ffn], out_weight.token)
    sp["mix_pre_g"] = sp["mix_pre_g"] + (mla_weights.token[0, 0] + out_weight.token[0, 0] + ffn_weights.token[0, 0])

    state = {}

    def ffn_grads_ready(grads):
        state["ffn_pairs"], token = _pair_exchange_start("ffn", [_grad_to_chips(cfg, n, grads[n]) for n in ffn_grads])
        return token

    def pair_sums(names, grads, theirs):
        return [_pair_sum(n, g, t, core) for n, g, t in zip(names, grads, theirs)]

    def early_grads_ready(grads):
        g_out = [_grad_to_chips(cfg, "w_out", grads["w_out"])]
        g_ffn, t_ffn = _pair_exchange_wait("ffn", state["ffn_pairs"], g_out[0])
        sums = pair_sums(ffn_grads, g_ffn, t_ffn) + pair_sums(["w_out"], g_out, _pair_exchange("out", g_out))
        state["early"] = _chip_exchange_start("early", sums)
        return state["early"][-1]

    def reduced_halves(tag, names, after):
        send_sems, recv_sems, s_bufs, l_bufs, _ = state[tag]
        s_bufs, l_bufs = _chip_exchange_wait(tag, send_sems, recv_sems, s_bufs, l_bufs, after)
        return [_chip_sum(n, s, t, chip) for n, s, t in zip(names, s_bufs, l_bufs)]

    def in_grad_ready(grads):
        grads = [_grad_to_chips(cfg, n, grads[n]) for n in first]
        state["rest"] = _chip_exchange_start("rest", pair_sums(first, grads, _pair_exchange("rest", grads)))
        return state["rest"][-1]

    ffn_grads = ("w_down", "w_gate", "w_up")
    early = ffn_grads + ("w_out",)
    loss, grad_x, gW, gs = _local_grads(cfg, a["x"], a["loss_target"], W, sp, mla_weights, out_weight, ffn_weights,
                                        ffn_grads_ready, early_grads_ready, in_grad_ready)
    out = {"grad_x": grad_x}

    def adamw(names, mine, theirs):
        for n, gm, gt in zip(names, mine, theirs):
            out["grad_" + n], out["delta_" + n], out["new_m_" + n], out["new_v_" + n] = _adamw_halves(
                "adamw_" + n, a[n], gm, gt, a["m_" + n], a["v_" + n], core)

    mine = reduced_halves("early", early, grad_x)
    adamw(early, mine, _sibling_exchange("early", mine))
    mine = reduced_halves("rest", first, out["new_v_" + early[-1]])
    theirs = _sibling_exchange("rest", mine)
    adamw(first, mine, theirs)

    shapes = [gs[n].shape for n in SMALL] + [(1, LANE)]
    red = _unpack_small(_allreduce_small("allreduce_small", _pack_small([gs[n] for n in SMALL] + [loss]), theirs[0]), shapes)
    g_small = dict(zip(SMALL, red[:-1]))
    for n in SMALL_SHARDED:
        cs = a[n].shape[1]
        g_small[n] = lax.dynamic_slice_in_dim(g_small[n], chip * cs, cs, axis=1)
    out["loss"] = red[-1][0, 0]
    sshapes = [a[n].shape for n in SMALL]
    d, nm, nv = _adamw("adamw_small", _pack_small([a[n] for n in SMALL]), _pack_small([g_small[n] for n in SMALL]),
                       _pack_small([a["m_" + n] for n in SMALL]), _pack_small([a["v_" + n] for n in SMALL]))
    for n, dd, mm, vv in zip(SMALL, _unpack_small(d, sshapes), _unpack_small(nm, sshapes), _unpack_small(nv, sshapes)):
        out["grad_" + n], out["delta_" + n], out["new_m_" + n], out["new_v_" + n] = g_small[n], dd, mm, vv
    return out


def kernel(x, mix_pre_g, w_in, q_norm_g, w_uq, kv_norm_g, w_ukv, ssm_conv_w, ssm_conv_b, dt_bias, a_log, d_skip, ssm_norm_g, w_out, mix_post_g, ffn_pre_g, w_gate, w_up, ffn_conv_w, ffn_conv_b, w_down, ffn_post_g, loss_target, m_mix_pre_g, m_w_in, m_q_norm_g, m_w_uq, m_kv_norm_g, m_w_ukv, m_ssm_conv_w, m_ssm_conv_b, m_dt_bias, m_a_log, m_d_skip, m_ssm_norm_g, m_w_out, m_mix_post_g, m_ffn_pre_g, m_w_gate, m_w_up, m_ffn_conv_w, m_ffn_conv_b, m_w_down, m_ffn_post_g, v_mix_pre_g, v_w_in, v_q_norm_g, v_w_uq, v_kv_norm_g, v_w_ukv, v_ssm_conv_w, v_ssm_conv_b, v_dt_bias, v_a_log, v_d_skip, v_ssm_norm_g, v_w_out, v_mix_post_g, v_ffn_pre_g, v_w_gate, v_w_up, v_ffn_conv_w, v_ffn_conv_b, v_w_down, v_ffn_post_g):
    args = dict(locals())
    def given(k, v):
        if k in ("w_in", "m_w_in", "v_w_in"):
            return jnp.transpose(v, (2, 0, 1))
        return v if k.removeprefix("m_").removeprefix("v_") in BIG or v.ndim < 3 else v[0]

    out = _step(_FULL, {k: given(k, v) for k, v in args.items()})
    res = [out["loss"], out["grad_x"][None]]
    for pre in ("grad_", "delta_", "new_m_", "new_v_"):
        for n in WEIGHTS:
            o = out[pre + n]
            res.append(jnp.transpose(o, (1, 2, 0)) if n == "w_in" else o if n in BIG or args[n].ndim < 3 else o[None])
    return tuple(res)
```

```python
import functools
import math

import jax
import jax.numpy as jnp
from jax import lax
from jax.experimental import pallas as pl
from jax.experimental.pallas import tpu as pltpu

F32, BF16, I32 = jnp.float32, jnp.bfloat16, jnp.int32
NN = (((1,), (0,)), ((), ()))
NT = (((1,), (1,)), ((), ()))
TN = (((0,), (0,)), ((), ()))
HI = lax.Precision.HIGHEST
MESH_ID = pl.DeviceIdType.MESH

EPS = 1e-6
CHUNK = 64
NOPE, ROPE, VH = 128, 64, 128
ROPE_THETA = 10000.0
HP, NST = 64, 128
SSM_K, FFN_K = 4, 3
LANE = 128
N_CHIPS = 4
VMEM_LIMIT = 52 * 1024 * 1024
MM_TILE, MM_TILE_K = 1408, 2816

ADAM_LR, ADAM_B1, ADAM_B2, ADAM_EPS, ADAM_WD, ADAM_STEP = 0.001, 0.9, 0.999, 1e-08, 0.01, 10


class _Cfg:
    def __init__(self, S, D, QL, KVL, H, HS, G, DFF, T):
        self.S, self.D, self.QL, self.KVL, self.H, self.HS, self.G, self.DFF, self.T = S, D, QL, KVL, H, HS, G, DFF, T
        self.INNER = HS * HP
        self.CONVCH = self.INNER + 2 * G * NST
        self.QW = H * (NOPE + ROPE)
        self.KVW = H * (NOPE + VH)
        self.MLAW = H * VH
        self.MIXW = self.MLAW + self.INNER
        self.IN_COLS = QL + KVL + ROPE + self.INNER + self.CONVCH + HS
        natural, at = {}, 0
        for name, w in (("c_q", QL), ("c_kv", KVL), ("kr", ROPE), ("z", self.INNER), ("xbc", self.CONVCH), ("dt", HS)):
            natural[name] = (at, w)
            at += w
        self.seg, taken = {}, []
        for name in sorted(natural, key=lambda n: -natural[n][1]):
            w = -(-natural[name][1] // LANE) * LANE
            off = next(o for o in range(0, 64 * w, w) if all(o + w <= t or o >= t + tw for t, tw in taken))
            taken.append((off, w))
            self.seg[name] = (off, w) + natural[name]
        self.EXT = max(o + w for o, w in taken)
        self.NPAIR = HS // 2
        self.REP = HS // G

    def window(self, name):
        off, w, _, _ = self.seg[name]
        return w, off // w


_FULL = _Cfg(S=2048, D=2048, QL=768, KVL=512, H=8, HS=16, G=2, DFF=5632, T=256)
BIG = ("w_in", "w_uq", "w_ukv", "w_out", "w_gate", "w_up", "w_down")

SMALL = ("mix_pre_g", "q_norm_g", "kv_norm_g", "ssm_conv_w", "ssm_conv_b", "dt_bias", "a_log", "d_skip", "ssm_norm_g",
         "mix_post_g", "ffn_pre_g", "ffn_conv_w", "ffn_conv_b", "ffn_post_g")
SMALL_SHARDED = ("ssm_conv_w", "ffn_conv_w")
WEIGHTS = ("mix_pre_g", "w_in", "q_norm_g", "w_uq", "kv_norm_g", "w_ukv", "ssm_conv_w", "ssm_conv_b", "dt_bias", "a_log",
           "d_skip", "ssm_norm_g", "w_out", "mix_post_g", "ffn_pre_g", "w_gate", "w_up", "ffn_conv_w", "ffn_conv_b",
           "w_down", "ffn_post_g")


def _pick(n, target, mult):
    best = None
    for d in range(mult, min(n, target) + 1, mult):
        if n % d == 0:
            best = d
    return best if best is not None else n


def _params(sem=None):
    kw = dict(vmem_limit_bytes=VMEM_LIMIT)
    if sem is not None:
        kw["dimension_semantics"] = sem
    return pltpu.CompilerParams(**kw)


def _dot(a, b, dims=NN, precision=None):
    return lax.dot_general(a, b, dims, preferred_element_type=F32, precision=precision)


def _sigmoid(x):
    return 1.0 / (1.0 + jnp.exp(-x))


def _rs(x):
    return lax.rsqrt(jnp.mean(x * x, axis=-1, keepdims=True) + EPS)


def _rms_back(xh, r, dn):
    return r * (dn - xh * jnp.mean(dn * xh, axis=-1, keepdims=True))


def _colsum(v):
    return jnp.sum(v, axis=0, keepdims=True)


def _matmul(name, a, b, mode, out_dtype, a2=None, b2=None, chips=False):
    cs = None
    if mode == "nn":
        (M, K), N = a.shape, b.shape[-1]
        if chips:
            cs, N = N, N_CHIPS * N
    elif mode == "nt":
        (M, K), N = a.shape, b.shape[-2]
        if chips:
            cs = b.shape[-1]
    else:
        (K, M), N = a.shape, b.shape[1]
        if chips:
            cs = N // N_CHIPS
    tm = _pick(M, MM_TILE, LANE)
    tn = _pick(cs if chips and mode != "nt" else N, MM_TILE, LANE)
    tk = _pick(cs, MM_TILE, LANE) if chips and mode == "nt" else _pick(K, MM_TILE_K, LANE)
    nk = K // tk
    dims = {"nn": NN, "nt": NT, "tn": TN}[mode]
    a_spec = pl.BlockSpec((tk, tm), lambda i, j, k: (k, i)) if mode == "tn" else pl.BlockSpec((tm, tk), lambda i, j, k: (i, k))
    b_spec = pl.BlockSpec((tn, tk), lambda i, j, k: (j, k)) if mode == "nt" else pl.BlockSpec((tk, tn), lambda i, j, k: (k, j))
    o_spec = pl.BlockSpec((tm, tn), lambda i, j, k: (i, j))
    o_shape = (M, N)
    if chips and mode == "nn":
        per = cs // tn
        b_spec = pl.BlockSpec((None, tk, tn), lambda i, j, k: (j // per, k, j % per))
    elif chips and mode == "nt":
        per = cs // tk
        b_spec = pl.BlockSpec((None, tn, tk), lambda i, j, k: (k // per, j, k % per))
    elif chips:
        per = cs // tn
        o_spec = pl.BlockSpec((None, tm, tn), lambda i, j, k: (j // per, i, j % per))
        o_shape = (N_CHIPS, M, cs)
    two = a2 is not None

    def product(refs):
        part = _dot(refs[0][...].astype(BF16), refs[1][...].astype(BF16), dims)
        if two:
            part += _dot(refs[2][...].astype(BF16), refs[3][...].astype(BF16), dims)
        return part

    def body_whole_k(*refs):
        refs[-1][...] = product(refs).astype(refs[-1].dtype)

    def body(*refs):
        o_ref, acc_ref = refs[-2], refs[-1]
        k = pl.program_id(2)

        @pl.when(k == 0)
        def _():
            acc_ref[...] = product(refs)

        @pl.when(k > 0)
        def _():
            acc_ref[...] += product(refs)

        @pl.when(k == nk - 1)
        def _():
            o_ref[...] = acc_ref[...].astype(o_ref.dtype)

    ins = (a, b, a2, b2) if two else (a, b)
    return pl.pallas_call(
        body_whole_k if nk == 1 else body, name=name, grid=(M // tm, N // tn, nk),
        in_specs=[a_spec, b_spec] * (2 if two else 1),
        out_specs=o_spec,
        out_shape=jax.ShapeDtypeStruct(o_shape, out_dtype),
        scratch_shapes=[] if nk == 1 else [pltpu.VMEM((tm, tn), F32)],
        compiler_params=_params(("parallel", "parallel", "arbitrary")),
    )(*ins)


def _window(a):
    return (a[0], *a[1]) if isinstance(a, tuple) else (a, a.shape[1], 0)


def _rowwise(name, fn, rows, mats, outs, reds, ts):
    rows, widths, blocks = zip(*[_window(a) for a in rows])
    S = rows[0].shape[0]
    nr, nm, no = len(rows), len(mats), len(outs)

    def body(*refs):
        res = fn(*[r[...] for r in refs[:nr + nm]])
        res = res if isinstance(res, (tuple, list)) else (res,)
        for r, v in zip(refs[nr + nm:nr + nm + no], res[:no]):
            r[...] = v.astype(r.dtype)
        first = pl.program_id(0) == 0
        for r, v in zip(refs[nr + nm + no:], res[no:]):
            @pl.when(first)
            def _():
                r[...] = jnp.broadcast_to(v, r.shape)

            @pl.when(jnp.logical_not(first))
            def _():
                r[...] += jnp.broadcast_to(v, r.shape)

    in_specs = [pl.BlockSpec((ts, w), lambda i, b=b: (i, b)) for w, b in zip(widths, blocks)]
    in_specs += [pl.BlockSpec(m.shape, lambda i, nd=m.ndim: (0,) * nd) for m in mats]
    out_specs = [pl.BlockSpec((ts, w), lambda i: (i, 0)) for w, _ in outs]
    out_specs += [pl.BlockSpec(s, lambda i: (0, 0)) for s in reds]
    out_shape = [jax.ShapeDtypeStruct((S, w), dt) for w, dt in outs] + [jax.ShapeDtypeStruct(s, F32) for s in reds]
    return pl.pallas_call(
        body, name=name, grid=(S // ts,), in_specs=in_specs, out_specs=out_specs, out_shape=out_shape,
        compiler_params=_params(("arbitrary",) if reds else ("parallel",)),
    )(*rows, *mats)


def _shift_down(v, s):
    if s == 0:
        return v
    rows = lax.broadcasted_iota(I32, v.shape, 0)
    return jnp.where(rows >= s, pltpu.roll(v, s, 0), 0.0)


def _shift_up(v, s):
    if s == 0:
        return v
    n = v.shape[0]
    rows = lax.broadcasted_iota(I32, v.shape, 0)
    return jnp.where(rows < n - s, pltpu.roll(v, n - s, 0), 0.0)


def _conv(x, w, b):
    K = w.shape[0]
    y = jnp.broadcast_to(b, x.shape)
    for k in range(K):
        y = y + w[k:k + 1, :] * _shift_down(x, K - 1 - k)
    return y


def _conv_back(x, w, dc):
    K = w.shape[0]
    dx = jnp.zeros_like(x)
    dw = []
    for k in range(K):
        dx = dx + w[k:k + 1, :] * _shift_up(dc, K - 1 - k)
        dw.append(_colsum(dc * _shift_down(x, K - 1 - k)))
    return dx, jnp.concatenate(dw, axis=0), _colsum(dc)


def _colwise(name, fn, cols, vecs, outs, pouts, tc):
    cols, widths, blocks = zip(*[_window(a) for a in cols])
    S, C = cols[0].shape[0], widths[0]
    firsts = [b * (C // tc) for b in blocks]
    nc_, nv, no = len(cols), len(vecs), len(outs)

    def body(*refs):
        res = fn(*[r[...] for r in refs[:nc_ + nv]])
        res = res if isinstance(res, (tuple, list)) else (res,)
        for r, v in zip(refs[nc_ + nv:], res):
            r[...] = v.astype(r.dtype)

    in_specs = [pl.BlockSpec((S, tc), lambda j, f=f: (0, f + j)) for f in firsts]
    in_specs += [pl.BlockSpec((v.shape[0], tc), lambda j: (0, j)) for v in vecs]
    out_specs = [pl.BlockSpec((S, tc), lambda j: (0, j)) for _ in outs] + [pl.BlockSpec((k, tc), lambda j: (0, j)) for k in pouts]
    out_shape = [jax.ShapeDtypeStruct((S, C), dt) for dt in outs] + [jax.ShapeDtypeStruct((k, C), F32) for k in pouts]
    return pl.pallas_call(
        body, name=name, grid=(C // tc,), in_specs=in_specs, out_specs=out_specs, out_shape=out_shape,
        compiler_params=_params(("parallel",)),
    )(*cols, *vecs)


_G0, _G1 = math.sqrt(2.0 / math.pi), 0.044715


def _gelu(g):
    th = jnp.tanh(_G0 * (g + _G1 * g * g * g))
    return 0.5 * g * (1.0 + th), th


def _ffn_act(gate_pre, up, w, b):
    act, _ = _gelu(_conv(gate_pre, w, b))
    return act * up


def _ffn_act_back(dact, gate_pre, up, w, b):
    g = _conv(gate_pre, w, b)
    ge, th = _gelu(g)
    dge = 0.5 * (1.0 + th) + 0.5 * g * (1.0 - th * th) * _G0 * (1.0 + 3.0 * _G1 * g * g)
    dup = dact * ge
    dgate_pre, dw, db = _conv_back(gate_pre, w, dact * up * dge)
    return dgate_pre, dup, dw, db


def _ssm_act(xbc, w, b):
    c = _conv(xbc, w, b)
    return c * _sigmoid(c)


def _ssm_act_back(dxc, xbc, w, b):
    c = _conv(xbc, w, b)
    sg = _sigmoid(c)
    return _conv_back(xbc, w, dxc * sg * (1.0 + c * (1.0 - sg)))


def _rope_tables(S):
    inv = 1.0 / (ROPE_THETA ** (jnp.arange(0, ROPE, 2, dtype=F32) / ROPE))
    ang = jnp.arange(S, dtype=F32)[:, None] * inv[None, :]
    cos, sin = jnp.cos(ang), jnp.sin(ang)
    return jnp.tile(cos, (1, 4)), jnp.tile(jnp.concatenate([-sin, sin], axis=1), (1, 2))


def _swap_halves(x):
    lane = lax.broadcasted_iota(I32, x.shape, 1)
    w = x.shape[1]
    return jnp.where((lane % ROPE) < ROPE // 2, pltpu.roll(x, w - ROPE // 2, 1), pltpu.roll(x, ROPE // 2, 1))


def _rot(x, cos2, sin2):
    return x * cos2 + _swap_halves(x) * sin2


def _rot_back(dy, cos2, sin2):
    return dy * cos2 + _swap_halves(dy * sin2)


def _mla_pack(cfg, q, kv, kr, cos2, sin2):
    S, H = cfg.S, cfg.H
    ts = _pick(S, 256, 8)
    kr, _, kr_block = _window(kr)

    def body(q_ref, kv_ref, kr_ref, c_ref, s_ref, Q_ref, K_ref, V_ref):
        c2, s2 = c_ref[...], s_ref[...]
        krr = _rot(kr_ref[...], c2, s2)
        kr_half = (krr.astype(BF16), pltpu.roll(krr, ROPE, 1).astype(BF16))
        for j in range(H // 2):
            qr = _rot(q_ref[:, (H + j) * LANE:(H + j + 1) * LANE], c2, s2).astype(BF16)
            for h in (2 * j, 2 * j + 1):
                Q_ref[h, :, 0:LANE] = q_ref[:, h * LANE:(h + 1) * LANE].astype(BF16)
                Q_ref[h, :, LANE:] = qr
                K_ref[h, :, 0:LANE] = kv_ref[:, h * LANE:(h + 1) * LANE].astype(BF16)
                K_ref[h, :, LANE:] = kr_half[h % 2]
                V_ref[h] = kv_ref[:, (H + h) * LANE:(H + h + 1) * LANE].astype(BF16)

    tab = pl.BlockSpec((ts, LANE), lambda i: (i, 0))
    heads = lambda w: pl.BlockSpec((H, ts, w), lambda i: (0, i, 0))
    return pl.pallas_call(
        body, name="mla_pack", grid=(S // ts,),
        in_specs=[pl.BlockSpec((ts, cfg.QW), lambda i: (i, 0)), pl.BlockSpec((ts, cfg.KVW), lambda i: (i, 0)),
                  pl.BlockSpec((ts, LANE), lambda i: (i, kr_block)), tab, tab],
        out_specs=[heads(2 * LANE), heads(2 * LANE), heads(LANE)],
        out_shape=[jax.ShapeDtypeStruct((H, S, 2 * LANE), BF16), jax.ShapeDtypeStruct((H, S, 2 * LANE), BF16),
                   jax.ShapeDtypeStruct((H, S, LANE), BF16)],
        compiler_params=_params(("parallel",)),
    )(q, kv, kr, cos2, sin2)


def _mla_unpack(cfg, dQ, dK, dV, cos2, sin2):
    S, H = cfg.S, cfg.H
    ts = _pick(S, 256, 8)

    def body(dQ_ref, dK_ref, dV_ref, c_ref, s_ref, dq_ref, dkv_ref, dkr_ref):
        c2, s2 = c_ref[...], s_ref[...]
        lo = lax.broadcasted_iota(I32, (ts, LANE), 1) < ROPE
        tk = jnp.zeros((ts, LANE), F32)
        for h in range(H):
            dq_ref[:, h * LANE:(h + 1) * LANE] = dQ_ref[h, :, 0:LANE].astype(BF16)
            dkv_ref[:, h * LANE:(h + 1) * LANE] = dK_ref[h, :, 0:LANE].astype(BF16)
            dkv_ref[:, (H + h) * LANE:(H + h + 1) * LANE] = dV_ref[h].astype(BF16)
            own = lo if h % 2 == 0 else jnp.logical_not(lo)
            tk = tk + jnp.where(own, dK_ref[h, :, LANE:], 0.0)
        for j in range(H // 2):
            dr = dQ_ref[2 * j, :, LANE:] + dQ_ref[2 * j + 1, :, LANE:]
            dq_ref[:, (H + j) * LANE:(H + j + 1) * LANE] = _rot_back(dr, c2, s2).astype(BF16)
        dkr_rot = jnp.where(lo, tk + pltpu.roll(tk, ROPE, 1), 0.0)
        dkr_ref[...] = _rot_back(dkr_rot, c2, s2).astype(BF16)

    tab = pl.BlockSpec((ts, LANE), lambda i: (i, 0))
    return pl.pallas_call(
        body, name="mla_unpack", grid=(S // ts,),
        in_specs=[pl.BlockSpec((H, ts, 2 * LANE), lambda i: (0, i, 0)), pl.BlockSpec((H, ts, 2 * LANE), lambda i: (0, i, 0)),
                  pl.BlockSpec((H, ts, LANE), lambda i: (0, i, 0)), tab, tab],
        out_specs=[pl.BlockSpec((ts, cfg.QW), lambda i: (i, 0)), pl.BlockSpec((ts, cfg.KVW), lambda i: (i, 0)), tab],
        out_shape=[jax.ShapeDtypeStruct((S, cfg.QW), BF16), jax.ShapeDtypeStruct((S, cfg.KVW), BF16),
                   jax.ShapeDtypeStruct((S, LANE), BF16)],
        compiler_params=_params(("parallel",)),
    )(dQ, dK, dV, cos2, sin2)


_ATT_T = 256
_ATT_HB = 8
_ATT_SCALE = (NOPE + ROPE) ** -0.5


def _diag_mask(transposed=False):
    r = lax.broadcasted_iota(I32, (_ATT_T, _ATT_T), 0) // CHUNK
    c = lax.broadcasted_iota(I32, (_ATT_T, _ATT_T), 1) // CHUNK
    return r <= c if transposed else c <= r


def _row_form(col):
    return jnp.broadcast_to(col, (col.shape[0], LANE)).T[0:8, :]


def _attn_fwd(cfg, Q, K, V):
    S, H, T, HB = cfg.S, cfg.H, _ATT_T, min(cfg.H, _ATT_HB)

    def body(q_ref, k_ref, v_ref, o_ref, lse_ref, lse_t_ref):
        qi = pl.program_id(1)

        def head_step(b, kb, carry, mask):
            m, l, acc = carry
            ks = pl.multiple_of(kb * T, T)
            s = _dot(q_ref[b], k_ref[b, pl.ds(ks, T), :], NT) * _ATT_SCALE
            if mask is not None:
                s = jnp.where(mask, s, -1e30)
            m_new = jnp.maximum(m, jnp.max(s, axis=1, keepdims=True))
            p = jnp.exp(s - m_new)
            alpha = jnp.exp(m - m_new)
            l = alpha * l + jnp.sum(p, axis=1, keepdims=True)
            acc = alpha * acc + _dot(p.astype(BF16), v_ref[b, pl.ds(ks, T), :])
            return m_new, l, acc

        def step(kb, carry, mask=None):
            return tuple(head_step(b, kb, carry[b], mask) for b in range(HB))

        init = (jnp.full((T, 1), -1e30, F32), jnp.zeros((T, 1), F32), jnp.zeros((T, VH), F32))
        done = step(qi, lax.fori_loop(0, qi, step, (init,) * HB), _diag_mask())
        for b, (m, l, acc) in enumerate(done):
            o_ref[:, b * LANE:(b + 1) * LANE] = acc / l
            lse = m + jnp.log(l)
            lse_ref[:, b * LANE:(b + 1) * LANE] = jnp.broadcast_to(lse, (T, LANE))
            lse_t_ref[b] = _row_form(lse)

    return pl.pallas_call(
        body, name="attn_fwd", grid=(H // HB, S // T),
        in_specs=[pl.BlockSpec((HB, T, 2 * LANE), lambda h, i: (h, i, 0)), pl.BlockSpec((HB, S, 2 * LANE), lambda h, i: (h, 0, 0)),
                  pl.BlockSpec((HB, S, LANE), lambda h, i: (h, 0, 0))],
        out_specs=[pl.BlockSpec((T, HB * LANE), lambda h, i: (i, h)), pl.BlockSpec((T, HB * LANE), lambda h, i: (i, h)),
                   pl.BlockSpec((HB, 8, T), lambda h, i: (h, 0, i))],
        out_shape=[jax.ShapeDtypeStruct((S, H * LANE), F32), jax.ShapeDtypeStruct((S, H * LANE), F32),
                   jax.ShapeDtypeStruct((H, 8, S), F32)],
        compiler_params=_params(("parallel", "parallel")),
    )(Q, K, V)


def _attn_dq(cfg, Q, K, V, do, o, lse, after):
    S, H, T, HB = cfg.S, cfg.H, _ATT_T, min(cfg.H, _ATT_HB)

    def body(q_ref, k_ref, v_ref, do_ref, o_ref, lse_ref, after_ref, dq_ref, dl_t_ref):
        qi = pl.program_id(1)
        do = [do_ref[:, b * LANE:(b + 1) * LANE] for b in range(HB)]
        delta = [jnp.sum(do[b] * o_ref[:, b * LANE:(b + 1) * LANE], axis=1, keepdims=True) for b in range(HB)]
        dob = [d.astype(BF16) for d in do]

        def head_step(b, kb, dq, mask):
            ks = pl.multiple_of(kb * T, T)
            k = k_ref[b, pl.ds(ks, T), :]
            s = _dot(q_ref[b], k, NT) * _ATT_SCALE
            if mask is not None:
                s = jnp.where(mask, s, -1e30)
            p = jnp.exp(s - lse_ref[:, b * LANE:b * LANE + 1])
            dp = _dot(dob[b], v_ref[b, pl.ds(ks, T), :], NT)
            ds = p * (dp - delta[b]) * _ATT_SCALE
            return dq + _dot(ds.astype(BF16), k)

        def step(kb, dqs, mask=None):
            return tuple(head_step(b, kb, dqs[b], mask) for b in range(HB))

        dqs = step(qi, lax.fori_loop(0, qi, step, (jnp.zeros((T, 2 * LANE), F32),) * HB), _diag_mask())
        for b in range(HB):
            dq_ref[b] = dqs[b]
            dl_t_ref[b] = _row_form(delta[b])

    col = pl.BlockSpec((T, HB * LANE), lambda h, i: (i, h))
    return pl.pallas_call(
        body, name="attn_dq", grid=(H // HB, S // T),
        in_specs=[pl.BlockSpec((HB, T, 2 * LANE), lambda h, i: (h, i, 0)), pl.BlockSpec((HB, S, 2 * LANE), lambda h, i: (h, 0, 0)),
                  pl.BlockSpec((HB, S, LANE), lambda h, i: (h, 0, 0)), col, col, col, _ANY],
        out_specs=[pl.BlockSpec((HB, T, 2 * LANE), lambda h, i: (h, i, 0)), pl.BlockSpec((HB, 8, T), lambda h, i: (h, 0, i))],
        out_shape=[jax.ShapeDtypeStruct((H, S, 2 * LANE), F32), jax.ShapeDtypeStruct((H, 8, S), F32)],
        compiler_params=_params(("parallel", "parallel")),
    )(Q, K, V, do, o, lse, after)


def _attn_dkv(cfg, Q, K, V, do, lse_t, delta_t):
    S, H, T, HB = cfg.S, cfg.H, _ATT_T, min(cfg.H, _ATT_HB)
    nq = S // T

    def body(q_ref, k_ref, v_ref, do_ref, lse_ref, dl_ref, dk_ref, dv_ref):
        kb = pl.program_id(1)

        def head_step(b, qi, carry, mask):
            dk, dv = carry
            qs = pl.multiple_of(qi * T, T)
            q = q_ref[b, pl.ds(qs, T), :]
            dob = do_ref[pl.ds(qs, T), b * LANE:(b + 1) * LANE].astype(BF16)
            s = _dot(k_ref[b], q, NT) * _ATT_SCALE
            if mask is not None:
                s = jnp.where(mask, s, -1e30)
            p = jnp.exp(s - lse_ref[b, 0:1, pl.ds(qs, T)])
            dv = dv + _dot(p.astype(BF16), dob)
            dp = _dot(v_ref[b], dob, NT)
            ds = p * (dp - dl_ref[b, 0:1, pl.ds(qs, T)]) * _ATT_SCALE
            dk = dk + _dot(ds.astype(BF16), q)
            return dk, dv

        def step(qi, carry, mask=None):
            return tuple(head_step(b, qi, carry[b], mask) for b in range(HB))

        zero = (jnp.zeros((T, 2 * LANE), F32), jnp.zeros((T, VH), F32))
        done = lax.fori_loop(kb + 1, nq, step, step(kb, (zero,) * HB, _diag_mask(transposed=True)))
        for b, (dk, dv) in enumerate(done):
            dk_ref[b] = dk
            dv_ref[b] = dv

    row = pl.BlockSpec((HB, 8, S), lambda h, j: (h, 0, 0))
    return pl.pallas_call(
        body, name="attn_dkv", grid=(H // HB, S // T),
        in_specs=[pl.BlockSpec((HB, S, 2 * LANE), lambda h, j: (h, 0, 0)), pl.BlockSpec((HB, T, 2 * LANE), lambda h, j: (h, j, 0)),
                  pl.BlockSpec((HB, T, LANE), lambda h, j: (h, j, 0)), pl.BlockSpec((S, HB * LANE), lambda h, j: (0, h)), row, row],
        out_specs=[pl.BlockSpec((HB, T, 2 * LANE), lambda h, j: (h, j, 0)), pl.BlockSpec((HB, T, LANE), lambda h, j: (h, j, 0))],
        out_shape=[jax.ShapeDtypeStruct((H, S, 2 * LANE), F32), jax.ShapeDtypeStruct((H, S, LANE), F32)],
        compiler_params=_params(("parallel", "parallel")),
    )(Q, K, V, do, lse_t, delta_t)


def _expand_matrix(cfg):
    r = lax.broadcasted_iota(I32, (LANE, cfg.INNER), 0)
    c = lax.broadcasted_iota(I32, (LANE, cfg.INNER), 1)
    return (r == c // HP).astype(F32)


def _softplus(x):
    return jnp.maximum(x, 0.0) + jnp.log(1.0 + jnp.exp(-jnp.abs(x)))


def _ssd_prep(cfg, dt_raw, dt_bias_pad, a_log_pad, expand):
    HS = cfg.HS

    def fn(raw, bias, alog, E):
        heads = lax.broadcasted_iota(I32, raw.shape, 1) < HS
        dt = jnp.where(heads, _softplus(raw + bias), 0.0)
        a = dt * jnp.where(heads[0:1], -jnp.exp(alog), 0.0)
        return dt, a, _dot(dt, E, precision=HI), _dot(a, E, precision=HI)

    return _rowwise("ssd_prep", fn, [dt_raw], [dt_bias_pad, a_log_pad, expand],
                    [(LANE, F32), (LANE, F32), (cfg.INNER, F32), (cfg.INNER, F32)], [], _pick(cfg.S, 512, 8))


def _tril(T):
    return lax.broadcasted_iota(I32, (T, T), 0) >= lax.broadcasted_iota(I32, (T, T), 1)


def _ssd_fwd(cfg, xc, dt_exp, a_exp, a_small, dskip_exp):
    S, T, INNER, G, NPAIR = cfg.S, cfg.T, cfg.INNER, cfg.G, cfg.NPAIR
    NC = S // T

    def body(xc_ref, dte_ref, ae_ref, as_ref, dsk_ref, y_ref, hin_ref, ht_ref):
        @pl.when(pl.program_id(0) == 0)
        def _():
            ht_ref[...] = jnp.zeros_like(ht_ref)

        tril = _tril(T)
        tri = tril.astype(F32)
        acs_s = _dot(tri, as_ref[...], precision=HI)
        acs_e = _dot(tri, ae_ref[...], precision=HI)
        acs_t = acs_s.T
        lo = lax.broadcasted_iota(I32, (T, LANE), 1) < HP
        for g in range(G):
            Bb = xc_ref[:, INNER + g * NST:INNER + (g + 1) * NST].astype(BF16)
            Cb = xc_ref[:, INNER + (G + g) * NST:INNER + (G + g + 1) * NST].astype(BF16)
            Gm = _dot(Cb, Bb, NT)
            for j in range(g * NPAIR // G, (g + 1) * NPAIR // G):
                sl = slice(j * LANE, (j + 1) * LANE)
                Xp = xc_ref[:, sl]
                Xdt = Xp * dte_ref[:, sl]
                Xb = Xdt.astype(BF16)
                acs_p = acs_e[:, sl]
                last = acs_p[T - 1:T, :]
                Hin = ht_ref[j]
                hin_ref[0, j] = Hin
                yd = []
                for e in (0, 1):
                    h = 2 * j + e
                    Lm = jnp.exp(jnp.where(tril, acs_s[:, h:h + 1] - acs_t[h:h + 1, :], -1e30))
                    yd.append(_dot((Gm * Lm).astype(BF16), Xb))
                y_off = _dot(Cb, Hin.astype(BF16)) * jnp.exp(acs_p)
                y_ref[:, sl] = jnp.where(lo, yd[0], yd[1]) + y_off + Xp * dsk_ref[:, sl]
                st = _dot(Bb, (Xdt * jnp.exp(last - acs_p)).astype(BF16), TN)
                ht_ref[j] = jnp.exp(last) * Hin + st

    rows = lambda w: pl.BlockSpec((T, w), lambda c: (c, 0))
    return pl.pallas_call(
        body, name="ssd_fwd", grid=(NC,),
        in_specs=[rows(cfg.CONVCH), rows(INNER), rows(INNER), rows(LANE), pl.BlockSpec((1, INNER), lambda c: (0, 0))],
        out_specs=[rows(INNER), pl.BlockSpec((1, NPAIR, NST, LANE), lambda c: (c, 0, 0, 0))],
        out_shape=[jax.ShapeDtypeStruct((S, INNER), F32), jax.ShapeDtypeStruct((NC, NPAIR, NST, LANE), F32)],
        scratch_shapes=[pltpu.VMEM((NPAIR, NST, LANE), F32)],
        compiler_params=_params(("arbitrary",)),
    )(xc, dt_exp, a_exp, a_small, dskip_exp)


def _ssd_bwd(cfg, dy, xc, dt_exp, a_exp, a_small, dskip_exp, hin, dt_raw, dt_bias_pad, a_log_pad, expand):
    S, T, INNER, G, NPAIR, HS = cfg.S, cfg.T, cfg.INNER, cfg.G, cfg.NPAIR, cfg.HS
    NC = S // T

    def body(dy_ref, xc_ref, dte_ref, ae_ref, as_ref, dsk_ref, hin_ref, raw_ref, bias_ref, alog_ref, e_ref,
             dxc_ref, draw_ref, dbias_ref, dalog_ref, dskip_ref, dht_ref, cols_ref, rows_ref, dacs_ref, ddt_ref):
        first = pl.program_id(0) == 0

        @pl.when(first)
        def _():
            dht_ref[...] = jnp.zeros_like(dht_ref)

        tril = _tril(T)
        tri = tril.astype(F32)
        a_s = as_ref[...]
        acs_s = _dot(tri, a_s, precision=HI)
        acs_e = _dot(tri, ae_ref[...], precision=HI)
        acs_t = acs_s.T
        lo = lax.broadcasted_iota(I32, (T, LANE), 1) < HP
        last_row = lax.broadcasted_iota(I32, (T, LANE), 0) == T - 1
        cols_ref[...] = jnp.zeros_like(cols_ref)
        rows_ref[...] = jnp.zeros_like(rows_ref)
        dsk_parts = []
        for g in range(G):
            bsl = slice(INNER + g * NST, INNER + (g + 1) * NST)
            csl = slice(INNER + (G + g) * NST, INNER + (G + g + 1) * NST)
            Bb = xc_ref[:, bsl].astype(BF16)
            Cb = xc_ref[:, csl].astype(BF16)
            Gm = _dot(Cb, Bb, NT)
            dG = jnp.zeros((T, T), F32)
            dB = jnp.zeros((T, NST), F32)
            dC = jnp.zeros((T, NST), F32)
            for j in range(g * NPAIR // G, (g + 1) * NPAIR // G):
                sl = slice(j * LANE, (j + 1) * LANE)
                Xp = xc_ref[:, sl]
                dtp = dte_ref[:, sl]
                Xdt = Xp * dtp
                Xb = Xdt.astype(BF16)
                acs_p = acs_e[:, sl]
                last = acs_p[T - 1:T, :]
                e_p, dec, cd = jnp.exp(acs_p), jnp.exp(last - acs_p), jnp.exp(last)
                Hin = hin_ref[0, j]
                Hb = Hin.astype(BF16)
                dHn = dht_ref[j]
                dHb = dHn.astype(BF16)
                dYp = dy_ref[:, sl]
                z = _dot(Cb, Hb)
                dz = (dYp * e_p).astype(BF16)
                dacs_p = dYp * z * e_p
                dC = dC + _dot(dz, Hb, NT)
                dHin = _dot(Cb, dz, TN) + cd * dHn
                dlast = _colsum(dHn * Hin) * cd
                qv = _dot(Bb, dHb)
                dXdt = qv * dec
                ddec = qv * Xdt * dec
                dacs_p = dacs_p - ddec
                dlast = dlast + _colsum(ddec)
                dB = dB + _dot((Xdt * dec).astype(BF16), dHb, NT)
                for e in (0, 1):
                    h = 2 * j + e
                    Lm = jnp.exp(jnp.where(tril, acs_s[:, h:h + 1] - acs_t[h:h + 1, :], -1e30))
                    Mh = Gm * Lm
                    dYe = jnp.where(lo if e == 0 else jnp.logical_not(lo), dYp, 0.0).astype(BF16)
                    dM = _dot(dYe, Xb, NT)
                    dXdt = dXdt + _dot(Mh.astype(BF16), dYe, TN)
                    W = dM * Mh
                    cols_ref[:, h:h + 1] = jnp.sum(W, axis=1, keepdims=True)
                    rows_ref[h:h + 1, :] = _colsum(W)
                    dG = dG + dM * Lm
                dacs_ref[:, sl] = dacs_p + jnp.where(last_row, dlast, 0.0)
                ddt_ref[:, sl] = dXdt * Xp
                dxc_ref[:, sl] = dXdt * dtp + dYp * dsk_ref[:, sl]
                dsk_parts.append(_colsum(dYp * Xp))
                dht_ref[j] = dHin
            dGb = dG.astype(BF16)
            dxc_ref[:, bsl] = dB + _dot(dGb, Cb, TN)
            dxc_ref[:, csl] = dC + _dot(dGb, Bb)
        E = e_ref[...]
        dacs_s = cols_ref[...] - rows_ref[...].T + _dot(dacs_ref[...], E, NT, precision=HI)
        da = _dot(tri, dacs_s, TN, precision=HI)
        heads = lax.broadcasted_iota(I32, (1, LANE), 1) < HS
        A = jnp.where(heads, -jnp.exp(alog_ref[...]), 0.0)
        ddt = _dot(ddt_ref[...], E, NT, precision=HI) + da * A
        draw = jnp.where(heads, ddt * _sigmoid(raw_ref[...] + bias_ref[...]), 0.0)
        draw_ref[...] = draw
        dsk = _dot(jnp.broadcast_to(jnp.concatenate(dsk_parts, axis=1), (8, INNER)), E, NT, precision=HI)[0:1]
        for ref, val in ((dbias_ref, _colsum(draw)), (dalog_ref, _colsum(da * a_s)), (dskip_ref, dsk)):
            @pl.when(first)
            def _():
                ref[...] = val

            @pl.when(jnp.logical_not(first))
            def _():
                ref[...] += val

    dt_raw, _, raw_block = _window(dt_raw)
    rows = lambda w, b=0: pl.BlockSpec((T, w), lambda c: (NC - 1 - c, b))
    vec = lambda w: pl.BlockSpec((1, w), lambda c: (0, 0))
    return pl.pallas_call(
        body, name="ssd_bwd", grid=(NC,),
        in_specs=[rows(INNER), rows(cfg.CONVCH), rows(INNER), rows(INNER), rows(LANE), vec(INNER),
                  pl.BlockSpec((1, NPAIR, NST, LANE), lambda c: (NC - 1 - c, 0, 0, 0)), rows(LANE, raw_block), vec(LANE), vec(LANE),
                  pl.BlockSpec((LANE, INNER), lambda c: (0, 0))],
        out_specs=[rows(cfg.CONVCH), rows(LANE), vec(LANE), vec(LANE), vec(LANE)],
        out_shape=[jax.ShapeDtypeStruct((S, cfg.CONVCH), F32), jax.ShapeDtypeStruct((S, LANE), F32)]
        + [jax.ShapeDtypeStruct((1, LANE), F32)] * 3,
        scratch_shapes=[pltpu.VMEM((NPAIR, NST, LANE), F32), pltpu.VMEM((T, LANE), F32), pltpu.VMEM((LANE, T), F32),
                        pltpu.VMEM((T, INNER), F32), pltpu.VMEM((T, INNER), F32)],
        compiler_params=_params(("arbitrary",)),
    )(dy, xc, dt_exp, a_exp, a_small, dskip_exp, hin, dt_raw, dt_bias_pad, a_log_pad, expand)


def _ssd_post(cfg, y, z, norm_g):
    W = cfg.INNER // cfg.G

    def fn(y, z, g):
        yz = y * z * _sigmoid(z)
        return jnp.concatenate([yz[:, i * W:(i + 1) * W] * _rs(yz[:, i * W:(i + 1) * W]) for i in range(cfg.G)], axis=1) * g

    return _rowwise("ssd_post", fn, [y, z], [norm_g], [(cfg.INNER, BF16)], [], _pick(cfg.S, 256, 8))[0]


def _ssd_post_bwd(cfg, db, y, z, norm_g):
    W = cfg.INNER // cfg.G

    def fn(db, y, z, g):
        sg = _sigmoid(z)
        yz = y * z * sg
        dn = db * g
        dyz, nh = [], []
        for i in range(cfg.G):
            seg = yz[:, i * W:(i + 1) * W]
            r = _rs(seg)
            nh.append(seg * r)
            dyz.append(_rms_back(nh[-1], r, dn[:, i * W:(i + 1) * W]))
        dyz = jnp.concatenate(dyz, axis=1)
        return dyz * z * sg, dyz * y * sg * (1.0 + z * (1.0 - sg)), _colsum(db * jnp.concatenate(nh, axis=1))

    return _rowwise("ssd_post_bwd", fn, [db, y, z], [norm_g], [(cfg.INNER, F32), (cfg.INNER, F32)], [(1, cfg.INNER)],
                    _pick(cfg.S, 256, 8))


def _local_grads(cfg, x, tgt, W, sp, mla_weights=None, out_weight=None, ffn_weights=None, down_weight=None,
                 ffn_grads_ready=None, early_grads_ready=None, in_grad_ready=None):
    S, D, H, INNER = cfg.S, cfg.D, cfg.H, cfg.INNER
    ts = _pick(S, 256, 8)
    tc = 256

    xn = _rowwise("rms_pre", lambda x, g: x * _rs(x) * g, [x], [sp["mix_pre_g"]], [(D, BF16)], [], ts)[0]
    u = _matmul("mm_in", xn, W["w_in"], "nt", F32)
    c_q, c_kv, kr, z, xbc, dt_raw = [(u, cfg.window(n)) for n in ("c_q", "c_kv", "kr", "z", "xbc", "dt")]

    if mla_weights is not None:
        sp = dict(sp, q_norm_g=sp["q_norm_g"] + mla_weights.pass_on(u)[0, 0])
    cqn = _rowwise("rms_q", lambda x, g: x * _rs(x) * g, [c_q], [sp["q_norm_g"]], [(cfg.QL, BF16)], [], ts)[0]
    ckvn = _rowwise("rms_kv", lambda x, g: x * _rs(x) * g, [c_kv], [sp["kv_norm_g"]], [(cfg.KVL, BF16)], [], ts)[0]
    if mla_weights is not None:
        W = dict(W, **mla_weights.arrived(ckvn))
    q = _matmul("mm_uq", cqn, W["w_uq"], "nn", F32)
    kv = _matmul("mm_ukv", ckvn, W["w_ukv"], "nn", F32)
    cos2, sin2 = _rope_tables(S)
    Qh, Kh, Vh = _mla_pack(cfg, q, kv, kr, cos2, sin2)
    a_out, lse, lse_t = _attn_fwd(cfg, Qh, Kh, Vh)
    if out_weight is not None:
        sp = dict(sp, ssm_conv_b=sp["ssm_conv_b"] + out_weight.pass_on(a_out)[0, 0])

    pad = lambda v: jnp.pad(v, ((0, 0), (0, LANE - v.shape[1])))
    expand = _expand_matrix(cfg)
    dt_bias_pad, a_log_pad = pad(sp["dt_bias"]), pad(sp["a_log"])
    dskip_exp = jnp.repeat(sp["d_skip"], HP, axis=1)
    xc = _colwise("ssm_act", _ssm_act, [xbc], [sp["ssm_conv_w"], sp["ssm_conv_b"]], [F32], [], tc)[0]
    dt_s, a_s, dt_exp, a_exp = _ssd_prep(cfg, dt_raw, dt_bias_pad, a_log_pad, expand)
    y_ssd, hin = _ssd_fwd(cfg, xc, dt_exp, a_exp, a_s, dskip_exp)
    b_out = _ssd_post(cfg, y_ssd, z, sp["ssm_norm_g"])

    ab_out = jnp.concatenate([a_out.astype(BF16), b_out], axis=1)
    if out_weight is not None:
        W = dict(W, **out_weight.arrived(ab_out))
    if ffn_weights is not None:
        sp = dict(sp, mix_post_g=sp["mix_post_g"] + ffn_weights.pass_on(ab_out)[0, 0])
    mix = _matmul("mm_out", ab_out, W["w_out"], "nn", F32)

    def mid(x, mix, g_mp, g_fp):
        x1 = x + mix * _rs(mix) * g_mp
        return x1, x1 * _rs(x1) * g_fp

    x1, h2 = _rowwise("fwd_mid", mid, [x, mix], [sp["mix_post_g"], sp["ffn_pre_g"]], [(D, F32), (D, BF16)], [], ts)
    if ffn_weights is not None:
        W = dict(W, **ffn_weights.arrived(h2))
    gate_pre = _matmul("mm_gate", h2, W["w_gate"], "nn", F32, chips=True)
    if down_weight is not None:
        sp = dict(sp, ffn_conv_b=sp["ffn_conv_b"] + down_weight.pass_on(gate_pre)[0, 0])
    up = _matmul("mm_up", h2, W["w_up"], "nn", F32, chips=True)
    act = _colwise("ffn_act", _ffn_act, [gate_pre, up], [sp["ffn_conv_w"], sp["ffn_conv_b"]], [BF16], [], tc)[0]
    if down_weight is not None:
        W = dict(W, **down_weight.arrived(act))
    f = _matmul("mm_down", act, W["w_down"], "nn", F32)

    def final(x1, f, t, g):
        r = _rs(f)
        fh = f * r
        err = x1 + fh * g - t
        loss = 0.5 * jnp.sum(jnp.mean(err * err, axis=-1, keepdims=True), axis=0, keepdims=True)
        dy = err * (1.0 / D)
        return dy, _rms_back(fh, r, dy * g), _colsum(dy * fh), loss

    dy, df, g_ffn_post, loss = _rowwise("final", final, [x1, f, tgt], [sp["ffn_post_g"]], [(D, F32), (D, BF16)],
                                        [(1, D), (1, LANE)], ts)
    gW = {}
    dact = _matmul("mm_down_dx", df, W["w_down"], "nt", F32)
    gW["w_down"] = _matmul("mm_down_dw", act, df, "tn", BF16)
    dgate, dup, g_ffn_conv_w, g_ffn_conv_b = _colwise(
        "ffn_act_bwd", _ffn_act_back, [dact, gate_pre, up], [sp["ffn_conv_w"], sp["ffn_conv_b"]], [BF16, BF16], [FFN_K, 1], tc)
    gW["w_gate"] = _matmul("mm_gate_dw", h2, dgate, "tn", BF16, chips=True)
    gW["w_up"] = _matmul("mm_up_dw", h2, dup, "tn", BF16, chips=True)
    if ffn_grads_ready is not None:
        sp = dict(sp, ffn_pre_g=sp["ffn_pre_g"] + ffn_grads_ready({n: gW[n] for n in ("w_down", "w_gate", "w_up")})[0, 0])
    dh2 = _matmul("mm_gu_dx", dgate, W["w_gate"], "nt", F32, dup, W["w_up"], chips=True)

    def mid_back(dy, dh2, x1, mix, g_mp, g_fp):
        r2 = _rs(x1)
        xh = x1 * r2
        dx1 = dy + _rms_back(xh, r2, dh2 * g_fp)
        r1 = _rs(mix)
        mh = mix * r1
        return dx1, _rms_back(mh, r1, dx1 * g_mp), _colsum(dh2 * xh), _colsum(dx1 * mh)

    dx1, dmix, g_ffn_pre, g_mix_post = _rowwise("bwd_mid", mid_back, [dy, dh2, x1, mix], [sp["mix_post_g"], sp["ffn_pre_g"]],
                                                [(D, F32), (D, BF16)], [(1, D), (1, D)], ts)
    dab_out = _matmul("mm_out_dx", dmix, W["w_out"], "nt", F32)
    db_out = (dab_out, (INNER, cfg.MLAW // INNER))
    gW["w_out"] = _matmul("mm_out_dw", ab_out, dmix, "tn", BF16)
    early_token = jnp.zeros((8, LANE), F32)
    if early_grads_ready is not None:
        early_token = early_grads_ready({n: gW[n] for n in ("w_down", "w_gate", "w_up", "w_out")})
        sp = dict(sp, ssm_norm_g=sp["ssm_norm_g"] + early_token[0, 0])

    dy_ssd, dz, g_ssm_norm = _ssd_post_bwd(cfg, db_out, y_ssd, z, sp["ssm_norm_g"])
    dxc, ddt_raw, g_dt_bias, g_a_log, g_d_skip = _ssd_bwd(cfg, dy_ssd, xc, dt_exp, a_exp, a_s, dskip_exp, hin, dt_raw,
                                                          dt_bias_pad, a_log_pad, expand)
    dxbc, g_ssm_conv_w, g_ssm_conv_b = _colwise("ssm_act_bwd", _ssm_act_back, [dxc, xbc], [sp["ssm_conv_w"], sp["ssm_conv_b"]],
                                                [BF16], [SSM_K, 1], tc)

    dQ, delta_t = _attn_dq(cfg, Qh, Kh, Vh, dab_out, a_out, lse, early_token)
    dK, dV = _attn_dkv(cfg, Qh, Kh, Vh, dab_out, lse_t, delta_t)
    dq, dkv, dkr = _mla_unpack(cfg, dQ, dK, dV, cos2, sin2)
    dcqn = _matmul("mm_uq_dx", dq, W["w_uq"], "nt", F32)
    dckvn = _matmul("mm_ukv_dx", dkv, W["w_ukv"], "nt", F32)
    gW["w_uq"] = _matmul("mm_uq_dw", cqn, dq, "tn", BF16)
    gW["w_ukv"] = _matmul("mm_ukv_dw", ckvn, dkv, "tn", BF16)

    def rms_back(x, dy, g):
        r = _rs(x)
        xh = x * r
        return _rms_back(xh, r, dy * g), _colsum(dy * xh)

    dc_q, g_q_norm = _rowwise("rms_q_bwd", rms_back, [c_q, dcqn], [sp["q_norm_g"]], [(cfg.QL, BF16)], [(1, cfg.QL)], ts)
    dc_kv, g_kv_norm = _rowwise("rms_kv_bwd", rms_back, [c_kv, dckvn], [sp["kv_norm_g"]], [(cfg.KVL, BF16)], [(1, cfg.KVL)], ts)

    du = dict(c_q=dc_q, c_kv=dc_kv, kr=dkr, z=dz.astype(BF16), xbc=dxbc, dt=ddt_raw.astype(BF16))
    du = jnp.concatenate([du[n] for n in sorted(du, key=lambda n: cfg.seg[n][0])], axis=1)
    assert du.shape[1] == cfg.EXT, "the layout of u has gaps"
    gW["w_in"] = _matmul("mm_in_dw", du, xn, "tn", BF16)
    if in_grad_ready is not None:
        token = in_grad_ready({n: gW[n] for n in ("w_in", "w_uq", "w_ukv")})
        sp = dict(sp, mix_pre_g=sp["mix_pre_g"] + token[0, 0])
    dxn = _matmul("mm_in_dx", du, W["w_in"], "nn", F32)

    def first_back(dx1, dxn, x, g):
        r = _rs(x)
        xh = x * r
        return dx1 + _rms_back(xh, r, dxn * g), _colsum(dxn * xh)

    grad_x, g_mix_pre = _rowwise("bwd_first", first_back, [dx1, dxn, x], [sp["mix_pre_g"]], [(D, F32)], [(1, D)], ts)

    gs = dict(mix_pre_g=g_mix_pre, q_norm_g=g_q_norm, kv_norm_g=g_kv_norm, ssm_conv_w=g_ssm_conv_w, ssm_conv_b=g_ssm_conv_b,
              dt_bias=g_dt_bias[:, :cfg.HS], a_log=g_a_log[:, :cfg.HS], d_skip=g_d_skip[:, :cfg.HS], ssm_norm_g=g_ssm_norm,
              mix_post_g=g_mix_post, ffn_pre_g=g_ffn_pre, ffn_conv_w=g_ffn_conv_w, ffn_conv_b=g_ffn_conv_b,
              ffn_post_g=g_ffn_post)
    return loss, grad_x, gW, gs


def _to_kernel_layout(cfg, name, w):
    if name == "w_in":
        parts, at = [], 0
        for off, width, n_off, n_width in sorted(cfg.seg.values()):
            parts += [jnp.zeros((off - at, w.shape[1]), w.dtype), w[n_off:n_off + n_width],
                      jnp.zeros((width - n_width, w.shape[1]), w.dtype)]
            at = off + width
        parts.append(jnp.zeros((cfg.EXT - at, w.shape[1]), w.dtype))
        return jnp.concatenate([p for p in parts if p.shape[0]], axis=0)
    if name in ("w_uq", "w_ukv"):
        per = NOPE + (ROPE if name == "w_uq" else VH)
        return jnp.concatenate([w[:, h * per:h * per + NOPE] for h in range(cfg.H)]
                               + [w[:, h * per + NOPE:(h + 1) * per] for h in range(cfg.H)], axis=1)
    return w


def _from_kernel_layout(cfg, name, g):
    if name == "w_in":
        return jnp.concatenate([g[off:off + n_width] for off, _, _, n_width in sorted(cfg.seg.values(), key=lambda s: s[2])], axis=0)
    if name in ("w_uq", "w_ukv"):
        second = ROPE if name == "w_uq" else VH
        base = cfg.H * NOPE
        parts = []
        for h in range(cfg.H):
            parts += [g[:, h * NOPE:(h + 1) * NOPE], g[:, base + h * second:base + (h + 1) * second]]
        return jnp.concatenate(parts, axis=1)
    return g


def _cols_to_chips(w):
    r, c = w.shape
    return w.reshape(r, N_CHIPS, c // N_CHIPS).transpose(1, 0, 2)


def _chips_to_cols(g):
    k, r, cs = g.shape
    return g.transpose(1, 0, 2).reshape(r, k * cs)


_CHIP_MAJOR = ("w_gate", "w_up")
_RELAYOUT = ("w_uq", "w_ukv")
_LAYOUT_ROWS = 256


def _w_in_layout(cfg, wg):
    _, rs, d = wg.shape
    tc = _pick(d, _LAYOUT_ROWS, LANE)

    def body(w_ref, o_ref):
        o_ref[...] = _to_kernel_layout(cfg, "w_in", jnp.concatenate([w_ref[k] for k in range(N_CHIPS)], axis=0))

    return pl.pallas_call(
        body, name="layout_w_in", grid=(d // tc,),
        in_specs=[pl.BlockSpec((N_CHIPS, rs, tc), lambda j: (0, 0, j))], out_specs=pl.BlockSpec((cfg.EXT, tc), lambda j: (0, j)),
        out_shape=jax.ShapeDtypeStruct((cfg.EXT, d), wg.dtype), compiler_params=_params(("parallel",)),
    )(wg)


def _w_in_grad_to_chips(cfg, g):
    _, d = g.shape
    rs = cfg.IN_COLS // N_CHIPS
    tc = _pick(d, _LAYOUT_ROWS, LANE)

    def body(g_ref, o_ref):
        nat = _from_kernel_layout(cfg, "w_in", g_ref[...])
        for k in range(N_CHIPS):
            o_ref[k] = nat[k * rs:(k + 1) * rs]

    return pl.pallas_call(
        body, name="layout_grad_w_in", grid=(d // tc,),
        in_specs=[pl.BlockSpec((cfg.EXT, tc), lambda j: (0, j))], out_specs=pl.BlockSpec((N_CHIPS, rs, tc), lambda j: (0, 0, j)),
        out_shape=jax.ShapeDtypeStruct((N_CHIPS, rs, d), g.dtype), compiler_params=_params(("parallel",)),
    )(g)


def _gathered_to_kernel(cfg, name, wg):
    if name in _CHIP_MAJOR:
        return wg
    if name == "w_in":
        return _w_in_layout(cfg, wg)
    if name not in _RELAYOUT:
        return wg.reshape(wg.shape[0] * wg.shape[1], wg.shape[2])
    _, rows, cs = wg.shape
    tr = _pick(rows, _LAYOUT_ROWS, 16)

    def body(w_ref, o_ref):
        o_ref[...] = _to_kernel_layout(cfg, name, jnp.concatenate([w_ref[k] for k in range(N_CHIPS)], axis=1))

    wide = jax.eval_shape(lambda w: _to_kernel_layout(cfg, name, w), jax.ShapeDtypeStruct((rows, N_CHIPS * cs), wg.dtype)).shape[1]
    return pl.pallas_call(
        body, name="layout_" + name, grid=(rows // tr,),
        in_specs=[pl.BlockSpec((N_CHIPS, tr, cs), lambda i: (0, i, 0))], out_specs=pl.BlockSpec((tr, wide), lambda i: (i, 0)),
        out_shape=jax.ShapeDtypeStruct((rows, wide), wg.dtype), compiler_params=_params(("parallel",)),
    )(wg)


def _grad_to_chips(cfg, name, g):
    if name in _CHIP_MAJOR:
        return g
    if name == "w_in":
        return _w_in_grad_to_chips(cfg, g)
    if name not in _RELAYOUT:
        return g.reshape(N_CHIPS, g.shape[0] // N_CHIPS, g.shape[1])
    rows, wide = g.shape
    tr = _pick(rows, _LAYOUT_ROWS, 16)
    cs = jax.eval_shape(lambda v: _from_kernel_layout(cfg, name, v), g).shape[1] // N_CHIPS

    def body(g_ref, o_ref):
        nat = _from_kernel_layout(cfg, name, g_ref[...])
        for k in range(N_CHIPS):
            o_ref[k] = nat[:, k * cs:(k + 1) * cs]

    return pl.pallas_call(
        body, name="layout_grad_" + name, grid=(rows // tr,),
        in_specs=[pl.BlockSpec((tr, wide), lambda i: (i, 0))], out_specs=pl.BlockSpec((N_CHIPS, tr, cs), lambda i: (0, i, 0)),
        out_shape=jax.ShapeDtypeStruct((N_CHIPS, rows, cs), g.dtype), compiler_params=_params(("parallel",)),
    )(g)


def _me():
    return lax.axis_index("x"), lax.axis_index("y"), lax.axis_index("c")


def _other_chips(x, y):
    return [(1 - x, y), (x, 1 - y), (1 - x, 1 - y)]


_ANY = pl.BlockSpec(memory_space=pl.ANY)


BLOCK_ELEMS = 1 << 19
BLOCK_ELEMS_FEW = 1 << 20


def _row_block(rows, cols, mult, elems=BLOCK_ELEMS):
    return _pick(rows, max(mult, elems // cols // mult * mult), mult)


def _scalar(v):
    return v.astype(I32).reshape(1)


def _blocks2d(r, c, mult, elems=BLOCK_ELEMS):
    if r % mult == 0:
        tr = _row_block(r, c, mult, elems)
        return (tr, c), r // tr, lambda i: (i, 0)
    tc = _pick(c, max(LANE, elems // r // LANE * LANE), LANE)
    return (r, tc), c // tc, lambda i: (0, i)


def _by_rows(rows):
    return rows % 32 == 0


def _half_shape(rows, cols):
    return (rows // 2, cols) if _by_rows(rows) else (rows, cols // 2)


def _half_blocks(rows, cols, mult, elems=BLOCK_ELEMS):
    hr, hc = _half_shape(rows, cols)
    block, n, part = _blocks2d(hr, hc, mult, elems)
    assert (hr % mult == 0) == _by_rows(rows), (rows, cols, mult)
    full = (lambda h, i: (h * n + i, 0)) if _by_rows(rows) else (lambda h, i: (0, h * n + i))
    return block, n, full, part


def _half(ref, k, half):
    hr, hc = _half_shape(ref.shape[1], ref.shape[2])
    if _by_rows(ref.shape[1]):
        return ref.at[k, pl.ds(pl.multiple_of(half * hr, 16), hr), :]
    return ref.at[k, :, pl.ds(pl.multiple_of(half * hc, LANE), hc)]


def _shard_blocks(w, br, bc):
    if w.shape[0] == 1:
        def write(ref, v):
            ref[...] = v
        return (lambda f: pl.BlockSpec((None, br, bc), lambda *a: (0, *f(*a)))), (lambda ref: ref[...]), write
    assert w.shape[1] == 1 and br == w.shape[0], w.shape

    def write_rows(ref, v):
        ref[:, 0, :] = v
    return (lambda f: pl.BlockSpec((br, 1, bc), lambda *a: (0, 0, f(*a)[1]))), (lambda ref: ref[:, 0, :]), write_rows


def _stage_shard(name, w, chip, after=None):
    rs, cs = w.shape[0] * w.shape[1], w.shape[2]
    (br, bc), n, idx = _blocks2d(rs, cs, 16, BLOCK_ELEMS_FEW)
    spec, get, _ = _shard_blocks(w, br, bc)

    def body(chip_ref, w_ref, *refs):
        refs[-1][...] = get(w_ref).astype(BF16)

    return pl.pallas_call(
        body, name="stage_" + name,
        grid_spec=pltpu.PrefetchScalarGridSpec(
            num_scalar_prefetch=1, grid=(n,),
            in_specs=[spec(lambda i, chip_ref: idx(i))] + ([] if after is None else [_ANY]),
            out_specs=pl.BlockSpec((None, br, bc), lambda i, chip_ref: (chip_ref[0], *idx(i)))),
        out_shape=jax.ShapeDtypeStruct((N_CHIPS, rs, cs), BF16),
        compiler_params=_params(("parallel",)),
    )(_scalar(chip), w, *([] if after is None else [after]))


_HBM = pl.BlockSpec(memory_space=pltpu.HBM)
_SEM = pl.BlockSpec(memory_space=pltpu.SEMAPHORE)
_EFFECT = pltpu.SideEffectType.DATAFLOW_SIDE_EFFECTING


def _split_start(name, bufs, n_copies, copies, after):
    n = len(bufs)

    def body(*refs):
        for cp in copies(refs[:n], refs[n + 1], refs[n + 2]):
            cp.start()
        refs[-1][...] = jnp.zeros_like(refs[-1])

    res = pl.pallas_call(
        body, name=name,
        out_shape=(pltpu.SemaphoreType.DMA((n_copies,)), pltpu.SemaphoreType.DMA((n_copies,)),
                   *[pltpu.HBM(b.shape, b.dtype) for b in bufs], jax.ShapeDtypeStruct((8, LANE), F32)),
        in_specs=[_HBM] * n + [_ANY], out_specs=(_SEM, _SEM, *[_HBM] * n, pl.BlockSpec(memory_space=pltpu.VMEM)),
        input_output_aliases={i: 2 + i for i in range(n)},
        compiler_params=pltpu.CompilerParams(has_side_effects=_EFFECT),
    )(*[pltpu.with_memory_space_constraint(b, pltpu.HBM) for b in bufs], after)
    return res[0], res[1], list(res[2:2 + n]), res[-1]


def _split_wait(name, send_sems, recv_sems, bufs, after, copies):
    n = len(bufs)

    def body(*refs):
        for cp in copies(refs[:n], refs[n], refs[n + 1]):
            cp.wait_send()
            cp.wait_recv()

    return list(pl.pallas_call(
        body, name=name, out_shape=[pltpu.HBM(b.shape, b.dtype) for b in bufs],
        in_specs=[_HBM] * n + [_SEM, _SEM, _ANY], out_specs=[_HBM] * n,
        input_output_aliases={i: i for i in range(n)},
        compiler_params=pltpu.CompilerParams(has_side_effects=_EFFECT),
    )(*bufs, send_sems, recv_sems, after))


def _gather_to_chips(bufs, send_sems, recv_sems):
    x, y, c = _me()
    return [pltpu.make_async_remote_copy(src_ref=_half(b, 2 * x + y, c), dst_ref=_half(b, 2 * x + y, c),
                                         send_sem=send_sems.at[3 * w + j], recv_sem=recv_sems.at[3 * w + j],
                                         device_id=(cx, cy, c), device_id_type=MESH_ID)
            for w, b in enumerate(bufs) for j, (cx, cy) in enumerate(_other_chips(x, y))]


def _gather_to_sibling(bufs, send_sems, recv_sems):
    x, y, c = _me()
    return [pltpu.make_async_remote_copy(src_ref=_half(b, 2 * cx + cy, c), dst_ref=_half(b, 2 * cx + cy, c),
                                         send_sem=send_sems.at[3 * w + j], recv_sem=recv_sems.at[3 * w + j],
                                         device_id=(x, y, 1 - c), device_id_type=MESH_ID)
            for w, b in enumerate(bufs) for j, (cx, cy) in enumerate(_other_chips(x, y))]


def _pair_exchange(name, grads):
    n = len(grads)

    def body(*refs):
        ins, outs, send_sems, recv_sems = refs[:n], refs[n:2 * n], refs[2 * n], refs[2 * n + 1]
        x, y, c = _me()
        cps = []
        for w, (g_ref, o_ref) in enumerate(zip(ins, outs)):
            cps.append(pltpu.make_async_remote_copy(src_ref=_half(g_ref, slice(None), 1 - c), dst_ref=o_ref,
                                                    send_sem=send_sems.at[w], recv_sem=recv_sems.at[w],
                                                    device_id=(x, y, 1 - c), device_id_type=MESH_ID))
            cps[-1].start()
        for cp in cps:
            cp.wait()

    return pl.pallas_call(
        body, name="pair_exchange_" + name, in_specs=[_ANY] * n, out_specs=[_ANY] * n,
        out_shape=[jax.ShapeDtypeStruct((g.shape[0], *_half_shape(g.shape[1], g.shape[2])), g.dtype) for g in grads],
        scratch_shapes=[pltpu.SemaphoreType.DMA((n,)), pltpu.SemaphoreType.DMA((n,))],
    )(*grads)


def _pair_copies(grads, lands, send_sems, recv_sems):
    x, y, c = _me()
    return [pltpu.make_async_remote_copy(src_ref=_half(g_ref, slice(None), 1 - c), dst_ref=l_ref, send_sem=send_sems.at[w],
                                         recv_sem=recv_sems.at[w], device_id=(x, y, 1 - c), device_id_type=MESH_ID)
            for w, (g_ref, l_ref) in enumerate(zip(grads, lands))]


def _pair_exchange_start(name, grads):
    n = len(grads)
    lands = [lax.empty((g.shape[0], *_half_shape(g.shape[1], g.shape[2])), g.dtype) for g in grads]
    send_sems, recv_sems, bufs, token = _split_start(
        "pair_exchange_start_" + name, [*grads, *lands], n, lambda refs, ss, rs: _pair_copies(refs[:n], refs[n:], ss, rs),
        jnp.zeros((8, LANE), F32))
    return (send_sems, recv_sems, bufs), token


def _pair_exchange_wait(name, state, after):
    send_sems, recv_sems, bufs = state
    n = len(bufs) // 2
    bufs = _split_wait("pair_exchange_wait_" + name, send_sems, recv_sems, bufs, after,
                       lambda refs, ss, rs: _pair_copies(refs[:n], refs[n:], ss, rs))
    return bufs[:n], bufs[n:]


def _pair_sum(name, g, theirs, c):
    (br, bc), nb, full, part = _half_blocks(g.shape[1], g.shape[2], 16, 2 * BLOCK_ELEMS_FEW)

    def body(c_ref, a_ref, b_ref, o_ref):
        o_ref[...] = (a_ref[...].astype(F32) + b_ref[...].astype(F32)).astype(o_ref.dtype)

    return pl.pallas_call(
        body, name="pair_sum_" + name,
        grid_spec=pltpu.PrefetchScalarGridSpec(
            num_scalar_prefetch=1, grid=(N_CHIPS, nb),
            in_specs=[pl.BlockSpec((None, br, bc), lambda k, i, c_ref: (k, *full(c_ref[0], i))),
                      pl.BlockSpec((None, br, bc), lambda k, i, c_ref: (k, *part(i)))],
            out_specs=pl.BlockSpec((None, br, bc), lambda k, i, c_ref: (k, *part(i)))),
        out_shape=jax.ShapeDtypeStruct(theirs.shape, BF16),
        compiler_params=_params(("parallel", "parallel")),
    )(_scalar(c), g, theirs)


def _chip_copies(srcs, lands, send_sems, recv_sems):
    x, y, c = _me()
    return [pltpu.make_async_remote_copy(src_ref=s_ref.at[2 * cx + cy], dst_ref=l_ref.at[j], send_sem=send_sems.at[3 * w + j],
                                         recv_sem=recv_sems.at[3 * w + j], device_id=(cx, cy, c), device_id_type=MESH_ID)
            for w, (s_ref, l_ref) in enumerate(zip(srcs, lands)) for j, (cx, cy) in enumerate(_other_chips(x, y))]


def _chip_exchange_start(name, sums):
    n = len(sums)
    lands = [lax.empty((3,) + s.shape[1:], s.dtype) for s in sums]
    send_sems, recv_sems, bufs, token = _split_start(
        "chip_exchange_start_" + name, [*sums, *lands], 3 * n, lambda refs, ss, rs: _chip_copies(refs[:n], refs[n:], ss, rs),
        jnp.zeros((8, LANE), F32))
    return send_sems, recv_sems, bufs[:n], bufs[n:], token


def _chip_exchange_wait(name, send_sems, recv_sems, sums, lands, after):
    n = len(sums)
    bufs = _split_wait("chip_exchange_wait_" + name, send_sems, recv_sems, [*sums, *lands], after,
                       lambda refs, ss, rs: _chip_copies(refs[:n], refs[n:], ss, rs))
    return bufs[:n], bufs[n:]


def _chip_sum(name, sums, theirs, chip):
    _, h, cs = sums.shape
    (br, bc), nb, idx = _blocks2d(h, cs, 16, BLOCK_ELEMS_FEW)

    def body(chip_ref, s_ref, t_ref, o_ref):
        acc = s_ref[...].astype(F32)
        for k in range(3):
            acc = acc + t_ref[k].astype(F32)
        o_ref[...] = acc

    return pl.pallas_call(
        body, name="chip_sum_" + name,
        grid_spec=pltpu.PrefetchScalarGridSpec(
            num_scalar_prefetch=1, grid=(nb,),
            in_specs=[pl.BlockSpec((None, br, bc), lambda i, chip_ref: (chip_ref[0], *idx(i))),
                      pl.BlockSpec((3, br, bc), lambda i, chip_ref: (0, *idx(i)))],
            out_specs=pl.BlockSpec((br, bc), lambda i, chip_ref: idx(i))),
        out_shape=jax.ShapeDtypeStruct((h, cs), F32),
        compiler_params=_params(("parallel",)),
    )(_scalar(chip), sums, theirs)


def _sibling_exchange(name, halves):
    n = len(halves)

    def body(*refs):
        ins, outs, send_sems, recv_sems = refs[:n], refs[n:2 * n], refs[2 * n], refs[2 * n + 1]
        x, y, c = _me()
        cps = []
        for w, (h_ref, o_ref) in enumerate(zip(ins, outs)):
            cps.append(pltpu.make_async_remote_copy(src_ref=h_ref, dst_ref=o_ref, send_sem=send_sems.at[w], recv_sem=recv_sems.at[w],
                                                    device_id=(x, y, 1 - c), device_id_type=MESH_ID))
            cps[-1].start()
        for cp in cps:
            cp.wait()

    return pl.pallas_call(
        body, name="sibling_exchange_" + name, in_specs=[_ANY] * n, out_specs=[_ANY] * n,
        out_shape=[jax.ShapeDtypeStruct(h.shape, h.dtype) for h in halves],
        scratch_shapes=[pltpu.SemaphoreType.DMA((n,)), pltpu.SemaphoreType.DMA((n,))],
    )(*halves)


def _allreduce_small(name, vec, after):
    def body(v_ref, after_ref, o_ref, buf_ref, send_sems, recv_sems):
        x, y, c = _me()
        me = 4 * x + 2 * y + c
        cps = []
        for p in range(1, 8):
            px, py, pc = x ^ (p >> 2), y ^ ((p >> 1) & 1), c ^ (p & 1)
            cps.append(pltpu.make_async_remote_copy(src_ref=v_ref, dst_ref=buf_ref.at[me], send_sem=send_sems.at[p - 1],
                                                    recv_sem=recv_sems.at[p - 1], device_id=(px, py, pc), device_id_type=MESH_ID))
            cps[-1].start()
        buf_ref[me] = v_ref[...]
        for p in range(1, 8):
            theirs = buf_ref.at[me ^ p]
            pltpu.make_async_remote_copy(src_ref=theirs, dst_ref=theirs, send_sem=send_sems.at[p - 1], recv_sem=recv_sems.at[p - 1],
                                         device_id=(x, y, c), device_id_type=MESH_ID).wait_recv()
        for cp in cps:
            cp.wait_send()
        acc = buf_ref[0]
        for k in range(1, 8):
            acc = acc + buf_ref[k]
        o_ref[...] = acc

    vm = pl.BlockSpec(memory_space=pltpu.VMEM)
    return pl.pallas_call(
        body, name=name, in_specs=[vm, _ANY], out_specs=vm, out_shape=jax.ShapeDtypeStruct(vec.shape, F32),
        scratch_shapes=[pltpu.VMEM((8,) + vec.shape, F32), pltpu.SemaphoreType.DMA((7,)), pltpu.SemaphoreType.DMA((7,))],
    )(vec, after)


def _adam_math(w, g, m, v):
    m = ADAM_B1 * m + (1.0 - ADAM_B1) * g
    v = ADAM_B2 * v + (1.0 - ADAM_B2) * (g * g)
    m_hat = m / (1.0 - ADAM_B1 ** ADAM_STEP)
    v_hat = v / (1.0 - ADAM_B2 ** ADAM_STEP)
    return -ADAM_LR * (m_hat / (jnp.sqrt(v_hat) + ADAM_EPS) + ADAM_WD * w), m, v


def _adamw(name, w, g, m, v):
    R, C = w.shape
    tr = _row_block(R, C, 8)

    def body(w_ref, g_ref, m_ref, v_ref, d_ref, nm_ref, nv_ref):
        d_ref[...], nm_ref[...], nv_ref[...] = _adam_math(w_ref[...], g_ref[...], m_ref[...], v_ref[...])

    blk = pl.BlockSpec((tr, C), lambda i: (i, 0))
    return pl.pallas_call(
        body, name=name, grid=(R // tr,), in_specs=[blk] * 4, out_specs=[blk] * 3,
        out_shape=[jax.ShapeDtypeStruct((R, C), F32)] * 3, compiler_params=_params(("parallel",)),
    )(w, g, m, v)


def _adamw_halves(name, w, mine, theirs, m, v, c):
    rs, cs = w.shape[0] * w.shape[1], w.shape[2]
    (br, bc), nb, whole, half = _half_blocks(rs, cs, 8)
    spec, get, put = _shard_blocks(w, br, bc)

    def body(c_ref, w_ref, a_ref, b_ref, m_ref, v_ref, g_ref, d_ref, nm_ref, nv_ref):
        g = jnp.where(pl.program_id(0) == c_ref[0], a_ref[...], b_ref[...])
        put(g_ref, g)
        for ref, val in zip((d_ref, nm_ref, nv_ref), _adam_math(get(w_ref), g, get(m_ref), get(v_ref))):
            put(ref, val)

    full = spec(lambda s, i, c_ref: whole(s, i))
    part = pl.BlockSpec((br, bc), lambda s, i, c_ref: half(i))
    return pl.pallas_call(
        body, name=name,
        grid_spec=pltpu.PrefetchScalarGridSpec(num_scalar_prefetch=1, grid=(2, nb), in_specs=[full, part, part, full, full],
                                               out_specs=[full] * 4),
        out_shape=[jax.ShapeDtypeStruct(w.shape, F32)] * 4, compiler_params=_params(("parallel", "parallel")),
    )(_scalar(c), w, mine, theirs, m, v)


def _pack_small(arrs):
    flat = jnp.concatenate([a.reshape(-1) for a in arrs])
    n = -(-flat.shape[0] // (8 * LANE)) * 8 * LANE
    return jnp.pad(flat, (0, n - flat.shape[0])).reshape(8, n // 8)


def _unpack_small(vec, shapes):
    flat, out, off = vec.reshape(-1), [], 0
    for s in shapes:
        out.append(flat[off:off + s[0] * s[1]].reshape(s))
        off += s[0] * s[1]
    return out


class _LateWeights:
    def __init__(self, cfg, tag, names, staged, after):
        self.cfg, self.tag, self.names, self.k = cfg, tag, names, 3 * len(names)
        self.send, self.recv, self.bufs, self.token = _split_start(f"gather_{tag}_chips_start", staged, self.k, _gather_to_chips,
                                                                    after)

    def pass_on(self, after):
        bufs = _split_wait(f"gather_{self.tag}_chips_wait", self.send, self.recv, self.bufs, after, _gather_to_chips)
        self.send, self.recv, self.bufs, token = _split_start(f"gather_{self.tag}_sibling_start", bufs, self.k, _gather_to_sibling,
                                                               self.token)
        return token

    def arrived(self, after):
        bufs = _split_wait(f"gather_{self.tag}_sibling_wait", self.send, self.recv, self.bufs, after, _gather_to_sibling)
        return {n: _gathered_to_kernel(self.cfg, n, b) for n, b in zip(self.names, bufs)}


def _step(cfg, a):
    chip = 2 * lax.axis_index("x") + lax.axis_index("y")
    core = lax.axis_index("c")
    big = BIG

    ffn = ("w_gate", "w_up", "w_down")
    first = ("w_in", "w_uq", "w_ukv")
    staged = {"w_in": _stage_shard("w_in", a["w_in"], chip)}
    in_weight = _LateWeights(cfg, "in", ("w_in",), [staged["w_in"]], jnp.zeros((8, LANE), F32))
    staged.update({n: _stage_shard(n, a[n], chip, in_weight.token) for n in big if n != "w_in"})
    in_sibling_leg = in_weight.pass_on(staged[big[-1]])

    sp = {n: a[n] for n in SMALL}
    sharded = _pack_small([a[n] for n in SMALL_SHARDED])
    slot = jnp.where(lax.broadcasted_iota(I32, (N_CHIPS,) + sharded.shape, 0) == chip, 0.5 * sharded[None], 0.0)
    allp = _allreduce_small("allgather_small", slot.reshape(N_CHIPS * 8, -1), in_sibling_leg).reshape((N_CHIPS,) + sharded.shape)
    per_chip = [_unpack_small(allp[ch], [a[n].shape for n in SMALL_SHARDED]) for ch in range(N_CHIPS)]
    for k, n in enumerate(SMALL_SHARDED):
        sp[n] = jnp.concatenate([per_chip[ch][k] for ch in range(N_CHIPS)], axis=1)
    W = in_weight.arrived(allp)

    mla_weights = _LateWeights(cfg, "mla", first[1:], [staged[n] for n in first[1:]], W["w_in"])
    out_weight = _LateWeights(cfg, "out", ("w_out",), [staged["w_out"]], mla_weights.token)
    ffn_weights = _LateWeights(cfg, "ffn", ffn[:2], [staged[n] for n in ffn[:2]], out_weight.token)
    down_weight = _LateWeights(cfg, "down", ffn[2:], [staged[n] for n in ffn[2:]], ffn_weights.token)
    sp["mix_pre_g"] = sp["mix_pre_g"] + (mla_weights.token[0, 0] + out_weight.token[0, 0] + ffn_weights.token[0, 0]
                                         + down_weight.token[0, 0])

    state = {}

    def ffn_grads_ready(grads):
        state["ffn_pairs"], token = _pair_exchange_start("ffn", [_grad_to_chips(cfg, n, grads[n]) for n in ffn_grads])
        return token

    def pair_sums(names, grads, theirs):
        return [_pair_sum(n, g, t, core) for n, g, t in zip(names, grads, theirs)]

    def early_grads_ready(grads):
        g_out = [_grad_to_chips(cfg, "w_out", grads["w_out"])]
        g_ffn, t_ffn = _pair_exchange_wait("ffn", state["ffn_pairs"], g_out[0])
        sums = pair_sums(ffn_grads, g_ffn, t_ffn) + pair_sums(["w_out"], g_out, _pair_exchange("out", g_out))
        state["early"] = _chip_exchange_start("early", sums)
        return state["early"][-1]

    def reduced_halves(tag, names, after):
        send_sems, recv_sems, s_bufs, l_bufs, _ = state[tag]
        s_bufs, l_bufs = _chip_exchange_wait(tag, send_sems, recv_sems, s_bufs, l_bufs, after)
        return [_chip_sum(n, s, t, chip) for n, s, t in zip(names, s_bufs, l_bufs)]

    def in_grad_ready(grads):
        grads = [_grad_to_chips(cfg, n, grads[n]) for n in first]
        state["rest"] = _chip_exchange_start("rest", pair_sums(first, grads, _pair_exchange("rest", grads)))
        return state["rest"][-1]

    ffn_grads = ("w_down", "w_gate", "w_up")
    early = ffn_grads + ("w_out",)
    loss, grad_x, gW, gs = _local_grads(cfg, a["x"], a["loss_target"], W, sp, mla_weights, out_weight, ffn_weights, down_weight,
                                        ffn_grads_ready, early_grads_ready, in_grad_ready)
    out = {"grad_x": grad_x}

    def adamw(names, mine, theirs):
        for n, gm, gt in zip(names, mine, theirs):
            out["grad_" + n], out["delta_" + n], out["new_m_" + n], out["new_v_" + n] = _adamw_halves(
                "adamw_" + n, a[n], gm, gt, a["m_" + n], a["v_" + n], core)

    mine = reduced_halves("early", early, grad_x)
    adamw(early, mine, _sibling_exchange("early", mine))
    mine = reduced_halves("rest", first, out["new_v_" + early[-1]])
    theirs = _sibling_exchange("rest", mine)
    adamw(first, mine, theirs)

    shapes = [gs[n].shape for n in SMALL] + [(1, LANE)]
    red = _unpack_small(_allreduce_small("allreduce_small", _pack_small([gs[n] for n in SMALL] + [loss]), theirs[0]), shapes)
    g_small = dict(zip(SMALL, red[:-1]))
    for n in SMALL_SHARDED:
        cs = a[n].shape[1]
        g_small[n] = lax.dynamic_slice_in_dim(g_small[n], chip * cs, cs, axis=1)
    out["loss"] = red[-1][0, 0]
    sshapes = [a[n].shape for n in SMALL]
    d, nm, nv = _adamw("adamw_small", _pack_small([a[n] for n in SMALL]), _pack_small([g_small[n] for n in SMALL]),
                       _pack_small([a["m_" + n] for n in SMALL]), _pack_small([a["v_" + n] for n in SMALL]))
    for n, dd, mm, vv in zip(SMALL, _unpack_small(d, sshapes), _unpack_small(nm, sshapes), _unpack_small(nv, sshapes)):
        out["grad_" + n], out["delta_" + n], out["new_m_" + n], out["new_v_" + n] = g_small[n], dd, mm, vv
    return out


def kernel(x, mix_pre_g, w_in, q_norm_g, w_uq, kv_norm_g, w_ukv, ssm_conv_w, ssm_conv_b, dt_bias, a_log, d_skip, ssm_norm_g, w_out, mix_post_g, ffn_pre_g, w_gate, w_up, ffn_conv_w, ffn_conv_b, w_down, ffn_post_g, loss_target, m_mix_pre_g, m_w_in, m_q_norm_g, m_w_uq, m_kv_norm_g, m_w_ukv, m_ssm_conv_w, m_ssm_conv_b, m_dt_bias, m_a_log, m_d_skip, m_ssm_norm_g, m_w_out, m_mix_post_g, m_ffn_pre_g, m_w_gate, m_w_up, m_ffn_conv_w, m_ffn_conv_b, m_w_down, m_ffn_post_g, v_mix_pre_g, v_w_in, v_q_norm_g, v_w_uq, v_kv_norm_g, v_w_ukv, v_ssm_conv_w, v_ssm_conv_b, v_dt_bias, v_a_log, v_d_skip, v_ssm_norm_g, v_w_out, v_mix_post_g, v_ffn_pre_g, v_w_gate, v_w_up, v_ffn_conv_w, v_ffn_conv_b, v_w_down, v_ffn_post_g):
    args = dict(locals())
    def given(k, v):
        if k in ("w_in", "m_w_in", "v_w_in"):
            return jnp.transpose(v, (2, 0, 1))
        return v if k.removeprefix("m_").removeprefix("v_") in BIG or v.ndim < 3 else v[0]

    out = _step(_FULL, {k: given(k, v) for k, v in args.items()})
    res = [out["loss"], out["grad_x"][None]]
    for pre in ("grad_", "delta_", "new_m_", "new_v_"):
        for n in WEIGHTS:
            o = out[pre + n]
            res.append(jnp.transpose(o, (1, 2, 0)) if n == "w_in" else o if n in BIG or args[n].ndim < 3 else o[None])
    return tuple(res)
```

```python
import functools
import math

import jax
import jax.numpy as jnp
from jax import lax
from jax.experimental import pallas as pl
from jax.experimental.pallas import tpu as pltpu

F32, BF16, I32 = jnp.float32, jnp.bfloat16, jnp.int32
NN = (((1,), (0,)), ((), ()))
NT = (((1,), (1,)), ((), ()))
TN = (((0,), (0,)), ((), ()))
HI = lax.Precision.HIGHEST
MESH_ID = pl.DeviceIdType.MESH

EPS = 1e-6
CHUNK = 64
NOPE, ROPE, VH = 128, 64, 128
ROPE_THETA = 10000.0
HP, NST = 64, 128
SSM_K, FFN_K = 4, 3
LANE = 128
N_CHIPS = 4
VMEM_LIMIT = 52 * 1024 * 1024
MM_TILE, MM_TILE_K = 1408, 2816

ADAM_LR, ADAM_B1, ADAM_B2, ADAM_EPS, ADAM_WD, ADAM_STEP = 0.001, 0.9, 0.999, 1e-08, 0.01, 10


class _Cfg:
    def __init__(self, S, D, QL, KVL, H, HS, G, DFF, T):
        self.S, self.D, self.QL, self.KVL, self.H, self.HS, self.G, self.DFF, self.T = S, D, QL, KVL, H, HS, G, DFF, T
        self.INNER = HS * HP
        self.CONVCH = self.INNER + 2 * G * NST
        self.QW = H * (NOPE + ROPE)
        self.KVW = H * (NOPE + VH)
        self.MLAW = H * VH
        self.MIXW = self.MLAW + self.INNER
        self.IN_COLS = QL + KVL + ROPE + self.INNER + self.CONVCH + HS
        natural, at = {}, 0
        for name, w in (("c_q", QL), ("c_kv", KVL), ("kr", ROPE), ("z", self.INNER), ("xbc", self.CONVCH), ("dt", HS)):
            natural[name] = (at, w)
            at += w
        self.seg, taken = {}, []
        for name in sorted(natural, key=lambda n: -natural[n][1]):
            w = -(-natural[name][1] // LANE) * LANE
            off = next(o for o in range(0, 64 * w, w) if all(o + w <= t or o >= t + tw for t, tw in taken))
            taken.append((off, w))
            self.seg[name] = (off, w) + natural[name]
        self.EXT = max(o + w for o, w in taken)
        self.NPAIR = HS // 2
        self.REP = HS // G

    def window(self, name):
        off, w, _, _ = self.seg[name]
        return w, off // w


_FULL = _Cfg(S=2048, D=2048, QL=768, KVL=512, H=8, HS=16, G=2, DFF=5632, T=256)
BIG = ("w_in", "w_uq", "w_ukv", "w_out", "w_gate", "w_up", "w_down")

SMALL = ("mix_pre_g", "q_norm_g", "kv_norm_g", "ssm_conv_w", "ssm_conv_b", "dt_bias", "a_log", "d_skip", "ssm_norm_g",
         "mix_post_g", "ffn_pre_g", "ffn_conv_w", "ffn_conv_b", "ffn_post_g")
SMALL_SHARDED = ("ssm_conv_w", "ffn_conv_w")
WEIGHTS = ("mix_pre_g", "w_in", "q_norm_g", "w_uq", "kv_norm_g", "w_ukv", "ssm_conv_w", "ssm_conv_b", "dt_bias", "a_log",
           "d_skip", "ssm_norm_g", "w_out", "mix_post_g", "ffn_pre_g", "w_gate", "w_up", "ffn_conv_w", "ffn_conv_b",
           "w_down", "ffn_post_g")


def _pick(n, target, mult):
    best = None
    for d in range(mult, min(n, target) + 1, mult):
        if n % d == 0:
            best = d
    return best if best is not None else n


def _params(sem=None):
    kw = dict(vmem_limit_bytes=VMEM_LIMIT)
    if sem is not None:
        kw["dimension_semantics"] = sem
    return pltpu.CompilerParams(**kw)


def _dot(a, b, dims=NN, precision=None):
    return lax.dot_general(a, b, dims, preferred_element_type=F32, precision=precision)


def _sigmoid(x):
    return 1.0 / (1.0 + jnp.exp(-x))


def _rs(x):
    return lax.rsqrt(jnp.mean(x * x, axis=-1, keepdims=True) + EPS)


def _rms_back(xh, r, dn):
    return r * (dn - xh * jnp.mean(dn * xh, axis=-1, keepdims=True))


def _colsum(v):
    return jnp.sum(v, axis=0, keepdims=True)


def _matmul(name, a, b, mode, out_dtype, a2=None, b2=None, chips=False):
    cs = None
    if mode == "nn":
        (M, K), N = a.shape, b.shape[-1]
        if chips:
            cs, N = N, N_CHIPS * N
    elif mode == "nt":
        (M, K), N = a.shape, b.shape[-2]
        if chips:
            cs = b.shape[-1]
    else:
        (K, M), N = a.shape, b.shape[1]
        if chips:
            cs = N // N_CHIPS
    tm = _pick(M, MM_TILE, LANE)
    tn = _pick(cs if chips and mode != "nt" else N, MM_TILE, LANE)
    tk = _pick(cs, MM_TILE, LANE) if chips and mode == "nt" else _pick(K, MM_TILE_K, LANE)
    nk = K // tk
    dims = {"nn": NN, "nt": NT, "tn": TN}[mode]
    a_spec = pl.BlockSpec((tk, tm), lambda i, j, k: (k, i)) if mode == "tn" else pl.BlockSpec((tm, tk), lambda i, j, k: (i, k))
    b_spec = pl.BlockSpec((tn, tk), lambda i, j, k: (j, k)) if mode == "nt" else pl.BlockSpec((tk, tn), lambda i, j, k: (k, j))
    o_spec = pl.BlockSpec((tm, tn), lambda i, j, k: (i, j))
    o_shape = (M, N)
    if chips and mode == "nn":
        per = cs // tn
        b_spec = pl.BlockSpec((None, tk, tn), lambda i, j, k: (j // per, k, j % per))
    elif chips and mode == "nt":
        per = cs // tk
        b_spec = pl.BlockSpec((None, tn, tk), lambda i, j, k: (k // per, j, k % per))
    elif chips:
        per = cs // tn
        o_spec = pl.BlockSpec((None, tm, tn), lambda i, j, k: (j // per, i, j % per))
        o_shape = (N_CHIPS, M, cs)
    two = a2 is not None

    def product(refs):
        part = _dot(refs[0][...].astype(BF16), refs[1][...].astype(BF16), dims)
        if two:
            part += _dot(refs[2][...].astype(BF16), refs[3][...].astype(BF16), dims)
        return part

    def body_whole_k(*refs):
        refs[-1][...] = product(refs).astype(refs[-1].dtype)

    def body(*refs):
        o_ref, acc_ref = refs[-2], refs[-1]
        k = pl.program_id(2)

        @pl.when(k == 0)
        def _():
            acc_ref[...] = product(refs)

        @pl.when(k > 0)
        def _():
            acc_ref[...] += product(refs)

        @pl.when(k == nk - 1)
        def _():
            o_ref[...] = acc_ref[...].astype(o_ref.dtype)

    ins = (a, b, a2, b2) if two else (a, b)
    return pl.pallas_call(
        body_whole_k if nk == 1 else body, name=name, grid=(M // tm, N // tn, nk),
        in_specs=[a_spec, b_spec] * (2 if two else 1),
        out_specs=o_spec,
        out_shape=jax.ShapeDtypeStruct(o_shape, out_dtype),
        scratch_shapes=[] if nk == 1 else [pltpu.VMEM((tm, tn), F32)],
        compiler_params=_params(("parallel", "parallel", "arbitrary")),
    )(*ins)


def _window(a):
    return (a[0], *a[1]) if isinstance(a, tuple) else (a, a.shape[1], 0)


def _rowwise(name, fn, rows, mats, outs, reds, ts):
    rows, widths, blocks = zip(*[_window(a) for a in rows])
    S = rows[0].shape[0]
    nr, nm, no = len(rows), len(mats), len(outs)

    def body(*refs):
        res = fn(*[r[...] for r in refs[:nr + nm]])
        res = res if isinstance(res, (tuple, list)) else (res,)
        for r, v in zip(refs[nr + nm:nr + nm + no], res[:no]):
            r[...] = v.astype(r.dtype)
        first = pl.program_id(0) == 0
        for r, v in zip(refs[nr + nm + no:], res[no:]):
            @pl.when(first)
            def _():
                r[...] = jnp.broadcast_to(v, r.shape)

            @pl.when(jnp.logical_not(first))
            def _():
                r[...] += jnp.broadcast_to(v, r.shape)

    in_specs = [pl.BlockSpec((ts, w), lambda i, b=b: (i, b)) for w, b in zip(widths, blocks)]
    in_specs += [pl.BlockSpec(m.shape, lambda i, nd=m.ndim: (0,) * nd) for m in mats]
    out_specs = [pl.BlockSpec((ts, w), lambda i: (i, 0)) for w, _ in outs]
    out_specs += [pl.BlockSpec(s, lambda i: (0, 0)) for s in reds]
    out_shape = [jax.ShapeDtypeStruct((S, w), dt) for w, dt in outs] + [jax.ShapeDtypeStruct(s, F32) for s in reds]
    return pl.pallas_call(
        body, name=name, grid=(S // ts,), in_specs=in_specs, out_specs=out_specs, out_shape=out_shape,
        compiler_params=_params(("arbitrary",) if reds else ("parallel",)),
    )(*rows, *mats)


def _shift_down(v, s):
    if s == 0:
        return v
    rows = lax.broadcasted_iota(I32, v.shape, 0)
    return jnp.where(rows >= s, pltpu.roll(v, s, 0), 0.0)


def _shift_up(v, s):
    if s == 0:
        return v
    n = v.shape[0]
    rows = lax.broadcasted_iota(I32, v.shape, 0)
    return jnp.where(rows < n - s, pltpu.roll(v, n - s, 0), 0.0)


def _conv(x, w, b):
    K = w.shape[0]
    y = jnp.broadcast_to(b, x.shape)
    for k in range(K):
        y = y + w[k:k + 1, :] * _shift_down(x, K - 1 - k)
    return y


def _conv_back(x, w, dc):
    K = w.shape[0]
    dx = jnp.zeros_like(x)
    dw = []
    for k in range(K):
        dx = dx + w[k:k + 1, :] * _shift_up(dc, K - 1 - k)
        dw.append(_colsum(dc * _shift_down(x, K - 1 - k)))
    return dx, jnp.concatenate(dw, axis=0), _colsum(dc)


def _colwise(name, fn, cols, vecs, outs, pouts, tc):
    cols, widths, blocks = zip(*[_window(a) for a in cols])
    S, C = cols[0].shape[0], widths[0]
    firsts = [b * (C // tc) for b in blocks]
    nc_, nv, no = len(cols), len(vecs), len(outs)

    def body(*refs):
        res = fn(*[r[...] for r in refs[:nc_ + nv]])
        res = res if isinstance(res, (tuple, list)) else (res,)
        for r, v in zip(refs[nc_ + nv:], res):
            r[...] = v.astype(r.dtype)

    in_specs = [pl.BlockSpec((S, tc), lambda j, f=f: (0, f + j)) for f in firsts]
    in_specs += [pl.BlockSpec((v.shape[0], tc), lambda j: (0, j)) for v in vecs]
    out_specs = [pl.BlockSpec((S, tc), lambda j: (0, j)) for _ in outs] + [pl.BlockSpec((k, tc), lambda j: (0, j)) for k in pouts]
    out_shape = [jax.ShapeDtypeStruct((S, C), dt) for dt in outs] + [jax.ShapeDtypeStruct((k, C), F32) for k in pouts]
    return pl.pallas_call(
        body, name=name, grid=(C // tc,), in_specs=in_specs, out_specs=out_specs, out_shape=out_shape,
        compiler_params=_params(("parallel",)),
    )(*cols, *vecs)


_G0, _G1 = math.sqrt(2.0 / math.pi), 0.044715


def _gelu(g):
    th = jnp.tanh(_G0 * (g + _G1 * g * g * g))
    return 0.5 * g * (1.0 + th), th


def _ffn_act(gate_pre, up, w, b):
    act, _ = _gelu(_conv(gate_pre, w, b))
    return act * up


def _ffn_act_back(dact, gate_pre, up, w, b):
    g = _conv(gate_pre, w, b)
    ge, th = _gelu(g)
    dge = 0.5 * (1.0 + th) + 0.5 * g * (1.0 - th * th) * _G0 * (1.0 + 3.0 * _G1 * g * g)
    dup = dact * ge
    dgate_pre, dw, db = _conv_back(gate_pre, w, dact * up * dge)
    return dgate_pre, dup, dw, db


def _ssm_act(xbc, w, b):
    c = _conv(xbc, w, b)
    return c * _sigmoid(c)


def _ssm_act_back(dxc, xbc, w, b):
    c = _conv(xbc, w, b)
    sg = _sigmoid(c)
    return _conv_back(xbc, w, dxc * sg * (1.0 + c * (1.0 - sg)))


def _rope_tables(S):
    inv = 1.0 / (ROPE_THETA ** (jnp.arange(0, ROPE, 2, dtype=F32) / ROPE))
    ang = jnp.arange(S, dtype=F32)[:, None] * inv[None, :]
    cos, sin = jnp.cos(ang), jnp.sin(ang)
    return jnp.tile(cos, (1, 4)), jnp.tile(jnp.concatenate([-sin, sin], axis=1), (1, 2))


def _swap_halves(x):
    lane = lax.broadcasted_iota(I32, x.shape, 1)
    w = x.shape[1]
    return jnp.where((lane % ROPE) < ROPE // 2, pltpu.roll(x, w - ROPE // 2, 1), pltpu.roll(x, ROPE // 2, 1))


def _rot(x, cos2, sin2):
    return x * cos2 + _swap_halves(x) * sin2


def _rot_back(dy, cos2, sin2):
    return dy * cos2 + _swap_halves(dy * sin2)


def _mla_pack(cfg, q, kv, kr, cos2, sin2):
    S, H = cfg.S, cfg.H
    ts = _pick(S, 256, 8)
    kr, _, kr_block = _window(kr)

    def body(q_ref, kv_ref, kr_ref, c_ref, s_ref, Q_ref, K_ref, V_ref):
        c2, s2 = c_ref[...], s_ref[...]
        krr = _rot(kr_ref[...], c2, s2)
        kr_half = (krr.astype(BF16), pltpu.roll(krr, ROPE, 1).astype(BF16))
        for j in range(H // 2):
            qr = _rot(q_ref[:, (H + j) * LANE:(H + j + 1) * LANE], c2, s2).astype(BF16)
            for h in (2 * j, 2 * j + 1):
                Q_ref[h, :, 0:LANE] = q_ref[:, h * LANE:(h + 1) * LANE].astype(BF16)
                Q_ref[h, :, LANE:] = qr
                K_ref[h, :, 0:LANE] = kv_ref[:, h * LANE:(h + 1) * LANE].astype(BF16)
                K_ref[h, :, LANE:] = kr_half[h % 2]
                V_ref[h] = kv_ref[:, (H + h) * LANE:(H + h + 1) * LANE].astype(BF16)

    tab = pl.BlockSpec((ts, LANE), lambda i: (i, 0))
    heads = lambda w: pl.BlockSpec((H, ts, w), lambda i: (0, i, 0))
    return pl.pallas_call(
        body, name="mla_pack", grid=(S // ts,),
        in_specs=[pl.BlockSpec((ts, cfg.QW), lambda i: (i, 0)), pl.BlockSpec((ts, cfg.KVW), lambda i: (i, 0)),
                  pl.BlockSpec((ts, LANE), lambda i: (i, kr_block)), tab, tab],
        out_specs=[heads(2 * LANE), heads(2 * LANE), heads(LANE)],
        out_shape=[jax.ShapeDtypeStruct((H, S, 2 * LANE), BF16), jax.ShapeDtypeStruct((H, S, 2 * LANE), BF16),
                   jax.ShapeDtypeStruct((H, S, LANE), BF16)],
        compiler_params=_params(("parallel",)),
    )(q, kv, kr, cos2, sin2)


def _mla_unpack(cfg, dQ, dK, dV, cos2, sin2):
    S, H = cfg.S, cfg.H
    ts = _pick(S, 256, 8)

    def body(dQ_ref, dK_ref, dV_ref, c_ref, s_ref, dq_ref, dkv_ref, dkr_ref):
        c2, s2 = c_ref[...], s_ref[...]
        lo = lax.broadcasted_iota(I32, (ts, LANE), 1) < ROPE
        tk = jnp.zeros((ts, LANE), F32)
        for h in range(H):
            dq_ref[:, h * LANE:(h + 1) * LANE] = dQ_ref[h, :, 0:LANE].astype(BF16)
            dkv_ref[:, h * LANE:(h + 1) * LANE] = dK_ref[h, :, 0:LANE].astype(BF16)
            dkv_ref[:, (H + h) * LANE:(H + h + 1) * LANE] = dV_ref[h].astype(BF16)
            own = lo if h % 2 == 0 else jnp.logical_not(lo)
            tk = tk + jnp.where(own, dK_ref[h, :, LANE:], 0.0)
        for j in range(H // 2):
            dr = dQ_ref[2 * j, :, LANE:] + dQ_ref[2 * j + 1, :, LANE:]
            dq_ref[:, (H + j) * LANE:(H + j + 1) * LANE] = _rot_back(dr, c2, s2).astype(BF16)
        dkr_rot = jnp.where(lo, tk + pltpu.roll(tk, ROPE, 1), 0.0)
        dkr_ref[...] = _rot_back(dkr_rot, c2, s2).astype(BF16)

    tab = pl.BlockSpec((ts, LANE), lambda i: (i, 0))
    return pl.pallas_call(
        body, name="mla_unpack", grid=(S // ts,),
        in_specs=[pl.BlockSpec((H, ts, 2 * LANE), lambda i: (0, i, 0)), pl.BlockSpec((H, ts, 2 * LANE), lambda i: (0, i, 0)),
                  pl.BlockSpec((H, ts, LANE), lambda i: (0, i, 0)), tab, tab],
        out_specs=[pl.BlockSpec((ts, cfg.QW), lambda i: (i, 0)), pl.BlockSpec((ts, cfg.KVW), lambda i: (i, 0)), tab],
        out_shape=[jax.ShapeDtypeStruct((S, cfg.QW), BF16), jax.ShapeDtypeStruct((S, cfg.KVW), BF16),
                   jax.ShapeDtypeStruct((S, LANE), BF16)],
        compiler_params=_params(("parallel",)),
    )(dQ, dK, dV, cos2, sin2)


_ATT_T = 256
_ATT_HB = 8
_ATT_SCALE = (NOPE + ROPE) ** -0.5


def _diag_mask(transposed=False):
    r = lax.broadcasted_iota(I32, (_ATT_T, _ATT_T), 0) // CHUNK
    c = lax.broadcasted_iota(I32, (_ATT_T, _ATT_T), 1) // CHUNK
    return r <= c if transposed else c <= r


def _row_form(col):
    return jnp.broadcast_to(col, (col.shape[0], LANE)).T[0:8, :]


def _attn_fwd(cfg, Q, K, V):
    S, H, T, HB = cfg.S, cfg.H, _ATT_T, min(cfg.H, _ATT_HB)

    def body(q_ref, k_ref, v_ref, o_ref, lse_ref, lse_t_ref):
        qi = pl.program_id(1)

        def head_step(b, kb, carry, mask):
            m, l, acc = carry
            ks = pl.multiple_of(kb * T, T)
            s = _dot(q_ref[b], k_ref[b, pl.ds(ks, T), :], NT) * _ATT_SCALE
            if mask is not None:
                s = jnp.where(mask, s, -1e30)
            m_new = jnp.maximum(m, jnp.max(s, axis=1, keepdims=True))
            p = jnp.exp(s - m_new)
            alpha = jnp.exp(m - m_new)
            l = alpha * l + jnp.sum(p, axis=1, keepdims=True)
            acc = alpha * acc + _dot(p.astype(BF16), v_ref[b, pl.ds(ks, T), :])
            return m_new, l, acc

        def step(kb, carry, mask=None):
            return tuple(head_step(b, kb, carry[b], mask) for b in range(HB))

        init = (jnp.full((T, 1), -1e30, F32), jnp.zeros((T, 1), F32), jnp.zeros((T, VH), F32))
        done = step(qi, lax.fori_loop(0, qi, step, (init,) * HB), _diag_mask())
        for b, (m, l, acc) in enumerate(done):
            o_ref[:, b * LANE:(b + 1) * LANE] = acc / l
            lse = m + jnp.log(l)
            lse_ref[:, b * LANE:(b + 1) * LANE] = jnp.broadcast_to(lse, (T, LANE))
            lse_t_ref[b] = _row_form(lse)

    return pl.pallas_call(
        body, name="attn_fwd", grid=(H // HB, S // T),
        in_specs=[pl.BlockSpec((HB, T, 2 * LANE), lambda h, i: (h, i, 0)), pl.BlockSpec((HB, S, 2 * LANE), lambda h, i: (h, 0, 0)),
                  pl.BlockSpec((HB, S, LANE), lambda h, i: (h, 0, 0))],
        out_specs=[pl.BlockSpec((T, HB * LANE), lambda h, i: (i, h)), pl.BlockSpec((T, HB * LANE), lambda h, i: (i, h)),
                   pl.BlockSpec((HB, 8, T), lambda h, i: (h, 0, i))],
        out_shape=[jax.ShapeDtypeStruct((S, H * LANE), F32), jax.ShapeDtypeStruct((S, H * LANE), F32),
                   jax.ShapeDtypeStruct((H, 8, S), F32)],
        compiler_params=_params(("parallel", "parallel")),
    )(Q, K, V)


def _attn_dq(cfg, Q, K, V, do, o, lse, after):
    S, H, T, HB = cfg.S, cfg.H, _ATT_T, min(cfg.H, _ATT_HB)

    def body(q_ref, k_ref, v_ref, do_ref, o_ref, lse_ref, after_ref, dq_ref, dl_t_ref):
        qi = pl.program_id(1)
        do = [do_ref[:, b * LANE:(b + 1) * LANE] for b in range(HB)]
        delta = [jnp.sum(do[b] * o_ref[:, b * LANE:(b + 1) * LANE], axis=1, keepdims=True) for b in range(HB)]
        dob = [d.astype(BF16) for d in do]

        def head_step(b, kb, dq, mask):
            ks = pl.multiple_of(kb * T, T)
            k = k_ref[b, pl.ds(ks, T), :]
            s = _dot(q_ref[b], k, NT) * _ATT_SCALE
            if mask is not None:
                s = jnp.where(mask, s, -1e30)
            p = jnp.exp(s - lse_ref[:, b * LANE:b * LANE + 1])
            dp = _dot(dob[b], v_ref[b, pl.ds(ks, T), :], NT)
            ds = p * (dp - delta[b]) * _ATT_SCALE
            return dq + _dot(ds.astype(BF16), k)

        def step(kb, dqs, mask=None):
            return tuple(head_step(b, kb, dqs[b], mask) for b in range(HB))

        dqs = step(qi, lax.fori_loop(0, qi, step, (jnp.zeros((T, 2 * LANE), F32),) * HB), _diag_mask())
        for b in range(HB):
            dq_ref[b] = dqs[b]
            dl_t_ref[b] = _row_form(delta[b])

    col = pl.BlockSpec((T, HB * LANE), lambda h, i: (i, h))
    return pl.pallas_call(
        body, name="attn_dq", grid=(H // HB, S // T),
        in_specs=[pl.BlockSpec((HB, T, 2 * LANE), lambda h, i: (h, i, 0)), pl.BlockSpec((HB, S, 2 * LANE), lambda h, i: (h, 0, 0)),
                  pl.BlockSpec((HB, S, LANE), lambda h, i: (h, 0, 0)), col, col, col, _ANY],
        out_specs=[pl.BlockSpec((HB, T, 2 * LANE), lambda h, i: (h, i, 0)), pl.BlockSpec((HB, 8, T), lambda h, i: (h, 0, i))],
        out_shape=[jax.ShapeDtypeStruct((H, S, 2 * LANE), F32), jax.ShapeDtypeStruct((H, 8, S), F32)],
        compiler_params=_params(("parallel", "parallel")),
    )(Q, K, V, do, o, lse, after)


def _attn_dkv(cfg, Q, K, V, do, lse_t, delta_t):
    S, H, T, HB = cfg.S, cfg.H, _ATT_T, min(cfg.H, _ATT_HB)
    nq = S // T

    def body(q_ref, k_ref, v_ref, do_ref, lse_ref, dl_ref, dk_ref, dv_ref):
        kb = pl.program_id(1)

        def head_step(b, qi, carry, mask):
            dk, dv = carry
            qs = pl.multiple_of(qi * T, T)
            q = q_ref[b, pl.ds(qs, T), :]
            dob = do_ref[pl.ds(qs, T), b * LANE:(b + 1) * LANE].astype(BF16)
            s = _dot(k_ref[b], q, NT) * _ATT_SCALE
            if mask is not None:
                s = jnp.where(mask, s, -1e30)
            p = jnp.exp(s - lse_ref[b, 0:1, pl.ds(qs, T)])
            dv = dv + _dot(p.astype(BF16), dob)
            dp = _dot(v_ref[b], dob, NT)
            ds = p * (dp - dl_ref[b, 0:1, pl.ds(qs, T)]) * _ATT_SCALE
            dk = dk + _dot(ds.astype(BF16), q)
            return dk, dv

        def step(qi, carry, mask=None):
            return tuple(head_step(b, qi, carry[b], mask) for b in range(HB))

        zero = (jnp.zeros((T, 2 * LANE), F32), jnp.zeros((T, VH), F32))
        done = lax.fori_loop(kb + 1, nq, step, step(kb, (zero,) * HB, _diag_mask(transposed=True)))
        for b, (dk, dv) in enumerate(done):
            dk_ref[b] = dk
            dv_ref[b] = dv

    row = pl.BlockSpec((HB, 8, S), lambda h, j: (h, 0, 0))
    return pl.pallas_call(
        body, name="attn_dkv", grid=(H // HB, S // T),
        in_specs=[pl.BlockSpec((HB, S, 2 * LANE), lambda h, j: (h, 0, 0)), pl.BlockSpec((HB, T, 2 * LANE), lambda h, j: (h, j, 0)),
                  pl.BlockSpec((HB, T, LANE), lambda h, j: (h, j, 0)), pl.BlockSpec((S, HB * LANE), lambda h, j: (0, h)), row, row],
        out_specs=[pl.BlockSpec((HB, T, 2 * LANE), lambda h, j: (h, j, 0)), pl.BlockSpec((HB, T, LANE), lambda h, j: (h, j, 0))],
        out_shape=[jax.ShapeDtypeStruct((H, S, 2 * LANE), F32), jax.ShapeDtypeStruct((H, S, LANE), F32)],
        compiler_params=_params(("parallel", "parallel")),
    )(Q, K, V, do, lse_t, delta_t)


def _expand_matrix(cfg):
    r = lax.broadcasted_iota(I32, (LANE, cfg.INNER), 0)
    c = lax.broadcasted_iota(I32, (LANE, cfg.INNER), 1)
    return (r == c // HP).astype(F32)


def _softplus(x):
    return jnp.maximum(x, 0.0) + jnp.log(1.0 + jnp.exp(-jnp.abs(x)))


def _ssd_prep(cfg, dt_raw, dt_bias_pad, a_log_pad, expand):
    HS = cfg.HS

    def fn(raw, bias, alog, E):
        heads = lax.broadcasted_iota(I32, raw.shape, 1) < HS
        dt = jnp.where(heads, _softplus(raw + bias), 0.0)
        a = dt * jnp.where(heads[0:1], -jnp.exp(alog), 0.0)
        return dt, a, _dot(dt, E, precision=HI), _dot(a, E, precision=HI)

    return _rowwise("ssd_prep", fn, [dt_raw], [dt_bias_pad, a_log_pad, expand],
                    [(LANE, F32), (LANE, F32), (cfg.INNER, F32), (cfg.INNER, F32)], [], _pick(cfg.S, 512, 8))


def _tril(T):
    return lax.broadcasted_iota(I32, (T, T), 0) >= lax.broadcasted_iota(I32, (T, T), 1)


def _ssd_fwd(cfg, xc, dt_exp, a_exp, a_small, dskip_exp):
    S, T, INNER, G, NPAIR = cfg.S, cfg.T, cfg.INNER, cfg.G, cfg.NPAIR
    NC = S // T

    def body(xc_ref, dte_ref, ae_ref, as_ref, dsk_ref, y_ref, hin_ref, ht_ref):
        @pl.when(pl.program_id(0) == 0)
        def _():
            ht_ref[...] = jnp.zeros_like(ht_ref)

        tril = _tril(T)
        tri = tril.astype(F32)
        acs_s = _dot(tri, as_ref[...], precision=HI)
        acs_e = _dot(tri, ae_ref[...], precision=HI)
        acs_t = acs_s.T
        lo = lax.broadcasted_iota(I32, (T, LANE), 1) < HP
        for g in range(G):
            Bb = xc_ref[:, INNER + g * NST:INNER + (g + 1) * NST].astype(BF16)
            Cb = xc_ref[:, INNER + (G + g) * NST:INNER + (G + g + 1) * NST].astype(BF16)
            Gm = _dot(Cb, Bb, NT)
            for j in range(g * NPAIR // G, (g + 1) * NPAIR // G):
                sl = slice(j * LANE, (j + 1) * LANE)
                Xp = xc_ref[:, sl]
                Xdt = Xp * dte_ref[:, sl]
                Xb = Xdt.astype(BF16)
                acs_p = acs_e[:, sl]
                last = acs_p[T - 1:T, :]
                Hin = ht_ref[j]
                hin_ref[0, j] = Hin
                yd = []
                for e in (0, 1):
                    h = 2 * j + e
                    Lm = jnp.exp(jnp.where(tril, acs_s[:, h:h + 1] - acs_t[h:h + 1, :], -1e30))
                    yd.append(_dot((Gm * Lm).astype(BF16), Xb))
                y_off = _dot(Cb, Hin.astype(BF16)) * jnp.exp(acs_p)
                y_ref[:, sl] = jnp.where(lo, yd[0], yd[1]) + y_off + Xp * dsk_ref[:, sl]
                st = _dot(Bb, (Xdt * jnp.exp(last - acs_p)).astype(BF16), TN)
                ht_ref[j] = jnp.exp(last) * Hin + st

    rows = lambda w: pl.BlockSpec((T, w), lambda c: (c, 0))
    return pl.pallas_call(
        body, name="ssd_fwd", grid=(NC,),
        in_specs=[rows(cfg.CONVCH), rows(INNER), rows(INNER), rows(LANE), pl.BlockSpec((1, INNER), lambda c: (0, 0))],
        out_specs=[rows(INNER), pl.BlockSpec((1, NPAIR, NST, LANE), lambda c: (c, 0, 0, 0))],
        out_shape=[jax.ShapeDtypeStruct((S, INNER), F32), jax.ShapeDtypeStruct((NC, NPAIR, NST, LANE), F32)],
        scratch_shapes=[pltpu.VMEM((NPAIR, NST, LANE), F32)],
        compiler_params=_params(("arbitrary",)),
    )(xc, dt_exp, a_exp, a_small, dskip_exp)


def _ssd_bwd(cfg, dy, xc, dt_exp, a_exp, a_small, dskip_exp, hin, dt_raw, dt_bias_pad, a_log_pad, expand):
    S, T, INNER, G, NPAIR, HS = cfg.S, cfg.T, cfg.INNER, cfg.G, cfg.NPAIR, cfg.HS
    NC = S // T

    def body(dy_ref, xc_ref, dte_ref, ae_ref, as_ref, dsk_ref, hin_ref, raw_ref, bias_ref, alog_ref, e_ref,
             dxc_ref, draw_ref, dbias_ref, dalog_ref, dskip_ref, dht_ref, cols_ref, rows_ref, dacs_ref, ddt_ref):
        first = pl.program_id(0) == 0

        @pl.when(first)
        def _():
            dht_ref[...] = jnp.zeros_like(dht_ref)

        tril = _tril(T)
        tri = tril.astype(F32)
        a_s = as_ref[...]
        acs_s = _dot(tri, a_s, precision=HI)
        acs_e = _dot(tri, ae_ref[...], precision=HI)
        acs_t = acs_s.T
        lo = lax.broadcasted_iota(I32, (T, LANE), 1) < HP
        last_row = lax.broadcasted_iota(I32, (T, LANE), 0) == T - 1
        cols_ref[...] = jnp.zeros_like(cols_ref)
        rows_ref[...] = jnp.zeros_like(rows_ref)
        dsk_parts = []
        for g in range(G):
            bsl = slice(INNER + g * NST, INNER + (g + 1) * NST)
            csl = slice(INNER + (G + g) * NST, INNER + (G + g + 1) * NST)
            Bb = xc_ref[:, bsl].astype(BF16)
            Cb = xc_ref[:, csl].astype(BF16)
            Gm = _dot(Cb, Bb, NT)
            dG = jnp.zeros((T, T), F32)
            dB = jnp.zeros((T, NST), F32)
            dC = jnp.zeros((T, NST), F32)
            for j in range(g * NPAIR // G, (g + 1) * NPAIR // G):
                sl = slice(j * LANE, (j + 1) * LANE)
                Xp = xc_ref[:, sl]
                dtp = dte_ref[:, sl]
                Xdt = Xp * dtp
                Xb = Xdt.astype(BF16)
                acs_p = acs_e[:, sl]
                last = acs_p[T - 1:T, :]
                e_p, dec, cd = jnp.exp(acs_p), jnp.exp(last - acs_p), jnp.exp(last)
                Hin = hin_ref[0, j]
                Hb = Hin.astype(BF16)
                dHn = dht_ref[j]
                dHb = dHn.astype(BF16)
                dYp = dy_ref[:, sl]
                z = _dot(Cb, Hb)
                dz = (dYp * e_p).astype(BF16)
                dacs_p = dYp * z * e_p
                dC = dC + _dot(dz, Hb, NT)
                dHin = _dot(Cb, dz, TN) + cd * dHn
                dlast = _colsum(dHn * Hin) * cd
                qv = _dot(Bb, dHb)
                dXdt = qv * dec
                ddec = qv * Xdt * dec
                dacs_p = dacs_p - ddec
                dlast = dlast + _colsum(ddec)
                dB = dB + _dot((Xdt * dec).astype(BF16), dHb, NT)
                for e in (0, 1):
                    h = 2 * j + e
                    Lm = jnp.exp(jnp.where(tril, acs_s[:, h:h + 1] - acs_t[h:h + 1, :], -1e30))
                    Mh = Gm * Lm
                    dYe = jnp.where(lo if e == 0 else jnp.logical_not(lo), dYp, 0.0).astype(BF16)
                    dM = _dot(dYe, Xb, NT)
                    dXdt = dXdt + _dot(Mh.astype(BF16), dYe, TN)
                    W = dM * Mh
                    cols_ref[:, h:h + 1] = jnp.sum(W, axis=1, keepdims=True)
                    rows_ref[h:h + 1, :] = _colsum(W)
                    dG = dG + dM * Lm
                dacs_ref[:, sl] = dacs_p + jnp.where(last_row, dlast, 0.0)
                ddt_ref[:, sl] = dXdt * Xp
                dxc_ref[:, sl] = dXdt * dtp + dYp * dsk_ref[:, sl]
                dsk_parts.append(_colsum(dYp * Xp))
                dht_ref[j] = dHin
            dGb = dG.astype(BF16)
            dxc_ref[:, bsl] = dB + _dot(dGb, Cb, TN)
            dxc_ref[:, csl] = dC + _dot(dGb, Bb)
        E = e_ref[...]
        dacs_s = cols_ref[...] - rows_ref[...].T + _dot(dacs_ref[...], E, NT, precision=HI)
        da = _dot(tri, dacs_s, TN, precision=HI)
        heads = lax.broadcasted_iota(I32, (1, LANE), 1) < HS
        A = jnp.where(heads, -jnp.exp(alog_ref[...]), 0.0)
        ddt = _dot(ddt_ref[...], E, NT, precision=HI) + da * A
        draw = jnp.where(heads, ddt * _sigmoid(raw_ref[...] + bias_ref[...]), 0.0)
        draw_ref[...] = draw
        dsk = _dot(jnp.broadcast_to(jnp.concatenate(dsk_parts, axis=1), (8, INNER)), E, NT, precision=HI)[0:1]
        for ref, val in ((dbias_ref, _colsum(draw)), (dalog_ref, _colsum(da * a_s)), (dskip_ref, dsk)):
            @pl.when(first)
            def _():
                ref[...] = val

            @pl.when(jnp.logical_not(first))
            def _():
                ref[...] += val

    dt_raw, _, raw_block = _window(dt_raw)
    rows = lambda w, b=0: pl.BlockSpec((T, w), lambda c: (NC - 1 - c, b))
    vec = lambda w: pl.BlockSpec((1, w), lambda c: (0, 0))
    return pl.pallas_call(
        body, name="ssd_bwd", grid=(NC,),
        in_specs=[rows(INNER), rows(cfg.CONVCH), rows(INNER), rows(INNER), rows(LANE), vec(INNER),
                  pl.BlockSpec((1, NPAIR, NST, LANE), lambda c: (NC - 1 - c, 0, 0, 0)), rows(LANE, raw_block), vec(LANE), vec(LANE),
                  pl.BlockSpec((LANE, INNER), lambda c: (0, 0))],
        out_specs=[rows(cfg.CONVCH), rows(LANE), vec(LANE), vec(LANE), vec(LANE)],
        out_shape=[jax.ShapeDtypeStruct((S, cfg.CONVCH), F32), jax.ShapeDtypeStruct((S, LANE), F32)]
        + [jax.ShapeDtypeStruct((1, LANE), F32)] * 3,
        scratch_shapes=[pltpu.VMEM((NPAIR, NST, LANE), F32), pltpu.VMEM((T, LANE), F32), pltpu.VMEM((LANE, T), F32),
                        pltpu.VMEM((T, INNER), F32), pltpu.VMEM((T, INNER), F32)],
        compiler_params=_params(("arbitrary",)),
    )(dy, xc, dt_exp, a_exp, a_small, dskip_exp, hin, dt_raw, dt_bias_pad, a_log_pad, expand)


def _ssd_post(cfg, y, z, norm_g):
    W = cfg.INNER // cfg.G

    def fn(y, z, g):
        yz = y * z * _sigmoid(z)
        return jnp.concatenate([yz[:, i * W:(i + 1) * W] * _rs(yz[:, i * W:(i + 1) * W]) for i in range(cfg.G)], axis=1) * g

    return _rowwise("ssd_post", fn, [y, z], [norm_g], [(cfg.INNER, BF16)], [], _pick(cfg.S, 256, 8))[0]


def _ssd_post_bwd(cfg, db, y, z, norm_g):
    W = cfg.INNER // cfg.G

    def fn(db, y, z, g):
        sg = _sigmoid(z)
        yz = y * z * sg
        dn = db * g
        dyz, nh = [], []
        for i in range(cfg.G):
            seg = yz[:, i * W:(i + 1) * W]
            r = _rs(seg)
            nh.append(seg * r)
            dyz.append(_rms_back(nh[-1], r, dn[:, i * W:(i + 1) * W]))
        dyz = jnp.concatenate(dyz, axis=1)
        return dyz * z * sg, dyz * y * sg * (1.0 + z * (1.0 - sg)), _colsum(db * jnp.concatenate(nh, axis=1))

    return _rowwise("ssd_post_bwd", fn, [db, y, z], [norm_g], [(cfg.INNER, F32), (cfg.INNER, F32)], [(1, cfg.INNER)],
                    _pick(cfg.S, 256, 8))


def _local_grads(cfg, x, tgt, W, sp, mla_weights=None, out_weight=None, ffn_weights=None, down_weight=None,
                 ffn_grads_ready=None, early_grads_ready=None, in_grad_ready=None):
    S, D, H, INNER = cfg.S, cfg.D, cfg.H, cfg.INNER
    ts = _pick(S, 256, 8)
    tc = 256

    xn = _rowwise("rms_pre", lambda x, g: x * _rs(x) * g, [x], [sp["mix_pre_g"]], [(D, BF16)], [], ts)[0]
    u = _matmul("mm_in", xn, W["w_in"], "nt", F32)
    c_q, c_kv, kr, z, xbc, dt_raw = [(u, cfg.window(n)) for n in ("c_q", "c_kv", "kr", "z", "xbc", "dt")]

    if mla_weights is not None:
        sp = dict(sp, q_norm_g=sp["q_norm_g"] + mla_weights.pass_on(u)[0, 0])
    cqn = _rowwise("rms_q", lambda x, g: x * _rs(x) * g, [c_q], [sp["q_norm_g"]], [(cfg.QL, BF16)], [], ts)[0]
    ckvn = _rowwise("rms_kv", lambda x, g: x * _rs(x) * g, [c_kv], [sp["kv_norm_g"]], [(cfg.KVL, BF16)], [], ts)[0]
    if mla_weights is not None:
        W = dict(W, **mla_weights.arrived(ckvn))
    q = _matmul("mm_uq", cqn, W["w_uq"], "nn", F32)
    kv = _matmul("mm_ukv", ckvn, W["w_ukv"], "nn", F32)
    cos2, sin2 = _rope_tables(S)
    Qh, Kh, Vh = _mla_pack(cfg, q, kv, kr, cos2, sin2)
    a_out, lse, lse_t = _attn_fwd(cfg, Qh, Kh, Vh)
    if out_weight is not None:
        sp = dict(sp, ssm_conv_b=sp["ssm_conv_b"] + out_weight.pass_on(a_out)[0, 0])

    pad = lambda v: jnp.pad(v, ((0, 0), (0, LANE - v.shape[1])))
    expand = _expand_matrix(cfg)
    dt_bias_pad, a_log_pad = pad(sp["dt_bias"]), pad(sp["a_log"])
    dskip_exp = jnp.repeat(sp["d_skip"], HP, axis=1)
    xc = _colwise("ssm_act", _ssm_act, [xbc], [sp["ssm_conv_w"], sp["ssm_conv_b"]], [F32], [], tc)[0]
    dt_s, a_s, dt_exp, a_exp = _ssd_prep(cfg, dt_raw, dt_bias_pad, a_log_pad, expand)
    y_ssd, hin = _ssd_fwd(cfg, xc, dt_exp, a_exp, a_s, dskip_exp)
    b_out = _ssd_post(cfg, y_ssd, z, sp["ssm_norm_g"])

    ab_out = jnp.concatenate([a_out.astype(BF16), b_out], axis=1)
    if out_weight is not None:
        W = dict(W, **out_weight.arrived(ab_out))
    if ffn_weights is not None:
        sp = dict(sp, mix_post_g=sp["mix_post_g"] + ffn_weights.pass_on(ab_out)[0, 0])
    mix = _matmul("mm_out", ab_out, W["w_out"], "nn", F32)

    def mid(x, mix, g_mp, g_fp):
        x1 = x + mix * _rs(mix) * g_mp
        return x1, x1 * _rs(x1) * g_fp

    x1, h2 = _rowwise("fwd_mid", mid, [x, mix], [sp["mix_post_g"], sp["ffn_pre_g"]], [(D, F32), (D, BF16)], [], ts)
    if ffn_weights is not None:
        W = dict(W, **ffn_weights.arrived(h2))
    gate_pre = _matmul("mm_gate", h2, W["w_gate"], "nn", F32, chips=True)
    if down_weight is not None:
        sp = dict(sp, ffn_conv_b=sp["ffn_conv_b"] + down_weight.pass_on(gate_pre)[0, 0])
    up = _matmul("mm_up", h2, W["w_up"], "nn", F32, chips=True)
    act = _colwise("ffn_act", _ffn_act, [gate_pre, up], [sp["ffn_conv_w"], sp["ffn_conv_b"]], [BF16], [], tc)[0]
    if down_weight is not None:
        W = dict(W, **down_weight.arrived(act))
    f = _matmul("mm_down", act, W["w_down"], "nn", F32)

    def final(x1, f, t, g):
        r = _rs(f)
        fh = f * r
        err = x1 + fh * g - t
        loss = 0.5 * jnp.sum(jnp.mean(err * err, axis=-1, keepdims=True), axis=0, keepdims=True)
        dy = err * (1.0 / D)
        return dy, _rms_back(fh, r, dy * g), _colsum(dy * fh), loss

    dy, df, g_ffn_post, loss = _rowwise("final", final, [x1, f, tgt], [sp["ffn_post_g"]], [(D, F32), (D, BF16)],
                                        [(1, D), (1, LANE)], ts)
    gW = {}
    dact = _matmul("mm_down_dx", df, W["w_down"], "nt", F32)
    gW["w_down"] = _matmul("mm_down_dw", act, df, "tn", BF16)
    dgate, dup, g_ffn_conv_w, g_ffn_conv_b = _colwise(
        "ffn_act_bwd", _ffn_act_back, [dact, gate_pre, up], [sp["ffn_conv_w"], sp["ffn_conv_b"]], [BF16, BF16], [FFN_K, 1], tc)
    gW["w_gate"] = _matmul("mm_gate_dw", h2, dgate, "tn", BF16, chips=True)
    gW["w_up"] = _matmul("mm_up_dw", h2, dup, "tn", BF16, chips=True)
    if ffn_grads_ready is not None:
        sp = dict(sp, ffn_pre_g=sp["ffn_pre_g"] + ffn_grads_ready({n: gW[n] for n in ("w_down", "w_gate", "w_up")})[0, 0])
    dh2 = _matmul("mm_gu_dx", dgate, W["w_gate"], "nt", F32, dup, W["w_up"], chips=True)

    def mid_back(dy, dh2, x1, mix, g_mp, g_fp):
        r2 = _rs(x1)
        xh = x1 * r2
        dx1 = dy + _rms_back(xh, r2, dh2 * g_fp)
        r1 = _rs(mix)
        mh = mix * r1
        return dx1, _rms_back(mh, r1, dx1 * g_mp), _colsum(dh2 * xh), _colsum(dx1 * mh)

    dx1, dmix, g_ffn_pre, g_mix_post = _rowwise("bwd_mid", mid_back, [dy, dh2, x1, mix], [sp["mix_post_g"], sp["ffn_pre_g"]],
                                                [(D, F32), (D, BF16)], [(1, D), (1, D)], ts)
    dab_out = _matmul("mm_out_dx", dmix, W["w_out"], "nt", F32)
    db_out = (dab_out, (INNER, cfg.MLAW // INNER))
    gW["w_out"] = _matmul("mm_out_dw", ab_out, dmix, "tn", BF16)
    early_token = jnp.zeros((8, LANE), F32)
    if early_grads_ready is not None:
        early_token = early_grads_ready({n: gW[n] for n in ("w_down", "w_gate", "w_up", "w_out")})
        sp = dict(sp, ssm_norm_g=sp["ssm_norm_g"] + early_token[0, 0])

    dy_ssd, dz, g_ssm_norm = _ssd_post_bwd(cfg, db_out, y_ssd, z, sp["ssm_norm_g"])
    dxc, ddt_raw, g_dt_bias, g_a_log, g_d_skip = _ssd_bwd(cfg, dy_ssd, xc, dt_exp, a_exp, a_s, dskip_exp, hin, dt_raw,
                                                          dt_bias_pad, a_log_pad, expand)
    dxbc, g_ssm_conv_w, g_ssm_conv_b = _colwise("ssm_act_bwd", _ssm_act_back, [dxc, xbc], [sp["ssm_conv_w"], sp["ssm_conv_b"]],
                                                [BF16], [SSM_K, 1], tc)

    dQ, delta_t = _attn_dq(cfg, Qh, Kh, Vh, dab_out, a_out, lse, early_token)
    dK, dV = _attn_dkv(cfg, Qh, Kh, Vh, dab_out, lse_t, delta_t)
    dq, dkv, dkr = _mla_unpack(cfg, dQ, dK, dV, cos2, sin2)
    dcqn = _matmul("mm_uq_dx", dq, W["w_uq"], "nt", F32)
    dckvn = _matmul("mm_ukv_dx", dkv, W["w_ukv"], "nt", F32)
    gW["w_uq"] = _matmul("mm_uq_dw", cqn, dq, "tn", BF16)
    gW["w_ukv"] = _matmul("mm_ukv_dw", ckvn, dkv, "tn", BF16)

    def rms_back(x, dy, g):
        r = _rs(x)
        xh = x * r
        return _rms_back(xh, r, dy * g), _colsum(dy * xh)

    dc_q, g_q_norm = _rowwise("rms_q_bwd", rms_back, [c_q, dcqn], [sp["q_norm_g"]], [(cfg.QL, BF16)], [(1, cfg.QL)], ts)
    dc_kv, g_kv_norm = _rowwise("rms_kv_bwd", rms_back, [c_kv, dckvn], [sp["kv_norm_g"]], [(cfg.KVL, BF16)], [(1, cfg.KVL)], ts)

    du = dict(c_q=dc_q, c_kv=dc_kv, kr=dkr, z=dz.astype(BF16), xbc=dxbc, dt=ddt_raw.astype(BF16))
    du = jnp.concatenate([du[n] for n in sorted(du, key=lambda n: cfg.seg[n][0])], axis=1)
    assert du.shape[1] == cfg.EXT, "the layout of u has gaps"
    gW["w_in"] = _matmul("mm_in_dw", du, xn, "tn", BF16)
    if in_grad_ready is not None:
        token = in_grad_ready({n: gW[n] for n in ("w_in", "w_uq", "w_ukv")})
        sp = dict(sp, mix_pre_g=sp["mix_pre_g"] + token[0, 0])
    dxn = _matmul("mm_in_dx", du, W["w_in"], "nn", F32)

    def first_back(dx1, dxn, x, g):
        r = _rs(x)
        xh = x * r
        return dx1 + _rms_back(xh, r, dxn * g), _colsum(dxn * xh)

    grad_x, g_mix_pre = _rowwise("bwd_first", first_back, [dx1, dxn, x], [sp["mix_pre_g"]], [(D, F32)], [(1, D)], ts)

    gs = dict(mix_pre_g=g_mix_pre, q_norm_g=g_q_norm, kv_norm_g=g_kv_norm, ssm_conv_w=g_ssm_conv_w, ssm_conv_b=g_ssm_conv_b,
              dt_bias=g_dt_bias[:, :cfg.HS], a_log=g_a_log[:, :cfg.HS], d_skip=g_d_skip[:, :cfg.HS], ssm_norm_g=g_ssm_norm,
              mix_post_g=g_mix_post, ffn_pre_g=g_ffn_pre, ffn_conv_w=g_ffn_conv_w, ffn_conv_b=g_ffn_conv_b,
              ffn_post_g=g_ffn_post)
    return loss, grad_x, gW, gs


def _to_kernel_layout(cfg, name, w):
    if name == "w_in":
        parts, at = [], 0
        for off, width, n_off, n_width in sorted(cfg.seg.values()):
            parts += [jnp.zeros((off - at, w.shape[1]), w.dtype), w[n_off:n_off + n_width],
                      jnp.zeros((width - n_width, w.shape[1]), w.dtype)]
            at = off + width
        parts.append(jnp.zeros((cfg.EXT - at, w.shape[1]), w.dtype))
        return jnp.concatenate([p for p in parts if p.shape[0]], axis=0)
    if name in ("w_uq", "w_ukv"):
        per = NOPE + (ROPE if name == "w_uq" else VH)
        return jnp.concatenate([w[:, h * per:h * per + NOPE] for h in range(cfg.H)]
                               + [w[:, h * per + NOPE:(h + 1) * per] for h in range(cfg.H)], axis=1)
    return w


def _from_kernel_layout(cfg, name, g):
    if name == "w_in":
        return jnp.concatenate([g[off:off + n_width] for off, _, _, n_width in sorted(cfg.seg.values(), key=lambda s: s[2])], axis=0)
    if name in ("w_uq", "w_ukv"):
        second = ROPE if name == "w_uq" else VH
        base = cfg.H * NOPE
        parts = []
        for h in range(cfg.H):
            parts += [g[:, h * NOPE:(h + 1) * NOPE], g[:, base + h * second:base + (h + 1) * second]]
        return jnp.concatenate(parts, axis=1)
    return g


def _cols_to_chips(w):
    r, c = w.shape
    return w.reshape(r, N_CHIPS, c // N_CHIPS).transpose(1, 0, 2)


def _chips_to_cols(g):
    k, r, cs = g.shape
    return g.transpose(1, 0, 2).reshape(r, k * cs)


_CHIP_MAJOR = ("w_gate", "w_up")
_RELAYOUT = ("w_uq", "w_ukv")
_LAYOUT_ROWS = 256


def _w_in_layout(cfg, wg):
    _, rs, d = wg.shape
    tc = _pick(d, _LAYOUT_ROWS, LANE)

    def body(w_ref, o_ref):
        o_ref[...] = _to_kernel_layout(cfg, "w_in", jnp.concatenate([w_ref[k] for k in range(N_CHIPS)], axis=0))

    return pl.pallas_call(
        body, name="layout_w_in", grid=(d // tc,),
        in_specs=[pl.BlockSpec((N_CHIPS, rs, tc), lambda j: (0, 0, j))], out_specs=pl.BlockSpec((cfg.EXT, tc), lambda j: (0, j)),
        out_shape=jax.ShapeDtypeStruct((cfg.EXT, d), wg.dtype), compiler_params=_params(("parallel",)),
    )(wg)


def _w_in_grad_to_chips(cfg, g):
    _, d = g.shape
    rs = cfg.IN_COLS // N_CHIPS
    tc = _pick(d, _LAYOUT_ROWS, LANE)

    def body(g_ref, o_ref):
        nat = _from_kernel_layout(cfg, "w_in", g_ref[...])
        for k in range(N_CHIPS):
            o_ref[k] = nat[k * rs:(k + 1) * rs]

    return pl.pallas_call(
        body, name="layout_grad_w_in", grid=(d // tc,),
        in_specs=[pl.BlockSpec((cfg.EXT, tc), lambda j: (0, j))], out_specs=pl.BlockSpec((N_CHIPS, rs, tc), lambda j: (0, 0, j)),
        out_shape=jax.ShapeDtypeStruct((N_CHIPS, rs, d), g.dtype), compiler_params=_params(("parallel",)),
    )(g)


def _gathered_to_kernel(cfg, name, wg):
    if name in _CHIP_MAJOR:
        return wg
    if name == "w_in":
        return _w_in_layout(cfg, wg)
    if name not in _RELAYOUT:
        return wg.reshape(wg.shape[0] * wg.shape[1], wg.shape[2])
    _, rows, cs = wg.shape
    tr = _pick(rows, _LAYOUT_ROWS, 16)

    def body(w_ref, o_ref):
        o_ref[...] = _to_kernel_layout(cfg, name, jnp.concatenate([w_ref[k] for k in range(N_CHIPS)], axis=1))

    wide = jax.eval_shape(lambda w: _to_kernel_layout(cfg, name, w), jax.ShapeDtypeStruct((rows, N_CHIPS * cs), wg.dtype)).shape[1]
    return pl.pallas_call(
        body, name="layout_" + name, grid=(rows // tr,),
        in_specs=[pl.BlockSpec((N_CHIPS, tr, cs), lambda i: (0, i, 0))], out_specs=pl.BlockSpec((tr, wide), lambda i: (i, 0)),
        out_shape=jax.ShapeDtypeStruct((rows, wide), wg.dtype), compiler_params=_params(("parallel",)),
    )(wg)


def _grad_to_chips(cfg, name, g):
    if name in _CHIP_MAJOR:
        return g
    if name == "w_in":
        return _w_in_grad_to_chips(cfg, g)
    if name not in _RELAYOUT:
        return g.reshape(N_CHIPS, g.shape[0] // N_CHIPS, g.shape[1])
    rows, wide = g.shape
    tr = _pick(rows, _LAYOUT_ROWS, 16)
    cs = jax.eval_shape(lambda v: _from_kernel_layout(cfg, name, v), g).shape[1] // N_CHIPS

    def body(g_ref, o_ref):
        nat = _from_kernel_layout(cfg, name, g_ref[...])
        for k in range(N_CHIPS):
            o_ref[k] = nat[:, k * cs:(k + 1) * cs]

    return pl.pallas_call(
        body, name="layout_grad_" + name, grid=(rows // tr,),
        in_specs=[pl.BlockSpec((tr, wide), lambda i: (i, 0))], out_specs=pl.BlockSpec((N_CHIPS, tr, cs), lambda i: (0, i, 0)),
        out_shape=jax.ShapeDtypeStruct((N_CHIPS, rows, cs), g.dtype), compiler_params=_params(("parallel",)),
    )(g)


def _me():
    return lax.axis_index("x"), lax.axis_index("y"), lax.axis_index("c")


def _other_chips(x, y):
    return [(1 - x, y), (x, 1 - y), (1 - x, 1 - y)]


_ANY = pl.BlockSpec(memory_space=pl.ANY)


BLOCK_ELEMS = 1 << 19
BLOCK_ELEMS_FEW = 1 << 20


def _row_block(rows, cols, mult, elems=BLOCK_ELEMS):
    return _pick(rows, max(mult, elems // cols // mult * mult), mult)


def _scalar(v):
    return v.astype(I32).reshape(1)


def _blocks2d(r, c, mult, elems=BLOCK_ELEMS):
    if r % mult == 0:
        tr = _row_block(r, c, mult, elems)
        return (tr, c), r // tr, lambda i: (i, 0)
    tc = _pick(c, max(LANE, elems // r // LANE * LANE), LANE)
    return (r, tc), c // tc, lambda i: (0, i)


def _by_rows(rows):
    return rows % 32 == 0


def _half_shape(rows, cols):
    return (rows // 2, cols) if _by_rows(rows) else (rows, cols // 2)


def _half_blocks(rows, cols, mult, elems=BLOCK_ELEMS):
    hr, hc = _half_shape(rows, cols)
    block, n, part = _blocks2d(hr, hc, mult, elems)
    assert (hr % mult == 0) == _by_rows(rows), (rows, cols, mult)
    full = (lambda h, i: (h * n + i, 0)) if _by_rows(rows) else (lambda h, i: (0, h * n + i))
    return block, n, full, part


def _half(ref, k, half):
    hr, hc = _half_shape(ref.shape[1], ref.shape[2])
    if _by_rows(ref.shape[1]):
        return ref.at[k, pl.ds(pl.multiple_of(half * hr, 16), hr), :]
    return ref.at[k, :, pl.ds(pl.multiple_of(half * hc, LANE), hc)]


def _shard_blocks(w, br, bc):
    if w.shape[0] == 1:
        def write(ref, v):
            ref[...] = v
        return (lambda f: pl.BlockSpec((None, br, bc), lambda *a: (0, *f(*a)))), (lambda ref: ref[...]), write
    assert w.shape[1] == 1 and br == w.shape[0], w.shape

    def write_rows(ref, v):
        ref[:, 0, :] = v
    return (lambda f: pl.BlockSpec((br, 1, bc), lambda *a: (0, 0, f(*a)[1]))), (lambda ref: ref[:, 0, :]), write_rows


def _stage_shard(name, w, chip, after=None):
    rs, cs = w.shape[0] * w.shape[1], w.shape[2]
    (br, bc), n, idx = _blocks2d(rs, cs, 16, BLOCK_ELEMS_FEW)
    spec, get, _ = _shard_blocks(w, br, bc)

    def body(chip_ref, w_ref, *refs):
        refs[-1][...] = get(w_ref).astype(BF16)

    return pl.pallas_call(
        body, name="stage_" + name,
        grid_spec=pltpu.PrefetchScalarGridSpec(
            num_scalar_prefetch=1, grid=(n,),
            in_specs=[spec(lambda i, chip_ref: idx(i))] + ([] if after is None else [_ANY]),
            out_specs=pl.BlockSpec((None, br, bc), lambda i, chip_ref: (chip_ref[0], *idx(i)))),
        out_shape=jax.ShapeDtypeStruct((N_CHIPS, rs, cs), BF16),
        compiler_params=_params(("parallel",)),
    )(_scalar(chip), w, *([] if after is None else [after]))


_HBM = pl.BlockSpec(memory_space=pltpu.HBM)
_SEM = pl.BlockSpec(memory_space=pltpu.SEMAPHORE)
_EFFECT = pltpu.SideEffectType.DATAFLOW_SIDE_EFFECTING


def _split_start(name, bufs, n_copies, copies, after):
    n = len(bufs)

    def body(*refs):
        for cp in copies(refs[:n], refs[n + 1], refs[n + 2]):
            cp.start()
        refs[-1][...] = jnp.zeros_like(refs[-1])

    res = pl.pallas_call(
        body, name=name,
        out_shape=(pltpu.SemaphoreType.DMA((n_copies,)), pltpu.SemaphoreType.DMA((n_copies,)),
                   *[pltpu.HBM(b.shape, b.dtype) for b in bufs], jax.ShapeDtypeStruct((8, LANE), F32)),
        in_specs=[_HBM] * n + [_ANY], out_specs=(_SEM, _SEM, *[_HBM] * n, pl.BlockSpec(memory_space=pltpu.VMEM)),
        input_output_aliases={i: 2 + i for i in range(n)},
        compiler_params=pltpu.CompilerParams(has_side_effects=_EFFECT),
    )(*[pltpu.with_memory_space_constraint(b, pltpu.HBM) for b in bufs], after)
    return res[0], res[1], list(res[2:2 + n]), res[-1]


def _split_wait(name, send_sems, recv_sems, bufs, after, copies):
    n = len(bufs)

    def body(*refs):
        for cp in copies(refs[:n], refs[n], refs[n + 1]):
            cp.wait_send()
            cp.wait_recv()

    return list(pl.pallas_call(
        body, name=name, out_shape=[pltpu.HBM(b.shape, b.dtype) for b in bufs],
        in_specs=[_HBM] * n + [_SEM, _SEM, _ANY], out_specs=[_HBM] * n,
        input_output_aliases={i: i for i in range(n)},
        compiler_params=pltpu.CompilerParams(has_side_effects=_EFFECT),
    )(*bufs, send_sems, recv_sems, after))


def _gather_to_chips(bufs, send_sems, recv_sems):
    x, y, c = _me()
    return [pltpu.make_async_remote_copy(src_ref=_half(b, 2 * x + y, c), dst_ref=_half(b, 2 * x + y, c),
                                         send_sem=send_sems.at[3 * w + j], recv_sem=recv_sems.at[3 * w + j],
                                         device_id=(cx, cy, c), device_id_type=MESH_ID)
            for w, b in enumerate(bufs) for j, (cx, cy) in enumerate(_other_chips(x, y))]


def _gather_to_sibling(bufs, send_sems, recv_sems):
    x, y, c = _me()
    return [pltpu.make_async_remote_copy(src_ref=_half(b, 2 * cx + cy, c), dst_ref=_half(b, 2 * cx + cy, c),
                                         send_sem=send_sems.at[3 * w + j], recv_sem=recv_sems.at[3 * w + j],
                                         device_id=(x, y, 1 - c), device_id_type=MESH_ID)
            for w, b in enumerate(bufs) for j, (cx, cy) in enumerate(_other_chips(x, y))]


def _pair_exchange(name, grads):
    n = len(grads)

    def body(*refs):
        ins, outs, send_sems, recv_sems = refs[:n], refs[n:2 * n], refs[2 * n], refs[2 * n + 1]
        x, y, c = _me()
        cps = []
        for w, (g_ref, o_ref) in enumerate(zip(ins, outs)):
            cps.append(pltpu.make_async_remote_copy(src_ref=_half(g_ref, slice(None), 1 - c), dst_ref=o_ref,
                                                    send_sem=send_sems.at[w], recv_sem=recv_sems.at[w],
                                                    device_id=(x, y, 1 - c), device_id_type=MESH_ID))
            cps[-1].start()
        for cp in cps:
            cp.wait()

    return pl.pallas_call(
        body, name="pair_exchange_" + name, in_specs=[_ANY] * n, out_specs=[_ANY] * n,
        out_shape=[jax.ShapeDtypeStruct((g.shape[0], *_half_shape(g.shape[1], g.shape[2])), g.dtype) for g in grads],
        scratch_shapes=[pltpu.SemaphoreType.DMA((n,)), pltpu.SemaphoreType.DMA((n,))],
    )(*grads)


def _pair_copies(grads, lands, send_sems, recv_sems):
    x, y, c = _me()
    return [pltpu.make_async_remote_copy(src_ref=_half(g_ref, slice(None), 1 - c), dst_ref=l_ref, send_sem=send_sems.at[w],
                                         recv_sem=recv_sems.at[w], device_id=(x, y, 1 - c), device_id_type=MESH_ID)
            for w, (g_ref, l_ref) in enumerate(zip(grads, lands))]


def _pair_exchange_start(name, grads):
    n = len(grads)
    lands = [lax.empty((g.shape[0], *_half_shape(g.shape[1], g.shape[2])), g.dtype) for g in grads]
    send_sems, recv_sems, bufs, token = _split_start(
        "pair_exchange_start_" + name, [*grads, *lands], n, lambda refs, ss, rs: _pair_copies(refs[:n], refs[n:], ss, rs),
        jnp.zeros((8, LANE), F32))
    return (send_sems, recv_sems, bufs), token


def _pair_exchange_wait(name, state, after):
    send_sems, recv_sems, bufs = state
    n = len(bufs) // 2
    bufs = _split_wait("pair_exchange_wait_" + name, send_sems, recv_sems, bufs, after,
                       lambda refs, ss, rs: _pair_copies(refs[:n], refs[n:], ss, rs))
    return bufs[:n], bufs[n:]


def _pair_sum(name, g, theirs, c):
    (br, bc), nb, full, part = _half_blocks(g.shape[1], g.shape[2], 16, 2 * BLOCK_ELEMS_FEW)

    def body(c_ref, a_ref, b_ref, o_ref):
        o_ref[...] = (a_ref[...].astype(F32) + b_ref[...].astype(F32)).astype(o_ref.dtype)

    return pl.pallas_call(
        body, name="pair_sum_" + name,
        grid_spec=pltpu.PrefetchScalarGridSpec(
            num_scalar_prefetch=1, grid=(N_CHIPS, nb),
            in_specs=[pl.BlockSpec((None, br, bc), lambda k, i, c_ref: (k, *full(c_ref[0], i))),
                      pl.BlockSpec((None, br, bc), lambda k, i, c_ref: (k, *part(i)))],
            out_specs=pl.BlockSpec((None, br, bc), lambda k, i, c_ref: (k, *part(i)))),
        out_shape=jax.ShapeDtypeStruct(theirs.shape, BF16),
        compiler_params=_params(("parallel", "parallel")),
    )(_scalar(c), g, theirs)


def _chip_copies(srcs, lands, send_sems, recv_sems):
    x, y, c = _me()
    return [pltpu.make_async_remote_copy(src_ref=s_ref.at[2 * cx + cy], dst_ref=l_ref.at[j], send_sem=send_sems.at[3 * w + j],
                                         recv_sem=recv_sems.at[3 * w + j], device_id=(cx, cy, c), device_id_type=MESH_ID)
            for w, (s_ref, l_ref) in enumerate(zip(srcs, lands)) for j, (cx, cy) in enumerate(_other_chips(x, y))]


def _chip_exchange_start(name, sums):
    n = len(sums)
    lands = [lax.empty((3,) + s.shape[1:], s.dtype) for s in sums]
    send_sems, recv_sems, bufs, token = _split_start(
        "chip_exchange_start_" + name, [*sums, *lands], 3 * n, lambda refs, ss, rs: _chip_copies(refs[:n], refs[n:], ss, rs),
        jnp.zeros((8, LANE), F32))
    return send_sems, recv_sems, bufs[:n], bufs[n:], token


def _chip_exchange_wait(name, send_sems, recv_sems, sums, lands, after):
    n = len(sums)
    bufs = _split_wait("chip_exchange_wait_" + name, send_sems, recv_sems, [*sums, *lands], after,
                       lambda refs, ss, rs: _chip_copies(refs[:n], refs[n:], ss, rs))
    return bufs[:n], bufs[n:]


def _chip_sum(name, sums, theirs, chip):
    _, h, cs = sums.shape
    (br, bc), nb, idx = _blocks2d(h, cs, 16, BLOCK_ELEMS_FEW)

    def body(chip_ref, s_ref, t_ref, o_ref):
        acc = s_ref[...].astype(F32)
        for k in range(3):
            acc = acc + t_ref[k].astype(F32)
        o_ref[...] = acc

    return pl.pallas_call(
        body, name="chip_sum_" + name,
        grid_spec=pltpu.PrefetchScalarGridSpec(
            num_scalar_prefetch=1, grid=(nb,),
            in_specs=[pl.BlockSpec((None, br, bc), lambda i, chip_ref: (chip_ref[0], *idx(i))),
                      pl.BlockSpec((3, br, bc), lambda i, chip_ref: (0, *idx(i)))],
            out_specs=pl.BlockSpec((br, bc), lambda i, chip_ref: idx(i))),
        out_shape=jax.ShapeDtypeStruct((h, cs), F32),
        compiler_params=_params(("parallel",)),
    )(_scalar(chip), sums, theirs)


def _sibling_copies(halves, lands, send_sems, recv_sems):
    x, y, c = _me()
    return [pltpu.make_async_remote_copy(src_ref=h_ref, dst_ref=l_ref, send_sem=send_sems.at[w], recv_sem=recv_sems.at[w],
                                         device_id=(x, y, 1 - c), device_id_type=MESH_ID)
            for w, (h_ref, l_ref) in enumerate(zip(halves, lands))]


def _sibling_exchange_start(name, halves, after):
    n = len(halves)
    lands = [lax.empty(h.shape, h.dtype) for h in halves]
    send_sems, recv_sems, bufs, token = _split_start(
        "sibling_exchange_start_" + name, [*halves, *lands], n, lambda refs, ss, rs: _sibling_copies(refs[:n], refs[n:], ss, rs),
        after)
    return (send_sems, recv_sems, bufs), token


def _sibling_exchange_wait(name, state, after):
    send_sems, recv_sems, bufs = state
    n = len(bufs) // 2
    bufs = _split_wait("sibling_exchange_wait_" + name, send_sems, recv_sems, bufs, after,
                       lambda refs, ss, rs: _sibling_copies(refs[:n], refs[n:], ss, rs))
    return bufs[:n], bufs[n:]


def _sibling_exchange(name, halves):
    n = len(halves)

    def body(*refs):
        ins, outs, send_sems, recv_sems = refs[:n], refs[n:2 * n], refs[2 * n], refs[2 * n + 1]
        x, y, c = _me()
        cps = []
        for w, (h_ref, o_ref) in enumerate(zip(ins, outs)):
            cps.append(pltpu.make_async_remote_copy(src_ref=h_ref, dst_ref=o_ref, send_sem=send_sems.at[w], recv_sem=recv_sems.at[w],
                                                    device_id=(x, y, 1 - c), device_id_type=MESH_ID))
            cps[-1].start()
        for cp in cps:
            cp.wait()

    return pl.pallas_call(
        body, name="sibling_exchange_" + name, in_specs=[_ANY] * n, out_specs=[_ANY] * n,
        out_shape=[jax.ShapeDtypeStruct(h.shape, h.dtype) for h in halves],
        scratch_shapes=[pltpu.SemaphoreType.DMA((n,)), pltpu.SemaphoreType.DMA((n,))],
    )(*halves)


def _allreduce_small(name, vec, after):
    def body(v_ref, after_ref, o_ref, buf_ref, send_sems, recv_sems):
        x, y, c = _me()
        me = 4 * x + 2 * y + c
        cps = []
        for p in range(1, 8):
            px, py, pc = x ^ (p >> 2), y ^ ((p >> 1) & 1), c ^ (p & 1)
            cps.append(pltpu.make_async_remote_copy(src_ref=v_ref, dst_ref=buf_ref.at[me], send_sem=send_sems.at[p - 1],
                                                    recv_sem=recv_sems.at[p - 1], device_id=(px, py, pc), device_id_type=MESH_ID))
            cps[-1].start()
        buf_ref[me] = v_ref[...]
        for p in range(1, 8):
            theirs = buf_ref.at[me ^ p]
            pltpu.make_async_remote_copy(src_ref=theirs, dst_ref=theirs, send_sem=send_sems.at[p - 1], recv_sem=recv_sems.at[p - 1],
                                         device_id=(x, y, c), device_id_type=MESH_ID).wait_recv()
        for cp in cps:
            cp.wait_send()
        acc = buf_ref[0]
        for k in range(1, 8):
            acc = acc + buf_ref[k]
        o_ref[...] = acc

    vm = pl.BlockSpec(memory_space=pltpu.VMEM)
    return pl.pallas_call(
        body, name=name, in_specs=[vm, _ANY], out_specs=vm, out_shape=jax.ShapeDtypeStruct(vec.shape, F32),
        scratch_shapes=[pltpu.VMEM((8,) + vec.shape, F32), pltpu.SemaphoreType.DMA((7,)), pltpu.SemaphoreType.DMA((7,))],
    )(vec, after)


def _adam_math(w, g, m, v):
    m = ADAM_B1 * m + (1.0 - ADAM_B1) * g
    v = ADAM_B2 * v + (1.0 - ADAM_B2) * (g * g)
    m_hat = m / (1.0 - ADAM_B1 ** ADAM_STEP)
    v_hat = v / (1.0 - ADAM_B2 ** ADAM_STEP)
    return -ADAM_LR * (m_hat / (jnp.sqrt(v_hat) + ADAM_EPS) + ADAM_WD * w), m, v


def _adamw(name, w, g, m, v):
    R, C = w.shape
    tr = _row_block(R, C, 8)

    def body(w_ref, g_ref, m_ref, v_ref, d_ref, nm_ref, nv_ref):
        d_ref[...], nm_ref[...], nv_ref[...] = _adam_math(w_ref[...], g_ref[...], m_ref[...], v_ref[...])

    blk = pl.BlockSpec((tr, C), lambda i: (i, 0))
    return pl.pallas_call(
        body, name=name, grid=(R // tr,), in_specs=[blk] * 4, out_specs=[blk] * 3,
        out_shape=[jax.ShapeDtypeStruct((R, C), F32)] * 3, compiler_params=_params(("parallel",)),
    )(w, g, m, v)


def _adamw_halves(name, w, mine, theirs, m, v, c):
    rs, cs = w.shape[0] * w.shape[1], w.shape[2]
    (br, bc), nb, whole, half = _half_blocks(rs, cs, 8)
    spec, get, put = _shard_blocks(w, br, bc)

    def body(c_ref, w_ref, a_ref, b_ref, m_ref, v_ref, g_ref, d_ref, nm_ref, nv_ref):
        g = jnp.where(pl.program_id(0) == c_ref[0], a_ref[...], b_ref[...])
        put(g_ref, g)
        for ref, val in zip((d_ref, nm_ref, nv_ref), _adam_math(get(w_ref), g, get(m_ref), get(v_ref))):
            put(ref, val)

    full = spec(lambda s, i, c_ref: whole(s, i))
    part = pl.BlockSpec((br, bc), lambda s, i, c_ref: half(i))
    return pl.pallas_call(
        body, name=name,
        grid_spec=pltpu.PrefetchScalarGridSpec(num_scalar_prefetch=1, grid=(2, nb), in_specs=[full, part, part, full, full],
                                               out_specs=[full] * 4),
        out_shape=[jax.ShapeDtypeStruct(w.shape, F32)] * 4, compiler_params=_params(("parallel", "parallel")),
    )(_scalar(c), w, mine, theirs, m, v)


def _pack_small(arrs, lanes=LANE):
    flat = jnp.concatenate([a.reshape(-1) for a in arrs])
    n = -(-flat.shape[0] // (8 * lanes)) * 8 * lanes
    return jnp.pad(flat, (0, n - flat.shape[0])).reshape(8, n // 8)


def _unpack_small(vec, shapes):
    flat, out, off = vec.reshape(-1), [], 0
    for s in shapes:
        out.append(flat[off:off + s[0] * s[1]].reshape(s))
        off += s[0] * s[1]
    return out


class _LateWeights:
    def __init__(self, cfg, tag, names, staged, after):
        self.cfg, self.tag, self.names, self.k = cfg, tag, names, 3 * len(names)
        self.send, self.recv, self.bufs, self.token = _split_start(f"gather_{tag}_chips_start", staged, self.k, _gather_to_chips,
                                                                    after)

    def pass_on(self, after):
        bufs = _split_wait(f"gather_{self.tag}_chips_wait", self.send, self.recv, self.bufs, after, _gather_to_chips)
        self.send, self.recv, self.bufs, token = _split_start(f"gather_{self.tag}_sibling_start", bufs, self.k, _gather_to_sibling,
                                                               self.token)
        return token

    def arrived(self, after):
        bufs = _split_wait(f"gather_{self.tag}_sibling_wait", self.send, self.recv, self.bufs, after, _gather_to_sibling)
        return {n: _gathered_to_kernel(self.cfg, n, b) for n, b in zip(self.names, bufs)}


def _step(cfg, a):
    chip = 2 * lax.axis_index("x") + lax.axis_index("y")
    core = lax.axis_index("c")
    big = BIG

    ffn = ("w_gate", "w_up", "w_down")
    first = ("w_in", "w_uq", "w_ukv")
    sp = {n: a[n] for n in SMALL}
    sharded = _pack_small([a[n] for n in SMALL_SHARDED], 2 * LANE)
    slabs = jnp.where(lax.broadcasted_iota(I32, (N_CHIPS,) + sharded.shape, 0) == chip, sharded[None], 0.0)
    staged = {"w_in": _stage_shard("w_in", a["w_in"], chip)}
    in_weight = _LateWeights(cfg, "in", ("w_in", "sharded_small"), [staged["w_in"], slabs], jnp.zeros((8, LANE), F32))
    staged.update({n: _stage_shard(n, a[n], chip, in_weight.token) for n in big if n != "w_in"})
    in_weight.pass_on(staged[big[-1]])
    W = in_weight.arrived(sp["mix_pre_g"])
    allp = W.pop("sharded_small").reshape((N_CHIPS,) + sharded.shape)
    per_chip = [_unpack_small(allp[ch], [a[n].shape for n in SMALL_SHARDED]) for ch in range(N_CHIPS)]
    for k, n in enumerate(SMALL_SHARDED):
        sp[n] = jnp.concatenate([per_chip[ch][k] for ch in range(N_CHIPS)], axis=1)

    mla_weights = _LateWeights(cfg, "mla", first[1:], [staged[n] for n in first[1:]], W["w_in"])
    out_weight = _LateWeights(cfg, "out", ("w_out",), [staged["w_out"]], mla_weights.token)
    ffn_weights = _LateWeights(cfg, "ffn", ffn[:2], [staged[n] for n in ffn[:2]], out_weight.token)
    down_weight = _LateWeights(cfg, "down", ffn[2:], [staged[n] for n in ffn[2:]], ffn_weights.token)
    sp["mix_pre_g"] = sp["mix_pre_g"] + (mla_weights.token[0, 0] + out_weight.token[0, 0] + ffn_weights.token[0, 0]
                                         + down_weight.token[0, 0])

    state = {}

    def ffn_grads_ready(grads):
        state["ffn_pairs"], token = _pair_exchange_start("ffn", [_grad_to_chips(cfg, n, grads[n]) for n in ffn_grads])
        return token

    def pair_sums(names, grads, theirs):
        return [_pair_sum(n, g, t, core) for n, g, t in zip(names, grads, theirs)]

    def early_grads_ready(grads):
        g_out = [_grad_to_chips(cfg, "w_out", grads["w_out"])]
        g_ffn, t_ffn = _pair_exchange_wait("ffn", state["ffn_pairs"], g_out[0])
        sums = pair_sums(ffn_grads, g_ffn, t_ffn) + pair_sums(["w_out"], g_out, _pair_exchange("out", g_out))
        state["early"] = _chip_exchange_start("early", sums)
        return state["early"][-1]

    def reduced_halves(tag, names, after):
        send_sems, recv_sems, s_bufs, l_bufs, _ = state[tag]
        s_bufs, l_bufs = _chip_exchange_wait(tag, send_sems, recv_sems, s_bufs, l_bufs, after)
        return [_chip_sum(n, s, t, chip) for n, s, t in zip(names, s_bufs, l_bufs)]

    def in_grad_ready(grads):
        grads = [_grad_to_chips(cfg, n, grads[n]) for n in first]
        state["rest"] = _chip_exchange_start("rest", pair_sums(first, grads, _pair_exchange("rest", grads)))
        return state["rest"][-1]

    ffn_grads = ("w_down", "w_gate", "w_up")
    early = ffn_grads + ("w_out",)
    loss, grad_x, gW, gs = _local_grads(cfg, a["x"], a["loss_target"], W, sp, mla_weights, out_weight, ffn_weights, down_weight,
                                        ffn_grads_ready, early_grads_ready, in_grad_ready)
    out = {"grad_x": grad_x}

    def adamw(names, mine, theirs):
        for n, gm, gt in zip(names, mine, theirs):
            out["grad_" + n], out["delta_" + n], out["new_m_" + n], out["new_v_" + n] = _adamw_halves(
                "adamw_" + n, a[n], gm, gt, a["m_" + n], a["v_" + n], core)

    mine = reduced_halves("early", early, grad_x)
    theirs = _sibling_exchange("early", mine[:1])
    later, _ = _sibling_exchange_start("early", mine[1:], theirs[0])
    adamw(early[:1], mine[:1], theirs)
    adamw(early[1:], *_sibling_exchange_wait("early", later, out["new_v_" + early[0]]))
    mine = reduced_halves("rest", first, out["new_v_" + early[-1]])
    theirs = _sibling_exchange("rest", mine)
    adamw(first, mine, theirs)

    shapes = [gs[n].shape for n in SMALL] + [(1, LANE)]
    red = _unpack_small(_allreduce_small("allreduce_small", _pack_small([gs[n] for n in SMALL] + [loss]), theirs[0]), shapes)
    g_small = dict(zip(SMALL, red[:-1]))
    for n in SMALL_SHARDED:
        cs = a[n].shape[1]
        g_small[n] = lax.dynamic_slice_in_dim(g_small[n], chip * cs, cs, axis=1)
    out["loss"] = red[-1][0, 0]
    sshapes = [a[n].shape for n in SMALL]
    d, nm, nv = _adamw("adamw_small", _pack_small([a[n] for n in SMALL]), _pack_small([g_small[n] for n in SMALL]),
                       _pack_small([a["m_" + n] for n in SMALL]), _pack_small([a["v_" + n] for n in SMALL]))
    for n, dd, mm, vv in zip(SMALL, _unpack_small(d, sshapes), _unpack_small(nm, sshapes), _unpack_small(nv, sshapes)):
        out["grad_" + n], out["delta_" + n], out["new_m_" + n], out["new_v_" + n] = g_small[n], dd, mm, vv
    return out


def kernel(x, mix_pre_g, w_in, q_norm_g, w_uq, kv_norm_g, w_ukv, ssm_conv_w, ssm_conv_b, dt_bias, a_log, d_skip, ssm_norm_g, w_out, mix_post_g, ffn_pre_g, w_gate, w_up, ffn_conv_w, ffn_conv_b, w_down, ffn_post_g, loss_target, m_mix_pre_g, m_w_in, m_q_norm_g, m_w_uq, m_kv_norm_g, m_w_ukv, m_ssm_conv_w, m_ssm_conv_b, m_dt_bias, m_a_log, m_d_skip, m_ssm_norm_g, m_w_out, m_mix_post_g, m_ffn_pre_g, m_w_gate, m_w_up, m_ffn_conv_w, m_ffn_conv_b, m_w_down, m_ffn_post_g, v_mix_pre_g, v_w_in, v_q_norm_g, v_w_uq, v_kv_norm_g, v_w_ukv, v_ssm_conv_w, v_ssm_conv_b, v_dt_bias, v_a_log, v_d_skip, v_ssm_norm_g, v_w_out, v_mix_post_g, v_ffn_pre_g, v_w_gate, v_w_up, v_ffn_conv_w, v_ffn_conv_b, v_w_down, v_ffn_post_g):
    args = dict(locals())
    def given(k, v):
        if k in ("w_in", "m_w_in", "v_w_in"):
            return jnp.transpose(v, (2, 0, 1))
        return v if k.removeprefix("m_").removeprefix("v_") in BIG or v.ndim < 3 else v[0]

    out = _step(_FULL, {k: given(k, v) for k, v in args.items()})
    res = [out["loss"], out["grad_x"][None]]
    for pre in ("grad_", "delta_", "new_m_", "new_v_"):
        for n in WEIGHTS:
            o = out[pre + n]
            res.append(jnp.transpose(o, (1, 2, 0)) if n == "w_in" else o if n in BIG or args[n].ndim < 3 else o[None])
    return tuple(res)
```

```python
import math

import jax
import jax.numpy as jnp
from jax import lax
from jax.experimental import pallas as pl
from jax.experimental.pallas import tpu as pltpu

F32, BF16, I32 = jnp.float32, jnp.bfloat16, jnp.int32
NN = (((1,), (0,)), ((), ()))
NT = (((1,), (1,)), ((), ()))
TN = (((0,), (0,)), ((), ()))
HI = lax.Precision.HIGHEST
MESH_ID = pl.DeviceIdType.MESH

EPS = 1e-6
CHUNK = 64
NOPE, ROPE, VH = 128, 64, 128
ROPE_THETA = 10000.0
HP, NST = 64, 128
SSM_K, FFN_K = 4, 3
LANE = 128
N_CHIPS = 4
VMEM_LIMIT = 52 * 1024 * 1024
MM_TILE, MM_TILE_K = 1408, 2816

ADAM_LR, ADAM_B1, ADAM_B2, ADAM_EPS, ADAM_WD, ADAM_STEP = 0.001, 0.9, 0.999, 1e-08, 0.01, 10


class _Cfg:
    def __init__(self, S, D, QL, KVL, H, HS, G, DFF, T):
        self.S, self.D, self.QL, self.KVL, self.H, self.HS, self.G, self.DFF, self.T = S, D, QL, KVL, H, HS, G, DFF, T
        self.INNER = HS * HP
        self.CONVCH = self.INNER + 2 * G * NST
        self.QW = H * (NOPE + ROPE)
        self.KVW = H * (NOPE + VH)
        self.MLAW = H * VH
        self.MIXW = self.MLAW + self.INNER
        self.IN_COLS = QL + KVL + ROPE + self.INNER + self.CONVCH + HS
        natural, at = {}, 0
        for name, w in (("c_q", QL), ("c_kv", KVL), ("kr", ROPE), ("z", self.INNER), ("xbc", self.CONVCH), ("dt", HS)):
            natural[name] = (at, w)
            at += w
        self.seg, taken = {}, []
        for name in sorted(natural, key=lambda n: -natural[n][1]):
            w = -(-natural[name][1] // LANE) * LANE
            off = next(o for o in range(0, self.IN_COLS * 2, w) if all(o + w <= t or o >= t + tw for t, tw in taken))
            taken.append((off, w))
            self.seg[name] = (off, w) + natural[name]
        self.EXT = max(o + w for o, w in taken)
        self.NPAIR = HS // 2
        self.REP = HS // G

    def window(self, name):
        off, w, _, _ = self.seg[name]
        return w, off // w


_FULL = _Cfg(S=2048, D=2048, QL=768, KVL=512, H=8, HS=16, G=2, DFF=5632, T=256)
BIG = ("w_in", "w_uq", "w_ukv", "w_out", "w_gate", "w_up", "w_down")

SMALL = ("mix_pre_g", "q_norm_g", "kv_norm_g", "ssm_conv_w", "ssm_conv_b", "dt_bias", "a_log", "d_skip", "ssm_norm_g",
         "mix_post_g", "ffn_pre_g", "ffn_conv_w", "ffn_conv_b", "ffn_post_g")
SMALL_SHARDED = ("ssm_conv_w", "ffn_conv_w")
WEIGHTS = ("mix_pre_g", "w_in", "q_norm_g", "w_uq", "kv_norm_g", "w_ukv", "ssm_conv_w", "ssm_conv_b", "dt_bias", "a_log",
           "d_skip", "ssm_norm_g", "w_out", "mix_post_g", "ffn_pre_g", "w_gate", "w_up", "ffn_conv_w", "ffn_conv_b",
           "w_down", "ffn_post_g")


def _pick(n, target, mult):
    best = None
    for d in range(mult, min(n, target) + 1, mult):
        if n % d == 0:
            best = d
    return best if best is not None else n


def _params(sem=None):
    kw = dict(vmem_limit_bytes=VMEM_LIMIT)
    if sem is not None:
        kw["dimension_semantics"] = sem
    return pltpu.CompilerParams(**kw)


def _dot(a, b, dims=NN, precision=None):
    return lax.dot_general(a, b, dims, preferred_element_type=F32, precision=precision)


def _sigmoid(x):
    return 1.0 / (1.0 + jnp.exp(-x))


def _rs(x):
    return lax.rsqrt(jnp.mean(x * x, axis=-1, keepdims=True) + EPS)


def _rms_back(xh, r, dn):
    return r * (dn - xh * jnp.mean(dn * xh, axis=-1, keepdims=True))


def _colsum(v):
    return jnp.sum(v, axis=0, keepdims=True)


def _matmul(name, a, b, mode, out_dtype, a2=None, b2=None, chips=False):
    cs = None
    if mode == "nn":
        (M, K), N = a.shape, b.shape[-1]
        if chips:
            cs, N = N, N_CHIPS * N
    elif mode == "nt":
        (M, K), N = a.shape, b.shape[-2]
        if chips:
            cs = b.shape[-1]
    else:
        (K, M), N = a.shape, b.shape[1]
        if chips:
            cs = N // N_CHIPS
    tm = _pick(M, MM_TILE, LANE)
    tn = _pick(cs if chips and mode != "nt" else N, MM_TILE, LANE)
    tk = _pick(cs, MM_TILE, LANE) if chips and mode == "nt" else _pick(K, MM_TILE_K, LANE)
    nk = K // tk
    dims = {"nn": NN, "nt": NT, "tn": TN}[mode]
    a_spec = pl.BlockSpec((tk, tm), lambda i, j, k: (k, i)) if mode == "tn" else pl.BlockSpec((tm, tk), lambda i, j, k: (i, k))
    b_spec = pl.BlockSpec((tn, tk), lambda i, j, k: (j, k)) if mode == "nt" else pl.BlockSpec((tk, tn), lambda i, j, k: (k, j))
    o_spec = pl.BlockSpec((tm, tn), lambda i, j, k: (i, j))
    o_shape = (M, N)
    if chips and mode == "nn":
        per = cs // tn
        b_spec = pl.BlockSpec((None, tk, tn), lambda i, j, k: (j // per, k, j % per))
    elif chips and mode == "nt":
        per = cs // tk
        b_spec = pl.BlockSpec((None, tn, tk), lambda i, j, k: (k // per, j, k % per))
    elif chips:
        per = cs // tn
        o_spec = pl.BlockSpec((None, tm, tn), lambda i, j, k: (j // per, i, j % per))
        o_shape = (N_CHIPS, M, cs)
    two = a2 is not None

    def product(refs):
        part = _dot(refs[0][...].astype(BF16), refs[1][...].astype(BF16), dims)
        if two:
            part += _dot(refs[2][...].astype(BF16), refs[3][...].astype(BF16), dims)
        return part

    def body_whole_k(*refs):
        refs[-1][...] = product(refs).astype(refs[-1].dtype)

    def body(*refs):
        o_ref, acc_ref = refs[-2], refs[-1]
        k = pl.program_id(2)

        @pl.when(k == 0)
        def _():
            acc_ref[...] = product(refs)

        @pl.when(k > 0)
        def _():
            acc_ref[...] += product(refs)

        @pl.when(k == nk - 1)
        def _():
            o_ref[...] = acc_ref[...].astype(o_ref.dtype)

    ins = (a, b, a2, b2) if two else (a, b)
    return pl.pallas_call(
        body_whole_k if nk == 1 else body, name=name, grid=(M // tm, N // tn, nk),
        in_specs=[a_spec, b_spec] * (2 if two else 1),
        out_specs=o_spec,
        out_shape=jax.ShapeDtypeStruct(o_shape, out_dtype),
        scratch_shapes=[] if nk == 1 else [pltpu.VMEM((tm, tn), F32)],
        compiler_params=_params(("parallel", "parallel", "arbitrary")),
    )(*ins)


def _window(a):
    return (a[0], *a[1]) if isinstance(a, tuple) else (a, a.shape[1], 0)


def _rowwise(name, fn, rows, mats, outs, reds, ts):
    rows, widths, blocks = zip(*[_window(a) for a in rows])
    S = rows[0].shape[0]
    nr, nm, no = len(rows), len(mats), len(outs)

    def body(*refs):
        res = fn(*[r[...] for r in refs[:nr + nm]])
        res = res if isinstance(res, (tuple, list)) else (res,)
        for r, v in zip(refs[nr + nm:nr + nm + no], res[:no]):
            r[...] = v.astype(r.dtype)
        first = pl.program_id(0) == 0
        for r, v in zip(refs[nr + nm + no:], res[no:]):
            @pl.when(first)
            def _():
                r[...] = jnp.broadcast_to(v, r.shape)

            @pl.when(jnp.logical_not(first))
            def _():
                r[...] += jnp.broadcast_to(v, r.shape)

    in_specs = [pl.BlockSpec((ts, w), lambda i, b=b: (i, b)) for w, b in zip(widths, blocks)]
    in_specs += [pl.BlockSpec(m.shape, lambda i, nd=m.ndim: (0,) * nd) for m in mats]
    out_specs = [pl.BlockSpec((ts, w), lambda i: (i, 0)) for w, _ in outs]
    out_specs += [pl.BlockSpec(s, lambda i: (0, 0)) for s in reds]
    out_shape = [jax.ShapeDtypeStruct((S, w), dt) for w, dt in outs] + [jax.ShapeDtypeStruct(s, F32) for s in reds]
    return pl.pallas_call(
        body, name=name, grid=(S // ts,), in_specs=in_specs, out_specs=out_specs, out_shape=out_shape,
        compiler_params=_params(("arbitrary",) if reds else ("parallel",)),
    )(*rows, *mats)


def _shift_down(v, s):
    if s == 0:
        return v
    rows = lax.broadcasted_iota(I32, v.shape, 0)
    return jnp.where(rows >= s, pltpu.roll(v, s, 0), 0.0)


def _shift_up(v, s):
    if s == 0:
        return v
    n = v.shape[0]
    rows = lax.broadcasted_iota(I32, v.shape, 0)
    return jnp.where(rows < n - s, pltpu.roll(v, n - s, 0), 0.0)


def _conv(x, w, b):
    K = w.shape[0]
    y = jnp.broadcast_to(b, x.shape)
    for k in range(K):
        y = y + w[k:k + 1, :] * _shift_down(x, K - 1 - k)
    return y


def _conv_back(x, w, dc):
    K = w.shape[0]
    dx = jnp.zeros_like(x)
    dw = []
    for k in range(K):
        up = _shift_up(dc, K - 1 - k)
        dx = dx + w[k:k + 1, :] * up
        dw.append(_colsum(up * x))
    return dx, jnp.concatenate(dw, axis=0), _colsum(dc)


def _colwise(name, fn, cols, vecs, outs, pouts, tc):
    cols, widths, blocks = zip(*[_window(a) for a in cols])
    S, C = cols[0].shape[0], widths[0]
    firsts = [b * (C // tc) for b in blocks]
    nc_, nv, no = len(cols), len(vecs), len(outs)

    def body(*refs):
        res = fn(*[r[...] for r in refs[:nc_ + nv]])
        res = res if isinstance(res, (tuple, list)) else (res,)
        for r, v in zip(refs[nc_ + nv:], res):
            r[...] = v.astype(r.dtype)

    in_specs = [pl.BlockSpec((S, tc), lambda j, f=f: (0, f + j)) for f in firsts]
    in_specs += [pl.BlockSpec((v.shape[0], tc), lambda j: (0, j)) for v in vecs]
    out_specs = [pl.BlockSpec((S, tc), lambda j: (0, j)) for _ in outs] + [pl.BlockSpec((k, tc), lambda j: (0, j)) for k in pouts]
    out_shape = [jax.ShapeDtypeStruct((S, C), dt) for dt in outs] + [jax.ShapeDtypeStruct((k, C), F32) for k in pouts]
    return pl.pallas_call(
        body, name=name, grid=(C // tc,), in_specs=in_specs, out_specs=out_specs, out_shape=out_shape,
        compiler_params=_params(("parallel",)),
    )(*cols, *vecs)


_G0, _G1 = math.sqrt(2.0 / math.pi), 0.044715


def _gelu(g):
    th = jnp.tanh(_G0 * (g + _G1 * g * g * g))
    return 0.5 * g * (1.0 + th), th


def _ffn_act(gate_pre, up, w, b):
    act, _ = _gelu(_conv(gate_pre, w, b))
    return act * up


def _ffn_act_back(dact, gate_pre, up, w, b):
    g = _conv(gate_pre, w, b)
    ge, th = _gelu(g)
    dge = 0.5 * (1.0 + th) + 0.5 * g * (1.0 - th * th) * _G0 * (1.0 + 3.0 * _G1 * g * g)
    dup = dact * ge
    dgate_pre, dw, db = _conv_back(gate_pre, w, dact * up * dge)
    return dgate_pre, dup, dw, db


def _ssm_act(xbc, w, b):
    c = _conv(xbc, w, b)
    return c * _sigmoid(c)


def _ssm_act_back(dxc, xbc, w, b):
    c = _conv(xbc, w, b)
    sg = _sigmoid(c)
    return _conv_back(xbc, w, dxc * sg * (1.0 + c * (1.0 - sg)))


def _rope_tables(S):
    inv = 1.0 / (ROPE_THETA ** (jnp.arange(0, ROPE, 2, dtype=F32) / ROPE))
    ang = jnp.arange(S, dtype=F32)[:, None] * inv[None, :]
    cos, sin = jnp.cos(ang), jnp.sin(ang)
    return jnp.tile(cos, (1, 4)), jnp.tile(jnp.concatenate([-sin, sin], axis=1), (1, 2))


def _swap_halves(x):
    lane = lax.broadcasted_iota(I32, x.shape, 1)
    w = x.shape[1]
    return jnp.where((lane % ROPE) < ROPE // 2, pltpu.roll(x, w - ROPE // 2, 1), pltpu.roll(x, ROPE // 2, 1))


def _rot(x, cos2, sin2):
    return x * cos2 + _swap_halves(x) * sin2


def _rot_back(dy, cos2, sin2):
    return dy * cos2 + _swap_halves(dy * sin2)


def _mla_pack(cfg, q, kv, kr, cos2, sin2):
    S, H = cfg.S, cfg.H
    ts = _pick(S, 256, 8)
    kr, _, kr_block = _window(kr)

    def body(q_ref, kv_ref, kr_ref, c_ref, s_ref, Q_ref, K_ref, V_ref):
        c2, s2 = c_ref[...], s_ref[...]
        krr = _rot(kr_ref[...], c2, s2)
        kr_half = (krr.astype(BF16), pltpu.roll(krr, ROPE, 1).astype(BF16))
        for j in range(H // 2):
            qr = _rot(q_ref[:, (H + j) * LANE:(H + j + 1) * LANE], c2, s2).astype(BF16)
            for h in (2 * j, 2 * j + 1):
                Q_ref[h, :, 0:LANE] = q_ref[:, h * LANE:(h + 1) * LANE].astype(BF16)
                Q_ref[h, :, LANE:] = qr
                K_ref[h, :, 0:LANE] = kv_ref[:, h * LANE:(h + 1) * LANE].astype(BF16)
                K_ref[h, :, LANE:] = kr_half[h % 2]
                V_ref[h] = kv_ref[:, (H + h) * LANE:(H + h + 1) * LANE].astype(BF16)

    tab = pl.BlockSpec((ts, LANE), lambda i: (i, 0))
    heads = lambda w: pl.BlockSpec((H, ts, w), lambda i: (0, i, 0))
    return pl.pallas_call(
        body, name="mla_pack", grid=(S // ts,),
        in_specs=[pl.BlockSpec((ts, cfg.QW), lambda i: (i, 0)), pl.BlockSpec((ts, cfg.KVW), lambda i: (i, 0)),
                  pl.BlockSpec((ts, LANE), lambda i: (i, kr_block)), tab, tab],
        out_specs=[heads(2 * LANE), heads(2 * LANE), heads(LANE)],
        out_shape=[jax.ShapeDtypeStruct((H, S, 2 * LANE), BF16), jax.ShapeDtypeStruct((H, S, 2 * LANE), BF16),
                   jax.ShapeDtypeStruct((H, S, LANE), BF16)],
        compiler_params=_params(("parallel",)),
    )(q, kv, kr, cos2, sin2)


def _mla_unpack(cfg, dQ, dK, dV, cos2, sin2):
    S, H = cfg.S, cfg.H
    ts = _pick(S, 256, 8)

    def body(dQ_ref, dK_ref, dV_ref, c_ref, s_ref, dq_ref, dkv_ref, dkr_ref):
        c2, s2 = c_ref[...], s_ref[...]
        lo = lax.broadcasted_iota(I32, (ts, LANE), 1) < ROPE
        tk = jnp.zeros((ts, LANE), F32)
        for h in range(H):
            dq_ref[:, h * LANE:(h + 1) * LANE] = dQ_ref[h, :, 0:LANE].astype(BF16)
            dkv_ref[:, h * LANE:(h + 1) * LANE] = dK_ref[h, :, 0:LANE].astype(BF16)
            dkv_ref[:, (H + h) * LANE:(H + h + 1) * LANE] = dV_ref[h].astype(BF16)
            own = lo if h % 2 == 0 else jnp.logical_not(lo)
            tk = tk + jnp.where(own, dK_ref[h, :, LANE:], 0.0)
        for j in range(H // 2):
            dr = dQ_ref[2 * j, :, LANE:] + dQ_ref[2 * j + 1, :, LANE:]
            dq_ref[:, (H + j) * LANE:(H + j + 1) * LANE] = _rot_back(dr, c2, s2).astype(BF16)
        dkr_rot = jnp.where(lo, tk + pltpu.roll(tk, ROPE, 1), 0.0)
        dkr_ref[...] = _rot_back(dkr_rot, c2, s2).astype(BF16)

    tab = pl.BlockSpec((ts, LANE), lambda i: (i, 0))
    return pl.pallas_call(
        body, name="mla_unpack", grid=(S // ts,),
        in_specs=[pl.BlockSpec((H, ts, 2 * LANE), lambda i: (0, i, 0)), pl.BlockSpec((H, ts, 2 * LANE), lambda i: (0, i, 0)),
                  pl.BlockSpec((H, ts, LANE), lambda i: (0, i, 0)), tab, tab],
        out_specs=[pl.BlockSpec((ts, cfg.QW), lambda i: (i, 0)), pl.BlockSpec((ts, cfg.KVW), lambda i: (i, 0)), tab],
        out_shape=[jax.ShapeDtypeStruct((S, cfg.QW), BF16), jax.ShapeDtypeStruct((S, cfg.KVW), BF16),
                   jax.ShapeDtypeStruct((S, LANE), BF16)],
        compiler_params=_params(("parallel",)),
    )(dQ, dK, dV, cos2, sin2)


_ATT_T = 256
_ATT_HB = 8
_ATT_SCALE = (NOPE + ROPE) ** -0.5


def _diag_mask(transposed=False):
    r = lax.broadcasted_iota(I32, (_ATT_T, _ATT_T), 0) // CHUNK
    c = lax.broadcasted_iota(I32, (_ATT_T, _ATT_T), 1) // CHUNK
    return r <= c if transposed else c <= r


def _row_form(col):
    return jnp.broadcast_to(col, (col.shape[0], LANE)).T[0:8, :]


def _attn_fwd(cfg, Q, K, V):
    S, H, T, HB = cfg.S, cfg.H, _ATT_T, min(cfg.H, _ATT_HB)

    def body(q_ref, k_ref, v_ref, o_ref, lse_ref, lse_t_ref):
        qi = pl.program_id(1)

        def head_step(b, kb, carry, mask):
            m, l, acc = carry
            ks = pl.multiple_of(kb * T, T)
            s = _dot(q_ref[b], k_ref[b, pl.ds(ks, T), :], NT) * _ATT_SCALE
            if mask is not None:
                s = jnp.where(mask, s, -1e30)
            m_new = jnp.maximum(m, jnp.max(s, axis=1, keepdims=True))
            p = jnp.exp(s - m_new)
            alpha = jnp.exp(m - m_new)
            l = alpha * l + jnp.sum(p, axis=1, keepdims=True)
            acc = alpha * acc + _dot(p.astype(BF16), v_ref[b, pl.ds(ks, T), :])
            return m_new, l, acc

        def step(kb, carry, mask=None):
            return tuple(head_step(b, kb, carry[b], mask) for b in range(HB))

        init = (jnp.full((T, 1), -1e30, F32), jnp.zeros((T, 1), F32), jnp.zeros((T, VH), F32))
        done = step(qi, lax.fori_loop(0, qi, step, (init,) * HB), _diag_mask())
        for b, (m, l, acc) in enumerate(done):
            o_ref[:, b * LANE:(b + 1) * LANE] = acc / l
            lse = m + jnp.log(l)
            lse_ref[:, b * LANE:(b + 1) * LANE] = jnp.broadcast_to(lse, (T, LANE))
            lse_t_ref[b] = _row_form(lse)

    return pl.pallas_call(
        body, name="attn_fwd", grid=(H // HB, S // T),
        in_specs=[pl.BlockSpec((HB, T, 2 * LANE), lambda h, i: (h, i, 0)), pl.BlockSpec((HB, S, 2 * LANE), lambda h, i: (h, 0, 0)),
                  pl.BlockSpec((HB, S, LANE), lambda h, i: (h, 0, 0))],
        out_specs=[pl.BlockSpec((T, HB * LANE), lambda h, i: (i, h)), pl.BlockSpec((T, HB * LANE), lambda h, i: (i, h)),
                   pl.BlockSpec((HB, 8, T), lambda h, i: (h, 0, i))],
        out_shape=[jax.ShapeDtypeStruct((S, H * LANE), F32), jax.ShapeDtypeStruct((S, H * LANE), F32),
                   jax.ShapeDtypeStruct((H, 8, S), F32)],
        compiler_params=_params(("parallel", "parallel")),
    )(Q, K, V)


def _attn_dq(cfg, Q, K, V, do, o, lse, after):
    S, H, T, HB = cfg.S, cfg.H, _ATT_T, min(cfg.H, _ATT_HB)

    def body(q_ref, k_ref, v_ref, do_ref, o_ref, lse_ref, after_ref, dq_ref, dl_t_ref):
        qi = pl.program_id(1)
        do = [do_ref[:, b * LANE:(b + 1) * LANE] for b in range(HB)]
        delta = [jnp.sum(do[b] * o_ref[:, b * LANE:(b + 1) * LANE], axis=1, keepdims=True) for b in range(HB)]
        dob = [d.astype(BF16) for d in do]

        def head_step(b, kb, dq, mask):
            ks = pl.multiple_of(kb * T, T)
            k = k_ref[b, pl.ds(ks, T), :]
            s = _dot(q_ref[b], k, NT) * _ATT_SCALE
            if mask is not None:
                s = jnp.where(mask, s, -1e30)
            p = jnp.exp(s - lse_ref[:, b * LANE:b * LANE + 1])
            dp = _dot(dob[b], v_ref[b, pl.ds(ks, T), :], NT)
            ds = p * (dp - delta[b]) * _ATT_SCALE
            return dq + _dot(ds.astype(BF16), k)

        def step(kb, dqs, mask=None):
            return tuple(head_step(b, kb, dqs[b], mask) for b in range(HB))

        dqs = step(qi, lax.fori_loop(0, qi, step, (jnp.zeros((T, 2 * LANE), F32),) * HB), _diag_mask())
        for b in range(HB):
            dq_ref[b] = dqs[b]
            dl_t_ref[b] = _row_form(delta[b])

    col = pl.BlockSpec((T, HB * LANE), lambda h, i: (i, h))
    return pl.pallas_call(
        body, name="attn_dq", grid=(H // HB, S // T),
        in_specs=[pl.BlockSpec((HB, T, 2 * LANE), lambda h, i: (h, i, 0)), pl.BlockSpec((HB, S, 2 * LANE), lambda h, i: (h, 0, 0)),
                  pl.BlockSpec((HB, S, LANE), lambda h, i: (h, 0, 0)), col, col, col, _ANY],
        out_specs=[pl.BlockSpec((HB, T, 2 * LANE), lambda h, i: (h, i, 0)), pl.BlockSpec((HB, 8, T), lambda h, i: (h, 0, i))],
        out_shape=[jax.ShapeDtypeStruct((H, S, 2 * LANE), F32), jax.ShapeDtypeStruct((H, 8, S), F32)],
        compiler_params=_params(("parallel", "parallel")),
    )(Q, K, V, do, o, lse, after)


def _attn_dkv(cfg, Q, K, V, do, lse_t, delta_t):
    S, H, T, HB = cfg.S, cfg.H, _ATT_T, min(cfg.H, _ATT_HB)
    nq = S // T

    def body(q_ref, k_ref, v_ref, do_ref, lse_ref, dl_ref, dk_ref, dv_ref):
        kb = pl.program_id(1)

        def head_step(b, qi, carry, mask):
            dk, dv = carry
            qs = pl.multiple_of(qi * T, T)
            q = q_ref[b, pl.ds(qs, T), :]
            dob = do_ref[pl.ds(qs, T), b * LANE:(b + 1) * LANE].astype(BF16)
            s = _dot(k_ref[b], q, NT) * _ATT_SCALE
            if mask is not None:
                s = jnp.where(mask, s, -1e30)
            p = jnp.exp(s - lse_ref[b, 0:1, pl.ds(qs, T)])
            dv = dv + _dot(p.astype(BF16), dob)
            dp = _dot(v_ref[b], dob, NT)
            ds = p * (dp - dl_ref[b, 0:1, pl.ds(qs, T)]) * _ATT_SCALE
            dk = dk + _dot(ds.astype(BF16), q)
            return dk, dv

        def step(qi, carry, mask=None):
            return tuple(head_step(b, qi, carry[b], mask) for b in range(HB))

        zero = (jnp.zeros((T, 2 * LANE), F32), jnp.zeros((T, VH), F32))
        done = lax.fori_loop(kb + 1, nq, step, step(kb, (zero,) * HB, _diag_mask(transposed=True)))
        for b, (dk, dv) in enumerate(done):
            dk_ref[b] = dk
            dv_ref[b] = dv

    row = pl.BlockSpec((HB, 8, S), lambda h, j: (h, 0, 0))
    return pl.pallas_call(
        body, name="attn_dkv", grid=(H // HB, S // T),
        in_specs=[pl.BlockSpec((HB, S, 2 * LANE), lambda h, j: (h, 0, 0)), pl.BlockSpec((HB, T, 2 * LANE), lambda h, j: (h, j, 0)),
                  pl.BlockSpec((HB, T, LANE), lambda h, j: (h, j, 0)), pl.BlockSpec((S, HB * LANE), lambda h, j: (0, h)), row, row],
        out_specs=[pl.BlockSpec((HB, T, 2 * LANE), lambda h, j: (h, j, 0)), pl.BlockSpec((HB, T, LANE), lambda h, j: (h, j, 0))],
        out_shape=[jax.ShapeDtypeStruct((H, S, 2 * LANE), F32), jax.ShapeDtypeStruct((H, S, LANE), F32)],
        compiler_params=_params(("parallel", "parallel")),
    )(Q, K, V, do, lse_t, delta_t)


def _expand_matrix(cfg):
    r = lax.broadcasted_iota(I32, (LANE, cfg.INNER), 0)
    c = lax.broadcasted_iota(I32, (LANE, cfg.INNER), 1)
    return (r == c // HP).astype(F32)


def _softplus(x):
    return jnp.maximum(x, 0.0) + jnp.log(1.0 + jnp.exp(-jnp.abs(x)))


def _ssd_prep(cfg, dt_raw, dt_bias_pad, a_log_pad, expand):
    HS = cfg.HS

    def fn(raw, bias, alog, E):
        heads = lax.broadcasted_iota(I32, raw.shape, 1) < HS
        dt = jnp.where(heads, _softplus(raw + bias), 0.0)
        a = dt * jnp.where(heads[0:1], -jnp.exp(alog), 0.0)
        return dt, a, _dot(dt, E, precision=HI), _dot(a, E, precision=HI)

    return _rowwise("ssd_prep", fn, [dt_raw], [dt_bias_pad, a_log_pad, expand],
                    [(LANE, F32), (LANE, F32), (cfg.INNER, F32), (cfg.INNER, F32)], [], _pick(cfg.S, 512, 8))


def _tril(T):
    return lax.broadcasted_iota(I32, (T, T), 0) >= lax.broadcasted_iota(I32, (T, T), 1)


def _ssd_fwd(cfg, xc, dt_exp, a_exp, a_small, dskip_exp):
    S, T, INNER, G, NPAIR = cfg.S, cfg.T, cfg.INNER, cfg.G, cfg.NPAIR
    NC = S // T

    def body(xc_ref, dte_ref, ae_ref, as_ref, dsk_ref, y_ref, hin_ref, ht_ref):
        @pl.when(pl.program_id(0) == 0)
        def _():
            ht_ref[...] = jnp.zeros_like(ht_ref)

        tril = _tril(T)
        tri = tril.astype(F32)
        acs_s = _dot(tri, as_ref[...], precision=HI)
        acs_e = _dot(tri, ae_ref[...], precision=HI)
        acs_t = acs_s.T
        lo = lax.broadcasted_iota(I32, (T, LANE), 1) < HP
        for g in range(G):
            Bb = xc_ref[:, INNER + g * NST:INNER + (g + 1) * NST].astype(BF16)
            Cb = xc_ref[:, INNER + (G + g) * NST:INNER + (G + g + 1) * NST].astype(BF16)
            Gm = _dot(Cb, Bb, NT)
            for j in range(g * NPAIR // G, (g + 1) * NPAIR // G):
                sl = slice(j * LANE, (j + 1) * LANE)
                Xp = xc_ref[:, sl]
                Xdt = Xp * dte_ref[:, sl]
                Xb = Xdt.astype(BF16)
                acs_p = acs_e[:, sl]
                last = acs_p[T - 1:T, :]
                Hin = ht_ref[j]
                hin_ref[0, j] = Hin
                yd = []
                for e in (0, 1):
                    h = 2 * j + e
                    Lm = jnp.exp(jnp.where(tril, acs_s[:, h:h + 1] - acs_t[h:h + 1, :], -1e30))
                    yd.append(_dot((Gm * Lm).astype(BF16), Xb))
                y_off = _dot(Cb, Hin.astype(BF16)) * jnp.exp(acs_p)
                y_ref[:, sl] = jnp.where(lo, yd[0], yd[1]) + y_off + Xp * dsk_ref[:, sl]
                st = _dot(Bb, (Xdt * jnp.exp(last - acs_p)).astype(BF16), TN)
                ht_ref[j] = jnp.exp(last) * Hin + st

    rows = lambda w: pl.BlockSpec((T, w), lambda c: (c, 0))
    return pl.pallas_call(
        body, name="ssd_fwd", grid=(NC,),
        in_specs=[rows(cfg.CONVCH), rows(INNER), rows(INNER), rows(LANE), pl.BlockSpec((1, INNER), lambda c: (0, 0))],
        out_specs=[rows(INNER), pl.BlockSpec((1, NPAIR, NST, LANE), lambda c: (c, 0, 0, 0))],
        out_shape=[jax.ShapeDtypeStruct((S, INNER), F32), jax.ShapeDtypeStruct((NC, NPAIR, NST, LANE), F32)],
        scratch_shapes=[pltpu.VMEM((NPAIR, NST, LANE), F32)],
        compiler_params=_params(("arbitrary",)),
    )(xc, dt_exp, a_exp, a_small, dskip_exp)


def _ssd_bwd(cfg, dy, xc, dt_exp, a_exp, a_small, dskip_exp, hin, dt_raw, dt_bias_pad, a_log_pad, expand):
    S, T, INNER, G, NPAIR, HS = cfg.S, cfg.T, cfg.INNER, cfg.G, cfg.NPAIR, cfg.HS
    NC = S // T

    def body(dy_ref, xc_ref, dte_ref, ae_ref, as_ref, dsk_ref, hin_ref, raw_ref, bias_ref, alog_ref, e_ref,
             dxc_ref, draw_ref, dbias_ref, dalog_ref, dskip_ref, dht_ref, cols_ref, rows_ref, dacs_ref, ddt_ref):
        first = pl.program_id(0) == 0

        @pl.when(first)
        def _():
            dht_ref[...] = jnp.zeros_like(dht_ref)

        tril = _tril(T)
        tri = tril.astype(F32)
        a_s = as_ref[...]
        acs_s = _dot(tri, a_s, precision=HI)
        acs_e = _dot(tri, ae_ref[...], precision=HI)
        acs_t = acs_s.T
        lo = lax.broadcasted_iota(I32, (T, LANE), 1) < HP
        last_row = lax.broadcasted_iota(I32, (T, LANE), 0) == T - 1
        cols_ref[...] = jnp.zeros_like(cols_ref)
        rows_ref[...] = jnp.zeros_like(rows_ref)
        dsk_parts = []
        for g in range(G):
            bsl = slice(INNER + g * NST, INNER + (g + 1) * NST)
            csl = slice(INNER + (G + g) * NST, INNER + (G + g + 1) * NST)
            Bb = xc_ref[:, bsl].astype(BF16)
            Cb = xc_ref[:, csl].astype(BF16)
            Gm = _dot(Cb, Bb, NT)
            dG = jnp.zeros((T, T), F32)
            dB = jnp.zeros((T, NST), F32)
            dC = jnp.zeros((T, NST), F32)
            for j in range(g * NPAIR // G, (g + 1) * NPAIR // G):
                sl = slice(j * LANE, (j + 1) * LANE)
                Xp = xc_ref[:, sl]
                dtp = dte_ref[:, sl]
                Xdt = Xp * dtp
                Xb = Xdt.astype(BF16)
                acs_p = acs_e[:, sl]
                last = acs_p[T - 1:T, :]
                e_p, dec, cd = jnp.exp(acs_p), jnp.exp(last - acs_p), jnp.exp(last)
                Hin = hin_ref[0, j]
                Hb = Hin.astype(BF16)
                dHn = dht_ref[j]
                dHb = dHn.astype(BF16)
                dYp = dy_ref[:, sl]
                z = _dot(Cb, Hb)
                dz = (dYp * e_p).astype(BF16)
                dacs_p = dYp * z * e_p
                dC = dC + _dot(dz, Hb, NT)
                dHin = _dot(Cb, dz, TN) + cd * dHn
                dlast = _colsum(dHn * Hin) * cd
                qv = _dot(Bb, dHb)
                dXdt = qv * dec
                ddec = qv * Xdt * dec
                dacs_p = dacs_p - ddec
                dlast = dlast + _colsum(ddec)
                dB = dB + _dot((Xdt * dec).astype(BF16), dHb, NT)
                for e in (0, 1):
                    h = 2 * j + e
                    Lm = jnp.exp(jnp.where(tril, acs_s[:, h:h + 1] - acs_t[h:h + 1, :], -1e30))
                    Mh = Gm * Lm
                    dYe = jnp.where(lo if e == 0 else jnp.logical_not(lo), dYp, 0.0).astype(BF16)
                    dM = _dot(dYe, Xb, NT)
                    dXdt = dXdt + _dot(Mh.astype(BF16), dYe, TN)
                    W = dM * Mh
                    cols_ref[:, h:h + 1] = jnp.sum(W, axis=1, keepdims=True)
                    rows_ref[h:h + 1, :] = _colsum(W)
                    dG = dG + dM * Lm
                dacs_ref[:, sl] = dacs_p + jnp.where(last_row, dlast, 0.0)
                ddt_ref[:, sl] = dXdt * Xp
                dxc_ref[:, sl] = dXdt * dtp + dYp * dsk_ref[:, sl]
                dsk_parts.append(_colsum(dYp * Xp))
                dht_ref[j] = dHin
            dGb = dG.astype(BF16)
            dxc_ref[:, bsl] = dB + _dot(dGb, Cb, TN)
            dxc_ref[:, csl] = dC + _dot(dGb, Bb)
        E = e_ref[...]
        dacs_s = cols_ref[...] - rows_ref[...].T + _dot(dacs_ref[...], E, NT, precision=HI)
        da = _dot(tri, dacs_s, TN, precision=HI)
        heads = lax.broadcasted_iota(I32, (1, LANE), 1) < HS
        A = jnp.where(heads, -jnp.exp(alog_ref[...]), 0.0)
        ddt = _dot(ddt_ref[...], E, NT, precision=HI) + da * A
        draw = jnp.where(heads, ddt * _sigmoid(raw_ref[...] + bias_ref[...]), 0.0)
        draw_ref[...] = draw
        dsk = _dot(jnp.broadcast_to(jnp.concatenate(dsk_parts, axis=1), (8, INNER)), E, NT, precision=HI)[0:1]
        for ref, val in ((dbias_ref, _colsum(draw)), (dalog_ref, _colsum(da * a_s)), (dskip_ref, dsk)):
            @pl.when(first)
            def _():
                ref[...] = val

            @pl.when(jnp.logical_not(first))
            def _():
                ref[...] += val

    dt_raw, _, raw_block = _window(dt_raw)
    rows = lambda w, b=0: pl.BlockSpec((T, w), lambda c: (NC - 1 - c, b))
    vec = lambda w: pl.BlockSpec((1, w), lambda c: (0, 0))
    return pl.pallas_call(
        body, name="ssd_bwd", grid=(NC,),
        in_specs=[rows(INNER), rows(cfg.CONVCH), rows(INNER), rows(INNER), rows(LANE), vec(INNER),
                  pl.BlockSpec((1, NPAIR, NST, LANE), lambda c: (NC - 1 - c, 0, 0, 0)), rows(LANE, raw_block), vec(LANE), vec(LANE),
                  pl.BlockSpec((LANE, INNER), lambda c: (0, 0))],
        out_specs=[rows(cfg.CONVCH), rows(LANE), vec(LANE), vec(LANE), vec(LANE)],
        out_shape=[jax.ShapeDtypeStruct((S, cfg.CONVCH), F32), jax.ShapeDtypeStruct((S, LANE), F32)]
        + [jax.ShapeDtypeStruct((1, LANE), F32)] * 3,
        scratch_shapes=[pltpu.VMEM((NPAIR, NST, LANE), F32), pltpu.VMEM((T, LANE), F32), pltpu.VMEM((LANE, T), F32),
                        pltpu.VMEM((T, INNER), F32), pltpu.VMEM((T, INNER), F32)],
        compiler_params=_params(("arbitrary",)),
    )(dy, xc, dt_exp, a_exp, a_small, dskip_exp, hin, dt_raw, dt_bias_pad, a_log_pad, expand)


def _ssd_post(cfg, y, z, norm_g):
    W = cfg.INNER // cfg.G

    def fn(y, z, g):
        yz = y * z * _sigmoid(z)
        return jnp.concatenate([yz[:, i * W:(i + 1) * W] * _rs(yz[:, i * W:(i + 1) * W]) for i in range(cfg.G)], axis=1) * g

    return _rowwise("ssd_post", fn, [y, z], [norm_g], [(cfg.INNER, BF16)], [], _pick(cfg.S, 256, 8))[0]


def _ssd_post_bwd(cfg, db, y, z, norm_g):
    W = cfg.INNER // cfg.G

    def fn(db, y, z, g):
        sg = _sigmoid(z)
        yz = y * z * sg
        dn = db * g
        dyz, nh = [], []
        for i in range(cfg.G):
            seg = yz[:, i * W:(i + 1) * W]
            r = _rs(seg)
            nh.append(seg * r)
            dyz.append(_rms_back(nh[-1], r, dn[:, i * W:(i + 1) * W]))
        dyz = jnp.concatenate(dyz, axis=1)
        return dyz * z * sg, dyz * y * sg * (1.0 + z * (1.0 - sg)), _colsum(db * jnp.concatenate(nh, axis=1))

    return _rowwise("ssd_post_bwd", fn, [db, y, z], [norm_g], [(cfg.INNER, F32), (cfg.INNER, F32)], [(1, cfg.INNER)],
                    _pick(cfg.S, 256, 8))


def _local_grads(cfg, x, tgt, W, sp, mla_weights=None, out_weight=None, ffn_weights=None, down_weight=None,
                 ffn_grads_ready=None, early_grads_ready=None, in_grad_ready=None):
    S, D, H, INNER = cfg.S, cfg.D, cfg.H, cfg.INNER
    ts = _pick(S, 256, 8)
    tc = _CONV_COLS

    xn = _rowwise("rms_pre", lambda x, g: x * _rs(x) * g, [x], [sp["mix_pre_g"]], [(D, BF16)], [], ts)[0]
    u = _matmul("mm_in", xn, W["w_in"], "nt", F32)
    c_q, c_kv, kr, z, xbc, dt_raw = [(u, cfg.window(n)) for n in ("c_q", "c_kv", "kr", "z", "xbc", "dt")]

    if mla_weights is not None:
        sp = dict(sp, q_norm_g=sp["q_norm_g"] + mla_weights.pass_on(u)[0, 0])
    cqn = _rowwise("rms_q", lambda x, g: x * _rs(x) * g, [c_q], [sp["q_norm_g"]], [(cfg.QL, BF16)], [], ts)[0]
    ckvn = _rowwise("rms_kv", lambda x, g: x * _rs(x) * g, [c_kv], [sp["kv_norm_g"]], [(cfg.KVL, BF16)], [], ts)[0]
    if mla_weights is not None:
        W = dict(W, **mla_weights.arrived(ckvn))
    q = _matmul("mm_uq", cqn, W["w_uq"], "nn", F32)
    kv = _matmul("mm_ukv", ckvn, W["w_ukv"], "nn", F32)
    cos2, sin2 = _rope_tables(S)
    Qh, Kh, Vh = _mla_pack(cfg, q, kv, kr, cos2, sin2)
    a_out, lse, lse_t = _attn_fwd(cfg, Qh, Kh, Vh)
    if out_weight is not None:
        sp = dict(sp, ssm_conv_b=sp["ssm_conv_b"] + out_weight.pass_on(a_out)[0, 0])

    pad = lambda v: jnp.pad(v, ((0, 0), (0, LANE - v.shape[1])))
    expand = _expand_matrix(cfg)
    dt_bias_pad, a_log_pad = pad(sp["dt_bias"]), pad(sp["a_log"])
    dskip_exp = jnp.repeat(sp["d_skip"], HP, axis=1)
    xc = _colwise("ssm_act", _ssm_act, [xbc], [sp["ssm_conv_w"], sp["ssm_conv_b"]], [F32], [], tc)[0]
    dt_s, a_s, dt_exp, a_exp = _ssd_prep(cfg, dt_raw, dt_bias_pad, a_log_pad, expand)
    y_ssd, hin = _ssd_fwd(cfg, xc, dt_exp, a_exp, a_s, dskip_exp)
    b_out = _ssd_post(cfg, y_ssd, z, sp["ssm_norm_g"])

    ab_out = jnp.concatenate([a_out.astype(BF16), b_out], axis=1)
    if out_weight is not None:
        W = dict(W, **out_weight.arrived(ab_out))
    if ffn_weights is not None:
        sp = dict(sp, mix_post_g=sp["mix_post_g"] + ffn_weights.pass_on(ab_out)[0, 0])
    mix = _matmul("mm_out", ab_out, W["w_out"], "nn", F32)

    def mid(x, mix, g_mp, g_fp):
        x1 = x + mix * _rs(mix) * g_mp
        return x1, x1 * _rs(x1) * g_fp

    x1, h2 = _rowwise("fwd_mid", mid, [x, mix], [sp["mix_post_g"], sp["ffn_pre_g"]], [(D, F32), (D, BF16)], [], ts)
    if ffn_weights is not None:
        W = dict(W, **ffn_weights.arrived(h2))
    gate_pre = _matmul("mm_gate", h2, W["w_gate"], "nn", F32, chips=True)
    if down_weight is not None:
        sp = dict(sp, ffn_conv_b=sp["ffn_conv_b"] + down_weight.pass_on(gate_pre)[0, 0])
    up = _matmul("mm_up", h2, W["w_up"], "nn", F32, chips=True)
    act = _colwise("ffn_act", _ffn_act, [gate_pre, up], [sp["ffn_conv_w"], sp["ffn_conv_b"]], [BF16], [], tc)[0]
    if down_weight is not None:
        W = dict(W, **down_weight.arrived(act))
    f = _matmul("mm_down", act, W["w_down"], "nn", F32)

    def final(x1, f, t, g):
        r = _rs(f)
        fh = f * r
        err = x1 + fh * g - t
        loss = 0.5 * jnp.sum(jnp.mean(err * err, axis=-1, keepdims=True), axis=0, keepdims=True)
        dy = err * (1.0 / D)
        return dy, _rms_back(fh, r, dy * g), _colsum(dy * fh), loss

    dy, df, g_ffn_post, loss = _rowwise("final", final, [x1, f, tgt], [sp["ffn_post_g"]], [(D, F32), (D, BF16)],
                                        [(1, D), (1, LANE)], ts)
    gW = {}
    dact = _matmul("mm_down_dx", df, W["w_down"], "nt", F32)
    gW["w_down"] = _matmul("mm_down_dw", act, df, "tn", BF16)
    dgate, dup, g_ffn_conv_w, g_ffn_conv_b = _colwise(
        "ffn_act_bwd", _ffn_act_back, [dact, gate_pre, up], [sp["ffn_conv_w"], sp["ffn_conv_b"]], [BF16, BF16], [FFN_K, 1], tc)
    gW["w_gate"] = _matmul("mm_gate_dw", h2, dgate, "tn", BF16, chips=True)
    gW["w_up"] = _matmul("mm_up_dw", h2, dup, "tn", BF16, chips=True)
    if ffn_grads_ready is not None:
        sp = dict(sp, ffn_pre_g=sp["ffn_pre_g"] + ffn_grads_ready({n: gW[n] for n in ("w_down", "w_gate", "w_up")})[0, 0])
    dh2 = _matmul("mm_gu_dx", dgate, W["w_gate"], "nt", F32, dup, W["w_up"], chips=True)

    def mid_back(dy, dh2, x1, mix, g_mp, g_fp):
        r2 = _rs(x1)
        xh = x1 * r2
        dx1 = dy + _rms_back(xh, r2, dh2 * g_fp)
        r1 = _rs(mix)
        mh = mix * r1
        return dx1, _rms_back(mh, r1, dx1 * g_mp), _colsum(dh2 * xh), _colsum(dx1 * mh)

    dx1, dmix, g_ffn_pre, g_mix_post = _rowwise("bwd_mid", mid_back, [dy, dh2, x1, mix], [sp["mix_post_g"], sp["ffn_pre_g"]],
                                                [(D, F32), (D, BF16)], [(1, D), (1, D)], ts)
    dab_out = _matmul("mm_out_dx", dmix, W["w_out"], "nt", F32)
    db_out = (dab_out, (INNER, cfg.MLAW // INNER))
    gW["w_out"] = _matmul("mm_out_dw", ab_out, dmix, "tn", BF16)
    early_token = jnp.zeros((8, LANE), F32)
    if early_grads_ready is not None:
        early_token = early_grads_ready({n: gW[n] for n in ("w_down", "w_gate", "w_up", "w_out")})
        sp = dict(sp, ssm_norm_g=sp["ssm_norm_g"] + early_token[0, 0])

    dy_ssd, dz, g_ssm_norm = _ssd_post_bwd(cfg, db_out, y_ssd, z, sp["ssm_norm_g"])
    dxc, ddt_raw, g_dt_bias, g_a_log, g_d_skip = _ssd_bwd(cfg, dy_ssd, xc, dt_exp, a_exp, a_s, dskip_exp, hin, dt_raw,
                                                          dt_bias_pad, a_log_pad, expand)
    dxbc, g_ssm_conv_w, g_ssm_conv_b = _colwise("ssm_act_bwd", _ssm_act_back, [dxc, xbc], [sp["ssm_conv_w"], sp["ssm_conv_b"]],
                                                [BF16], [SSM_K, 1], tc)

    dQ, delta_t = _attn_dq(cfg, Qh, Kh, Vh, dab_out, a_out, lse, early_token)
    dK, dV = _attn_dkv(cfg, Qh, Kh, Vh, dab_out, lse_t, delta_t)
    dq, dkv, dkr = _mla_unpack(cfg, dQ, dK, dV, cos2, sin2)
    dcqn = _matmul("mm_uq_dx", dq, W["w_uq"], "nt", F32)
    dckvn = _matmul("mm_ukv_dx", dkv, W["w_ukv"], "nt", F32)
    gW["w_uq"] = _matmul("mm_uq_dw", cqn, dq, "tn", BF16)
    gW["w_ukv"] = _matmul("mm_ukv_dw", ckvn, dkv, "tn", BF16)

    def rms_back(x, dy, g):
        r = _rs(x)
        xh = x * r
        return _rms_back(xh, r, dy * g), _colsum(dy * xh)

    dc_q, g_q_norm = _rowwise("rms_q_bwd", rms_back, [c_q, dcqn], [sp["q_norm_g"]], [(cfg.QL, BF16)], [(1, cfg.QL)], ts)
    dc_kv, g_kv_norm = _rowwise("rms_kv_bwd", rms_back, [c_kv, dckvn], [sp["kv_norm_g"]], [(cfg.KVL, BF16)], [(1, cfg.KVL)], ts)

    du = dict(c_q=dc_q, c_kv=dc_kv, kr=dkr, z=dz.astype(BF16), xbc=dxbc, dt=ddt_raw.astype(BF16))
    du = jnp.concatenate([du[n] for n in sorted(du, key=lambda n: cfg.seg[n][0])], axis=1)
    assert du.shape[1] == cfg.EXT, "the layout of u has gaps"
    gW["w_in"] = _matmul("mm_in_dw", du, xn, "tn", BF16)
    if in_grad_ready is not None:
        token = in_grad_ready({n: gW[n] for n in ("w_in", "w_uq", "w_ukv")})
        sp = dict(sp, mix_pre_g=sp["mix_pre_g"] + token[0, 0])
    dxn = _matmul("mm_in_dx", du, W["w_in"], "nn", F32)

    def first_back(dx1, dxn, x, g):
        r = _rs(x)
        xh = x * r
        return dx1 + _rms_back(xh, r, dxn * g), _colsum(dxn * xh)

    grad_x, g_mix_pre = _rowwise("bwd_first", first_back, [dx1, dxn, x], [sp["mix_pre_g"]], [(D, F32)], [(1, D)], ts)

    gs = dict(mix_pre_g=g_mix_pre, q_norm_g=g_q_norm, kv_norm_g=g_kv_norm, ssm_conv_w=g_ssm_conv_w, ssm_conv_b=g_ssm_conv_b,
              dt_bias=g_dt_bias[:, :cfg.HS], a_log=g_a_log[:, :cfg.HS], d_skip=g_d_skip[:, :cfg.HS], ssm_norm_g=g_ssm_norm,
              mix_post_g=g_mix_post, ffn_pre_g=g_ffn_pre, ffn_conv_w=g_ffn_conv_w, ffn_conv_b=g_ffn_conv_b,
              ffn_post_g=g_ffn_post)
    return loss, grad_x, gW, gs


def _to_kernel_layout(cfg, name, w):
    if name == "w_in":
        parts, at = [], 0
        for off, width, n_off, n_width in sorted(cfg.seg.values()):
            parts += [jnp.zeros((off - at, w.shape[1]), w.dtype), w[n_off:n_off + n_width],
                      jnp.zeros((width - n_width, w.shape[1]), w.dtype)]
            at = off + width
        parts.append(jnp.zeros((cfg.EXT - at, w.shape[1]), w.dtype))
        return jnp.concatenate([p for p in parts if p.shape[0]], axis=0)
    if name in ("w_uq", "w_ukv"):
        per = NOPE + (ROPE if name == "w_uq" else VH)
        return jnp.concatenate([w[:, h * per:h * per + NOPE] for h in range(cfg.H)]
                               + [w[:, h * per + NOPE:(h + 1) * per] for h in range(cfg.H)], axis=1)
    return w


def _from_kernel_layout(cfg, name, g):
    if name == "w_in":
        return jnp.concatenate([g[off:off + n_width] for off, _, _, n_width in sorted(cfg.seg.values(), key=lambda s: s[2])], axis=0)
    if name in ("w_uq", "w_ukv"):
        second = ROPE if name == "w_uq" else VH
        base = cfg.H * NOPE
        parts = []
        for h in range(cfg.H):
            parts += [g[:, h * NOPE:(h + 1) * NOPE], g[:, base + h * second:base + (h + 1) * second]]
        return jnp.concatenate(parts, axis=1)
    return g


_CHIP_MAJOR = ("w_gate", "w_up")
_RELAYOUT = ("w_uq", "w_ukv")
_LAYOUT_ROWS = 256
_CONV_COLS = 256


def _w_in_layout(cfg, wg):
    _, rs, d = wg.shape
    tc = _pick(d, _LAYOUT_ROWS, LANE)

    def body(w_ref, o_ref):
        o_ref[...] = _to_kernel_layout(cfg, "w_in", jnp.concatenate([w_ref[k] for k in range(N_CHIPS)], axis=0))

    return pl.pallas_call(
        body, name="layout_w_in", grid=(d // tc,),
        in_specs=[pl.BlockSpec((N_CHIPS, rs, tc), lambda j: (0, 0, j))], out_specs=pl.BlockSpec((cfg.EXT, tc), lambda j: (0, j)),
        out_shape=jax.ShapeDtypeStruct((cfg.EXT, d), wg.dtype), compiler_params=_params(("parallel",)),
    )(wg)


def _w_in_grad_to_chips(cfg, g):
    _, d = g.shape
    rs = cfg.IN_COLS // N_CHIPS
    tc = _pick(d, _LAYOUT_ROWS, LANE)

    def body(g_ref, o_ref):
        nat = _from_kernel_layout(cfg, "w_in", g_ref[...])
        for k in range(N_CHIPS):
            o_ref[k] = nat[k * rs:(k + 1) * rs]

    return pl.pallas_call(
        body, name="layout_grad_w_in", grid=(d // tc,),
        in_specs=[pl.BlockSpec((cfg.EXT, tc), lambda j: (0, j))], out_specs=pl.BlockSpec((N_CHIPS, rs, tc), lambda j: (0, 0, j)),
        out_shape=jax.ShapeDtypeStruct((N_CHIPS, rs, d), g.dtype), compiler_params=_params(("parallel",)),
    )(g)


def _gathered_to_kernel(cfg, name, wg):
    if name in _CHIP_MAJOR:
        return wg
    if name == "w_in":
        return _w_in_layout(cfg, wg)
    if name not in _RELAYOUT:
        return wg.reshape(wg.shape[0] * wg.shape[1], wg.shape[2])
    _, rows, cs = wg.shape
    tr = _pick(rows, _LAYOUT_ROWS, 16)

    def body(w_ref, o_ref):
        o_ref[...] = _to_kernel_layout(cfg, name, jnp.concatenate([w_ref[k] for k in range(N_CHIPS)], axis=1))

    wide = jax.eval_shape(lambda w: _to_kernel_layout(cfg, name, w), jax.ShapeDtypeStruct((rows, N_CHIPS * cs), wg.dtype)).shape[1]
    return pl.pallas_call(
        body, name="layout_" + name, grid=(rows // tr,),
        in_specs=[pl.BlockSpec((N_CHIPS, tr, cs), lambda i: (0, i, 0))], out_specs=pl.BlockSpec((tr, wide), lambda i: (i, 0)),
        out_shape=jax.ShapeDtypeStruct((rows, wide), wg.dtype), compiler_params=_params(("parallel",)),
    )(wg)


def _grad_to_chips(cfg, name, g):
    if name in _CHIP_MAJOR:
        return g
    if name == "w_in":
        return _w_in_grad_to_chips(cfg, g)
    if name not in _RELAYOUT:
        return g.reshape(N_CHIPS, g.shape[0] // N_CHIPS, g.shape[1])
    rows, wide = g.shape
    tr = _pick(rows, _LAYOUT_ROWS, 16)
    cs = jax.eval_shape(lambda v: _from_kernel_layout(cfg, name, v), g).shape[1] // N_CHIPS

    def body(g_ref, o_ref):
        nat = _from_kernel_layout(cfg, name, g_ref[...])
        for k in range(N_CHIPS):
            o_ref[k] = nat[:, k * cs:(k + 1) * cs]

    return pl.pallas_call(
        body, name="layout_grad_" + name, grid=(rows // tr,),
        in_specs=[pl.BlockSpec((tr, wide), lambda i: (i, 0))], out_specs=pl.BlockSpec((N_CHIPS, tr, cs), lambda i: (0, i, 0)),
        out_shape=jax.ShapeDtypeStruct((N_CHIPS, rows, cs), g.dtype), compiler_params=_params(("parallel",)),
    )(g)


def _me():
    return lax.axis_index("x"), lax.axis_index("y"), lax.axis_index("c")


def _other_chips(x, y):
    return [(1 - x, y), (x, 1 - y), (1 - x, 1 - y)]


_ANY = pl.BlockSpec(memory_space=pl.ANY)


BLOCK_ELEMS = 1 << 19
BLOCK_ELEMS_FEW = 1 << 20


def _row_block(rows, cols, mult, elems=BLOCK_ELEMS):
    return _pick(rows, max(mult, elems // cols // mult * mult), mult)


def _scalar(v):
    return v.astype(I32).reshape(1)


def _blocks2d(r, c, mult, elems=BLOCK_ELEMS):
    if r % mult == 0:
        tr = _row_block(r, c, mult, elems)
        return (tr, c), r // tr, lambda i: (i, 0)
    tc = _pick(c, max(LANE, elems // r // LANE * LANE), LANE)
    return (r, tc), c // tc, lambda i: (0, i)


def _by_rows(rows):
    return rows % 32 == 0


def _half_shape(rows, cols):
    return (rows // 2, cols) if _by_rows(rows) else (rows, cols // 2)


def _half_blocks(rows, cols, mult, elems=BLOCK_ELEMS):
    hr, hc = _half_shape(rows, cols)
    block, n, part = _blocks2d(hr, hc, mult, elems)
    assert (hr % mult == 0) == _by_rows(rows), (rows, cols, mult)
    full = (lambda h, i: (h * n + i, 0)) if _by_rows(rows) else (lambda h, i: (0, h * n + i))
    return block, n, full, part


def _half(ref, k, half):
    hr, hc = _half_shape(ref.shape[1], ref.shape[2])
    if _by_rows(ref.shape[1]):
        return ref.at[k, pl.ds(pl.multiple_of(half * hr, 16), hr), :]
    return ref.at[k, :, pl.ds(pl.multiple_of(half * hc, LANE), hc)]


def _shard_blocks(w, br, bc):
    if w.shape[0] == 1:
        def write(ref, v):
            ref[...] = v
        return (lambda f: pl.BlockSpec((None, br, bc), lambda *a: (0, *f(*a)))), (lambda ref: ref[...]), write
    assert w.shape[1] == 1 and br == w.shape[0], w.shape

    def write_rows(ref, v):
        ref[:, 0, :] = v
    return (lambda f: pl.BlockSpec((br, 1, bc), lambda *a: (0, 0, f(*a)[1]))), (lambda ref: ref[:, 0, :]), write_rows


def _stage_shard(name, w, chip, after=None):
    rs, cs = w.shape[0] * w.shape[1], w.shape[2]
    (br, bc), n, idx = _blocks2d(rs, cs, 16, BLOCK_ELEMS_FEW)
    spec, get, _ = _shard_blocks(w, br, bc)

    def body(chip_ref, w_ref, *refs):
        refs[-1][...] = get(w_ref).astype(BF16)

    return pl.pallas_call(
        body, name="stage_" + name,
        grid_spec=pltpu.PrefetchScalarGridSpec(
            num_scalar_prefetch=1, grid=(n,),
            in_specs=[spec(lambda i, chip_ref: idx(i))] + ([] if after is None else [_ANY]),
            out_specs=pl.BlockSpec((None, br, bc), lambda i, chip_ref: (chip_ref[0], *idx(i)))),
        out_shape=jax.ShapeDtypeStruct((N_CHIPS, rs, cs), BF16),
        compiler_params=_params(("parallel",)),
    )(_scalar(chip), w, *([] if after is None else [after]))


_HBM = pl.BlockSpec(memory_space=pltpu.HBM)
_SEM = pl.BlockSpec(memory_space=pltpu.SEMAPHORE)
_EFFECT = pltpu.SideEffectType.DATAFLOW_SIDE_EFFECTING


def _split_start(name, bufs, n_copies, copies, after):
    n = len(bufs)

    def body(*refs):
        for cp in copies(refs[:n], refs[n + 1], refs[n + 2]):
            cp.start()
        refs[-1][...] = jnp.zeros_like(refs[-1])

    res = pl.pallas_call(
        body, name=name,
        out_shape=(pltpu.SemaphoreType.DMA((n_copies,)), pltpu.SemaphoreType.DMA((n_copies,)),
                   *[pltpu.HBM(b.shape, b.dtype) for b in bufs], jax.ShapeDtypeStruct((8, LANE), F32)),
        in_specs=[_HBM] * n + [_ANY], out_specs=(_SEM, _SEM, *[_HBM] * n, pl.BlockSpec(memory_space=pltpu.VMEM)),
        input_output_aliases={i: 2 + i for i in range(n)},
        compiler_params=pltpu.CompilerParams(has_side_effects=_EFFECT),
    )(*[pltpu.with_memory_space_constraint(b, pltpu.HBM) for b in bufs], after)
    return res[0], res[1], list(res[2:2 + n]), res[-1]


def _split_wait(name, send_sems, recv_sems, bufs, after, copies):
    n = len(bufs)

    def body(*refs):
        for cp in copies(refs[:n], refs[n], refs[n + 1]):
            cp.wait_send()
            cp.wait_recv()

    return list(pl.pallas_call(
        body, name=name, out_shape=[pltpu.HBM(b.shape, b.dtype) for b in bufs],
        in_specs=[_HBM] * n + [_SEM, _SEM, _ANY], out_specs=[_HBM] * n,
        input_output_aliases={i: i for i in range(n)},
        compiler_params=pltpu.CompilerParams(has_side_effects=_EFFECT),
    )(*bufs, send_sems, recv_sems, after))


def _gather_to_chips(bufs, send_sems, recv_sems):
    x, y, c = _me()
    return [pltpu.make_async_remote_copy(src_ref=_half(b, 2 * x + y, c), dst_ref=_half(b, 2 * x + y, c),
                                         send_sem=send_sems.at[3 * w + j], recv_sem=recv_sems.at[3 * w + j],
                                         device_id=(cx, cy, c), device_id_type=MESH_ID)
            for w, b in enumerate(bufs) for j, (cx, cy) in enumerate(_other_chips(x, y))]


def _gather_to_sibling(bufs, send_sems, recv_sems):
    x, y, c = _me()
    return [pltpu.make_async_remote_copy(src_ref=_half(b, 2 * cx + cy, c), dst_ref=_half(b, 2 * cx + cy, c),
                                         send_sem=send_sems.at[3 * w + j], recv_sem=recv_sems.at[3 * w + j],
                                         device_id=(x, y, 1 - c), device_id_type=MESH_ID)
            for w, b in enumerate(bufs) for j, (cx, cy) in enumerate(_other_chips(x, y))]


def _pair_exchange(name, grads):
    n = len(grads)

    def body(*refs):
        ins, outs, send_sems, recv_sems = refs[:n], refs[n:2 * n], refs[2 * n], refs[2 * n + 1]
        x, y, c = _me()
        cps = []
        for w, (g_ref, o_ref) in enumerate(zip(ins, outs)):
            cps.append(pltpu.make_async_remote_copy(src_ref=_half(g_ref, slice(None), 1 - c), dst_ref=o_ref,
                                                    send_sem=send_sems.at[w], recv_sem=recv_sems.at[w],
                                                    device_id=(x, y, 1 - c), device_id_type=MESH_ID))
            cps[-1].start()
        for cp in cps:
            cp.wait()

    return pl.pallas_call(
        body, name="pair_exchange_" + name, in_specs=[_ANY] * n, out_specs=[_ANY] * n,
        out_shape=[jax.ShapeDtypeStruct((g.shape[0], *_half_shape(g.shape[1], g.shape[2])), g.dtype) for g in grads],
        scratch_shapes=[pltpu.SemaphoreType.DMA((n,)), pltpu.SemaphoreType.DMA((n,))],
    )(*grads)


def _pair_copies(grads, lands, send_sems, recv_sems):
    x, y, c = _me()
    return [pltpu.make_async_remote_copy(src_ref=_half(g_ref, slice(None), 1 - c), dst_ref=l_ref, send_sem=send_sems.at[w],
                                         recv_sem=recv_sems.at[w], device_id=(x, y, 1 - c), device_id_type=MESH_ID)
            for w, (g_ref, l_ref) in enumerate(zip(grads, lands))]


def _pair_exchange_start(name, grads):
    n = len(grads)
    lands = [lax.empty((g.shape[0], *_half_shape(g.shape[1], g.shape[2])), g.dtype) for g in grads]
    send_sems, recv_sems, bufs, token = _split_start(
        "pair_exchange_start_" + name, [*grads, *lands], n, lambda refs, ss, rs: _pair_copies(refs[:n], refs[n:], ss, rs),
        jnp.zeros((8, LANE), F32))
    return (send_sems, recv_sems, bufs), token


def _pair_exchange_wait(name, state, after):
    send_sems, recv_sems, bufs = state
    n = len(bufs) // 2
    bufs = _split_wait("pair_exchange_wait_" + name, send_sems, recv_sems, bufs, after,
                       lambda refs, ss, rs: _pair_copies(refs[:n], refs[n:], ss, rs))
    return bufs[:n], bufs[n:]


def _pair_sum(name, g, theirs, c):
    (br, bc), nb, full, part = _half_blocks(g.shape[1], g.shape[2], 16, 2 * BLOCK_ELEMS_FEW)

    def body(c_ref, a_ref, b_ref, o_ref):
        o_ref[...] = (a_ref[...].astype(F32) + b_ref[...].astype(F32)).astype(o_ref.dtype)

    return pl.pallas_call(
        body, name="pair_sum_" + name,
        grid_spec=pltpu.PrefetchScalarGridSpec(
            num_scalar_prefetch=1, grid=(N_CHIPS, nb),
            in_specs=[pl.BlockSpec((None, br, bc), lambda k, i, c_ref: (k, *full(c_ref[0], i))),
                      pl.BlockSpec((None, br, bc), lambda k, i, c_ref: (k, *part(i)))],
            out_specs=pl.BlockSpec((None, br, bc), lambda k, i, c_ref: (k, *part(i)))),
        out_shape=jax.ShapeDtypeStruct(theirs.shape, BF16),
        compiler_params=_params(("parallel", "parallel")),
    )(_scalar(c), g, theirs)


def _chip_copies(srcs, lands, send_sems, recv_sems):
    x, y, c = _me()
    return [pltpu.make_async_remote_copy(src_ref=s_ref.at[2 * cx + cy], dst_ref=l_ref.at[j], send_sem=send_sems.at[3 * w + j],
                                         recv_sem=recv_sems.at[3 * w + j], device_id=(cx, cy, c), device_id_type=MESH_ID)
            for w, (s_ref, l_ref) in enumerate(zip(srcs, lands)) for j, (cx, cy) in enumerate(_other_chips(x, y))]


def _chip_exchange_start(name, sums):
    n = len(sums)
    lands = [lax.empty((3,) + s.shape[1:], s.dtype) for s in sums]
    send_sems, recv_sems, bufs, token = _split_start(
        "chip_exchange_start_" + name, [*sums, *lands], 3 * n, lambda refs, ss, rs: _chip_copies(refs[:n], refs[n:], ss, rs),
        jnp.zeros((8, LANE), F32))
    return send_sems, recv_sems, bufs[:n], bufs[n:], token


def _chip_exchange_wait(name, send_sems, recv_sems, sums, lands, after):
    n = len(sums)
    bufs = _split_wait("chip_exchange_wait_" + name, send_sems, recv_sems, [*sums, *lands], after,
                       lambda refs, ss, rs: _chip_copies(refs[:n], refs[n:], ss, rs))
    return bufs[:n], bufs[n:]


def _chip_sum(name, sums, theirs, chip):
    _, h, cs = sums.shape
    (br, bc), nb, idx = _blocks2d(h, cs, 16, BLOCK_ELEMS_FEW)

    def body(chip_ref, s_ref, t_ref, o_ref):
        acc = s_ref[...].astype(F32)
        for k in range(3):
            acc = acc + t_ref[k].astype(F32)
        o_ref[...] = acc

    return pl.pallas_call(
        body, name="chip_sum_" + name,
        grid_spec=pltpu.PrefetchScalarGridSpec(
            num_scalar_prefetch=1, grid=(nb,),
            in_specs=[pl.BlockSpec((None, br, bc), lambda i, chip_ref: (chip_ref[0], *idx(i))),
                      pl.BlockSpec((3, br, bc), lambda i, chip_ref: (0, *idx(i)))],
            out_specs=pl.BlockSpec((br, bc), lambda i, chip_ref: idx(i))),
        out_shape=jax.ShapeDtypeStruct((h, cs), F32),
        compiler_params=_params(("parallel",)),
    )(_scalar(chip), sums, theirs)


def _sibling_copies(halves, lands, send_sems, recv_sems):
    x, y, c = _me()
    return [pltpu.make_async_remote_copy(src_ref=h_ref, dst_ref=l_ref, send_sem=send_sems.at[w], recv_sem=recv_sems.at[w],
                                         device_id=(x, y, 1 - c), device_id_type=MESH_ID)
            for w, (h_ref, l_ref) in enumerate(zip(halves, lands))]


def _sibling_exchange_start(name, halves, after):
    n = len(halves)
    lands = [lax.empty(h.shape, h.dtype) for h in halves]
    send_sems, recv_sems, bufs, token = _split_start(
        "sibling_exchange_start_" + name, [*halves, *lands], n, lambda refs, ss, rs: _sibling_copies(refs[:n], refs[n:], ss, rs),
        after)
    return (send_sems, recv_sems, bufs), token


def _sibling_exchange_wait(name, state, after):
    send_sems, recv_sems, bufs = state
    n = len(bufs) // 2
    bufs = _split_wait("sibling_exchange_wait_" + name, send_sems, recv_sems, bufs, after,
                       lambda refs, ss, rs: _sibling_copies(refs[:n], refs[n:], ss, rs))
    return bufs[:n], bufs[n:]


def _sibling_exchange(name, halves):
    n = len(halves)

    def body(*refs):
        ins, outs, send_sems, recv_sems = refs[:n], refs[n:2 * n], refs[2 * n], refs[2 * n + 1]
        x, y, c = _me()
        cps = []
        for w, (h_ref, o_ref) in enumerate(zip(ins, outs)):
            cps.append(pltpu.make_async_remote_copy(src_ref=h_ref, dst_ref=o_ref, send_sem=send_sems.at[w], recv_sem=recv_sems.at[w],
                                                    device_id=(x, y, 1 - c), device_id_type=MESH_ID))
            cps[-1].start()
        for cp in cps:
            cp.wait()

    return pl.pallas_call(
        body, name="sibling_exchange_" + name, in_specs=[_ANY] * n, out_specs=[_ANY] * n,
        out_shape=[jax.ShapeDtypeStruct(h.shape, h.dtype) for h in halves],
        scratch_shapes=[pltpu.SemaphoreType.DMA((n,)), pltpu.SemaphoreType.DMA((n,))],
    )(*halves)


def _allreduce_small(name, vec, after):
    def body(v_ref, after_ref, o_ref, buf_ref, send_sems, recv_sems):
        x, y, c = _me()
        me = 4 * x + 2 * y + c
        cps = []
        for p in range(1, 8):
            px, py, pc = x ^ (p >> 2), y ^ ((p >> 1) & 1), c ^ (p & 1)
            cps.append(pltpu.make_async_remote_copy(src_ref=v_ref, dst_ref=buf_ref.at[me], send_sem=send_sems.at[p - 1],
                                                    recv_sem=recv_sems.at[p - 1], device_id=(px, py, pc), device_id_type=MESH_ID))
            cps[-1].start()
        buf_ref[me] = v_ref[...]
        for p in range(1, 8):
            theirs = buf_ref.at[me ^ p]
            pltpu.make_async_remote_copy(src_ref=theirs, dst_ref=theirs, send_sem=send_sems.at[p - 1], recv_sem=recv_sems.at[p - 1],
                                         device_id=(x, y, c), device_id_type=MESH_ID).wait_recv()
        for cp in cps:
            cp.wait_send()
        acc = buf_ref[0]
        for k in range(1, 8):
            acc = acc + buf_ref[k]
        o_ref[...] = acc

    vm = pl.BlockSpec(memory_space=pltpu.VMEM)
    return pl.pallas_call(
        body, name=name, in_specs=[vm, _ANY], out_specs=vm, out_shape=jax.ShapeDtypeStruct(vec.shape, F32),
        scratch_shapes=[pltpu.VMEM((8,) + vec.shape, F32), pltpu.SemaphoreType.DMA((7,)), pltpu.SemaphoreType.DMA((7,))],
    )(vec, after)


def _adam_math(w, g, m, v):
    m = ADAM_B1 * m + (1.0 - ADAM_B1) * g
    v = ADAM_B2 * v + (1.0 - ADAM_B2) * (g * g)
    m_hat = m / (1.0 - ADAM_B1 ** ADAM_STEP)
    v_hat = v / (1.0 - ADAM_B2 ** ADAM_STEP)
    return -ADAM_LR * (m_hat / (jnp.sqrt(v_hat) + ADAM_EPS) + ADAM_WD * w), m, v


def _adamw(name, w, g, m, v):
    R, C = w.shape
    tr = _row_block(R, C, 8)

    def body(w_ref, g_ref, m_ref, v_ref, d_ref, nm_ref, nv_ref):
        d_ref[...], nm_ref[...], nv_ref[...] = _adam_math(w_ref[...], g_ref[...], m_ref[...], v_ref[...])

    blk = pl.BlockSpec((tr, C), lambda i: (i, 0))
    return pl.pallas_call(
        body, name=name, grid=(R // tr,), in_specs=[blk] * 4, out_specs=[blk] * 3,
        out_shape=[jax.ShapeDtypeStruct((R, C), F32)] * 3, compiler_params=_params(("parallel",)),
    )(w, g, m, v)


def _adamw_halves(name, w, mine, theirs, m, v, c):
    rs, cs = w.shape[0] * w.shape[1], w.shape[2]
    (br, bc), nb, whole, half = _half_blocks(rs, cs, 8)
    spec, get, put = _shard_blocks(w, br, bc)

    def body(c_ref, w_ref, a_ref, b_ref, m_ref, v_ref, g_ref, d_ref, nm_ref, nv_ref):
        g = jnp.where(pl.program_id(0) == c_ref[0], a_ref[...], b_ref[...])
        put(g_ref, g)
        for ref, val in zip((d_ref, nm_ref, nv_ref), _adam_math(get(w_ref), g, get(m_ref), get(v_ref))):
            put(ref, val)

    full = spec(lambda s, i, c_ref: whole(s, i))
    part = pl.BlockSpec((br, bc), lambda s, i, c_ref: half(i))
    return pl.pallas_call(
        body, name=name,
        grid_spec=pltpu.PrefetchScalarGridSpec(num_scalar_prefetch=1, grid=(2, nb), in_specs=[full, part, part, full, full],
                                               out_specs=[full] * 4),
        out_shape=[jax.ShapeDtypeStruct(w.shape, F32)] * 4, compiler_params=_params(("parallel", "parallel")),
    )(_scalar(c), w, mine, theirs, m, v)


def _pack_small(arrs, lanes=LANE):
    flat = jnp.concatenate([a.reshape(-1) for a in arrs])
    n = -(-flat.shape[0] // (8 * lanes)) * 8 * lanes
    return jnp.pad(flat, (0, n - flat.shape[0])).reshape(8, n // 8)


def _unpack_small(vec, shapes):
    flat, out, off = vec.reshape(-1), [], 0
    for s in shapes:
        out.append(flat[off:off + s[0] * s[1]].reshape(s))
        off += s[0] * s[1]
    return out


class _LateWeights:
    def __init__(self, cfg, tag, names, staged, after):
        self.cfg, self.tag, self.names, self.k = cfg, tag, names, 3 * len(names)
        self.send, self.recv, self.bufs, self.token = _split_start(f"gather_{tag}_chips_start", staged, self.k, _gather_to_chips,
                                                                    after)

    def pass_on(self, after):
        bufs = _split_wait(f"gather_{self.tag}_chips_wait", self.send, self.recv, self.bufs, after, _gather_to_chips)
        self.send, self.recv, self.bufs, token = _split_start(f"gather_{self.tag}_sibling_start", bufs, self.k, _gather_to_sibling,
                                                               self.token)
        return token

    def arrived(self, after):
        bufs = _split_wait(f"gather_{self.tag}_sibling_wait", self.send, self.recv, self.bufs, after, _gather_to_sibling)
        return {n: _gathered_to_kernel(self.cfg, n, b) for n, b in zip(self.names, bufs)}


def _step(cfg, a):
    chip = 2 * lax.axis_index("x") + lax.axis_index("y")
    core = lax.axis_index("c")
    big = BIG

    ffn = ("w_gate", "w_up", "w_down")
    first = ("w_in", "w_uq", "w_ukv")
    sp = {n: a[n] for n in SMALL}
    sharded = _pack_small([a[n] for n in SMALL_SHARDED], 2 * LANE)
    slabs = jnp.where(lax.broadcasted_iota(I32, (N_CHIPS,) + sharded.shape, 0) == chip, sharded[None], 0.0)
    staged = {"w_in": _stage_shard("w_in", a["w_in"], chip)}
    in_weight = _LateWeights(cfg, "in", ("w_in", "sharded_small"), [staged["w_in"], slabs], jnp.zeros((8, LANE), F32))
    staged.update({n: _stage_shard(n, a[n], chip, in_weight.token) for n in big if n != "w_in"})
    in_weight.pass_on(staged[big[-1]])
    W = in_weight.arrived(sp["mix_pre_g"])
    allp = W.pop("sharded_small").reshape((N_CHIPS,) + sharded.shape)
    per_chip = [_unpack_small(allp[ch], [a[n].shape for n in SMALL_SHARDED]) for ch in range(N_CHIPS)]
    for k, n in enumerate(SMALL_SHARDED):
        sp[n] = jnp.concatenate([per_chip[ch][k] for ch in range(N_CHIPS)], axis=1)

    mla_weights = _LateWeights(cfg, "mla", first[1:], [staged[n] for n in first[1:]], W["w_in"])
    out_weight = _LateWeights(cfg, "out", ("w_out",), [staged["w_out"]], mla_weights.token)
    ffn_weights = _LateWeights(cfg, "ffn", ffn[:2], [staged[n] for n in ffn[:2]], out_weight.token)
    down_weight = _LateWeights(cfg, "down", ffn[2:], [staged[n] for n in ffn[2:]], ffn_weights.token)
    sp["mix_pre_g"] = sp["mix_pre_g"] + (mla_weights.token[0, 0] + out_weight.token[0, 0] + ffn_weights.token[0, 0]
                                         + down_weight.token[0, 0])

    state = {}

    def ffn_grads_ready(grads):
        state["ffn_pairs"], token = _pair_exchange_start("ffn", [_grad_to_chips(cfg, n, grads[n]) for n in ffn_grads])
        return token

    def pair_sums(names, grads, theirs):
        return [_pair_sum(n, g, t, core) for n, g, t in zip(names, grads, theirs)]

    def early_grads_ready(grads):
        g_out = [_grad_to_chips(cfg, "w_out", grads["w_out"])]
        g_ffn, t_ffn = _pair_exchange_wait("ffn", state["ffn_pairs"], g_out[0])
        sums = pair_sums(ffn_grads, g_ffn, t_ffn) + pair_sums(["w_out"], g_out, _pair_exchange("out", g_out))
        state["early"] = _chip_exchange_start("early", sums)
        return state["early"][-1]

    def reduced_halves(tag, names, after):
        send_sems, recv_sems, s_bufs, l_bufs, _ = state[tag]
        s_bufs, l_bufs = _chip_exchange_wait(tag, send_sems, recv_sems, s_bufs, l_bufs, after)
        return [_chip_sum(n, s, t, chip) for n, s, t in zip(names, s_bufs, l_bufs)]

    def in_grad_ready(grads):
        grads = [_grad_to_chips(cfg, n, grads[n]) for n in first]
        state["rest"] = _chip_exchange_start("rest", pair_sums(first, grads, _pair_exchange("rest", grads)))
        return state["rest"][-1]

    ffn_grads = ("w_down", "w_gate", "w_up")
    early = ffn_grads + ("w_out",)
    loss, grad_x, gW, gs = _local_grads(cfg, a["x"], a["loss_target"], W, sp, mla_weights, out_weight, ffn_weights, down_weight,
                                        ffn_grads_ready, early_grads_ready, in_grad_ready)
    out = {"grad_x": grad_x}

    def adamw(names, mine, theirs):
        for n, gm, gt in zip(names, mine, theirs):
            out["grad_" + n], out["delta_" + n], out["new_m_" + n], out["new_v_" + n] = _adamw_halves(
                "adamw_" + n, a[n], gm, gt, a["m_" + n], a["v_" + n], core)

    mine = reduced_halves("early", early, grad_x)
    theirs = _sibling_exchange("early", mine[:1])
    later, _ = _sibling_exchange_start("early", mine[1:], theirs[0])
    adamw(early[:1], mine[:1], theirs)
    adamw(early[1:], *_sibling_exchange_wait("early", later, out["new_v_" + early[0]]))
    mine = reduced_halves("rest", first, out["new_v_" + early[-1]])
    theirs = _sibling_exchange("rest", mine)
    adamw(first, mine, theirs)

    shapes = [gs[n].shape for n in SMALL] + [(1, LANE)]
    red = _unpack_small(_allreduce_small("allreduce_small", _pack_small([gs[n] for n in SMALL] + [loss]), theirs[0]), shapes)
    g_small = dict(zip(SMALL, red[:-1]))
    for n in SMALL_SHARDED:
        cs = a[n].shape[1]
        g_small[n] = lax.dynamic_slice_in_dim(g_small[n], chip * cs, cs, axis=1)
    out["loss"] = red[-1][0, 0]
    sshapes = [a[n].shape for n in SMALL]
    d, nm, nv = _adamw("adamw_small", _pack_small([a[n] for n in SMALL]), _pack_small([g_small[n] for n in SMALL]),
                       _pack_small([a["m_" + n] for n in SMALL]), _pack_small([a["v_" + n] for n in SMALL]))
    for n, dd, mm, vv in zip(SMALL, _unpack_small(d, sshapes), _unpack_small(nm, sshapes), _unpack_small(nv, sshapes)):
        out["grad_" + n], out["delta_" + n], out["new_m_" + n], out["new_v_" + n] = g_small[n], dd, mm, vv
    return out


def kernel(x, mix_pre_g, w_in, q_norm_g, w_uq, kv_norm_g, w_ukv, ssm_conv_w, ssm_conv_b, dt_bias, a_log, d_skip, ssm_norm_g, w_out, mix_post_g, ffn_pre_g, w_gate, w_up, ffn_conv_w, ffn_conv_b, w_down, ffn_post_g, loss_target, m_mix_pre_g, m_w_in, m_q_norm_g, m_w_uq, m_kv_norm_g, m_w_ukv, m_ssm_conv_w, m_ssm_conv_b, m_dt_bias, m_a_log, m_d_skip, m_ssm_norm_g, m_w_out, m_mix_post_g, m_ffn_pre_g, m_w_gate, m_w_up, m_ffn_conv_w, m_ffn_conv_b, m_w_down, m_ffn_post_g, v_mix_pre_g, v_w_in, v_q_norm_g, v_w_uq, v_kv_norm_g, v_w_ukv, v_ssm_conv_w, v_ssm_conv_b, v_dt_bias, v_a_log, v_d_skip, v_ssm_norm_g, v_w_out, v_mix_post_g, v_ffn_pre_g, v_w_gate, v_w_up, v_ffn_conv_w, v_ffn_conv_b, v_w_down, v_ffn_post_g):
    args = dict(locals())
    def given(k, v):
        if k in ("w_in", "m_w_in", "v_w_in"):
            return jnp.transpose(v, (2, 0, 1))
        return v if k.removeprefix("m_").removeprefix("v_") in BIG or v.ndim < 3 else v[0]

    out = _step(_FULL, {k: given(k, v) for k, v in args.items()})
    res = [out["loss"], out["grad_x"][None]]
    for pre in ("grad_", "delta_", "new_m_", "new_v_"):
        for n in WEIGHTS:
            o = out[pre + n]
            res.append(jnp.transpose(o, (1, 2, 0)) if n == "w_in" else o if n in BIG or args[n].ndim < 3 else o[None])
    return tuple(res)
```

```python
import math

import jax
import jax.numpy as jnp
from jax import lax
from jax.experimental import pallas as pl
from jax.experimental.pallas import tpu as pltpu

F32, BF16, I32 = jnp.float32, jnp.bfloat16, jnp.int32
NN = (((1,), (0,)), ((), ()))
NT = (((1,), (1,)), ((), ()))
TN = (((0,), (0,)), ((), ()))
HI = lax.Precision.HIGHEST
MESH_ID = pl.DeviceIdType.MESH

EPS = 1e-6
CHUNK = 64
NOPE, ROPE, VH = 128, 64, 128
ROPE_THETA = 10000.0
HP, NST = 64, 128
SSM_K, FFN_K = 4, 3
LANE = 128
N_CHIPS = 4
VMEM_LIMIT = 52 * 1024 * 1024
MM_TILE, MM_TILE_K = 1408, 2816

ADAM_LR, ADAM_B1, ADAM_B2, ADAM_EPS, ADAM_WD, ADAM_STEP = 0.001, 0.9, 0.999, 1e-08, 0.01, 10


class _Cfg:
    def __init__(self, S, D, QL, KVL, H, HS, G, DFF, T):
        self.S, self.D, self.QL, self.KVL, self.H, self.HS, self.G, self.DFF, self.T = S, D, QL, KVL, H, HS, G, DFF, T
        self.INNER = HS * HP
        self.CONVCH = self.INNER + 2 * G * NST
        self.QW = H * (NOPE + ROPE)
        self.KVW = H * (NOPE + VH)
        self.MLAW = H * VH
        self.MIXW = self.MLAW + self.INNER
        self.IN_COLS = QL + KVL + ROPE + self.INNER + self.CONVCH + HS
        natural, at = {}, 0
        for name, w in (("c_q", QL), ("c_kv", KVL), ("kr", ROPE), ("z", self.INNER), ("xbc", self.CONVCH), ("dt", HS)):
            natural[name] = (at, w)
            at += w
        self.seg, taken = {}, []
        for name in sorted(natural, key=lambda n: -natural[n][1]):
            w = -(-natural[name][1] // LANE) * LANE
            off = next(o for o in range(0, self.IN_COLS * 2, w) if all(o + w <= t or o >= t + tw for t, tw in taken))
            taken.append((off, w))
            self.seg[name] = (off, w) + natural[name]
        self.EXT = max(o + w for o, w in taken)
        self.NPAIR = HS // 2
        self.REP = HS // G

    def window(self, name):
        off, w, _, _ = self.seg[name]
        return w, off // w


_FULL = _Cfg(S=2048, D=2048, QL=768, KVL=512, H=8, HS=16, G=2, DFF=5632, T=256)
BIG = ("w_in", "w_uq", "w_ukv", "w_out", "w_gate", "w_up", "w_down")

SMALL = ("mix_pre_g", "q_norm_g", "kv_norm_g", "ssm_conv_w", "ssm_conv_b", "dt_bias", "a_log", "d_skip", "ssm_norm_g",
         "mix_post_g", "ffn_pre_g", "ffn_conv_w", "ffn_conv_b", "ffn_post_g")
SMALL_SHARDED = ("ssm_conv_w", "ffn_conv_w")
WEIGHTS = ("mix_pre_g", "w_in", "q_norm_g", "w_uq", "kv_norm_g", "w_ukv", "ssm_conv_w", "ssm_conv_b", "dt_bias", "a_log",
           "d_skip", "ssm_norm_g", "w_out", "mix_post_g", "ffn_pre_g", "w_gate", "w_up", "ffn_conv_w", "ffn_conv_b",
           "w_down", "ffn_post_g")


def _pick(n, target, mult):
    best = None
    for d in range(mult, min(n, target) + 1, mult):
        if n % d == 0:
            best = d
    return best if best is not None else n


def _params(sem=None):
    kw = dict(vmem_limit_bytes=VMEM_LIMIT)
    if sem is not None:
        kw["dimension_semantics"] = sem
    return pltpu.CompilerParams(**kw)


def _dot(a, b, dims=NN, precision=None):
    return lax.dot_general(a, b, dims, preferred_element_type=F32, precision=precision)


def _sigmoid(x):
    return 1.0 / (1.0 + jnp.exp(-x))


def _rs(x):
    return lax.rsqrt(jnp.mean(x * x, axis=-1, keepdims=True) + EPS)


def _rms_back(xh, r, dn):
    return r * (dn - xh * jnp.mean(dn * xh, axis=-1, keepdims=True))


def _colsum(v):
    return jnp.sum(v, axis=0, keepdims=True)


def _matmul(name, a, b, mode, out_dtype, a2=None, b2=None, chips=False, after=None):
    cs = None
    if mode == "nn":
        (M, K), N = a.shape, b.shape[-1]
        if chips:
            cs, N = N, N_CHIPS * N
    elif mode == "nt":
        (M, K), N = a.shape, b.shape[-2]
        if chips:
            cs = b.shape[-1]
    else:
        (K, M), N = a.shape, b.shape[1]
        if chips:
            cs = N // N_CHIPS
    tm = _pick(M, MM_TILE, LANE)
    tn = _pick(cs if chips and mode != "nt" else N, MM_TILE, LANE)
    tk = _pick(cs, MM_TILE, LANE) if chips and mode == "nt" else _pick(K, MM_TILE_K, LANE)
    nk = K // tk
    dims = {"nn": NN, "nt": NT, "tn": TN}[mode]
    a_spec = pl.BlockSpec((tk, tm), lambda i, j, k: (k, i)) if mode == "tn" else pl.BlockSpec((tm, tk), lambda i, j, k: (i, k))
    b_spec = pl.BlockSpec((tn, tk), lambda i, j, k: (j, k)) if mode == "nt" else pl.BlockSpec((tk, tn), lambda i, j, k: (k, j))
    o_spec = pl.BlockSpec((tm, tn), lambda i, j, k: (i, j))
    o_shape = (M, N)
    if chips and mode == "nn":
        per = cs // tn
        b_spec = pl.BlockSpec((None, tk, tn), lambda i, j, k: (j // per, k, j % per))
    elif chips and mode == "nt":
        per = cs // tk
        b_spec = pl.BlockSpec((None, tn, tk), lambda i, j, k: (k // per, j, k % per))
    elif chips:
        per = cs // tn
        o_spec = pl.BlockSpec((None, tm, tn), lambda i, j, k: (j // per, i, j % per))
        o_shape = (N_CHIPS, M, cs)
    two = a2 is not None

    def product(refs):
        part = _dot(refs[0][...].astype(BF16), refs[1][...].astype(BF16), dims)
        if two:
            part += _dot(refs[2][...].astype(BF16), refs[3][...].astype(BF16), dims)
        return part

    def body_whole_k(*refs):
        refs[-1][...] = product(refs).astype(refs[-1].dtype)

    def body(*refs):
        o_ref, acc_ref = refs[-2], refs[-1]
        k = pl.program_id(2)

        @pl.when(k == 0)
        def _():
            acc_ref[...] = product(refs)

        @pl.when(k > 0)
        def _():
            acc_ref[...] += product(refs)

        @pl.when(k == nk - 1)
        def _():
            o_ref[...] = acc_ref[...].astype(o_ref.dtype)

    ins = ((a, b, a2, b2) if two else (a, b)) + (() if after is None else (after,))
    return pl.pallas_call(
        body_whole_k if nk == 1 else body, name=name, grid=(M // tm, N // tn, nk),
        in_specs=[a_spec, b_spec] * (2 if two else 1) + ([] if after is None else [pl.BlockSpec(memory_space=pl.ANY)]),
        out_specs=o_spec,
        out_shape=jax.ShapeDtypeStruct(o_shape, out_dtype),
        scratch_shapes=[] if nk == 1 else [pltpu.VMEM((tm, tn), F32)],
        compiler_params=_params(("parallel", "parallel", "arbitrary")),
    )(*ins)


def _window(a):
    return (a[0], *a[1]) if isinstance(a, tuple) else (a, a.shape[1], 0)


def _rowwise(name, fn, rows, mats, outs, reds, ts):
    rows, widths, blocks = zip(*[_window(a) for a in rows])
    S = rows[0].shape[0]
    nr, nm, no = len(rows), len(mats), len(outs)

    def body(*refs):
        res = fn(*[r[...] for r in refs[:nr + nm]])
        res = res if isinstance(res, (tuple, list)) else (res,)
        for r, v in zip(refs[nr + nm:nr + nm + no], res[:no]):
            r[...] = v.astype(r.dtype)
        first = pl.program_id(0) == 0
        for r, v in zip(refs[nr + nm + no:], res[no:]):
            @pl.when(first)
            def _():
                r[...] = jnp.broadcast_to(v, r.shape)

            @pl.when(jnp.logical_not(first))
            def _():
                r[...] += jnp.broadcast_to(v, r.shape)

    in_specs = [pl.BlockSpec((ts, w), lambda i, b=b: (i, b)) for w, b in zip(widths, blocks)]
    in_specs += [pl.BlockSpec(m.shape, lambda i, nd=m.ndim: (0,) * nd) for m in mats]
    out_specs = [pl.BlockSpec((ts, w), lambda i: (i, 0)) for w, _ in outs]
    out_specs += [pl.BlockSpec(s, lambda i: (0, 0)) for s in reds]
    out_shape = [jax.ShapeDtypeStruct((S, w), dt) for w, dt in outs] + [jax.ShapeDtypeStruct(s, F32) for s in reds]
    return pl.pallas_call(
        body, name=name, grid=(S // ts,), in_specs=in_specs, out_specs=out_specs, out_shape=out_shape,
        compiler_params=_params(("arbitrary",) if reds else ("parallel",)),
    )(*rows, *mats)


def _shift_down(v, s):
    if s == 0:
        return v
    rows = lax.broadcasted_iota(I32, v.shape, 0)
    return jnp.where(rows >= s, pltpu.roll(v, s, 0), 0.0)


def _shift_up(v, s):
    if s == 0:
        return v
    n = v.shape[0]
    rows = lax.broadcasted_iota(I32, v.shape, 0)
    return jnp.where(rows < n - s, pltpu.roll(v, n - s, 0), 0.0)


def _conv(x, w, b):
    K = w.shape[0]
    y = jnp.broadcast_to(b, x.shape)
    for k in range(K):
        y = y + w[k:k + 1, :] * _shift_down(x, K - 1 - k)
    return y


def _conv_back(x, w, dc):
    K = w.shape[0]
    dx = jnp.zeros_like(x)
    dw = []
    for k in range(K):
        up = _shift_up(dc, K - 1 - k)
        dx = dx + w[k:k + 1, :] * up
        dw.append(_colsum(up * x))
    return dx, jnp.concatenate(dw, axis=0), _colsum(dc)


def _colwise(name, fn, cols, vecs, outs, pouts, tc):
    cols, widths, blocks = zip(*[_window(a) for a in cols])
    S, C = cols[0].shape[0], widths[0]
    firsts = [b * (C // tc) for b in blocks]
    nc_, nv, no = len(cols), len(vecs), len(outs)

    def body(*refs):
        res = fn(*[r[...] for r in refs[:nc_ + nv]])
        res = res if isinstance(res, (tuple, list)) else (res,)
        for r, v in zip(refs[nc_ + nv:], res):
            r[...] = v.astype(r.dtype)

    in_specs = [pl.BlockSpec((S, tc), lambda j, f=f: (0, f + j)) for f in firsts]
    in_specs += [pl.BlockSpec((v.shape[0], tc), lambda j: (0, j)) for v in vecs]
    out_specs = [pl.BlockSpec((S, tc), lambda j: (0, j)) for _ in outs] + [pl.BlockSpec((k, tc), lambda j: (0, j)) for k in pouts]
    out_shape = [jax.ShapeDtypeStruct((S, C), dt) for dt in outs] + [jax.ShapeDtypeStruct((k, C), F32) for k in pouts]
    return pl.pallas_call(
        body, name=name, grid=(C // tc,), in_specs=in_specs, out_specs=out_specs, out_shape=out_shape,
        compiler_params=_params(("parallel",)),
    )(*cols, *vecs)


_G0, _G1 = math.sqrt(2.0 / math.pi), 0.044715


def _gelu(g):
    th = jnp.tanh(_G0 * (g + _G1 * g * g * g))
    return 0.5 * g * (1.0 + th), th


def _ffn_act(gate_pre, up, w, b):
    act, _ = _gelu(_conv(gate_pre, w, b))
    return act * up


def _ffn_act_back(dact, gate_pre, up, w, b):
    g = _conv(gate_pre, w, b)
    ge, th = _gelu(g)
    dge = 0.5 * (1.0 + th) + 0.5 * g * (1.0 - th * th) * _G0 * (1.0 + 3.0 * _G1 * g * g)
    dup = dact * ge
    dgate_pre, dw, db = _conv_back(gate_pre, w, dact * up * dge)
    return dgate_pre, dup, dw, db


def _ssm_act(xbc, w, b):
    c = _conv(xbc, w, b)
    return c * _sigmoid(c)


def _ssm_act_back(dxc, xbc, w, b):
    c = _conv(xbc, w, b)
    sg = _sigmoid(c)
    return _conv_back(xbc, w, dxc * sg * (1.0 + c * (1.0 - sg)))


def _rope_tables(S):
    inv = 1.0 / (ROPE_THETA ** (jnp.arange(0, ROPE, 2, dtype=F32) / ROPE))
    ang = jnp.arange(S, dtype=F32)[:, None] * inv[None, :]
    cos, sin = jnp.cos(ang), jnp.sin(ang)
    return jnp.tile(cos, (1, 4)), jnp.tile(jnp.concatenate([-sin, sin], axis=1), (1, 2))


def _swap_halves(x):
    lane = lax.broadcasted_iota(I32, x.shape, 1)
    w = x.shape[1]
    return jnp.where((lane % ROPE) < ROPE // 2, pltpu.roll(x, w - ROPE // 2, 1), pltpu.roll(x, ROPE // 2, 1))


def _rot(x, cos2, sin2):
    return x * cos2 + _swap_halves(x) * sin2


def _rot_back(dy, cos2, sin2):
    return dy * cos2 + _swap_halves(dy * sin2)


def _mla_pack(cfg, q, kv, kr, cos2, sin2):
    S, H = cfg.S, cfg.H
    ts = _pick(S, 256, 8)
    kr, _, kr_block = _window(kr)

    def body(q_ref, kv_ref, kr_ref, c_ref, s_ref, Q_ref, K_ref, V_ref):
        c2, s2 = c_ref[...], s_ref[...]
        krr = _rot(kr_ref[...], c2, s2)
        kr_half = (krr.astype(BF16), pltpu.roll(krr, ROPE, 1).astype(BF16))
        for j in range(H // 2):
            qr = _rot(q_ref[:, (H + j) * LANE:(H + j + 1) * LANE], c2, s2).astype(BF16)
            for h in (2 * j, 2 * j + 1):
                Q_ref[h, :, 0:LANE] = q_ref[:, h * LANE:(h + 1) * LANE].astype(BF16)
                Q_ref[h, :, LANE:] = qr
                K_ref[h, :, 0:LANE] = kv_ref[:, h * LANE:(h + 1) * LANE].astype(BF16)
                K_ref[h, :, LANE:] = kr_half[h % 2]
                V_ref[h] = kv_ref[:, (H + h) * LANE:(H + h + 1) * LANE].astype(BF16)

    tab = pl.BlockSpec((ts, LANE), lambda i: (i, 0))
    heads = lambda w: pl.BlockSpec((H, ts, w), lambda i: (0, i, 0))
    return pl.pallas_call(
        body, name="mla_pack", grid=(S // ts,),
        in_specs=[pl.BlockSpec((ts, cfg.QW), lambda i: (i, 0)), pl.BlockSpec((ts, cfg.KVW), lambda i: (i, 0)),
                  pl.BlockSpec((ts, LANE), lambda i: (i, kr_block)), tab, tab],
        out_specs=[heads(2 * LANE), heads(2 * LANE), heads(LANE)],
        out_shape=[jax.ShapeDtypeStruct((H, S, 2 * LANE), BF16), jax.ShapeDtypeStruct((H, S, 2 * LANE), BF16),
                   jax.ShapeDtypeStruct((H, S, LANE), BF16)],
        compiler_params=_params(("parallel",)),
    )(q, kv, kr, cos2, sin2)


def _mla_unpack(cfg, dQ, dK, dV, cos2, sin2):
    S, H = cfg.S, cfg.H
    ts = _pick(S, 256, 8)

    def body(dQ_ref, dK_ref, dV_ref, c_ref, s_ref, dq_ref, dkv_ref, dkr_ref):
        c2, s2 = c_ref[...], s_ref[...]
        lo = lax.broadcasted_iota(I32, (ts, LANE), 1) < ROPE
        tk = jnp.zeros((ts, LANE), F32)
        for h in range(H):
            dq_ref[:, h * LANE:(h + 1) * LANE] = dQ_ref[h, :, 0:LANE].astype(BF16)
            dkv_ref[:, h * LANE:(h + 1) * LANE] = dK_ref[h, :, 0:LANE].astype(BF16)
            dkv_ref[:, (H + h) * LANE:(H + h + 1) * LANE] = dV_ref[h].astype(BF16)
            own = lo if h % 2 == 0 else jnp.logical_not(lo)
            tk = tk + jnp.where(own, dK_ref[h, :, LANE:], 0.0)
        for j in range(H // 2):
            dr = dQ_ref[2 * j, :, LANE:] + dQ_ref[2 * j + 1, :, LANE:]
            dq_ref[:, (H + j) * LANE:(H + j + 1) * LANE] = _rot_back(dr, c2, s2).astype(BF16)
        dkr_rot = jnp.where(lo, tk + pltpu.roll(tk, ROPE, 1), 0.0)
        dkr_ref[...] = _rot_back(dkr_rot, c2, s2).astype(BF16)

    tab = pl.BlockSpec((ts, LANE), lambda i: (i, 0))
    return pl.pallas_call(
        body, name="mla_unpack", grid=(S // ts,),
        in_specs=[pl.BlockSpec((H, ts, 2 * LANE), lambda i: (0, i, 0)), pl.BlockSpec((H, ts, 2 * LANE), lambda i: (0, i, 0)),
                  pl.BlockSpec((H, ts, LANE), lambda i: (0, i, 0)), tab, tab],
        out_specs=[pl.BlockSpec((ts, cfg.QW), lambda i: (i, 0)), pl.BlockSpec((ts, cfg.KVW), lambda i: (i, 0)), tab],
        out_shape=[jax.ShapeDtypeStruct((S, cfg.QW), BF16), jax.ShapeDtypeStruct((S, cfg.KVW), BF16),
                   jax.ShapeDtypeStruct((S, LANE), BF16)],
        compiler_params=_params(("parallel",)),
    )(dQ, dK, dV, cos2, sin2)


_ATT_T = 256
_ATT_HB = 8
_ATT_SCALE = (NOPE + ROPE) ** -0.5


def _diag_mask(transposed=False):
    r = lax.broadcasted_iota(I32, (_ATT_T, _ATT_T), 0) // CHUNK
    c = lax.broadcasted_iota(I32, (_ATT_T, _ATT_T), 1) // CHUNK
    return r <= c if transposed else c <= r


def _row_form(col):
    return jnp.broadcast_to(col, (col.shape[0], LANE)).T[0:8, :]


def _attn_fwd(cfg, Q, K, V):
    S, H, T, HB = cfg.S, cfg.H, _ATT_T, min(cfg.H, _ATT_HB)

    def body(q_ref, k_ref, v_ref, o_ref, lse_ref, lse_t_ref):
        qi = pl.program_id(1)

        def head_step(b, kb, carry, mask):
            m, l, acc = carry
            ks = pl.multiple_of(kb * T, T)
            s = _dot(q_ref[b], k_ref[b, pl.ds(ks, T), :], NT) * _ATT_SCALE
            if mask is not None:
                s = jnp.where(mask, s, -1e30)
            m_new = jnp.maximum(m, jnp.max(s, axis=1, keepdims=True))
            p = jnp.exp(s - m_new)
            alpha = jnp.exp(m - m_new)
            l = alpha * l + jnp.sum(p, axis=1, keepdims=True)
            acc = alpha * acc + _dot(p.astype(BF16), v_ref[b, pl.ds(ks, T), :])
            return m_new, l, acc

        def step(kb, carry, mask=None):
            return tuple(head_step(b, kb, carry[b], mask) for b in range(HB))

        init = (jnp.full((T, 1), -1e30, F32), jnp.zeros((T, 1), F32), jnp.zeros((T, VH), F32))
        done = step(qi, lax.fori_loop(0, qi, step, (init,) * HB), _diag_mask())
        for b, (m, l, acc) in enumerate(done):
            o_ref[:, b * LANE:(b + 1) * LANE] = acc / l
            lse = m + jnp.log(l)
            lse_ref[:, b * LANE:(b + 1) * LANE] = jnp.broadcast_to(lse, (T, LANE))
            lse_t_ref[b] = _row_form(lse)

    return pl.pallas_call(
        body, name="attn_fwd", grid=(H // HB, S // T),
        in_specs=[pl.BlockSpec((HB, T, 2 * LANE), lambda h, i: (h, i, 0)), pl.BlockSpec((HB, S, 2 * LANE), lambda h, i: (h, 0, 0)),
                  pl.BlockSpec((HB, S, LANE), lambda h, i: (h, 0, 0))],
        out_specs=[pl.BlockSpec((T, HB * LANE), lambda h, i: (i, h)), pl.BlockSpec((T, HB * LANE), lambda h, i: (i, h)),
                   pl.BlockSpec((HB, 8, T), lambda h, i: (h, 0, i))],
        out_shape=[jax.ShapeDtypeStruct((S, H * LANE), F32), jax.ShapeDtypeStruct((S, H * LANE), F32),
                   jax.ShapeDtypeStruct((H, 8, S), F32)],
        compiler_params=_params(("parallel", "parallel")),
    )(Q, K, V)


def _attn_dq(cfg, Q, K, V, do, o, lse, after):
    S, H, T, HB = cfg.S, cfg.H, _ATT_T, min(cfg.H, _ATT_HB)

    def body(q_ref, k_ref, v_ref, do_ref, o_ref, lse_ref, after_ref, dq_ref, dl_t_ref):
        qi = pl.program_id(1)
        do = [do_ref[:, b * LANE:(b + 1) * LANE] for b in range(HB)]
        delta = [jnp.sum(do[b] * o_ref[:, b * LANE:(b + 1) * LANE], axis=1, keepdims=True) for b in range(HB)]
        dob = [d.astype(BF16) for d in do]

        def head_step(b, kb, dq, mask):
            ks = pl.multiple_of(kb * T, T)
            k = k_ref[b, pl.ds(ks, T), :]
            s = _dot(q_ref[b], k, NT) * _ATT_SCALE
            if mask is not None:
                s = jnp.where(mask, s, -1e30)
            p = jnp.exp(s - lse_ref[:, b * LANE:b * LANE + 1])
            dp = _dot(dob[b], v_ref[b, pl.ds(ks, T), :], NT)
            ds = p * (dp - delta[b]) * _ATT_SCALE
            return dq + _dot(ds.astype(BF16), k)

        def step(kb, dqs, mask=None):
            return tuple(head_step(b, kb, dqs[b], mask) for b in range(HB))

        dqs = step(qi, lax.fori_loop(0, qi, step, (jnp.zeros((T, 2 * LANE), F32),) * HB), _diag_mask())
        for b in range(HB):
            dq_ref[b] = dqs[b]
            dl_t_ref[b] = _row_form(delta[b])

    col = pl.BlockSpec((T, HB * LANE), lambda h, i: (i, h))
    return pl.pallas_call(
        body, name="attn_dq", grid=(H // HB, S // T),
        in_specs=[pl.BlockSpec((HB, T, 2 * LANE), lambda h, i: (h, i, 0)), pl.BlockSpec((HB, S, 2 * LANE), lambda h, i: (h, 0, 0)),
                  pl.BlockSpec((HB, S, LANE), lambda h, i: (h, 0, 0)), col, col, col, _ANY],
        out_specs=[pl.BlockSpec((HB, T, 2 * LANE), lambda h, i: (h, i, 0)), pl.BlockSpec((HB, 8, T), lambda h, i: (h, 0, i))],
        out_shape=[jax.ShapeDtypeStruct((H, S, 2 * LANE), F32), jax.ShapeDtypeStruct((H, 8, S), F32)],
        compiler_params=_params(("parallel", "parallel")),
    )(Q, K, V, do, o, lse, after)


def _attn_dkv(cfg, Q, K, V, do, lse_t, delta_t):
    S, H, T, HB = cfg.S, cfg.H, _ATT_T, min(cfg.H, _ATT_HB)
    nq = S // T

    def body(q_ref, k_ref, v_ref, do_ref, lse_ref, dl_ref, dk_ref, dv_ref):
        kb = pl.program_id(1)

        def head_step(b, qi, carry, mask):
            dk, dv = carry
            qs = pl.multiple_of(qi * T, T)
            q = q_ref[b, pl.ds(qs, T), :]
            dob = do_ref[pl.ds(qs, T), b * LANE:(b + 1) * LANE].astype(BF16)
            s = _dot(k_ref[b], q, NT) * _ATT_SCALE
            if mask is not None:
                s = jnp.where(mask, s, -1e30)
            p = jnp.exp(s - lse_ref[b, 0:1, pl.ds(qs, T)])
            dv = dv + _dot(p.astype(BF16), dob)
            dp = _dot(v_ref[b], dob, NT)
            ds = p * (dp - dl_ref[b, 0:1, pl.ds(qs, T)]) * _ATT_SCALE
            dk = dk + _dot(ds.astype(BF16), q)
            return dk, dv

        def step(qi, carry, mask=None):
            return tuple(head_step(b, qi, carry[b], mask) for b in range(HB))

        zero = (jnp.zeros((T, 2 * LANE), F32), jnp.zeros((T, VH), F32))
        done = lax.fori_loop(kb + 1, nq, step, step(kb, (zero,) * HB, _diag_mask(transposed=True)))
        for b, (dk, dv) in enumerate(done):
            dk_ref[b] = dk
            dv_ref[b] = dv

    row = pl.BlockSpec((HB, 8, S), lambda h, j: (h, 0, 0))
    return pl.pallas_call(
        body, name="attn_dkv", grid=(H // HB, S // T),
        in_specs=[pl.BlockSpec((HB, S, 2 * LANE), lambda h, j: (h, 0, 0)), pl.BlockSpec((HB, T, 2 * LANE), lambda h, j: (h, j, 0)),
                  pl.BlockSpec((HB, T, LANE), lambda h, j: (h, j, 0)), pl.BlockSpec((S, HB * LANE), lambda h, j: (0, h)), row, row],
        out_specs=[pl.BlockSpec((HB, T, 2 * LANE), lambda h, j: (h, j, 0)), pl.BlockSpec((HB, T, LANE), lambda h, j: (h, j, 0))],
        out_shape=[jax.ShapeDtypeStruct((H, S, 2 * LANE), F32), jax.ShapeDtypeStruct((H, S, LANE), F32)],
        compiler_params=_params(("parallel", "parallel")),
    )(Q, K, V, do, lse_t, delta_t)


def _expand_matrix(cfg):
    r = lax.broadcasted_iota(I32, (LANE, cfg.INNER), 0)
    c = lax.broadcasted_iota(I32, (LANE, cfg.INNER), 1)
    return (r == c // HP).astype(F32)


def _softplus(x):
    return jnp.maximum(x, 0.0) + jnp.log(1.0 + jnp.exp(-jnp.abs(x)))


def _ssd_prep(cfg, dt_raw, dt_bias_pad, a_log_pad, expand):
    HS = cfg.HS

    def fn(raw, bias, alog, E):
        heads = lax.broadcasted_iota(I32, raw.shape, 1) < HS
        dt = jnp.where(heads, _softplus(raw + bias), 0.0)
        a = dt * jnp.where(heads[0:1], -jnp.exp(alog), 0.0)
        return dt, a, _dot(dt, E, precision=HI), _dot(a, E, precision=HI)

    return _rowwise("ssd_prep", fn, [dt_raw], [dt_bias_pad, a_log_pad, expand],
                    [(LANE, F32), (LANE, F32), (cfg.INNER, F32), (cfg.INNER, F32)], [], _pick(cfg.S, 512, 8))


def _tril(T):
    return lax.broadcasted_iota(I32, (T, T), 0) >= lax.broadcasted_iota(I32, (T, T), 1)


def _ssd_fwd(cfg, xc, dt_exp, a_exp, a_small, dskip_exp):
    S, T, INNER, G, NPAIR = cfg.S, cfg.T, cfg.INNER, cfg.G, cfg.NPAIR
    NC = S // T

    def body(xc_ref, dte_ref, ae_ref, as_ref, dsk_ref, y_ref, hin_ref, ht_ref):
        @pl.when(pl.program_id(0) == 0)
        def _():
            ht_ref[...] = jnp.zeros_like(ht_ref)

        tril = _tril(T)
        tri = tril.astype(F32)
        acs_s = _dot(tri, as_ref[...], precision=HI)
        acs_e = _dot(tri, ae_ref[...], precision=HI)
        acs_t = acs_s.T
        lo = lax.broadcasted_iota(I32, (T, LANE), 1) < HP
        for g in range(G):
            Bb = xc_ref[:, INNER + g * NST:INNER + (g + 1) * NST].astype(BF16)
            Cb = xc_ref[:, INNER + (G + g) * NST:INNER + (G + g + 1) * NST].astype(BF16)
            Gm = _dot(Cb, Bb, NT)
            for j in range(g * NPAIR // G, (g + 1) * NPAIR // G):
                sl = slice(j * LANE, (j + 1) * LANE)
                Xp = xc_ref[:, sl]
                Xdt = Xp * dte_ref[:, sl]
                Xb = Xdt.astype(BF16)
                acs_p = acs_e[:, sl]
                last = acs_p[T - 1:T, :]
                Hin = ht_ref[j]
                hin_ref[0, j] = Hin
                yd = []
                for e in (0, 1):
                    h = 2 * j + e
                    Lm = jnp.exp(jnp.where(tril, acs_s[:, h:h + 1] - acs_t[h:h + 1, :], -1e30))
                    yd.append(_dot((Gm * Lm).astype(BF16), Xb))
                y_off = _dot(Cb, Hin.astype(BF16)) * jnp.exp(acs_p)
                y_ref[:, sl] = jnp.where(lo, yd[0], yd[1]) + y_off + Xp * dsk_ref[:, sl]
                st = _dot(Bb, (Xdt * jnp.exp(last - acs_p)).astype(BF16), TN)
                ht_ref[j] = jnp.exp(last) * Hin + st

    rows = lambda w: pl.BlockSpec((T, w), lambda c: (c, 0))
    return pl.pallas_call(
        body, name="ssd_fwd", grid=(NC,),
        in_specs=[rows(cfg.CONVCH), rows(INNER), rows(INNER), rows(LANE), pl.BlockSpec((1, INNER), lambda c: (0, 0))],
        out_specs=[rows(INNER), pl.BlockSpec((1, NPAIR, NST, LANE), lambda c: (c, 0, 0, 0))],
        out_shape=[jax.ShapeDtypeStruct((S, INNER), F32), jax.ShapeDtypeStruct((NC, NPAIR, NST, LANE), F32)],
        scratch_shapes=[pltpu.VMEM((NPAIR, NST, LANE), F32)],
        compiler_params=_params(("arbitrary",)),
    )(xc, dt_exp, a_exp, a_small, dskip_exp)


def _ssd_bwd(cfg, dy, xc, dt_exp, a_exp, a_small, dskip_exp, hin, dt_raw, dt_bias_pad, a_log_pad, expand):
    S, T, INNER, G, NPAIR, HS = cfg.S, cfg.T, cfg.INNER, cfg.G, cfg.NPAIR, cfg.HS
    NC = S // T

    def body(dy_ref, xc_ref, dte_ref, ae_ref, as_ref, dsk_ref, hin_ref, raw_ref, bias_ref, alog_ref, e_ref,
             dxc_ref, draw_ref, dbias_ref, dalog_ref, dskip_ref, dht_ref, cols_ref, rows_ref, dacs_ref, ddt_ref):
        first = pl.program_id(0) == 0

        @pl.when(first)
        def _():
            dht_ref[...] = jnp.zeros_like(dht_ref)

        tril = _tril(T)
        tri = tril.astype(F32)
        a_s = as_ref[...]
        acs_s = _dot(tri, a_s, precision=HI)
        acs_e = _dot(tri, ae_ref[...], precision=HI)
        acs_t = acs_s.T
        lo = lax.broadcasted_iota(I32, (T, LANE), 1) < HP
        last_row = lax.broadcasted_iota(I32, (T, LANE), 0) == T - 1
        cols_ref[...] = jnp.zeros_like(cols_ref)
        rows_ref[...] = jnp.zeros_like(rows_ref)
        dsk_parts = []
        for g in range(G):
            bsl = slice(INNER + g * NST, INNER + (g + 1) * NST)
            csl = slice(INNER + (G + g) * NST, INNER + (G + g + 1) * NST)
            Bb = xc_ref[:, bsl].astype(BF16)
            Cb = xc_ref[:, csl].astype(BF16)
            Gm = _dot(Cb, Bb, NT)
            dG = jnp.zeros((T, T), F32)
            dB = jnp.zeros((T, NST), F32)
            dC = jnp.zeros((T, NST), F32)
            for j in range(g * NPAIR // G, (g + 1) * NPAIR // G):
                sl = slice(j * LANE, (j + 1) * LANE)
                Xp = xc_ref[:, sl]
                dtp = dte_ref[:, sl]
                Xdt = Xp * dtp
                Xb = Xdt.astype(BF16)
                acs_p = acs_e[:, sl]
                last = acs_p[T - 1:T, :]
                e_p, dec, cd = jnp.exp(acs_p), jnp.exp(last - acs_p), jnp.exp(last)
                Hin = hin_ref[0, j]
                Hb = Hin.astype(BF16)
                dHn = dht_ref[j]
                dHb = dHn.astype(BF16)
                dYp = dy_ref[:, sl]
                z = _dot(Cb, Hb)
                dz = (dYp * e_p).astype(BF16)
                dacs_p = dYp * z * e_p
                dC = dC + _dot(dz, Hb, NT)
                dHin = _dot(Cb, dz, TN) + cd * dHn
                dlast = _colsum(dHn * Hin) * cd
                qv = _dot(Bb, dHb)
                dXdt = qv * dec
                ddec = qv * Xdt * dec
                dacs_p = dacs_p - ddec
                dlast = dlast + _colsum(ddec)
                dB = dB + _dot((Xdt * dec).astype(BF16), dHb, NT)
                for e in (0, 1):
                    h = 2 * j + e
                    Lm = jnp.exp(jnp.where(tril, acs_s[:, h:h + 1] - acs_t[h:h + 1, :], -1e30))
                    Mh = Gm * Lm
                    dYe = jnp.where(lo if e == 0 else jnp.logical_not(lo), dYp, 0.0).astype(BF16)
                    dM = _dot(dYe, Xb, NT)
                    dXdt = dXdt + _dot(Mh.astype(BF16), dYe, TN)
                    W = dM * Mh
                    cols_ref[:, h:h + 1] = jnp.sum(W, axis=1, keepdims=True)
                    rows_ref[h:h + 1, :] = _colsum(W)
                    dG = dG + dM * Lm
                dacs_ref[:, sl] = dacs_p + jnp.where(last_row, dlast, 0.0)
                ddt_ref[:, sl] = dXdt * Xp
                dxc_ref[:, sl] = dXdt * dtp + dYp * dsk_ref[:, sl]
                dsk_parts.append(_colsum(dYp * Xp))
                dht_ref[j] = dHin
            dGb = dG.astype(BF16)
            dxc_ref[:, bsl] = dB + _dot(dGb, Cb, TN)
            dxc_ref[:, csl] = dC + _dot(dGb, Bb)
        E = e_ref[...]
        dacs_s = cols_ref[...] - rows_ref[...].T + _dot(dacs_ref[...], E, NT, precision=HI)
        da = _dot(tri, dacs_s, TN, precision=HI)
        heads = lax.broadcasted_iota(I32, (1, LANE), 1) < HS
        A = jnp.where(heads, -jnp.exp(alog_ref[...]), 0.0)
        ddt = _dot(ddt_ref[...], E, NT, precision=HI) + da * A
        draw = jnp.where(heads, ddt * _sigmoid(raw_ref[...] + bias_ref[...]), 0.0)
        draw_ref[...] = draw
        dsk = _dot(jnp.broadcast_to(jnp.concatenate(dsk_parts, axis=1), (8, INNER)), E, NT, precision=HI)[0:1]
        for ref, val in ((dbias_ref, _colsum(draw)), (dalog_ref, _colsum(da * a_s)), (dskip_ref, dsk)):
            @pl.when(first)
            def _():
                ref[...] = val

            @pl.when(jnp.logical_not(first))
            def _():
                ref[...] += val

    dt_raw, _, raw_block = _window(dt_raw)
    rows = lambda w, b=0: pl.BlockSpec((T, w), lambda c: (NC - 1 - c, b))
    vec = lambda w: pl.BlockSpec((1, w), lambda c: (0, 0))
    return pl.pallas_call(
        body, name="ssd_bwd", grid=(NC,),
        in_specs=[rows(INNER), rows(cfg.CONVCH), rows(INNER), rows(INNER), rows(LANE), vec(INNER),
                  pl.BlockSpec((1, NPAIR, NST, LANE), lambda c: (NC - 1 - c, 0, 0, 0)), rows(LANE, raw_block), vec(LANE), vec(LANE),
                  pl.BlockSpec((LANE, INNER), lambda c: (0, 0))],
        out_specs=[rows(cfg.CONVCH), rows(LANE), vec(LANE), vec(LANE), vec(LANE)],
        out_shape=[jax.ShapeDtypeStruct((S, cfg.CONVCH), F32), jax.ShapeDtypeStruct((S, LANE), F32)]
        + [jax.ShapeDtypeStruct((1, LANE), F32)] * 3,
        scratch_shapes=[pltpu.VMEM((NPAIR, NST, LANE), F32), pltpu.VMEM((T, LANE), F32), pltpu.VMEM((LANE, T), F32),
                        pltpu.VMEM((T, INNER), F32), pltpu.VMEM((T, INNER), F32)],
        compiler_params=_params(("arbitrary",)),
    )(dy, xc, dt_exp, a_exp, a_small, dskip_exp, hin, dt_raw, dt_bias_pad, a_log_pad, expand)


def _ssd_post(cfg, y, z, norm_g):
    W = cfg.INNER // cfg.G

    def fn(y, z, g):
        yz = y * z * _sigmoid(z)
        return jnp.concatenate([yz[:, i * W:(i + 1) * W] * _rs(yz[:, i * W:(i + 1) * W]) for i in range(cfg.G)], axis=1) * g

    return _rowwise("ssd_post", fn, [y, z], [norm_g], [(cfg.INNER, BF16)], [], _pick(cfg.S, 256, 8))[0]


def _ssd_post_bwd(cfg, db, y, z, norm_g):
    W = cfg.INNER // cfg.G

    def fn(db, y, z, g):
        sg = _sigmoid(z)
        yz = y * z * sg
        dn = db * g
        dyz, nh = [], []
        for i in range(cfg.G):
            seg = yz[:, i * W:(i + 1) * W]
            r = _rs(seg)
            nh.append(seg * r)
            dyz.append(_rms_back(nh[-1], r, dn[:, i * W:(i + 1) * W]))
        dyz = jnp.concatenate(dyz, axis=1)
        return dyz * z * sg, dyz * y * sg * (1.0 + z * (1.0 - sg)), _colsum(db * jnp.concatenate(nh, axis=1))

    return _rowwise("ssd_post_bwd", fn, [db, y, z], [norm_g], [(cfg.INNER, F32), (cfg.INNER, F32)], [(1, cfg.INNER)],
                    _pick(cfg.S, 256, 8))


def _rms_pre(cfg, x, g):
    return _rowwise("rms_pre", lambda x, g: x * _rs(x) * g, [x], [g], [(cfg.D, BF16)], [], _pick(cfg.S, 256, 8))[0]


def _local_grads(cfg, x, tgt, W, sp, mla_weights=None, out_weight=None, ffn_weights=None, down_weight=None,
                 ffn_grads_ready=None, early_grads_ready=None, in_grad_ready=None, xn=None, after_in=None):
    S, D, H, INNER = cfg.S, cfg.D, cfg.H, cfg.INNER
    ts = _pick(S, 256, 8)
    tc = _CONV_COLS

    if xn is None:
        xn = _rms_pre(cfg, x, sp["mix_pre_g"])
    u = _matmul("mm_in", xn, W["w_in"], "nt", F32, after=after_in)
    c_q, c_kv, kr, z, xbc, dt_raw = [(u, cfg.window(n)) for n in ("c_q", "c_kv", "kr", "z", "xbc", "dt")]

    if mla_weights is not None:
        sp = dict(sp, q_norm_g=sp["q_norm_g"] + mla_weights.pass_on(u)[0, 0])
    cqn = _rowwise("rms_q", lambda x, g: x * _rs(x) * g, [c_q], [sp["q_norm_g"]], [(cfg.QL, BF16)], [], ts)[0]
    ckvn = _rowwise("rms_kv", lambda x, g: x * _rs(x) * g, [c_kv], [sp["kv_norm_g"]], [(cfg.KVL, BF16)], [], ts)[0]
    if mla_weights is not None:
        W = dict(W, **mla_weights.arrived(ckvn))
    q = _matmul("mm_uq", cqn, W["w_uq"], "nn", F32)
    kv = _matmul("mm_ukv", ckvn, W["w_ukv"], "nn", F32)
    cos2, sin2 = _rope_tables(S)
    Qh, Kh, Vh = _mla_pack(cfg, q, kv, kr, cos2, sin2)
    a_out, lse, lse_t = _attn_fwd(cfg, Qh, Kh, Vh)
    if out_weight is not None:
        sp = dict(sp, ssm_conv_b=sp["ssm_conv_b"] + out_weight.pass_on(a_out)[0, 0])

    pad = lambda v: jnp.pad(v, ((0, 0), (0, LANE - v.shape[1])))
    expand = _expand_matrix(cfg)
    dt_bias_pad, a_log_pad = pad(sp["dt_bias"]), pad(sp["a_log"])
    dskip_exp = jnp.repeat(sp["d_skip"], HP, axis=1)
    xc = _colwise("ssm_act", _ssm_act, [xbc], [sp["ssm_conv_w"], sp["ssm_conv_b"]], [F32], [], tc)[0]
    dt_s, a_s, dt_exp, a_exp = _ssd_prep(cfg, dt_raw, dt_bias_pad, a_log_pad, expand)
    y_ssd, hin = _ssd_fwd(cfg, xc, dt_exp, a_exp, a_s, dskip_exp)
    b_out = _ssd_post(cfg, y_ssd, z, sp["ssm_norm_g"])

    ab_out = jnp.concatenate([a_out.astype(BF16), b_out], axis=1)
    if out_weight is not None:
        W = dict(W, **out_weight.arrived(ab_out))
    if ffn_weights is not None:
        sp = dict(sp, mix_post_g=sp["mix_post_g"] + ffn_weights.pass_on(ab_out)[0, 0])
    mix = _matmul("mm_out", ab_out, W["w_out"], "nn", F32)

    def mid(x, mix, g_mp, g_fp):
        x1 = x + mix * _rs(mix) * g_mp
        return x1, x1 * _rs(x1) * g_fp

    x1, h2 = _rowwise("fwd_mid", mid, [x, mix], [sp["mix_post_g"], sp["ffn_pre_g"]], [(D, F32), (D, BF16)], [], ts)
    if ffn_weights is not None:
        W = dict(W, **ffn_weights.arrived(h2))
    gate_pre = _matmul("mm_gate", h2, W["w_gate"], "nn", F32, chips=True)
    if down_weight is not None:
        sp = dict(sp, ffn_conv_b=sp["ffn_conv_b"] + down_weight.pass_on(gate_pre)[0, 0])
    up = _matmul("mm_up", h2, W["w_up"], "nn", F32, chips=True)
    act = _colwise("ffn_act", _ffn_act, [gate_pre, up], [sp["ffn_conv_w"], sp["ffn_conv_b"]], [BF16], [], tc)[0]
    if down_weight is not None:
        W = dict(W, **down_weight.arrived(act))
    f = _matmul("mm_down", act, W["w_down"], "nn", F32)

    def final(x1, f, t, g):
        r = _rs(f)
        fh = f * r
        err = x1 + fh * g - t
        loss = 0.5 * jnp.sum(jnp.mean(err * err, axis=-1, keepdims=True), axis=0, keepdims=True)
        dy = err * (1.0 / D)
        return dy, _rms_back(fh, r, dy * g), _colsum(dy * fh), loss

    dy, df, g_ffn_post, loss = _rowwise("final", final, [x1, f, tgt], [sp["ffn_post_g"]], [(D, F32), (D, BF16)],
                                        [(1, D), (1, LANE)], ts)
    gW = {}
    dact = _matmul("mm_down_dx", df, W["w_down"], "nt", F32)
    gW["w_down"] = _matmul("mm_down_dw", act, df, "tn", BF16)
    dgate, dup, g_ffn_conv_w, g_ffn_conv_b = _colwise(
        "ffn_act_bwd", _ffn_act_back, [dact, gate_pre, up], [sp["ffn_conv_w"], sp["ffn_conv_b"]], [BF16, BF16], [FFN_K, 1], tc)
    gW["w_gate"] = _matmul("mm_gate_dw", h2, dgate, "tn", BF16, chips=True)
    gW["w_up"] = _matmul("mm_up_dw", h2, dup, "tn", BF16, chips=True)
    if ffn_grads_ready is not None:
        sp = dict(sp, ffn_pre_g=sp["ffn_pre_g"] + ffn_grads_ready({n: gW[n] for n in ("w_down", "w_gate", "w_up")})[0, 0])
    dh2 = _matmul("mm_gu_dx", dgate, W["w_gate"], "nt", F32, dup, W["w_up"], chips=True)

    def mid_back(dy, dh2, x1, mix, g_mp, g_fp):
        r2 = _rs(x1)
        xh = x1 * r2
        dx1 = dy + _rms_back(xh, r2, dh2 * g_fp)
        r1 = _rs(mix)
        mh = mix * r1
        return dx1, _rms_back(mh, r1, dx1 * g_mp), _colsum(dh2 * xh), _colsum(dx1 * mh)

    dx1, dmix, g_ffn_pre, g_mix_post = _rowwise("bwd_mid", mid_back, [dy, dh2, x1, mix], [sp["mix_post_g"], sp["ffn_pre_g"]],
                                                [(D, F32), (D, BF16)], [(1, D), (1, D)], ts)
    dab_out = _matmul("mm_out_dx", dmix, W["w_out"], "nt", F32)
    db_out = (dab_out, (INNER, cfg.MLAW // INNER))
    gW["w_out"] = _matmul("mm_out_dw", ab_out, dmix, "tn", BF16)
    early_token = jnp.zeros((8, LANE), F32)
    if early_grads_ready is not None:
        early_token = early_grads_ready({n: gW[n] for n in ("w_down", "w_gate", "w_up", "w_out")})
        sp = dict(sp, ssm_norm_g=sp["ssm_norm_g"] + early_token[0, 0])

    dy_ssd, dz, g_ssm_norm = _ssd_post_bwd(cfg, db_out, y_ssd, z, sp["ssm_norm_g"])
    dxc, ddt_raw, g_dt_bias, g_a_log, g_d_skip = _ssd_bwd(cfg, dy_ssd, xc, dt_exp, a_exp, a_s, dskip_exp, hin, dt_raw,
                                                          dt_bias_pad, a_log_pad, expand)
    dxbc, g_ssm_conv_w, g_ssm_conv_b = _colwise("ssm_act_bwd", _ssm_act_back, [dxc, xbc], [sp["ssm_conv_w"], sp["ssm_conv_b"]],
                                                [BF16], [SSM_K, 1], tc)

    dQ, delta_t = _attn_dq(cfg, Qh, Kh, Vh, dab_out, a_out, lse, early_token)
    dK, dV = _attn_dkv(cfg, Qh, Kh, Vh, dab_out, lse_t, delta_t)
    dq, dkv, dkr = _mla_unpack(cfg, dQ, dK, dV, cos2, sin2)
    dcqn = _matmul("mm_uq_dx", dq, W["w_uq"], "nt", F32)
    dckvn = _matmul("mm_ukv_dx", dkv, W["w_ukv"], "nt", F32)
    gW["w_uq"] = _matmul("mm_uq_dw", cqn, dq, "tn", BF16)
    gW["w_ukv"] = _matmul("mm_ukv_dw", ckvn, dkv, "tn", BF16)

    def rms_back(x, dy, g):
        r = _rs(x)
        xh = x * r
        return _rms_back(xh, r, dy * g), _colsum(dy * xh)

    dc_q, g_q_norm = _rowwise("rms_q_bwd", rms_back, [c_q, dcqn], [sp["q_norm_g"]], [(cfg.QL, BF16)], [(1, cfg.QL)], ts)
    dc_kv, g_kv_norm = _rowwise("rms_kv_bwd", rms_back, [c_kv, dckvn], [sp["kv_norm_g"]], [(cfg.KVL, BF16)], [(1, cfg.KVL)], ts)

    du = dict(c_q=dc_q, c_kv=dc_kv, kr=dkr, z=dz.astype(BF16), xbc=dxbc, dt=ddt_raw.astype(BF16))
    du = jnp.concatenate([du[n] for n in sorted(du, key=lambda n: cfg.seg[n][0])], axis=1)
    assert du.shape[1] == cfg.EXT, "the layout of u has gaps"
    gW["w_in"] = _matmul("mm_in_dw", du, xn, "tn", BF16)
    if in_grad_ready is not None:
        token = in_grad_ready({n: gW[n] for n in ("w_in", "w_uq", "w_ukv")})
        sp = dict(sp, mix_pre_g=sp["mix_pre_g"] + token[0, 0])
    dxn = _matmul("mm_in_dx", du, W["w_in"], "nn", F32)

    def first_back(dx1, dxn, x, g):
        r = _rs(x)
        xh = x * r
        return dx1 + _rms_back(xh, r, dxn * g), _colsum(dxn * xh)

    grad_x, g_mix_pre = _rowwise("bwd_first", first_back, [dx1, dxn, x], [sp["mix_pre_g"]], [(D, F32)], [(1, D)], ts)

    gs = dict(mix_pre_g=g_mix_pre, q_norm_g=g_q_norm, kv_norm_g=g_kv_norm, ssm_conv_w=g_ssm_conv_w, ssm_conv_b=g_ssm_conv_b,
              dt_bias=g_dt_bias[:, :cfg.HS], a_log=g_a_log[:, :cfg.HS], d_skip=g_d_skip[:, :cfg.HS], ssm_norm_g=g_ssm_norm,
              mix_post_g=g_mix_post, ffn_pre_g=g_ffn_pre, ffn_conv_w=g_ffn_conv_w, ffn_conv_b=g_ffn_conv_b,
              ffn_post_g=g_ffn_post)
    return loss, grad_x, gW, gs


def _to_kernel_layout(cfg, name, w):
    if name == "w_in":
        parts, at = [], 0
        for off, width, n_off, n_width in sorted(cfg.seg.values()):
            parts += [jnp.zeros((off - at, w.shape[1]), w.dtype), w[n_off:n_off + n_width],
                      jnp.zeros((width - n_width, w.shape[1]), w.dtype)]
            at = off + width
        parts.append(jnp.zeros((cfg.EXT - at, w.shape[1]), w.dtype))
        return jnp.concatenate([p for p in parts if p.shape[0]], axis=0)
    if name in ("w_uq", "w_ukv"):
        per = NOPE + (ROPE if name == "w_uq" else VH)
        return jnp.concatenate([w[:, h * per:h * per + NOPE] for h in range(cfg.H)]
                               + [w[:, h * per + NOPE:(h + 1) * per] for h in range(cfg.H)], axis=1)
    return w


def _from_kernel_layout(cfg, name, g):
    if name == "w_in":
        return jnp.concatenate([g[off:off + n_width] for off, _, _, n_width in sorted(cfg.seg.values(), key=lambda s: s[2])], axis=0)
    if name in ("w_uq", "w_ukv"):
        second = ROPE if name == "w_uq" else VH
        base = cfg.H * NOPE
        parts = []
        for h in range(cfg.H):
            parts += [g[:, h * NOPE:(h + 1) * NOPE], g[:, base + h * second:base + (h + 1) * second]]
        return jnp.concatenate(parts, axis=1)
    return g


_CHIP_MAJOR = ("w_gate", "w_up")
_RELAYOUT = ("w_uq", "w_ukv")
_LAYOUT_ROWS = 256
_CONV_COLS = 256


def _w_in_layout(cfg, wg):
    _, rs, d = wg.shape
    tc = _pick(d, _LAYOUT_ROWS, LANE)

    def body(w_ref, o_ref):
        o_ref[...] = _to_kernel_layout(cfg, "w_in", jnp.concatenate([w_ref[k] for k in range(N_CHIPS)], axis=0))

    return pl.pallas_call(
        body, name="layout_w_in", grid=(d // tc,),
        in_specs=[pl.BlockSpec((N_CHIPS, rs, tc), lambda j: (0, 0, j))], out_specs=pl.BlockSpec((cfg.EXT, tc), lambda j: (0, j)),
        out_shape=jax.ShapeDtypeStruct((cfg.EXT, d), wg.dtype), compiler_params=_params(("parallel",)),
    )(wg)


def _w_in_grad_to_chips(cfg, g):
    _, d = g.shape
    rs = cfg.IN_COLS // N_CHIPS
    tc = _pick(d, _LAYOUT_ROWS, LANE)

    def body(g_ref, o_ref):
        nat = _from_kernel_layout(cfg, "w_in", g_ref[...])
        for k in range(N_CHIPS):
            o_ref[k] = nat[k * rs:(k + 1) * rs]

    return pl.pallas_call(
        body, name="layout_grad_w_in", grid=(d // tc,),
        in_specs=[pl.BlockSpec((cfg.EXT, tc), lambda j: (0, j))], out_specs=pl.BlockSpec((N_CHIPS, rs, tc), lambda j: (0, 0, j)),
        out_shape=jax.ShapeDtypeStruct((N_CHIPS, rs, d), g.dtype), compiler_params=_params(("parallel",)),
    )(g)


def _gathered_to_kernel(cfg, name, wg):
    if name in _CHIP_MAJOR:
        return wg
    if name == "w_in":
        return _w_in_layout(cfg, wg)
    if name not in _RELAYOUT:
        return wg.reshape(wg.shape[0] * wg.shape[1], wg.shape[2])
    _, rows, cs = wg.shape
    tr = _pick(rows, _LAYOUT_ROWS, 16)

    def body(w_ref, o_ref):
        o_ref[...] = _to_kernel_layout(cfg, name, jnp.concatenate([w_ref[k] for k in range(N_CHIPS)], axis=1))

    wide = jax.eval_shape(lambda w: _to_kernel_layout(cfg, name, w), jax.ShapeDtypeStruct((rows, N_CHIPS * cs), wg.dtype)).shape[1]
    return pl.pallas_call(
        body, name="layout_" + name, grid=(rows // tr,),
        in_specs=[pl.BlockSpec((N_CHIPS, tr, cs), lambda i: (0, i, 0))], out_specs=pl.BlockSpec((tr, wide), lambda i: (i, 0)),
        out_shape=jax.ShapeDtypeStruct((rows, wide), wg.dtype), compiler_params=_params(("parallel",)),
    )(wg)


def _grad_to_chips(cfg, name, g):
    if name in _CHIP_MAJOR:
        return g
    if name == "w_in":
        return _w_in_grad_to_chips(cfg, g)
    if name not in _RELAYOUT:
        return g.reshape(N_CHIPS, g.shape[0] // N_CHIPS, g.shape[1])
    rows, wide = g.shape
    tr = _pick(rows, _LAYOUT_ROWS, 16)
    cs = jax.eval_shape(lambda v: _from_kernel_layout(cfg, name, v), g).shape[1] // N_CHIPS

    def body(g_ref, o_ref):
        nat = _from_kernel_layout(cfg, name, g_ref[...])
        for k in range(N_CHIPS):
            o_ref[k] = nat[:, k * cs:(k + 1) * cs]

    return pl.pallas_call(
        body, name="layout_grad_" + name, grid=(rows // tr,),
        in_specs=[pl.BlockSpec((tr, wide), lambda i: (i, 0))], out_specs=pl.BlockSpec((N_CHIPS, tr, cs), lambda i: (0, i, 0)),
        out_shape=jax.ShapeDtypeStruct((N_CHIPS, rows, cs), g.dtype), compiler_params=_params(("parallel",)),
    )(g)


def _me():
    return lax.axis_index("x"), lax.axis_index("y"), lax.axis_index("c")


def _other_chips(x, y):
    return [(1 - x, y), (x, 1 - y), (1 - x, 1 - y)]


_ANY = pl.BlockSpec(memory_space=pl.ANY)


BLOCK_ELEMS = 1 << 19
BLOCK_ELEMS_FEW = 1 << 20


def _row_block(rows, cols, mult, elems=BLOCK_ELEMS):
    return _pick(rows, max(mult, elems // cols // mult * mult), mult)


def _scalar(v):
    return v.astype(I32).reshape(1)


def _blocks2d(r, c, mult, elems=BLOCK_ELEMS):
    if r % mult == 0:
        tr = _row_block(r, c, mult, elems)
        return (tr, c), r // tr, lambda i: (i, 0)
    tc = _pick(c, max(LANE, elems // r // LANE * LANE), LANE)
    return (r, tc), c // tc, lambda i: (0, i)


def _by_rows(rows):
    return rows % 32 == 0


def _half_shape(rows, cols):
    return (rows // 2, cols) if _by_rows(rows) else (rows, cols // 2)


def _half_blocks(rows, cols, mult, elems=BLOCK_ELEMS):
    hr, hc = _half_shape(rows, cols)
    block, n, part = _blocks2d(hr, hc, mult, elems)
    assert (hr % mult == 0) == _by_rows(rows), (rows, cols, mult)
    full = (lambda h, i: (h * n + i, 0)) if _by_rows(rows) else (lambda h, i: (0, h * n + i))
    return block, n, full, part


def _half(ref, k, half):
    hr, hc = _half_shape(ref.shape[1], ref.shape[2])
    if _by_rows(ref.shape[1]):
        return ref.at[k, pl.ds(pl.multiple_of(half * hr, 16), hr), :]
    return ref.at[k, :, pl.ds(pl.multiple_of(half * hc, LANE), hc)]


def _shard_blocks(w, br, bc):
    if w.shape[0] == 1:
        def write(ref, v):
            ref[...] = v
        return (lambda f: pl.BlockSpec((None, br, bc), lambda *a: (0, *f(*a)))), (lambda ref: ref[...]), write
    assert w.shape[1] == 1 and br == w.shape[0], w.shape

    def write_rows(ref, v):
        ref[:, 0, :] = v
    return (lambda f: pl.BlockSpec((br, 1, bc), lambda *a: (0, 0, f(*a)[1]))), (lambda ref: ref[:, 0, :]), write_rows


def _stage_shard(name, w, chip, after=None):
    rs, cs = w.shape[0] * w.shape[1], w.shape[2]
    (br, bc), n, idx = _blocks2d(rs, cs, 16, BLOCK_ELEMS_FEW)
    spec, get, _ = _shard_blocks(w, br, bc)

    def body(chip_ref, w_ref, *refs):
        refs[-1][...] = get(w_ref).astype(BF16)

    return pl.pallas_call(
        body, name="stage_" + name,
        grid_spec=pltpu.PrefetchScalarGridSpec(
            num_scalar_prefetch=1, grid=(n,),
            in_specs=[spec(lambda i, chip_ref: idx(i))] + ([] if after is None else [_ANY]),
            out_specs=pl.BlockSpec((None, br, bc), lambda i, chip_ref: (chip_ref[0], *idx(i)))),
        out_shape=jax.ShapeDtypeStruct((N_CHIPS, rs, cs), BF16),
        compiler_params=_params(("parallel",)),
    )(_scalar(chip), w, *([] if after is None else [after]))


_HBM = pl.BlockSpec(memory_space=pltpu.HBM)
_SEM = pl.BlockSpec(memory_space=pltpu.SEMAPHORE)
_EFFECT = pltpu.SideEffectType.DATAFLOW_SIDE_EFFECTING


def _split_start(name, bufs, n_copies, copies, after):
    n = len(bufs)

    def body(*refs):
        for cp in copies(refs[:n], refs[n + 1], refs[n + 2]):
            cp.start()
        refs[-1][...] = jnp.zeros_like(refs[-1])

    res = pl.pallas_call(
        body, name=name,
        out_shape=(pltpu.SemaphoreType.DMA((n_copies,)), pltpu.SemaphoreType.DMA((n_copies,)),
                   *[pltpu.HBM(b.shape, b.dtype) for b in bufs], jax.ShapeDtypeStruct((8, LANE), F32)),
        in_specs=[_HBM] * n + [_ANY], out_specs=(_SEM, _SEM, *[_HBM] * n, pl.BlockSpec(memory_space=pltpu.VMEM)),
        input_output_aliases={i: 2 + i for i in range(n)},
        compiler_params=pltpu.CompilerParams(has_side_effects=_EFFECT),
    )(*[pltpu.with_memory_space_constraint(b, pltpu.HBM) for b in bufs], after)
    return res[0], res[1], list(res[2:2 + n]), res[-1]


def _split_wait(name, send_sems, recv_sems, bufs, after, copies):
    n = len(bufs)

    def body(*refs):
        for cp in copies(refs[:n], refs[n], refs[n + 1]):
            cp.wait_send()
            cp.wait_recv()

    return list(pl.pallas_call(
        body, name=name, out_shape=[pltpu.HBM(b.shape, b.dtype) for b in bufs],
        in_specs=[_HBM] * n + [_SEM, _SEM, _ANY], out_specs=[_HBM] * n,
        input_output_aliases={i: i for i in range(n)},
        compiler_params=pltpu.CompilerParams(has_side_effects=_EFFECT),
    )(*bufs, send_sems, recv_sems, after))


def _gather_to_chips(bufs, send_sems, recv_sems):
    x, y, c = _me()
    return [pltpu.make_async_remote_copy(src_ref=_half(b, 2 * x + y, c), dst_ref=_half(b, 2 * x + y, c),
                                         send_sem=send_sems.at[3 * w + j], recv_sem=recv_sems.at[3 * w + j],
                                         device_id=(cx, cy, c), device_id_type=MESH_ID)
            for w, b in enumerate(bufs) for j, (cx, cy) in enumerate(_other_chips(x, y))]


def _gather_to_sibling(bufs, send_sems, recv_sems):
    x, y, c = _me()
    return [pltpu.make_async_remote_copy(src_ref=_half(b, 2 * cx + cy, c), dst_ref=_half(b, 2 * cx + cy, c),
                                         send_sem=send_sems.at[3 * w + j], recv_sem=recv_sems.at[3 * w + j],
                                         device_id=(x, y, 1 - c), device_id_type=MESH_ID)
            for w, b in enumerate(bufs) for j, (cx, cy) in enumerate(_other_chips(x, y))]


def _pair_exchange(name, grads):
    n = len(grads)

    def body(*refs):
        ins, outs, send_sems, recv_sems = refs[:n], refs[n:2 * n], refs[2 * n], refs[2 * n + 1]
        x, y, c = _me()
        cps = []
        for w, (g_ref, o_ref) in enumerate(zip(ins, outs)):
            cps.append(pltpu.make_async_remote_copy(src_ref=_half(g_ref, slice(None), 1 - c), dst_ref=o_ref,
                                                    send_sem=send_sems.at[w], recv_sem=recv_sems.at[w],
                                                    device_id=(x, y, 1 - c), device_id_type=MESH_ID))
            cps[-1].start()
        for cp in cps:
            cp.wait()

    return pl.pallas_call(
        body, name="pair_exchange_" + name, in_specs=[_ANY] * n, out_specs=[_ANY] * n,
        out_shape=[jax.ShapeDtypeStruct((g.shape[0], *_half_shape(g.shape[1], g.shape[2])), g.dtype) for g in grads],
        scratch_shapes=[pltpu.SemaphoreType.DMA((n,)), pltpu.SemaphoreType.DMA((n,))],
    )(*grads)


def _pair_copies(grads, lands, send_sems, recv_sems):
    x, y, c = _me()
    return [pltpu.make_async_remote_copy(src_ref=_half(g_ref, slice(None), 1 - c), dst_ref=l_ref, send_sem=send_sems.at[w],
                                         recv_sem=recv_sems.at[w], device_id=(x, y, 1 - c), device_id_type=MESH_ID)
            for w, (g_ref, l_ref) in enumerate(zip(grads, lands))]


def _pair_exchange_start(name, grads):
    n = len(grads)
    lands = [lax.empty((g.shape[0], *_half_shape(g.shape[1], g.shape[2])), g.dtype) for g in grads]
    send_sems, recv_sems, bufs, token = _split_start(
        "pair_exchange_start_" + name, [*grads, *lands], n, lambda refs, ss, rs: _pair_copies(refs[:n], refs[n:], ss, rs),
        jnp.zeros((8, LANE), F32))
    return (send_sems, recv_sems, bufs), token


def _pair_exchange_wait(name, state, after):
    send_sems, recv_sems, bufs = state
    n = len(bufs) // 2
    bufs = _split_wait("pair_exchange_wait_" + name, send_sems, recv_sems, bufs, after,
                       lambda refs, ss, rs: _pair_copies(refs[:n], refs[n:], ss, rs))
    return bufs[:n], bufs[n:]


def _pair_sum(name, g, theirs, c):
    (br, bc), nb, full, part = _half_blocks(g.shape[1], g.shape[2], 16, 2 * BLOCK_ELEMS_FEW)

    def body(c_ref, a_ref, b_ref, o_ref):
        o_ref[...] = (a_ref[...].astype(F32) + b_ref[...].astype(F32)).astype(o_ref.dtype)

    return pl.pallas_call(
        body, name="pair_sum_" + name,
        grid_spec=pltpu.PrefetchScalarGridSpec(
            num_scalar_prefetch=1, grid=(N_CHIPS, nb),
            in_specs=[pl.BlockSpec((None, br, bc), lambda k, i, c_ref: (k, *full(c_ref[0], i))),
                      pl.BlockSpec((None, br, bc), lambda k, i, c_ref: (k, *part(i)))],
            out_specs=pl.BlockSpec((None, br, bc), lambda k, i, c_ref: (k, *part(i)))),
        out_shape=jax.ShapeDtypeStruct(theirs.shape, BF16),
        compiler_params=_params(("parallel", "parallel")),
    )(_scalar(c), g, theirs)


def _chip_copies(srcs, lands, send_sems, recv_sems):
    x, y, c = _me()
    return [pltpu.make_async_remote_copy(src_ref=s_ref.at[2 * cx + cy], dst_ref=l_ref.at[j], send_sem=send_sems.at[3 * w + j],
                                         recv_sem=recv_sems.at[3 * w + j], device_id=(cx, cy, c), device_id_type=MESH_ID)
            for w, (s_ref, l_ref) in enumerate(zip(srcs, lands)) for j, (cx, cy) in enumerate(_other_chips(x, y))]


def _chip_exchange_start(name, sums):
    n = len(sums)
    lands = [lax.empty((3,) + s.shape[1:], s.dtype) for s in sums]
    send_sems, recv_sems, bufs, token = _split_start(
        "chip_exchange_start_" + name, [*sums, *lands], 3 * n, lambda refs, ss, rs: _chip_copies(refs[:n], refs[n:], ss, rs),
        jnp.zeros((8, LANE), F32))
    return send_sems, recv_sems, bufs[:n], bufs[n:], token


def _chip_exchange_wait(name, send_sems, recv_sems, sums, lands, after):
    n = len(sums)
    bufs = _split_wait("chip_exchange_wait_" + name, send_sems, recv_sems, [*sums, *lands], after,
                       lambda refs, ss, rs: _chip_copies(refs[:n], refs[n:], ss, rs))
    return bufs[:n], bufs[n:]


def _chip_sum(name, sums, theirs, chip):
    _, h, cs = sums.shape
    (br, bc), nb, idx = _blocks2d(h, cs, 16, BLOCK_ELEMS_FEW)

    def body(chip_ref, s_ref, t_ref, o_ref):
        acc = s_ref[...].astype(F32)
        for k in range(3):
            acc = acc + t_ref[k].astype(F32)
        o_ref[...] = acc

    return pl.pallas_call(
        body, name="chip_sum_" + name,
        grid_spec=pltpu.PrefetchScalarGridSpec(
            num_scalar_prefetch=1, grid=(nb,),
            in_specs=[pl.BlockSpec((None, br, bc), lambda i, chip_ref: (chip_ref[0], *idx(i))),
                      pl.BlockSpec((3, br, bc), lambda i, chip_ref: (0, *idx(i)))],
            out_specs=pl.BlockSpec((br, bc), lambda i, chip_ref: idx(i))),
        out_shape=jax.ShapeDtypeStruct((h, cs), F32),
        compiler_params=_params(("parallel",)),
    )(_scalar(chip), sums, theirs)


def _sibling_copies(halves, lands, send_sems, recv_sems):
    x, y, c = _me()
    return [pltpu.make_async_remote_copy(src_ref=h_ref, dst_ref=l_ref, send_sem=send_sems.at[w], recv_sem=recv_sems.at[w],
                                         device_id=(x, y, 1 - c), device_id_type=MESH_ID)
            for w, (h_ref, l_ref) in enumerate(zip(halves, lands))]


def _sibling_exchange_start(name, halves, after):
    n = len(halves)
    lands = [lax.empty(h.shape, h.dtype) for h in halves]
    send_sems, recv_sems, bufs, token = _split_start(
        "sibling_exchange_start_" + name, [*halves, *lands], n, lambda refs, ss, rs: _sibling_copies(refs[:n], refs[n:], ss, rs),
        after)
    return (send_sems, recv_sems, bufs), token


def _sibling_exchange_wait(name, state, after):
    send_sems, recv_sems, bufs = state
    n = len(bufs) // 2
    bufs = _split_wait("sibling_exchange_wait_" + name, send_sems, recv_sems, bufs, after,
                       lambda refs, ss, rs: _sibling_copies(refs[:n], refs[n:], ss, rs))
    return bufs[:n], bufs[n:]


def _sibling_exchange(name, halves):
    n = len(halves)

    def body(*refs):
        ins, outs, send_sems, recv_sems = refs[:n], refs[n:2 * n], refs[2 * n], refs[2 * n + 1]
        x, y, c = _me()
        cps = []
        for w, (h_ref, o_ref) in enumerate(zip(ins, outs)):
            cps.append(pltpu.make_async_remote_copy(src_ref=h_ref, dst_ref=o_ref, send_sem=send_sems.at[w], recv_sem=recv_sems.at[w],
                                                    device_id=(x, y, 1 - c), device_id_type=MESH_ID))
            cps[-1].start()
        for cp in cps:
            cp.wait()

    return pl.pallas_call(
        body, name="sibling_exchange_" + name, in_specs=[_ANY] * n, out_specs=[_ANY] * n,
        out_shape=[jax.ShapeDtypeStruct(h.shape, h.dtype) for h in halves],
        scratch_shapes=[pltpu.SemaphoreType.DMA((n,)), pltpu.SemaphoreType.DMA((n,))],
    )(*halves)


N_DEV = 8


def _peer_copies(bufs, send_sems, recv_sems):
    vec, land = bufs
    x, y, c = _me()
    return [pltpu.make_async_remote_copy(src_ref=vec, dst_ref=land.at[4 * x + 2 * y + c], send_sem=send_sems.at[p - 1],
                                         recv_sem=recv_sems.at[p - 1], device_id=(x ^ (p >> 2), y ^ ((p >> 1) & 1), c ^ (p & 1)),
                                         device_id_type=MESH_ID) for p in range(1, N_DEV)]


def _allreduce_small_start(vec, after):
    land = jnp.zeros((N_DEV,) + vec.shape, F32)
    send_sems, recv_sems, bufs, _ = _split_start("allreduce_small_start", [vec, land], N_DEV - 1, _peer_copies, after)
    return send_sems, recv_sems, bufs


def _allreduce_small_wait(state, chip, core, after):
    send_sems, recv_sems, bufs = state
    vec, land = _split_wait("allreduce_small_wait", send_sems, recv_sems, bufs, after, _peer_copies)

    def body(me_ref, v_ref, l_ref, o_ref):
        acc = None
        for k in range(N_DEV):
            term = jnp.where(me_ref[0] == k, v_ref[...], l_ref[k])
            acc = term if acc is None else acc + term
        o_ref[...] = acc

    return pl.pallas_call(
        body, name="allreduce_small_sum",
        grid_spec=pltpu.PrefetchScalarGridSpec(
            num_scalar_prefetch=1, grid=(1,),
            in_specs=[pl.BlockSpec(vec.shape, lambda i, me_ref: (0, 0)), pl.BlockSpec(land.shape, lambda i, me_ref: (0, 0, 0))],
            out_specs=pl.BlockSpec(vec.shape, lambda i, me_ref: (0, 0))),
        out_shape=jax.ShapeDtypeStruct(vec.shape, F32), compiler_params=_params(("arbitrary",)),
    )(_scalar(2 * chip + core), vec, land)


def _adam_math(w, g, m, v):
    m = ADAM_B1 * m + (1.0 - ADAM_B1) * g
    v = ADAM_B2 * v + (1.0 - ADAM_B2) * (g * g)
    m_hat = m / (1.0 - ADAM_B1 ** ADAM_STEP)
    v_hat = v / (1.0 - ADAM_B2 ** ADAM_STEP)
    return -ADAM_LR * (m_hat / (jnp.sqrt(v_hat) + ADAM_EPS) + ADAM_WD * w), m, v


def _adamw(name, w, g, m, v):
    R, C = w.shape
    tr = _row_block(R, C, 8)

    def body(w_ref, g_ref, m_ref, v_ref, d_ref, nm_ref, nv_ref):
        d_ref[...], nm_ref[...], nv_ref[...] = _adam_math(w_ref[...], g_ref[...], m_ref[...], v_ref[...])

    blk = pl.BlockSpec((tr, C), lambda i: (i, 0))
    return pl.pallas_call(
        body, name=name, grid=(R // tr,), in_specs=[blk] * 4, out_specs=[blk] * 3,
        out_shape=[jax.ShapeDtypeStruct((R, C), F32)] * 3, compiler_params=_params(("parallel",)),
    )(w, g, m, v)


def _adamw_halves(name, w, mine, theirs, m, v, c):
    rs, cs = w.shape[0] * w.shape[1], w.shape[2]
    (br, bc), nb, whole, half = _half_blocks(rs, cs, 8)
    spec, get, put = _shard_blocks(w, br, bc)

    def body(c_ref, w_ref, a_ref, b_ref, m_ref, v_ref, g_ref, d_ref, nm_ref, nv_ref):
        g = jnp.where(pl.program_id(0) == c_ref[0], a_ref[...], b_ref[...])
        put(g_ref, g)
        for ref, val in zip((d_ref, nm_ref, nv_ref), _adam_math(get(w_ref), g, get(m_ref), get(v_ref))):
            put(ref, val)

    full = spec(lambda s, i, c_ref: whole(s, i))
    part = pl.BlockSpec((br, bc), lambda s, i, c_ref: half(i))
    return pl.pallas_call(
        body, name=name,
        grid_spec=pltpu.PrefetchScalarGridSpec(num_scalar_prefetch=1, grid=(2, nb), in_specs=[full, part, part, full, full],
                                               out_specs=[full] * 4),
        out_shape=[jax.ShapeDtypeStruct(w.shape, F32)] * 4, compiler_params=_params(("parallel", "parallel")),
    )(_scalar(c), w, mine, theirs, m, v)


def _pack_small(arrs, lanes=LANE):
    flat = jnp.concatenate([a.reshape(-1) for a in arrs])
    n = -(-flat.shape[0] // (8 * lanes)) * 8 * lanes
    return jnp.pad(flat, (0, n - flat.shape[0])).reshape(8, n // 8)


def _unpack_small(vec, shapes):
    flat, out, off = vec.reshape(-1), [], 0
    for s in shapes:
        out.append(flat[off:off + s[0] * s[1]].reshape(s))
        off += s[0] * s[1]
    return out


class _LateWeights:
    def __init__(self, cfg, tag, names, staged, after):
        self.cfg, self.tag, self.names, self.k = cfg, tag, names, 3 * len(names)
        self.send, self.recv, self.bufs, self.token = _split_start(f"gather_{tag}_chips_start", staged, self.k, _gather_to_chips,
                                                                    after)

    def pass_on(self, after):
        bufs = _split_wait(f"gather_{self.tag}_chips_wait", self.send, self.recv, self.bufs, after, _gather_to_chips)
        self.send, self.recv, self.bufs, token = _split_start(f"gather_{self.tag}_sibling_start", bufs, self.k, _gather_to_sibling,
                                                               self.token)
        return token

    def arrived(self, after):
        bufs = _split_wait(f"gather_{self.tag}_sibling_wait", self.send, self.recv, self.bufs, after, _gather_to_sibling)
        return {n: _gathered_to_kernel(self.cfg, n, b) for n, b in zip(self.names, bufs)}


def _step(cfg, a):
    chip = 2 * lax.axis_index("x") + lax.axis_index("y")
    core = lax.axis_index("c")
    big = BIG

    ffn = ("w_gate", "w_up", "w_down")
    first = ("w_in", "w_uq", "w_ukv")
    sp = {n: a[n] for n in SMALL}
    sharded = _pack_small([a[n] for n in SMALL_SHARDED], 2 * LANE)
    slabs = jnp.where(lax.broadcasted_iota(I32, (N_CHIPS,) + sharded.shape, 0) == chip, sharded[None], 0.0)
    staged = {"w_in": _stage_shard("w_in", a["w_in"], chip)}
    in_weight = _LateWeights(cfg, "in", ("w_in", "sharded_small"), [staged["w_in"], slabs], jnp.zeros((8, LANE), F32))
    behind = in_weight.token
    for n in big[1:]:
        behind = staged[n] = _stage_shard(n, a[n], chip, behind)
    in_weight.pass_on(behind)
    xn_early = _rms_pre(cfg, a["x"], sp["mix_pre_g"])
    W = in_weight.arrived(xn_early)
    allp = W.pop("sharded_small").reshape((N_CHIPS,) + sharded.shape)
    per_chip = [_unpack_small(allp[ch], [a[n].shape for n in SMALL_SHARDED]) for ch in range(N_CHIPS)]
    for k, n in enumerate(SMALL_SHARDED):
        sp[n] = jnp.concatenate([per_chip[ch][k] for ch in range(N_CHIPS)], axis=1)

    mla_weights = _LateWeights(cfg, "mla", first[1:], [staged[n] for n in first[1:]], W["w_in"])
    out_weight = _LateWeights(cfg, "out", ("w_out",), [staged["w_out"]], mla_weights.token)
    ffn_weights = _LateWeights(cfg, "ffn", ffn[:2], [staged[n] for n in ffn[:2]], out_weight.token)
    down_weight = _LateWeights(cfg, "down", ffn[2:], [staged[n] for n in ffn[2:]], ffn_weights.token)

    state = {}

    def ffn_grads_ready(grads):
        state["ffn_pairs"], token = _pair_exchange_start("ffn", [_grad_to_chips(cfg, n, grads[n]) for n in ffn_grads])
        return token

    def pair_sums(names, grads, theirs):
        return [_pair_sum(n, g, t, core) for n, g, t in zip(names, grads, theirs)]

    def early_grads_ready(grads):
        g_out = [_grad_to_chips(cfg, "w_out", grads["w_out"])]
        g_ffn, t_ffn = _pair_exchange_wait("ffn", state["ffn_pairs"], g_out[0])
        sums = pair_sums(ffn_grads, g_ffn, t_ffn) + pair_sums(["w_out"], g_out, _pair_exchange("out", g_out))
        state["early"] = _chip_exchange_start("early", sums)
        return state["early"][-1]

    def reduced_halves(tag, names, after):
        send_sems, recv_sems, s_bufs, l_bufs, _ = state[tag]
        s_bufs, l_bufs = _chip_exchange_wait(tag, send_sems, recv_sems, s_bufs, l_bufs, after)
        return [_chip_sum(n, s, t, chip) for n, s, t in zip(names, s_bufs, l_bufs)]

    def in_grad_ready(grads):
        grads = [_grad_to_chips(cfg, n, grads[n]) for n in first]
        state["rest"] = _chip_exchange_start("rest", pair_sums(first, grads, _pair_exchange("rest", grads)))
        return state["rest"][-1]

    ffn_grads = ("w_down", "w_gate", "w_up")
    early = ffn_grads + ("w_out",)
    loss, grad_x, gW, gs = _local_grads(cfg, a["x"], a["loss_target"], W, sp, mla_weights, out_weight, ffn_weights, down_weight,
                                        ffn_grads_ready, early_grads_ready, in_grad_ready, xn_early, down_weight.token)
    out = {"grad_x": grad_x}

    def adamw(names, mine, theirs):
        for n, gm, gt in zip(names, mine, theirs):
            out["grad_" + n], out["delta_" + n], out["new_m_" + n], out["new_v_" + n] = _adamw_halves(
                "adamw_" + n, a[n], gm, gt, a["m_" + n], a["v_" + n], core)

    mine = reduced_halves("early", early, grad_x)
    theirs = _sibling_exchange("early", mine[:1])
    later, _ = _sibling_exchange_start("early", mine[1:], theirs[0])
    adamw(early[:1], mine[:1], theirs)
    e_mine, e_theirs = _sibling_exchange_wait("early", later, out["new_v_" + early[0]])
    adamw(early[3:], e_mine[2:], e_theirs[2:])
    mine = reduced_halves("rest", first, out["new_v_" + early[-1]])
    rest, token = _sibling_exchange_start("rest", mine, mine[0])
    small = _allreduce_small_start(_pack_small([gs[n] for n in SMALL] + [loss]), token)
    adamw(early[1:3], e_mine[:2], e_theirs[:2])
    adamw(first, *_sibling_exchange_wait("rest", rest, out["new_v_" + early[2]]))
    shapes = [gs[n].shape for n in SMALL] + [(1, LANE)]
    red = _unpack_small(_allreduce_small_wait(small, chip, core, out["new_v_" + first[-1]]), shapes)
    g_small = dict(zip(SMALL, red[:-1]))
    for n in SMALL_SHARDED:
        cs = a[n].shape[1]
        g_small[n] = lax.dynamic_slice_in_dim(g_small[n], chip * cs, cs, axis=1)
    out["loss"] = red[-1][0, 0]
    sshapes = [a[n].shape for n in SMALL]
    d, nm, nv = _adamw("adamw_small", _pack_small([a[n] for n in SMALL]), _pack_small([g_small[n] for n in SMALL]),
                       _pack_small([a["m_" + n] for n in SMALL]), _pack_small([a["v_" + n] for n in SMALL]))
    for n, dd, mm, vv in zip(SMALL, _unpack_small(d, sshapes), _unpack_small(nm, sshapes), _unpack_small(nv, sshapes)):
        out["grad_" + n], out["delta_" + n], out["new_m_" + n], out["new_v_" + n] = g_small[n], dd, mm, vv
    return out


def kernel(x, mix_pre_g, w_in, q_norm_g, w_uq, kv_norm_g, w_ukv, ssm_conv_w, ssm_conv_b, dt_bias, a_log, d_skip, ssm_norm_g, w_out, mix_post_g, ffn_pre_g, w_gate, w_up, ffn_conv_w, ffn_conv_b, w_down, ffn_post_g, loss_target, m_mix_pre_g, m_w_in, m_q_norm_g, m_w_uq, m_kv_norm_g, m_w_ukv, m_ssm_conv_w, m_ssm_conv_b, m_dt_bias, m_a_log, m_d_skip, m_ssm_norm_g, m_w_out, m_mix_post_g, m_ffn_pre_g, m_w_gate, m_w_up, m_ffn_conv_w, m_ffn_conv_b, m_w_down, m_ffn_post_g, v_mix_pre_g, v_w_in, v_q_norm_g, v_w_uq, v_kv_norm_g, v_w_ukv, v_ssm_conv_w, v_ssm_conv_b, v_dt_bias, v_a_log, v_d_skip, v_ssm_norm_g, v_w_out, v_mix_post_g, v_ffn_pre_g, v_w_gate, v_w_up, v_ffn_conv_w, v_ffn_conv_b, v_w_down, v_ffn_post_g):
    args = dict(locals())
    def given(k, v):
        if k in ("w_in", "m_w_in", "v_w_in"):
            return jnp.transpose(v, (2, 0, 1))
        return v if k.removeprefix("m_").removeprefix("v_") in BIG or v.ndim < 3 else v[0]

    out = _step(_FULL, {k: given(k, v) for k, v in args.items()})
    res = [out["loss"], out["grad_x"][None]]
    for pre in ("grad_", "delta_", "new_m_", "new_v_"):
        for n in WEIGHTS:
            o = out[pre + n]
            res.append(jnp.transpose(o, (1, 2, 0)) if n == "w_in" else o if n in BIG or args[n].ndim < 3 else o[None])
    return tuple(res)
```

```python
import math

import jax
import jax.numpy as jnp
from jax import lax
from jax.experimental import pallas as pl
from jax.experimental.pallas import tpu as pltpu

F32, BF16, I32 = jnp.float32, jnp.bfloat16, jnp.int32
NN = (((1,), (0,)), ((), ()))
NT = (((1,), (1,)), ((), ()))
TN = (((0,), (0,)), ((), ()))
HI = lax.Precision.HIGHEST
MESH_ID = pl.DeviceIdType.MESH

EPS = 1e-6
CHUNK = 64
NOPE, ROPE, VH = 128, 64, 128
ROPE_THETA = 10000.0
HP, NST = 64, 128
SSM_K, FFN_K = 4, 3
LANE = 128
N_CHIPS = 4
VMEM_LIMIT = 52 * 1024 * 1024
MM_TILE, MM_TILE_K = 1408, 2816

ADAM_LR, ADAM_B1, ADAM_B2, ADAM_EPS, ADAM_WD, ADAM_STEP = 0.001, 0.9, 0.999, 1e-08, 0.01, 10


class _Cfg:
    def __init__(self, S, D, QL, KVL, H, HS, G, DFF, T):
        self.S, self.D, self.QL, self.KVL, self.H, self.HS, self.G, self.DFF, self.T = S, D, QL, KVL, H, HS, G, DFF, T
        self.INNER = HS * HP
        self.CONVCH = self.INNER + 2 * G * NST
        self.QW = H * (NOPE + ROPE)
        self.KVW = H * (NOPE + VH)
        self.MLAW = H * VH
        self.MIXW = self.MLAW + self.INNER
        self.IN_COLS = QL + KVL + ROPE + self.INNER + self.CONVCH + HS
        natural, at = {}, 0
        for name, w in (("c_q", QL), ("c_kv", KVL), ("kr", ROPE), ("z", self.INNER), ("xbc", self.CONVCH), ("dt", HS)):
            natural[name] = (at, w)
            at += w
        self.seg, taken = {}, []
        for name in sorted(natural, key=lambda n: -natural[n][1]):
            w = -(-natural[name][1] // LANE) * LANE
            off = next(o for o in range(0, self.IN_COLS * 2, w) if all(o + w <= t or o >= t + tw for t, tw in taken))
            taken.append((off, w))
            self.seg[name] = (off, w) + natural[name]
        self.EXT = max(o + w for o, w in taken)
        self.NPAIR = HS // 2
        self.REP = HS // G

    def window(self, name):
        off, w, _, _ = self.seg[name]
        return w, off // w


_FULL = _Cfg(S=2048, D=2048, QL=768, KVL=512, H=8, HS=16, G=2, DFF=5632, T=256)
BIG = ("w_in", "w_uq", "w_ukv", "w_out", "w_gate", "w_up", "w_down")

SMALL = ("mix_pre_g", "q_norm_g", "kv_norm_g", "ssm_conv_w", "ssm_conv_b", "dt_bias", "a_log", "d_skip", "ssm_norm_g",
         "mix_post_g", "ffn_pre_g", "ffn_conv_w", "ffn_conv_b", "ffn_post_g")
SMALL_SHARDED = ("ssm_conv_w", "ffn_conv_w")
WEIGHTS = ("mix_pre_g", "w_in", "q_norm_g", "w_uq", "kv_norm_g", "w_ukv", "ssm_conv_w", "ssm_conv_b", "dt_bias", "a_log",
           "d_skip", "ssm_norm_g", "w_out", "mix_post_g", "ffn_pre_g", "w_gate", "w_up", "ffn_conv_w", "ffn_conv_b",
           "w_down", "ffn_post_g")


def _pick(n, target, mult):
    best = None
    for d in range(mult, min(n, target) + 1, mult):
        if n % d == 0:
            best = d
    return best if best is not None else n


def _params(sem=None):
    kw = dict(vmem_limit_bytes=VMEM_LIMIT)
    if sem is not None:
        kw["dimension_semantics"] = sem
    return pltpu.CompilerParams(**kw)


def _dot(a, b, dims=NN, precision=None):
    return lax.dot_general(a, b, dims, preferred_element_type=F32, precision=precision)


def _sigmoid(x):
    return 1.0 / (1.0 + jnp.exp(-x))


def _rs(x):
    return lax.rsqrt(jnp.mean(x * x, axis=-1, keepdims=True) + EPS)


def _rms_back(xh, r, dn):
    return r * (dn - xh * jnp.mean(dn * xh, axis=-1, keepdims=True))


def _colsum(v):
    return jnp.sum(v, axis=0, keepdims=True)


def _matmul(name, a, b, mode, out_dtype, a2=None, b2=None, chips=False, after=None):
    cs = None
    if mode == "nn":
        (M, K), N = a.shape, b.shape[-1]
        if chips:
            cs, N = N, N_CHIPS * N
    elif mode == "nt":
        (M, K), N = a.shape, b.shape[-2]
        if chips:
            cs = b.shape[-1]
    else:
        (K, M), N = a.shape, b.shape[1]
        if chips:
            cs = N // N_CHIPS
    tm = _pick(M, MM_TILE, LANE)
    tn = _pick(cs if chips and mode != "nt" else N, MM_TILE, LANE)
    tk = _pick(cs, MM_TILE, LANE) if chips and mode == "nt" else _pick(K, MM_TILE_K, LANE)
    nk = K // tk
    dims = {"nn": NN, "nt": NT, "tn": TN}[mode]
    a_spec = pl.BlockSpec((tk, tm), lambda i, j, k: (k, i)) if mode == "tn" else pl.BlockSpec((tm, tk), lambda i, j, k: (i, k))
    b_spec = pl.BlockSpec((tn, tk), lambda i, j, k: (j, k)) if mode == "nt" else pl.BlockSpec((tk, tn), lambda i, j, k: (k, j))
    o_spec = pl.BlockSpec((tm, tn), lambda i, j, k: (i, j))
    o_shape = (M, N)
    if chips and mode == "nn":
        per = cs // tn
        b_spec = pl.BlockSpec((None, tk, tn), lambda i, j, k: (j // per, k, j % per))
    elif chips and mode == "nt":
        per = cs // tk
        b_spec = pl.BlockSpec((None, tn, tk), lambda i, j, k: (k // per, j, k % per))
    elif chips:
        per = cs // tn
        o_spec = pl.BlockSpec((None, tm, tn), lambda i, j, k: (j // per, i, j % per))
        o_shape = (N_CHIPS, M, cs)
    two = a2 is not None

    def product(refs):
        part = _dot(refs[0][...].astype(BF16), refs[1][...].astype(BF16), dims)
        if two:
            part += _dot(refs[2][...].astype(BF16), refs[3][...].astype(BF16), dims)
        return part

    def body_whole_k(*refs):
        refs[-1][...] = product(refs).astype(refs[-1].dtype)

    def body(*refs):
        o_ref, acc_ref = refs[-2], refs[-1]
        k = pl.program_id(2)

        @pl.when(k == 0)
        def _():
            acc_ref[...] = product(refs)

        @pl.when(k > 0)
        def _():
            acc_ref[...] += product(refs)

        @pl.when(k == nk - 1)
        def _():
            o_ref[...] = acc_ref[...].astype(o_ref.dtype)

    ins = ((a, b, a2, b2) if two else (a, b)) + (() if after is None else (after,))
    return pl.pallas_call(
        body_whole_k if nk == 1 else body, name=name, grid=(M // tm, N // tn, nk),
        in_specs=[a_spec, b_spec] * (2 if two else 1) + ([] if after is None else [pl.BlockSpec(memory_space=pl.ANY)]),
        out_specs=o_spec,
        out_shape=jax.ShapeDtypeStruct(o_shape, out_dtype),
        scratch_shapes=[] if nk == 1 else [pltpu.VMEM((tm, tn), F32)],
        compiler_params=_params(("parallel", "parallel", "arbitrary")),
    )(*ins)


def _window(a):
    return (a[0], *a[1]) if isinstance(a, tuple) else (a, a.shape[1], 0)


def _rowwise(name, fn, rows, mats, outs, reds, ts):
    rows, widths, blocks = zip(*[_window(a) for a in rows])
    S = rows[0].shape[0]
    nr, nm, no = len(rows), len(mats), len(outs)

    def body(*refs):
        res = fn(*[r[...] for r in refs[:nr + nm]])
        res = res if isinstance(res, (tuple, list)) else (res,)
        for r, v in zip(refs[nr + nm:nr + nm + no], res[:no]):
            r[...] = v.astype(r.dtype)
        first = pl.program_id(0) == 0
        for r, v in zip(refs[nr + nm + no:], res[no:]):
            @pl.when(first)
            def _():
                r[...] = jnp.broadcast_to(v, r.shape)

            @pl.when(jnp.logical_not(first))
            def _():
                r[...] += jnp.broadcast_to(v, r.shape)

    in_specs = [pl.BlockSpec((ts, w), lambda i, b=b: (i, b)) for w, b in zip(widths, blocks)]
    in_specs += [pl.BlockSpec(m.shape, lambda i, nd=m.ndim: (0,) * nd) for m in mats]
    out_specs = [pl.BlockSpec((ts, w), lambda i: (i, 0)) for w, _ in outs]
    out_specs += [pl.BlockSpec(s, lambda i: (0, 0)) for s in reds]
    out_shape = [jax.ShapeDtypeStruct((S, w), dt) for w, dt in outs] + [jax.ShapeDtypeStruct(s, F32) for s in reds]
    return pl.pallas_call(
        body, name=name, grid=(S // ts,), in_specs=in_specs, out_specs=out_specs, out_shape=out_shape,
        compiler_params=_params(("arbitrary",) if reds else ("parallel",)),
    )(*rows, *mats)


def _shift_down(v, s):
    if s == 0:
        return v
    rows = lax.broadcasted_iota(I32, v.shape, 0)
    return jnp.where(rows >= s, pltpu.roll(v, s, 0), 0.0)


def _shift_up(v, s):
    if s == 0:
        return v
    n = v.shape[0]
    rows = lax.broadcasted_iota(I32, v.shape, 0)
    return jnp.where(rows < n - s, pltpu.roll(v, n - s, 0), 0.0)


def _conv(x, w, b):
    K = w.shape[0]
    y = jnp.broadcast_to(b, x.shape)
    for k in range(K):
        y = y + w[k:k + 1, :] * _shift_down(x, K - 1 - k)
    return y


def _conv_back(x, w, dc):
    K = w.shape[0]
    dx = jnp.zeros_like(x)
    dw = []
    for k in range(K):
        up = _shift_up(dc, K - 1 - k)
        dx = dx + w[k:k + 1, :] * up
        dw.append(_colsum(up * x))
    return dx, jnp.concatenate(dw, axis=0), _colsum(dc)


def _colwise(name, fn, cols, vecs, outs, pouts, tc):
    cols, widths, blocks = zip(*[_window(a) for a in cols])
    S, C = cols[0].shape[0], widths[0]
    firsts = [b * (C // tc) for b in blocks]
    nc_, nv, no = len(cols), len(vecs), len(outs)

    def body(*refs):
        res = fn(*[r[...] for r in refs[:nc_ + nv]])
        res = res if isinstance(res, (tuple, list)) else (res,)
        for r, v in zip(refs[nc_ + nv:], res):
            r[...] = v.astype(r.dtype)

    in_specs = [pl.BlockSpec((S, tc), lambda j, f=f: (0, f + j)) for f in firsts]
    in_specs += [pl.BlockSpec((v.shape[0], tc), lambda j: (0, j)) for v in vecs]
    out_specs = [pl.BlockSpec((S, tc), lambda j: (0, j)) for _ in outs] + [pl.BlockSpec((k, tc), lambda j: (0, j)) for k in pouts]
    out_shape = [jax.ShapeDtypeStruct((S, C), dt) for dt in outs] + [jax.ShapeDtypeStruct((k, C), F32) for k in pouts]
    return pl.pallas_call(
        body, name=name, grid=(C // tc,), in_specs=in_specs, out_specs=out_specs, out_shape=out_shape,
        compiler_params=_params(("parallel",)),
    )(*cols, *vecs)


_G0, _G1 = math.sqrt(2.0 / math.pi), 0.044715


def _gelu(g):
    th = jnp.tanh(_G0 * (g + _G1 * g * g * g))
    return 0.5 * g * (1.0 + th), th


def _ffn_act(gate_pre, up, w, b):
    act, _ = _gelu(_conv(gate_pre, w, b))
    return act * up


def _ffn_act_back(dact, gate_pre, up, w, b):
    g = _conv(gate_pre, w, b)
    ge, th = _gelu(g)
    dge = 0.5 * (1.0 + th) + 0.5 * g * (1.0 - th * th) * _G0 * (1.0 + 3.0 * _G1 * g * g)
    dup = dact * ge
    dgate_pre, dw, db = _conv_back(gate_pre, w, dact * up * dge)
    return dgate_pre, dup, dw, db


def _ssm_act(xbc, w, b):
    c = _conv(xbc, w, b)
    return c * _sigmoid(c)


def _ssm_act_back(dxc, xbc, w, b):
    c = _conv(xbc, w, b)
    sg = _sigmoid(c)
    return _conv_back(xbc, w, dxc * sg * (1.0 + c * (1.0 - sg)))


def _rope_tables(S):
    inv = 1.0 / (ROPE_THETA ** (jnp.arange(0, ROPE, 2, dtype=F32) / ROPE))
    ang = jnp.arange(S, dtype=F32)[:, None] * inv[None, :]
    cos, sin = jnp.cos(ang), jnp.sin(ang)
    return jnp.tile(cos, (1, 4)), jnp.tile(jnp.concatenate([-sin, sin], axis=1), (1, 2))


def _swap_halves(x):
    lane = lax.broadcasted_iota(I32, x.shape, 1)
    w = x.shape[1]
    return jnp.where((lane % ROPE) < ROPE // 2, pltpu.roll(x, w - ROPE // 2, 1), pltpu.roll(x, ROPE // 2, 1))


def _rot(x, cos2, sin2):
    return x * cos2 + _swap_halves(x) * sin2


def _rot_back(dy, cos2, sin2):
    return dy * cos2 + _swap_halves(dy * sin2)


def _mla_pack(cfg, q, kv, kr, cos2, sin2):
    S, H = cfg.S, cfg.H
    ts = _pick(S, 256, 8)
    kr, _, kr_block = _window(kr)

    def body(q_ref, kv_ref, kr_ref, c_ref, s_ref, Q_ref, K_ref, V_ref):
        c2, s2 = c_ref[...], s_ref[...]
        krr = _rot(kr_ref[...], c2, s2)
        kr_half = (krr.astype(BF16), pltpu.roll(krr, ROPE, 1).astype(BF16))
        for j in range(H // 2):
            qr = _rot(q_ref[:, (H + j) * LANE:(H + j + 1) * LANE], c2, s2).astype(BF16)
            for h in (2 * j, 2 * j + 1):
                Q_ref[h, :, 0:LANE] = q_ref[:, h * LANE:(h + 1) * LANE].astype(BF16)
                Q_ref[h, :, LANE:] = qr
                K_ref[h, :, 0:LANE] = kv_ref[:, h * LANE:(h + 1) * LANE].astype(BF16)
                K_ref[h, :, LANE:] = kr_half[h % 2]
                V_ref[h] = kv_ref[:, (H + h) * LANE:(H + h + 1) * LANE].astype(BF16)

    tab = pl.BlockSpec((ts, LANE), lambda i: (i, 0))
    heads = lambda w: pl.BlockSpec((H, ts, w), lambda i: (0, i, 0))
    return pl.pallas_call(
        body, name="mla_pack", grid=(S // ts,),
        in_specs=[pl.BlockSpec((ts, cfg.QW), lambda i: (i, 0)), pl.BlockSpec((ts, cfg.KVW), lambda i: (i, 0)),
                  pl.BlockSpec((ts, LANE), lambda i: (i, kr_block)), tab, tab],
        out_specs=[heads(2 * LANE), heads(2 * LANE), heads(LANE)],
        out_shape=[jax.ShapeDtypeStruct((H, S, 2 * LANE), BF16), jax.ShapeDtypeStruct((H, S, 2 * LANE), BF16),
                   jax.ShapeDtypeStruct((H, S, LANE), BF16)],
        compiler_params=_params(("parallel",)),
    )(q, kv, kr, cos2, sin2)


def _mla_unpack(cfg, dQ, dK, dV, cos2, sin2):
    S, H = cfg.S, cfg.H
    ts = _pick(S, 256, 8)

    def body(dQ_ref, dK_ref, dV_ref, c_ref, s_ref, dq_ref, dkv_ref, dkr_ref):
        c2, s2 = c_ref[...], s_ref[...]
        lo = lax.broadcasted_iota(I32, (ts, LANE), 1) < ROPE
        tk = jnp.zeros((ts, LANE), F32)
        for h in range(H):
            dq_ref[:, h * LANE:(h + 1) * LANE] = dQ_ref[h, :, 0:LANE].astype(BF16)
            dkv_ref[:, h * LANE:(h + 1) * LANE] = dK_ref[h, :, 0:LANE].astype(BF16)
            dkv_ref[:, (H + h) * LANE:(H + h + 1) * LANE] = dV_ref[h].astype(BF16)
            own = lo if h % 2 == 0 else jnp.logical_not(lo)
            tk = tk + jnp.where(own, dK_ref[h, :, LANE:], 0.0)
        for j in range(H // 2):
            dr = dQ_ref[2 * j, :, LANE:] + dQ_ref[2 * j + 1, :, LANE:]
            dq_ref[:, (H + j) * LANE:(H + j + 1) * LANE] = _rot_back(dr, c2, s2).astype(BF16)
        dkr_rot = jnp.where(lo, tk + pltpu.roll(tk, ROPE, 1), 0.0)
        dkr_ref[...] = _rot_back(dkr_rot, c2, s2).astype(BF16)

    tab = pl.BlockSpec((ts, LANE), lambda i: (i, 0))
    return pl.pallas_call(
        body, name="mla_unpack", grid=(S // ts,),
        in_specs=[pl.BlockSpec((H, ts, 2 * LANE), lambda i: (0, i, 0)), pl.BlockSpec((H, ts, 2 * LANE), lambda i: (0, i, 0)),
                  pl.BlockSpec((H, ts, LANE), lambda i: (0, i, 0)), tab, tab],
        out_specs=[pl.BlockSpec((ts, cfg.QW), lambda i: (i, 0)), pl.BlockSpec((ts, cfg.KVW), lambda i: (i, 0)), tab],
        out_shape=[jax.ShapeDtypeStruct((S, cfg.QW), BF16), jax.ShapeDtypeStruct((S, cfg.KVW), BF16),
                   jax.ShapeDtypeStruct((S, LANE), BF16)],
        compiler_params=_params(("parallel",)),
    )(dQ, dK, dV, cos2, sin2)


_ATT_T = 256
_ATT_HB = 8
_ATT_SCALE = (NOPE + ROPE) ** -0.5


def _diag_mask(transposed=False):
    r = lax.broadcasted_iota(I32, (_ATT_T, _ATT_T), 0) // CHUNK
    c = lax.broadcasted_iota(I32, (_ATT_T, _ATT_T), 1) // CHUNK
    return r <= c if transposed else c <= r


def _row_form(col):
    return jnp.broadcast_to(col, (col.shape[0], LANE)).T[0:8, :]


def _attn_fwd(cfg, Q, K, V):
    S, H, T, HB = cfg.S, cfg.H, _ATT_T, min(cfg.H, _ATT_HB)

    def body(q_ref, k_ref, v_ref, o_ref, lse_ref, lse_t_ref):
        qi = pl.program_id(1)

        def head_step(b, kb, carry, mask):
            m, l, acc = carry
            ks = pl.multiple_of(kb * T, T)
            s = _dot(q_ref[b], k_ref[b, pl.ds(ks, T), :], NT) * _ATT_SCALE
            if mask is not None:
                s = jnp.where(mask, s, -1e30)
            m_new = jnp.maximum(m, jnp.max(s, axis=1, keepdims=True))
            p = jnp.exp(s - m_new)
            alpha = jnp.exp(m - m_new)
            l = alpha * l + jnp.sum(p, axis=1, keepdims=True)
            acc = alpha * acc + _dot(p.astype(BF16), v_ref[b, pl.ds(ks, T), :])
            return m_new, l, acc

        def step(kb, carry, mask=None):
            return tuple(head_step(b, kb, carry[b], mask) for b in range(HB))

        init = (jnp.full((T, 1), -1e30, F32), jnp.zeros((T, 1), F32), jnp.zeros((T, VH), F32))
        done = step(qi, lax.fori_loop(0, qi, step, (init,) * HB), _diag_mask())
        for b, (m, l, acc) in enumerate(done):
            o_ref[:, b * LANE:(b + 1) * LANE] = acc / l
            lse = m + jnp.log(l)
            lse_ref[:, b * LANE:(b + 1) * LANE] = jnp.broadcast_to(lse, (T, LANE))
            lse_t_ref[b] = _row_form(lse)

    return pl.pallas_call(
        body, name="attn_fwd", grid=(H // HB, S // T),
        in_specs=[pl.BlockSpec((HB, T, 2 * LANE), lambda h, i: (h, i, 0)), pl.BlockSpec((HB, S, 2 * LANE), lambda h, i: (h, 0, 0)),
                  pl.BlockSpec((HB, S, LANE), lambda h, i: (h, 0, 0))],
        out_specs=[pl.BlockSpec((T, HB * LANE), lambda h, i: (i, h)), pl.BlockSpec((T, HB * LANE), lambda h, i: (i, h)),
                   pl.BlockSpec((HB, 8, T), lambda h, i: (h, 0, i))],
        out_shape=[jax.ShapeDtypeStruct((S, H * LANE), F32), jax.ShapeDtypeStruct((S, H * LANE), F32),
                   jax.ShapeDtypeStruct((H, 8, S), F32)],
        compiler_params=_params(("parallel", "parallel")),
    )(Q, K, V)


def _attn_dq(cfg, Q, K, V, do, o, lse, after):
    S, H, T, HB = cfg.S, cfg.H, _ATT_T, min(cfg.H, _ATT_HB)

    def body(q_ref, k_ref, v_ref, do_ref, o_ref, lse_ref, after_ref, dq_ref, dl_t_ref):
        qi = pl.program_id(1)
        do = [do_ref[:, b * LANE:(b + 1) * LANE] for b in range(HB)]
        delta = [jnp.sum(do[b] * o_ref[:, b * LANE:(b + 1) * LANE], axis=1, keepdims=True) for b in range(HB)]
        dob = [d.astype(BF16) for d in do]

        def head_step(b, kb, dq, mask):
            ks = pl.multiple_of(kb * T, T)
            k = k_ref[b, pl.ds(ks, T), :]
            s = _dot(q_ref[b], k, NT) * _ATT_SCALE
            if mask is not None:
                s = jnp.where(mask, s, -1e30)
            p = jnp.exp(s - lse_ref[:, b * LANE:b * LANE + 1])
            dp = _dot(dob[b], v_ref[b, pl.ds(ks, T), :], NT)
            ds = p * (dp - delta[b]) * _ATT_SCALE
            return dq + _dot(ds.astype(BF16), k)

        def step(kb, dqs, mask=None):
            return tuple(head_step(b, kb, dqs[b], mask) for b in range(HB))

        dqs = step(qi, lax.fori_loop(0, qi, step, (jnp.zeros((T, 2 * LANE), F32),) * HB), _diag_mask())
        for b in range(HB):
            dq_ref[b] = dqs[b]
            dl_t_ref[b] = _row_form(delta[b])

    col = pl.BlockSpec((T, HB * LANE), lambda h, i: (i, h))
    return pl.pallas_call(
        body, name="attn_dq", grid=(H // HB, S // T),
        in_specs=[pl.BlockSpec((HB, T, 2 * LANE), lambda h, i: (h, i, 0)), pl.BlockSpec((HB, S, 2 * LANE), lambda h, i: (h, 0, 0)),
                  pl.BlockSpec((HB, S, LANE), lambda h, i: (h, 0, 0)), col, col, col, _ANY],
        out_specs=[pl.BlockSpec((HB, T, 2 * LANE), lambda h, i: (h, i, 0)), pl.BlockSpec((HB, 8, T), lambda h, i: (h, 0, i))],
        out_shape=[jax.ShapeDtypeStruct((H, S, 2 * LANE), F32), jax.ShapeDtypeStruct((H, 8, S), F32)],
        compiler_params=_params(("parallel", "parallel")),
    )(Q, K, V, do, o, lse, after)


def _attn_dkv(cfg, Q, K, V, do, lse_t, delta_t):
    S, H, T, HB = cfg.S, cfg.H, _ATT_T, min(cfg.H, _ATT_HB)
    nq = S // T

    def body(q_ref, k_ref, v_ref, do_ref, lse_ref, dl_ref, dk_ref, dv_ref):
        kb = pl.program_id(1)

        def head_step(b, qi, carry, mask):
            dk, dv = carry
            qs = pl.multiple_of(qi * T, T)
            q = q_ref[b, pl.ds(qs, T), :]
            dob = do_ref[pl.ds(qs, T), b * LANE:(b + 1) * LANE].astype(BF16)
            s = _dot(k_ref[b], q, NT) * _ATT_SCALE
            if mask is not None:
                s = jnp.where(mask, s, -1e30)
            p = jnp.exp(s - lse_ref[b, 0:1, pl.ds(qs, T)])
            dv = dv + _dot(p.astype(BF16), dob)
            dp = _dot(v_ref[b], dob, NT)
            ds = p * (dp - dl_ref[b, 0:1, pl.ds(qs, T)]) * _ATT_SCALE
            dk = dk + _dot(ds.astype(BF16), q)
            return dk, dv

        def step(qi, carry, mask=None):
            return tuple(head_step(b, qi, carry[b], mask) for b in range(HB))

        zero = (jnp.zeros((T, 2 * LANE), F32), jnp.zeros((T, VH), F32))
        done = lax.fori_loop(kb + 1, nq, step, step(kb, (zero,) * HB, _diag_mask(transposed=True)))
        for b, (dk, dv) in enumerate(done):
            dk_ref[b] = dk
            dv_ref[b] = dv

    row = pl.BlockSpec((HB, 8, S), lambda h, j: (h, 0, 0))
    return pl.pallas_call(
        body, name="attn_dkv", grid=(H // HB, S // T),
        in_specs=[pl.BlockSpec((HB, S, 2 * LANE), lambda h, j: (h, 0, 0)), pl.BlockSpec((HB, T, 2 * LANE), lambda h, j: (h, j, 0)),
                  pl.BlockSpec((HB, T, LANE), lambda h, j: (h, j, 0)), pl.BlockSpec((S, HB * LANE), lambda h, j: (0, h)), row, row],
        out_specs=[pl.BlockSpec((HB, T, 2 * LANE), lambda h, j: (h, j, 0)), pl.BlockSpec((HB, T, LANE), lambda h, j: (h, j, 0))],
        out_shape=[jax.ShapeDtypeStruct((H, S, 2 * LANE), F32), jax.ShapeDtypeStruct((H, S, LANE), F32)],
        compiler_params=_params(("parallel", "parallel")),
    )(Q, K, V, do, lse_t, delta_t)


def _expand_matrix(cfg):
    r = lax.broadcasted_iota(I32, (LANE, cfg.INNER), 0)
    c = lax.broadcasted_iota(I32, (LANE, cfg.INNER), 1)
    return (r == c // HP).astype(F32)


def _softplus(x):
    return jnp.maximum(x, 0.0) + jnp.log(1.0 + jnp.exp(-jnp.abs(x)))


def _ssd_prep(cfg, dt_raw, dt_bias_pad, a_log_pad, expand):
    HS = cfg.HS

    def fn(raw, bias, alog, E):
        heads = lax.broadcasted_iota(I32, raw.shape, 1) < HS
        dt = jnp.where(heads, _softplus(raw + bias), 0.0)
        a = dt * jnp.where(heads[0:1], -jnp.exp(alog), 0.0)
        return dt, a, _dot(dt, E, precision=HI), _dot(a, E, precision=HI)

    return _rowwise("ssd_prep", fn, [dt_raw], [dt_bias_pad, a_log_pad, expand],
                    [(LANE, F32), (LANE, F32), (cfg.INNER, F32), (cfg.INNER, F32)], [], _pick(cfg.S, 512, 8))


def _tril(T):
    return lax.broadcasted_iota(I32, (T, T), 0) >= lax.broadcasted_iota(I32, (T, T), 1)


def _ssd_fwd(cfg, xc, dt_exp, a_exp, a_small, dskip_exp):
    S, T, INNER, G, NPAIR = cfg.S, cfg.T, cfg.INNER, cfg.G, cfg.NPAIR
    NC = S // T

    def body(xc_ref, dte_ref, ae_ref, as_ref, dsk_ref, y_ref, hin_ref, ht_ref):
        @pl.when(pl.program_id(0) == 0)
        def _():
            ht_ref[...] = jnp.zeros_like(ht_ref)

        tril = _tril(T)
        tri = tril.astype(F32)
        acs_s = _dot(tri, as_ref[...], precision=HI)
        acs_e = _dot(tri, ae_ref[...], precision=HI)
        acs_t = acs_s.T
        lo = lax.broadcasted_iota(I32, (T, LANE), 1) < HP
        for g in range(G):
            Bb = xc_ref[:, INNER + g * NST:INNER + (g + 1) * NST].astype(BF16)
            Cb = xc_ref[:, INNER + (G + g) * NST:INNER + (G + g + 1) * NST].astype(BF16)
            Gm = _dot(Cb, Bb, NT)
            for j in range(g * NPAIR // G, (g + 1) * NPAIR // G):
                sl = slice(j * LANE, (j + 1) * LANE)
                Xp = xc_ref[:, sl]
                Xdt = Xp * dte_ref[:, sl]
                Xb = Xdt.astype(BF16)
                acs_p = acs_e[:, sl]
                last = acs_p[T - 1:T, :]
                Hin = ht_ref[j]
                hin_ref[0, j] = Hin
                yd = []
                for e in (0, 1):
                    h = 2 * j + e
                    Lm = jnp.exp(jnp.where(tril, acs_s[:, h:h + 1] - acs_t[h:h + 1, :], -1e30))
                    yd.append(_dot((Gm * Lm).astype(BF16), Xb))
                y_off = _dot(Cb, Hin.astype(BF16)) * jnp.exp(acs_p)
                y_ref[:, sl] = jnp.where(lo, yd[0], yd[1]) + y_off + Xp * dsk_ref[:, sl]
                st = _dot(Bb, (Xdt * jnp.exp(last - acs_p)).astype(BF16), TN)
                ht_ref[j] = jnp.exp(last) * Hin + st

    rows = lambda w: pl.BlockSpec((T, w), lambda c: (c, 0))
    return pl.pallas_call(
        body, name="ssd_fwd", grid=(NC,),
        in_specs=[rows(cfg.CONVCH), rows(INNER), rows(INNER), rows(LANE), pl.BlockSpec((1, INNER), lambda c: (0, 0))],
        out_specs=[rows(INNER), pl.BlockSpec((1, NPAIR, NST, LANE), lambda c: (c, 0, 0, 0))],
        out_shape=[jax.ShapeDtypeStruct((S, INNER), F32), jax.ShapeDtypeStruct((NC, NPAIR, NST, LANE), F32)],
        scratch_shapes=[pltpu.VMEM((NPAIR, NST, LANE), F32)],
        compiler_params=_params(("arbitrary",)),
    )(xc, dt_exp, a_exp, a_small, dskip_exp)


def _ssd_bwd(cfg, dy, xc, dt_exp, a_exp, a_small, dskip_exp, hin, dt_raw, dt_bias_pad, a_log_pad, expand):
    S, T, INNER, G, NPAIR, HS = cfg.S, cfg.T, cfg.INNER, cfg.G, cfg.NPAIR, cfg.HS
    NC = S // T

    def body(dy_ref, xc_ref, dte_ref, ae_ref, as_ref, dsk_ref, hin_ref, raw_ref, bias_ref, alog_ref, e_ref,
             dxc_ref, draw_ref, dbias_ref, dalog_ref, dskip_ref, dht_ref, cols_ref, rows_ref, dacs_ref, ddt_ref):
        first = pl.program_id(0) == 0

        @pl.when(first)
        def _():
            dht_ref[...] = jnp.zeros_like(dht_ref)

        tril = _tril(T)
        tri = tril.astype(F32)
        a_s = as_ref[...]
        acs_s = _dot(tri, a_s, precision=HI)
        acs_e = _dot(tri, ae_ref[...], precision=HI)
        acs_t = acs_s.T
        lo = lax.broadcasted_iota(I32, (T, LANE), 1) < HP
        last_row = lax.broadcasted_iota(I32, (T, LANE), 0) == T - 1
        cols_ref[...] = jnp.zeros_like(cols_ref)
        rows_ref[...] = jnp.zeros_like(rows_ref)
        dsk_parts = []
        for g in range(G):
            bsl = slice(INNER + g * NST, INNER + (g + 1) * NST)
            csl = slice(INNER + (G + g) * NST, INNER + (G + g + 1) * NST)
            Bb = xc_ref[:, bsl].astype(BF16)
            Cb = xc_ref[:, csl].astype(BF16)
            Gm = _dot(Cb, Bb, NT)
            dG = jnp.zeros((T, T), F32)
            dB = jnp.zeros((T, NST), F32)
            dC = jnp.zeros((T, NST), F32)
            for j in range(g * NPAIR // G, (g + 1) * NPAIR // G):
                sl = slice(j * LANE, (j + 1) * LANE)
                Xp = xc_ref[:, sl]
                dtp = dte_ref[:, sl]
                Xdt = Xp * dtp
                Xb = Xdt.astype(BF16)
                acs_p = acs_e[:, sl]
                last = acs_p[T - 1:T, :]
                e_p, dec, cd = jnp.exp(acs_p), jnp.exp(last - acs_p), jnp.exp(last)
                Hin = hin_ref[0, j]
                Hb = Hin.astype(BF16)
                dHn = dht_ref[j]
                dHb = dHn.astype(BF16)
                dYp = dy_ref[:, sl]
                z = _dot(Cb, Hb)
                dz = (dYp * e_p).astype(BF16)
                dacs_p = dYp * z * e_p
                dC = dC + _dot(dz, Hb, NT)
                dHin = _dot(Cb, dz, TN) + cd * dHn
                dlast = _colsum(dHn * Hin) * cd
                qv = _dot(Bb, dHb)
                dXdt = qv * dec
                ddec = qv * Xdt * dec
                dacs_p = dacs_p - ddec
                dlast = dlast + _colsum(ddec)
                dB = dB + _dot((Xdt * dec).astype(BF16), dHb, NT)
                for e in (0, 1):
                    h = 2 * j + e
                    Lm = jnp.exp(jnp.where(tril, acs_s[:, h:h + 1] - acs_t[h:h + 1, :], -1e30))
                    Mh = Gm * Lm
                    dYe = jnp.where(lo if e == 0 else jnp.logical_not(lo), dYp, 0.0).astype(BF16)
                    dM = _dot(dYe, Xb, NT)
                    dXdt = dXdt + _dot(Mh.astype(BF16), dYe, TN)
                    W = dM * Mh
                    cols_ref[:, h:h + 1] = jnp.sum(W, axis=1, keepdims=True)
                    rows_ref[h:h + 1, :] = _colsum(W)
                    dG = dG + dM * Lm
                dacs_ref[:, sl] = dacs_p + jnp.where(last_row, dlast, 0.0)
                ddt_ref[:, sl] = dXdt * Xp
                dxc_ref[:, sl] = dXdt * dtp + dYp * dsk_ref[:, sl]
                dsk_parts.append(_colsum(dYp * Xp))
                dht_ref[j] = dHin
            dGb = dG.astype(BF16)
            dxc_ref[:, bsl] = dB + _dot(dGb, Cb, TN)
            dxc_ref[:, csl] = dC + _dot(dGb, Bb)
        E = e_ref[...]
        dacs_s = cols_ref[...] - rows_ref[...].T + _dot(dacs_ref[...], E, NT, precision=HI)
        da = _dot(tri, dacs_s, TN, precision=HI)
        heads = lax.broadcasted_iota(I32, (1, LANE), 1) < HS
        A = jnp.where(heads, -jnp.exp(alog_ref[...]), 0.0)
        ddt = _dot(ddt_ref[...], E, NT, precision=HI) + da * A
        draw = jnp.where(heads, ddt * _sigmoid(raw_ref[...] + bias_ref[...]), 0.0)
        draw_ref[...] = draw
        dsk = _dot(jnp.broadcast_to(jnp.concatenate(dsk_parts, axis=1), (8, INNER)), E, NT, precision=HI)[0:1]
        for ref, val in ((dbias_ref, _colsum(draw)), (dalog_ref, _colsum(da * a_s)), (dskip_ref, dsk)):
            @pl.when(first)
            def _():
                ref[...] = val

            @pl.when(jnp.logical_not(first))
            def _():
                ref[...] += val

    dt_raw, _, raw_block = _window(dt_raw)
    rows = lambda w, b=0: pl.BlockSpec((T, w), lambda c: (NC - 1 - c, b))
    vec = lambda w: pl.BlockSpec((1, w), lambda c: (0, 0))
    return pl.pallas_call(
        body, name="ssd_bwd", grid=(NC,),
        in_specs=[rows(INNER), rows(cfg.CONVCH), rows(INNER), rows(INNER), rows(LANE), vec(INNER),
                  pl.BlockSpec((1, NPAIR, NST, LANE), lambda c: (NC - 1 - c, 0, 0, 0)), rows(LANE, raw_block), vec(LANE), vec(LANE),
                  pl.BlockSpec((LANE, INNER), lambda c: (0, 0))],
        out_specs=[rows(cfg.CONVCH), rows(LANE), vec(LANE), vec(LANE), vec(LANE)],
        out_shape=[jax.ShapeDtypeStruct((S, cfg.CONVCH), F32), jax.ShapeDtypeStruct((S, LANE), F32)]
        + [jax.ShapeDtypeStruct((1, LANE), F32)] * 3,
        scratch_shapes=[pltpu.VMEM((NPAIR, NST, LANE), F32), pltpu.VMEM((T, LANE), F32), pltpu.VMEM((LANE, T), F32),
                        pltpu.VMEM((T, INNER), F32), pltpu.VMEM((T, INNER), F32)],
        compiler_params=_params(("arbitrary",)),
    )(dy, xc, dt_exp, a_exp, a_small, dskip_exp, hin, dt_raw, dt_bias_pad, a_log_pad, expand)


def _ssd_post(cfg, y, z, norm_g):
    W = cfg.INNER // cfg.G

    def fn(y, z, g):
        yz = y * z * _sigmoid(z)
        return jnp.concatenate([yz[:, i * W:(i + 1) * W] * _rs(yz[:, i * W:(i + 1) * W]) for i in range(cfg.G)], axis=1) * g

    return _rowwise("ssd_post", fn, [y, z], [norm_g], [(cfg.INNER, BF16)], [], _pick(cfg.S, 256, 8))[0]


def _ssd_post_bwd(cfg, db, y, z, norm_g):
    W = cfg.INNER // cfg.G

    def fn(db, y, z, g):
        sg = _sigmoid(z)
        yz = y * z * sg
        dn = db * g
        dyz, nh = [], []
        for i in range(cfg.G):
            seg = yz[:, i * W:(i + 1) * W]
            r = _rs(seg)
            nh.append(seg * r)
            dyz.append(_rms_back(nh[-1], r, dn[:, i * W:(i + 1) * W]))
        dyz = jnp.concatenate(dyz, axis=1)
        return dyz * z * sg, dyz * y * sg * (1.0 + z * (1.0 - sg)), _colsum(db * jnp.concatenate(nh, axis=1))

    return _rowwise("ssd_post_bwd", fn, [db, y, z], [norm_g], [(cfg.INNER, F32), (cfg.INNER, F32)], [(1, cfg.INNER)],
                    _pick(cfg.S, 256, 8))


def _rms_pre(cfg, x, g):
    return _rowwise("rms_pre", lambda x, g: x * _rs(x) * g, [x], [g], [(cfg.D, BF16)], [], _pick(cfg.S, 256, 8))[0]


def _local_grads(cfg, x, tgt, W, sp, mla_weights=None, out_weight=None, ffn_weights=None, down_weight=None,
                 ffn_grads_ready=None, early_grads_ready=None, in_grad_ready=None, xn=None, after_in=None):
    S, D, H, INNER = cfg.S, cfg.D, cfg.H, cfg.INNER
    ts = _pick(S, 256, 8)
    tc = _CONV_COLS

    if xn is None:
        xn = _rms_pre(cfg, x, sp["mix_pre_g"])
    u = _matmul("mm_in", xn, W["w_in"], "nt", F32, after=after_in)
    c_q, c_kv, kr, z, xbc, dt_raw = [(u, cfg.window(n)) for n in ("c_q", "c_kv", "kr", "z", "xbc", "dt")]

    if mla_weights is not None:
        sp = dict(sp, q_norm_g=sp["q_norm_g"] + mla_weights.pass_on(u)[0, 0])
    cqn = _rowwise("rms_q", lambda x, g: x * _rs(x) * g, [c_q], [sp["q_norm_g"]], [(cfg.QL, BF16)], [], ts)[0]
    ckvn = _rowwise("rms_kv", lambda x, g: x * _rs(x) * g, [c_kv], [sp["kv_norm_g"]], [(cfg.KVL, BF16)], [], ts)[0]
    if mla_weights is not None:
        W = dict(W, **mla_weights.arrived(ckvn))
    q = _matmul("mm_uq", cqn, W["w_uq"], "nn", F32)
    kv = _matmul("mm_ukv", ckvn, W["w_ukv"], "nn", F32)
    cos2, sin2 = _rope_tables(S)
    Qh, Kh, Vh = _mla_pack(cfg, q, kv, kr, cos2, sin2)
    a_out, lse, lse_t = _attn_fwd(cfg, Qh, Kh, Vh)
    if out_weight is not None:
        sp = dict(sp, ssm_conv_b=sp["ssm_conv_b"] + out_weight.pass_on(a_out)[0, 0])

    pad = lambda v: jnp.pad(v, ((0, 0), (0, LANE - v.shape[1])))
    expand = _expand_matrix(cfg)
    dt_bias_pad, a_log_pad = pad(sp["dt_bias"]), pad(sp["a_log"])
    dskip_exp = jnp.repeat(sp["d_skip"], HP, axis=1)
    xc = _colwise("ssm_act", _ssm_act, [xbc], [sp["ssm_conv_w"], sp["ssm_conv_b"]], [F32], [], tc)[0]
    dt_s, a_s, dt_exp, a_exp = _ssd_prep(cfg, dt_raw, dt_bias_pad, a_log_pad, expand)
    y_ssd, hin = _ssd_fwd(cfg, xc, dt_exp, a_exp, a_s, dskip_exp)
    b_out = _ssd_post(cfg, y_ssd, z, sp["ssm_norm_g"])

    ab_out = jnp.concatenate([a_out.astype(BF16), b_out], axis=1)
    if out_weight is not None:
        W = dict(W, **out_weight.arrived(ab_out))
    if ffn_weights is not None:
        sp = dict(sp, mix_post_g=sp["mix_post_g"] + ffn_weights.pass_on(ab_out)[0, 0])
    mix = _matmul("mm_out", ab_out, W["w_out"], "nn", F32)

    def mid(x, mix, g_mp, g_fp):
        x1 = x + mix * _rs(mix) * g_mp
        return x1, x1 * _rs(x1) * g_fp

    x1, h2 = _rowwise("fwd_mid", mid, [x, mix], [sp["mix_post_g"], sp["ffn_pre_g"]], [(D, F32), (D, BF16)], [], ts)
    if ffn_weights is not None:
        W = dict(W, **ffn_weights.arrived(h2))
    gate_pre = _matmul("mm_gate", h2, W["w_gate"], "nn", F32, chips=True)
    if down_weight is not None:
        sp = dict(sp, ffn_conv_b=sp["ffn_conv_b"] + down_weight.pass_on(gate_pre)[0, 0])
    up = _matmul("mm_up", h2, W["w_up"], "nn", F32, chips=True)
    act = _colwise("ffn_act", _ffn_act, [gate_pre, up], [sp["ffn_conv_w"], sp["ffn_conv_b"]], [BF16], [], tc)[0]
    if down_weight is not None:
        W = dict(W, **down_weight.arrived(act))
    f = _matmul("mm_down", act, W["w_down"], "nn", F32)

    def final(x1, f, t, g):
        r = _rs(f)
        fh = f * r
        err = x1 + fh * g - t
        loss = 0.5 * jnp.sum(jnp.mean(err * err, axis=-1, keepdims=True), axis=0, keepdims=True)
        dy = err * (1.0 / D)
        return dy, _rms_back(fh, r, dy * g), _colsum(dy * fh), loss

    dy, df, g_ffn_post, loss = _rowwise("final", final, [x1, f, tgt], [sp["ffn_post_g"]], [(D, F32), (D, BF16)],
                                        [(1, D), (1, LANE)], ts)
    gW = {}
    dact = _matmul("mm_down_dx", df, W["w_down"], "nt", F32)
    gW["w_down"] = _matmul("mm_down_dw", act, df, "tn", BF16)
    dgate, dup, g_ffn_conv_w, g_ffn_conv_b = _colwise(
        "ffn_act_bwd", _ffn_act_back, [dact, gate_pre, up], [sp["ffn_conv_w"], sp["ffn_conv_b"]], [BF16, BF16], [FFN_K, 1], tc)
    gW["w_gate"] = _matmul("mm_gate_dw", h2, dgate, "tn", BF16, chips=True)
    gW["w_up"] = _matmul("mm_up_dw", h2, dup, "tn", BF16, chips=True)
    if ffn_grads_ready is not None:
        sp = dict(sp, ffn_pre_g=sp["ffn_pre_g"] + ffn_grads_ready({n: gW[n] for n in ("w_down", "w_gate", "w_up")})[0, 0])
    dh2 = _matmul("mm_gu_dx", dgate, W["w_gate"], "nt", F32, dup, W["w_up"], chips=True)

    def mid_back(dy, dh2, x1, mix, g_mp, g_fp):
        r2 = _rs(x1)
        xh = x1 * r2
        dx1 = dy + _rms_back(xh, r2, dh2 * g_fp)
        r1 = _rs(mix)
        mh = mix * r1
        return dx1, _rms_back(mh, r1, dx1 * g_mp), _colsum(dh2 * xh), _colsum(dx1 * mh)

    dx1, dmix, g_ffn_pre, g_mix_post = _rowwise("bwd_mid", mid_back, [dy, dh2, x1, mix], [sp["mix_post_g"], sp["ffn_pre_g"]],
                                                [(D, F32), (D, BF16)], [(1, D), (1, D)], ts)
    dab_out = _matmul("mm_out_dx", dmix, W["w_out"], "nt", F32)
    db_out = (dab_out, (INNER, cfg.MLAW // INNER))
    gW["w_out"] = _matmul("mm_out_dw", ab_out, dmix, "tn", BF16)
    early_token = jnp.zeros((8, LANE), F32)
    if early_grads_ready is not None:
        early_token = early_grads_ready({n: gW[n] for n in ("w_down", "w_gate", "w_up", "w_out")})
        sp = dict(sp, ssm_norm_g=sp["ssm_norm_g"] + early_token[0, 0])

    dy_ssd, dz, g_ssm_norm = _ssd_post_bwd(cfg, db_out, y_ssd, z, sp["ssm_norm_g"])
    dxc, ddt_raw, g_dt_bias, g_a_log, g_d_skip = _ssd_bwd(cfg, dy_ssd, xc, dt_exp, a_exp, a_s, dskip_exp, hin, dt_raw,
                                                          dt_bias_pad, a_log_pad, expand)
    dxbc, g_ssm_conv_w, g_ssm_conv_b = _colwise("ssm_act_bwd", _ssm_act_back, [dxc, xbc], [sp["ssm_conv_w"], sp["ssm_conv_b"]],
                                                [BF16], [SSM_K, 1], tc)

    dQ, delta_t = _attn_dq(cfg, Qh, Kh, Vh, dab_out, a_out, lse, early_token)
    dK, dV = _attn_dkv(cfg, Qh, Kh, Vh, dab_out, lse_t, delta_t)
    dq, dkv, dkr = _mla_unpack(cfg, dQ, dK, dV, cos2, sin2)
    dcqn = _matmul("mm_uq_dx", dq, W["w_uq"], "nt", F32)
    dckvn = _matmul("mm_ukv_dx", dkv, W["w_ukv"], "nt", F32)
    gW["w_uq"] = _matmul("mm_uq_dw", cqn, dq, "tn", BF16)
    gW["w_ukv"] = _matmul("mm_ukv_dw", ckvn, dkv, "tn", BF16)

    def rms_back(x, dy, g):
        r = _rs(x)
        xh = x * r
        return _rms_back(xh, r, dy * g), _colsum(dy * xh)

    dc_q, g_q_norm = _rowwise("rms_q_bwd", rms_back, [c_q, dcqn], [sp["q_norm_g"]], [(cfg.QL, BF16)], [(1, cfg.QL)], ts)
    dc_kv, g_kv_norm = _rowwise("rms_kv_bwd", rms_back, [c_kv, dckvn], [sp["kv_norm_g"]], [(cfg.KVL, BF16)], [(1, cfg.KVL)], ts)

    du = dict(c_q=dc_q, c_kv=dc_kv, kr=dkr, z=dz.astype(BF16), xbc=dxbc, dt=ddt_raw.astype(BF16))
    du = jnp.concatenate([du[n] for n in sorted(du, key=lambda n: cfg.seg[n][0])], axis=1)
    assert du.shape[1] == cfg.EXT, "the layout of u has gaps"
    gW["w_in"] = _matmul("mm_in_dw", du, xn, "tn", BF16)
    if in_grad_ready is not None:
        token = in_grad_ready({n: gW[n] for n in ("w_in", "w_uq", "w_ukv")})
        sp = dict(sp, mix_pre_g=sp["mix_pre_g"] + token[0, 0])
    dxn = _matmul("mm_in_dx", du, W["w_in"], "nn", F32)

    def first_back(dx1, dxn, x, g):
        r = _rs(x)
        xh = x * r
        return dx1 + _rms_back(xh, r, dxn * g), _colsum(dxn * xh)

    grad_x, g_mix_pre = _rowwise("bwd_first", first_back, [dx1, dxn, x], [sp["mix_pre_g"]], [(D, F32)], [(1, D)], ts)

    gs = dict(mix_pre_g=g_mix_pre, q_norm_g=g_q_norm, kv_norm_g=g_kv_norm, ssm_conv_w=g_ssm_conv_w, ssm_conv_b=g_ssm_conv_b,
              dt_bias=g_dt_bias[:, :cfg.HS], a_log=g_a_log[:, :cfg.HS], d_skip=g_d_skip[:, :cfg.HS], ssm_norm_g=g_ssm_norm,
              mix_post_g=g_mix_post, ffn_pre_g=g_ffn_pre, ffn_conv_w=g_ffn_conv_w, ffn_conv_b=g_ffn_conv_b,
              ffn_post_g=g_ffn_post)
    return loss, grad_x, gW, gs


def _to_kernel_layout(cfg, name, w):
    if name == "w_in":
        parts, at = [], 0
        for off, width, n_off, n_width in sorted(cfg.seg.values()):
            parts += [jnp.zeros((off - at, w.shape[1]), w.dtype), w[n_off:n_off + n_width],
                      jnp.zeros((width - n_width, w.shape[1]), w.dtype)]
            at = off + width
        parts.append(jnp.zeros((cfg.EXT - at, w.shape[1]), w.dtype))
        return jnp.concatenate([p for p in parts if p.shape[0]], axis=0)
    if name in ("w_uq", "w_ukv"):
        per = NOPE + (ROPE if name == "w_uq" else VH)
        return jnp.concatenate([w[:, h * per:h * per + NOPE] for h in range(cfg.H)]
                               + [w[:, h * per + NOPE:(h + 1) * per] for h in range(cfg.H)], axis=1)
    return w


def _from_kernel_layout(cfg, name, g):
    if name == "w_in":
        return jnp.concatenate([g[off:off + n_width] for off, _, _, n_width in sorted(cfg.seg.values(), key=lambda s: s[2])], axis=0)
    if name in ("w_uq", "w_ukv"):
        second = ROPE if name == "w_uq" else VH
        base = cfg.H * NOPE
        parts = []
        for h in range(cfg.H):
            parts += [g[:, h * NOPE:(h + 1) * NOPE], g[:, base + h * second:base + (h + 1) * second]]
        return jnp.concatenate(parts, axis=1)
    return g


_CHIP_MAJOR = ("w_gate", "w_up")
_RELAYOUT = ("w_uq", "w_ukv")
_LAYOUT_ROWS = 256
_CONV_COLS = 256


def _w_in_layout(cfg, wg):
    _, rs, d = wg.shape
    tc = _pick(d, _LAYOUT_ROWS, LANE)

    def body(w_ref, o_ref):
        o_ref[...] = _to_kernel_layout(cfg, "w_in", jnp.concatenate([w_ref[k] for k in range(N_CHIPS)], axis=0))

    return pl.pallas_call(
        body, name="layout_w_in", grid=(d // tc,),
        in_specs=[pl.BlockSpec((N_CHIPS, rs, tc), lambda j: (0, 0, j))], out_specs=pl.BlockSpec((cfg.EXT, tc), lambda j: (0, j)),
        out_shape=jax.ShapeDtypeStruct((cfg.EXT, d), wg.dtype), compiler_params=_params(("parallel",)),
    )(wg)


def _w_in_grad_to_chips(cfg, g):
    _, d = g.shape
    rs = cfg.IN_COLS // N_CHIPS
    tc = _pick(d, _LAYOUT_ROWS, LANE)

    def body(g_ref, o_ref):
        nat = _from_kernel_layout(cfg, "w_in", g_ref[...])
        for k in range(N_CHIPS):
            o_ref[k] = nat[k * rs:(k + 1) * rs]

    return pl.pallas_call(
        body, name="layout_grad_w_in", grid=(d // tc,),
        in_specs=[pl.BlockSpec((cfg.EXT, tc), lambda j: (0, j))], out_specs=pl.BlockSpec((N_CHIPS, rs, tc), lambda j: (0, 0, j)),
        out_shape=jax.ShapeDtypeStruct((N_CHIPS, rs, d), g.dtype), compiler_params=_params(("parallel",)),
    )(g)


def _gathered_to_kernel(cfg, name, wg):
    if name in _CHIP_MAJOR:
        return wg
    if name == "w_in":
        return _w_in_layout(cfg, wg)
    if name not in _RELAYOUT:
        return wg.reshape(wg.shape[0] * wg.shape[1], wg.shape[2])
    _, rows, cs = wg.shape
    tr = _pick(rows, _LAYOUT_ROWS, 16)

    def body(w_ref, o_ref):
        o_ref[...] = _to_kernel_layout(cfg, name, jnp.concatenate([w_ref[k] for k in range(N_CHIPS)], axis=1))

    wide = jax.eval_shape(lambda w: _to_kernel_layout(cfg, name, w), jax.ShapeDtypeStruct((rows, N_CHIPS * cs), wg.dtype)).shape[1]
    return pl.pallas_call(
        body, name="layout_" + name, grid=(rows // tr,),
        in_specs=[pl.BlockSpec((N_CHIPS, tr, cs), lambda i: (0, i, 0))], out_specs=pl.BlockSpec((tr, wide), lambda i: (i, 0)),
        out_shape=jax.ShapeDtypeStruct((rows, wide), wg.dtype), compiler_params=_params(("parallel",)),
    )(wg)


def _grad_to_chips(cfg, name, g):
    if name in _CHIP_MAJOR:
        return g
    if name == "w_in":
        return _w_in_grad_to_chips(cfg, g)
    if name not in _RELAYOUT:
        return g.reshape(N_CHIPS, g.shape[0] // N_CHIPS, g.shape[1])
    rows, wide = g.shape
    tr = _pick(rows, _LAYOUT_ROWS, 16)
    cs = jax.eval_shape(lambda v: _from_kernel_layout(cfg, name, v), g).shape[1] // N_CHIPS

    def body(g_ref, o_ref):
        nat = _from_kernel_layout(cfg, name, g_ref[...])
        for k in range(N_CHIPS):
            o_ref[k] = nat[:, k * cs:(k + 1) * cs]

    return pl.pallas_call(
        body, name="layout_grad_" + name, grid=(rows // tr,),
        in_specs=[pl.BlockSpec((tr, wide), lambda i: (i, 0))], out_specs=pl.BlockSpec((N_CHIPS, tr, cs), lambda i: (0, i, 0)),
        out_shape=jax.ShapeDtypeStruct((N_CHIPS, rows, cs), g.dtype), compiler_params=_params(("parallel",)),
    )(g)


def _me():
    return lax.axis_index("x"), lax.axis_index("y"), lax.axis_index("c")


def _other_chips(x, y):
    return [(1 - x, y), (x, 1 - y), (1 - x, 1 - y)]


_ANY = pl.BlockSpec(memory_space=pl.ANY)


BLOCK_ELEMS = 1 << 19
BLOCK_ELEMS_FEW = 1 << 20


def _row_block(rows, cols, mult, elems=BLOCK_ELEMS):
    return _pick(rows, max(mult, elems // cols // mult * mult), mult)


def _scalar(v):
    return v.astype(I32).reshape(1)


def _blocks2d(r, c, mult, elems=BLOCK_ELEMS):
    if r % mult == 0:
        tr = _row_block(r, c, mult, elems)
        return (tr, c), r // tr, lambda i: (i, 0)
    tc = _pick(c, max(LANE, elems // r // LANE * LANE), LANE)
    return (r, tc), c // tc, lambda i: (0, i)


def _by_rows(rows):
    return rows % 32 == 0


def _half_shape(rows, cols):
    return (rows // 2, cols) if _by_rows(rows) else (rows, cols // 2)


def _half_blocks(rows, cols, mult, elems=BLOCK_ELEMS):
    hr, hc = _half_shape(rows, cols)
    block, n, part = _blocks2d(hr, hc, mult, elems)
    assert (hr % mult == 0) == _by_rows(rows), (rows, cols, mult)
    full = (lambda h, i: (h * n + i, 0)) if _by_rows(rows) else (lambda h, i: (0, h * n + i))
    return block, n, full, part


def _half(ref, k, half):
    hr, hc = _half_shape(ref.shape[1], ref.shape[2])
    if _by_rows(ref.shape[1]):
        return ref.at[k, pl.ds(pl.multiple_of(half * hr, 16), hr), :]
    return ref.at[k, :, pl.ds(pl.multiple_of(half * hc, LANE), hc)]


def _shard_blocks(w, br, bc):
    if w.shape[0] == 1:
        def write(ref, v):
            ref[...] = v
        return (lambda f: pl.BlockSpec((None, br, bc), lambda *a: (0, *f(*a)))), (lambda ref: ref[...]), write
    assert w.shape[1] == 1 and br == w.shape[0], w.shape

    def write_rows(ref, v):
        ref[:, 0, :] = v
    return (lambda f: pl.BlockSpec((br, 1, bc), lambda *a: (0, 0, f(*a)[1]))), (lambda ref: ref[:, 0, :]), write_rows


def _stage_shard(name, w, chip, after=None):
    rs, cs = w.shape[0] * w.shape[1], w.shape[2]
    (br, bc), n, idx = _blocks2d(rs, cs, 16, BLOCK_ELEMS_FEW)
    spec, get, _ = _shard_blocks(w, br, bc)

    def body(chip_ref, w_ref, *refs):
        refs[-1][...] = get(w_ref).astype(BF16)

    return pl.pallas_call(
        body, name="stage_" + name,
        grid_spec=pltpu.PrefetchScalarGridSpec(
            num_scalar_prefetch=1, grid=(n,),
            in_specs=[spec(lambda i, chip_ref: idx(i))] + ([] if after is None else [_ANY]),
            out_specs=pl.BlockSpec((None, br, bc), lambda i, chip_ref: (chip_ref[0], *idx(i)))),
        out_shape=jax.ShapeDtypeStruct((N_CHIPS, rs, cs), BF16),
        compiler_params=_params(("parallel",)),
    )(_scalar(chip), w, *([] if after is None else [after]))


_HBM = pl.BlockSpec(memory_space=pltpu.HBM)
_SEM = pl.BlockSpec(memory_space=pltpu.SEMAPHORE)
_EFFECT = pltpu.SideEffectType.DATAFLOW_SIDE_EFFECTING


def _split_start(name, bufs, n_copies, copies, after):
    n = len(bufs)

    def body(*refs):
        for cp in copies(refs[:n], refs[n + 1], refs[n + 2]):
            cp.start()
        refs[-1][...] = jnp.zeros_like(refs[-1])

    res = pl.pallas_call(
        body, name=name,
        out_shape=(pltpu.SemaphoreType.DMA((n_copies,)), pltpu.SemaphoreType.DMA((n_copies,)),
                   *[pltpu.HBM(b.shape, b.dtype) for b in bufs], jax.ShapeDtypeStruct((8, LANE), F32)),
        in_specs=[_HBM] * n + [_ANY], out_specs=(_SEM, _SEM, *[_HBM] * n, pl.BlockSpec(memory_space=pltpu.VMEM)),
        input_output_aliases={i: 2 + i for i in range(n)},
        compiler_params=pltpu.CompilerParams(has_side_effects=_EFFECT),
    )(*[pltpu.with_memory_space_constraint(b, pltpu.HBM) for b in bufs], after)
    return res[0], res[1], list(res[2:2 + n]), res[-1]


def _split_wait(name, send_sems, recv_sems, bufs, after, copies):
    n = len(bufs)

    def body(*refs):
        for cp in copies(refs[:n], refs[n], refs[n + 1]):
            cp.wait_send()
            cp.wait_recv()

    return list(pl.pallas_call(
        body, name=name, out_shape=[pltpu.HBM(b.shape, b.dtype) for b in bufs],
        in_specs=[_HBM] * n + [_SEM, _SEM, _ANY], out_specs=[_HBM] * n,
        input_output_aliases={i: i for i in range(n)},
        compiler_params=pltpu.CompilerParams(has_side_effects=_EFFECT),
    )(*bufs, send_sems, recv_sems, after))


def _gather_to_chips(bufs, send_sems, recv_sems):
    x, y, c = _me()
    return [pltpu.make_async_remote_copy(src_ref=_half(b, 2 * x + y, c), dst_ref=_half(b, 2 * x + y, c),
                                         send_sem=send_sems.at[3 * w + j], recv_sem=recv_sems.at[3 * w + j],
                                         device_id=(cx, cy, c), device_id_type=MESH_ID)
            for w, b in enumerate(bufs) for j, (cx, cy) in enumerate(_other_chips(x, y))]


def _gather_to_sibling(bufs, send_sems, recv_sems):
    x, y, c = _me()
    return [pltpu.make_async_remote_copy(src_ref=_half(b, 2 * cx + cy, c), dst_ref=_half(b, 2 * cx + cy, c),
                                         send_sem=send_sems.at[3 * w + j], recv_sem=recv_sems.at[3 * w + j],
                                         device_id=(x, y, 1 - c), device_id_type=MESH_ID)
            for w, b in enumerate(bufs) for j, (cx, cy) in enumerate(_other_chips(x, y))]


def _pair_exchange(name, grads):
    n = len(grads)

    def body(*refs):
        ins, outs, send_sems, recv_sems = refs[:n], refs[n:2 * n], refs[2 * n], refs[2 * n + 1]
        x, y, c = _me()
        cps = []
        for w, (g_ref, o_ref) in enumerate(zip(ins, outs)):
            cps.append(pltpu.make_async_remote_copy(src_ref=_half(g_ref, slice(None), 1 - c), dst_ref=o_ref,
                                                    send_sem=send_sems.at[w], recv_sem=recv_sems.at[w],
                                                    device_id=(x, y, 1 - c), device_id_type=MESH_ID))
            cps[-1].start()
        for cp in cps:
            cp.wait()

    return pl.pallas_call(
        body, name="pair_exchange_" + name, in_specs=[_ANY] * n, out_specs=[_ANY] * n,
        out_shape=[jax.ShapeDtypeStruct((g.shape[0], *_half_shape(g.shape[1], g.shape[2])), g.dtype) for g in grads],
        scratch_shapes=[pltpu.SemaphoreType.DMA((n,)), pltpu.SemaphoreType.DMA((n,))],
    )(*grads)


def _pair_copies(grads, lands, send_sems, recv_sems):
    x, y, c = _me()
    return [pltpu.make_async_remote_copy(src_ref=_half(g_ref, slice(None), 1 - c), dst_ref=l_ref, send_sem=send_sems.at[w],
                                         recv_sem=recv_sems.at[w], device_id=(x, y, 1 - c), device_id_type=MESH_ID)
            for w, (g_ref, l_ref) in enumerate(zip(grads, lands))]


def _pair_exchange_start(name, grads):
    n = len(grads)
    lands = [lax.empty((g.shape[0], *_half_shape(g.shape[1], g.shape[2])), g.dtype) for g in grads]
    send_sems, recv_sems, bufs, token = _split_start(
        "pair_exchange_start_" + name, [*grads, *lands], n, lambda refs, ss, rs: _pair_copies(refs[:n], refs[n:], ss, rs),
        jnp.zeros((8, LANE), F32))
    return (send_sems, recv_sems, bufs), token


def _pair_exchange_wait(name, state, after):
    send_sems, recv_sems, bufs = state
    n = len(bufs) // 2
    bufs = _split_wait("pair_exchange_wait_" + name, send_sems, recv_sems, bufs, after,
                       lambda refs, ss, rs: _pair_copies(refs[:n], refs[n:], ss, rs))
    return bufs[:n], bufs[n:]


def _pair_sum(name, g, theirs, c):
    (br, bc), nb, full, part = _half_blocks(g.shape[1], g.shape[2], 16, 2 * BLOCK_ELEMS_FEW)

    def body(c_ref, a_ref, b_ref, o_ref):
        o_ref[...] = (a_ref[...].astype(F32) + b_ref[...].astype(F32)).astype(o_ref.dtype)

    return pl.pallas_call(
        body, name="pair_sum_" + name,
        grid_spec=pltpu.PrefetchScalarGridSpec(
            num_scalar_prefetch=1, grid=(N_CHIPS, nb),
            in_specs=[pl.BlockSpec((None, br, bc), lambda k, i, c_ref: (k, *full(c_ref[0], i))),
                      pl.BlockSpec((None, br, bc), lambda k, i, c_ref: (k, *part(i)))],
            out_specs=pl.BlockSpec((None, br, bc), lambda k, i, c_ref: (k, *part(i)))),
        out_shape=jax.ShapeDtypeStruct(theirs.shape, BF16),
        compiler_params=_params(("parallel", "parallel")),
    )(_scalar(c), g, theirs)


def _chip_copies(srcs, lands, send_sems, recv_sems):
    x, y, c = _me()
    return [pltpu.make_async_remote_copy(src_ref=s_ref.at[2 * cx + cy], dst_ref=l_ref.at[j], send_sem=send_sems.at[3 * w + j],
                                         recv_sem=recv_sems.at[3 * w + j], device_id=(cx, cy, c), device_id_type=MESH_ID)
            for w, (s_ref, l_ref) in enumerate(zip(srcs, lands)) for j, (cx, cy) in enumerate(_other_chips(x, y))]


def _chip_exchange_start(name, sums):
    n = len(sums)
    lands = [lax.empty((3,) + s.shape[1:], s.dtype) for s in sums]
    send_sems, recv_sems, bufs, token = _split_start(
        "chip_exchange_start_" + name, [*sums, *lands], 3 * n, lambda refs, ss, rs: _chip_copies(refs[:n], refs[n:], ss, rs),
        jnp.zeros((8, LANE), F32))
    return send_sems, recv_sems, bufs[:n], bufs[n:], token


def _chip_exchange_wait(name, send_sems, recv_sems, sums, lands, after):
    n = len(sums)
    bufs = _split_wait("chip_exchange_wait_" + name, send_sems, recv_sems, [*sums, *lands], after,
                       lambda refs, ss, rs: _chip_copies(refs[:n], refs[n:], ss, rs))
    return bufs[:n], bufs[n:]


def _chip_sum(name, sums, theirs, chip):
    _, h, cs = sums.shape
    (br, bc), nb, idx = _blocks2d(h, cs, 16, BLOCK_ELEMS_FEW)

    def body(chip_ref, s_ref, t_ref, o_ref):
        acc = s_ref[...].astype(F32)
        for k in range(3):
            acc = acc + t_ref[k].astype(F32)
        o_ref[...] = acc

    return pl.pallas_call(
        body, name="chip_sum_" + name,
        grid_spec=pltpu.PrefetchScalarGridSpec(
            num_scalar_prefetch=1, grid=(nb,),
            in_specs=[pl.BlockSpec((None, br, bc), lambda i, chip_ref: (chip_ref[0], *idx(i))),
                      pl.BlockSpec((3, br, bc), lambda i, chip_ref: (0, *idx(i)))],
            out_specs=pl.BlockSpec((br, bc), lambda i, chip_ref: idx(i))),
        out_shape=jax.ShapeDtypeStruct((h, cs), F32),
        compiler_params=_params(("parallel",)),
    )(_scalar(chip), sums, theirs)


def _sibling_copies(halves, lands, send_sems, recv_sems):
    x, y, c = _me()
    return [pltpu.make_async_remote_copy(src_ref=h_ref, dst_ref=l_ref, send_sem=send_sems.at[w], recv_sem=recv_sems.at[w],
                                         device_id=(x, y, 1 - c), device_id_type=MESH_ID)
            for w, (h_ref, l_ref) in enumerate(zip(halves, lands))]


def _sibling_exchange_start(name, halves, after):
    n = len(halves)
    lands = [lax.empty(h.shape, h.dtype) for h in halves]
    send_sems, recv_sems, bufs, token = _split_start(
        "sibling_exchange_start_" + name, [*halves, *lands], n, lambda refs, ss, rs: _sibling_copies(refs[:n], refs[n:], ss, rs),
        after)
    return (send_sems, recv_sems, bufs), token


def _sibling_exchange_wait(name, state, after):
    send_sems, recv_sems, bufs = state
    n = len(bufs) // 2
    bufs = _split_wait("sibling_exchange_wait_" + name, send_sems, recv_sems, bufs, after,
                       lambda refs, ss, rs: _sibling_copies(refs[:n], refs[n:], ss, rs))
    return bufs[:n], bufs[n:]


def _sibling_exchange(name, halves):
    n = len(halves)

    def body(*refs):
        ins, outs, send_sems, recv_sems = refs[:n], refs[n:2 * n], refs[2 * n], refs[2 * n + 1]
        x, y, c = _me()
        cps = []
        for w, (h_ref, o_ref) in enumerate(zip(ins, outs)):
            cps.append(pltpu.make_async_remote_copy(src_ref=h_ref, dst_ref=o_ref, send_sem=send_sems.at[w], recv_sem=recv_sems.at[w],
                                                    device_id=(x, y, 1 - c), device_id_type=MESH_ID))
            cps[-1].start()
        for cp in cps:
            cp.wait()

    return pl.pallas_call(
        body, name="sibling_exchange_" + name, in_specs=[_ANY] * n, out_specs=[_ANY] * n,
        out_shape=[jax.ShapeDtypeStruct(h.shape, h.dtype) for h in halves],
        scratch_shapes=[pltpu.SemaphoreType.DMA((n,)), pltpu.SemaphoreType.DMA((n,))],
    )(*halves)


N_DEV = 8


def _peer_copies(bufs, send_sems, recv_sems):
    vec, land = bufs
    x, y, c = _me()
    return [pltpu.make_async_remote_copy(src_ref=vec, dst_ref=land.at[4 * x + 2 * y + c], send_sem=send_sems.at[p - 1],
                                         recv_sem=recv_sems.at[p - 1], device_id=(x ^ (p >> 2), y ^ ((p >> 1) & 1), c ^ (p & 1)),
                                         device_id_type=MESH_ID) for p in range(1, N_DEV)]


def _allreduce_small_start(vec, after):
    land = jnp.zeros((N_DEV,) + vec.shape, F32)
    send_sems, recv_sems, bufs, _ = _split_start("allreduce_small_start", [vec, land], N_DEV - 1, _peer_copies, after)
    return send_sems, recv_sems, bufs


def _allreduce_small_wait(state, chip, core, after):
    send_sems, recv_sems, bufs = state
    vec, land = _split_wait("allreduce_small_wait", send_sems, recv_sems, bufs, after, _peer_copies)

    def body(me_ref, v_ref, l_ref, o_ref):
        acc = None
        for k in range(N_DEV):
            term = jnp.where(me_ref[0] == k, v_ref[...], l_ref[k])
            acc = term if acc is None else acc + term
        o_ref[...] = acc

    return pl.pallas_call(
        body, name="allreduce_small_sum",
        grid_spec=pltpu.PrefetchScalarGridSpec(
            num_scalar_prefetch=1, grid=(1,),
            in_specs=[pl.BlockSpec(vec.shape, lambda i, me_ref: (0, 0)), pl.BlockSpec(land.shape, lambda i, me_ref: (0, 0, 0))],
            out_specs=pl.BlockSpec(vec.shape, lambda i, me_ref: (0, 0))),
        out_shape=jax.ShapeDtypeStruct(vec.shape, F32), compiler_params=_params(("arbitrary",)),
    )(_scalar(2 * chip + core), vec, land)


def _adam_math(w, g, m, v):
    m = ADAM_B1 * m + (1.0 - ADAM_B1) * g
    v = ADAM_B2 * v + (1.0 - ADAM_B2) * (g * g)
    m_hat = m / (1.0 - ADAM_B1 ** ADAM_STEP)
    v_hat = v / (1.0 - ADAM_B2 ** ADAM_STEP)
    return -ADAM_LR * (m_hat / (jnp.sqrt(v_hat) + ADAM_EPS) + ADAM_WD * w), m, v


def _adamw(name, w, g, m, v):
    R, C = w.shape
    tr = _row_block(R, C, 8)

    def body(w_ref, g_ref, m_ref, v_ref, d_ref, nm_ref, nv_ref):
        d_ref[...], nm_ref[...], nv_ref[...] = _adam_math(w_ref[...], g_ref[...], m_ref[...], v_ref[...])

    blk = pl.BlockSpec((tr, C), lambda i: (i, 0))
    return pl.pallas_call(
        body, name=name, grid=(R // tr,), in_specs=[blk] * 4, out_specs=[blk] * 3,
        out_shape=[jax.ShapeDtypeStruct((R, C), F32)] * 3, compiler_params=_params(("parallel",)),
    )(w, g, m, v)


def _adamw_halves(name, w, mine, theirs, m, v, c):
    rs, cs = w.shape[0] * w.shape[1], w.shape[2]
    (br, bc), nb, whole, half = _half_blocks(rs, cs, 8)
    spec, get, put = _shard_blocks(w, br, bc)

    def body(c_ref, w_ref, a_ref, b_ref, m_ref, v_ref, g_ref, d_ref, nm_ref, nv_ref):
        g = jnp.where(pl.program_id(0) == c_ref[0], a_ref[...], b_ref[...])
        put(g_ref, g)
        for ref, val in zip((d_ref, nm_ref, nv_ref), _adam_math(get(w_ref), g, get(m_ref), get(v_ref))):
            put(ref, val)

    full = spec(lambda s, i, c_ref: whole(s, i))
    part = pl.BlockSpec((br, bc), lambda s, i, c_ref: half(i))
    return pl.pallas_call(
        body, name=name,
        grid_spec=pltpu.PrefetchScalarGridSpec(num_scalar_prefetch=1, grid=(2, nb), in_specs=[full, part, part, full, full],
                                               out_specs=[full] * 4),
        out_shape=[jax.ShapeDtypeStruct(w.shape, F32)] * 4, compiler_params=_params(("parallel", "parallel")),
    )(_scalar(c), w, mine, theirs, m, v)


def _pack_small(arrs, lanes=LANE):
    flat = jnp.concatenate([a.reshape(-1) for a in arrs])
    n = -(-flat.shape[0] // (8 * lanes)) * 8 * lanes
    return jnp.pad(flat, (0, n - flat.shape[0])).reshape(8, n // 8)


def _unpack_small(vec, shapes):
    flat, out, off = vec.reshape(-1), [], 0
    for s in shapes:
        out.append(flat[off:off + s[0] * s[1]].reshape(s))
        off += s[0] * s[1]
    return out


class _LateWeights:
    def __init__(self, cfg, tag, names, staged, after):
        self.cfg, self.tag, self.names, self.k = cfg, tag, names, 3 * len(names)
        self.send, self.recv, self.bufs, self.token = _split_start(f"gather_{tag}_chips_start", staged, self.k, _gather_to_chips,
                                                                    after)

    def pass_on(self, after):
        bufs = _split_wait(f"gather_{self.tag}_chips_wait", self.send, self.recv, self.bufs, after, _gather_to_chips)
        self.send, self.recv, self.bufs, token = _split_start(f"gather_{self.tag}_sibling_start", bufs, self.k, _gather_to_sibling,
                                                               self.token)
        return token

    def arrived(self, after):
        bufs = _split_wait(f"gather_{self.tag}_sibling_wait", self.send, self.recv, self.bufs, after, _gather_to_sibling)
        return {n: _gathered_to_kernel(self.cfg, n, b) for n, b in zip(self.names, bufs)}


def _step(cfg, a):
    chip = 2 * lax.axis_index("x") + lax.axis_index("y")
    core = lax.axis_index("c")
    big = BIG

    ffn = ("w_gate", "w_up", "w_down")
    first = ("w_in", "w_uq", "w_ukv")
    sp = {n: a[n] for n in SMALL}
    sharded = _pack_small([a[n] for n in SMALL_SHARDED], 2 * LANE)
    slabs = jnp.where(lax.broadcasted_iota(I32, (N_CHIPS,) + sharded.shape, 0) == chip, sharded[None], 0.0)
    staged = {"w_in": _stage_shard("w_in", a["w_in"], chip)}
    in_weight = _LateWeights(cfg, "in", ("w_in", "sharded_small"), [staged["w_in"], slabs], jnp.zeros((8, LANE), F32))
    behind = in_weight.token
    for n in big[1:]:
        behind = staged[n] = _stage_shard(n, a[n], chip, behind)
    in_weight.pass_on(behind)
    xn_early = _rms_pre(cfg, a["x"], sp["mix_pre_g"] + in_weight.token[0, 0])
    W = in_weight.arrived(xn_early)
    allp = W.pop("sharded_small").reshape((N_CHIPS,) + sharded.shape)
    per_chip = [_unpack_small(allp[ch], [a[n].shape for n in SMALL_SHARDED]) for ch in range(N_CHIPS)]
    for k, n in enumerate(SMALL_SHARDED):
        sp[n] = jnp.concatenate([per_chip[ch][k] for ch in range(N_CHIPS)], axis=1)

    mla_weights = _LateWeights(cfg, "mla", first[1:], [staged[n] for n in first[1:]], W["w_in"])
    out_weight = _LateWeights(cfg, "out", ("w_out",), [staged["w_out"]], mla_weights.token)
    ffn_weights = _LateWeights(cfg, "ffn", ffn[:2], [staged[n] for n in ffn[:2]], out_weight.token)
    down_weight = _LateWeights(cfg, "down", ffn[2:], [staged[n] for n in ffn[2:]], ffn_weights.token)

    state = {}

    def ffn_grads_ready(grads):
        state["ffn_pairs"], token = _pair_exchange_start("ffn", [_grad_to_chips(cfg, n, grads[n]) for n in ffn_grads])
        return token

    def pair_sums(names, grads, theirs):
        return [_pair_sum(n, g, t, core) for n, g, t in zip(names, grads, theirs)]

    def early_grads_ready(grads):
        g_out = [_grad_to_chips(cfg, "w_out", grads["w_out"])]
        g_ffn, t_ffn = _pair_exchange_wait("ffn", state["ffn_pairs"], g_out[0])
        sums = pair_sums(ffn_grads, g_ffn, t_ffn) + pair_sums(["w_out"], g_out, _pair_exchange("out", g_out))
        state["early"] = _chip_exchange_start("early", sums)
        return state["early"][-1]

    def reduced_halves(tag, names, after):
        send_sems, recv_sems, s_bufs, l_bufs, _ = state[tag]
        s_bufs, l_bufs = _chip_exchange_wait(tag, send_sems, recv_sems, s_bufs, l_bufs, after)
        return [_chip_sum(n, s, t, chip) for n, s, t in zip(names, s_bufs, l_bufs)]

    def in_grad_ready(grads):
        grads = [_grad_to_chips(cfg, n, grads[n]) for n in first]
        state["rest"] = _chip_exchange_start("rest", pair_sums(first, grads, _pair_exchange("rest", grads)))
        return state["rest"][-1]

    ffn_grads = ("w_down", "w_gate", "w_up")
    early = ffn_grads + ("w_out",)
    loss, grad_x, gW, gs = _local_grads(cfg, a["x"], a["loss_target"], W, sp, mla_weights, out_weight, ffn_weights, down_weight,
                                        ffn_grads_ready, early_grads_ready, in_grad_ready, xn_early, down_weight.token)
    out = {"grad_x": grad_x}

    def adamw(names, mine, theirs):
        for n, gm, gt in zip(names, mine, theirs):
            out["grad_" + n], out["delta_" + n], out["new_m_" + n], out["new_v_" + n] = _adamw_halves(
                "adamw_" + n, a[n], gm, gt, a["m_" + n], a["v_" + n], core)

    mine = reduced_halves("early", early, grad_x)
    theirs = _sibling_exchange("early", mine[:1])
    later, _ = _sibling_exchange_start("early", mine[1:], theirs[0])
    adamw(early[:1], mine[:1], theirs)
    e_mine, e_theirs = _sibling_exchange_wait("early", later, out["new_v_" + early[0]])
    adamw(early[3:], e_mine[2:], e_theirs[2:])
    mine = reduced_halves("rest", first, out["new_v_" + early[-1]])
    rest, token = _sibling_exchange_start("rest", mine, mine[0])
    small = _allreduce_small_start(_pack_small([gs[n] for n in SMALL] + [loss]), token)
    adamw(early[1:3], e_mine[:2], e_theirs[:2])
    adamw(first, *_sibling_exchange_wait("rest", rest, out["new_v_" + early[2]]))
    shapes = [gs[n].shape for n in SMALL] + [(1, LANE)]
    red = _unpack_small(_allreduce_small_wait(small, chip, core, out["new_v_" + first[-1]]), shapes)
    g_small = dict(zip(SMALL, red[:-1]))
    for n in SMALL_SHARDED:
        cs = a[n].shape[1]
        g_small[n] = lax.dynamic_slice_in_dim(g_small[n], chip * cs, cs, axis=1)
    out["loss"] = red[-1][0, 0]
    sshapes = [a[n].shape for n in SMALL]
    d, nm, nv = _adamw("adamw_small", _pack_small([a[n] for n in SMALL]), _pack_small([g_small[n] for n in SMALL]),
                       _pack_small([a["m_" + n] for n in SMALL]), _pack_small([a["v_" + n] for n in SMALL]))
    for n, dd, mm, vv in zip(SMALL, _unpack_small(d, sshapes), _unpack_small(nm, sshapes), _unpack_small(nv, sshapes)):
        out["grad_" + n], out["delta_" + n], out["new_m_" + n], out["new_v_" + n] = g_small[n], dd, mm, vv
    return out


def kernel(x, mix_pre_g, w_in, q_norm_g, w_uq, kv_norm_g, w_ukv, ssm_conv_w, ssm_conv_b, dt_bias, a_log, d_skip, ssm_norm_g, w_out, mix_post_g, ffn_pre_g, w_gate, w_up, ffn_conv_w, ffn_conv_b, w_down, ffn_post_g, loss_target, m_mix_pre_g, m_w_in, m_q_norm_g, m_w_uq, m_kv_norm_g, m_w_ukv, m_ssm_conv_w, m_ssm_conv_b, m_dt_bias, m_a_log, m_d_skip, m_ssm_norm_g, m_w_out, m_mix_post_g, m_ffn_pre_g, m_w_gate, m_w_up, m_ffn_conv_w, m_ffn_conv_b, m_w_down, m_ffn_post_g, v_mix_pre_g, v_w_in, v_q_norm_g, v_w_uq, v_kv_norm_g, v_w_ukv, v_ssm_conv_w, v_ssm_conv_b, v_dt_bias, v_a_log, v_d_skip, v_ssm_norm_g, v_w_out, v_mix_post_g, v_ffn_pre_g, v_w_gate, v_w_up, v_ffn_conv_w, v_ffn_conv_b, v_w_down, v_ffn_post_g):
    args = dict(locals())
    def given(k, v):
        if k in ("w_in", "m_w_in", "v_w_in"):
            return jnp.transpose(v, (2, 0, 1))
        return v if k.removeprefix("m_").removeprefix("v_") in BIG or v.ndim < 3 else v[0]

    out = _step(_FULL, {k: given(k, v) for k, v in args.items()})
    res = [out["loss"], out["grad_x"][None]]
    for pre in ("grad_", "delta_", "new_m_", "new_v_"):
        for n in WEIGHTS:
            o = out[pre + n]
            res.append(jnp.transpose(o, (1, 2, 0)) if n == "w_in" else o if n in BIG or args[n].ndim < 3 else o[None])
    return tuple(res)
```

```python
import math

import jax
import jax.numpy as jnp
from jax import lax
from jax.experimental import pallas as pl
from jax.experimental.pallas import tpu as pltpu

F32, BF16, I32 = jnp.float32, jnp.bfloat16, jnp.int32
NN = (((1,), (0,)), ((), ()))
NT = (((1,), (1,)), ((), ()))
TN = (((0,), (0,)), ((), ()))
HI = lax.Precision.HIGHEST
MESH_ID = pl.DeviceIdType.MESH

EPS = 1e-6
CHUNK = 64
NOPE, ROPE, VH = 128, 64, 128
ROPE_THETA = 10000.0
HP, NST = 64, 128
SSM_K, FFN_K = 4, 3
LANE = 128
N_CHIPS = 4
VMEM_LIMIT = 52 * 1024 * 1024
MM_TILE, MM_TILE_K = 1408, 2816

ADAM_LR, ADAM_B1, ADAM_B2, ADAM_EPS, ADAM_WD, ADAM_STEP = 0.001, 0.9, 0.999, 1e-08, 0.01, 10


class _Cfg:
    def __init__(self, S, D, QL, KVL, H, HS, G, DFF, T):
        self.S, self.D, self.QL, self.KVL, self.H, self.HS, self.G, self.DFF, self.T = S, D, QL, KVL, H, HS, G, DFF, T
        self.INNER = HS * HP
        self.CONVCH = self.INNER + 2 * G * NST
        self.QW = H * (NOPE + ROPE)
        self.KVW = H * (NOPE + VH)
        self.MLAW = H * VH
        self.MIXW = self.MLAW + self.INNER
        self.IN_COLS = QL + KVL + ROPE + self.INNER + self.CONVCH + HS
        natural, at = {}, 0
        for name, w in (("c_q", QL), ("c_kv", KVL), ("kr", ROPE), ("z", self.INNER), ("xbc", self.CONVCH), ("dt", HS)):
            natural[name] = (at, w)
            at += w
        self.seg, taken = {}, []
        for name in sorted(natural, key=lambda n: -natural[n][1]):
            w = -(-natural[name][1] // LANE) * LANE
            off = next(o for o in range(0, self.IN_COLS * 2, w) if all(o + w <= t or o >= t + tw for t, tw in taken))
            taken.append((off, w))
            self.seg[name] = (off, w) + natural[name]
        self.EXT = max(o + w for o, w in taken)
        self.NPAIR = HS // 2
        self.REP = HS // G

    def window(self, name):
        off, w, _, _ = self.seg[name]
        return w, off // w


_FULL = _Cfg(S=2048, D=2048, QL=768, KVL=512, H=8, HS=16, G=2, DFF=5632, T=256)
BIG = ("w_in", "w_uq", "w_ukv", "w_out", "w_gate", "w_up", "w_down")

SMALL = ("mix_pre_g", "q_norm_g", "kv_norm_g", "ssm_conv_w", "ssm_conv_b", "dt_bias", "a_log", "d_skip", "ssm_norm_g",
         "mix_post_g", "ffn_pre_g", "ffn_conv_w", "ffn_conv_b", "ffn_post_g")
SMALL_SHARDED = ("ssm_conv_w", "ffn_conv_w")
WEIGHTS = ("mix_pre_g", "w_in", "q_norm_g", "w_uq", "kv_norm_g", "w_ukv", "ssm_conv_w", "ssm_conv_b", "dt_bias", "a_log",
           "d_skip", "ssm_norm_g", "w_out", "mix_post_g", "ffn_pre_g", "w_gate", "w_up", "ffn_conv_w", "ffn_conv_b",
           "w_down", "ffn_post_g")


def _pick(n, target, mult):
    best = None
    for d in range(mult, min(n, target) + 1, mult):
        if n % d == 0:
            best = d
    return best if best is not None else n


def _params(sem=None):
    kw = dict(vmem_limit_bytes=VMEM_LIMIT)
    if sem is not None:
        kw["dimension_semantics"] = sem
    return pltpu.CompilerParams(**kw)


def _dot(a, b, dims=NN, precision=None):
    return lax.dot_general(a, b, dims, preferred_element_type=F32, precision=precision)


def _sigmoid(x):
    return 1.0 / (1.0 + jnp.exp(-x))


def _rs(x):
    return lax.rsqrt(jnp.mean(x * x, axis=-1, keepdims=True) + EPS)


def _rms_back(xh, r, dn):
    return r * (dn - xh * jnp.mean(dn * xh, axis=-1, keepdims=True))


def _colsum(v):
    return jnp.sum(v, axis=0, keepdims=True)


def _matmul(name, a, b, mode, out_dtype, a2=None, b2=None, chips=False, after=None):
    cs = None
    if mode == "nn":
        (M, K), N = a.shape, b.shape[-1]
        if chips:
            cs, N = N, N_CHIPS * N
    elif mode == "nt":
        (M, K), N = a.shape, b.shape[-2]
        if chips:
            cs = b.shape[-1]
    else:
        (K, M), N = a.shape, b.shape[1]
        if chips:
            cs = N // N_CHIPS
    tm = _pick(M, MM_TILE, LANE)
    tn = _pick(cs if chips and mode != "nt" else N, MM_TILE, LANE)
    tk = _pick(cs, MM_TILE, LANE) if chips and mode == "nt" else _pick(K, MM_TILE_K, LANE)
    nk = K // tk
    dims = {"nn": NN, "nt": NT, "tn": TN}[mode]
    a_spec = pl.BlockSpec((tk, tm), lambda i, j, k: (k, i)) if mode == "tn" else pl.BlockSpec((tm, tk), lambda i, j, k: (i, k))
    b_spec = pl.BlockSpec((tn, tk), lambda i, j, k: (j, k)) if mode == "nt" else pl.BlockSpec((tk, tn), lambda i, j, k: (k, j))
    o_spec = pl.BlockSpec((tm, tn), lambda i, j, k: (i, j))
    o_shape = (M, N)
    if chips and mode == "nn":
        per = cs // tn
        b_spec = pl.BlockSpec((None, tk, tn), lambda i, j, k: (j // per, k, j % per))
    elif chips and mode == "nt":
        per = cs // tk
        b_spec = pl.BlockSpec((None, tn, tk), lambda i, j, k: (k // per, j, k % per))
    elif chips:
        per = cs // tn
        o_spec = pl.BlockSpec((None, tm, tn), lambda i, j, k: (j // per, i, j % per))
        o_shape = (N_CHIPS, M, cs)
    two = a2 is not None

    def product(refs):
        part = _dot(refs[0][...].astype(BF16), refs[1][...].astype(BF16), dims)
        if two:
            part += _dot(refs[2][...].astype(BF16), refs[3][...].astype(BF16), dims)
        return part

    def body_whole_k(*refs):
        refs[-1][...] = product(refs).astype(refs[-1].dtype)

    def body(*refs):
        o_ref, acc_ref = refs[-2], refs[-1]
        k = pl.program_id(2)

        @pl.when(k == 0)
        def _():
            acc_ref[...] = product(refs)

        @pl.when(k > 0)
        def _():
            acc_ref[...] += product(refs)

        @pl.when(k == nk - 1)
        def _():
            o_ref[...] = acc_ref[...].astype(o_ref.dtype)

    ins = ((a, b, a2, b2) if two else (a, b)) + (() if after is None else (after,))
    return pl.pallas_call(
        body_whole_k if nk == 1 else body, name=name, grid=(M // tm, N // tn, nk),
        in_specs=[a_spec, b_spec] * (2 if two else 1) + ([] if after is None else [pl.BlockSpec(memory_space=pl.ANY)]),
        out_specs=o_spec,
        out_shape=jax.ShapeDtypeStruct(o_shape, out_dtype),
        scratch_shapes=[] if nk == 1 else [pltpu.VMEM((tm, tn), F32)],
        compiler_params=_params(("parallel", "parallel", "arbitrary")),
    )(*ins)


def _window(a):
    return (a[0], *a[1]) if isinstance(a, tuple) else (a, a.shape[1], 0)


def _rowwise(name, fn, rows, mats, outs, reds, ts):
    rows, widths, blocks = zip(*[_window(a) for a in rows])
    S = rows[0].shape[0]
    nr, nm, no = len(rows), len(mats), len(outs)

    def body(*refs):
        res = fn(*[r[...] for r in refs[:nr + nm]])
        res = res if isinstance(res, (tuple, list)) else (res,)
        for r, v in zip(refs[nr + nm:nr + nm + no], res[:no]):
            r[...] = v.astype(r.dtype)
        first = pl.program_id(0) == 0
        for r, v in zip(refs[nr + nm + no:], res[no:]):
            @pl.when(first)
            def _():
                r[...] = jnp.broadcast_to(v, r.shape)

            @pl.when(jnp.logical_not(first))
            def _():
                r[...] += jnp.broadcast_to(v, r.shape)

    in_specs = [pl.BlockSpec((ts, w), lambda i, b=b: (i, b)) for w, b in zip(widths, blocks)]
    in_specs += [pl.BlockSpec(m.shape, lambda i, nd=m.ndim: (0,) * nd) for m in mats]
    out_specs = [pl.BlockSpec((ts, w), lambda i: (i, 0)) for w, _ in outs]
    out_specs += [pl.BlockSpec(s, lambda i: (0, 0)) for s in reds]
    out_shape = [jax.ShapeDtypeStruct((S, w), dt) for w, dt in outs] + [jax.ShapeDtypeStruct(s, F32) for s in reds]
    return pl.pallas_call(
        body, name=name, grid=(S // ts,), in_specs=in_specs, out_specs=out_specs, out_shape=out_shape,
        compiler_params=_params(("arbitrary",) if reds else ("parallel",)),
    )(*rows, *mats)


def _shift_down(v, s):
    if s == 0:
        return v
    rows = lax.broadcasted_iota(I32, v.shape, 0)
    return jnp.where(rows >= s, pltpu.roll(v, s, 0), 0.0)


def _shift_up(v, s):
    if s == 0:
        return v
    n = v.shape[0]
    rows = lax.broadcasted_iota(I32, v.shape, 0)
    return jnp.where(rows < n - s, pltpu.roll(v, n - s, 0), 0.0)


def _conv(x, w, b):
    K = w.shape[0]
    y = jnp.broadcast_to(b, x.shape)
    for k in range(K):
        y = y + w[k:k + 1, :] * _shift_down(x, K - 1 - k)
    return y


def _conv_back(x, w, dc):
    K = w.shape[0]
    dx = jnp.zeros_like(x)
    dw = []
    for k in range(K):
        up = _shift_up(dc, K - 1 - k)
        dx = dx + w[k:k + 1, :] * up
        dw.append(_colsum(up * x))
    return dx, jnp.concatenate(dw, axis=0), _colsum(dc)


def _colwise(name, fn, cols, vecs, outs, pouts, tc):
    cols, widths, blocks = zip(*[_window(a) for a in cols])
    S, C = cols[0].shape[0], widths[0]
    firsts = [b * (C // tc) for b in blocks]
    nc_, nv, no = len(cols), len(vecs), len(outs)

    def body(*refs):
        res = fn(*[r[...] for r in refs[:nc_ + nv]])
        res = res if isinstance(res, (tuple, list)) else (res,)
        for r, v in zip(refs[nc_ + nv:], res):
            r[...] = v.astype(r.dtype)

    in_specs = [pl.BlockSpec((S, tc), lambda j, f=f: (0, f + j)) for f in firsts]
    in_specs += [pl.BlockSpec((v.shape[0], tc), lambda j: (0, j)) for v in vecs]
    out_specs = [pl.BlockSpec((S, tc), lambda j: (0, j)) for _ in outs] + [pl.BlockSpec((k, tc), lambda j: (0, j)) for k in pouts]
    out_shape = [jax.ShapeDtypeStruct((S, C), dt) for dt in outs] + [jax.ShapeDtypeStruct((k, C), F32) for k in pouts]
    return pl.pallas_call(
        body, name=name, grid=(C // tc,), in_specs=in_specs, out_specs=out_specs, out_shape=out_shape,
        compiler_params=_params(("parallel",)),
    )(*cols, *vecs)


_G0, _G1 = math.sqrt(2.0 / math.pi), 0.044715


def _gelu(g):
    th = jnp.tanh(_G0 * (g + _G1 * g * g * g))
    return 0.5 * g * (1.0 + th), th


def _ffn_act(gate_pre, up, w, b):
    act, _ = _gelu(_conv(gate_pre, w, b))
    return act * up


def _ffn_act_back(dact, gate_pre, up, w, b):
    g = _conv(gate_pre, w, b)
    ge, th = _gelu(g)
    dge = 0.5 * (1.0 + th) + 0.5 * g * (1.0 - th * th) * _G0 * (1.0 + 3.0 * _G1 * g * g)
    dup = dact * ge
    dgate_pre, dw, db = _conv_back(gate_pre, w, dact * up * dge)
    return dgate_pre, dup, dw, db


def _ssm_act(xbc, w, b):
    c = _conv(xbc, w, b)
    return c * _sigmoid(c)


def _ssm_act_back(dxc, xbc, w, b):
    c = _conv(xbc, w, b)
    sg = _sigmoid(c)
    return _conv_back(xbc, w, dxc * sg * (1.0 + c * (1.0 - sg)))


def _rope_tables(S):
    inv = 1.0 / (ROPE_THETA ** (jnp.arange(0, ROPE, 2, dtype=F32) / ROPE))
    ang = jnp.arange(S, dtype=F32)[:, None] * inv[None, :]
    cos, sin = jnp.cos(ang), jnp.sin(ang)
    return jnp.tile(cos, (1, 4)), jnp.tile(jnp.concatenate([-sin, sin], axis=1), (1, 2))


def _swap_halves(x):
    lane = lax.broadcasted_iota(I32, x.shape, 1)
    w = x.shape[1]
    return jnp.where((lane % ROPE) < ROPE // 2, pltpu.roll(x, w - ROPE // 2, 1), pltpu.roll(x, ROPE // 2, 1))


def _rot(x, cos2, sin2):
    return x * cos2 + _swap_halves(x) * sin2


def _rot_back(dy, cos2, sin2):
    return dy * cos2 + _swap_halves(dy * sin2)


def _mla_pack(cfg, q, kv, kr, cos2, sin2):
    S, H = cfg.S, cfg.H
    ts = _pick(S, 256, 8)
    kr, _, kr_block = _window(kr)

    def body(q_ref, kv_ref, kr_ref, c_ref, s_ref, Q_ref, K_ref, V_ref):
        c2, s2 = c_ref[...], s_ref[...]
        krr = _rot(kr_ref[...], c2, s2)
        kr_half = (krr.astype(BF16), pltpu.roll(krr, ROPE, 1).astype(BF16))
        for j in range(H // 2):
            qr = _rot(q_ref[:, (H + j) * LANE:(H + j + 1) * LANE], c2, s2).astype(BF16)
            for h in (2 * j, 2 * j + 1):
                Q_ref[h, :, 0:LANE] = q_ref[:, h * LANE:(h + 1) * LANE].astype(BF16)
                Q_ref[h, :, LANE:] = qr
                K_ref[h, :, 0:LANE] = kv_ref[:, h * LANE:(h + 1) * LANE].astype(BF16)
                K_ref[h, :, LANE:] = kr_half[h % 2]
                V_ref[h] = kv_ref[:, (H + h) * LANE:(H + h + 1) * LANE].astype(BF16)

    tab = pl.BlockSpec((ts, LANE), lambda i: (i, 0))
    heads = lambda w: pl.BlockSpec((H, ts, w), lambda i: (0, i, 0))
    return pl.pallas_call(
        body, name="mla_pack", grid=(S // ts,),
        in_specs=[pl.BlockSpec((ts, cfg.QW), lambda i: (i, 0)), pl.BlockSpec((ts, cfg.KVW), lambda i: (i, 0)),
                  pl.BlockSpec((ts, LANE), lambda i: (i, kr_block)), tab, tab],
        out_specs=[heads(2 * LANE), heads(2 * LANE), heads(LANE)],
        out_shape=[jax.ShapeDtypeStruct((H, S, 2 * LANE), BF16), jax.ShapeDtypeStruct((H, S, 2 * LANE), BF16),
                   jax.ShapeDtypeStruct((H, S, LANE), BF16)],
        compiler_params=_params(("parallel",)),
    )(q, kv, kr, cos2, sin2)


def _mla_unpack(cfg, dQ, dK, dV, cos2, sin2):
    S, H = cfg.S, cfg.H
    ts = _pick(S, 256, 8)

    def body(dQ_ref, dK_ref, dV_ref, c_ref, s_ref, dq_ref, dkv_ref, dkr_ref):
        c2, s2 = c_ref[...], s_ref[...]
        lo = lax.broadcasted_iota(I32, (ts, LANE), 1) < ROPE
        tk = jnp.zeros((ts, LANE), F32)
        for h in range(H):
            dq_ref[:, h * LANE:(h + 1) * LANE] = dQ_ref[h, :, 0:LANE].astype(BF16)
            dkv_ref[:, h * LANE:(h + 1) * LANE] = dK_ref[h, :, 0:LANE].astype(BF16)
            dkv_ref[:, (H + h) * LANE:(H + h + 1) * LANE] = dV_ref[h].astype(BF16)
            own = lo if h % 2 == 0 else jnp.logical_not(lo)
            tk = tk + jnp.where(own, dK_ref[h, :, LANE:], 0.0)
        for j in range(H // 2):
            dr = dQ_ref[2 * j, :, LANE:] + dQ_ref[2 * j + 1, :, LANE:]
            dq_ref[:, (H + j) * LANE:(H + j + 1) * LANE] = _rot_back(dr, c2, s2).astype(BF16)
        dkr_rot = jnp.where(lo, tk + pltpu.roll(tk, ROPE, 1), 0.0)
        dkr_ref[...] = _rot_back(dkr_rot, c2, s2).astype(BF16)

    tab = pl.BlockSpec((ts, LANE), lambda i: (i, 0))
    return pl.pallas_call(
        body, name="mla_unpack", grid=(S // ts,),
        in_specs=[pl.BlockSpec((H, ts, 2 * LANE), lambda i: (0, i, 0)), pl.BlockSpec((H, ts, 2 * LANE), lambda i: (0, i, 0)),
                  pl.BlockSpec((H, ts, LANE), lambda i: (0, i, 0)), tab, tab],
        out_specs=[pl.BlockSpec((ts, cfg.QW), lambda i: (i, 0)), pl.BlockSpec((ts, cfg.KVW), lambda i: (i, 0)), tab],
        out_shape=[jax.ShapeDtypeStruct((S, cfg.QW), BF16), jax.ShapeDtypeStruct((S, cfg.KVW), BF16),
                   jax.ShapeDtypeStruct((S, LANE), BF16)],
        compiler_params=_params(("parallel",)),
    )(dQ, dK, dV, cos2, sin2)


_ATT_T = 256
_ATT_HB = 8
_ATT_SCALE = (NOPE + ROPE) ** -0.5


def _diag_mask(transposed=False):
    r = lax.broadcasted_iota(I32, (_ATT_T, _ATT_T), 0) // CHUNK
    c = lax.broadcasted_iota(I32, (_ATT_T, _ATT_T), 1) // CHUNK
    return r <= c if transposed else c <= r


def _row_form(col):
    return jnp.broadcast_to(col, (col.shape[0], LANE)).T[0:8, :]


def _attn_fwd(cfg, Q, K, V):
    S, H, T, HB = cfg.S, cfg.H, _ATT_T, min(cfg.H, _ATT_HB)

    def body(q_ref, k_ref, v_ref, o_ref, lse_ref, lse_t_ref):
        qi = pl.program_id(1)

        def head_step(b, kb, carry, mask):
            m, l, acc = carry
            ks = pl.multiple_of(kb * T, T)
            s = _dot(q_ref[b], k_ref[b, pl.ds(ks, T), :], NT) * _ATT_SCALE
            if mask is not None:
                s = jnp.where(mask, s, -1e30)
            m_new = jnp.maximum(m, jnp.max(s, axis=1, keepdims=True))
            p = jnp.exp(s - m_new)
            alpha = jnp.exp(m - m_new)
            l = alpha * l + jnp.sum(p, axis=1, keepdims=True)
            acc = alpha * acc + _dot(p.astype(BF16), v_ref[b, pl.ds(ks, T), :])
            return m_new, l, acc

        def step(kb, carry, mask=None):
            return tuple(head_step(b, kb, carry[b], mask) for b in range(HB))

        init = (jnp.full((T, 1), -1e30, F32), jnp.zeros((T, 1), F32), jnp.zeros((T, VH), F32))
        done = step(qi, lax.fori_loop(0, qi, step, (init,) * HB), _diag_mask())
        for b, (m, l, acc) in enumerate(done):
            o_ref[:, b * LANE:(b + 1) * LANE] = acc / l
            lse = m + jnp.log(l)
            lse_ref[:, b * LANE:(b + 1) * LANE] = jnp.broadcast_to(lse, (T, LANE))
            lse_t_ref[b] = _row_form(lse)

    return pl.pallas_call(
        body, name="attn_fwd", grid=(H // HB, S // T),
        in_specs=[pl.BlockSpec((HB, T, 2 * LANE), lambda h, i: (h, i, 0)), pl.BlockSpec((HB, S, 2 * LANE), lambda h, i: (h, 0, 0)),
                  pl.BlockSpec((HB, S, LANE), lambda h, i: (h, 0, 0))],
        out_specs=[pl.BlockSpec((T, HB * LANE), lambda h, i: (i, h)), pl.BlockSpec((T, HB * LANE), lambda h, i: (i, h)),
                   pl.BlockSpec((HB, 8, T), lambda h, i: (h, 0, i))],
        out_shape=[jax.ShapeDtypeStruct((S, H * LANE), F32), jax.ShapeDtypeStruct((S, H * LANE), F32),
                   jax.ShapeDtypeStruct((H, 8, S), F32)],
        compiler_params=_params(("parallel", "parallel")),
    )(Q, K, V)


def _attn_dq(cfg, Q, K, V, do, o, lse, after):
    S, H, T, HB = cfg.S, cfg.H, _ATT_T, min(cfg.H, _ATT_HB)

    def body(q_ref, k_ref, v_ref, do_ref, o_ref, lse_ref, after_ref, dq_ref, dl_t_ref):
        qi = pl.program_id(1)
        do = [do_ref[:, b * LANE:(b + 1) * LANE] for b in range(HB)]
        delta = [jnp.sum(do[b] * o_ref[:, b * LANE:(b + 1) * LANE], axis=1, keepdims=True) for b in range(HB)]
        dob = [d.astype(BF16) for d in do]

        def head_step(b, kb, dq, mask):
            ks = pl.multiple_of(kb * T, T)
            k = k_ref[b, pl.ds(ks, T), :]
            s = _dot(q_ref[b], k, NT) * _ATT_SCALE
            if mask is not None:
                s = jnp.where(mask, s, -1e30)
            p = jnp.exp(s - lse_ref[:, b * LANE:b * LANE + 1])
            dp = _dot(dob[b], v_ref[b, pl.ds(ks, T), :], NT)
            ds = p * (dp - delta[b]) * _ATT_SCALE
            return dq + _dot(ds.astype(BF16), k)

        def step(kb, dqs, mask=None):
            return tuple(head_step(b, kb, dqs[b], mask) for b in range(HB))

        dqs = step(qi, lax.fori_loop(0, qi, step, (jnp.zeros((T, 2 * LANE), F32),) * HB), _diag_mask())
        for b in range(HB):
            dq_ref[b] = dqs[b]
            dl_t_ref[b] = _row_form(delta[b])

    col = pl.BlockSpec((T, HB * LANE), lambda h, i: (i, h))
    return pl.pallas_call(
        body, name="attn_dq", grid=(H // HB, S // T),
        in_specs=[pl.BlockSpec((HB, T, 2 * LANE), lambda h, i: (h, i, 0)), pl.BlockSpec((HB, S, 2 * LANE), lambda h, i: (h, 0, 0)),
                  pl.BlockSpec((HB, S, LANE), lambda h, i: (h, 0, 0)), col, col, col, _ANY],
        out_specs=[pl.BlockSpec((HB, T, 2 * LANE), lambda h, i: (h, i, 0)), pl.BlockSpec((HB, 8, T), lambda h, i: (h, 0, i))],
        out_shape=[jax.ShapeDtypeStruct((H, S, 2 * LANE), F32), jax.ShapeDtypeStruct((H, 8, S), F32)],
        compiler_params=_params(("parallel", "parallel")),
    )(Q, K, V, do, o, lse, after)


def _attn_dkv(cfg, Q, K, V, do, lse_t, delta_t):
    S, H, T, HB = cfg.S, cfg.H, _ATT_T, min(cfg.H, _ATT_HB)
    nq = S // T

    def body(q_ref, k_ref, v_ref, do_ref, lse_ref, dl_ref, dk_ref, dv_ref):
        kb = pl.program_id(1)

        def head_step(b, qi, carry, mask):
            dk, dv = carry
            qs = pl.multiple_of(qi * T, T)
            q = q_ref[b, pl.ds(qs, T), :]
            dob = do_ref[pl.ds(qs, T), b * LANE:(b + 1) * LANE].astype(BF16)
            s = _dot(k_ref[b], q, NT) * _ATT_SCALE
            if mask is not None:
                s = jnp.where(mask, s, -1e30)
            p = jnp.exp(s - lse_ref[b, 0:1, pl.ds(qs, T)])
            dv = dv + _dot(p.astype(BF16), dob)
            dp = _dot(v_ref[b], dob, NT)
            ds = p * (dp - dl_ref[b, 0:1, pl.ds(qs, T)]) * _ATT_SCALE
            dk = dk + _dot(ds.astype(BF16), q)
            return dk, dv

        def step(qi, carry, mask=None):
            return tuple(head_step(b, qi, carry[b], mask) for b in range(HB))

        zero = (jnp.zeros((T, 2 * LANE), F32), jnp.zeros((T, VH), F32))
        done = lax.fori_loop(kb + 1, nq, step, step(kb, (zero,) * HB, _diag_mask(transposed=True)))
        for b, (dk, dv) in enumerate(done):
            dk_ref[b] = dk
            dv_ref[b] = dv

    row = pl.BlockSpec((HB, 8, S), lambda h, j: (h, 0, 0))
    return pl.pallas_call(
        body, name="attn_dkv", grid=(H // HB, S // T),
        in_specs=[pl.BlockSpec((HB, S, 2 * LANE), lambda h, j: (h, 0, 0)), pl.BlockSpec((HB, T, 2 * LANE), lambda h, j: (h, j, 0)),
                  pl.BlockSpec((HB, T, LANE), lambda h, j: (h, j, 0)), pl.BlockSpec((S, HB * LANE), lambda h, j: (0, h)), row, row],
        out_specs=[pl.BlockSpec((HB, T, 2 * LANE), lambda h, j: (h, j, 0)), pl.BlockSpec((HB, T, LANE), lambda h, j: (h, j, 0))],
        out_shape=[jax.ShapeDtypeStruct((H, S, 2 * LANE), F32), jax.ShapeDtypeStruct((H, S, LANE), F32)],
        compiler_params=_params(("parallel", "parallel")),
    )(Q, K, V, do, lse_t, delta_t)


def _expand_matrix(cfg):
    r = lax.broadcasted_iota(I32, (LANE, cfg.INNER), 0)
    c = lax.broadcasted_iota(I32, (LANE, cfg.INNER), 1)
    return (r == c // HP).astype(F32)


def _softplus(x):
    return jnp.maximum(x, 0.0) + jnp.log(1.0 + jnp.exp(-jnp.abs(x)))


def _ssd_prep(cfg, dt_raw, dt_bias_pad, a_log_pad, expand):
    HS = cfg.HS

    def fn(raw, bias, alog, E):
        heads = lax.broadcasted_iota(I32, raw.shape, 1) < HS
        dt = jnp.where(heads, _softplus(raw + bias), 0.0)
        a = dt * jnp.where(heads[0:1], -jnp.exp(alog), 0.0)
        return dt, a, _dot(dt, E, precision=HI)

    return _rowwise("ssd_prep", fn, [dt_raw], [dt_bias_pad, a_log_pad, expand],
                    [(LANE, F32), (LANE, F32), (cfg.INNER, F32)], [], _pick(cfg.S, 512, 8))


def _tril(T):
    return lax.broadcasted_iota(I32, (T, T), 0) >= lax.broadcasted_iota(I32, (T, T), 1)


def _ssd_fwd(cfg, xc, dt_exp, a_small, dskip_exp, expand):
    S, T, INNER, G, NPAIR = cfg.S, cfg.T, cfg.INNER, cfg.G, cfg.NPAIR
    NC = S // T

    def body(xc_ref, dte_ref, as_ref, dsk_ref, e_ref, y_ref, hin_ref, ht_ref):
        @pl.when(pl.program_id(0) == 0)
        def _():
            ht_ref[...] = jnp.zeros_like(ht_ref)

        tril = _tril(T)
        tri = tril.astype(F32)
        acs_s = _dot(tri, as_ref[...], precision=HI)
        acs_e = _dot(acs_s, e_ref[...], precision=HI)
        acs_t = acs_s.T
        lo = lax.broadcasted_iota(I32, (T, LANE), 1) < HP
        for g in range(G):
            Bb = xc_ref[:, INNER + g * NST:INNER + (g + 1) * NST].astype(BF16)
            Cb = xc_ref[:, INNER + (G + g) * NST:INNER + (G + g + 1) * NST].astype(BF16)
            Gm = _dot(Cb, Bb, NT)
            for j in range(g * NPAIR // G, (g + 1) * NPAIR // G):
                sl = slice(j * LANE, (j + 1) * LANE)
                Xp = xc_ref[:, sl]
                Xdt = Xp * dte_ref[:, sl]
                Xb = Xdt.astype(BF16)
                acs_p = acs_e[:, sl]
                last = acs_p[T - 1:T, :]
                Hin = ht_ref[j]
                hin_ref[0, j] = Hin
                yd = []
                for e in (0, 1):
                    h = 2 * j + e
                    Lm = jnp.exp(jnp.where(tril, acs_s[:, h:h + 1] - acs_t[h:h + 1, :], -1e30))
                    yd.append(_dot((Gm * Lm).astype(BF16), Xb))
                y_off = _dot(Cb, Hin.astype(BF16)) * jnp.exp(acs_p)
                y_ref[:, sl] = jnp.where(lo, yd[0], yd[1]) + y_off + Xp * dsk_ref[:, sl]
                st = _dot(Bb, (Xdt * jnp.exp(last - acs_p)).astype(BF16), TN)
                ht_ref[j] = jnp.exp(last) * Hin + st

    rows = lambda w: pl.BlockSpec((T, w), lambda c: (c, 0))
    return pl.pallas_call(
        body, name="ssd_fwd", grid=(NC,),
        in_specs=[rows(cfg.CONVCH), rows(INNER), rows(LANE), pl.BlockSpec((1, INNER), lambda c: (0, 0)),
                  pl.BlockSpec((LANE, INNER), lambda c: (0, 0))],
        out_specs=[rows(INNER), pl.BlockSpec((1, NPAIR, NST, LANE), lambda c: (c, 0, 0, 0))],
        out_shape=[jax.ShapeDtypeStruct((S, INNER), F32), jax.ShapeDtypeStruct((NC, NPAIR, NST, LANE), F32)],
        scratch_shapes=[pltpu.VMEM((NPAIR, NST, LANE), F32)],
        compiler_params=_params(("arbitrary",)),
    )(xc, dt_exp, a_small, dskip_exp, expand)


def _ssd_bwd(cfg, dy, xc, dt_exp, a_small, dskip_exp, hin, dt_raw, dt_bias_pad, a_log_pad, expand):
    S, T, INNER, G, NPAIR, HS = cfg.S, cfg.T, cfg.INNER, cfg.G, cfg.NPAIR, cfg.HS
    NC = S // T

    def body(dy_ref, xc_ref, dte_ref, as_ref, dsk_ref, hin_ref, raw_ref, bias_ref, alog_ref, e_ref,
             dxc_ref, draw_ref, dbias_ref, dalog_ref, dskip_ref, dht_ref, cols_ref, rows_ref, dacs_ref, ddt_ref):
        first = pl.program_id(0) == 0

        @pl.when(first)
        def _():
            dht_ref[...] = jnp.zeros_like(dht_ref)

        tril = _tril(T)
        tri = tril.astype(F32)
        a_s = as_ref[...]
        acs_s = _dot(tri, a_s, precision=HI)
        acs_e = _dot(acs_s, e_ref[...], precision=HI)
        acs_t = acs_s.T
        lo = lax.broadcasted_iota(I32, (T, LANE), 1) < HP
        last_row = lax.broadcasted_iota(I32, (T, LANE), 0) == T - 1
        cols_ref[...] = jnp.zeros_like(cols_ref)
        rows_ref[...] = jnp.zeros_like(rows_ref)
        dsk_parts = []
        for g in range(G):
            bsl = slice(INNER + g * NST, INNER + (g + 1) * NST)
            csl = slice(INNER + (G + g) * NST, INNER + (G + g + 1) * NST)
            Bb = xc_ref[:, bsl].astype(BF16)
            Cb = xc_ref[:, csl].astype(BF16)
            Gm = _dot(Cb, Bb, NT)
            dG = jnp.zeros((T, T), F32)
            dB = jnp.zeros((T, NST), F32)
            dC = jnp.zeros((T, NST), F32)
            for j in range(g * NPAIR // G, (g + 1) * NPAIR // G):
                sl = slice(j * LANE, (j + 1) * LANE)
                Xp = xc_ref[:, sl]
                dtp = dte_ref[:, sl]
                Xdt = Xp * dtp
                Xb = Xdt.astype(BF16)
                acs_p = acs_e[:, sl]
                last = acs_p[T - 1:T, :]
                e_p, dec, cd = jnp.exp(acs_p), jnp.exp(last - acs_p), jnp.exp(last)
                Hin = hin_ref[0, j]
                Hb = Hin.astype(BF16)
                dHn = dht_ref[j]
                dHb = dHn.astype(BF16)
                dYp = dy_ref[:, sl]
                z = _dot(Cb, Hb)
                dz = (dYp * e_p).astype(BF16)
                dacs_p = dYp * z * e_p
                dC = dC + _dot(dz, Hb, NT)
                dHin = _dot(Cb, dz, TN) + cd * dHn
                dlast = _colsum(dHn * Hin) * cd
                qv = _dot(Bb, dHb)
                dXdt = qv * dec
                ddec = qv * Xdt * dec
                dacs_p = dacs_p - ddec
                dlast = dlast + _colsum(ddec)
                dB = dB + _dot((Xdt * dec).astype(BF16), dHb, NT)
                for e in (0, 1):
                    h = 2 * j + e
                    Lm = jnp.exp(jnp.where(tril, acs_s[:, h:h + 1] - acs_t[h:h + 1, :], -1e30))
                    Mh = Gm * Lm
                    dYe = jnp.where(lo if e == 0 else jnp.logical_not(lo), dYp, 0.0).astype(BF16)
                    dM = _dot(dYe, Xb, NT)
                    dXdt = dXdt + _dot(Mh.astype(BF16), dYe, TN)
                    W = dM * Mh
                    cols_ref[:, h:h + 1] = jnp.sum(W, axis=1, keepdims=True)
                    rows_ref[h:h + 1, :] = _colsum(W)
                    dG = dG + dM * Lm
                dacs_ref[:, sl] = dacs_p + jnp.where(last_row, dlast, 0.0)
                ddt_ref[:, sl] = dXdt * Xp
                dxc_ref[:, sl] = dXdt * dtp + dYp * dsk_ref[:, sl]
                dsk_parts.append(_colsum(dYp * Xp))
                dht_ref[j] = dHin
            dGb = dG.astype(BF16)
            dxc_ref[:, bsl] = dB + _dot(dGb, Cb, TN)
            dxc_ref[:, csl] = dC + _dot(dGb, Bb)
        E = e_ref[...]
        dacs_s = cols_ref[...] - rows_ref[...].T + _dot(dacs_ref[...], E, NT, precision=HI)
        da = _dot(tri, dacs_s, TN, precision=HI)
        heads = lax.broadcasted_iota(I32, (1, LANE), 1) < HS
        A = jnp.where(heads, -jnp.exp(alog_ref[...]), 0.0)
        ddt = _dot(ddt_ref[...], E, NT, precision=HI) + da * A
        draw = jnp.where(heads, ddt * _sigmoid(raw_ref[...] + bias_ref[...]), 0.0)
        draw_ref[...] = draw
        dsk = _dot(jnp.broadcast_to(jnp.concatenate(dsk_parts, axis=1), (8, INNER)), E, NT, precision=HI)[0:1]
        for ref, val in ((dbias_ref, _colsum(draw)), (dalog_ref, _colsum(da * a_s)), (dskip_ref, dsk)):
            @pl.when(first)
            def _():
                ref[...] = val

            @pl.when(jnp.logical_not(first))
            def _():
                ref[...] += val

    dt_raw, _, raw_block = _window(dt_raw)
    rows = lambda w, b=0: pl.BlockSpec((T, w), lambda c: (NC - 1 - c, b))
    vec = lambda w: pl.BlockSpec((1, w), lambda c: (0, 0))
    return pl.pallas_call(
        body, name="ssd_bwd", grid=(NC,),
        in_specs=[rows(INNER), rows(cfg.CONVCH), rows(INNER), rows(LANE), vec(INNER),
                  pl.BlockSpec((1, NPAIR, NST, LANE), lambda c: (NC - 1 - c, 0, 0, 0)), rows(LANE, raw_block), vec(LANE), vec(LANE),
                  pl.BlockSpec((LANE, INNER), lambda c: (0, 0))],
        out_specs=[rows(cfg.CONVCH), rows(LANE), vec(LANE), vec(LANE), vec(LANE)],
        out_shape=[jax.ShapeDtypeStruct((S, cfg.CONVCH), F32), jax.ShapeDtypeStruct((S, LANE), F32)]
        + [jax.ShapeDtypeStruct((1, LANE), F32)] * 3,
        scratch_shapes=[pltpu.VMEM((NPAIR, NST, LANE), F32), pltpu.VMEM((T, LANE), F32), pltpu.VMEM((LANE, T), F32),
                        pltpu.VMEM((T, INNER), F32), pltpu.VMEM((T, INNER), F32)],
        compiler_params=_params(("arbitrary",)),
    )(dy, xc, dt_exp, a_small, dskip_exp, hin, dt_raw, dt_bias_pad, a_log_pad, expand)


def _ssd_post(cfg, y, z, norm_g):
    W = cfg.INNER // cfg.G

    def fn(y, z, g):
        yz = y * z * _sigmoid(z)
        return jnp.concatenate([yz[:, i * W:(i + 1) * W] * _rs(yz[:, i * W:(i + 1) * W]) for i in range(cfg.G)], axis=1) * g

    return _rowwise("ssd_post", fn, [y, z], [norm_g], [(cfg.INNER, BF16)], [], _pick(cfg.S, 256, 8))[0]


def _ssd_post_bwd(cfg, db, y, z, norm_g):
    W = cfg.INNER // cfg.G

    def fn(db, y, z, g):
        sg = _sigmoid(z)
        yz = y * z * sg
        dn = db * g
        dyz, nh = [], []
        for i in range(cfg.G):
            seg = yz[:, i * W:(i + 1) * W]
            r = _rs(seg)
            nh.append(seg * r)
            dyz.append(_rms_back(nh[-1], r, dn[:, i * W:(i + 1) * W]))
        dyz = jnp.concatenate(dyz, axis=1)
        return dyz * z * sg, dyz * y * sg * (1.0 + z * (1.0 - sg)), _colsum(db * jnp.concatenate(nh, axis=1))

    return _rowwise("ssd_post_bwd", fn, [db, y, z], [norm_g], [(cfg.INNER, F32), (cfg.INNER, F32)], [(1, cfg.INNER)],
                    _pick(cfg.S, 256, 8))


def _rms_pre(cfg, x, g):
    return _rowwise("rms_pre", lambda x, g: x * _rs(x) * g, [x], [g], [(cfg.D, BF16)], [], _pick(cfg.S, 256, 8))[0]


def _local_grads(cfg, x, tgt, W, sp, mla_weights=None, out_weight=None, ffn_weights=None, down_weight=None,
                 ffn_grads_ready=None, early_grads_ready=None, in_grad_ready=None, xn=None, after_in=None):
    S, D, H, INNER = cfg.S, cfg.D, cfg.H, cfg.INNER
    ts = _pick(S, 256, 8)
    tc = _CONV_COLS

    if xn is None:
        xn = _rms_pre(cfg, x, sp["mix_pre_g"])
    u = _matmul("mm_in", xn, W["w_in"], "nt", F32, after=after_in)
    c_q, c_kv, kr, z, xbc, dt_raw = [(u, cfg.window(n)) for n in ("c_q", "c_kv", "kr", "z", "xbc", "dt")]

    if mla_weights is not None:
        sp = dict(sp, q_norm_g=sp["q_norm_g"] + mla_weights.pass_on(u)[0, 0])
    cqn = _rowwise("rms_q", lambda x, g: x * _rs(x) * g, [c_q], [sp["q_norm_g"]], [(cfg.QL, BF16)], [], ts)[0]
    ckvn = _rowwise("rms_kv", lambda x, g: x * _rs(x) * g, [c_kv], [sp["kv_norm_g"]], [(cfg.KVL, BF16)], [], ts)[0]
    if mla_weights is not None:
        W = dict(W, **mla_weights.arrived(ckvn))
    q = _matmul("mm_uq", cqn, W["w_uq"], "nn", F32)
    kv = _matmul("mm_ukv", ckvn, W["w_ukv"], "nn", F32)
    cos2, sin2 = _rope_tables(S)
    Qh, Kh, Vh = _mla_pack(cfg, q, kv, kr, cos2, sin2)
    a_out, lse, lse_t = _attn_fwd(cfg, Qh, Kh, Vh)
    if out_weight is not None:
        sp = dict(sp, ssm_conv_b=sp["ssm_conv_b"] + out_weight.pass_on(a_out)[0, 0])

    pad = lambda v: jnp.pad(v, ((0, 0), (0, LANE - v.shape[1])))
    expand = _expand_matrix(cfg)
    dt_bias_pad, a_log_pad = pad(sp["dt_bias"]), pad(sp["a_log"])
    dskip_exp = jnp.repeat(sp["d_skip"], HP, axis=1)
    xc = _colwise("ssm_act", _ssm_act, [xbc], [sp["ssm_conv_w"], sp["ssm_conv_b"]], [F32], [], tc)[0]
    dt_s, a_s, dt_exp = _ssd_prep(cfg, dt_raw, dt_bias_pad, a_log_pad, expand)
    y_ssd, hin = _ssd_fwd(cfg, xc, dt_exp, a_s, dskip_exp, expand)
    b_out = _ssd_post(cfg, y_ssd, z, sp["ssm_norm_g"])

    ab_out = jnp.concatenate([a_out.astype(BF16), b_out], axis=1)
    if out_weight is not None:
        W = dict(W, **out_weight.arrived(ab_out))
    if ffn_weights is not None:
        sp = dict(sp, mix_post_g=sp["mix_post_g"] + ffn_weights.pass_on(ab_out)[0, 0])
    mix = _matmul("mm_out", ab_out, W["w_out"], "nn", F32)

    def mid(x, mix, g_mp, g_fp):
        x1 = x + mix * _rs(mix) * g_mp
        return x1, x1 * _rs(x1) * g_fp

    x1, h2 = _rowwise("fwd_mid", mid, [x, mix], [sp["mix_post_g"], sp["ffn_pre_g"]], [(D, F32), (D, BF16)], [], ts)
    if ffn_weights is not None:
        W = dict(W, **ffn_weights.arrived(h2))
    gate_pre = _matmul("mm_gate", h2, W["w_gate"], "nn", F32, chips=True)
    if down_weight is not None:
        sp = dict(sp, ffn_conv_b=sp["ffn_conv_b"] + down_weight.pass_on(gate_pre)[0, 0])
    up = _matmul("mm_up", h2, W["w_up"], "nn", F32, chips=True)
    act = _colwise("ffn_act", _ffn_act, [gate_pre, up], [sp["ffn_conv_w"], sp["ffn_conv_b"]], [BF16], [], tc)[0]
    if down_weight is not None:
        W = dict(W, **down_weight.arrived(act))
    f = _matmul("mm_down", act, W["w_down"], "nn", F32)

    def final(x1, f, t, g):
        r = _rs(f)
        fh = f * r
        err = x1 + fh * g - t
        loss = 0.5 * jnp.sum(jnp.mean(err * err, axis=-1, keepdims=True), axis=0, keepdims=True)
        dy = err * (1.0 / D)
        return dy, _rms_back(fh, r, dy * g), _colsum(dy * fh), loss

    dy, df, g_ffn_post, loss = _rowwise("final", final, [x1, f, tgt], [sp["ffn_post_g"]], [(D, F32), (D, BF16)],
                                        [(1, D), (1, LANE)], ts)
    gW = {}
    dact = _matmul("mm_down_dx", df, W["w_down"], "nt", F32)
    gW["w_down"] = _matmul("mm_down_dw", act, df, "tn", BF16)
    dgate, dup, g_ffn_conv_w, g_ffn_conv_b = _colwise(
        "ffn_act_bwd", _ffn_act_back, [dact, gate_pre, up], [sp["ffn_conv_w"], sp["ffn_conv_b"]], [BF16, BF16], [FFN_K, 1], tc)
    gW["w_gate"] = _matmul("mm_gate_dw", h2, dgate, "tn", BF16, chips=True)
    gW["w_up"] = _matmul("mm_up_dw", h2, dup, "tn", BF16, chips=True)
    if ffn_grads_ready is not None:
        sp = dict(sp, ffn_pre_g=sp["ffn_pre_g"] + ffn_grads_ready({n: gW[n] for n in ("w_down", "w_gate", "w_up")})[0, 0])
    dh2 = _matmul("mm_gu_dx", dgate, W["w_gate"], "nt", F32, dup, W["w_up"], chips=True)

    def mid_back(dy, dh2, x1, mix, g_mp, g_fp):
        r2 = _rs(x1)
        xh = x1 * r2
        dx1 = dy + _rms_back(xh, r2, dh2 * g_fp)
        r1 = _rs(mix)
        mh = mix * r1
        return dx1, _rms_back(mh, r1, dx1 * g_mp), _colsum(dh2 * xh), _colsum(dx1 * mh)

    dx1, dmix, g_ffn_pre, g_mix_post = _rowwise("bwd_mid", mid_back, [dy, dh2, x1, mix], [sp["mix_post_g"], sp["ffn_pre_g"]],
                                                [(D, F32), (D, BF16)], [(1, D), (1, D)], ts)
    dab_out = _matmul("mm_out_dx", dmix, W["w_out"], "nt", F32)
    db_out = (dab_out, (INNER, cfg.MLAW // INNER))
    gW["w_out"] = _matmul("mm_out_dw", ab_out, dmix, "tn", BF16)
    early_token = jnp.zeros((8, LANE), F32)
    if early_grads_ready is not None:
        early_token = early_grads_ready({n: gW[n] for n in ("w_down", "w_gate", "w_up", "w_out")})
        sp = dict(sp, ssm_norm_g=sp["ssm_norm_g"] + early_token[0, 0])

    dy_ssd, dz, g_ssm_norm = _ssd_post_bwd(cfg, db_out, y_ssd, z, sp["ssm_norm_g"])
    dxc, ddt_raw, g_dt_bias, g_a_log, g_d_skip = _ssd_bwd(cfg, dy_ssd, xc, dt_exp, a_s, dskip_exp, hin, dt_raw,
                                                          dt_bias_pad, a_log_pad, expand)
    dxbc, g_ssm_conv_w, g_ssm_conv_b = _colwise("ssm_act_bwd", _ssm_act_back, [dxc, xbc], [sp["ssm_conv_w"], sp["ssm_conv_b"]],
                                                [BF16], [SSM_K, 1], tc)

    dQ, delta_t = _attn_dq(cfg, Qh, Kh, Vh, dab_out, a_out, lse, early_token)
    dK, dV = _attn_dkv(cfg, Qh, Kh, Vh, dab_out, lse_t, delta_t)
    dq, dkv, dkr = _mla_unpack(cfg, dQ, dK, dV, cos2, sin2)
    dcqn = _matmul("mm_uq_dx", dq, W["w_uq"], "nt", F32)
    dckvn = _matmul("mm_ukv_dx", dkv, W["w_ukv"], "nt", F32)
    gW["w_uq"] = _matmul("mm_uq_dw", cqn, dq, "tn", BF16)
    gW["w_ukv"] = _matmul("mm_ukv_dw", ckvn, dkv, "tn", BF16)

    def rms_back(x, dy, g):
        r = _rs(x)
        xh = x * r
        return _rms_back(xh, r, dy * g), _colsum(dy * xh)

    dc_q, g_q_norm = _rowwise("rms_q_bwd", rms_back, [c_q, dcqn], [sp["q_norm_g"]], [(cfg.QL, BF16)], [(1, cfg.QL)], ts)
    dc_kv, g_kv_norm = _rowwise("rms_kv_bwd", rms_back, [c_kv, dckvn], [sp["kv_norm_g"]], [(cfg.KVL, BF16)], [(1, cfg.KVL)], ts)

    du = dict(c_q=dc_q, c_kv=dc_kv, kr=dkr, z=dz.astype(BF16), xbc=dxbc, dt=ddt_raw.astype(BF16))
    du = jnp.concatenate([du[n] for n in sorted(du, key=lambda n: cfg.seg[n][0])], axis=1)
    assert du.shape[1] == cfg.EXT, "the layout of u has gaps"
    gW["w_in"] = _matmul("mm_in_dw", du, xn, "tn", BF16)
    if in_grad_ready is not None:
        token = in_grad_ready({n: gW[n] for n in ("w_in", "w_uq", "w_ukv")})
        sp = dict(sp, mix_pre_g=sp["mix_pre_g"] + token[0, 0])
    dxn = _matmul("mm_in_dx", du, W["w_in"], "nn", F32)

    def first_back(dx1, dxn, x, g):
        r = _rs(x)
        xh = x * r
        return dx1 + _rms_back(xh, r, dxn * g), _colsum(dxn * xh)

    grad_x, g_mix_pre = _rowwise("bwd_first", first_back, [dx1, dxn, x], [sp["mix_pre_g"]], [(D, F32)], [(1, D)], ts)

    gs = dict(mix_pre_g=g_mix_pre, q_norm_g=g_q_norm, kv_norm_g=g_kv_norm, ssm_conv_w=g_ssm_conv_w, ssm_conv_b=g_ssm_conv_b,
              dt_bias=g_dt_bias[:, :cfg.HS], a_log=g_a_log[:, :cfg.HS], d_skip=g_d_skip[:, :cfg.HS], ssm_norm_g=g_ssm_norm,
              mix_post_g=g_mix_post, ffn_pre_g=g_ffn_pre, ffn_conv_w=g_ffn_conv_w, ffn_conv_b=g_ffn_conv_b,
              ffn_post_g=g_ffn_post)
    return loss, grad_x, gW, gs


def _to_kernel_layout(cfg, name, w):
    if name == "w_in":
        parts, at = [], 0
        for off, width, n_off, n_width in sorted(cfg.seg.values()):
            parts += [jnp.zeros((off - at, w.shape[1]), w.dtype), w[n_off:n_off + n_width],
                      jnp.zeros((width - n_width, w.shape[1]), w.dtype)]
            at = off + width
        parts.append(jnp.zeros((cfg.EXT - at, w.shape[1]), w.dtype))
        return jnp.concatenate([p for p in parts if p.shape[0]], axis=0)
    if name in ("w_uq", "w_ukv"):
        per = NOPE + (ROPE if name == "w_uq" else VH)
        return jnp.concatenate([w[:, h * per:h * per + NOPE] for h in range(cfg.H)]
                               + [w[:, h * per + NOPE:(h + 1) * per] for h in range(cfg.H)], axis=1)
    return w


def _from_kernel_layout(cfg, name, g):
    if name == "w_in":
        return jnp.concatenate([g[off:off + n_width] for off, _, _, n_width in sorted(cfg.seg.values(), key=lambda s: s[2])], axis=0)
    if name in ("w_uq", "w_ukv"):
        second = ROPE if name == "w_uq" else VH
        base = cfg.H * NOPE
        parts = []
        for h in range(cfg.H):
            parts += [g[:, h * NOPE:(h + 1) * NOPE], g[:, base + h * second:base + (h + 1) * second]]
        return jnp.concatenate(parts, axis=1)
    return g


_CHIP_MAJOR = ("w_gate", "w_up")
_RELAYOUT = ("w_uq", "w_ukv")
_LAYOUT_ROWS = 256
_CONV_COLS = 256


def _w_in_layout(cfg, wg):
    _, rs, d = wg.shape
    tc = _pick(d, _LAYOUT_ROWS, LANE)

    def body(w_ref, o_ref):
        o_ref[...] = _to_kernel_layout(cfg, "w_in", jnp.concatenate([w_ref[k] for k in range(N_CHIPS)], axis=0))

    return pl.pallas_call(
        body, name="layout_w_in", grid=(d // tc,),
        in_specs=[pl.BlockSpec((N_CHIPS, rs, tc), lambda j: (0, 0, j))], out_specs=pl.BlockSpec((cfg.EXT, tc), lambda j: (0, j)),
        out_shape=jax.ShapeDtypeStruct((cfg.EXT, d), wg.dtype), compiler_params=_params(("parallel",)),
    )(wg)


def _w_in_grad_to_chips(cfg, g):
    _, d = g.shape
    rs = cfg.IN_COLS // N_CHIPS
    tc = _pick(d, _LAYOUT_ROWS, LANE)

    def body(g_ref, o_ref):
        nat = _from_kernel_layout(cfg, "w_in", g_ref[...])
        for k in range(N_CHIPS):
            o_ref[k] = nat[k * rs:(k + 1) * rs]

    return pl.pallas_call(
        body, name="layout_grad_w_in", grid=(d // tc,),
        in_specs=[pl.BlockSpec((cfg.EXT, tc), lambda j: (0, j))], out_specs=pl.BlockSpec((N_CHIPS, rs, tc), lambda j: (0, 0, j)),
        out_shape=jax.ShapeDtypeStruct((N_CHIPS, rs, d), g.dtype), compiler_params=_params(("parallel",)),
    )(g)


def _gathered_to_kernel(cfg, name, wg):
    if name in _CHIP_MAJOR:
        return wg
    if name == "w_in":
        return _w_in_layout(cfg, wg)
    if name not in _RELAYOUT:
        return wg.reshape(wg.shape[0] * wg.shape[1], wg.shape[2])
    _, rows, cs = wg.shape
    tr = _pick(rows, _LAYOUT_ROWS, 16)

    def body(w_ref, o_ref):
        o_ref[...] = _to_kernel_layout(cfg, name, jnp.concatenate([w_ref[k] for k in range(N_CHIPS)], axis=1))

    wide = jax.eval_shape(lambda w: _to_kernel_layout(cfg, name, w), jax.ShapeDtypeStruct((rows, N_CHIPS * cs), wg.dtype)).shape[1]
    return pl.pallas_call(
        body, name="layout_" + name, grid=(rows // tr,),
        in_specs=[pl.BlockSpec((N_CHIPS, tr, cs), lambda i: (0, i, 0))], out_specs=pl.BlockSpec((tr, wide), lambda i: (i, 0)),
        out_shape=jax.ShapeDtypeStruct((rows, wide), wg.dtype), compiler_params=_params(("parallel",)),
    )(wg)


def _grad_to_chips(cfg, name, g):
    if name in _CHIP_MAJOR:
        return g
    if name == "w_in":
        return _w_in_grad_to_chips(cfg, g)
    if name not in _RELAYOUT:
        return g.reshape(N_CHIPS, g.shape[0] // N_CHIPS, g.shape[1])
    rows, wide = g.shape
    tr = _pick(rows, _LAYOUT_ROWS, 16)
    cs = jax.eval_shape(lambda v: _from_kernel_layout(cfg, name, v), g).shape[1] // N_CHIPS

    def body(g_ref, o_ref):
        nat = _from_kernel_layout(cfg, name, g_ref[...])
        for k in range(N_CHIPS):
            o_ref[k] = nat[:, k * cs:(k + 1) * cs]

    return pl.pallas_call(
        body, name="layout_grad_" + name, grid=(rows // tr,),
        in_specs=[pl.BlockSpec((tr, wide), lambda i: (i, 0))], out_specs=pl.BlockSpec((N_CHIPS, tr, cs), lambda i: (0, i, 0)),
        out_shape=jax.ShapeDtypeStruct((N_CHIPS, rows, cs), g.dtype), compiler_params=_params(("parallel",)),
    )(g)


def _me():
    return lax.axis_index("x"), lax.axis_index("y"), lax.axis_index("c")


def _other_chips(x, y):
    return [(1 - x, y), (x, 1 - y), (1 - x, 1 - y)]


_ANY = pl.BlockSpec(memory_space=pl.ANY)


BLOCK_ELEMS = 1 << 19
BLOCK_ELEMS_FEW = 1 << 20


def _row_block(rows, cols, mult, elems=BLOCK_ELEMS):
    return _pick(rows, max(mult, elems // cols // mult * mult), mult)


def _scalar(v):
    return v.astype(I32).reshape(1)


def _blocks2d(r, c, mult, elems=BLOCK_ELEMS):
    if r % mult == 0:
        tr = _row_block(r, c, mult, elems)
        return (tr, c), r // tr, lambda i: (i, 0)
    tc = _pick(c, max(LANE, elems // r // LANE * LANE), LANE)
    return (r, tc), c // tc, lambda i: (0, i)


def _by_rows(rows):
    return rows % 32 == 0


def _half_shape(rows, cols):
    return (rows // 2, cols) if _by_rows(rows) else (rows, cols // 2)


def _half_blocks(rows, cols, mult, elems=BLOCK_ELEMS):
    hr, hc = _half_shape(rows, cols)
    block, n, part = _blocks2d(hr, hc, mult, elems)
    assert (hr % mult == 0) == _by_rows(rows), (rows, cols, mult)
    full = (lambda h, i: (h * n + i, 0)) if _by_rows(rows) else (lambda h, i: (0, h * n + i))
    return block, n, full, part


def _half(ref, k, half):
    hr, hc = _half_shape(ref.shape[1], ref.shape[2])
    if _by_rows(ref.shape[1]):
        return ref.at[k, pl.ds(pl.multiple_of(half * hr, 16), hr), :]
    return ref.at[k, :, pl.ds(pl.multiple_of(half * hc, LANE), hc)]


def _shard_blocks(w, br, bc):
    if w.shape[0] == 1:
        def write(ref, v):
            ref[...] = v
        return (lambda f: pl.BlockSpec((None, br, bc), lambda *a: (0, *f(*a)))), (lambda ref: ref[...]), write
    assert w.shape[1] == 1 and br == w.shape[0], w.shape

    def write_rows(ref, v):
        ref[:, 0, :] = v
    return (lambda f: pl.BlockSpec((br, 1, bc), lambda *a: (0, 0, f(*a)[1]))), (lambda ref: ref[:, 0, :]), write_rows


def _stage_shard(name, w, chip, after=None):
    rs, cs = w.shape[0] * w.shape[1], w.shape[2]
    (br, bc), n, idx = _blocks2d(rs, cs, 16, BLOCK_ELEMS_FEW)
    spec, get, _ = _shard_blocks(w, br, bc)

    def body(chip_ref, w_ref, *refs):
        refs[-1][...] = get(w_ref).astype(BF16)

    return pl.pallas_call(
        body, name="stage_" + name,
        grid_spec=pltpu.PrefetchScalarGridSpec(
            num_scalar_prefetch=1, grid=(n,),
            in_specs=[spec(lambda i, chip_ref: idx(i))] + ([] if after is None else [_ANY]),
            out_specs=pl.BlockSpec((None, br, bc), lambda i, chip_ref: (chip_ref[0], *idx(i)))),
        out_shape=jax.ShapeDtypeStruct((N_CHIPS, rs, cs), BF16),
        compiler_params=_params(("parallel",)),
    )(_scalar(chip), w, *([] if after is None else [after]))


_HBM = pl.BlockSpec(memory_space=pltpu.HBM)
_SEM = pl.BlockSpec(memory_space=pltpu.SEMAPHORE)
_EFFECT = pltpu.SideEffectType.DATAFLOW_SIDE_EFFECTING


def _split_start(name, bufs, n_copies, copies, after):
    n = len(bufs)

    def body(*refs):
        for cp in copies(refs[:n], refs[n + 1], refs[n + 2]):
            cp.start()
        refs[-1][...] = jnp.zeros_like(refs[-1])

    res = pl.pallas_call(
        body, name=name,
        out_shape=(pltpu.SemaphoreType.DMA((n_copies,)), pltpu.SemaphoreType.DMA((n_copies,)),
                   *[pltpu.HBM(b.shape, b.dtype) for b in bufs], jax.ShapeDtypeStruct((8, LANE), F32)),
        in_specs=[_HBM] * n + [_ANY], out_specs=(_SEM, _SEM, *[_HBM] * n, pl.BlockSpec(memory_space=pltpu.VMEM)),
        input_output_aliases={i: 2 + i for i in range(n)},
        compiler_params=pltpu.CompilerParams(has_side_effects=_EFFECT),
    )(*[pltpu.with_memory_space_constraint(b, pltpu.HBM) for b in bufs], after)
    return res[0], res[1], list(res[2:2 + n]), res[-1]


def _split_wait(name, send_sems, recv_sems, bufs, after, copies):
    n = len(bufs)

    def body(*refs):
        for cp in copies(refs[:n], refs[n], refs[n + 1]):
            cp.wait_send()
            cp.wait_recv()

    return list(pl.pallas_call(
        body, name=name, out_shape=[pltpu.HBM(b.shape, b.dtype) for b in bufs],
        in_specs=[_HBM] * n + [_SEM, _SEM, _ANY], out_specs=[_HBM] * n,
        input_output_aliases={i: i for i in range(n)},
        compiler_params=pltpu.CompilerParams(has_side_effects=_EFFECT),
    )(*bufs, send_sems, recv_sems, after))


def _gather_to_chips(bufs, send_sems, recv_sems):
    x, y, c = _me()
    return [pltpu.make_async_remote_copy(src_ref=_half(b, 2 * x + y, c), dst_ref=_half(b, 2 * x + y, c),
                                         send_sem=send_sems.at[3 * w + j], recv_sem=recv_sems.at[3 * w + j],
                                         device_id=(cx, cy, c), device_id_type=MESH_ID)
            for w, b in enumerate(bufs) for j, (cx, cy) in enumerate(_other_chips(x, y))]


def _gather_to_sibling(bufs, send_sems, recv_sems):
    x, y, c = _me()
    return [pltpu.make_async_remote_copy(src_ref=_half(b, 2 * cx + cy, c), dst_ref=_half(b, 2 * cx + cy, c),
                                         send_sem=send_sems.at[3 * w + j], recv_sem=recv_sems.at[3 * w + j],
                                         device_id=(x, y, 1 - c), device_id_type=MESH_ID)
            for w, b in enumerate(bufs) for j, (cx, cy) in enumerate(_other_chips(x, y))]


def _pair_exchange(name, grads):
    n = len(grads)

    def body(*refs):
        ins, outs, send_sems, recv_sems = refs[:n], refs[n:2 * n], refs[2 * n], refs[2 * n + 1]
        x, y, c = _me()
        cps = []
        for w, (g_ref, o_ref) in enumerate(zip(ins, outs)):
            cps.append(pltpu.make_async_remote_copy(src_ref=_half(g_ref, slice(None), 1 - c), dst_ref=o_ref,
                                                    send_sem=send_sems.at[w], recv_sem=recv_sems.at[w],
                                                    device_id=(x, y, 1 - c), device_id_type=MESH_ID))
            cps[-1].start()
        for cp in cps:
            cp.wait()

    return pl.pallas_call(
        body, name="pair_exchange_" + name, in_specs=[_ANY] * n, out_specs=[_ANY] * n,
        out_shape=[jax.ShapeDtypeStruct((g.shape[0], *_half_shape(g.shape[1], g.shape[2])), g.dtype) for g in grads],
        scratch_shapes=[pltpu.SemaphoreType.DMA((n,)), pltpu.SemaphoreType.DMA((n,))],
    )(*grads)


def _pair_copies(grads, lands, send_sems, recv_sems):
    x, y, c = _me()
    return [pltpu.make_async_remote_copy(src_ref=_half(g_ref, slice(None), 1 - c), dst_ref=l_ref, send_sem=send_sems.at[w],
                                         recv_sem=recv_sems.at[w], device_id=(x, y, 1 - c), device_id_type=MESH_ID)
            for w, (g_ref, l_ref) in enumerate(zip(grads, lands))]


def _pair_exchange_start(name, grads):
    n = len(grads)
    lands = [lax.empty((g.shape[0], *_half_shape(g.shape[1], g.shape[2])), g.dtype) for g in grads]
    send_sems, recv_sems, bufs, token = _split_start(
        "pair_exchange_start_" + name, [*grads, *lands], n, lambda refs, ss, rs: _pair_copies(refs[:n], refs[n:], ss, rs),
        jnp.zeros((8, LANE), F32))
    return (send_sems, recv_sems, bufs), token


def _pair_exchange_wait(name, state, after):
    send_sems, recv_sems, bufs = state
    n = len(bufs) // 2
    bufs = _split_wait("pair_exchange_wait_" + name, send_sems, recv_sems, bufs, after,
                       lambda refs, ss, rs: _pair_copies(refs[:n], refs[n:], ss, rs))
    return bufs[:n], bufs[n:]


def _pair_sum(name, g, theirs, c):
    (br, bc), nb, full, part = _half_blocks(g.shape[1], g.shape[2], 16, 2 * BLOCK_ELEMS_FEW)

    def body(c_ref, a_ref, b_ref, o_ref):
        o_ref[...] = (a_ref[...].astype(F32) + b_ref[...].astype(F32)).astype(o_ref.dtype)

    return pl.pallas_call(
        body, name="pair_sum_" + name,
        grid_spec=pltpu.PrefetchScalarGridSpec(
            num_scalar_prefetch=1, grid=(N_CHIPS, nb),
            in_specs=[pl.BlockSpec((None, br, bc), lambda k, i, c_ref: (k, *full(c_ref[0], i))),
                      pl.BlockSpec((None, br, bc), lambda k, i, c_ref: (k, *part(i)))],
            out_specs=pl.BlockSpec((None, br, bc), lambda k, i, c_ref: (k, *part(i)))),
        out_shape=jax.ShapeDtypeStruct(theirs.shape, BF16),
        compiler_params=_params(("parallel", "parallel")),
    )(_scalar(c), g, theirs)


def _chip_copies(srcs, lands, send_sems, recv_sems):
    x, y, c = _me()
    return [pltpu.make_async_remote_copy(src_ref=s_ref.at[2 * cx + cy], dst_ref=l_ref.at[j], send_sem=send_sems.at[3 * w + j],
                                         recv_sem=recv_sems.at[3 * w + j], device_id=(cx, cy, c), device_id_type=MESH_ID)
            for w, (s_ref, l_ref) in enumerate(zip(srcs, lands)) for j, (cx, cy) in enumerate(_other_chips(x, y))]


def _chip_exchange_start(name, sums):
    n = len(sums)
    lands = [lax.empty((3,) + s.shape[1:], s.dtype) for s in sums]
    send_sems, recv_sems, bufs, token = _split_start(
        "chip_exchange_start_" + name, [*sums, *lands], 3 * n, lambda refs, ss, rs: _chip_copies(refs[:n], refs[n:], ss, rs),
        jnp.zeros((8, LANE), F32))
    return send_sems, recv_sems, bufs[:n], bufs[n:], token


def _chip_exchange_wait(name, send_sems, recv_sems, sums, lands, after):
    n = len(sums)
    bufs = _split_wait("chip_exchange_wait_" + name, send_sems, recv_sems, [*sums, *lands], after,
                       lambda refs, ss, rs: _chip_copies(refs[:n], refs[n:], ss, rs))
    return bufs[:n], bufs[n:]


def _chip_sum(name, sums, theirs, chip):
    _, h, cs = sums.shape
    (br, bc), nb, idx = _blocks2d(h, cs, 16, BLOCK_ELEMS_FEW)

    def body(chip_ref, s_ref, t_ref, o_ref):
        acc = s_ref[...].astype(F32)
        for k in range(3):
            acc = acc + t_ref[k].astype(F32)
        o_ref[...] = acc

    return pl.pallas_call(
        body, name="chip_sum_" + name,
        grid_spec=pltpu.PrefetchScalarGridSpec(
            num_scalar_prefetch=1, grid=(nb,),
            in_specs=[pl.BlockSpec((None, br, bc), lambda i, chip_ref: (chip_ref[0], *idx(i))),
                      pl.BlockSpec((3, br, bc), lambda i, chip_ref: (0, *idx(i)))],
            out_specs=pl.BlockSpec((br, bc), lambda i, chip_ref: idx(i))),
        out_shape=jax.ShapeDtypeStruct((h, cs), F32),
        compiler_params=_params(("parallel",)),
    )(_scalar(chip), sums, theirs)


def _sibling_copies(halves, lands, send_sems, recv_sems):
    x, y, c = _me()
    return [pltpu.make_async_remote_copy(src_ref=h_ref, dst_ref=l_ref, send_sem=send_sems.at[w], recv_sem=recv_sems.at[w],
                                         device_id=(x, y, 1 - c), device_id_type=MESH_ID)
            for w, (h_ref, l_ref) in enumerate(zip(halves, lands))]


def _sibling_exchange_start(name, halves, after):
    n = len(halves)
    lands = [lax.empty(h.shape, h.dtype) for h in halves]
    send_sems, recv_sems, bufs, token = _split_start(
        "sibling_exchange_start_" + name, [*halves, *lands], n, lambda refs, ss, rs: _sibling_copies(refs[:n], refs[n:], ss, rs),
        after)
    return (send_sems, recv_sems, bufs), token


def _sibling_exchange_wait(name, state, after):
    send_sems, recv_sems, bufs = state
    n = len(bufs) // 2
    bufs = _split_wait("sibling_exchange_wait_" + name, send_sems, recv_sems, bufs, after,
                       lambda refs, ss, rs: _sibling_copies(refs[:n], refs[n:], ss, rs))
    return bufs[:n], bufs[n:]


def _sibling_exchange(name, halves):
    n = len(halves)

    def body(*refs):
        ins, outs, send_sems, recv_sems = refs[:n], refs[n:2 * n], refs[2 * n], refs[2 * n + 1]
        x, y, c = _me()
        cps = []
        for w, (h_ref, o_ref) in enumerate(zip(ins, outs)):
            cps.append(pltpu.make_async_remote_copy(src_ref=h_ref, dst_ref=o_ref, send_sem=send_sems.at[w], recv_sem=recv_sems.at[w],
                                                    device_id=(x, y, 1 - c), device_id_type=MESH_ID))
            cps[-1].start()
        for cp in cps:
            cp.wait()

    return pl.pallas_call(
        body, name="sibling_exchange_" + name, in_specs=[_ANY] * n, out_specs=[_ANY] * n,
        out_shape=[jax.ShapeDtypeStruct(h.shape, h.dtype) for h in halves],
        scratch_shapes=[pltpu.SemaphoreType.DMA((n,)), pltpu.SemaphoreType.DMA((n,))],
    )(*halves)


N_DEV = 8


def _peer_copies(bufs, send_sems, recv_sems):
    vec, land = bufs
    x, y, c = _me()
    return [pltpu.make_async_remote_copy(src_ref=vec, dst_ref=land.at[4 * x + 2 * y + c], send_sem=send_sems.at[p - 1],
                                         recv_sem=recv_sems.at[p - 1], device_id=(x ^ (p >> 2), y ^ ((p >> 1) & 1), c ^ (p & 1)),
                                         device_id_type=MESH_ID) for p in range(1, N_DEV)]


def _allreduce_small_start(vec, after):
    land = jnp.zeros((N_DEV,) + vec.shape, F32)
    send_sems, recv_sems, bufs, _ = _split_start("allreduce_small_start", [vec, land], N_DEV - 1, _peer_copies, after)
    return send_sems, recv_sems, bufs


def _allreduce_small_wait(state, chip, core, after):
    send_sems, recv_sems, bufs = state
    vec, land = _split_wait("allreduce_small_wait", send_sems, recv_sems, bufs, after, _peer_copies)

    def body(me_ref, v_ref, l_ref, o_ref):
        acc = None
        for k in range(N_DEV):
            term = jnp.where(me_ref[0] == k, v_ref[...], l_ref[k])
            acc = term if acc is None else acc + term
        o_ref[...] = acc

    return pl.pallas_call(
        body, name="allreduce_small_sum",
        grid_spec=pltpu.PrefetchScalarGridSpec(
            num_scalar_prefetch=1, grid=(1,),
            in_specs=[pl.BlockSpec(vec.shape, lambda i, me_ref: (0, 0)), pl.BlockSpec(land.shape, lambda i, me_ref: (0, 0, 0))],
            out_specs=pl.BlockSpec(vec.shape, lambda i, me_ref: (0, 0))),
        out_shape=jax.ShapeDtypeStruct(vec.shape, F32), compiler_params=_params(("arbitrary",)),
    )(_scalar(2 * chip + core), vec, land)


def _adam_math(w, g, m, v):
    m = ADAM_B1 * m + (1.0 - ADAM_B1) * g
    v = ADAM_B2 * v + (1.0 - ADAM_B2) * (g * g)
    m_hat = m / (1.0 - ADAM_B1 ** ADAM_STEP)
    v_hat = v / (1.0 - ADAM_B2 ** ADAM_STEP)
    return -ADAM_LR * (m_hat / (jnp.sqrt(v_hat) + ADAM_EPS) + ADAM_WD * w), m, v


def _adamw(name, w, g, m, v):
    R, C = w.shape
    tr = _row_block(R, C, 8)

    def body(w_ref, g_ref, m_ref, v_ref, d_ref, nm_ref, nv_ref):
        d_ref[...], nm_ref[...], nv_ref[...] = _adam_math(w_ref[...], g_ref[...], m_ref[...], v_ref[...])

    blk = pl.BlockSpec((tr, C), lambda i: (i, 0))
    return pl.pallas_call(
        body, name=name, grid=(R // tr,), in_specs=[blk] * 4, out_specs=[blk] * 3,
        out_shape=[jax.ShapeDtypeStruct((R, C), F32)] * 3, compiler_params=_params(("parallel",)),
    )(w, g, m, v)


def _adamw_halves(name, w, mine, theirs, m, v, c):
    rs, cs = w.shape[0] * w.shape[1], w.shape[2]
    (br, bc), nb, whole, half = _half_blocks(rs, cs, 8)
    spec, get, put = _shard_blocks(w, br, bc)

    def body(c_ref, w_ref, a_ref, b_ref, m_ref, v_ref, g_ref, d_ref, nm_ref, nv_ref):
        g = jnp.where(pl.program_id(0) == c_ref[0], a_ref[...], b_ref[...])
        put(g_ref, g)
        for ref, val in zip((d_ref, nm_ref, nv_ref), _adam_math(get(w_ref), g, get(m_ref), get(v_ref))):
            put(ref, val)

    full = spec(lambda s, i, c_ref: whole(s, i))
    part = pl.BlockSpec((br, bc), lambda s, i, c_ref: half(i))
    return pl.pallas_call(
        body, name=name,
        grid_spec=pltpu.PrefetchScalarGridSpec(num_scalar_prefetch=1, grid=(2, nb), in_specs=[full, part, part, full, full],
                                               out_specs=[full] * 4),
        out_shape=[jax.ShapeDtypeStruct(w.shape, F32)] * 4, compiler_params=_params(("parallel", "parallel")),
    )(_scalar(c), w, mine, theirs, m, v)


def _pack_small(arrs, lanes=LANE):
    flat = jnp.concatenate([a.reshape(-1) for a in arrs])
    n = -(-flat.shape[0] // (8 * lanes)) * 8 * lanes
    return jnp.pad(flat, (0, n - flat.shape[0])).reshape(8, n // 8)


def _unpack_small(vec, shapes):
    flat, out, off = vec.reshape(-1), [], 0
    for s in shapes:
        out.append(flat[off:off + s[0] * s[1]].reshape(s))
        off += s[0] * s[1]
    return out


class _LateWeights:
    def __init__(self, cfg, tag, names, staged, after):
        self.cfg, self.tag, self.names, self.k = cfg, tag, names, 3 * len(names)
        self.send, self.recv, self.bufs, self.token = _split_start(f"gather_{tag}_chips_start", staged, self.k, _gather_to_chips,
                                                                    after)

    def pass_on(self, after):
        bufs = _split_wait(f"gather_{self.tag}_chips_wait", self.send, self.recv, self.bufs, after, _gather_to_chips)
        self.send, self.recv, self.bufs, token = _split_start(f"gather_{self.tag}_sibling_start", bufs, self.k, _gather_to_sibling,
                                                               self.token)
        return token

    def arrived(self, after):
        bufs = _split_wait(f"gather_{self.tag}_sibling_wait", self.send, self.recv, self.bufs, after, _gather_to_sibling)
        return {n: _gathered_to_kernel(self.cfg, n, b) for n, b in zip(self.names, bufs)}


def _step(cfg, a):
    chip = 2 * lax.axis_index("x") + lax.axis_index("y")
    core = lax.axis_index("c")
    big = BIG

    ffn = ("w_gate", "w_up", "w_down")
    first = ("w_in", "w_uq", "w_ukv")
    sp = {n: a[n] for n in SMALL}
    sharded = _pack_small([a[n] for n in SMALL_SHARDED], 2 * LANE)
    slabs = jnp.where(lax.broadcasted_iota(I32, (N_CHIPS,) + sharded.shape, 0) == chip, sharded[None], 0.0)
    staged = {"w_in": _stage_shard("w_in", a["w_in"], chip)}
    in_weight = _LateWeights(cfg, "in", ("w_in", "sharded_small"), [staged["w_in"], slabs], jnp.zeros((8, LANE), F32))
    behind = in_weight.token
    for n in big[1:]:
        behind = staged[n] = _stage_shard(n, a[n], chip, behind)
    in_weight.pass_on(behind)
    xn_early = _rms_pre(cfg, a["x"], sp["mix_pre_g"] + in_weight.token[0, 0])
    W = in_weight.arrived(xn_early)
    allp = W.pop("sharded_small").reshape((N_CHIPS,) + sharded.shape)
    per_chip = [_unpack_small(allp[ch], [a[n].shape for n in SMALL_SHARDED]) for ch in range(N_CHIPS)]
    for k, n in enumerate(SMALL_SHARDED):
        sp[n] = jnp.concatenate([per_chip[ch][k] for ch in range(N_CHIPS)], axis=1)

    mla_weights = _LateWeights(cfg, "mla", first[1:], [staged[n] for n in first[1:]], W["w_in"])
    out_weight = _LateWeights(cfg, "out", ("w_out",), [staged["w_out"]], mla_weights.token)
    ffn_weights = _LateWeights(cfg, "ffn", ffn[:2], [staged[n] for n in ffn[:2]], out_weight.token)
    down_weight = _LateWeights(cfg, "down", ffn[2:], [staged[n] for n in ffn[2:]], ffn_weights.token)

    state = {}

    def ffn_grads_ready(grads):
        state["ffn_pairs"], token = _pair_exchange_start("ffn", [_grad_to_chips(cfg, n, grads[n]) for n in ffn_grads])
        return token

    def pair_sums(names, grads, theirs):
        return [_pair_sum(n, g, t, core) for n, g, t in zip(names, grads, theirs)]

    def early_grads_ready(grads):
        g_out = [_grad_to_chips(cfg, "w_out", grads["w_out"])]
        g_ffn, t_ffn = _pair_exchange_wait("ffn", state["ffn_pairs"], g_out[0])
        sums = pair_sums(ffn_grads, g_ffn, t_ffn) + pair_sums(["w_out"], g_out, _pair_exchange("out", g_out))
        state["early"] = _chip_exchange_start("early", sums)
        return state["early"][-1]

    def reduced_halves(tag, names, after):
        send_sems, recv_sems, s_bufs, l_bufs, _ = state[tag]
        s_bufs, l_bufs = _chip_exchange_wait(tag, send_sems, recv_sems, s_bufs, l_bufs, after)
        return [_chip_sum(n, s, t, chip) for n, s, t in zip(names, s_bufs, l_bufs)]

    def in_grad_ready(grads):
        grads = [_grad_to_chips(cfg, n, grads[n]) for n in first]
        state["rest"] = _chip_exchange_start("rest", pair_sums(first, grads, _pair_exchange("rest", grads)))
        return state["rest"][-1]

    ffn_grads = ("w_down", "w_gate", "w_up")
    early = ffn_grads + ("w_out",)
    loss, grad_x, gW, gs = _local_grads(cfg, a["x"], a["loss_target"], W, sp, mla_weights, out_weight, ffn_weights, down_weight,
                                        ffn_grads_ready, early_grads_ready, in_grad_ready, xn_early, down_weight.token)
    out = {"grad_x": grad_x}

    def adamw(names, mine, theirs):
        for n, gm, gt in zip(names, mine, theirs):
            out["grad_" + n], out["delta_" + n], out["new_m_" + n], out["new_v_" + n] = _adamw_halves(
                "adamw_" + n, a[n], gm, gt, a["m_" + n], a["v_" + n], core)

    mine = reduced_halves("early", early, grad_x)
    theirs = _sibling_exchange("early", mine[:1])
    later, _ = _sibling_exchange_start("early", mine[1:], theirs[0])
    adamw(early[:1], mine[:1], theirs)
    e_mine, e_theirs = _sibling_exchange_wait("early", later, out["new_v_" + early[0]])
    adamw(early[3:], e_mine[2:], e_theirs[2:])
    mine = reduced_halves("rest", first, out["new_v_" + early[-1]])
    rest, token = _sibling_exchange_start("rest", mine, mine[0])
    small = _allreduce_small_start(_pack_small([gs[n] for n in SMALL] + [loss]), token)
    adamw(early[1:3], e_mine[:2], e_theirs[:2])
    adamw(first, *_sibling_exchange_wait("rest", rest, out["new_v_" + early[2]]))
    shapes = [gs[n].shape for n in SMALL] + [(1, LANE)]
    red = _unpack_small(_allreduce_small_wait(small, chip, core, out["new_v_" + first[-1]]), shapes)
    g_small = dict(zip(SMALL, red[:-1]))
    for n in SMALL_SHARDED:
        cs = a[n].shape[1]
        g_small[n] = lax.dynamic_slice_in_dim(g_small[n], chip * cs, cs, axis=1)
    out["loss"] = red[-1][0, 0]
    sshapes = [a[n].shape for n in SMALL]
    d, nm, nv = _adamw("adamw_small", _pack_small([a[n] for n in SMALL]), _pack_small([g_small[n] for n in SMALL]),
                       _pack_small([a["m_" + n] for n in SMALL]), _pack_small([a["v_" + n] for n in SMALL]))
    for n, dd, mm, vv in zip(SMALL, _unpack_small(d, sshapes), _unpack_small(nm, sshapes), _unpack_small(nv, sshapes)):
        out["grad_" + n], out["delta_" + n], out["new_m_" + n], out["new_v_" + n] = g_small[n], dd, mm, vv
    return out


def kernel(x, mix_pre_g, w_in, q_norm_g, w_uq, kv_norm_g, w_ukv, ssm_conv_w, ssm_conv_b, dt_bias, a_log, d_skip, ssm_norm_g, w_out, mix_post_g, ffn_pre_g, w_gate, w_up, ffn_conv_w, ffn_conv_b, w_down, ffn_post_g, loss_target, m_mix_pre_g, m_w_in, m_q_norm_g, m_w_uq, m_kv_norm_g, m_w_ukv, m_ssm_conv_w, m_ssm_conv_b, m_dt_bias, m_a_log, m_d_skip, m_ssm_norm_g, m_w_out, m_mix_post_g, m_ffn_pre_g, m_w_gate, m_w_up, m_ffn_conv_w, m_ffn_conv_b, m_w_down, m_ffn_post_g, v_mix_pre_g, v_w_in, v_q_norm_g, v_w_uq, v_kv_norm_g, v_w_ukv, v_ssm_conv_w, v_ssm_conv_b, v_dt_bias, v_a_log, v_d_skip, v_ssm_norm_g, v_w_out, v_mix_post_g, v_ffn_pre_g, v_w_gate, v_w_up, v_ffn_conv_w, v_ffn_conv_b, v_w_down, v_ffn_post_g):
    args = dict(locals())
    def given(k, v):
        if k in ("w_in", "m_w_in", "v_w_in"):
            return jnp.transpose(v, (2, 0, 1))
        return v if k.removeprefix("m_").removeprefix("v_") in BIG or v.ndim < 3 else v[0]

    out = _step(_FULL, {k: given(k, v) for k, v in args.items()})
    res = [out["loss"], out["grad_x"][None]]
    for pre in ("grad_", "delta_", "new_m_", "new_v_"):
        for n in WEIGHTS:
            o = out[pre + n]
            res.append(jnp.transpose(o, (1, 2, 0)) if n == "w_in" else o if n in BIG or args[n].ndim < 3 else o[None])
    return tuple(res)
```

```python
import math

import jax
import jax.numpy as jnp
from jax import lax
from jax.experimental import pallas as pl
from jax.experimental.pallas import tpu as pltpu

F32, BF16, I32 = jnp.float32, jnp.bfloat16, jnp.int32
NN = (((1,), (0,)), ((), ()))
NT = (((1,), (1,)), ((), ()))
TN = (((0,), (0,)), ((), ()))
HI = lax.Precision.HIGHEST
MESH_ID = pl.DeviceIdType.MESH

EPS = 1e-6
CHUNK = 64
NOPE, ROPE, VH = 128, 64, 128
ROPE_THETA = 10000.0
HP, NST = 64, 128
SSM_K, FFN_K = 4, 3
LANE = 128
N_CHIPS = 4
VMEM_LIMIT = 52 * 1024 * 1024
MM_TILE, MM_TILE_K = 1408, 2816

ADAM_LR, ADAM_B1, ADAM_B2, ADAM_EPS, ADAM_WD, ADAM_STEP = 0.001, 0.9, 0.999, 1e-08, 0.01, 10


class _Cfg:
    def __init__(self, S, D, QL, KVL, H, HS, G, DFF, T):
        self.S, self.D, self.QL, self.KVL, self.H, self.HS, self.G, self.DFF, self.T = S, D, QL, KVL, H, HS, G, DFF, T
        self.INNER = HS * HP
        self.CONVCH = self.INNER + 2 * G * NST
        self.QW = H * (NOPE + ROPE)
        self.KVW = H * (NOPE + VH)
        self.MLAW = H * VH
        self.MIXW = self.MLAW + self.INNER
        self.IN_COLS = QL + KVL + ROPE + self.INNER + self.CONVCH + HS
        natural, at = {}, 0
        for name, w in (("c_q", QL), ("c_kv", KVL), ("kr", ROPE), ("z", self.INNER), ("xbc", self.CONVCH), ("dt", HS)):
            natural[name] = (at, w)
            at += w
        self.seg, taken = {}, []
        for name in sorted(natural, key=lambda n: -natural[n][1]):
            w = -(-natural[name][1] // LANE) * LANE
            off = next(o for o in range(0, self.IN_COLS * 2, w) if all(o + w <= t or o >= t + tw for t, tw in taken))
            taken.append((off, w))
            self.seg[name] = (off, w) + natural[name]
        self.EXT = max(o + w for o, w in taken)
        self.NPAIR = HS // 2
        self.REP = HS // G

    def window(self, name):
        off, w, _, _ = self.seg[name]
        return w, off // w


_FULL = _Cfg(S=2048, D=2048, QL=768, KVL=512, H=8, HS=16, G=2, DFF=5632, T=256)
BIG = ("w_in", "w_uq", "w_ukv", "w_out", "w_gate", "w_up", "w_down")

SMALL = ("mix_pre_g", "q_norm_g", "kv_norm_g", "ssm_conv_w", "ssm_conv_b", "dt_bias", "a_log", "d_skip", "ssm_norm_g",
         "mix_post_g", "ffn_pre_g", "ffn_conv_w", "ffn_conv_b", "ffn_post_g")
SMALL_SHARDED = ("ssm_conv_w", "ffn_conv_w")
WEIGHTS = ("mix_pre_g", "w_in", "q_norm_g", "w_uq", "kv_norm_g", "w_ukv", "ssm_conv_w", "ssm_conv_b", "dt_bias", "a_log",
           "d_skip", "ssm_norm_g", "w_out", "mix_post_g", "ffn_pre_g", "w_gate", "w_up", "ffn_conv_w", "ffn_conv_b",
           "w_down", "ffn_post_g")


def _pick(n, target, mult):
    best = None
    for d in range(mult, min(n, target) + 1, mult):
        if n % d == 0:
            best = d
    return best if best is not None else n


def _params(sem=None):
    kw = dict(vmem_limit_bytes=VMEM_LIMIT)
    if sem is not None:
        kw["dimension_semantics"] = sem
    return pltpu.CompilerParams(**kw)


def _dot(a, b, dims=NN, precision=None):
    return lax.dot_general(a, b, dims, preferred_element_type=F32, precision=precision)


def _sigmoid(x):
    return 1.0 / (1.0 + jnp.exp(-x))


def _rs(x):
    return lax.rsqrt(jnp.mean(x * x, axis=-1, keepdims=True) + EPS)


def _rms_back(xh, r, dn):
    return r * (dn - xh * jnp.mean(dn * xh, axis=-1, keepdims=True))


def _colsum(v):
    return jnp.sum(v, axis=0, keepdims=True)


def _matmul(name, a, b, mode, out_dtype, a2=None, b2=None, chips=False, after=None):
    cs = None
    if mode == "nn":
        (M, K), N = a.shape, b.shape[-1]
        if chips:
            cs, N = N, N_CHIPS * N
    elif mode == "nt":
        (M, K), N = a.shape, b.shape[-2]
        if chips:
            cs = b.shape[-1]
    else:
        (K, M), N = a.shape, b.shape[1]
        if chips:
            cs = N // N_CHIPS
    tm = _pick(M, MM_TILE, LANE)
    tn = _pick(cs if chips and mode != "nt" else N, MM_TILE, LANE)
    tk = _pick(cs, MM_TILE, LANE) if chips and mode == "nt" else _pick(K, MM_TILE_K, LANE)
    nk = K // tk
    dims = {"nn": NN, "nt": NT, "tn": TN}[mode]
    a_spec = pl.BlockSpec((tk, tm), lambda i, j, k: (k, i)) if mode == "tn" else pl.BlockSpec((tm, tk), lambda i, j, k: (i, k))
    b_spec = pl.BlockSpec((tn, tk), lambda i, j, k: (j, k)) if mode == "nt" else pl.BlockSpec((tk, tn), lambda i, j, k: (k, j))
    o_spec = pl.BlockSpec((tm, tn), lambda i, j, k: (i, j))
    o_shape = (M, N)
    if chips and mode == "nn":
        per = cs // tn
        b_spec = pl.BlockSpec((None, tk, tn), lambda i, j, k: (j // per, k, j % per))
    elif chips and mode == "nt":
        per = cs // tk
        b_spec = pl.BlockSpec((None, tn, tk), lambda i, j, k: (k // per, j, k % per))
    elif chips:
        per = cs // tn
        o_spec = pl.BlockSpec((None, tm, tn), lambda i, j, k: (j // per, i, j % per))
        o_shape = (N_CHIPS, M, cs)
    two = a2 is not None

    def product(refs):
        part = _dot(refs[0][...].astype(BF16), refs[1][...].astype(BF16), dims)
        if two:
            part += _dot(refs[2][...].astype(BF16), refs[3][...].astype(BF16), dims)
        return part

    def body_whole_k(*refs):
        refs[-1][...] = product(refs).astype(refs[-1].dtype)

    def body(*refs):
        o_ref, acc_ref = refs[-2], refs[-1]
        k = pl.program_id(2)

        @pl.when(k == 0)
        def _():
            acc_ref[...] = product(refs)

        @pl.when(k > 0)
        def _():
            acc_ref[...] += product(refs)

        @pl.when(k == nk - 1)
        def _():
            o_ref[...] = acc_ref[...].astype(o_ref.dtype)

    ins = ((a, b, a2, b2) if two else (a, b)) + (() if after is None else (after,))
    return pl.pallas_call(
        body_whole_k if nk == 1 else body, name=name, grid=(M // tm, N // tn, nk),
        in_specs=[a_spec, b_spec] * (2 if two else 1) + ([] if after is None else [pl.BlockSpec(memory_space=pl.ANY)]),
        out_specs=o_spec,
        out_shape=jax.ShapeDtypeStruct(o_shape, out_dtype),
        scratch_shapes=[] if nk == 1 else [pltpu.VMEM((tm, tn), F32)],
        compiler_params=_params(("parallel", "parallel", "arbitrary")),
    )(*ins)


def _window(a):
    return (a[0], *a[1]) if isinstance(a, tuple) else (a, a.shape[1], 0)


def _rowwise(name, fn, rows, mats, outs, reds, ts):
    rows, widths, blocks = zip(*[_window(a) for a in rows])
    S = rows[0].shape[0]
    nr, nm, no = len(rows), len(mats), len(outs)

    def body(*refs):
        res = fn(*[r[...] for r in refs[:nr + nm]])
        res = res if isinstance(res, (tuple, list)) else (res,)
        for r, v in zip(refs[nr + nm:nr + nm + no], res[:no]):
            r[...] = v.astype(r.dtype)
        first = pl.program_id(0) == 0
        for r, v in zip(refs[nr + nm + no:], res[no:]):
            @pl.when(first)
            def _():
                r[...] = jnp.broadcast_to(v, r.shape)

            @pl.when(jnp.logical_not(first))
            def _():
                r[...] += jnp.broadcast_to(v, r.shape)

    in_specs = [pl.BlockSpec((ts, w), lambda i, b=b: (i, b)) for w, b in zip(widths, blocks)]
    in_specs += [pl.BlockSpec(m.shape, lambda i, nd=m.ndim: (0,) * nd) for m in mats]
    out_specs = [pl.BlockSpec((ts, w), lambda i: (i, 0)) for w, _ in outs]
    out_specs += [pl.BlockSpec(s, lambda i: (0, 0)) for s in reds]
    out_shape = [jax.ShapeDtypeStruct((S, w), dt) for w, dt in outs] + [jax.ShapeDtypeStruct(s, F32) for s in reds]
    return pl.pallas_call(
        body, name=name, grid=(S // ts,), in_specs=in_specs, out_specs=out_specs, out_shape=out_shape,
        compiler_params=_params(("arbitrary",) if reds else ("parallel",)),
    )(*rows, *mats)


def _shift_down(v, s):
    if s == 0:
        return v
    rows = lax.broadcasted_iota(I32, v.shape, 0)
    return jnp.where(rows >= s, pltpu.roll(v, s, 0), 0.0)


def _shift_up(v, s):
    if s == 0:
        return v
    n = v.shape[0]
    rows = lax.broadcasted_iota(I32, v.shape, 0)
    return jnp.where(rows < n - s, pltpu.roll(v, n - s, 0), 0.0)


def _conv(x, w, b):
    K = w.shape[0]
    y = jnp.broadcast_to(b, x.shape)
    for k in range(K):
        y = y + w[k:k + 1, :] * _shift_down(x, K - 1 - k)
    return y


def _conv_back(x, w, dc):
    K = w.shape[0]
    dx = jnp.zeros_like(x)
    dw = []
    for k in range(K):
        up = _shift_up(dc, K - 1 - k)
        dx = dx + w[k:k + 1, :] * up
        dw.append(_colsum(up * x))
    return dx, jnp.concatenate(dw, axis=0), _colsum(dc)


def _colwise(name, fn, cols, vecs, outs, pouts, tc):
    cols, widths, blocks = zip(*[_window(a) for a in cols])
    S, C = cols[0].shape[0], widths[0]
    firsts = [b * (C // tc) for b in blocks]
    nc_, nv, no = len(cols), len(vecs), len(outs)

    def body(*refs):
        res = fn(*[r[...] for r in refs[:nc_ + nv]])
        res = res if isinstance(res, (tuple, list)) else (res,)
        for r, v in zip(refs[nc_ + nv:], res):
            r[...] = v.astype(r.dtype)

    in_specs = [pl.BlockSpec((S, tc), lambda j, f=f: (0, f + j)) for f in firsts]
    in_specs += [pl.BlockSpec((v.shape[0], tc), lambda j: (0, j)) for v in vecs]
    out_specs = [pl.BlockSpec((S, tc), lambda j: (0, j)) for _ in outs] + [pl.BlockSpec((k, tc), lambda j: (0, j)) for k in pouts]
    out_shape = [jax.ShapeDtypeStruct((S, C), dt) for dt in outs] + [jax.ShapeDtypeStruct((k, C), F32) for k in pouts]
    return pl.pallas_call(
        body, name=name, grid=(C // tc,), in_specs=in_specs, out_specs=out_specs, out_shape=out_shape,
        compiler_params=_params(("parallel",)),
    )(*cols, *vecs)


_G0, _G1 = math.sqrt(2.0 / math.pi), 0.044715


def _gelu(g):
    th = jnp.tanh(_G0 * (g + _G1 * g * g * g))
    return 0.5 * g * (1.0 + th), th


def _ffn_act(gate_pre, up, w, b):
    act, _ = _gelu(_conv(gate_pre, w, b))
    return act * up


def _ffn_act_back(dact, gate_pre, up, w, b):
    g = _conv(gate_pre, w, b)
    ge, th = _gelu(g)
    dge = 0.5 * (1.0 + th) + 0.5 * g * (1.0 - th * th) * _G0 * (1.0 + 3.0 * _G1 * g * g)
    dup = dact * ge
    dgate_pre, dw, db = _conv_back(gate_pre, w, dact * up * dge)
    return dgate_pre, dup, dw, db


def _ssm_act(xbc, w, b):
    c = _conv(xbc, w, b)
    return c * _sigmoid(c)


def _ssm_act_back(dxc, xbc, w, b):
    c = _conv(xbc, w, b)
    sg = _sigmoid(c)
    return _conv_back(xbc, w, dxc * sg * (1.0 + c * (1.0 - sg)))


def _rope_tables(S):
    inv = 1.0 / (ROPE_THETA ** (jnp.arange(0, ROPE, 2, dtype=F32) / ROPE))
    ang = jnp.arange(S, dtype=F32)[:, None] * inv[None, :]
    cos, sin = jnp.cos(ang), jnp.sin(ang)
    return jnp.tile(cos, (1, 4)), jnp.tile(jnp.concatenate([-sin, sin], axis=1), (1, 2))


def _swap_halves(x):
    lane = lax.broadcasted_iota(I32, x.shape, 1)
    w = x.shape[1]
    return jnp.where((lane % ROPE) < ROPE // 2, pltpu.roll(x, w - ROPE // 2, 1), pltpu.roll(x, ROPE // 2, 1))


def _rot(x, cos2, sin2):
    return x * cos2 + _swap_halves(x) * sin2


def _rot_back(dy, cos2, sin2):
    return dy * cos2 + _swap_halves(dy * sin2)


def _mla_pack(cfg, q, kv, kr, cos2, sin2):
    S, H = cfg.S, cfg.H
    ts = _pick(S, 256, 8)
    kr, _, kr_block = _window(kr)

    def body(q_ref, kv_ref, kr_ref, c_ref, s_ref, Q_ref, K_ref, V_ref):
        c2, s2 = c_ref[...], s_ref[...]
        krr = _rot(kr_ref[...], c2, s2)
        kr_half = (krr.astype(BF16), pltpu.roll(krr, ROPE, 1).astype(BF16))
        for j in range(H // 2):
            qr = _rot(q_ref[:, (H + j) * LANE:(H + j + 1) * LANE], c2, s2).astype(BF16)
            for h in (2 * j, 2 * j + 1):
                Q_ref[h, :, 0:LANE] = q_ref[:, h * LANE:(h + 1) * LANE].astype(BF16)
                Q_ref[h, :, LANE:] = qr
                K_ref[h, :, 0:LANE] = kv_ref[:, h * LANE:(h + 1) * LANE].astype(BF16)
                K_ref[h, :, LANE:] = kr_half[h % 2]
                V_ref[h] = kv_ref[:, (H + h) * LANE:(H + h + 1) * LANE].astype(BF16)

    tab = pl.BlockSpec((ts, LANE), lambda i: (i, 0))
    heads = lambda w: pl.BlockSpec((H, ts, w), lambda i: (0, i, 0))
    return pl.pallas_call(
        body, name="mla_pack", grid=(S // ts,),
        in_specs=[pl.BlockSpec((ts, cfg.QW), lambda i: (i, 0)), pl.BlockSpec((ts, cfg.KVW), lambda i: (i, 0)),
                  pl.BlockSpec((ts, LANE), lambda i: (i, kr_block)), tab, tab],
        out_specs=[heads(2 * LANE), heads(2 * LANE), heads(LANE)],
        out_shape=[jax.ShapeDtypeStruct((H, S, 2 * LANE), BF16), jax.ShapeDtypeStruct((H, S, 2 * LANE), BF16),
                   jax.ShapeDtypeStruct((H, S, LANE), BF16)],
        compiler_params=_params(("parallel",)),
    )(q, kv, kr, cos2, sin2)


def _mla_unpack(cfg, dQ, dK, dV, cos2, sin2):
    S, H = cfg.S, cfg.H
    ts = _pick(S, 256, 8)

    def body(dQ_ref, dK_ref, dV_ref, c_ref, s_ref, dq_ref, dkv_ref, dkr_ref):
        c2, s2 = c_ref[...], s_ref[...]
        lo = lax.broadcasted_iota(I32, (ts, LANE), 1) < ROPE
        tk = jnp.zeros((ts, LANE), F32)
        for h in range(H):
            dq_ref[:, h * LANE:(h + 1) * LANE] = dQ_ref[h, :, 0:LANE].astype(BF16)
            dkv_ref[:, h * LANE:(h + 1) * LANE] = dK_ref[h, :, 0:LANE].astype(BF16)
            dkv_ref[:, (H + h) * LANE:(H + h + 1) * LANE] = dV_ref[h].astype(BF16)
            own = lo if h % 2 == 0 else jnp.logical_not(lo)
            tk = tk + jnp.where(own, dK_ref[h, :, LANE:], 0.0)
        for j in range(H // 2):
            dr = dQ_ref[2 * j, :, LANE:] + dQ_ref[2 * j + 1, :, LANE:]
            dq_ref[:, (H + j) * LANE:(H + j + 1) * LANE] = _rot_back(dr, c2, s2).astype(BF16)
        dkr_rot = jnp.where(lo, tk + pltpu.roll(tk, ROPE, 1), 0.0)
        dkr_ref[...] = _rot_back(dkr_rot, c2, s2).astype(BF16)

    tab = pl.BlockSpec((ts, LANE), lambda i: (i, 0))
    return pl.pallas_call(
        body, name="mla_unpack", grid=(S // ts,),
        in_specs=[pl.BlockSpec((H, ts, 2 * LANE), lambda i: (0, i, 0)), pl.BlockSpec((H, ts, 2 * LANE), lambda i: (0, i, 0)),
                  pl.BlockSpec((H, ts, LANE), lambda i: (0, i, 0)), tab, tab],
        out_specs=[pl.BlockSpec((ts, cfg.QW), lambda i: (i, 0)), pl.BlockSpec((ts, cfg.KVW), lambda i: (i, 0)), tab],
        out_shape=[jax.ShapeDtypeStruct((S, cfg.QW), BF16), jax.ShapeDtypeStruct((S, cfg.KVW), BF16),
                   jax.ShapeDtypeStruct((S, LANE), BF16)],
        compiler_params=_params(("parallel",)),
    )(dQ, dK, dV, cos2, sin2)


_ATT_T = 256
_ATT_HB = 8
_ATT_SCALE = (NOPE + ROPE) ** -0.5


def _diag_mask(transposed=False):
    r = lax.broadcasted_iota(I32, (_ATT_T, _ATT_T), 0) // CHUNK
    c = lax.broadcasted_iota(I32, (_ATT_T, _ATT_T), 1) // CHUNK
    return r <= c if transposed else c <= r


def _row_form(col):
    return jnp.broadcast_to(col, (col.shape[0], LANE)).T[0:8, :]


def _attn_fwd(cfg, Q, K, V):
    S, H, T, HB = cfg.S, cfg.H, _ATT_T, min(cfg.H, _ATT_HB)

    def body(q_ref, k_ref, v_ref, o_ref, lse_ref, lse_t_ref):
        qi = pl.program_id(1)

        def head_step(b, kb, carry, mask):
            m, l, acc = carry
            ks = pl.multiple_of(kb * T, T)
            s = _dot(q_ref[b], k_ref[b, pl.ds(ks, T), :], NT) * _ATT_SCALE
            if mask is not None:
                s = jnp.where(mask, s, -1e30)
            m_new = jnp.maximum(m, jnp.max(s, axis=1, keepdims=True))
            p = jnp.exp(s - m_new)
            alpha = jnp.exp(m - m_new)
            l = alpha * l + jnp.sum(p, axis=1, keepdims=True)
            acc = alpha * acc + _dot(p.astype(BF16), v_ref[b, pl.ds(ks, T), :])
            return m_new, l, acc

        def step(kb, carry, mask=None):
            return tuple(head_step(b, kb, carry[b], mask) for b in range(HB))

        init = (jnp.full((T, 1), -1e30, F32), jnp.zeros((T, 1), F32), jnp.zeros((T, VH), F32))
        done = step(qi, lax.fori_loop(0, qi, step, (init,) * HB), _diag_mask())
        for b, (m, l, acc) in enumerate(done):
            o_ref[:, b * LANE:(b + 1) * LANE] = acc / l
            lse = m + jnp.log(l)
            lse_ref[:, b * LANE:(b + 1) * LANE] = jnp.broadcast_to(lse, (T, LANE))
            lse_t_ref[b] = _row_form(lse)

    return pl.pallas_call(
        body, name="attn_fwd", grid=(H // HB, S // T),
        in_specs=[pl.BlockSpec((HB, T, 2 * LANE), lambda h, i: (h, i, 0)), pl.BlockSpec((HB, S, 2 * LANE), lambda h, i: (h, 0, 0)),
                  pl.BlockSpec((HB, S, LANE), lambda h, i: (h, 0, 0))],
        out_specs=[pl.BlockSpec((T, HB * LANE), lambda h, i: (i, h)), pl.BlockSpec((T, HB * LANE), lambda h, i: (i, h)),
                   pl.BlockSpec((HB, 8, T), lambda h, i: (h, 0, i))],
        out_shape=[jax.ShapeDtypeStruct((S, H * LANE), F32), jax.ShapeDtypeStruct((S, H * LANE), F32),
                   jax.ShapeDtypeStruct((H, 8, S), F32)],
        compiler_params=_params(("parallel", "parallel")),
    )(Q, K, V)


def _attn_dq(cfg, Q, K, V, do, o, lse, after):
    S, H, T, HB = cfg.S, cfg.H, _ATT_T, min(cfg.H, _ATT_HB)

    def body(q_ref, k_ref, v_ref, do_ref, o_ref, lse_ref, after_ref, dq_ref, dl_t_ref):
        qi = pl.program_id(1)
        do = [do_ref[:, b * LANE:(b + 1) * LANE] for b in range(HB)]
        delta = [jnp.sum(do[b] * o_ref[:, b * LANE:(b + 1) * LANE], axis=1, keepdims=True) for b in range(HB)]
        dob = [d.astype(BF16) for d in do]

        def head_step(b, kb, dq, mask):
            ks = pl.multiple_of(kb * T, T)
            k = k_ref[b, pl.ds(ks, T), :]
            s = _dot(q_ref[b], k, NT) * _ATT_SCALE
            if mask is not None:
                s = jnp.where(mask, s, -1e30)
            p = jnp.exp(s - lse_ref[:, b * LANE:b * LANE + 1])
            dp = _dot(dob[b], v_ref[b, pl.ds(ks, T), :], NT)
            ds = p * (dp - delta[b]) * _ATT_SCALE
            return dq + _dot(ds.astype(BF16), k)

        def step(kb, dqs, mask=None):
            return tuple(head_step(b, kb, dqs[b], mask) for b in range(HB))

        dqs = step(qi, lax.fori_loop(0, qi, step, (jnp.zeros((T, 2 * LANE), F32),) * HB), _diag_mask())
        for b in range(HB):
            dq_ref[b] = dqs[b]
            dl_t_ref[b] = _row_form(delta[b])

    col = pl.BlockSpec((T, HB * LANE), lambda h, i: (i, h))
    return pl.pallas_call(
        body, name="attn_dq", grid=(H // HB, S // T),
        in_specs=[pl.BlockSpec((HB, T, 2 * LANE), lambda h, i: (h, i, 0)), pl.BlockSpec((HB, S, 2 * LANE), lambda h, i: (h, 0, 0)),
                  pl.BlockSpec((HB, S, LANE), lambda h, i: (h, 0, 0)), col, col, col, _ANY],
        out_specs=[pl.BlockSpec((HB, T, 2 * LANE), lambda h, i: (h, i, 0)), pl.BlockSpec((HB, 8, T), lambda h, i: (h, 0, i))],
        out_shape=[jax.ShapeDtypeStruct((H, S, 2 * LANE), F32), jax.ShapeDtypeStruct((H, 8, S), F32)],
        compiler_params=_params(("parallel", "parallel")),
    )(Q, K, V, do, o, lse, after)


def _attn_dkv(cfg, Q, K, V, do, lse_t, delta_t):
    S, H, T, HB = cfg.S, cfg.H, _ATT_T, min(cfg.H, _ATT_HB)
    nq = S // T

    def body(q_ref, k_ref, v_ref, do_ref, lse_ref, dl_ref, dk_ref, dv_ref):
        kb = pl.program_id(1)

        def head_step(b, qi, carry, mask):
            dk, dv = carry
            qs = pl.multiple_of(qi * T, T)
            q = q_ref[b, pl.ds(qs, T), :]
            dob = do_ref[pl.ds(qs, T), b * LANE:(b + 1) * LANE].astype(BF16)
            s = _dot(k_ref[b], q, NT) * _ATT_SCALE
            if mask is not None:
                s = jnp.where(mask, s, -1e30)
            p = jnp.exp(s - lse_ref[b, 0:1, pl.ds(qs, T)])
            dv = dv + _dot(p.astype(BF16), dob)
            dp = _dot(v_ref[b], dob, NT)
            ds = p * (dp - dl_ref[b, 0:1, pl.ds(qs, T)]) * _ATT_SCALE
            dk = dk + _dot(ds.astype(BF16), q)
            return dk, dv

        def step(qi, carry, mask=None):
            return tuple(head_step(b, qi, carry[b], mask) for b in range(HB))

        zero = (jnp.zeros((T, 2 * LANE), F32), jnp.zeros((T, VH), F32))
        done = lax.fori_loop(kb + 1, nq, step, step(kb, (zero,) * HB, _diag_mask(transposed=True)))
        for b, (dk, dv) in enumerate(done):
            dk_ref[b] = dk
            dv_ref[b] = dv

    row = pl.BlockSpec((HB, 8, S), lambda h, j: (h, 0, 0))
    return pl.pallas_call(
        body, name="attn_dkv", grid=(H // HB, S // T),
        in_specs=[pl.BlockSpec((HB, S, 2 * LANE), lambda h, j: (h, 0, 0)), pl.BlockSpec((HB, T, 2 * LANE), lambda h, j: (h, j, 0)),
                  pl.BlockSpec((HB, T, LANE), lambda h, j: (h, j, 0)), pl.BlockSpec((S, HB * LANE), lambda h, j: (0, h)), row, row],
        out_specs=[pl.BlockSpec((HB, T, 2 * LANE), lambda h, j: (h, j, 0)), pl.BlockSpec((HB, T, LANE), lambda h, j: (h, j, 0))],
        out_shape=[jax.ShapeDtypeStruct((H, S, 2 * LANE), F32), jax.ShapeDtypeStruct((H, S, LANE), F32)],
        compiler_params=_params(("parallel", "parallel")),
    )(Q, K, V, do, lse_t, delta_t)


def _expand_matrix(cfg):
    r = lax.broadcasted_iota(I32, (LANE, cfg.INNER), 0)
    c = lax.broadcasted_iota(I32, (LANE, cfg.INNER), 1)
    return (r == c // HP).astype(F32)


def _softplus(x):
    return jnp.maximum(x, 0.0) + jnp.log(1.0 + jnp.exp(-jnp.abs(x)))


def _ssd_prep(cfg, dt_raw, dt_bias_pad, a_log_pad, expand):
    HS = cfg.HS

    def fn(raw, bias, alog, E):
        heads = lax.broadcasted_iota(I32, raw.shape, 1) < HS
        dt = jnp.where(heads, _softplus(raw + bias), 0.0)
        a = dt * jnp.where(heads[0:1], -jnp.exp(alog), 0.0)
        return dt, a, _dot(dt, E, precision=HI)

    return _rowwise("ssd_prep", fn, [dt_raw], [dt_bias_pad, a_log_pad, expand],
                    [(LANE, F32), (LANE, F32), (cfg.INNER, F32)], [], _pick(cfg.S, 512, 8))


def _tril(T):
    return lax.broadcasted_iota(I32, (T, T), 0) >= lax.broadcasted_iota(I32, (T, T), 1)


def _ssd_fwd(cfg, xc, dt_exp, a_small, dskip_exp, expand):
    S, T, INNER, G, NPAIR = cfg.S, cfg.T, cfg.INNER, cfg.G, cfg.NPAIR
    NC = S // T

    def body(xc_ref, dte_ref, as_ref, dsk_ref, e_ref, y_ref, hin_ref, ht_ref):
        @pl.when(pl.program_id(0) == 0)
        def _():
            ht_ref[...] = jnp.zeros_like(ht_ref)

        tril = _tril(T)
        tri = tril.astype(F32)
        acs_s = _dot(tri, as_ref[...], precision=HI)
        acs_e = _dot(acs_s, e_ref[...], precision=HI)
        acs_t = acs_s.T
        lo = lax.broadcasted_iota(I32, (T, LANE), 1) < HP
        for g in range(G):
            Bb = xc_ref[:, INNER + g * NST:INNER + (g + 1) * NST].astype(BF16)
            Cb = xc_ref[:, INNER + (G + g) * NST:INNER + (G + g + 1) * NST].astype(BF16)
            Gm = _dot(Cb, Bb, NT)
            for j in range(g * NPAIR // G, (g + 1) * NPAIR // G):
                sl = slice(j * LANE, (j + 1) * LANE)
                Xp = xc_ref[:, sl]
                Xdt = Xp * dte_ref[:, sl]
                Xb = Xdt.astype(BF16)
                acs_p = acs_e[:, sl]
                last = acs_p[T - 1:T, :]
                Hin = ht_ref[j]
                hin_ref[0, j] = Hin
                yd = []
                for e in (0, 1):
                    h = 2 * j + e
                    Lm = jnp.exp(jnp.where(tril, acs_s[:, h:h + 1] - acs_t[h:h + 1, :], -1e30))
                    yd.append(_dot((Gm * Lm).astype(BF16), Xb))
                y_off = _dot(Cb, Hin.astype(BF16)) * jnp.exp(acs_p)
                y_ref[:, sl] = jnp.where(lo, yd[0], yd[1]) + y_off + Xp * dsk_ref[:, sl]
                st = _dot(Bb, (Xdt * jnp.exp(last - acs_p)).astype(BF16), TN)
                ht_ref[j] = jnp.exp(last) * Hin + st

    rows = lambda w: pl.BlockSpec((T, w), lambda c: (c, 0))
    return pl.pallas_call(
        body, name="ssd_fwd", grid=(NC,),
        in_specs=[rows(cfg.CONVCH), rows(INNER), rows(LANE), pl.BlockSpec((1, INNER), lambda c: (0, 0)),
                  pl.BlockSpec((LANE, INNER), lambda c: (0, 0))],
        out_specs=[rows(INNER), pl.BlockSpec((1, NPAIR, NST, LANE), lambda c: (c, 0, 0, 0))],
        out_shape=[jax.ShapeDtypeStruct((S, INNER), F32), jax.ShapeDtypeStruct((NC, NPAIR, NST, LANE), F32)],
        scratch_shapes=[pltpu.VMEM((NPAIR, NST, LANE), F32)],
        compiler_params=_params(("arbitrary",)),
    )(xc, dt_exp, a_small, dskip_exp, expand)


def _ssd_bwd(cfg, dy, xc, dt_exp, a_small, dskip_exp, hin, dt_raw, dt_bias_pad, a_log_pad, expand):
    S, T, INNER, G, NPAIR, HS = cfg.S, cfg.T, cfg.INNER, cfg.G, cfg.NPAIR, cfg.HS
    NC = S // T

    def body(dy_ref, xc_ref, dte_ref, as_ref, dsk_ref, hin_ref, raw_ref, bias_ref, alog_ref, e_ref,
             dxc_ref, draw_ref, dbias_ref, dalog_ref, dskip_ref, dht_ref, cols_ref, rows_ref, dacs_ref, ddt_ref):
        first = pl.program_id(0) == 0

        @pl.when(first)
        def _():
            dht_ref[...] = jnp.zeros_like(dht_ref)

        tril = _tril(T)
        tri = tril.astype(F32)
        a_s = as_ref[...]
        acs_s = _dot(tri, a_s, precision=HI)
        acs_e = _dot(acs_s, e_ref[...], precision=HI)
        acs_t = acs_s.T
        lo = lax.broadcasted_iota(I32, (T, LANE), 1) < HP
        last_row = lax.broadcasted_iota(I32, (T, LANE), 0) == T - 1
        cols_ref[...] = jnp.zeros_like(cols_ref)
        rows_ref[...] = jnp.zeros_like(rows_ref)
        dsk_parts = []
        for g in range(G):
            bsl = slice(INNER + g * NST, INNER + (g + 1) * NST)
            csl = slice(INNER + (G + g) * NST, INNER + (G + g + 1) * NST)
            Bb = xc_ref[:, bsl].astype(BF16)
            Cb = xc_ref[:, csl].astype(BF16)
            Gm = _dot(Cb, Bb, NT)
            dG = jnp.zeros((T, T), F32)
            dB = jnp.zeros((T, NST), F32)
            dC = jnp.zeros((T, NST), F32)
            for j in range(g * NPAIR // G, (g + 1) * NPAIR // G):
                sl = slice(j * LANE, (j + 1) * LANE)
                Xp = xc_ref[:, sl]
                dtp = dte_ref[:, sl]
                Xdt = Xp * dtp
                Xb = Xdt.astype(BF16)
                acs_p = acs_e[:, sl]
                last = acs_p[T - 1:T, :]
                e_p, dec, cd = jnp.exp(acs_p), jnp.exp(last - acs_p), jnp.exp(last)
                Hin = hin_ref[0, j]
                Hb = Hin.astype(BF16)
                dHn = dht_ref[j]
                dHb = dHn.astype(BF16)
                dYp = dy_ref[:, sl]
                z = _dot(Cb, Hb)
                dz = (dYp * e_p).astype(BF16)
                dacs_p = dYp * z * e_p
                dC = dC + _dot(dz, Hb, NT)
                dHin = _dot(Cb, dz, TN) + cd * dHn
                dlast = _colsum(dHn * Hin) * cd
                qv = _dot(Bb, dHb)
                dXdt = qv * dec
                ddec = qv * Xdt * dec
                dacs_p = dacs_p - ddec
                dlast = dlast + _colsum(ddec)
                dB = dB + _dot((Xdt * dec).astype(BF16), dHb, NT)
                for e in (0, 1):
                    h = 2 * j + e
                    Lm = jnp.exp(jnp.where(tril, acs_s[:, h:h + 1] - acs_t[h:h + 1, :], -1e30))
                    Mh = Gm * Lm
                    dYe = jnp.where(lo if e == 0 else jnp.logical_not(lo), dYp, 0.0).astype(BF16)
                    dM = _dot(dYe, Xb, NT)
                    dXdt = dXdt + _dot(Mh.astype(BF16), dYe, TN)
                    W = dM * Mh
                    cols_ref[:, h:h + 1] = jnp.sum(W, axis=1, keepdims=True)
                    rows_ref[h:h + 1, :] = _colsum(W)
                    dG = dG + dM * Lm
                dacs_ref[:, sl] = dacs_p + jnp.where(last_row, dlast, 0.0)
                ddt_ref[:, sl] = dXdt * Xp
                dxc_ref[:, sl] = dXdt * dtp + dYp * dsk_ref[:, sl]
                dsk_parts.append(_colsum(dYp * Xp))
                dht_ref[j] = dHin
            dGb = dG.astype(BF16)
            dxc_ref[:, bsl] = dB + _dot(dGb, Cb, TN)
            dxc_ref[:, csl] = dC + _dot(dGb, Bb)
        E = e_ref[...]
        dacs_s = cols_ref[...] - rows_ref[...].T + _dot(dacs_ref[...], E, NT, precision=HI)
        da = _dot(tri, dacs_s, TN, precision=HI)
        heads = lax.broadcasted_iota(I32, (1, LANE), 1) < HS
        A = jnp.where(heads, -jnp.exp(alog_ref[...]), 0.0)
        ddt = _dot(ddt_ref[...], E, NT, precision=HI) + da * A
        draw = jnp.where(heads, ddt * _sigmoid(raw_ref[...] + bias_ref[...]), 0.0)
        draw_ref[...] = draw
        dsk = _dot(jnp.broadcast_to(jnp.concatenate(dsk_parts, axis=1), (8, INNER)), E, NT, precision=HI)[0:1]
        for ref, val in ((dbias_ref, _colsum(draw)), (dalog_ref, _colsum(da * a_s)), (dskip_ref, dsk)):
            @pl.when(first)
            def _():
                ref[...] = val

            @pl.when(jnp.logical_not(first))
            def _():
                ref[...] += val

    dt_raw, _, raw_block = _window(dt_raw)
    rows = lambda w, b=0: pl.BlockSpec((T, w), lambda c: (NC - 1 - c, b))
    vec = lambda w: pl.BlockSpec((1, w), lambda c: (0, 0))
    return pl.pallas_call(
        body, name="ssd_bwd", grid=(NC,),
        in_specs=[rows(INNER), rows(cfg.CONVCH), rows(INNER), rows(LANE), vec(INNER),
                  pl.BlockSpec((1, NPAIR, NST, LANE), lambda c: (NC - 1 - c, 0, 0, 0)), rows(LANE, raw_block), vec(LANE), vec(LANE),
                  pl.BlockSpec((LANE, INNER), lambda c: (0, 0))],
        out_specs=[rows(cfg.CONVCH), rows(LANE), vec(LANE), vec(LANE), vec(LANE)],
        out_shape=[jax.ShapeDtypeStruct((S, cfg.CONVCH), F32), jax.ShapeDtypeStruct((S, LANE), F32)]
        + [jax.ShapeDtypeStruct((1, LANE), F32)] * 3,
        scratch_shapes=[pltpu.VMEM((NPAIR, NST, LANE), F32), pltpu.VMEM((T, LANE), F32), pltpu.VMEM((LANE, T), F32),
                        pltpu.VMEM((T, INNER), F32), pltpu.VMEM((T, INNER), F32)],
        compiler_params=_params(("arbitrary",)),
    )(dy, xc, dt_exp, a_small, dskip_exp, hin, dt_raw, dt_bias_pad, a_log_pad, expand)


def _ssd_post(cfg, y, z, norm_g):
    W = cfg.INNER // cfg.G

    def fn(y, z, g):
        yz = y * z * _sigmoid(z)
        return jnp.concatenate([yz[:, i * W:(i + 1) * W] * _rs(yz[:, i * W:(i + 1) * W]) for i in range(cfg.G)], axis=1) * g

    return _rowwise("ssd_post", fn, [y, z], [norm_g], [(cfg.INNER, BF16)], [], _pick(cfg.S, 256, 8))[0]


def _ssd_post_bwd(cfg, db, y, z, norm_g):
    W = cfg.INNER // cfg.G

    def fn(db, y, z, g):
        sg = _sigmoid(z)
        yz = y * z * sg
        dn = db * g
        dyz, nh = [], []
        for i in range(cfg.G):
            seg = yz[:, i * W:(i + 1) * W]
            r = _rs(seg)
            nh.append(seg * r)
            dyz.append(_rms_back(nh[-1], r, dn[:, i * W:(i + 1) * W]))
        dyz = jnp.concatenate(dyz, axis=1)
        return dyz * z * sg, dyz * y * sg * (1.0 + z * (1.0 - sg)), _colsum(db * jnp.concatenate(nh, axis=1))

    return _rowwise("ssd_post_bwd", fn, [db, y, z], [norm_g], [(cfg.INNER, F32), (cfg.INNER, F32)], [(1, cfg.INNER)],
                    _pick(cfg.S, 256, 8))


def _rms_pre(cfg, x, g):
    return _rowwise("rms_pre", lambda x, g: x * _rs(x) * g, [x], [g], [(cfg.D, BF16)], [], _pick(cfg.S, 512, 8))[0]


def _local_grads(cfg, x, tgt, W, sp, mla_weights=None, out_weight=None, ffn_weights=None, down_weight=None,
                 ffn_grads_ready=None, early_grads_ready=None, in_grad_ready=None, xn=None, after_in=None):
    S, D, H, INNER = cfg.S, cfg.D, cfg.H, cfg.INNER
    ts = _pick(S, 256, 8)
    ts_few = _pick(S, 512, 8)
    tc = _CONV_COLS

    if xn is None:
        xn = _rms_pre(cfg, x, sp["mix_pre_g"])
    u = _matmul("mm_in", xn, W["w_in"], "nt", F32, after=after_in)
    c_q, c_kv, kr, z, xbc, dt_raw = [(u, cfg.window(n)) for n in ("c_q", "c_kv", "kr", "z", "xbc", "dt")]

    if mla_weights is not None:
        sp = dict(sp, q_norm_g=sp["q_norm_g"] + mla_weights.pass_on(u)[0, 0])
    cqn = _rowwise("rms_q", lambda x, g: x * _rs(x) * g, [c_q], [sp["q_norm_g"]], [(cfg.QL, BF16)], [], ts)[0]
    ckvn = _rowwise("rms_kv", lambda x, g: x * _rs(x) * g, [c_kv], [sp["kv_norm_g"]], [(cfg.KVL, BF16)], [], ts)[0]
    if mla_weights is not None:
        W = dict(W, **mla_weights.arrived(ckvn))
    q = _matmul("mm_uq", cqn, W["w_uq"], "nn", F32)
    kv = _matmul("mm_ukv", ckvn, W["w_ukv"], "nn", F32)
    cos2, sin2 = _rope_tables(S)
    Qh, Kh, Vh = _mla_pack(cfg, q, kv, kr, cos2, sin2)
    a_out, lse, lse_t = _attn_fwd(cfg, Qh, Kh, Vh)
    if out_weight is not None:
        sp = dict(sp, ssm_conv_b=sp["ssm_conv_b"] + out_weight.pass_on(a_out)[0, 0])

    pad = lambda v: jnp.pad(v, ((0, 0), (0, LANE - v.shape[1])))
    expand = _expand_matrix(cfg)
    dt_bias_pad, a_log_pad = pad(sp["dt_bias"]), pad(sp["a_log"])
    dskip_exp = jnp.repeat(sp["d_skip"], HP, axis=1)
    xc = _colwise("ssm_act", _ssm_act, [xbc], [sp["ssm_conv_w"], sp["ssm_conv_b"]], [F32], [], tc)[0]
    dt_s, a_s, dt_exp = _ssd_prep(cfg, dt_raw, dt_bias_pad, a_log_pad, expand)
    y_ssd, hin = _ssd_fwd(cfg, xc, dt_exp, a_s, dskip_exp, expand)
    b_out = _ssd_post(cfg, y_ssd, z, sp["ssm_norm_g"])

    ab_out = jnp.concatenate([a_out.astype(BF16), b_out], axis=1)
    if out_weight is not None:
        W = dict(W, **out_weight.arrived(ab_out))
    if ffn_weights is not None:
        sp = dict(sp, mix_post_g=sp["mix_post_g"] + ffn_weights.pass_on(ab_out)[0, 0])
    mix = _matmul("mm_out", ab_out, W["w_out"], "nn", F32)

    def mid(x, mix, g_mp, g_fp):
        x1 = x + mix * _rs(mix) * g_mp
        return x1, x1 * _rs(x1) * g_fp

    x1, h2 = _rowwise("fwd_mid", mid, [x, mix], [sp["mix_post_g"], sp["ffn_pre_g"]], [(D, F32), (D, BF16)], [], ts_few)
    if ffn_weights is not None:
        W = dict(W, **ffn_weights.arrived(h2))
    gate_pre = _matmul("mm_gate", h2, W["w_gate"], "nn", F32, chips=True)
    if down_weight is not None:
        sp = dict(sp, ffn_conv_b=sp["ffn_conv_b"] + down_weight.pass_on(gate_pre)[0, 0])
    up = _matmul("mm_up", h2, W["w_up"], "nn", F32, chips=True)
    act = _colwise("ffn_act", _ffn_act, [gate_pre, up], [sp["ffn_conv_w"], sp["ffn_conv_b"]], [BF16], [], tc)[0]
    if down_weight is not None:
        W = dict(W, **down_weight.arrived(act))
    f = _matmul("mm_down", act, W["w_down"], "nn", F32)

    def final(x1, f, t, g):
        r = _rs(f)
        fh = f * r
        err = x1 + fh * g - t
        loss = 0.5 * jnp.sum(jnp.mean(err * err, axis=-1, keepdims=True), axis=0, keepdims=True)
        dy = err * (1.0 / D)
        return dy, _rms_back(fh, r, dy * g), _colsum(dy * fh), loss

    dy, df, g_ffn_post, loss = _rowwise("final", final, [x1, f, tgt], [sp["ffn_post_g"]], [(D, F32), (D, BF16)],
                                        [(1, D), (1, LANE)], ts_few)
    gW = {}
    dact = _matmul("mm_down_dx", df, W["w_down"], "nt", F32)
    gW["w_down"] = _matmul("mm_down_dw", act, df, "tn", BF16)
    dgate, dup, g_ffn_conv_w, g_ffn_conv_b = _colwise(
        "ffn_act_bwd", _ffn_act_back, [dact, gate_pre, up], [sp["ffn_conv_w"], sp["ffn_conv_b"]], [BF16, BF16], [FFN_K, 1], tc)
    gW["w_gate"] = _matmul("mm_gate_dw", h2, dgate, "tn", BF16, chips=True)
    gW["w_up"] = _matmul("mm_up_dw", h2, dup, "tn", BF16, chips=True)
    if ffn_grads_ready is not None:
        sp = dict(sp, ffn_pre_g=sp["ffn_pre_g"] + ffn_grads_ready({n: gW[n] for n in ("w_down", "w_gate", "w_up")})[0, 0])
    dh2 = _matmul("mm_gu_dx", dgate, W["w_gate"], "nt", F32, dup, W["w_up"], chips=True)

    def mid_back(dy, dh2, x1, mix, g_mp, g_fp):
        r2 = _rs(x1)
        xh = x1 * r2
        dx1 = dy + _rms_back(xh, r2, dh2 * g_fp)
        r1 = _rs(mix)
        mh = mix * r1
        return dx1, _rms_back(mh, r1, dx1 * g_mp), _colsum(dh2 * xh), _colsum(dx1 * mh)

    dx1, dmix, g_ffn_pre, g_mix_post = _rowwise("bwd_mid", mid_back, [dy, dh2, x1, mix], [sp["mix_post_g"], sp["ffn_pre_g"]],
                                                [(D, F32), (D, BF16)], [(1, D), (1, D)], ts)
    dab_out = _matmul("mm_out_dx", dmix, W["w_out"], "nt", F32)
    db_out = (dab_out, (INNER, cfg.MLAW // INNER))
    gW["w_out"] = _matmul("mm_out_dw", ab_out, dmix, "tn", BF16)
    early_token = jnp.zeros((8, LANE), F32)
    if early_grads_ready is not None:
        early_token = early_grads_ready({n: gW[n] for n in ("w_down", "w_gate", "w_up", "w_out")})
        sp = dict(sp, ssm_norm_g=sp["ssm_norm_g"] + early_token[0, 0])

    dy_ssd, dz, g_ssm_norm = _ssd_post_bwd(cfg, db_out, y_ssd, z, sp["ssm_norm_g"])
    dxc, ddt_raw, g_dt_bias, g_a_log, g_d_skip = _ssd_bwd(cfg, dy_ssd, xc, dt_exp, a_s, dskip_exp, hin, dt_raw,
                                                          dt_bias_pad, a_log_pad, expand)
    dxbc, g_ssm_conv_w, g_ssm_conv_b = _colwise("ssm_act_bwd", _ssm_act_back, [dxc, xbc], [sp["ssm_conv_w"], sp["ssm_conv_b"]],
                                                [BF16], [SSM_K, 1], tc)

    dQ, delta_t = _attn_dq(cfg, Qh, Kh, Vh, dab_out, a_out, lse, early_token)
    dK, dV = _attn_dkv(cfg, Qh, Kh, Vh, dab_out, lse_t, delta_t)
    dq, dkv, dkr = _mla_unpack(cfg, dQ, dK, dV, cos2, sin2)
    dcqn = _matmul("mm_uq_dx", dq, W["w_uq"], "nt", F32)
    dckvn = _matmul("mm_ukv_dx", dkv, W["w_ukv"], "nt", F32)
    gW["w_uq"] = _matmul("mm_uq_dw", cqn, dq, "tn", BF16)
    gW["w_ukv"] = _matmul("mm_ukv_dw", ckvn, dkv, "tn", BF16)

    def rms_back(x, dy, g):
        r = _rs(x)
        xh = x * r
        return _rms_back(xh, r, dy * g), _colsum(dy * xh)

    dc_q, g_q_norm = _rowwise("rms_q_bwd", rms_back, [c_q, dcqn], [sp["q_norm_g"]], [(cfg.QL, BF16)], [(1, cfg.QL)], ts)
    dc_kv, g_kv_norm = _rowwise("rms_kv_bwd", rms_back, [c_kv, dckvn], [sp["kv_norm_g"]], [(cfg.KVL, BF16)], [(1, cfg.KVL)], ts)

    du = dict(c_q=dc_q, c_kv=dc_kv, kr=dkr, z=dz.astype(BF16), xbc=dxbc, dt=ddt_raw.astype(BF16))
    du = jnp.concatenate([du[n] for n in sorted(du, key=lambda n: cfg.seg[n][0])], axis=1)
    assert du.shape[1] == cfg.EXT, "the layout of u has gaps"
    gW["w_in"] = _matmul("mm_in_dw", du, xn, "tn", BF16)
    if in_grad_ready is not None:
        token = in_grad_ready({n: gW[n] for n in ("w_in", "w_uq", "w_ukv")})
        sp = dict(sp, mix_pre_g=sp["mix_pre_g"] + token[0, 0])
    dxn = _matmul("mm_in_dx", du, W["w_in"], "nn", F32)

    def first_back(dx1, dxn, x, g):
        r = _rs(x)
        xh = x * r
        return dx1 + _rms_back(xh, r, dxn * g), _colsum(dxn * xh)

    grad_x, g_mix_pre = _rowwise("bwd_first", first_back, [dx1, dxn, x], [sp["mix_pre_g"]], [(D, F32)], [(1, D)], ts_few)

    gs = dict(mix_pre_g=g_mix_pre, q_norm_g=g_q_norm, kv_norm_g=g_kv_norm, ssm_conv_w=g_ssm_conv_w, ssm_conv_b=g_ssm_conv_b,
              dt_bias=g_dt_bias[:, :cfg.HS], a_log=g_a_log[:, :cfg.HS], d_skip=g_d_skip[:, :cfg.HS], ssm_norm_g=g_ssm_norm,
              mix_post_g=g_mix_post, ffn_pre_g=g_ffn_pre, ffn_conv_w=g_ffn_conv_w, ffn_conv_b=g_ffn_conv_b,
              ffn_post_g=g_ffn_post)
    return loss, grad_x, gW, gs


def _to_kernel_layout(cfg, name, w):
    if name == "w_in":
        parts, at = [], 0
        for off, width, n_off, n_width in sorted(cfg.seg.values()):
            parts += [jnp.zeros((off - at, w.shape[1]), w.dtype), w[n_off:n_off + n_width],
                      jnp.zeros((width - n_width, w.shape[1]), w.dtype)]
            at = off + width
        parts.append(jnp.zeros((cfg.EXT - at, w.shape[1]), w.dtype))
        return jnp.concatenate([p for p in parts if p.shape[0]], axis=0)
    if name in ("w_uq", "w_ukv"):
        per = NOPE + (ROPE if name == "w_uq" else VH)
        return jnp.concatenate([w[:, h * per:h * per + NOPE] for h in range(cfg.H)]
                               + [w[:, h * per + NOPE:(h + 1) * per] for h in range(cfg.H)], axis=1)
    return w


def _from_kernel_layout(cfg, name, g):
    if name == "w_in":
        return jnp.concatenate([g[off:off + n_width] for off, _, _, n_width in sorted(cfg.seg.values(), key=lambda s: s[2])], axis=0)
    if name in ("w_uq", "w_ukv"):
        second = ROPE if name == "w_uq" else VH
        base = cfg.H * NOPE
        parts = []
        for h in range(cfg.H):
            parts += [g[:, h * NOPE:(h + 1) * NOPE], g[:, base + h * second:base + (h + 1) * second]]
        return jnp.concatenate(parts, axis=1)
    return g


_CHIP_MAJOR = ("w_gate", "w_up")
_RELAYOUT = ("w_uq", "w_ukv")
_LAYOUT_ROWS = 256
_CONV_COLS = 256


def _w_in_layout(cfg, wg):
    _, rs, d = wg.shape
    tc = _pick(d, _LAYOUT_ROWS, LANE)

    def body(w_ref, o_ref):
        o_ref[...] = _to_kernel_layout(cfg, "w_in", jnp.concatenate([w_ref[k] for k in range(N_CHIPS)], axis=0))

    return pl.pallas_call(
        body, name="layout_w_in", grid=(d // tc,),
        in_specs=[pl.BlockSpec((N_CHIPS, rs, tc), lambda j: (0, 0, j))], out_specs=pl.BlockSpec((cfg.EXT, tc), lambda j: (0, j)),
        out_shape=jax.ShapeDtypeStruct((cfg.EXT, d), wg.dtype), compiler_params=_params(("parallel",)),
    )(wg)


def _w_in_grad_to_chips(cfg, g):
    _, d = g.shape
    rs = cfg.IN_COLS // N_CHIPS
    tc = _pick(d, _LAYOUT_ROWS, LANE)

    def body(g_ref, o_ref):
        nat = _from_kernel_layout(cfg, "w_in", g_ref[...])
        for k in range(N_CHIPS):
            o_ref[k] = nat[k * rs:(k + 1) * rs]

    return pl.pallas_call(
        body, name="layout_grad_w_in", grid=(d // tc,),
        in_specs=[pl.BlockSpec((cfg.EXT, tc), lambda j: (0, j))], out_specs=pl.BlockSpec((N_CHIPS, rs, tc), lambda j: (0, 0, j)),
        out_shape=jax.ShapeDtypeStruct((N_CHIPS, rs, d), g.dtype), compiler_params=_params(("parallel",)),
    )(g)


def _gathered_to_kernel(cfg, name, wg):
    if name in _CHIP_MAJOR:
        return wg
    if name == "w_in":
        return _w_in_layout(cfg, wg)
    if name not in _RELAYOUT:
        return wg.reshape(wg.shape[0] * wg.shape[1], wg.shape[2])
    _, rows, cs = wg.shape
    tr = _pick(rows, _LAYOUT_ROWS, 16)

    def body(w_ref, o_ref):
        o_ref[...] = _to_kernel_layout(cfg, name, jnp.concatenate([w_ref[k] for k in range(N_CHIPS)], axis=1))

    wide = jax.eval_shape(lambda w: _to_kernel_layout(cfg, name, w), jax.ShapeDtypeStruct((rows, N_CHIPS * cs), wg.dtype)).shape[1]
    return pl.pallas_call(
        body, name="layout_" + name, grid=(rows // tr,),
        in_specs=[pl.BlockSpec((N_CHIPS, tr, cs), lambda i: (0, i, 0))], out_specs=pl.BlockSpec((tr, wide), lambda i: (i, 0)),
        out_shape=jax.ShapeDtypeStruct((rows, wide), wg.dtype), compiler_params=_params(("parallel",)),
    )(wg)


def _grad_to_chips(cfg, name, g):
    if name in _CHIP_MAJOR:
        return g
    if name == "w_in":
        return _w_in_grad_to_chips(cfg, g)
    if name not in _RELAYOUT:
        return g.reshape(N_CHIPS, g.shape[0] // N_CHIPS, g.shape[1])
    rows, wide = g.shape
    tr = _pick(rows, _LAYOUT_ROWS, 16)
    cs = jax.eval_shape(lambda v: _from_kernel_layout(cfg, name, v), g).shape[1] // N_CHIPS

    def body(g_ref, o_ref):
        nat = _from_kernel_layout(cfg, name, g_ref[...])
        for k in range(N_CHIPS):
            o_ref[k] = nat[:, k * cs:(k + 1) * cs]

    return pl.pallas_call(
        body, name="layout_grad_" + name, grid=(rows // tr,),
        in_specs=[pl.BlockSpec((tr, wide), lambda i: (i, 0))], out_specs=pl.BlockSpec((N_CHIPS, tr, cs), lambda i: (0, i, 0)),
        out_shape=jax.ShapeDtypeStruct((N_CHIPS, rows, cs), g.dtype), compiler_params=_params(("parallel",)),
    )(g)


def _me():
    return lax.axis_index("x"), lax.axis_index("y"), lax.axis_index("c")


def _other_chips(x, y):
    return [(1 - x, y), (x, 1 - y), (1 - x, 1 - y)]


_ANY = pl.BlockSpec(memory_space=pl.ANY)


BLOCK_ELEMS = 1 << 19
BLOCK_ELEMS_FEW = 1 << 20


def _row_block(rows, cols, mult, elems=BLOCK_ELEMS):
    return _pick(rows, max(mult, elems // cols // mult * mult), mult)


def _scalar(v):
    return v.astype(I32).reshape(1)


def _blocks2d(r, c, mult, elems=BLOCK_ELEMS):
    if r % mult == 0:
        tr = _row_block(r, c, mult, elems)
        return (tr, c), r // tr, lambda i: (i, 0)
    tc = _pick(c, max(LANE, elems // r // LANE * LANE), LANE)
    return (r, tc), c // tc, lambda i: (0, i)


def _by_rows(rows):
    return rows % 32 == 0


def _half_shape(rows, cols):
    return (rows // 2, cols) if _by_rows(rows) else (rows, cols // 2)


def _half_blocks(rows, cols, mult, elems=BLOCK_ELEMS):
    hr, hc = _half_shape(rows, cols)
    block, n, part = _blocks2d(hr, hc, mult, elems)
    assert (hr % mult == 0) == _by_rows(rows), (rows, cols, mult)
    full = (lambda h, i: (h * n + i, 0)) if _by_rows(rows) else (lambda h, i: (0, h * n + i))
    return block, n, full, part


def _half(ref, k, half):
    hr, hc = _half_shape(ref.shape[1], ref.shape[2])
    if _by_rows(ref.shape[1]):
        return ref.at[k, pl.ds(pl.multiple_of(half * hr, 16), hr), :]
    return ref.at[k, :, pl.ds(pl.multiple_of(half * hc, LANE), hc)]


def _shard_blocks(w, br, bc):
    if w.shape[0] == 1:
        def write(ref, v):
            ref[...] = v
        return (lambda f: pl.BlockSpec((None, br, bc), lambda *a: (0, *f(*a)))), (lambda ref: ref[...]), write
    assert w.shape[1] == 1 and br == w.shape[0], w.shape

    def write_rows(ref, v):
        ref[:, 0, :] = v
    return (lambda f: pl.BlockSpec((br, 1, bc), lambda *a: (0, 0, f(*a)[1]))), (lambda ref: ref[:, 0, :]), write_rows


def _stage_shard(name, w, chip, after=None):
    rs, cs = w.shape[0] * w.shape[1], w.shape[2]
    (br, bc), n, idx = _blocks2d(rs, cs, 16, BLOCK_ELEMS_FEW)
    spec, get, _ = _shard_blocks(w, br, bc)

    def body(chip_ref, w_ref, *refs):
        refs[-1][...] = get(w_ref).astype(BF16)

    return pl.pallas_call(
        body, name="stage_" + name,
        grid_spec=pltpu.PrefetchScalarGridSpec(
            num_scalar_prefetch=1, grid=(n,),
            in_specs=[spec(lambda i, chip_ref: idx(i))] + ([] if after is None else [_ANY]),
            out_specs=pl.BlockSpec((None, br, bc), lambda i, chip_ref: (chip_ref[0], *idx(i)))),
        out_shape=jax.ShapeDtypeStruct((N_CHIPS, rs, cs), BF16),
        compiler_params=_params(("parallel",)),
    )(_scalar(chip), w, *([] if after is None else [after]))


_HBM = pl.BlockSpec(memory_space=pltpu.HBM)
_SEM = pl.BlockSpec(memory_space=pltpu.SEMAPHORE)
_EFFECT = pltpu.SideEffectType.DATAFLOW_SIDE_EFFECTING


def _split_start(name, bufs, n_copies, copies, after):
    n = len(bufs)

    def body(*refs):
        for cp in copies(refs[:n], refs[n + 1], refs[n + 2]):
            cp.start()
        refs[-1][...] = jnp.zeros_like(refs[-1])

    res = pl.pallas_call(
        body, name=name,
        out_shape=(pltpu.SemaphoreType.DMA((n_copies,)), pltpu.SemaphoreType.DMA((n_copies,)),
                   *[pltpu.HBM(b.shape, b.dtype) for b in bufs], jax.ShapeDtypeStruct((8, LANE), F32)),
        in_specs=[_HBM] * n + [_ANY], out_specs=(_SEM, _SEM, *[_HBM] * n, pl.BlockSpec(memory_space=pltpu.VMEM)),
        input_output_aliases={i: 2 + i for i in range(n)},
        compiler_params=pltpu.CompilerParams(has_side_effects=_EFFECT),
    )(*[pltpu.with_memory_space_constraint(b, pltpu.HBM) for b in bufs], after)
    return res[0], res[1], list(res[2:2 + n]), res[-1]


def _split_wait(name, send_sems, recv_sems, bufs, after, copies):
    n = len(bufs)

    def body(*refs):
        for cp in copies(refs[:n], refs[n], refs[n + 1]):
            cp.wait_send()
            cp.wait_recv()

    return list(pl.pallas_call(
        body, name=name, out_shape=[pltpu.HBM(b.shape, b.dtype) for b in bufs],
        in_specs=[_HBM] * n + [_SEM, _SEM, _ANY], out_specs=[_HBM] * n,
        input_output_aliases={i: i for i in range(n)},
        compiler_params=pltpu.CompilerParams(has_side_effects=_EFFECT),
    )(*bufs, send_sems, recv_sems, after))


def _gather_to_chips(bufs, send_sems, recv_sems):
    x, y, c = _me()
    return [pltpu.make_async_remote_copy(src_ref=_half(b, 2 * x + y, c), dst_ref=_half(b, 2 * x + y, c),
                                         send_sem=send_sems.at[3 * w + j], recv_sem=recv_sems.at[3 * w + j],
                                         device_id=(cx, cy, c), device_id_type=MESH_ID)
            for w, b in enumerate(bufs) for j, (cx, cy) in enumerate(_other_chips(x, y))]


def _gather_to_sibling(bufs, send_sems, recv_sems):
    x, y, c = _me()
    return [pltpu.make_async_remote_copy(src_ref=_half(b, 2 * cx + cy, c), dst_ref=_half(b, 2 * cx + cy, c),
                                         send_sem=send_sems.at[3 * w + j], recv_sem=recv_sems.at[3 * w + j],
                                         device_id=(x, y, 1 - c), device_id_type=MESH_ID)
            for w, b in enumerate(bufs) for j, (cx, cy) in enumerate(_other_chips(x, y))]


def _pair_exchange(name, grads):
    n = len(grads)

    def body(*refs):
        ins, outs, send_sems, recv_sems = refs[:n], refs[n:2 * n], refs[2 * n], refs[2 * n + 1]
        x, y, c = _me()
        cps = []
        for w, (g_ref, o_ref) in enumerate(zip(ins, outs)):
            cps.append(pltpu.make_async_remote_copy(src_ref=_half(g_ref, slice(None), 1 - c), dst_ref=o_ref,
                                                    send_sem=send_sems.at[w], recv_sem=recv_sems.at[w],
                                                    device_id=(x, y, 1 - c), device_id_type=MESH_ID))
            cps[-1].start()
        for cp in cps:
            cp.wait()

    return pl.pallas_call(
        body, name="pair_exchange_" + name, in_specs=[_ANY] * n, out_specs=[_ANY] * n,
        out_shape=[jax.ShapeDtypeStruct((g.shape[0], *_half_shape(g.shape[1], g.shape[2])), g.dtype) for g in grads],
        scratch_shapes=[pltpu.SemaphoreType.DMA((n,)), pltpu.SemaphoreType.DMA((n,))],
    )(*grads)


def _pair_copies(grads, lands, send_sems, recv_sems):
    x, y, c = _me()
    return [pltpu.make_async_remote_copy(src_ref=_half(g_ref, slice(None), 1 - c), dst_ref=l_ref, send_sem=send_sems.at[w],
                                         recv_sem=recv_sems.at[w], device_id=(x, y, 1 - c), device_id_type=MESH_ID)
            for w, (g_ref, l_ref) in enumerate(zip(grads, lands))]


def _pair_exchange_start(name, grads):
    n = len(grads)
    lands = [lax.empty((g.shape[0], *_half_shape(g.shape[1], g.shape[2])), g.dtype) for g in grads]
    send_sems, recv_sems, bufs, token = _split_start(
        "pair_exchange_start_" + name, [*grads, *lands], n, lambda refs, ss, rs: _pair_copies(refs[:n], refs[n:], ss, rs),
        jnp.zeros((8, LANE), F32))
    return (send_sems, recv_sems, bufs), token


def _pair_exchange_wait(name, state, after):
    send_sems, recv_sems, bufs = state
    n = len(bufs) // 2
    bufs = _split_wait("pair_exchange_wait_" + name, send_sems, recv_sems, bufs, after,
                       lambda refs, ss, rs: _pair_copies(refs[:n], refs[n:], ss, rs))
    return bufs[:n], bufs[n:]


def _pair_sum(name, g, theirs, c):
    (br, bc), nb, full, part = _half_blocks(g.shape[1], g.shape[2], 16, 2 * BLOCK_ELEMS_FEW)

    def body(c_ref, a_ref, b_ref, o_ref):
        o_ref[...] = (a_ref[...].astype(F32) + b_ref[...].astype(F32)).astype(o_ref.dtype)

    return pl.pallas_call(
        body, name="pair_sum_" + name,
        grid_spec=pltpu.PrefetchScalarGridSpec(
            num_scalar_prefetch=1, grid=(N_CHIPS, nb),
            in_specs=[pl.BlockSpec((None, br, bc), lambda k, i, c_ref: (k, *full(c_ref[0], i))),
                      pl.BlockSpec((None, br, bc), lambda k, i, c_ref: (k, *part(i)))],
            out_specs=pl.BlockSpec((None, br, bc), lambda k, i, c_ref: (k, *part(i)))),
        out_shape=jax.ShapeDtypeStruct(theirs.shape, BF16),
        compiler_params=_params(("parallel", "parallel")),
    )(_scalar(c), g, theirs)


def _chip_copies(srcs, lands, send_sems, recv_sems):
    x, y, c = _me()
    return [pltpu.make_async_remote_copy(src_ref=s_ref.at[2 * cx + cy], dst_ref=l_ref.at[j], send_sem=send_sems.at[3 * w + j],
                                         recv_sem=recv_sems.at[3 * w + j], device_id=(cx, cy, c), device_id_type=MESH_ID)
            for w, (s_ref, l_ref) in enumerate(zip(srcs, lands)) for j, (cx, cy) in enumerate(_other_chips(x, y))]


def _chip_exchange_start(name, sums):
    n = len(sums)
    lands = [lax.empty((3,) + s.shape[1:], s.dtype) for s in sums]
    send_sems, recv_sems, bufs, token = _split_start(
        "chip_exchange_start_" + name, [*sums, *lands], 3 * n, lambda refs, ss, rs: _chip_copies(refs[:n], refs[n:], ss, rs),
        jnp.zeros((8, LANE), F32))
    return send_sems, recv_sems, bufs[:n], bufs[n:], token


def _chip_exchange_wait(name, send_sems, recv_sems, sums, lands, after):
    n = len(sums)
    bufs = _split_wait("chip_exchange_wait_" + name, send_sems, recv_sems, [*sums, *lands], after,
                       lambda refs, ss, rs: _chip_copies(refs[:n], refs[n:], ss, rs))
    return bufs[:n], bufs[n:]


def _chip_sum(name, sums, theirs, chip):
    _, h, cs = sums.shape
    (br, bc), nb, idx = _blocks2d(h, cs, 16, BLOCK_ELEMS_FEW)

    def body(chip_ref, s_ref, t_ref, o_ref):
        acc = s_ref[...].astype(F32)
        for k in range(3):
            acc = acc + t_ref[k].astype(F32)
        o_ref[...] = acc

    return pl.pallas_call(
        body, name="chip_sum_" + name,
        grid_spec=pltpu.PrefetchScalarGridSpec(
            num_scalar_prefetch=1, grid=(nb,),
            in_specs=[pl.BlockSpec((None, br, bc), lambda i, chip_ref: (chip_ref[0], *idx(i))),
                      pl.BlockSpec((3, br, bc), lambda i, chip_ref: (0, *idx(i)))],
            out_specs=pl.BlockSpec((br, bc), lambda i, chip_ref: idx(i))),
        out_shape=jax.ShapeDtypeStruct((h, cs), F32),
        compiler_params=_params(("parallel",)),
    )(_scalar(chip), sums, theirs)


def _sibling_copies(halves, lands, send_sems, recv_sems):
    x, y, c = _me()
    return [pltpu.make_async_remote_copy(src_ref=h_ref, dst_ref=l_ref, send_sem=send_sems.at[w], recv_sem=recv_sems.at[w],
                                         device_id=(x, y, 1 - c), device_id_type=MESH_ID)
            for w, (h_ref, l_ref) in enumerate(zip(halves, lands))]


def _sibling_exchange_start(name, halves, after):
    n = len(halves)
    lands = [lax.empty(h.shape, h.dtype) for h in halves]
    send_sems, recv_sems, bufs, token = _split_start(
        "sibling_exchange_start_" + name, [*halves, *lands], n, lambda refs, ss, rs: _sibling_copies(refs[:n], refs[n:], ss, rs),
        after)
    return (send_sems, recv_sems, bufs), token


def _sibling_exchange_wait(name, state, after):
    send_sems, recv_sems, bufs = state
    n = len(bufs) // 2
    bufs = _split_wait("sibling_exchange_wait_" + name, send_sems, recv_sems, bufs, after,
                       lambda refs, ss, rs: _sibling_copies(refs[:n], refs[n:], ss, rs))
    return bufs[:n], bufs[n:]


def _sibling_exchange(name, halves):
    n = len(halves)

    def body(*refs):
        ins, outs, send_sems, recv_sems = refs[:n], refs[n:2 * n], refs[2 * n], refs[2 * n + 1]
        x, y, c = _me()
        cps = []
        for w, (h_ref, o_ref) in enumerate(zip(ins, outs)):
            cps.append(pltpu.make_async_remote_copy(src_ref=h_ref, dst_ref=o_ref, send_sem=send_sems.at[w], recv_sem=recv_sems.at[w],
                                                    device_id=(x, y, 1 - c), device_id_type=MESH_ID))
            cps[-1].start()
        for cp in cps:
            cp.wait()

    return pl.pallas_call(
        body, name="sibling_exchange_" + name, in_specs=[_ANY] * n, out_specs=[_ANY] * n,
        out_shape=[jax.ShapeDtypeStruct(h.shape, h.dtype) for h in halves],
        scratch_shapes=[pltpu.SemaphoreType.DMA((n,)), pltpu.SemaphoreType.DMA((n,))],
    )(*halves)


N_DEV = 8


def _peer_copies(bufs, send_sems, recv_sems):
    vec, land = bufs
    x, y, c = _me()
    return [pltpu.make_async_remote_copy(src_ref=vec, dst_ref=land.at[4 * x + 2 * y + c], send_sem=send_sems.at[p - 1],
                                         recv_sem=recv_sems.at[p - 1], device_id=(x ^ (p >> 2), y ^ ((p >> 1) & 1), c ^ (p & 1)),
                                         device_id_type=MESH_ID) for p in range(1, N_DEV)]


def _allreduce_small_start(vec, after):
    land = jnp.zeros((N_DEV,) + vec.shape, F32)
    send_sems, recv_sems, bufs, _ = _split_start("allreduce_small_start", [vec, land], N_DEV - 1, _peer_copies, after)
    return send_sems, recv_sems, bufs


def _allreduce_small_wait(state, chip, core, after):
    send_sems, recv_sems, bufs = state
    vec, land = _split_wait("allreduce_small_wait", send_sems, recv_sems, bufs, after, _peer_copies)

    def body(me_ref, v_ref, l_ref, o_ref):
        acc = None
        for k in range(N_DEV):
            term = jnp.where(me_ref[0] == k, v_ref[...], l_ref[k])
            acc = term if acc is None else acc + term
        o_ref[...] = acc

    return pl.pallas_call(
        body, name="allreduce_small_sum",
        grid_spec=pltpu.PrefetchScalarGridSpec(
            num_scalar_prefetch=1, grid=(1,),
            in_specs=[pl.BlockSpec(vec.shape, lambda i, me_ref: (0, 0)), pl.BlockSpec(land.shape, lambda i, me_ref: (0, 0, 0))],
            out_specs=pl.BlockSpec(vec.shape, lambda i, me_ref: (0, 0))),
        out_shape=jax.ShapeDtypeStruct(vec.shape, F32), compiler_params=_params(("arbitrary",)),
    )(_scalar(2 * chip + core), vec, land)


def _adam_math(w, g, m, v):
    m = ADAM_B1 * m + (1.0 - ADAM_B1) * g
    v = ADAM_B2 * v + (1.0 - ADAM_B2) * (g * g)
    m_hat = m / (1.0 - ADAM_B1 ** ADAM_STEP)
    v_hat = v / (1.0 - ADAM_B2 ** ADAM_STEP)
    return -ADAM_LR * (m_hat / (jnp.sqrt(v_hat) + ADAM_EPS) + ADAM_WD * w), m, v


def _adamw(name, w, g, m, v):
    R, C = w.shape
    tr = _row_block(R, C, 8)

    def body(w_ref, g_ref, m_ref, v_ref, d_ref, nm_ref, nv_ref):
        d_ref[...], nm_ref[...], nv_ref[...] = _adam_math(w_ref[...], g_ref[...], m_ref[...], v_ref[...])

    blk = pl.BlockSpec((tr, C), lambda i: (i, 0))
    return pl.pallas_call(
        body, name=name, grid=(R // tr,), in_specs=[blk] * 4, out_specs=[blk] * 3,
        out_shape=[jax.ShapeDtypeStruct((R, C), F32)] * 3, compiler_params=_params(("parallel",)),
    )(w, g, m, v)


def _adamw_halves(name, w, mine, theirs, m, v, c):
    rs, cs = w.shape[0] * w.shape[1], w.shape[2]
    (br, bc), nb, whole, half = _half_blocks(rs, cs, 8)
    spec, get, put = _shard_blocks(w, br, bc)

    def body(c_ref, w_ref, a_ref, b_ref, m_ref, v_ref, g_ref, d_ref, nm_ref, nv_ref):
        g = jnp.where(pl.program_id(0) == c_ref[0], a_ref[...], b_ref[...])
        put(g_ref, g)
        for ref, val in zip((d_ref, nm_ref, nv_ref), _adam_math(get(w_ref), g, get(m_ref), get(v_ref))):
            put(ref, val)

    full = spec(lambda s, i, c_ref: whole(s, i))
    part = pl.BlockSpec((br, bc), lambda s, i, c_ref: half(i))
    return pl.pallas_call(
        body, name=name,
        grid_spec=pltpu.PrefetchScalarGridSpec(num_scalar_prefetch=1, grid=(2, nb), in_specs=[full, part, part, full, full],
                                               out_specs=[full] * 4),
        out_shape=[jax.ShapeDtypeStruct(w.shape, F32)] * 4, compiler_params=_params(("parallel", "parallel")),
    )(_scalar(c), w, mine, theirs, m, v)


def _pack_small(arrs, lanes=LANE):
    flat = jnp.concatenate([a.reshape(-1) for a in arrs])
    n = -(-flat.shape[0] // (8 * lanes)) * 8 * lanes
    return jnp.pad(flat, (0, n - flat.shape[0])).reshape(8, n // 8)


def _unpack_small(vec, shapes):
    flat, out, off = vec.reshape(-1), [], 0
    for s in shapes:
        out.append(flat[off:off + s[0] * s[1]].reshape(s))
        off += s[0] * s[1]
    return out


class _LateWeights:
    def __init__(self, cfg, tag, names, staged, after):
        self.cfg, self.tag, self.names, self.k = cfg, tag, names, 3 * len(names)
        self.send, self.recv, self.bufs, self.token = _split_start(f"gather_{tag}_chips_start", staged, self.k, _gather_to_chips,
                                                                    after)

    def pass_on(self, after):
        bufs = _split_wait(f"gather_{self.tag}_chips_wait", self.send, self.recv, self.bufs, after, _gather_to_chips)
        self.send, self.recv, self.bufs, token = _split_start(f"gather_{self.tag}_sibling_start", bufs, self.k, _gather_to_sibling,
                                                               self.token)
        return token

    def arrived(self, after):
        bufs = _split_wait(f"gather_{self.tag}_sibling_wait", self.send, self.recv, self.bufs, after, _gather_to_sibling)
        return {n: _gathered_to_kernel(self.cfg, n, b) for n, b in zip(self.names, bufs)}


def _step(cfg, a):
    chip = 2 * lax.axis_index("x") + lax.axis_index("y")
    core = lax.axis_index("c")
    big = BIG

    ffn = ("w_gate", "w_up", "w_down")
    first = ("w_in", "w_uq", "w_ukv")
    sp = {n: a[n] for n in SMALL}
    sharded = _pack_small([a[n] for n in SMALL_SHARDED], 2 * LANE)
    slabs = jnp.where(lax.broadcasted_iota(I32, (N_CHIPS,) + sharded.shape, 0) == chip, sharded[None], 0.0)
    staged = {"w_in": _stage_shard("w_in", a["w_in"], chip)}
    in_weight = _LateWeights(cfg, "in", ("w_in", "sharded_small"), [staged["w_in"], slabs], jnp.zeros((8, LANE), F32))
    behind = in_weight.token
    for n in big[1:]:
        behind = staged[n] = _stage_shard(n, a[n], chip, behind)
    in_weight.pass_on(behind)
    xn_early = _rms_pre(cfg, a["x"], sp["mix_pre_g"] + in_weight.token[0, 0])
    W = in_weight.arrived(xn_early)
    allp = W.pop("sharded_small").reshape((N_CHIPS,) + sharded.shape)
    per_chip = [_unpack_small(allp[ch], [a[n].shape for n in SMALL_SHARDED]) for ch in range(N_CHIPS)]
    for k, n in enumerate(SMALL_SHARDED):
        sp[n] = jnp.concatenate([per_chip[ch][k] for ch in range(N_CHIPS)], axis=1)

    mla_weights = _LateWeights(cfg, "mla", first[1:], [staged[n] for n in first[1:]], W["w_in"])
    out_weight = _LateWeights(cfg, "out", ("w_out",), [staged["w_out"]], mla_weights.token)
    ffn_weights = _LateWeights(cfg, "ffn", ffn[:2], [staged[n] for n in ffn[:2]], out_weight.token)
    down_weight = _LateWeights(cfg, "down", ffn[2:], [staged[n] for n in ffn[2:]], ffn_weights.token)

    state = {}

    def ffn_grads_ready(grads):
        state["ffn_pairs"], token = _pair_exchange_start("ffn", [_grad_to_chips(cfg, n, grads[n]) for n in ffn_grads])
        return token

    def pair_sums(names, grads, theirs):
        return [_pair_sum(n, g, t, core) for n, g, t in zip(names, grads, theirs)]

    def early_grads_ready(grads):
        g_out = [_grad_to_chips(cfg, "w_out", grads["w_out"])]
        g_ffn, t_ffn = _pair_exchange_wait("ffn", state["ffn_pairs"], g_out[0])
        sums = pair_sums(ffn_grads, g_ffn, t_ffn) + pair_sums(["w_out"], g_out, _pair_exchange("out", g_out))
        state["early"] = _chip_exchange_start("early", sums)
        return state["early"][-1]

    def reduced_halves(tag, names, after):
        send_sems, recv_sems, s_bufs, l_bufs, _ = state[tag]
        s_bufs, l_bufs = _chip_exchange_wait(tag, send_sems, recv_sems, s_bufs, l_bufs, after)
        return [_chip_sum(n, s, t, chip) for n, s, t in zip(names, s_bufs, l_bufs)]

    def in_grad_ready(grads):
        grads = [_grad_to_chips(cfg, n, grads[n]) for n in first]
        state["rest"] = _chip_exchange_start("rest", pair_sums(first, grads, _pair_exchange("rest", grads)))
        return state["rest"][-1]

    ffn_grads = ("w_down", "w_gate", "w_up")
    early = ffn_grads + ("w_out",)
    loss, grad_x, gW, gs = _local_grads(cfg, a["x"], a["loss_target"], W, sp, mla_weights, out_weight, ffn_weights, down_weight,
                                        ffn_grads_ready, early_grads_ready, in_grad_ready, xn_early, down_weight.token)
    out = {"grad_x": grad_x}

    def adamw(names, mine, theirs):
        for n, gm, gt in zip(names, mine, theirs):
            out["grad_" + n], out["delta_" + n], out["new_m_" + n], out["new_v_" + n] = _adamw_halves(
                "adamw_" + n, a[n], gm, gt, a["m_" + n], a["v_" + n], core)

    mine = reduced_halves("early", early, grad_x)
    theirs = _sibling_exchange("early", mine[:1])
    later, _ = _sibling_exchange_start("early", mine[1:], theirs[0])
    adamw(early[:1], mine[:1], theirs)
    e_mine, e_theirs = _sibling_exchange_wait("early", later, out["new_v_" + early[0]])
    adamw(early[3:], e_mine[2:], e_theirs[2:])
    mine = reduced_halves("rest", first, out["new_v_" + early[-1]])
    rest, token = _sibling_exchange_start("rest", mine, mine[0])
    small = _allreduce_small_start(_pack_small([gs[n] for n in SMALL] + [loss]), token)
    adamw(early[1:3], e_mine[:2], e_theirs[:2])
    adamw(first, *_sibling_exchange_wait("rest", rest, out["new_v_" + early[2]]))
    shapes = [gs[n].shape for n in SMALL] + [(1, LANE)]
    red = _unpack_small(_allreduce_small_wait(small, chip, core, out["new_v_" + first[-1]]), shapes)
    g_small = dict(zip(SMALL, red[:-1]))
    for n in SMALL_SHARDED:
        cs = a[n].shape[1]
        g_small[n] = lax.dynamic_slice_in_dim(g_small[n], chip * cs, cs, axis=1)
    out["loss"] = red[-1][0, 0]
    sshapes = [a[n].shape for n in SMALL]
    d, nm, nv = _adamw("adamw_small", _pack_small([a[n] for n in SMALL]), _pack_small([g_small[n] for n in SMALL]),
                       _pack_small([a["m_" + n] for n in SMALL]), _pack_small([a["v_" + n] for n in SMALL]))
    for n, dd, mm, vv in zip(SMALL, _unpack_small(d, sshapes), _unpack_small(nm, sshapes), _unpack_small(nv, sshapes)):
        out["grad_" + n], out["delta_" + n], out["new_m_" + n], out["new_v_" + n] = g_small[n], dd, mm, vv
    return out


def kernel(x, mix_pre_g, w_in, q_norm_g, w_uq, kv_norm_g, w_ukv, ssm_conv_w, ssm_conv_b, dt_bias, a_log, d_skip, ssm_norm_g, w_out, mix_post_g, ffn_pre_g, w_gate, w_up, ffn_conv_w, ffn_conv_b, w_down, ffn_post_g, loss_target, m_mix_pre_g, m_w_in, m_q_norm_g, m_w_uq, m_kv_norm_g, m_w_ukv, m_ssm_conv_w, m_ssm_conv_b, m_dt_bias, m_a_log, m_d_skip, m_ssm_norm_g, m_w_out, m_mix_post_g, m_ffn_pre_g, m_w_gate, m_w_up, m_ffn_conv_w, m_ffn_conv_b, m_w_down, m_ffn_post_g, v_mix_pre_g, v_w_in, v_q_norm_g, v_w_uq, v_kv_norm_g, v_w_ukv, v_ssm_conv_w, v_ssm_conv_b, v_dt_bias, v_a_log, v_d_skip, v_ssm_norm_g, v_w_out, v_mix_post_g, v_ffn_pre_g, v_w_gate, v_w_up, v_ffn_conv_w, v_ffn_conv_b, v_w_down, v_ffn_post_g):
    args = dict(locals())
    def given(k, v):
        if k in ("w_in", "m_w_in", "v_w_in"):
            return jnp.transpose(v, (2, 0, 1))
        return v if k.removeprefix("m_").removeprefix("v_") in BIG or v.ndim < 3 else v[0]

    out = _step(_FULL, {k: given(k, v) for k, v in args.items()})
    res = [out["loss"], out["grad_x"][None]]
    for pre in ("grad_", "delta_", "new_m_", "new_v_"):
        for n in WEIGHTS:
            o = out[pre + n]
            res.append(jnp.transpose(o, (1, 2, 0)) if n == "w_in" else o if n in BIG or args[n].ndim < 3 else o[None])
    return tuple(res)
```

```python
import math

import jax
import jax.numpy as jnp
from jax import lax
from jax.experimental import pallas as pl
from jax.experimental.pallas import tpu as pltpu

F32, BF16, I32 = jnp.float32, jnp.bfloat16, jnp.int32
NN = (((1,), (0,)), ((), ()))
NT = (((1,), (1,)), ((), ()))
TN = (((0,), (0,)), ((), ()))
HI = lax.Precision.HIGHEST
MESH_ID = pl.DeviceIdType.MESH

EPS = 1e-6
CHUNK = 64
NOPE, ROPE, VH = 128, 64, 128
ROPE_THETA = 10000.0
HP, NST = 64, 128
SSM_K, FFN_K = 4, 3
LANE = 128
N_CHIPS = 4
VMEM_LIMIT = 52 * 1024 * 1024
MM_TILE, MM_TILE_K = 1408, 2816

ADAM_LR, ADAM_B1, ADAM_B2, ADAM_EPS, ADAM_WD, ADAM_STEP = 0.001, 0.9, 0.999, 1e-08, 0.01, 10


class _Cfg:
    def __init__(self, S, D, QL, KVL, H, HS, G, DFF, T):
        self.S, self.D, self.QL, self.KVL, self.H, self.HS, self.G, self.DFF, self.T = S, D, QL, KVL, H, HS, G, DFF, T
        self.INNER = HS * HP
        self.CONVCH = self.INNER + 2 * G * NST
        self.QW = H * (NOPE + ROPE)
        self.KVW = H * (NOPE + VH)
        self.MLAW = H * VH
        self.MIXW = self.MLAW + self.INNER
        self.IN_COLS = QL + KVL + ROPE + self.INNER + self.CONVCH + HS
        natural, at = {}, 0
        for name, w in (("c_q", QL), ("c_kv", KVL), ("kr", ROPE), ("z", self.INNER), ("xbc", self.CONVCH), ("dt", HS)):
            natural[name] = (at, w)
            at += w
        self.seg, taken = {}, []
        for name in sorted(natural, key=lambda n: -natural[n][1]):
            w = -(-natural[name][1] // LANE) * LANE
            off = next(o for o in range(0, self.IN_COLS * 2, w) if all(o + w <= t or o >= t + tw for t, tw in taken))
            taken.append((off, w))
            self.seg[name] = (off, w) + natural[name]
        self.EXT = max(o + w for o, w in taken)
        self.NPAIR = HS // 2
        self.REP = HS // G

    def window(self, name):
        off, w, _, _ = self.seg[name]
        return w, off // w


_FULL = _Cfg(S=2048, D=2048, QL=768, KVL=512, H=8, HS=16, G=2, DFF=5632, T=256)
BIG = ("w_in", "w_uq", "w_ukv", "w_out", "w_gate", "w_up", "w_down")

SMALL = ("mix_pre_g", "q_norm_g", "kv_norm_g", "ssm_conv_w", "ssm_conv_b", "dt_bias", "a_log", "d_skip", "ssm_norm_g",
         "mix_post_g", "ffn_pre_g", "ffn_conv_w", "ffn_conv_b", "ffn_post_g")
SMALL_SHARDED = ("ssm_conv_w", "ffn_conv_w")
WEIGHTS = ("mix_pre_g", "w_in", "q_norm_g", "w_uq", "kv_norm_g", "w_ukv", "ssm_conv_w", "ssm_conv_b", "dt_bias", "a_log",
           "d_skip", "ssm_norm_g", "w_out", "mix_post_g", "ffn_pre_g", "w_gate", "w_up", "ffn_conv_w", "ffn_conv_b",
           "w_down", "ffn_post_g")


def _pick(n, target, mult):
    best = None
    for d in range(mult, min(n, target) + 1, mult):
        if n % d == 0:
            best = d
    return best if best is not None else n


def _params(sem=None):
    kw = dict(vmem_limit_bytes=VMEM_LIMIT)
    if sem is not None:
        kw["dimension_semantics"] = sem
    return pltpu.CompilerParams(**kw)


def _dot(a, b, dims=NN, precision=None):
    return lax.dot_general(a, b, dims, preferred_element_type=F32, precision=precision)


def _sigmoid(x):
    return 1.0 / (1.0 + jnp.exp(-x))


def _rs(x):
    return lax.rsqrt(jnp.mean(x * x, axis=-1, keepdims=True) + EPS)


def _rms_back(xh, r, dn):
    return r * (dn - xh * jnp.mean(dn * xh, axis=-1, keepdims=True))


def _colsum(v):
    return jnp.sum(v, axis=0, keepdims=True)


def _matmul(name, a, b, mode, out_dtype, a2=None, b2=None, chips=False, after=None):
    cs = None
    if mode == "nn":
        (M, K), N = a.shape, b.shape[-1]
        if chips:
            cs, N = N, N_CHIPS * N
    elif mode == "nt":
        (M, K), N = a.shape, b.shape[-2]
        if chips:
            cs = b.shape[-1]
    else:
        (K, M), N = a.shape, b.shape[1]
        if chips:
            cs = N // N_CHIPS
    tm = _pick(M, MM_TILE, LANE)
    tn = _pick(cs if chips and mode != "nt" else N, MM_TILE, LANE)
    tk = _pick(cs, MM_TILE, LANE) if chips and mode == "nt" else _pick(K, MM_TILE_K, LANE)
    nk = K // tk
    dims = {"nn": NN, "nt": NT, "tn": TN}[mode]
    a_spec = pl.BlockSpec((tk, tm), lambda i, j, k: (k, i)) if mode == "tn" else pl.BlockSpec((tm, tk), lambda i, j, k: (i, k))
    b_spec = pl.BlockSpec((tn, tk), lambda i, j, k: (j, k)) if mode == "nt" else pl.BlockSpec((tk, tn), lambda i, j, k: (k, j))
    o_spec = pl.BlockSpec((tm, tn), lambda i, j, k: (i, j))
    o_shape = (M, N)
    if chips and mode == "nn":
        per = cs // tn
        b_spec = pl.BlockSpec((None, tk, tn), lambda i, j, k: (j // per, k, j % per))
    elif chips and mode == "nt":
        per = cs // tk
        b_spec = pl.BlockSpec((None, tn, tk), lambda i, j, k: (k // per, j, k % per))
    elif chips:
        per = cs // tn
        o_spec = pl.BlockSpec((None, tm, tn), lambda i, j, k: (j // per, i, j % per))
        o_shape = (N_CHIPS, M, cs)
    two = a2 is not None

    def product(refs):
        part = _dot(refs[0][...].astype(BF16), refs[1][...].astype(BF16), dims)
        if two:
            part += _dot(refs[2][...].astype(BF16), refs[3][...].astype(BF16), dims)
        return part

    def body_whole_k(*refs):
        refs[-1][...] = product(refs).astype(refs[-1].dtype)

    def body(*refs):
        o_ref, acc_ref = refs[-2], refs[-1]
        k = pl.program_id(2)

        @pl.when(k == 0)
        def _():
            acc_ref[...] = product(refs)

        @pl.when(k > 0)
        def _():
            acc_ref[...] += product(refs)

        @pl.when(k == nk - 1)
        def _():
            o_ref[...] = acc_ref[...].astype(o_ref.dtype)

    ins = ((a, b, a2, b2) if two else (a, b)) + (() if after is None else (after,))
    return pl.pallas_call(
        body_whole_k if nk == 1 else body, name=name, grid=(M // tm, N // tn, nk),
        in_specs=[a_spec, b_spec] * (2 if two else 1) + ([] if after is None else [pl.BlockSpec(memory_space=pl.ANY)]),
        out_specs=o_spec,
        out_shape=jax.ShapeDtypeStruct(o_shape, out_dtype),
        scratch_shapes=[] if nk == 1 else [pltpu.VMEM((tm, tn), F32)],
        compiler_params=_params(("parallel", "parallel", "arbitrary")),
    )(*ins)


def _window(a):
    return (a[0], *a[1]) if isinstance(a, tuple) else (a, a.shape[1], 0)


def _rowwise(name, fn, rows, mats, outs, reds, ts):
    rows, widths, blocks = zip(*[_window(a) for a in rows])
    S = rows[0].shape[0]
    nr, nm, no = len(rows), len(mats), len(outs)

    def body(*refs):
        res = fn(*[r[...] for r in refs[:nr + nm]])
        res = res if isinstance(res, (tuple, list)) else (res,)
        for r, v in zip(refs[nr + nm:nr + nm + no], res[:no]):
            r[...] = v.astype(r.dtype)
        first = pl.program_id(0) == 0
        for r, v in zip(refs[nr + nm + no:], res[no:]):
            @pl.when(first)
            def _():
                r[...] = jnp.broadcast_to(v, r.shape)

            @pl.when(jnp.logical_not(first))
            def _():
                r[...] += jnp.broadcast_to(v, r.shape)

    in_specs = [pl.BlockSpec((ts, w), lambda i, b=b: (i, b)) for w, b in zip(widths, blocks)]
    in_specs += [pl.BlockSpec(m.shape, lambda i, nd=m.ndim: (0,) * nd) for m in mats]
    out_specs = [pl.BlockSpec((ts, w), lambda i: (i, 0)) for w, _ in outs]
    out_specs += [pl.BlockSpec(s, lambda i: (0, 0)) for s in reds]
    out_shape = [jax.ShapeDtypeStruct((S, w), dt) for w, dt in outs] + [jax.ShapeDtypeStruct(s, F32) for s in reds]
    return pl.pallas_call(
        body, name=name, grid=(S // ts,), in_specs=in_specs, out_specs=out_specs, out_shape=out_shape,
        compiler_params=_params(("arbitrary",) if reds else ("parallel",)),
    )(*rows, *mats)


def _shift_down(v, s):
    if s == 0:
        return v
    rows = lax.broadcasted_iota(I32, v.shape, 0)
    return jnp.where(rows >= s, pltpu.roll(v, s, 0), 0.0)


def _shift_up(v, s):
    if s == 0:
        return v
    n = v.shape[0]
    rows = lax.broadcasted_iota(I32, v.shape, 0)
    return jnp.where(rows < n - s, pltpu.roll(v, n - s, 0), 0.0)


def _conv(x, w, b):
    K = w.shape[0]
    y = jnp.broadcast_to(b, x.shape)
    for k in range(K):
        y = y + w[k:k + 1, :] * _shift_down(x, K - 1 - k)
    return y


def _conv_back(x, w, dc):
    K = w.shape[0]
    dx = jnp.zeros_like(x)
    dw = []
    for k in range(K):
        up = _shift_up(dc, K - 1 - k)
        dx = dx + w[k:k + 1, :] * up
        dw.append(_colsum(up * x))
    return dx, jnp.concatenate(dw, axis=0), _colsum(dc)


def _colwise(name, fn, cols, vecs, outs, pouts, tc):
    cols, widths, blocks = zip(*[_window(a) for a in cols])
    S, C = cols[0].shape[0], widths[0]
    firsts = [b * (C // tc) for b in blocks]
    nc_, nv, no = len(cols), len(vecs), len(outs)

    def body(*refs):
        res = fn(*[r[...] for r in refs[:nc_ + nv]])
        res = res if isinstance(res, (tuple, list)) else (res,)
        for r, v in zip(refs[nc_ + nv:], res):
            r[...] = v.astype(r.dtype)

    in_specs = [pl.BlockSpec((S, tc), lambda j, f=f: (0, f + j)) for f in firsts]
    in_specs += [pl.BlockSpec((v.shape[0], tc), lambda j: (0, j)) for v in vecs]
    out_specs = [pl.BlockSpec((S, tc), lambda j: (0, j)) for _ in outs] + [pl.BlockSpec((k, tc), lambda j: (0, j)) for k in pouts]
    out_shape = [jax.ShapeDtypeStruct((S, C), dt) for dt in outs] + [jax.ShapeDtypeStruct((k, C), F32) for k in pouts]
    return pl.pallas_call(
        body, name=name, grid=(C // tc,), in_specs=in_specs, out_specs=out_specs, out_shape=out_shape,
        compiler_params=_params(("parallel",)),
    )(*cols, *vecs)


_G0, _G1 = math.sqrt(2.0 / math.pi), 0.044715


def _gelu(g):
    th = jnp.tanh(_G0 * (g + _G1 * g * g * g))
    return 0.5 * g * (1.0 + th), th


def _ffn_act(gate_pre, up, w, b):
    act, _ = _gelu(_conv(gate_pre, w, b))
    return act * up


def _ffn_act_back(dact, gate_pre, up, w, b):
    g = _conv(gate_pre, w, b)
    ge, th = _gelu(g)
    dge = 0.5 * (1.0 + th) + 0.5 * g * (1.0 - th * th) * _G0 * (1.0 + 3.0 * _G1 * g * g)
    dup = dact * ge
    dgate_pre, dw, db = _conv_back(gate_pre, w, dact * up * dge)
    return dgate_pre, dup, dw, db


def _ssm_act(xbc, w, b):
    c = _conv(xbc, w, b)
    return c * _sigmoid(c)


def _ssm_act_back(dxc, xbc, w, b):
    c = _conv(xbc, w, b)
    sg = _sigmoid(c)
    return _conv_back(xbc, w, dxc * sg * (1.0 + c * (1.0 - sg)))


def _rope_tables(S):
    inv = 1.0 / (ROPE_THETA ** (jnp.arange(0, ROPE, 2, dtype=F32) / ROPE))
    ang = jnp.arange(S, dtype=F32)[:, None] * inv[None, :]
    cos, sin = jnp.cos(ang), jnp.sin(ang)
    return jnp.tile(cos, (1, 4)), jnp.tile(jnp.concatenate([-sin, sin], axis=1), (1, 2))


def _swap_halves(x):
    lane = lax.broadcasted_iota(I32, x.shape, 1)
    w = x.shape[1]
    return jnp.where((lane % ROPE) < ROPE // 2, pltpu.roll(x, w - ROPE // 2, 1), pltpu.roll(x, ROPE // 2, 1))


def _rot(x, cos2, sin2):
    return x * cos2 + _swap_halves(x) * sin2


def _rot_back(dy, cos2, sin2):
    return dy * cos2 + _swap_halves(dy * sin2)


def _mla_pack(cfg, q, kv, kr, cos2, sin2):
    S, H = cfg.S, cfg.H
    ts = _pick(S, 256, 8)
    kr, _, kr_block = _window(kr)

    def body(q_ref, kv_ref, kr_ref, c_ref, s_ref, Q_ref, K_ref, V_ref):
        c2, s2 = c_ref[...], s_ref[...]
        krr = _rot(kr_ref[...], c2, s2)
        kr_half = (krr.astype(BF16), pltpu.roll(krr, ROPE, 1).astype(BF16))
        for j in range(H // 2):
            qr = _rot(q_ref[:, (H + j) * LANE:(H + j + 1) * LANE], c2, s2).astype(BF16)
            for h in (2 * j, 2 * j + 1):
                Q_ref[h, :, 0:LANE] = q_ref[:, h * LANE:(h + 1) * LANE].astype(BF16)
                Q_ref[h, :, LANE:] = qr
                K_ref[h, :, 0:LANE] = kv_ref[:, h * LANE:(h + 1) * LANE].astype(BF16)
                K_ref[h, :, LANE:] = kr_half[h % 2]
                V_ref[h] = kv_ref[:, (H + h) * LANE:(H + h + 1) * LANE].astype(BF16)

    tab = pl.BlockSpec((ts, LANE), lambda i: (i, 0))
    heads = lambda w: pl.BlockSpec((H, ts, w), lambda i: (0, i, 0))
    return pl.pallas_call(
        body, name="mla_pack", grid=(S // ts,),
        in_specs=[pl.BlockSpec((ts, cfg.QW), lambda i: (i, 0)), pl.BlockSpec((ts, cfg.KVW), lambda i: (i, 0)),
                  pl.BlockSpec((ts, LANE), lambda i: (i, kr_block)), tab, tab],
        out_specs=[heads(2 * LANE), heads(2 * LANE), heads(LANE)],
        out_shape=[jax.ShapeDtypeStruct((H, S, 2 * LANE), BF16), jax.ShapeDtypeStruct((H, S, 2 * LANE), BF16),
                   jax.ShapeDtypeStruct((H, S, LANE), BF16)],
        compiler_params=_params(("parallel",)),
    )(q, kv, kr, cos2, sin2)


def _mla_unpack(cfg, dQ, dK, dV, cos2, sin2):
    S, H = cfg.S, cfg.H
    ts = _pick(S, 256, 8)

    def body(dQ_ref, dK_ref, dV_ref, c_ref, s_ref, dq_ref, dkv_ref, dkr_ref):
        c2, s2 = c_ref[...], s_ref[...]
        lo = lax.broadcasted_iota(I32, (ts, LANE), 1) < ROPE
        tk = jnp.zeros((ts, LANE), F32)
        for h in range(H):
            dq_ref[:, h * LANE:(h + 1) * LANE] = dQ_ref[h, :, 0:LANE].astype(BF16)
            dkv_ref[:, h * LANE:(h + 1) * LANE] = dK_ref[h, :, 0:LANE].astype(BF16)
            dkv_ref[:, (H + h) * LANE:(H + h + 1) * LANE] = dV_ref[h].astype(BF16)
            own = lo if h % 2 == 0 else jnp.logical_not(lo)
            tk = tk + jnp.where(own, dK_ref[h, :, LANE:], 0.0)
        for j in range(H // 2):
            dr = dQ_ref[2 * j, :, LANE:] + dQ_ref[2 * j + 1, :, LANE:]
            dq_ref[:, (H + j) * LANE:(H + j + 1) * LANE] = _rot_back(dr, c2, s2).astype(BF16)
        dkr_rot = jnp.where(lo, tk + pltpu.roll(tk, ROPE, 1), 0.0)
        dkr_ref[...] = _rot_back(dkr_rot, c2, s2).astype(BF16)

    tab = pl.BlockSpec((ts, LANE), lambda i: (i, 0))
    return pl.pallas_call(
        body, name="mla_unpack", grid=(S // ts,),
        in_specs=[pl.BlockSpec((H, ts, 2 * LANE), lambda i: (0, i, 0)), pl.BlockSpec((H, ts, 2 * LANE), lambda i: (0, i, 0)),
                  pl.BlockSpec((H, ts, LANE), lambda i: (0, i, 0)), tab, tab],
        out_specs=[pl.BlockSpec((ts, cfg.QW), lambda i: (i, 0)), pl.BlockSpec((ts, cfg.KVW), lambda i: (i, 0)), tab],
        out_shape=[jax.ShapeDtypeStruct((S, cfg.QW), BF16), jax.ShapeDtypeStruct((S, cfg.KVW), BF16),
                   jax.ShapeDtypeStruct((S, LANE), BF16)],
        compiler_params=_params(("parallel",)),
    )(dQ, dK, dV, cos2, sin2)


_ATT_T = 256
_ATT_HB = 8
_ATT_SCALE = (NOPE + ROPE) ** -0.5


def _diag_mask(transposed=False):
    r = lax.broadcasted_iota(I32, (_ATT_T, _ATT_T), 0) // CHUNK
    c = lax.broadcasted_iota(I32, (_ATT_T, _ATT_T), 1) // CHUNK
    return r <= c if transposed else c <= r


def _row_form(col):
    return jnp.broadcast_to(col, (col.shape[0], LANE)).T[0:8, :]


def _attn_fwd(cfg, Q, K, V):
    S, H, T, HB = cfg.S, cfg.H, _ATT_T, min(cfg.H, _ATT_HB)

    def body(q_ref, k_ref, v_ref, o_ref, lse_ref, lse_t_ref):
        qi = pl.program_id(1)

        def head_step(b, kb, carry, mask):
            m, l, acc = carry
            ks = pl.multiple_of(kb * T, T)
            s = _dot(q_ref[b], k_ref[b, pl.ds(ks, T), :], NT) * _ATT_SCALE
            if mask is not None:
                s = jnp.where(mask, s, -1e30)
            m_new = jnp.maximum(m, jnp.max(s, axis=1, keepdims=True))
            p = jnp.exp(s - m_new)
            alpha = jnp.exp(m - m_new)
            l = alpha * l + jnp.sum(p, axis=1, keepdims=True)
            acc = alpha * acc + _dot(p.astype(BF16), v_ref[b, pl.ds(ks, T), :])
            return m_new, l, acc

        def step(kb, carry, mask=None):
            return tuple(head_step(b, kb, carry[b], mask) for b in range(HB))

        init = (jnp.full((T, 1), -1e30, F32), jnp.zeros((T, 1), F32), jnp.zeros((T, VH), F32))
        done = step(qi, lax.fori_loop(0, qi, step, (init,) * HB), _diag_mask())
        for b, (m, l, acc) in enumerate(done):
            o_ref[:, b * LANE:(b + 1) * LANE] = acc / l
            lse = m + jnp.log(l)
            lse_ref[:, b * LANE:(b + 1) * LANE] = jnp.broadcast_to(lse, (T, LANE))
            lse_t_ref[b] = _row_form(lse)

    return pl.pallas_call(
        body, name="attn_fwd", grid=(H // HB, S // T),
        in_specs=[pl.BlockSpec((HB, T, 2 * LANE), lambda h, i: (h, i, 0)), pl.BlockSpec((HB, S, 2 * LANE), lambda h, i: (h, 0, 0)),
                  pl.BlockSpec((HB, S, LANE), lambda h, i: (h, 0, 0))],
        out_specs=[pl.BlockSpec((T, HB * LANE), lambda h, i: (i, h)), pl.BlockSpec((T, HB * LANE), lambda h, i: (i, h)),
                   pl.BlockSpec((HB, 8, T), lambda h, i: (h, 0, i))],
        out_shape=[jax.ShapeDtypeStruct((S, H * LANE), F32), jax.ShapeDtypeStruct((S, H * LANE), F32),
                   jax.ShapeDtypeStruct((H, 8, S), F32)],
        compiler_params=_params(("parallel", "parallel")),
    )(Q, K, V)


def _attn_dq(cfg, Q, K, V, do, o, lse, after):
    S, H, T, HB = cfg.S, cfg.H, _ATT_T, min(cfg.H, _ATT_HB)

    def body(q_ref, k_ref, v_ref, do_ref, o_ref, lse_ref, after_ref, dq_ref, dl_t_ref):
        qi = pl.program_id(1)
        do = [do_ref[:, b * LANE:(b + 1) * LANE] for b in range(HB)]
        delta = [jnp.sum(do[b] * o_ref[:, b * LANE:(b + 1) * LANE], axis=1, keepdims=True) for b in range(HB)]
        dob = [d.astype(BF16) for d in do]

        def head_step(b, kb, dq, mask):
            ks = pl.multiple_of(kb * T, T)
            k = k_ref[b, pl.ds(ks, T), :]
            s = _dot(q_ref[b], k, NT) * _ATT_SCALE
            if mask is not None:
                s = jnp.where(mask, s, -1e30)
            p = jnp.exp(s - lse_ref[:, b * LANE:b * LANE + 1])
            dp = _dot(dob[b], v_ref[b, pl.ds(ks, T), :], NT)
            ds = p * (dp - delta[b]) * _ATT_SCALE
            return dq + _dot(ds.astype(BF16), k)

        def step(kb, dqs, mask=None):
            return tuple(head_step(b, kb, dqs[b], mask) for b in range(HB))

        dqs = step(qi, lax.fori_loop(0, qi, step, (jnp.zeros((T, 2 * LANE), F32),) * HB), _diag_mask())
        for b in range(HB):
            dq_ref[b] = dqs[b]
            dl_t_ref[b] = _row_form(delta[b])

    col = pl.BlockSpec((T, HB * LANE), lambda h, i: (i, h))
    return pl.pallas_call(
        body, name="attn_dq", grid=(H // HB, S // T),
        in_specs=[pl.BlockSpec((HB, T, 2 * LANE), lambda h, i: (h, i, 0)), pl.BlockSpec((HB, S, 2 * LANE), lambda h, i: (h, 0, 0)),
                  pl.BlockSpec((HB, S, LANE), lambda h, i: (h, 0, 0)), col, col, col, _ANY],
        out_specs=[pl.BlockSpec((HB, T, 2 * LANE), lambda h, i: (h, i, 0)), pl.BlockSpec((HB, 8, T), lambda h, i: (h, 0, i))],
        out_shape=[jax.ShapeDtypeStruct((H, S, 2 * LANE), F32), jax.ShapeDtypeStruct((H, 8, S), F32)],
        compiler_params=_params(("parallel", "parallel")),
    )(Q, K, V, do, o, lse, after)


def _attn_dkv(cfg, Q, K, V, do, lse_t, delta_t):
    S, H, T, HB = cfg.S, cfg.H, _ATT_T, min(cfg.H, _ATT_HB)
    nq = S // T

    def body(q_ref, k_ref, v_ref, do_ref, lse_ref, dl_ref, dk_ref, dv_ref):
        kb = pl.program_id(1)

        def head_step(b, qi, carry, mask):
            dk, dv = carry
            qs = pl.multiple_of(qi * T, T)
            q = q_ref[b, pl.ds(qs, T), :]
            dob = do_ref[pl.ds(qs, T), b * LANE:(b + 1) * LANE].astype(BF16)
            s = _dot(k_ref[b], q, NT) * _ATT_SCALE
            if mask is not None:
                s = jnp.where(mask, s, -1e30)
            p = jnp.exp(s - lse_ref[b, 0:1, pl.ds(qs, T)])
            dv = dv + _dot(p.astype(BF16), dob)
            dp = _dot(v_ref[b], dob, NT)
            ds = p * (dp - dl_ref[b, 0:1, pl.ds(qs, T)]) * _ATT_SCALE
            dk = dk + _dot(ds.astype(BF16), q)
            return dk, dv

        def step(qi, carry, mask=None):
            return tuple(head_step(b, qi, carry[b], mask) for b in range(HB))

        zero = (jnp.zeros((T, 2 * LANE), F32), jnp.zeros((T, VH), F32))
        done = lax.fori_loop(kb + 1, nq, step, step(kb, (zero,) * HB, _diag_mask(transposed=True)))
        for b, (dk, dv) in enumerate(done):
            dk_ref[b] = dk
            dv_ref[b] = dv

    row = pl.BlockSpec((HB, 8, S), lambda h, j: (h, 0, 0))
    return pl.pallas_call(
        body, name="attn_dkv", grid=(H // HB, S // T),
        in_specs=[pl.BlockSpec((HB, S, 2 * LANE), lambda h, j: (h, 0, 0)), pl.BlockSpec((HB, T, 2 * LANE), lambda h, j: (h, j, 0)),
                  pl.BlockSpec((HB, T, LANE), lambda h, j: (h, j, 0)), pl.BlockSpec((S, HB * LANE), lambda h, j: (0, h)), row, row],
        out_specs=[pl.BlockSpec((HB, T, 2 * LANE), lambda h, j: (h, j, 0)), pl.BlockSpec((HB, T, LANE), lambda h, j: (h, j, 0))],
        out_shape=[jax.ShapeDtypeStruct((H, S, 2 * LANE), F32), jax.ShapeDtypeStruct((H, S, LANE), F32)],
        compiler_params=_params(("parallel", "parallel")),
    )(Q, K, V, do, lse_t, delta_t)


def _attn_delta(cfg, do, o, after):
    S, H, T = cfg.S, cfg.H, _ATT_T

    def body(do_ref, o_ref, after_ref, dl_t_ref):
        for h in range(H):
            sl = slice(h * LANE, (h + 1) * LANE)
            dl_t_ref[h] = _row_form(jnp.sum(do_ref[:, sl] * o_ref[:, sl], axis=1, keepdims=True))

    wide = pl.BlockSpec((T, H * LANE), lambda i: (i, 0))
    return pl.pallas_call(
        body, name="attn_delta", grid=(S // T,), in_specs=[wide, wide, _ANY],
        out_specs=pl.BlockSpec((H, 8, T), lambda i: (0, 0, i)), out_shape=jax.ShapeDtypeStruct((H, 8, S), F32),
        compiler_params=_params(("parallel",)),
    )(do, o, after)


_ATT_HB_BWD = 4


def _attn_bwd(cfg, Q, K, V, do, lse_t, delta_t):
    S, H, T, HB = cfg.S, cfg.H, _ATT_T, min(cfg.H, _ATT_HB_BWD)
    nq = S // T

    def body(q_ref, k_ref, v_ref, do_ref, lse_ref, dl_ref, dq_ref, dk_ref, dv_ref):
        kb = pl.program_id(1)

        @pl.when(kb == 0)
        def _():
            dq_ref[...] = jnp.zeros_like(dq_ref)

        def head_step(b, qi, carry, mask):
            dk, dv = carry
            qs = pl.multiple_of(qi * T, T)
            q = q_ref[b, pl.ds(qs, T), :]
            k = k_ref[b]
            dob = do_ref[pl.ds(qs, T), b * LANE:(b + 1) * LANE].astype(BF16)
            s = _dot(k, q, NT) * _ATT_SCALE
            if mask is not None:
                s = jnp.where(mask, s, -1e30)
            p = jnp.exp(s - lse_ref[b, 0:1, pl.ds(qs, T)])
            dv = dv + _dot(p.astype(BF16), dob)
            dp = _dot(v_ref[b], dob, NT)
            ds = (p * (dp - dl_ref[b, 0:1, pl.ds(qs, T)]) * _ATT_SCALE).astype(BF16)
            dk = dk + _dot(ds, q)
            dq_ref[b, pl.ds(qs, T), :] += _dot(ds, k, TN)
            return dk, dv

        def step(qi, carry, mask=None):
            return tuple(head_step(b, qi, carry[b], mask) for b in range(HB))

        zero = (jnp.zeros((T, 2 * LANE), F32), jnp.zeros((T, VH), F32))
        done = lax.fori_loop(kb + 1, nq, step, step(kb, (zero,) * HB, _diag_mask(transposed=True)))
        for b, (dk, dv) in enumerate(done):
            dk_ref[b] = dk
            dv_ref[b] = dv

    row = pl.BlockSpec((HB, 8, S), lambda h, j: (h, 0, 0))
    whole = pl.BlockSpec((HB, S, 2 * LANE), lambda h, j: (h, 0, 0))
    return pl.pallas_call(
        body, name="attn_bwd", grid=(H // HB, S // T),
        in_specs=[whole, pl.BlockSpec((HB, T, 2 * LANE), lambda h, j: (h, j, 0)), pl.BlockSpec((HB, T, LANE), lambda h, j: (h, j, 0)),
                  pl.BlockSpec((S, HB * LANE), lambda h, j: (0, h)), row, row],
        out_specs=[whole, pl.BlockSpec((HB, T, 2 * LANE), lambda h, j: (h, j, 0)), pl.BlockSpec((HB, T, LANE), lambda h, j: (h, j, 0))],
        out_shape=[jax.ShapeDtypeStruct((H, S, 2 * LANE), F32), jax.ShapeDtypeStruct((H, S, 2 * LANE), F32),
                   jax.ShapeDtypeStruct((H, S, LANE), F32)],
        compiler_params=_params(("parallel", "arbitrary")),
    )(Q, K, V, do, lse_t, delta_t)


def _expand_matrix(cfg):
    r = lax.broadcasted_iota(I32, (LANE, cfg.INNER), 0)
    c = lax.broadcasted_iota(I32, (LANE, cfg.INNER), 1)
    return (r == c // HP).astype(F32)


def _softplus(x):
    return jnp.maximum(x, 0.0) + jnp.log(1.0 + jnp.exp(-jnp.abs(x)))


def _ssd_prep(cfg, dt_raw, dt_bias_pad, a_log_pad, expand):
    HS = cfg.HS

    def fn(raw, bias, alog, E):
        heads = lax.broadcasted_iota(I32, raw.shape, 1) < HS
        dt = jnp.where(heads, _softplus(raw + bias), 0.0)
        a = dt * jnp.where(heads[0:1], -jnp.exp(alog), 0.0)
        return dt, a, _dot(dt, E, precision=HI)

    return _rowwise("ssd_prep", fn, [dt_raw], [dt_bias_pad, a_log_pad, expand],
                    [(LANE, F32), (LANE, F32), (cfg.INNER, F32)], [], _pick(cfg.S, 512, 8))


def _tril(T):
    return lax.broadcasted_iota(I32, (T, T), 0) >= lax.broadcasted_iota(I32, (T, T), 1)


def _ssd_fwd(cfg, xc, dt_exp, a_small, dskip_exp, expand):
    S, T, INNER, G, NPAIR = cfg.S, cfg.T, cfg.INNER, cfg.G, cfg.NPAIR
    NC = S // T

    def body(xc_ref, dte_ref, as_ref, dsk_ref, e_ref, y_ref, hin_ref, ht_ref):
        @pl.when(pl.program_id(0) == 0)
        def _():
            ht_ref[...] = jnp.zeros_like(ht_ref)

        tril = _tril(T)
        tri = tril.astype(F32)
        acs_s = _dot(tri, as_ref[...], precision=HI)
        acs_e = _dot(acs_s, e_ref[...], precision=HI)
        acs_t = acs_s.T
        lo = lax.broadcasted_iota(I32, (T, LANE), 1) < HP
        for g in range(G):
            Bb = xc_ref[:, INNER + g * NST:INNER + (g + 1) * NST].astype(BF16)
            Cb = xc_ref[:, INNER + (G + g) * NST:INNER + (G + g + 1) * NST].astype(BF16)
            Gm = _dot(Cb, Bb, NT)
            for j in range(g * NPAIR // G, (g + 1) * NPAIR // G):
                sl = slice(j * LANE, (j + 1) * LANE)
                Xp = xc_ref[:, sl]
                Xdt = Xp * dte_ref[:, sl]
                Xb = Xdt.astype(BF16)
                acs_p = acs_e[:, sl]
                last = acs_p[T - 1:T, :]
                Hin = ht_ref[j]
                hin_ref[0, j] = Hin
                yd = []
                for e in (0, 1):
                    h = 2 * j + e
                    Lm = jnp.exp(jnp.where(tril, acs_s[:, h:h + 1] - acs_t[h:h + 1, :], -1e30))
                    yd.append(_dot((Gm * Lm).astype(BF16), Xb))
                y_off = _dot(Cb, Hin.astype(BF16)) * jnp.exp(acs_p)
                y_ref[:, sl] = jnp.where(lo, yd[0], yd[1]) + y_off + Xp * dsk_ref[:, sl]
                st = _dot(Bb, (Xdt * jnp.exp(last - acs_p)).astype(BF16), TN)
                ht_ref[j] = jnp.exp(last) * Hin + st

    rows = lambda w: pl.BlockSpec((T, w), lambda c: (c, 0))
    return pl.pallas_call(
        body, name="ssd_fwd", grid=(NC,),
        in_specs=[rows(cfg.CONVCH), rows(INNER), rows(LANE), pl.BlockSpec((1, INNER), lambda c: (0, 0)),
                  pl.BlockSpec((LANE, INNER), lambda c: (0, 0))],
        out_specs=[rows(INNER), pl.BlockSpec((1, NPAIR, NST, LANE), lambda c: (c, 0, 0, 0))],
        out_shape=[jax.ShapeDtypeStruct((S, INNER), F32), jax.ShapeDtypeStruct((NC, NPAIR, NST, LANE), F32)],
        scratch_shapes=[pltpu.VMEM((NPAIR, NST, LANE), F32)],
        compiler_params=_params(("arbitrary",)),
    )(xc, dt_exp, a_small, dskip_exp, expand)


def _ssd_bwd(cfg, dy, xc, dt_exp, a_small, dskip_exp, hin, dt_raw, dt_bias_pad, a_log_pad, expand):
    S, T, INNER, G, NPAIR, HS = cfg.S, cfg.T, cfg.INNER, cfg.G, cfg.NPAIR, cfg.HS
    NC = S // T

    def body(dy_ref, xc_ref, dte_ref, as_ref, dsk_ref, hin_ref, raw_ref, bias_ref, alog_ref, e_ref,
             dxc_ref, draw_ref, dbias_ref, dalog_ref, dskip_ref, dht_ref, cols_ref, rows_ref, dacs_ref, ddt_ref):
        first = pl.program_id(0) == 0

        @pl.when(first)
        def _():
            dht_ref[...] = jnp.zeros_like(dht_ref)

        tril = _tril(T)
        tri = tril.astype(F32)
        a_s = as_ref[...]
        acs_s = _dot(tri, a_s, precision=HI)
        acs_e = _dot(acs_s, e_ref[...], precision=HI)
        acs_t = acs_s.T
        lo = lax.broadcasted_iota(I32, (T, LANE), 1) < HP
        last_row = lax.broadcasted_iota(I32, (T, LANE), 0) == T - 1
        cols_ref[...] = jnp.zeros_like(cols_ref)
        rows_ref[...] = jnp.zeros_like(rows_ref)
        dsk_parts = []
        for g in range(G):
            bsl = slice(INNER + g * NST, INNER + (g + 1) * NST)
            csl = slice(INNER + (G + g) * NST, INNER + (G + g + 1) * NST)
            Bb = xc_ref[:, bsl].astype(BF16)
            Cb = xc_ref[:, csl].astype(BF16)
            Gm = _dot(Cb, Bb, NT)
            dG = jnp.zeros((T, T), F32)
            dB = jnp.zeros((T, NST), F32)
            dC = jnp.zeros((T, NST), F32)
            for j in range(g * NPAIR // G, (g + 1) * NPAIR // G):
                sl = slice(j * LANE, (j + 1) * LANE)
                Xp = xc_ref[:, sl]
                dtp = dte_ref[:, sl]
                Xdt = Xp * dtp
                Xb = Xdt.astype(BF16)
                acs_p = acs_e[:, sl]
                last = acs_p[T - 1:T, :]
                e_p, dec, cd = jnp.exp(acs_p), jnp.exp(last - acs_p), jnp.exp(last)
                Hin = hin_ref[0, j]
                Hb = Hin.astype(BF16)
                dHn = dht_ref[j]
                dHb = dHn.astype(BF16)
                dYp = dy_ref[:, sl]
                z = _dot(Cb, Hb)
                dz = (dYp * e_p).astype(BF16)
                dacs_p = dYp * z * e_p
                dC = dC + _dot(dz, Hb, NT)
                dHin = _dot(Cb, dz, TN) + cd * dHn
                dlast = _colsum(dHn * Hin) * cd
                qv = _dot(Bb, dHb)
                dXdt = qv * dec
                ddec = qv * Xdt * dec
                dacs_p = dacs_p - ddec
                dlast = dlast + _colsum(ddec)
                dB = dB + _dot((Xdt * dec).astype(BF16), dHb, NT)
                for e in (0, 1):
                    h = 2 * j + e
                    Lm = jnp.exp(jnp.where(tril, acs_s[:, h:h + 1] - acs_t[h:h + 1, :], -1e30))
                    Mh = Gm * Lm
                    dYe = jnp.where(lo if e == 0 else jnp.logical_not(lo), dYp, 0.0).astype(BF16)
                    dM = _dot(dYe, Xb, NT)
                    dXdt = dXdt + _dot(Mh.astype(BF16), dYe, TN)
                    W = dM * Mh
                    cols_ref[:, h:h + 1] = jnp.sum(W, axis=1, keepdims=True)
                    rows_ref[h:h + 1, :] = _colsum(W)
                    dG = dG + dM * Lm
                dacs_ref[:, sl] = dacs_p + jnp.where(last_row, dlast, 0.0)
                ddt_ref[:, sl] = dXdt * Xp
                dxc_ref[:, sl] = dXdt * dtp + dYp * dsk_ref[:, sl]
                dsk_parts.append(_colsum(dYp * Xp))
                dht_ref[j] = dHin
            dGb = dG.astype(BF16)
            dxc_ref[:, bsl] = dB + _dot(dGb, Cb, TN)
            dxc_ref[:, csl] = dC + _dot(dGb, Bb)
        E = e_ref[...]
        dacs_s = cols_ref[...] - rows_ref[...].T + _dot(dacs_ref[...], E, NT, precision=HI)
        da = _dot(tri, dacs_s, TN, precision=HI)
        heads = lax.broadcasted_iota(I32, (1, LANE), 1) < HS
        A = jnp.where(heads, -jnp.exp(alog_ref[...]), 0.0)
        ddt = _dot(ddt_ref[...], E, NT, precision=HI) + da * A
        draw = jnp.where(heads, ddt * _sigmoid(raw_ref[...] + bias_ref[...]), 0.0)
        draw_ref[...] = draw
        dsk = _dot(jnp.broadcast_to(jnp.concatenate(dsk_parts, axis=1), (8, INNER)), E, NT, precision=HI)[0:1]
        for ref, val in ((dbias_ref, _colsum(draw)), (dalog_ref, _colsum(da * a_s)), (dskip_ref, dsk)):
            @pl.when(first)
            def _():
                ref[...] = val

            @pl.when(jnp.logical_not(first))
            def _():
                ref[...] += val

    dt_raw, _, raw_block = _window(dt_raw)
    rows = lambda w, b=0: pl.BlockSpec((T, w), lambda c: (NC - 1 - c, b))
    vec = lambda w: pl.BlockSpec((1, w), lambda c: (0, 0))
    return pl.pallas_call(
        body, name="ssd_bwd", grid=(NC,),
        in_specs=[rows(INNER), rows(cfg.CONVCH), rows(INNER), rows(LANE), vec(INNER),
                  pl.BlockSpec((1, NPAIR, NST, LANE), lambda c: (NC - 1 - c, 0, 0, 0)), rows(LANE, raw_block), vec(LANE), vec(LANE),
                  pl.BlockSpec((LANE, INNER), lambda c: (0, 0))],
        out_specs=[rows(cfg.CONVCH), rows(LANE), vec(LANE), vec(LANE), vec(LANE)],
        out_shape=[jax.ShapeDtypeStruct((S, cfg.CONVCH), F32), jax.ShapeDtypeStruct((S, LANE), F32)]
        + [jax.ShapeDtypeStruct((1, LANE), F32)] * 3,
        scratch_shapes=[pltpu.VMEM((NPAIR, NST, LANE), F32), pltpu.VMEM((T, LANE), F32), pltpu.VMEM((LANE, T), F32),
                        pltpu.VMEM((T, INNER), F32), pltpu.VMEM((T, INNER), F32)],
        compiler_params=_params(("arbitrary",)),
    )(dy, xc, dt_exp, a_small, dskip_exp, hin, dt_raw, dt_bias_pad, a_log_pad, expand)


def _ssd_post(cfg, y, z, norm_g):
    W = cfg.INNER // cfg.G

    def fn(y, z, g):
        yz = y * z * _sigmoid(z)
        return jnp.concatenate([yz[:, i * W:(i + 1) * W] * _rs(yz[:, i * W:(i + 1) * W]) for i in range(cfg.G)], axis=1) * g

    return _rowwise("ssd_post", fn, [y, z], [norm_g], [(cfg.INNER, BF16)], [], _pick(cfg.S, 256, 8))[0]


def _ssd_post_bwd(cfg, db, y, z, norm_g):
    W = cfg.INNER // cfg.G

    def fn(db, y, z, g):
        sg = _sigmoid(z)
        yz = y * z * sg
        dn = db * g
        dyz, nh = [], []
        for i in range(cfg.G):
            seg = yz[:, i * W:(i + 1) * W]
            r = _rs(seg)
            nh.append(seg * r)
            dyz.append(_rms_back(nh[-1], r, dn[:, i * W:(i + 1) * W]))
        dyz = jnp.concatenate(dyz, axis=1)
        return dyz * z * sg, dyz * y * sg * (1.0 + z * (1.0 - sg)), _colsum(db * jnp.concatenate(nh, axis=1))

    return _rowwise("ssd_post_bwd", fn, [db, y, z], [norm_g], [(cfg.INNER, F32), (cfg.INNER, F32)], [(1, cfg.INNER)],
                    _pick(cfg.S, 256, 8))


def _rms_pre(cfg, x, g):
    return _rowwise("rms_pre", lambda x, g: x * _rs(x) * g, [x], [g], [(cfg.D, BF16)], [], _pick(cfg.S, 256, 8))[0]


def _local_grads(cfg, x, tgt, W, sp, mla_weights=None, out_weight=None, ffn_weights=None, down_weight=None,
                 ffn_grads_ready=None, early_grads_ready=None, in_grad_ready=None, xn=None, after_in=None):
    S, D, H, INNER = cfg.S, cfg.D, cfg.H, cfg.INNER
    ts = _pick(S, 256, 8)
    tc = _CONV_COLS

    if xn is None:
        xn = _rms_pre(cfg, x, sp["mix_pre_g"])
    u = _matmul("mm_in", xn, W["w_in"], "nt", F32, after=after_in)
    c_q, c_kv, kr, z, xbc, dt_raw = [(u, cfg.window(n)) for n in ("c_q", "c_kv", "kr", "z", "xbc", "dt")]

    if mla_weights is not None:
        sp = dict(sp, q_norm_g=sp["q_norm_g"] + mla_weights.pass_on(u)[0, 0])
    cqn = _rowwise("rms_q", lambda x, g: x * _rs(x) * g, [c_q], [sp["q_norm_g"]], [(cfg.QL, BF16)], [], ts)[0]
    ckvn = _rowwise("rms_kv", lambda x, g: x * _rs(x) * g, [c_kv], [sp["kv_norm_g"]], [(cfg.KVL, BF16)], [], ts)[0]
    if mla_weights is not None:
        W = dict(W, **mla_weights.arrived(ckvn))
    q = _matmul("mm_uq", cqn, W["w_uq"], "nn", F32)
    kv = _matmul("mm_ukv", ckvn, W["w_ukv"], "nn", F32)
    cos2, sin2 = _rope_tables(S)
    Qh, Kh, Vh = _mla_pack(cfg, q, kv, kr, cos2, sin2)
    a_out, lse, lse_t = _attn_fwd(cfg, Qh, Kh, Vh)
    if out_weight is not None:
        sp = dict(sp, ssm_conv_b=sp["ssm_conv_b"] + out_weight.pass_on(a_out)[0, 0])

    pad = lambda v: jnp.pad(v, ((0, 0), (0, LANE - v.shape[1])))
    expand = _expand_matrix(cfg)
    dt_bias_pad, a_log_pad = pad(sp["dt_bias"]), pad(sp["a_log"])
    dskip_exp = jnp.repeat(sp["d_skip"], HP, axis=1)
    xc = _colwise("ssm_act", _ssm_act, [xbc], [sp["ssm_conv_w"], sp["ssm_conv_b"]], [F32], [], tc)[0]
    dt_s, a_s, dt_exp = _ssd_prep(cfg, dt_raw, dt_bias_pad, a_log_pad, expand)
    y_ssd, hin = _ssd_fwd(cfg, xc, dt_exp, a_s, dskip_exp, expand)
    b_out = _ssd_post(cfg, y_ssd, z, sp["ssm_norm_g"])

    ab_out = jnp.concatenate([a_out.astype(BF16), b_out], axis=1)
    if out_weight is not None:
        W = dict(W, **out_weight.arrived(ab_out))
    if ffn_weights is not None:
        sp = dict(sp, mix_post_g=sp["mix_post_g"] + ffn_weights.pass_on(ab_out)[0, 0])
    mix = _matmul("mm_out", ab_out, W["w_out"], "nn", F32)

    def mid(x, mix, g_mp, g_fp):
        x1 = x + mix * _rs(mix) * g_mp
        return x1, x1 * _rs(x1) * g_fp

    x1, h2 = _rowwise("fwd_mid", mid, [x, mix], [sp["mix_post_g"], sp["ffn_pre_g"]], [(D, F32), (D, BF16)], [], ts)
    if ffn_weights is not None:
        W = dict(W, **ffn_weights.arrived(h2))
    gate_pre = _matmul("mm_gate", h2, W["w_gate"], "nn", F32, chips=True)
    if down_weight is not None:
        sp = dict(sp, ffn_conv_b=sp["ffn_conv_b"] + down_weight.pass_on(gate_pre)[0, 0])
    up = _matmul("mm_up", h2, W["w_up"], "nn", F32, chips=True)
    act = _colwise("ffn_act", _ffn_act, [gate_pre, up], [sp["ffn_conv_w"], sp["ffn_conv_b"]], [BF16], [], tc)[0]
    if down_weight is not None:
        W = dict(W, **down_weight.arrived(act))
    f = _matmul("mm_down", act, W["w_down"], "nn", F32)

    def final(x1, f, t, g):
        r = _rs(f)
        fh = f * r
        err = x1 + fh * g - t
        loss = 0.5 * jnp.sum(jnp.mean(err * err, axis=-1, keepdims=True), axis=0, keepdims=True)
        dy = err * (1.0 / D)
        return dy, _rms_back(fh, r, dy * g), _colsum(dy * fh), loss

    dy, df, g_ffn_post, loss = _rowwise("final", final, [x1, f, tgt], [sp["ffn_post_g"]], [(D, F32), (D, BF16)],
                                        [(1, D), (1, LANE)], ts)
    gW = {}
    dact = _matmul("mm_down_dx", df, W["w_down"], "nt", F32)
    gW["w_down"] = _matmul("mm_down_dw", act, df, "tn", BF16)
    dgate, dup, g_ffn_conv_w, g_ffn_conv_b = _colwise(
        "ffn_act_bwd", _ffn_act_back, [dact, gate_pre, up], [sp["ffn_conv_w"], sp["ffn_conv_b"]], [BF16, BF16], [FFN_K, 1], tc)
    gW["w_gate"] = _matmul("mm_gate_dw", h2, dgate, "tn", BF16, chips=True)
    gW["w_up"] = _matmul("mm_up_dw", h2, dup, "tn", BF16, chips=True)
    if ffn_grads_ready is not None:
        sp = dict(sp, ffn_pre_g=sp["ffn_pre_g"] + ffn_grads_ready({n: gW[n] for n in ("w_down", "w_gate", "w_up")})[0, 0])
    dh2 = _matmul("mm_gu_dx", dgate, W["w_gate"], "nt", F32, dup, W["w_up"], chips=True)

    def mid_back(dy, dh2, x1, mix, g_mp, g_fp):
        r2 = _rs(x1)
        xh = x1 * r2
        dx1 = dy + _rms_back(xh, r2, dh2 * g_fp)
        r1 = _rs(mix)
        mh = mix * r1
        return dx1, _rms_back(mh, r1, dx1 * g_mp), _colsum(dh2 * xh), _colsum(dx1 * mh)

    dx1, dmix, g_ffn_pre, g_mix_post = _rowwise("bwd_mid", mid_back, [dy, dh2, x1, mix], [sp["mix_post_g"], sp["ffn_pre_g"]],
                                                [(D, F32), (D, BF16)], [(1, D), (1, D)], ts)
    dab_out = _matmul("mm_out_dx", dmix, W["w_out"], "nt", F32)
    db_out = (dab_out, (INNER, cfg.MLAW // INNER))
    gW["w_out"] = _matmul("mm_out_dw", ab_out, dmix, "tn", BF16)
    early_token = jnp.zeros((8, LANE), F32)
    if early_grads_ready is not None:
        early_token = early_grads_ready({n: gW[n] for n in ("w_down", "w_gate", "w_up", "w_out")})
        sp = dict(sp, ssm_norm_g=sp["ssm_norm_g"] + early_token[0, 0])

    dy_ssd, dz, g_ssm_norm = _ssd_post_bwd(cfg, db_out, y_ssd, z, sp["ssm_norm_g"])
    dxc, ddt_raw, g_dt_bias, g_a_log, g_d_skip = _ssd_bwd(cfg, dy_ssd, xc, dt_exp, a_s, dskip_exp, hin, dt_raw,
                                                          dt_bias_pad, a_log_pad, expand)
    dxbc, g_ssm_conv_w, g_ssm_conv_b = _colwise("ssm_act_bwd", _ssm_act_back, [dxc, xbc], [sp["ssm_conv_w"], sp["ssm_conv_b"]],
                                                [BF16], [SSM_K, 1], tc)

    delta_t = _attn_delta(cfg, dab_out, a_out, early_token)
    dQ, dK, dV = _attn_bwd(cfg, Qh, Kh, Vh, dab_out, lse_t, delta_t)
    dq, dkv, dkr = _mla_unpack(cfg, dQ, dK, dV, cos2, sin2)
    dcqn = _matmul("mm_uq_dx", dq, W["w_uq"], "nt", F32)
    dckvn = _matmul("mm_ukv_dx", dkv, W["w_ukv"], "nt", F32)
    gW["w_uq"] = _matmul("mm_uq_dw", cqn, dq, "tn", BF16)
    gW["w_ukv"] = _matmul("mm_ukv_dw", ckvn, dkv, "tn", BF16)

    def rms_back(x, dy, g):
        r = _rs(x)
        xh = x * r
        return _rms_back(xh, r, dy * g), _colsum(dy * xh)

    dc_q, g_q_norm = _rowwise("rms_q_bwd", rms_back, [c_q, dcqn], [sp["q_norm_g"]], [(cfg.QL, BF16)], [(1, cfg.QL)], ts)
    dc_kv, g_kv_norm = _rowwise("rms_kv_bwd", rms_back, [c_kv, dckvn], [sp["kv_norm_g"]], [(cfg.KVL, BF16)], [(1, cfg.KVL)], ts)

    du = dict(c_q=dc_q, c_kv=dc_kv, kr=dkr, z=dz.astype(BF16), xbc=dxbc, dt=ddt_raw.astype(BF16))
    du = jnp.concatenate([du[n] for n in sorted(du, key=lambda n: cfg.seg[n][0])], axis=1)
    assert du.shape[1] == cfg.EXT, "the layout of u has gaps"
    gW["w_in"] = _matmul("mm_in_dw", du, xn, "tn", BF16)
    if in_grad_ready is not None:
        token = in_grad_ready({n: gW[n] for n in ("w_in", "w_uq", "w_ukv")})
        sp = dict(sp, mix_pre_g=sp["mix_pre_g"] + token[0, 0])
    dxn = _matmul("mm_in_dx", du, W["w_in"], "nn", F32)

    def first_back(dx1, dxn, x, g):
        r = _rs(x)
        xh = x * r
        return dx1 + _rms_back(xh, r, dxn * g), _colsum(dxn * xh)

    grad_x, g_mix_pre = _rowwise("bwd_first", first_back, [dx1, dxn, x], [sp["mix_pre_g"]], [(D, F32)], [(1, D)], ts)

    gs = dict(mix_pre_g=g_mix_pre, q_norm_g=g_q_norm, kv_norm_g=g_kv_norm, ssm_conv_w=g_ssm_conv_w, ssm_conv_b=g_ssm_conv_b,
              dt_bias=g_dt_bias[:, :cfg.HS], a_log=g_a_log[:, :cfg.HS], d_skip=g_d_skip[:, :cfg.HS], ssm_norm_g=g_ssm_norm,
              mix_post_g=g_mix_post, ffn_pre_g=g_ffn_pre, ffn_conv_w=g_ffn_conv_w, ffn_conv_b=g_ffn_conv_b,
              ffn_post_g=g_ffn_post)
    return loss, grad_x, gW, gs


def _to_kernel_layout(cfg, name, w):
    if name == "w_in":
        parts, at = [], 0
        for off, width, n_off, n_width in sorted(cfg.seg.values()):
            parts += [jnp.zeros((off - at, w.shape[1]), w.dtype), w[n_off:n_off + n_width],
                      jnp.zeros((width - n_width, w.shape[1]), w.dtype)]
            at = off + width
        parts.append(jnp.zeros((cfg.EXT - at, w.shape[1]), w.dtype))
        return jnp.concatenate([p for p in parts if p.shape[0]], axis=0)
    if name in ("w_uq", "w_ukv"):
        per = NOPE + (ROPE if name == "w_uq" else VH)
        return jnp.concatenate([w[:, h * per:h * per + NOPE] for h in range(cfg.H)]
                               + [w[:, h * per + NOPE:(h + 1) * per] for h in range(cfg.H)], axis=1)
    return w


def _from_kernel_layout(cfg, name, g):
    if name == "w_in":
        return jnp.concatenate([g[off:off + n_width] for off, _, _, n_width in sorted(cfg.seg.values(), key=lambda s: s[2])], axis=0)
    if name in ("w_uq", "w_ukv"):
        second = ROPE if name == "w_uq" else VH
        base = cfg.H * NOPE
        parts = []
        for h in range(cfg.H):
            parts += [g[:, h * NOPE:(h + 1) * NOPE], g[:, base + h * second:base + (h + 1) * second]]
        return jnp.concatenate(parts, axis=1)
    return g


_CHIP_MAJOR = ("w_gate", "w_up")
_RELAYOUT = ("w_uq", "w_ukv")
_LAYOUT_ROWS = 256
_CONV_COLS = 256


def _w_in_layout(cfg, wg):
    _, rs, d = wg.shape
    tc = _pick(d, _LAYOUT_ROWS, LANE)

    def body(w_ref, o_ref):
        o_ref[...] = _to_kernel_layout(cfg, "w_in", jnp.concatenate([w_ref[k] for k in range(N_CHIPS)], axis=0))

    return pl.pallas_call(
        body, name="layout_w_in", grid=(d // tc,),
        in_specs=[pl.BlockSpec((N_CHIPS, rs, tc), lambda j: (0, 0, j))], out_specs=pl.BlockSpec((cfg.EXT, tc), lambda j: (0, j)),
        out_shape=jax.ShapeDtypeStruct((cfg.EXT, d), wg.dtype), compiler_params=_params(("parallel",)),
    )(wg)


def _w_in_grad_to_chips(cfg, g):
    _, d = g.shape
    rs = cfg.IN_COLS // N_CHIPS
    tc = _pick(d, _LAYOUT_ROWS, LANE)

    def body(g_ref, o_ref):
        nat = _from_kernel_layout(cfg, "w_in", g_ref[...])
        for k in range(N_CHIPS):
            o_ref[k] = nat[k * rs:(k + 1) * rs]

    return pl.pallas_call(
        body, name="layout_grad_w_in", grid=(d // tc,),
        in_specs=[pl.BlockSpec((cfg.EXT, tc), lambda j: (0, j))], out_specs=pl.BlockSpec((N_CHIPS, rs, tc), lambda j: (0, 0, j)),
        out_shape=jax.ShapeDtypeStruct((N_CHIPS, rs, d), g.dtype), compiler_params=_params(("parallel",)),
    )(g)


def _gathered_to_kernel(cfg, name, wg):
    if name in _CHIP_MAJOR:
        return wg
    if name == "w_in":
        return _w_in_layout(cfg, wg)
    if name not in _RELAYOUT:
        return wg.reshape(wg.shape[0] * wg.shape[1], wg.shape[2])
    _, rows, cs = wg.shape
    tr = _pick(rows, _LAYOUT_ROWS, 16)

    def body(w_ref, o_ref):
        o_ref[...] = _to_kernel_layout(cfg, name, jnp.concatenate([w_ref[k] for k in range(N_CHIPS)], axis=1))

    wide = jax.eval_shape(lambda w: _to_kernel_layout(cfg, name, w), jax.ShapeDtypeStruct((rows, N_CHIPS * cs), wg.dtype)).shape[1]
    return pl.pallas_call(
        body, name="layout_" + name, grid=(rows // tr,),
        in_specs=[pl.BlockSpec((N_CHIPS, tr, cs), lambda i: (0, i, 0))], out_specs=pl.BlockSpec((tr, wide), lambda i: (i, 0)),
        out_shape=jax.ShapeDtypeStruct((rows, wide), wg.dtype), compiler_params=_params(("parallel",)),
    )(wg)


def _grad_to_chips(cfg, name, g):
    if name in _CHIP_MAJOR:
        return g
    if name == "w_in":
        return _w_in_grad_to_chips(cfg, g)
    if name not in _RELAYOUT:
        return g.reshape(N_CHIPS, g.shape[0] // N_CHIPS, g.shape[1])
    rows, wide = g.shape
    tr = _pick(rows, _LAYOUT_ROWS, 16)
    cs = jax.eval_shape(lambda v: _from_kernel_layout(cfg, name, v), g).shape[1] // N_CHIPS

    def body(g_ref, o_ref):
        nat = _from_kernel_layout(cfg, name, g_ref[...])
        for k in range(N_CHIPS):
            o_ref[k] = nat[:, k * cs:(k + 1) * cs]

    return pl.pallas_call(
        body, name="layout_grad_" + name, grid=(rows // tr,),
        in_specs=[pl.BlockSpec((tr, wide), lambda i: (i, 0))], out_specs=pl.BlockSpec((N_CHIPS, tr, cs), lambda i: (0, i, 0)),
        out_shape=jax.ShapeDtypeStruct((N_CHIPS, rows, cs), g.dtype), compiler_params=_params(("parallel",)),
    )(g)


def _me():
    return lax.axis_index("x"), lax.axis_index("y"), lax.axis_index("c")


def _other_chips(x, y):
    return [(1 - x, y), (x, 1 - y), (1 - x, 1 - y)]


_ANY = pl.BlockSpec(memory_space=pl.ANY)


BLOCK_ELEMS = 1 << 19
BLOCK_ELEMS_FEW = 1 << 20


def _row_block(rows, cols, mult, elems=BLOCK_ELEMS):
    return _pick(rows, max(mult, elems // cols // mult * mult), mult)


def _scalar(v):
    return v.astype(I32).reshape(1)


def _blocks2d(r, c, mult, elems=BLOCK_ELEMS):
    if r % mult == 0:
        tr = _row_block(r, c, mult, elems)
        return (tr, c), r // tr, lambda i: (i, 0)
    tc = _pick(c, max(LANE, elems // r // LANE * LANE), LANE)
    return (r, tc), c // tc, lambda i: (0, i)


def _by_rows(rows):
    return rows % 32 == 0


def _half_shape(rows, cols):
    return (rows // 2, cols) if _by_rows(rows) else (rows, cols // 2)


def _half_blocks(rows, cols, mult, elems=BLOCK_ELEMS):
    hr, hc = _half_shape(rows, cols)
    block, n, part = _blocks2d(hr, hc, mult, elems)
    assert (hr % mult == 0) == _by_rows(rows), (rows, cols, mult)
    full = (lambda h, i: (h * n + i, 0)) if _by_rows(rows) else (lambda h, i: (0, h * n + i))
    return block, n, full, part


def _half(ref, k, half):
    hr, hc = _half_shape(ref.shape[1], ref.shape[2])
    if _by_rows(ref.shape[1]):
        return ref.at[k, pl.ds(pl.multiple_of(half * hr, 16), hr), :]
    return ref.at[k, :, pl.ds(pl.multiple_of(half * hc, LANE), hc)]


def _shard_blocks(w, br, bc):
    if w.shape[0] == 1:
        def write(ref, v):
            ref[...] = v
        return (lambda f: pl.BlockSpec((None, br, bc), lambda *a: (0, *f(*a)))), (lambda ref: ref[...]), write
    assert w.shape[1] == 1 and br == w.shape[0], w.shape

    def write_rows(ref, v):
        ref[:, 0, :] = v
    return (lambda f: pl.BlockSpec((br, 1, bc), lambda *a: (0, 0, f(*a)[1]))), (lambda ref: ref[:, 0, :]), write_rows


def _stage_shard(name, w, chip, after=None):
    rs, cs = w.shape[0] * w.shape[1], w.shape[2]
    (br, bc), n, idx = _blocks2d(rs, cs, 16, BLOCK_ELEMS_FEW)
    spec, get, _ = _shard_blocks(w, br, bc)

    def body(chip_ref, w_ref, *refs):
        refs[-1][...] = get(w_ref).astype(BF16)

    return pl.pallas_call(
        body, name="stage_" + name,
        grid_spec=pltpu.PrefetchScalarGridSpec(
            num_scalar_prefetch=1, grid=(n,),
            in_specs=[spec(lambda i, chip_ref: idx(i))] + ([] if after is None else [_ANY]),
            out_specs=pl.BlockSpec((None, br, bc), lambda i, chip_ref: (chip_ref[0], *idx(i)))),
        out_shape=jax.ShapeDtypeStruct((N_CHIPS, rs, cs), BF16),
        compiler_params=_params(("parallel",)),
    )(_scalar(chip), w, *([] if after is None else [after]))


_HBM = pl.BlockSpec(memory_space=pltpu.HBM)
_SEM = pl.BlockSpec(memory_space=pltpu.SEMAPHORE)
_EFFECT = pltpu.SideEffectType.DATAFLOW_SIDE_EFFECTING


def _split_start(name, bufs, n_copies, copies, after):
    n = len(bufs)

    def body(*refs):
        for cp in copies(refs[:n], refs[n + 1], refs[n + 2]):
            cp.start()
        refs[-1][...] = jnp.zeros_like(refs[-1])

    res = pl.pallas_call(
        body, name=name,
        out_shape=(pltpu.SemaphoreType.DMA((n_copies,)), pltpu.SemaphoreType.DMA((n_copies,)),
                   *[pltpu.HBM(b.shape, b.dtype) for b in bufs], jax.ShapeDtypeStruct((8, LANE), F32)),
        in_specs=[_HBM] * n + [_ANY], out_specs=(_SEM, _SEM, *[_HBM] * n, pl.BlockSpec(memory_space=pltpu.VMEM)),
        input_output_aliases={i: 2 + i for i in range(n)},
        compiler_params=pltpu.CompilerParams(has_side_effects=_EFFECT),
    )(*[pltpu.with_memory_space_constraint(b, pltpu.HBM) for b in bufs], after)
    return res[0], res[1], list(res[2:2 + n]), res[-1]


def _split_wait(name, send_sems, recv_sems, bufs, after, copies):
    n = len(bufs)

    def body(*refs):
        for cp in copies(refs[:n], refs[n], refs[n + 1]):
            cp.wait_send()
            cp.wait_recv()

    return list(pl.pallas_call(
        body, name=name, out_shape=[pltpu.HBM(b.shape, b.dtype) for b in bufs],
        in_specs=[_HBM] * n + [_SEM, _SEM, _ANY], out_specs=[_HBM] * n,
        input_output_aliases={i: i for i in range(n)},
        compiler_params=pltpu.CompilerParams(has_side_effects=_EFFECT),
    )(*bufs, send_sems, recv_sems, after))


def _gather_to_chips(bufs, send_sems, recv_sems):
    x, y, c = _me()
    return [pltpu.make_async_remote_copy(src_ref=_half(b, 2 * x + y, c), dst_ref=_half(b, 2 * x + y, c),
                                         send_sem=send_sems.at[3 * w + j], recv_sem=recv_sems.at[3 * w + j],
                                         device_id=(cx, cy, c), device_id_type=MESH_ID)
            for w, b in enumerate(bufs) for j, (cx, cy) in enumerate(_other_chips(x, y))]


def _gather_to_sibling(bufs, send_sems, recv_sems):
    x, y, c = _me()
    return [pltpu.make_async_remote_copy(src_ref=_half(b, 2 * cx + cy, c), dst_ref=_half(b, 2 * cx + cy, c),
                                         send_sem=send_sems.at[3 * w + j], recv_sem=recv_sems.at[3 * w + j],
                                         device_id=(x, y, 1 - c), device_id_type=MESH_ID)
            for w, b in enumerate(bufs) for j, (cx, cy) in enumerate(_other_chips(x, y))]


def _pair_exchange(name, grads):
    n = len(grads)

    def body(*refs):
        ins, outs, send_sems, recv_sems = refs[:n], refs[n:2 * n], refs[2 * n], refs[2 * n + 1]
        x, y, c = _me()
        cps = []
        for w, (g_ref, o_ref) in enumerate(zip(ins, outs)):
            cps.append(pltpu.make_async_remote_copy(src_ref=_half(g_ref, slice(None), 1 - c), dst_ref=o_ref,
                                                    send_sem=send_sems.at[w], recv_sem=recv_sems.at[w],
                                                    device_id=(x, y, 1 - c), device_id_type=MESH_ID))
            cps[-1].start()
        for cp in cps:
            cp.wait()

    return pl.pallas_call(
        body, name="pair_exchange_" + name, in_specs=[_ANY] * n, out_specs=[_ANY] * n,
        out_shape=[jax.ShapeDtypeStruct((g.shape[0], *_half_shape(g.shape[1], g.shape[2])), g.dtype) for g in grads],
        scratch_shapes=[pltpu.SemaphoreType.DMA((n,)), pltpu.SemaphoreType.DMA((n,))],
    )(*grads)


def _pair_copies(grads, lands, send_sems, recv_sems):
    x, y, c = _me()
    return [pltpu.make_async_remote_copy(src_ref=_half(g_ref, slice(None), 1 - c), dst_ref=l_ref, send_sem=send_sems.at[w],
                                         recv_sem=recv_sems.at[w], device_id=(x, y, 1 - c), device_id_type=MESH_ID)
            for w, (g_ref, l_ref) in enumerate(zip(grads, lands))]


def _pair_exchange_start(name, grads):
    n = len(grads)
    lands = [lax.empty((g.shape[0], *_half_shape(g.shape[1], g.shape[2])), g.dtype) for g in grads]
    send_sems, recv_sems, bufs, token = _split_start(
        "pair_exchange_start_" + name, [*grads, *lands], n, lambda refs, ss, rs: _pair_copies(refs[:n], refs[n:], ss, rs),
        jnp.zeros((8, LANE), F32))
    return (send_sems, recv_sems, bufs), token


def _pair_exchange_wait(name, state, after):
    send_sems, recv_sems, bufs = state
    n = len(bufs) // 2
    bufs = _split_wait("pair_exchange_wait_" + name, send_sems, recv_sems, bufs, after,
                       lambda refs, ss, rs: _pair_copies(refs[:n], refs[n:], ss, rs))
    return bufs[:n], bufs[n:]


def _pair_sum(name, g, theirs, c):
    (br, bc), nb, full, part = _half_blocks(g.shape[1], g.shape[2], 16, 2 * BLOCK_ELEMS_FEW)

    def body(c_ref, a_ref, b_ref, o_ref):
        o_ref[...] = (a_ref[...].astype(F32) + b_ref[...].astype(F32)).astype(o_ref.dtype)

    return pl.pallas_call(
        body, name="pair_sum_" + name,
        grid_spec=pltpu.PrefetchScalarGridSpec(
            num_scalar_prefetch=1, grid=(N_CHIPS, nb),
            in_specs=[pl.BlockSpec((None, br, bc), lambda k, i, c_ref: (k, *full(c_ref[0], i))),
                      pl.BlockSpec((None, br, bc), lambda k, i, c_ref: (k, *part(i)))],
            out_specs=pl.BlockSpec((None, br, bc), lambda k, i, c_ref: (k, *part(i)))),
        out_shape=jax.ShapeDtypeStruct(theirs.shape, BF16),
        compiler_params=_params(("parallel", "parallel")),
    )(_scalar(c), g, theirs)


def _chip_copies(srcs, lands, send_sems, recv_sems):
    x, y, c = _me()
    return [pltpu.make_async_remote_copy(src_ref=s_ref.at[2 * cx + cy], dst_ref=l_ref.at[j], send_sem=send_sems.at[3 * w + j],
                                         recv_sem=recv_sems.at[3 * w + j], device_id=(cx, cy, c), device_id_type=MESH_ID)
            for w, (s_ref, l_ref) in enumerate(zip(srcs, lands)) for j, (cx, cy) in enumerate(_other_chips(x, y))]


def _chip_exchange_start(name, sums):
    n = len(sums)
    lands = [lax.empty((3,) + s.shape[1:], s.dtype) for s in sums]
    send_sems, recv_sems, bufs, token = _split_start(
        "chip_exchange_start_" + name, [*sums, *lands], 3 * n, lambda refs, ss, rs: _chip_copies(refs[:n], refs[n:], ss, rs),
        jnp.zeros((8, LANE), F32))
    return send_sems, recv_sems, bufs[:n], bufs[n:], token


def _chip_exchange_wait(name, send_sems, recv_sems, sums, lands, after):
    n = len(sums)
    bufs = _split_wait("chip_exchange_wait_" + name, send_sems, recv_sems, [*sums, *lands], after,
                       lambda refs, ss, rs: _chip_copies(refs[:n], refs[n:], ss, rs))
    return bufs[:n], bufs[n:]


def _chip_sum(name, sums, theirs, chip):
    _, h, cs = sums.shape
    (br, bc), nb, idx = _blocks2d(h, cs, 16, BLOCK_ELEMS_FEW)

    def body(chip_ref, s_ref, t_ref, o_ref):
        acc = s_ref[...].astype(F32)
        for k in range(3):
            acc = acc + t_ref[k].astype(F32)
        o_ref[...] = acc

    return pl.pallas_call(
        body, name="chip_sum_" + name,
        grid_spec=pltpu.PrefetchScalarGridSpec(
            num_scalar_prefetch=1, grid=(nb,),
            in_specs=[pl.BlockSpec((None, br, bc), lambda i, chip_ref: (chip_ref[0], *idx(i))),
                      pl.BlockSpec((3, br, bc), lambda i, chip_ref: (0, *idx(i)))],
            out_specs=pl.BlockSpec((br, bc), lambda i, chip_ref: idx(i))),
        out_shape=jax.ShapeDtypeStruct((h, cs), F32),
        compiler_params=_params(("parallel",)),
    )(_scalar(chip), sums, theirs)


def _sibling_copies(halves, lands, send_sems, recv_sems):
    x, y, c = _me()
    return [pltpu.make_async_remote_copy(src_ref=h_ref, dst_ref=l_ref, send_sem=send_sems.at[w], recv_sem=recv_sems.at[w],
                                         device_id=(x, y, 1 - c), device_id_type=MESH_ID)
            for w, (h_ref, l_ref) in enumerate(zip(halves, lands))]


def _sibling_exchange_start(name, halves, after):
    n = len(halves)
    lands = [lax.empty(h.shape, h.dtype) for h in halves]
    send_sems, recv_sems, bufs, token = _split_start(
        "sibling_exchange_start_" + name, [*halves, *lands], n, lambda refs, ss, rs: _sibling_copies(refs[:n], refs[n:], ss, rs),
        after)
    return (send_sems, recv_sems, bufs), token


def _sibling_exchange_wait(name, state, after):
    send_sems, recv_sems, bufs = state
    n = len(bufs) // 2
    bufs = _split_wait("sibling_exchange_wait_" + name, send_sems, recv_sems, bufs, after,
                       lambda refs, ss, rs: _sibling_copies(refs[:n], refs[n:], ss, rs))
    return bufs[:n], bufs[n:]


def _sibling_exchange(name, halves):
    n = len(halves)

    def body(*refs):
        ins, outs, send_sems, recv_sems = refs[:n], refs[n:2 * n], refs[2 * n], refs[2 * n + 1]
        x, y, c = _me()
        cps = []
        for w, (h_ref, o_ref) in enumerate(zip(ins, outs)):
            cps.append(pltpu.make_async_remote_copy(src_ref=h_ref, dst_ref=o_ref, send_sem=send_sems.at[w], recv_sem=recv_sems.at[w],
                                                    device_id=(x, y, 1 - c), device_id_type=MESH_ID))
            cps[-1].start()
        for cp in cps:
            cp.wait()

    return pl.pallas_call(
        body, name="sibling_exchange_" + name, in_specs=[_ANY] * n, out_specs=[_ANY] * n,
        out_shape=[jax.ShapeDtypeStruct(h.shape, h.dtype) for h in halves],
        scratch_shapes=[pltpu.SemaphoreType.DMA((n,)), pltpu.SemaphoreType.DMA((n,))],
    )(*halves)


N_DEV = 8


def _peer_copies(bufs, send_sems, recv_sems):
    vec, land = bufs
    x, y, c = _me()
    return [pltpu.make_async_remote_copy(src_ref=vec, dst_ref=land.at[4 * x + 2 * y + c], send_sem=send_sems.at[p - 1],
                                         recv_sem=recv_sems.at[p - 1], device_id=(x ^ (p >> 2), y ^ ((p >> 1) & 1), c ^ (p & 1)),
                                         device_id_type=MESH_ID) for p in range(1, N_DEV)]


def _allreduce_small_start(vec, after):
    land = jnp.zeros((N_DEV,) + vec.shape, F32)
    send_sems, recv_sems, bufs, _ = _split_start("allreduce_small_start", [vec, land], N_DEV - 1, _peer_copies, after)
    return send_sems, recv_sems, bufs


def _allreduce_small_wait(state, chip, core, after):
    send_sems, recv_sems, bufs = state
    vec, land = _split_wait("allreduce_small_wait", send_sems, recv_sems, bufs, after, _peer_copies)

    def body(me_ref, v_ref, l_ref, o_ref):
        acc = None
        for k in range(N_DEV):
            term = jnp.where(me_ref[0] == k, v_ref[...], l_ref[k])
            acc = term if acc is None else acc + term
        o_ref[...] = acc

    return pl.pallas_call(
        body, name="allreduce_small_sum",
        grid_spec=pltpu.PrefetchScalarGridSpec(
            num_scalar_prefetch=1, grid=(1,),
            in_specs=[pl.BlockSpec(vec.shape, lambda i, me_ref: (0, 0)), pl.BlockSpec(land.shape, lambda i, me_ref: (0, 0, 0))],
            out_specs=pl.BlockSpec(vec.shape, lambda i, me_ref: (0, 0))),
        out_shape=jax.ShapeDtypeStruct(vec.shape, F32), compiler_params=_params(("arbitrary",)),
    )(_scalar(2 * chip + core), vec, land)


def _adam_math(w, g, m, v):
    m = ADAM_B1 * m + (1.0 - ADAM_B1) * g
    v = ADAM_B2 * v + (1.0 - ADAM_B2) * (g * g)
    m_hat = m / (1.0 - ADAM_B1 ** ADAM_STEP)
    v_hat = v / (1.0 - ADAM_B2 ** ADAM_STEP)
    return -ADAM_LR * (m_hat / (jnp.sqrt(v_hat) + ADAM_EPS) + ADAM_WD * w), m, v


def _adamw(name, w, g, m, v):
    R, C = w.shape
    tr = _row_block(R, C, 8)

    def body(w_ref, g_ref, m_ref, v_ref, d_ref, nm_ref, nv_ref):
        d_ref[...], nm_ref[...], nv_ref[...] = _adam_math(w_ref[...], g_ref[...], m_ref[...], v_ref[...])

    blk = pl.BlockSpec((tr, C), lambda i: (i, 0))
    return pl.pallas_call(
        body, name=name, grid=(R // tr,), in_specs=[blk] * 4, out_specs=[blk] * 3,
        out_shape=[jax.ShapeDtypeStruct((R, C), F32)] * 3, compiler_params=_params(("parallel",)),
    )(w, g, m, v)


def _adamw_halves(name, w, mine, theirs, m, v, c):
    rs, cs = w.shape[0] * w.shape[1], w.shape[2]
    (br, bc), nb, whole, half = _half_blocks(rs, cs, 8)
    spec, get, put = _shard_blocks(w, br, bc)

    def body(c_ref, w_ref, a_ref, b_ref, m_ref, v_ref, g_ref, d_ref, nm_ref, nv_ref):
        g = jnp.where(pl.program_id(0) == c_ref[0], a_ref[...], b_ref[...])
        put(g_ref, g)
        for ref, val in zip((d_ref, nm_ref, nv_ref), _adam_math(get(w_ref), g, get(m_ref), get(v_ref))):
            put(ref, val)

    full = spec(lambda s, i, c_ref: whole(s, i))
    part = pl.BlockSpec((br, bc), lambda s, i, c_ref: half(i))
    return pl.pallas_call(
        body, name=name,
        grid_spec=pltpu.PrefetchScalarGridSpec(num_scalar_prefetch=1, grid=(2, nb), in_specs=[full, part, part, full, full],
                                               out_specs=[full] * 4),
        out_shape=[jax.ShapeDtypeStruct(w.shape, F32)] * 4, compiler_params=_params(("parallel", "parallel")),
    )(_scalar(c), w, mine, theirs, m, v)


def _pack_small(arrs, lanes=LANE):
    flat = jnp.concatenate([a.reshape(-1) for a in arrs])
    n = -(-flat.shape[0] // (8 * lanes)) * 8 * lanes
    return jnp.pad(flat, (0, n - flat.shape[0])).reshape(8, n // 8)


def _unpack_small(vec, shapes):
    flat, out, off = vec.reshape(-1), [], 0
    for s in shapes:
        out.append(flat[off:off + s[0] * s[1]].reshape(s))
        off += s[0] * s[1]
    return out


class _LateWeights:
    def __init__(self, cfg, tag, names, staged, after):
        self.cfg, self.tag, self.names, self.k = cfg, tag, names, 3 * len(names)
        self.send, self.recv, self.bufs, self.token = _split_start(f"gather_{tag}_chips_start", staged, self.k, _gather_to_chips,
                                                                    after)

    def pass_on(self, after):
        bufs = _split_wait(f"gather_{self.tag}_chips_wait", self.send, self.recv, self.bufs, after, _gather_to_chips)
        self.send, self.recv, self.bufs, token = _split_start(f"gather_{self.tag}_sibling_start", bufs, self.k, _gather_to_sibling,
                                                               self.token)
        return token

    def arrived(self, after):
        bufs = _split_wait(f"gather_{self.tag}_sibling_wait", self.send, self.recv, self.bufs, after, _gather_to_sibling)
        return {n: _gathered_to_kernel(self.cfg, n, b) for n, b in zip(self.names, bufs)}


def _step(cfg, a):
    chip = 2 * lax.axis_index("x") + lax.axis_index("y")
    core = lax.axis_index("c")
    big = BIG

    ffn = ("w_gate", "w_up", "w_down")
    first = ("w_in", "w_uq", "w_ukv")
    sp = {n: a[n] for n in SMALL}
    sharded = _pack_small([a[n] for n in SMALL_SHARDED], 2 * LANE)
    slabs = jnp.where(lax.broadcasted_iota(I32, (N_CHIPS,) + sharded.shape, 0) == chip, sharded[None], 0.0)
    staged = {"w_in": _stage_shard("w_in", a["w_in"], chip)}
    in_weight = _LateWeights(cfg, "in", ("w_in", "sharded_small"), [staged["w_in"], slabs], jnp.zeros((8, LANE), F32))
    behind = in_weight.token
    for n in big[1:]:
        behind = staged[n] = _stage_shard(n, a[n], chip, behind)
    in_weight.pass_on(behind)
    xn_early = _rms_pre(cfg, a["x"], sp["mix_pre_g"] + in_weight.token[0, 0])
    W = in_weight.arrived(xn_early)
    allp = W.pop("sharded_small").reshape((N_CHIPS,) + sharded.shape)
    per_chip = [_unpack_small(allp[ch], [a[n].shape for n in SMALL_SHARDED]) for ch in range(N_CHIPS)]
    for k, n in enumerate(SMALL_SHARDED):
        sp[n] = jnp.concatenate([per_chip[ch][k] for ch in range(N_CHIPS)], axis=1)

    mla_weights = _LateWeights(cfg, "mla", first[1:], [staged[n] for n in first[1:]], W["w_in"])
    out_weight = _LateWeights(cfg, "out", ("w_out",), [staged["w_out"]], mla_weights.token)
    ffn_weights = _LateWeights(cfg, "ffn", ffn[:2], [staged[n] for n in ffn[:2]], out_weight.token)
    down_weight = _LateWeights(cfg, "down", ffn[2:], [staged[n] for n in ffn[2:]], ffn_weights.token)

    state = {}

    def ffn_grads_ready(grads):
        state["ffn_pairs"], token = _pair_exchange_start("ffn", [_grad_to_chips(cfg, n, grads[n]) for n in ffn_grads])
        return token

    def pair_sums(names, grads, theirs):
        return [_pair_sum(n, g, t, core) for n, g, t in zip(names, grads, theirs)]

    def early_grads_ready(grads):
        g_out = [_grad_to_chips(cfg, "w_out", grads["w_out"])]
        g_ffn, t_ffn = _pair_exchange_wait("ffn", state["ffn_pairs"], g_out[0])
        sums = pair_sums(ffn_grads, g_ffn, t_ffn) + pair_sums(["w_out"], g_out, _pair_exchange("out", g_out))
        state["early"] = _chip_exchange_start("early", sums)
        return state["early"][-1]

    def reduced_halves(tag, names, after):
        send_sems, recv_sems, s_bufs, l_bufs, _ = state[tag]
        s_bufs, l_bufs = _chip_exchange_wait(tag, send_sems, recv_sems, s_bufs, l_bufs, after)
        return [_chip_sum(n, s, t, chip) for n, s, t in zip(names, s_bufs, l_bufs)]

    def in_grad_ready(grads):
        grads = [_grad_to_chips(cfg, n, grads[n]) for n in first]
        state["rest"] = _chip_exchange_start("rest", pair_sums(first, grads, _pair_exchange("rest", grads)))
        return state["rest"][-1]

    ffn_grads = ("w_down", "w_gate", "w_up")
    early = ffn_grads + ("w_out",)
    loss, grad_x, gW, gs = _local_grads(cfg, a["x"], a["loss_target"], W, sp, mla_weights, out_weight, ffn_weights, down_weight,
                                        ffn_grads_ready, early_grads_ready, in_grad_ready, xn_early, down_weight.token)
    out = {"grad_x": grad_x}

    def adamw(names, mine, theirs):
        for n, gm, gt in zip(names, mine, theirs):
            out["grad_" + n], out["delta_" + n], out["new_m_" + n], out["new_v_" + n] = _adamw_halves(
                "adamw_" + n, a[n], gm, gt, a["m_" + n], a["v_" + n], core)

    mine = reduced_halves("early", early, grad_x)
    theirs = _sibling_exchange("early", mine[:1])
    later, _ = _sibling_exchange_start("early", mine[1:], theirs[0])
    adamw(early[:1], mine[:1], theirs)
    e_mine, e_theirs = _sibling_exchange_wait("early", later, out["new_v_" + early[0]])
    adamw(early[3:], e_mine[2:], e_theirs[2:])
    mine = reduced_halves("rest", first, out["new_v_" + early[-1]])
    rest, token = _sibling_exchange_start("rest", mine, mine[0])
    small = _allreduce_small_start(_pack_small([gs[n] for n in SMALL] + [loss]), token)
    adamw(early[1:3], e_mine[:2], e_theirs[:2])
    adamw(first, *_sibling_exchange_wait("rest", rest, out["new_v_" + early[2]]))
    shapes = [gs[n].shape for n in SMALL] + [(1, LANE)]
    red = _unpack_small(_allreduce_small_wait(small, chip, core, out["new_v_" + first[-1]]), shapes)
    g_small = dict(zip(SMALL, red[:-1]))
    for n in SMALL_SHARDED:
        cs = a[n].shape[1]
        g_small[n] = lax.dynamic_slice_in_dim(g_small[n], chip * cs, cs, axis=1)
    out["loss"] = red[-1][0, 0]
    sshapes = [a[n].shape for n in SMALL]
    d, nm, nv = _adamw("adamw_small", _pack_small([a[n] for n in SMALL]), _pack_small([g_small[n] for n in SMALL]),
                       _pack_small([a["m_" + n] for n in SMALL]), _pack_small([a["v_" + n] for n in SMALL]))
    for n, dd, mm, vv in zip(SMALL, _unpack_small(d, sshapes), _unpack_small(nm, sshapes), _unpack_small(nv, sshapes)):
        out["grad_" + n], out["delta_" + n], out["new_m_" + n], out["new_v_" + n] = g_small[n], dd, mm, vv
    return out


def kernel(x, mix_pre_g, w_in, q_norm_g, w_uq, kv_norm_g, w_ukv, ssm_conv_w, ssm_conv_b, dt_bias, a_log, d_skip, ssm_norm_g, w_out, mix_post_g, ffn_pre_g, w_gate, w_up, ffn_conv_w, ffn_conv_b, w_down, ffn_post_g, loss_target, m_mix_pre_g, m_w_in, m_q_norm_g, m_w_uq, m_kv_norm_g, m_w_ukv, m_ssm_conv_w, m_ssm_conv_b, m_dt_bias, m_a_log, m_d_skip, m_ssm_norm_g, m_w_out, m_mix_post_g, m_ffn_pre_g, m_w_gate, m_w_up, m_ffn_conv_w, m_ffn_conv_b, m_w_down, m_ffn_post_g, v_mix_pre_g, v_w_in, v_q_norm_g, v_w_uq, v_kv_norm_g, v_w_ukv, v_ssm_conv_w, v_ssm_conv_b, v_dt_bias, v_a_log, v_d_skip, v_ssm_norm_g, v_w_out, v_mix_post_g, v_ffn_pre_g, v_w_gate, v_w_up, v_ffn_conv_w, v_ffn_conv_b, v_w_down, v_ffn_post_g):
    args = dict(locals())
    def given(k, v):
        if k in ("w_in", "m_w_in", "v_w_in"):
            return jnp.transpose(v, (2, 0, 1))
        return v if k.removeprefix("m_").removeprefix("v_") in BIG or v.ndim < 3 else v[0]

    out = _step(_FULL, {k: given(k, v) for k, v in args.items()})
    res = [out["loss"], out["grad_x"][None]]
    for pre in ("grad_", "delta_", "new_m_", "new_v_"):
        for n in WEIGHTS:
            o = out[pre + n]
            res.append(jnp.transpose(o, (1, 2, 0)) if n == "w_in" else o if n in BIG or args[n].ndim < 3 else o[None])
    return tuple(res)
```

```python
import math

import jax
import jax.numpy as jnp
from jax import lax
from jax.experimental import pallas as pl
from jax.experimental.pallas import tpu as pltpu

F32, BF16, I32 = jnp.float32, jnp.bfloat16, jnp.int32
NN = (((1,), (0,)), ((), ()))
NT = (((1,), (1,)), ((), ()))
TN = (((0,), (0,)), ((), ()))
HI = lax.Precision.HIGHEST
MESH_ID = pl.DeviceIdType.MESH

EPS = 1e-6
CHUNK = 64
NOPE, ROPE, VH = 128, 64, 128
ROPE_THETA = 10000.0
HP, NST = 64, 128
SSM_K, FFN_K = 4, 3
LANE = 128
N_CHIPS = 4
VMEM_LIMIT = 52 * 1024 * 1024
MM_TILE, MM_TILE_K = 1408, 2816

ADAM_LR, ADAM_B1, ADAM_B2, ADAM_EPS, ADAM_WD, ADAM_STEP = 0.001, 0.9, 0.999, 1e-08, 0.01, 10


class _Cfg:
    def __init__(self, S, D, QL, KVL, H, HS, G, DFF, T):
        self.S, self.D, self.QL, self.KVL, self.H, self.HS, self.G, self.DFF, self.T = S, D, QL, KVL, H, HS, G, DFF, T
        self.INNER = HS * HP
        self.CONVCH = self.INNER + 2 * G * NST
        self.QW = H * (NOPE + ROPE)
        self.KVW = H * (NOPE + VH)
        self.MLAW = H * VH
        self.MIXW = self.MLAW + self.INNER
        self.IN_COLS = QL + KVL + ROPE + self.INNER + self.CONVCH + HS
        natural, at = {}, 0
        for name, w in (("c_q", QL), ("c_kv", KVL), ("kr", ROPE), ("z", self.INNER), ("xbc", self.CONVCH), ("dt", HS)):
            natural[name] = (at, w)
            at += w
        self.seg, taken = {}, []
        for name in sorted(natural, key=lambda n: -natural[n][1]):
            w = -(-natural[name][1] // LANE) * LANE
            off = next(o for o in range(0, self.IN_COLS * 2, w) if all(o + w <= t or o >= t + tw for t, tw in taken))
            taken.append((off, w))
            self.seg[name] = (off, w) + natural[name]
        self.EXT = max(o + w for o, w in taken)
        self.NPAIR = HS // 2
        self.REP = HS // G

    def window(self, name):
        off, w, _, _ = self.seg[name]
        return w, off // w


_FULL = _Cfg(S=2048, D=2048, QL=768, KVL=512, H=8, HS=16, G=2, DFF=5632, T=256)
BIG = ("w_in", "w_uq", "w_ukv", "w_out", "w_gate", "w_up", "w_down")

SMALL = ("mix_pre_g", "q_norm_g", "kv_norm_g", "ssm_conv_w", "ssm_conv_b", "dt_bias", "a_log", "d_skip", "ssm_norm_g",
         "mix_post_g", "ffn_pre_g", "ffn_conv_w", "ffn_conv_b", "ffn_post_g")
SMALL_SHARDED = ("ssm_conv_w", "ffn_conv_w")
WEIGHTS = ("mix_pre_g", "w_in", "q_norm_g", "w_uq", "kv_norm_g", "w_ukv", "ssm_conv_w", "ssm_conv_b", "dt_bias", "a_log",
           "d_skip", "ssm_norm_g", "w_out", "mix_post_g", "ffn_pre_g", "w_gate", "w_up", "ffn_conv_w", "ffn_conv_b",
           "w_down", "ffn_post_g")


def _pick(n, target, mult):
    best = None
    for d in range(mult, min(n, target) + 1, mult):
        if n % d == 0:
            best = d
    return best if best is not None else n


def _params(sem=None):
    kw = dict(vmem_limit_bytes=VMEM_LIMIT)
    if sem is not None:
        kw["dimension_semantics"] = sem
    return pltpu.CompilerParams(**kw)


def _dot(a, b, dims=NN, precision=None):
    return lax.dot_general(a, b, dims, preferred_element_type=F32, precision=precision)


def _sigmoid(x):
    return 1.0 / (1.0 + jnp.exp(-x))


def _rs(x):
    return lax.rsqrt(jnp.mean(x * x, axis=-1, keepdims=True) + EPS)


def _rms_back(xh, r, dn):
    return r * (dn - xh * jnp.mean(dn * xh, axis=-1, keepdims=True))


def _colsum(v):
    return jnp.sum(v, axis=0, keepdims=True)


def _matmul(name, a, b, mode, out_dtype, a2=None, b2=None, chips=False, after=None):
    cs = None
    if mode == "nn":
        (M, K), N = a.shape, b.shape[-1]
        if chips:
            cs, N = N, N_CHIPS * N
    elif mode == "nt":
        (M, K), N = a.shape, b.shape[-2]
        if chips:
            cs = b.shape[-1]
    else:
        (K, M), N = a.shape, b.shape[1]
        if chips:
            cs = N // N_CHIPS
    tm = _pick(M, MM_TILE, LANE)
    tn = _pick(cs if chips and mode != "nt" else N, MM_TILE, LANE)
    tk = _pick(cs, MM_TILE, LANE) if chips and mode == "nt" else _pick(K, MM_TILE_K, LANE)
    nk = K // tk
    dims = {"nn": NN, "nt": NT, "tn": TN}[mode]
    a_spec = pl.BlockSpec((tk, tm), lambda i, j, k: (k, i)) if mode == "tn" else pl.BlockSpec((tm, tk), lambda i, j, k: (i, k))
    b_spec = pl.BlockSpec((tn, tk), lambda i, j, k: (j, k)) if mode == "nt" else pl.BlockSpec((tk, tn), lambda i, j, k: (k, j))
    o_spec = pl.BlockSpec((tm, tn), lambda i, j, k: (i, j))
    o_shape = (M, N)
    if chips and mode == "nn":
        per = cs // tn
        b_spec = pl.BlockSpec((None, tk, tn), lambda i, j, k: (j // per, k, j % per))
    elif chips and mode == "nt":
        per = cs // tk
        b_spec = pl.BlockSpec((None, tn, tk), lambda i, j, k: (k // per, j, k % per))
    elif chips:
        per = cs // tn
        o_spec = pl.BlockSpec((None, tm, tn), lambda i, j, k: (j // per, i, j % per))
        o_shape = (N_CHIPS, M, cs)
    two = a2 is not None

    def product(refs):
        part = _dot(refs[0][...].astype(BF16), refs[1][...].astype(BF16), dims)
        if two:
            part += _dot(refs[2][...].astype(BF16), refs[3][...].astype(BF16), dims)
        return part

    def body_whole_k(*refs):
        refs[-1][...] = product(refs).astype(refs[-1].dtype)

    def body(*refs):
        o_ref, acc_ref = refs[-2], refs[-1]
        k = pl.program_id(2)

        @pl.when(k == 0)
        def _():
            acc_ref[...] = product(refs)

        @pl.when(k > 0)
        def _():
            acc_ref[...] += product(refs)

        @pl.when(k == nk - 1)
        def _():
            o_ref[...] = acc_ref[...].astype(o_ref.dtype)

    ins = ((a, b, a2, b2) if two else (a, b)) + (() if after is None else (after,))
    return pl.pallas_call(
        body_whole_k if nk == 1 else body, name=name, grid=(M // tm, N // tn, nk),
        in_specs=[a_spec, b_spec] * (2 if two else 1) + ([] if after is None else [pl.BlockSpec(memory_space=pl.ANY)]),
        out_specs=o_spec,
        out_shape=jax.ShapeDtypeStruct(o_shape, out_dtype),
        scratch_shapes=[] if nk == 1 else [pltpu.VMEM((tm, tn), F32)],
        compiler_params=_params(("parallel", "parallel", "arbitrary")),
    )(*ins)


def _window(a):
    return (a[0], *a[1]) if isinstance(a, tuple) else (a, a.shape[1], 0)


def _rowwise(name, fn, rows, mats, outs, reds, ts):
    rows, widths, blocks = zip(*[_window(a) for a in rows])
    S = rows[0].shape[0]
    nr, nm, no = len(rows), len(mats), len(outs)

    def body(*refs):
        res = fn(*[r[...] for r in refs[:nr + nm]])
        res = res if isinstance(res, (tuple, list)) else (res,)
        for r, v in zip(refs[nr + nm:nr + nm + no], res[:no]):
            r[...] = v.astype(r.dtype)
        first = pl.program_id(0) == 0
        for r, v in zip(refs[nr + nm + no:], res[no:]):
            @pl.when(first)
            def _():
                r[...] = jnp.broadcast_to(v, r.shape)

            @pl.when(jnp.logical_not(first))
            def _():
                r[...] += jnp.broadcast_to(v, r.shape)

    in_specs = [pl.BlockSpec((ts, w), lambda i, b=b: (i, b)) for w, b in zip(widths, blocks)]
    in_specs += [pl.BlockSpec(m.shape, lambda i, nd=m.ndim: (0,) * nd) for m in mats]
    out_specs = [pl.BlockSpec((ts, w), lambda i: (i, 0)) for w, _ in outs]
    out_specs += [pl.BlockSpec(s, lambda i: (0, 0)) for s in reds]
    out_shape = [jax.ShapeDtypeStruct((S, w), dt) for w, dt in outs] + [jax.ShapeDtypeStruct(s, F32) for s in reds]
    return pl.pallas_call(
        body, name=name, grid=(S // ts,), in_specs=in_specs, out_specs=out_specs, out_shape=out_shape,
        compiler_params=_params(("arbitrary",) if reds else ("parallel",)),
    )(*rows, *mats)


def _shift_down(v, s):
    if s == 0:
        return v
    rows = lax.broadcasted_iota(I32, v.shape, 0)
    return jnp.where(rows >= s, pltpu.roll(v, s, 0), 0.0)


def _shift_up(v, s):
    if s == 0:
        return v
    n = v.shape[0]
    rows = lax.broadcasted_iota(I32, v.shape, 0)
    return jnp.where(rows < n - s, pltpu.roll(v, n - s, 0), 0.0)


def _conv(x, w, b):
    K = w.shape[0]
    y = jnp.broadcast_to(b, x.shape)
    for k in range(K):
        y = y + w[k:k + 1, :] * _shift_down(x, K - 1 - k)
    return y


def _conv_back(x, w, dc):
    K = w.shape[0]
    dx = jnp.zeros_like(x)
    dw = []
    for k in range(K):
        up = _shift_up(dc, K - 1 - k)
        dx = dx + w[k:k + 1, :] * up
        dw.append(_colsum(up * x))
    return dx, jnp.concatenate(dw, axis=0), _colsum(dc)


def _colwise(name, fn, cols, vecs, outs, pouts, tc):
    cols, widths, blocks = zip(*[_window(a) for a in cols])
    S, C = cols[0].shape[0], widths[0]
    firsts = [b * (C // tc) for b in blocks]
    nc_, nv, no = len(cols), len(vecs), len(outs)

    def body(*refs):
        res = fn(*[r[...] for r in refs[:nc_ + nv]])
        res = res if isinstance(res, (tuple, list)) else (res,)
        for r, v in zip(refs[nc_ + nv:], res):
            r[...] = v.astype(r.dtype)

    in_specs = [pl.BlockSpec((S, tc), lambda j, f=f: (0, f + j)) for f in firsts]
    in_specs += [pl.BlockSpec((v.shape[0], tc), lambda j: (0, j)) for v in vecs]
    out_specs = [pl.BlockSpec((S, tc), lambda j: (0, j)) for _ in outs] + [pl.BlockSpec((k, tc), lambda j: (0, j)) for k in pouts]
    out_shape = [jax.ShapeDtypeStruct((S, C), dt) for dt in outs] + [jax.ShapeDtypeStruct((k, C), F32) for k in pouts]
    return pl.pallas_call(
        body, name=name, grid=(C // tc,), in_specs=in_specs, out_specs=out_specs, out_shape=out_shape,
        compiler_params=_params(("parallel",)),
    )(*cols, *vecs)


_G0, _G1 = math.sqrt(2.0 / math.pi), 0.044715


def _gelu(g):
    th = jnp.tanh(_G0 * (g + _G1 * g * g * g))
    return 0.5 * g * (1.0 + th), th


def _ffn_act(gate_pre, up, w, b):
    act, _ = _gelu(_conv(gate_pre, w, b))
    return act * up


def _ffn_act_back(dact, gate_pre, up, w, b):
    g = _conv(gate_pre, w, b)
    ge, th = _gelu(g)
    dge = 0.5 * (1.0 + th) + 0.5 * g * (1.0 - th * th) * _G0 * (1.0 + 3.0 * _G1 * g * g)
    dup = dact * ge
    dgate_pre, dw, db = _conv_back(gate_pre, w, dact * up * dge)
    return dgate_pre, dup, dw, db


def _ssm_act(xbc, w, b):
    c = _conv(xbc, w, b)
    return c * _sigmoid(c)


def _ssm_act_back(dxc, xbc, w, b):
    c = _conv(xbc, w, b)
    sg = _sigmoid(c)
    return _conv_back(xbc, w, dxc * sg * (1.0 + c * (1.0 - sg)))


def _rope_tables(S):
    inv = 1.0 / (ROPE_THETA ** (jnp.arange(0, ROPE, 2, dtype=F32) / ROPE))
    ang = jnp.arange(S, dtype=F32)[:, None] * inv[None, :]
    cos, sin = jnp.cos(ang), jnp.sin(ang)
    return jnp.tile(cos, (1, 4)), jnp.tile(jnp.concatenate([-sin, sin], axis=1), (1, 2))


def _swap_halves(x):
    lane = lax.broadcasted_iota(I32, x.shape, 1)
    w = x.shape[1]
    return jnp.where((lane % ROPE) < ROPE // 2, pltpu.roll(x, w - ROPE // 2, 1), pltpu.roll(x, ROPE // 2, 1))


def _rot(x, cos2, sin2):
    return x * cos2 + _swap_halves(x) * sin2


def _rot_back(dy, cos2, sin2):
    return dy * cos2 + _swap_halves(dy * sin2)


def _mla_pack(cfg, q, kv, kr, cos2, sin2):
    S, H = cfg.S, cfg.H
    ts = _pick(S, 256, 8)
    kr, _, kr_block = _window(kr)

    def body(q_ref, kv_ref, kr_ref, c_ref, s_ref, Q_ref, K_ref, V_ref):
        c2, s2 = c_ref[...], s_ref[...]
        krr = _rot(kr_ref[...], c2, s2)
        kr_half = (krr.astype(BF16), pltpu.roll(krr, ROPE, 1).astype(BF16))
        for j in range(H // 2):
            qr = _rot(q_ref[:, (H + j) * LANE:(H + j + 1) * LANE], c2, s2).astype(BF16)
            for h in (2 * j, 2 * j + 1):
                Q_ref[h, :, 0:LANE] = q_ref[:, h * LANE:(h + 1) * LANE].astype(BF16)
                Q_ref[h, :, LANE:] = qr
                K_ref[h, :, 0:LANE] = kv_ref[:, h * LANE:(h + 1) * LANE].astype(BF16)
                K_ref[h, :, LANE:] = kr_half[h % 2]
                V_ref[h] = kv_ref[:, (H + h) * LANE:(H + h + 1) * LANE].astype(BF16)

    tab = pl.BlockSpec((ts, LANE), lambda i: (i, 0))
    heads = lambda w: pl.BlockSpec((H, ts, w), lambda i: (0, i, 0))
    return pl.pallas_call(
        body, name="mla_pack", grid=(S // ts,),
        in_specs=[pl.BlockSpec((ts, cfg.QW), lambda i: (i, 0)), pl.BlockSpec((ts, cfg.KVW), lambda i: (i, 0)),
                  pl.BlockSpec((ts, LANE), lambda i: (i, kr_block)), tab, tab],
        out_specs=[heads(2 * LANE), heads(2 * LANE), heads(LANE)],
        out_shape=[jax.ShapeDtypeStruct((H, S, 2 * LANE), BF16), jax.ShapeDtypeStruct((H, S, 2 * LANE), BF16),
                   jax.ShapeDtypeStruct((H, S, LANE), BF16)],
        compiler_params=_params(("parallel",)),
    )(q, kv, kr, cos2, sin2)


def _mla_unpack(cfg, dQ, dK, dV, cos2, sin2):
    S, H = cfg.S, cfg.H
    ts = _pick(S, 256, 8)

    def body(dQ_ref, dK_ref, dV_ref, c_ref, s_ref, dq_ref, dkv_ref, dkr_ref):
        c2, s2 = c_ref[...], s_ref[...]
        lo = lax.broadcasted_iota(I32, (ts, LANE), 1) < ROPE
        tk = jnp.zeros((ts, LANE), F32)
        for h in range(H):
            dq_ref[:, h * LANE:(h + 1) * LANE] = dQ_ref[h, :, 0:LANE].astype(BF16)
            dkv_ref[:, h * LANE:(h + 1) * LANE] = dK_ref[h, :, 0:LANE].astype(BF16)
            dkv_ref[:, (H + h) * LANE:(H + h + 1) * LANE] = dV_ref[h].astype(BF16)
            own = lo if h % 2 == 0 else jnp.logical_not(lo)
            tk = tk + jnp.where(own, dK_ref[h, :, LANE:], 0.0)
        for j in range(H // 2):
            dr = dQ_ref[2 * j, :, LANE:] + dQ_ref[2 * j + 1, :, LANE:]
            dq_ref[:, (H + j) * LANE:(H + j + 1) * LANE] = _rot_back(dr, c2, s2).astype(BF16)
        dkr_rot = jnp.where(lo, tk + pltpu.roll(tk, ROPE, 1), 0.0)
        dkr_ref[...] = _rot_back(dkr_rot, c2, s2).astype(BF16)

    tab = pl.BlockSpec((ts, LANE), lambda i: (i, 0))
    return pl.pallas_call(
        body, name="mla_unpack", grid=(S // ts,),
        in_specs=[pl.BlockSpec((H, ts, 2 * LANE), lambda i: (0, i, 0)), pl.BlockSpec((H, ts, 2 * LANE), lambda i: (0, i, 0)),
                  pl.BlockSpec((H, ts, LANE), lambda i: (0, i, 0)), tab, tab],
        out_specs=[pl.BlockSpec((ts, cfg.QW), lambda i: (i, 0)), pl.BlockSpec((ts, cfg.KVW), lambda i: (i, 0)), tab],
        out_shape=[jax.ShapeDtypeStruct((S, cfg.QW), BF16), jax.ShapeDtypeStruct((S, cfg.KVW), BF16),
                   jax.ShapeDtypeStruct((S, LANE), BF16)],
        compiler_params=_params(("parallel",)),
    )(dQ, dK, dV, cos2, sin2)


_ATT_T = 256
_ATT_HB = 8
_ATT_SCALE = (NOPE + ROPE) ** -0.5


def _diag_mask(transposed=False):
    r = lax.broadcasted_iota(I32, (_ATT_T, _ATT_T), 0) // CHUNK
    c = lax.broadcasted_iota(I32, (_ATT_T, _ATT_T), 1) // CHUNK
    return r <= c if transposed else c <= r


def _row_form(col):
    return jnp.broadcast_to(col, (col.shape[0], LANE)).T[0:8, :]


def _attn_fwd(cfg, Q, K, V):
    S, H, T, HB = cfg.S, cfg.H, _ATT_T, min(cfg.H, _ATT_HB)

    def body(q_ref, k_ref, v_ref, o_ref, lse_t_ref):
        qi = pl.program_id(1)

        def head_step(b, kb, carry, mask):
            m, l, acc = carry
            ks = pl.multiple_of(kb * T, T)
            s = _dot(q_ref[b], k_ref[b, pl.ds(ks, T), :], NT) * _ATT_SCALE
            if mask is not None:
                s = jnp.where(mask, s, -1e30)
            m_new = jnp.maximum(m, jnp.max(s, axis=1, keepdims=True))
            p = jnp.exp(s - m_new)
            alpha = jnp.exp(m - m_new)
            l = alpha * l + jnp.sum(p, axis=1, keepdims=True)
            acc = alpha * acc + _dot(p.astype(BF16), v_ref[b, pl.ds(ks, T), :])
            return m_new, l, acc

        def step(kb, carry, mask=None):
            return tuple(head_step(b, kb, carry[b], mask) for b in range(HB))

        init = (jnp.full((T, 1), -1e30, F32), jnp.zeros((T, 1), F32), jnp.zeros((T, VH), F32))
        done = step(qi, lax.fori_loop(0, qi, step, (init,) * HB), _diag_mask())
        for b, (m, l, acc) in enumerate(done):
            o_ref[:, b * LANE:(b + 1) * LANE] = acc / l
            lse_t_ref[b] = _row_form(m + jnp.log(l))

    return pl.pallas_call(
        body, name="attn_fwd", grid=(H // HB, S // T),
        in_specs=[pl.BlockSpec((HB, T, 2 * LANE), lambda h, i: (h, i, 0)), pl.BlockSpec((HB, S, 2 * LANE), lambda h, i: (h, 0, 0)),
                  pl.BlockSpec((HB, S, LANE), lambda h, i: (h, 0, 0))],
        out_specs=[pl.BlockSpec((T, HB * LANE), lambda h, i: (i, h)), pl.BlockSpec((HB, 8, T), lambda h, i: (h, 0, i))],
        out_shape=[jax.ShapeDtypeStruct((S, H * LANE), F32), jax.ShapeDtypeStruct((H, 8, S), F32)],
        compiler_params=_params(("parallel", "parallel")),
    )(Q, K, V)


def _attn_delta(cfg, do, o, after):
    S, H, T = cfg.S, cfg.H, _ATT_T

    def body(do_ref, o_ref, after_ref, dl_t_ref):
        for h in range(H):
            sl = slice(h * LANE, (h + 1) * LANE)
            dl_t_ref[h] = _row_form(jnp.sum(do_ref[:, sl] * o_ref[:, sl], axis=1, keepdims=True))

    wide = pl.BlockSpec((T, H * LANE), lambda i: (i, 0))
    return pl.pallas_call(
        body, name="attn_delta", grid=(S // T,), in_specs=[wide, wide, _ANY],
        out_specs=pl.BlockSpec((H, 8, T), lambda i: (0, 0, i)), out_shape=jax.ShapeDtypeStruct((H, 8, S), F32),
        compiler_params=_params(("parallel",)),
    )(do, o, after)


_ATT_HB_BWD = 4


def _attn_bwd(cfg, Q, K, V, do, lse_t, delta_t):
    S, H, T, HB = cfg.S, cfg.H, _ATT_T, min(cfg.H, _ATT_HB_BWD)
    nq = S // T

    def body(q_ref, k_ref, v_ref, do_ref, lse_ref, dl_ref, dq_ref, dk_ref, dv_ref):
        kb = pl.program_id(1)

        @pl.when(kb == 0)
        def _():
            dq_ref[...] = jnp.zeros_like(dq_ref)

        def head_step(b, qi, carry, mask):
            dk, dv = carry
            qs = pl.multiple_of(qi * T, T)
            q = q_ref[b, pl.ds(qs, T), :]
            k = k_ref[b]
            dob = do_ref[pl.ds(qs, T), b * LANE:(b + 1) * LANE].astype(BF16)
            s = _dot(k, q, NT) * _ATT_SCALE
            if mask is not None:
                s = jnp.where(mask, s, -1e30)
            p = jnp.exp(s - lse_ref[b, 0:1, pl.ds(qs, T)])
            dv = dv + _dot(p.astype(BF16), dob)
            dp = _dot(v_ref[b], dob, NT)
            ds = (p * (dp - dl_ref[b, 0:1, pl.ds(qs, T)]) * _ATT_SCALE).astype(BF16)
            dk = dk + _dot(ds, q)
            dq_ref[b, pl.ds(qs, T), :] += _dot(ds, k, TN)
            return dk, dv

        def step(qi, carry, mask=None):
            return tuple(head_step(b, qi, carry[b], mask) for b in range(HB))

        zero = (jnp.zeros((T, 2 * LANE), F32), jnp.zeros((T, VH), F32))
        done = lax.fori_loop(kb + 1, nq, step, step(kb, (zero,) * HB, _diag_mask(transposed=True)))
        for b, (dk, dv) in enumerate(done):
            dk_ref[b] = dk
            dv_ref[b] = dv

    row = pl.BlockSpec((HB, 8, S), lambda h, j: (h, 0, 0))
    whole = pl.BlockSpec((HB, S, 2 * LANE), lambda h, j: (h, 0, 0))
    return pl.pallas_call(
        body, name="attn_bwd", grid=(H // HB, S // T),
        in_specs=[whole, pl.BlockSpec((HB, T, 2 * LANE), lambda h, j: (h, j, 0)), pl.BlockSpec((HB, T, LANE), lambda h, j: (h, j, 0)),
                  pl.BlockSpec((S, HB * LANE), lambda h, j: (0, h)), row, row],
        out_specs=[whole, pl.BlockSpec((HB, T, 2 * LANE), lambda h, j: (h, j, 0)), pl.BlockSpec((HB, T, LANE), lambda h, j: (h, j, 0))],
        out_shape=[jax.ShapeDtypeStruct((H, S, 2 * LANE), F32), jax.ShapeDtypeStruct((H, S, 2 * LANE), F32),
                   jax.ShapeDtypeStruct((H, S, LANE), F32)],
        compiler_params=_params(("parallel", "arbitrary")),
    )(Q, K, V, do, lse_t, delta_t)


def _expand_matrix(cfg):
    r = lax.broadcasted_iota(I32, (LANE, cfg.INNER), 0)
    c = lax.broadcasted_iota(I32, (LANE, cfg.INNER), 1)
    return (r == c // HP).astype(F32)


def _softplus(x):
    return jnp.maximum(x, 0.0) + jnp.log(1.0 + jnp.exp(-jnp.abs(x)))


def _ssd_prep(cfg, dt_raw, dt_bias_pad, a_log_pad, expand):
    HS = cfg.HS

    def fn(raw, bias, alog, E):
        heads = lax.broadcasted_iota(I32, raw.shape, 1) < HS
        dt = jnp.where(heads, _softplus(raw + bias), 0.0)
        a = dt * jnp.where(heads[0:1], -jnp.exp(alog), 0.0)
        return dt, a, _dot(dt, E, precision=HI)

    return _rowwise("ssd_prep", fn, [dt_raw], [dt_bias_pad, a_log_pad, expand],
                    [(LANE, F32), (LANE, F32), (cfg.INNER, F32)], [], _pick(cfg.S, 512, 8))


def _tril(T):
    return lax.broadcasted_iota(I32, (T, T), 0) >= lax.broadcasted_iota(I32, (T, T), 1)


def _ssd_fwd(cfg, xc, dt_exp, a_small, dskip_exp, expand):
    S, T, INNER, G, NPAIR = cfg.S, cfg.T, cfg.INNER, cfg.G, cfg.NPAIR
    NC = S // T

    def body(xc_ref, dte_ref, as_ref, dsk_ref, e_ref, y_ref, hin_ref, ht_ref):
        @pl.when(pl.program_id(0) == 0)
        def _():
            ht_ref[...] = jnp.zeros_like(ht_ref)

        tril = _tril(T)
        tri = tril.astype(F32)
        acs_s = _dot(tri, as_ref[...], precision=HI)
        acs_e = _dot(acs_s, e_ref[...], precision=HI)
        acs_t = acs_s.T
        lo = lax.broadcasted_iota(I32, (T, LANE), 1) < HP
        for g in range(G):
            Bb = xc_ref[:, INNER + g * NST:INNER + (g + 1) * NST].astype(BF16)
            Cb = xc_ref[:, INNER + (G + g) * NST:INNER + (G + g + 1) * NST].astype(BF16)
            Gm = _dot(Cb, Bb, NT)
            for j in range(g * NPAIR // G, (g + 1) * NPAIR // G):
                sl = slice(j * LANE, (j + 1) * LANE)
                Xp = xc_ref[:, sl]
                Xdt = Xp * dte_ref[:, sl]
                Xb = Xdt.astype(BF16)
                acs_p = acs_e[:, sl]
                last = acs_p[T - 1:T, :]
                Hin = ht_ref[j]
                hin_ref[0, j] = Hin
                yd = []
                for e in (0, 1):
                    h = 2 * j + e
                    Lm = jnp.exp(jnp.where(tril, acs_s[:, h:h + 1] - acs_t[h:h + 1, :], -1e30))
                    yd.append(_dot((Gm * Lm).astype(BF16), Xb))
                y_off = _dot(Cb, Hin.astype(BF16)) * jnp.exp(acs_p)
                y_ref[:, sl] = jnp.where(lo, yd[0], yd[1]) + y_off + Xp * dsk_ref[:, sl]
                st = _dot(Bb, (Xdt * jnp.exp(last - acs_p)).astype(BF16), TN)
                ht_ref[j] = jnp.exp(last) * Hin + st

    rows = lambda w: pl.BlockSpec((T, w), lambda c: (c, 0))
    return pl.pallas_call(
        body, name="ssd_fwd", grid=(NC,),
        in_specs=[rows(cfg.CONVCH), rows(INNER), rows(LANE), pl.BlockSpec((1, INNER), lambda c: (0, 0)),
                  pl.BlockSpec((LANE, INNER), lambda c: (0, 0))],
        out_specs=[rows(INNER), pl.BlockSpec((1, NPAIR, NST, LANE), lambda c: (c, 0, 0, 0))],
        out_shape=[jax.ShapeDtypeStruct((S, INNER), F32), jax.ShapeDtypeStruct((NC, NPAIR, NST, LANE), F32)],
        scratch_shapes=[pltpu.VMEM((NPAIR, NST, LANE), F32)],
        compiler_params=_params(("arbitrary",)),
    )(xc, dt_exp, a_small, dskip_exp, expand)


def _ssd_bwd(cfg, dy, xc, dt_exp, a_small, dskip_exp, hin, dt_raw, dt_bias_pad, a_log_pad, expand):
    S, T, INNER, G, NPAIR, HS = cfg.S, cfg.T, cfg.INNER, cfg.G, cfg.NPAIR, cfg.HS
    NC = S // T

    def body(dy_ref, xc_ref, dte_ref, as_ref, dsk_ref, hin_ref, raw_ref, bias_ref, alog_ref, e_ref,
             dxc_ref, draw_ref, dbias_ref, dalog_ref, dskip_ref, dht_ref, cols_ref, rows_ref, dacs_ref, ddt_ref):
        first = pl.program_id(0) == 0

        @pl.when(first)
        def _():
            dht_ref[...] = jnp.zeros_like(dht_ref)

        tril = _tril(T)
        tri = tril.astype(F32)
        a_s = as_ref[...]
        acs_s = _dot(tri, a_s, precision=HI)
        acs_e = _dot(acs_s, e_ref[...], precision=HI)
        acs_t = acs_s.T
        lo = lax.broadcasted_iota(I32, (T, LANE), 1) < HP
        last_row = lax.broadcasted_iota(I32, (T, LANE), 0) == T - 1
        cols_ref[...] = jnp.zeros_like(cols_ref)
        rows_ref[...] = jnp.zeros_like(rows_ref)
        dsk_parts = []
        for g in range(G):
            bsl = slice(INNER + g * NST, INNER + (g + 1) * NST)
            csl = slice(INNER + (G + g) * NST, INNER + (G + g + 1) * NST)
            Bb = xc_ref[:, bsl].astype(BF16)
            Cb = xc_ref[:, csl].astype(BF16)
            Gm = _dot(Cb, Bb, NT)
            dG = jnp.zeros((T, T), F32)
            dB = jnp.zeros((T, NST), F32)
            dC = jnp.zeros((T, NST), F32)
            for j in range(g * NPAIR // G, (g + 1) * NPAIR // G):
                sl = slice(j * LANE, (j + 1) * LANE)
                Xp = xc_ref[:, sl]
                dtp = dte_ref[:, sl]
                Xdt = Xp * dtp
                Xb = Xdt.astype(BF16)
                acs_p = acs_e[:, sl]
                last = acs_p[T - 1:T, :]
                e_p, dec, cd = jnp.exp(acs_p), jnp.exp(last - acs_p), jnp.exp(last)
                Hin = hin_ref[0, j]
                Hb = Hin.astype(BF16)
                dHn = dht_ref[j]
                dHb = dHn.astype(BF16)
                dYp = dy_ref[:, sl]
                z = _dot(Cb, Hb)
                dz = (dYp * e_p).astype(BF16)
                dacs_p = dYp * z * e_p
                dC = dC + _dot(dz, Hb, NT)
                dHin = _dot(Cb, dz, TN) + cd * dHn
                dlast = _colsum(dHn * Hin) * cd
                qv = _dot(Bb, dHb)
                dXdt = qv * dec
                ddec = qv * Xdt * dec
                dacs_p = dacs_p - ddec
                dlast = dlast + _colsum(ddec)
                dB = dB + _dot((Xdt * dec).astype(BF16), dHb, NT)
                for e in (0, 1):
                    h = 2 * j + e
                    Lm = jnp.exp(jnp.where(tril, acs_s[:, h:h + 1] - acs_t[h:h + 1, :], -1e30))
                    Mh = Gm * Lm
                    dYe = jnp.where(lo if e == 0 else jnp.logical_not(lo), dYp, 0.0).astype(BF16)
                    dM = _dot(dYe, Xb, NT)
                    dXdt = dXdt + _dot(Mh.astype(BF16), dYe, TN)
                    W = dM * Mh
                    cols_ref[:, h:h + 1] = jnp.sum(W, axis=1, keepdims=True)
                    rows_ref[h:h + 1, :] = _colsum(W)
                    dG = dG + dM * Lm
                dacs_ref[:, sl] = dacs_p + jnp.where(last_row, dlast, 0.0)
                ddt_ref[:, sl] = dXdt * Xp
                dxc_ref[:, sl] = dXdt * dtp + dYp * dsk_ref[:, sl]
                dsk_parts.append(_colsum(dYp * Xp))
                dht_ref[j] = dHin
            dGb = dG.astype(BF16)
            dxc_ref[:, bsl] = dB + _dot(dGb, Cb, TN)
            dxc_ref[:, csl] = dC + _dot(dGb, Bb)
        E = e_ref[...]
        dacs_s = cols_ref[...] - rows_ref[...].T + _dot(dacs_ref[...], E, NT, precision=HI)
        da = _dot(tri, dacs_s, TN, precision=HI)
        heads = lax.broadcasted_iota(I32, (1, LANE), 1) < HS
        A = jnp.where(heads, -jnp.exp(alog_ref[...]), 0.0)
        ddt = _dot(ddt_ref[...], E, NT, precision=HI) + da * A
        draw = jnp.where(heads, ddt * _sigmoid(raw_ref[...] + bias_ref[...]), 0.0)
        draw_ref[...] = draw
        dsk = _dot(jnp.broadcast_to(jnp.concatenate(dsk_parts, axis=1), (8, INNER)), E, NT, precision=HI)[0:1]
        for ref, val in ((dbias_ref, _colsum(draw)), (dalog_ref, _colsum(da * a_s)), (dskip_ref, dsk)):
            @pl.when(first)
            def _():
                ref[...] = val

            @pl.when(jnp.logical_not(first))
            def _():
                ref[...] += val

    dt_raw, _, raw_block = _window(dt_raw)
    rows = lambda w, b=0: pl.BlockSpec((T, w), lambda c: (NC - 1 - c, b))
    vec = lambda w: pl.BlockSpec((1, w), lambda c: (0, 0))
    return pl.pallas_call(
        body, name="ssd_bwd", grid=(NC,),
        in_specs=[rows(INNER), rows(cfg.CONVCH), rows(INNER), rows(LANE), vec(INNER),
                  pl.BlockSpec((1, NPAIR, NST, LANE), lambda c: (NC - 1 - c, 0, 0, 0)), rows(LANE, raw_block), vec(LANE), vec(LANE),
                  pl.BlockSpec((LANE, INNER), lambda c: (0, 0))],
        out_specs=[rows(cfg.CONVCH), rows(LANE), vec(LANE), vec(LANE), vec(LANE)],
        out_shape=[jax.ShapeDtypeStruct((S, cfg.CONVCH), F32), jax.ShapeDtypeStruct((S, LANE), F32)]
        + [jax.ShapeDtypeStruct((1, LANE), F32)] * 3,
        scratch_shapes=[pltpu.VMEM((NPAIR, NST, LANE), F32), pltpu.VMEM((T, LANE), F32), pltpu.VMEM((LANE, T), F32),
                        pltpu.VMEM((T, INNER), F32), pltpu.VMEM((T, INNER), F32)],
        compiler_params=_params(("arbitrary",)),
    )(dy, xc, dt_exp, a_small, dskip_exp, hin, dt_raw, dt_bias_pad, a_log_pad, expand)


def _ssd_post(cfg, y, z, norm_g):
    W = cfg.INNER // cfg.G

    def fn(y, z, g):
        yz = y * z * _sigmoid(z)
        return jnp.concatenate([yz[:, i * W:(i + 1) * W] * _rs(yz[:, i * W:(i + 1) * W]) for i in range(cfg.G)], axis=1) * g

    return _rowwise("ssd_post", fn, [y, z], [norm_g], [(cfg.INNER, BF16)], [], _pick(cfg.S, 256, 8))[0]


def _ssd_post_bwd(cfg, db, y, z, norm_g):
    W = cfg.INNER // cfg.G

    def fn(db, y, z, g):
        sg = _sigmoid(z)
        yz = y * z * sg
        dn = db * g
        dyz, nh = [], []
        for i in range(cfg.G):
            seg = yz[:, i * W:(i + 1) * W]
            r = _rs(seg)
            nh.append(seg * r)
            dyz.append(_rms_back(nh[-1], r, dn[:, i * W:(i + 1) * W]))
        dyz = jnp.concatenate(dyz, axis=1)
        return dyz * z * sg, dyz * y * sg * (1.0 + z * (1.0 - sg)), _colsum(db * jnp.concatenate(nh, axis=1))

    return _rowwise("ssd_post_bwd", fn, [db, y, z], [norm_g], [(cfg.INNER, F32), (cfg.INNER, F32)], [(1, cfg.INNER)],
                    _pick(cfg.S, 256, 8))


def _rms_pre(cfg, x, g):
    return _rowwise("rms_pre", lambda x, g: x * _rs(x) * g, [x], [g], [(cfg.D, BF16)], [], _pick(cfg.S, 256, 8))[0]


def _local_grads(cfg, x, tgt, W, sp, mla_weights=None, out_weight=None, ffn_weights=None, down_weight=None,
                 ffn_grads_ready=None, early_grads_ready=None, in_grad_ready=None, xn=None, after_in=None):
    S, D, H, INNER = cfg.S, cfg.D, cfg.H, cfg.INNER
    ts = _pick(S, 256, 8)
    tc = _CONV_COLS

    if xn is None:
        xn = _rms_pre(cfg, x, sp["mix_pre_g"])
    u = _matmul("mm_in", xn, W["w_in"], "nt", F32, after=after_in)
    c_q, c_kv, kr, z, xbc, dt_raw = [(u, cfg.window(n)) for n in ("c_q", "c_kv", "kr", "z", "xbc", "dt")]

    if mla_weights is not None:
        sp = dict(sp, q_norm_g=sp["q_norm_g"] + mla_weights.pass_on(u)[0, 0])
    cqn = _rowwise("rms_q", lambda x, g: x * _rs(x) * g, [c_q], [sp["q_norm_g"]], [(cfg.QL, BF16)], [], ts)[0]
    ckvn = _rowwise("rms_kv", lambda x, g: x * _rs(x) * g, [c_kv], [sp["kv_norm_g"]], [(cfg.KVL, BF16)], [], ts)[0]
    if mla_weights is not None:
        W = dict(W, **mla_weights.arrived(ckvn))
    q = _matmul("mm_uq", cqn, W["w_uq"], "nn", F32)
    kv = _matmul("mm_ukv", ckvn, W["w_ukv"], "nn", F32)
    cos2, sin2 = _rope_tables(S)
    Qh, Kh, Vh = _mla_pack(cfg, q, kv, kr, cos2, sin2)
    a_out, lse_t = _attn_fwd(cfg, Qh, Kh, Vh)
    if out_weight is not None:
        sp = dict(sp, ssm_conv_b=sp["ssm_conv_b"] + out_weight.pass_on(a_out)[0, 0])

    pad = lambda v: jnp.pad(v, ((0, 0), (0, LANE - v.shape[1])))
    expand = _expand_matrix(cfg)
    dt_bias_pad, a_log_pad = pad(sp["dt_bias"]), pad(sp["a_log"])
    dskip_exp = jnp.repeat(sp["d_skip"], HP, axis=1)
    xc = _colwise("ssm_act", _ssm_act, [xbc], [sp["ssm_conv_w"], sp["ssm_conv_b"]], [F32], [], tc)[0]
    dt_s, a_s, dt_exp = _ssd_prep(cfg, dt_raw, dt_bias_pad, a_log_pad, expand)
    y_ssd, hin = _ssd_fwd(cfg, xc, dt_exp, a_s, dskip_exp, expand)
    b_out = _ssd_post(cfg, y_ssd, z, sp["ssm_norm_g"])

    ab_out = jnp.concatenate([a_out.astype(BF16), b_out], axis=1)
    if out_weight is not None:
        W = dict(W, **out_weight.arrived(ab_out))
    if ffn_weights is not None:
        sp = dict(sp, mix_post_g=sp["mix_post_g"] + ffn_weights.pass_on(ab_out)[0, 0])
    mix = _matmul("mm_out", ab_out, W["w_out"], "nn", F32)

    def mid(x, mix, g_mp, g_fp):
        x1 = x + mix * _rs(mix) * g_mp
        return x1, x1 * _rs(x1) * g_fp

    x1, h2 = _rowwise("fwd_mid", mid, [x, mix], [sp["mix_post_g"], sp["ffn_pre_g"]], [(D, F32), (D, BF16)], [], ts)
    if ffn_weights is not None:
        W = dict(W, **ffn_weights.arrived(h2))
    gate_pre = _matmul("mm_gate", h2, W["w_gate"], "nn", F32, chips=True)
    if down_weight is not None:
        sp = dict(sp, ffn_conv_b=sp["ffn_conv_b"] + down_weight.pass_on(gate_pre)[0, 0])
    up = _matmul("mm_up", h2, W["w_up"], "nn", F32, chips=True)
    act = _colwise("ffn_act", _ffn_act, [gate_pre, up], [sp["ffn_conv_w"], sp["ffn_conv_b"]], [BF16], [], tc)[0]
    if down_weight is not None:
        W = dict(W, **down_weight.arrived(act))
    f = _matmul("mm_down", act, W["w_down"], "nn", F32)

    def final(x1, f, t, g):
        r = _rs(f)
        fh = f * r
        err = x1 + fh * g - t
        loss = 0.5 * jnp.sum(jnp.mean(err * err, axis=-1, keepdims=True), axis=0, keepdims=True)
        dy = err * (1.0 / D)
        return dy, _rms_back(fh, r, dy * g), _colsum(dy * fh), loss

    dy, df, g_ffn_post, loss = _rowwise("final", final, [x1, f, tgt], [sp["ffn_post_g"]], [(D, F32), (D, BF16)],
                                        [(1, D), (1, LANE)], ts)
    gW = {}
    dact = _matmul("mm_down_dx", df, W["w_down"], "nt", F32)
    gW["w_down"] = _matmul("mm_down_dw", act, df, "tn", BF16)
    dgate, dup, g_ffn_conv_w, g_ffn_conv_b = _colwise(
        "ffn_act_bwd", _ffn_act_back, [dact, gate_pre, up], [sp["ffn_conv_w"], sp["ffn_conv_b"]], [BF16, BF16], [FFN_K, 1], tc)
    gW["w_gate"] = _matmul("mm_gate_dw", h2, dgate, "tn", BF16, chips=True)
    gW["w_up"] = _matmul("mm_up_dw", h2, dup, "tn", BF16, chips=True)
    if ffn_grads_ready is not None:
        sp = dict(sp, ffn_pre_g=sp["ffn_pre_g"] + ffn_grads_ready({n: gW[n] for n in ("w_down", "w_gate", "w_up")})[0, 0])
    dh2 = _matmul("mm_gu_dx", dgate, W["w_gate"], "nt", F32, dup, W["w_up"], chips=True)

    def mid_back(dy, dh2, x1, mix, g_mp, g_fp):
        r2 = _rs(x1)
        xh = x1 * r2
        dx1 = dy + _rms_back(xh, r2, dh2 * g_fp)
        r1 = _rs(mix)
        mh = mix * r1
        return dx1, _rms_back(mh, r1, dx1 * g_mp), _colsum(dh2 * xh), _colsum(dx1 * mh)

    dx1, dmix, g_ffn_pre, g_mix_post = _rowwise("bwd_mid", mid_back, [dy, dh2, x1, mix], [sp["mix_post_g"], sp["ffn_pre_g"]],
                                                [(D, F32), (D, BF16)], [(1, D), (1, D)], ts)
    dab_out = _matmul("mm_out_dx", dmix, W["w_out"], "nt", F32)
    db_out = (dab_out, (INNER, cfg.MLAW // INNER))
    gW["w_out"] = _matmul("mm_out_dw", ab_out, dmix, "tn", BF16)
    early_token = jnp.zeros((8, LANE), F32)
    if early_grads_ready is not None:
        early_token = early_grads_ready({n: gW[n] for n in ("w_down", "w_gate", "w_up", "w_out")})
        sp = dict(sp, ssm_norm_g=sp["ssm_norm_g"] + early_token[0, 0])

    dy_ssd, dz, g_ssm_norm = _ssd_post_bwd(cfg, db_out, y_ssd, z, sp["ssm_norm_g"])
    dxc, ddt_raw, g_dt_bias, g_a_log, g_d_skip = _ssd_bwd(cfg, dy_ssd, xc, dt_exp, a_s, dskip_exp, hin, dt_raw,
                                                          dt_bias_pad, a_log_pad, expand)
    dxbc, g_ssm_conv_w, g_ssm_conv_b = _colwise("ssm_act_bwd", _ssm_act_back, [dxc, xbc], [sp["ssm_conv_w"], sp["ssm_conv_b"]],
                                                [BF16], [SSM_K, 1], tc)

    delta_t = _attn_delta(cfg, dab_out, a_out, early_token)
    dQ, dK, dV = _attn_bwd(cfg, Qh, Kh, Vh, dab_out, lse_t, delta_t)
    dq, dkv, dkr = _mla_unpack(cfg, dQ, dK, dV, cos2, sin2)
    dcqn = _matmul("mm_uq_dx", dq, W["w_uq"], "nt", F32)
    dckvn = _matmul("mm_ukv_dx", dkv, W["w_ukv"], "nt", F32)
    gW["w_uq"] = _matmul("mm_uq_dw", cqn, dq, "tn", BF16)
    gW["w_ukv"] = _matmul("mm_ukv_dw", ckvn, dkv, "tn", BF16)

    def rms_back(x, dy, g):
        r = _rs(x)
        xh = x * r
        return _rms_back(xh, r, dy * g), _colsum(dy * xh)

    dc_q, g_q_norm = _rowwise("rms_q_bwd", rms_back, [c_q, dcqn], [sp["q_norm_g"]], [(cfg.QL, BF16)], [(1, cfg.QL)], ts)
    dc_kv, g_kv_norm = _rowwise("rms_kv_bwd", rms_back, [c_kv, dckvn], [sp["kv_norm_g"]], [(cfg.KVL, BF16)], [(1, cfg.KVL)], ts)

    du = dict(c_q=dc_q, c_kv=dc_kv, kr=dkr, z=dz.astype(BF16), xbc=dxbc, dt=ddt_raw.astype(BF16))
    du = jnp.concatenate([du[n] for n in sorted(du, key=lambda n: cfg.seg[n][0])], axis=1)
    assert du.shape[1] == cfg.EXT, "the layout of u has gaps"
    gW["w_in"] = _matmul("mm_in_dw", du, xn, "tn", BF16)
    if in_grad_ready is not None:
        token = in_grad_ready({n: gW[n] for n in ("w_in", "w_uq", "w_ukv")})
        sp = dict(sp, mix_pre_g=sp["mix_pre_g"] + token[0, 0])
    dxn = _matmul("mm_in_dx", du, W["w_in"], "nn", F32)

    def first_back(dx1, dxn, x, g):
        r = _rs(x)
        xh = x * r
        return dx1 + _rms_back(xh, r, dxn * g), _colsum(dxn * xh)

    grad_x, g_mix_pre = _rowwise("bwd_first", first_back, [dx1, dxn, x], [sp["mix_pre_g"]], [(D, F32)], [(1, D)], ts)

    gs = dict(mix_pre_g=g_mix_pre, q_norm_g=g_q_norm, kv_norm_g=g_kv_norm, ssm_conv_w=g_ssm_conv_w, ssm_conv_b=g_ssm_conv_b,
              dt_bias=g_dt_bias[:, :cfg.HS], a_log=g_a_log[:, :cfg.HS], d_skip=g_d_skip[:, :cfg.HS], ssm_norm_g=g_ssm_norm,
              mix_post_g=g_mix_post, ffn_pre_g=g_ffn_pre, ffn_conv_w=g_ffn_conv_w, ffn_conv_b=g_ffn_conv_b,
              ffn_post_g=g_ffn_post)
    return loss, grad_x, gW, gs


def _to_kernel_layout(cfg, name, w):
    if name == "w_in":
        parts, at = [], 0
        for off, width, n_off, n_width in sorted(cfg.seg.values()):
            parts += [jnp.zeros((off - at, w.shape[1]), w.dtype), w[n_off:n_off + n_width],
                      jnp.zeros((width - n_width, w.shape[1]), w.dtype)]
            at = off + width
        parts.append(jnp.zeros((cfg.EXT - at, w.shape[1]), w.dtype))
        return jnp.concatenate([p for p in parts if p.shape[0]], axis=0)
    if name in ("w_uq", "w_ukv"):
        per = NOPE + (ROPE if name == "w_uq" else VH)
        return jnp.concatenate([w[:, h * per:h * per + NOPE] for h in range(cfg.H)]
                               + [w[:, h * per + NOPE:(h + 1) * per] for h in range(cfg.H)], axis=1)
    return w


def _from_kernel_layout(cfg, name, g):
    if name == "w_in":
        return jnp.concatenate([g[off:off + n_width] for off, _, _, n_width in sorted(cfg.seg.values(), key=lambda s: s[2])], axis=0)
    if name in ("w_uq", "w_ukv"):
        second = ROPE if name == "w_uq" else VH
        base = cfg.H * NOPE
        parts = []
        for h in range(cfg.H):
            parts += [g[:, h * NOPE:(h + 1) * NOPE], g[:, base + h * second:base + (h + 1) * second]]
        return jnp.concatenate(parts, axis=1)
    return g


_CHIP_MAJOR = ("w_gate", "w_up")
_RELAYOUT = ("w_uq", "w_ukv")
_LAYOUT_ROWS = 256
_CONV_COLS = 256


def _w_in_layout(cfg, wg):
    _, rs, d = wg.shape
    tc = _pick(d, _LAYOUT_ROWS, LANE)

    def body(w_ref, o_ref):
        o_ref[...] = _to_kernel_layout(cfg, "w_in", jnp.concatenate([w_ref[k] for k in range(N_CHIPS)], axis=0))

    return pl.pallas_call(
        body, name="layout_w_in", grid=(d // tc,),
        in_specs=[pl.BlockSpec((N_CHIPS, rs, tc), lambda j: (0, 0, j))], out_specs=pl.BlockSpec((cfg.EXT, tc), lambda j: (0, j)),
        out_shape=jax.ShapeDtypeStruct((cfg.EXT, d), wg.dtype), compiler_params=_params(("parallel",)),
    )(wg)


def _w_in_grad_to_chips(cfg, g):
    _, d = g.shape
    rs = cfg.IN_COLS // N_CHIPS
    tc = _pick(d, _LAYOUT_ROWS, LANE)

    def body(g_ref, o_ref):
        nat = _from_kernel_layout(cfg, "w_in", g_ref[...])
        for k in range(N_CHIPS):
            o_ref[k] = nat[k * rs:(k + 1) * rs]

    return pl.pallas_call(
        body, name="layout_grad_w_in", grid=(d // tc,),
        in_specs=[pl.BlockSpec((cfg.EXT, tc), lambda j: (0, j))], out_specs=pl.BlockSpec((N_CHIPS, rs, tc), lambda j: (0, 0, j)),
        out_shape=jax.ShapeDtypeStruct((N_CHIPS, rs, d), g.dtype), compiler_params=_params(("parallel",)),
    )(g)


def _gathered_to_kernel(cfg, name, wg):
    if name in _CHIP_MAJOR:
        return wg
    if name == "w_in":
        return _w_in_layout(cfg, wg)
    if name not in _RELAYOUT:
        return wg.reshape(wg.shape[0] * wg.shape[1], wg.shape[2])
    _, rows, cs = wg.shape
    tr = _pick(rows, _LAYOUT_ROWS, 16)

    def body(w_ref, o_ref):
        o_ref[...] = _to_kernel_layout(cfg, name, jnp.concatenate([w_ref[k] for k in range(N_CHIPS)], axis=1))

    wide = jax.eval_shape(lambda w: _to_kernel_layout(cfg, name, w), jax.ShapeDtypeStruct((rows, N_CHIPS * cs), wg.dtype)).shape[1]
    return pl.pallas_call(
        body, name="layout_" + name, grid=(rows // tr,),
        in_specs=[pl.BlockSpec((N_CHIPS, tr, cs), lambda i: (0, i, 0))], out_specs=pl.BlockSpec((tr, wide), lambda i: (i, 0)),
        out_shape=jax.ShapeDtypeStruct((rows, wide), wg.dtype), compiler_params=_params(("parallel",)),
    )(wg)


def _grad_to_chips(cfg, name, g):
    if name in _CHIP_MAJOR:
        return g
    if name == "w_in":
        return _w_in_grad_to_chips(cfg, g)
    if name not in _RELAYOUT:
        return g.reshape(N_CHIPS, g.shape[0] // N_CHIPS, g.shape[1])
    rows, wide = g.shape
    tr = _pick(rows, _LAYOUT_ROWS, 16)
    cs = jax.eval_shape(lambda v: _from_kernel_layout(cfg, name, v), g).shape[1] // N_CHIPS

    def body(g_ref, o_ref):
        nat = _from_kernel_layout(cfg, name, g_ref[...])
        for k in range(N_CHIPS):
            o_ref[k] = nat[:, k * cs:(k + 1) * cs]

    return pl.pallas_call(
        body, name="layout_grad_" + name, grid=(rows // tr,),
        in_specs=[pl.BlockSpec((tr, wide), lambda i: (i, 0))], out_specs=pl.BlockSpec((N_CHIPS, tr, cs), lambda i: (0, i, 0)),
        out_shape=jax.ShapeDtypeStruct((N_CHIPS, rows, cs), g.dtype), compiler_params=_params(("parallel",)),
    )(g)


def _me():
    return lax.axis_index("x"), lax.axis_index("y"), lax.axis_index("c")


def _other_chips(x, y):
    return [(1 - x, y), (x, 1 - y), (1 - x, 1 - y)]


_ANY = pl.BlockSpec(memory_space=pl.ANY)


BLOCK_ELEMS = 1 << 19
BLOCK_ELEMS_FEW = 1 << 20


def _row_block(rows, cols, mult, elems=BLOCK_ELEMS):
    return _pick(rows, max(mult, elems // cols // mult * mult), mult)


def _scalar(v):
    return v.astype(I32).reshape(1)


def _blocks2d(r, c, mult, elems=BLOCK_ELEMS):
    if r % mult == 0:
        tr = _row_block(r, c, mult, elems)
        return (tr, c), r // tr, lambda i: (i, 0)
    tc = _pick(c, max(LANE, elems // r // LANE * LANE), LANE)
    return (r, tc), c // tc, lambda i: (0, i)


def _by_rows(rows):
    return rows % 32 == 0


def _half_shape(rows, cols):
    return (rows // 2, cols) if _by_rows(rows) else (rows, cols // 2)


def _half_blocks(rows, cols, mult, elems=BLOCK_ELEMS):
    hr, hc = _half_shape(rows, cols)
    block, n, part = _blocks2d(hr, hc, mult, elems)
    assert (hr % mult == 0) == _by_rows(rows), (rows, cols, mult)
    full = (lambda h, i: (h * n + i, 0)) if _by_rows(rows) else (lambda h, i: (0, h * n + i))
    return block, n, full, part


def _half(ref, k, half):
    hr, hc = _half_shape(ref.shape[1], ref.shape[2])
    if _by_rows(ref.shape[1]):
        return ref.at[k, pl.ds(pl.multiple_of(half * hr, 16), hr), :]
    return ref.at[k, :, pl.ds(pl.multiple_of(half * hc, LANE), hc)]


def _shard_blocks(w, br, bc):
    if w.shape[0] == 1:
        def write(ref, v):
            ref[...] = v
        return (lambda f: pl.BlockSpec((None, br, bc), lambda *a: (0, *f(*a)))), (lambda ref: ref[...]), write
    assert w.shape[1] == 1 and br == w.shape[0], w.shape

    def write_rows(ref, v):
        ref[:, 0, :] = v
    return (lambda f: pl.BlockSpec((br, 1, bc), lambda *a: (0, 0, f(*a)[1]))), (lambda ref: ref[:, 0, :]), write_rows


def _stage_shard(name, w, chip, after=None):
    rs, cs = w.shape[0] * w.shape[1], w.shape[2]
    (br, bc), n, idx = _blocks2d(rs, cs, 16, BLOCK_ELEMS_FEW)
    spec, get, _ = _shard_blocks(w, br, bc)

    def body(chip_ref, w_ref, *refs):
        refs[-1][...] = get(w_ref).astype(BF16)

    return pl.pallas_call(
        body, name="stage_" + name,
        grid_spec=pltpu.PrefetchScalarGridSpec(
            num_scalar_prefetch=1, grid=(n,),
            in_specs=[spec(lambda i, chip_ref: idx(i))] + ([] if after is None else [_ANY]),
            out_specs=pl.BlockSpec((None, br, bc), lambda i, chip_ref: (chip_ref[0], *idx(i)))),
        out_shape=jax.ShapeDtypeStruct((N_CHIPS, rs, cs), BF16),
        compiler_params=_params(("parallel",)),
    )(_scalar(chip), w, *([] if after is None else [after]))


_HBM = pl.BlockSpec(memory_space=pltpu.HBM)
_SEM = pl.BlockSpec(memory_space=pltpu.SEMAPHORE)
_EFFECT = pltpu.SideEffectType.DATAFLOW_SIDE_EFFECTING


def _split_start(name, bufs, n_copies, copies, after):
    n = len(bufs)

    def body(*refs):
        for cp in copies(refs[:n], refs[n + 1], refs[n + 2]):
            cp.start()
        refs[-1][...] = jnp.zeros_like(refs[-1])

    res = pl.pallas_call(
        body, name=name,
        out_shape=(pltpu.SemaphoreType.DMA((n_copies,)), pltpu.SemaphoreType.DMA((n_copies,)),
                   *[pltpu.HBM(b.shape, b.dtype) for b in bufs], jax.ShapeDtypeStruct((8, LANE), F32)),
        in_specs=[_HBM] * n + [_ANY], out_specs=(_SEM, _SEM, *[_HBM] * n, pl.BlockSpec(memory_space=pltpu.VMEM)),
        input_output_aliases={i: 2 + i for i in range(n)},
        compiler_params=pltpu.CompilerParams(has_side_effects=_EFFECT),
    )(*[pltpu.with_memory_space_constraint(b, pltpu.HBM) for b in bufs], after)
    return res[0], res[1], list(res[2:2 + n]), res[-1]


def _split_wait(name, send_sems, recv_sems, bufs, after, copies):
    n = len(bufs)

    def body(*refs):
        for cp in copies(refs[:n], refs[n], refs[n + 1]):
            cp.wait_send()
            cp.wait_recv()

    return list(pl.pallas_call(
        body, name=name, out_shape=[pltpu.HBM(b.shape, b.dtype) for b in bufs],
        in_specs=[_HBM] * n + [_SEM, _SEM, _ANY], out_specs=[_HBM] * n,
        input_output_aliases={i: i for i in range(n)},
        compiler_params=pltpu.CompilerParams(has_side_effects=_EFFECT),
    )(*bufs, send_sems, recv_sems, after))


def _gather_to_chips(bufs, send_sems, recv_sems):
    x, y, c = _me()
    return [pltpu.make_async_remote_copy(src_ref=_half(b, 2 * x + y, c), dst_ref=_half(b, 2 * x + y, c),
                                         send_sem=send_sems.at[3 * w + j], recv_sem=recv_sems.at[3 * w + j],
                                         device_id=(cx, cy, c), device_id_type=MESH_ID)
            for w, b in enumerate(bufs) for j, (cx, cy) in enumerate(_other_chips(x, y))]


def _gather_to_sibling(bufs, send_sems, recv_sems):
    x, y, c = _me()
    return [pltpu.make_async_remote_copy(src_ref=_half(b, 2 * cx + cy, c), dst_ref=_half(b, 2 * cx + cy, c),
                                         send_sem=send_sems.at[3 * w + j], recv_sem=recv_sems.at[3 * w + j],
                                         device_id=(x, y, 1 - c), device_id_type=MESH_ID)
            for w, b in enumerate(bufs) for j, (cx, cy) in enumerate(_other_chips(x, y))]


def _pair_exchange(name, grads):
    n = len(grads)

    def body(*refs):
        ins, outs, send_sems, recv_sems = refs[:n], refs[n:2 * n], refs[2 * n], refs[2 * n + 1]
        x, y, c = _me()
        cps = []
        for w, (g_ref, o_ref) in enumerate(zip(ins, outs)):
            cps.append(pltpu.make_async_remote_copy(src_ref=_half(g_ref, slice(None), 1 - c), dst_ref=o_ref,
                                                    send_sem=send_sems.at[w], recv_sem=recv_sems.at[w],
                                                    device_id=(x, y, 1 - c), device_id_type=MESH_ID))
            cps[-1].start()
        for cp in cps:
            cp.wait()

    return pl.pallas_call(
        body, name="pair_exchange_" + name, in_specs=[_ANY] * n, out_specs=[_ANY] * n,
        out_shape=[jax.ShapeDtypeStruct((g.shape[0], *_half_shape(g.shape[1], g.shape[2])), g.dtype) for g in grads],
        scratch_shapes=[pltpu.SemaphoreType.DMA((n,)), pltpu.SemaphoreType.DMA((n,))],
    )(*grads)


def _pair_copies(grads, lands, send_sems, recv_sems):
    x, y, c = _me()
    return [pltpu.make_async_remote_copy(src_ref=_half(g_ref, slice(None), 1 - c), dst_ref=l_ref, send_sem=send_sems.at[w],
                                         recv_sem=recv_sems.at[w], device_id=(x, y, 1 - c), device_id_type=MESH_ID)
            for w, (g_ref, l_ref) in enumerate(zip(grads, lands))]


def _pair_exchange_start(name, grads):
    n = len(grads)
    lands = [lax.empty((g.shape[0], *_half_shape(g.shape[1], g.shape[2])), g.dtype) for g in grads]
    send_sems, recv_sems, bufs, token = _split_start(
        "pair_exchange_start_" + name, [*grads, *lands], n, lambda refs, ss, rs: _pair_copies(refs[:n], refs[n:], ss, rs),
        jnp.zeros((8, LANE), F32))
    return (send_sems, recv_sems, bufs), token


def _pair_exchange_wait(name, state, after):
    send_sems, recv_sems, bufs = state
    n = len(bufs) // 2
    bufs = _split_wait("pair_exchange_wait_" + name, send_sems, recv_sems, bufs, after,
                       lambda refs, ss, rs: _pair_copies(refs[:n], refs[n:], ss, rs))
    return bufs[:n], bufs[n:]


def _pair_sum(name, g, theirs, c):
    (br, bc), nb, full, part = _half_blocks(g.shape[1], g.shape[2], 16, 2 * BLOCK_ELEMS_FEW)

    def body(c_ref, a_ref, b_ref, o_ref):
        o_ref[...] = (a_ref[...].astype(F32) + b_ref[...].astype(F32)).astype(o_ref.dtype)

    return pl.pallas_call(
        body, name="pair_sum_" + name,
        grid_spec=pltpu.PrefetchScalarGridSpec(
            num_scalar_prefetch=1, grid=(N_CHIPS, nb),
            in_specs=[pl.BlockSpec((None, br, bc), lambda k, i, c_ref: (k, *full(c_ref[0], i))),
                      pl.BlockSpec((None, br, bc), lambda k, i, c_ref: (k, *part(i)))],
            out_specs=pl.BlockSpec((None, br, bc), lambda k, i, c_ref: (k, *part(i)))),
        out_shape=jax.ShapeDtypeStruct(theirs.shape, BF16),
        compiler_params=_params(("parallel", "parallel")),
    )(_scalar(c), g, theirs)


def _chip_copies(srcs, lands, send_sems, recv_sems):
    x, y, c = _me()
    return [pltpu.make_async_remote_copy(src_ref=s_ref.at[2 * cx + cy], dst_ref=l_ref.at[j], send_sem=send_sems.at[3 * w + j],
                                         recv_sem=recv_sems.at[3 * w + j], device_id=(cx, cy, c), device_id_type=MESH_ID)
            for w, (s_ref, l_ref) in enumerate(zip(srcs, lands)) for j, (cx, cy) in enumerate(_other_chips(x, y))]


def _chip_exchange_start(name, sums):
    n = len(sums)
    lands = [lax.empty((3,) + s.shape[1:], s.dtype) for s in sums]
    send_sems, recv_sems, bufs, token = _split_start(
        "chip_exchange_start_" + name, [*sums, *lands], 3 * n, lambda refs, ss, rs: _chip_copies(refs[:n], refs[n:], ss, rs),
        jnp.zeros((8, LANE), F32))
    return send_sems, recv_sems, bufs[:n], bufs[n:], token


def _chip_exchange_wait(name, send_sems, recv_sems, sums, lands, after):
    n = len(sums)
    bufs = _split_wait("chip_exchange_wait_" + name, send_sems, recv_sems, [*sums, *lands], after,
                       lambda refs, ss, rs: _chip_copies(refs[:n], refs[n:], ss, rs))
    return bufs[:n], bufs[n:]


def _chip_sum(name, sums, theirs, chip):
    _, h, cs = sums.shape
    (br, bc), nb, idx = _blocks2d(h, cs, 16, BLOCK_ELEMS_FEW)

    def body(chip_ref, s_ref, t_ref, o_ref):
        acc = s_ref[...].astype(F32)
        for k in range(3):
            acc = acc + t_ref[k].astype(F32)
        o_ref[...] = acc

    return pl.pallas_call(
        body, name="chip_sum_" + name,
        grid_spec=pltpu.PrefetchScalarGridSpec(
            num_scalar_prefetch=1, grid=(nb,),
            in_specs=[pl.BlockSpec((None, br, bc), lambda i, chip_ref: (chip_ref[0], *idx(i))),
                      pl.BlockSpec((3, br, bc), lambda i, chip_ref: (0, *idx(i)))],
            out_specs=pl.BlockSpec((br, bc), lambda i, chip_ref: idx(i))),
        out_shape=jax.ShapeDtypeStruct((h, cs), F32),
        compiler_params=_params(("parallel",)),
    )(_scalar(chip), sums, theirs)


def _sibling_copies(halves, lands, send_sems, recv_sems):
    x, y, c = _me()
    return [pltpu.make_async_remote_copy(src_ref=h_ref, dst_ref=l_ref, send_sem=send_sems.at[w], recv_sem=recv_sems.at[w],
                                         device_id=(x, y, 1 - c), device_id_type=MESH_ID)
            for w, (h_ref, l_ref) in enumerate(zip(halves, lands))]


def _sibling_exchange_start(name, halves, after):
    n = len(halves)
    lands = [lax.empty(h.shape, h.dtype) for h in halves]
    send_sems, recv_sems, bufs, token = _split_start(
        "sibling_exchange_start_" + name, [*halves, *lands], n, lambda refs, ss, rs: _sibling_copies(refs[:n], refs[n:], ss, rs),
        after)
    return (send_sems, recv_sems, bufs), token


def _sibling_exchange_wait(name, state, after):
    send_sems, recv_sems, bufs = state
    n = len(bufs) // 2
    bufs = _split_wait("sibling_exchange_wait_" + name, send_sems, recv_sems, bufs, after,
                       lambda refs, ss, rs: _sibling_copies(refs[:n], refs[n:], ss, rs))
    return bufs[:n], bufs[n:]


def _sibling_exchange(name, halves):
    n = len(halves)

    def body(*refs):
        ins, outs, send_sems, recv_sems = refs[:n], refs[n:2 * n], refs[2 * n], refs[2 * n + 1]
        x, y, c = _me()
        cps = []
        for w, (h_ref, o_ref) in enumerate(zip(ins, outs)):
            cps.append(pltpu.make_async_remote_copy(src_ref=h_ref, dst_ref=o_ref, send_sem=send_sems.at[w], recv_sem=recv_sems.at[w],
                                                    device_id=(x, y, 1 - c), device_id_type=MESH_ID))
            cps[-1].start()
        for cp in cps:
            cp.wait()

    return pl.pallas_call(
        body, name="sibling_exchange_" + name, in_specs=[_ANY] * n, out_specs=[_ANY] * n,
        out_shape=[jax.ShapeDtypeStruct(h.shape, h.dtype) for h in halves],
        scratch_shapes=[pltpu.SemaphoreType.DMA((n,)), pltpu.SemaphoreType.DMA((n,))],
    )(*halves)


N_DEV = 8


def _peer_copies(bufs, send_sems, recv_sems):
    vec, land = bufs
    x, y, c = _me()
    return [pltpu.make_async_remote_copy(src_ref=vec, dst_ref=land.at[4 * x + 2 * y + c], send_sem=send_sems.at[p - 1],
                                         recv_sem=recv_sems.at[p - 1], device_id=(x ^ (p >> 2), y ^ ((p >> 1) & 1), c ^ (p & 1)),
                                         device_id_type=MESH_ID) for p in range(1, N_DEV)]


def _allreduce_small_start(vec, after):
    land = jnp.zeros((N_DEV,) + vec.shape, F32)
    send_sems, recv_sems, bufs, _ = _split_start("allreduce_small_start", [vec, land], N_DEV - 1, _peer_copies, after)
    return send_sems, recv_sems, bufs


def _allreduce_small_wait(state, chip, core, after):
    send_sems, recv_sems, bufs = state
    vec, land = _split_wait("allreduce_small_wait", send_sems, recv_sems, bufs, after, _peer_copies)

    def body(me_ref, v_ref, l_ref, o_ref):
        acc = None
        for k in range(N_DEV):
            term = jnp.where(me_ref[0] == k, v_ref[...], l_ref[k])
            acc = term if acc is None else acc + term
        o_ref[...] = acc

    return pl.pallas_call(
        body, name="allreduce_small_sum",
        grid_spec=pltpu.PrefetchScalarGridSpec(
            num_scalar_prefetch=1, grid=(1,),
            in_specs=[pl.BlockSpec(vec.shape, lambda i, me_ref: (0, 0)), pl.BlockSpec(land.shape, lambda i, me_ref: (0, 0, 0))],
            out_specs=pl.BlockSpec(vec.shape, lambda i, me_ref: (0, 0))),
        out_shape=jax.ShapeDtypeStruct(vec.shape, F32), compiler_params=_params(("arbitrary",)),
    )(_scalar(2 * chip + core), vec, land)


def _adam_math(w, g, m, v):
    m = ADAM_B1 * m + (1.0 - ADAM_B1) * g
    v = ADAM_B2 * v + (1.0 - ADAM_B2) * (g * g)
    m_hat = m / (1.0 - ADAM_B1 ** ADAM_STEP)
    v_hat = v / (1.0 - ADAM_B2 ** ADAM_STEP)
    return -ADAM_LR * (m_hat / (jnp.sqrt(v_hat) + ADAM_EPS) + ADAM_WD * w), m, v


def _adamw(name, w, g, m, v):
    R, C = w.shape
    tr = _row_block(R, C, 8)

    def body(w_ref, g_ref, m_ref, v_ref, d_ref, nm_ref, nv_ref):
        d_ref[...], nm_ref[...], nv_ref[...] = _adam_math(w_ref[...], g_ref[...], m_ref[...], v_ref[...])

    blk = pl.BlockSpec((tr, C), lambda i: (i, 0))
    return pl.pallas_call(
        body, name=name, grid=(R // tr,), in_specs=[blk] * 4, out_specs=[blk] * 3,
        out_shape=[jax.ShapeDtypeStruct((R, C), F32)] * 3, compiler_params=_params(("parallel",)),
    )(w, g, m, v)


def _adamw_halves(name, w, mine, theirs, m, v, c):
    rs, cs = w.shape[0] * w.shape[1], w.shape[2]
    (br, bc), nb, whole, half = _half_blocks(rs, cs, 8)
    spec, get, put = _shard_blocks(w, br, bc)

    def body(c_ref, w_ref, a_ref, b_ref, m_ref, v_ref, g_ref, d_ref, nm_ref, nv_ref):
        g = jnp.where(pl.program_id(0) == c_ref[0], a_ref[...], b_ref[...])
        put(g_ref, g)
        for ref, val in zip((d_ref, nm_ref, nv_ref), _adam_math(get(w_ref), g, get(m_ref), get(v_ref))):
            put(ref, val)

    full = spec(lambda s, i, c_ref: whole(s, i))
    part = pl.BlockSpec((br, bc), lambda s, i, c_ref: half(i))
    return pl.pallas_call(
        body, name=name,
        grid_spec=pltpu.PrefetchScalarGridSpec(num_scalar_prefetch=1, grid=(2, nb), in_specs=[full, part, part, full, full],
                                               out_specs=[full] * 4),
        out_shape=[jax.ShapeDtypeStruct(w.shape, F32)] * 4, compiler_params=_params(("parallel", "parallel")),
    )(_scalar(c), w, mine, theirs, m, v)


def _pack_small(arrs, lanes=LANE):
    flat = jnp.concatenate([a.reshape(-1) for a in arrs])
    n = -(-flat.shape[0] // (8 * lanes)) * 8 * lanes
    return jnp.pad(flat, (0, n - flat.shape[0])).reshape(8, n // 8)


def _unpack_small(vec, shapes):
    flat, out, off = vec.reshape(-1), [], 0
    for s in shapes:
        out.append(flat[off:off + s[0] * s[1]].reshape(s))
        off += s[0] * s[1]
    return out


class _LateWeights:
    def __init__(self, cfg, tag, names, staged, after):
        self.cfg, self.tag, self.names, self.k = cfg, tag, names, 3 * len(names)
        self.send, self.recv, self.bufs, self.token = _split_start(f"gather_{tag}_chips_start", staged, self.k, _gather_to_chips,
                                                                    after)

    def pass_on(self, after):
        bufs = _split_wait(f"gather_{self.tag}_chips_wait", self.send, self.recv, self.bufs, after, _gather_to_chips)
        self.send, self.recv, self.bufs, token = _split_start(f"gather_{self.tag}_sibling_start", bufs, self.k, _gather_to_sibling,
                                                               self.token)
        return token

    def arrived(self, after):
        bufs = _split_wait(f"gather_{self.tag}_sibling_wait", self.send, self.recv, self.bufs, after, _gather_to_sibling)
        return {n: _gathered_to_kernel(self.cfg, n, b) for n, b in zip(self.names, bufs)}


def _step(cfg, a):
    chip = 2 * lax.axis_index("x") + lax.axis_index("y")
    core = lax.axis_index("c")
    big = BIG

    ffn = ("w_gate", "w_up", "w_down")
    first = ("w_in", "w_uq", "w_ukv")
    sp = {n: a[n] for n in SMALL}
    sharded = _pack_small([a[n] for n in SMALL_SHARDED], 2 * LANE)
    slabs = jnp.where(lax.broadcasted_iota(I32, (N_CHIPS,) + sharded.shape, 0) == chip, sharded[None], 0.0)
    staged = {"w_in": _stage_shard("w_in", a["w_in"], chip)}
    in_weight = _LateWeights(cfg, "in", ("w_in", "sharded_small"), [staged["w_in"], slabs], jnp.zeros((8, LANE), F32))
    behind = in_weight.token
    for n in big[1:]:
        behind = staged[n] = _stage_shard(n, a[n], chip, behind)
    in_weight.pass_on(behind)
    xn_early = _rms_pre(cfg, a["x"], sp["mix_pre_g"] + in_weight.token[0, 0])
    W = in_weight.arrived(xn_early)
    allp = W.pop("sharded_small").reshape((N_CHIPS,) + sharded.shape)
    per_chip = [_unpack_small(allp[ch], [a[n].shape for n in SMALL_SHARDED]) for ch in range(N_CHIPS)]
    for k, n in enumerate(SMALL_SHARDED):
        sp[n] = jnp.concatenate([per_chip[ch][k] for ch in range(N_CHIPS)], axis=1)

    mla_weights = _LateWeights(cfg, "mla", first[1:], [staged[n] for n in first[1:]], W["w_in"])
    out_weight = _LateWeights(cfg, "out", ("w_out",), [staged["w_out"]], mla_weights.token)
    ffn_weights = _LateWeights(cfg, "ffn", ffn[:2], [staged[n] for n in ffn[:2]], out_weight.token)
    down_weight = _LateWeights(cfg, "down", ffn[2:], [staged[n] for n in ffn[2:]], ffn_weights.token)

    state = {}

    def ffn_grads_ready(grads):
        state["ffn_pairs"], token = _pair_exchange_start("ffn", [_grad_to_chips(cfg, n, grads[n]) for n in ffn_grads])
        return token

    def pair_sums(names, grads, theirs):
        return [_pair_sum(n, g, t, core) for n, g, t in zip(names, grads, theirs)]

    def early_grads_ready(grads):
        g_out = [_grad_to_chips(cfg, "w_out", grads["w_out"])]
        g_ffn, t_ffn = _pair_exchange_wait("ffn", state["ffn_pairs"], g_out[0])
        sums = pair_sums(ffn_grads, g_ffn, t_ffn) + pair_sums(["w_out"], g_out, _pair_exchange("out", g_out))
        state["early"] = _chip_exchange_start("early", sums)
        return state["early"][-1]

    def reduced_halves(tag, names, after):
        send_sems, recv_sems, s_bufs, l_bufs, _ = state[tag]
        s_bufs, l_bufs = _chip_exchange_wait(tag, send_sems, recv_sems, s_bufs, l_bufs, after)
        return [_chip_sum(n, s, t, chip) for n, s, t in zip(names, s_bufs, l_bufs)]

    def in_grad_ready(grads):
        grads = [_grad_to_chips(cfg, n, grads[n]) for n in first]
        state["rest"] = _chip_exchange_start("rest", pair_sums(first, grads, _pair_exchange("rest", grads)))
        return state["rest"][-1]

    ffn_grads = ("w_down", "w_gate", "w_up")
    early = ffn_grads + ("w_out",)
    loss, grad_x, gW, gs = _local_grads(cfg, a["x"], a["loss_target"], W, sp, mla_weights, out_weight, ffn_weights, down_weight,
                                        ffn_grads_ready, early_grads_ready, in_grad_ready, xn_early, down_weight.token)
    out = {"grad_x": grad_x}

    def adamw(names, mine, theirs):
        for n, gm, gt in zip(names, mine, theirs):
            out["grad_" + n], out["delta_" + n], out["new_m_" + n], out["new_v_" + n] = _adamw_halves(
                "adamw_" + n, a[n], gm, gt, a["m_" + n], a["v_" + n], core)

    mine = reduced_halves("early", early, grad_x)
    theirs = _sibling_exchange("early", mine[:1])
    later, _ = _sibling_exchange_start("early", mine[1:], theirs[0])
    adamw(early[:1], mine[:1], theirs)
    e_mine, e_theirs = _sibling_exchange_wait("early", later, out["new_v_" + early[0]])
    adamw(early[3:], e_mine[2:], e_theirs[2:])
    mine = reduced_halves("rest", first, out["new_v_" + early[-1]])
    rest, token = _sibling_exchange_start("rest", mine, mine[0])
    small = _allreduce_small_start(_pack_small([gs[n] for n in SMALL] + [loss]), token)
    adamw(early[1:3], e_mine[:2], e_theirs[:2])
    adamw(first, *_sibling_exchange_wait("rest", rest, out["new_v_" + early[2]]))
    shapes = [gs[n].shape for n in SMALL] + [(1, LANE)]
    red = _unpack_small(_allreduce_small_wait(small, chip, core, out["new_v_" + first[-1]]), shapes)
    g_small = dict(zip(SMALL, red[:-1]))
    for n in SMALL_SHARDED:
        cs = a[n].shape[1]
        g_small[n] = lax.dynamic_slice_in_dim(g_small[n], chip * cs, cs, axis=1)
    out["loss"] = red[-1][0, 0]
    sshapes = [a[n].shape for n in SMALL]
    d, nm, nv = _adamw("adamw_small", _pack_small([a[n] for n in SMALL]), _pack_small([g_small[n] for n in SMALL]),
                       _pack_small([a["m_" + n] for n in SMALL]), _pack_small([a["v_" + n] for n in SMALL]))
    for n, dd, mm, vv in zip(SMALL, _unpack_small(d, sshapes), _unpack_small(nm, sshapes), _unpack_small(nv, sshapes)):
        out["grad_" + n], out["delta_" + n], out["new_m_" + n], out["new_v_" + n] = g_small[n], dd, mm, vv
    return out


def kernel(x, mix_pre_g, w_in, q_norm_g, w_uq, kv_norm_g, w_ukv, ssm_conv_w, ssm_conv_b, dt_bias, a_log, d_skip, ssm_norm_g, w_out, mix_post_g, ffn_pre_g, w_gate, w_up, ffn_conv_w, ffn_conv_b, w_down, ffn_post_g, loss_target, m_mix_pre_g, m_w_in, m_q_norm_g, m_w_uq, m_kv_norm_g, m_w_ukv, m_ssm_conv_w, m_ssm_conv_b, m_dt_bias, m_a_log, m_d_skip, m_ssm_norm_g, m_w_out, m_mix_post_g, m_ffn_pre_g, m_w_gate, m_w_up, m_ffn_conv_w, m_ffn_conv_b, m_w_down, m_ffn_post_g, v_mix_pre_g, v_w_in, v_q_norm_g, v_w_uq, v_kv_norm_g, v_w_ukv, v_ssm_conv_w, v_ssm_conv_b, v_dt_bias, v_a_log, v_d_skip, v_ssm_norm_g, v_w_out, v_mix_post_g, v_ffn_pre_g, v_w_gate, v_w_up, v_ffn_conv_w, v_ffn_conv_b, v_w_down, v_ffn_post_g):
    args = dict(locals())
    def given(k, v):
        if k in ("w_in", "m_w_in", "v_w_in"):
            return jnp.transpose(v, (2, 0, 1))
        return v if k.removeprefix("m_").removeprefix("v_") in BIG or v.ndim < 3 else v[0]

    out = _step(_FULL, {k: given(k, v) for k, v in args.items()})
    res = [out["loss"], out["grad_x"][None]]
    for pre in ("grad_", "delta_", "new_m_", "new_v_"):
        for n in WEIGHTS:
            o = out[pre + n]
            res.append(jnp.transpose(o, (1, 2, 0)) if n == "w_in" else o if n in BIG or args[n].ndim < 3 else o[None])
    return tuple(res)
```

```python
import math

import jax
import jax.numpy as jnp
from jax import lax
from jax.experimental import pallas as pl
from jax.experimental.pallas import tpu as pltpu

F32, BF16, I32 = jnp.float32, jnp.bfloat16, jnp.int32
NN = (((1,), (0,)), ((), ()))
NT = (((1,), (1,)), ((), ()))
TN = (((0,), (0,)), ((), ()))
HI = lax.Precision.HIGHEST
MESH_ID = pl.DeviceIdType.MESH

EPS = 1e-6
CHUNK = 64
NOPE, ROPE, VH = 128, 64, 128
ROPE_THETA = 10000.0
HP, NST = 64, 128
SSM_K, FFN_K = 4, 3
LANE = 128
N_CHIPS = 4
VMEM_LIMIT = 52 * 1024 * 1024
MM_TILE, MM_TILE_K = 1408, 2816

ADAM_LR, ADAM_B1, ADAM_B2, ADAM_EPS, ADAM_WD, ADAM_STEP = 0.001, 0.9, 0.999, 1e-08, 0.01, 10


class _Cfg:
    def __init__(self, S, D, QL, KVL, H, HS, G, DFF, T):
        self.S, self.D, self.QL, self.KVL, self.H, self.HS, self.G, self.DFF, self.T = S, D, QL, KVL, H, HS, G, DFF, T
        self.INNER = HS * HP
        self.CONVCH = self.INNER + 2 * G * NST
        self.QW = H * (NOPE + ROPE)
        self.KVW = H * (NOPE + VH)
        self.MLAW = H * VH
        self.MIXW = self.MLAW + self.INNER
        self.IN_COLS = QL + KVL + ROPE + self.INNER + self.CONVCH + HS
        natural, at = {}, 0
        for name, w in (("c_q", QL), ("c_kv", KVL), ("kr", ROPE), ("z", self.INNER), ("xbc", self.CONVCH), ("dt", HS)):
            natural[name] = (at, w)
            at += w
        self.seg, taken = {}, []
        for name in sorted(natural, key=lambda n: -natural[n][1]):
            w = -(-natural[name][1] // LANE) * LANE
            off = next(o for o in range(0, self.IN_COLS * 2, w) if all(o + w <= t or o >= t + tw for t, tw in taken))
            taken.append((off, w))
            self.seg[name] = (off, w) + natural[name]
        self.EXT = max(o + w for o, w in taken)
        self.NPAIR = HS // 2
        self.REP = HS // G

    def window(self, name):
        off, w, _, _ = self.seg[name]
        return w, off // w


_FULL = _Cfg(S=2048, D=2048, QL=768, KVL=512, H=8, HS=16, G=2, DFF=5632, T=256)
BIG = ("w_in", "w_uq", "w_ukv", "w_out", "w_gate", "w_up", "w_down")

SMALL = ("mix_pre_g", "q_norm_g", "kv_norm_g", "ssm_conv_w", "ssm_conv_b", "dt_bias", "a_log", "d_skip", "ssm_norm_g",
         "mix_post_g", "ffn_pre_g", "ffn_conv_w", "ffn_conv_b", "ffn_post_g")
SMALL_SHARDED = ("ssm_conv_w", "ffn_conv_w")
WEIGHTS = ("mix_pre_g", "w_in", "q_norm_g", "w_uq", "kv_norm_g", "w_ukv", "ssm_conv_w", "ssm_conv_b", "dt_bias", "a_log",
           "d_skip", "ssm_norm_g", "w_out", "mix_post_g", "ffn_pre_g", "w_gate", "w_up", "ffn_conv_w", "ffn_conv_b",
           "w_down", "ffn_post_g")


def _pick(n, target, mult):
    best = None
    for d in range(mult, min(n, target) + 1, mult):
        if n % d == 0:
            best = d
    return best if best is not None else n


def _params(sem=None):
    kw = dict(vmem_limit_bytes=VMEM_LIMIT)
    if sem is not None:
        kw["dimension_semantics"] = sem
    return pltpu.CompilerParams(**kw)


def _dot(a, b, dims=NN, precision=None):
    return lax.dot_general(a, b, dims, preferred_element_type=F32, precision=precision)


def _sigmoid(x):
    return 1.0 / (1.0 + jnp.exp(-x))


def _rs(x):
    return lax.rsqrt(jnp.mean(x * x, axis=-1, keepdims=True) + EPS)


def _rms_back(xh, r, dn):
    return r * (dn - xh * jnp.mean(dn * xh, axis=-1, keepdims=True))


def _colsum(v):
    return jnp.sum(v, axis=0, keepdims=True)


def _matmul(name, a, b, mode, out_dtype, a2=None, b2=None, chips=False, after=None):
    cs = None
    if mode == "nn":
        (M, K), N = a.shape, b.shape[-1]
        if chips:
            cs, N = N, N_CHIPS * N
    elif mode == "nt":
        (M, K), N = a.shape, b.shape[-2]
        if chips:
            cs = b.shape[-1]
    else:
        (K, M), N = a.shape, b.shape[1]
        if chips:
            cs = N // N_CHIPS
    tm = _pick(M, MM_TILE, LANE)
    tn = _pick(cs if chips and mode != "nt" else N, MM_TILE, LANE)
    tk = _pick(cs, MM_TILE, LANE) if chips and mode == "nt" else _pick(K, MM_TILE_K, LANE)
    nk = K // tk
    dims = {"nn": NN, "nt": NT, "tn": TN}[mode]
    a_spec = pl.BlockSpec((tk, tm), lambda i, j, k: (k, i)) if mode == "tn" else pl.BlockSpec((tm, tk), lambda i, j, k: (i, k))
    b_spec = pl.BlockSpec((tn, tk), lambda i, j, k: (j, k)) if mode == "nt" else pl.BlockSpec((tk, tn), lambda i, j, k: (k, j))
    o_spec = pl.BlockSpec((tm, tn), lambda i, j, k: (i, j))
    o_shape = (M, N)
    if chips and mode == "nn":
        per = cs // tn
        b_spec = pl.BlockSpec((None, tk, tn), lambda i, j, k: (j // per, k, j % per))
    elif chips and mode == "nt":
        per = cs // tk
        b_spec = pl.BlockSpec((None, tn, tk), lambda i, j, k: (k // per, j, k % per))
    elif chips:
        per = cs // tn
        o_spec = pl.BlockSpec((None, tm, tn), lambda i, j, k: (j // per, i, j % per))
        o_shape = (N_CHIPS, M, cs)
    two = a2 is not None

    def product(refs):
        part = _dot(refs[0][...].astype(BF16), refs[1][...].astype(BF16), dims)
        if two:
            part += _dot(refs[2][...].astype(BF16), refs[3][...].astype(BF16), dims)
        return part

    def body_whole_k(*refs):
        refs[-1][...] = product(refs).astype(refs[-1].dtype)

    def body(*refs):
        o_ref, acc_ref = refs[-2], refs[-1]
        k = pl.program_id(2)

        @pl.when(k == 0)
        def _():
            acc_ref[...] = product(refs)

        @pl.when(k > 0)
        def _():
            acc_ref[...] += product(refs)

        @pl.when(k == nk - 1)
        def _():
            o_ref[...] = acc_ref[...].astype(o_ref.dtype)

    ins = ((a, b, a2, b2) if two else (a, b)) + (() if after is None else (after,))
    return pl.pallas_call(
        body_whole_k if nk == 1 else body, name=name, grid=(M // tm, N // tn, nk),
        in_specs=[a_spec, b_spec] * (2 if two else 1) + ([] if after is None else [pl.BlockSpec(memory_space=pl.ANY)]),
        out_specs=o_spec,
        out_shape=jax.ShapeDtypeStruct(o_shape, out_dtype),
        scratch_shapes=[] if nk == 1 else [pltpu.VMEM((tm, tn), F32)],
        compiler_params=_params(("parallel", "parallel", "arbitrary")),
    )(*ins)


def _matmul_twin(name, a, b1, b2, mode, out_dtype):
    if mode == "nn":
        (M, K), cs = a.shape, b1.shape[-1]
        tm = _pick(M, MM_TILE // 2, LANE)
    else:
        (K, M), cs = a.shape, b1.shape[1] // N_CHIPS
        tm = _pick(M, MM_TILE, LANE)
    tn = _pick(cs, MM_TILE, LANE)
    per = cs // tn
    dims = NN if mode == "nn" else TN

    def body(a_ref, b1_ref, b2_ref, o1_ref, o2_ref):
        lhs = a_ref[...].astype(BF16)
        o1_ref[...] = _dot(lhs, b1_ref[...].astype(BF16), dims).astype(o1_ref.dtype)
        o2_ref[...] = _dot(lhs, b2_ref[...].astype(BF16), dims).astype(o2_ref.dtype)

    if mode == "nn":
        a_spec = pl.BlockSpec((tm, K), lambda i, j: (i, 0))
        b_spec = pl.BlockSpec((None, K, tn), lambda i, j: (j // per, 0, j % per))
        o_spec, o_shape = pl.BlockSpec((tm, tn), lambda i, j: (i, j)), (M, N_CHIPS * cs)
    else:
        a_spec = pl.BlockSpec((K, tm), lambda i, j: (0, i))
        b_spec = pl.BlockSpec((K, tn), lambda i, j: (0, j))
        o_spec, o_shape = pl.BlockSpec((None, tm, tn), lambda i, j: (j // per, i, j % per)), (N_CHIPS, M, cs)
    return pl.pallas_call(
        body, name=name, grid=(M // tm, N_CHIPS * per), in_specs=[a_spec, b_spec, b_spec], out_specs=[o_spec, o_spec],
        out_shape=[jax.ShapeDtypeStruct(o_shape, out_dtype)] * 2, compiler_params=_params(("parallel", "parallel")),
    )(a, b1, b2)


def _window(a):
    return (a[0], *a[1]) if isinstance(a, tuple) else (a, a.shape[1], 0)


def _rowwise(name, fn, rows, mats, outs, reds, ts):
    rows, widths, blocks = zip(*[_window(a) for a in rows])
    S = rows[0].shape[0]
    nr, nm, no = len(rows), len(mats), len(outs)

    def body(*refs):
        res = fn(*[r[...] for r in refs[:nr + nm]])
        res = res if isinstance(res, (tuple, list)) else (res,)
        for r, v in zip(refs[nr + nm:nr + nm + no], res[:no]):
            r[...] = v.astype(r.dtype)
        first = pl.program_id(0) == 0
        for r, v in zip(refs[nr + nm + no:], res[no:]):
            @pl.when(first)
            def _():
                r[...] = jnp.broadcast_to(v, r.shape)

            @pl.when(jnp.logical_not(first))
            def _():
                r[...] += jnp.broadcast_to(v, r.shape)

    in_specs = [pl.BlockSpec((ts, w), lambda i, b=b: (i, b)) for w, b in zip(widths, blocks)]
    in_specs += [pl.BlockSpec(m.shape, lambda i, nd=m.ndim: (0,) * nd) for m in mats]
    out_specs = [pl.BlockSpec((ts, w), lambda i: (i, 0)) for w, _ in outs]
    out_specs += [pl.BlockSpec(s, lambda i: (0, 0)) for s in reds]
    out_shape = [jax.ShapeDtypeStruct((S, w), dt) for w, dt in outs] + [jax.ShapeDtypeStruct(s, F32) for s in reds]
    return pl.pallas_call(
        body, name=name, grid=(S // ts,), in_specs=in_specs, out_specs=out_specs, out_shape=out_shape,
        compiler_params=_params(("arbitrary",) if reds else ("parallel",)),
    )(*rows, *mats)


def _shift_down(v, s):
    if s == 0:
        return v
    rows = lax.broadcasted_iota(I32, v.shape, 0)
    return jnp.where(rows >= s, pltpu.roll(v, s, 0), 0.0)


def _shift_up(v, s):
    if s == 0:
        return v
    n = v.shape[0]
    rows = lax.broadcasted_iota(I32, v.shape, 0)
    return jnp.where(rows < n - s, pltpu.roll(v, n - s, 0), 0.0)


def _conv(x, w, b):
    K = w.shape[0]
    y = jnp.broadcast_to(b, x.shape)
    for k in range(K):
        y = y + w[k:k + 1, :] * _shift_down(x, K - 1 - k)
    return y


def _conv_back(x, w, dc):
    K = w.shape[0]
    dx = jnp.zeros_like(x)
    dw = []
    for k in range(K):
        up = _shift_up(dc, K - 1 - k)
        dx = dx + w[k:k + 1, :] * up
        dw.append(_colsum(up * x))
    return dx, jnp.concatenate(dw, axis=0), _colsum(dc)


def _colwise(name, fn, cols, vecs, outs, pouts, tc):
    cols, widths, blocks = zip(*[_window(a) for a in cols])
    S, C = cols[0].shape[0], widths[0]
    firsts = [b * (C // tc) for b in blocks]
    nc_, nv, no = len(cols), len(vecs), len(outs)

    def body(*refs):
        res = fn(*[r[...] for r in refs[:nc_ + nv]])
        res = res if isinstance(res, (tuple, list)) else (res,)
        for r, v in zip(refs[nc_ + nv:], res):
            r[...] = v.astype(r.dtype)

    in_specs = [pl.BlockSpec((S, tc), lambda j, f=f: (0, f + j)) for f in firsts]
    in_specs += [pl.BlockSpec((v.shape[0], tc), lambda j: (0, j)) for v in vecs]
    out_specs = [pl.BlockSpec((S, tc), lambda j: (0, j)) for _ in outs] + [pl.BlockSpec((k, tc), lambda j: (0, j)) for k in pouts]
    out_shape = [jax.ShapeDtypeStruct((S, C), dt) for dt in outs] + [jax.ShapeDtypeStruct((k, C), F32) for k in pouts]
    return pl.pallas_call(
        body, name=name, grid=(C // tc,), in_specs=in_specs, out_specs=out_specs, out_shape=out_shape,
        compiler_params=_params(("parallel",)),
    )(*cols, *vecs)


_G0, _G1 = math.sqrt(2.0 / math.pi), 0.044715


def _gelu(g):
    th = jnp.tanh(_G0 * (g + _G1 * g * g * g))
    return 0.5 * g * (1.0 + th), th


def _ffn_act(gate_pre, up, w, b):
    act, _ = _gelu(_conv(gate_pre, w, b))
    return act * up


def _ffn_act_back(dact, gate_pre, up, w, b):
    g = _conv(gate_pre, w, b)
    ge, th = _gelu(g)
    dge = 0.5 * (1.0 + th) + 0.5 * g * (1.0 - th * th) * _G0 * (1.0 + 3.0 * _G1 * g * g)
    dup = dact * ge
    dgate_pre, dw, db = _conv_back(gate_pre, w, dact * up * dge)
    return dgate_pre, dup, dw, db


def _ssm_act(xbc, w, b):
    c = _conv(xbc, w, b)
    return c * _sigmoid(c)


def _ssm_act_back(dxc, xbc, w, b):
    c = _conv(xbc, w, b)
    sg = _sigmoid(c)
    return _conv_back(xbc, w, dxc * sg * (1.0 + c * (1.0 - sg)))


def _rope_tables(S):
    inv = 1.0 / (ROPE_THETA ** (jnp.arange(0, ROPE, 2, dtype=F32) / ROPE))
    ang = jnp.arange(S, dtype=F32)[:, None] * inv[None, :]
    cos, sin = jnp.cos(ang), jnp.sin(ang)
    return jnp.tile(cos, (1, 4)), jnp.tile(jnp.concatenate([-sin, sin], axis=1), (1, 2))


def _swap_halves(x):
    lane = lax.broadcasted_iota(I32, x.shape, 1)
    w = x.shape[1]
    return jnp.where((lane % ROPE) < ROPE // 2, pltpu.roll(x, w - ROPE // 2, 1), pltpu.roll(x, ROPE // 2, 1))


def _rot(x, cos2, sin2):
    return x * cos2 + _swap_halves(x) * sin2


def _rot_back(dy, cos2, sin2):
    return dy * cos2 + _swap_halves(dy * sin2)


def _mla_pack(cfg, q, kv, kr, cos2, sin2):
    S, H = cfg.S, cfg.H
    ts = _pick(S, 256, 8)
    kr, _, kr_block = _window(kr)

    def body(q_ref, kv_ref, kr_ref, c_ref, s_ref, Q_ref, K_ref, V_ref):
        c2, s2 = c_ref[...], s_ref[...]
        krr = _rot(kr_ref[...], c2, s2)
        kr_half = (krr.astype(BF16), pltpu.roll(krr, ROPE, 1).astype(BF16))
        for j in range(H // 2):
            qr = _rot(q_ref[:, (H + j) * LANE:(H + j + 1) * LANE], c2, s2).astype(BF16)
            for h in (2 * j, 2 * j + 1):
                Q_ref[h, :, 0:LANE] = q_ref[:, h * LANE:(h + 1) * LANE].astype(BF16)
                Q_ref[h, :, LANE:] = qr
                K_ref[h, :, 0:LANE] = kv_ref[:, h * LANE:(h + 1) * LANE].astype(BF16)
                K_ref[h, :, LANE:] = kr_half[h % 2]
                V_ref[h] = kv_ref[:, (H + h) * LANE:(H + h + 1) * LANE].astype(BF16)

    tab = pl.BlockSpec((ts, LANE), lambda i: (i, 0))
    heads = lambda w: pl.BlockSpec((H, ts, w), lambda i: (0, i, 0))
    return pl.pallas_call(
        body, name="mla_pack", grid=(S // ts,),
        in_specs=[pl.BlockSpec((ts, cfg.QW), lambda i: (i, 0)), pl.BlockSpec((ts, cfg.KVW), lambda i: (i, 0)),
                  pl.BlockSpec((ts, LANE), lambda i: (i, kr_block)), tab, tab],
        out_specs=[heads(2 * LANE), heads(2 * LANE), heads(LANE)],
        out_shape=[jax.ShapeDtypeStruct((H, S, 2 * LANE), BF16), jax.ShapeDtypeStruct((H, S, 2 * LANE), BF16),
                   jax.ShapeDtypeStruct((H, S, LANE), BF16)],
        compiler_params=_params(("parallel",)),
    )(q, kv, kr, cos2, sin2)


def _mla_unpack(cfg, dQ, dK, dV, cos2, sin2):
    S, H = cfg.S, cfg.H
    ts = _pick(S, 256, 8)

    def body(dQ_ref, dK_ref, dV_ref, c_ref, s_ref, dq_ref, dkv_ref, dkr_ref):
        c2, s2 = c_ref[...], s_ref[...]
        lo = lax.broadcasted_iota(I32, (ts, LANE), 1) < ROPE
        tk = jnp.zeros((ts, LANE), F32)
        for h in range(H):
            dq_ref[:, h * LANE:(h + 1) * LANE] = dQ_ref[h, :, 0:LANE].astype(BF16)
            dkv_ref[:, h * LANE:(h + 1) * LANE] = dK_ref[h, :, 0:LANE].astype(BF16)
            dkv_ref[:, (H + h) * LANE:(H + h + 1) * LANE] = dV_ref[h].astype(BF16)
            own = lo if h % 2 == 0 else jnp.logical_not(lo)
            tk = tk + jnp.where(own, dK_ref[h, :, LANE:], 0.0)
        for j in range(H // 2):
            dr = dQ_ref[2 * j, :, LANE:] + dQ_ref[2 * j + 1, :, LANE:]
            dq_ref[:, (H + j) * LANE:(H + j + 1) * LANE] = _rot_back(dr, c2, s2).astype(BF16)
        dkr_rot = jnp.where(lo, tk + pltpu.roll(tk, ROPE, 1), 0.0)
        dkr_ref[...] = _rot_back(dkr_rot, c2, s2).astype(BF16)

    tab = pl.BlockSpec((ts, LANE), lambda i: (i, 0))
    return pl.pallas_call(
        body, name="mla_unpack", grid=(S // ts,),
        in_specs=[pl.BlockSpec((H, ts, 2 * LANE), lambda i: (0, i, 0)), pl.BlockSpec((H, ts, 2 * LANE), lambda i: (0, i, 0)),
                  pl.BlockSpec((H, ts, LANE), lambda i: (0, i, 0)), tab, tab],
        out_specs=[pl.BlockSpec((ts, cfg.QW), lambda i: (i, 0)), pl.BlockSpec((ts, cfg.KVW), lambda i: (i, 0)), tab],
        out_shape=[jax.ShapeDtypeStruct((S, cfg.QW), BF16), jax.ShapeDtypeStruct((S, cfg.KVW), BF16),
                   jax.ShapeDtypeStruct((S, LANE), BF16)],
        compiler_params=_params(("parallel",)),
    )(dQ, dK, dV, cos2, sin2)


_ATT_T = 256
_ATT_HB = 8
_ATT_SCALE = (NOPE + ROPE) ** -0.5


def _diag_mask(transposed=False):
    r = lax.broadcasted_iota(I32, (_ATT_T, _ATT_T), 0) // CHUNK
    c = lax.broadcasted_iota(I32, (_ATT_T, _ATT_T), 1) // CHUNK
    return r <= c if transposed else c <= r


def _row_form(col):
    return jnp.broadcast_to(col, (col.shape[0], LANE)).T[0:8, :]


def _attn_fwd(cfg, Q, K, V):
    S, H, T, HB = cfg.S, cfg.H, _ATT_T, min(cfg.H, _ATT_HB)

    def body(q_ref, k_ref, v_ref, o_ref, lse_t_ref):
        qi = pl.program_id(1)

        def head_step(b, kb, carry, mask):
            m, l, acc = carry
            ks = pl.multiple_of(kb * T, T)
            s = _dot(q_ref[b], k_ref[b, pl.ds(ks, T), :], NT) * _ATT_SCALE
            if mask is not None:
                s = jnp.where(mask, s, -1e30)
            m_new = jnp.maximum(m, jnp.max(s, axis=1, keepdims=True))
            p = jnp.exp(s - m_new)
            alpha = jnp.exp(m - m_new)
            l = alpha * l + jnp.sum(p, axis=1, keepdims=True)
            acc = alpha * acc + _dot(p.astype(BF16), v_ref[b, pl.ds(ks, T), :])
            return m_new, l, acc

        def step(kb, carry, mask=None):
            return tuple(head_step(b, kb, carry[b], mask) for b in range(HB))

        init = (jnp.full((T, 1), -1e30, F32), jnp.zeros((T, 1), F32), jnp.zeros((T, VH), F32))
        done = step(qi, lax.fori_loop(0, qi, step, (init,) * HB), _diag_mask())
        for b, (m, l, acc) in enumerate(done):
            o_ref[:, b * LANE:(b + 1) * LANE] = acc / l
            lse_t_ref[b] = _row_form(m + jnp.log(l))

    return pl.pallas_call(
        body, name="attn_fwd", grid=(H // HB, S // T),
        in_specs=[pl.BlockSpec((HB, T, 2 * LANE), lambda h, i: (h, i, 0)), pl.BlockSpec((HB, S, 2 * LANE), lambda h, i: (h, 0, 0)),
                  pl.BlockSpec((HB, S, LANE), lambda h, i: (h, 0, 0))],
        out_specs=[pl.BlockSpec((T, HB * LANE), lambda h, i: (i, h)), pl.BlockSpec((HB, 8, T), lambda h, i: (h, 0, i))],
        out_shape=[jax.ShapeDtypeStruct((S, H * LANE), F32), jax.ShapeDtypeStruct((H, 8, S), F32)],
        compiler_params=_params(("parallel", "parallel")),
    )(Q, K, V)


def _attn_delta(cfg, do, o, after):
    S, H, T = cfg.S, cfg.H, _ATT_T

    def body(do_ref, o_ref, after_ref, dl_t_ref):
        for h in range(H):
            sl = slice(h * LANE, (h + 1) * LANE)
            dl_t_ref[h] = _row_form(jnp.sum(do_ref[:, sl] * o_ref[:, sl], axis=1, keepdims=True))

    wide = pl.BlockSpec((T, H * LANE), lambda i: (i, 0))
    return pl.pallas_call(
        body, name="attn_delta", grid=(S // T,), in_specs=[wide, wide, _ANY],
        out_specs=pl.BlockSpec((H, 8, T), lambda i: (0, 0, i)), out_shape=jax.ShapeDtypeStruct((H, 8, S), F32),
        compiler_params=_params(("parallel",)),
    )(do, o, after)


_ATT_HB_BWD = 4


def _attn_bwd(cfg, Q, K, V, do, lse_t, delta_t):
    S, H, T, HB = cfg.S, cfg.H, _ATT_T, min(cfg.H, _ATT_HB_BWD)
    nq = S // T

    def body(q_ref, k_ref, v_ref, do_ref, lse_ref, dl_ref, dq_ref, dk_ref, dv_ref):
        kb = pl.program_id(1)

        @pl.when(kb == 0)
        def _():
            dq_ref[...] = jnp.zeros_like(dq_ref)

        def head_step(b, qi, carry, mask):
            dk, dv = carry
            qs = pl.multiple_of(qi * T, T)
            q = q_ref[b, pl.ds(qs, T), :]
            k = k_ref[b]
            dob = do_ref[pl.ds(qs, T), b * LANE:(b + 1) * LANE].astype(BF16)
            s = _dot(k, q, NT) * _ATT_SCALE
            if mask is not None:
                s = jnp.where(mask, s, -1e30)
            p = jnp.exp(s - lse_ref[b, 0:1, pl.ds(qs, T)])
            dv = dv + _dot(p.astype(BF16), dob)
            dp = _dot(v_ref[b], dob, NT)
            ds = (p * (dp - dl_ref[b, 0:1, pl.ds(qs, T)]) * _ATT_SCALE).astype(BF16)
            dk = dk + _dot(ds, q)
            dq_ref[b, pl.ds(qs, T), :] += _dot(ds, k, TN)
            return dk, dv

        def step(qi, carry, mask=None):
            return tuple(head_step(b, qi, carry[b], mask) for b in range(HB))

        zero = (jnp.zeros((T, 2 * LANE), F32), jnp.zeros((T, VH), F32))
        done = lax.fori_loop(kb + 1, nq, step, step(kb, (zero,) * HB, _diag_mask(transposed=True)))
        for b, (dk, dv) in enumerate(done):
            dk_ref[b] = dk
            dv_ref[b] = dv

    row = pl.BlockSpec((HB, 8, S), lambda h, j: (h, 0, 0))
    whole = pl.BlockSpec((HB, S, 2 * LANE), lambda h, j: (h, 0, 0))
    return pl.pallas_call(
        body, name="attn_bwd", grid=(H // HB, S // T),
        in_specs=[whole, pl.BlockSpec((HB, T, 2 * LANE), lambda h, j: (h, j, 0)), pl.BlockSpec((HB, T, LANE), lambda h, j: (h, j, 0)),
                  pl.BlockSpec((S, HB * LANE), lambda h, j: (0, h)), row, row],
        out_specs=[whole, pl.BlockSpec((HB, T, 2 * LANE), lambda h, j: (h, j, 0)), pl.BlockSpec((HB, T, LANE), lambda h, j: (h, j, 0))],
        out_shape=[jax.ShapeDtypeStruct((H, S, 2 * LANE), F32), jax.ShapeDtypeStruct((H, S, 2 * LANE), F32),
                   jax.ShapeDtypeStruct((H, S, LANE), F32)],
        compiler_params=_params(("parallel", "arbitrary")),
    )(Q, K, V, do, lse_t, delta_t)


def _expand_matrix(cfg):
    r = lax.broadcasted_iota(I32, (LANE, cfg.INNER), 0)
    c = lax.broadcasted_iota(I32, (LANE, cfg.INNER), 1)
    return (r == c // HP).astype(F32)


def _softplus(x):
    return jnp.maximum(x, 0.0) + jnp.log(1.0 + jnp.exp(-jnp.abs(x)))


def _ssd_prep(cfg, dt_raw, dt_bias_pad, a_log_pad, expand):
    HS = cfg.HS

    def fn(raw, bias, alog, E):
        heads = lax.broadcasted_iota(I32, raw.shape, 1) < HS
        dt = jnp.where(heads, _softplus(raw + bias), 0.0)
        a = dt * jnp.where(heads[0:1], -jnp.exp(alog), 0.0)
        return dt, a, _dot(dt, E, precision=HI)

    return _rowwise("ssd_prep", fn, [dt_raw], [dt_bias_pad, a_log_pad, expand],
                    [(LANE, F32), (LANE, F32), (cfg.INNER, F32)], [], _pick(cfg.S, 512, 8))


def _tril(T):
    return lax.broadcasted_iota(I32, (T, T), 0) >= lax.broadcasted_iota(I32, (T, T), 1)


def _ssd_fwd(cfg, xc, dt_exp, a_small, dskip_exp, expand):
    S, T, INNER, G, NPAIR = cfg.S, cfg.T, cfg.INNER, cfg.G, cfg.NPAIR
    NC = S // T

    def body(xc_ref, dte_ref, as_ref, dsk_ref, e_ref, y_ref, hin_ref, ht_ref):
        @pl.when(pl.program_id(0) == 0)
        def _():
            ht_ref[...] = jnp.zeros_like(ht_ref)

        tril = _tril(T)
        tri = tril.astype(F32)
        acs_s = _dot(tri, as_ref[...], precision=HI)
        acs_e = _dot(acs_s, e_ref[...], precision=HI)
        acs_t = acs_s.T
        lo = lax.broadcasted_iota(I32, (T, LANE), 1) < HP
        for g in range(G):
            Bb = xc_ref[:, INNER + g * NST:INNER + (g + 1) * NST].astype(BF16)
            Cb = xc_ref[:, INNER + (G + g) * NST:INNER + (G + g + 1) * NST].astype(BF16)
            Gm = _dot(Cb, Bb, NT)
            for j in range(g * NPAIR // G, (g + 1) * NPAIR // G):
                sl = slice(j * LANE, (j + 1) * LANE)
                Xp = xc_ref[:, sl]
                Xdt = Xp * dte_ref[:, sl]
                Xb = Xdt.astype(BF16)
                acs_p = acs_e[:, sl]
                last = acs_p[T - 1:T, :]
                Hin = ht_ref[j]
                hin_ref[0, j] = Hin
                yd = []
                for e in (0, 1):
                    h = 2 * j + e
                    Lm = jnp.exp(jnp.where(tril, acs_s[:, h:h + 1] - acs_t[h:h + 1, :], -1e30))
                    yd.append(_dot((Gm * Lm).astype(BF16), Xb))
                y_off = _dot(Cb, Hin.astype(BF16)) * jnp.exp(acs_p)
                y_ref[:, sl] = jnp.where(lo, yd[0], yd[1]) + y_off + Xp * dsk_ref[:, sl]
                st = _dot(Bb, (Xdt * jnp.exp(last - acs_p)).astype(BF16), TN)
                ht_ref[j] = jnp.exp(last) * Hin + st

    rows = lambda w: pl.BlockSpec((T, w), lambda c: (c, 0))
    return pl.pallas_call(
        body, name="ssd_fwd", grid=(NC,),
        in_specs=[rows(cfg.CONVCH), rows(INNER), rows(LANE), pl.BlockSpec((1, INNER), lambda c: (0, 0)),
                  pl.BlockSpec((LANE, INNER), lambda c: (0, 0))],
        out_specs=[rows(INNER), pl.BlockSpec((1, NPAIR, NST, LANE), lambda c: (c, 0, 0, 0))],
        out_shape=[jax.ShapeDtypeStruct((S, INNER), F32), jax.ShapeDtypeStruct((NC, NPAIR, NST, LANE), F32)],
        scratch_shapes=[pltpu.VMEM((NPAIR, NST, LANE), F32)],
        compiler_params=_params(("arbitrary",)),
    )(xc, dt_exp, a_small, dskip_exp, expand)


def _ssd_bwd(cfg, dy, xc, dt_exp, a_small, dskip_exp, hin, dt_raw, dt_bias_pad, a_log_pad, expand):
    S, T, INNER, G, NPAIR, HS = cfg.S, cfg.T, cfg.INNER, cfg.G, cfg.NPAIR, cfg.HS
    NC = S // T

    def body(dy_ref, xc_ref, dte_ref, as_ref, dsk_ref, hin_ref, raw_ref, bias_ref, alog_ref, e_ref,
             dxc_ref, draw_ref, dbias_ref, dalog_ref, dskip_ref, dht_ref, cols_ref, rows_ref, dacs_ref, ddt_ref):
        first = pl.program_id(0) == 0

        @pl.when(first)
        def _():
            dht_ref[...] = jnp.zeros_like(dht_ref)

        tril = _tril(T)
        tri = tril.astype(F32)
        a_s = as_ref[...]
        acs_s = _dot(tri, a_s, precision=HI)
        acs_e = _dot(acs_s, e_ref[...], precision=HI)
        acs_t = acs_s.T
        lo = lax.broadcasted_iota(I32, (T, LANE), 1) < HP
        last_row = lax.broadcasted_iota(I32, (T, LANE), 0) == T - 1
        cols_ref[...] = jnp.zeros_like(cols_ref)
        rows_ref[...] = jnp.zeros_like(rows_ref)
        dsk_parts = []
        for g in range(G):
            bsl = slice(INNER + g * NST, INNER + (g + 1) * NST)
            csl = slice(INNER + (G + g) * NST, INNER + (G + g + 1) * NST)
            Bb = xc_ref[:, bsl].astype(BF16)
            Cb = xc_ref[:, csl].astype(BF16)
            Gm = _dot(Cb, Bb, NT)
            dG = jnp.zeros((T, T), F32)
            dB = jnp.zeros((T, NST), F32)
            dC = jnp.zeros((T, NST), F32)
            for j in range(g * NPAIR // G, (g + 1) * NPAIR // G):
                sl = slice(j * LANE, (j + 1) * LANE)
                Xp = xc_ref[:, sl]
                dtp = dte_ref[:, sl]
                Xdt = Xp * dtp
                Xb = Xdt.astype(BF16)
                acs_p = acs_e[:, sl]
                last = acs_p[T - 1:T, :]
                e_p, dec, cd = jnp.exp(acs_p), jnp.exp(last - acs_p), jnp.exp(last)
                Hin = hin_ref[0, j]
                Hb = Hin.astype(BF16)
                dHn = dht_ref[j]
                dHb = dHn.astype(BF16)
                dYp = dy_ref[:, sl]
                z = _dot(Cb, Hb)
                dz = (dYp * e_p).astype(BF16)
                dacs_p = dYp * z * e_p
                dC = dC + _dot(dz, Hb, NT)
                dHin = _dot(Cb, dz, TN) + cd * dHn
                dlast = _colsum(dHn * Hin) * cd
                qv = _dot(Bb, dHb)
                dXdt = qv * dec
                ddec = qv * Xdt * dec
                dacs_p = dacs_p - ddec
                dlast = dlast + _colsum(ddec)
                dB = dB + _dot((Xdt * dec).astype(BF16), dHb, NT)
                for e in (0, 1):
                    h = 2 * j + e
                    Lm = jnp.exp(jnp.where(tril, acs_s[:, h:h + 1] - acs_t[h:h + 1, :], -1e30))
                    Mh = Gm * Lm
                    dYe = jnp.where(lo if e == 0 else jnp.logical_not(lo), dYp, 0.0).astype(BF16)
                    dM = _dot(dYe, Xb, NT)
                    dXdt = dXdt + _dot(Mh.astype(BF16), dYe, TN)
                    W = dM * Mh
                    cols_ref[:, h:h + 1] = jnp.sum(W, axis=1, keepdims=True)
                    rows_ref[h:h + 1, :] = _colsum(W)
                    dG = dG + dM * Lm
                dacs_ref[:, sl] = dacs_p + jnp.where(last_row, dlast, 0.0)
                ddt_ref[:, sl] = dXdt * Xp
                dxc_ref[:, sl] = dXdt * dtp + dYp * dsk_ref[:, sl]
                dsk_parts.append(_colsum(dYp * Xp))
                dht_ref[j] = dHin
            dGb = dG.astype(BF16)
            dxc_ref[:, bsl] = dB + _dot(dGb, Cb, TN)
            dxc_ref[:, csl] = dC + _dot(dGb, Bb)
        E = e_ref[...]
        dacs_s = cols_ref[...] - rows_ref[...].T + _dot(dacs_ref[...], E, NT, precision=HI)
        da = _dot(tri, dacs_s, TN, precision=HI)
        heads = lax.broadcasted_iota(I32, (1, LANE), 1) < HS
        A = jnp.where(heads, -jnp.exp(alog_ref[...]), 0.0)
        ddt = _dot(ddt_ref[...], E, NT, precision=HI) + da * A
        draw = jnp.where(heads, ddt * _sigmoid(raw_ref[...] + bias_ref[...]), 0.0)
        draw_ref[...] = draw
        dsk = _dot(jnp.broadcast_to(jnp.concatenate(dsk_parts, axis=1), (8, INNER)), E, NT, precision=HI)[0:1]
        for ref, val in ((dbias_ref, _colsum(draw)), (dalog_ref, _colsum(da * a_s)), (dskip_ref, dsk)):
            @pl.when(first)
            def _():
                ref[...] = val

            @pl.when(jnp.logical_not(first))
            def _():
                ref[...] += val

    dt_raw, _, raw_block = _window(dt_raw)
    rows = lambda w, b=0: pl.BlockSpec((T, w), lambda c: (NC - 1 - c, b))
    vec = lambda w: pl.BlockSpec((1, w), lambda c: (0, 0))
    return pl.pallas_call(
        body, name="ssd_bwd", grid=(NC,),
        in_specs=[rows(INNER), rows(cfg.CONVCH), rows(INNER), rows(LANE), vec(INNER),
                  pl.BlockSpec((1, NPAIR, NST, LANE), lambda c: (NC - 1 - c, 0, 0, 0)), rows(LANE, raw_block), vec(LANE), vec(LANE),
                  pl.BlockSpec((LANE, INNER), lambda c: (0, 0))],
        out_specs=[rows(cfg.CONVCH), rows(LANE), vec(LANE), vec(LANE), vec(LANE)],
        out_shape=[jax.ShapeDtypeStruct((S, cfg.CONVCH), F32), jax.ShapeDtypeStruct((S, LANE), F32)]
        + [jax.ShapeDtypeStruct((1, LANE), F32)] * 3,
        scratch_shapes=[pltpu.VMEM((NPAIR, NST, LANE), F32), pltpu.VMEM((T, LANE), F32), pltpu.VMEM((LANE, T), F32),
                        pltpu.VMEM((T, INNER), F32), pltpu.VMEM((T, INNER), F32)],
        compiler_params=_params(("arbitrary",)),
    )(dy, xc, dt_exp, a_small, dskip_exp, hin, dt_raw, dt_bias_pad, a_log_pad, expand)


def _ssd_post(cfg, y, z, norm_g):
    W = cfg.INNER // cfg.G

    def fn(y, z, g):
        yz = y * z * _sigmoid(z)
        return jnp.concatenate([yz[:, i * W:(i + 1) * W] * _rs(yz[:, i * W:(i + 1) * W]) for i in range(cfg.G)], axis=1) * g

    return _rowwise("ssd_post", fn, [y, z], [norm_g], [(cfg.INNER, BF16)], [], _pick(cfg.S, 256, 8))[0]


def _ssd_post_bwd(cfg, db, y, z, norm_g):
    W = cfg.INNER // cfg.G

    def fn(db, y, z, g):
        sg = _sigmoid(z)
        yz = y * z * sg
        dn = db * g
        dyz, nh = [], []
        for i in range(cfg.G):
            seg = yz[:, i * W:(i + 1) * W]
            r = _rs(seg)
            nh.append(seg * r)
            dyz.append(_rms_back(nh[-1], r, dn[:, i * W:(i + 1) * W]))
        dyz = jnp.concatenate(dyz, axis=1)
        return dyz * z * sg, dyz * y * sg * (1.0 + z * (1.0 - sg)), _colsum(db * jnp.concatenate(nh, axis=1))

    return _rowwise("ssd_post_bwd", fn, [db, y, z], [norm_g], [(cfg.INNER, F32), (cfg.INNER, F32)], [(1, cfg.INNER)],
                    _pick(cfg.S, 256, 8))


def _rms_pre(cfg, x, g):
    return _rowwise("rms_pre", lambda x, g: x * _rs(x) * g, [x], [g], [(cfg.D, BF16)], [], _pick(cfg.S, 256, 8))[0]


def _local_grads(cfg, x, tgt, W, sp, mla_weights=None, out_weight=None, ffn_weights=None, down_weight=None,
                 ffn_grads_ready=None, early_grads_ready=None, in_grad_ready=None, xn=None, after_in=None):
    S, D, H, INNER = cfg.S, cfg.D, cfg.H, cfg.INNER
    ts = _pick(S, 256, 8)
    tc = _CONV_COLS

    if xn is None:
        xn = _rms_pre(cfg, x, sp["mix_pre_g"])
    u = _matmul("mm_in", xn, W["w_in"], "nt", F32, after=after_in)
    c_q, c_kv, kr, z, xbc, dt_raw = [(u, cfg.window(n)) for n in ("c_q", "c_kv", "kr", "z", "xbc", "dt")]

    if mla_weights is not None:
        sp = dict(sp, q_norm_g=sp["q_norm_g"] + mla_weights.pass_on(u)[0, 0])
    cqn = _rowwise("rms_q", lambda x, g: x * _rs(x) * g, [c_q], [sp["q_norm_g"]], [(cfg.QL, BF16)], [], ts)[0]
    ckvn = _rowwise("rms_kv", lambda x, g: x * _rs(x) * g, [c_kv], [sp["kv_norm_g"]], [(cfg.KVL, BF16)], [], ts)[0]
    if mla_weights is not None:
        W = dict(W, **mla_weights.arrived(ckvn))
    q = _matmul("mm_uq", cqn, W["w_uq"], "nn", F32)
    kv = _matmul("mm_ukv", ckvn, W["w_ukv"], "nn", F32)
    cos2, sin2 = _rope_tables(S)
    Qh, Kh, Vh = _mla_pack(cfg, q, kv, kr, cos2, sin2)
    a_out, lse_t = _attn_fwd(cfg, Qh, Kh, Vh)
    if out_weight is not None:
        sp = dict(sp, ssm_conv_b=sp["ssm_conv_b"] + out_weight.pass_on(a_out)[0, 0])

    pad = lambda v: jnp.pad(v, ((0, 0), (0, LANE - v.shape[1])))
    expand = _expand_matrix(cfg)
    dt_bias_pad, a_log_pad = pad(sp["dt_bias"]), pad(sp["a_log"])
    dskip_exp = jnp.repeat(sp["d_skip"], HP, axis=1)
    xc = _colwise("ssm_act", _ssm_act, [xbc], [sp["ssm_conv_w"], sp["ssm_conv_b"]], [F32], [], tc)[0]
    dt_s, a_s, dt_exp = _ssd_prep(cfg, dt_raw, dt_bias_pad, a_log_pad, expand)
    y_ssd, hin = _ssd_fwd(cfg, xc, dt_exp, a_s, dskip_exp, expand)
    b_out = _ssd_post(cfg, y_ssd, z, sp["ssm_norm_g"])

    ab_out = jnp.concatenate([a_out.astype(BF16), b_out], axis=1)
    if out_weight is not None:
        W = dict(W, **out_weight.arrived(ab_out))
    if ffn_weights is not None:
        sp = dict(sp, mix_post_g=sp["mix_post_g"] + ffn_weights.pass_on(ab_out)[0, 0])
    mix = _matmul("mm_out", ab_out, W["w_out"], "nn", F32)

    def mid(x, mix, g_mp, g_fp):
        x1 = x + mix * _rs(mix) * g_mp
        return x1, x1 * _rs(x1) * g_fp

    x1, h2 = _rowwise("fwd_mid", mid, [x, mix], [sp["mix_post_g"], sp["ffn_pre_g"]], [(D, F32), (D, BF16)], [], ts)
    if ffn_weights is not None:
        W = dict(W, **ffn_weights.arrived(h2))
    gate_pre, up = _matmul_twin("mm_gate_up", h2, W["w_gate"], W["w_up"], "nn", F32)
    if down_weight is not None:
        sp = dict(sp, ffn_conv_b=sp["ffn_conv_b"] + down_weight.pass_on(gate_pre)[0, 0])
    act = _colwise("ffn_act", _ffn_act, [gate_pre, up], [sp["ffn_conv_w"], sp["ffn_conv_b"]], [BF16], [], tc)[0]
    if down_weight is not None:
        W = dict(W, **down_weight.arrived(act))
    f = _matmul("mm_down", act, W["w_down"], "nn", F32)

    def final(x1, f, t, g):
        r = _rs(f)
        fh = f * r
        err = x1 + fh * g - t
        loss = 0.5 * jnp.sum(jnp.mean(err * err, axis=-1, keepdims=True), axis=0, keepdims=True)
        dy = err * (1.0 / D)
        return dy, _rms_back(fh, r, dy * g), _colsum(dy * fh), loss

    dy, df, g_ffn_post, loss = _rowwise("final", final, [x1, f, tgt], [sp["ffn_post_g"]], [(D, F32), (D, BF16)],
                                        [(1, D), (1, LANE)], ts)
    gW = {}
    dact = _matmul("mm_down_dx", df, W["w_down"], "nt", F32)
    gW["w_down"] = _matmul("mm_down_dw", act, df, "tn", BF16)
    dgate, dup, g_ffn_conv_w, g_ffn_conv_b = _colwise(
        "ffn_act_bwd", _ffn_act_back, [dact, gate_pre, up], [sp["ffn_conv_w"], sp["ffn_conv_b"]], [BF16, BF16], [FFN_K, 1], tc)
    gW["w_gate"], gW["w_up"] = _matmul_twin("mm_gate_up_dw", h2, dgate, dup, "tn", BF16)
    if ffn_grads_ready is not None:
        sp = dict(sp, ffn_pre_g=sp["ffn_pre_g"] + ffn_grads_ready({n: gW[n] for n in ("w_down", "w_gate", "w_up")})[0, 0])
    dh2 = _matmul("mm_gu_dx", dgate, W["w_gate"], "nt", F32, dup, W["w_up"], chips=True)

    def mid_back(dy, dh2, x1, mix, g_mp, g_fp):
        r2 = _rs(x1)
        xh = x1 * r2
        dx1 = dy + _rms_back(xh, r2, dh2 * g_fp)
        r1 = _rs(mix)
        mh = mix * r1
        return dx1, _rms_back(mh, r1, dx1 * g_mp), _colsum(dh2 * xh), _colsum(dx1 * mh)

    dx1, dmix, g_ffn_pre, g_mix_post = _rowwise("bwd_mid", mid_back, [dy, dh2, x1, mix], [sp["mix_post_g"], sp["ffn_pre_g"]],
                                                [(D, F32), (D, BF16)], [(1, D), (1, D)], ts)
    dab_out = _matmul("mm_out_dx", dmix, W["w_out"], "nt", F32)
    db_out = (dab_out, (INNER, cfg.MLAW // INNER))
    gW["w_out"] = _matmul("mm_out_dw", ab_out, dmix, "tn", BF16)
    early_token = jnp.zeros((8, LANE), F32)
    if early_grads_ready is not None:
        early_token = early_grads_ready({n: gW[n] for n in ("w_down", "w_gate", "w_up", "w_out")})
        sp = dict(sp, ssm_norm_g=sp["ssm_norm_g"] + early_token[0, 0])

    dy_ssd, dz, g_ssm_norm = _ssd_post_bwd(cfg, db_out, y_ssd, z, sp["ssm_norm_g"])
    dxc, ddt_raw, g_dt_bias, g_a_log, g_d_skip = _ssd_bwd(cfg, dy_ssd, xc, dt_exp, a_s, dskip_exp, hin, dt_raw,
                                                          dt_bias_pad, a_log_pad, expand)
    dxbc, g_ssm_conv_w, g_ssm_conv_b = _colwise("ssm_act_bwd", _ssm_act_back, [dxc, xbc], [sp["ssm_conv_w"], sp["ssm_conv_b"]],
                                                [BF16], [SSM_K, 1], tc)

    delta_t = _attn_delta(cfg, dab_out, a_out, early_token)
    dQ, dK, dV = _attn_bwd(cfg, Qh, Kh, Vh, dab_out, lse_t, delta_t)
    dq, dkv, dkr = _mla_unpack(cfg, dQ, dK, dV, cos2, sin2)
    dcqn = _matmul("mm_uq_dx", dq, W["w_uq"], "nt", F32)
    dckvn = _matmul("mm_ukv_dx", dkv, W["w_ukv"], "nt", F32)
    gW["w_uq"] = _matmul("mm_uq_dw", cqn, dq, "tn", BF16)
    gW["w_ukv"] = _matmul("mm_ukv_dw", ckvn, dkv, "tn", BF16)

    def rms_back(x, dy, g):
        r = _rs(x)
        xh = x * r
        return _rms_back(xh, r, dy * g), _colsum(dy * xh)

    dc_q, g_q_norm = _rowwise("rms_q_bwd", rms_back, [c_q, dcqn], [sp["q_norm_g"]], [(cfg.QL, BF16)], [(1, cfg.QL)], ts)
    dc_kv, g_kv_norm = _rowwise("rms_kv_bwd", rms_back, [c_kv, dckvn], [sp["kv_norm_g"]], [(cfg.KVL, BF16)], [(1, cfg.KVL)], ts)

    du = dict(c_q=dc_q, c_kv=dc_kv, kr=dkr, z=dz.astype(BF16), xbc=dxbc, dt=ddt_raw.astype(BF16))
    du = jnp.concatenate([du[n] for n in sorted(du, key=lambda n: cfg.seg[n][0])], axis=1)
    assert du.shape[1] == cfg.EXT, "the layout of u has gaps"
    gW["w_in"] = _matmul("mm_in_dw", du, xn, "tn", BF16)
    if in_grad_ready is not None:
        token = in_grad_ready({n: gW[n] for n in ("w_in", "w_uq", "w_ukv")})
        sp = dict(sp, mix_pre_g=sp["mix_pre_g"] + token[0, 0])
    dxn = _matmul("mm_in_dx", du, W["w_in"], "nn", F32)

    def first_back(dx1, dxn, x, g):
        r = _rs(x)
        xh = x * r
        return dx1 + _rms_back(xh, r, dxn * g), _colsum(dxn * xh)

    grad_x, g_mix_pre = _rowwise("bwd_first", first_back, [dx1, dxn, x], [sp["mix_pre_g"]], [(D, F32)], [(1, D)], ts)

    gs = dict(mix_pre_g=g_mix_pre, q_norm_g=g_q_norm, kv_norm_g=g_kv_norm, ssm_conv_w=g_ssm_conv_w, ssm_conv_b=g_ssm_conv_b,
              dt_bias=g_dt_bias[:, :cfg.HS], a_log=g_a_log[:, :cfg.HS], d_skip=g_d_skip[:, :cfg.HS], ssm_norm_g=g_ssm_norm,
              mix_post_g=g_mix_post, ffn_pre_g=g_ffn_pre, ffn_conv_w=g_ffn_conv_w, ffn_conv_b=g_ffn_conv_b,
              ffn_post_g=g_ffn_post)
    return loss, grad_x, gW, gs


def _to_kernel_layout(cfg, name, w):
    if name == "w_in":
        parts, at = [], 0
        for off, width, n_off, n_width in sorted(cfg.seg.values()):
            parts += [jnp.zeros((off - at, w.shape[1]), w.dtype), w[n_off:n_off + n_width],
                      jnp.zeros((width - n_width, w.shape[1]), w.dtype)]
            at = off + width
        parts.append(jnp.zeros((cfg.EXT - at, w.shape[1]), w.dtype))
        return jnp.concatenate([p for p in parts if p.shape[0]], axis=0)
    if name in ("w_uq", "w_ukv"):
        per = NOPE + (ROPE if name == "w_uq" else VH)
        return jnp.concatenate([w[:, h * per:h * per + NOPE] for h in range(cfg.H)]
                               + [w[:, h * per + NOPE:(h + 1) * per] for h in range(cfg.H)], axis=1)
    return w


def _from_kernel_layout(cfg, name, g):
    if name == "w_in":
        return jnp.concatenate([g[off:off + n_width] for off, _, _, n_width in sorted(cfg.seg.values(), key=lambda s: s[2])], axis=0)
    if name in ("w_uq", "w_ukv"):
        second = ROPE if name == "w_uq" else VH
        base = cfg.H * NOPE
        parts = []
        for h in range(cfg.H):
            parts += [g[:, h * NOPE:(h + 1) * NOPE], g[:, base + h * second:base + (h + 1) * second]]
        return jnp.concatenate(parts, axis=1)
    return g


_CHIP_MAJOR = ("w_gate", "w_up")
_RELAYOUT = ("w_uq", "w_ukv")
_LAYOUT_ROWS = 256
_CONV_COLS = 256


def _w_in_layout(cfg, wg):
    _, rs, d = wg.shape
    tc = _pick(d, _LAYOUT_ROWS, LANE)

    def body(w_ref, o_ref):
        o_ref[...] = _to_kernel_layout(cfg, "w_in", jnp.concatenate([w_ref[k] for k in range(N_CHIPS)], axis=0))

    return pl.pallas_call(
        body, name="layout_w_in", grid=(d // tc,),
        in_specs=[pl.BlockSpec((N_CHIPS, rs, tc), lambda j: (0, 0, j))], out_specs=pl.BlockSpec((cfg.EXT, tc), lambda j: (0, j)),
        out_shape=jax.ShapeDtypeStruct((cfg.EXT, d), wg.dtype), compiler_params=_params(("parallel",)),
    )(wg)


def _w_in_grad_to_chips(cfg, g):
    _, d = g.shape
    rs = cfg.IN_COLS // N_CHIPS
    tc = _pick(d, _LAYOUT_ROWS, LANE)

    def body(g_ref, o_ref):
        nat = _from_kernel_layout(cfg, "w_in", g_ref[...])
        for k in range(N_CHIPS):
            o_ref[k] = nat[k * rs:(k + 1) * rs]

    return pl.pallas_call(
        body, name="layout_grad_w_in", grid=(d // tc,),
        in_specs=[pl.BlockSpec((cfg.EXT, tc), lambda j: (0, j))], out_specs=pl.BlockSpec((N_CHIPS, rs, tc), lambda j: (0, 0, j)),
        out_shape=jax.ShapeDtypeStruct((N_CHIPS, rs, d), g.dtype), compiler_params=_params(("parallel",)),
    )(g)


def _gathered_to_kernel(cfg, name, wg):
    if name in _CHIP_MAJOR:
        return wg
    if name == "w_in":
        return _w_in_layout(cfg, wg)
    if name not in _RELAYOUT:
        return wg.reshape(wg.shape[0] * wg.shape[1], wg.shape[2])
    _, rows, cs = wg.shape
    tr = _pick(rows, _LAYOUT_ROWS, 16)

    def body(w_ref, o_ref):
        o_ref[...] = _to_kernel_layout(cfg, name, jnp.concatenate([w_ref[k] for k in range(N_CHIPS)], axis=1))

    wide = jax.eval_shape(lambda w: _to_kernel_layout(cfg, name, w), jax.ShapeDtypeStruct((rows, N_CHIPS * cs), wg.dtype)).shape[1]
    return pl.pallas_call(
        body, name="layout_" + name, grid=(rows // tr,),
        in_specs=[pl.BlockSpec((N_CHIPS, tr, cs), lambda i: (0, i, 0))], out_specs=pl.BlockSpec((tr, wide), lambda i: (i, 0)),
        out_shape=jax.ShapeDtypeStruct((rows, wide), wg.dtype), compiler_params=_params(("parallel",)),
    )(wg)


def _grad_to_chips(cfg, name, g):
    if name in _CHIP_MAJOR:
        return g
    if name == "w_in":
        return _w_in_grad_to_chips(cfg, g)
    if name not in _RELAYOUT:
        return g.reshape(N_CHIPS, g.shape[0] // N_CHIPS, g.shape[1])
    rows, wide = g.shape
    tr = _pick(rows, _LAYOUT_ROWS, 16)
    cs = jax.eval_shape(lambda v: _from_kernel_layout(cfg, name, v), g).shape[1] // N_CHIPS

    def body(g_ref, o_ref):
        nat = _from_kernel_layout(cfg, name, g_ref[...])
        for k in range(N_CHIPS):
            o_ref[k] = nat[:, k * cs:(k + 1) * cs]

    return pl.pallas_call(
        body, name="layout_grad_" + name, grid=(rows // tr,),
        in_specs=[pl.BlockSpec((tr, wide), lambda i: (i, 0))], out_specs=pl.BlockSpec((N_CHIPS, tr, cs), lambda i: (0, i, 0)),
        out_shape=jax.ShapeDtypeStruct((N_CHIPS, rows, cs), g.dtype), compiler_params=_params(("parallel",)),
    )(g)


def _me():
    return lax.axis_index("x"), lax.axis_index("y"), lax.axis_index("c")


def _other_chips(x, y):
    return [(1 - x, y), (x, 1 - y), (1 - x, 1 - y)]


_ANY = pl.BlockSpec(memory_space=pl.ANY)


BLOCK_ELEMS = 1 << 19
BLOCK_ELEMS_FEW = 1 << 20


def _row_block(rows, cols, mult, elems=BLOCK_ELEMS):
    return _pick(rows, max(mult, elems // cols // mult * mult), mult)


def _scalar(v):
    return v.astype(I32).reshape(1)


def _blocks2d(r, c, mult, elems=BLOCK_ELEMS):
    if r % mult == 0:
        tr = _row_block(r, c, mult, elems)
        return (tr, c), r // tr, lambda i: (i, 0)
    tc = _pick(c, max(LANE, elems // r // LANE * LANE), LANE)
    return (r, tc), c // tc, lambda i: (0, i)


def _by_rows(rows):
    return rows % 32 == 0


def _half_shape(rows, cols):
    return (rows // 2, cols) if _by_rows(rows) else (rows, cols // 2)


def _half_blocks(rows, cols, mult, elems=BLOCK_ELEMS):
    hr, hc = _half_shape(rows, cols)
    block, n, part = _blocks2d(hr, hc, mult, elems)
    assert (hr % mult == 0) == _by_rows(rows), (rows, cols, mult)
    full = (lambda h, i: (h * n + i, 0)) if _by_rows(rows) else (lambda h, i: (0, h * n + i))
    return block, n, full, part


def _half(ref, k, half):
    hr, hc = _half_shape(ref.shape[1], ref.shape[2])
    if _by_rows(ref.shape[1]):
        return ref.at[k, pl.ds(pl.multiple_of(half * hr, 16), hr), :]
    return ref.at[k, :, pl.ds(pl.multiple_of(half * hc, LANE), hc)]


def _shard_blocks(w, br, bc):
    if w.shape[0] == 1:
        def write(ref, v):
            ref[...] = v
        return (lambda f: pl.BlockSpec((None, br, bc), lambda *a: (0, *f(*a)))), (lambda ref: ref[...]), write
    assert w.shape[1] == 1 and br == w.shape[0], w.shape

    def write_rows(ref, v):
        ref[:, 0, :] = v
    return (lambda f: pl.BlockSpec((br, 1, bc), lambda *a: (0, 0, f(*a)[1]))), (lambda ref: ref[:, 0, :]), write_rows


def _stage_shard(name, w, chip, after=None):
    rs, cs = w.shape[0] * w.shape[1], w.shape[2]
    (br, bc), n, idx = _blocks2d(rs, cs, 16, BLOCK_ELEMS_FEW)
    spec, get, _ = _shard_blocks(w, br, bc)

    def body(chip_ref, w_ref, *refs):
        refs[-1][...] = get(w_ref).astype(BF16)

    return pl.pallas_call(
        body, name="stage_" + name,
        grid_spec=pltpu.PrefetchScalarGridSpec(
            num_scalar_prefetch=1, grid=(n,),
            in_specs=[spec(lambda i, chip_ref: idx(i))] + ([] if after is None else [_ANY]),
            out_specs=pl.BlockSpec((None, br, bc), lambda i, chip_ref: (chip_ref[0], *idx(i)))),
        out_shape=jax.ShapeDtypeStruct((N_CHIPS, rs, cs), BF16),
        compiler_params=_params(("parallel",)),
    )(_scalar(chip), w, *([] if after is None else [after]))


_HBM = pl.BlockSpec(memory_space=pltpu.HBM)
_SEM = pl.BlockSpec(memory_space=pltpu.SEMAPHORE)
_EFFECT = pltpu.SideEffectType.DATAFLOW_SIDE_EFFECTING


def _split_start(name, bufs, n_copies, copies, after):
    n = len(bufs)

    def body(*refs):
        for cp in copies(refs[:n], refs[n + 1], refs[n + 2]):
            cp.start()
        refs[-1][...] = jnp.zeros_like(refs[-1])

    res = pl.pallas_call(
        body, name=name,
        out_shape=(pltpu.SemaphoreType.DMA((n_copies,)), pltpu.SemaphoreType.DMA((n_copies,)),
                   *[pltpu.HBM(b.shape, b.dtype) for b in bufs], jax.ShapeDtypeStruct((8, LANE), F32)),
        in_specs=[_HBM] * n + [_ANY], out_specs=(_SEM, _SEM, *[_HBM] * n, pl.BlockSpec(memory_space=pltpu.VMEM)),
        input_output_aliases={i: 2 + i for i in range(n)},
        compiler_params=pltpu.CompilerParams(has_side_effects=_EFFECT),
    )(*[pltpu.with_memory_space_constraint(b, pltpu.HBM) for b in bufs], after)
    return res[0], res[1], list(res[2:2 + n]), res[-1]


def _split_wait(name, send_sems, recv_sems, bufs, after, copies):
    n = len(bufs)

    def body(*refs):
        for cp in copies(refs[:n], refs[n], refs[n + 1]):
            cp.wait_send()
            cp.wait_recv()

    return list(pl.pallas_call(
        body, name=name, out_shape=[pltpu.HBM(b.shape, b.dtype) for b in bufs],
        in_specs=[_HBM] * n + [_SEM, _SEM, _ANY], out_specs=[_HBM] * n,
        input_output_aliases={i: i for i in range(n)},
        compiler_params=pltpu.CompilerParams(has_side_effects=_EFFECT),
    )(*bufs, send_sems, recv_sems, after))


def _gather_to_chips(bufs, send_sems, recv_sems):
    x, y, c = _me()
    return [pltpu.make_async_remote_copy(src_ref=_half(b, 2 * x + y, c), dst_ref=_half(b, 2 * x + y, c),
                                         send_sem=send_sems.at[3 * w + j], recv_sem=recv_sems.at[3 * w + j],
                                         device_id=(cx, cy, c), device_id_type=MESH_ID)
            for w, b in enumerate(bufs) for j, (cx, cy) in enumerate(_other_chips(x, y))]


def _gather_to_sibling(bufs, send_sems, recv_sems):
    x, y, c = _me()
    return [pltpu.make_async_remote_copy(src_ref=_half(b, 2 * cx + cy, c), dst_ref=_half(b, 2 * cx + cy, c),
                                         send_sem=send_sems.at[3 * w + j], recv_sem=recv_sems.at[3 * w + j],
                                         device_id=(x, y, 1 - c), device_id_type=MESH_ID)
            for w, b in enumerate(bufs) for j, (cx, cy) in enumerate(_other_chips(x, y))]


def _pair_exchange(name, grads):
    n = len(grads)

    def body(*refs):
        ins, outs, send_sems, recv_sems = refs[:n], refs[n:2 * n], refs[2 * n], refs[2 * n + 1]
        x, y, c = _me()
        cps = []
        for w, (g_ref, o_ref) in enumerate(zip(ins, outs)):
            cps.append(pltpu.make_async_remote_copy(src_ref=_half(g_ref, slice(None), 1 - c), dst_ref=o_ref,
                                                    send_sem=send_sems.at[w], recv_sem=recv_sems.at[w],
                                                    device_id=(x, y, 1 - c), device_id_type=MESH_ID))
            cps[-1].start()
        for cp in cps:
            cp.wait()

    return pl.pallas_call(
        body, name="pair_exchange_" + name, in_specs=[_ANY] * n, out_specs=[_ANY] * n,
        out_shape=[jax.ShapeDtypeStruct((g.shape[0], *_half_shape(g.shape[1], g.shape[2])), g.dtype) for g in grads],
        scratch_shapes=[pltpu.SemaphoreType.DMA((n,)), pltpu.SemaphoreType.DMA((n,))],
    )(*grads)


def _pair_copies(grads, lands, send_sems, recv_sems):
    x, y, c = _me()
    return [pltpu.make_async_remote_copy(src_ref=_half(g_ref, slice(None), 1 - c), dst_ref=l_ref, send_sem=send_sems.at[w],
                                         recv_sem=recv_sems.at[w], device_id=(x, y, 1 - c), device_id_type=MESH_ID)
            for w, (g_ref, l_ref) in enumerate(zip(grads, lands))]


def _pair_exchange_start(name, grads):
    n = len(grads)
    lands = [lax.empty((g.shape[0], *_half_shape(g.shape[1], g.shape[2])), g.dtype) for g in grads]
    send_sems, recv_sems, bufs, token = _split_start(
        "pair_exchange_start_" + name, [*grads, *lands], n, lambda refs, ss, rs: _pair_copies(refs[:n], refs[n:], ss, rs),
        jnp.zeros((8, LANE), F32))
    return (send_sems, recv_sems, bufs), token


def _pair_exchange_wait(name, state, after):
    send_sems, recv_sems, bufs = state
    n = len(bufs) // 2
    bufs = _split_wait("pair_exchange_wait_" + name, send_sems, recv_sems, bufs, after,
                       lambda refs, ss, rs: _pair_copies(refs[:n], refs[n:], ss, rs))
    return bufs[:n], bufs[n:]


def _pair_sum(name, g, theirs, c):
    (br, bc), nb, full, part = _half_blocks(g.shape[1], g.shape[2], 16, 2 * BLOCK_ELEMS_FEW)

    def body(c_ref, a_ref, b_ref, o_ref):
        o_ref[...] = (a_ref[...].astype(F32) + b_ref[...].astype(F32)).astype(o_ref.dtype)

    return pl.pallas_call(
        body, name="pair_sum_" + name,
        grid_spec=pltpu.PrefetchScalarGridSpec(
            num_scalar_prefetch=1, grid=(N_CHIPS, nb),
            in_specs=[pl.BlockSpec((None, br, bc), lambda k, i, c_ref: (k, *full(c_ref[0], i))),
                      pl.BlockSpec((None, br, bc), lambda k, i, c_ref: (k, *part(i)))],
            out_specs=pl.BlockSpec((None, br, bc), lambda k, i, c_ref: (k, *part(i)))),
        out_shape=jax.ShapeDtypeStruct(theirs.shape, BF16),
        compiler_params=_params(("parallel", "parallel")),
    )(_scalar(c), g, theirs)


def _chip_copies(srcs, lands, send_sems, recv_sems):
    x, y, c = _me()
    return [pltpu.make_async_remote_copy(src_ref=s_ref.at[2 * cx + cy], dst_ref=l_ref.at[j], send_sem=send_sems.at[3 * w + j],
                                         recv_sem=recv_sems.at[3 * w + j], device_id=(cx, cy, c), device_id_type=MESH_ID)
            for w, (s_ref, l_ref) in enumerate(zip(srcs, lands)) for j, (cx, cy) in enumerate(_other_chips(x, y))]


def _chip_exchange_start(name, sums):
    n = len(sums)
    lands = [lax.empty((3,) + s.shape[1:], s.dtype) for s in sums]
    send_sems, recv_sems, bufs, token = _split_start(
        "chip_exchange_start_" + name, [*sums, *lands], 3 * n, lambda refs, ss, rs: _chip_copies(refs[:n], refs[n:], ss, rs),
        jnp.zeros((8, LANE), F32))
    return send_sems, recv_sems, bufs[:n], bufs[n:], token


def _chip_exchange_wait(name, send_sems, recv_sems, sums, lands, after):
    n = len(sums)
    bufs = _split_wait("chip_exchange_wait_" + name, send_sems, recv_sems, [*sums, *lands], after,
                       lambda refs, ss, rs: _chip_copies(refs[:n], refs[n:], ss, rs))
    return bufs[:n], bufs[n:]


def _chip_sum(name, sums, theirs, chip):
    _, h, cs = sums.shape
    (br, bc), nb, idx = _blocks2d(h, cs, 16, BLOCK_ELEMS_FEW)

    def body(chip_ref, s_ref, t_ref, o_ref):
        acc = s_ref[...].astype(F32)
        for k in range(3):
            acc = acc + t_ref[k].astype(F32)
        o_ref[...] = acc

    return pl.pallas_call(
        body, name="chip_sum_" + name,
        grid_spec=pltpu.PrefetchScalarGridSpec(
            num_scalar_prefetch=1, grid=(nb,),
            in_specs=[pl.BlockSpec((None, br, bc), lambda i, chip_ref: (chip_ref[0], *idx(i))),
                      pl.BlockSpec((3, br, bc), lambda i, chip_ref: (0, *idx(i)))],
            out_specs=pl.BlockSpec((br, bc), lambda i, chip_ref: idx(i))),
        out_shape=jax.ShapeDtypeStruct((h, cs), F32),
        compiler_params=_params(("parallel",)),
    )(_scalar(chip), sums, theirs)


def _sibling_copies(halves, lands, send_sems, recv_sems):
    x, y, c = _me()
    return [pltpu.make_async_remote_copy(src_ref=h_ref, dst_ref=l_ref, send_sem=send_sems.at[w], recv_sem=recv_sems.at[w],
                                         device_id=(x, y, 1 - c), device_id_type=MESH_ID)
            for w, (h_ref, l_ref) in enumerate(zip(halves, lands))]


def _sibling_exchange_start(name, halves, after):
    n = len(halves)
    lands = [lax.empty(h.shape, h.dtype) for h in halves]
    send_sems, recv_sems, bufs, token = _split_start(
        "sibling_exchange_start_" + name, [*halves, *lands], n, lambda refs, ss, rs: _sibling_copies(refs[:n], refs[n:], ss, rs),
        after)
    return (send_sems, recv_sems, bufs), token


def _sibling_exchange_wait(name, state, after):
    send_sems, recv_sems, bufs = state
    n = len(bufs) // 2
    bufs = _split_wait("sibling_exchange_wait_" + name, send_sems, recv_sems, bufs, after,
                       lambda refs, ss, rs: _sibling_copies(refs[:n], refs[n:], ss, rs))
    return bufs[:n], bufs[n:]


def _sibling_exchange(name, halves):
    n = len(halves)

    def body(*refs):
        ins, outs, send_sems, recv_sems = refs[:n], refs[n:2 * n], refs[2 * n], refs[2 * n + 1]
        x, y, c = _me()
        cps = []
        for w, (h_ref, o_ref) in enumerate(zip(ins, outs)):
            cps.append(pltpu.make_async_remote_copy(src_ref=h_ref, dst_ref=o_ref, send_sem=send_sems.at[w], recv_sem=recv_sems.at[w],
                                                    device_id=(x, y, 1 - c), device_id_type=MESH_ID))
            cps[-1].start()
        for cp in cps:
            cp.wait()

    return pl.pallas_call(
        body, name="sibling_exchange_" + name, in_specs=[_ANY] * n, out_specs=[_ANY] * n,
        out_shape=[jax.ShapeDtypeStruct(h.shape, h.dtype) for h in halves],
        scratch_shapes=[pltpu.SemaphoreType.DMA((n,)), pltpu.SemaphoreType.DMA((n,))],
    )(*halves)


N_DEV = 8


def _peer_copies(bufs, send_sems, recv_sems):
    vec, land = bufs
    x, y, c = _me()
    return [pltpu.make_async_remote_copy(src_ref=vec, dst_ref=land.at[4 * x + 2 * y + c], send_sem=send_sems.at[p - 1],
                                         recv_sem=recv_sems.at[p - 1], device_id=(x ^ (p >> 2), y ^ ((p >> 1) & 1), c ^ (p & 1)),
                                         device_id_type=MESH_ID) for p in range(1, N_DEV)]


def _allreduce_small_start(vec, after):
    land = jnp.zeros((N_DEV,) + vec.shape, F32)
    send_sems, recv_sems, bufs, _ = _split_start("allreduce_small_start", [vec, land], N_DEV - 1, _peer_copies, after)
    return send_sems, recv_sems, bufs


def _allreduce_small_wait(state, chip, core, after):
    send_sems, recv_sems, bufs = state
    vec, land = _split_wait("allreduce_small_wait", send_sems, recv_sems, bufs, after, _peer_copies)

    def body(me_ref, v_ref, l_ref, o_ref):
        acc = None
        for k in range(N_DEV):
            term = jnp.where(me_ref[0] == k, v_ref[...], l_ref[k])
            acc = term if acc is None else acc + term
        o_ref[...] = acc

    return pl.pallas_call(
        body, name="allreduce_small_sum",
        grid_spec=pltpu.PrefetchScalarGridSpec(
            num_scalar_prefetch=1, grid=(1,),
            in_specs=[pl.BlockSpec(vec.shape, lambda i, me_ref: (0, 0)), pl.BlockSpec(land.shape, lambda i, me_ref: (0, 0, 0))],
            out_specs=pl.BlockSpec(vec.shape, lambda i, me_ref: (0, 0))),
        out_shape=jax.ShapeDtypeStruct(vec.shape, F32), compiler_params=_params(("arbitrary",)),
    )(_scalar(2 * chip + core), vec, land)


def _adam_math(w, g, m, v):
    m = ADAM_B1 * m + (1.0 - ADAM_B1) * g
    v = ADAM_B2 * v + (1.0 - ADAM_B2) * (g * g)
    m_hat = m / (1.0 - ADAM_B1 ** ADAM_STEP)
    v_hat = v / (1.0 - ADAM_B2 ** ADAM_STEP)
    return -ADAM_LR * (m_hat / (jnp.sqrt(v_hat) + ADAM_EPS) + ADAM_WD * w), m, v


def _adamw(name, w, g, m, v):
    R, C = w.shape
    tr = _row_block(R, C, 8)

    def body(w_ref, g_ref, m_ref, v_ref, d_ref, nm_ref, nv_ref):
        d_ref[...], nm_ref[...], nv_ref[...] = _adam_math(w_ref[...], g_ref[...], m_ref[...], v_ref[...])

    blk = pl.BlockSpec((tr, C), lambda i: (i, 0))
    return pl.pallas_call(
        body, name=name, grid=(R // tr,), in_specs=[blk] * 4, out_specs=[blk] * 3,
        out_shape=[jax.ShapeDtypeStruct((R, C), F32)] * 3, compiler_params=_params(("parallel",)),
    )(w, g, m, v)


def _adamw_halves(name, w, mine, theirs, m, v, c):
    rs, cs = w.shape[0] * w.shape[1], w.shape[2]
    (br, bc), nb, whole, half = _half_blocks(rs, cs, 8)
    spec, get, put = _shard_blocks(w, br, bc)

    def body(c_ref, w_ref, a_ref, b_ref, m_ref, v_ref, g_ref, d_ref, nm_ref, nv_ref):
        g = jnp.where(pl.program_id(0) == c_ref[0], a_ref[...], b_ref[...])
        put(g_ref, g)
        for ref, val in zip((d_ref, nm_ref, nv_ref), _adam_math(get(w_ref), g, get(m_ref), get(v_ref))):
            put(ref, val)

    full = spec(lambda s, i, c_ref: whole(s, i))
    part = pl.BlockSpec((br, bc), lambda s, i, c_ref: half(i))
    return pl.pallas_call(
        body, name=name,
        grid_spec=pltpu.PrefetchScalarGridSpec(num_scalar_prefetch=1, grid=(2, nb), in_specs=[full, part, part, full, full],
                                               out_specs=[full] * 4),
        out_shape=[jax.ShapeDtypeStruct(w.shape, F32)] * 4, compiler_params=_params(("parallel", "parallel")),
    )(_scalar(c), w, mine, theirs, m, v)


def _pack_small(arrs, lanes=LANE):
    flat = jnp.concatenate([a.reshape(-1) for a in arrs])
    n = -(-flat.shape[0] // (8 * lanes)) * 8 * lanes
    return jnp.pad(flat, (0, n - flat.shape[0])).reshape(8, n // 8)


def _unpack_small(vec, shapes):
    flat, out, off = vec.reshape(-1), [], 0
    for s in shapes:
        out.append(flat[off:off + s[0] * s[1]].reshape(s))
        off += s[0] * s[1]
    return out


class _LateWeights:
    def __init__(self, cfg, tag, names, staged, after):
        self.cfg, self.tag, self.names, self.k = cfg, tag, names, 3 * len(names)
        self.send, self.recv, self.bufs, self.token = _split_start(f"gather_{tag}_chips_start", staged, self.k, _gather_to_chips,
                                                                    after)

    def pass_on(self, after):
        bufs = _split_wait(f"gather_{self.tag}_chips_wait", self.send, self.recv, self.bufs, after, _gather_to_chips)
        self.send, self.recv, self.bufs, token = _split_start(f"gather_{self.tag}_sibling_start", bufs, self.k, _gather_to_sibling,
                                                               self.token)
        return token

    def arrived(self, after):
        bufs = _split_wait(f"gather_{self.tag}_sibling_wait", self.send, self.recv, self.bufs, after, _gather_to_sibling)
        return {n: _gathered_to_kernel(self.cfg, n, b) for n, b in zip(self.names, bufs)}


def _step(cfg, a):
    chip = 2 * lax.axis_index("x") + lax.axis_index("y")
    core = lax.axis_index("c")
    big = BIG

    ffn = ("w_gate", "w_up", "w_down")
    first = ("w_in", "w_uq", "w_ukv")
    sp = {n: a[n] for n in SMALL}
    sharded = _pack_small([a[n] for n in SMALL_SHARDED], 2 * LANE)
    slabs = jnp.where(lax.broadcasted_iota(I32, (N_CHIPS,) + sharded.shape, 0) == chip, sharded[None], 0.0)
    staged = {"w_in": _stage_shard("w_in", a["w_in"], chip)}
    in_weight = _LateWeights(cfg, "in", ("w_in", "sharded_small"), [staged["w_in"], slabs], jnp.zeros((8, LANE), F32))
    behind = in_weight.token
    for n in big[1:]:
        behind = staged[n] = _stage_shard(n, a[n], chip, behind)
    in_weight.pass_on(behind)
    xn_early = _rms_pre(cfg, a["x"], sp["mix_pre_g"] + in_weight.token[0, 0])
    W = in_weight.arrived(xn_early)
    allp = W.pop("sharded_small").reshape((N_CHIPS,) + sharded.shape)
    per_chip = [_unpack_small(allp[ch], [a[n].shape for n in SMALL_SHARDED]) for ch in range(N_CHIPS)]
    for k, n in enumerate(SMALL_SHARDED):
        sp[n] = jnp.concatenate([per_chip[ch][k] for ch in range(N_CHIPS)], axis=1)

    mla_weights = _LateWeights(cfg, "mla", first[1:], [staged[n] for n in first[1:]], W["w_in"])
    out_weight = _LateWeights(cfg, "out", ("w_out",), [staged["w_out"]], mla_weights.token)
    ffn_weights = _LateWeights(cfg, "ffn", ffn[:2], [staged[n] for n in ffn[:2]], out_weight.token)
    down_weight = _LateWeights(cfg, "down", ffn[2:], [staged[n] for n in ffn[2:]], ffn_weights.token)

    state = {}

    def ffn_grads_ready(grads):
        state["ffn_pairs"], token = _pair_exchange_start("ffn", [_grad_to_chips(cfg, n, grads[n]) for n in ffn_grads])
        return token

    def pair_sums(names, grads, theirs):
        return [_pair_sum(n, g, t, core) for n, g, t in zip(names, grads, theirs)]

    def early_grads_ready(grads):
        g_out = [_grad_to_chips(cfg, "w_out", grads["w_out"])]
        g_ffn, t_ffn = _pair_exchange_wait("ffn", state["ffn_pairs"], g_out[0])
        sums = pair_sums(ffn_grads, g_ffn, t_ffn) + pair_sums(["w_out"], g_out, _pair_exchange("out", g_out))
        state["early"] = _chip_exchange_start("early", sums)
        return state["early"][-1]

    def reduced_halves(tag, names, after):
        send_sems, recv_sems, s_bufs, l_bufs, _ = state[tag]
        s_bufs, l_bufs = _chip_exchange_wait(tag, send_sems, recv_sems, s_bufs, l_bufs, after)
        return [_chip_sum(n, s, t, chip) for n, s, t in zip(names, s_bufs, l_bufs)]

    def in_grad_ready(grads):
        grads = [_grad_to_chips(cfg, n, grads[n]) for n in first]
        state["rest"] = _chip_exchange_start("rest", pair_sums(first, grads, _pair_exchange("rest", grads)))
        return state["rest"][-1]

    ffn_grads = ("w_down", "w_gate", "w_up")
    early = ffn_grads + ("w_out",)
    loss, grad_x, gW, gs = _local_grads(cfg, a["x"], a["loss_target"], W, sp, mla_weights, out_weight, ffn_weights, down_weight,
                                        ffn_grads_ready, early_grads_ready, in_grad_ready, xn_early, down_weight.token)
    out = {"grad_x": grad_x}

    def adamw(names, mine, theirs):
        for n, gm, gt in zip(names, mine, theirs):
            out["grad_" + n], out["delta_" + n], out["new_m_" + n], out["new_v_" + n] = _adamw_halves(
                "adamw_" + n, a[n], gm, gt, a["m_" + n], a["v_" + n], core)

    mine = reduced_halves("early", early, grad_x)
    theirs = _sibling_exchange("early", mine[:1])
    later, _ = _sibling_exchange_start("early", mine[1:], theirs[0])
    adamw(early[:1], mine[:1], theirs)
    e_mine, e_theirs = _sibling_exchange_wait("early", later, out["new_v_" + early[0]])
    adamw(early[3:], e_mine[2:], e_theirs[2:])
    mine = reduced_halves("rest", first, out["new_v_" + early[-1]])
    rest, token = _sibling_exchange_start("rest", mine, mine[0])
    small = _allreduce_small_start(_pack_small([gs[n] for n in SMALL] + [loss]), token)
    adamw(early[1:3], e_mine[:2], e_theirs[:2])
    adamw(first, *_sibling_exchange_wait("rest", rest, out["new_v_" + early[2]]))
    shapes = [gs[n].shape for n in SMALL] + [(1, LANE)]
    red = _unpack_small(_allreduce_small_wait(small, chip, core, out["new_v_" + first[-1]]), shapes)
    g_small = dict(zip(SMALL, red[:-1]))
    for n in SMALL_SHARDED:
        cs = a[n].shape[1]
        g_small[n] = lax.dynamic_slice_in_dim(g_small[n], chip * cs, cs, axis=1)
    out["loss"] = red[-1][0, 0]
    sshapes = [a[n].shape for n in SMALL]
    d, nm, nv = _adamw("adamw_small", _pack_small([a[n] for n in SMALL]), _pack_small([g_small[n] for n in SMALL]),
                       _pack_small([a["m_" + n] for n in SMALL]), _pack_small([a["v_" + n] for n in SMALL]))
    for n, dd, mm, vv in zip(SMALL, _unpack_small(d, sshapes), _unpack_small(nm, sshapes), _unpack_small(nv, sshapes)):
        out["grad_" + n], out["delta_" + n], out["new_m_" + n], out["new_v_" + n] = g_small[n], dd, mm, vv
    return out


def kernel(x, mix_pre_g, w_in, q_norm_g, w_uq, kv_norm_g, w_ukv, ssm_conv_w, ssm_conv_b, dt_bias, a_log, d_skip, ssm_norm_g, w_out, mix_post_g, ffn_pre_g, w_gate, w_up, ffn_conv_w, ffn_conv_b, w_down, ffn_post_g, loss_target, m_mix_pre_g, m_w_in, m_q_norm_g, m_w_uq, m_kv_norm_g, m_w_ukv, m_ssm_conv_w, m_ssm_conv_b, m_dt_bias, m_a_log, m_d_skip, m_ssm_norm_g, m_w_out, m_mix_post_g, m_ffn_pre_g, m_w_gate, m_w_up, m_ffn_conv_w, m_ffn_conv_b, m_w_down, m_ffn_post_g, v_mix_pre_g, v_w_in, v_q_norm_g, v_w_uq, v_kv_norm_g, v_w_ukv, v_ssm_conv_w, v_ssm_conv_b, v_dt_bias, v_a_log, v_d_skip, v_ssm_norm_g, v_w_out, v_mix_post_g, v_ffn_pre_g, v_w_gate, v_w_up, v_ffn_conv_w, v_ffn_conv_b, v_w_down, v_ffn_post_g):
    args = dict(locals())
    def given(k, v):
        if k in ("w_in", "m_w_in", "v_w_in"):
            return jnp.transpose(v, (2, 0, 1))
        return v if k.removeprefix("m_").removeprefix("v_") in BIG or v.ndim < 3 else v[0]

    out = _step(_FULL, {k: given(k, v) for k, v in args.items()})
    res = [out["loss"], out["grad_x"][None]]
    for pre in ("grad_", "delta_", "new_m_", "new_v_"):
        for n in WEIGHTS:
            o = out[pre + n]
            res.append(jnp.transpose(o, (1, 2, 0)) if n == "w_in" else o if n in BIG or args[n].ndim < 3 else o[None])
    return tuple(res)
```

```python
import math

import jax
import jax.numpy as jnp
from jax import lax
from jax.experimental import pallas as pl
from jax.experimental.pallas import tpu as pltpu

F32, BF16, I32 = jnp.float32, jnp.bfloat16, jnp.int32
NN = (((1,), (0,)), ((), ()))
NT = (((1,), (1,)), ((), ()))
TN = (((0,), (0,)), ((), ()))
HI = lax.Precision.HIGHEST
MESH_ID = pl.DeviceIdType.MESH

EPS = 1e-6
CHUNK = 64
NOPE, ROPE, VH = 128, 64, 128
ROPE_THETA = 10000.0
HP, NST = 64, 128
SSM_K, FFN_K = 4, 3
LANE = 128
N_CHIPS = 4
VMEM_LIMIT = 52 * 1024 * 1024
MM_TILE, MM_TILE_K = 1408, 2816

ADAM_LR, ADAM_B1, ADAM_B2, ADAM_EPS, ADAM_WD, ADAM_STEP = 0.001, 0.9, 0.999, 1e-08, 0.01, 10


class _Cfg:
    def __init__(self, S, D, QL, KVL, H, HS, G, DFF, T):
        self.S, self.D, self.QL, self.KVL, self.H, self.HS, self.G, self.DFF, self.T = S, D, QL, KVL, H, HS, G, DFF, T
        self.INNER = HS * HP
        self.CONVCH = self.INNER + 2 * G * NST
        self.QW = H * (NOPE + ROPE)
        self.KVW = H * (NOPE + VH)
        self.MLAW = H * VH
        self.MIXW = self.MLAW + self.INNER
        self.IN_COLS = QL + KVL + ROPE + self.INNER + self.CONVCH + HS
        natural, at = {}, 0
        for name, w in (("c_q", QL), ("c_kv", KVL), ("kr", ROPE), ("z", self.INNER), ("xbc", self.CONVCH), ("dt", HS)):
            natural[name] = (at, w)
            at += w
        self.seg, taken = {}, []
        for name in sorted(natural, key=lambda n: -natural[n][1]):
            w = -(-natural[name][1] // LANE) * LANE
            off = next(o for o in range(0, self.IN_COLS * 2, w) if all(o + w <= t or o >= t + tw for t, tw in taken))
            taken.append((off, w))
            self.seg[name] = (off, w) + natural[name]
        self.EXT = max(o + w for o, w in taken)
        self.NPAIR = HS // 2
        self.REP = HS // G

    def window(self, name):
        off, w, _, _ = self.seg[name]
        return w, off // w


_FULL = _Cfg(S=2048, D=2048, QL=768, KVL=512, H=8, HS=16, G=2, DFF=5632, T=256)
BIG = ("w_in", "w_uq", "w_ukv", "w_out", "w_gate", "w_up", "w_down")

SMALL = ("mix_pre_g", "q_norm_g", "kv_norm_g", "ssm_conv_w", "ssm_conv_b", "dt_bias", "a_log", "d_skip", "ssm_norm_g",
         "mix_post_g", "ffn_pre_g", "ffn_conv_w", "ffn_conv_b", "ffn_post_g")
SMALL_SHARDED = ("ssm_conv_w", "ffn_conv_w")
WEIGHTS = ("mix_pre_g", "w_in", "q_norm_g", "w_uq", "kv_norm_g", "w_ukv", "ssm_conv_w", "ssm_conv_b", "dt_bias", "a_log",
           "d_skip", "ssm_norm_g", "w_out", "mix_post_g", "ffn_pre_g", "w_gate", "w_up", "ffn_conv_w", "ffn_conv_b",
           "w_down", "ffn_post_g")


def _pick(n, target, mult):
    best = None
    for d in range(mult, min(n, target) + 1, mult):
        if n % d == 0:
            best = d
    return best if best is not None else n


def _params(sem=None):
    kw = dict(vmem_limit_bytes=VMEM_LIMIT)
    if sem is not None:
        kw["dimension_semantics"] = sem
    return pltpu.CompilerParams(**kw)


def _dot(a, b, dims=NN, precision=None):
    return lax.dot_general(a, b, dims, preferred_element_type=F32, precision=precision)


def _sigmoid(x):
    return 1.0 / (1.0 + jnp.exp(-x))


def _rs(x):
    return lax.rsqrt(jnp.mean(x * x, axis=-1, keepdims=True) + EPS)


def _rms_back(xh, r, dn):
    return r * (dn - xh * jnp.mean(dn * xh, axis=-1, keepdims=True))


def _colsum(v):
    return jnp.sum(v, axis=0, keepdims=True)


def _matmul(name, a, b, mode, out_dtype, a2=None, b2=None, chips=False, after=None):
    cs = None
    if mode == "nn":
        (M, K), N = a.shape, b.shape[-1]
        if chips:
            cs, N = N, N_CHIPS * N
    elif mode == "nt":
        (M, K), N = a.shape, b.shape[-2]
        if chips:
            cs = b.shape[-1]
    else:
        (K, M), N = a.shape, b.shape[1]
        if chips:
            cs = N // N_CHIPS
    tm = _pick(M, MM_TILE, LANE)
    tn = _pick(cs if chips and mode != "nt" else N, MM_TILE, LANE)
    tk = _pick(cs, MM_TILE, LANE) if chips and mode == "nt" else _pick(K, MM_TILE_K, LANE)
    nk = K // tk
    dims = {"nn": NN, "nt": NT, "tn": TN}[mode]
    a_spec = pl.BlockSpec((tk, tm), lambda i, j, k: (k, i)) if mode == "tn" else pl.BlockSpec((tm, tk), lambda i, j, k: (i, k))
    b_spec = pl.BlockSpec((tn, tk), lambda i, j, k: (j, k)) if mode == "nt" else pl.BlockSpec((tk, tn), lambda i, j, k: (k, j))
    o_spec = pl.BlockSpec((tm, tn), lambda i, j, k: (i, j))
    o_shape = (M, N)
    if chips and mode == "nn":
        per = cs // tn
        b_spec = pl.BlockSpec((None, tk, tn), lambda i, j, k: (j // per, k, j % per))
    elif chips and mode == "nt":
        per = cs // tk
        b_spec = pl.BlockSpec((None, tn, tk), lambda i, j, k: (k // per, j, k % per))
    elif chips:
        per = cs // tn
        o_spec = pl.BlockSpec((None, tm, tn), lambda i, j, k: (j // per, i, j % per))
        o_shape = (N_CHIPS, M, cs)
    two = a2 is not None

    def product(refs):
        part = _dot(refs[0][...].astype(BF16), refs[1][...].astype(BF16), dims)
        if two:
            part += _dot(refs[2][...].astype(BF16), refs[3][...].astype(BF16), dims)
        return part

    def body_whole_k(*refs):
        refs[-1][...] = product(refs).astype(refs[-1].dtype)

    def body(*refs):
        o_ref, acc_ref = refs[-2], refs[-1]
        k = pl.program_id(2)

        @pl.when(k == 0)
        def _():
            acc_ref[...] = product(refs)

        @pl.when(k > 0)
        def _():
            acc_ref[...] += product(refs)

        @pl.when(k == nk - 1)
        def _():
            o_ref[...] = acc_ref[...].astype(o_ref.dtype)

    ins = ((a, b, a2, b2) if two else (a, b)) + (() if after is None else (after,))
    return pl.pallas_call(
        body_whole_k if nk == 1 else body, name=name, grid=(M // tm, N // tn, nk),
        in_specs=[a_spec, b_spec] * (2 if two else 1) + ([] if after is None else [pl.BlockSpec(memory_space=pl.ANY)]),
        out_specs=o_spec,
        out_shape=jax.ShapeDtypeStruct(o_shape, out_dtype),
        scratch_shapes=[] if nk == 1 else [pltpu.VMEM((tm, tn), F32)],
        compiler_params=_params(("parallel", "parallel", "arbitrary")),
    )(*ins)


def _matmul_twin(name, a, b1, b2, mode, out_dtype):
    if mode == "nn":
        (M, K), cs = a.shape, b1.shape[-1]
        tm = _pick(M, MM_TILE // 2, LANE)
    else:
        (K, M), cs = a.shape, b1.shape[1] // N_CHIPS
        tm = _pick(M, MM_TILE, LANE)
    tn = _pick(cs, MM_TILE, LANE)
    per = cs // tn
    dims = NN if mode == "nn" else TN

    def body(a_ref, b1_ref, b2_ref, o1_ref, o2_ref):
        lhs = a_ref[...].astype(BF16)
        o1_ref[...] = _dot(lhs, b1_ref[...].astype(BF16), dims).astype(o1_ref.dtype)
        o2_ref[...] = _dot(lhs, b2_ref[...].astype(BF16), dims).astype(o2_ref.dtype)

    if mode == "nn":
        a_spec = pl.BlockSpec((tm, K), lambda i, j: (i, 0))
        b_spec = pl.BlockSpec((None, K, tn), lambda i, j: (j // per, 0, j % per))
        o_spec, o_shape = pl.BlockSpec((tm, tn), lambda i, j: (i, j)), (M, N_CHIPS * cs)
    else:
        a_spec = pl.BlockSpec((K, tm), lambda i, j: (0, i))
        b_spec = pl.BlockSpec((K, tn), lambda i, j: (0, j))
        o_spec, o_shape = pl.BlockSpec((None, tm, tn), lambda i, j: (j // per, i, j % per)), (N_CHIPS, M, cs)
    return pl.pallas_call(
        body, name=name, grid=(M // tm, N_CHIPS * per), in_specs=[a_spec, b_spec, b_spec], out_specs=[o_spec, o_spec],
        out_shape=[jax.ShapeDtypeStruct(o_shape, out_dtype)] * 2, compiler_params=_params(("parallel", "parallel")),
    )(a, b1, b2)


def _window(a):
    return (a[0], *a[1]) if isinstance(a, tuple) else (a, a.shape[1], 0)


def _rowwise(name, fn, rows, mats, outs, reds, ts):
    rows, widths, blocks = zip(*[_window(a) for a in rows])
    S = rows[0].shape[0]
    nr, nm, no = len(rows), len(mats), len(outs)

    def body(*refs):
        res = fn(*[r[...] for r in refs[:nr + nm]])
        res = res if isinstance(res, (tuple, list)) else (res,)
        for r, v in zip(refs[nr + nm:nr + nm + no], res[:no]):
            r[...] = v.astype(r.dtype)
        first = pl.program_id(0) == 0
        for r, v in zip(refs[nr + nm + no:], res[no:]):
            @pl.when(first)
            def _():
                r[...] = jnp.broadcast_to(v, r.shape)

            @pl.when(jnp.logical_not(first))
            def _():
                r[...] += jnp.broadcast_to(v, r.shape)

    in_specs = [pl.BlockSpec((ts, w), lambda i, b=b: (i, b)) for w, b in zip(widths, blocks)]
    in_specs += [pl.BlockSpec(m.shape, lambda i, nd=m.ndim: (0,) * nd) for m in mats]
    out_specs = [pl.BlockSpec((ts, w), lambda i: (i, 0)) for w, _ in outs]
    out_specs += [pl.BlockSpec(s, lambda i: (0, 0)) for s in reds]
    out_shape = [jax.ShapeDtypeStruct((S, w), dt) for w, dt in outs] + [jax.ShapeDtypeStruct(s, F32) for s in reds]
    return pl.pallas_call(
        body, name=name, grid=(S // ts,), in_specs=in_specs, out_specs=out_specs, out_shape=out_shape,
        compiler_params=_params(("arbitrary",) if reds else ("parallel",)),
    )(*rows, *mats)


def _shift_down(v, s):
    if s == 0:
        return v
    rows = lax.broadcasted_iota(I32, v.shape, 0)
    return jnp.where(rows >= s, pltpu.roll(v, s, 0), 0.0)


def _shift_up(v, s):
    if s == 0:
        return v
    n = v.shape[0]
    rows = lax.broadcasted_iota(I32, v.shape, 0)
    return jnp.where(rows < n - s, pltpu.roll(v, n - s, 0), 0.0)


def _conv(x, w, b):
    K = w.shape[0]
    y = jnp.broadcast_to(b, x.shape)
    for k in range(K):
        y = y + w[k:k + 1, :] * _shift_down(x, K - 1 - k)
    return y


def _conv_back(x, w, dc):
    K = w.shape[0]
    dx = jnp.zeros_like(x)
    dw = []
    for k in range(K):
        up = _shift_up(dc, K - 1 - k)
        dx = dx + w[k:k + 1, :] * up
        dw.append(_colsum(up * x))
    return dx, jnp.concatenate(dw, axis=0), _colsum(dc)


def _colwise(name, fn, cols, vecs, outs, pouts, tc):
    cols, widths, blocks = zip(*[_window(a) for a in cols])
    S, C = cols[0].shape[0], widths[0]
    firsts = [b * (C // tc) for b in blocks]
    nc_, nv, no = len(cols), len(vecs), len(outs)

    def body(*refs):
        res = fn(*[r[...] for r in refs[:nc_ + nv]])
        res = res if isinstance(res, (tuple, list)) else (res,)
        for r, v in zip(refs[nc_ + nv:], res):
            r[...] = v.astype(r.dtype)

    in_specs = [pl.BlockSpec((S, tc), lambda j, f=f: (0, f + j)) for f in firsts]
    in_specs += [pl.BlockSpec((v.shape[0], tc), lambda j: (0, j)) for v in vecs]
    out_specs = [pl.BlockSpec((S, tc), lambda j: (0, j)) for _ in outs] + [pl.BlockSpec((k, tc), lambda j: (0, j)) for k in pouts]
    out_shape = [jax.ShapeDtypeStruct((S, C), dt) for dt in outs] + [jax.ShapeDtypeStruct((k, C), F32) for k in pouts]
    return pl.pallas_call(
        body, name=name, grid=(C // tc,), in_specs=in_specs, out_specs=out_specs, out_shape=out_shape,
        compiler_params=_params(("parallel",)),
    )(*cols, *vecs)


_G0, _G1 = math.sqrt(2.0 / math.pi), 0.044715


def _gelu(g):
    th = jnp.tanh(_G0 * (g + _G1 * g * g * g))
    return 0.5 * g * (1.0 + th), th


def _ffn_act(gate_pre, up, w, b):
    act, _ = _gelu(_conv(gate_pre, w, b))
    return act * up


def _ffn_act_back(dact, gate_pre, up, w, b):
    g = _conv(gate_pre, w, b)
    ge, th = _gelu(g)
    dge = 0.5 * (1.0 + th) + 0.5 * g * (1.0 - th * th) * _G0 * (1.0 + 3.0 * _G1 * g * g)
    dup = dact * ge
    dgate_pre, dw, db = _conv_back(gate_pre, w, dact * up * dge)
    return dgate_pre, dup, dw, db


def _ssm_act(xbc, w, b):
    c = _conv(xbc, w, b)
    return c * _sigmoid(c)


def _ssm_act_back(dxc, xbc, w, b):
    c = _conv(xbc, w, b)
    sg = _sigmoid(c)
    return _conv_back(xbc, w, dxc * sg * (1.0 + c * (1.0 - sg)))


def _rope_tables(S):
    inv = 1.0 / (ROPE_THETA ** (jnp.arange(0, ROPE, 2, dtype=F32) / ROPE))
    ang = jnp.arange(S, dtype=F32)[:, None] * inv[None, :]
    cos, sin = jnp.cos(ang), jnp.sin(ang)
    return jnp.tile(cos, (1, 4)), jnp.tile(jnp.concatenate([-sin, sin], axis=1), (1, 2))


def _swap_halves(x):
    lane = lax.broadcasted_iota(I32, x.shape, 1)
    w = x.shape[1]
    return jnp.where((lane % ROPE) < ROPE // 2, pltpu.roll(x, w - ROPE // 2, 1), pltpu.roll(x, ROPE // 2, 1))


def _rot(x, cos2, sin2):
    return x * cos2 + _swap_halves(x) * sin2


def _rot_back(dy, cos2, sin2):
    return dy * cos2 + _swap_halves(dy * sin2)


def _mla_pack(cfg, q, kv, kr, cos2, sin2):
    S, H = cfg.S, cfg.H
    ts = _pick(S, 256, 8)
    kr, _, kr_block = _window(kr)

    def body(q_ref, kv_ref, kr_ref, c_ref, s_ref, Q_ref, K_ref, V_ref):
        c2, s2 = c_ref[...], s_ref[...]
        krr = _rot(kr_ref[...], c2, s2)
        kr_half = (krr.astype(BF16), pltpu.roll(krr, ROPE, 1).astype(BF16))
        for j in range(H // 2):
            qr = _rot(q_ref[:, (H + j) * LANE:(H + j + 1) * LANE], c2, s2).astype(BF16)
            for h in (2 * j, 2 * j + 1):
                Q_ref[h, :, 0:LANE] = q_ref[:, h * LANE:(h + 1) * LANE].astype(BF16)
                Q_ref[h, :, LANE:] = qr
                K_ref[h, :, 0:LANE] = kv_ref[:, h * LANE:(h + 1) * LANE].astype(BF16)
                K_ref[h, :, LANE:] = kr_half[h % 2]
                V_ref[h] = kv_ref[:, (H + h) * LANE:(H + h + 1) * LANE].astype(BF16)

    tab = pl.BlockSpec((ts, LANE), lambda i: (i, 0))
    heads = lambda w: pl.BlockSpec((H, ts, w), lambda i: (0, i, 0))
    return pl.pallas_call(
        body, name="mla_pack", grid=(S // ts,),
        in_specs=[pl.BlockSpec((ts, cfg.QW), lambda i: (i, 0)), pl.BlockSpec((ts, cfg.KVW), lambda i: (i, 0)),
                  pl.BlockSpec((ts, LANE), lambda i: (i, kr_block)), tab, tab],
        out_specs=[heads(2 * LANE), heads(2 * LANE), heads(LANE)],
        out_shape=[jax.ShapeDtypeStruct((H, S, 2 * LANE), BF16), jax.ShapeDtypeStruct((H, S, 2 * LANE), BF16),
                   jax.ShapeDtypeStruct((H, S, LANE), BF16)],
        compiler_params=_params(("parallel",)),
    )(q, kv, kr, cos2, sin2)


def _mla_unpack(cfg, dQ, dK, dV, cos2, sin2):
    S, H = cfg.S, cfg.H
    ts = _pick(S, 256, 8)

    def body(dQ_ref, dK_ref, dV_ref, c_ref, s_ref, dq_ref, dkv_ref, dkr_ref):
        c2, s2 = c_ref[...], s_ref[...]
        lo = lax.broadcasted_iota(I32, (ts, LANE), 1) < ROPE
        tk = jnp.zeros((ts, LANE), F32)
        for h in range(H):
            dq_ref[:, h * LANE:(h + 1) * LANE] = dQ_ref[h, :, 0:LANE].astype(BF16)
            dkv_ref[:, h * LANE:(h + 1) * LANE] = dK_ref[h, :, 0:LANE].astype(BF16)
            dkv_ref[:, (H + h) * LANE:(H + h + 1) * LANE] = dV_ref[h].astype(BF16)
            own = lo if h % 2 == 0 else jnp.logical_not(lo)
            tk = tk + jnp.where(own, dK_ref[h, :, LANE:], 0.0)
        for j in range(H // 2):
            dr = dQ_ref[2 * j, :, LANE:] + dQ_ref[2 * j + 1, :, LANE:]
            dq_ref[:, (H + j) * LANE:(H + j + 1) * LANE] = _rot_back(dr, c2, s2).astype(BF16)
        dkr_rot = jnp.where(lo, tk + pltpu.roll(tk, ROPE, 1), 0.0)
        dkr_ref[...] = _rot_back(dkr_rot, c2, s2).astype(BF16)

    tab = pl.BlockSpec((ts, LANE), lambda i: (i, 0))
    return pl.pallas_call(
        body, name="mla_unpack", grid=(S // ts,),
        in_specs=[pl.BlockSpec((H, ts, 2 * LANE), lambda i: (0, i, 0)), pl.BlockSpec((H, ts, 2 * LANE), lambda i: (0, i, 0)),
                  pl.BlockSpec((H, ts, LANE), lambda i: (0, i, 0)), tab, tab],
        out_specs=[pl.BlockSpec((ts, cfg.QW), lambda i: (i, 0)), pl.BlockSpec((ts, cfg.KVW), lambda i: (i, 0)), tab],
        out_shape=[jax.ShapeDtypeStruct((S, cfg.QW), BF16), jax.ShapeDtypeStruct((S, cfg.KVW), BF16),
                   jax.ShapeDtypeStruct((S, LANE), BF16)],
        compiler_params=_params(("parallel",)),
    )(dQ, dK, dV, cos2, sin2)


_ATT_T = 256
_ATT_HB = 8
_ATT_SCALE = (NOPE + ROPE) ** -0.5


def _diag_mask(transposed=False):
    r = lax.broadcasted_iota(I32, (_ATT_T, _ATT_T), 0) // CHUNK
    c = lax.broadcasted_iota(I32, (_ATT_T, _ATT_T), 1) // CHUNK
    return r <= c if transposed else c <= r


def _row_form(col):
    return jnp.broadcast_to(col, (col.shape[0], LANE)).T[0:8, :]


def _attn_fwd(cfg, Q, K, V):
    S, H, T, HB = cfg.S, cfg.H, _ATT_T, min(cfg.H, _ATT_HB)

    def body(q_ref, k_ref, v_ref, o_ref, lse_t_ref):
        qi = pl.program_id(1)

        def head_step(b, kb, carry, mask):
            m, l, acc = carry
            ks = pl.multiple_of(kb * T, T)
            s = _dot(q_ref[b], k_ref[b, pl.ds(ks, T), :], NT) * _ATT_SCALE
            if mask is not None:
                s = jnp.where(mask, s, -1e30)
            m_new = jnp.maximum(m, jnp.max(s, axis=1, keepdims=True))
            p = jnp.exp(s - m_new)
            alpha = jnp.exp(m - m_new)
            l = alpha * l + jnp.sum(p, axis=1, keepdims=True)
            acc = alpha * acc + _dot(p.astype(BF16), v_ref[b, pl.ds(ks, T), :])
            return m_new, l, acc

        def step(kb, carry, mask=None):
            return tuple(head_step(b, kb, carry[b], mask) for b in range(HB))

        init = (jnp.full((T, 1), -1e30, F32), jnp.zeros((T, 1), F32), jnp.zeros((T, VH), F32))
        done = step(qi, lax.fori_loop(0, qi, step, (init,) * HB), _diag_mask())
        for b, (m, l, acc) in enumerate(done):
            o_ref[:, b * LANE:(b + 1) * LANE] = acc / l
            lse_t_ref[b] = _row_form(m + jnp.log(l))

    return pl.pallas_call(
        body, name="attn_fwd", grid=(H // HB, S // T),
        in_specs=[pl.BlockSpec((HB, T, 2 * LANE), lambda h, i: (h, i, 0)), pl.BlockSpec((HB, S, 2 * LANE), lambda h, i: (h, 0, 0)),
                  pl.BlockSpec((HB, S, LANE), lambda h, i: (h, 0, 0))],
        out_specs=[pl.BlockSpec((T, HB * LANE), lambda h, i: (i, h)), pl.BlockSpec((HB, 8, T), lambda h, i: (h, 0, i))],
        out_shape=[jax.ShapeDtypeStruct((S, H * LANE), F32), jax.ShapeDtypeStruct((H, 8, S), F32)],
        compiler_params=_params(("parallel", "parallel")),
    )(Q, K, V)


def _attn_delta(cfg, do, o, after):
    S, H, T = cfg.S, cfg.H, _ATT_T

    def body(do_ref, o_ref, after_ref, dl_t_ref):
        for h in range(H):
            sl = slice(h * LANE, (h + 1) * LANE)
            dl_t_ref[h] = _row_form(jnp.sum(do_ref[:, sl] * o_ref[:, sl], axis=1, keepdims=True))

    wide = pl.BlockSpec((T, H * LANE), lambda i: (i, 0))
    return pl.pallas_call(
        body, name="attn_delta", grid=(S // T,), in_specs=[wide, wide, _ANY],
        out_specs=pl.BlockSpec((H, 8, T), lambda i: (0, 0, i)), out_shape=jax.ShapeDtypeStruct((H, 8, S), F32),
        compiler_params=_params(("parallel",)),
    )(do, o, after)


_ATT_HB_BWD = 4


def _attn_bwd(cfg, Q, K, V, do, lse_t, delta_t):
    S, H, T, HB = cfg.S, cfg.H, _ATT_T, min(cfg.H, _ATT_HB_BWD)
    nq = S // T

    def body(q_ref, k_ref, v_ref, do_ref, lse_ref, dl_ref, dq_ref, dk_ref, dv_ref):
        kb = pl.program_id(1)

        @pl.when(kb == 0)
        def _():
            dq_ref[...] = jnp.zeros_like(dq_ref)

        def head_step(b, qi, carry, mask):
            dk, dv = carry
            qs = pl.multiple_of(qi * T, T)
            q = q_ref[b, pl.ds(qs, T), :]
            k = k_ref[b]
            dob = do_ref[pl.ds(qs, T), b * LANE:(b + 1) * LANE].astype(BF16)
            s = _dot(k, q, NT) * _ATT_SCALE
            if mask is not None:
                s = jnp.where(mask, s, -1e30)
            p = jnp.exp(s - lse_ref[b, 0:1, pl.ds(qs, T)])
            dv = dv + _dot(p.astype(BF16), dob)
            dp = _dot(v_ref[b], dob, NT)
            ds = (p * (dp - dl_ref[b, 0:1, pl.ds(qs, T)]) * _ATT_SCALE).astype(BF16)
            dk = dk + _dot(ds, q)
            dq_ref[b, pl.ds(qs, T), :] += _dot(ds, k, TN)
            return dk, dv

        def step(qi, carry, mask=None):
            return tuple(head_step(b, qi, carry[b], mask) for b in range(HB))

        zero = (jnp.zeros((T, 2 * LANE), F32), jnp.zeros((T, VH), F32))
        done = lax.fori_loop(kb + 1, nq, step, step(kb, (zero,) * HB, _diag_mask(transposed=True)))
        for b, (dk, dv) in enumerate(done):
            dk_ref[b] = dk
            dv_ref[b] = dv

    row = pl.BlockSpec((HB, 8, S), lambda h, j: (h, 0, 0))
    whole = pl.BlockSpec((HB, S, 2 * LANE), lambda h, j: (h, 0, 0))
    return pl.pallas_call(
        body, name="attn_bwd", grid=(H // HB, S // T),
        in_specs=[whole, pl.BlockSpec((HB, T, 2 * LANE), lambda h, j: (h, j, 0)), pl.BlockSpec((HB, T, LANE), lambda h, j: (h, j, 0)),
                  pl.BlockSpec((S, HB * LANE), lambda h, j: (0, h)), row, row],
        out_specs=[whole, pl.BlockSpec((HB, T, 2 * LANE), lambda h, j: (h, j, 0)), pl.BlockSpec((HB, T, LANE), lambda h, j: (h, j, 0))],
        out_shape=[jax.ShapeDtypeStruct((H, S, 2 * LANE), F32), jax.ShapeDtypeStruct((H, S, 2 * LANE), F32),
                   jax.ShapeDtypeStruct((H, S, LANE), F32)],
        compiler_params=_params(("parallel", "arbitrary")),
    )(Q, K, V, do, lse_t, delta_t)


def _expand_matrix(cfg):
    r = lax.broadcasted_iota(I32, (LANE, cfg.INNER), 0)
    c = lax.broadcasted_iota(I32, (LANE, cfg.INNER), 1)
    return (r == c // HP).astype(F32)


def _softplus(x):
    return jnp.maximum(x, 0.0) + jnp.log(1.0 + jnp.exp(-jnp.abs(x)))


def _ssd_prep(cfg, dt_raw, dt_bias_pad, a_log_pad, expand):
    HS = cfg.HS

    def fn(raw, bias, alog, E):
        heads = lax.broadcasted_iota(I32, raw.shape, 1) < HS
        dt = jnp.where(heads, _softplus(raw + bias), 0.0)
        a = dt * jnp.where(heads[0:1], -jnp.exp(alog), 0.0)
        return dt, a, _dot(dt, E, precision=HI)

    return _rowwise("ssd_prep", fn, [dt_raw], [dt_bias_pad, a_log_pad, expand],
                    [(LANE, F32), (LANE, F32), (cfg.INNER, F32)], [], _pick(cfg.S, 512, 8))


def _tril(T):
    return lax.broadcasted_iota(I32, (T, T), 0) >= lax.broadcasted_iota(I32, (T, T), 1)


def _ssd_fwd(cfg, xc, dt_exp, a_small, dskip_exp, expand):
    S, T, INNER, G, NPAIR = cfg.S, cfg.T, cfg.INNER, cfg.G, cfg.NPAIR
    NC = S // T

    def body(xc_ref, dte_ref, as_ref, dsk_ref, e_ref, y_ref, hin_ref, ht_ref):
        @pl.when(pl.program_id(0) == 0)
        def _():
            ht_ref[...] = jnp.zeros_like(ht_ref)

        tril = _tril(T)
        tri = tril.astype(F32)
        acs_s = _dot(tri, as_ref[...], precision=HI)
        acs_e = _dot(acs_s, e_ref[...], precision=HI)
        acs_t = acs_s.T
        lo = lax.broadcasted_iota(I32, (T, LANE), 1) < HP
        for g in range(G):
            Bb = xc_ref[:, INNER + g * NST:INNER + (g + 1) * NST].astype(BF16)
            Cb = xc_ref[:, INNER + (G + g) * NST:INNER + (G + g + 1) * NST].astype(BF16)
            Gm = _dot(Cb, Bb, NT)
            for j in range(g * NPAIR // G, (g + 1) * NPAIR // G):
                sl = slice(j * LANE, (j + 1) * LANE)
                Xp = xc_ref[:, sl]
                Xdt = Xp * dte_ref[:, sl]
                Xb = Xdt.astype(BF16)
                acs_p = acs_e[:, sl]
                last = acs_p[T - 1:T, :]
                Hin = ht_ref[j]
                hin_ref[0, j] = Hin
                yd = []
                for e in (0, 1):
                    h = 2 * j + e
                    Lm = jnp.exp(jnp.where(tril, acs_s[:, h:h + 1] - acs_t[h:h + 1, :], -1e30))
                    yd.append(_dot((Gm * Lm).astype(BF16), Xb))
                y_off = _dot(Cb, Hin.astype(BF16)) * jnp.exp(acs_p)
                y_ref[:, sl] = jnp.where(lo, yd[0], yd[1]) + y_off + Xp * dsk_ref[:, sl]
                st = _dot(Bb, (Xdt * jnp.exp(last - acs_p)).astype(BF16), TN)
                ht_ref[j] = jnp.exp(last) * Hin + st

    rows = lambda w: pl.BlockSpec((T, w), lambda c: (c, 0))
    return pl.pallas_call(
        body, name="ssd_fwd", grid=(NC,),
        in_specs=[rows(cfg.CONVCH), rows(INNER), rows(LANE), pl.BlockSpec((1, INNER), lambda c: (0, 0)),
                  pl.BlockSpec((LANE, INNER), lambda c: (0, 0))],
        out_specs=[rows(INNER), pl.BlockSpec((1, NPAIR, NST, LANE), lambda c: (c, 0, 0, 0))],
        out_shape=[jax.ShapeDtypeStruct((S, INNER), F32), jax.ShapeDtypeStruct((NC, NPAIR, NST, LANE), F32)],
        scratch_shapes=[pltpu.VMEM((NPAIR, NST, LANE), F32)],
        compiler_params=_params(("arbitrary",)),
    )(xc, dt_exp, a_small, dskip_exp, expand)


def _ssd_bwd(cfg, dy, xc, dt_exp, a_small, dskip_exp, hin, dt_raw, dt_bias_pad, a_log_pad, expand):
    S, T, INNER, G, NPAIR, HS = cfg.S, cfg.T, cfg.INNER, cfg.G, cfg.NPAIR, cfg.HS
    NC = S // T

    def body(dy_ref, xc_ref, dte_ref, as_ref, dsk_ref, hin_ref, raw_ref, bias_ref, alog_ref, e_ref,
             dxc_ref, draw_ref, dbias_ref, dalog_ref, dskip_ref, dht_ref, cols_ref, rows_ref, dacs_ref, ddt_ref):
        first = pl.program_id(0) == 0

        @pl.when(first)
        def _():
            dht_ref[...] = jnp.zeros_like(dht_ref)

        tril = _tril(T)
        tri = tril.astype(F32)
        a_s = as_ref[...]
        acs_s = _dot(tri, a_s, precision=HI)
        acs_e = _dot(acs_s, e_ref[...], precision=HI)
        acs_t = acs_s.T
        lo = lax.broadcasted_iota(I32, (T, LANE), 1) < HP
        last_row = lax.broadcasted_iota(I32, (T, LANE), 0) == T - 1
        cols_ref[...] = jnp.zeros_like(cols_ref)
        rows_ref[...] = jnp.zeros_like(rows_ref)
        dsk_parts = []
        for g in range(G):
            bsl = slice(INNER + g * NST, INNER + (g + 1) * NST)
            csl = slice(INNER + (G + g) * NST, INNER + (G + g + 1) * NST)
            Bb = xc_ref[:, bsl].astype(BF16)
            Cb = xc_ref[:, csl].astype(BF16)
            Gm = _dot(Cb, Bb, NT)
            dG = jnp.zeros((T, T), F32)
            dB = jnp.zeros((T, NST), F32)
            dC = jnp.zeros((T, NST), F32)
            for j in range(g * NPAIR // G, (g + 1) * NPAIR // G):
                sl = slice(j * LANE, (j + 1) * LANE)
                Xp = xc_ref[:, sl]
                dtp = dte_ref[:, sl]
                Xdt = Xp * dtp
                Xb = Xdt.astype(BF16)
                acs_p = acs_e[:, sl]
                last = acs_p[T - 1:T, :]
                e_p, dec, cd = jnp.exp(acs_p), jnp.exp(last - acs_p), jnp.exp(last)
                Hin = hin_ref[0, j]
                Hb = Hin.astype(BF16)
                dHn = dht_ref[j]
                dHb = dHn.astype(BF16)
                dYp = dy_ref[:, sl]
                z = _dot(Cb, Hb)
                dz = (dYp * e_p).astype(BF16)
                dacs_p = dYp * z * e_p
                dC = dC + _dot(dz, Hb, NT)
                dHin = _dot(Cb, dz, TN) + cd * dHn
                dlast = _colsum(dHn * Hin) * cd
                qv = _dot(Bb, dHb)
                dXdt = qv * dec
                ddec = qv * Xdt * dec
                dacs_p = dacs_p - ddec
                dlast = dlast + _colsum(ddec)
                dB = dB + _dot((Xdt * dec).astype(BF16), dHb, NT)
                for e in (0, 1):
                    h = 2 * j + e
                    Lm = jnp.exp(jnp.where(tril, acs_s[:, h:h + 1] - acs_t[h:h + 1, :], -1e30))
                    Mh = Gm * Lm
                    dYe = jnp.where(lo if e == 0 else jnp.logical_not(lo), dYp, 0.0).astype(BF16)
                    dM = _dot(dYe, Xb, NT)
                    dXdt = dXdt + _dot(Mh.astype(BF16), dYe, TN)
                    W = dM * Mh
                    cols_ref[:, h:h + 1] = jnp.sum(W, axis=1, keepdims=True)
                    rows_ref[h:h + 1, :] = _colsum(W)
                    dG = dG + dM * Lm
                dacs_ref[:, sl] = dacs_p + jnp.where(last_row, dlast, 0.0)
                ddt_ref[:, sl] = dXdt * Xp
                dxc_ref[:, sl] = dXdt * dtp + dYp * dsk_ref[:, sl]
                dsk_parts.append(_colsum(dYp * Xp))
                dht_ref[j] = dHin
            dGb = dG.astype(BF16)
            dxc_ref[:, bsl] = dB + _dot(dGb, Cb, TN)
            dxc_ref[:, csl] = dC + _dot(dGb, Bb)
        E = e_ref[...]
        dacs_s = cols_ref[...] - rows_ref[...].T + _dot(dacs_ref[...], E, NT, precision=HI)
        da = _dot(tri, dacs_s, TN, precision=HI)
        heads = lax.broadcasted_iota(I32, (1, LANE), 1) < HS
        A = jnp.where(heads, -jnp.exp(alog_ref[...]), 0.0)
        ddt = _dot(ddt_ref[...], E, NT, precision=HI) + da * A
        draw = jnp.where(heads, ddt * _sigmoid(raw_ref[...] + bias_ref[...]), 0.0)
        draw_ref[...] = draw
        dsk = _dot(jnp.broadcast_to(jnp.concatenate(dsk_parts, axis=1), (8, INNER)), E, NT, precision=HI)[0:1]
        for ref, val in ((dbias_ref, _colsum(draw)), (dalog_ref, _colsum(da * a_s)), (dskip_ref, dsk)):
            @pl.when(first)
            def _():
                ref[...] = val

            @pl.when(jnp.logical_not(first))
            def _():
                ref[...] += val

    dt_raw, _, raw_block = _window(dt_raw)
    rows = lambda w, b=0: pl.BlockSpec((T, w), lambda c: (NC - 1 - c, b))
    vec = lambda w: pl.BlockSpec((1, w), lambda c: (0, 0))
    return pl.pallas_call(
        body, name="ssd_bwd", grid=(NC,),
        in_specs=[rows(INNER), rows(cfg.CONVCH), rows(INNER), rows(LANE), vec(INNER),
                  pl.BlockSpec((1, NPAIR, NST, LANE), lambda c: (NC - 1 - c, 0, 0, 0)), rows(LANE, raw_block), vec(LANE), vec(LANE),
                  pl.BlockSpec((LANE, INNER), lambda c: (0, 0))],
        out_specs=[rows(cfg.CONVCH), rows(LANE), vec(LANE), vec(LANE), vec(LANE)],
        out_shape=[jax.ShapeDtypeStruct((S, cfg.CONVCH), F32), jax.ShapeDtypeStruct((S, LANE), F32)]
        + [jax.ShapeDtypeStruct((1, LANE), F32)] * 3,
        scratch_shapes=[pltpu.VMEM((NPAIR, NST, LANE), F32), pltpu.VMEM((T, LANE), F32), pltpu.VMEM((LANE, T), F32),
                        pltpu.VMEM((T, INNER), F32), pltpu.VMEM((T, INNER), F32)],
        compiler_params=_params(("arbitrary",)),
    )(dy, xc, dt_exp, a_small, dskip_exp, hin, dt_raw, dt_bias_pad, a_log_pad, expand)


def _ssd_post(cfg, y, z, norm_g):
    W = cfg.INNER // cfg.G

    def fn(y, z, g):
        yz = y * z * _sigmoid(z)
        return jnp.concatenate([yz[:, i * W:(i + 1) * W] * _rs(yz[:, i * W:(i + 1) * W]) for i in range(cfg.G)], axis=1) * g

    return _rowwise("ssd_post", fn, [y, z], [norm_g], [(cfg.INNER, BF16)], [], _pick(cfg.S, 256, 8))[0]


def _ssd_post_bwd(cfg, db, y, z, norm_g):
    W = cfg.INNER // cfg.G

    def fn(db, y, z, g):
        sg = _sigmoid(z)
        yz = y * z * sg
        dn = db * g
        dyz, nh = [], []
        for i in range(cfg.G):
            seg = yz[:, i * W:(i + 1) * W]
            r = _rs(seg)
            nh.append(seg * r)
            dyz.append(_rms_back(nh[-1], r, dn[:, i * W:(i + 1) * W]))
        dyz = jnp.concatenate(dyz, axis=1)
        return dyz * z * sg, dyz * y * sg * (1.0 + z * (1.0 - sg)), _colsum(db * jnp.concatenate(nh, axis=1))

    return _rowwise("ssd_post_bwd", fn, [db, y, z], [norm_g], [(cfg.INNER, F32), (cfg.INNER, F32)], [(1, cfg.INNER)],
                    _pick(cfg.S, 256, 8))


def _rms_pre(cfg, x, g):
    return _rowwise("rms_pre", lambda x, g: x * _rs(x) * g, [x], [g], [(cfg.D, BF16)], [], _pick(cfg.S, 256, 8))[0]


def _local_grads(cfg, x, tgt, W, sp, mla_weights=None, out_weight=None, ffn_weights=None, down_weight=None,
                 ffn_grads_ready=None, early_grads_ready=None, in_grad_ready=None, xn=None, after_in=None):
    S, D, H, INNER = cfg.S, cfg.D, cfg.H, cfg.INNER
    ts = _pick(S, 256, 8)
    tc = _CONV_COLS

    if xn is None:
        xn = _rms_pre(cfg, x, sp["mix_pre_g"])
    u = _matmul("mm_in", xn, W["w_in"], "nt", F32, after=after_in)
    c_q, c_kv, kr, z, xbc, dt_raw = [(u, cfg.window(n)) for n in ("c_q", "c_kv", "kr", "z", "xbc", "dt")]

    if mla_weights is not None:
        sp = dict(sp, q_norm_g=sp["q_norm_g"] + mla_weights.pass_on(u)[0, 0])
    cqn = _rowwise("rms_q", lambda x, g: x * _rs(x) * g, [c_q], [sp["q_norm_g"]], [(cfg.QL, BF16)], [], ts)[0]
    ckvn = _rowwise("rms_kv", lambda x, g: x * _rs(x) * g, [c_kv], [sp["kv_norm_g"]], [(cfg.KVL, BF16)], [], ts)[0]
    if mla_weights is not None:
        W = dict(W, **mla_weights.arrived(ckvn))
    q = _matmul("mm_uq", cqn, W["w_uq"], "nn", F32)
    kv = _matmul("mm_ukv", ckvn, W["w_ukv"], "nn", F32)
    cos2, sin2 = _rope_tables(S)
    Qh, Kh, Vh = _mla_pack(cfg, q, kv, kr, cos2, sin2)
    a_out, lse_t = _attn_fwd(cfg, Qh, Kh, Vh)
    if out_weight is not None:
        sp = dict(sp, ssm_conv_b=sp["ssm_conv_b"] + out_weight.pass_on(a_out)[0, 0])

    pad = lambda v: jnp.pad(v, ((0, 0), (0, LANE - v.shape[1])))
    expand = _expand_matrix(cfg)
    dt_bias_pad, a_log_pad = pad(sp["dt_bias"]), pad(sp["a_log"])
    dskip_exp = jnp.repeat(sp["d_skip"], HP, axis=1)
    xc = _colwise("ssm_act", _ssm_act, [xbc], [sp["ssm_conv_w"], sp["ssm_conv_b"]], [F32], [], tc)[0]
    dt_s, a_s, dt_exp = _ssd_prep(cfg, dt_raw, dt_bias_pad, a_log_pad, expand)
    y_ssd, hin = _ssd_fwd(cfg, xc, dt_exp, a_s, dskip_exp, expand)
    b_out = _ssd_post(cfg, y_ssd, z, sp["ssm_norm_g"])

    ab_out = jnp.concatenate([a_out.astype(BF16), b_out], axis=1)
    if out_weight is not None:
        W = dict(W, **out_weight.arrived(ab_out))
    if ffn_weights is not None:
        sp = dict(sp, mix_post_g=sp["mix_post_g"] + ffn_weights.pass_on(ab_out)[0, 0])
    mix = _matmul("mm_out", ab_out, W["w_out"], "nn", F32)

    def mid(x, mix, g_mp, g_fp):
        x1 = x + mix * _rs(mix) * g_mp
        return x1, x1 * _rs(x1) * g_fp

    x1, h2 = _rowwise("fwd_mid", mid, [x, mix], [sp["mix_post_g"], sp["ffn_pre_g"]], [(D, F32), (D, BF16)], [], ts)
    if ffn_weights is not None:
        W = dict(W, **ffn_weights.arrived(h2))
    gate_pre, up = _matmul_twin("mm_gate_up", h2, W["w_gate"], W["w_up"], "nn", F32)
    if down_weight is not None:
        sp = dict(sp, ffn_conv_b=sp["ffn_conv_b"] + down_weight.pass_on(gate_pre)[0, 0])
    act = _colwise("ffn_act", _ffn_act, [gate_pre, up], [sp["ffn_conv_w"], sp["ffn_conv_b"]], [BF16], [], tc)[0]
    if down_weight is not None:
        W = dict(W, **down_weight.arrived(act))
    f = _matmul("mm_down", act, W["w_down"], "nn", F32)

    def final(x1, f, t, g):
        r = _rs(f)
        fh = f * r
        err = x1 + fh * g - t
        loss = 0.5 * jnp.sum(jnp.mean(err * err, axis=-1, keepdims=True), axis=0, keepdims=True)
        dy = err * (1.0 / D)
        return dy, _rms_back(fh, r, dy * g), _colsum(dy * fh), loss

    dy, df, g_ffn_post, loss = _rowwise("final", final, [x1, f, tgt], [sp["ffn_post_g"]], [(D, F32), (D, BF16)],
                                        [(1, D), (1, LANE)], ts)
    gW = {}
    dact = _matmul("mm_down_dx", df, W["w_down"], "nt", F32)
    gW["w_down"] = _matmul("mm_down_dw", act, df, "tn", BF16)
    dgate, dup, g_ffn_conv_w, g_ffn_conv_b = _colwise(
        "ffn_act_bwd", _ffn_act_back, [dact, gate_pre, up], [sp["ffn_conv_w"], sp["ffn_conv_b"]], [BF16, BF16], [FFN_K, 1], tc)
    gW["w_gate"], gW["w_up"] = _matmul_twin("mm_gate_up_dw", h2, dgate, dup, "tn", BF16)
    if ffn_grads_ready is not None:
        sp = dict(sp, ffn_pre_g=sp["ffn_pre_g"] + ffn_grads_ready({n: gW[n] for n in ("w_down", "w_gate", "w_up")})[0, 0])
    dh2 = _matmul("mm_gu_dx", dgate, W["w_gate"], "nt", F32, dup, W["w_up"], chips=True)

    def mid_back(dy, dh2, x1, mix, g_mp, g_fp):
        r2 = _rs(x1)
        xh = x1 * r2
        dx1 = dy + _rms_back(xh, r2, dh2 * g_fp)
        r1 = _rs(mix)
        mh = mix * r1
        return dx1, _rms_back(mh, r1, dx1 * g_mp), _colsum(dh2 * xh), _colsum(dx1 * mh)

    dx1, dmix, g_ffn_pre, g_mix_post = _rowwise("bwd_mid", mid_back, [dy, dh2, x1, mix], [sp["mix_post_g"], sp["ffn_pre_g"]],
                                                [(D, F32), (D, BF16)], [(1, D), (1, D)], ts)
    dab_out = _matmul("mm_out_dx", dmix, W["w_out"], "nt", F32)
    db_out = (dab_out, (INNER, cfg.MLAW // INNER))
    gW["w_out"] = _matmul("mm_out_dw", ab_out, dmix, "tn", BF16)
    early_token = jnp.zeros((8, LANE), F32)
    if early_grads_ready is not None:
        early_token = early_grads_ready({n: gW[n] for n in ("w_down", "w_gate", "w_up", "w_out")})
        sp = dict(sp, ssm_norm_g=sp["ssm_norm_g"] + early_token[0, 0])

    dy_ssd, dz, g_ssm_norm = _ssd_post_bwd(cfg, db_out, y_ssd, z, sp["ssm_norm_g"])
    dxc, ddt_raw, g_dt_bias, g_a_log, g_d_skip = _ssd_bwd(cfg, dy_ssd, xc, dt_exp, a_s, dskip_exp, hin, dt_raw,
                                                          dt_bias_pad, a_log_pad, expand)
    dxbc, g_ssm_conv_w, g_ssm_conv_b = _colwise("ssm_act_bwd", _ssm_act_back, [dxc, xbc], [sp["ssm_conv_w"], sp["ssm_conv_b"]],
                                                [BF16], [SSM_K, 1], tc)

    delta_t = _attn_delta(cfg, dab_out, a_out, early_token)
    dQ, dK, dV = _attn_bwd(cfg, Qh, Kh, Vh, dab_out, lse_t, delta_t)
    dq, dkv, dkr = _mla_unpack(cfg, dQ, dK, dV, cos2, sin2)
    dcqn = _matmul("mm_uq_dx", dq, W["w_uq"], "nt", F32)
    dckvn = _matmul("mm_ukv_dx", dkv, W["w_ukv"], "nt", F32)
    gW["w_uq"] = _matmul("mm_uq_dw", cqn, dq, "tn", BF16)
    gW["w_ukv"] = _matmul("mm_ukv_dw", ckvn, dkv, "tn", BF16)

    def rms_back(x, dy, g):
        r = _rs(x)
        xh = x * r
        return _rms_back(xh, r, dy * g), _colsum(dy * xh)

    dc_q, g_q_norm = _rowwise("rms_q_bwd", rms_back, [c_q, dcqn], [sp["q_norm_g"]], [(cfg.QL, BF16)], [(1, cfg.QL)], ts)
    dc_kv, g_kv_norm = _rowwise("rms_kv_bwd", rms_back, [c_kv, dckvn], [sp["kv_norm_g"]], [(cfg.KVL, BF16)], [(1, cfg.KVL)], ts)

    du = dict(c_q=dc_q, c_kv=dc_kv, kr=dkr, z=dz.astype(BF16), xbc=dxbc, dt=ddt_raw.astype(BF16))
    du = jnp.concatenate([du[n] for n in sorted(du, key=lambda n: cfg.seg[n][0])], axis=1)
    assert du.shape[1] == cfg.EXT, "the layout of u has gaps"
    gW["w_in"] = _matmul("mm_in_dw", du, xn, "tn", BF16)
    if in_grad_ready is None:
        dxn = _matmul("mm_in_dx", du, W["w_in"], "nn", F32)
    else:
        token = in_grad_ready({n: gW[n] for n in ("w_in", "w_uq", "w_ukv")}, None)
        dxn = _matmul("mm_in_dx", du, W["w_in"], "nn", F32, after=token)
        sp = dict(sp, mix_pre_g=sp["mix_pre_g"] + in_grad_ready(None, dxn)[0, 0])

    def first_back(dx1, dxn, x, g):
        r = _rs(x)
        xh = x * r
        return dx1 + _rms_back(xh, r, dxn * g), _colsum(dxn * xh)

    grad_x, g_mix_pre = _rowwise("bwd_first", first_back, [dx1, dxn, x], [sp["mix_pre_g"]], [(D, F32)], [(1, D)], ts)

    gs = dict(mix_pre_g=g_mix_pre, q_norm_g=g_q_norm, kv_norm_g=g_kv_norm, ssm_conv_w=g_ssm_conv_w, ssm_conv_b=g_ssm_conv_b,
              dt_bias=g_dt_bias[:, :cfg.HS], a_log=g_a_log[:, :cfg.HS], d_skip=g_d_skip[:, :cfg.HS], ssm_norm_g=g_ssm_norm,
              mix_post_g=g_mix_post, ffn_pre_g=g_ffn_pre, ffn_conv_w=g_ffn_conv_w, ffn_conv_b=g_ffn_conv_b,
              ffn_post_g=g_ffn_post)
    return loss, grad_x, gW, gs


def _to_kernel_layout(cfg, name, w):
    if name == "w_in":
        parts, at = [], 0
        for off, width, n_off, n_width in sorted(cfg.seg.values()):
            parts += [jnp.zeros((off - at, w.shape[1]), w.dtype), w[n_off:n_off + n_width],
                      jnp.zeros((width - n_width, w.shape[1]), w.dtype)]
            at = off + width
        parts.append(jnp.zeros((cfg.EXT - at, w.shape[1]), w.dtype))
        return jnp.concatenate([p for p in parts if p.shape[0]], axis=0)
    if name in ("w_uq", "w_ukv"):
        per = NOPE + (ROPE if name == "w_uq" else VH)
        return jnp.concatenate([w[:, h * per:h * per + NOPE] for h in range(cfg.H)]
                               + [w[:, h * per + NOPE:(h + 1) * per] for h in range(cfg.H)], axis=1)
    return w


def _from_kernel_layout(cfg, name, g):
    if name == "w_in":
        return jnp.concatenate([g[off:off + n_width] for off, _, _, n_width in sorted(cfg.seg.values(), key=lambda s: s[2])], axis=0)
    if name in ("w_uq", "w_ukv"):
        second = ROPE if name == "w_uq" else VH
        base = cfg.H * NOPE
        parts = []
        for h in range(cfg.H):
            parts += [g[:, h * NOPE:(h + 1) * NOPE], g[:, base + h * second:base + (h + 1) * second]]
        return jnp.concatenate(parts, axis=1)
    return g


_CHIP_MAJOR = ("w_gate", "w_up")
_RELAYOUT = ("w_uq", "w_ukv")
_LAYOUT_ROWS = 256
_CONV_COLS = 256


def _w_in_layout(cfg, wg):
    _, rs, d = wg.shape
    tc = _pick(d, _LAYOUT_ROWS, LANE)

    def body(w_ref, o_ref):
        o_ref[...] = _to_kernel_layout(cfg, "w_in", jnp.concatenate([w_ref[k] for k in range(N_CHIPS)], axis=0))

    return pl.pallas_call(
        body, name="layout_w_in", grid=(d // tc,),
        in_specs=[pl.BlockSpec((N_CHIPS, rs, tc), lambda j: (0, 0, j))], out_specs=pl.BlockSpec((cfg.EXT, tc), lambda j: (0, j)),
        out_shape=jax.ShapeDtypeStruct((cfg.EXT, d), wg.dtype), compiler_params=_params(("parallel",)),
    )(wg)


def _w_in_grad_to_chips(cfg, g):
    _, d = g.shape
    rs = cfg.IN_COLS // N_CHIPS
    tc = _pick(d, _LAYOUT_ROWS, LANE)

    def body(g_ref, o_ref):
        nat = _from_kernel_layout(cfg, "w_in", g_ref[...])
        for k in range(N_CHIPS):
            o_ref[k] = nat[k * rs:(k + 1) * rs]

    return pl.pallas_call(
        body, name="layout_grad_w_in", grid=(d // tc,),
        in_specs=[pl.BlockSpec((cfg.EXT, tc), lambda j: (0, j))], out_specs=pl.BlockSpec((N_CHIPS, rs, tc), lambda j: (0, 0, j)),
        out_shape=jax.ShapeDtypeStruct((N_CHIPS, rs, d), g.dtype), compiler_params=_params(("parallel",)),
    )(g)


def _gathered_to_kernel(cfg, name, wg):
    if name in _CHIP_MAJOR:
        return wg
    if name == "w_in":
        return _w_in_layout(cfg, wg)
    if name not in _RELAYOUT:
        return wg.reshape(wg.shape[0] * wg.shape[1], wg.shape[2])
    _, rows, cs = wg.shape
    tr = _pick(rows, _LAYOUT_ROWS, 16)

    def body(w_ref, o_ref):
        o_ref[...] = _to_kernel_layout(cfg, name, jnp.concatenate([w_ref[k] for k in range(N_CHIPS)], axis=1))

    wide = jax.eval_shape(lambda w: _to_kernel_layout(cfg, name, w), jax.ShapeDtypeStruct((rows, N_CHIPS * cs), wg.dtype)).shape[1]
    return pl.pallas_call(
        body, name="layout_" + name, grid=(rows // tr,),
        in_specs=[pl.BlockSpec((N_CHIPS, tr, cs), lambda i: (0, i, 0))], out_specs=pl.BlockSpec((tr, wide), lambda i: (i, 0)),
        out_shape=jax.ShapeDtypeStruct((rows, wide), wg.dtype), compiler_params=_params(("parallel",)),
    )(wg)


def _grad_to_chips(cfg, name, g):
    if name in _CHIP_MAJOR:
        return g
    if name == "w_in":
        return _w_in_grad_to_chips(cfg, g)
    if name not in _RELAYOUT:
        return g.reshape(N_CHIPS, g.shape[0] // N_CHIPS, g.shape[1])
    rows, wide = g.shape
    tr = _pick(rows, _LAYOUT_ROWS, 16)
    cs = jax.eval_shape(lambda v: _from_kernel_layout(cfg, name, v), g).shape[1] // N_CHIPS

    def body(g_ref, o_ref):
        nat = _from_kernel_layout(cfg, name, g_ref[...])
        for k in range(N_CHIPS):
            o_ref[k] = nat[:, k * cs:(k + 1) * cs]

    return pl.pallas_call(
        body, name="layout_grad_" + name, grid=(rows // tr,),
        in_specs=[pl.BlockSpec((tr, wide), lambda i: (i, 0))], out_specs=pl.BlockSpec((N_CHIPS, tr, cs), lambda i: (0, i, 0)),
        out_shape=jax.ShapeDtypeStruct((N_CHIPS, rows, cs), g.dtype), compiler_params=_params(("parallel",)),
    )(g)


def _me():
    return lax.axis_index("x"), lax.axis_index("y"), lax.axis_index("c")


def _other_chips(x, y):
    return [(1 - x, y), (x, 1 - y), (1 - x, 1 - y)]


_ANY = pl.BlockSpec(memory_space=pl.ANY)


BLOCK_ELEMS = 1 << 19
BLOCK_ELEMS_FEW = 1 << 20


def _row_block(rows, cols, mult, elems=BLOCK_ELEMS):
    return _pick(rows, max(mult, elems // cols // mult * mult), mult)


def _scalar(v):
    return v.astype(I32).reshape(1)


def _blocks2d(r, c, mult, elems=BLOCK_ELEMS):
    if r % mult == 0:
        tr = _row_block(r, c, mult, elems)
        return (tr, c), r // tr, lambda i: (i, 0)
    tc = _pick(c, max(LANE, elems // r // LANE * LANE), LANE)
    return (r, tc), c // tc, lambda i: (0, i)


def _by_rows(rows):
    return rows % 32 == 0


def _half_shape(rows, cols):
    return (rows // 2, cols) if _by_rows(rows) else (rows, cols // 2)


def _half_blocks(rows, cols, mult, elems=BLOCK_ELEMS):
    hr, hc = _half_shape(rows, cols)
    block, n, part = _blocks2d(hr, hc, mult, elems)
    assert (hr % mult == 0) == _by_rows(rows), (rows, cols, mult)
    full = (lambda h, i: (h * n + i, 0)) if _by_rows(rows) else (lambda h, i: (0, h * n + i))
    return block, n, full, part


def _half(ref, k, half):
    hr, hc = _half_shape(ref.shape[1], ref.shape[2])
    if _by_rows(ref.shape[1]):
        return ref.at[k, pl.ds(pl.multiple_of(half * hr, 16), hr), :]
    return ref.at[k, :, pl.ds(pl.multiple_of(half * hc, LANE), hc)]


def _shard_blocks(w, br, bc):
    if w.shape[0] == 1:
        def write(ref, v):
            ref[...] = v
        return (lambda f: pl.BlockSpec((None, br, bc), lambda *a: (0, *f(*a)))), (lambda ref: ref[...]), write
    assert w.shape[1] == 1 and br == w.shape[0], w.shape

    def write_rows(ref, v):
        ref[:, 0, :] = v
    return (lambda f: pl.BlockSpec((br, 1, bc), lambda *a: (0, 0, f(*a)[1]))), (lambda ref: ref[:, 0, :]), write_rows


def _stage_shard(name, w, chip, after=None):
    rs, cs = w.shape[0] * w.shape[1], w.shape[2]
    (br, bc), n, idx = _blocks2d(rs, cs, 16, BLOCK_ELEMS_FEW)
    spec, get, _ = _shard_blocks(w, br, bc)

    def body(chip_ref, w_ref, *refs):
        refs[-1][...] = get(w_ref).astype(BF16)

    return pl.pallas_call(
        body, name="stage_" + name,
        grid_spec=pltpu.PrefetchScalarGridSpec(
            num_scalar_prefetch=1, grid=(n,),
            in_specs=[spec(lambda i, chip_ref: idx(i))] + ([] if after is None else [_ANY]),
            out_specs=pl.BlockSpec((None, br, bc), lambda i, chip_ref: (chip_ref[0], *idx(i)))),
        out_shape=jax.ShapeDtypeStruct((N_CHIPS, rs, cs), BF16),
        compiler_params=_params(("parallel",)),
    )(_scalar(chip), w, *([] if after is None else [after]))


_HBM = pl.BlockSpec(memory_space=pltpu.HBM)
_SEM = pl.BlockSpec(memory_space=pltpu.SEMAPHORE)
_EFFECT = pltpu.SideEffectType.DATAFLOW_SIDE_EFFECTING


def _split_start(name, bufs, n_copies, copies, after):
    n = len(bufs)

    def body(*refs):
        for cp in copies(refs[:n], refs[n + 1], refs[n + 2]):
            cp.start()
        refs[-1][...] = jnp.zeros_like(refs[-1])

    res = pl.pallas_call(
        body, name=name,
        out_shape=(pltpu.SemaphoreType.DMA((n_copies,)), pltpu.SemaphoreType.DMA((n_copies,)),
                   *[pltpu.HBM(b.shape, b.dtype) for b in bufs], jax.ShapeDtypeStruct((8, LANE), F32)),
        in_specs=[_HBM] * n + [_ANY], out_specs=(_SEM, _SEM, *[_HBM] * n, pl.BlockSpec(memory_space=pltpu.VMEM)),
        input_output_aliases={i: 2 + i for i in range(n)},
        compiler_params=pltpu.CompilerParams(has_side_effects=_EFFECT),
    )(*[pltpu.with_memory_space_constraint(b, pltpu.HBM) for b in bufs], after)
    return res[0], res[1], list(res[2:2 + n]), res[-1]


def _split_wait(name, send_sems, recv_sems, bufs, after, copies):
    n = len(bufs)

    def body(*refs):
        for cp in copies(refs[:n], refs[n], refs[n + 1]):
            cp.wait_send()
            cp.wait_recv()

    return list(pl.pallas_call(
        body, name=name, out_shape=[pltpu.HBM(b.shape, b.dtype) for b in bufs],
        in_specs=[_HBM] * n + [_SEM, _SEM, _ANY], out_specs=[_HBM] * n,
        input_output_aliases={i: i for i in range(n)},
        compiler_params=pltpu.CompilerParams(has_side_effects=_EFFECT),
    )(*bufs, send_sems, recv_sems, after))


def _gather_to_chips(bufs, send_sems, recv_sems):
    x, y, c = _me()
    return [pltpu.make_async_remote_copy(src_ref=_half(b, 2 * x + y, c), dst_ref=_half(b, 2 * x + y, c),
                                         send_sem=send_sems.at[3 * w + j], recv_sem=recv_sems.at[3 * w + j],
                                         device_id=(cx, cy, c), device_id_type=MESH_ID)
            for w, b in enumerate(bufs) for j, (cx, cy) in enumerate(_other_chips(x, y))]


def _gather_to_sibling(bufs, send_sems, recv_sems):
    x, y, c = _me()
    return [pltpu.make_async_remote_copy(src_ref=_half(b, 2 * cx + cy, c), dst_ref=_half(b, 2 * cx + cy, c),
                                         send_sem=send_sems.at[3 * w + j], recv_sem=recv_sems.at[3 * w + j],
                                         device_id=(x, y, 1 - c), device_id_type=MESH_ID)
            for w, b in enumerate(bufs) for j, (cx, cy) in enumerate(_other_chips(x, y))]


def _pair_exchange(name, grads):
    n = len(grads)

    def body(*refs):
        ins, outs, send_sems, recv_sems = refs[:n], refs[n:2 * n], refs[2 * n], refs[2 * n + 1]
        x, y, c = _me()
        cps = []
        for w, (g_ref, o_ref) in enumerate(zip(ins, outs)):
            cps.append(pltpu.make_async_remote_copy(src_ref=_half(g_ref, slice(None), 1 - c), dst_ref=o_ref,
                                                    send_sem=send_sems.at[w], recv_sem=recv_sems.at[w],
                                                    device_id=(x, y, 1 - c), device_id_type=MESH_ID))
            cps[-1].start()
        for cp in cps:
            cp.wait()

    return pl.pallas_call(
        body, name="pair_exchange_" + name, in_specs=[_ANY] * n, out_specs=[_ANY] * n,
        out_shape=[jax.ShapeDtypeStruct((g.shape[0], *_half_shape(g.shape[1], g.shape[2])), g.dtype) for g in grads],
        scratch_shapes=[pltpu.SemaphoreType.DMA((n,)), pltpu.SemaphoreType.DMA((n,))],
    )(*grads)


def _pair_copies(grads, lands, send_sems, recv_sems):
    x, y, c = _me()
    return [pltpu.make_async_remote_copy(src_ref=_half(g_ref, slice(None), 1 - c), dst_ref=l_ref, send_sem=send_sems.at[w],
                                         recv_sem=recv_sems.at[w], device_id=(x, y, 1 - c), device_id_type=MESH_ID)
            for w, (g_ref, l_ref) in enumerate(zip(grads, lands))]


def _pair_exchange_start(name, grads):
    n = len(grads)
    lands = [lax.empty((g.shape[0], *_half_shape(g.shape[1], g.shape[2])), g.dtype) for g in grads]
    send_sems, recv_sems, bufs, token = _split_start(
        "pair_exchange_start_" + name, [*grads, *lands], n, lambda refs, ss, rs: _pair_copies(refs[:n], refs[n:], ss, rs),
        jnp.zeros((8, LANE), F32))
    return (send_sems, recv_sems, bufs), token


def _pair_exchange_wait(name, state, after):
    send_sems, recv_sems, bufs = state
    n = len(bufs) // 2
    bufs = _split_wait("pair_exchange_wait_" + name, send_sems, recv_sems, bufs, after,
                       lambda refs, ss, rs: _pair_copies(refs[:n], refs[n:], ss, rs))
    return bufs[:n], bufs[n:]


def _pair_sum(name, g, theirs, c):
    (br, bc), nb, full, part = _half_blocks(g.shape[1], g.shape[2], 16, 2 * BLOCK_ELEMS_FEW)

    def body(c_ref, a_ref, b_ref, o_ref):
        o_ref[...] = (a_ref[...].astype(F32) + b_ref[...].astype(F32)).astype(o_ref.dtype)

    return pl.pallas_call(
        body, name="pair_sum_" + name,
        grid_spec=pltpu.PrefetchScalarGridSpec(
            num_scalar_prefetch=1, grid=(N_CHIPS, nb),
            in_specs=[pl.BlockSpec((None, br, bc), lambda k, i, c_ref: (k, *full(c_ref[0], i))),
                      pl.BlockSpec((None, br, bc), lambda k, i, c_ref: (k, *part(i)))],
            out_specs=pl.BlockSpec((None, br, bc), lambda k, i, c_ref: (k, *part(i)))),
        out_shape=jax.ShapeDtypeStruct(theirs.shape, BF16),
        compiler_params=_params(("parallel", "parallel")),
    )(_scalar(c), g, theirs)


def _chip_copies(srcs, lands, send_sems, recv_sems):
    x, y, c = _me()
    return [pltpu.make_async_remote_copy(src_ref=s_ref.at[2 * cx + cy], dst_ref=l_ref.at[j], send_sem=send_sems.at[3 * w + j],
                                         recv_sem=recv_sems.at[3 * w + j], device_id=(cx, cy, c), device_id_type=MESH_ID)
            for w, (s_ref, l_ref) in enumerate(zip(srcs, lands)) for j, (cx, cy) in enumerate(_other_chips(x, y))]


def _chip_exchange_start(name, sums):
    n = len(sums)
    lands = [lax.empty((3,) + s.shape[1:], s.dtype) for s in sums]
    send_sems, recv_sems, bufs, token = _split_start(
        "chip_exchange_start_" + name, [*sums, *lands], 3 * n, lambda refs, ss, rs: _chip_copies(refs[:n], refs[n:], ss, rs),
        jnp.zeros((8, LANE), F32))
    return send_sems, recv_sems, bufs[:n], bufs[n:], token


def _chip_exchange_wait(name, send_sems, recv_sems, sums, lands, after):
    n = len(sums)
    bufs = _split_wait("chip_exchange_wait_" + name, send_sems, recv_sems, [*sums, *lands], after,
                       lambda refs, ss, rs: _chip_copies(refs[:n], refs[n:], ss, rs))
    return bufs[:n], bufs[n:]


def _chip_sum(name, sums, theirs, chip):
    _, h, cs = sums.shape
    (br, bc), nb, idx = _blocks2d(h, cs, 16, BLOCK_ELEMS_FEW)

    def body(chip_ref, s_ref, t_ref, o_ref):
        acc = s_ref[...].astype(F32)
        for k in range(3):
            acc = acc + t_ref[k].astype(F32)
        o_ref[...] = acc

    return pl.pallas_call(
        body, name="chip_sum_" + name,
        grid_spec=pltpu.PrefetchScalarGridSpec(
            num_scalar_prefetch=1, grid=(nb,),
            in_specs=[pl.BlockSpec((None, br, bc), lambda i, chip_ref: (chip_ref[0], *idx(i))),
                      pl.BlockSpec((3, br, bc), lambda i, chip_ref: (0, *idx(i)))],
            out_specs=pl.BlockSpec((br, bc), lambda i, chip_ref: idx(i))),
        out_shape=jax.ShapeDtypeStruct((h, cs), F32),
        compiler_params=_params(("parallel",)),
    )(_scalar(chip), sums, theirs)


def _sibling_copies(halves, lands, send_sems, recv_sems):
    x, y, c = _me()
    return [pltpu.make_async_remote_copy(src_ref=h_ref, dst_ref=l_ref, send_sem=send_sems.at[w], recv_sem=recv_sems.at[w],
                                         device_id=(x, y, 1 - c), device_id_type=MESH_ID)
            for w, (h_ref, l_ref) in enumerate(zip(halves, lands))]


def _sibling_exchange_start(name, halves, after):
    n = len(halves)
    lands = [lax.empty(h.shape, h.dtype) for h in halves]
    send_sems, recv_sems, bufs, token = _split_start(
        "sibling_exchange_start_" + name, [*halves, *lands], n, lambda refs, ss, rs: _sibling_copies(refs[:n], refs[n:], ss, rs),
        after)
    return (send_sems, recv_sems, bufs), token


def _sibling_exchange_wait(name, state, after):
    send_sems, recv_sems, bufs = state
    n = len(bufs) // 2
    bufs = _split_wait("sibling_exchange_wait_" + name, send_sems, recv_sems, bufs, after,
                       lambda refs, ss, rs: _sibling_copies(refs[:n], refs[n:], ss, rs))
    return bufs[:n], bufs[n:]


def _sibling_exchange(name, halves):
    n = len(halves)

    def body(*refs):
        ins, outs, send_sems, recv_sems = refs[:n], refs[n:2 * n], refs[2 * n], refs[2 * n + 1]
        x, y, c = _me()
        cps = []
        for w, (h_ref, o_ref) in enumerate(zip(ins, outs)):
            cps.append(pltpu.make_async_remote_copy(src_ref=h_ref, dst_ref=o_ref, send_sem=send_sems.at[w], recv_sem=recv_sems.at[w],
                                                    device_id=(x, y, 1 - c), device_id_type=MESH_ID))
            cps[-1].start()
        for cp in cps:
            cp.wait()

    return pl.pallas_call(
        body, name="sibling_exchange_" + name, in_specs=[_ANY] * n, out_specs=[_ANY] * n,
        out_shape=[jax.ShapeDtypeStruct(h.shape, h.dtype) for h in halves],
        scratch_shapes=[pltpu.SemaphoreType.DMA((n,)), pltpu.SemaphoreType.DMA((n,))],
    )(*halves)


N_DEV = 8


def _peer_copies(bufs, send_sems, recv_sems):
    vec, land = bufs
    x, y, c = _me()
    return [pltpu.make_async_remote_copy(src_ref=vec, dst_ref=land.at[4 * x + 2 * y + c], send_sem=send_sems.at[p - 1],
                                         recv_sem=recv_sems.at[p - 1], device_id=(x ^ (p >> 2), y ^ ((p >> 1) & 1), c ^ (p & 1)),
                                         device_id_type=MESH_ID) for p in range(1, N_DEV)]


def _allreduce_small_start(vec, after):
    land = jnp.zeros((N_DEV,) + vec.shape, F32)
    send_sems, recv_sems, bufs, _ = _split_start("allreduce_small_start", [vec, land], N_DEV - 1, _peer_copies, after)
    return send_sems, recv_sems, bufs


def _allreduce_small_wait(state, chip, core, after):
    send_sems, recv_sems, bufs = state
    vec, land = _split_wait("allreduce_small_wait", send_sems, recv_sems, bufs, after, _peer_copies)

    def body(me_ref, v_ref, l_ref, o_ref):
        acc = None
        for k in range(N_DEV):
            term = jnp.where(me_ref[0] == k, v_ref[...], l_ref[k])
            acc = term if acc is None else acc + term
        o_ref[...] = acc

    return pl.pallas_call(
        body, name="allreduce_small_sum",
        grid_spec=pltpu.PrefetchScalarGridSpec(
            num_scalar_prefetch=1, grid=(1,),
            in_specs=[pl.BlockSpec(vec.shape, lambda i, me_ref: (0, 0)), pl.BlockSpec(land.shape, lambda i, me_ref: (0, 0, 0))],
            out_specs=pl.BlockSpec(vec.shape, lambda i, me_ref: (0, 0))),
        out_shape=jax.ShapeDtypeStruct(vec.shape, F32), compiler_params=_params(("arbitrary",)),
    )(_scalar(2 * chip + core), vec, land)


def _adam_math(w, g, m, v):
    m = ADAM_B1 * m + (1.0 - ADAM_B1) * g
    v = ADAM_B2 * v + (1.0 - ADAM_B2) * (g * g)
    m_hat = m / (1.0 - ADAM_B1 ** ADAM_STEP)
    v_hat = v / (1.0 - ADAM_B2 ** ADAM_STEP)
    return -ADAM_LR * (m_hat / (jnp.sqrt(v_hat) + ADAM_EPS) + ADAM_WD * w), m, v


def _adamw(name, w, g, m, v):
    R, C = w.shape
    tr = _row_block(R, C, 8)

    def body(w_ref, g_ref, m_ref, v_ref, d_ref, nm_ref, nv_ref):
        d_ref[...], nm_ref[...], nv_ref[...] = _adam_math(w_ref[...], g_ref[...], m_ref[...], v_ref[...])

    blk = pl.BlockSpec((tr, C), lambda i: (i, 0))
    return pl.pallas_call(
        body, name=name, grid=(R // tr,), in_specs=[blk] * 4, out_specs=[blk] * 3,
        out_shape=[jax.ShapeDtypeStruct((R, C), F32)] * 3, compiler_params=_params(("parallel",)),
    )(w, g, m, v)


def _adamw_halves(name, w, mine, theirs, m, v, c):
    rs, cs = w.shape[0] * w.shape[1], w.shape[2]
    (br, bc), nb, whole, half = _half_blocks(rs, cs, 8)
    spec, get, put = _shard_blocks(w, br, bc)

    def body(c_ref, w_ref, a_ref, b_ref, m_ref, v_ref, g_ref, d_ref, nm_ref, nv_ref):
        g = jnp.where(pl.program_id(0) == c_ref[0], a_ref[...], b_ref[...])
        put(g_ref, g)
        for ref, val in zip((d_ref, nm_ref, nv_ref), _adam_math(get(w_ref), g, get(m_ref), get(v_ref))):
            put(ref, val)

    full = spec(lambda s, i, c_ref: whole(s, i))
    part = pl.BlockSpec((br, bc), lambda s, i, c_ref: half(i))
    return pl.pallas_call(
        body, name=name,
        grid_spec=pltpu.PrefetchScalarGridSpec(num_scalar_prefetch=1, grid=(2, nb), in_specs=[full, part, part, full, full],
                                               out_specs=[full] * 4),
        out_shape=[jax.ShapeDtypeStruct(w.shape, F32)] * 4, compiler_params=_params(("parallel", "parallel")),
    )(_scalar(c), w, mine, theirs, m, v)


def _pack_small(arrs, lanes=LANE):
    flat = jnp.concatenate([a.reshape(-1) for a in arrs])
    n = -(-flat.shape[0] // (8 * lanes)) * 8 * lanes
    return jnp.pad(flat, (0, n - flat.shape[0])).reshape(8, n // 8)


def _unpack_small(vec, shapes):
    flat, out, off = vec.reshape(-1), [], 0
    for s in shapes:
        out.append(flat[off:off + s[0] * s[1]].reshape(s))
        off += s[0] * s[1]
    return out


class _LateWeights:
    def __init__(self, cfg, tag, names, staged, after):
        self.cfg, self.tag, self.names, self.k = cfg, tag, names, 3 * len(names)
        self.send, self.recv, self.bufs, self.token = _split_start(f"gather_{tag}_chips_start", staged, self.k, _gather_to_chips,
                                                                    after)

    def pass_on(self, after):
        bufs = _split_wait(f"gather_{self.tag}_chips_wait", self.send, self.recv, self.bufs, after, _gather_to_chips)
        self.send, self.recv, self.bufs, token = _split_start(f"gather_{self.tag}_sibling_start", bufs, self.k, _gather_to_sibling,
                                                               self.token)
        return token

    def arrived(self, after):
        bufs = _split_wait(f"gather_{self.tag}_sibling_wait", self.send, self.recv, self.bufs, after, _gather_to_sibling)
        return {n: _gathered_to_kernel(self.cfg, n, b) for n, b in zip(self.names, bufs)}


def _step(cfg, a):
    chip = 2 * lax.axis_index("x") + lax.axis_index("y")
    core = lax.axis_index("c")
    big = BIG

    ffn = ("w_gate", "w_up", "w_down")
    first = ("w_in", "w_uq", "w_ukv")
    sp = {n: a[n] for n in SMALL}
    sharded = _pack_small([a[n] for n in SMALL_SHARDED], 2 * LANE)
    slabs = jnp.where(lax.broadcasted_iota(I32, (N_CHIPS,) + sharded.shape, 0) == chip, sharded[None], 0.0)
    staged = {"w_in": _stage_shard("w_in", a["w_in"], chip)}
    in_weight = _LateWeights(cfg, "in", ("w_in", "sharded_small"), [staged["w_in"], slabs], jnp.zeros((8, LANE), F32))
    behind = in_weight.token
    for n in big[1:]:
        behind = staged[n] = _stage_shard(n, a[n], chip, behind)
    in_weight.pass_on(behind)
    xn_early = _rms_pre(cfg, a["x"], sp["mix_pre_g"] + in_weight.token[0, 0])
    W = in_weight.arrived(xn_early)
    allp = W.pop("sharded_small").reshape((N_CHIPS,) + sharded.shape)
    per_chip = [_unpack_small(allp[ch], [a[n].shape for n in SMALL_SHARDED]) for ch in range(N_CHIPS)]
    for k, n in enumerate(SMALL_SHARDED):
        sp[n] = jnp.concatenate([per_chip[ch][k] for ch in range(N_CHIPS)], axis=1)

    mla_weights = _LateWeights(cfg, "mla", first[1:], [staged[n] for n in first[1:]], W["w_in"])
    out_weight = _LateWeights(cfg, "out", ("w_out",), [staged["w_out"]], mla_weights.token)
    ffn_weights = _LateWeights(cfg, "ffn", ffn[:2], [staged[n] for n in ffn[:2]], out_weight.token)
    down_weight = _LateWeights(cfg, "down", ffn[2:], [staged[n] for n in ffn[2:]], ffn_weights.token)

    state = {}

    def ffn_grads_ready(grads):
        state["ffn_pairs"], token = _pair_exchange_start("ffn", [_grad_to_chips(cfg, n, grads[n]) for n in ffn_grads])
        return token

    def pair_sums(names, grads, theirs):
        return [_pair_sum(n, g, t, core) for n, g, t in zip(names, grads, theirs)]

    def early_grads_ready(grads):
        g_out = [_grad_to_chips(cfg, "w_out", grads["w_out"])]
        g_ffn, t_ffn = _pair_exchange_wait("ffn", state["ffn_pairs"], g_out[0])
        sums = pair_sums(ffn_grads, g_ffn, t_ffn) + pair_sums(["w_out"], g_out, _pair_exchange("out", g_out))
        state["early"] = _chip_exchange_start("early", sums)
        return state["early"][-1]

    def reduced_halves(tag, names, after):
        send_sems, recv_sems, s_bufs, l_bufs, _ = state[tag]
        s_bufs, l_bufs = _chip_exchange_wait(tag, send_sems, recv_sems, s_bufs, l_bufs, after)
        return [_chip_sum(n, s, t, chip) for n, s, t in zip(names, s_bufs, l_bufs)]

    def in_grad_ready(grads, after):
        if grads is not None:
            state["rest_pairs"], token = _pair_exchange_start("rest", [_grad_to_chips(cfg, n, grads[n]) for n in first])
            return token
        state["rest"] = _chip_exchange_start("rest", pair_sums(first, *_pair_exchange_wait("rest", state["rest_pairs"], after)))
        return state["rest"][-1]

    ffn_grads = ("w_down", "w_gate", "w_up")
    early = ffn_grads + ("w_out",)
    loss, grad_x, gW, gs = _local_grads(cfg, a["x"], a["loss_target"], W, sp, mla_weights, out_weight, ffn_weights, down_weight,
                                        ffn_grads_ready, early_grads_ready, in_grad_ready, xn_early, down_weight.token)
    out = {"grad_x": grad_x}

    def adamw(names, mine, theirs):
        for n, gm, gt in zip(names, mine, theirs):
            out["grad_" + n], out["delta_" + n], out["new_m_" + n], out["new_v_" + n] = _adamw_halves(
                "adamw_" + n, a[n], gm, gt, a["m_" + n], a["v_" + n], core)

    mine = reduced_halves("early", early, grad_x)
    theirs = _sibling_exchange("early", mine[:1])
    later, _ = _sibling_exchange_start("early", mine[1:], theirs[0])
    adamw(early[:1], mine[:1], theirs)
    e_mine, e_theirs = _sibling_exchange_wait("early", later, out["new_v_" + early[0]])
    adamw(early[3:], e_mine[2:], e_theirs[2:])
    mine = reduced_halves("rest", first, out["new_v_" + early[-1]])
    rest, token = _sibling_exchange_start("rest", mine, mine[0])
    small = _allreduce_small_start(_pack_small([gs[n] for n in SMALL] + [loss]), token)
    adamw(early[1:3], e_mine[:2], e_theirs[:2])
    adamw(first, *_sibling_exchange_wait("rest", rest, out["new_v_" + early[2]]))
    shapes = [gs[n].shape for n in SMALL] + [(1, LANE)]
    red = _unpack_small(_allreduce_small_wait(small, chip, core, out["new_v_" + first[-1]]), shapes)
    g_small = dict(zip(SMALL, red[:-1]))
    for n in SMALL_SHARDED:
        cs = a[n].shape[1]
        g_small[n] = lax.dynamic_slice_in_dim(g_small[n], chip * cs, cs, axis=1)
    out["loss"] = red[-1][0, 0]
    sshapes = [a[n].shape for n in SMALL]
    d, nm, nv = _adamw("adamw_small", _pack_small([a[n] for n in SMALL]), _pack_small([g_small[n] for n in SMALL]),
                       _pack_small([a["m_" + n] for n in SMALL]), _pack_small([a["v_" + n] for n in SMALL]))
    for n, dd, mm, vv in zip(SMALL, _unpack_small(d, sshapes), _unpack_small(nm, sshapes), _unpack_small(nv, sshapes)):
        out["grad_" + n], out["delta_" + n], out["new_m_" + n], out["new_v_" + n] = g_small[n], dd, mm, vv
    return out


def kernel(x, mix_pre_g, w_in, q_norm_g, w_uq, kv_norm_g, w_ukv, ssm_conv_w, ssm_conv_b, dt_bias, a_log, d_skip, ssm_norm_g, w_out, mix_post_g, ffn_pre_g, w_gate, w_up, ffn_conv_w, ffn_conv_b, w_down, ffn_post_g, loss_target, m_mix_pre_g, m_w_in, m_q_norm_g, m_w_uq, m_kv_norm_g, m_w_ukv, m_ssm_conv_w, m_ssm_conv_b, m_dt_bias, m_a_log, m_d_skip, m_ssm_norm_g, m_w_out, m_mix_post_g, m_ffn_pre_g, m_w_gate, m_w_up, m_ffn_conv_w, m_ffn_conv_b, m_w_down, m_ffn_post_g, v_mix_pre_g, v_w_in, v_q_norm_g, v_w_uq, v_kv_norm_g, v_w_ukv, v_ssm_conv_w, v_ssm_conv_b, v_dt_bias, v_a_log, v_d_skip, v_ssm_norm_g, v_w_out, v_mix_post_g, v_ffn_pre_g, v_w_gate, v_w_up, v_ffn_conv_w, v_ffn_conv_b, v_w_down, v_ffn_post_g):
    args = dict(locals())
    def given(k, v):
        if k in ("w_in", "m_w_in", "v_w_in"):
            return jnp.transpose(v, (2, 0, 1))
        return v if k.removeprefix("m_").removeprefix("v_") in BIG or v.ndim < 3 else v[0]

    out = _step(_FULL, {k: given(k, v) for k, v in args.items()})
    res = [out["loss"], out["grad_x"][None]]
    for pre in ("grad_", "delta_", "new_m_", "new_v_"):
        for n in WEIGHTS:
            o = out[pre + n]
            res.append(jnp.transpose(o, (1, 2, 0)) if n == "w_in" else o if n in BIG or args[n].ndim < 3 else o[None])
    return tuple(res)
```

```python
import math

import jax
import jax.numpy as jnp
from jax import lax
from jax.experimental import pallas as pl
from jax.experimental.pallas import tpu as pltpu

F32, BF16, I32 = jnp.float32, jnp.bfloat16, jnp.int32
NN = (((1,), (0,)), ((), ()))
NT = (((1,), (1,)), ((), ()))
TN = (((0,), (0,)), ((), ()))
HI = lax.Precision.HIGHEST
MESH_ID = pl.DeviceIdType.MESH

EPS = 1e-6
CHUNK = 64
NOPE, ROPE, VH = 128, 64, 128
ROPE_THETA = 10000.0
HP, NST = 64, 128
SSM_K, FFN_K = 4, 3
LANE = 128
N_CHIPS = 4
VMEM_LIMIT = 52 * 1024 * 1024
MM_TILE, MM_TILE_K = 1408, 2816

ADAM_LR, ADAM_B1, ADAM_B2, ADAM_EPS, ADAM_WD, ADAM_STEP = 0.001, 0.9, 0.999, 1e-08, 0.01, 10


class _Cfg:
    def __init__(self, S, D, QL, KVL, H, HS, G, DFF, T):
        self.S, self.D, self.QL, self.KVL, self.H, self.HS, self.G, self.DFF, self.T = S, D, QL, KVL, H, HS, G, DFF, T
        self.INNER = HS * HP
        self.CONVCH = self.INNER + 2 * G * NST
        self.QW = H * (NOPE + ROPE)
        self.KVW = H * (NOPE + VH)
        self.MLAW = H * VH
        self.MIXW = self.MLAW + self.INNER
        self.IN_COLS = QL + KVL + ROPE + self.INNER + self.CONVCH + HS
        natural, at = {}, 0
        for name, w in (("c_q", QL), ("c_kv", KVL), ("kr", ROPE), ("z", self.INNER), ("xbc", self.CONVCH), ("dt", HS)):
            natural[name] = (at, w)
            at += w
        self.seg, taken = {}, []
        for name in sorted(natural, key=lambda n: -natural[n][1]):
            w = -(-natural[name][1] // LANE) * LANE
            off = next(o for o in range(0, self.IN_COLS * 2, w) if all(o + w <= t or o >= t + tw for t, tw in taken))
            taken.append((off, w))
            self.seg[name] = (off, w) + natural[name]
        self.EXT = max(o + w for o, w in taken)
        self.NPAIR = HS // 2
        self.REP = HS // G

    def window(self, name):
        off, w, _, _ = self.seg[name]
        return w, off // w


_FULL = _Cfg(S=2048, D=2048, QL=768, KVL=512, H=8, HS=16, G=2, DFF=5632, T=256)
BIG = ("w_in", "w_uq", "w_ukv", "w_out", "w_gate", "w_up", "w_down")

SMALL = ("mix_pre_g", "q_norm_g", "kv_norm_g", "ssm_conv_w", "ssm_conv_b", "dt_bias", "a_log", "d_skip", "ssm_norm_g",
         "mix_post_g", "ffn_pre_g", "ffn_conv_w", "ffn_conv_b", "ffn_post_g")
SMALL_SHARDED = ("ssm_conv_w", "ffn_conv_w")
WEIGHTS = ("mix_pre_g", "w_in", "q_norm_g", "w_uq", "kv_norm_g", "w_ukv", "ssm_conv_w", "ssm_conv_b", "dt_bias", "a_log",
           "d_skip", "ssm_norm_g", "w_out", "mix_post_g", "ffn_pre_g", "w_gate", "w_up", "ffn_conv_w", "ffn_conv_b",
           "w_down", "ffn_post_g")


def _pick(n, target, mult):
    best = None
    for d in range(mult, min(n, target) + 1, mult):
        if n % d == 0:
            best = d
    return best if best is not None else n


def _params(sem=None):
    kw = dict(vmem_limit_bytes=VMEM_LIMIT)
    if sem is not None:
        kw["dimension_semantics"] = sem
    return pltpu.CompilerParams(**kw)


def _dot(a, b, dims=NN, precision=None):
    return lax.dot_general(a, b, dims, preferred_element_type=F32, precision=precision)


def _sigmoid(x):
    return 1.0 / (1.0 + jnp.exp(-x))


def _rs(x):
    return lax.rsqrt(jnp.mean(x * x, axis=-1, keepdims=True) + EPS)


def _rms_back(xh, r, dn):
    return r * (dn - xh * jnp.mean(dn * xh, axis=-1, keepdims=True))


def _colsum(v):
    return jnp.sum(v, axis=0, keepdims=True)


def _matmul(name, a, b, mode, out_dtype, a2=None, b2=None, chips=False, after=None):
    cs = None
    if mode == "nn":
        (M, K), N = a.shape, b.shape[-1]
        if chips:
            cs, N = N, N_CHIPS * N
    elif mode == "nt":
        (M, K), N = a.shape, b.shape[-2]
        if chips:
            cs = b.shape[-1]
    else:
        (K, M), N = a.shape, b.shape[1]
        if chips:
            cs = N // N_CHIPS
    tm = _pick(M, MM_TILE, LANE)
    tn = _pick(cs if chips and mode != "nt" else N, MM_TILE, LANE)
    tk = _pick(cs, MM_TILE, LANE) if chips and mode == "nt" else _pick(K, MM_TILE_K, LANE)
    nk = K // tk
    dims = {"nn": NN, "nt": NT, "tn": TN}[mode]
    a_spec = pl.BlockSpec((tk, tm), lambda i, j, k: (k, i)) if mode == "tn" else pl.BlockSpec((tm, tk), lambda i, j, k: (i, k))
    b_spec = pl.BlockSpec((tn, tk), lambda i, j, k: (j, k)) if mode == "nt" else pl.BlockSpec((tk, tn), lambda i, j, k: (k, j))
    o_spec = pl.BlockSpec((tm, tn), lambda i, j, k: (i, j))
    o_shape = (M, N)
    if chips and mode == "nn":
        per = cs // tn
        b_spec = pl.BlockSpec((None, tk, tn), lambda i, j, k: (j // per, k, j % per))
    elif chips and mode == "nt":
        per = cs // tk
        b_spec = pl.BlockSpec((None, tn, tk), lambda i, j, k: (k // per, j, k % per))
    elif chips:
        per = cs // tn
        o_spec = pl.BlockSpec((None, tm, tn), lambda i, j, k: (j // per, i, j % per))
        o_shape = (N_CHIPS, M, cs)
    two = a2 is not None

    def product(refs):
        part = _dot(refs[0][...].astype(BF16), refs[1][...].astype(BF16), dims)
        if two:
            part += _dot(refs[2][...].astype(BF16), refs[3][...].astype(BF16), dims)
        return part

    def body_whole_k(*refs):
        refs[-1][...] = product(refs).astype(refs[-1].dtype)

    def body(*refs):
        o_ref, acc_ref = refs[-2], refs[-1]
        k = pl.program_id(2)

        @pl.when(k == 0)
        def _():
            acc_ref[...] = product(refs)

        @pl.when(k > 0)
        def _():
            acc_ref[...] += product(refs)

        @pl.when(k == nk - 1)
        def _():
            o_ref[...] = acc_ref[...].astype(o_ref.dtype)

    ins = ((a, b, a2, b2) if two else (a, b)) + (() if after is None else (after,))
    return pl.pallas_call(
        body_whole_k if nk == 1 else body, name=name, grid=(M // tm, N // tn, nk),
        in_specs=[a_spec, b_spec] * (2 if two else 1) + ([] if after is None else [pl.BlockSpec(memory_space=pl.ANY)]),
        out_specs=o_spec,
        out_shape=jax.ShapeDtypeStruct(o_shape, out_dtype),
        scratch_shapes=[] if nk == 1 else [pltpu.VMEM((tm, tn), F32)],
        compiler_params=_params(("parallel", "parallel", "arbitrary")),
    )(*ins)


def _matmul_twin(name, a, b1, b2, mode, out_dtype):
    if mode == "nn":
        (M, K), cs = a.shape, b1.shape[-1]
        tm = _pick(M, MM_TILE // 2, LANE)
    else:
        (K, M), cs = a.shape, b1.shape[1] // N_CHIPS
        tm = _pick(M, MM_TILE, LANE)
    tn = _pick(cs, MM_TILE, LANE)
    per = cs // tn
    dims = NN if mode == "nn" else TN

    def body(a_ref, b1_ref, b2_ref, o1_ref, o2_ref):
        lhs = a_ref[...].astype(BF16)
        o1_ref[...] = _dot(lhs, b1_ref[...].astype(BF16), dims).astype(o1_ref.dtype)
        o2_ref[...] = _dot(lhs, b2_ref[...].astype(BF16), dims).astype(o2_ref.dtype)

    if mode == "nn":
        a_spec = pl.BlockSpec((tm, K), lambda i, j: (i, 0))
        b_spec = pl.BlockSpec((None, K, tn), lambda i, j: (j // per, 0, j % per))
        o_spec, o_shape = pl.BlockSpec((tm, tn), lambda i, j: (i, j)), (M, N_CHIPS * cs)
    else:
        a_spec = pl.BlockSpec((K, tm), lambda i, j: (0, i))
        b_spec = pl.BlockSpec((K, tn), lambda i, j: (0, j))
        o_spec, o_shape = pl.BlockSpec((None, tm, tn), lambda i, j: (j // per, i, j % per)), (N_CHIPS, M, cs)
    return pl.pallas_call(
        body, name=name, grid=(M // tm, N_CHIPS * per), in_specs=[a_spec, b_spec, b_spec], out_specs=[o_spec, o_spec],
        out_shape=[jax.ShapeDtypeStruct(o_shape, out_dtype)] * 2, compiler_params=_params(("parallel", "parallel")),
    )(a, b1, b2)


def _window(a):
    return (a[0], *a[1]) if isinstance(a, tuple) else (a, a.shape[1], 0)


def _rowwise(name, fn, rows, mats, outs, reds, ts):
    rows, widths, blocks = zip(*[_window(a) for a in rows])
    S = rows[0].shape[0]
    nr, nm, no = len(rows), len(mats), len(outs)

    def body(*refs):
        res = fn(*[r[...] for r in refs[:nr + nm]])
        res = res if isinstance(res, (tuple, list)) else (res,)
        for r, v in zip(refs[nr + nm:nr + nm + no], res[:no]):
            r[...] = v.astype(r.dtype)
        first = pl.program_id(0) == 0
        for r, v in zip(refs[nr + nm + no:], res[no:]):
            @pl.when(first)
            def _():
                r[...] = jnp.broadcast_to(v, r.shape)

            @pl.when(jnp.logical_not(first))
            def _():
                r[...] += jnp.broadcast_to(v, r.shape)

    in_specs = [pl.BlockSpec((ts, w), lambda i, b=b: (i, b)) for w, b in zip(widths, blocks)]
    in_specs += [pl.BlockSpec(m.shape, lambda i, nd=m.ndim: (0,) * nd) for m in mats]
    out_specs = [pl.BlockSpec((ts, w), lambda i: (i, 0)) for w, _ in outs]
    out_specs += [pl.BlockSpec(s, lambda i: (0, 0)) for s in reds]
    out_shape = [jax.ShapeDtypeStruct((S, w), dt) for w, dt in outs] + [jax.ShapeDtypeStruct(s, F32) for s in reds]
    return pl.pallas_call(
        body, name=name, grid=(S // ts,), in_specs=in_specs, out_specs=out_specs, out_shape=out_shape,
        compiler_params=_params(("arbitrary",) if reds else ("parallel",)),
    )(*rows, *mats)


def _shift_down(v, s):
    if s == 0:
        return v
    rows = lax.broadcasted_iota(I32, v.shape, 0)
    return jnp.where(rows >= s, pltpu.roll(v, s, 0), 0.0)


def _shift_up(v, s):
    if s == 0:
        return v
    n = v.shape[0]
    rows = lax.broadcasted_iota(I32, v.shape, 0)
    return jnp.where(rows < n - s, pltpu.roll(v, n - s, 0), 0.0)


def _conv(x, w, b):
    K = w.shape[0]
    y = jnp.broadcast_to(b, x.shape)
    for k in range(K):
        y = y + w[k:k + 1, :] * _shift_down(x, K - 1 - k)
    return y


def _conv_back(x, w, dc):
    K = w.shape[0]
    dx = jnp.zeros_like(x)
    dw = []
    for k in range(K):
        up = _shift_up(dc, K - 1 - k)
        dx = dx + w[k:k + 1, :] * up
        dw.append(_colsum(up * x))
    return dx, jnp.concatenate(dw, axis=0), _colsum(dc)


def _colwise(name, fn, cols, vecs, outs, pouts, tc):
    cols, widths, blocks = zip(*[_window(a) for a in cols])
    S, C = cols[0].shape[0], widths[0]
    firsts = [b * (C // tc) for b in blocks]
    nc_, nv, no = len(cols), len(vecs), len(outs)

    def body(*refs):
        res = fn(*[r[...] for r in refs[:nc_ + nv]])
        res = res if isinstance(res, (tuple, list)) else (res,)
        for r, v in zip(refs[nc_ + nv:], res):
            r[...] = v.astype(r.dtype)

    in_specs = [pl.BlockSpec((S, tc), lambda j, f=f: (0, f + j)) for f in firsts]
    in_specs += [pl.BlockSpec((v.shape[0], tc), lambda j: (0, j)) for v in vecs]
    out_specs = [pl.BlockSpec((S, tc), lambda j: (0, j)) for _ in outs] + [pl.BlockSpec((k, tc), lambda j: (0, j)) for k in pouts]
    out_shape = [jax.ShapeDtypeStruct((S, C), dt) for dt in outs] + [jax.ShapeDtypeStruct((k, C), F32) for k in pouts]
    return pl.pallas_call(
        body, name=name, grid=(C // tc,), in_specs=in_specs, out_specs=out_specs, out_shape=out_shape,
        compiler_params=_params(("parallel",)),
    )(*cols, *vecs)


_G0, _G1 = math.sqrt(2.0 / math.pi), 0.044715


def _gelu(g):
    th = jnp.tanh(_G0 * (g + _G1 * g * g * g))
    return 0.5 * g * (1.0 + th), th


def _ffn_act(gate_pre, up, w, b):
    act, _ = _gelu(_conv(gate_pre, w, b))
    return act * up


def _ffn_act_back(dact, gate_pre, up, w, b):
    g = _conv(gate_pre, w, b)
    ge, th = _gelu(g)
    dge = 0.5 * (1.0 + th) + 0.5 * g * (1.0 - th * th) * _G0 * (1.0 + 3.0 * _G1 * g * g)
    dup = dact * ge
    dgate_pre, dw, db = _conv_back(gate_pre, w, dact * up * dge)
    return dgate_pre, dup, dw, db


def _ssm_act(xbc, w, b):
    c = _conv(xbc, w, b)
    return c * _sigmoid(c)


def _ssm_act_back(dxc, xbc, w, b):
    c = _conv(xbc, w, b)
    sg = _sigmoid(c)
    return _conv_back(xbc, w, dxc * sg * (1.0 + c * (1.0 - sg)))


def _rope_tables(S):
    inv = 1.0 / (ROPE_THETA ** (jnp.arange(0, ROPE, 2, dtype=F32) / ROPE))
    ang = jnp.arange(S, dtype=F32)[:, None] * inv[None, :]
    cos, sin = jnp.cos(ang), jnp.sin(ang)
    return jnp.tile(cos, (1, 4)), jnp.tile(jnp.concatenate([-sin, sin], axis=1), (1, 2))


def _swap_halves(x):
    lane = lax.broadcasted_iota(I32, x.shape, 1)
    w = x.shape[1]
    return jnp.where((lane % ROPE) < ROPE // 2, pltpu.roll(x, w - ROPE // 2, 1), pltpu.roll(x, ROPE // 2, 1))


def _rot(x, cos2, sin2):
    return x * cos2 + _swap_halves(x) * sin2


def _rot_back(dy, cos2, sin2):
    return dy * cos2 + _swap_halves(dy * sin2)


def _mla_pack(cfg, q, kv, kr, cos2, sin2):
    S, H = cfg.S, cfg.H
    ts = _pick(S, 256, 8)
    kr, _, kr_block = _window(kr)

    def body(q_ref, kv_ref, kr_ref, c_ref, s_ref, Q_ref, K_ref, V_ref):
        c2, s2 = c_ref[...], s_ref[...]
        krr = _rot(kr_ref[...], c2, s2)
        kr_half = (krr.astype(BF16), pltpu.roll(krr, ROPE, 1).astype(BF16))
        for j in range(H // 2):
            qr = _rot(q_ref[:, (H + j) * LANE:(H + j + 1) * LANE], c2, s2).astype(BF16)
            for h in (2 * j, 2 * j + 1):
                Q_ref[h, :, 0:LANE] = q_ref[:, h * LANE:(h + 1) * LANE].astype(BF16)
                Q_ref[h, :, LANE:] = qr
                K_ref[h, :, 0:LANE] = kv_ref[:, h * LANE:(h + 1) * LANE].astype(BF16)
                K_ref[h, :, LANE:] = kr_half[h % 2]
                V_ref[h] = kv_ref[:, (H + h) * LANE:(H + h + 1) * LANE].astype(BF16)

    tab = pl.BlockSpec((ts, LANE), lambda i: (i, 0))
    heads = lambda w: pl.BlockSpec((H, ts, w), lambda i: (0, i, 0))
    return pl.pallas_call(
        body, name="mla_pack", grid=(S // ts,),
        in_specs=[pl.BlockSpec((ts, cfg.QW), lambda i: (i, 0)), pl.BlockSpec((ts, cfg.KVW), lambda i: (i, 0)),
                  pl.BlockSpec((ts, LANE), lambda i: (i, kr_block)), tab, tab],
        out_specs=[heads(2 * LANE), heads(2 * LANE), heads(LANE)],
        out_shape=[jax.ShapeDtypeStruct((H, S, 2 * LANE), BF16), jax.ShapeDtypeStruct((H, S, 2 * LANE), BF16),
                   jax.ShapeDtypeStruct((H, S, LANE), BF16)],
        compiler_params=_params(("parallel",)),
    )(q, kv, kr, cos2, sin2)


def _mla_unpack(cfg, dQ, dK, dV, cos2, sin2):
    S, H = cfg.S, cfg.H
    ts = _pick(S, 256, 8)

    def body(dQ_ref, dK_ref, dV_ref, c_ref, s_ref, dq_ref, dkv_ref, dkr_ref):
        c2, s2 = c_ref[...], s_ref[...]
        lo = lax.broadcasted_iota(I32, (ts, LANE), 1) < ROPE
        tk = jnp.zeros((ts, LANE), F32)
        for h in range(H):
            dq_ref[:, h * LANE:(h + 1) * LANE] = dQ_ref[h, :, 0:LANE].astype(BF16)
            dkv_ref[:, h * LANE:(h + 1) * LANE] = dK_ref[h, :, 0:LANE].astype(BF16)
            dkv_ref[:, (H + h) * LANE:(H + h + 1) * LANE] = dV_ref[h].astype(BF16)
            own = lo if h % 2 == 0 else jnp.logical_not(lo)
            tk = tk + jnp.where(own, dK_ref[h, :, LANE:], 0.0)
        for j in range(H // 2):
            dr = dQ_ref[2 * j, :, LANE:] + dQ_ref[2 * j + 1, :, LANE:]
            dq_ref[:, (H + j) * LANE:(H + j + 1) * LANE] = _rot_back(dr, c2, s2).astype(BF16)
        dkr_rot = jnp.where(lo, tk + pltpu.roll(tk, ROPE, 1), 0.0)
        dkr_ref[...] = _rot_back(dkr_rot, c2, s2).astype(BF16)

    tab = pl.BlockSpec((ts, LANE), lambda i: (i, 0))
    return pl.pallas_call(
        body, name="mla_unpack", grid=(S // ts,),
        in_specs=[pl.BlockSpec((H, ts, 2 * LANE), lambda i: (0, i, 0)), pl.BlockSpec((H, ts, 2 * LANE), lambda i: (0, i, 0)),
                  pl.BlockSpec((H, ts, LANE), lambda i: (0, i, 0)), tab, tab],
        out_specs=[pl.BlockSpec((ts, cfg.QW), lambda i: (i, 0)), pl.BlockSpec((ts, cfg.KVW), lambda i: (i, 0)), tab],
        out_shape=[jax.ShapeDtypeStruct((S, cfg.QW), BF16), jax.ShapeDtypeStruct((S, cfg.KVW), BF16),
                   jax.ShapeDtypeStruct((S, LANE), BF16)],
        compiler_params=_params(("parallel",)),
    )(dQ, dK, dV, cos2, sin2)


_ATT_T = 256
_ATT_HB = 8
_ATT_SCALE = (NOPE + ROPE) ** -0.5


def _diag_mask(transposed=False):
    r = lax.broadcasted_iota(I32, (_ATT_T, _ATT_T), 0) // CHUNK
    c = lax.broadcasted_iota(I32, (_ATT_T, _ATT_T), 1) // CHUNK
    return r <= c if transposed else c <= r


def _row_form(col):
    return jnp.broadcast_to(col, (col.shape[0], LANE)).T[0:8, :]


def _attn_fwd(cfg, Q, K, V):
    S, H, T, HB = cfg.S, cfg.H, _ATT_T, min(cfg.H, _ATT_HB)

    def body(q_ref, k_ref, v_ref, o_ref, lse_t_ref):
        qi = pl.program_id(1)

        def head_step(b, kb, carry, mask):
            m, l, acc = carry
            ks = pl.multiple_of(kb * T, T)
            s = _dot(q_ref[b], k_ref[b, pl.ds(ks, T), :], NT) * _ATT_SCALE
            if mask is not None:
                s = jnp.where(mask, s, -1e30)
            m_new = jnp.maximum(m, jnp.max(s, axis=1, keepdims=True))
            p = jnp.exp(s - m_new)
            alpha = jnp.exp(m - m_new)
            l = alpha * l + jnp.sum(p, axis=1, keepdims=True)
            acc = alpha * acc + _dot(p.astype(BF16), v_ref[b, pl.ds(ks, T), :])
            return m_new, l, acc

        def step(kb, carry, mask=None):
            return tuple(head_step(b, kb, carry[b], mask) for b in range(HB))

        init = (jnp.full((T, 1), -1e30, F32), jnp.zeros((T, 1), F32), jnp.zeros((T, VH), F32))
        done = step(qi, lax.fori_loop(0, qi, step, (init,) * HB), _diag_mask())
        for b, (m, l, acc) in enumerate(done):
            o_ref[:, b * LANE:(b + 1) * LANE] = acc / l
            lse_t_ref[b] = _row_form(m + jnp.log(l))

    return pl.pallas_call(
        body, name="attn_fwd", grid=(H // HB, S // T),
        in_specs=[pl.BlockSpec((HB, T, 2 * LANE), lambda h, i: (h, i, 0)), pl.BlockSpec((HB, S, 2 * LANE), lambda h, i: (h, 0, 0)),
                  pl.BlockSpec((HB, S, LANE), lambda h, i: (h, 0, 0))],
        out_specs=[pl.BlockSpec((T, HB * LANE), lambda h, i: (i, h)), pl.BlockSpec((HB, 8, T), lambda h, i: (h, 0, i))],
        out_shape=[jax.ShapeDtypeStruct((S, H * LANE), F32), jax.ShapeDtypeStruct((H, 8, S), F32)],
        compiler_params=_params(("parallel", "parallel")),
    )(Q, K, V)


def _attn_delta(cfg, do, o, after):
    S, H, T = cfg.S, cfg.H, _ATT_T

    def body(do_ref, o_ref, after_ref, dl_t_ref):
        for h in range(H):
            sl = slice(h * LANE, (h + 1) * LANE)
            dl_t_ref[h] = _row_form(jnp.sum(do_ref[:, sl] * o_ref[:, sl], axis=1, keepdims=True))

    wide = pl.BlockSpec((T, H * LANE), lambda i: (i, 0))
    return pl.pallas_call(
        body, name="attn_delta", grid=(S // T,), in_specs=[wide, wide, _ANY],
        out_specs=pl.BlockSpec((H, 8, T), lambda i: (0, 0, i)), out_shape=jax.ShapeDtypeStruct((H, 8, S), F32),
        compiler_params=_params(("parallel",)),
    )(do, o, after)


_ATT_HB_BWD = 4


def _attn_bwd(cfg, Q, K, V, do, lse_t, delta_t):
    S, H, T, HB = cfg.S, cfg.H, _ATT_T, min(cfg.H, _ATT_HB_BWD)
    nq = S // T

    def body(q_ref, k_ref, v_ref, do_ref, lse_ref, dl_ref, dq_ref, dk_ref, dv_ref):
        kb = pl.program_id(1)

        @pl.when(kb == 0)
        def _():
            dq_ref[...] = jnp.zeros_like(dq_ref)

        def head_step(b, qi, carry, mask):
            dk, dv = carry
            qs = pl.multiple_of(qi * T, T)
            q = q_ref[b, pl.ds(qs, T), :]
            k = k_ref[b]
            dob = do_ref[pl.ds(qs, T), b * LANE:(b + 1) * LANE].astype(BF16)
            s = _dot(k, q, NT) * _ATT_SCALE
            if mask is not None:
                s = jnp.where(mask, s, -1e30)
            p = jnp.exp(s - lse_ref[b, 0:1, pl.ds(qs, T)])
            dv = dv + _dot(p.astype(BF16), dob)
            dp = _dot(v_ref[b], dob, NT)
            ds = (p * (dp - dl_ref[b, 0:1, pl.ds(qs, T)]) * _ATT_SCALE).astype(BF16)
            dk = dk + _dot(ds, q)
            dq_ref[b, pl.ds(qs, T), :] += _dot(ds, k, TN)
            return dk, dv

        def step(qi, carry, mask=None):
            return tuple(head_step(b, qi, carry[b], mask) for b in range(HB))

        zero = (jnp.zeros((T, 2 * LANE), F32), jnp.zeros((T, VH), F32))
        done = lax.fori_loop(kb + 1, nq, step, step(kb, (zero,) * HB, _diag_mask(transposed=True)))
        for b, (dk, dv) in enumerate(done):
            dk_ref[b] = dk
            dv_ref[b] = dv

    row = pl.BlockSpec((HB, 8, S), lambda h, j: (h, 0, 0))
    whole = pl.BlockSpec((HB, S, 2 * LANE), lambda h, j: (h, 0, 0))
    return pl.pallas_call(
        body, name="attn_bwd", grid=(H // HB, S // T),
        in_specs=[whole, pl.BlockSpec((HB, T, 2 * LANE), lambda h, j: (h, j, 0)), pl.BlockSpec((HB, T, LANE), lambda h, j: (h, j, 0)),
                  pl.BlockSpec((S, HB * LANE), lambda h, j: (0, h)), row, row],
        out_specs=[whole, pl.BlockSpec((HB, T, 2 * LANE), lambda h, j: (h, j, 0)), pl.BlockSpec((HB, T, LANE), lambda h, j: (h, j, 0))],
        out_shape=[jax.ShapeDtypeStruct((H, S, 2 * LANE), F32), jax.ShapeDtypeStruct((H, S, 2 * LANE), F32),
                   jax.ShapeDtypeStruct((H, S, LANE), F32)],
        compiler_params=_params(("parallel", "arbitrary")),
    )(Q, K, V, do, lse_t, delta_t)


def _expand_matrix(cfg):
    r = lax.broadcasted_iota(I32, (LANE, cfg.INNER), 0)
    c = lax.broadcasted_iota(I32, (LANE, cfg.INNER), 1)
    return (r == c // HP).astype(F32)


def _softplus(x):
    return jnp.maximum(x, 0.0) + jnp.log(1.0 + jnp.exp(-jnp.abs(x)))


def _ssd_prep(cfg, dt_raw, dt_bias_pad, a_log_pad, expand):
    HS = cfg.HS

    def fn(raw, bias, alog, E):
        heads = lax.broadcasted_iota(I32, raw.shape, 1) < HS
        dt = jnp.where(heads, _softplus(raw + bias), 0.0)
        a = dt * jnp.where(heads[0:1], -jnp.exp(alog), 0.0)
        return dt, a, _dot(dt, E, precision=HI)

    return _rowwise("ssd_prep", fn, [dt_raw], [dt_bias_pad, a_log_pad, expand],
                    [(LANE, F32), (LANE, F32), (cfg.INNER, F32)], [], _pick(cfg.S, 512, 8))


def _tril(T):
    return lax.broadcasted_iota(I32, (T, T), 0) >= lax.broadcasted_iota(I32, (T, T), 1)


def _ssd_fwd(cfg, xc, dt_exp, a_small, dskip_exp, expand):
    S, T, INNER, G, NPAIR = cfg.S, cfg.T, cfg.INNER, cfg.G, cfg.NPAIR
    NC = S // T

    def body(xc_ref, dte_ref, as_ref, dsk_ref, e_ref, y_ref, hin_ref, ht_ref):
        @pl.when(pl.program_id(0) == 0)
        def _():
            ht_ref[...] = jnp.zeros_like(ht_ref)

        tril = _tril(T)
        tri = tril.astype(F32)
        acs_s = _dot(tri, as_ref[...], precision=HI)
        acs_e = _dot(acs_s, e_ref[...], precision=HI)
        acs_t = acs_s.T
        lo = lax.broadcasted_iota(I32, (T, LANE), 1) < HP
        for g in range(G):
            Bb = xc_ref[:, INNER + g * NST:INNER + (g + 1) * NST].astype(BF16)
            Cb = xc_ref[:, INNER + (G + g) * NST:INNER + (G + g + 1) * NST].astype(BF16)
            Gm = _dot(Cb, Bb, NT)
            for j in range(g * NPAIR // G, (g + 1) * NPAIR // G):
                sl = slice(j * LANE, (j + 1) * LANE)
                Xp = xc_ref[:, sl]
                Xdt = Xp * dte_ref[:, sl]
                Xb = Xdt.astype(BF16)
                acs_p = acs_e[:, sl]
                last = acs_p[T - 1:T, :]
                Hin = ht_ref[j]
                hin_ref[0, j] = Hin
                yd = []
                for e in (0, 1):
                    h = 2 * j + e
                    Lm = jnp.exp(jnp.where(tril, acs_s[:, h:h + 1] - acs_t[h:h + 1, :], -1e30))
                    yd.append(_dot((Gm * Lm).astype(BF16), Xb))
                y_off = _dot(Cb, Hin.astype(BF16)) * jnp.exp(acs_p)
                y_ref[:, sl] = jnp.where(lo, yd[0], yd[1]) + y_off + Xp * dsk_ref[:, sl]
                st = _dot(Bb, (Xdt * jnp.exp(last - acs_p)).astype(BF16), TN)
                ht_ref[j] = jnp.exp(last) * Hin + st

    rows = lambda w: pl.BlockSpec((T, w), lambda c: (c, 0))
    return pl.pallas_call(
        body, name="ssd_fwd", grid=(NC,),
        in_specs=[rows(cfg.CONVCH), rows(INNER), rows(LANE), pl.BlockSpec((1, INNER), lambda c: (0, 0)),
                  pl.BlockSpec((LANE, INNER), lambda c: (0, 0))],
        out_specs=[rows(INNER), pl.BlockSpec((1, NPAIR, NST, LANE), lambda c: (c, 0, 0, 0))],
        out_shape=[jax.ShapeDtypeStruct((S, INNER), F32), jax.ShapeDtypeStruct((NC, NPAIR, NST, LANE), F32)],
        scratch_shapes=[pltpu.VMEM((NPAIR, NST, LANE), F32)],
        compiler_params=_params(("arbitrary",)),
    )(xc, dt_exp, a_small, dskip_exp, expand)


def _ssd_bwd(cfg, dy, xc, dt_exp, a_small, dskip_exp, hin, dt_raw, dt_bias_pad, a_log_pad, expand):
    S, T, INNER, G, NPAIR, HS = cfg.S, cfg.T, cfg.INNER, cfg.G, cfg.NPAIR, cfg.HS
    NC = S // T

    def body(dy_ref, xc_ref, dte_ref, as_ref, dsk_ref, hin_ref, raw_ref, bias_ref, alog_ref, e_ref,
             dxc_ref, draw_ref, dbias_ref, dalog_ref, dskip_ref, dht_ref, cols_ref, rows_ref, dacs_ref, ddt_ref):
        first = pl.program_id(0) == 0

        @pl.when(first)
        def _():
            dht_ref[...] = jnp.zeros_like(dht_ref)

        tril = _tril(T)
        tri = tril.astype(F32)
        a_s = as_ref[...]
        acs_s = _dot(tri, a_s, precision=HI)
        acs_e = _dot(acs_s, e_ref[...], precision=HI)
        acs_t = acs_s.T
        lo = lax.broadcasted_iota(I32, (T, LANE), 1) < HP
        last_row = lax.broadcasted_iota(I32, (T, LANE), 0) == T - 1
        cols_ref[...] = jnp.zeros_like(cols_ref)
        rows_ref[...] = jnp.zeros_like(rows_ref)
        dsk_parts = []
        for g in range(G):
            bsl = slice(INNER + g * NST, INNER + (g + 1) * NST)
            csl = slice(INNER + (G + g) * NST, INNER + (G + g + 1) * NST)
            Bb = xc_ref[:, bsl].astype(BF16)
            Cb = xc_ref[:, csl].astype(BF16)
            Gm = _dot(Cb, Bb, NT)
            dG = jnp.zeros((T, T), F32)
            dB = jnp.zeros((T, NST), F32)
            dC = jnp.zeros((T, NST), F32)
            for j in range(g * NPAIR // G, (g + 1) * NPAIR // G):
                sl = slice(j * LANE, (j + 1) * LANE)
                Xp = xc_ref[:, sl]
                dtp = dte_ref[:, sl]
                Xdt = Xp * dtp
                Xb = Xdt.astype(BF16)
                acs_p = acs_e[:, sl]
                last = acs_p[T - 1:T, :]
                e_p, dec, cd = jnp.exp(acs_p), jnp.exp(last - acs_p), jnp.exp(last)
                Hin = hin_ref[0, j]
                Hb = Hin.astype(BF16)
                dHn = dht_ref[j]
                dHb = dHn.astype(BF16)
                dYp = dy_ref[:, sl]
                z = _dot(Cb, Hb)
                dz = (dYp * e_p).astype(BF16)
                dacs_p = dYp * z * e_p
                dC = dC + _dot(dz, Hb, NT)
                dHin = _dot(Cb, dz, TN) + cd * dHn
                dlast = _colsum(dHn * Hin) * cd
                qv = _dot(Bb, dHb)
                dXdt = qv * dec
                ddec = qv * Xdt * dec
                dacs_p = dacs_p - ddec
                dlast = dlast + _colsum(ddec)
                dB = dB + _dot((Xdt * dec).astype(BF16), dHb, NT)
                for e in (0, 1):
                    h = 2 * j + e
                    Lm = jnp.exp(jnp.where(tril, acs_s[:, h:h + 1] - acs_t[h:h + 1, :], -1e30))
                    Mh = Gm * Lm
                    dYe = jnp.where(lo if e == 0 else jnp.logical_not(lo), dYp, 0.0).astype(BF16)
                    dM = _dot(dYe, Xb, NT)
                    dXdt = dXdt + _dot(Mh.astype(BF16), dYe, TN)
                    W = dM * Mh
                    cols_ref[:, h:h + 1] = jnp.sum(W, axis=1, keepdims=True)
                    rows_ref[h:h + 1, :] = _colsum(W)
                    dG = dG + dM * Lm
                dacs_ref[:, sl] = dacs_p + jnp.where(last_row, dlast, 0.0)
                ddt_ref[:, sl] = dXdt * Xp
                dxc_ref[:, sl] = dXdt * dtp + dYp * dsk_ref[:, sl]
                dsk_parts.append(_colsum(dYp * Xp))
                dht_ref[j] = dHin
            dGb = dG.astype(BF16)
            dxc_ref[:, bsl] = dB + _dot(dGb, Cb, TN)
            dxc_ref[:, csl] = dC + _dot(dGb, Bb)
        E = e_ref[...]
        dacs_s = cols_ref[...] - rows_ref[...].T + _dot(dacs_ref[...], E, NT, precision=HI)
        da = _dot(tri, dacs_s, TN, precision=HI)
        heads = lax.broadcasted_iota(I32, (1, LANE), 1) < HS
        A = jnp.where(heads, -jnp.exp(alog_ref[...]), 0.0)
        ddt = _dot(ddt_ref[...], E, NT, precision=HI) + da * A
        draw = jnp.where(heads, ddt * _sigmoid(raw_ref[...] + bias_ref[...]), 0.0)
        draw_ref[...] = draw
        dsk = _dot(jnp.broadcast_to(jnp.concatenate(dsk_parts, axis=1), (8, INNER)), E, NT, precision=HI)[0:1]
        for ref, val in ((dbias_ref, _colsum(draw)), (dalog_ref, _colsum(da * a_s)), (dskip_ref, dsk)):
            @pl.when(first)
            def _():
                ref[...] = val

            @pl.when(jnp.logical_not(first))
            def _():
                ref[...] += val

    dt_raw, _, raw_block = _window(dt_raw)
    rows = lambda w, b=0: pl.BlockSpec((T, w), lambda c: (NC - 1 - c, b))
    vec = lambda w: pl.BlockSpec((1, w), lambda c: (0, 0))
    return pl.pallas_call(
        body, name="ssd_bwd", grid=(NC,),
        in_specs=[rows(INNER), rows(cfg.CONVCH), rows(INNER), rows(LANE), vec(INNER),
                  pl.BlockSpec((1, NPAIR, NST, LANE), lambda c: (NC - 1 - c, 0, 0, 0)), rows(LANE, raw_block), vec(LANE), vec(LANE),
                  pl.BlockSpec((LANE, INNER), lambda c: (0, 0))],
        out_specs=[rows(cfg.CONVCH), rows(LANE), vec(LANE), vec(LANE), vec(LANE)],
        out_shape=[jax.ShapeDtypeStruct((S, cfg.CONVCH), F32), jax.ShapeDtypeStruct((S, LANE), F32)]
        + [jax.ShapeDtypeStruct((1, LANE), F32)] * 3,
        scratch_shapes=[pltpu.VMEM((NPAIR, NST, LANE), F32), pltpu.VMEM((T, LANE), F32), pltpu.VMEM((LANE, T), F32),
                        pltpu.VMEM((T, INNER), F32), pltpu.VMEM((T, INNER), F32)],
        compiler_params=_params(("arbitrary",)),
    )(dy, xc, dt_exp, a_small, dskip_exp, hin, dt_raw, dt_bias_pad, a_log_pad, expand)


def _ssd_post(cfg, y, z, norm_g):
    W = cfg.INNER // cfg.G

    def fn(y, z, g):
        yz = y * z * _sigmoid(z)
        return jnp.concatenate([yz[:, i * W:(i + 1) * W] * _rs(yz[:, i * W:(i + 1) * W]) for i in range(cfg.G)], axis=1) * g

    return _rowwise("ssd_post", fn, [y, z], [norm_g], [(cfg.INNER, BF16)], [], _pick(cfg.S, 256, 8))[0]


def _ssd_post_bwd(cfg, db, y, z, norm_g):
    W = cfg.INNER // cfg.G

    def fn(db, y, z, g):
        sg = _sigmoid(z)
        yz = y * z * sg
        dn = db * g
        dyz, nh = [], []
        for i in range(cfg.G):
            seg = yz[:, i * W:(i + 1) * W]
            r = _rs(seg)
            nh.append(seg * r)
            dyz.append(_rms_back(nh[-1], r, dn[:, i * W:(i + 1) * W]))
        dyz = jnp.concatenate(dyz, axis=1)
        return dyz * z * sg, dyz * y * sg * (1.0 + z * (1.0 - sg)), _colsum(db * jnp.concatenate(nh, axis=1))

    return _rowwise("ssd_post_bwd", fn, [db, y, z], [norm_g], [(cfg.INNER, F32), (cfg.INNER, F32)], [(1, cfg.INNER)],
                    _pick(cfg.S, 256, 8))


def _rms_pre(cfg, x, g):
    return _rowwise("rms_pre", lambda x, g: x * _rs(x) * g, [x], [g], [(cfg.D, BF16)], [], _pick(cfg.S, 256, 8))[0]


def _local_grads(cfg, x, tgt, W, sp, mla_weights=None, out_weight=None, ffn_weights=None, down_weight=None,
                 ffn_grads_ready=None, early_grads_ready=None, in_grad_ready=None, xn=None, after_in=None):
    S, D, H, INNER = cfg.S, cfg.D, cfg.H, cfg.INNER
    ts = _pick(S, 256, 8)
    tc = _CONV_COLS

    if xn is None:
        xn = _rms_pre(cfg, x, sp["mix_pre_g"])
    u = _matmul("mm_in", xn, W["w_in"], "nt", F32, after=after_in)
    c_q, c_kv, kr, z, xbc, dt_raw = [(u, cfg.window(n)) for n in ("c_q", "c_kv", "kr", "z", "xbc", "dt")]

    if mla_weights is not None:
        sp = dict(sp, q_norm_g=sp["q_norm_g"] + mla_weights.pass_on(u)[0, 0])
    cqn = _rowwise("rms_q", lambda x, g: x * _rs(x) * g, [c_q], [sp["q_norm_g"]], [(cfg.QL, BF16)], [], ts)[0]
    ckvn = _rowwise("rms_kv", lambda x, g: x * _rs(x) * g, [c_kv], [sp["kv_norm_g"]], [(cfg.KVL, BF16)], [], ts)[0]
    if mla_weights is not None:
        W = dict(W, **mla_weights.arrived(ckvn))
    q = _matmul("mm_uq", cqn, W["w_uq"], "nn", F32)
    kv = _matmul("mm_ukv", ckvn, W["w_ukv"], "nn", F32)
    cos2, sin2 = _rope_tables(S)
    Qh, Kh, Vh = _mla_pack(cfg, q, kv, kr, cos2, sin2)
    a_out, lse_t = _attn_fwd(cfg, Qh, Kh, Vh)
    if out_weight is not None:
        sp = dict(sp, ssm_conv_b=sp["ssm_conv_b"] + out_weight.pass_on(a_out)[0, 0])

    pad = lambda v: jnp.pad(v, ((0, 0), (0, LANE - v.shape[1])))
    expand = _expand_matrix(cfg)
    dt_bias_pad, a_log_pad = pad(sp["dt_bias"]), pad(sp["a_log"])
    dskip_exp = jnp.repeat(sp["d_skip"], HP, axis=1)
    xc = _colwise("ssm_act", _ssm_act, [xbc], [sp["ssm_conv_w"], sp["ssm_conv_b"]], [F32], [], tc)[0]
    dt_s, a_s, dt_exp = _ssd_prep(cfg, dt_raw, dt_bias_pad, a_log_pad, expand)
    y_ssd, hin = _ssd_fwd(cfg, xc, dt_exp, a_s, dskip_exp, expand)
    b_out = _ssd_post(cfg, y_ssd, z, sp["ssm_norm_g"])

    ab_out = jnp.concatenate([a_out.astype(BF16), b_out], axis=1)
    if out_weight is not None:
        W = dict(W, **out_weight.arrived(ab_out))
    if ffn_weights is not None:
        sp = dict(sp, mix_post_g=sp["mix_post_g"] + ffn_weights.pass_on(ab_out)[0, 0])
    mix = _matmul("mm_out", ab_out, W["w_out"], "nn", F32)

    def mid(x, mix, g_mp, g_fp):
        x1 = x + mix * _rs(mix) * g_mp
        return x1, x1 * _rs(x1) * g_fp

    x1, h2 = _rowwise("fwd_mid", mid, [x, mix], [sp["mix_post_g"], sp["ffn_pre_g"]], [(D, F32), (D, BF16)], [], ts)
    if ffn_weights is not None:
        W = dict(W, **ffn_weights.arrived(h2))
    gate_pre, up = _matmul_twin("mm_gate_up", h2, W["w_gate"], W["w_up"], "nn", F32)
    if down_weight is not None:
        sp = dict(sp, ffn_conv_b=sp["ffn_conv_b"] + down_weight.pass_on(gate_pre)[0, 0])
    act = _colwise("ffn_act", _ffn_act, [gate_pre, up], [sp["ffn_conv_w"], sp["ffn_conv_b"]], [BF16], [], tc)[0]
    if down_weight is not None:
        W = dict(W, **down_weight.arrived(act))
    f = _matmul("mm_down", act, W["w_down"], "nn", F32)

    def final(x1, f, t, g):
        r = _rs(f)
        fh = f * r
        err = x1 + fh * g - t
        loss = 0.5 * jnp.sum(jnp.mean(err * err, axis=-1, keepdims=True), axis=0, keepdims=True)
        dy = err * (1.0 / D)
        return dy, _rms_back(fh, r, dy * g), _colsum(dy * fh), loss

    dy, df, g_ffn_post, loss = _rowwise("final", final, [x1, f, tgt], [sp["ffn_post_g"]], [(D, F32), (D, BF16)],
                                        [(1, D), (1, LANE)], ts)
    gW = {}
    dact = _matmul("mm_down_dx", df, W["w_down"], "nt", F32)
    gW["w_down"] = _matmul("mm_down_dw", act, df, "tn", BF16)
    dgate, dup, g_ffn_conv_w, g_ffn_conv_b = _colwise(
        "ffn_act_bwd", _ffn_act_back, [dact, gate_pre, up], [sp["ffn_conv_w"], sp["ffn_conv_b"]], [BF16, BF16], [FFN_K, 1], tc)
    gW["w_gate"], gW["w_up"] = _matmul_twin("mm_gate_up_dw", h2, dgate, dup, "tn", BF16)
    if ffn_grads_ready is not None:
        sp = dict(sp, ffn_pre_g=sp["ffn_pre_g"] + ffn_grads_ready({n: gW[n] for n in ("w_down", "w_gate", "w_up")})[0, 0])
    dh2 = _matmul("mm_gu_dx", dgate, W["w_gate"], "nt", F32, dup, W["w_up"], chips=True)

    def mid_back(dy, dh2, x1, mix, g_mp, g_fp):
        r2 = _rs(x1)
        xh = x1 * r2
        dx1 = dy + _rms_back(xh, r2, dh2 * g_fp)
        r1 = _rs(mix)
        mh = mix * r1
        return dx1, _rms_back(mh, r1, dx1 * g_mp), _colsum(dh2 * xh), _colsum(dx1 * mh)

    dx1, dmix, g_ffn_pre, g_mix_post = _rowwise("bwd_mid", mid_back, [dy, dh2, x1, mix], [sp["mix_post_g"], sp["ffn_pre_g"]],
                                                [(D, F32), (D, BF16)], [(1, D), (1, D)], ts)
    dab_out = _matmul("mm_out_dx", dmix, W["w_out"], "nt", F32)
    db_out = (dab_out, (INNER, cfg.MLAW // INNER))
    gW["w_out"] = _matmul("mm_out_dw", ab_out, dmix, "tn", BF16)
    early_token = jnp.zeros((8, LANE), F32)
    if early_grads_ready is not None:
        early_token = early_grads_ready({n: gW[n] for n in ("w_down", "w_gate", "w_up", "w_out")})
        sp = dict(sp, ssm_norm_g=sp["ssm_norm_g"] + early_token[0, 0])

    dy_ssd, dz, g_ssm_norm = _ssd_post_bwd(cfg, db_out, y_ssd, z, sp["ssm_norm_g"])
    dxc, ddt_raw, g_dt_bias, g_a_log, g_d_skip = _ssd_bwd(cfg, dy_ssd, xc, dt_exp, a_s, dskip_exp, hin, dt_raw,
                                                          dt_bias_pad, a_log_pad, expand)
    dxbc, g_ssm_conv_w, g_ssm_conv_b = _colwise("ssm_act_bwd", _ssm_act_back, [dxc, xbc], [sp["ssm_conv_w"], sp["ssm_conv_b"]],
                                                [BF16], [SSM_K, 1], tc)

    delta_t = _attn_delta(cfg, dab_out, a_out, early_token)
    dQ, dK, dV = _attn_bwd(cfg, Qh, Kh, Vh, dab_out, lse_t, delta_t)
    dq, dkv, dkr = _mla_unpack(cfg, dQ, dK, dV, cos2, sin2)
    dcqn = _matmul("mm_uq_dx", dq, W["w_uq"], "nt", F32)
    dckvn = _matmul("mm_ukv_dx", dkv, W["w_ukv"], "nt", F32)
    gW["w_uq"] = _matmul("mm_uq_dw", cqn, dq, "tn", BF16)
    gW["w_ukv"] = _matmul("mm_ukv_dw", ckvn, dkv, "tn", BF16)

    def rms_back(x, dy, g):
        r = _rs(x)
        xh = x * r
        return _rms_back(xh, r, dy * g), _colsum(dy * xh)

    dc_q, g_q_norm = _rowwise("rms_q_bwd", rms_back, [c_q, dcqn], [sp["q_norm_g"]], [(cfg.QL, BF16)], [(1, cfg.QL)], ts)
    dc_kv, g_kv_norm = _rowwise("rms_kv_bwd", rms_back, [c_kv, dckvn], [sp["kv_norm_g"]], [(cfg.KVL, BF16)], [(1, cfg.KVL)], ts)

    du = dict(c_q=dc_q, c_kv=dc_kv, kr=dkr, z=dz.astype(BF16), xbc=dxbc, dt=ddt_raw.astype(BF16))
    du = jnp.concatenate([du[n] for n in sorted(du, key=lambda n: cfg.seg[n][0])], axis=1)
    assert du.shape[1] == cfg.EXT, "the layout of u has gaps"
    gW["w_in"] = _matmul("mm_in_dw", du, xn, "tn", BF16)
    if in_grad_ready is None:
        dxn = _matmul("mm_in_dx", du, W["w_in"], "nn", F32)
    else:
        token = in_grad_ready({n: gW[n] for n in ("w_in", "w_uq", "w_ukv")}, None)
        dxn = _matmul("mm_in_dx", du, W["w_in"], "nn", F32, after=token)
        sp = dict(sp, mix_pre_g=sp["mix_pre_g"] + in_grad_ready(None, dxn)[0, 0])

    def first_back(dx1, dxn, x, g):
        r = _rs(x)
        xh = x * r
        return dx1 + _rms_back(xh, r, dxn * g), _colsum(dxn * xh)

    grad_x, g_mix_pre = _rowwise("bwd_first", first_back, [dx1, dxn, x], [sp["mix_pre_g"]], [(D, F32)], [(1, D)], ts)

    gs = dict(mix_pre_g=g_mix_pre, q_norm_g=g_q_norm, kv_norm_g=g_kv_norm, ssm_conv_w=g_ssm_conv_w, ssm_conv_b=g_ssm_conv_b,
              dt_bias=g_dt_bias[:, :cfg.HS], a_log=g_a_log[:, :cfg.HS], d_skip=g_d_skip[:, :cfg.HS], ssm_norm_g=g_ssm_norm,
              mix_post_g=g_mix_post, ffn_pre_g=g_ffn_pre, ffn_conv_w=g_ffn_conv_w, ffn_conv_b=g_ffn_conv_b,
              ffn_post_g=g_ffn_post)
    return loss, grad_x, gW, gs


def _to_kernel_layout(cfg, name, w):
    if name == "w_in":
        parts, at = [], 0
        for off, width, n_off, n_width in sorted(cfg.seg.values()):
            parts += [jnp.zeros((off - at, w.shape[1]), w.dtype), w[n_off:n_off + n_width],
                      jnp.zeros((width - n_width, w.shape[1]), w.dtype)]
            at = off + width
        parts.append(jnp.zeros((cfg.EXT - at, w.shape[1]), w.dtype))
        return jnp.concatenate([p for p in parts if p.shape[0]], axis=0)
    if name in ("w_uq", "w_ukv"):
        per = NOPE + (ROPE if name == "w_uq" else VH)
        return jnp.concatenate([w[:, h * per:h * per + NOPE] for h in range(cfg.H)]
                               + [w[:, h * per + NOPE:(h + 1) * per] for h in range(cfg.H)], axis=1)
    return w


def _from_kernel_layout(cfg, name, g):
    if name == "w_in":
        return jnp.concatenate([g[off:off + n_width] for off, _, _, n_width in sorted(cfg.seg.values(), key=lambda s: s[2])], axis=0)
    if name in ("w_uq", "w_ukv"):
        second = ROPE if name == "w_uq" else VH
        base = cfg.H * NOPE
        parts = []
        for h in range(cfg.H):
            parts += [g[:, h * NOPE:(h + 1) * NOPE], g[:, base + h * second:base + (h + 1) * second]]
        return jnp.concatenate(parts, axis=1)
    return g


_CHIP_MAJOR = ("w_gate", "w_up")
_RELAYOUT = ("w_uq", "w_ukv")
_LAYOUT_ROWS = 256
_CONV_COLS = 256


def _w_in_layout(cfg, wg):
    _, rs, d = wg.shape
    tc = _pick(d, _LAYOUT_ROWS, LANE)

    def body(w_ref, o_ref):
        o_ref[...] = _to_kernel_layout(cfg, "w_in", jnp.concatenate([w_ref[k] for k in range(N_CHIPS)], axis=0))

    return pl.pallas_call(
        body, name="layout_w_in", grid=(d // tc,),
        in_specs=[pl.BlockSpec((N_CHIPS, rs, tc), lambda j: (0, 0, j))], out_specs=pl.BlockSpec((cfg.EXT, tc), lambda j: (0, j)),
        out_shape=jax.ShapeDtypeStruct((cfg.EXT, d), wg.dtype), compiler_params=_params(("parallel",)),
    )(wg)


def _w_in_grad_to_chips(cfg, g):
    _, d = g.shape
    rs = cfg.IN_COLS // N_CHIPS
    tc = _pick(d, _LAYOUT_ROWS, LANE)

    def body(g_ref, o_ref):
        nat = _from_kernel_layout(cfg, "w_in", g_ref[...])
        for k in range(N_CHIPS):
            o_ref[k] = nat[k * rs:(k + 1) * rs]

    return pl.pallas_call(
        body, name="layout_grad_w_in", grid=(d // tc,),
        in_specs=[pl.BlockSpec((cfg.EXT, tc), lambda j: (0, j))], out_specs=pl.BlockSpec((N_CHIPS, rs, tc), lambda j: (0, 0, j)),
        out_shape=jax.ShapeDtypeStruct((N_CHIPS, rs, d), g.dtype), compiler_params=_params(("parallel",)),
    )(g)


def _gathered_to_kernel(cfg, name, wg):
    if name in _CHIP_MAJOR:
        return wg
    if name == "w_in":
        return _w_in_layout(cfg, wg)
    if name not in _RELAYOUT:
        return wg.reshape(wg.shape[0] * wg.shape[1], wg.shape[2])
    _, rows, cs = wg.shape
    tr = _pick(rows, _LAYOUT_ROWS, 16)

    def body(w_ref, o_ref):
        o_ref[...] = _to_kernel_layout(cfg, name, jnp.concatenate([w_ref[k] for k in range(N_CHIPS)], axis=1))

    wide = jax.eval_shape(lambda w: _to_kernel_layout(cfg, name, w), jax.ShapeDtypeStruct((rows, N_CHIPS * cs), wg.dtype)).shape[1]
    return pl.pallas_call(
        body, name="layout_" + name, grid=(rows // tr,),
        in_specs=[pl.BlockSpec((N_CHIPS, tr, cs), lambda i: (0, i, 0))], out_specs=pl.BlockSpec((tr, wide), lambda i: (i, 0)),
        out_shape=jax.ShapeDtypeStruct((rows, wide), wg.dtype), compiler_params=_params(("parallel",)),
    )(wg)


def _grad_to_chips(cfg, name, g):
    if name in _CHIP_MAJOR:
        return g
    if name == "w_in":
        return _w_in_grad_to_chips(cfg, g)
    if name not in _RELAYOUT:
        return g.reshape(N_CHIPS, g.shape[0] // N_CHIPS, g.shape[1])
    rows, wide = g.shape
    tr = _pick(rows, _LAYOUT_ROWS, 16)
    cs = jax.eval_shape(lambda v: _from_kernel_layout(cfg, name, v), g).shape[1] // N_CHIPS

    def body(g_ref, o_ref):
        nat = _from_kernel_layout(cfg, name, g_ref[...])
        for k in range(N_CHIPS):
            o_ref[k] = nat[:, k * cs:(k + 1) * cs]

    return pl.pallas_call(
        body, name="layout_grad_" + name, grid=(rows // tr,),
        in_specs=[pl.BlockSpec((tr, wide), lambda i: (i, 0))], out_specs=pl.BlockSpec((N_CHIPS, tr, cs), lambda i: (0, i, 0)),
        out_shape=jax.ShapeDtypeStruct((N_CHIPS, rows, cs), g.dtype), compiler_params=_params(("parallel",)),
    )(g)


def _me():
    return lax.axis_index("x"), lax.axis_index("y"), lax.axis_index("c")


def _other_chips(x, y):
    return [(1 - x, y), (x, 1 - y), (1 - x, 1 - y)]


_ANY = pl.BlockSpec(memory_space=pl.ANY)


BLOCK_ELEMS = 1 << 19
BLOCK_ELEMS_FEW = 1 << 20


def _row_block(rows, cols, mult, elems=BLOCK_ELEMS):
    return _pick(rows, max(mult, elems // cols // mult * mult), mult)


def _scalar(v):
    return v.astype(I32).reshape(1)


def _blocks2d(r, c, mult, elems=BLOCK_ELEMS):
    if r % mult == 0:
        tr = _row_block(r, c, mult, elems)
        return (tr, c), r // tr, lambda i: (i, 0)
    tc = _pick(c, max(LANE, elems // r // LANE * LANE), LANE)
    return (r, tc), c // tc, lambda i: (0, i)


def _by_rows(rows):
    return rows % 32 == 0


def _half_shape(rows, cols):
    return (rows // 2, cols) if _by_rows(rows) else (rows, cols // 2)


def _half_blocks(rows, cols, mult, elems=BLOCK_ELEMS):
    hr, hc = _half_shape(rows, cols)
    block, n, part = _blocks2d(hr, hc, mult, elems)
    assert (hr % mult == 0) == _by_rows(rows), (rows, cols, mult)
    full = (lambda h, i: (h * n + i, 0)) if _by_rows(rows) else (lambda h, i: (0, h * n + i))
    return block, n, full, part


def _half(ref, k, half):
    hr, hc = _half_shape(ref.shape[1], ref.shape[2])
    if _by_rows(ref.shape[1]):
        return ref.at[k, pl.ds(pl.multiple_of(half * hr, 16), hr), :]
    return ref.at[k, :, pl.ds(pl.multiple_of(half * hc, LANE), hc)]


def _shard_blocks(w, br, bc):
    if w.shape[0] == 1:
        def write(ref, v):
            ref[...] = v
        return (lambda f: pl.BlockSpec((None, br, bc), lambda *a: (0, *f(*a)))), (lambda ref: ref[...]), write
    assert w.shape[1] == 1 and br == w.shape[0], w.shape

    def write_rows(ref, v):
        ref[:, 0, :] = v
    return (lambda f: pl.BlockSpec((br, 1, bc), lambda *a: (0, 0, f(*a)[1]))), (lambda ref: ref[:, 0, :]), write_rows


def _stage_shard(name, w, chip, after=None):
    rs, cs = w.shape[0] * w.shape[1], w.shape[2]
    (br, bc), n, idx = _blocks2d(rs, cs, 16, BLOCK_ELEMS_FEW)
    spec, get, _ = _shard_blocks(w, br, bc)

    def body(chip_ref, w_ref, *refs):
        refs[-1][...] = get(w_ref).astype(BF16)

    return pl.pallas_call(
        body, name="stage_" + name,
        grid_spec=pltpu.PrefetchScalarGridSpec(
            num_scalar_prefetch=1, grid=(n,),
            in_specs=[spec(lambda i, chip_ref: idx(i))] + ([] if after is None else [_ANY]),
            out_specs=pl.BlockSpec((None, br, bc), lambda i, chip_ref: (chip_ref[0], *idx(i)))),
        out_shape=jax.ShapeDtypeStruct((N_CHIPS, rs, cs), BF16),
        compiler_params=_params(("parallel",)),
    )(_scalar(chip), w, *([] if after is None else [after]))


_HBM = pl.BlockSpec(memory_space=pltpu.HBM)
_SEM = pl.BlockSpec(memory_space=pltpu.SEMAPHORE)
_EFFECT = pltpu.SideEffectType.DATAFLOW_SIDE_EFFECTING


def _split_start(name, bufs, n_copies, copies, after):
    n = len(bufs)

    def body(*refs):
        for cp in copies(refs[:n], refs[n + 1], refs[n + 2]):
            cp.start()
        refs[-1][...] = jnp.zeros_like(refs[-1])

    res = pl.pallas_call(
        body, name=name,
        out_shape=(pltpu.SemaphoreType.DMA((n_copies,)), pltpu.SemaphoreType.DMA((n_copies,)),
                   *[pltpu.HBM(b.shape, b.dtype) for b in bufs], jax.ShapeDtypeStruct((8, LANE), F32)),
        in_specs=[_HBM] * n + [_ANY], out_specs=(_SEM, _SEM, *[_HBM] * n, pl.BlockSpec(memory_space=pltpu.VMEM)),
        input_output_aliases={i: 2 + i for i in range(n)},
        compiler_params=pltpu.CompilerParams(has_side_effects=_EFFECT),
    )(*[pltpu.with_memory_space_constraint(b, pltpu.HBM) for b in bufs], after)
    return res[0], res[1], list(res[2:2 + n]), res[-1]


def _split_wait(name, send_sems, recv_sems, bufs, after, copies):
    n = len(bufs)

    def body(*refs):
        for cp in copies(refs[:n], refs[n], refs[n + 1]):
            cp.wait_send()
            cp.wait_recv()

    return list(pl.pallas_call(
        body, name=name, out_shape=[pltpu.HBM(b.shape, b.dtype) for b in bufs],
        in_specs=[_HBM] * n + [_SEM, _SEM, _ANY], out_specs=[_HBM] * n,
        input_output_aliases={i: i for i in range(n)},
        compiler_params=pltpu.CompilerParams(has_side_effects=_EFFECT),
    )(*bufs, send_sems, recv_sems, after))


def _gather_to_chips(bufs, send_sems, recv_sems):
    x, y, c = _me()
    return [pltpu.make_async_remote_copy(src_ref=_half(b, 2 * x + y, c), dst_ref=_half(b, 2 * x + y, c),
                                         send_sem=send_sems.at[3 * w + j], recv_sem=recv_sems.at[3 * w + j],
                                         device_id=(cx, cy, c), device_id_type=MESH_ID)
            for w, b in enumerate(bufs) for j, (cx, cy) in enumerate(_other_chips(x, y))]


def _gather_to_sibling(bufs, send_sems, recv_sems):
    x, y, c = _me()
    return [pltpu.make_async_remote_copy(src_ref=_half(b, 2 * cx + cy, c), dst_ref=_half(b, 2 * cx + cy, c),
                                         send_sem=send_sems.at[3 * w + j], recv_sem=recv_sems.at[3 * w + j],
                                         device_id=(x, y, 1 - c), device_id_type=MESH_ID)
            for w, b in enumerate(bufs) for j, (cx, cy) in enumerate(_other_chips(x, y))]


def _pair_copies(grads, lands, send_sems, recv_sems):
    x, y, c = _me()
    return [pltpu.make_async_remote_copy(src_ref=_half(g_ref, slice(None), 1 - c), dst_ref=l_ref, send_sem=send_sems.at[w],
                                         recv_sem=recv_sems.at[w], device_id=(x, y, 1 - c), device_id_type=MESH_ID)
            for w, (g_ref, l_ref) in enumerate(zip(grads, lands))]


def _pair_exchange_start(name, grads):
    n = len(grads)
    lands = [lax.empty((g.shape[0], *_half_shape(g.shape[1], g.shape[2])), g.dtype) for g in grads]
    send_sems, recv_sems, bufs, token = _split_start(
        "pair_exchange_start_" + name, [*grads, *lands], n, lambda refs, ss, rs: _pair_copies(refs[:n], refs[n:], ss, rs),
        jnp.zeros((8, LANE), F32))
    return (send_sems, recv_sems, bufs), token


def _pair_exchange_wait(name, state, after):
    send_sems, recv_sems, bufs = state
    n = len(bufs) // 2
    bufs = _split_wait("pair_exchange_wait_" + name, send_sems, recv_sems, bufs, after,
                       lambda refs, ss, rs: _pair_copies(refs[:n], refs[n:], ss, rs))
    return bufs[:n], bufs[n:]


def _pair_sum(name, g, theirs, c):
    (br, bc), nb, full, part = _half_blocks(g.shape[1], g.shape[2], 16, 2 * BLOCK_ELEMS_FEW)

    def body(c_ref, a_ref, b_ref, o_ref):
        o_ref[...] = (a_ref[...].astype(F32) + b_ref[...].astype(F32)).astype(o_ref.dtype)

    return pl.pallas_call(
        body, name="pair_sum_" + name,
        grid_spec=pltpu.PrefetchScalarGridSpec(
            num_scalar_prefetch=1, grid=(N_CHIPS, nb),
            in_specs=[pl.BlockSpec((None, br, bc), lambda k, i, c_ref: (k, *full(c_ref[0], i))),
                      pl.BlockSpec((None, br, bc), lambda k, i, c_ref: (k, *part(i)))],
            out_specs=pl.BlockSpec((None, br, bc), lambda k, i, c_ref: (k, *part(i)))),
        out_shape=jax.ShapeDtypeStruct(theirs.shape, BF16),
        compiler_params=_params(("parallel", "parallel")),
    )(_scalar(c), g, theirs)


def _chip_copies(srcs, lands, send_sems, recv_sems):
    x, y, c = _me()
    return [pltpu.make_async_remote_copy(src_ref=s_ref.at[2 * cx + cy], dst_ref=l_ref.at[j], send_sem=send_sems.at[3 * w + j],
                                         recv_sem=recv_sems.at[3 * w + j], device_id=(cx, cy, c), device_id_type=MESH_ID)
            for w, (s_ref, l_ref) in enumerate(zip(srcs, lands)) for j, (cx, cy) in enumerate(_other_chips(x, y))]


def _chip_exchange_start(name, sums):
    n = len(sums)
    lands = [lax.empty((3,) + s.shape[1:], s.dtype) for s in sums]
    send_sems, recv_sems, bufs, token = _split_start(
        "chip_exchange_start_" + name, [*sums, *lands], 3 * n, lambda refs, ss, rs: _chip_copies(refs[:n], refs[n:], ss, rs),
        jnp.zeros((8, LANE), F32))
    return send_sems, recv_sems, bufs[:n], bufs[n:], token


def _chip_exchange_wait(name, send_sems, recv_sems, sums, lands, after):
    n = len(sums)
    bufs = _split_wait("chip_exchange_wait_" + name, send_sems, recv_sems, [*sums, *lands], after,
                       lambda refs, ss, rs: _chip_copies(refs[:n], refs[n:], ss, rs))
    return bufs[:n], bufs[n:]


def _chip_sum(name, sums, theirs, chip):
    _, h, cs = sums.shape
    (br, bc), nb, idx = _blocks2d(h, cs, 16, BLOCK_ELEMS_FEW)

    def body(chip_ref, s_ref, t_ref, o_ref):
        acc = s_ref[...].astype(F32)
        for k in range(3):
            acc = acc + t_ref[k].astype(F32)
        o_ref[...] = acc

    return pl.pallas_call(
        body, name="chip_sum_" + name,
        grid_spec=pltpu.PrefetchScalarGridSpec(
            num_scalar_prefetch=1, grid=(nb,),
            in_specs=[pl.BlockSpec((None, br, bc), lambda i, chip_ref: (chip_ref[0], *idx(i))),
                      pl.BlockSpec((3, br, bc), lambda i, chip_ref: (0, *idx(i)))],
            out_specs=pl.BlockSpec((br, bc), lambda i, chip_ref: idx(i))),
        out_shape=jax.ShapeDtypeStruct((h, cs), F32),
        compiler_params=_params(("parallel",)),
    )(_scalar(chip), sums, theirs)


def _sibling_copies(halves, lands, send_sems, recv_sems):
    x, y, c = _me()
    return [pltpu.make_async_remote_copy(src_ref=h_ref, dst_ref=l_ref, send_sem=send_sems.at[w], recv_sem=recv_sems.at[w],
                                         device_id=(x, y, 1 - c), device_id_type=MESH_ID)
            for w, (h_ref, l_ref) in enumerate(zip(halves, lands))]


def _sibling_exchange_start(name, halves, after):
    n = len(halves)
    lands = [lax.empty(h.shape, h.dtype) for h in halves]
    send_sems, recv_sems, bufs, token = _split_start(
        "sibling_exchange_start_" + name, [*halves, *lands], n, lambda refs, ss, rs: _sibling_copies(refs[:n], refs[n:], ss, rs),
        after)
    return (send_sems, recv_sems, bufs), token


def _sibling_exchange_wait(name, state, after):
    send_sems, recv_sems, bufs = state
    n = len(bufs) // 2
    bufs = _split_wait("sibling_exchange_wait_" + name, send_sems, recv_sems, bufs, after,
                       lambda refs, ss, rs: _sibling_copies(refs[:n], refs[n:], ss, rs))
    return bufs[:n], bufs[n:]


def _sibling_exchange(name, halves):
    n = len(halves)

    def body(*refs):
        ins, outs, send_sems, recv_sems = refs[:n], refs[n:2 * n], refs[2 * n], refs[2 * n + 1]
        x, y, c = _me()
        cps = []
        for w, (h_ref, o_ref) in enumerate(zip(ins, outs)):
            cps.append(pltpu.make_async_remote_copy(src_ref=h_ref, dst_ref=o_ref, send_sem=send_sems.at[w], recv_sem=recv_sems.at[w],
                                                    device_id=(x, y, 1 - c), device_id_type=MESH_ID))
            cps[-1].start()
        for cp in cps:
            cp.wait()

    return pl.pallas_call(
        body, name="sibling_exchange_" + name, in_specs=[_ANY] * n, out_specs=[_ANY] * n,
        out_shape=[jax.ShapeDtypeStruct(h.shape, h.dtype) for h in halves],
        scratch_shapes=[pltpu.SemaphoreType.DMA((n,)), pltpu.SemaphoreType.DMA((n,))],
    )(*halves)


N_DEV = 8


def _peer_copies(bufs, send_sems, recv_sems):
    vec, land = bufs
    x, y, c = _me()
    return [pltpu.make_async_remote_copy(src_ref=vec, dst_ref=land.at[4 * x + 2 * y + c], send_sem=send_sems.at[p - 1],
                                         recv_sem=recv_sems.at[p - 1], device_id=(x ^ (p >> 2), y ^ ((p >> 1) & 1), c ^ (p & 1)),
                                         device_id_type=MESH_ID) for p in range(1, N_DEV)]


def _allreduce_small_start(vec, after):
    land = jnp.zeros((N_DEV,) + vec.shape, F32)
    send_sems, recv_sems, bufs, _ = _split_start("allreduce_small_start", [vec, land], N_DEV - 1, _peer_copies, after)
    return send_sems, recv_sems, bufs


def _allreduce_small_wait(state, chip, core, after):
    send_sems, recv_sems, bufs = state
    vec, land = _split_wait("allreduce_small_wait", send_sems, recv_sems, bufs, after, _peer_copies)

    def body(me_ref, v_ref, l_ref, o_ref):
        acc = None
        for k in range(N_DEV):
            term = jnp.where(me_ref[0] == k, v_ref[...], l_ref[k])
            acc = term if acc is None else acc + term
        o_ref[...] = acc

    return pl.pallas_call(
        body, name="allreduce_small_sum",
        grid_spec=pltpu.PrefetchScalarGridSpec(
            num_scalar_prefetch=1, grid=(1,),
            in_specs=[pl.BlockSpec(vec.shape, lambda i, me_ref: (0, 0)), pl.BlockSpec(land.shape, lambda i, me_ref: (0, 0, 0))],
            out_specs=pl.BlockSpec(vec.shape, lambda i, me_ref: (0, 0))),
        out_shape=jax.ShapeDtypeStruct(vec.shape, F32), compiler_params=_params(("arbitrary",)),
    )(_scalar(2 * chip + core), vec, land)


def _adam_math(w, g, m, v):
    m = ADAM_B1 * m + (1.0 - ADAM_B1) * g
    v = ADAM_B2 * v + (1.0 - ADAM_B2) * (g * g)
    m_hat = m / (1.0 - ADAM_B1 ** ADAM_STEP)
    v_hat = v / (1.0 - ADAM_B2 ** ADAM_STEP)
    return -ADAM_LR * (m_hat / (jnp.sqrt(v_hat) + ADAM_EPS) + ADAM_WD * w), m, v


def _adamw(name, w, g, m, v):
    R, C = w.shape
    tr = _row_block(R, C, 8)

    def body(w_ref, g_ref, m_ref, v_ref, d_ref, nm_ref, nv_ref):
        d_ref[...], nm_ref[...], nv_ref[...] = _adam_math(w_ref[...], g_ref[...], m_ref[...], v_ref[...])

    blk = pl.BlockSpec((tr, C), lambda i: (i, 0))
    return pl.pallas_call(
        body, name=name, grid=(R // tr,), in_specs=[blk] * 4, out_specs=[blk] * 3,
        out_shape=[jax.ShapeDtypeStruct((R, C), F32)] * 3, compiler_params=_params(("parallel",)),
    )(w, g, m, v)


def _adamw_halves(name, w, mine, theirs, m, v, c):
    rs, cs = w.shape[0] * w.shape[1], w.shape[2]
    (br, bc), nb, whole, half = _half_blocks(rs, cs, 8)
    spec, get, put = _shard_blocks(w, br, bc)

    def body(c_ref, w_ref, a_ref, b_ref, m_ref, v_ref, g_ref, d_ref, nm_ref, nv_ref):
        g = jnp.where(pl.program_id(0) == c_ref[0], a_ref[...], b_ref[...])
        put(g_ref, g)
        for ref, val in zip((d_ref, nm_ref, nv_ref), _adam_math(get(w_ref), g, get(m_ref), get(v_ref))):
            put(ref, val)

    full = spec(lambda s, i, c_ref: whole(s, i))
    part = pl.BlockSpec((br, bc), lambda s, i, c_ref: half(i))
    return pl.pallas_call(
        body, name=name,
        grid_spec=pltpu.PrefetchScalarGridSpec(num_scalar_prefetch=1, grid=(2, nb), in_specs=[full, part, part, full, full],
                                               out_specs=[full] * 4),
        out_shape=[jax.ShapeDtypeStruct(w.shape, F32)] * 4, compiler_params=_params(("parallel", "parallel")),
    )(_scalar(c), w, mine, theirs, m, v)


def _pack_small(arrs, lanes=LANE):
    flat = jnp.concatenate([a.reshape(-1) for a in arrs])
    n = -(-flat.shape[0] // (8 * lanes)) * 8 * lanes
    return jnp.pad(flat, (0, n - flat.shape[0])).reshape(8, n // 8)


def _unpack_small(vec, shapes):
    flat, out, off = vec.reshape(-1), [], 0
    for s in shapes:
        out.append(flat[off:off + s[0] * s[1]].reshape(s))
        off += s[0] * s[1]
    return out


class _LateWeights:
    def __init__(self, cfg, tag, names, staged, after):
        self.cfg, self.tag, self.names, self.k = cfg, tag, names, 3 * len(names)
        self.send, self.recv, self.bufs, self.token = _split_start(f"gather_{tag}_chips_start", staged, self.k, _gather_to_chips,
                                                                    after)

    def pass_on(self, after):
        bufs = _split_wait(f"gather_{self.tag}_chips_wait", self.send, self.recv, self.bufs, after, _gather_to_chips)
        self.send, self.recv, self.bufs, token = _split_start(f"gather_{self.tag}_sibling_start", bufs, self.k, _gather_to_sibling,
                                                               self.token)
        return token

    def arrived(self, after):
        bufs = _split_wait(f"gather_{self.tag}_sibling_wait", self.send, self.recv, self.bufs, after, _gather_to_sibling)
        return {n: _gathered_to_kernel(self.cfg, n, b) for n, b in zip(self.names, bufs)}


def _step(cfg, a):
    chip = 2 * lax.axis_index("x") + lax.axis_index("y")
    core = lax.axis_index("c")
    big = BIG

    ffn = ("w_gate", "w_up", "w_down")
    first = ("w_in", "w_uq", "w_ukv")
    sp = {n: a[n] for n in SMALL}
    sharded = _pack_small([a[n] for n in SMALL_SHARDED], 2 * LANE)
    slabs = jnp.where(lax.broadcasted_iota(I32, (N_CHIPS,) + sharded.shape, 0) == chip, sharded[None], 0.0)
    staged = {"w_in": _stage_shard("w_in", a["w_in"], chip)}
    in_weight = _LateWeights(cfg, "in", ("w_in", "sharded_small"), [staged["w_in"], slabs], jnp.zeros((8, LANE), F32))
    behind = in_weight.token
    for n in big[1:]:
        behind = staged[n] = _stage_shard(n, a[n], chip, behind)
    in_weight.pass_on(behind)
    xn_early = _rms_pre(cfg, a["x"], sp["mix_pre_g"] + in_weight.token[0, 0])
    W = in_weight.arrived(xn_early)
    allp = W.pop("sharded_small").reshape((N_CHIPS,) + sharded.shape)
    per_chip = [_unpack_small(allp[ch], [a[n].shape for n in SMALL_SHARDED]) for ch in range(N_CHIPS)]
    for k, n in enumerate(SMALL_SHARDED):
        sp[n] = jnp.concatenate([per_chip[ch][k] for ch in range(N_CHIPS)], axis=1)

    mla_weights = _LateWeights(cfg, "mla", first[1:], [staged[n] for n in first[1:]], W["w_in"])
    out_weight = _LateWeights(cfg, "out", ("w_out",), [staged["w_out"]], mla_weights.token)
    ffn_weights = _LateWeights(cfg, "ffn", ffn[:2], [staged[n] for n in ffn[:2]], out_weight.token)
    down_weight = _LateWeights(cfg, "down", ffn[2:], [staged[n] for n in ffn[2:]], ffn_weights.token)

    state = {}

    def ffn_grads_ready(grads):
        state["ffn_pairs"], token = _pair_exchange_start("ffn", [_grad_to_chips(cfg, n, grads[n]) for n in ffn_grads])
        return token

    def pair_sums(names, grads, theirs):
        return [_pair_sum(n, g, t, core) for n, g, t in zip(names, grads, theirs)]

    def early_grads_ready(grads):
        out_pairs, token = _pair_exchange_start("out", [_grad_to_chips(cfg, "w_out", grads["w_out"])])
        sums = pair_sums(ffn_grads, *_pair_exchange_wait("ffn", state["ffn_pairs"], token))
        sums += pair_sums(["w_out"], *_pair_exchange_wait("out", out_pairs, sums[-1]))
        state["early"] = _chip_exchange_start("early", sums)
        return state["early"][-1]

    def reduced_halves(tag, names, after):
        send_sems, recv_sems, s_bufs, l_bufs, _ = state[tag]
        s_bufs, l_bufs = _chip_exchange_wait(tag, send_sems, recv_sems, s_bufs, l_bufs, after)
        return [_chip_sum(n, s, t, chip) for n, s, t in zip(names, s_bufs, l_bufs)]

    def in_grad_ready(grads, after):
        if grads is not None:
            state["rest_pairs"], token = _pair_exchange_start("rest", [_grad_to_chips(cfg, n, grads[n]) for n in first])
            return token
        state["rest"] = _chip_exchange_start("rest", pair_sums(first, *_pair_exchange_wait("rest", state["rest_pairs"], after)))
        return state["rest"][-1]

    ffn_grads = ("w_down", "w_gate", "w_up")
    early = ffn_grads + ("w_out",)
    loss, grad_x, gW, gs = _local_grads(cfg, a["x"], a["loss_target"], W, sp, mla_weights, out_weight, ffn_weights, down_weight,
                                        ffn_grads_ready, early_grads_ready, in_grad_ready, xn_early, down_weight.token)
    out = {"grad_x": grad_x}

    def adamw(names, mine, theirs):
        for n, gm, gt in zip(names, mine, theirs):
            out["grad_" + n], out["delta_" + n], out["new_m_" + n], out["new_v_" + n] = _adamw_halves(
                "adamw_" + n, a[n], gm, gt, a["m_" + n], a["v_" + n], core)

    mine = reduced_halves("early", early, grad_x)
    theirs = _sibling_exchange("early", mine[:1])
    later, _ = _sibling_exchange_start("early", mine[1:], theirs[0])
    adamw(early[:1], mine[:1], theirs)
    e_mine, e_theirs = _sibling_exchange_wait("early", later, out["new_v_" + early[0]])
    adamw(early[3:], e_mine[2:], e_theirs[2:])
    mine = reduced_halves("rest", first, out["new_v_" + early[-1]])
    rest, token = _sibling_exchange_start("rest", mine, mine[0])
    small = _allreduce_small_start(_pack_small([gs[n] for n in SMALL] + [loss]), token)
    adamw(early[1:3], e_mine[:2], e_theirs[:2])
    adamw(first, *_sibling_exchange_wait("rest", rest, out["new_v_" + early[2]]))
    shapes = [gs[n].shape for n in SMALL] + [(1, LANE)]
    red = _unpack_small(_allreduce_small_wait(small, chip, core, out["new_v_" + first[-1]]), shapes)
    g_small = dict(zip(SMALL, red[:-1]))
    for n in SMALL_SHARDED:
        cs = a[n].shape[1]
        g_small[n] = lax.dynamic_slice_in_dim(g_small[n], chip * cs, cs, axis=1)
    out["loss"] = red[-1][0, 0]
    sshapes = [a[n].shape for n in SMALL]
    d, nm, nv = _adamw("adamw_small", _pack_small([a[n] for n in SMALL]), _pack_small([g_small[n] for n in SMALL]),
                       _pack_small([a["m_" + n] for n in SMALL]), _pack_small([a["v_" + n] for n in SMALL]))
    for n, dd, mm, vv in zip(SMALL, _unpack_small(d, sshapes), _unpack_small(nm, sshapes), _unpack_small(nv, sshapes)):
        out["grad_" + n], out["delta_" + n], out["new_m_" + n], out["new_v_" + n] = g_small[n], dd, mm, vv
    return out


def kernel(x, mix_pre_g, w_in, q_norm_g, w_uq, kv_norm_g, w_ukv, ssm_conv_w, ssm_conv_b, dt_bias, a_log, d_skip, ssm_norm_g, w_out, mix_post_g, ffn_pre_g, w_gate, w_up, ffn_conv_w, ffn_conv_b, w_down, ffn_post_g, loss_target, m_mix_pre_g, m_w_in, m_q_norm_g, m_w_uq, m_kv_norm_g, m_w_ukv, m_ssm_conv_w, m_ssm_conv_b, m_dt_bias, m_a_log, m_d_skip, m_ssm_norm_g, m_w_out, m_mix_post_g, m_ffn_pre_g, m_w_gate, m_w_up, m_ffn_conv_w, m_ffn_conv_b, m_w_down, m_ffn_post_g, v_mix_pre_g, v_w_in, v_q_norm_g, v_w_uq, v_kv_norm_g, v_w_ukv, v_ssm_conv_w, v_ssm_conv_b, v_dt_bias, v_a_log, v_d_skip, v_ssm_norm_g, v_w_out, v_mix_post_g, v_ffn_pre_g, v_w_gate, v_w_up, v_ffn_conv_w, v_ffn_conv_b, v_w_down, v_ffn_post_g):
    args = dict(locals())
    def given(k, v):
        if k in ("w_in", "m_w_in", "v_w_in"):
            return jnp.transpose(v, (2, 0, 1))
        return v if k.removeprefix("m_").removeprefix("v_") in BIG or v.ndim < 3 else v[0]

    out = _step(_FULL, {k: given(k, v) for k, v in args.items()})
    res = [out["loss"], out["grad_x"][None]]
    for pre in ("grad_", "delta_", "new_m_", "new_v_"):
        for n in WEIGHTS:
            o = out[pre + n]
            res.append(jnp.transpose(o, (1, 2, 0)) if n == "w_in" else o if n in BIG or args[n].ndim < 3 else o[None])
    return tuple(res)
```

```python
import math

import jax
import jax.numpy as jnp
from jax import lax
from jax.experimental import pallas as pl
from jax.experimental.pallas import tpu as pltpu

F32, BF16, I32 = jnp.float32, jnp.bfloat16, jnp.int32
NN = (((1,), (0,)), ((), ()))
NT = (((1,), (1,)), ((), ()))
TN = (((0,), (0,)), ((), ()))
HI = lax.Precision.HIGHEST
MESH_ID = pl.DeviceIdType.MESH

EPS = 1e-6
CHUNK = 64
NOPE, ROPE, VH = 128, 64, 128
ROPE_THETA = 10000.0
HP, NST = 64, 128
SSM_K, FFN_K = 4, 3
LANE = 128
N_CHIPS = 4
VMEM_LIMIT = 52 * 1024 * 1024
MM_TILE, MM_TILE_K = 1408, 2816

ADAM_LR, ADAM_B1, ADAM_B2, ADAM_EPS, ADAM_WD, ADAM_STEP = 0.001, 0.9, 0.999, 1e-08, 0.01, 10


class _Cfg:
    def __init__(self, S, D, QL, KVL, H, HS, G, DFF, T):
        self.S, self.D, self.QL, self.KVL, self.H, self.HS, self.G, self.DFF, self.T = S, D, QL, KVL, H, HS, G, DFF, T
        self.INNER = HS * HP
        self.CONVCH = self.INNER + 2 * G * NST
        self.QW = H * (NOPE + ROPE)
        self.KVW = H * (NOPE + VH)
        self.MLAW = H * VH
        self.MIXW = self.MLAW + self.INNER
        self.IN_COLS = QL + KVL + ROPE + self.INNER + self.CONVCH + HS
        natural, at = {}, 0
        for name, w in (("c_q", QL), ("c_kv", KVL), ("kr", ROPE), ("z", self.INNER), ("xbc", self.CONVCH), ("dt", HS)):
            natural[name] = (at, w)
            at += w
        self.seg, taken = {}, []
        for name in sorted(natural, key=lambda n: -natural[n][1]):
            w = -(-natural[name][1] // LANE) * LANE
            off = next(o for o in range(0, self.IN_COLS * 2, w) if all(o + w <= t or o >= t + tw for t, tw in taken))
            taken.append((off, w))
            self.seg[name] = (off, w) + natural[name]
        self.EXT = max(o + w for o, w in taken)
        self.NPAIR = HS // 2
        self.REP = HS // G

    def window(self, name):
        off, w, _, _ = self.seg[name]
        return w, off // w


_FULL = _Cfg(S=2048, D=2048, QL=768, KVL=512, H=8, HS=16, G=2, DFF=5632, T=256)
BIG = ("w_in", "w_uq", "w_ukv", "w_out", "w_gate", "w_up", "w_down")

SMALL = ("mix_pre_g", "q_norm_g", "kv_norm_g", "ssm_conv_w", "ssm_conv_b", "dt_bias", "a_log", "d_skip", "ssm_norm_g",
         "mix_post_g", "ffn_pre_g", "ffn_conv_w", "ffn_conv_b", "ffn_post_g")
SMALL_SHARDED = ("ssm_conv_w", "ffn_conv_w")
WEIGHTS = ("mix_pre_g", "w_in", "q_norm_g", "w_uq", "kv_norm_g", "w_ukv", "ssm_conv_w", "ssm_conv_b", "dt_bias", "a_log",
           "d_skip", "ssm_norm_g", "w_out", "mix_post_g", "ffn_pre_g", "w_gate", "w_up", "ffn_conv_w", "ffn_conv_b",
           "w_down", "ffn_post_g")


def _pick(n, target, mult):
    best = None
    for d in range(mult, min(n, target) + 1, mult):
        if n % d == 0:
            best = d
    return best if best is not None else n


def _params(sem=None):
    kw = dict(vmem_limit_bytes=VMEM_LIMIT)
    if sem is not None:
        kw["dimension_semantics"] = sem
    return pltpu.CompilerParams(**kw)


def _dot(a, b, dims=NN, precision=None):
    return lax.dot_general(a, b, dims, preferred_element_type=F32, precision=precision)


def _sigmoid(x):
    return 1.0 / (1.0 + jnp.exp(-x))


def _rs(x):
    return lax.rsqrt(jnp.mean(x * x, axis=-1, keepdims=True) + EPS)


def _rms_back(xh, r, dn):
    return r * (dn - xh * jnp.mean(dn * xh, axis=-1, keepdims=True))


def _colsum(v):
    return jnp.sum(v, axis=0, keepdims=True)


def _matmul(name, a, b, mode, out_dtype, a2=None, b2=None, chips=False, after=None):
    cs = None
    if mode == "nn":
        (M, K), N = a.shape, b.shape[-1]
        if chips:
            cs, N = N, N_CHIPS * N
    elif mode == "nt":
        (M, K), N = a.shape, b.shape[-2]
        if chips:
            cs = b.shape[-1]
    else:
        (K, M), N = a.shape, b.shape[1]
        if chips:
            cs = N // N_CHIPS
    tm = _pick(M, MM_TILE, LANE)
    tn = _pick(cs if chips and mode != "nt" else N, MM_TILE, LANE)
    tk = _pick(cs, MM_TILE, LANE) if chips and mode == "nt" else _pick(K, MM_TILE_K, LANE)
    nk = K // tk
    dims = {"nn": NN, "nt": NT, "tn": TN}[mode]
    a_spec = pl.BlockSpec((tk, tm), lambda i, j, k: (k, i)) if mode == "tn" else pl.BlockSpec((tm, tk), lambda i, j, k: (i, k))
    b_spec = pl.BlockSpec((tn, tk), lambda i, j, k: (j, k)) if mode == "nt" else pl.BlockSpec((tk, tn), lambda i, j, k: (k, j))
    o_spec = pl.BlockSpec((tm, tn), lambda i, j, k: (i, j))
    o_shape = (M, N)
    if chips and mode == "nn":
        per = cs // tn
        b_spec = pl.BlockSpec((None, tk, tn), lambda i, j, k: (j // per, k, j % per))
    elif chips and mode == "nt":
        per = cs // tk
        b_spec = pl.BlockSpec((None, tn, tk), lambda i, j, k: (k // per, j, k % per))
    elif chips:
        per = cs // tn
        o_spec = pl.BlockSpec((None, tm, tn), lambda i, j, k: (j // per, i, j % per))
        o_shape = (N_CHIPS, M, cs)
    two = a2 is not None

    def product(refs):
        part = _dot(refs[0][...].astype(BF16), refs[1][...].astype(BF16), dims)
        if two:
            part += _dot(refs[2][...].astype(BF16), refs[3][...].astype(BF16), dims)
        return part

    def body_whole_k(*refs):
        refs[-1][...] = product(refs).astype(refs[-1].dtype)

    def body(*refs):
        o_ref, acc_ref = refs[-2], refs[-1]
        k = pl.program_id(2)

        @pl.when(k == 0)
        def _():
            acc_ref[...] = product(refs)

        @pl.when(k > 0)
        def _():
            acc_ref[...] += product(refs)

        @pl.when(k == nk - 1)
        def _():
            o_ref[...] = acc_ref[...].astype(o_ref.dtype)

    ins = ((a, b, a2, b2) if two else (a, b)) + (() if after is None else (after,))
    return pl.pallas_call(
        body_whole_k if nk == 1 else body, name=name, grid=(M // tm, N // tn, nk),
        in_specs=[a_spec, b_spec] * (2 if two else 1) + ([] if after is None else [pl.BlockSpec(memory_space=pl.ANY)]),
        out_specs=o_spec,
        out_shape=jax.ShapeDtypeStruct(o_shape, out_dtype),
        scratch_shapes=[] if nk == 1 else [pltpu.VMEM((tm, tn), F32)],
        compiler_params=_params(("parallel", "parallel", "arbitrary")),
    )(*ins)


def _matmul_twin(name, a, b1, b2, mode, out_dtype):
    if mode == "nn":
        (M, K), cs = a.shape, b1.shape[-1]
        tm = _pick(M, MM_TILE // 2, LANE)
    else:
        (K, M), cs = a.shape, b1.shape[1] // N_CHIPS
        tm = _pick(M, MM_TILE, LANE)
    tn = _pick(cs, MM_TILE, LANE)
    per = cs // tn
    dims = NN if mode == "nn" else TN

    def body(a_ref, b1_ref, b2_ref, o1_ref, o2_ref):
        lhs = a_ref[...].astype(BF16)
        o1_ref[...] = _dot(lhs, b1_ref[...].astype(BF16), dims).astype(o1_ref.dtype)
        o2_ref[...] = _dot(lhs, b2_ref[...].astype(BF16), dims).astype(o2_ref.dtype)

    if mode == "nn":
        a_spec = pl.BlockSpec((tm, K), lambda i, j: (i, 0))
        b_spec = pl.BlockSpec((None, K, tn), lambda i, j: (j // per, 0, j % per))
        o_spec, o_shape = pl.BlockSpec((tm, tn), lambda i, j: (i, j)), (M, N_CHIPS * cs)
    else:
        a_spec = pl.BlockSpec((K, tm), lambda i, j: (0, i))
        b_spec = pl.BlockSpec((K, tn), lambda i, j: (0, j))
        o_spec, o_shape = pl.BlockSpec((None, tm, tn), lambda i, j: (j // per, i, j % per)), (N_CHIPS, M, cs)
    return pl.pallas_call(
        body, name=name, grid=(M // tm, N_CHIPS * per), in_specs=[a_spec, b_spec, b_spec], out_specs=[o_spec, o_spec],
        out_shape=[jax.ShapeDtypeStruct(o_shape, out_dtype)] * 2, compiler_params=_params(("parallel", "parallel")),
    )(a, b1, b2)


def _window(a):
    return (a[0], *a[1]) if isinstance(a, tuple) else (a, a.shape[1], 0)


def _rowwise(name, fn, rows, mats, outs, reds, ts):
    rows, widths, blocks = zip(*[_window(a) for a in rows])
    S = rows[0].shape[0]
    nr, nm, no = len(rows), len(mats), len(outs)

    def body(*refs):
        res = fn(*[r[...] for r in refs[:nr + nm]])
        res = res if isinstance(res, (tuple, list)) else (res,)
        for r, v in zip(refs[nr + nm:nr + nm + no], res[:no]):
            r[...] = v.astype(r.dtype)
        first = pl.program_id(0) == 0
        for r, v in zip(refs[nr + nm + no:], res[no:]):
            @pl.when(first)
            def _():
                r[...] = jnp.broadcast_to(v, r.shape)

            @pl.when(jnp.logical_not(first))
            def _():
                r[...] += jnp.broadcast_to(v, r.shape)

    in_specs = [pl.BlockSpec((ts, w), lambda i, b=b: (i, b)) for w, b in zip(widths, blocks)]
    in_specs += [pl.BlockSpec(m.shape, lambda i, nd=m.ndim: (0,) * nd) for m in mats]
    out_specs = [pl.BlockSpec((ts, w), lambda i: (i, 0)) for w, _ in outs]
    out_specs += [pl.BlockSpec(s, lambda i: (0, 0)) for s in reds]
    out_shape = [jax.ShapeDtypeStruct((S, w), dt) for w, dt in outs] + [jax.ShapeDtypeStruct(s, F32) for s in reds]
    return pl.pallas_call(
        body, name=name, grid=(S // ts,), in_specs=in_specs, out_specs=out_specs, out_shape=out_shape,
        compiler_params=_params(("arbitrary",) if reds else ("parallel",)),
    )(*rows, *mats)


def _shift_down(v, s):
    if s == 0:
        return v
    rows = lax.broadcasted_iota(I32, v.shape, 0)
    return jnp.where(rows >= s, pltpu.roll(v, s, 0), 0.0)


def _shift_up(v, s):
    if s == 0:
        return v
    n = v.shape[0]
    rows = lax.broadcasted_iota(I32, v.shape, 0)
    return jnp.where(rows < n - s, pltpu.roll(v, n - s, 0), 0.0)


def _conv(x, w, b):
    K = w.shape[0]
    y = jnp.broadcast_to(b, x.shape)
    for k in range(K):
        y = y + w[k:k + 1, :] * _shift_down(x, K - 1 - k)
    return y


def _conv_back(x, w, dc):
    K = w.shape[0]
    dx = jnp.zeros_like(x)
    dw = []
    for k in range(K):
        up = _shift_up(dc, K - 1 - k)
        dx = dx + w[k:k + 1, :] * up
        dw.append(_colsum(up * x))
    return dx, jnp.concatenate(dw, axis=0), _colsum(dc)


def _colwise(name, fn, cols, vecs, outs, pouts, tc):
    cols, widths, blocks = zip(*[_window(a) for a in cols])
    S, C = cols[0].shape[0], widths[0]
    firsts = [b * (C // tc) for b in blocks]
    nc_, nv, no = len(cols), len(vecs), len(outs)

    def body(*refs):
        res = fn(*[r[...] for r in refs[:nc_ + nv]])
        res = res if isinstance(res, (tuple, list)) else (res,)
        for r, v in zip(refs[nc_ + nv:], res):
            r[...] = v.astype(r.dtype)

    in_specs = [pl.BlockSpec((S, tc), lambda j, f=f: (0, f + j)) for f in firsts]
    in_specs += [pl.BlockSpec((v.shape[0], tc), lambda j: (0, j)) for v in vecs]
    out_specs = [pl.BlockSpec((S, tc), lambda j: (0, j)) for _ in outs] + [pl.BlockSpec((k, tc), lambda j: (0, j)) for k in pouts]
    out_shape = [jax.ShapeDtypeStruct((S, C), dt) for dt in outs] + [jax.ShapeDtypeStruct((k, C), F32) for k in pouts]
    return pl.pallas_call(
        body, name=name, grid=(C // tc,), in_specs=in_specs, out_specs=out_specs, out_shape=out_shape,
        compiler_params=_params(("parallel",)),
    )(*cols, *vecs)


_G0, _G1 = math.sqrt(2.0 / math.pi), 0.044715


def _gelu(g):
    th = jnp.tanh(_G0 * (g + _G1 * g * g * g))
    return 0.5 * g * (1.0 + th), th


def _ffn_act(gate_pre, up, w, b):
    act, _ = _gelu(_conv(gate_pre, w, b))
    return act * up


def _ffn_act_back(dact, gate_pre, up, w, b):
    g = _conv(gate_pre, w, b)
    ge, th = _gelu(g)
    dge = 0.5 * (1.0 + th) + 0.5 * g * (1.0 - th * th) * _G0 * (1.0 + 3.0 * _G1 * g * g)
    dup = dact * ge
    dgate_pre, dw, db = _conv_back(gate_pre, w, dact * up * dge)
    return dgate_pre, dup, dw, db


def _ssm_act(xbc, w, b):
    c = _conv(xbc, w, b)
    return c * _sigmoid(c)


def _ssm_act_back(dxc, xbc, w, b):
    c = _conv(xbc, w, b)
    sg = _sigmoid(c)
    return _conv_back(xbc, w, dxc * sg * (1.0 + c * (1.0 - sg)))


def _rope_tables(S):
    inv = 1.0 / (ROPE_THETA ** (jnp.arange(0, ROPE, 2, dtype=F32) / ROPE))
    ang = jnp.arange(S, dtype=F32)[:, None] * inv[None, :]
    cos, sin = jnp.cos(ang), jnp.sin(ang)
    return jnp.tile(cos, (1, 4)), jnp.tile(jnp.concatenate([-sin, sin], axis=1), (1, 2))


def _swap_halves(x):
    lane = lax.broadcasted_iota(I32, x.shape, 1)
    w = x.shape[1]
    return jnp.where((lane % ROPE) < ROPE // 2, pltpu.roll(x, w - ROPE // 2, 1), pltpu.roll(x, ROPE // 2, 1))


def _rot(x, cos2, sin2):
    return x * cos2 + _swap_halves(x) * sin2


def _rot_back(dy, cos2, sin2):
    return dy * cos2 + _swap_halves(dy * sin2)


def _mla_pack(cfg, q, kv, kr, cos2, sin2):
    S, H = cfg.S, cfg.H
    ts = _pick(S, 256, 8)
    kr, _, kr_block = _window(kr)

    def body(q_ref, kv_ref, kr_ref, c_ref, s_ref, Q_ref, K_ref, V_ref):
        c2, s2 = c_ref[...], s_ref[...]
        krr = _rot(kr_ref[...], c2, s2)
        kr_half = (krr.astype(BF16), pltpu.roll(krr, ROPE, 1).astype(BF16))
        for j in range(H // 2):
            qr = _rot(q_ref[:, (H + j) * LANE:(H + j + 1) * LANE], c2, s2).astype(BF16)
            for h in (2 * j, 2 * j + 1):
                Q_ref[h, :, 0:LANE] = q_ref[:, h * LANE:(h + 1) * LANE].astype(BF16)
                Q_ref[h, :, LANE:] = qr
                K_ref[h, :, 0:LANE] = kv_ref[:, h * LANE:(h + 1) * LANE].astype(BF16)
                K_ref[h, :, LANE:] = kr_half[h % 2]
                V_ref[h] = kv_ref[:, (H + h) * LANE:(H + h + 1) * LANE].astype(BF16)

    tab = pl.BlockSpec((ts, LANE), lambda i: (i, 0))
    heads = lambda w: pl.BlockSpec((H, ts, w), lambda i: (0, i, 0))
    return pl.pallas_call(
        body, name="mla_pack", grid=(S // ts,),
        in_specs=[pl.BlockSpec((ts, cfg.QW), lambda i: (i, 0)), pl.BlockSpec((ts, cfg.KVW), lambda i: (i, 0)),
                  pl.BlockSpec((ts, LANE), lambda i: (i, kr_block)), tab, tab],
        out_specs=[heads(2 * LANE), heads(2 * LANE), heads(LANE)],
        out_shape=[jax.ShapeDtypeStruct((H, S, 2 * LANE), BF16), jax.ShapeDtypeStruct((H, S, 2 * LANE), BF16),
                   jax.ShapeDtypeStruct((H, S, LANE), BF16)],
        compiler_params=_params(("parallel",)),
    )(q, kv, kr, cos2, sin2)


def _mla_unpack(cfg, dQ, dK, dV, cos2, sin2):
    S, H = cfg.S, cfg.H
    ts = _pick(S, 256, 8)

    def body(dQ_ref, dK_ref, dV_ref, c_ref, s_ref, dq_ref, dkv_ref, dkr_ref):
        c2, s2 = c_ref[...], s_ref[...]
        lo = lax.broadcasted_iota(I32, (ts, LANE), 1) < ROPE
        tk = jnp.zeros((ts, LANE), F32)
        for h in range(H):
            dq_ref[:, h * LANE:(h + 1) * LANE] = dQ_ref[h, :, 0:LANE].astype(BF16)
            dkv_ref[:, h * LANE:(h + 1) * LANE] = dK_ref[h, :, 0:LANE].astype(BF16)
            dkv_ref[:, (H + h) * LANE:(H + h + 1) * LANE] = dV_ref[h].astype(BF16)
            own = lo if h % 2 == 0 else jnp.logical_not(lo)
            tk = tk + jnp.where(own, dK_ref[h, :, LANE:], 0.0)
        for j in range(H // 2):
            dr = dQ_ref[2 * j, :, LANE:] + dQ_ref[2 * j + 1, :, LANE:]
            dq_ref[:, (H + j) * LANE:(H + j + 1) * LANE] = _rot_back(dr, c2, s2).astype(BF16)
        dkr_rot = jnp.where(lo, tk + pltpu.roll(tk, ROPE, 1), 0.0)
        dkr_ref[...] = _rot_back(dkr_rot, c2, s2).astype(BF16)

    tab = pl.BlockSpec((ts, LANE), lambda i: (i, 0))
    return pl.pallas_call(
        body, name="mla_unpack", grid=(S // ts,),
        in_specs=[pl.BlockSpec((H, ts, 2 * LANE), lambda i: (0, i, 0)), pl.BlockSpec((H, ts, 2 * LANE), lambda i: (0, i, 0)),
                  pl.BlockSpec((H, ts, LANE), lambda i: (0, i, 0)), tab, tab],
        out_specs=[pl.BlockSpec((ts, cfg.QW), lambda i: (i, 0)), pl.BlockSpec((ts, cfg.KVW), lambda i: (i, 0)), tab],
        out_shape=[jax.ShapeDtypeStruct((S, cfg.QW), BF16), jax.ShapeDtypeStruct((S, cfg.KVW), BF16),
                   jax.ShapeDtypeStruct((S, LANE), BF16)],
        compiler_params=_params(("parallel",)),
    )(dQ, dK, dV, cos2, sin2)


_ATT_T = 256
_ATT_HB = 8
_ATT_SCALE = (NOPE + ROPE) ** -0.5


def _diag_mask(transposed=False):
    r = lax.broadcasted_iota(I32, (_ATT_T, _ATT_T), 0) // CHUNK
    c = lax.broadcasted_iota(I32, (_ATT_T, _ATT_T), 1) // CHUNK
    return r <= c if transposed else c <= r


def _row_form(col):
    return jnp.broadcast_to(col, (col.shape[0], LANE)).T[0:8, :]


def _attn_fwd(cfg, Q, K, V):
    S, H, T, HB = cfg.S, cfg.H, _ATT_T, min(cfg.H, _ATT_HB)

    def body(q_ref, k_ref, v_ref, o_ref, lse_t_ref):
        qi = pl.program_id(1)

        def head_step(b, kb, carry, mask):
            m, l, acc = carry
            ks = pl.multiple_of(kb * T, T)
            s = _dot(q_ref[b], k_ref[b, pl.ds(ks, T), :], NT) * _ATT_SCALE
            if mask is not None:
                s = jnp.where(mask, s, -1e30)
            m_new = jnp.maximum(m, jnp.max(s, axis=1, keepdims=True))
            p = jnp.exp(s - m_new)
            alpha = jnp.exp(m - m_new)
            l = alpha * l + jnp.sum(p, axis=1, keepdims=True)
            acc = alpha * acc + _dot(p.astype(BF16), v_ref[b, pl.ds(ks, T), :])
            return m_new, l, acc

        def step(kb, carry, mask=None):
            return tuple(head_step(b, kb, carry[b], mask) for b in range(HB))

        init = (jnp.full((T, 1), -1e30, F32), jnp.zeros((T, 1), F32), jnp.zeros((T, VH), F32))
        done = step(qi, lax.fori_loop(0, qi, step, (init,) * HB), _diag_mask())
        for b, (m, l, acc) in enumerate(done):
            o_ref[:, b * LANE:(b + 1) * LANE] = acc / l
            lse_t_ref[b] = _row_form(m + jnp.log(l))

    return pl.pallas_call(
        body, name="attn_fwd", grid=(H // HB, S // T),
        in_specs=[pl.BlockSpec((HB, T, 2 * LANE), lambda h, i: (h, i, 0)), pl.BlockSpec((HB, S, 2 * LANE), lambda h, i: (h, 0, 0)),
                  pl.BlockSpec((HB, S, LANE), lambda h, i: (h, 0, 0))],
        out_specs=[pl.BlockSpec((T, HB * LANE), lambda h, i: (i, h)), pl.BlockSpec((HB, 8, T), lambda h, i: (h, 0, i))],
        out_shape=[jax.ShapeDtypeStruct((S, H * LANE), F32), jax.ShapeDtypeStruct((H, 8, S), F32)],
        compiler_params=_params(("parallel", "parallel")),
    )(Q, K, V)


def _attn_delta(cfg, do, o, after):
    S, H, T = cfg.S, cfg.H, _ATT_T

    def body(do_ref, o_ref, after_ref, dl_t_ref):
        for h in range(H):
            sl = slice(h * LANE, (h + 1) * LANE)
            dl_t_ref[h] = _row_form(jnp.sum(do_ref[:, sl] * o_ref[:, sl], axis=1, keepdims=True))

    wide = pl.BlockSpec((T, H * LANE), lambda i: (i, 0))
    return pl.pallas_call(
        body, name="attn_delta", grid=(S // T,), in_specs=[wide, wide, _ANY],
        out_specs=pl.BlockSpec((H, 8, T), lambda i: (0, 0, i)), out_shape=jax.ShapeDtypeStruct((H, 8, S), F32),
        compiler_params=_params(("parallel",)),
    )(do, o, after)


_ATT_HB_BWD = 4


def _attn_bwd(cfg, Q, K, V, do, lse_t, delta_t):
    S, H, T, HB = cfg.S, cfg.H, _ATT_T, min(cfg.H, _ATT_HB_BWD)
    nq = S // T

    def body(q_ref, k_ref, v_ref, do_ref, lse_ref, dl_ref, dq_ref, dk_ref, dv_ref):
        kb = pl.program_id(1)

        @pl.when(kb == 0)
        def _():
            dq_ref[...] = jnp.zeros_like(dq_ref)

        def head_step(b, qi, carry, mask):
            dk, dv = carry
            qs = pl.multiple_of(qi * T, T)
            q = q_ref[b, pl.ds(qs, T), :]
            k = k_ref[b]
            dob = do_ref[pl.ds(qs, T), b * LANE:(b + 1) * LANE].astype(BF16)
            s = _dot(k, q, NT) * _ATT_SCALE
            if mask is not None:
                s = jnp.where(mask, s, -1e30)
            p = jnp.exp(s - lse_ref[b, 0:1, pl.ds(qs, T)])
            dv = dv + _dot(p.astype(BF16), dob)
            dp = _dot(v_ref[b], dob, NT)
            ds = (p * (dp - dl_ref[b, 0:1, pl.ds(qs, T)]) * _ATT_SCALE).astype(BF16)
            dk = dk + _dot(ds, q)
            dq_ref[b, pl.ds(qs, T), :] += _dot(ds, k, TN)
            return dk, dv

        def step(qi, carry, mask=None):
            return tuple(head_step(b, qi, carry[b], mask) for b in range(HB))

        zero = (jnp.zeros((T, 2 * LANE), F32), jnp.zeros((T, VH), F32))
        done = lax.fori_loop(kb + 1, nq, step, step(kb, (zero,) * HB, _diag_mask(transposed=True)))
        for b, (dk, dv) in enumerate(done):
            dk_ref[b] = dk
            dv_ref[b] = dv

    row = pl.BlockSpec((HB, 8, S), lambda h, j: (h, 0, 0))
    whole = pl.BlockSpec((HB, S, 2 * LANE), lambda h, j: (h, 0, 0))
    return pl.pallas_call(
        body, name="attn_bwd", grid=(H // HB, S // T),
        in_specs=[whole, pl.BlockSpec((HB, T, 2 * LANE), lambda h, j: (h, j, 0)), pl.BlockSpec((HB, T, LANE), lambda h, j: (h, j, 0)),
                  pl.BlockSpec((S, HB * LANE), lambda h, j: (0, h)), row, row],
        out_specs=[whole, pl.BlockSpec((HB, T, 2 * LANE), lambda h, j: (h, j, 0)), pl.BlockSpec((HB, T, LANE), lambda h, j: (h, j, 0))],
        out_shape=[jax.ShapeDtypeStruct((H, S, 2 * LANE), F32), jax.ShapeDtypeStruct((H, S, 2 * LANE), F32),
                   jax.ShapeDtypeStruct((H, S, LANE), F32)],
        compiler_params=_params(("parallel", "arbitrary")),
    )(Q, K, V, do, lse_t, delta_t)


def _expand_matrix(cfg):
    r = lax.broadcasted_iota(I32, (LANE, cfg.INNER), 0)
    c = lax.broadcasted_iota(I32, (LANE, cfg.INNER), 1)
    return (r == c // HP).astype(F32)


def _softplus(x):
    return jnp.maximum(x, 0.0) + jnp.log(1.0 + jnp.exp(-jnp.abs(x)))


def _ssd_prep(cfg, dt_raw, dt_bias_pad, a_log_pad, expand):
    HS = cfg.HS

    def fn(raw, bias, alog, E):
        heads = lax.broadcasted_iota(I32, raw.shape, 1) < HS
        dt = jnp.where(heads, _softplus(raw + bias), 0.0)
        a = dt * jnp.where(heads[0:1], -jnp.exp(alog), 0.0)
        return dt, a, _dot(dt, E, precision=HI)

    return _rowwise("ssd_prep", fn, [dt_raw], [dt_bias_pad, a_log_pad, expand],
                    [(LANE, F32), (LANE, F32), (cfg.INNER, F32)], [], _pick(cfg.S, 512, 8))


def _tril(T):
    return lax.broadcasted_iota(I32, (T, T), 0) >= lax.broadcasted_iota(I32, (T, T), 1)


def _ssd_fwd(cfg, xc, dt_exp, a_small, dskip_exp, expand):
    S, T, INNER, G, NPAIR = cfg.S, cfg.T, cfg.INNER, cfg.G, cfg.NPAIR
    NC = S // T

    def body(xc_ref, dte_ref, as_ref, dsk_ref, e_ref, y_ref, hin_ref, ht_ref):
        @pl.when(pl.program_id(0) == 0)
        def _():
            ht_ref[...] = jnp.zeros_like(ht_ref)

        tril = _tril(T)
        tri = tril.astype(F32)
        acs_s = _dot(tri, as_ref[...], precision=HI)
        acs_e = _dot(acs_s, e_ref[...], precision=HI)
        acs_t = acs_s.T
        lo = lax.broadcasted_iota(I32, (T, LANE), 1) < HP
        for g in range(G):
            Bb = xc_ref[:, INNER + g * NST:INNER + (g + 1) * NST].astype(BF16)
            Cb = xc_ref[:, INNER + (G + g) * NST:INNER + (G + g + 1) * NST].astype(BF16)
            Gm = _dot(Cb, Bb, NT)
            for j in range(g * NPAIR // G, (g + 1) * NPAIR // G):
                sl = slice(j * LANE, (j + 1) * LANE)
                Xp = xc_ref[:, sl]
                Xdt = Xp * dte_ref[:, sl]
                Xb = Xdt.astype(BF16)
                acs_p = acs_e[:, sl]
                last = acs_p[T - 1:T, :]
                Hin = ht_ref[j]
                hin_ref[0, j] = Hin
                yd = []
                for e in (0, 1):
                    h = 2 * j + e
                    Lm = jnp.exp(jnp.where(tril, acs_s[:, h:h + 1] - acs_t[h:h + 1, :], -1e30))
                    yd.append(_dot((Gm * Lm).astype(BF16), Xb))
                y_off = _dot(Cb, Hin.astype(BF16)) * jnp.exp(acs_p)
                y_ref[:, sl] = jnp.where(lo, yd[0], yd[1]) + y_off + Xp * dsk_ref[:, sl]
                st = _dot(Bb, (Xdt * jnp.exp(last - acs_p)).astype(BF16), TN)
                ht_ref[j] = jnp.exp(last) * Hin + st

    rows = lambda w: pl.BlockSpec((T, w), lambda c: (c, 0))
    return pl.pallas_call(
        body, name="ssd_fwd", grid=(NC,),
        in_specs=[rows(cfg.CONVCH), rows(INNER), rows(LANE), pl.BlockSpec((1, INNER), lambda c: (0, 0)),
                  pl.BlockSpec((LANE, INNER), lambda c: (0, 0))],
        out_specs=[rows(INNER), pl.BlockSpec((1, NPAIR, NST, LANE), lambda c: (c, 0, 0, 0))],
        out_shape=[jax.ShapeDtypeStruct((S, INNER), F32), jax.ShapeDtypeStruct((NC, NPAIR, NST, LANE), F32)],
        scratch_shapes=[pltpu.VMEM((NPAIR, NST, LANE), F32)],
        compiler_params=_params(("arbitrary",)),
    )(xc, dt_exp, a_small, dskip_exp, expand)


def _ssd_bwd(cfg, dy, xc, dt_exp, a_small, dskip_exp, hin, dt_raw, dt_bias_pad, a_log_pad, expand):
    S, T, INNER, G, NPAIR, HS = cfg.S, cfg.T, cfg.INNER, cfg.G, cfg.NPAIR, cfg.HS
    NC = S // T

    def body(dy_ref, xc_ref, dte_ref, as_ref, dsk_ref, hin_ref, raw_ref, bias_ref, alog_ref, e_ref,
             dxc_ref, draw_ref, dbias_ref, dalog_ref, dskip_ref, dht_ref, cols_ref, rows_ref, dacs_ref, ddt_ref):
        first = pl.program_id(0) == 0

        @pl.when(first)
        def _():
            dht_ref[...] = jnp.zeros_like(dht_ref)

        tril = _tril(T)
        tri = tril.astype(F32)
        a_s = as_ref[...]
        acs_s = _dot(tri, a_s, precision=HI)
        acs_e = _dot(acs_s, e_ref[...], precision=HI)
        acs_t = acs_s.T
        lo = lax.broadcasted_iota(I32, (T, LANE), 1) < HP
        last_row = lax.broadcasted_iota(I32, (T, LANE), 0) == T - 1
        cols_ref[...] = jnp.zeros_like(cols_ref)
        rows_ref[...] = jnp.zeros_like(rows_ref)
        dsk_parts = []
        for g in range(G):
            bsl = slice(INNER + g * NST, INNER + (g + 1) * NST)
            csl = slice(INNER + (G + g) * NST, INNER + (G + g + 1) * NST)
            Bb = xc_ref[:, bsl].astype(BF16)
            Cb = xc_ref[:, csl].astype(BF16)
            Gm = _dot(Cb, Bb, NT)
            dG = jnp.zeros((T, T), F32)
            dB = jnp.zeros((T, NST), F32)
            dC = jnp.zeros((T, NST), F32)
            for j in range(g * NPAIR // G, (g + 1) * NPAIR // G):
                sl = slice(j * LANE, (j + 1) * LANE)
                Xp = xc_ref[:, sl]
                dtp = dte_ref[:, sl]
                Xdt = Xp * dtp
                Xb = Xdt.astype(BF16)
                acs_p = acs_e[:, sl]
                last = acs_p[T - 1:T, :]
                e_p, dec, cd = jnp.exp(acs_p), jnp.exp(last - acs_p), jnp.exp(last)
                Hin = hin_ref[0, j]
                Hb = Hin.astype(BF16)
                dHn = dht_ref[j]
                dHb = dHn.astype(BF16)
                dYp = dy_ref[:, sl]
                z = _dot(Cb, Hb)
                dz = (dYp * e_p).astype(BF16)
                dacs_p = dYp * z * e_p
                dC = dC + _dot(dz, Hb, NT)
                dHin = _dot(Cb, dz, TN) + cd * dHn
                dlast = _colsum(dHn * Hin) * cd
                qv = _dot(Bb, dHb)
                dXdt = qv * dec
                ddec = qv * Xdt * dec
                dacs_p = dacs_p - ddec
                dlast = dlast + _colsum(ddec)
                dB = dB + _dot((Xdt * dec).astype(BF16), dHb, NT)
                for e in (0, 1):
                    h = 2 * j + e
                    Lm = jnp.exp(jnp.where(tril, acs_s[:, h:h + 1] - acs_t[h:h + 1, :], -1e30))
                    Mh = Gm * Lm
                    dYe = jnp.where(lo if e == 0 else jnp.logical_not(lo), dYp, 0.0).astype(BF16)
                    dM = _dot(dYe, Xb, NT)
                    dXdt = dXdt + _dot(Mh.astype(BF16), dYe, TN)
                    W = dM * Mh
                    cols_ref[:, h:h + 1] = jnp.sum(W, axis=1, keepdims=True)
                    rows_ref[h:h + 1, :] = _colsum(W)
                    dG = dG + dM * Lm
                dacs_ref[:, sl] = dacs_p + jnp.where(last_row, dlast, 0.0)
                ddt_ref[:, sl] = dXdt * Xp
                dxc_ref[:, sl] = dXdt * dtp + dYp * dsk_ref[:, sl]
                dsk_parts.append(_colsum(dYp * Xp))
                dht_ref[j] = dHin
            dGb = dG.astype(BF16)
            dxc_ref[:, bsl] = dB + _dot(dGb, Cb, TN)
            dxc_ref[:, csl] = dC + _dot(dGb, Bb)
        E = e_ref[...]
        dacs_s = cols_ref[...] - rows_ref[...].T + _dot(dacs_ref[...], E, NT, precision=HI)
        da = _dot(tri, dacs_s, TN, precision=HI)
        heads = lax.broadcasted_iota(I32, (1, LANE), 1) < HS
        A = jnp.where(heads, -jnp.exp(alog_ref[...]), 0.0)
        ddt = _dot(ddt_ref[...], E, NT, precision=HI) + da * A
        draw = jnp.where(heads, ddt * _sigmoid(raw_ref[...] + bias_ref[...]), 0.0)
        draw_ref[...] = draw
        dsk = _dot(jnp.broadcast_to(jnp.concatenate(dsk_parts, axis=1), (8, INNER)), E, NT, precision=HI)[0:1]
        for ref, val in ((dbias_ref, _colsum(draw)), (dalog_ref, _colsum(da * a_s)), (dskip_ref, dsk)):
            @pl.when(first)
            def _():
                ref[...] = val

            @pl.when(jnp.logical_not(first))
            def _():
                ref[...] += val

    dt_raw, _, raw_block = _window(dt_raw)
    rows = lambda w, b=0: pl.BlockSpec((T, w), lambda c: (NC - 1 - c, b))
    vec = lambda w: pl.BlockSpec((1, w), lambda c: (0, 0))
    return pl.pallas_call(
        body, name="ssd_bwd", grid=(NC,),
        in_specs=[rows(INNER), rows(cfg.CONVCH), rows(INNER), rows(LANE), vec(INNER),
                  pl.BlockSpec((1, NPAIR, NST, LANE), lambda c: (NC - 1 - c, 0, 0, 0)), rows(LANE, raw_block), vec(LANE), vec(LANE),
                  pl.BlockSpec((LANE, INNER), lambda c: (0, 0))],
        out_specs=[rows(cfg.CONVCH), rows(LANE), vec(LANE), vec(LANE), vec(LANE)],
        out_shape=[jax.ShapeDtypeStruct((S, cfg.CONVCH), F32), jax.ShapeDtypeStruct((S, LANE), F32)]
        + [jax.ShapeDtypeStruct((1, LANE), F32)] * 3,
        scratch_shapes=[pltpu.VMEM((NPAIR, NST, LANE), F32), pltpu.VMEM((T, LANE), F32), pltpu.VMEM((LANE, T), F32),
                        pltpu.VMEM((T, INNER), F32), pltpu.VMEM((T, INNER), F32)],
        compiler_params=_params(("arbitrary",)),
    )(dy, xc, dt_exp, a_small, dskip_exp, hin, dt_raw, dt_bias_pad, a_log_pad, expand)


def _ssd_post(cfg, y, z, norm_g):
    W = cfg.INNER // cfg.G

    def fn(y, z, g):
        yz = y * z * _sigmoid(z)
        return jnp.concatenate([yz[:, i * W:(i + 1) * W] * _rs(yz[:, i * W:(i + 1) * W]) for i in range(cfg.G)], axis=1) * g

    return _rowwise("ssd_post", fn, [y, z], [norm_g], [(cfg.INNER, BF16)], [], _pick(cfg.S, 256, 8))[0]


def _ssd_post_bwd(cfg, db, y, z, norm_g):
    W = cfg.INNER // cfg.G

    def fn(db, y, z, g):
        sg = _sigmoid(z)
        yz = y * z * sg
        dn = db * g
        dyz, nh = [], []
        for i in range(cfg.G):
            seg = yz[:, i * W:(i + 1) * W]
            r = _rs(seg)
            nh.append(seg * r)
            dyz.append(_rms_back(nh[-1], r, dn[:, i * W:(i + 1) * W]))
        dyz = jnp.concatenate(dyz, axis=1)
        return dyz * z * sg, dyz * y * sg * (1.0 + z * (1.0 - sg)), _colsum(db * jnp.concatenate(nh, axis=1))

    return _rowwise("ssd_post_bwd", fn, [db, y, z], [norm_g], [(cfg.INNER, F32), (cfg.INNER, F32)], [(1, cfg.INNER)],
                    _pick(cfg.S, 256, 8))


def _rms_pre(cfg, x, g):
    return _rowwise("rms_pre", lambda x, g: x * _rs(x) * g, [x], [g], [(cfg.D, BF16)], [], _pick(cfg.S, 256, 8))[0]


def _local_grads(cfg, x, tgt, W, sp, mla_weights=None, out_weight=None, ffn_weights=None, down_weight=None,
                 ffn_grads_ready=None, early_grads_ready=None, in_grad_ready=None, xn=None, after_in=None):
    S, D, H, INNER = cfg.S, cfg.D, cfg.H, cfg.INNER
    ts = _pick(S, 256, 8)
    tc = _CONV_COLS

    if xn is None:
        xn = _rms_pre(cfg, x, sp["mix_pre_g"])
    u = _matmul("mm_in", xn, W["w_in"], "nt", F32, after=after_in)
    c_q, c_kv, kr, z, xbc, dt_raw = [(u, cfg.window(n)) for n in ("c_q", "c_kv", "kr", "z", "xbc", "dt")]

    if mla_weights is not None:
        sp = dict(sp, q_norm_g=sp["q_norm_g"] + mla_weights.pass_on(u)[0, 0])
    cqn = _rowwise("rms_q", lambda x, g: x * _rs(x) * g, [c_q], [sp["q_norm_g"]], [(cfg.QL, BF16)], [], ts)[0]
    ckvn = _rowwise("rms_kv", lambda x, g: x * _rs(x) * g, [c_kv], [sp["kv_norm_g"]], [(cfg.KVL, BF16)], [], ts)[0]
    if mla_weights is not None:
        W = dict(W, **mla_weights.arrived(ckvn))
    q = _matmul("mm_uq", cqn, W["w_uq"], "nn", F32)
    kv = _matmul("mm_ukv", ckvn, W["w_ukv"], "nn", F32)
    cos2, sin2 = _rope_tables(S)
    Qh, Kh, Vh = _mla_pack(cfg, q, kv, kr, cos2, sin2)
    a_out, lse_t = _attn_fwd(cfg, Qh, Kh, Vh)
    if out_weight is not None:
        sp = dict(sp, ssm_conv_b=sp["ssm_conv_b"] + out_weight.pass_on(a_out)[0, 0])

    pad = lambda v: jnp.pad(v, ((0, 0), (0, LANE - v.shape[1])))
    expand = _expand_matrix(cfg)
    dt_bias_pad, a_log_pad = pad(sp["dt_bias"]), pad(sp["a_log"])
    dskip_exp = jnp.repeat(sp["d_skip"], HP, axis=1)
    xc = _colwise("ssm_act", _ssm_act, [xbc], [sp["ssm_conv_w"], sp["ssm_conv_b"]], [F32], [], tc)[0]
    dt_s, a_s, dt_exp = _ssd_prep(cfg, dt_raw, dt_bias_pad, a_log_pad, expand)
    y_ssd, hin = _ssd_fwd(cfg, xc, dt_exp, a_s, dskip_exp, expand)
    b_out = _ssd_post(cfg, y_ssd, z, sp["ssm_norm_g"])

    ab_out = jnp.concatenate([a_out.astype(BF16), b_out], axis=1)
    if out_weight is not None:
        W = dict(W, **out_weight.arrived(ab_out))
    if ffn_weights is not None:
        sp = dict(sp, mix_post_g=sp["mix_post_g"] + ffn_weights.pass_on(ab_out)[0, 0])
    mix = _matmul("mm_out", ab_out, W["w_out"], "nn", F32)

    def mid(x, mix, g_mp, g_fp):
        x1 = x + mix * _rs(mix) * g_mp
        return x1, x1 * _rs(x1) * g_fp

    x1, h2 = _rowwise("fwd_mid", mid, [x, mix], [sp["mix_post_g"], sp["ffn_pre_g"]], [(D, F32), (D, BF16)], [], ts)
    if ffn_weights is not None:
        W = dict(W, **ffn_weights.arrived(h2))
    gate_pre, up = _matmul_twin("mm_gate_up", h2, W["w_gate"], W["w_up"], "nn", F32)
    if down_weight is not None:
        sp = dict(sp, ffn_conv_b=sp["ffn_conv_b"] + down_weight.pass_on(gate_pre)[0, 0])
    act = _colwise("ffn_act", _ffn_act, [gate_pre, up], [sp["ffn_conv_w"], sp["ffn_conv_b"]], [BF16], [], tc)[0]
    if down_weight is not None:
        W = dict(W, **down_weight.arrived(act))
    f = _matmul("mm_down", act, W["w_down"], "nn", F32)

    def final(x1, f, t, g):
        r = _rs(f)
        fh = f * r
        err = x1 + fh * g - t
        loss = 0.5 * jnp.sum(jnp.mean(err * err, axis=-1, keepdims=True), axis=0, keepdims=True)
        dy = err * (1.0 / D)
        return dy, _rms_back(fh, r, dy * g), _colsum(dy * fh), loss

    dy, df, g_ffn_post, loss = _rowwise("final", final, [x1, f, tgt], [sp["ffn_post_g"]], [(D, F32), (D, BF16)],
                                        [(1, D), (1, LANE)], ts)
    gW = {}
    dact = _matmul("mm_down_dx", df, W["w_down"], "nt", F32)
    gW["w_down"] = _matmul("mm_down_dw", act, df, "tn", BF16)
    dgate, dup, g_ffn_conv_w, g_ffn_conv_b = _colwise(
        "ffn_act_bwd", _ffn_act_back, [dact, gate_pre, up], [sp["ffn_conv_w"], sp["ffn_conv_b"]], [BF16, BF16], [FFN_K, 1], tc)
    gW["w_gate"], gW["w_up"] = _matmul_twin("mm_gate_up_dw", h2, dgate, dup, "tn", BF16)
    if ffn_grads_ready is not None:
        sp = dict(sp, ffn_pre_g=sp["ffn_pre_g"] + ffn_grads_ready({n: gW[n] for n in ("w_down", "w_gate", "w_up")})[0, 0])
    dh2 = _matmul("mm_gu_dx", dgate, W["w_gate"], "nt", F32, dup, W["w_up"], chips=True)

    def mid_back(dy, dh2, x1, mix, g_mp, g_fp):
        r2 = _rs(x1)
        xh = x1 * r2
        dx1 = dy + _rms_back(xh, r2, dh2 * g_fp)
        r1 = _rs(mix)
        mh = mix * r1
        return dx1, _rms_back(mh, r1, dx1 * g_mp), _colsum(dh2 * xh), _colsum(dx1 * mh)

    dx1, dmix, g_ffn_pre, g_mix_post = _rowwise("bwd_mid", mid_back, [dy, dh2, x1, mix], [sp["mix_post_g"], sp["ffn_pre_g"]],
                                                [(D, F32), (D, BF16)], [(1, D), (1, D)], ts)
    dab_out = _matmul("mm_out_dx", dmix, W["w_out"], "nt", F32)
    db_out = (dab_out, (INNER, cfg.MLAW // INNER))
    gW["w_out"] = _matmul("mm_out_dw", ab_out, dmix, "tn", BF16)
    early_token = jnp.zeros((8, LANE), F32)
    if early_grads_ready is not None:
        early_token = early_grads_ready({n: gW[n] for n in ("w_down", "w_gate", "w_up", "w_out")})
        sp = dict(sp, ssm_norm_g=sp["ssm_norm_g"] + early_token[0, 0])

    dy_ssd, dz, g_ssm_norm = _ssd_post_bwd(cfg, db_out, y_ssd, z, sp["ssm_norm_g"])
    dxc, ddt_raw, g_dt_bias, g_a_log, g_d_skip = _ssd_bwd(cfg, dy_ssd, xc, dt_exp, a_s, dskip_exp, hin, dt_raw,
                                                          dt_bias_pad, a_log_pad, expand)
    dxbc, g_ssm_conv_w, g_ssm_conv_b = _colwise("ssm_act_bwd", _ssm_act_back, [dxc, xbc], [sp["ssm_conv_w"], sp["ssm_conv_b"]],
                                                [BF16], [SSM_K, 1], tc)

    delta_t = _attn_delta(cfg, dab_out, a_out, early_token)
    dQ, dK, dV = _attn_bwd(cfg, Qh, Kh, Vh, dab_out, lse_t, delta_t)
    dq, dkv, dkr = _mla_unpack(cfg, dQ, dK, dV, cos2, sin2)
    dcqn = _matmul("mm_uq_dx", dq, W["w_uq"], "nt", F32)
    dckvn = _matmul("mm_ukv_dx", dkv, W["w_ukv"], "nt", F32)
    gW["w_uq"] = _matmul("mm_uq_dw", cqn, dq, "tn", BF16)
    gW["w_ukv"] = _matmul("mm_ukv_dw", ckvn, dkv, "tn", BF16)

    def rms_back(x, dy, g):
        r = _rs(x)
        xh = x * r
        return _rms_back(xh, r, dy * g), _colsum(dy * xh)

    dc_q, g_q_norm = _rowwise("rms_q_bwd", rms_back, [c_q, dcqn], [sp["q_norm_g"]], [(cfg.QL, BF16)], [(1, cfg.QL)], ts)
    dc_kv, g_kv_norm = _rowwise("rms_kv_bwd", rms_back, [c_kv, dckvn], [sp["kv_norm_g"]], [(cfg.KVL, BF16)], [(1, cfg.KVL)], ts)

    du = dict(c_q=dc_q, c_kv=dc_kv, kr=dkr, z=dz.astype(BF16), xbc=dxbc, dt=ddt_raw.astype(BF16))
    du = jnp.concatenate([du[n] for n in sorted(du, key=lambda n: cfg.seg[n][0])], axis=1)
    assert du.shape[1] == cfg.EXT, "the layout of u has gaps"
    gW["w_in"] = _matmul("mm_in_dw", du, xn, "tn", BF16)
    if in_grad_ready is None:
        dxn = _matmul("mm_in_dx", du, W["w_in"], "nn", F32)
    else:
        token = in_grad_ready({n: gW[n] for n in ("w_in", "w_uq", "w_ukv")}, None)
        dxn = _matmul("mm_in_dx", du, W["w_in"], "nn", F32, after=token)
        sp = dict(sp, mix_pre_g=sp["mix_pre_g"] + in_grad_ready(None, dxn)[0, 0])

    def first_back(dx1, dxn, x, g):
        r = _rs(x)
        xh = x * r
        return dx1 + _rms_back(xh, r, dxn * g), _colsum(dxn * xh)

    grad_x, g_mix_pre = _rowwise("bwd_first", first_back, [dx1, dxn, x], [sp["mix_pre_g"]], [(D, F32)], [(1, D)], ts)

    gs = dict(mix_pre_g=g_mix_pre, q_norm_g=g_q_norm, kv_norm_g=g_kv_norm, ssm_conv_w=g_ssm_conv_w, ssm_conv_b=g_ssm_conv_b,
              dt_bias=g_dt_bias[:, :cfg.HS], a_log=g_a_log[:, :cfg.HS], d_skip=g_d_skip[:, :cfg.HS], ssm_norm_g=g_ssm_norm,
              mix_post_g=g_mix_post, ffn_pre_g=g_ffn_pre, ffn_conv_w=g_ffn_conv_w, ffn_conv_b=g_ffn_conv_b,
              ffn_post_g=g_ffn_post)
    return loss, grad_x, gW, gs


def _to_kernel_layout(cfg, name, w):
    if name == "w_in":
        parts, at = [], 0
        for off, width, n_off, n_width in sorted(cfg.seg.values()):
            parts += [jnp.zeros((off - at, w.shape[1]), w.dtype), w[n_off:n_off + n_width],
                      jnp.zeros((width - n_width, w.shape[1]), w.dtype)]
            at = off + width
        parts.append(jnp.zeros((cfg.EXT - at, w.shape[1]), w.dtype))
        return jnp.concatenate([p for p in parts if p.shape[0]], axis=0)
    if name in ("w_uq", "w_ukv"):
        per = NOPE + (ROPE if name == "w_uq" else VH)
        return jnp.concatenate([w[:, h * per:h * per + NOPE] for h in range(cfg.H)]
                               + [w[:, h * per + NOPE:(h + 1) * per] for h in range(cfg.H)], axis=1)
    return w


def _from_kernel_layout(cfg, name, g):
    if name == "w_in":
        return jnp.concatenate([g[off:off + n_width] for off, _, _, n_width in sorted(cfg.seg.values(), key=lambda s: s[2])], axis=0)
    if name in ("w_uq", "w_ukv"):
        second = ROPE if name == "w_uq" else VH
        base = cfg.H * NOPE
        parts = []
        for h in range(cfg.H):
            parts += [g[:, h * NOPE:(h + 1) * NOPE], g[:, base + h * second:base + (h + 1) * second]]
        return jnp.concatenate(parts, axis=1)
    return g


_CHIP_MAJOR = ("w_gate", "w_up")
_RELAYOUT = ("w_uq", "w_ukv")
_LAYOUT_ROWS = 256
_CONV_COLS = 256


def _w_in_layout(cfg, wg):
    _, rs, d = wg.shape
    tc = _pick(d, _LAYOUT_ROWS, LANE)

    def body(w_ref, o_ref):
        o_ref[...] = _to_kernel_layout(cfg, "w_in", jnp.concatenate([w_ref[k] for k in range(N_CHIPS)], axis=0))

    return pl.pallas_call(
        body, name="layout_w_in", grid=(d // tc,),
        in_specs=[pl.BlockSpec((N_CHIPS, rs, tc), lambda j: (0, 0, j))], out_specs=pl.BlockSpec((cfg.EXT, tc), lambda j: (0, j)),
        out_shape=jax.ShapeDtypeStruct((cfg.EXT, d), wg.dtype), compiler_params=_params(("parallel",)),
    )(wg)


def _w_in_grad_to_chips(cfg, g):
    _, d = g.shape
    rs = cfg.IN_COLS // N_CHIPS
    tc = _pick(d, _LAYOUT_ROWS, LANE)

    def body(g_ref, o_ref):
        nat = _from_kernel_layout(cfg, "w_in", g_ref[...])
        for k in range(N_CHIPS):
            o_ref[k] = nat[k * rs:(k + 1) * rs]

    return pl.pallas_call(
        body, name="layout_grad_w_in", grid=(d // tc,),
        in_specs=[pl.BlockSpec((cfg.EXT, tc), lambda j: (0, j))], out_specs=pl.BlockSpec((N_CHIPS, rs, tc), lambda j: (0, 0, j)),
        out_shape=jax.ShapeDtypeStruct((N_CHIPS, rs, d), g.dtype), compiler_params=_params(("parallel",)),
    )(g)


def _gathered_to_kernel(cfg, name, wg):
    if name in _CHIP_MAJOR:
        return wg
    if name == "w_in":
        return _w_in_layout(cfg, wg)
    if name not in _RELAYOUT:
        return wg.reshape(wg.shape[0] * wg.shape[1], wg.shape[2])
    _, rows, cs = wg.shape
    tr = _pick(rows, _LAYOUT_ROWS, 16)

    def body(w_ref, o_ref):
        o_ref[...] = _to_kernel_layout(cfg, name, jnp.concatenate([w_ref[k] for k in range(N_CHIPS)], axis=1))

    wide = jax.eval_shape(lambda w: _to_kernel_layout(cfg, name, w), jax.ShapeDtypeStruct((rows, N_CHIPS * cs), wg.dtype)).shape[1]
    return pl.pallas_call(
        body, name="layout_" + name, grid=(rows // tr,),
        in_specs=[pl.BlockSpec((N_CHIPS, tr, cs), lambda i: (0, i, 0))], out_specs=pl.BlockSpec((tr, wide), lambda i: (i, 0)),
        out_shape=jax.ShapeDtypeStruct((rows, wide), wg.dtype), compiler_params=_params(("parallel",)),
    )(wg)


def _grad_to_chips(cfg, name, g):
    if name in _CHIP_MAJOR:
        return g
    if name == "w_in":
        return _w_in_grad_to_chips(cfg, g)
    if name not in _RELAYOUT:
        return g.reshape(N_CHIPS, g.shape[0] // N_CHIPS, g.shape[1])
    rows, wide = g.shape
    tr = _pick(rows, _LAYOUT_ROWS, 16)
    cs = jax.eval_shape(lambda v: _from_kernel_layout(cfg, name, v), g).shape[1] // N_CHIPS

    def body(g_ref, o_ref):
        nat = _from_kernel_layout(cfg, name, g_ref[...])
        for k in range(N_CHIPS):
            o_ref[k] = nat[:, k * cs:(k + 1) * cs]

    return pl.pallas_call(
        body, name="layout_grad_" + name, grid=(rows // tr,),
        in_specs=[pl.BlockSpec((tr, wide), lambda i: (i, 0))], out_specs=pl.BlockSpec((N_CHIPS, tr, cs), lambda i: (0, i, 0)),
        out_shape=jax.ShapeDtypeStruct((N_CHIPS, rows, cs), g.dtype), compiler_params=_params(("parallel",)),
    )(g)


def _me():
    return lax.axis_index("x"), lax.axis_index("y"), lax.axis_index("c")


def _other_chips(x, y):
    return [(1 - x, y), (x, 1 - y), (1 - x, 1 - y)]


_ANY = pl.BlockSpec(memory_space=pl.ANY)


BLOCK_ELEMS = 1 << 19
BLOCK_ELEMS_FEW = 1 << 20


def _row_block(rows, cols, mult, elems=BLOCK_ELEMS):
    return _pick(rows, max(mult, elems // cols // mult * mult), mult)


def _scalar(v):
    return v.astype(I32).reshape(1)


def _blocks2d(r, c, mult, elems=BLOCK_ELEMS):
    if r % mult == 0:
        tr = _row_block(r, c, mult, elems)
        return (tr, c), r // tr, lambda i: (i, 0)
    tc = _pick(c, max(LANE, elems // r // LANE * LANE), LANE)
    return (r, tc), c // tc, lambda i: (0, i)


def _by_rows(rows):
    return rows % 32 == 0


def _half_shape(rows, cols):
    return (rows // 2, cols) if _by_rows(rows) else (rows, cols // 2)


def _half_blocks(rows, cols, mult, elems=BLOCK_ELEMS):
    hr, hc = _half_shape(rows, cols)
    block, n, part = _blocks2d(hr, hc, mult, elems)
    assert (hr % mult == 0) == _by_rows(rows), (rows, cols, mult)
    full = (lambda h, i: (h * n + i, 0)) if _by_rows(rows) else (lambda h, i: (0, h * n + i))
    return block, n, full, part


def _half(ref, k, half):
    hr, hc = _half_shape(ref.shape[1], ref.shape[2])
    if _by_rows(ref.shape[1]):
        return ref.at[k, pl.ds(pl.multiple_of(half * hr, 16), hr), :]
    return ref.at[k, :, pl.ds(pl.multiple_of(half * hc, LANE), hc)]


def _shard_blocks(w, br, bc):
    if w.shape[0] == 1:
        def write(ref, v):
            ref[...] = v
        return (lambda f: pl.BlockSpec((None, br, bc), lambda *a: (0, *f(*a)))), (lambda ref: ref[...]), write
    assert w.shape[1] == 1 and br == w.shape[0], w.shape

    def write_rows(ref, v):
        ref[:, 0, :] = v
    return (lambda f: pl.BlockSpec((br, 1, bc), lambda *a: (0, 0, f(*a)[1]))), (lambda ref: ref[:, 0, :]), write_rows


def _stage_shard(name, w, chip, after=None):
    rs, cs = w.shape[0] * w.shape[1], w.shape[2]
    (br, bc), n, idx = _blocks2d(rs, cs, 16, BLOCK_ELEMS_FEW)
    spec, get, _ = _shard_blocks(w, br, bc)

    def body(chip_ref, w_ref, *refs):
        refs[-1][...] = get(w_ref).astype(BF16)

    return pl.pallas_call(
        body, name="stage_" + name,
        grid_spec=pltpu.PrefetchScalarGridSpec(
            num_scalar_prefetch=1, grid=(n,),
            in_specs=[spec(lambda i, chip_ref: idx(i))] + ([] if after is None else [_ANY]),
            out_specs=pl.BlockSpec((None, br, bc), lambda i, chip_ref: (chip_ref[0], *idx(i)))),
        out_shape=jax.ShapeDtypeStruct((N_CHIPS, rs, cs), BF16),
        compiler_params=_params(("parallel",)),
    )(_scalar(chip), w, *([] if after is None else [after]))


_HBM = pl.BlockSpec(memory_space=pltpu.HBM)
_SEM = pl.BlockSpec(memory_space=pltpu.SEMAPHORE)
_EFFECT = pltpu.SideEffectType.DATAFLOW_SIDE_EFFECTING


def _split_start(name, bufs, n_copies, copies, after):
    n = len(bufs)

    def body(*refs):
        for cp in copies(refs[:n], refs[n + 1], refs[n + 2]):
            cp.start()
        refs[-1][...] = jnp.zeros_like(refs[-1])

    res = pl.pallas_call(
        body, name=name,
        out_shape=(pltpu.SemaphoreType.DMA((n_copies,)), pltpu.SemaphoreType.DMA((n_copies,)),
                   *[pltpu.HBM(b.shape, b.dtype) for b in bufs], jax.ShapeDtypeStruct((8, LANE), F32)),
        in_specs=[_HBM] * n + [_ANY], out_specs=(_SEM, _SEM, *[_HBM] * n, pl.BlockSpec(memory_space=pltpu.VMEM)),
        input_output_aliases={i: 2 + i for i in range(n)},
        compiler_params=pltpu.CompilerParams(has_side_effects=_EFFECT),
    )(*[pltpu.with_memory_space_constraint(b, pltpu.HBM) for b in bufs], after)
    return res[0], res[1], list(res[2:2 + n]), res[-1]


def _split_wait(name, send_sems, recv_sems, bufs, after, copies):
    n = len(bufs)

    def body(*refs):
        for cp in copies(refs[:n], refs[n], refs[n + 1]):
            cp.wait_send()
            cp.wait_recv()

    return list(pl.pallas_call(
        body, name=name, out_shape=[pltpu.HBM(b.shape, b.dtype) for b in bufs],
        in_specs=[_HBM] * n + [_SEM, _SEM, _ANY], out_specs=[_HBM] * n,
        input_output_aliases={i: i for i in range(n)},
        compiler_params=pltpu.CompilerParams(has_side_effects=_EFFECT),
    )(*bufs, send_sems, recv_sems, after))


def _gather_to_chips(bufs, send_sems, recv_sems):
    x, y, c = _me()
    return [pltpu.make_async_remote_copy(src_ref=_half(b, 2 * x + y, c), dst_ref=_half(b, 2 * x + y, c),
                                         send_sem=send_sems.at[3 * w + j], recv_sem=recv_sems.at[3 * w + j],
                                         device_id=(cx, cy, c), device_id_type=MESH_ID)
            for w, b in enumerate(bufs) for j, (cx, cy) in enumerate(_other_chips(x, y))]


def _gather_to_sibling(bufs, send_sems, recv_sems):
    x, y, c = _me()
    return [pltpu.make_async_remote_copy(src_ref=_half(b, 2 * cx + cy, c), dst_ref=_half(b, 2 * cx + cy, c),
                                         send_sem=send_sems.at[3 * w + j], recv_sem=recv_sems.at[3 * w + j],
                                         device_id=(x, y, 1 - c), device_id_type=MESH_ID)
            for w, b in enumerate(bufs) for j, (cx, cy) in enumerate(_other_chips(x, y))]


def _pair_copies(grads, lands, send_sems, recv_sems):
    x, y, c = _me()
    return [pltpu.make_async_remote_copy(src_ref=_half(g_ref, slice(None), 1 - c), dst_ref=l_ref, send_sem=send_sems.at[w],
                                         recv_sem=recv_sems.at[w], device_id=(x, y, 1 - c), device_id_type=MESH_ID)
            for w, (g_ref, l_ref) in enumerate(zip(grads, lands))]


def _pair_exchange_start(name, grads):
    n = len(grads)
    lands = [lax.empty((g.shape[0], *_half_shape(g.shape[1], g.shape[2])), g.dtype) for g in grads]
    send_sems, recv_sems, bufs, token = _split_start(
        "pair_exchange_start_" + name, [*grads, *lands], n, lambda refs, ss, rs: _pair_copies(refs[:n], refs[n:], ss, rs),
        jnp.zeros((8, LANE), F32))
    return (send_sems, recv_sems, bufs), token


def _pair_exchange_wait(name, state, after):
    send_sems, recv_sems, bufs = state
    n = len(bufs) // 2
    bufs = _split_wait("pair_exchange_wait_" + name, send_sems, recv_sems, bufs, after,
                       lambda refs, ss, rs: _pair_copies(refs[:n], refs[n:], ss, rs))
    return bufs[:n], bufs[n:]


def _pair_sum(name, g, theirs, c):
    (br, bc), nb, full, part = _half_blocks(g.shape[1], g.shape[2], 16, 2 * BLOCK_ELEMS_FEW)

    def body(c_ref, a_ref, b_ref, o_ref):
        o_ref[...] = (a_ref[...].astype(F32) + b_ref[...].astype(F32)).astype(o_ref.dtype)

    return pl.pallas_call(
        body, name="pair_sum_" + name,
        grid_spec=pltpu.PrefetchScalarGridSpec(
            num_scalar_prefetch=1, grid=(N_CHIPS, nb),
            in_specs=[pl.BlockSpec((None, br, bc), lambda k, i, c_ref: (k, *full(c_ref[0], i))),
                      pl.BlockSpec((None, br, bc), lambda k, i, c_ref: (k, *part(i)))],
            out_specs=pl.BlockSpec((None, br, bc), lambda k, i, c_ref: (k, *part(i)))),
        out_shape=jax.ShapeDtypeStruct(theirs.shape, BF16),
        compiler_params=_params(("parallel", "parallel")),
    )(_scalar(c), g, theirs)


def _chip_copies(srcs, lands, send_sems, recv_sems):
    x, y, c = _me()
    return [pltpu.make_async_remote_copy(src_ref=s_ref.at[2 * cx + cy], dst_ref=l_ref.at[j], send_sem=send_sems.at[3 * w + j],
                                         recv_sem=recv_sems.at[3 * w + j], device_id=(cx, cy, c), device_id_type=MESH_ID)
            for w, (s_ref, l_ref) in enumerate(zip(srcs, lands)) for j, (cx, cy) in enumerate(_other_chips(x, y))]


def _chip_exchange_start(name, sums):
    n = len(sums)
    lands = [lax.empty((3,) + s.shape[1:], s.dtype) for s in sums]
    send_sems, recv_sems, bufs, token = _split_start(
        "chip_exchange_start_" + name, [*sums, *lands], 3 * n, lambda refs, ss, rs: _chip_copies(refs[:n], refs[n:], ss, rs),
        jnp.zeros((8, LANE), F32))
    return send_sems, recv_sems, bufs[:n], bufs[n:], token


def _chip_exchange_wait(name, send_sems, recv_sems, sums, lands, after):
    n = len(sums)
    bufs = _split_wait("chip_exchange_wait_" + name, send_sems, recv_sems, [*sums, *lands], after,
                       lambda refs, ss, rs: _chip_copies(refs[:n], refs[n:], ss, rs))
    return bufs[:n], bufs[n:]


def _chip_sum(name, sums, theirs, chip, after=None):
    _, h, cs = sums.shape
    (br, bc), nb, idx = _blocks2d(h, cs, 16, BLOCK_ELEMS_FEW)

    def body(chip_ref, s_ref, t_ref, *refs):
        acc = s_ref[...].astype(F32)
        for k in range(3):
            acc = acc + t_ref[k].astype(F32)
        refs[-1][...] = acc

    return pl.pallas_call(
        body, name="chip_sum_" + name,
        grid_spec=pltpu.PrefetchScalarGridSpec(
            num_scalar_prefetch=1, grid=(nb,),
            in_specs=[pl.BlockSpec((None, br, bc), lambda i, chip_ref: (chip_ref[0], *idx(i))),
                      pl.BlockSpec((3, br, bc), lambda i, chip_ref: (0, *idx(i)))] + ([] if after is None else [_ANY]),
            out_specs=pl.BlockSpec((br, bc), lambda i, chip_ref: idx(i))),
        out_shape=jax.ShapeDtypeStruct((h, cs), F32),
        compiler_params=_params(("parallel",)),
    )(_scalar(chip), sums, theirs, *([] if after is None else [after]))


def _sibling_copies(halves, lands, send_sems, recv_sems):
    x, y, c = _me()
    return [pltpu.make_async_remote_copy(src_ref=h_ref, dst_ref=l_ref, send_sem=send_sems.at[w], recv_sem=recv_sems.at[w],
                                         device_id=(x, y, 1 - c), device_id_type=MESH_ID)
            for w, (h_ref, l_ref) in enumerate(zip(halves, lands))]


def _sibling_exchange_start(name, halves, after):
    n = len(halves)
    lands = [lax.empty(h.shape, h.dtype) for h in halves]
    send_sems, recv_sems, bufs, token = _split_start(
        "sibling_exchange_start_" + name, [*halves, *lands], n, lambda refs, ss, rs: _sibling_copies(refs[:n], refs[n:], ss, rs),
        after)
    return (send_sems, recv_sems, bufs), token


def _sibling_exchange_wait(name, state, after):
    send_sems, recv_sems, bufs = state
    n = len(bufs) // 2
    bufs = _split_wait("sibling_exchange_wait_" + name, send_sems, recv_sems, bufs, after,
                       lambda refs, ss, rs: _sibling_copies(refs[:n], refs[n:], ss, rs))
    return bufs[:n], bufs[n:]


N_DEV = 8


def _peer_copies(bufs, send_sems, recv_sems):
    vec, land = bufs
    x, y, c = _me()
    return [pltpu.make_async_remote_copy(src_ref=vec, dst_ref=land.at[4 * x + 2 * y + c], send_sem=send_sems.at[p - 1],
                                         recv_sem=recv_sems.at[p - 1], device_id=(x ^ (p >> 2), y ^ ((p >> 1) & 1), c ^ (p & 1)),
                                         device_id_type=MESH_ID) for p in range(1, N_DEV)]


def _allreduce_small_start(vec, after):
    land = jnp.zeros((N_DEV,) + vec.shape, F32)
    send_sems, recv_sems, bufs, _ = _split_start("allreduce_small_start", [vec, land], N_DEV - 1, _peer_copies, after)
    return send_sems, recv_sems, bufs


def _allreduce_small_wait(state, chip, core, after):
    send_sems, recv_sems, bufs = state
    vec, land = _split_wait("allreduce_small_wait", send_sems, recv_sems, bufs, after, _peer_copies)

    def body(me_ref, v_ref, l_ref, o_ref):
        acc = None
        for k in range(N_DEV):
            term = jnp.where(me_ref[0] == k, v_ref[...], l_ref[k])
            acc = term if acc is None else acc + term
        o_ref[...] = acc

    return pl.pallas_call(
        body, name="allreduce_small_sum",
        grid_spec=pltpu.PrefetchScalarGridSpec(
            num_scalar_prefetch=1, grid=(1,),
            in_specs=[pl.BlockSpec(vec.shape, lambda i, me_ref: (0, 0)), pl.BlockSpec(land.shape, lambda i, me_ref: (0, 0, 0))],
            out_specs=pl.BlockSpec(vec.shape, lambda i, me_ref: (0, 0))),
        out_shape=jax.ShapeDtypeStruct(vec.shape, F32), compiler_params=_params(("arbitrary",)),
    )(_scalar(2 * chip + core), vec, land)


def _adam_math(w, g, m, v):
    m = ADAM_B1 * m + (1.0 - ADAM_B1) * g
    v = ADAM_B2 * v + (1.0 - ADAM_B2) * (g * g)
    m_hat = m / (1.0 - ADAM_B1 ** ADAM_STEP)
    v_hat = v / (1.0 - ADAM_B2 ** ADAM_STEP)
    return -ADAM_LR * (m_hat / (jnp.sqrt(v_hat) + ADAM_EPS) + ADAM_WD * w), m, v


def _adamw(name, w, g, m, v):
    R, C = w.shape
    tr = _row_block(R, C, 8)

    def body(w_ref, g_ref, m_ref, v_ref, d_ref, nm_ref, nv_ref):
        d_ref[...], nm_ref[...], nv_ref[...] = _adam_math(w_ref[...], g_ref[...], m_ref[...], v_ref[...])

    blk = pl.BlockSpec((tr, C), lambda i: (i, 0))
    return pl.pallas_call(
        body, name=name, grid=(R // tr,), in_specs=[blk] * 4, out_specs=[blk] * 3,
        out_shape=[jax.ShapeDtypeStruct((R, C), F32)] * 3, compiler_params=_params(("parallel",)),
    )(w, g, m, v)


def _adamw_halves(name, w, mine, theirs, m, v, c):
    rs, cs = w.shape[0] * w.shape[1], w.shape[2]
    (br, bc), nb, whole, half = _half_blocks(rs, cs, 8)
    spec, get, put = _shard_blocks(w, br, bc)

    def body(c_ref, w_ref, a_ref, b_ref, m_ref, v_ref, g_ref, d_ref, nm_ref, nv_ref):
        g = jnp.where(pl.program_id(0) == c_ref[0], a_ref[...], b_ref[...])
        put(g_ref, g)
        for ref, val in zip((d_ref, nm_ref, nv_ref), _adam_math(get(w_ref), g, get(m_ref), get(v_ref))):
            put(ref, val)

    full = spec(lambda s, i, c_ref: whole(s, i))
    part = pl.BlockSpec((br, bc), lambda s, i, c_ref: half(i))
    return pl.pallas_call(
        body, name=name,
        grid_spec=pltpu.PrefetchScalarGridSpec(num_scalar_prefetch=1, grid=(2, nb), in_specs=[full, part, part, full, full],
                                               out_specs=[full] * 4),
        out_shape=[jax.ShapeDtypeStruct(w.shape, F32)] * 4, compiler_params=_params(("parallel", "parallel")),
    )(_scalar(c), w, mine, theirs, m, v)


def _pack_small(arrs, lanes=LANE):
    flat = jnp.concatenate([a.reshape(-1) for a in arrs])
    n = -(-flat.shape[0] // (8 * lanes)) * 8 * lanes
    return jnp.pad(flat, (0, n - flat.shape[0])).reshape(8, n // 8)


def _unpack_small(vec, shapes):
    flat, out, off = vec.reshape(-1), [], 0
    for s in shapes:
        out.append(flat[off:off + s[0] * s[1]].reshape(s))
        off += s[0] * s[1]
    return out


class _LateWeights:
    def __init__(self, cfg, tag, names, staged, after):
        self.cfg, self.tag, self.names, self.k = cfg, tag, names, 3 * len(names)
        self.send, self.recv, self.bufs, self.token = _split_start(f"gather_{tag}_chips_start", staged, self.k, _gather_to_chips,
                                                                    after)

    def pass_on(self, after):
        bufs = _split_wait(f"gather_{self.tag}_chips_wait", self.send, self.recv, self.bufs, after, _gather_to_chips)
        self.send, self.recv, self.bufs, token = _split_start(f"gather_{self.tag}_sibling_start", bufs, self.k, _gather_to_sibling,
                                                               self.token)
        return token

    def arrived(self, after):
        bufs = _split_wait(f"gather_{self.tag}_sibling_wait", self.send, self.recv, self.bufs, after, _gather_to_sibling)
        return {n: _gathered_to_kernel(self.cfg, n, b) for n, b in zip(self.names, bufs)}


def _step(cfg, a):
    chip = 2 * lax.axis_index("x") + lax.axis_index("y")
    core = lax.axis_index("c")
    big = BIG

    ffn = ("w_gate", "w_up", "w_down")
    first = ("w_in", "w_uq", "w_ukv")
    sp = {n: a[n] for n in SMALL}
    sharded = _pack_small([a[n] for n in SMALL_SHARDED], 2 * LANE)
    slabs = jnp.where(lax.broadcasted_iota(I32, (N_CHIPS,) + sharded.shape, 0) == chip, sharded[None], 0.0)
    staged = {"w_in": _stage_shard("w_in", a["w_in"], chip)}
    in_weight = _LateWeights(cfg, "in", ("w_in", "sharded_small"), [staged["w_in"], slabs], jnp.zeros((8, LANE), F32))
    behind = in_weight.token
    for n in big[1:]:
        behind = staged[n] = _stage_shard(n, a[n], chip, behind)
    in_weight.pass_on(behind)
    xn_early = _rms_pre(cfg, a["x"], sp["mix_pre_g"] + in_weight.token[0, 0])
    W = in_weight.arrived(xn_early)
    allp = W.pop("sharded_small").reshape((N_CHIPS,) + sharded.shape)
    per_chip = [_unpack_small(allp[ch], [a[n].shape for n in SMALL_SHARDED]) for ch in range(N_CHIPS)]
    for k, n in enumerate(SMALL_SHARDED):
        sp[n] = jnp.concatenate([per_chip[ch][k] for ch in range(N_CHIPS)], axis=1)

    mla_weights = _LateWeights(cfg, "mla", first[1:], [staged[n] for n in first[1:]], W["w_in"])
    out_weight = _LateWeights(cfg, "out", ("w_out",), [staged["w_out"]], mla_weights.token)
    ffn_weights = _LateWeights(cfg, "ffn", ffn[:2], [staged[n] for n in ffn[:2]], out_weight.token)
    down_weight = _LateWeights(cfg, "down", ffn[2:], [staged[n] for n in ffn[2:]], ffn_weights.token)

    state = {}

    def ffn_grads_ready(grads):
        state["ffn_pairs"], token = _pair_exchange_start("ffn", [_grad_to_chips(cfg, n, grads[n]) for n in ffn_grads])
        return token

    def pair_sums(names, grads, theirs):
        return [_pair_sum(n, g, t, core) for n, g, t in zip(names, grads, theirs)]

    def early_grads_ready(grads):
        out_pairs, token = _pair_exchange_start("out", [_grad_to_chips(cfg, "w_out", grads["w_out"])])
        sums = pair_sums(ffn_grads, *_pair_exchange_wait("ffn", state["ffn_pairs"], token))
        sums += pair_sums(["w_out"], *_pair_exchange_wait("out", out_pairs, sums[-1]))
        state["early"] = _chip_exchange_start("early", sums)
        return state["early"][-1]

    def reduced_halves(tag, names, after):
        send_sems, recv_sems, s_bufs, l_bufs, _ = state[tag]
        s_bufs, l_bufs = _chip_exchange_wait(tag, send_sems, recv_sems, s_bufs, l_bufs, after)
        return [_chip_sum(n, s, t, chip) for n, s, t in zip(names, s_bufs, l_bufs)]

    def in_grad_ready(grads, after):
        if grads is not None:
            state["rest_pairs"], token = _pair_exchange_start("rest", [_grad_to_chips(cfg, n, grads[n]) for n in first])
            return token
        state["rest"] = _chip_exchange_start("rest", pair_sums(first, *_pair_exchange_wait("rest", state["rest_pairs"], after)))
        return state["rest"][-1]

    ffn_grads = ("w_down", "w_gate", "w_up")
    early = ffn_grads + ("w_out",)
    loss, grad_x, gW, gs = _local_grads(cfg, a["x"], a["loss_target"], W, sp, mla_weights, out_weight, ffn_weights, down_weight,
                                        ffn_grads_ready, early_grads_ready, in_grad_ready, xn_early, down_weight.token)
    out = {"grad_x": grad_x}

    def adamw(names, mine, theirs):
        for n, gm, gt in zip(names, mine, theirs):
            out["grad_" + n], out["delta_" + n], out["new_m_" + n], out["new_v_" + n] = _adamw_halves(
                "adamw_" + n, a[n], gm, gt, a["m_" + n], a["v_" + n], core)

    send_sems, recv_sems, s_bufs, l_bufs, _ = state["early"]
    s_bufs, l_bufs = _chip_exchange_wait("early", send_sems, recv_sems, s_bufs, l_bufs, grad_x)
    mine = [_chip_sum(early[0], s_bufs[0], l_bufs[0], chip)]
    sooner, token = _sibling_exchange_start("early_first", mine, mine[0])
    mine += [_chip_sum(n, s, t, chip, token) for n, s, t in zip(early[1:], s_bufs[1:], l_bufs[1:])]
    later, _ = _sibling_exchange_start("early", mine[1:], token)
    adamw(early[:1], *_sibling_exchange_wait("early_first", sooner, mine[-1]))
    e_mine, e_theirs = _sibling_exchange_wait("early", later, out["new_v_" + early[0]])
    adamw(early[3:], e_mine[2:], e_theirs[2:])
    mine = reduced_halves("rest", first, out["new_v_" + early[-1]])
    rest, token = _sibling_exchange_start("rest", mine, mine[0])
    small = _allreduce_small_start(_pack_small([gs[n] for n in SMALL] + [loss]), token)
    adamw(early[1:3], e_mine[:2], e_theirs[:2])
    adamw(first, *_sibling_exchange_wait("rest", rest, out["new_v_" + early[2]]))
    shapes = [gs[n].shape for n in SMALL] + [(1, LANE)]
    red = _unpack_small(_allreduce_small_wait(small, chip, core, out["new_v_" + first[-1]]), shapes)
    g_small = dict(zip(SMALL, red[:-1]))
    for n in SMALL_SHARDED:
        cs = a[n].shape[1]
        g_small[n] = lax.dynamic_slice_in_dim(g_small[n], chip * cs, cs, axis=1)
    out["loss"] = red[-1][0, 0]
    sshapes = [a[n].shape for n in SMALL]
    d, nm, nv = _adamw("adamw_small", _pack_small([a[n] for n in SMALL]), _pack_small([g_small[n] for n in SMALL]),
                       _pack_small([a["m_" + n] for n in SMALL]), _pack_small([a["v_" + n] for n in SMALL]))
    for n, dd, mm, vv in zip(SMALL, _unpack_small(d, sshapes), _unpack_small(nm, sshapes), _unpack_small(nv, sshapes)):
        out["grad_" + n], out["delta_" + n], out["new_m_" + n], out["new_v_" + n] = g_small[n], dd, mm, vv
    return out


def kernel(x, mix_pre_g, w_in, q_norm_g, w_uq, kv_norm_g, w_ukv, ssm_conv_w, ssm_conv_b, dt_bias, a_log, d_skip, ssm_norm_g, w_out, mix_post_g, ffn_pre_g, w_gate, w_up, ffn_conv_w, ffn_conv_b, w_down, ffn_post_g, loss_target, m_mix_pre_g, m_w_in, m_q_norm_g, m_w_uq, m_kv_norm_g, m_w_ukv, m_ssm_conv_w, m_ssm_conv_b, m_dt_bias, m_a_log, m_d_skip, m_ssm_norm_g, m_w_out, m_mix_post_g, m_ffn_pre_g, m_w_gate, m_w_up, m_ffn_conv_w, m_ffn_conv_b, m_w_down, m_ffn_post_g, v_mix_pre_g, v_w_in, v_q_norm_g, v_w_uq, v_kv_norm_g, v_w_ukv, v_ssm_conv_w, v_ssm_conv_b, v_dt_bias, v_a_log, v_d_skip, v_ssm_norm_g, v_w_out, v_mix_post_g, v_ffn_pre_g, v_w_gate, v_w_up, v_ffn_conv_w, v_ffn_conv_b, v_w_down, v_ffn_post_g):
    args = dict(locals())
    def given(k, v):
        if k in ("w_in", "m_w_in", "v_w_in"):
            return jnp.transpose(v, (2, 0, 1))
        return v if k.removeprefix("m_").removeprefix("v_") in BIG or v.ndim < 3 else v[0]

    out = _step(_FULL, {k: given(k, v) for k, v in args.items()})
    res = [out["loss"], out["grad_x"][None]]
    for pre in ("grad_", "delta_", "new_m_", "new_v_"):
        for n in WEIGHTS:
            o = out[pre + n]
            res.append(jnp.transpose(o, (1, 2, 0)) if n == "w_in" else o if n in BIG or args[n].ndim < 3 else o[None])
    return tuple(res)
```

```python
import math

import jax
import jax.numpy as jnp
from jax import lax
from jax.experimental import pallas as pl
from jax.experimental.pallas import tpu as pltpu

F32, BF16, I32 = jnp.float32, jnp.bfloat16, jnp.int32
NN = (((1,), (0,)), ((), ()))
NT = (((1,), (1,)), ((), ()))
TN = (((0,), (0,)), ((), ()))
HI = lax.Precision.HIGHEST
MESH_ID = pl.DeviceIdType.MESH

EPS = 1e-6
CHUNK = 64
NOPE, ROPE, VH = 128, 64, 128
ROPE_THETA = 10000.0
HP, NST = 64, 128
SSM_K, FFN_K = 4, 3
LANE = 128
N_CHIPS = 4
VMEM_LIMIT = 52 * 1024 * 1024
MM_TILE, MM_TILE_K = 1408, 2816

ADAM_LR, ADAM_B1, ADAM_B2, ADAM_EPS, ADAM_WD, ADAM_STEP = 0.001, 0.9, 0.999, 1e-08, 0.01, 10


class _Cfg:
    def __init__(self, S, D, QL, KVL, H, HS, G, DFF, T):
        self.S, self.D, self.QL, self.KVL, self.H, self.HS, self.G, self.DFF, self.T = S, D, QL, KVL, H, HS, G, DFF, T
        self.INNER = HS * HP
        self.CONVCH = self.INNER + 2 * G * NST
        self.QW = H * (NOPE + ROPE)
        self.KVW = H * (NOPE + VH)
        self.MLAW = H * VH
        self.MIXW = self.MLAW + self.INNER
        self.IN_COLS = QL + KVL + ROPE + self.INNER + self.CONVCH + HS
        natural, at = {}, 0
        for name, w in (("c_q", QL), ("c_kv", KVL), ("kr", ROPE), ("z", self.INNER), ("xbc", self.CONVCH), ("dt", HS)):
            natural[name] = (at, w)
            at += w
        self.seg, taken = {}, []
        for name in sorted(natural, key=lambda n: -natural[n][1]):
            w = -(-natural[name][1] // LANE) * LANE
            off = next(o for o in range(0, self.IN_COLS * 2, w) if all(o + w <= t or o >= t + tw for t, tw in taken))
            taken.append((off, w))
            self.seg[name] = (off, w) + natural[name]
        self.EXT = max(o + w for o, w in taken)
        self.NPAIR = HS // 2
        self.REP = HS // G

    def window(self, name):
        off, w, _, _ = self.seg[name]
        return w, off // w


_FULL = _Cfg(S=2048, D=2048, QL=768, KVL=512, H=8, HS=16, G=2, DFF=5632, T=256)
BIG = ("w_in", "w_uq", "w_ukv", "w_out", "w_gate", "w_up", "w_down")

SMALL = ("mix_pre_g", "q_norm_g", "kv_norm_g", "ssm_conv_w", "ssm_conv_b", "dt_bias", "a_log", "d_skip", "ssm_norm_g",
         "mix_post_g", "ffn_pre_g", "ffn_conv_w", "ffn_conv_b", "ffn_post_g")
SMALL_SHARDED = ("ssm_conv_w", "ffn_conv_w")
WEIGHTS = ("mix_pre_g", "w_in", "q_norm_g", "w_uq", "kv_norm_g", "w_ukv", "ssm_conv_w", "ssm_conv_b", "dt_bias", "a_log",
           "d_skip", "ssm_norm_g", "w_out", "mix_post_g", "ffn_pre_g", "w_gate", "w_up", "ffn_conv_w", "ffn_conv_b",
           "w_down", "ffn_post_g")


def _pick(n, target, mult):
    best = None
    for d in range(mult, min(n, target) + 1, mult):
        if n % d == 0:
            best = d
    return best if best is not None else n


def _params(sem=None):
    kw = dict(vmem_limit_bytes=VMEM_LIMIT)
    if sem is not None:
        kw["dimension_semantics"] = sem
    return pltpu.CompilerParams(**kw)


def _dot(a, b, dims=NN, precision=None):
    return lax.dot_general(a, b, dims, preferred_element_type=F32, precision=precision)


def _sigmoid(x):
    return 1.0 / (1.0 + jnp.exp(-x))


def _rs(x):
    return lax.rsqrt(jnp.mean(x * x, axis=-1, keepdims=True) + EPS)


def _rms_back(xh, r, dn):
    return r * (dn - xh * jnp.mean(dn * xh, axis=-1, keepdims=True))


def _colsum(v):
    return jnp.sum(v, axis=0, keepdims=True)


def _matmul(name, a, b, mode, out_dtype, a2=None, b2=None, chips=False, after=None):
    cs = None
    if mode == "nn":
        (M, K), N = a.shape, b.shape[-1]
        if chips:
            cs, N = N, N_CHIPS * N
    elif mode == "nt":
        (M, K), N = a.shape, b.shape[-2]
        if chips:
            cs = b.shape[-1]
    else:
        (K, M), N = a.shape, b.shape[1]
        if chips:
            cs = N // N_CHIPS
    tm = _pick(M, MM_TILE, LANE)
    tn = _pick(cs if chips and mode != "nt" else N, MM_TILE, LANE)
    tk = _pick(cs, MM_TILE, LANE) if chips and mode == "nt" else _pick(K, MM_TILE_K, LANE)
    nk = K // tk
    dims = {"nn": NN, "nt": NT, "tn": TN}[mode]
    a_spec = pl.BlockSpec((tk, tm), lambda i, j, k: (k, i)) if mode == "tn" else pl.BlockSpec((tm, tk), lambda i, j, k: (i, k))
    b_spec = pl.BlockSpec((tn, tk), lambda i, j, k: (j, k)) if mode == "nt" else pl.BlockSpec((tk, tn), lambda i, j, k: (k, j))
    o_spec = pl.BlockSpec((tm, tn), lambda i, j, k: (i, j))
    o_shape = (M, N)
    if chips and mode == "nn":
        per = cs // tn
        b_spec = pl.BlockSpec((None, tk, tn), lambda i, j, k: (j // per, k, j % per))
    elif chips and mode == "nt":
        per = cs // tk
        b_spec = pl.BlockSpec((None, tn, tk), lambda i, j, k: (k // per, j, k % per))
    elif chips:
        per = cs // tn
        o_spec = pl.BlockSpec((None, tm, tn), lambda i, j, k: (j // per, i, j % per))
        o_shape = (N_CHIPS, M, cs)
    two = a2 is not None

    def product(refs):
        part = _dot(refs[0][...].astype(BF16), refs[1][...].astype(BF16), dims)
        if two:
            part += _dot(refs[2][...].astype(BF16), refs[3][...].astype(BF16), dims)
        return part

    def body_whole_k(*refs):
        refs[-1][...] = product(refs).astype(refs[-1].dtype)

    def body(*refs):
        o_ref, acc_ref = refs[-2], refs[-1]
        k = pl.program_id(2)

        @pl.when(k == 0)
        def _():
            acc_ref[...] = product(refs)

        @pl.when(k > 0)
        def _():
            acc_ref[...] += product(refs)

        @pl.when(k == nk - 1)
        def _():
            o_ref[...] = acc_ref[...].astype(o_ref.dtype)

    ins = ((a, b, a2, b2) if two else (a, b)) + (() if after is None else (after,))
    return pl.pallas_call(
        body_whole_k if nk == 1 else body, name=name, grid=(M // tm, N // tn, nk),
        in_specs=[a_spec, b_spec] * (2 if two else 1) + ([] if after is None else [pl.BlockSpec(memory_space=pl.ANY)]),
        out_specs=o_spec,
        out_shape=jax.ShapeDtypeStruct(o_shape, out_dtype),
        scratch_shapes=[] if nk == 1 else [pltpu.VMEM((tm, tn), F32)],
        compiler_params=_params(("parallel", "parallel", "arbitrary")),
    )(*ins)


def _matmul_twin(name, a, b1, b2, mode, out_dtype):
    if mode == "nn":
        (M, K), cs = a.shape, b1.shape[-1]
        tm = _pick(M, MM_TILE // 2, LANE)
    else:
        (K, M), cs = a.shape, b1.shape[1] // N_CHIPS
        tm = _pick(M, MM_TILE, LANE)
    tn = _pick(cs, MM_TILE, LANE)
    per = cs // tn
    dims = NN if mode == "nn" else TN

    def body(a_ref, b1_ref, b2_ref, o1_ref, o2_ref):
        lhs = a_ref[...].astype(BF16)
        o1_ref[...] = _dot(lhs, b1_ref[...].astype(BF16), dims).astype(o1_ref.dtype)
        o2_ref[...] = _dot(lhs, b2_ref[...].astype(BF16), dims).astype(o2_ref.dtype)

    if mode == "nn":
        a_spec = pl.BlockSpec((tm, K), lambda i, j: (i, 0))
        b_spec = pl.BlockSpec((None, K, tn), lambda i, j: (j // per, 0, j % per))
        o_spec, o_shape = pl.BlockSpec((tm, tn), lambda i, j: (i, j)), (M, N_CHIPS * cs)
    else:
        a_spec = pl.BlockSpec((K, tm), lambda i, j: (0, i))
        b_spec = pl.BlockSpec((K, tn), lambda i, j: (0, j))
        o_spec, o_shape = pl.BlockSpec((None, tm, tn), lambda i, j: (j // per, i, j % per)), (N_CHIPS, M, cs)
    return pl.pallas_call(
        body, name=name, grid=(M // tm, N_CHIPS * per), in_specs=[a_spec, b_spec, b_spec], out_specs=[o_spec, o_spec],
        out_shape=[jax.ShapeDtypeStruct(o_shape, out_dtype)] * 2, compiler_params=_params(("parallel", "parallel")),
    )(a, b1, b2)


def _window(a):
    return (a[0], *a[1]) if isinstance(a, tuple) else (a, a.shape[1], 0)


def _rowwise(name, fn, rows, mats, outs, reds, ts):
    rows, widths, blocks = zip(*[_window(a) for a in rows])
    S = rows[0].shape[0]
    nr, nm, no = len(rows), len(mats), len(outs)

    def body(*refs):
        res = fn(*[r[...] for r in refs[:nr + nm]])
        res = res if isinstance(res, (tuple, list)) else (res,)
        for r, v in zip(refs[nr + nm:nr + nm + no], res[:no]):
            r[...] = v.astype(r.dtype)
        first = pl.program_id(0) == 0
        for r, v in zip(refs[nr + nm + no:], res[no:]):
            @pl.when(first)
            def _():
                r[...] = jnp.broadcast_to(v, r.shape)

            @pl.when(jnp.logical_not(first))
            def _():
                r[...] += jnp.broadcast_to(v, r.shape)

    in_specs = [pl.BlockSpec((ts, w), lambda i, b=b: (i, b)) for w, b in zip(widths, blocks)]
    in_specs += [pl.BlockSpec(m.shape, lambda i, nd=m.ndim: (0,) * nd) for m in mats]
    out_specs = [pl.BlockSpec((ts, w), lambda i: (i, 0)) for w, _ in outs]
    out_specs += [pl.BlockSpec(s, lambda i: (0, 0)) for s in reds]
    out_shape = [jax.ShapeDtypeStruct((S, w), dt) for w, dt in outs] + [jax.ShapeDtypeStruct(s, F32) for s in reds]
    return pl.pallas_call(
        body, name=name, grid=(S // ts,), in_specs=in_specs, out_specs=out_specs, out_shape=out_shape,
        compiler_params=_params(("arbitrary",) if reds else ("parallel",)),
    )(*rows, *mats)


def _shift_down(v, s):
    if s == 0:
        return v
    rows = lax.broadcasted_iota(I32, v.shape, 0)
    return jnp.where(rows >= s, pltpu.roll(v, s, 0), 0.0)


def _shift_up(v, s):
    if s == 0:
        return v
    n = v.shape[0]
    rows = lax.broadcasted_iota(I32, v.shape, 0)
    return jnp.where(rows < n - s, pltpu.roll(v, n - s, 0), 0.0)


def _conv(x, w, b):
    K = w.shape[0]
    y = jnp.broadcast_to(b, x.shape)
    for k in range(K):
        y = y + w[k:k + 1, :] * _shift_down(x, K - 1 - k)
    return y


def _conv_back(x, w, dc):
    K = w.shape[0]
    dx = jnp.zeros_like(x)
    dw = []
    for k in range(K):
        up = _shift_up(dc, K - 1 - k)
        dx = dx + w[k:k + 1, :] * up
        dw.append(_colsum(up * x))
    return dx, jnp.concatenate(dw, axis=0), _colsum(dc)


def _colwise(name, fn, cols, vecs, outs, pouts, tc):
    cols, widths, blocks = zip(*[_window(a) for a in cols])
    S, C = cols[0].shape[0], widths[0]
    firsts = [b * (C // tc) for b in blocks]
    nc_, nv, no = len(cols), len(vecs), len(outs)

    def body(*refs):
        res = fn(*[r[...] for r in refs[:nc_ + nv]])
        res = res if isinstance(res, (tuple, list)) else (res,)
        for r, v in zip(refs[nc_ + nv:], res):
            r[...] = v.astype(r.dtype)

    in_specs = [pl.BlockSpec((S, tc), lambda j, f=f: (0, f + j)) for f in firsts]
    in_specs += [pl.BlockSpec((v.shape[0], tc), lambda j: (0, j)) for v in vecs]
    out_specs = [pl.BlockSpec((S, tc), lambda j: (0, j)) for _ in outs] + [pl.BlockSpec((k, tc), lambda j: (0, j)) for k in pouts]
    out_shape = [jax.ShapeDtypeStruct((S, C), dt) for dt in outs] + [jax.ShapeDtypeStruct((k, C), F32) for k in pouts]
    return pl.pallas_call(
        body, name=name, grid=(C // tc,), in_specs=in_specs, out_specs=out_specs, out_shape=out_shape,
        compiler_params=_params(("parallel",)),
    )(*cols, *vecs)


_G0, _G1 = math.sqrt(2.0 / math.pi), 0.044715


def _gelu(g):
    th = jnp.tanh(_G0 * (g + _G1 * g * g * g))
    return 0.5 * g * (1.0 + th), th


def _ffn_act(gate_pre, up, w, b):
    act, _ = _gelu(_conv(gate_pre, w, b))
    return act * up


def _ffn_act_back(dact, gate_pre, up, w, b):
    g = _conv(gate_pre, w, b)
    ge, th = _gelu(g)
    dge = 0.5 * (1.0 + th) + 0.5 * g * (1.0 - th * th) * _G0 * (1.0 + 3.0 * _G1 * g * g)
    dup = dact * ge
    dgate_pre, dw, db = _conv_back(gate_pre, w, dact * up * dge)
    return dgate_pre, dup, dw, db


def _ssm_act(xbc, w, b):
    c = _conv(xbc, w, b)
    return c * _sigmoid(c)


def _ssm_act_back(dxc, xbc, w, b):
    c = _conv(xbc, w, b)
    sg = _sigmoid(c)
    return _conv_back(xbc, w, dxc * sg * (1.0 + c * (1.0 - sg)))


def _rope_tables(S):
    inv = 1.0 / (ROPE_THETA ** (jnp.arange(0, ROPE, 2, dtype=F32) / ROPE))
    ang = jnp.arange(S, dtype=F32)[:, None] * inv[None, :]
    cos, sin = jnp.cos(ang), jnp.sin(ang)
    return jnp.tile(cos, (1, 4)), jnp.tile(jnp.concatenate([-sin, sin], axis=1), (1, 2))


def _swap_halves(x):
    lane = lax.broadcasted_iota(I32, x.shape, 1)
    w = x.shape[1]
    return jnp.where((lane % ROPE) < ROPE // 2, pltpu.roll(x, w - ROPE // 2, 1), pltpu.roll(x, ROPE // 2, 1))


def _rot(x, cos2, sin2):
    return x * cos2 + _swap_halves(x) * sin2


def _rot_back(dy, cos2, sin2):
    return dy * cos2 + _swap_halves(dy * sin2)


def _mla_pack(cfg, q, kv, kr, cos2, sin2):
    S, H = cfg.S, cfg.H
    ts = _pick(S, 256, 8)
    kr, _, kr_block = _window(kr)

    def body(q_ref, kv_ref, kr_ref, c_ref, s_ref, Q_ref, K_ref, V_ref):
        c2, s2 = c_ref[...], s_ref[...]
        krr = _rot(kr_ref[...], c2, s2)
        kr_half = (krr.astype(BF16), pltpu.roll(krr, ROPE, 1).astype(BF16))
        for j in range(H // 2):
            qr = _rot(q_ref[:, (H + j) * LANE:(H + j + 1) * LANE], c2, s2).astype(BF16)
            for h in (2 * j, 2 * j + 1):
                Q_ref[h, :, 0:LANE] = q_ref[:, h * LANE:(h + 1) * LANE].astype(BF16)
                Q_ref[h, :, LANE:] = qr
                K_ref[h, :, 0:LANE] = kv_ref[:, h * LANE:(h + 1) * LANE].astype(BF16)
                K_ref[h, :, LANE:] = kr_half[h % 2]
                V_ref[h] = kv_ref[:, (H + h) * LANE:(H + h + 1) * LANE].astype(BF16)

    tab = pl.BlockSpec((ts, LANE), lambda i: (i, 0))
    heads = lambda w: pl.BlockSpec((H, ts, w), lambda i: (0, i, 0))
    return pl.pallas_call(
        body, name="mla_pack", grid=(S // ts,),
        in_specs=[pl.BlockSpec((ts, cfg.QW), lambda i: (i, 0)), pl.BlockSpec((ts, cfg.KVW), lambda i: (i, 0)),
                  pl.BlockSpec((ts, LANE), lambda i: (i, kr_block)), tab, tab],
        out_specs=[heads(2 * LANE), heads(2 * LANE), heads(LANE)],
        out_shape=[jax.ShapeDtypeStruct((H, S, 2 * LANE), BF16), jax.ShapeDtypeStruct((H, S, 2 * LANE), BF16),
                   jax.ShapeDtypeStruct((H, S, LANE), BF16)],
        compiler_params=_params(("parallel",)),
    )(q, kv, kr, cos2, sin2)


def _mla_unpack(cfg, dQ, dK, dV, cos2, sin2):
    S, H = cfg.S, cfg.H
    ts = _pick(S, 256, 8)

    def body(dQ_ref, dK_ref, dV_ref, c_ref, s_ref, dq_ref, dkv_ref, dkr_ref):
        c2, s2 = c_ref[...], s_ref[...]
        lo = lax.broadcasted_iota(I32, (ts, LANE), 1) < ROPE
        tk = jnp.zeros((ts, LANE), F32)
        for h in range(H):
            dq_ref[:, h * LANE:(h + 1) * LANE] = dQ_ref[h, :, 0:LANE].astype(BF16)
            dkv_ref[:, h * LANE:(h + 1) * LANE] = dK_ref[h, :, 0:LANE].astype(BF16)
            dkv_ref[:, (H + h) * LANE:(H + h + 1) * LANE] = dV_ref[h].astype(BF16)
            own = lo if h % 2 == 0 else jnp.logical_not(lo)
            tk = tk + jnp.where(own, dK_ref[h, :, LANE:], 0.0)
        for j in range(H // 2):
            dr = dQ_ref[2 * j, :, LANE:] + dQ_ref[2 * j + 1, :, LANE:]
            dq_ref[:, (H + j) * LANE:(H + j + 1) * LANE] = _rot_back(dr, c2, s2).astype(BF16)
        dkr_rot = jnp.where(lo, tk + pltpu.roll(tk, ROPE, 1), 0.0)
        dkr_ref[...] = _rot_back(dkr_rot, c2, s2).astype(BF16)

    tab = pl.BlockSpec((ts, LANE), lambda i: (i, 0))
    return pl.pallas_call(
        body, name="mla_unpack", grid=(S // ts,),
        in_specs=[pl.BlockSpec((H, ts, 2 * LANE), lambda i: (0, i, 0)), pl.BlockSpec((H, ts, 2 * LANE), lambda i: (0, i, 0)),
                  pl.BlockSpec((H, ts, LANE), lambda i: (0, i, 0)), tab, tab],
        out_specs=[pl.BlockSpec((ts, cfg.QW), lambda i: (i, 0)), pl.BlockSpec((ts, cfg.KVW), lambda i: (i, 0)), tab],
        out_shape=[jax.ShapeDtypeStruct((S, cfg.QW), BF16), jax.ShapeDtypeStruct((S, cfg.KVW), BF16),
                   jax.ShapeDtypeStruct((S, LANE), BF16)],
        compiler_params=_params(("parallel",)),
    )(dQ, dK, dV, cos2, sin2)


_ATT_T = 256
_ATT_HB = 8
_ATT_SCALE = (NOPE + ROPE) ** -0.5


def _diag_mask(transposed=False):
    r = lax.broadcasted_iota(I32, (_ATT_T, _ATT_T), 0) // CHUNK
    c = lax.broadcasted_iota(I32, (_ATT_T, _ATT_T), 1) // CHUNK
    return r <= c if transposed else c <= r


def _row_form(col):
    return jnp.broadcast_to(col, (col.shape[0], LANE)).T[0:8, :]


def _attn_fwd(cfg, Q, K, V):
    S, H, T, HB = cfg.S, cfg.H, _ATT_T, min(cfg.H, _ATT_HB)

    def body(q_ref, k_ref, v_ref, o_ref, lse_t_ref):
        qi = pl.program_id(1)

        def head_step(b, kb, carry, mask):
            m, l, acc = carry
            ks = pl.multiple_of(kb * T, T)
            s = _dot(q_ref[b], k_ref[b, pl.ds(ks, T), :], NT) * _ATT_SCALE
            if mask is not None:
                s = jnp.where(mask, s, -1e30)
            m_new = jnp.maximum(m, jnp.max(s, axis=1, keepdims=True))
            p = jnp.exp(s - m_new)
            alpha = jnp.exp(m - m_new)
            l = alpha * l + jnp.sum(p, axis=1, keepdims=True)
            acc = alpha * acc + _dot(p.astype(BF16), v_ref[b, pl.ds(ks, T), :])
            return m_new, l, acc

        def step(kb, carry, mask=None):
            return tuple(head_step(b, kb, carry[b], mask) for b in range(HB))

        init = (jnp.full((T, 1), -1e30, F32), jnp.zeros((T, 1), F32), jnp.zeros((T, VH), F32))
        done = step(qi, lax.fori_loop(0, qi, step, (init,) * HB), _diag_mask())
        for b, (m, l, acc) in enumerate(done):
            o_ref[:, b * LANE:(b + 1) * LANE] = acc / l
            lse_t_ref[b] = _row_form(m + jnp.log(l))

    return pl.pallas_call(
        body, name="attn_fwd", grid=(H // HB, S // T),
        in_specs=[pl.BlockSpec((HB, T, 2 * LANE), lambda h, i: (h, i, 0)), pl.BlockSpec((HB, S, 2 * LANE), lambda h, i: (h, 0, 0)),
                  pl.BlockSpec((HB, S, LANE), lambda h, i: (h, 0, 0))],
        out_specs=[pl.BlockSpec((T, HB * LANE), lambda h, i: (i, h)), pl.BlockSpec((HB, 8, T), lambda h, i: (h, 0, i))],
        out_shape=[jax.ShapeDtypeStruct((S, H * LANE), F32), jax.ShapeDtypeStruct((H, 8, S), F32)],
        compiler_params=_params(("parallel", "parallel")),
    )(Q, K, V)


def _attn_delta(cfg, do, o, after):
    S, H, T = cfg.S, cfg.H, _ATT_T

    def body(do_ref, o_ref, after_ref, dl_t_ref):
        for h in range(H):
            sl = slice(h * LANE, (h + 1) * LANE)
            dl_t_ref[h] = _row_form(jnp.sum(do_ref[:, sl] * o_ref[:, sl], axis=1, keepdims=True))

    wide = pl.BlockSpec((T, H * LANE), lambda i: (i, 0))
    return pl.pallas_call(
        body, name="attn_delta", grid=(S // T,), in_specs=[wide, wide, _ANY],
        out_specs=pl.BlockSpec((H, 8, T), lambda i: (0, 0, i)), out_shape=jax.ShapeDtypeStruct((H, 8, S), F32),
        compiler_params=_params(("parallel",)),
    )(do, o, after)


_ATT_HB_BWD = 4


def _attn_bwd(cfg, Q, K, V, do, lse_t, delta_t):
    S, H, T, HB = cfg.S, cfg.H, _ATT_T, min(cfg.H, _ATT_HB_BWD)
    nq = S // T

    def body(q_ref, k_ref, v_ref, do_ref, lse_ref, dl_ref, dq_ref, dk_ref, dv_ref):
        kb = pl.program_id(1)

        @pl.when(kb == 0)
        def _():
            dq_ref[...] = jnp.zeros_like(dq_ref)

        def head_step(b, qi, carry, mask):
            dk, dv = carry
            qs = pl.multiple_of(qi * T, T)
            q = q_ref[b, pl.ds(qs, T), :]
            k = k_ref[b]
            dob = do_ref[pl.ds(qs, T), b * LANE:(b + 1) * LANE].astype(BF16)
            s = _dot(k, q, NT) * _ATT_SCALE
            if mask is not None:
                s = jnp.where(mask, s, -1e30)
            p = jnp.exp(s - lse_ref[b, 0:1, pl.ds(qs, T)])
            dv = dv + _dot(p.astype(BF16), dob)
            dp = _dot(v_ref[b], dob, NT)
            ds = (p * (dp - dl_ref[b, 0:1, pl.ds(qs, T)]) * _ATT_SCALE).astype(BF16)
            dk = dk + _dot(ds, q)
            dq_ref[b, pl.ds(qs, T), :] += _dot(ds, k, TN)
            return dk, dv

        def step(qi, carry, mask=None):
            return tuple(head_step(b, qi, carry[b], mask) for b in range(HB))

        zero = (jnp.zeros((T, 2 * LANE), F32), jnp.zeros((T, VH), F32))
        done = lax.fori_loop(kb + 1, nq, step, step(kb, (zero,) * HB, _diag_mask(transposed=True)))
        for b, (dk, dv) in enumerate(done):
            dk_ref[b] = dk
            dv_ref[b] = dv

    row = pl.BlockSpec((HB, 8, S), lambda h, j: (h, 0, 0))
    whole = pl.BlockSpec((HB, S, 2 * LANE), lambda h, j: (h, 0, 0))
    return pl.pallas_call(
        body, name="attn_bwd", grid=(H // HB, S // T),
        in_specs=[whole, pl.BlockSpec((HB, T, 2 * LANE), lambda h, j: (h, j, 0)), pl.BlockSpec((HB, T, LANE), lambda h, j: (h, j, 0)),
                  pl.BlockSpec((S, HB * LANE), lambda h, j: (0, h)), row, row],
        out_specs=[whole, pl.BlockSpec((HB, T, 2 * LANE), lambda h, j: (h, j, 0)), pl.BlockSpec((HB, T, LANE), lambda h, j: (h, j, 0))],
        out_shape=[jax.ShapeDtypeStruct((H, S, 2 * LANE), F32), jax.ShapeDtypeStruct((H, S, 2 * LANE), F32),
                   jax.ShapeDtypeStruct((H, S, LANE), F32)],
        compiler_params=_params(("parallel", "arbitrary")),
    )(Q, K, V, do, lse_t, delta_t)


def _expand_matrix(cfg):
    r = lax.broadcasted_iota(I32, (LANE, cfg.INNER), 0)
    c = lax.broadcasted_iota(I32, (LANE, cfg.INNER), 1)
    return (r == c // HP).astype(F32)


def _softplus(x):
    return jnp.maximum(x, 0.0) + jnp.log(1.0 + jnp.exp(-jnp.abs(x)))


def _ssd_prep(cfg, dt_raw, dt_bias_pad, a_log_pad, expand):
    HS = cfg.HS

    def fn(raw, bias, alog, E):
        heads = lax.broadcasted_iota(I32, raw.shape, 1) < HS
        dt = jnp.where(heads, _softplus(raw + bias), 0.0)
        a = dt * jnp.where(heads[0:1], -jnp.exp(alog), 0.0)
        return dt, a, _dot(dt, E, precision=HI)

    return _rowwise("ssd_prep", fn, [dt_raw], [dt_bias_pad, a_log_pad, expand],
                    [(LANE, F32), (LANE, F32), (cfg.INNER, F32)], [], _pick(cfg.S, 512, 8))


def _tril(T):
    return lax.broadcasted_iota(I32, (T, T), 0) >= lax.broadcasted_iota(I32, (T, T), 1)


def _ssd_fwd(cfg, xc, dt_exp, a_small, dskip_exp, expand):
    S, T, INNER, G, NPAIR = cfg.S, cfg.T, cfg.INNER, cfg.G, cfg.NPAIR
    NC = S // T

    def body(xc_ref, dte_ref, as_ref, dsk_ref, e_ref, y_ref, hin_ref, ht_ref):
        @pl.when(pl.program_id(0) == 0)
        def _():
            ht_ref[...] = jnp.zeros_like(ht_ref)

        tril = _tril(T)
        tri = tril.astype(F32)
        acs_s = _dot(tri, as_ref[...], precision=HI)
        acs_e = _dot(acs_s, e_ref[...], precision=HI)
        acs_t = acs_s.T
        lo = lax.broadcasted_iota(I32, (T, LANE), 1) < HP
        for g in range(G):
            Bb = xc_ref[:, INNER + g * NST:INNER + (g + 1) * NST].astype(BF16)
            Cb = xc_ref[:, INNER + (G + g) * NST:INNER + (G + g + 1) * NST].astype(BF16)
            Gm = _dot(Cb, Bb, NT)
            for j in range(g * NPAIR // G, (g + 1) * NPAIR // G):
                sl = slice(j * LANE, (j + 1) * LANE)
                Xp = xc_ref[:, sl]
                Xdt = Xp * dte_ref[:, sl]
                Xb = Xdt.astype(BF16)
                acs_p = acs_e[:, sl]
                last = acs_p[T - 1:T, :]
                Hin = ht_ref[j]
                hin_ref[0, j] = Hin
                yd = []
                for e in (0, 1):
                    h = 2 * j + e
                    Lm = jnp.exp(jnp.where(tril, acs_s[:, h:h + 1] - acs_t[h:h + 1, :], -1e30))
                    yd.append(_dot((Gm * Lm).astype(BF16), Xb))
                y_off = _dot(Cb, Hin.astype(BF16)) * jnp.exp(acs_p)
                y_ref[:, sl] = jnp.where(lo, yd[0], yd[1]) + y_off + Xp * dsk_ref[:, sl]
                st = _dot(Bb, (Xdt * jnp.exp(last - acs_p)).astype(BF16), TN)
                ht_ref[j] = jnp.exp(last) * Hin + st

    rows = lambda w: pl.BlockSpec((T, w), lambda c: (c, 0))
    return pl.pallas_call(
        body, name="ssd_fwd", grid=(NC,),
        in_specs=[rows(cfg.CONVCH), rows(INNER), rows(LANE), pl.BlockSpec((1, INNER), lambda c: (0, 0)),
                  pl.BlockSpec((LANE, INNER), lambda c: (0, 0))],
        out_specs=[rows(INNER), pl.BlockSpec((1, NPAIR, NST, LANE), lambda c: (c, 0, 0, 0))],
        out_shape=[jax.ShapeDtypeStruct((S, INNER), F32), jax.ShapeDtypeStruct((NC, NPAIR, NST, LANE), F32)],
        scratch_shapes=[pltpu.VMEM((NPAIR, NST, LANE), F32)],
        compiler_params=_params(("arbitrary",)),
    )(xc, dt_exp, a_small, dskip_exp, expand)


def _ssd_bwd(cfg, dy, xc, dt_exp, a_small, dskip_exp, hin, dt_raw, dt_bias_pad, a_log_pad, expand):
    S, T, INNER, G, NPAIR, HS = cfg.S, cfg.T, cfg.INNER, cfg.G, cfg.NPAIR, cfg.HS
    NC = S // T

    def body(dy_ref, xc_ref, dte_ref, as_ref, dsk_ref, hin_ref, raw_ref, bias_ref, alog_ref, e_ref,
             dxc_ref, draw_ref, dbias_ref, dalog_ref, dskip_ref, dht_ref, cols_ref, rows_ref, dacs_ref, ddt_ref):
        first = pl.program_id(0) == 0

        @pl.when(first)
        def _():
            dht_ref[...] = jnp.zeros_like(dht_ref)

        tril = _tril(T)
        tri = tril.astype(F32)
        a_s = as_ref[...]
        acs_s = _dot(tri, a_s, precision=HI)
        acs_e = _dot(acs_s, e_ref[...], precision=HI)
        acs_t = acs_s.T
        lo = lax.broadcasted_iota(I32, (T, LANE), 1) < HP
        last_row = lax.broadcasted_iota(I32, (T, LANE), 0) == T - 1
        cols_ref[...] = jnp.zeros_like(cols_ref)
        rows_ref[...] = jnp.zeros_like(rows_ref)
        dsk_parts = []
        for g in range(G):
            bsl = slice(INNER + g * NST, INNER + (g + 1) * NST)
            csl = slice(INNER + (G + g) * NST, INNER + (G + g + 1) * NST)
            Bb = xc_ref[:, bsl].astype(BF16)
            Cb = xc_ref[:, csl].astype(BF16)
            Gm = _dot(Cb, Bb, NT)
            dG = jnp.zeros((T, T), F32)
            dB = jnp.zeros((T, NST), F32)
            dC = jnp.zeros((T, NST), F32)
            for j in range(g * NPAIR // G, (g + 1) * NPAIR // G):
                sl = slice(j * LANE, (j + 1) * LANE)
                Xp = xc_ref[:, sl]
                dtp = dte_ref[:, sl]
                Xdt = Xp * dtp
                Xb = Xdt.astype(BF16)
                acs_p = acs_e[:, sl]
                last = acs_p[T - 1:T, :]
                e_p, dec, cd = jnp.exp(acs_p), jnp.exp(last - acs_p), jnp.exp(last)
                Hin = hin_ref[0, j]
                Hb = Hin.astype(BF16)
                dHn = dht_ref[j]
                dHb = dHn.astype(BF16)
                dYp = dy_ref[:, sl]
                z = _dot(Cb, Hb)
                dz = (dYp * e_p).astype(BF16)
                dacs_p = dYp * z * e_p
                dC = dC + _dot(dz, Hb, NT)
                dHin = _dot(Cb, dz, TN) + cd * dHn
                dlast = _colsum(dHn * Hin) * cd
                qv = _dot(Bb, dHb)
                dXdt = qv * dec
                ddec = qv * Xdt * dec
                dacs_p = dacs_p - ddec
                dlast = dlast + _colsum(ddec)
                dB = dB + _dot((Xdt * dec).astype(BF16), dHb, NT)
                for e in (0, 1):
                    h = 2 * j + e
                    Lm = jnp.exp(jnp.where(tril, acs_s[:, h:h + 1] - acs_t[h:h + 1, :], -1e30))
                    Mh = Gm * Lm
                    dYe = jnp.where(lo if e == 0 else jnp.logical_not(lo), dYp, 0.0).astype(BF16)
                    dM = _dot(dYe, Xb, NT)
                    dXdt = dXdt + _dot(Mh.astype(BF16), dYe, TN)
                    W = dM * Mh
                    cols_ref[:, h:h + 1] = jnp.sum(W, axis=1, keepdims=True)
                    rows_ref[h:h + 1, :] = _colsum(W)
                    dG = dG + dM * Lm
                dacs_ref[:, sl] = dacs_p + jnp.where(last_row, dlast, 0.0)
                ddt_ref[:, sl] = dXdt * Xp
                dxc_ref[:, sl] = dXdt * dtp + dYp * dsk_ref[:, sl]
                dsk_parts.append(_colsum(dYp * Xp))
                dht_ref[j] = dHin
            dGb = dG.astype(BF16)
            dxc_ref[:, bsl] = dB + _dot(dGb, Cb, TN)
            dxc_ref[:, csl] = dC + _dot(dGb, Bb)
        E = e_ref[...]
        dacs_s = cols_ref[...] - rows_ref[...].T + _dot(dacs_ref[...], E, NT, precision=HI)
        da = _dot(tri, dacs_s, TN, precision=HI)
        heads = lax.broadcasted_iota(I32, (1, LANE), 1) < HS
        A = jnp.where(heads, -jnp.exp(alog_ref[...]), 0.0)
        ddt = _dot(ddt_ref[...], E, NT, precision=HI) + da * A
        draw = jnp.where(heads, ddt * _sigmoid(raw_ref[...] + bias_ref[...]), 0.0)
        draw_ref[...] = draw
        dsk = _dot(jnp.broadcast_to(jnp.concatenate(dsk_parts, axis=1), (8, INNER)), E, NT, precision=HI)[0:1]
        for ref, val in ((dbias_ref, _colsum(draw)), (dalog_ref, _colsum(da * a_s)), (dskip_ref, dsk)):
            @pl.when(first)
            def _():
                ref[...] = val

            @pl.when(jnp.logical_not(first))
            def _():
                ref[...] += val

    dt_raw, _, raw_block = _window(dt_raw)
    rows = lambda w, b=0: pl.BlockSpec((T, w), lambda c: (NC - 1 - c, b))
    vec = lambda w: pl.BlockSpec((1, w), lambda c: (0, 0))
    return pl.pallas_call(
        body, name="ssd_bwd", grid=(NC,),
        in_specs=[rows(INNER), rows(cfg.CONVCH), rows(INNER), rows(LANE), vec(INNER),
                  pl.BlockSpec((1, NPAIR, NST, LANE), lambda c: (NC - 1 - c, 0, 0, 0)), rows(LANE, raw_block), vec(LANE), vec(LANE),
                  pl.BlockSpec((LANE, INNER), lambda c: (0, 0))],
        out_specs=[rows(cfg.CONVCH), rows(LANE), vec(LANE), vec(LANE), vec(LANE)],
        out_shape=[jax.ShapeDtypeStruct((S, cfg.CONVCH), F32), jax.ShapeDtypeStruct((S, LANE), F32)]
        + [jax.ShapeDtypeStruct((1, LANE), F32)] * 3,
        scratch_shapes=[pltpu.VMEM((NPAIR, NST, LANE), F32), pltpu.VMEM((T, LANE), F32), pltpu.VMEM((LANE, T), F32),
                        pltpu.VMEM((T, INNER), F32), pltpu.VMEM((T, INNER), F32)],
        compiler_params=_params(("arbitrary",)),
    )(dy, xc, dt_exp, a_small, dskip_exp, hin, dt_raw, dt_bias_pad, a_log_pad, expand)


def _ssd_post(cfg, y, z, norm_g):
    W = cfg.INNER // cfg.G

    def fn(y, z, g):
        yz = y * z * _sigmoid(z)
        return jnp.concatenate([yz[:, i * W:(i + 1) * W] * _rs(yz[:, i * W:(i + 1) * W]) for i in range(cfg.G)], axis=1) * g

    return _rowwise("ssd_post", fn, [y, z], [norm_g], [(cfg.INNER, BF16)], [], _pick(cfg.S, 256, 8))[0]


def _ssd_post_bwd(cfg, db, y, z, norm_g):
    W = cfg.INNER // cfg.G

    def fn(db, y, z, g):
        sg = _sigmoid(z)
        yz = y * z * sg
        dn = db * g
        dyz, nh = [], []
        for i in range(cfg.G):
            seg = yz[:, i * W:(i + 1) * W]
            r = _rs(seg)
            nh.append(seg * r)
            dyz.append(_rms_back(nh[-1], r, dn[:, i * W:(i + 1) * W]))
        dyz = jnp.concatenate(dyz, axis=1)
        return dyz * z * sg, dyz * y * sg * (1.0 + z * (1.0 - sg)), _colsum(db * jnp.concatenate(nh, axis=1))

    return _rowwise("ssd_post_bwd", fn, [db, y, z], [norm_g], [(cfg.INNER, F32), (cfg.INNER, F32)], [(1, cfg.INNER)],
                    _pick(cfg.S, 256, 8))


def _rms_pre(cfg, x, g):
    return _rowwise("rms_pre", lambda x, g: x * _rs(x) * g, [x], [g], [(cfg.D, BF16)], [], _pick(cfg.S, 256, 8))[0]


def _local_grads(cfg, x, tgt, W, sp, mla_weights=None, out_weight=None, ffn_weights=None, down_weight=None,
                 ffn_grads_ready=None, early_grads_ready=None, in_grad_ready=None, xn=None, after_in=None):
    S, D, H, INNER = cfg.S, cfg.D, cfg.H, cfg.INNER
    ts = _pick(S, 256, 8)
    tc = _CONV_COLS

    if xn is None:
        xn = _rms_pre(cfg, x, sp["mix_pre_g"])
    u = _matmul("mm_in", xn, W["w_in"], "nt", F32, after=after_in)
    c_q, c_kv, kr, z, xbc, dt_raw = [(u, cfg.window(n)) for n in ("c_q", "c_kv", "kr", "z", "xbc", "dt")]

    if mla_weights is not None:
        sp = dict(sp, q_norm_g=sp["q_norm_g"] + mla_weights.pass_on(u)[0, 0])
    cqn = _rowwise("rms_q", lambda x, g: x * _rs(x) * g, [c_q], [sp["q_norm_g"]], [(cfg.QL, BF16)], [], ts)[0]
    ckvn = _rowwise("rms_kv", lambda x, g: x * _rs(x) * g, [c_kv], [sp["kv_norm_g"]], [(cfg.KVL, BF16)], [], ts)[0]
    if mla_weights is not None:
        W = dict(W, **mla_weights.arrived(ckvn))
    q = _matmul("mm_uq", cqn, W["w_uq"], "nn", F32)
    kv = _matmul("mm_ukv", ckvn, W["w_ukv"], "nn", F32)
    cos2, sin2 = _rope_tables(S)
    Qh, Kh, Vh = _mla_pack(cfg, q, kv, kr, cos2, sin2)
    a_out, lse_t = _attn_fwd(cfg, Qh, Kh, Vh)
    if out_weight is not None:
        sp = dict(sp, ssm_conv_b=sp["ssm_conv_b"] + out_weight.pass_on(a_out)[0, 0])

    pad = lambda v: jnp.pad(v, ((0, 0), (0, LANE - v.shape[1])))
    expand = _expand_matrix(cfg)
    dt_bias_pad, a_log_pad = pad(sp["dt_bias"]), pad(sp["a_log"])
    dskip_exp = jnp.repeat(sp["d_skip"], HP, axis=1)
    xc = _colwise("ssm_act", _ssm_act, [xbc], [sp["ssm_conv_w"], sp["ssm_conv_b"]], [F32], [], tc)[0]
    dt_s, a_s, dt_exp = _ssd_prep(cfg, dt_raw, dt_bias_pad, a_log_pad, expand)
    y_ssd, hin = _ssd_fwd(cfg, xc, dt_exp, a_s, dskip_exp, expand)
    b_out = _ssd_post(cfg, y_ssd, z, sp["ssm_norm_g"])

    ab_out = jnp.concatenate([a_out.astype(BF16), b_out], axis=1)
    if out_weight is not None:
        W = dict(W, **out_weight.arrived(ab_out))
    if ffn_weights is not None:
        sp = dict(sp, mix_post_g=sp["mix_post_g"] + ffn_weights.pass_on(ab_out)[0, 0])
    mix = _matmul("mm_out", ab_out, W["w_out"], "nn", F32)

    def mid(x, mix, g_mp, g_fp):
        x1 = x + mix * _rs(mix) * g_mp
        return x1, x1 * _rs(x1) * g_fp

    x1, h2 = _rowwise("fwd_mid", mid, [x, mix], [sp["mix_post_g"], sp["ffn_pre_g"]], [(D, F32), (D, BF16)], [], ts)
    if ffn_weights is not None:
        W = dict(W, **ffn_weights.arrived(h2))
    gate_pre, up = _matmul_twin("mm_gate_up", h2, W["w_gate"], W["w_up"], "nn", F32)
    if down_weight is not None:
        sp = dict(sp, ffn_conv_b=sp["ffn_conv_b"] + down_weight.pass_on(gate_pre)[0, 0])
    act = _colwise("ffn_act", _ffn_act, [gate_pre, up], [sp["ffn_conv_w"], sp["ffn_conv_b"]], [BF16], [], tc)[0]
    if down_weight is not None:
        W = dict(W, **down_weight.arrived(act))
    f = _matmul("mm_down", act, W["w_down"], "nn", F32)

    def final(x1, f, t, g):
        r = _rs(f)
        fh = f * r
        err = x1 + fh * g - t
        loss = 0.5 * jnp.sum(jnp.mean(err * err, axis=-1, keepdims=True), axis=0, keepdims=True)
        dy = err * (1.0 / D)
        return dy, _rms_back(fh, r, dy * g), _colsum(dy * fh), loss

    dy, df, g_ffn_post, loss = _rowwise("final", final, [x1, f, tgt], [sp["ffn_post_g"]], [(D, F32), (D, BF16)],
                                        [(1, D), (1, LANE)], ts)
    gW = {}
    dact = _matmul("mm_down_dx", df, W["w_down"], "nt", F32)
    gW["w_down"] = _matmul("mm_down_dw", act, df, "tn", BF16)
    dgate, dup, g_ffn_conv_w, g_ffn_conv_b = _colwise(
        "ffn_act_bwd", _ffn_act_back, [dact, gate_pre, up], [sp["ffn_conv_w"], sp["ffn_conv_b"]], [BF16, BF16], [FFN_K, 1], tc)
    gW["w_gate"], gW["w_up"] = _matmul_twin("mm_gate_up_dw", h2, dgate, dup, "tn", BF16)
    if ffn_grads_ready is not None:
        sp = dict(sp, ffn_pre_g=sp["ffn_pre_g"] + ffn_grads_ready({n: gW[n] for n in ("w_down", "w_gate", "w_up")})[0, 0])
    dh2 = _matmul("mm_gu_dx", dgate, W["w_gate"], "nt", F32, dup, W["w_up"], chips=True)

    def mid_back(dy, dh2, x1, mix, g_mp, g_fp):
        r2 = _rs(x1)
        xh = x1 * r2
        dx1 = dy + _rms_back(xh, r2, dh2 * g_fp)
        r1 = _rs(mix)
        mh = mix * r1
        return dx1, _rms_back(mh, r1, dx1 * g_mp), _colsum(dh2 * xh), _colsum(dx1 * mh)

    dx1, dmix, g_ffn_pre, g_mix_post = _rowwise("bwd_mid", mid_back, [dy, dh2, x1, mix], [sp["mix_post_g"], sp["ffn_pre_g"]],
                                                [(D, F32), (D, BF16)], [(1, D), (1, D)], ts)
    dab_out = _matmul("mm_out_dx", dmix, W["w_out"], "nt", F32)
    db_out = (dab_out, (INNER, cfg.MLAW // INNER))
    gW["w_out"] = _matmul("mm_out_dw", ab_out, dmix, "tn", BF16)
    early_token = jnp.zeros((8, LANE), F32)
    if early_grads_ready is not None:
        early_token = early_grads_ready({n: gW[n] for n in ("w_down", "w_gate", "w_up", "w_out")})
        sp = dict(sp, ssm_norm_g=sp["ssm_norm_g"] + early_token[0, 0])

    dy_ssd, dz, g_ssm_norm = _ssd_post_bwd(cfg, db_out, y_ssd, z, sp["ssm_norm_g"])
    dxc, ddt_raw, g_dt_bias, g_a_log, g_d_skip = _ssd_bwd(cfg, dy_ssd, xc, dt_exp, a_s, dskip_exp, hin, dt_raw,
                                                          dt_bias_pad, a_log_pad, expand)
    dxbc, g_ssm_conv_w, g_ssm_conv_b = _colwise("ssm_act_bwd", _ssm_act_back, [dxc, xbc], [sp["ssm_conv_w"], sp["ssm_conv_b"]],
                                                [BF16], [SSM_K, 1], tc)

    delta_t = _attn_delta(cfg, dab_out, a_out, early_token)
    dQ, dK, dV = _attn_bwd(cfg, Qh, Kh, Vh, dab_out, lse_t, delta_t)
    dq, dkv, dkr = _mla_unpack(cfg, dQ, dK, dV, cos2, sin2)
    dcqn = _matmul("mm_uq_dx", dq, W["w_uq"], "nt", F32)
    dckvn = _matmul("mm_ukv_dx", dkv, W["w_ukv"], "nt", F32)
    gW["w_uq"] = _matmul("mm_uq_dw", cqn, dq, "tn", BF16)
    gW["w_ukv"] = _matmul("mm_ukv_dw", ckvn, dkv, "tn", BF16)

    def rms_back(x, dy, g):
        r = _rs(x)
        xh = x * r
        return _rms_back(xh, r, dy * g), _colsum(dy * xh)

    dc_q, g_q_norm = _rowwise("rms_q_bwd", rms_back, [c_q, dcqn], [sp["q_norm_g"]], [(cfg.QL, BF16)], [(1, cfg.QL)], ts)
    dc_kv, g_kv_norm = _rowwise("rms_kv_bwd", rms_back, [c_kv, dckvn], [sp["kv_norm_g"]], [(cfg.KVL, BF16)], [(1, cfg.KVL)], ts)

    du = dict(c_q=dc_q, c_kv=dc_kv, kr=dkr, z=dz.astype(BF16), xbc=dxbc, dt=ddt_raw.astype(BF16))
    du = jnp.concatenate([du[n] for n in sorted(du, key=lambda n: cfg.seg[n][0])], axis=1)
    assert du.shape[1] == cfg.EXT, "the layout of u has gaps"
    gW["w_in"] = _matmul("mm_in_dw", du, xn, "tn", BF16)
    if in_grad_ready is None:
        dxn = _matmul("mm_in_dx", du, W["w_in"], "nn", F32)
    else:
        token = in_grad_ready({n: gW[n] for n in ("w_in", "w_uq", "w_ukv")}, None)
        dxn = _matmul("mm_in_dx", du, W["w_in"], "nn", F32, after=token)
        sp = dict(sp, mix_pre_g=sp["mix_pre_g"] + in_grad_ready(None, dxn)[0, 0])

    def first_back(dx1, dxn, x, g):
        r = _rs(x)
        xh = x * r
        return dx1 + _rms_back(xh, r, dxn * g), _colsum(dxn * xh)

    grad_x, g_mix_pre = _rowwise("bwd_first", first_back, [dx1, dxn, x], [sp["mix_pre_g"]], [(D, F32)], [(1, D)], ts)

    gs = dict(mix_pre_g=g_mix_pre, q_norm_g=g_q_norm, kv_norm_g=g_kv_norm, ssm_conv_w=g_ssm_conv_w, ssm_conv_b=g_ssm_conv_b,
              dt_bias=g_dt_bias[:, :cfg.HS], a_log=g_a_log[:, :cfg.HS], d_skip=g_d_skip[:, :cfg.HS], ssm_norm_g=g_ssm_norm,
              mix_post_g=g_mix_post, ffn_pre_g=g_ffn_pre, ffn_conv_w=g_ffn_conv_w, ffn_conv_b=g_ffn_conv_b,
              ffn_post_g=g_ffn_post)
    return loss, grad_x, gW, gs


def _to_kernel_layout(cfg, name, w):
    if name == "w_in":
        parts, at = [], 0
        for off, width, n_off, n_width in sorted(cfg.seg.values()):
            parts += [jnp.zeros((off - at, w.shape[1]), w.dtype), w[n_off:n_off + n_width],
                      jnp.zeros((width - n_width, w.shape[1]), w.dtype)]
            at = off + width
        parts.append(jnp.zeros((cfg.EXT - at, w.shape[1]), w.dtype))
        return jnp.concatenate([p for p in parts if p.shape[0]], axis=0)
    if name in ("w_uq", "w_ukv"):
        per = NOPE + (ROPE if name == "w_uq" else VH)
        return jnp.concatenate([w[:, h * per:h * per + NOPE] for h in range(cfg.H)]
                               + [w[:, h * per + NOPE:(h + 1) * per] for h in range(cfg.H)], axis=1)
    return w


def _from_kernel_layout(cfg, name, g):
    if name == "w_in":
        return jnp.concatenate([g[off:off + n_width] for off, _, _, n_width in sorted(cfg.seg.values(), key=lambda s: s[2])], axis=0)
    if name in ("w_uq", "w_ukv"):
        second = ROPE if name == "w_uq" else VH
        base = cfg.H * NOPE
        parts = []
        for h in range(cfg.H):
            parts += [g[:, h * NOPE:(h + 1) * NOPE], g[:, base + h * second:base + (h + 1) * second]]
        return jnp.concatenate(parts, axis=1)
    return g


_CHIP_MAJOR = ("w_gate", "w_up")
_RELAYOUT = ("w_uq", "w_ukv")
_LAYOUT_ROWS = 256
_CONV_COLS = 256


def _w_in_layout(cfg, wg):
    _, rs, d = wg.shape
    tc = _pick(d, _LAYOUT_ROWS, LANE)

    def body(w_ref, o_ref):
        o_ref[...] = _to_kernel_layout(cfg, "w_in", jnp.concatenate([w_ref[k] for k in range(N_CHIPS)], axis=0))

    return pl.pallas_call(
        body, name="layout_w_in", grid=(d // tc,),
        in_specs=[pl.BlockSpec((N_CHIPS, rs, tc), lambda j: (0, 0, j))], out_specs=pl.BlockSpec((cfg.EXT, tc), lambda j: (0, j)),
        out_shape=jax.ShapeDtypeStruct((cfg.EXT, d), wg.dtype), compiler_params=_params(("parallel",)),
    )(wg)


def _w_in_grad_to_chips(cfg, g):
    _, d = g.shape
    rs = cfg.IN_COLS // N_CHIPS
    tc = _pick(d, _LAYOUT_ROWS, LANE)

    def body(g_ref, o_ref):
        nat = _from_kernel_layout(cfg, "w_in", g_ref[...])
        for k in range(N_CHIPS):
            o_ref[k] = nat[k * rs:(k + 1) * rs]

    return pl.pallas_call(
        body, name="layout_grad_w_in", grid=(d // tc,),
        in_specs=[pl.BlockSpec((cfg.EXT, tc), lambda j: (0, j))], out_specs=pl.BlockSpec((N_CHIPS, rs, tc), lambda j: (0, 0, j)),
        out_shape=jax.ShapeDtypeStruct((N_CHIPS, rs, d), g.dtype), compiler_params=_params(("parallel",)),
    )(g)


def _gathered_to_kernel(cfg, name, wg):
    if name in _CHIP_MAJOR:
        return wg
    if name == "w_in":
        return _w_in_layout(cfg, wg)
    if name not in _RELAYOUT:
        return wg.reshape(wg.shape[0] * wg.shape[1], wg.shape[2])
    _, rows, cs = wg.shape
    tr = _pick(rows, _LAYOUT_ROWS, 16)

    def body(w_ref, o_ref):
        o_ref[...] = _to_kernel_layout(cfg, name, jnp.concatenate([w_ref[k] for k in range(N_CHIPS)], axis=1))

    wide = jax.eval_shape(lambda w: _to_kernel_layout(cfg, name, w), jax.ShapeDtypeStruct((rows, N_CHIPS * cs), wg.dtype)).shape[1]
    return pl.pallas_call(
        body, name="layout_" + name, grid=(rows // tr,),
        in_specs=[pl.BlockSpec((N_CHIPS, tr, cs), lambda i: (0, i, 0))], out_specs=pl.BlockSpec((tr, wide), lambda i: (i, 0)),
        out_shape=jax.ShapeDtypeStruct((rows, wide), wg.dtype), compiler_params=_params(("parallel",)),
    )(wg)


def _grad_to_chips(cfg, name, g):
    if name in _CHIP_MAJOR:
        return g
    if name == "w_in":
        return _w_in_grad_to_chips(cfg, g)
    if name not in _RELAYOUT:
        return g.reshape(N_CHIPS, g.shape[0] // N_CHIPS, g.shape[1])
    rows, wide = g.shape
    tr = _pick(rows, _LAYOUT_ROWS, 16)
    cs = jax.eval_shape(lambda v: _from_kernel_layout(cfg, name, v), g).shape[1] // N_CHIPS

    def body(g_ref, o_ref):
        nat = _from_kernel_layout(cfg, name, g_ref[...])
        for k in range(N_CHIPS):
            o_ref[k] = nat[:, k * cs:(k + 1) * cs]

    return pl.pallas_call(
        body, name="layout_grad_" + name, grid=(rows // tr,),
        in_specs=[pl.BlockSpec((tr, wide), lambda i: (i, 0))], out_specs=pl.BlockSpec((N_CHIPS, tr, cs), lambda i: (0, i, 0)),
        out_shape=jax.ShapeDtypeStruct((N_CHIPS, rows, cs), g.dtype), compiler_params=_params(("parallel",)),
    )(g)


def _me():
    return lax.axis_index("x"), lax.axis_index("y"), lax.axis_index("c")


def _other_chips(x, y):
    return [(1 - x, y), (x, 1 - y), (1 - x, 1 - y)]


_ANY = pl.BlockSpec(memory_space=pl.ANY)


BLOCK_ELEMS = 1 << 19
BLOCK_ELEMS_FEW = 1 << 20


def _row_block(rows, cols, mult, elems=BLOCK_ELEMS):
    return _pick(rows, max(mult, elems // cols // mult * mult), mult)


def _scalar(v):
    return v.astype(I32).reshape(1)


def _blocks2d(r, c, mult, elems=BLOCK_ELEMS):
    if r % mult == 0:
        tr = _row_block(r, c, mult, elems)
        return (tr, c), r // tr, lambda i: (i, 0)
    tc = _pick(c, max(LANE, elems // r // LANE * LANE), LANE)
    return (r, tc), c // tc, lambda i: (0, i)


def _by_rows(rows):
    return rows % 32 == 0


def _half_shape(rows, cols):
    return (rows // 2, cols) if _by_rows(rows) else (rows, cols // 2)


def _half_blocks(rows, cols, mult, elems=BLOCK_ELEMS):
    hr, hc = _half_shape(rows, cols)
    block, n, part = _blocks2d(hr, hc, mult, elems)
    assert (hr % mult == 0) == _by_rows(rows), (rows, cols, mult)
    full = (lambda h, i: (h * n + i, 0)) if _by_rows(rows) else (lambda h, i: (0, h * n + i))
    return block, n, full, part


def _half(ref, k, half):
    hr, hc = _half_shape(ref.shape[1], ref.shape[2])
    if _by_rows(ref.shape[1]):
        return ref.at[k, pl.ds(pl.multiple_of(half * hr, 16), hr), :]
    return ref.at[k, :, pl.ds(pl.multiple_of(half * hc, LANE), hc)]


def _shard_blocks(w, br, bc):
    if w.shape[0] == 1:
        def write(ref, v):
            ref[...] = v
        return (lambda f: pl.BlockSpec((None, br, bc), lambda *a: (0, *f(*a)))), (lambda ref: ref[...]), write
    assert w.shape[1] == 1 and br == w.shape[0], w.shape

    def write_rows(ref, v):
        ref[:, 0, :] = v
    return (lambda f: pl.BlockSpec((br, 1, bc), lambda *a: (0, 0, f(*a)[1]))), (lambda ref: ref[:, 0, :]), write_rows


def _stage_shard(name, w, chip, after=None):
    rs, cs = w.shape[0] * w.shape[1], w.shape[2]
    (br, bc), n, idx = _blocks2d(rs, cs, 16, BLOCK_ELEMS_FEW)
    spec, get, _ = _shard_blocks(w, br, bc)

    def body(chip_ref, w_ref, *refs):
        refs[-1][...] = get(w_ref).astype(BF16)

    return pl.pallas_call(
        body, name="stage_" + name,
        grid_spec=pltpu.PrefetchScalarGridSpec(
            num_scalar_prefetch=1, grid=(n,),
            in_specs=[spec(lambda i, chip_ref: idx(i))] + ([] if after is None else [_ANY]),
            out_specs=pl.BlockSpec((None, br, bc), lambda i, chip_ref: (chip_ref[0], *idx(i)))),
        out_shape=jax.ShapeDtypeStruct((N_CHIPS, rs, cs), BF16),
        compiler_params=_params(("parallel",)),
    )(_scalar(chip), w, *([] if after is None else [after]))


_HBM = pl.BlockSpec(memory_space=pltpu.HBM)
_SEM = pl.BlockSpec(memory_space=pltpu.SEMAPHORE)
_EFFECT = pltpu.SideEffectType.DATAFLOW_SIDE_EFFECTING


def _split_starts(name, groups, n_copies, copies, after):
    sizes = [len(g) for g in groups]
    bufs = [b for g in groups for b in g]
    n, k = len(bufs), len(groups)
    starts = [sum(sizes[:j]) for j in range(k)]

    def body(*refs):
        sems = refs[n + 1:n + 1 + 2 * k]
        for j, (at, size) in enumerate(zip(starts, sizes)):
            for cp in copies(refs[at:at + size], sems[2 * j], sems[2 * j + 1]):
                cp.start()
        refs[-1][...] = jnp.zeros_like(refs[-1])

    res = pl.pallas_call(
        body, name=name,
        out_shape=(*[pltpu.SemaphoreType.DMA((c,)) for c in n_copies for _ in range(2)],
                   *[pltpu.HBM(b.shape, b.dtype) for b in bufs], jax.ShapeDtypeStruct((8, LANE), F32)),
        in_specs=[_HBM] * n + [_ANY], out_specs=(*[_SEM] * (2 * k), *[_HBM] * n, pl.BlockSpec(memory_space=pltpu.VMEM)),
        input_output_aliases={i: 2 * k + i for i in range(n)},
        compiler_params=pltpu.CompilerParams(has_side_effects=_EFFECT),
    )(*[pltpu.with_memory_space_constraint(b, pltpu.HBM) for b in bufs], after)
    return [(res[2 * j], res[2 * j + 1], list(res[2 * k + at:2 * k + at + size]))
            for j, (at, size) in enumerate(zip(starts, sizes))], res[-1]


def _split_start(name, bufs, n_copies, copies, after):
    [(send_sems, recv_sems, bufs)], token = _split_starts(name, [bufs], [n_copies], copies, after)
    return send_sems, recv_sems, bufs, token


def _split_wait(name, send_sems, recv_sems, bufs, after, copies):
    n = len(bufs)

    def body(*refs):
        for cp in copies(refs[:n], refs[n], refs[n + 1]):
            cp.wait_send()
            cp.wait_recv()

    return list(pl.pallas_call(
        body, name=name, out_shape=[pltpu.HBM(b.shape, b.dtype) for b in bufs],
        in_specs=[_HBM] * n + [_SEM, _SEM, _ANY], out_specs=[_HBM] * n,
        input_output_aliases={i: i for i in range(n)},
        compiler_params=pltpu.CompilerParams(has_side_effects=_EFFECT),
    )(*bufs, send_sems, recv_sems, after))


def _gather_to_chips(bufs, send_sems, recv_sems):
    x, y, c = _me()
    return [pltpu.make_async_remote_copy(src_ref=_half(b, 2 * x + y, c), dst_ref=_half(b, 2 * x + y, c),
                                         send_sem=send_sems.at[3 * w + j], recv_sem=recv_sems.at[3 * w + j],
                                         device_id=(cx, cy, c), device_id_type=MESH_ID)
            for w, b in enumerate(bufs) for j, (cx, cy) in enumerate(_other_chips(x, y))]


def _gather_to_sibling(bufs, send_sems, recv_sems):
    x, y, c = _me()
    return [pltpu.make_async_remote_copy(src_ref=_half(b, 2 * cx + cy, c), dst_ref=_half(b, 2 * cx + cy, c),
                                         send_sem=send_sems.at[3 * w + j], recv_sem=recv_sems.at[3 * w + j],
                                         device_id=(x, y, 1 - c), device_id_type=MESH_ID)
            for w, b in enumerate(bufs) for j, (cx, cy) in enumerate(_other_chips(x, y))]


def _pair_copies(grads, lands, send_sems, recv_sems):
    x, y, c = _me()
    return [pltpu.make_async_remote_copy(src_ref=_half(g_ref, slice(None), 1 - c), dst_ref=l_ref, send_sem=send_sems.at[w],
                                         recv_sem=recv_sems.at[w], device_id=(x, y, 1 - c), device_id_type=MESH_ID)
            for w, (g_ref, l_ref) in enumerate(zip(grads, lands))]


def _pair_exchange_start(name, grads):
    n = len(grads)
    lands = [lax.empty((g.shape[0], *_half_shape(g.shape[1], g.shape[2])), g.dtype) for g in grads]
    send_sems, recv_sems, bufs, token = _split_start(
        "pair_exchange_start_" + name, [*grads, *lands], n, lambda refs, ss, rs: _pair_copies(refs[:n], refs[n:], ss, rs),
        jnp.zeros((8, LANE), F32))
    return (send_sems, recv_sems, bufs), token


def _pair_exchange_wait(name, state, after):
    send_sems, recv_sems, bufs = state
    n = len(bufs) // 2
    bufs = _split_wait("pair_exchange_wait_" + name, send_sems, recv_sems, bufs, after,
                       lambda refs, ss, rs: _pair_copies(refs[:n], refs[n:], ss, rs))
    return bufs[:n], bufs[n:]


def _pair_sum(name, g, theirs, c):
    (br, bc), nb, full, part = _half_blocks(g.shape[1], g.shape[2], 16, 2 * BLOCK_ELEMS_FEW)

    def body(c_ref, a_ref, b_ref, o_ref):
        o_ref[...] = (a_ref[...].astype(F32) + b_ref[...].astype(F32)).astype(o_ref.dtype)

    return pl.pallas_call(
        body, name="pair_sum_" + name,
        grid_spec=pltpu.PrefetchScalarGridSpec(
            num_scalar_prefetch=1, grid=(N_CHIPS, nb),
            in_specs=[pl.BlockSpec((None, br, bc), lambda k, i, c_ref: (k, *full(c_ref[0], i))),
                      pl.BlockSpec((None, br, bc), lambda k, i, c_ref: (k, *part(i)))],
            out_specs=pl.BlockSpec((None, br, bc), lambda k, i, c_ref: (k, *part(i)))),
        out_shape=jax.ShapeDtypeStruct(theirs.shape, BF16),
        compiler_params=_params(("parallel", "parallel")),
    )(_scalar(c), g, theirs)


def _chip_copies(srcs, lands, send_sems, recv_sems):
    x, y, c = _me()
    return [pltpu.make_async_remote_copy(src_ref=s_ref.at[2 * cx + cy], dst_ref=l_ref.at[j], send_sem=send_sems.at[3 * w + j],
                                         recv_sem=recv_sems.at[3 * w + j], device_id=(cx, cy, c), device_id_type=MESH_ID)
            for w, (s_ref, l_ref) in enumerate(zip(srcs, lands)) for j, (cx, cy) in enumerate(_other_chips(x, y))]


def _chip_exchange_start(name, sums):
    n = len(sums)
    lands = [lax.empty((3,) + s.shape[1:], s.dtype) for s in sums]
    send_sems, recv_sems, bufs, token = _split_start(
        "chip_exchange_start_" + name, [*sums, *lands], 3 * n, lambda refs, ss, rs: _chip_copies(refs[:n], refs[n:], ss, rs),
        jnp.zeros((8, LANE), F32))
    return send_sems, recv_sems, bufs[:n], bufs[n:], token


def _chip_exchange_wait(name, send_sems, recv_sems, sums, lands, after):
    n = len(sums)
    bufs = _split_wait("chip_exchange_wait_" + name, send_sems, recv_sems, [*sums, *lands], after,
                       lambda refs, ss, rs: _chip_copies(refs[:n], refs[n:], ss, rs))
    return bufs[:n], bufs[n:]


def _chip_sum(name, sums, theirs, chip):
    _, h, cs = sums.shape
    (br, bc), nb, idx = _blocks2d(h, cs, 16, BLOCK_ELEMS_FEW)

    def body(chip_ref, s_ref, t_ref, o_ref):
        acc = s_ref[...].astype(F32)
        for k in range(3):
            acc = acc + t_ref[k].astype(F32)
        o_ref[...] = acc

    return pl.pallas_call(
        body, name="chip_sum_" + name,
        grid_spec=pltpu.PrefetchScalarGridSpec(
            num_scalar_prefetch=1, grid=(nb,),
            in_specs=[pl.BlockSpec((None, br, bc), lambda i, chip_ref: (chip_ref[0], *idx(i))),
                      pl.BlockSpec((3, br, bc), lambda i, chip_ref: (0, *idx(i)))],
            out_specs=pl.BlockSpec((br, bc), lambda i, chip_ref: idx(i))),
        out_shape=jax.ShapeDtypeStruct((h, cs), F32),
        compiler_params=_params(("parallel",)),
    )(_scalar(chip), sums, theirs)


def _sibling_copies(halves, lands, send_sems, recv_sems):
    x, y, c = _me()
    return [pltpu.make_async_remote_copy(src_ref=h_ref, dst_ref=l_ref, send_sem=send_sems.at[w], recv_sem=recv_sems.at[w],
                                         device_id=(x, y, 1 - c), device_id_type=MESH_ID)
            for w, (h_ref, l_ref) in enumerate(zip(halves, lands))]


def _sibling_exchange_start(name, halves, after):
    n = len(halves)
    lands = [lax.empty(h.shape, h.dtype) for h in halves]
    send_sems, recv_sems, bufs, token = _split_start(
        "sibling_exchange_start_" + name, [*halves, *lands], n, lambda refs, ss, rs: _sibling_copies(refs[:n], refs[n:], ss, rs),
        after)
    return (send_sems, recv_sems, bufs), token


def _sibling_exchange_wait(name, state, after):
    send_sems, recv_sems, bufs = state
    n = len(bufs) // 2
    bufs = _split_wait("sibling_exchange_wait_" + name, send_sems, recv_sems, bufs, after,
                       lambda refs, ss, rs: _sibling_copies(refs[:n], refs[n:], ss, rs))
    return bufs[:n], bufs[n:]


def _sibling_exchange(name, halves):
    n = len(halves)

    def body(*refs):
        ins, outs, send_sems, recv_sems = refs[:n], refs[n:2 * n], refs[2 * n], refs[2 * n + 1]
        x, y, c = _me()
        cps = []
        for w, (h_ref, o_ref) in enumerate(zip(ins, outs)):
            cps.append(pltpu.make_async_remote_copy(src_ref=h_ref, dst_ref=o_ref, send_sem=send_sems.at[w], recv_sem=recv_sems.at[w],
                                                    device_id=(x, y, 1 - c), device_id_type=MESH_ID))
            cps[-1].start()
        for cp in cps:
            cp.wait()

    return pl.pallas_call(
        body, name="sibling_exchange_" + name, in_specs=[_ANY] * n, out_specs=[_ANY] * n,
        out_shape=[jax.ShapeDtypeStruct(h.shape, h.dtype) for h in halves],
        scratch_shapes=[pltpu.SemaphoreType.DMA((n,)), pltpu.SemaphoreType.DMA((n,))],
    )(*halves)


N_DEV = 8


def _peer_copies(bufs, send_sems, recv_sems):
    vec, land = bufs
    x, y, c = _me()
    return [pltpu.make_async_remote_copy(src_ref=vec, dst_ref=land.at[4 * x + 2 * y + c], send_sem=send_sems.at[p - 1],
                                         recv_sem=recv_sems.at[p - 1], device_id=(x ^ (p >> 2), y ^ ((p >> 1) & 1), c ^ (p & 1)),
                                         device_id_type=MESH_ID) for p in range(1, N_DEV)]


def _allreduce_small_start(vec, after):
    land = jnp.zeros((N_DEV,) + vec.shape, F32)
    send_sems, recv_sems, bufs, _ = _split_start("allreduce_small_start", [vec, land], N_DEV - 1, _peer_copies, after)
    return send_sems, recv_sems, bufs


def _allreduce_small_wait(state, chip, core, after):
    send_sems, recv_sems, bufs = state
    vec, land = _split_wait("allreduce_small_wait", send_sems, recv_sems, bufs, after, _peer_copies)

    def body(me_ref, v_ref, l_ref, o_ref):
        acc = None
        for k in range(N_DEV):
            term = jnp.where(me_ref[0] == k, v_ref[...], l_ref[k])
            acc = term if acc is None else acc + term
        o_ref[...] = acc

    return pl.pallas_call(
        body, name="allreduce_small_sum",
        grid_spec=pltpu.PrefetchScalarGridSpec(
            num_scalar_prefetch=1, grid=(1,),
            in_specs=[pl.BlockSpec(vec.shape, lambda i, me_ref: (0, 0)), pl.BlockSpec(land.shape, lambda i, me_ref: (0, 0, 0))],
            out_specs=pl.BlockSpec(vec.shape, lambda i, me_ref: (0, 0))),
        out_shape=jax.ShapeDtypeStruct(vec.shape, F32), compiler_params=_params(("arbitrary",)),
    )(_scalar(2 * chip + core), vec, land)


def _adam_math(w, g, m, v):
    m = ADAM_B1 * m + (1.0 - ADAM_B1) * g
    v = ADAM_B2 * v + (1.0 - ADAM_B2) * (g * g)
    m_hat = m / (1.0 - ADAM_B1 ** ADAM_STEP)
    v_hat = v / (1.0 - ADAM_B2 ** ADAM_STEP)
    return -ADAM_LR * (m_hat / (jnp.sqrt(v_hat) + ADAM_EPS) + ADAM_WD * w), m, v


def _adamw(name, w, g, m, v):
    R, C = w.shape
    tr = _row_block(R, C, 8)

    def body(w_ref, g_ref, m_ref, v_ref, d_ref, nm_ref, nv_ref):
        d_ref[...], nm_ref[...], nv_ref[...] = _adam_math(w_ref[...], g_ref[...], m_ref[...], v_ref[...])

    blk = pl.BlockSpec((tr, C), lambda i: (i, 0))
    return pl.pallas_call(
        body, name=name, grid=(R // tr,), in_specs=[blk] * 4, out_specs=[blk] * 3,
        out_shape=[jax.ShapeDtypeStruct((R, C), F32)] * 3, compiler_params=_params(("parallel",)),
    )(w, g, m, v)


def _adamw_halves(name, w, mine, theirs, m, v, c):
    rs, cs = w.shape[0] * w.shape[1], w.shape[2]
    (br, bc), nb, whole, half = _half_blocks(rs, cs, 8)
    spec, get, put = _shard_blocks(w, br, bc)

    def body(c_ref, w_ref, a_ref, b_ref, m_ref, v_ref, g_ref, d_ref, nm_ref, nv_ref):
        g = jnp.where(pl.program_id(0) == c_ref[0], a_ref[...], b_ref[...])
        put(g_ref, g)
        for ref, val in zip((d_ref, nm_ref, nv_ref), _adam_math(get(w_ref), g, get(m_ref), get(v_ref))):
            put(ref, val)

    full = spec(lambda s, i, c_ref: whole(s, i))
    part = pl.BlockSpec((br, bc), lambda s, i, c_ref: half(i))
    return pl.pallas_call(
        body, name=name,
        grid_spec=pltpu.PrefetchScalarGridSpec(num_scalar_prefetch=1, grid=(2, nb), in_specs=[full, part, part, full, full],
                                               out_specs=[full] * 4),
        out_shape=[jax.ShapeDtypeStruct(w.shape, F32)] * 4, compiler_params=_params(("parallel", "parallel")),
    )(_scalar(c), w, mine, theirs, m, v)


def _pack_small(arrs, lanes=LANE):
    flat = jnp.concatenate([a.reshape(-1) for a in arrs])
    n = -(-flat.shape[0] // (8 * lanes)) * 8 * lanes
    return jnp.pad(flat, (0, n - flat.shape[0])).reshape(8, n // 8)


def _unpack_small(vec, shapes):
    flat, out, off = vec.reshape(-1), [], 0
    for s in shapes:
        out.append(flat[off:off + s[0] * s[1]].reshape(s))
        off += s[0] * s[1]
    return out


class _LateWeights:
    def __init__(self, cfg, tag, names, started, token):
        self.cfg, self.tag, self.names, self.k, self.token = cfg, tag, names, 3 * len(names), token
        self.send, self.recv, self.bufs = started

    @staticmethod
    def start(cfg, groups, staged, after):
        started, token = _split_starts("gather_" + "_".join(groups) + "_chips_start", [[staged[n] for n in names] for names in groups.values()],
                                       [3 * len(names) for names in groups.values()], _gather_to_chips, after)
        return [_LateWeights(cfg, tag, names, s, token) for (tag, names), s in zip(groups.items(), started)]

    def pass_on(self, after):
        bufs = _split_wait(f"gather_{self.tag}_chips_wait", self.send, self.recv, self.bufs, after, _gather_to_chips)
        self.send, self.recv, self.bufs, token = _split_start(f"gather_{self.tag}_sibling_start", bufs, self.k, _gather_to_sibling,
                                                               self.token)
        return token

    def arrived(self, after):
        bufs = _split_wait(f"gather_{self.tag}_sibling_wait", self.send, self.recv, self.bufs, after, _gather_to_sibling)
        return {n: _gathered_to_kernel(self.cfg, n, b) for n, b in zip(self.names, bufs)}


def _step(cfg, a):
    chip = 2 * lax.axis_index("x") + lax.axis_index("y")
    core = lax.axis_index("c")
    big = BIG

    ffn = ("w_gate", "w_up", "w_down")
    first = ("w_in", "w_uq", "w_ukv")
    sp = {n: a[n] for n in SMALL}
    sharded = _pack_small([a[n] for n in SMALL_SHARDED], 2 * LANE)
    slabs = jnp.where(lax.broadcasted_iota(I32, (N_CHIPS,) + sharded.shape, 0) == chip, sharded[None], 0.0)
    staged = {"w_in": _stage_shard("w_in", a["w_in"], chip), "sharded_small": slabs}
    [in_weight] = _LateWeights.start(cfg, {"in": ("w_in", "sharded_small")}, staged, jnp.zeros((8, LANE), F32))
    behind = in_weight.token
    for n in big[1:]:
        behind = staged[n] = _stage_shard(n, a[n], chip, behind)
    in_weight.pass_on(behind)
    xn_early = _rms_pre(cfg, a["x"], sp["mix_pre_g"] + in_weight.token[0, 0])
    W = in_weight.arrived(xn_early)
    allp = W.pop("sharded_small").reshape((N_CHIPS,) + sharded.shape)
    per_chip = [_unpack_small(allp[ch], [a[n].shape for n in SMALL_SHARDED]) for ch in range(N_CHIPS)]
    for k, n in enumerate(SMALL_SHARDED):
        sp[n] = jnp.concatenate([per_chip[ch][k] for ch in range(N_CHIPS)], axis=1)

    mla_weights, out_weight, ffn_weights, down_weight = _LateWeights.start(
        cfg, {"mla": first[1:], "out": ("w_out",), "ffn": ffn[:2], "down": ffn[2:]}, staged, W["w_in"])

    state = {}

    def ffn_grads_ready(grads):
        state["ffn_pairs"], token = _pair_exchange_start("ffn", [_grad_to_chips(cfg, n, grads[n]) for n in ffn_grads])
        return token

    def pair_sums(names, grads, theirs):
        return [_pair_sum(n, g, t, core) for n, g, t in zip(names, grads, theirs)]

    def early_grads_ready(grads):
        out_pairs, token = _pair_exchange_start("out", [_grad_to_chips(cfg, "w_out", grads["w_out"])])
        sums = pair_sums(ffn_grads, *_pair_exchange_wait("ffn", state["ffn_pairs"], token))
        sums += pair_sums(["w_out"], *_pair_exchange_wait("out", out_pairs, sums[-1]))
        state["early"] = _chip_exchange_start("early", sums)
        return state["early"][-1]

    def reduced_halves(tag, names, after):
        send_sems, recv_sems, s_bufs, l_bufs, _ = state[tag]
        s_bufs, l_bufs = _chip_exchange_wait(tag, send_sems, recv_sems, s_bufs, l_bufs, after)
        return [_chip_sum(n, s, t, chip) for n, s, t in zip(names, s_bufs, l_bufs)]

    def in_grad_ready(grads, after):
        if grads is not None:
            state["rest_pairs"], token = _pair_exchange_start("rest", [_grad_to_chips(cfg, n, grads[n]) for n in first])
            return token
        state["rest"] = _chip_exchange_start("rest", pair_sums(first, *_pair_exchange_wait("rest", state["rest_pairs"], after)))
        return state["rest"][-1]

    ffn_grads = ("w_down", "w_gate", "w_up")
    early = ffn_grads + ("w_out",)
    loss, grad_x, gW, gs = _local_grads(cfg, a["x"], a["loss_target"], W, sp, mla_weights, out_weight, ffn_weights, down_weight,
                                        ffn_grads_ready, early_grads_ready, in_grad_ready, xn_early, down_weight.token)
    out = {"grad_x": grad_x}

    def adamw(names, mine, theirs):
        for n, gm, gt in zip(names, mine, theirs):
            out["grad_" + n], out["delta_" + n], out["new_m_" + n], out["new_v_" + n] = _adamw_halves(
                "adamw_" + n, a[n], gm, gt, a["m_" + n], a["v_" + n], core)

    mine = reduced_halves("early", early, grad_x)
    theirs = _sibling_exchange("early", mine[:1])
    later, _ = _sibling_exchange_start("early", mine[1:], theirs[0])
    adamw(early[:1], mine[:1], theirs)
    e_mine, e_theirs = _sibling_exchange_wait("early", later, out["new_v_" + early[0]])
    adamw(early[3:], e_mine[2:], e_theirs[2:])
    mine = reduced_halves("rest", first, out["new_v_" + early[-1]])
    rest, token = _sibling_exchange_start("rest", mine, mine[0])
    small = _allreduce_small_start(_pack_small([gs[n] for n in SMALL] + [loss]), token)
    adamw(early[1:3], e_mine[:2], e_theirs[:2])
    adamw(first, *_sibling_exchange_wait("rest", rest, out["new_v_" + early[2]]))
    shapes = [gs[n].shape for n in SMALL] + [(1, LANE)]
    red = _unpack_small(_allreduce_small_wait(small, chip, core, out["new_v_" + first[-1]]), shapes)
    g_small = dict(zip(SMALL, red[:-1]))
    for n in SMALL_SHARDED:
        cs = a[n].shape[1]
        g_small[n] = lax.dynamic_slice_in_dim(g_small[n], chip * cs, cs, axis=1)
    out["loss"] = red[-1][0, 0]
    sshapes = [a[n].shape for n in SMALL]
    d, nm, nv = _adamw("adamw_small", _pack_small([a[n] for n in SMALL]), _pack_small([g_small[n] for n in SMALL]),
                       _pack_small([a["m_" + n] for n in SMALL]), _pack_small([a["v_" + n] for n in SMALL]))
    for n, dd, mm, vv in zip(SMALL, _unpack_small(d, sshapes), _unpack_small(nm, sshapes), _unpack_small(nv, sshapes)):
        out["grad_" + n], out["delta_" + n], out["new_m_" + n], out["new_v_" + n] = g_small[n], dd, mm, vv
    return out


def kernel(x, mix_pre_g, w_in, q_norm_g, w_uq, kv_norm_g, w_ukv, ssm_conv_w, ssm_conv_b, dt_bias, a_log, d_skip, ssm_norm_g, w_out, mix_post_g, ffn_pre_g, w_gate, w_up, ffn_conv_w, ffn_conv_b, w_down, ffn_post_g, loss_target, m_mix_pre_g, m_w_in, m_q_norm_g, m_w_uq, m_kv_norm_g, m_w_ukv, m_ssm_conv_w, m_ssm_conv_b, m_dt_bias, m_a_log, m_d_skip, m_ssm_norm_g, m_w_out, m_mix_post_g, m_ffn_pre_g, m_w_gate, m_w_up, m_ffn_conv_w, m_ffn_conv_b, m_w_down, m_ffn_post_g, v_mix_pre_g, v_w_in, v_q_norm_g, v_w_uq, v_kv_norm_g, v_w_ukv, v_ssm_conv_w, v_ssm_conv_b, v_dt_bias, v_a_log, v_d_skip, v_ssm_norm_g, v_w_out, v_mix_post_g, v_ffn_pre_g, v_w_gate, v_w_up, v_ffn_conv_w, v_ffn_conv_b, v_w_down, v_ffn_post_g):
    args = dict(locals())
    def given(k, v):
        if k in ("w_in", "m_w_in", "v_w_in"):
            return jnp.transpose(v, (2, 0, 1))
        return v if k.removeprefix("m_").removeprefix("v_") in BIG or v.ndim < 3 else v[0]

    out = _step(_FULL, {k: given(k, v) for k, v in args.items()})
    res = [out["loss"], out["grad_x"][None]]
    for pre in ("grad_", "delta_", "new_m_", "new_v_"):
        for n in WEIGHTS:
            o = out[pre + n]
            res.append(jnp.transpose(o, (1, 2, 0)) if n == "w_in" else o if n in BIG or args[n].ndim < 3 else o[None])
    return tuple(res)
```

```python
import math

import jax
import jax.numpy as jnp
from jax import lax
from jax.experimental import pallas as pl
from jax.experimental.pallas import tpu as pltpu

F32, BF16, I32 = jnp.float32, jnp.bfloat16, jnp.int32
NN = (((1,), (0,)), ((), ()))
NT = (((1,), (1,)), ((), ()))
TN = (((0,), (0,)), ((), ()))
HI = lax.Precision.HIGHEST
MESH_ID = pl.DeviceIdType.MESH

EPS = 1e-6
CHUNK = 64
NOPE, ROPE, VH = 128, 64, 128
ROPE_THETA = 10000.0
HP, NST = 64, 128
SSM_K, FFN_K = 4, 3
LANE = 128
N_CHIPS = 4
VMEM_LIMIT = 52 * 1024 * 1024
MM_TILE, MM_TILE_K = 1408, 2816

ADAM_LR, ADAM_B1, ADAM_B2, ADAM_EPS, ADAM_WD, ADAM_STEP = 0.001, 0.9, 0.999, 1e-08, 0.01, 10


class _Cfg:
    def __init__(self, S, D, QL, KVL, H, HS, G, DFF, T):
        self.S, self.D, self.QL, self.KVL, self.H, self.HS, self.G, self.DFF, self.T = S, D, QL, KVL, H, HS, G, DFF, T
        self.INNER = HS * HP
        self.CONVCH = self.INNER + 2 * G * NST
        self.QW = H * (NOPE + ROPE)
        self.KVW = H * (NOPE + VH)
        self.MLAW = H * VH
        self.MIXW = self.MLAW + self.INNER
        self.IN_COLS = QL + KVL + ROPE + self.INNER + self.CONVCH + HS
        natural, at = {}, 0
        for name, w in (("c_q", QL), ("c_kv", KVL), ("kr", ROPE), ("z", self.INNER), ("xbc", self.CONVCH), ("dt", HS)):
            natural[name] = (at, w)
            at += w
        self.seg, taken = {}, []
        for name in sorted(natural, key=lambda n: -natural[n][1]):
            w = -(-natural[name][1] // LANE) * LANE
            off = next(o for o in range(0, self.IN_COLS * 2, w) if all(o + w <= t or o >= t + tw for t, tw in taken))
            taken.append((off, w))
            self.seg[name] = (off, w) + natural[name]
        self.EXT = max(o + w for o, w in taken)
        self.NPAIR = HS // 2
        self.REP = HS // G

    def window(self, name):
        off, w, _, _ = self.seg[name]
        return w, off // w


_FULL = _Cfg(S=2048, D=2048, QL=768, KVL=512, H=8, HS=16, G=2, DFF=5632, T=256)
BIG = ("w_in", "w_uq", "w_ukv", "w_out", "w_gate", "w_up", "w_down")

SMALL = ("mix_pre_g", "q_norm_g", "kv_norm_g", "ssm_conv_w", "ssm_conv_b", "dt_bias", "a_log", "d_skip", "ssm_norm_g",
         "mix_post_g", "ffn_pre_g", "ffn_conv_w", "ffn_conv_b", "ffn_post_g")
SMALL_SHARDED = ("ssm_conv_w", "ffn_conv_w")
WEIGHTS = ("mix_pre_g", "w_in", "q_norm_g", "w_uq", "kv_norm_g", "w_ukv", "ssm_conv_w", "ssm_conv_b", "dt_bias", "a_log",
           "d_skip", "ssm_norm_g", "w_out", "mix_post_g", "ffn_pre_g", "w_gate", "w_up", "ffn_conv_w", "ffn_conv_b",
           "w_down", "ffn_post_g")


def _pick(n, target, mult):
    best = None
    for d in range(mult, min(n, target) + 1, mult):
        if n % d == 0:
            best = d
    return best if best is not None else n


def _params(sem=None):
    kw = dict(vmem_limit_bytes=VMEM_LIMIT)
    if sem is not None:
        kw["dimension_semantics"] = sem
    return pltpu.CompilerParams(**kw)


def _dot(a, b, dims=NN, precision=None):
    return lax.dot_general(a, b, dims, preferred_element_type=F32, precision=precision)


def _sigmoid(x):
    return 1.0 / (1.0 + jnp.exp(-x))


def _rs(x):
    return lax.rsqrt(jnp.mean(x * x, axis=-1, keepdims=True) + EPS)


def _rms_back(xh, r, dn):
    return r * (dn - xh * jnp.mean(dn * xh, axis=-1, keepdims=True))


def _colsum(v):
    return jnp.sum(v, axis=0, keepdims=True)


def _matmul(name, a, b, mode, out_dtype, a2=None, b2=None, chips=False, after=None):
    cs = None
    if mode == "nn":
        (M, K), N = a.shape, b.shape[-1]
        if chips:
            cs, N = N, N_CHIPS * N
    elif mode == "nt":
        (M, K), N = a.shape, b.shape[-2]
        if chips:
            cs = b.shape[-1]
    else:
        (K, M), N = a.shape, b.shape[1]
        if chips:
            cs = N // N_CHIPS
    tm = _pick(M, MM_TILE, LANE)
    tn = _pick(cs if chips and mode != "nt" else N, MM_TILE, LANE)
    tk = _pick(cs, MM_TILE, LANE) if chips and mode == "nt" else _pick(K, MM_TILE_K, LANE)
    nk = K // tk
    dims = {"nn": NN, "nt": NT, "tn": TN}[mode]
    a_spec = pl.BlockSpec((tk, tm), lambda i, j, k: (k, i)) if mode == "tn" else pl.BlockSpec((tm, tk), lambda i, j, k: (i, k))
    b_spec = pl.BlockSpec((tn, tk), lambda i, j, k: (j, k)) if mode == "nt" else pl.BlockSpec((tk, tn), lambda i, j, k: (k, j))
    o_spec = pl.BlockSpec((tm, tn), lambda i, j, k: (i, j))
    o_shape = (M, N)
    if chips and mode == "nn":
        per = cs // tn
        b_spec = pl.BlockSpec((None, tk, tn), lambda i, j, k: (j // per, k, j % per))
    elif chips and mode == "nt":
        per = cs // tk
        b_spec = pl.BlockSpec((None, tn, tk), lambda i, j, k: (k // per, j, k % per))
    elif chips:
        per = cs // tn
        o_spec = pl.BlockSpec((None, tm, tn), lambda i, j, k: (j // per, i, j % per))
        o_shape = (N_CHIPS, M, cs)
    two = a2 is not None

    def product(refs):
        part = _dot(refs[0][...].astype(BF16), refs[1][...].astype(BF16), dims)
        if two:
            part += _dot(refs[2][...].astype(BF16), refs[3][...].astype(BF16), dims)
        return part

    def body_whole_k(*refs):
        refs[-1][...] = product(refs).astype(refs[-1].dtype)

    def body(*refs):
        o_ref, acc_ref = refs[-2], refs[-1]
        k = pl.program_id(2)

        @pl.when(k == 0)
        def _():
            acc_ref[...] = product(refs)

        @pl.when(k > 0)
        def _():
            acc_ref[...] += product(refs)

        @pl.when(k == nk - 1)
        def _():
            o_ref[...] = acc_ref[...].astype(o_ref.dtype)

    ins = ((a, b, a2, b2) if two else (a, b)) + (() if after is None else (after,))
    return pl.pallas_call(
        body_whole_k if nk == 1 else body, name=name, grid=(M // tm, N // tn, nk),
        in_specs=[a_spec, b_spec] * (2 if two else 1) + ([] if after is None else [pl.BlockSpec(memory_space=pl.ANY)]),
        out_specs=o_spec,
        out_shape=jax.ShapeDtypeStruct(o_shape, out_dtype),
        scratch_shapes=[] if nk == 1 else [pltpu.VMEM((tm, tn), F32)],
        compiler_params=_params(("parallel", "parallel", "arbitrary")),
    )(*ins)


def _matmul_twin(name, a, b1, b2, mode, out_dtype):
    if mode == "nn":
        (M, K), cs = a.shape, b1.shape[-1]
        tm = _pick(M, MM_TILE // 2, LANE)
    else:
        (K, M), cs = a.shape, b1.shape[1] // N_CHIPS
        tm = _pick(M, MM_TILE, LANE)
    tn = _pick(cs, MM_TILE, LANE)
    per = cs // tn
    dims = NN if mode == "nn" else TN

    def body(a_ref, b1_ref, b2_ref, o1_ref, o2_ref):
        lhs = a_ref[...].astype(BF16)
        o1_ref[...] = _dot(lhs, b1_ref[...].astype(BF16), dims).astype(o1_ref.dtype)
        o2_ref[...] = _dot(lhs, b2_ref[...].astype(BF16), dims).astype(o2_ref.dtype)

    if mode == "nn":
        a_spec = pl.BlockSpec((tm, K), lambda i, j: (i, 0))
        b_spec = pl.BlockSpec((None, K, tn), lambda i, j: (j // per, 0, j % per))
        o_spec, o_shape = pl.BlockSpec((tm, tn), lambda i, j: (i, j)), (M, N_CHIPS * cs)
    else:
        a_spec = pl.BlockSpec((K, tm), lambda i, j: (0, i))
        b_spec = pl.BlockSpec((K, tn), lambda i, j: (0, j))
        o_spec, o_shape = pl.BlockSpec((None, tm, tn), lambda i, j: (j // per, i, j % per)), (N_CHIPS, M, cs)
    return pl.pallas_call(
        body, name=name, grid=(M // tm, N_CHIPS * per), in_specs=[a_spec, b_spec, b_spec], out_specs=[o_spec, o_spec],
        out_shape=[jax.ShapeDtypeStruct(o_shape, out_dtype)] * 2, compiler_params=_params(("parallel", "parallel")),
    )(a, b1, b2)


def _window(a):
    return (a[0], *a[1]) if isinstance(a, tuple) else (a, a.shape[1], 0)


def _rowwise(name, fn, rows, mats, outs, reds, ts):
    rows, widths, blocks = zip(*[_window(a) for a in rows])
    S = rows[0].shape[0]
    nr, nm, no = len(rows), len(mats), len(outs)

    def body(*refs):
        res = fn(*[r[...] for r in refs[:nr + nm]])
        res = res if isinstance(res, (tuple, list)) else (res,)
        for r, v in zip(refs[nr + nm:nr + nm + no], res[:no]):
            r[...] = v.astype(r.dtype)
        first = pl.program_id(0) == 0
        for r, v in zip(refs[nr + nm + no:], res[no:]):
            @pl.when(first)
            def _():
                r[...] = jnp.broadcast_to(v, r.shape)

            @pl.when(jnp.logical_not(first))
            def _():
                r[...] += jnp.broadcast_to(v, r.shape)

    in_specs = [pl.BlockSpec((ts, w), lambda i, b=b: (i, b)) for w, b in zip(widths, blocks)]
    in_specs += [pl.BlockSpec(m.shape, lambda i, nd=m.ndim: (0,) * nd) for m in mats]
    out_specs = [pl.BlockSpec((ts, w), lambda i: (i, 0)) for w, _ in outs]
    out_specs += [pl.BlockSpec(s, lambda i: (0, 0)) for s in reds]
    out_shape = [jax.ShapeDtypeStruct((S, w), dt) for w, dt in outs] + [jax.ShapeDtypeStruct(s, F32) for s in reds]
    return pl.pallas_call(
        body, name=name, grid=(S // ts,), in_specs=in_specs, out_specs=out_specs, out_shape=out_shape,
        compiler_params=_params(("arbitrary",) if reds else ("parallel",)),
    )(*rows, *mats)


def _shift_down(v, s):
    if s == 0:
        return v
    rows = lax.broadcasted_iota(I32, v.shape, 0)
    return jnp.where(rows >= s, pltpu.roll(v, s, 0), 0.0)


def _shift_up(v, s):
    if s == 0:
        return v
    n = v.shape[0]
    rows = lax.broadcasted_iota(I32, v.shape, 0)
    return jnp.where(rows < n - s, pltpu.roll(v, n - s, 0), 0.0)


def _conv(x, w, b):
    K = w.shape[0]
    y = jnp.broadcast_to(b, x.shape)
    for k in range(K):
        y = y + w[k:k + 1, :] * _shift_down(x, K - 1 - k)
    return y


def _conv_back(x, w, dc):
    K = w.shape[0]
    dx = jnp.zeros_like(x)
    dw = []
    for k in range(K):
        up = _shift_up(dc, K - 1 - k)
        dx = dx + w[k:k + 1, :] * up
        dw.append(_colsum(up * x))
    return dx, jnp.concatenate(dw, axis=0), _colsum(dc)


def _colwise(name, fn, cols, vecs, outs, pouts, tc):
    cols, widths, blocks = zip(*[_window(a) for a in cols])
    S, C = cols[0].shape[0], widths[0]
    firsts = [b * (C // tc) for b in blocks]
    nc_, nv, no = len(cols), len(vecs), len(outs)

    def body(*refs):
        res = fn(*[r[...] for r in refs[:nc_ + nv]])
        res = res if isinstance(res, (tuple, list)) else (res,)
        for r, v in zip(refs[nc_ + nv:], res):
            r[...] = v.astype(r.dtype)

    in_specs = [pl.BlockSpec((S, tc), lambda j, f=f: (0, f + j)) for f in firsts]
    in_specs += [pl.BlockSpec((v.shape[0], tc), lambda j: (0, j)) for v in vecs]
    out_specs = [pl.BlockSpec((S, tc), lambda j: (0, j)) for _ in outs] + [pl.BlockSpec((k, tc), lambda j: (0, j)) for k in pouts]
    out_shape = [jax.ShapeDtypeStruct((S, C), dt) for dt in outs] + [jax.ShapeDtypeStruct((k, C), F32) for k in pouts]
    return pl.pallas_call(
        body, name=name, grid=(C // tc,), in_specs=in_specs, out_specs=out_specs, out_shape=out_shape,
        compiler_params=_params(("parallel",)),
    )(*cols, *vecs)


_G0, _G1 = math.sqrt(2.0 / math.pi), 0.044715


def _gelu(g):
    th = jnp.tanh(_G0 * (g + _G1 * g * g * g))
    return 0.5 * g * (1.0 + th), th


def _ffn_act(gate_pre, up, w, b):
    act, _ = _gelu(_conv(gate_pre, w, b))
    return act * up


def _ffn_act_back(dact, gate_pre, up, w, b):
    g = _conv(gate_pre, w, b)
    ge, th = _gelu(g)
    dge = 0.5 * (1.0 + th) + 0.5 * g * (1.0 - th * th) * _G0 * (1.0 + 3.0 * _G1 * g * g)
    dup = dact * ge
    dgate_pre, dw, db = _conv_back(gate_pre, w, dact * up * dge)
    return dgate_pre, dup, dw, db


def _ssm_act(xbc, w, b):
    c = _conv(xbc, w, b)
    return c * _sigmoid(c)


def _ssm_act_back(dxc, xbc, w, b):
    c = _conv(xbc, w, b)
    sg = _sigmoid(c)
    return _conv_back(xbc, w, dxc * sg * (1.0 + c * (1.0 - sg)))


def _rope_tables(S):
    inv = 1.0 / (ROPE_THETA ** (jnp.arange(0, ROPE, 2, dtype=F32) / ROPE))
    ang = jnp.arange(S, dtype=F32)[:, None] * inv[None, :]
    cos, sin = jnp.cos(ang), jnp.sin(ang)
    return jnp.tile(cos, (1, 4)), jnp.tile(jnp.concatenate([-sin, sin], axis=1), (1, 2))


def _swap_halves(x):
    lane = lax.broadcasted_iota(I32, x.shape, 1)
    w = x.shape[1]
    return jnp.where((lane % ROPE) < ROPE // 2, pltpu.roll(x, w - ROPE // 2, 1), pltpu.roll(x, ROPE // 2, 1))


def _rot(x, cos2, sin2):
    return x * cos2 + _swap_halves(x) * sin2


def _rot_back(dy, cos2, sin2):
    return dy * cos2 + _swap_halves(dy * sin2)


def _mla_pack(cfg, q, kv, kr, cos2, sin2):
    S, H = cfg.S, cfg.H
    ts = _pick(S, 256, 8)
    kr, _, kr_block = _window(kr)

    def body(q_ref, kv_ref, kr_ref, c_ref, s_ref, Q_ref, K_ref, V_ref):
        c2, s2 = c_ref[...], s_ref[...]
        krr = _rot(kr_ref[...], c2, s2)
        kr_half = (krr.astype(BF16), pltpu.roll(krr, ROPE, 1).astype(BF16))
        for j in range(H // 2):
            qr = _rot(q_ref[:, (H + j) * LANE:(H + j + 1) * LANE], c2, s2).astype(BF16)
            for h in (2 * j, 2 * j + 1):
                Q_ref[h, :, 0:LANE] = q_ref[:, h * LANE:(h + 1) * LANE].astype(BF16)
                Q_ref[h, :, LANE:] = qr
                K_ref[h, :, 0:LANE] = kv_ref[:, h * LANE:(h + 1) * LANE].astype(BF16)
                K_ref[h, :, LANE:] = kr_half[h % 2]
                V_ref[h] = kv_ref[:, (H + h) * LANE:(H + h + 1) * LANE].astype(BF16)

    tab = pl.BlockSpec((ts, LANE), lambda i: (i, 0))
    heads = lambda w: pl.BlockSpec((H, ts, w), lambda i: (0, i, 0))
    return pl.pallas_call(
        body, name="mla_pack", grid=(S // ts,),
        in_specs=[pl.BlockSpec((ts, cfg.QW), lambda i: (i, 0)), pl.BlockSpec((ts, cfg.KVW), lambda i: (i, 0)),
                  pl.BlockSpec((ts, LANE), lambda i: (i, kr_block)), tab, tab],
        out_specs=[heads(2 * LANE), heads(2 * LANE), heads(LANE)],
        out_shape=[jax.ShapeDtypeStruct((H, S, 2 * LANE), BF16), jax.ShapeDtypeStruct((H, S, 2 * LANE), BF16),
                   jax.ShapeDtypeStruct((H, S, LANE), BF16)],
        compiler_params=_params(("parallel",)),
    )(q, kv, kr, cos2, sin2)


def _mla_unpack(cfg, dQ, dK, dV, cos2, sin2):
    S, H = cfg.S, cfg.H
    ts = _pick(S, 256, 8)

    def body(dQ_ref, dK_ref, dV_ref, c_ref, s_ref, dq_ref, dkv_ref, dkr_ref):
        c2, s2 = c_ref[...], s_ref[...]
        lo = lax.broadcasted_iota(I32, (ts, LANE), 1) < ROPE
        tk = jnp.zeros((ts, LANE), F32)
        for h in range(H):
            dq_ref[:, h * LANE:(h + 1) * LANE] = dQ_ref[h, :, 0:LANE].astype(BF16)
            dkv_ref[:, h * LANE:(h + 1) * LANE] = dK_ref[h, :, 0:LANE].astype(BF16)
            dkv_ref[:, (H + h) * LANE:(H + h + 1) * LANE] = dV_ref[h].astype(BF16)
            own = lo if h % 2 == 0 else jnp.logical_not(lo)
            tk = tk + jnp.where(own, dK_ref[h, :, LANE:], 0.0)
        for j in range(H // 2):
            dr = dQ_ref[2 * j, :, LANE:] + dQ_ref[2 * j + 1, :, LANE:]
            dq_ref[:, (H + j) * LANE:(H + j + 1) * LANE] = _rot_back(dr, c2, s2).astype(BF16)
        dkr_rot = jnp.where(lo, tk + pltpu.roll(tk, ROPE, 1), 0.0)
        dkr_ref[...] = _rot_back(dkr_rot, c2, s2).astype(BF16)

    tab = pl.BlockSpec((ts, LANE), lambda i: (i, 0))
    return pl.pallas_call(
        body, name="mla_unpack", grid=(S // ts,),
        in_specs=[pl.BlockSpec((H, ts, 2 * LANE), lambda i: (0, i, 0)), pl.BlockSpec((H, ts, 2 * LANE), lambda i: (0, i, 0)),
                  pl.BlockSpec((H, ts, LANE), lambda i: (0, i, 0)), tab, tab],
        out_specs=[pl.BlockSpec((ts, cfg.QW), lambda i: (i, 0)), pl.BlockSpec((ts, cfg.KVW), lambda i: (i, 0)), tab],
        out_shape=[jax.ShapeDtypeStruct((S, cfg.QW), BF16), jax.ShapeDtypeStruct((S, cfg.KVW), BF16),
                   jax.ShapeDtypeStruct((S, LANE), BF16)],
        compiler_params=_params(("parallel",)),
    )(dQ, dK, dV, cos2, sin2)


_ATT_T = 256
_ATT_HB = 8
_ATT_SCALE = (NOPE + ROPE) ** -0.5


def _diag_mask(transposed=False):
    r = lax.broadcasted_iota(I32, (_ATT_T, _ATT_T), 0) // CHUNK
    c = lax.broadcasted_iota(I32, (_ATT_T, _ATT_T), 1) // CHUNK
    return r <= c if transposed else c <= r


def _row_form(col):
    return jnp.broadcast_to(col, (col.shape[0], LANE)).T[0:8, :]


def _attn_fwd(cfg, Q, K, V):
    S, H, T, HB = cfg.S, cfg.H, _ATT_T, min(cfg.H, _ATT_HB)

    def body(q_ref, k_ref, v_ref, o_ref, lse_t_ref):
        qi = pl.program_id(1)

        def head_step(b, kb, carry, mask):
            m, l, acc = carry
            ks = pl.multiple_of(kb * T, T)
            s = _dot(q_ref[b], k_ref[b, pl.ds(ks, T), :], NT) * _ATT_SCALE
            if mask is not None:
                s = jnp.where(mask, s, -1e30)
            m_new = jnp.maximum(m, jnp.max(s, axis=1, keepdims=True))
            p = jnp.exp(s - m_new)
            alpha = jnp.exp(m - m_new)
            l = alpha * l + jnp.sum(p, axis=1, keepdims=True)
            acc = alpha * acc + _dot(p.astype(BF16), v_ref[b, pl.ds(ks, T), :])
            return m_new, l, acc

        def step(kb, carry, mask=None):
            return tuple(head_step(b, kb, carry[b], mask) for b in range(HB))

        init = (jnp.full((T, 1), -1e30, F32), jnp.zeros((T, 1), F32), jnp.zeros((T, VH), F32))
        done = step(qi, lax.fori_loop(0, qi, step, (init,) * HB), _diag_mask())
        for b, (m, l, acc) in enumerate(done):
            o_ref[:, b * LANE:(b + 1) * LANE] = acc / l
            lse_t_ref[b] = _row_form(m + jnp.log(l))

    return pl.pallas_call(
        body, name="attn_fwd", grid=(H // HB, S // T),
        in_specs=[pl.BlockSpec((HB, T, 2 * LANE), lambda h, i: (h, i, 0)), pl.BlockSpec((HB, S, 2 * LANE), lambda h, i: (h, 0, 0)),
                  pl.BlockSpec((HB, S, LANE), lambda h, i: (h, 0, 0))],
        out_specs=[pl.BlockSpec((T, HB * LANE), lambda h, i: (i, h)), pl.BlockSpec((HB, 8, T), lambda h, i: (h, 0, i))],
        out_shape=[jax.ShapeDtypeStruct((S, H * LANE), F32), jax.ShapeDtypeStruct((H, 8, S), F32)],
        compiler_params=_params(("parallel", "parallel")),
    )(Q, K, V)


def _attn_delta(cfg, do, o, after):
    S, H, T = cfg.S, cfg.H, _ATT_T

    def body(do_ref, o_ref, after_ref, dl_t_ref):
        for h in range(H):
            sl = slice(h * LANE, (h + 1) * LANE)
            dl_t_ref[h] = _row_form(jnp.sum(do_ref[:, sl] * o_ref[:, sl], axis=1, keepdims=True))

    wide = pl.BlockSpec((T, H * LANE), lambda i: (i, 0))
    return pl.pallas_call(
        body, name="attn_delta", grid=(S // T,), in_specs=[wide, wide, _ANY],
        out_specs=pl.BlockSpec((H, 8, T), lambda i: (0, 0, i)), out_shape=jax.ShapeDtypeStruct((H, 8, S), F32),
        compiler_params=_params(("parallel",)),
    )(do, o, after)


_ATT_HB_BWD = 4


def _attn_bwd(cfg, Q, K, V, do, lse_t, delta_t):
    S, H, T, HB = cfg.S, cfg.H, _ATT_T, min(cfg.H, _ATT_HB_BWD)
    nq = S // T

    def body(q_ref, k_ref, v_ref, do_ref, lse_ref, dl_ref, dq_ref, dk_ref, dv_ref):
        kb = pl.program_id(1)

        @pl.when(kb == 0)
        def _():
            dq_ref[...] = jnp.zeros_like(dq_ref)

        def head_step(b, qi, carry, mask):
            dk, dv = carry
            qs = pl.multiple_of(qi * T, T)
            q = q_ref[b, pl.ds(qs, T), :]
            k = k_ref[b]
            dob = do_ref[pl.ds(qs, T), b * LANE:(b + 1) * LANE].astype(BF16)
            s = _dot(k, q, NT) * _ATT_SCALE
            if mask is not None:
                s = jnp.where(mask, s, -1e30)
            p = jnp.exp(s - lse_ref[b, 0:1, pl.ds(qs, T)])
            dv = dv + _dot(p.astype(BF16), dob)
            dp = _dot(v_ref[b], dob, NT)
            ds = (p * (dp - dl_ref[b, 0:1, pl.ds(qs, T)]) * _ATT_SCALE).astype(BF16)
            dk = dk + _dot(ds, q)
            dq_ref[b, pl.ds(qs, T), :] += _dot(ds, k, TN)
            return dk, dv

        def step(qi, carry, mask=None):
            return tuple(head_step(b, qi, carry[b], mask) for b in range(HB))

        zero = (jnp.zeros((T, 2 * LANE), F32), jnp.zeros((T, VH), F32))
        done = lax.fori_loop(kb + 1, nq, step, step(kb, (zero,) * HB, _diag_mask(transposed=True)))
        for b, (dk, dv) in enumerate(done):
            dk_ref[b] = dk
            dv_ref[b] = dv

    row = pl.BlockSpec((HB, 8, S), lambda h, j: (h, 0, 0))
    whole = pl.BlockSpec((HB, S, 2 * LANE), lambda h, j: (h, 0, 0))
    return pl.pallas_call(
        body, name="attn_bwd", grid=(H // HB, S // T),
        in_specs=[whole, pl.BlockSpec((HB, T, 2 * LANE), lambda h, j: (h, j, 0)), pl.BlockSpec((HB, T, LANE), lambda h, j: (h, j, 0)),
                  pl.BlockSpec((S, HB * LANE), lambda h, j: (0, h)), row, row],
        out_specs=[whole, pl.BlockSpec((HB, T, 2 * LANE), lambda h, j: (h, j, 0)), pl.BlockSpec((HB, T, LANE), lambda h, j: (h, j, 0))],
        out_shape=[jax.ShapeDtypeStruct((H, S, 2 * LANE), F32), jax.ShapeDtypeStruct((H, S, 2 * LANE), F32),
                   jax.ShapeDtypeStruct((H, S, LANE), F32)],
        compiler_params=_params(("parallel", "arbitrary")),
    )(Q, K, V, do, lse_t, delta_t)


def _expand_matrix(cfg):
    r = lax.broadcasted_iota(I32, (LANE, cfg.INNER), 0)
    c = lax.broadcasted_iota(I32, (LANE, cfg.INNER), 1)
    return (r == c // HP).astype(F32)


def _softplus(x):
    return jnp.maximum(x, 0.0) + jnp.log(1.0 + jnp.exp(-jnp.abs(x)))


def _ssd_prep(cfg, dt_raw, dt_bias_pad, a_log_pad, expand):
    HS = cfg.HS

    def fn(raw, bias, alog, E):
        heads = lax.broadcasted_iota(I32, raw.shape, 1) < HS
        dt = jnp.where(heads, _softplus(raw + bias), 0.0)
        a = dt * jnp.where(heads[0:1], -jnp.exp(alog), 0.0)
        return dt, a, _dot(dt, E, precision=HI)

    return _rowwise("ssd_prep", fn, [dt_raw], [dt_bias_pad, a_log_pad, expand],
                    [(LANE, F32), (LANE, F32), (cfg.INNER, F32)], [], _pick(cfg.S, 512, 8))


def _tril(T):
    return lax.broadcasted_iota(I32, (T, T), 0) >= lax.broadcasted_iota(I32, (T, T), 1)


def _ssd_fwd(cfg, xc, dt_exp, a_small, dskip_exp, expand):
    S, T, INNER, G, NPAIR = cfg.S, cfg.T, cfg.INNER, cfg.G, cfg.NPAIR
    NC = S // T

    def body(xc_ref, dte_ref, as_ref, dsk_ref, e_ref, y_ref, hin_ref, ht_ref):
        @pl.when(pl.program_id(0) == 0)
        def _():
            ht_ref[...] = jnp.zeros_like(ht_ref)

        tril = _tril(T)
        tri = tril.astype(F32)
        acs_s = _dot(tri, as_ref[...], precision=HI)
        acs_e = _dot(acs_s, e_ref[...], precision=HI)
        acs_t = acs_s.T
        lo = lax.broadcasted_iota(I32, (T, LANE), 1) < HP
        for g in range(G):
            Bb = xc_ref[:, INNER + g * NST:INNER + (g + 1) * NST].astype(BF16)
            Cb = xc_ref[:, INNER + (G + g) * NST:INNER + (G + g + 1) * NST].astype(BF16)
            Gm = _dot(Cb, Bb, NT)
            for j in range(g * NPAIR // G, (g + 1) * NPAIR // G):
                sl = slice(j * LANE, (j + 1) * LANE)
                Xp = xc_ref[:, sl]
                Xdt = Xp * dte_ref[:, sl]
                Xb = Xdt.astype(BF16)
                acs_p = acs_e[:, sl]
                last = acs_p[T - 1:T, :]
                Hin = ht_ref[j]
                hin_ref[0, j] = Hin
                yd = []
                for e in (0, 1):
                    h = 2 * j + e
                    Lm = jnp.exp(jnp.where(tril, acs_s[:, h:h + 1] - acs_t[h:h + 1, :], -1e30))
                    yd.append(_dot((Gm * Lm).astype(BF16), Xb))
                y_off = _dot(Cb, Hin.astype(BF16)) * jnp.exp(acs_p)
                y_ref[:, sl] = jnp.where(lo, yd[0], yd[1]) + y_off + Xp * dsk_ref[:, sl]
                st = _dot(Bb, (Xdt * jnp.exp(last - acs_p)).astype(BF16), TN)
                ht_ref[j] = jnp.exp(last) * Hin + st

    rows = lambda w: pl.BlockSpec((T, w), lambda c: (c, 0))
    return pl.pallas_call(
        body, name="ssd_fwd", grid=(NC,),
        in_specs=[rows(cfg.CONVCH), rows(INNER), rows(LANE), pl.BlockSpec((1, INNER), lambda c: (0, 0)),
                  pl.BlockSpec((LANE, INNER), lambda c: (0, 0))],
        out_specs=[rows(INNER), pl.BlockSpec((1, NPAIR, NST, LANE), lambda c: (c, 0, 0, 0))],
        out_shape=[jax.ShapeDtypeStruct((S, INNER), F32), jax.ShapeDtypeStruct((NC, NPAIR, NST, LANE), F32)],
        scratch_shapes=[pltpu.VMEM((NPAIR, NST, LANE), F32)],
        compiler_params=_params(("arbitrary",)),
    )(xc, dt_exp, a_small, dskip_exp, expand)


def _ssd_bwd(cfg, dy, xc, dt_exp, a_small, dskip_exp, hin, dt_raw, dt_bias_pad, a_log_pad, expand):
    S, T, INNER, G, NPAIR, HS = cfg.S, cfg.T, cfg.INNER, cfg.G, cfg.NPAIR, cfg.HS
    NC = S // T

    def body(dy_ref, xc_ref, dte_ref, as_ref, dsk_ref, hin_ref, raw_ref, bias_ref, alog_ref, e_ref,
             dxc_ref, draw_ref, dbias_ref, dalog_ref, dskip_ref, dht_ref, cols_ref, rows_ref, dacs_ref, ddt_ref):
        first = pl.program_id(0) == 0

        @pl.when(first)
        def _():
            dht_ref[...] = jnp.zeros_like(dht_ref)

        tril = _tril(T)
        tri = tril.astype(F32)
        a_s = as_ref[...]
        acs_s = _dot(tri, a_s, precision=HI)
        acs_e = _dot(acs_s, e_ref[...], precision=HI)
        acs_t = acs_s.T
        lo = lax.broadcasted_iota(I32, (T, LANE), 1) < HP
        last_row = lax.broadcasted_iota(I32, (T, LANE), 0) == T - 1
        cols_ref[...] = jnp.zeros_like(cols_ref)
        rows_ref[...] = jnp.zeros_like(rows_ref)
        dsk_parts = []
        for g in range(G):
            bsl = slice(INNER + g * NST, INNER + (g + 1) * NST)
            csl = slice(INNER + (G + g) * NST, INNER + (G + g + 1) * NST)
            Bb = xc_ref[:, bsl].astype(BF16)
            Cb = xc_ref[:, csl].astype(BF16)
            Gm = _dot(Cb, Bb, NT)
            dG = jnp.zeros((T, T), F32)
            dB = jnp.zeros((T, NST), F32)
            dC = jnp.zeros((T, NST), F32)
            for j in range(g * NPAIR // G, (g + 1) * NPAIR // G):
                sl = slice(j * LANE, (j + 1) * LANE)
                Xp = xc_ref[:, sl]
                dtp = dte_ref[:, sl]
                Xdt = Xp * dtp
                Xb = Xdt.astype(BF16)
                acs_p = acs_e[:, sl]
                last = acs_p[T - 1:T, :]
                e_p, dec, cd = jnp.exp(acs_p), jnp.exp(last - acs_p), jnp.exp(last)
                Hin = hin_ref[0, j]
                Hb = Hin.astype(BF16)
                dHn = dht_ref[j]
                dHb = dHn.astype(BF16)
                dYp = dy_ref[:, sl]
                z = _dot(Cb, Hb)
                dz = (dYp * e_p).astype(BF16)
                dacs_p = dYp * z * e_p
                dC = dC + _dot(dz, Hb, NT)
                dHin = _dot(Cb, dz, TN) + cd * dHn
                dlast = _colsum(dHn * Hin) * cd
                qv = _dot(Bb, dHb)
                dXdt = qv * dec
                ddec = qv * Xdt * dec
                dacs_p = dacs_p - ddec
                dlast = dlast + _colsum(ddec)
                dB = dB + _dot((Xdt * dec).astype(BF16), dHb, NT)
                for e in (0, 1):
                    h = 2 * j + e
                    Lm = jnp.exp(jnp.where(tril, acs_s[:, h:h + 1] - acs_t[h:h + 1, :], -1e30))
                    Mh = Gm * Lm
                    dYe = jnp.where(lo if e == 0 else jnp.logical_not(lo), dYp, 0.0).astype(BF16)
                    dM = _dot(dYe, Xb, NT)
                    dXdt = dXdt + _dot(Mh.astype(BF16), dYe, TN)
                    W = dM * Mh
                    cols_ref[:, h:h + 1] = jnp.sum(W, axis=1, keepdims=True)
                    rows_ref[h:h + 1, :] = _colsum(W)
                    dG = dG + dM * Lm
                dacs_ref[:, sl] = dacs_p + jnp.where(last_row, dlast, 0.0)
                ddt_ref[:, sl] = dXdt * Xp
                dxc_ref[:, sl] = dXdt * dtp + dYp * dsk_ref[:, sl]
                dsk_parts.append(_colsum(dYp * Xp))
                dht_ref[j] = dHin
            dGb = dG.astype(BF16)
            dxc_ref[:, bsl] = dB + _dot(dGb, Cb, TN)
            dxc_ref[:, csl] = dC + _dot(dGb, Bb)
        E = e_ref[...]
        dacs_s = cols_ref[...] - rows_ref[...].T + _dot(dacs_ref[...], E, NT, precision=HI)
        da = _dot(tri, dacs_s, TN, precision=HI)
        heads = lax.broadcasted_iota(I32, (1, LANE), 1) < HS
        A = jnp.where(heads, -jnp.exp(alog_ref[...]), 0.0)
        ddt = _dot(ddt_ref[...], E, NT, precision=HI) + da * A
        draw = jnp.where(heads, ddt * _sigmoid(raw_ref[...] + bias_ref[...]), 0.0)
        draw_ref[...] = draw
        dsk = _dot(jnp.broadcast_to(jnp.concatenate(dsk_parts, axis=1), (8, INNER)), E, NT, precision=HI)[0:1]
        for ref, val in ((dbias_ref, _colsum(draw)), (dalog_ref, _colsum(da * a_s)), (dskip_ref, dsk)):
            @pl.when(first)
            def _():
                ref[...] = val

            @pl.when(jnp.logical_not(first))
            def _():
                ref[...] += val

    dt_raw, _, raw_block = _window(dt_raw)
    rows = lambda w, b=0: pl.BlockSpec((T, w), lambda c: (NC - 1 - c, b))
    vec = lambda w: pl.BlockSpec((1, w), lambda c: (0, 0))
    return pl.pallas_call(
        body, name="ssd_bwd", grid=(NC,),
        in_specs=[rows(INNER), rows(cfg.CONVCH), rows(INNER), rows(LANE), vec(INNER),
                  pl.BlockSpec((1, NPAIR, NST, LANE), lambda c: (NC - 1 - c, 0, 0, 0)), rows(LANE, raw_block), vec(LANE), vec(LANE),
                  pl.BlockSpec((LANE, INNER), lambda c: (0, 0))],
        out_specs=[rows(cfg.CONVCH), rows(LANE), vec(LANE), vec(LANE), vec(LANE)],
        out_shape=[jax.ShapeDtypeStruct((S, cfg.CONVCH), F32), jax.ShapeDtypeStruct((S, LANE), F32)]
        + [jax.ShapeDtypeStruct((1, LANE), F32)] * 3,
        scratch_shapes=[pltpu.VMEM((NPAIR, NST, LANE), F32), pltpu.VMEM((T, LANE), F32), pltpu.VMEM((LANE, T), F32),
                        pltpu.VMEM((T, INNER), F32), pltpu.VMEM((T, INNER), F32)],
        compiler_params=_params(("arbitrary",)),
    )(dy, xc, dt_exp, a_small, dskip_exp, hin, dt_raw, dt_bias_pad, a_log_pad, expand)


def _ssd_post(cfg, y, z, norm_g):
    W = cfg.INNER // cfg.G

    def fn(y, z, g):
        yz = y * z * _sigmoid(z)
        return jnp.concatenate([yz[:, i * W:(i + 1) * W] * _rs(yz[:, i * W:(i + 1) * W]) for i in range(cfg.G)], axis=1) * g

    return _rowwise("ssd_post", fn, [y, z], [norm_g], [(cfg.INNER, BF16)], [], _pick(cfg.S, 256, 8))[0]


def _ssd_post_bwd(cfg, db, y, z, norm_g):
    W = cfg.INNER // cfg.G

    def fn(db, y, z, g):
        sg = _sigmoid(z)
        yz = y * z * sg
        dn = db * g
        dyz, nh = [], []
        for i in range(cfg.G):
            seg = yz[:, i * W:(i + 1) * W]
            r = _rs(seg)
            nh.append(seg * r)
            dyz.append(_rms_back(nh[-1], r, dn[:, i * W:(i + 1) * W]))
        dyz = jnp.concatenate(dyz, axis=1)
        return dyz * z * sg, dyz * y * sg * (1.0 + z * (1.0 - sg)), _colsum(db * jnp.concatenate(nh, axis=1))

    return _rowwise("ssd_post_bwd", fn, [db, y, z], [norm_g], [(cfg.INNER, F32), (cfg.INNER, F32)], [(1, cfg.INNER)],
                    _pick(cfg.S, 256, 8))


def _rms_pre(cfg, x, g):
    return _rowwise("rms_pre", lambda x, g: x * _rs(x) * g, [x], [g], [(cfg.D, BF16)], [], _pick(cfg.S, 256, 8))[0]


def _local_grads(cfg, x, tgt, W, sp, mla_weights=None, out_weight=None, ffn_weights=None, down_weight=None,
                 ffn_grads_ready=None, early_grads_ready=None, in_grad_ready=None, xn=None, after_in=None):
    S, D, H, INNER = cfg.S, cfg.D, cfg.H, cfg.INNER
    ts = _pick(S, 256, 8)
    tc = _CONV_COLS

    if xn is None:
        xn = _rms_pre(cfg, x, sp["mix_pre_g"])
    u = _matmul("mm_in", xn, W["w_in"], "nt", F32, after=after_in)
    c_q, c_kv, kr, z, xbc, dt_raw = [(u, cfg.window(n)) for n in ("c_q", "c_kv", "kr", "z", "xbc", "dt")]

    if mla_weights is not None:
        sp = dict(sp, q_norm_g=sp["q_norm_g"] + mla_weights.pass_on(u)[0, 0])
    cqn = _rowwise("rms_q", lambda x, g: x * _rs(x) * g, [c_q], [sp["q_norm_g"]], [(cfg.QL, BF16)], [], ts)[0]
    ckvn = _rowwise("rms_kv", lambda x, g: x * _rs(x) * g, [c_kv], [sp["kv_norm_g"]], [(cfg.KVL, BF16)], [], ts)[0]
    if mla_weights is not None:
        W = dict(W, **mla_weights.arrived(ckvn))
    q = _matmul("mm_uq", cqn, W["w_uq"], "nn", F32)
    kv = _matmul("mm_ukv", ckvn, W["w_ukv"], "nn", F32)
    cos2, sin2 = _rope_tables(S)
    Qh, Kh, Vh = _mla_pack(cfg, q, kv, kr, cos2, sin2)
    a_out, lse_t = _attn_fwd(cfg, Qh, Kh, Vh)
    if out_weight is not None:
        sp = dict(sp, ssm_conv_b=sp["ssm_conv_b"] + out_weight.pass_on(a_out)[0, 0])

    pad = lambda v: jnp.pad(v, ((0, 0), (0, LANE - v.shape[1])))
    expand = _expand_matrix(cfg)
    dt_bias_pad, a_log_pad = pad(sp["dt_bias"]), pad(sp["a_log"])
    dskip_exp = jnp.repeat(sp["d_skip"], HP, axis=1)
    xc = _colwise("ssm_act", _ssm_act, [xbc], [sp["ssm_conv_w"], sp["ssm_conv_b"]], [F32], [], tc)[0]
    dt_s, a_s, dt_exp = _ssd_prep(cfg, dt_raw, dt_bias_pad, a_log_pad, expand)
    y_ssd, hin = _ssd_fwd(cfg, xc, dt_exp, a_s, dskip_exp, expand)
    b_out = _ssd_post(cfg, y_ssd, z, sp["ssm_norm_g"])

    ab_out = jnp.concatenate([a_out.astype(BF16), b_out], axis=1)
    if out_weight is not None:
        W = dict(W, **out_weight.arrived(ab_out))
    if ffn_weights is not None:
        sp = dict(sp, mix_post_g=sp["mix_post_g"] + ffn_weights.pass_on(ab_out)[0, 0])
    mix = _matmul("mm_out", ab_out, W["w_out"], "nn", F32)

    def mid(x, mix, g_mp, g_fp):
        x1 = x + mix * _rs(mix) * g_mp
        return x1, x1 * _rs(x1) * g_fp

    x1, h2 = _rowwise("fwd_mid", mid, [x, mix], [sp["mix_post_g"], sp["ffn_pre_g"]], [(D, F32), (D, BF16)], [], ts)
    if ffn_weights is not None:
        W = dict(W, **ffn_weights.arrived(h2))
    gate_pre, up = _matmul_twin("mm_gate_up", h2, W["w_gate"], W["w_up"], "nn", F32)
    if down_weight is not None:
        sp = dict(sp, ffn_conv_b=sp["ffn_conv_b"] + down_weight.pass_on(gate_pre)[0, 0])
    act = _colwise("ffn_act", _ffn_act, [gate_pre, up], [sp["ffn_conv_w"], sp["ffn_conv_b"]], [BF16], [], tc)[0]
    if down_weight is not None:
        W = dict(W, **down_weight.arrived(act))
    f = _matmul("mm_down", act, W["w_down"], "nn", F32)

    def final(x1, f, t, g):
        r = _rs(f)
        fh = f * r
        err = x1 + fh * g - t
        loss = 0.5 * jnp.sum(jnp.mean(err * err, axis=-1, keepdims=True), axis=0, keepdims=True)
        dy = err * (1.0 / D)
        return dy, _rms_back(fh, r, dy * g), _colsum(dy * fh), loss

    dy, df, g_ffn_post, loss = _rowwise("final", final, [x1, f, tgt], [sp["ffn_post_g"]], [(D, F32), (D, BF16)],
                                        [(1, D), (1, LANE)], ts)
    gW = {}
    dact = _matmul("mm_down_dx", df, W["w_down"], "nt", F32)
    gW["w_down"] = _matmul("mm_down_dw", act, df, "tn", BF16)
    dgate, dup, g_ffn_conv_w, g_ffn_conv_b = _colwise(
        "ffn_act_bwd", _ffn_act_back, [dact, gate_pre, up], [sp["ffn_conv_w"], sp["ffn_conv_b"]], [BF16, BF16], [FFN_K, 1], tc)
    gW["w_gate"], gW["w_up"] = _matmul_twin("mm_gate_up_dw", h2, dgate, dup, "tn", BF16)
    if ffn_grads_ready is not None:
        sp = dict(sp, ffn_pre_g=sp["ffn_pre_g"] + ffn_grads_ready({n: gW[n] for n in ("w_down", "w_gate", "w_up")})[0, 0])
    dh2 = _matmul("mm_gu_dx", dgate, W["w_gate"], "nt", F32, dup, W["w_up"], chips=True)

    def mid_back(dy, dh2, x1, mix, g_mp, g_fp):
        r2 = _rs(x1)
        xh = x1 * r2
        dx1 = dy + _rms_back(xh, r2, dh2 * g_fp)
        r1 = _rs(mix)
        mh = mix * r1
        return dx1, _rms_back(mh, r1, dx1 * g_mp), _colsum(dh2 * xh), _colsum(dx1 * mh)

    dx1, dmix, g_ffn_pre, g_mix_post = _rowwise("bwd_mid", mid_back, [dy, dh2, x1, mix], [sp["mix_post_g"], sp["ffn_pre_g"]],
                                                [(D, F32), (D, BF16)], [(1, D), (1, D)], ts)
    dab_out = _matmul("mm_out_dx", dmix, W["w_out"], "nt", F32)
    db_out = (dab_out, (INNER, cfg.MLAW // INNER))
    gW["w_out"] = _matmul("mm_out_dw", ab_out, dmix, "tn", BF16)
    early_token = jnp.zeros((8, LANE), F32)
    if early_grads_ready is not None:
        early_token = early_grads_ready({n: gW[n] for n in ("w_down", "w_gate", "w_up", "w_out")})
        sp = dict(sp, ssm_norm_g=sp["ssm_norm_g"] + early_token[0, 0])

    dy_ssd, dz, g_ssm_norm = _ssd_post_bwd(cfg, db_out, y_ssd, z, sp["ssm_norm_g"])
    dxc, ddt_raw, g_dt_bias, g_a_log, g_d_skip = _ssd_bwd(cfg, dy_ssd, xc, dt_exp, a_s, dskip_exp, hin, dt_raw,
                                                          dt_bias_pad, a_log_pad, expand)
    dxbc, g_ssm_conv_w, g_ssm_conv_b = _colwise("ssm_act_bwd", _ssm_act_back, [dxc, xbc], [sp["ssm_conv_w"], sp["ssm_conv_b"]],
                                                [BF16], [SSM_K, 1], tc)

    delta_t = _attn_delta(cfg, dab_out, a_out, early_token)
    dQ, dK, dV = _attn_bwd(cfg, Qh, Kh, Vh, dab_out, lse_t, delta_t)
    dq, dkv, dkr = _mla_unpack(cfg, dQ, dK, dV, cos2, sin2)
    dcqn = _matmul("mm_uq_dx", dq, W["w_uq"], "nt", F32)
    dckvn = _matmul("mm_ukv_dx", dkv, W["w_ukv"], "nt", F32)
    gW["w_uq"] = _matmul("mm_uq_dw", cqn, dq, "tn", BF16)
    gW["w_ukv"] = _matmul("mm_ukv_dw", ckvn, dkv, "tn", BF16)

    def rms_back(x, dy, g):
        r = _rs(x)
        xh = x * r
        return _rms_back(xh, r, dy * g), _colsum(dy * xh)

    dc_q, g_q_norm = _rowwise("rms_q_bwd", rms_back, [c_q, dcqn], [sp["q_norm_g"]], [(cfg.QL, BF16)], [(1, cfg.QL)], ts)
    dc_kv, g_kv_norm = _rowwise("rms_kv_bwd", rms_back, [c_kv, dckvn], [sp["kv_norm_g"]], [(cfg.KVL, BF16)], [(1, cfg.KVL)], ts)

    du = dict(c_q=dc_q, c_kv=dc_kv, kr=dkr, z=dz.astype(BF16), xbc=dxbc, dt=ddt_raw.astype(BF16))
    du = jnp.concatenate([du[n] for n in sorted(du, key=lambda n: cfg.seg[n][0])], axis=1)
    assert du.shape[1] == cfg.EXT, "the layout of u has gaps"
    gW["w_in"] = _matmul("mm_in_dw", du, xn, "tn", BF16)
    if in_grad_ready is None:
        dxn = _matmul("mm_in_dx", du, W["w_in"], "nn", F32)
    else:
        token = in_grad_ready({n: gW[n] for n in ("w_in", "w_uq", "w_ukv")}, None)
        dxn = _matmul("mm_in_dx", du, W["w_in"], "nn", F32, after=token)
        sp = dict(sp, mix_pre_g=sp["mix_pre_g"] + in_grad_ready(None, dxn)[0, 0])

    def first_back(dx1, dxn, x, g):
        r = _rs(x)
        xh = x * r
        return dx1 + _rms_back(xh, r, dxn * g), _colsum(dxn * xh)

    grad_x, g_mix_pre = _rowwise("bwd_first", first_back, [dx1, dxn, x], [sp["mix_pre_g"]], [(D, F32)], [(1, D)], ts)

    gs = dict(mix_pre_g=g_mix_pre, q_norm_g=g_q_norm, kv_norm_g=g_kv_norm, ssm_conv_w=g_ssm_conv_w, ssm_conv_b=g_ssm_conv_b,
              dt_bias=g_dt_bias[:, :cfg.HS], a_log=g_a_log[:, :cfg.HS], d_skip=g_d_skip[:, :cfg.HS], ssm_norm_g=g_ssm_norm,
              mix_post_g=g_mix_post, ffn_pre_g=g_ffn_pre, ffn_conv_w=g_ffn_conv_w, ffn_conv_b=g_ffn_conv_b,
              ffn_post_g=g_ffn_post)
    return loss, grad_x, gW, gs


def _to_kernel_layout(cfg, name, w):
    if name == "w_in":
        parts, at = [], 0
        for off, width, n_off, n_width in sorted(cfg.seg.values()):
            parts += [jnp.zeros((off - at, w.shape[1]), w.dtype), w[n_off:n_off + n_width],
                      jnp.zeros((width - n_width, w.shape[1]), w.dtype)]
            at = off + width
        parts.append(jnp.zeros((cfg.EXT - at, w.shape[1]), w.dtype))
        return jnp.concatenate([p for p in parts if p.shape[0]], axis=0)
    if name in ("w_uq", "w_ukv"):
        per = NOPE + (ROPE if name == "w_uq" else VH)
        return jnp.concatenate([w[:, h * per:h * per + NOPE] for h in range(cfg.H)]
                               + [w[:, h * per + NOPE:(h + 1) * per] for h in range(cfg.H)], axis=1)
    return w


def _from_kernel_layout(cfg, name, g):
    if name == "w_in":
        return jnp.concatenate([g[off:off + n_width] for off, _, _, n_width in sorted(cfg.seg.values(), key=lambda s: s[2])], axis=0)
    if name in ("w_uq", "w_ukv"):
        second = ROPE if name == "w_uq" else VH
        base = cfg.H * NOPE
        parts = []
        for h in range(cfg.H):
            parts += [g[:, h * NOPE:(h + 1) * NOPE], g[:, base + h * second:base + (h + 1) * second]]
        return jnp.concatenate(parts, axis=1)
    return g


_CHIP_MAJOR = ("w_gate", "w_up")
_RELAYOUT = ("w_uq", "w_ukv")
_LAYOUT_ROWS = 256
_CONV_COLS = 256


def _w_in_layout(cfg, wg):
    _, rs, d = wg.shape
    tc = _pick(d, _LAYOUT_ROWS, LANE)

    def body(w_ref, o_ref):
        o_ref[...] = _to_kernel_layout(cfg, "w_in", jnp.concatenate([w_ref[k] for k in range(N_CHIPS)], axis=0))

    return pl.pallas_call(
        body, name="layout_w_in", grid=(d // tc,),
        in_specs=[pl.BlockSpec((N_CHIPS, rs, tc), lambda j: (0, 0, j))], out_specs=pl.BlockSpec((cfg.EXT, tc), lambda j: (0, j)),
        out_shape=jax.ShapeDtypeStruct((cfg.EXT, d), wg.dtype), compiler_params=_params(("parallel",)),
    )(wg)


def _w_in_grad_to_chips(cfg, g):
    _, d = g.shape
    rs = cfg.IN_COLS // N_CHIPS
    tc = _pick(d, _LAYOUT_ROWS, LANE)

    def body(g_ref, o_ref):
        nat = _from_kernel_layout(cfg, "w_in", g_ref[...])
        for k in range(N_CHIPS):
            o_ref[k] = nat[k * rs:(k + 1) * rs]

    return pl.pallas_call(
        body, name="layout_grad_w_in", grid=(d // tc,),
        in_specs=[pl.BlockSpec((cfg.EXT, tc), lambda j: (0, j))], out_specs=pl.BlockSpec((N_CHIPS, rs, tc), lambda j: (0, 0, j)),
        out_shape=jax.ShapeDtypeStruct((N_CHIPS, rs, d), g.dtype), compiler_params=_params(("parallel",)),
    )(g)


def _gathered_to_kernel(cfg, name, wg):
    if name in _CHIP_MAJOR:
        return wg
    if name == "w_in":
        return _w_in_layout(cfg, wg)
    if name not in _RELAYOUT:
        return wg.reshape(wg.shape[0] * wg.shape[1], wg.shape[2])
    _, rows, cs = wg.shape
    tr = _pick(rows, _LAYOUT_ROWS, 16)

    def body(w_ref, o_ref):
        o_ref[...] = _to_kernel_layout(cfg, name, jnp.concatenate([w_ref[k] for k in range(N_CHIPS)], axis=1))

    wide = jax.eval_shape(lambda w: _to_kernel_layout(cfg, name, w), jax.ShapeDtypeStruct((rows, N_CHIPS * cs), wg.dtype)).shape[1]
    return pl.pallas_call(
        body, name="layout_" + name, grid=(rows // tr,),
        in_specs=[pl.BlockSpec((N_CHIPS, tr, cs), lambda i: (0, i, 0))], out_specs=pl.BlockSpec((tr, wide), lambda i: (i, 0)),
        out_shape=jax.ShapeDtypeStruct((rows, wide), wg.dtype), compiler_params=_params(("parallel",)),
    )(wg)


def _grad_to_chips(cfg, name, g):
    if name in _CHIP_MAJOR:
        return g
    if name == "w_in":
        return _w_in_grad_to_chips(cfg, g)
    if name not in _RELAYOUT:
        return g.reshape(N_CHIPS, g.shape[0] // N_CHIPS, g.shape[1])
    rows, wide = g.shape
    tr = _pick(rows, _LAYOUT_ROWS, 16)
    cs = jax.eval_shape(lambda v: _from_kernel_layout(cfg, name, v), g).shape[1] // N_CHIPS

    def body(g_ref, o_ref):
        nat = _from_kernel_layout(cfg, name, g_ref[...])
        for k in range(N_CHIPS):
            o_ref[k] = nat[:, k * cs:(k + 1) * cs]

    return pl.pallas_call(
        body, name="layout_grad_" + name, grid=(rows // tr,),
        in_specs=[pl.BlockSpec((tr, wide), lambda i: (i, 0))], out_specs=pl.BlockSpec((N_CHIPS, tr, cs), lambda i: (0, i, 0)),
        out_shape=jax.ShapeDtypeStruct((N_CHIPS, rows, cs), g.dtype), compiler_params=_params(("parallel",)),
    )(g)


def _me():
    return lax.axis_index("x"), lax.axis_index("y"), lax.axis_index("c")


def _other_chips(x, y):
    return [(1 - x, y), (x, 1 - y), (1 - x, 1 - y)]


_ANY = pl.BlockSpec(memory_space=pl.ANY)


BLOCK_ELEMS = 1 << 19
BLOCK_ELEMS_FEW = 1 << 20


def _row_block(rows, cols, mult, elems=BLOCK_ELEMS):
    return _pick(rows, max(mult, elems // cols // mult * mult), mult)


def _scalar(v):
    return v.astype(I32).reshape(1)


def _blocks2d(r, c, mult, elems=BLOCK_ELEMS):
    if r % mult == 0:
        tr = _row_block(r, c, mult, elems)
        return (tr, c), r // tr, lambda i: (i, 0)
    tc = _pick(c, max(LANE, elems // r // LANE * LANE), LANE)
    return (r, tc), c // tc, lambda i: (0, i)


def _by_rows(rows):
    return rows % 32 == 0


def _half_shape(rows, cols):
    return (rows // 2, cols) if _by_rows(rows) else (rows, cols // 2)


def _half_blocks(rows, cols, mult, elems=BLOCK_ELEMS):
    hr, hc = _half_shape(rows, cols)
    block, n, part = _blocks2d(hr, hc, mult, elems)
    assert (hr % mult == 0) == _by_rows(rows), (rows, cols, mult)
    full = (lambda h, i: (h * n + i, 0)) if _by_rows(rows) else (lambda h, i: (0, h * n + i))
    return block, n, full, part


def _half(ref, k, half):
    hr, hc = _half_shape(ref.shape[1], ref.shape[2])
    if _by_rows(ref.shape[1]):
        return ref.at[k, pl.ds(pl.multiple_of(half * hr, 16), hr), :]
    return ref.at[k, :, pl.ds(pl.multiple_of(half * hc, LANE), hc)]


def _shard_blocks(w, br, bc):
    if w.shape[0] == 1:
        def write(ref, v):
            ref[...] = v
        return (lambda f: pl.BlockSpec((None, br, bc), lambda *a: (0, *f(*a)))), (lambda ref: ref[...]), write
    assert w.shape[1] == 1 and br == w.shape[0], w.shape

    def write_rows(ref, v):
        ref[:, 0, :] = v
    return (lambda f: pl.BlockSpec((br, 1, bc), lambda *a: (0, 0, f(*a)[1]))), (lambda ref: ref[:, 0, :]), write_rows


def _stage_shard(name, w, chip, after=None):
    rs, cs = w.shape[0] * w.shape[1], w.shape[2]
    (br, bc), n, idx = _blocks2d(rs, cs, 16, BLOCK_ELEMS_FEW)
    spec, get, _ = _shard_blocks(w, br, bc)

    def body(chip_ref, w_ref, *refs):
        refs[-1][...] = get(w_ref).astype(BF16)

    return pl.pallas_call(
        body, name="stage_" + name,
        grid_spec=pltpu.PrefetchScalarGridSpec(
            num_scalar_prefetch=1, grid=(n,),
            in_specs=[spec(lambda i, chip_ref: idx(i))] + ([] if after is None else [_ANY]),
            out_specs=pl.BlockSpec((None, br, bc), lambda i, chip_ref: (chip_ref[0], *idx(i)))),
        out_shape=jax.ShapeDtypeStruct((N_CHIPS, rs, cs), BF16),
        compiler_params=_params(("parallel",)),
    )(_scalar(chip), w, *([] if after is None else [after]))


_HBM = pl.BlockSpec(memory_space=pltpu.HBM)
_SEM = pl.BlockSpec(memory_space=pltpu.SEMAPHORE)
_EFFECT = pltpu.SideEffectType.DATAFLOW_SIDE_EFFECTING


def _split_starts(name, groups, n_copies, copies, after):
    sizes = [len(g) for g in groups]
    bufs = [b for g in groups for b in g]
    n, k = len(bufs), len(groups)
    starts = [sum(sizes[:j]) for j in range(k)]

    def body(*refs):
        sems = refs[n + 1:n + 1 + 2 * k]
        for j, (at, size) in enumerate(zip(starts, sizes)):
            for cp in copies(refs[at:at + size], sems[2 * j], sems[2 * j + 1]):
                cp.start()
        refs[-1][...] = jnp.zeros_like(refs[-1])

    res = pl.pallas_call(
        body, name=name,
        out_shape=(*[pltpu.SemaphoreType.DMA((c,)) for c in n_copies for _ in range(2)],
                   *[pltpu.HBM(b.shape, b.dtype) for b in bufs], jax.ShapeDtypeStruct((8, LANE), F32)),
        in_specs=[_HBM] * n + [_ANY], out_specs=(*[_SEM] * (2 * k), *[_HBM] * n, pl.BlockSpec(memory_space=pltpu.VMEM)),
        input_output_aliases={i: 2 * k + i for i in range(n)},
        compiler_params=pltpu.CompilerParams(has_side_effects=_EFFECT),
    )(*[pltpu.with_memory_space_constraint(b, pltpu.HBM) for b in bufs], after)
    return [(res[2 * j], res[2 * j + 1], list(res[2 * k + at:2 * k + at + size]))
            for j, (at, size) in enumerate(zip(starts, sizes))], res[-1]


def _split_start(name, bufs, n_copies, copies, after):
    [(send_sems, recv_sems, bufs)], token = _split_starts(name, [bufs], [n_copies], copies, after)
    return send_sems, recv_sems, bufs, token


def _split_wait(name, send_sems, recv_sems, bufs, after, copies):
    n = len(bufs)

    def body(*refs):
        for cp in copies(refs[:n], refs[n], refs[n + 1]):
            cp.wait_send()
            cp.wait_recv()

    return list(pl.pallas_call(
        body, name=name, out_shape=[pltpu.HBM(b.shape, b.dtype) for b in bufs],
        in_specs=[_HBM] * n + [_SEM, _SEM, _ANY], out_specs=[_HBM] * n,
        input_output_aliases={i: i for i in range(n)},
        compiler_params=pltpu.CompilerParams(has_side_effects=_EFFECT),
    )(*bufs, send_sems, recv_sems, after))


def _gather_to_chips(bufs, send_sems, recv_sems):
    x, y, c = _me()
    return [pltpu.make_async_remote_copy(src_ref=_half(b, 2 * x + y, c), dst_ref=_half(b, 2 * x + y, c),
                                         send_sem=send_sems.at[3 * w + j], recv_sem=recv_sems.at[3 * w + j],
                                         device_id=(cx, cy, c), device_id_type=MESH_ID)
            for w, b in enumerate(bufs) for j, (cx, cy) in enumerate(_other_chips(x, y))]


def _gather_to_sibling(bufs, send_sems, recv_sems):
    x, y, c = _me()
    return [pltpu.make_async_remote_copy(src_ref=_half(b, 2 * cx + cy, c), dst_ref=_half(b, 2 * cx + cy, c),
                                         send_sem=send_sems.at[3 * w + j], recv_sem=recv_sems.at[3 * w + j],
                                         device_id=(x, y, 1 - c), device_id_type=MESH_ID)
            for w, b in enumerate(bufs) for j, (cx, cy) in enumerate(_other_chips(x, y))]


def _pair_copies(grads, lands, send_sems, recv_sems):
    x, y, c = _me()
    return [pltpu.make_async_remote_copy(src_ref=_half(g_ref, slice(None), 1 - c), dst_ref=l_ref, send_sem=send_sems.at[w],
                                         recv_sem=recv_sems.at[w], device_id=(x, y, 1 - c), device_id_type=MESH_ID)
            for w, (g_ref, l_ref) in enumerate(zip(grads, lands))]


def _pair_exchange_start(name, grads):
    n = len(grads)
    lands = [lax.empty((g.shape[0], *_half_shape(g.shape[1], g.shape[2])), g.dtype) for g in grads]
    send_sems, recv_sems, bufs, token = _split_start(
        "pair_exchange_start_" + name, [*grads, *lands], n, lambda refs, ss, rs: _pair_copies(refs[:n], refs[n:], ss, rs),
        jnp.zeros((8, LANE), F32))
    return (send_sems, recv_sems, bufs), token


def _pair_exchange_wait(name, state, after):
    send_sems, recv_sems, bufs = state
    n = len(bufs) // 2
    bufs = _split_wait("pair_exchange_wait_" + name, send_sems, recv_sems, bufs, after,
                       lambda refs, ss, rs: _pair_copies(refs[:n], refs[n:], ss, rs))
    return bufs[:n], bufs[n:]


def _pair_sum(name, g, theirs, c):
    (br, bc), nb, full, part = _half_blocks(g.shape[1], g.shape[2], 16, 2 * BLOCK_ELEMS_FEW)

    def body(c_ref, a_ref, b_ref, o_ref):
        o_ref[...] = (a_ref[...].astype(F32) + b_ref[...].astype(F32)).astype(o_ref.dtype)

    return pl.pallas_call(
        body, name="pair_sum_" + name,
        grid_spec=pltpu.PrefetchScalarGridSpec(
            num_scalar_prefetch=1, grid=(N_CHIPS, nb),
            in_specs=[pl.BlockSpec((None, br, bc), lambda k, i, c_ref: (k, *full(c_ref[0], i))),
                      pl.BlockSpec((None, br, bc), lambda k, i, c_ref: (k, *part(i)))],
            out_specs=pl.BlockSpec((None, br, bc), lambda k, i, c_ref: (k, *part(i)))),
        out_shape=jax.ShapeDtypeStruct(theirs.shape, BF16),
        compiler_params=_params(("parallel", "parallel")),
    )(_scalar(c), g, theirs)


def _chip_copies(srcs, lands, send_sems, recv_sems):
    x, y, c = _me()
    return [pltpu.make_async_remote_copy(src_ref=s_ref.at[2 * cx + cy], dst_ref=l_ref.at[j], send_sem=send_sems.at[3 * w + j],
                                         recv_sem=recv_sems.at[3 * w + j], device_id=(cx, cy, c), device_id_type=MESH_ID)
            for w, (s_ref, l_ref) in enumerate(zip(srcs, lands)) for j, (cx, cy) in enumerate(_other_chips(x, y))]


def _chip_exchange_start(name, sums):
    n = len(sums)
    lands = [lax.empty((3,) + s.shape[1:], s.dtype) for s in sums]
    send_sems, recv_sems, bufs, token = _split_start(
        "chip_exchange_start_" + name, [*sums, *lands], 3 * n, lambda refs, ss, rs: _chip_copies(refs[:n], refs[n:], ss, rs),
        jnp.zeros((8, LANE), F32))
    return send_sems, recv_sems, bufs[:n], bufs[n:], token


def _chip_exchange_wait(name, send_sems, recv_sems, sums, lands, after):
    n = len(sums)
    bufs = _split_wait("chip_exchange_wait_" + name, send_sems, recv_sems, [*sums, *lands], after,
                       lambda refs, ss, rs: _chip_copies(refs[:n], refs[n:], ss, rs))
    return bufs[:n], bufs[n:]


def _chip_sum(name, sums, theirs, chip):
    _, h, cs = sums.shape
    (br, bc), nb, idx = _blocks2d(h, cs, 16, BLOCK_ELEMS_FEW)

    def body(chip_ref, s_ref, t_ref, o_ref):
        acc = s_ref[...].astype(F32)
        for k in range(3):
            acc = acc + t_ref[k].astype(F32)
        o_ref[...] = acc

    return pl.pallas_call(
        body, name="chip_sum_" + name,
        grid_spec=pltpu.PrefetchScalarGridSpec(
            num_scalar_prefetch=1, grid=(nb,),
            in_specs=[pl.BlockSpec((None, br, bc), lambda i, chip_ref: (chip_ref[0], *idx(i))),
                      pl.BlockSpec((3, br, bc), lambda i, chip_ref: (0, *idx(i)))],
            out_specs=pl.BlockSpec((br, bc), lambda i, chip_ref: idx(i))),
        out_shape=jax.ShapeDtypeStruct((h, cs), F32),
        compiler_params=_params(("parallel",)),
    )(_scalar(chip), sums, theirs)


def _sibling_copies(halves, lands, send_sems, recv_sems):
    x, y, c = _me()
    return [pltpu.make_async_remote_copy(src_ref=h_ref, dst_ref=l_ref, send_sem=send_sems.at[w], recv_sem=recv_sems.at[w],
                                         device_id=(x, y, 1 - c), device_id_type=MESH_ID)
            for w, (h_ref, l_ref) in enumerate(zip(halves, lands))]


def _sibling_exchange_start(name, halves, after):
    n = len(halves)
    lands = [lax.empty(h.shape, h.dtype) for h in halves]
    send_sems, recv_sems, bufs, token = _split_start(
        "sibling_exchange_start_" + name, [*halves, *lands], n, lambda refs, ss, rs: _sibling_copies(refs[:n], refs[n:], ss, rs),
        after)
    return (send_sems, recv_sems, bufs), token


def _sibling_exchange_wait(name, state, after):
    send_sems, recv_sems, bufs = state
    n = len(bufs) // 2
    bufs = _split_wait("sibling_exchange_wait_" + name, send_sems, recv_sems, bufs, after,
                       lambda refs, ss, rs: _sibling_copies(refs[:n], refs[n:], ss, rs))
    return bufs[:n], bufs[n:]


def _sibling_exchange(name, halves):
    n = len(halves)

    def body(*refs):
        ins, outs, send_sems, recv_sems = refs[:n], refs[n:2 * n], refs[2 * n], refs[2 * n + 1]
        x, y, c = _me()
        cps = []
        for w, (h_ref, o_ref) in enumerate(zip(ins, outs)):
            cps.append(pltpu.make_async_remote_copy(src_ref=h_ref, dst_ref=o_ref, send_sem=send_sems.at[w], recv_sem=recv_sems.at[w],
                                                    device_id=(x, y, 1 - c), device_id_type=MESH_ID))
            cps[-1].start()
        for cp in cps:
            cp.wait()

    return pl.pallas_call(
        body, name="sibling_exchange_" + name, in_specs=[_ANY] * n, out_specs=[_ANY] * n,
        out_shape=[jax.ShapeDtypeStruct(h.shape, h.dtype) for h in halves],
        scratch_shapes=[pltpu.SemaphoreType.DMA((n,)), pltpu.SemaphoreType.DMA((n,))],
    )(*halves)


N_DEV = 8


def _peer_copies(bufs, send_sems, recv_sems):
    vec, land = bufs
    x, y, c = _me()
    return [pltpu.make_async_remote_copy(src_ref=vec, dst_ref=land.at[4 * x + 2 * y + c], send_sem=send_sems.at[p - 1],
                                         recv_sem=recv_sems.at[p - 1], device_id=(x ^ (p >> 2), y ^ ((p >> 1) & 1), c ^ (p & 1)),
                                         device_id_type=MESH_ID) for p in range(1, N_DEV)]


def _allreduce_small_start(vec, after):
    land = jnp.zeros((N_DEV,) + vec.shape, F32)
    send_sems, recv_sems, bufs, _ = _split_start("allreduce_small_start", [vec, land], N_DEV - 1, _peer_copies, after)
    return send_sems, recv_sems, bufs


def _allreduce_small_wait(state, chip, core, after):
    send_sems, recv_sems, bufs = state
    vec, land = _split_wait("allreduce_small_wait", send_sems, recv_sems, bufs, after, _peer_copies)

    def body(me_ref, v_ref, l_ref, o_ref):
        acc = None
        for k in range(N_DEV):
            term = jnp.where(me_ref[0] == k, v_ref[...], l_ref[k])
            acc = term if acc is None else acc + term
        o_ref[...] = acc

    return pl.pallas_call(
        body, name="allreduce_small_sum",
        grid_spec=pltpu.PrefetchScalarGridSpec(
            num_scalar_prefetch=1, grid=(1,),
            in_specs=[pl.BlockSpec(vec.shape, lambda i, me_ref: (0, 0)), pl.BlockSpec(land.shape, lambda i, me_ref: (0, 0, 0))],
            out_specs=pl.BlockSpec(vec.shape, lambda i, me_ref: (0, 0))),
        out_shape=jax.ShapeDtypeStruct(vec.shape, F32), compiler_params=_params(("arbitrary",)),
    )(_scalar(2 * chip + core), vec, land)


def _adam_math(w, g, m, v):
    m = ADAM_B1 * m + (1.0 - ADAM_B1) * g
    v = ADAM_B2 * v + (1.0 - ADAM_B2) * (g * g)
    m_hat = m / (1.0 - ADAM_B1 ** ADAM_STEP)
    v_hat = v / (1.0 - ADAM_B2 ** ADAM_STEP)
    return -ADAM_LR * (m_hat / (jnp.sqrt(v_hat) + ADAM_EPS) + ADAM_WD * w), m, v


def _adamw(name, w, g, m, v):
    R, C = w.shape
    tr = _row_block(R, C, 8)

    def body(w_ref, g_ref, m_ref, v_ref, d_ref, nm_ref, nv_ref):
        d_ref[...], nm_ref[...], nv_ref[...] = _adam_math(w_ref[...], g_ref[...], m_ref[...], v_ref[...])

    blk = pl.BlockSpec((tr, C), lambda i: (i, 0))
    return pl.pallas_call(
        body, name=name, grid=(R // tr,), in_specs=[blk] * 4, out_specs=[blk] * 3,
        out_shape=[jax.ShapeDtypeStruct((R, C), F32)] * 3, compiler_params=_params(("parallel",)),
    )(w, g, m, v)


def _adamw_halves(name, w, mine, theirs, m, v, c):
    rs, cs = w.shape[0] * w.shape[1], w.shape[2]
    (br, bc), nb, whole, half = _half_blocks(rs, cs, 8)
    spec, get, put = _shard_blocks(w, br, bc)

    def body(c_ref, w_ref, a_ref, b_ref, m_ref, v_ref, g_ref, d_ref, nm_ref, nv_ref):
        g = jnp.where(pl.program_id(0) == c_ref[0], a_ref[...], b_ref[...])
        put(g_ref, g)
        for ref, val in zip((d_ref, nm_ref, nv_ref), _adam_math(get(w_ref), g, get(m_ref), get(v_ref))):
            put(ref, val)

    full = spec(lambda s, i, c_ref: whole(s, i))
    part = pl.BlockSpec((br, bc), lambda s, i, c_ref: half(i))
    return pl.pallas_call(
        body, name=name,
        grid_spec=pltpu.PrefetchScalarGridSpec(num_scalar_prefetch=1, grid=(2, nb), in_specs=[full, part, part, full, full],
                                               out_specs=[full] * 4),
        out_shape=[jax.ShapeDtypeStruct(w.shape, F32)] * 4, compiler_params=_params(("parallel", "parallel")),
    )(_scalar(c), w, mine, theirs, m, v)


def _pack_small(arrs, lanes=LANE):
    flat = jnp.concatenate([a.reshape(-1) for a in arrs])
    n = -(-flat.shape[0] // (8 * lanes)) * 8 * lanes
    return jnp.pad(flat, (0, n - flat.shape[0])).reshape(8, n // 8)


def _unpack_small(vec, shapes):
    flat, out, off = vec.reshape(-1), [], 0
    for s in shapes:
        out.append(flat[off:off + s[0] * s[1]].reshape(s))
        off += s[0] * s[1]
    return out


class _LateWeights:
    def __init__(self, cfg, tag, names, started, token):
        self.cfg, self.tag, self.names, self.k, self.token = cfg, tag, names, 3 * len(names), token
        self.send, self.recv, self.bufs = started

    @staticmethod
    def start(cfg, groups, staged, after):
        started, token = _split_starts("gather_" + "_".join(groups) + "_chips_start", [[staged[n] for n in names] for names in groups.values()],
                                       [3 * len(names) for names in groups.values()], _gather_to_chips, after)
        return [_LateWeights(cfg, tag, names, s, token) for (tag, names), s in zip(groups.items(), started)]

    def pass_on(self, after):
        bufs = _split_wait(f"gather_{self.tag}_chips_wait", self.send, self.recv, self.bufs, after, _gather_to_chips)
        self.send, self.recv, self.bufs, token = _split_start(f"gather_{self.tag}_sibling_start", bufs, self.k, _gather_to_sibling,
                                                               self.token)
        return token

    def arrived(self, after):
        bufs = _split_wait(f"gather_{self.tag}_sibling_wait", self.send, self.recv, self.bufs, after, _gather_to_sibling)
        return {n: _gathered_to_kernel(self.cfg, n, b) for n, b in zip(self.names, bufs)}


def _step(cfg, a):
    chip = 2 * lax.axis_index("x") + lax.axis_index("y")
    core = lax.axis_index("c")
    big = BIG

    ffn = ("w_gate", "w_up", "w_down")
    first = ("w_in", "w_uq", "w_ukv")
    sp = {n: a[n] for n in SMALL}
    sharded = _pack_small([a[n] for n in SMALL_SHARDED], 2 * LANE)
    slabs = jnp.where(lax.broadcasted_iota(I32, (N_CHIPS,) + sharded.shape, 0) == chip, sharded[None], 0.0)
    staged = {"w_in": _stage_shard("w_in", a["w_in"], chip), "sharded_small": slabs}
    [in_weight] = _LateWeights.start(cfg, {"in": ("w_in", "sharded_small")}, staged, jnp.zeros((8, LANE), F32))
    behind = xn_early = _rms_pre(cfg, a["x"], sp["mix_pre_g"] + in_weight.token[0, 0])
    for n in big[1:]:
        behind = staged[n] = _stage_shard(n, a[n], chip, behind)
    mla_weights, out_weight, ffn_weights, down_weight = _LateWeights.start(
        cfg, {"mla": first[1:], "out": ("w_out",), "ffn": ffn[:2], "down": ffn[2:]}, staged, in_weight.pass_on(behind))
    W = in_weight.arrived(down_weight.token)
    allp = W.pop("sharded_small").reshape((N_CHIPS,) + sharded.shape)
    per_chip = [_unpack_small(allp[ch], [a[n].shape for n in SMALL_SHARDED]) for ch in range(N_CHIPS)]
    for k, n in enumerate(SMALL_SHARDED):
        sp[n] = jnp.concatenate([per_chip[ch][k] for ch in range(N_CHIPS)], axis=1)

    state = {}

    def ffn_grads_ready(grads):
        state["ffn_pairs"], token = _pair_exchange_start("ffn", [_grad_to_chips(cfg, n, grads[n]) for n in ffn_grads])
        return token

    def pair_sums(names, grads, theirs):
        return [_pair_sum(n, g, t, core) for n, g, t in zip(names, grads, theirs)]

    def early_grads_ready(grads):
        out_pairs, token = _pair_exchange_start("out", [_grad_to_chips(cfg, "w_out", grads["w_out"])])
        sums = pair_sums(ffn_grads, *_pair_exchange_wait("ffn", state["ffn_pairs"], token))
        sums += pair_sums(["w_out"], *_pair_exchange_wait("out", out_pairs, sums[-1]))
        state["early"] = _chip_exchange_start("early", sums)
        return state["early"][-1]

    def reduced_halves(tag, names, after):
        send_sems, recv_sems, s_bufs, l_bufs, _ = state[tag]
        s_bufs, l_bufs = _chip_exchange_wait(tag, send_sems, recv_sems, s_bufs, l_bufs, after)
        return [_chip_sum(n, s, t, chip) for n, s, t in zip(names, s_bufs, l_bufs)]

    def in_grad_ready(grads, after):
        if grads is not None:
            state["rest_pairs"], token = _pair_exchange_start("rest", [_grad_to_chips(cfg, n, grads[n]) for n in first])
            return token
        state["rest"] = _chip_exchange_start("rest", pair_sums(first, *_pair_exchange_wait("rest", state["rest_pairs"], after)))
        return state["rest"][-1]

    ffn_grads = ("w_down", "w_gate", "w_up")
    early = ffn_grads + ("w_out",)
    loss, grad_x, gW, gs = _local_grads(cfg, a["x"], a["loss_target"], W, sp, mla_weights, out_weight, ffn_weights, down_weight,
                                        ffn_grads_ready, early_grads_ready, in_grad_ready, xn_early, down_weight.token)
    out = {"grad_x": grad_x}

    def adamw(names, mine, theirs):
        for n, gm, gt in zip(names, mine, theirs):
            out["grad_" + n], out["delta_" + n], out["new_m_" + n], out["new_v_" + n] = _adamw_halves(
                "adamw_" + n, a[n], gm, gt, a["m_" + n], a["v_" + n], core)

    mine = reduced_halves("early", early, grad_x)
    theirs = _sibling_exchange("early", mine[:1])
    later, _ = _sibling_exchange_start("early", mine[1:], theirs[0])
    adamw(early[:1], mine[:1], theirs)
    e_mine, e_theirs = _sibling_exchange_wait("early", later, out["new_v_" + early[0]])
    adamw(early[3:], e_mine[2:], e_theirs[2:])
    mine = reduced_halves("rest", first, out["new_v_" + early[-1]])
    rest, token = _sibling_exchange_start("rest", mine, mine[0])
    small = _allreduce_small_start(_pack_small([gs[n] for n in SMALL] + [loss]), token)
    adamw(early[1:3], e_mine[:2], e_theirs[:2])
    adamw(first, *_sibling_exchange_wait("rest", rest, out["new_v_" + early[2]]))
    shapes = [gs[n].shape for n in SMALL] + [(1, LANE)]
    red = _unpack_small(_allreduce_small_wait(small, chip, core, out["new_v_" + first[-1]]), shapes)
    g_small = dict(zip(SMALL, red[:-1]))
    for n in SMALL_SHARDED:
        cs = a[n].shape[1]
        g_small[n] = lax.dynamic_slice_in_dim(g_small[n], chip * cs, cs, axis=1)
    out["loss"] = red[-1][0, 0]
    sshapes = [a[n].shape for n in SMALL]
    d, nm, nv = _adamw("adamw_small", _pack_small([a[n] for n in SMALL]), _pack_small([g_small[n] for n in SMALL]),
                       _pack_small([a["m_" + n] for n in SMALL]), _pack_small([a["v_" + n] for n in SMALL]))
    for n, dd, mm, vv in zip(SMALL, _unpack_small(d, sshapes), _unpack_small(nm, sshapes), _unpack_small(nv, sshapes)):
        out["grad_" + n], out["delta_" + n], out["new_m_" + n], out["new_v_" + n] = g_small[n], dd, mm, vv
    return out


def kernel(x, mix_pre_g, w_in, q_norm_g, w_uq, kv_norm_g, w_ukv, ssm_conv_w, ssm_conv_b, dt_bias, a_log, d_skip, ssm_norm_g, w_out, mix_post_g, ffn_pre_g, w_gate, w_up, ffn_conv_w, ffn_conv_b, w_down, ffn_post_g, loss_target, m_mix_pre_g, m_w_in, m_q_norm_g, m_w_uq, m_kv_norm_g, m_w_ukv, m_ssm_conv_w, m_ssm_conv_b, m_dt_bias, m_a_log, m_d_skip, m_ssm_norm_g, m_w_out, m_mix_post_g, m_ffn_pre_g, m_w_gate, m_w_up, m_ffn_conv_w, m_ffn_conv_b, m_w_down, m_ffn_post_g, v_mix_pre_g, v_w_in, v_q_norm_g, v_w_uq, v_kv_norm_g, v_w_ukv, v_ssm_conv_w, v_ssm_conv_b, v_dt_bias, v_a_log, v_d_skip, v_ssm_norm_g, v_w_out, v_mix_post_g, v_ffn_pre_g, v_w_gate, v_w_up, v_ffn_conv_w, v_ffn_conv_b, v_w_down, v_ffn_post_g):
    args = dict(locals())
    def given(k, v):
        if k in ("w_in", "m_w_in", "v_w_in"):
            return jnp.transpose(v, (2, 0, 1))
        return v if k.removeprefix("m_").removeprefix("v_") in BIG or v.ndim < 3 else v[0]

    out = _step(_FULL, {k: given(k, v) for k, v in args.items()})
    res = [out["loss"], out["grad_x"][None]]
    for pre in ("grad_", "delta_", "new_m_", "new_v_"):
        for n in WEIGHTS:
            o = out[pre + n]
            res.append(jnp.transpose(o, (1, 2, 0)) if n == "w_in" else o if n in BIG or args[n].ndim < 3 else o[None])
    return tuple(res)
```

```python
import math

import jax
import jax.numpy as jnp
from jax import lax
from jax.experimental import pallas as pl
from jax.experimental.pallas import tpu as pltpu

F32, BF16, I32 = jnp.float32, jnp.bfloat16, jnp.int32
NN = (((1,), (0,)), ((), ()))
NT = (((1,), (1,)), ((), ()))
TN = (((0,), (0,)), ((), ()))
HI = lax.Precision.HIGHEST
MESH_ID = pl.DeviceIdType.MESH

EPS = 1e-6
CHUNK = 64
NOPE, ROPE, VH = 128, 64, 128
ROPE_THETA = 10000.0
HP, NST = 64, 128
SSM_K, FFN_K = 4, 3
LANE = 128
N_CHIPS = 4
VMEM_LIMIT = 52 * 1024 * 1024
MM_TILE, MM_TILE_K = 1408, 2816

ADAM_LR, ADAM_B1, ADAM_B2, ADAM_EPS, ADAM_WD, ADAM_STEP = 0.001, 0.9, 0.999, 1e-08, 0.01, 10


class _Cfg:
    def __init__(self, S, D, QL, KVL, H, HS, G, DFF, T):
        self.S, self.D, self.QL, self.KVL, self.H, self.HS, self.G, self.DFF, self.T = S, D, QL, KVL, H, HS, G, DFF, T
        self.INNER = HS * HP
        self.CONVCH = self.INNER + 2 * G * NST
        self.QW = H * (NOPE + ROPE)
        self.KVW = H * (NOPE + VH)
        self.MLAW = H * VH
        self.MIXW = self.MLAW + self.INNER
        self.IN_COLS = QL + KVL + ROPE + self.INNER + self.CONVCH + HS
        natural, at = {}, 0
        for name, w in (("c_q", QL), ("c_kv", KVL), ("kr", ROPE), ("z", self.INNER), ("xbc", self.CONVCH), ("dt", HS)):
            natural[name] = (at, w)
            at += w
        self.seg, taken = {}, []
        for name in sorted(natural, key=lambda n: -natural[n][1]):
            w = -(-natural[name][1] // LANE) * LANE
            off = next(o for o in range(0, self.IN_COLS * 2, w) if all(o + w <= t or o >= t + tw for t, tw in taken))
            taken.append((off, w))
            self.seg[name] = (off, w) + natural[name]
        self.EXT = max(o + w for o, w in taken)
        self.NPAIR = HS // 2
        self.REP = HS // G

    def window(self, name):
        off, w, _, _ = self.seg[name]
        return w, off // w


_FULL = _Cfg(S=2048, D=2048, QL=768, KVL=512, H=8, HS=16, G=2, DFF=5632, T=256)
BIG = ("w_in", "w_uq", "w_ukv", "w_out", "w_gate", "w_up", "w_down")

SMALL = ("mix_pre_g", "q_norm_g", "kv_norm_g", "ssm_conv_w", "ssm_conv_b", "dt_bias", "a_log", "d_skip", "ssm_norm_g",
         "mix_post_g", "ffn_pre_g", "ffn_conv_w", "ffn_conv_b", "ffn_post_g")
SMALL_SHARDED = ("ssm_conv_w", "ffn_conv_w")
WEIGHTS = ("mix_pre_g", "w_in", "q_norm_g", "w_uq", "kv_norm_g", "w_ukv", "ssm_conv_w", "ssm_conv_b", "dt_bias", "a_log",
           "d_skip", "ssm_norm_g", "w_out", "mix_post_g", "ffn_pre_g", "w_gate", "w_up", "ffn_conv_w", "ffn_conv_b",
           "w_down", "ffn_post_g")


def _pick(n, target, mult):
    best = None
    for d in range(mult, min(n, target) + 1, mult):
        if n % d == 0:
            best = d
    return best if best is not None else n


def _params(sem=None):
    kw = dict(vmem_limit_bytes=VMEM_LIMIT)
    if sem is not None:
        kw["dimension_semantics"] = sem
    return pltpu.CompilerParams(**kw)


def _dot(a, b, dims=NN, precision=None):
    return lax.dot_general(a, b, dims, preferred_element_type=F32, precision=precision)


def _sigmoid(x):
    return 1.0 / (1.0 + jnp.exp(-x))


def _rs(x):
    return lax.rsqrt(jnp.mean(x * x, axis=-1, keepdims=True) + EPS)


def _rms_back(xh, r, dn):
    return r * (dn - xh * jnp.mean(dn * xh, axis=-1, keepdims=True))


def _colsum(v):
    return jnp.sum(v, axis=0, keepdims=True)


def _matmul(name, a, b, mode, out_dtype, a2=None, b2=None, chips=False, after=None):
    cs = None
    if mode == "nn":
        (M, K), N = a.shape, b.shape[-1]
        if chips:
            cs, N = N, N_CHIPS * N
    elif mode == "nt":
        (M, K), N = a.shape, b.shape[-2]
        if chips:
            cs = b.shape[-1]
    else:
        (K, M), N = a.shape, b.shape[1]
        if chips:
            cs = N // N_CHIPS
    tm = _pick(M, MM_TILE, LANE)
    tn = _pick(cs if chips and mode != "nt" else N, MM_TILE, LANE)
    tk = _pick(cs, MM_TILE, LANE) if chips and mode == "nt" else _pick(K, MM_TILE_K, LANE)
    nk = K // tk
    dims = {"nn": NN, "nt": NT, "tn": TN}[mode]
    a_spec = pl.BlockSpec((tk, tm), lambda i, j, k: (k, i)) if mode == "tn" else pl.BlockSpec((tm, tk), lambda i, j, k: (i, k))
    b_spec = pl.BlockSpec((tn, tk), lambda i, j, k: (j, k)) if mode == "nt" else pl.BlockSpec((tk, tn), lambda i, j, k: (k, j))
    o_spec = pl.BlockSpec((tm, tn), lambda i, j, k: (i, j))
    o_shape = (M, N)
    if chips and mode == "nn":
        per = cs // tn
        b_spec = pl.BlockSpec((None, tk, tn), lambda i, j, k: (j // per, k, j % per))
    elif chips and mode == "nt":
        per = cs // tk
        b_spec = pl.BlockSpec((None, tn, tk), lambda i, j, k: (k // per, j, k % per))
    elif chips:
        per = cs // tn
        o_spec = pl.BlockSpec((None, tm, tn), lambda i, j, k: (j // per, i, j % per))
        o_shape = (N_CHIPS, M, cs)
    two = a2 is not None

    def product(refs):
        part = _dot(refs[0][...].astype(BF16), refs[1][...].astype(BF16), dims)
        if two:
            part += _dot(refs[2][...].astype(BF16), refs[3][...].astype(BF16), dims)
        return part

    def body_whole_k(*refs):
        refs[-1][...] = product(refs).astype(refs[-1].dtype)

    def body(*refs):
        o_ref, acc_ref = refs[-2], refs[-1]
        k = pl.program_id(2)

        @pl.when(k == 0)
        def _():
            acc_ref[...] = product(refs)

        @pl.when(k > 0)
        def _():
            acc_ref[...] += product(refs)

        @pl.when(k == nk - 1)
        def _():
            o_ref[...] = acc_ref[...].astype(o_ref.dtype)

    ins = ((a, b, a2, b2) if two else (a, b)) + (() if after is None else (after,))
    return pl.pallas_call(
        body_whole_k if nk == 1 else body, name=name, grid=(M // tm, N // tn, nk),
        in_specs=[a_spec, b_spec] * (2 if two else 1) + ([] if after is None else [pl.BlockSpec(memory_space=pl.ANY)]),
        out_specs=o_spec,
        out_shape=jax.ShapeDtypeStruct(o_shape, out_dtype),
        scratch_shapes=[] if nk == 1 else [pltpu.VMEM((tm, tn), F32)],
        compiler_params=_params(("parallel", "parallel", "arbitrary")),
    )(*ins)


def _matmul_twin(name, a, b1, b2, mode, out_dtype):
    if mode == "nn":
        (M, K), cs = a.shape, b1.shape[-1]
        tm = _pick(M, MM_TILE // 2, LANE)
    else:
        (K, M), cs = a.shape, b1.shape[1] // N_CHIPS
        tm = _pick(M, MM_TILE, LANE)
    tn = _pick(cs, MM_TILE, LANE)
    per = cs // tn
    dims = NN if mode == "nn" else TN

    def body(a_ref, b1_ref, b2_ref, o1_ref, o2_ref):
        lhs = a_ref[...].astype(BF16)
        o1_ref[...] = _dot(lhs, b1_ref[...].astype(BF16), dims).astype(o1_ref.dtype)
        o2_ref[...] = _dot(lhs, b2_ref[...].astype(BF16), dims).astype(o2_ref.dtype)

    if mode == "nn":
        a_spec = pl.BlockSpec((tm, K), lambda i, j: (i, 0))
        b_spec = pl.BlockSpec((None, K, tn), lambda i, j: (j // per, 0, j % per))
        o_spec, o_shape = pl.BlockSpec((tm, tn), lambda i, j: (i, j)), (M, N_CHIPS * cs)
    else:
        a_spec = pl.BlockSpec((K, tm), lambda i, j: (0, i))
        b_spec = pl.BlockSpec((K, tn), lambda i, j: (0, j))
        o_spec, o_shape = pl.BlockSpec((None, tm, tn), lambda i, j: (j // per, i, j % per)), (N_CHIPS, M, cs)
    return pl.pallas_call(
        body, name=name, grid=(M // tm, N_CHIPS * per), in_specs=[a_spec, b_spec, b_spec], out_specs=[o_spec, o_spec],
        out_shape=[jax.ShapeDtypeStruct(o_shape, out_dtype)] * 2, compiler_params=_params(("parallel", "parallel")),
    )(a, b1, b2)


def _window(a):
    return (a[0], *a[1]) if isinstance(a, tuple) else (a, a.shape[1], 0)


def _rowwise(name, fn, rows, mats, outs, reds, ts, into=None):
    rows, widths, blocks = zip(*[_window(a) for a in rows])
    S = rows[0].shape[0]
    nr, nm, no = len(rows), len(mats), len(outs)
    ni = nr + nm + (into is not None)

    def body(*refs):
        res = fn(*[r[...] for r in refs[:nr + nm]])
        res = res if isinstance(res, (tuple, list)) else (res,)
        for r, v in zip(refs[ni:ni + no], res[:no]):
            r[...] = v.astype(r.dtype)
        first = pl.program_id(0) == 0
        for r, v in zip(refs[ni + no:], res[no:]):
            @pl.when(first)
            def _():
                r[...] = jnp.broadcast_to(v, r.shape)

            @pl.when(jnp.logical_not(first))
            def _():
                r[...] += jnp.broadcast_to(v, r.shape)

    in_specs = [pl.BlockSpec((ts, w), lambda i, b=b: (i, b)) for w, b in zip(widths, blocks)]
    in_specs += [pl.BlockSpec(m.shape, lambda i, nd=m.ndim: (0,) * nd) for m in mats]
    out_specs = [pl.BlockSpec((ts, w), lambda i: (i, 0)) for w, _ in outs]
    out_specs += [pl.BlockSpec(s, lambda i: (0, 0)) for s in reds]
    out_shape = [jax.ShapeDtypeStruct((S, w), dt) for w, dt in outs] + [jax.ShapeDtypeStruct(s, F32) for s in reds]
    if into is not None:
        in_specs.append(_ANY)
        out_specs[0] = pl.BlockSpec((ts, outs[0][0]), lambda i: (i, into[1]))
        out_shape[0] = jax.ShapeDtypeStruct(into[0].shape, into[0].dtype)
    return pl.pallas_call(
        body, name=name, grid=(S // ts,), in_specs=in_specs, out_specs=out_specs, out_shape=out_shape,
        input_output_aliases={} if into is None else {ni - 1: 0},
        compiler_params=_params(("arbitrary",) if reds else ("parallel",)),
    )(*rows, *mats, *([] if into is None else [into[0]]))


def _shift_down(v, s):
    if s == 0:
        return v
    rows = lax.broadcasted_iota(I32, v.shape, 0)
    return jnp.where(rows >= s, pltpu.roll(v, s, 0), 0.0)


def _shift_up(v, s):
    if s == 0:
        return v
    n = v.shape[0]
    rows = lax.broadcasted_iota(I32, v.shape, 0)
    return jnp.where(rows < n - s, pltpu.roll(v, n - s, 0), 0.0)


def _conv(x, w, b):
    K = w.shape[0]
    y = jnp.broadcast_to(b, x.shape)
    for k in range(K):
        y = y + w[k:k + 1, :] * _shift_down(x, K - 1 - k)
    return y


def _conv_back(x, w, dc):
    K = w.shape[0]
    dx = jnp.zeros_like(x)
    dw = []
    for k in range(K):
        up = _shift_up(dc, K - 1 - k)
        dx = dx + w[k:k + 1, :] * up
        dw.append(_colsum(up * x))
    return dx, jnp.concatenate(dw, axis=0), _colsum(dc)


def _colwise(name, fn, cols, vecs, outs, pouts, tc):
    cols, widths, blocks = zip(*[_window(a) for a in cols])
    S, C = cols[0].shape[0], widths[0]
    firsts = [b * (C // tc) for b in blocks]
    nc_, nv, no = len(cols), len(vecs), len(outs)

    def body(*refs):
        res = fn(*[r[...] for r in refs[:nc_ + nv]])
        res = res if isinstance(res, (tuple, list)) else (res,)
        for r, v in zip(refs[nc_ + nv:], res):
            r[...] = v.astype(r.dtype)

    in_specs = [pl.BlockSpec((S, tc), lambda j, f=f: (0, f + j)) for f in firsts]
    in_specs += [pl.BlockSpec((v.shape[0], tc), lambda j: (0, j)) for v in vecs]
    out_specs = [pl.BlockSpec((S, tc), lambda j: (0, j)) for _ in outs] + [pl.BlockSpec((k, tc), lambda j: (0, j)) for k in pouts]
    out_shape = [jax.ShapeDtypeStruct((S, C), dt) for dt in outs] + [jax.ShapeDtypeStruct((k, C), F32) for k in pouts]
    return pl.pallas_call(
        body, name=name, grid=(C // tc,), in_specs=in_specs, out_specs=out_specs, out_shape=out_shape,
        compiler_params=_params(("parallel",)),
    )(*cols, *vecs)


_G0, _G1 = math.sqrt(2.0 / math.pi), 0.044715


def _gelu(g):
    th = jnp.tanh(_G0 * (g + _G1 * g * g * g))
    return 0.5 * g * (1.0 + th), th


def _ffn_act(gate_pre, up, w, b):
    act, _ = _gelu(_conv(gate_pre, w, b))
    return act * up


def _ffn_act_back(dact, gate_pre, up, w, b):
    g = _conv(gate_pre, w, b)
    ge, th = _gelu(g)
    dge = 0.5 * (1.0 + th) + 0.5 * g * (1.0 - th * th) * _G0 * (1.0 + 3.0 * _G1 * g * g)
    dup = dact * ge
    dgate_pre, dw, db = _conv_back(gate_pre, w, dact * up * dge)
    return dgate_pre, dup, dw, db


def _ssm_act(xbc, w, b):
    c = _conv(xbc, w, b)
    return c * _sigmoid(c)


def _ssm_act_back(dxc, xbc, w, b):
    c = _conv(xbc, w, b)
    sg = _sigmoid(c)
    return _conv_back(xbc, w, dxc * sg * (1.0 + c * (1.0 - sg)))


def _rope_tables(S):
    inv = 1.0 / (ROPE_THETA ** (jnp.arange(0, ROPE, 2, dtype=F32) / ROPE))
    ang = jnp.arange(S, dtype=F32)[:, None] * inv[None, :]
    cos, sin = jnp.cos(ang), jnp.sin(ang)
    return jnp.tile(cos, (1, 4)), jnp.tile(jnp.concatenate([-sin, sin], axis=1), (1, 2))


def _swap_halves(x):
    lane = lax.broadcasted_iota(I32, x.shape, 1)
    w = x.shape[1]
    return jnp.where((lane % ROPE) < ROPE // 2, pltpu.roll(x, w - ROPE // 2, 1), pltpu.roll(x, ROPE // 2, 1))


def _rot(x, cos2, sin2):
    return x * cos2 + _swap_halves(x) * sin2


def _rot_back(dy, cos2, sin2):
    return dy * cos2 + _swap_halves(dy * sin2)


def _mla_pack(cfg, q, kv, kr, cos2, sin2):
    S, H = cfg.S, cfg.H
    ts = _pick(S, 256, 8)
    kr, _, kr_block = _window(kr)

    def body(q_ref, kv_ref, kr_ref, c_ref, s_ref, Q_ref, K_ref, V_ref):
        c2, s2 = c_ref[...], s_ref[...]
        krr = _rot(kr_ref[...], c2, s2)
        kr_half = (krr.astype(BF16), pltpu.roll(krr, ROPE, 1).astype(BF16))
        for j in range(H // 2):
            qr = _rot(q_ref[:, (H + j) * LANE:(H + j + 1) * LANE], c2, s2).astype(BF16)
            for h in (2 * j, 2 * j + 1):
                Q_ref[h, :, 0:LANE] = q_ref[:, h * LANE:(h + 1) * LANE].astype(BF16)
                Q_ref[h, :, LANE:] = qr
                K_ref[h, :, 0:LANE] = kv_ref[:, h * LANE:(h + 1) * LANE].astype(BF16)
                K_ref[h, :, LANE:] = kr_half[h % 2]
                V_ref[h] = kv_ref[:, (H + h) * LANE:(H + h + 1) * LANE].astype(BF16)

    tab = pl.BlockSpec((ts, LANE), lambda i: (i, 0))
    heads = lambda w: pl.BlockSpec((H, ts, w), lambda i: (0, i, 0))
    return pl.pallas_call(
        body, name="mla_pack", grid=(S // ts,),
        in_specs=[pl.BlockSpec((ts, cfg.QW), lambda i: (i, 0)), pl.BlockSpec((ts, cfg.KVW), lambda i: (i, 0)),
                  pl.BlockSpec((ts, LANE), lambda i: (i, kr_block)), tab, tab],
        out_specs=[heads(2 * LANE), heads(2 * LANE), heads(LANE)],
        out_shape=[jax.ShapeDtypeStruct((H, S, 2 * LANE), BF16), jax.ShapeDtypeStruct((H, S, 2 * LANE), BF16),
                   jax.ShapeDtypeStruct((H, S, LANE), BF16)],
        compiler_params=_params(("parallel",)),
    )(q, kv, kr, cos2, sin2)


def _mla_unpack(cfg, dQ, dK, dV, cos2, sin2):
    S, H = cfg.S, cfg.H
    ts = _pick(S, 256, 8)

    def body(dQ_ref, dK_ref, dV_ref, c_ref, s_ref, dq_ref, dkv_ref, dkr_ref):
        c2, s2 = c_ref[...], s_ref[...]
        lo = lax.broadcasted_iota(I32, (ts, LANE), 1) < ROPE
        tk = jnp.zeros((ts, LANE), F32)
        for h in range(H):
            dq_ref[:, h * LANE:(h + 1) * LANE] = dQ_ref[h, :, 0:LANE].astype(BF16)
            dkv_ref[:, h * LANE:(h + 1) * LANE] = dK_ref[h, :, 0:LANE].astype(BF16)
            dkv_ref[:, (H + h) * LANE:(H + h + 1) * LANE] = dV_ref[h].astype(BF16)
            own = lo if h % 2 == 0 else jnp.logical_not(lo)
            tk = tk + jnp.where(own, dK_ref[h, :, LANE:], 0.0)
        for j in range(H // 2):
            dr = dQ_ref[2 * j, :, LANE:] + dQ_ref[2 * j + 1, :, LANE:]
            dq_ref[:, (H + j) * LANE:(H + j + 1) * LANE] = _rot_back(dr, c2, s2).astype(BF16)
        dkr_rot = jnp.where(lo, tk + pltpu.roll(tk, ROPE, 1), 0.0)
        dkr_ref[...] = _rot_back(dkr_rot, c2, s2).astype(BF16)

    tab = pl.BlockSpec((ts, LANE), lambda i: (i, 0))
    return pl.pallas_call(
        body, name="mla_unpack", grid=(S // ts,),
        in_specs=[pl.BlockSpec((H, ts, 2 * LANE), lambda i: (0, i, 0)), pl.BlockSpec((H, ts, 2 * LANE), lambda i: (0, i, 0)),
                  pl.BlockSpec((H, ts, LANE), lambda i: (0, i, 0)), tab, tab],
        out_specs=[pl.BlockSpec((ts, cfg.QW), lambda i: (i, 0)), pl.BlockSpec((ts, cfg.KVW), lambda i: (i, 0)), tab],
        out_shape=[jax.ShapeDtypeStruct((S, cfg.QW), BF16), jax.ShapeDtypeStruct((S, cfg.KVW), BF16),
                   jax.ShapeDtypeStruct((S, LANE), BF16)],
        compiler_params=_params(("parallel",)),
    )(dQ, dK, dV, cos2, sin2)


_ATT_T = 256
_ATT_HB = 8
_ATT_SCALE = (NOPE + ROPE) ** -0.5


def _diag_mask(transposed=False):
    r = lax.broadcasted_iota(I32, (_ATT_T, _ATT_T), 0) // CHUNK
    c = lax.broadcasted_iota(I32, (_ATT_T, _ATT_T), 1) // CHUNK
    return r <= c if transposed else c <= r


def _row_form(col):
    return jnp.broadcast_to(col, (col.shape[0], LANE)).T[0:8, :]


def _attn_fwd(cfg, Q, K, V):
    S, H, T, HB = cfg.S, cfg.H, _ATT_T, min(cfg.H, _ATT_HB)

    def body(q_ref, k_ref, v_ref, o_ref, lse_t_ref, ob_ref):
        qi = pl.program_id(1)

        def head_step(b, kb, carry, mask):
            m, l, acc = carry
            ks = pl.multiple_of(kb * T, T)
            s = _dot(q_ref[b], k_ref[b, pl.ds(ks, T), :], NT) * _ATT_SCALE
            if mask is not None:
                s = jnp.where(mask, s, -1e30)
            m_new = jnp.maximum(m, jnp.max(s, axis=1, keepdims=True))
            p = jnp.exp(s - m_new)
            alpha = jnp.exp(m - m_new)
            l = alpha * l + jnp.sum(p, axis=1, keepdims=True)
            acc = alpha * acc + _dot(p.astype(BF16), v_ref[b, pl.ds(ks, T), :])
            return m_new, l, acc

        def step(kb, carry, mask=None):
            return tuple(head_step(b, kb, carry[b], mask) for b in range(HB))

        init = (jnp.full((T, 1), -1e30, F32), jnp.zeros((T, 1), F32), jnp.zeros((T, VH), F32))
        done = step(qi, lax.fori_loop(0, qi, step, (init,) * HB), _diag_mask())
        for b, (m, l, acc) in enumerate(done):
            o_ref[:, b * LANE:(b + 1) * LANE] = acc / l
            ob_ref[:, b * LANE:(b + 1) * LANE] = (acc / l).astype(BF16)
            lse_t_ref[b] = _row_form(m + jnp.log(l))

    return pl.pallas_call(
        body, name="attn_fwd", grid=(H // HB, S // T),
        in_specs=[pl.BlockSpec((HB, T, 2 * LANE), lambda h, i: (h, i, 0)), pl.BlockSpec((HB, S, 2 * LANE), lambda h, i: (h, 0, 0)),
                  pl.BlockSpec((HB, S, LANE), lambda h, i: (h, 0, 0))],
        out_specs=[pl.BlockSpec((T, HB * LANE), lambda h, i: (i, h)), pl.BlockSpec((HB, 8, T), lambda h, i: (h, 0, i)),
                   pl.BlockSpec((T, HB * LANE), lambda h, i: (i, h))],
        out_shape=[jax.ShapeDtypeStruct((S, H * LANE), F32), jax.ShapeDtypeStruct((H, 8, S), F32),
                   jax.ShapeDtypeStruct((S, cfg.MLAW + cfg.INNER), BF16)],
        compiler_params=_params(("parallel", "parallel")),
    )(Q, K, V)


def _attn_delta(cfg, do, o, after):
    S, H, T = cfg.S, cfg.H, _ATT_T

    def body(do_ref, o_ref, after_ref, dl_t_ref):
        for h in range(H):
            sl = slice(h * LANE, (h + 1) * LANE)
            dl_t_ref[h] = _row_form(jnp.sum(do_ref[:, sl] * o_ref[:, sl], axis=1, keepdims=True))

    wide = pl.BlockSpec((T, H * LANE), lambda i: (i, 0))
    return pl.pallas_call(
        body, name="attn_delta", grid=(S // T,), in_specs=[wide, wide, _ANY],
        out_specs=pl.BlockSpec((H, 8, T), lambda i: (0, 0, i)), out_shape=jax.ShapeDtypeStruct((H, 8, S), F32),
        compiler_params=_params(("parallel",)),
    )(do, o, after)


_ATT_HB_BWD = 4


def _attn_bwd(cfg, Q, K, V, do, lse_t, delta_t):
    S, H, T, HB = cfg.S, cfg.H, _ATT_T, min(cfg.H, _ATT_HB_BWD)
    nq = S // T

    def body(q_ref, k_ref, v_ref, do_ref, lse_ref, dl_ref, dq_ref, dk_ref, dv_ref):
        kb = pl.program_id(1)

        @pl.when(kb == 0)
        def _():
            dq_ref[...] = jnp.zeros_like(dq_ref)

        def head_step(b, qi, carry, mask):
            dk, dv = carry
            qs = pl.multiple_of(qi * T, T)
            q = q_ref[b, pl.ds(qs, T), :]
            k = k_ref[b]
            dob = do_ref[pl.ds(qs, T), b * LANE:(b + 1) * LANE].astype(BF16)
            s = _dot(k, q, NT) * _ATT_SCALE
            if mask is not None:
                s = jnp.where(mask, s, -1e30)
            p = jnp.exp(s - lse_ref[b, 0:1, pl.ds(qs, T)])
            dv = dv + _dot(p.astype(BF16), dob)
            dp = _dot(v_ref[b], dob, NT)
            ds = (p * (dp - dl_ref[b, 0:1, pl.ds(qs, T)]) * _ATT_SCALE).astype(BF16)
            dk = dk + _dot(ds, q)
            dq_ref[b, pl.ds(qs, T), :] += _dot(ds, k, TN)
            return dk, dv

        def step(qi, carry, mask=None):
            return tuple(head_step(b, qi, carry[b], mask) for b in range(HB))

        zero = (jnp.zeros((T, 2 * LANE), F32), jnp.zeros((T, VH), F32))
        done = lax.fori_loop(kb + 1, nq, step, step(kb, (zero,) * HB, _diag_mask(transposed=True)))
        for b, (dk, dv) in enumerate(done):
            dk_ref[b] = dk
            dv_ref[b] = dv

    row = pl.BlockSpec((HB, 8, S), lambda h, j: (h, 0, 0))
    whole = pl.BlockSpec((HB, S, 2 * LANE), lambda h, j: (h, 0, 0))
    return pl.pallas_call(
        body, name="attn_bwd", grid=(H // HB, S // T),
        in_specs=[whole, pl.BlockSpec((HB, T, 2 * LANE), lambda h, j: (h, j, 0)), pl.BlockSpec((HB, T, LANE), lambda h, j: (h, j, 0)),
                  pl.BlockSpec((S, HB * LANE), lambda h, j: (0, h)), row, row],
        out_specs=[whole, pl.BlockSpec((HB, T, 2 * LANE), lambda h, j: (h, j, 0)), pl.BlockSpec((HB, T, LANE), lambda h, j: (h, j, 0))],
        out_shape=[jax.ShapeDtypeStruct((H, S, 2 * LANE), F32), jax.ShapeDtypeStruct((H, S, 2 * LANE), F32),
                   jax.ShapeDtypeStruct((H, S, LANE), F32)],
        compiler_params=_params(("parallel", "arbitrary")),
    )(Q, K, V, do, lse_t, delta_t)


def _expand_matrix(cfg):
    r = lax.broadcasted_iota(I32, (LANE, cfg.INNER), 0)
    c = lax.broadcasted_iota(I32, (LANE, cfg.INNER), 1)
    return (r == c // HP).astype(F32)


def _softplus(x):
    return jnp.maximum(x, 0.0) + jnp.log(1.0 + jnp.exp(-jnp.abs(x)))


def _ssd_prep(cfg, dt_raw, dt_bias_pad, a_log_pad, expand):
    HS = cfg.HS

    def fn(raw, bias, alog, E):
        heads = lax.broadcasted_iota(I32, raw.shape, 1) < HS
        dt = jnp.where(heads, _softplus(raw + bias), 0.0)
        a = dt * jnp.where(heads[0:1], -jnp.exp(alog), 0.0)
        return dt, a, _dot(dt, E, precision=HI)

    return _rowwise("ssd_prep", fn, [dt_raw], [dt_bias_pad, a_log_pad, expand],
                    [(LANE, F32), (LANE, F32), (cfg.INNER, F32)], [], _pick(cfg.S, 512, 8))


def _tril(T):
    return lax.broadcasted_iota(I32, (T, T), 0) >= lax.broadcasted_iota(I32, (T, T), 1)


def _ssd_fwd(cfg, xc, dt_exp, a_small, dskip_exp, expand):
    S, T, INNER, G, NPAIR = cfg.S, cfg.T, cfg.INNER, cfg.G, cfg.NPAIR
    NC = S // T

    def body(xc_ref, dte_ref, as_ref, dsk_ref, e_ref, y_ref, hin_ref, ht_ref):
        @pl.when(pl.program_id(0) == 0)
        def _():
            ht_ref[...] = jnp.zeros_like(ht_ref)

        tril = _tril(T)
        tri = tril.astype(F32)
        acs_s = _dot(tri, as_ref[...], precision=HI)
        acs_e = _dot(acs_s, e_ref[...], precision=HI)
        acs_t = acs_s.T
        lo = lax.broadcasted_iota(I32, (T, LANE), 1) < HP
        for g in range(G):
            Bb = xc_ref[:, INNER + g * NST:INNER + (g + 1) * NST].astype(BF16)
            Cb = xc_ref[:, INNER + (G + g) * NST:INNER + (G + g + 1) * NST].astype(BF16)
            Gm = _dot(Cb, Bb, NT)
            for j in range(g * NPAIR // G, (g + 1) * NPAIR // G):
                sl = slice(j * LANE, (j + 1) * LANE)
                Xp = xc_ref[:, sl]
                Xdt = Xp * dte_ref[:, sl]
                Xb = Xdt.astype(BF16)
                acs_p = acs_e[:, sl]
                last = acs_p[T - 1:T, :]
                Hin = ht_ref[j]
                hin_ref[0, j] = Hin
                yd = []
                for e in (0, 1):
                    h = 2 * j + e
                    Lm = jnp.exp(jnp.where(tril, acs_s[:, h:h + 1] - acs_t[h:h + 1, :], -1e30))
                    yd.append(_dot((Gm * Lm).astype(BF16), Xb))
                y_off = _dot(Cb, Hin.astype(BF16)) * jnp.exp(acs_p)
                y_ref[:, sl] = jnp.where(lo, yd[0], yd[1]) + y_off + Xp * dsk_ref[:, sl]
                st = _dot(Bb, (Xdt * jnp.exp(last - acs_p)).astype(BF16), TN)
                ht_ref[j] = jnp.exp(last) * Hin + st

    rows = lambda w: pl.BlockSpec((T, w), lambda c: (c, 0))
    return pl.pallas_call(
        body, name="ssd_fwd", grid=(NC,),
        in_specs=[rows(cfg.CONVCH), rows(INNER), rows(LANE), pl.BlockSpec((1, INNER), lambda c: (0, 0)),
                  pl.BlockSpec((LANE, INNER), lambda c: (0, 0))],
        out_specs=[rows(INNER), pl.BlockSpec((1, NPAIR, NST, LANE), lambda c: (c, 0, 0, 0))],
        out_shape=[jax.ShapeDtypeStruct((S, INNER), F32), jax.ShapeDtypeStruct((NC, NPAIR, NST, LANE), F32)],
        scratch_shapes=[pltpu.VMEM((NPAIR, NST, LANE), F32)],
        compiler_params=_params(("arbitrary",)),
    )(xc, dt_exp, a_small, dskip_exp, expand)


def _ssd_bwd(cfg, dy, xc, dt_exp, a_small, dskip_exp, hin, dt_raw, dt_bias_pad, a_log_pad, expand):
    S, T, INNER, G, NPAIR, HS = cfg.S, cfg.T, cfg.INNER, cfg.G, cfg.NPAIR, cfg.HS
    NC = S // T

    def body(dy_ref, xc_ref, dte_ref, as_ref, dsk_ref, hin_ref, raw_ref, bias_ref, alog_ref, e_ref,
             dxc_ref, draw_ref, dbias_ref, dalog_ref, dskip_ref, dht_ref, cols_ref, rows_ref, dacs_ref, ddt_ref):
        first = pl.program_id(0) == 0

        @pl.when(first)
        def _():
            dht_ref[...] = jnp.zeros_like(dht_ref)

        tril = _tril(T)
        tri = tril.astype(F32)
        a_s = as_ref[...]
        acs_s = _dot(tri, a_s, precision=HI)
        acs_e = _dot(acs_s, e_ref[...], precision=HI)
        acs_t = acs_s.T
        lo = lax.broadcasted_iota(I32, (T, LANE), 1) < HP
        last_row = lax.broadcasted_iota(I32, (T, LANE), 0) == T - 1
        cols_ref[...] = jnp.zeros_like(cols_ref)
        rows_ref[...] = jnp.zeros_like(rows_ref)
        dsk_parts = []
        for g in range(G):
            bsl = slice(INNER + g * NST, INNER + (g + 1) * NST)
            csl = slice(INNER + (G + g) * NST, INNER + (G + g + 1) * NST)
            Bb = xc_ref[:, bsl].astype(BF16)
            Cb = xc_ref[:, csl].astype(BF16)
            Gm = _dot(Cb, Bb, NT)
            dG = jnp.zeros((T, T), F32)
            dB = jnp.zeros((T, NST), F32)
            dC = jnp.zeros((T, NST), F32)
            for j in range(g * NPAIR // G, (g + 1) * NPAIR // G):
                sl = slice(j * LANE, (j + 1) * LANE)
                Xp = xc_ref[:, sl]
                dtp = dte_ref[:, sl]
                Xdt = Xp * dtp
                Xb = Xdt.astype(BF16)
                acs_p = acs_e[:, sl]
                last = acs_p[T - 1:T, :]
                e_p, dec, cd = jnp.exp(acs_p), jnp.exp(last - acs_p), jnp.exp(last)
                Hin = hin_ref[0, j]
                Hb = Hin.astype(BF16)
                dHn = dht_ref[j]
                dHb = dHn.astype(BF16)
                dYp = dy_ref[:, sl]
                z = _dot(Cb, Hb)
                dz = (dYp * e_p).astype(BF16)
                dacs_p = dYp * z * e_p
                dC = dC + _dot(dz, Hb, NT)
                dHin = _dot(Cb, dz, TN) + cd * dHn
                dlast = _colsum(dHn * Hin) * cd
                qv = _dot(Bb, dHb)
                dXdt = qv * dec
                ddec = qv * Xdt * dec
                dacs_p = dacs_p - ddec
                dlast = dlast + _colsum(ddec)
                dB = dB + _dot((Xdt * dec).astype(BF16), dHb, NT)
                for e in (0, 1):
                    h = 2 * j + e
                    Lm = jnp.exp(jnp.where(tril, acs_s[:, h:h + 1] - acs_t[h:h + 1, :], -1e30))
                    Mh = Gm * Lm
                    dYe = jnp.where(lo if e == 0 else jnp.logical_not(lo), dYp, 0.0).astype(BF16)
                    dM = _dot(dYe, Xb, NT)
                    dXdt = dXdt + _dot(Mh.astype(BF16), dYe, TN)
                    W = dM * Mh
                    cols_ref[:, h:h + 1] = jnp.sum(W, axis=1, keepdims=True)
                    rows_ref[h:h + 1, :] = _colsum(W)
                    dG = dG + dM * Lm
                dacs_ref[:, sl] = dacs_p + jnp.where(last_row, dlast, 0.0)
                ddt_ref[:, sl] = dXdt * Xp
                dxc_ref[:, sl] = dXdt * dtp + dYp * dsk_ref[:, sl]
                dsk_parts.append(_colsum(dYp * Xp))
                dht_ref[j] = dHin
            dGb = dG.astype(BF16)
            dxc_ref[:, bsl] = dB + _dot(dGb, Cb, TN)
            dxc_ref[:, csl] = dC + _dot(dGb, Bb)
        E = e_ref[...]
        dacs_s = cols_ref[...] - rows_ref[...].T + _dot(dacs_ref[...], E, NT, precision=HI)
        da = _dot(tri, dacs_s, TN, precision=HI)
        heads = lax.broadcasted_iota(I32, (1, LANE), 1) < HS
        A = jnp.where(heads, -jnp.exp(alog_ref[...]), 0.0)
        ddt = _dot(ddt_ref[...], E, NT, precision=HI) + da * A
        draw = jnp.where(heads, ddt * _sigmoid(raw_ref[...] + bias_ref[...]), 0.0)
        draw_ref[...] = draw
        dsk = _dot(jnp.broadcast_to(jnp.concatenate(dsk_parts, axis=1), (8, INNER)), E, NT, precision=HI)[0:1]
        for ref, val in ((dbias_ref, _colsum(draw)), (dalog_ref, _colsum(da * a_s)), (dskip_ref, dsk)):
            @pl.when(first)
            def _():
                ref[...] = val

            @pl.when(jnp.logical_not(first))
            def _():
                ref[...] += val

    dt_raw, _, raw_block = _window(dt_raw)
    rows = lambda w, b=0: pl.BlockSpec((T, w), lambda c: (NC - 1 - c, b))
    vec = lambda w: pl.BlockSpec((1, w), lambda c: (0, 0))
    return pl.pallas_call(
        body, name="ssd_bwd", grid=(NC,),
        in_specs=[rows(INNER), rows(cfg.CONVCH), rows(INNER), rows(LANE), vec(INNER),
                  pl.BlockSpec((1, NPAIR, NST, LANE), lambda c: (NC - 1 - c, 0, 0, 0)), rows(LANE, raw_block), vec(LANE), vec(LANE),
                  pl.BlockSpec((LANE, INNER), lambda c: (0, 0))],
        out_specs=[rows(cfg.CONVCH), rows(LANE), vec(LANE), vec(LANE), vec(LANE)],
        out_shape=[jax.ShapeDtypeStruct((S, cfg.CONVCH), F32), jax.ShapeDtypeStruct((S, LANE), F32)]
        + [jax.ShapeDtypeStruct((1, LANE), F32)] * 3,
        scratch_shapes=[pltpu.VMEM((NPAIR, NST, LANE), F32), pltpu.VMEM((T, LANE), F32), pltpu.VMEM((LANE, T), F32),
                        pltpu.VMEM((T, INNER), F32), pltpu.VMEM((T, INNER), F32)],
        compiler_params=_params(("arbitrary",)),
    )(dy, xc, dt_exp, a_small, dskip_exp, hin, dt_raw, dt_bias_pad, a_log_pad, expand)


def _ssd_post(cfg, y, z, norm_g, into):
    W = cfg.INNER // cfg.G

    def fn(y, z, g):
        yz = y * z * _sigmoid(z)
        return jnp.concatenate([yz[:, i * W:(i + 1) * W] * _rs(yz[:, i * W:(i + 1) * W]) for i in range(cfg.G)], axis=1) * g

    return _rowwise("ssd_post", fn, [y, z], [norm_g], [(cfg.INNER, BF16)], [], _pick(cfg.S, 256, 8), (into, cfg.MLAW // cfg.INNER))[0]


def _ssd_post_bwd(cfg, db, y, z, norm_g):
    W = cfg.INNER // cfg.G

    def fn(db, y, z, g):
        sg = _sigmoid(z)
        yz = y * z * sg
        dn = db * g
        dyz, nh = [], []
        for i in range(cfg.G):
            seg = yz[:, i * W:(i + 1) * W]
            r = _rs(seg)
            nh.append(seg * r)
            dyz.append(_rms_back(nh[-1], r, dn[:, i * W:(i + 1) * W]))
        dyz = jnp.concatenate(dyz, axis=1)
        return dyz * z * sg, dyz * y * sg * (1.0 + z * (1.0 - sg)), _colsum(db * jnp.concatenate(nh, axis=1))

    return _rowwise("ssd_post_bwd", fn, [db, y, z], [norm_g], [(cfg.INNER, F32), (cfg.INNER, F32)], [(1, cfg.INNER)],
                    _pick(cfg.S, 256, 8))


def _rms_pre(cfg, x, g):
    return _rowwise("rms_pre", lambda x, g: x * _rs(x) * g, [x], [g], [(cfg.D, BF16)], [], _pick(cfg.S, 256, 8))[0]


def _local_grads(cfg, x, tgt, W, sp, mla_weights=None, out_weight=None, ffn_weights=None, down_weight=None,
                 ffn_grads_ready=None, early_grads_ready=None, in_grad_ready=None, xn=None, after_in=None):
    S, D, H, INNER = cfg.S, cfg.D, cfg.H, cfg.INNER
    ts = _pick(S, 256, 8)
    tc = _CONV_COLS

    if xn is None:
        xn = _rms_pre(cfg, x, sp["mix_pre_g"])
    u = _matmul("mm_in", xn, W["w_in"], "nt", F32, after=after_in)
    c_q, c_kv, kr, z, xbc, dt_raw = [(u, cfg.window(n)) for n in ("c_q", "c_kv", "kr", "z", "xbc", "dt")]

    if mla_weights is not None:
        sp = dict(sp, q_norm_g=sp["q_norm_g"] + mla_weights.pass_on(u)[0, 0])
    cqn = _rowwise("rms_q", lambda x, g: x * _rs(x) * g, [c_q], [sp["q_norm_g"]], [(cfg.QL, BF16)], [], ts)[0]
    ckvn = _rowwise("rms_kv", lambda x, g: x * _rs(x) * g, [c_kv], [sp["kv_norm_g"]], [(cfg.KVL, BF16)], [], ts)[0]
    if mla_weights is not None:
        W = dict(W, **mla_weights.arrived(ckvn))
    q = _matmul("mm_uq", cqn, W["w_uq"], "nn", F32)
    kv = _matmul("mm_ukv", ckvn, W["w_ukv"], "nn", F32)
    cos2, sin2 = _rope_tables(S)
    Qh, Kh, Vh = _mla_pack(cfg, q, kv, kr, cos2, sin2)
    a_out, lse_t, ab_out = _attn_fwd(cfg, Qh, Kh, Vh)
    if out_weight is not None:
        sp = dict(sp, ssm_conv_b=sp["ssm_conv_b"] + out_weight.pass_on(a_out)[0, 0])

    pad = lambda v: jnp.pad(v, ((0, 0), (0, LANE - v.shape[1])))
    expand = _expand_matrix(cfg)
    dt_bias_pad, a_log_pad = pad(sp["dt_bias"]), pad(sp["a_log"])
    dskip_exp = jnp.repeat(sp["d_skip"], HP, axis=1)
    xc = _colwise("ssm_act", _ssm_act, [xbc], [sp["ssm_conv_w"], sp["ssm_conv_b"]], [F32], [], tc)[0]
    dt_s, a_s, dt_exp = _ssd_prep(cfg, dt_raw, dt_bias_pad, a_log_pad, expand)
    y_ssd, hin = _ssd_fwd(cfg, xc, dt_exp, a_s, dskip_exp, expand)
    ab_out = _ssd_post(cfg, y_ssd, z, sp["ssm_norm_g"], ab_out)

    if out_weight is not None:
        W = dict(W, **out_weight.arrived(ab_out))
    if ffn_weights is not None:
        sp = dict(sp, mix_post_g=sp["mix_post_g"] + ffn_weights.pass_on(ab_out)[0, 0])
    mix = _matmul("mm_out", ab_out, W["w_out"], "nn", F32)

    def mid(x, mix, g_mp, g_fp):
        x1 = x + mix * _rs(mix) * g_mp
        return x1, x1 * _rs(x1) * g_fp

    x1, h2 = _rowwise("fwd_mid", mid, [x, mix], [sp["mix_post_g"], sp["ffn_pre_g"]], [(D, F32), (D, BF16)], [], ts)
    if ffn_weights is not None:
        W = dict(W, **ffn_weights.arrived(h2))
    gate_pre, up = _matmul_twin("mm_gate_up", h2, W["w_gate"], W["w_up"], "nn", F32)
    if down_weight is not None:
        sp = dict(sp, ffn_conv_b=sp["ffn_conv_b"] + down_weight.pass_on(gate_pre)[0, 0])
    act = _colwise("ffn_act", _ffn_act, [gate_pre, up], [sp["ffn_conv_w"], sp["ffn_conv_b"]], [BF16], [], tc)[0]
    if down_weight is not None:
        W = dict(W, **down_weight.arrived(act))
    f = _matmul("mm_down", act, W["w_down"], "nn", F32)

    def final(x1, f, t, g):
        r = _rs(f)
        fh = f * r
        err = x1 + fh * g - t
        loss = 0.5 * jnp.sum(jnp.mean(err * err, axis=-1, keepdims=True), axis=0, keepdims=True)
        dy = err * (1.0 / D)
        return dy, _rms_back(fh, r, dy * g), _colsum(dy * fh), loss

    dy, df, g_ffn_post, loss = _rowwise("final", final, [x1, f, tgt], [sp["ffn_post_g"]], [(D, F32), (D, BF16)],
                                        [(1, D), (1, LANE)], ts)
    gW = {}
    dact = _matmul("mm_down_dx", df, W["w_down"], "nt", F32)
    gW["w_down"] = _matmul("mm_down_dw", act, df, "tn", BF16)
    dgate, dup, g_ffn_conv_w, g_ffn_conv_b = _colwise(
        "ffn_act_bwd", _ffn_act_back, [dact, gate_pre, up], [sp["ffn_conv_w"], sp["ffn_conv_b"]], [BF16, BF16], [FFN_K, 1], tc)
    gW["w_gate"], gW["w_up"] = _matmul_twin("mm_gate_up_dw", h2, dgate, dup, "tn", BF16)
    if ffn_grads_ready is not None:
        sp = dict(sp, ffn_pre_g=sp["ffn_pre_g"] + ffn_grads_ready({n: gW[n] for n in ("w_down", "w_gate", "w_up")})[0, 0])
    dh2 = _matmul("mm_gu_dx", dgate, W["w_gate"], "nt", F32, dup, W["w_up"], chips=True)

    def mid_back(dy, dh2, x1, mix, g_mp, g_fp):
        r2 = _rs(x1)
        xh = x1 * r2
        dx1 = dy + _rms_back(xh, r2, dh2 * g_fp)
        r1 = _rs(mix)
        mh = mix * r1
        return dx1, _rms_back(mh, r1, dx1 * g_mp), _colsum(dh2 * xh), _colsum(dx1 * mh)

    dx1, dmix, g_ffn_pre, g_mix_post = _rowwise("bwd_mid", mid_back, [dy, dh2, x1, mix], [sp["mix_post_g"], sp["ffn_pre_g"]],
                                                [(D, F32), (D, BF16)], [(1, D), (1, D)], ts)
    dab_out = _matmul("mm_out_dx", dmix, W["w_out"], "nt", F32)
    db_out = (dab_out, (INNER, cfg.MLAW // INNER))
    gW["w_out"] = _matmul("mm_out_dw", ab_out, dmix, "tn", BF16)
    early_token = jnp.zeros((8, LANE), F32)
    if early_grads_ready is not None:
        early_token = early_grads_ready({n: gW[n] for n in ("w_down", "w_gate", "w_up", "w_out")})
        sp = dict(sp, ssm_norm_g=sp["ssm_norm_g"] + early_token[0, 0])

    dy_ssd, dz, g_ssm_norm = _ssd_post_bwd(cfg, db_out, y_ssd, z, sp["ssm_norm_g"])
    dxc, ddt_raw, g_dt_bias, g_a_log, g_d_skip = _ssd_bwd(cfg, dy_ssd, xc, dt_exp, a_s, dskip_exp, hin, dt_raw,
                                                          dt_bias_pad, a_log_pad, expand)
    dxbc, g_ssm_conv_w, g_ssm_conv_b = _colwise("ssm_act_bwd", _ssm_act_back, [dxc, xbc], [sp["ssm_conv_w"], sp["ssm_conv_b"]],
                                                [BF16], [SSM_K, 1], tc)

    delta_t = _attn_delta(cfg, dab_out, a_out, early_token)
    dQ, dK, dV = _attn_bwd(cfg, Qh, Kh, Vh, dab_out, lse_t, delta_t)
    dq, dkv, dkr = _mla_unpack(cfg, dQ, dK, dV, cos2, sin2)
    dcqn = _matmul("mm_uq_dx", dq, W["w_uq"], "nt", F32)
    dckvn = _matmul("mm_ukv_dx", dkv, W["w_ukv"], "nt", F32)
    gW["w_uq"] = _matmul("mm_uq_dw", cqn, dq, "tn", BF16)
    gW["w_ukv"] = _matmul("mm_ukv_dw", ckvn, dkv, "tn", BF16)

    def rms_back(x, dy, g):
        r = _rs(x)
        xh = x * r
        return _rms_back(xh, r, dy * g), _colsum(dy * xh)

    dc_q, g_q_norm = _rowwise("rms_q_bwd", rms_back, [c_q, dcqn], [sp["q_norm_g"]], [(cfg.QL, BF16)], [(1, cfg.QL)], ts)
    dc_kv, g_kv_norm = _rowwise("rms_kv_bwd", rms_back, [c_kv, dckvn], [sp["kv_norm_g"]], [(cfg.KVL, BF16)], [(1, cfg.KVL)], ts)

    du = dict(c_q=dc_q, c_kv=dc_kv, kr=dkr, z=dz.astype(BF16), xbc=dxbc, dt=ddt_raw.astype(BF16))
    du = jnp.concatenate([du[n] for n in sorted(du, key=lambda n: cfg.seg[n][0])], axis=1)
    assert du.shape[1] == cfg.EXT, "the layout of u has gaps"
    gW["w_in"] = _matmul("mm_in_dw", du, xn, "tn", BF16)
    if in_grad_ready is None:
        dxn = _matmul("mm_in_dx", du, W["w_in"], "nn", F32)
    else:
        token = in_grad_ready({n: gW[n] for n in ("w_in", "w_uq", "w_ukv")}, None)
        dxn = _matmul("mm_in_dx", du, W["w_in"], "nn", F32, after=token)
        sp = dict(sp, mix_pre_g=sp["mix_pre_g"] + in_grad_ready(None, dxn)[0, 0])

    def first_back(dx1, dxn, x, g):
        r = _rs(x)
        xh = x * r
        return dx1 + _rms_back(xh, r, dxn * g), _colsum(dxn * xh)

    grad_x, g_mix_pre = _rowwise("bwd_first", first_back, [dx1, dxn, x], [sp["mix_pre_g"]], [(D, F32)], [(1, D)], ts)

    gs = dict(mix_pre_g=g_mix_pre, q_norm_g=g_q_norm, kv_norm_g=g_kv_norm, ssm_conv_w=g_ssm_conv_w, ssm_conv_b=g_ssm_conv_b,
              dt_bias=g_dt_bias[:, :cfg.HS], a_log=g_a_log[:, :cfg.HS], d_skip=g_d_skip[:, :cfg.HS], ssm_norm_g=g_ssm_norm,
              mix_post_g=g_mix_post, ffn_pre_g=g_ffn_pre, ffn_conv_w=g_ffn_conv_w, ffn_conv_b=g_ffn_conv_b,
              ffn_post_g=g_ffn_post)
    return loss, grad_x, gW, gs


def _to_kernel_layout(cfg, name, w):
    if name == "w_in":
        parts, at = [], 0
        for off, width, n_off, n_width in sorted(cfg.seg.values()):
            parts += [jnp.zeros((off - at, w.shape[1]), w.dtype), w[n_off:n_off + n_width],
                      jnp.zeros((width - n_width, w.shape[1]), w.dtype)]
            at = off + width
        parts.append(jnp.zeros((cfg.EXT - at, w.shape[1]), w.dtype))
        return jnp.concatenate([p for p in parts if p.shape[0]], axis=0)
    if name in ("w_uq", "w_ukv"):
        per = NOPE + (ROPE if name == "w_uq" else VH)
        return jnp.concatenate([w[:, h * per:h * per + NOPE] for h in range(cfg.H)]
                               + [w[:, h * per + NOPE:(h + 1) * per] for h in range(cfg.H)], axis=1)
    return w


def _from_kernel_layout(cfg, name, g):
    if name == "w_in":
        return jnp.concatenate([g[off:off + n_width] for off, _, _, n_width in sorted(cfg.seg.values(), key=lambda s: s[2])], axis=0)
    if name in ("w_uq", "w_ukv"):
        second = ROPE if name == "w_uq" else VH
        base = cfg.H * NOPE
        parts = []
        for h in range(cfg.H):
            parts += [g[:, h * NOPE:(h + 1) * NOPE], g[:, base + h * second:base + (h + 1) * second]]
        return jnp.concatenate(parts, axis=1)
    return g


_CHIP_MAJOR = ("w_gate", "w_up")
_RELAYOUT = ("w_uq", "w_ukv")
_LAYOUT_ROWS = 256
_CONV_COLS = 256


def _w_in_layout(cfg, wg):
    _, rs, d = wg.shape
    tc = _pick(d, _LAYOUT_ROWS, LANE)

    def body(w_ref, o_ref):
        o_ref[...] = _to_kernel_layout(cfg, "w_in", jnp.concatenate([w_ref[k] for k in range(N_CHIPS)], axis=0))

    return pl.pallas_call(
        body, name="layout_w_in", grid=(d // tc,),
        in_specs=[pl.BlockSpec((N_CHIPS, rs, tc), lambda j: (0, 0, j))], out_specs=pl.BlockSpec((cfg.EXT, tc), lambda j: (0, j)),
        out_shape=jax.ShapeDtypeStruct((cfg.EXT, d), wg.dtype), compiler_params=_params(("parallel",)),
    )(wg)


def _w_in_grad_to_chips(cfg, g):
    _, d = g.shape
    rs = cfg.IN_COLS // N_CHIPS
    tc = _pick(d, _LAYOUT_ROWS, LANE)

    def body(g_ref, o_ref):
        nat = _from_kernel_layout(cfg, "w_in", g_ref[...])
        for k in range(N_CHIPS):
            o_ref[k] = nat[k * rs:(k + 1) * rs]

    return pl.pallas_call(
        body, name="layout_grad_w_in", grid=(d // tc,),
        in_specs=[pl.BlockSpec((cfg.EXT, tc), lambda j: (0, j))], out_specs=pl.BlockSpec((N_CHIPS, rs, tc), lambda j: (0, 0, j)),
        out_shape=jax.ShapeDtypeStruct((N_CHIPS, rs, d), g.dtype), compiler_params=_params(("parallel",)),
    )(g)


def _gathered_to_kernel(cfg, name, wg):
    if name in _CHIP_MAJOR:
        return wg
    if name == "w_in":
        return _w_in_layout(cfg, wg)
    if name not in _RELAYOUT:
        return wg.reshape(wg.shape[0] * wg.shape[1], wg.shape[2])
    _, rows, cs = wg.shape
    tr = _pick(rows, _LAYOUT_ROWS, 16)

    def body(w_ref, o_ref):
        o_ref[...] = _to_kernel_layout(cfg, name, jnp.concatenate([w_ref[k] for k in range(N_CHIPS)], axis=1))

    wide = jax.eval_shape(lambda w: _to_kernel_layout(cfg, name, w), jax.ShapeDtypeStruct((rows, N_CHIPS * cs), wg.dtype)).shape[1]
    return pl.pallas_call(
        body, name="layout_" + name, grid=(rows // tr,),
        in_specs=[pl.BlockSpec((N_CHIPS, tr, cs), lambda i: (0, i, 0))], out_specs=pl.BlockSpec((tr, wide), lambda i: (i, 0)),
        out_shape=jax.ShapeDtypeStruct((rows, wide), wg.dtype), compiler_params=_params(("parallel",)),
    )(wg)


def _grad_to_chips(cfg, name, g):
    if name in _CHIP_MAJOR:
        return g
    if name == "w_in":
        return _w_in_grad_to_chips(cfg, g)
    if name not in _RELAYOUT:
        return g.reshape(N_CHIPS, g.shape[0] // N_CHIPS, g.shape[1])
    rows, wide = g.shape
    tr = _pick(rows, _LAYOUT_ROWS, 16)
    cs = jax.eval_shape(lambda v: _from_kernel_layout(cfg, name, v), g).shape[1] // N_CHIPS

    def body(g_ref, o_ref):
        nat = _from_kernel_layout(cfg, name, g_ref[...])
        for k in range(N_CHIPS):
            o_ref[k] = nat[:, k * cs:(k + 1) * cs]

    return pl.pallas_call(
        body, name="layout_grad_" + name, grid=(rows // tr,),
        in_specs=[pl.BlockSpec((tr, wide), lambda i: (i, 0))], out_specs=pl.BlockSpec((N_CHIPS, tr, cs), lambda i: (0, i, 0)),
        out_shape=jax.ShapeDtypeStruct((N_CHIPS, rows, cs), g.dtype), compiler_params=_params(("parallel",)),
    )(g)


def _me():
    return lax.axis_index("x"), lax.axis_index("y"), lax.axis_index("c")


def _other_chips(x, y):
    return [(1 - x, y), (x, 1 - y), (1 - x, 1 - y)]


_ANY = pl.BlockSpec(memory_space=pl.ANY)


BLOCK_ELEMS = 1 << 19
BLOCK_ELEMS_FEW = 1 << 20


def _row_block(rows, cols, mult, elems=BLOCK_ELEMS):
    return _pick(rows, max(mult, elems // cols // mult * mult), mult)


def _scalar(v):
    return v.astype(I32).reshape(1)


def _blocks2d(r, c, mult, elems=BLOCK_ELEMS):
    if r % mult == 0:
        tr = _row_block(r, c, mult, elems)
        return (tr, c), r // tr, lambda i: (i, 0)
    tc = _pick(c, max(LANE, elems // r // LANE * LANE), LANE)
    return (r, tc), c // tc, lambda i: (0, i)


def _by_rows(rows):
    return rows % 32 == 0


def _half_shape(rows, cols):
    return (rows // 2, cols) if _by_rows(rows) else (rows, cols // 2)


def _half_blocks(rows, cols, mult, elems=BLOCK_ELEMS):
    hr, hc = _half_shape(rows, cols)
    block, n, part = _blocks2d(hr, hc, mult, elems)
    assert (hr % mult == 0) == _by_rows(rows), (rows, cols, mult)
    full = (lambda h, i: (h * n + i, 0)) if _by_rows(rows) else (lambda h, i: (0, h * n + i))
    return block, n, full, part


def _half(ref, k, half):
    hr, hc = _half_shape(ref.shape[1], ref.shape[2])
    if _by_rows(ref.shape[1]):
        return ref.at[k, pl.ds(pl.multiple_of(half * hr, 16), hr), :]
    return ref.at[k, :, pl.ds(pl.multiple_of(half * hc, LANE), hc)]


def _shard_blocks(w, br, bc):
    if w.shape[0] == 1:
        def write(ref, v):
            ref[...] = v
        return (lambda f: pl.BlockSpec((None, br, bc), lambda *a: (0, *f(*a)))), (lambda ref: ref[...]), write
    assert w.shape[1] == 1 and br == w.shape[0], w.shape

    def write_rows(ref, v):
        ref[:, 0, :] = v
    return (lambda f: pl.BlockSpec((br, 1, bc), lambda *a: (0, 0, f(*a)[1]))), (lambda ref: ref[:, 0, :]), write_rows


def _stage_shard(name, w, chip, after=None):
    rs, cs = w.shape[0] * w.shape[1], w.shape[2]
    (br, bc), n, idx = _blocks2d(rs, cs, 16, BLOCK_ELEMS_FEW)
    spec, get, _ = _shard_blocks(w, br, bc)

    def body(chip_ref, w_ref, *refs):
        refs[-1][...] = get(w_ref).astype(BF16)

    return pl.pallas_call(
        body, name="stage_" + name,
        grid_spec=pltpu.PrefetchScalarGridSpec(
            num_scalar_prefetch=1, grid=(n,),
            in_specs=[spec(lambda i, chip_ref: idx(i))] + ([] if after is None else [_ANY]),
            out_specs=pl.BlockSpec((None, br, bc), lambda i, chip_ref: (chip_ref[0], *idx(i)))),
        out_shape=jax.ShapeDtypeStruct((N_CHIPS, rs, cs), BF16),
        compiler_params=_params(("parallel",)),
    )(_scalar(chip), w, *([] if after is None else [after]))


_HBM = pl.BlockSpec(memory_space=pltpu.HBM)
_SEM = pl.BlockSpec(memory_space=pltpu.SEMAPHORE)
_EFFECT = pltpu.SideEffectType.DATAFLOW_SIDE_EFFECTING


def _split_starts(name, groups, n_copies, copies, after):
    sizes = [len(g) for g in groups]
    bufs = [b for g in groups for b in g]
    n, k = len(bufs), len(groups)
    starts = [sum(sizes[:j]) for j in range(k)]

    def body(*refs):
        sems = refs[n + 1:n + 1 + 2 * k]
        for j, (at, size) in enumerate(zip(starts, sizes)):
            for cp in copies(refs[at:at + size], sems[2 * j], sems[2 * j + 1]):
                cp.start()
        refs[-1][...] = jnp.zeros_like(refs[-1])

    res = pl.pallas_call(
        body, name=name,
        out_shape=(*[pltpu.SemaphoreType.DMA((c,)) for c in n_copies for _ in range(2)],
                   *[pltpu.HBM(b.shape, b.dtype) for b in bufs], jax.ShapeDtypeStruct((8, LANE), F32)),
        in_specs=[_HBM] * n + [_ANY], out_specs=(*[_SEM] * (2 * k), *[_HBM] * n, pl.BlockSpec(memory_space=pltpu.VMEM)),
        input_output_aliases={i: 2 * k + i for i in range(n)},
        compiler_params=pltpu.CompilerParams(has_side_effects=_EFFECT),
    )(*[pltpu.with_memory_space_constraint(b, pltpu.HBM) for b in bufs], after)
    return [(res[2 * j], res[2 * j + 1], list(res[2 * k + at:2 * k + at + size]))
            for j, (at, size) in enumerate(zip(starts, sizes))], res[-1]


def _split_start(name, bufs, n_copies, copies, after):
    [(send_sems, recv_sems, bufs)], token = _split_starts(name, [bufs], [n_copies], copies, after)
    return send_sems, recv_sems, bufs, token


def _split_wait(name, send_sems, recv_sems, bufs, after, copies):
    n = len(bufs)

    def body(*refs):
        for cp in copies(refs[:n], refs[n], refs[n + 1]):
            cp.wait_send()
            cp.wait_recv()

    return list(pl.pallas_call(
        body, name=name, out_shape=[pltpu.HBM(b.shape, b.dtype) for b in bufs],
        in_specs=[_HBM] * n + [_SEM, _SEM, _ANY], out_specs=[_HBM] * n,
        input_output_aliases={i: i for i in range(n)},
        compiler_params=pltpu.CompilerParams(has_side_effects=_EFFECT),
    )(*bufs, send_sems, recv_sems, after))


def _gather_to_chips(bufs, send_sems, recv_sems):
    x, y, c = _me()
    return [pltpu.make_async_remote_copy(src_ref=_half(b, 2 * x + y, c), dst_ref=_half(b, 2 * x + y, c),
                                         send_sem=send_sems.at[3 * w + j], recv_sem=recv_sems.at[3 * w + j],
                                         device_id=(cx, cy, c), device_id_type=MESH_ID)
            for w, b in enumerate(bufs) for j, (cx, cy) in enumerate(_other_chips(x, y))]


def _gather_to_sibling(bufs, send_sems, recv_sems):
    x, y, c = _me()
    return [pltpu.make_async_remote_copy(src_ref=_half(b, 2 * cx + cy, c), dst_ref=_half(b, 2 * cx + cy, c),
                                         send_sem=send_sems.at[3 * w + j], recv_sem=recv_sems.at[3 * w + j],
                                         device_id=(x, y, 1 - c), device_id_type=MESH_ID)
            for w, b in enumerate(bufs) for j, (cx, cy) in enumerate(_other_chips(x, y))]


def _pair_copies(grads, lands, send_sems, recv_sems):
    x, y, c = _me()
    return [pltpu.make_async_remote_copy(src_ref=_half(g_ref, slice(None), 1 - c), dst_ref=l_ref, send_sem=send_sems.at[w],
                                         recv_sem=recv_sems.at[w], device_id=(x, y, 1 - c), device_id_type=MESH_ID)
            for w, (g_ref, l_ref) in enumerate(zip(grads, lands))]


def _pair_exchange_start(name, grads):
    n = len(grads)
    lands = [lax.empty((g.shape[0], *_half_shape(g.shape[1], g.shape[2])), g.dtype) for g in grads]
    send_sems, recv_sems, bufs, token = _split_start(
        "pair_exchange_start_" + name, [*grads, *lands], n, lambda refs, ss, rs: _pair_copies(refs[:n], refs[n:], ss, rs),
        jnp.zeros((8, LANE), F32))
    return (send_sems, recv_sems, bufs), token


def _pair_exchange_wait(name, state, after):
    send_sems, recv_sems, bufs = state
    n = len(bufs) // 2
    bufs = _split_wait("pair_exchange_wait_" + name, send_sems, recv_sems, bufs, after,
                       lambda refs, ss, rs: _pair_copies(refs[:n], refs[n:], ss, rs))
    return bufs[:n], bufs[n:]


def _pair_sum(name, g, theirs, c):
    (br, bc), nb, full, part = _half_blocks(g.shape[1], g.shape[2], 16, 2 * BLOCK_ELEMS_FEW)

    def body(c_ref, a_ref, b_ref, o_ref):
        o_ref[...] = (a_ref[...].astype(F32) + b_ref[...].astype(F32)).astype(o_ref.dtype)

    return pl.pallas_call(
        body, name="pair_sum_" + name,
        grid_spec=pltpu.PrefetchScalarGridSpec(
            num_scalar_prefetch=1, grid=(N_CHIPS, nb),
            in_specs=[pl.BlockSpec((None, br, bc), lambda k, i, c_ref: (k, *full(c_ref[0], i))),
                      pl.BlockSpec((None, br, bc), lambda k, i, c_ref: (k, *part(i)))],
            out_specs=pl.BlockSpec((None, br, bc), lambda k, i, c_ref: (k, *part(i)))),
        out_shape=jax.ShapeDtypeStruct(theirs.shape, BF16),
        compiler_params=_params(("parallel", "parallel")),
    )(_scalar(c), g, theirs)


def _chip_copies(srcs, lands, send_sems, recv_sems):
    x, y, c = _me()
    return [pltpu.make_async_remote_copy(src_ref=s_ref.at[2 * cx + cy], dst_ref=l_ref.at[j], send_sem=send_sems.at[3 * w + j],
                                         recv_sem=recv_sems.at[3 * w + j], device_id=(cx, cy, c), device_id_type=MESH_ID)
            for w, (s_ref, l_ref) in enumerate(zip(srcs, lands)) for j, (cx, cy) in enumerate(_other_chips(x, y))]


def _chip_exchange_start(name, sums):
    n = len(sums)
    lands = [lax.empty((3,) + s.shape[1:], s.dtype) for s in sums]
    send_sems, recv_sems, bufs, token = _split_start(
        "chip_exchange_start_" + name, [*sums, *lands], 3 * n, lambda refs, ss, rs: _chip_copies(refs[:n], refs[n:], ss, rs),
        jnp.zeros((8, LANE), F32))
    return send_sems, recv_sems, bufs[:n], bufs[n:], token


def _chip_exchange_wait(name, send_sems, recv_sems, sums, lands, after):
    n = len(sums)
    bufs = _split_wait("chip_exchange_wait_" + name, send_sems, recv_sems, [*sums, *lands], after,
                       lambda refs, ss, rs: _chip_copies(refs[:n], refs[n:], ss, rs))
    return bufs[:n], bufs[n:]


def _chip_sum(name, sums, theirs, chip):
    _, h, cs = sums.shape
    (br, bc), nb, idx = _blocks2d(h, cs, 16, BLOCK_ELEMS_FEW)

    def body(chip_ref, s_ref, t_ref, o_ref):
        acc = s_ref[...].astype(F32)
        for k in range(3):
            acc = acc + t_ref[k].astype(F32)
        o_ref[...] = acc

    return pl.pallas_call(
        body, name="chip_sum_" + name,
        grid_spec=pltpu.PrefetchScalarGridSpec(
            num_scalar_prefetch=1, grid=(nb,),
            in_specs=[pl.BlockSpec((None, br, bc), lambda i, chip_ref: (chip_ref[0], *idx(i))),
                      pl.BlockSpec((3, br, bc), lambda i, chip_ref: (0, *idx(i)))],
            out_specs=pl.BlockSpec((br, bc), lambda i, chip_ref: idx(i))),
        out_shape=jax.ShapeDtypeStruct((h, cs), F32),
        compiler_params=_params(("parallel",)),
    )(_scalar(chip), sums, theirs)


def _sibling_copies(halves, lands, send_sems, recv_sems):
    x, y, c = _me()
    return [pltpu.make_async_remote_copy(src_ref=h_ref, dst_ref=l_ref, send_sem=send_sems.at[w], recv_sem=recv_sems.at[w],
                                         device_id=(x, y, 1 - c), device_id_type=MESH_ID)
            for w, (h_ref, l_ref) in enumerate(zip(halves, lands))]


def _sibling_exchange_start(name, halves, after):
    n = len(halves)
    lands = [lax.empty(h.shape, h.dtype) for h in halves]
    send_sems, recv_sems, bufs, token = _split_start(
        "sibling_exchange_start_" + name, [*halves, *lands], n, lambda refs, ss, rs: _sibling_copies(refs[:n], refs[n:], ss, rs),
        after)
    return (send_sems, recv_sems, bufs), token


def _sibling_exchange_wait(name, state, after):
    send_sems, recv_sems, bufs = state
    n = len(bufs) // 2
    bufs = _split_wait("sibling_exchange_wait_" + name, send_sems, recv_sems, bufs, after,
                       lambda refs, ss, rs: _sibling_copies(refs[:n], refs[n:], ss, rs))
    return bufs[:n], bufs[n:]


def _sibling_exchange(name, halves):
    n = len(halves)

    def body(*refs):
        ins, outs, send_sems, recv_sems = refs[:n], refs[n:2 * n], refs[2 * n], refs[2 * n + 1]
        x, y, c = _me()
        cps = []
        for w, (h_ref, o_ref) in enumerate(zip(ins, outs)):
            cps.append(pltpu.make_async_remote_copy(src_ref=h_ref, dst_ref=o_ref, send_sem=send_sems.at[w], recv_sem=recv_sems.at[w],
                                                    device_id=(x, y, 1 - c), device_id_type=MESH_ID))
            cps[-1].start()
        for cp in cps:
            cp.wait()

    return pl.pallas_call(
        body, name="sibling_exchange_" + name, in_specs=[_ANY] * n, out_specs=[_ANY] * n,
        out_shape=[jax.ShapeDtypeStruct(h.shape, h.dtype) for h in halves],
        scratch_shapes=[pltpu.SemaphoreType.DMA((n,)), pltpu.SemaphoreType.DMA((n,))],
    )(*halves)


N_DEV = 8


def _peer_copies(bufs, send_sems, recv_sems):
    vec, land = bufs
    x, y, c = _me()
    return [pltpu.make_async_remote_copy(src_ref=vec, dst_ref=land.at[4 * x + 2 * y + c], send_sem=send_sems.at[p - 1],
                                         recv_sem=recv_sems.at[p - 1], device_id=(x ^ (p >> 2), y ^ ((p >> 1) & 1), c ^ (p & 1)),
                                         device_id_type=MESH_ID) for p in range(1, N_DEV)]


def _allreduce_small_start(vec, after):
    land = jnp.zeros((N_DEV,) + vec.shape, F32)
    send_sems, recv_sems, bufs, _ = _split_start("allreduce_small_start", [vec, land], N_DEV - 1, _peer_copies, after)
    return send_sems, recv_sems, bufs


def _allreduce_small_wait(state, chip, core, after):
    send_sems, recv_sems, bufs = state
    vec, land = _split_wait("allreduce_small_wait", send_sems, recv_sems, bufs, after, _peer_copies)

    def body(me_ref, v_ref, l_ref, o_ref):
        acc = None
        for k in range(N_DEV):
            term = jnp.where(me_ref[0] == k, v_ref[...], l_ref[k])
            acc = term if acc is None else acc + term
        o_ref[...] = acc

    return pl.pallas_call(
        body, name="allreduce_small_sum",
        grid_spec=pltpu.PrefetchScalarGridSpec(
            num_scalar_prefetch=1, grid=(1,),
            in_specs=[pl.BlockSpec(vec.shape, lambda i, me_ref: (0, 0)), pl.BlockSpec(land.shape, lambda i, me_ref: (0, 0, 0))],
            out_specs=pl.BlockSpec(vec.shape, lambda i, me_ref: (0, 0))),
        out_shape=jax.ShapeDtypeStruct(vec.shape, F32), compiler_params=_params(("arbitrary",)),
    )(_scalar(2 * chip + core), vec, land)


def _adam_math(w, g, m, v):
    m = ADAM_B1 * m + (1.0 - ADAM_B1) * g
    v = ADAM_B2 * v + (1.0 - ADAM_B2) * (g * g)
    m_hat = m / (1.0 - ADAM_B1 ** ADAM_STEP)
    v_hat = v / (1.0 - ADAM_B2 ** ADAM_STEP)
    return -ADAM_LR * (m_hat / (jnp.sqrt(v_hat) + ADAM_EPS) + ADAM_WD * w), m, v


def _adamw(name, w, g, m, v):
    R, C = w.shape
    tr = _row_block(R, C, 8)

    def body(w_ref, g_ref, m_ref, v_ref, d_ref, nm_ref, nv_ref):
        d_ref[...], nm_ref[...], nv_ref[...] = _adam_math(w_ref[...], g_ref[...], m_ref[...], v_ref[...])

    blk = pl.BlockSpec((tr, C), lambda i: (i, 0))
    return pl.pallas_call(
        body, name=name, grid=(R // tr,), in_specs=[blk] * 4, out_specs=[blk] * 3,
        out_shape=[jax.ShapeDtypeStruct((R, C), F32)] * 3, compiler_params=_params(("parallel",)),
    )(w, g, m, v)


def _adamw_halves(name, w, mine, theirs, m, v, c):
    rs, cs = w.shape[0] * w.shape[1], w.shape[2]
    (br, bc), nb, whole, half = _half_blocks(rs, cs, 8)
    spec, get, put = _shard_blocks(w, br, bc)

    def body(c_ref, w_ref, a_ref, b_ref, m_ref, v_ref, g_ref, d_ref, nm_ref, nv_ref):
        g = jnp.where(pl.program_id(0) == c_ref[0], a_ref[...], b_ref[...])
        put(g_ref, g)
        for ref, val in zip((d_ref, nm_ref, nv_ref), _adam_math(get(w_ref), g, get(m_ref), get(v_ref))):
            put(ref, val)

    full = spec(lambda s, i, c_ref: whole(s, i))
    part = pl.BlockSpec((br, bc), lambda s, i, c_ref: half(i))
    return pl.pallas_call(
        body, name=name,
        grid_spec=pltpu.PrefetchScalarGridSpec(num_scalar_prefetch=1, grid=(2, nb), in_specs=[full, part, part, full, full],
                                               out_specs=[full] * 4),
        out_shape=[jax.ShapeDtypeStruct(w.shape, F32)] * 4, compiler_params=_params(("parallel", "parallel")),
    )(_scalar(c), w, mine, theirs, m, v)


def _pack_small(arrs, lanes=LANE):
    flat = jnp.concatenate([a.reshape(-1) for a in arrs])
    n = -(-flat.shape[0] // (8 * lanes)) * 8 * lanes
    return jnp.pad(flat, (0, n - flat.shape[0])).reshape(8, n // 8)


def _unpack_small(vec, shapes):
    flat, out, off = vec.reshape(-1), [], 0
    for s in shapes:
        out.append(flat[off:off + s[0] * s[1]].reshape(s))
        off += s[0] * s[1]
    return out


class _LateWeights:
    def __init__(self, cfg, tag, names, started, token):
        self.cfg, self.tag, self.names, self.k, self.token = cfg, tag, names, 3 * len(names), token
        self.send, self.recv, self.bufs = started

    @staticmethod
    def start(cfg, groups, staged, after):
        started, token = _split_starts("gather_" + "_".join(groups) + "_chips_start", [[staged[n] for n in names] for names in groups.values()],
                                       [3 * len(names) for names in groups.values()], _gather_to_chips, after)
        return [_LateWeights(cfg, tag, names, s, token) for (tag, names), s in zip(groups.items(), started)]

    def pass_on(self, after):
        bufs = _split_wait(f"gather_{self.tag}_chips_wait", self.send, self.recv, self.bufs, after, _gather_to_chips)
        self.send, self.recv, self.bufs, token = _split_start(f"gather_{self.tag}_sibling_start", bufs, self.k, _gather_to_sibling,
                                                               self.token)
        return token

    def arrived(self, after):
        bufs = _split_wait(f"gather_{self.tag}_sibling_wait", self.send, self.recv, self.bufs, after, _gather_to_sibling)
        return {n: _gathered_to_kernel(self.cfg, n, b) for n, b in zip(self.names, bufs)}


def _step(cfg, a):
    chip = 2 * lax.axis_index("x") + lax.axis_index("y")
    core = lax.axis_index("c")
    big = BIG

    ffn = ("w_gate", "w_up", "w_down")
    first = ("w_in", "w_uq", "w_ukv")
    sp = {n: a[n] for n in SMALL}
    sharded = _pack_small([a[n] for n in SMALL_SHARDED], 2 * LANE)
    slabs = jnp.where(lax.broadcasted_iota(I32, (N_CHIPS,) + sharded.shape, 0) == chip, sharded[None], 0.0)
    staged = {"w_in": _stage_shard("w_in", a["w_in"], chip), "sharded_small": slabs}
    [in_weight] = _LateWeights.start(cfg, {"in": ("w_in", "sharded_small")}, staged, jnp.zeros((8, LANE), F32))
    behind = xn_early = _rms_pre(cfg, a["x"], sp["mix_pre_g"] + in_weight.token[0, 0])
    for n in big[1:]:
        behind = staged[n] = _stage_shard(n, a[n], chip, behind)
    mla_weights, out_weight, ffn_weights, down_weight = _LateWeights.start(
        cfg, {"mla": first[1:], "out": ("w_out",), "ffn": ffn[:2], "down": ffn[2:]}, staged, in_weight.pass_on(behind))
    W = in_weight.arrived(down_weight.token)
    allp = W.pop("sharded_small").reshape((N_CHIPS,) + sharded.shape)
    per_chip = [_unpack_small(allp[ch], [a[n].shape for n in SMALL_SHARDED]) for ch in range(N_CHIPS)]
    for k, n in enumerate(SMALL_SHARDED):
        sp[n] = jnp.concatenate([per_chip[ch][k] for ch in range(N_CHIPS)], axis=1)

    state = {}

    def ffn_grads_ready(grads):
        state["ffn_pairs"], token = _pair_exchange_start("ffn", [_grad_to_chips(cfg, n, grads[n]) for n in ffn_grads])
        return token

    def pair_sums(names, grads, theirs):
        return [_pair_sum(n, g, t, core) for n, g, t in zip(names, grads, theirs)]

    def early_grads_ready(grads):
        out_pairs, token = _pair_exchange_start("out", [_grad_to_chips(cfg, "w_out", grads["w_out"])])
        sums = pair_sums(ffn_grads, *_pair_exchange_wait("ffn", state["ffn_pairs"], token))
        sums += pair_sums(["w_out"], *_pair_exchange_wait("out", out_pairs, sums[-1]))
        state["early"] = _chip_exchange_start("early", sums)
        return state["early"][-1]

    def reduced_halves(tag, names, after):
        send_sems, recv_sems, s_bufs, l_bufs, _ = state[tag]
        s_bufs, l_bufs = _chip_exchange_wait(tag, send_sems, recv_sems, s_bufs, l_bufs, after)
        return [_chip_sum(n, s, t, chip) for n, s, t in zip(names, s_bufs, l_bufs)]

    def in_grad_ready(grads, after):
        if grads is not None:
            state["rest_pairs"], token = _pair_exchange_start("rest", [_grad_to_chips(cfg, n, grads[n]) for n in first])
            return token
        state["rest"] = _chip_exchange_start("rest", pair_sums(first, *_pair_exchange_wait("rest", state["rest_pairs"], after)))
        return state["rest"][-1]

    ffn_grads = ("w_down", "w_gate", "w_up")
    early = ffn_grads + ("w_out",)
    loss, grad_x, gW, gs = _local_grads(cfg, a["x"], a["loss_target"], W, sp, mla_weights, out_weight, ffn_weights, down_weight,
                                        ffn_grads_ready, early_grads_ready, in_grad_ready, xn_early, down_weight.token)
    out = {"grad_x": grad_x}

    def adamw(names, mine, theirs):
        for n, gm, gt in zip(names, mine, theirs):
            out["grad_" + n], out["delta_" + n], out["new_m_" + n], out["new_v_" + n] = _adamw_halves(
                "adamw_" + n, a[n], gm, gt, a["m_" + n], a["v_" + n], core)

    mine = reduced_halves("early", early, grad_x)
    theirs = _sibling_exchange("early", mine[:1])
    later, _ = _sibling_exchange_start("early", mine[1:], theirs[0])
    adamw(early[:1], mine[:1], theirs)
    e_mine, e_theirs = _sibling_exchange_wait("early", later, out["new_v_" + early[0]])
    adamw(early[3:], e_mine[2:], e_theirs[2:])
    mine = reduced_halves("rest", first, out["new_v_" + early[-1]])
    rest, token = _sibling_exchange_start("rest", mine, mine[0])
    small = _allreduce_small_start(_pack_small([gs[n] for n in SMALL] + [loss]), token)
    adamw(early[1:3], e_mine[:2], e_theirs[:2])
    adamw(first, *_sibling_exchange_wait("rest", rest, out["new_v_" + early[2]]))
    shapes = [gs[n].shape for n in SMALL] + [(1, LANE)]
    red = _unpack_small(_allreduce_small_wait(small, chip, core, out["new_v_" + first[-1]]), shapes)
    g_small = dict(zip(SMALL, red[:-1]))
    for n in SMALL_SHARDED:
        cs = a[n].shape[1]
        g_small[n] = lax.dynamic_slice_in_dim(g_small[n], chip * cs, cs, axis=1)
    out["loss"] = red[-1][0, 0]
    sshapes = [a[n].shape for n in SMALL]
    d, nm, nv = _adamw("adamw_small", _pack_small([a[n] for n in SMALL]), _pack_small([g_small[n] for n in SMALL]),
                       _pack_small([a["m_" + n] for n in SMALL]), _pack_small([a["v_" + n] for n in SMALL]))
    for n, dd, mm, vv in zip(SMALL, _unpack_small(d, sshapes), _unpack_small(nm, sshapes), _unpack_small(nv, sshapes)):
        out["grad_" + n], out["delta_" + n], out["new_m_" + n], out["new_v_" + n] = g_small[n], dd, mm, vv
    return out


def kernel(x, mix_pre_g, w_in, q_norm_g, w_uq, kv_norm_g, w_ukv, ssm_conv_w, ssm_conv_b, dt_bias, a_log, d_skip, ssm_norm_g, w_out, mix_post_g, ffn_pre_g, w_gate, w_up, ffn_conv_w, ffn_conv_b, w_down, ffn_post_g, loss_target, m_mix_pre_g, m_w_in, m_q_norm_g, m_w_uq, m_kv_norm_g, m_w_ukv, m_ssm_conv_w, m_ssm_conv_b, m_dt_bias, m_a_log, m_d_skip, m_ssm_norm_g, m_w_out, m_mix_post_g, m_ffn_pre_g, m_w_gate, m_w_up, m_ffn_conv_w, m_ffn_conv_b, m_w_down, m_ffn_post_g, v_mix_pre_g, v_w_in, v_q_norm_g, v_w_uq, v_kv_norm_g, v_w_ukv, v_ssm_conv_w, v_ssm_conv_b, v_dt_bias, v_a_log, v_d_skip, v_ssm_norm_g, v_w_out, v_mix_post_g, v_ffn_pre_g, v_w_gate, v_w_up, v_ffn_conv_w, v_ffn_conv_b, v_w_down, v_ffn_post_g):
    args = dict(locals())
    def given(k, v):
        if k in ("w_in", "m_w_in", "v_w_in"):
            return jnp.transpose(v, (2, 0, 1))
        return v if k.removeprefix("m_").removeprefix("v_") in BIG or v.ndim < 3 else v[0]

    out = _step(_FULL, {k: given(k, v) for k, v in args.items()})
    res = [out["loss"], out["grad_x"][None]]
    for pre in ("grad_", "delta_", "new_m_", "new_v_"):
        for n in WEIGHTS:
            o = out[pre + n]
            res.append(jnp.transpose(o, (1, 2, 0)) if n == "w_in" else o if n in BIG or args[n].ndim < 3 else o[None])
    return tuple(res)
```

```python
import math

import jax
import jax.numpy as jnp
from jax import lax
from jax.experimental import pallas as pl
from jax.experimental.pallas import tpu as pltpu

F32, BF16, I32 = jnp.float32, jnp.bfloat16, jnp.int32
NN = (((1,), (0,)), ((), ()))
NT = (((1,), (1,)), ((), ()))
TN = (((0,), (0,)), ((), ()))
HI = lax.Precision.HIGHEST
MESH_ID = pl.DeviceIdType.MESH

EPS = 1e-6
CHUNK = 64
NOPE, ROPE, VH = 128, 64, 128
ROPE_THETA = 10000.0
HP, NST = 64, 128
SSM_K, FFN_K = 4, 3
LANE = 128
N_CHIPS = 4
VMEM_LIMIT = 52 * 1024 * 1024
MM_TILE, MM_TILE_K = 1408, 2816

ADAM_LR, ADAM_B1, ADAM_B2, ADAM_EPS, ADAM_WD, ADAM_STEP = 0.001, 0.9, 0.999, 1e-08, 0.01, 10


class _Cfg:
    def __init__(self, S, D, QL, KVL, H, HS, G, DFF, T):
        self.S, self.D, self.QL, self.KVL, self.H, self.HS, self.G, self.DFF, self.T = S, D, QL, KVL, H, HS, G, DFF, T
        self.INNER = HS * HP
        self.CONVCH = self.INNER + 2 * G * NST
        self.QW = H * (NOPE + ROPE)
        self.KVW = H * (NOPE + VH)
        self.MLAW = H * VH
        self.MIXW = self.MLAW + self.INNER
        self.IN_COLS = QL + KVL + ROPE + self.INNER + self.CONVCH + HS
        natural, at = {}, 0
        for name, w in (("c_q", QL), ("c_kv", KVL), ("kr", ROPE), ("z", self.INNER), ("xbc", self.CONVCH), ("dt", HS)):
            natural[name] = (at, w)
            at += w
        self.seg, taken = {}, []
        for name in sorted(natural, key=lambda n: -natural[n][1]):
            w = -(-natural[name][1] // LANE) * LANE
            off = next(o for o in range(0, self.IN_COLS * 2, w) if all(o + w <= t or o >= t + tw for t, tw in taken))
            taken.append((off, w))
            self.seg[name] = (off, w) + natural[name]
        self.EXT = max(o + w for o, w in taken)
        self.NPAIR = HS // 2
        self.REP = HS // G

    def window(self, name):
        off, w, _, _ = self.seg[name]
        return w, off // w


_FULL = _Cfg(S=2048, D=2048, QL=768, KVL=512, H=8, HS=16, G=2, DFF=5632, T=256)
BIG = ("w_in", "w_uq", "w_ukv", "w_out", "w_gate", "w_up", "w_down")

SMALL = ("mix_pre_g", "q_norm_g", "kv_norm_g", "ssm_conv_w", "ssm_conv_b", "dt_bias", "a_log", "d_skip", "ssm_norm_g",
         "mix_post_g", "ffn_pre_g", "ffn_conv_w", "ffn_conv_b", "ffn_post_g")
SMALL_SHARDED = ("ssm_conv_w", "ffn_conv_w")
WEIGHTS = ("mix_pre_g", "w_in", "q_norm_g", "w_uq", "kv_norm_g", "w_ukv", "ssm_conv_w", "ssm_conv_b", "dt_bias", "a_log",
           "d_skip", "ssm_norm_g", "w_out", "mix_post_g", "ffn_pre_g", "w_gate", "w_up", "ffn_conv_w", "ffn_conv_b",
           "w_down", "ffn_post_g")


def _pick(n, target, mult):
    best = None
    for d in range(mult, min(n, target) + 1, mult):
        if n % d == 0:
            best = d
    return best if best is not None else n


def _params(sem=None):
    kw = dict(vmem_limit_bytes=VMEM_LIMIT)
    if sem is not None:
        kw["dimension_semantics"] = sem
    return pltpu.CompilerParams(**kw)


def _dot(a, b, dims=NN, precision=None):
    return lax.dot_general(a, b, dims, preferred_element_type=F32, precision=precision)


def _sigmoid(x):
    return 1.0 / (1.0 + jnp.exp(-x))


def _rs(x):
    return lax.rsqrt(jnp.mean(x * x, axis=-1, keepdims=True) + EPS)


def _rms_back(xh, r, dn):
    return r * (dn - xh * jnp.mean(dn * xh, axis=-1, keepdims=True))


def _colsum(v):
    return jnp.sum(v, axis=0, keepdims=True)


def _matmul(name, a, b, mode, out_dtype, a2=None, b2=None, chips=False, after=None):
    cs = None
    if mode == "nn":
        (M, K), N = a.shape, b.shape[-1]
        if chips:
            cs, N = N, N_CHIPS * N
    elif mode == "nt":
        (M, K), N = a.shape, b.shape[-2]
        if chips:
            cs = b.shape[-1]
    else:
        (K, M), N = a.shape, b.shape[1]
        if chips:
            cs = N // N_CHIPS
    tm = _pick(M, MM_TILE, LANE)
    tn = _pick(cs if chips and mode != "nt" else N, MM_TILE, LANE)
    tk = _pick(cs, MM_TILE, LANE) if chips and mode == "nt" else _pick(K, MM_TILE_K, LANE)
    nk = K // tk
    dims = {"nn": NN, "nt": NT, "tn": TN}[mode]
    a_spec = pl.BlockSpec((tk, tm), lambda i, j, k: (k, i)) if mode == "tn" else pl.BlockSpec((tm, tk), lambda i, j, k: (i, k))
    b_spec = pl.BlockSpec((tn, tk), lambda i, j, k: (j, k)) if mode == "nt" else pl.BlockSpec((tk, tn), lambda i, j, k: (k, j))
    o_spec = pl.BlockSpec((tm, tn), lambda i, j, k: (i, j))
    o_shape = (M, N)
    if chips and mode == "nn":
        per = cs // tn
        b_spec = pl.BlockSpec((None, tk, tn), lambda i, j, k: (j // per, k, j % per))
    elif chips and mode == "nt":
        per = cs // tk
        b_spec = pl.BlockSpec((None, tn, tk), lambda i, j, k: (k // per, j, k % per))
    elif chips:
        per = cs // tn
        o_spec = pl.BlockSpec((None, tm, tn), lambda i, j, k: (j // per, i, j % per))
        o_shape = (N_CHIPS, M, cs)
    two = a2 is not None

    def product(refs):
        part = _dot(refs[0][...].astype(BF16), refs[1][...].astype(BF16), dims)
        if two:
            part += _dot(refs[2][...].astype(BF16), refs[3][...].astype(BF16), dims)
        return part

    def body_whole_k(*refs):
        refs[-1][...] = product(refs).astype(refs[-1].dtype)

    def body(*refs):
        o_ref, acc_ref = refs[-2], refs[-1]
        k = pl.program_id(2)

        @pl.when(k == 0)
        def _():
            acc_ref[...] = product(refs)

        @pl.when(k > 0)
        def _():
            acc_ref[...] += product(refs)

        @pl.when(k == nk - 1)
        def _():
            o_ref[...] = acc_ref[...].astype(o_ref.dtype)

    ins = ((a, b, a2, b2) if two else (a, b)) + (() if after is None else (after,))
    return pl.pallas_call(
        body_whole_k if nk == 1 else body, name=name, grid=(M // tm, N // tn, nk),
        in_specs=[a_spec, b_spec] * (2 if two else 1) + ([] if after is None else [pl.BlockSpec(memory_space=pl.ANY)]),
        out_specs=o_spec,
        out_shape=jax.ShapeDtypeStruct(o_shape, out_dtype),
        scratch_shapes=[] if nk == 1 else [pltpu.VMEM((tm, tn), F32)],
        compiler_params=_params(("parallel", "parallel", "arbitrary")),
    )(*ins)


def _matmul_twin(name, a, b1, b2, mode, out_dtype):
    if mode == "nn":
        (M, K), cs = a.shape, b1.shape[-1]
        tm = _pick(M, MM_TILE // 2, LANE)
    else:
        (K, M), cs = a.shape, b1.shape[1] // N_CHIPS
        tm = _pick(M, MM_TILE, LANE)
    tn = _pick(cs, MM_TILE, LANE)
    per = cs // tn
    dims = NN if mode == "nn" else TN

    def body(a_ref, b1_ref, b2_ref, o1_ref, o2_ref):
        lhs = a_ref[...].astype(BF16)
        o1_ref[...] = _dot(lhs, b1_ref[...].astype(BF16), dims).astype(o1_ref.dtype)
        o2_ref[...] = _dot(lhs, b2_ref[...].astype(BF16), dims).astype(o2_ref.dtype)

    if mode == "nn":
        a_spec = pl.BlockSpec((tm, K), lambda i, j: (i, 0))
        b_spec = pl.BlockSpec((None, K, tn), lambda i, j: (j // per, 0, j % per))
        o_spec, o_shape = pl.BlockSpec((tm, tn), lambda i, j: (i, j)), (M, N_CHIPS * cs)
    else:
        a_spec = pl.BlockSpec((K, tm), lambda i, j: (0, i))
        b_spec = pl.BlockSpec((K, tn), lambda i, j: (0, j))
        o_spec, o_shape = pl.BlockSpec((None, tm, tn), lambda i, j: (j // per, i, j % per)), (N_CHIPS, M, cs)
    return pl.pallas_call(
        body, name=name, grid=(M // tm, N_CHIPS * per), in_specs=[a_spec, b_spec, b_spec], out_specs=[o_spec, o_spec],
        out_shape=[jax.ShapeDtypeStruct(o_shape, out_dtype)] * 2, compiler_params=_params(("parallel", "parallel")),
    )(a, b1, b2)


def _window(a):
    return (a[0], *a[1]) if isinstance(a, tuple) else (a, a.shape[1], 0)


def _rowwise(name, fn, rows, mats, outs, reds, ts, into=None):
    rows, widths, blocks = zip(*[_window(a) for a in rows])
    S = rows[0].shape[0]
    nr, nm, no = len(rows), len(mats), len(outs)
    ni = nr + nm + (into is not None)

    def body(*refs):
        res = fn(*[r[...] for r in refs[:nr + nm]])
        res = res if isinstance(res, (tuple, list)) else (res,)
        for r, v in zip(refs[ni:ni + no], res[:no]):
            r[...] = v.astype(r.dtype)
        first = pl.program_id(0) == 0
        for r, v in zip(refs[ni + no:], res[no:]):
            @pl.when(first)
            def _():
                r[...] = jnp.broadcast_to(v, r.shape)

            @pl.when(jnp.logical_not(first))
            def _():
                r[...] += jnp.broadcast_to(v, r.shape)

    in_specs = [pl.BlockSpec((ts, w), lambda i, b=b: (i, b)) for w, b in zip(widths, blocks)]
    in_specs += [pl.BlockSpec(m.shape, lambda i, nd=m.ndim: (0,) * nd) for m in mats]
    out_specs = [pl.BlockSpec((ts, w), lambda i: (i, 0)) for w, _ in outs]
    out_specs += [pl.BlockSpec(s, lambda i: (0, 0)) for s in reds]
    out_shape = [jax.ShapeDtypeStruct((S, w), dt) for w, dt in outs] + [jax.ShapeDtypeStruct(s, F32) for s in reds]
    if into is not None:
        in_specs.append(_ANY)
        out_specs[0] = pl.BlockSpec((ts, outs[0][0]), lambda i: (i, into[1]))
        out_shape[0] = jax.ShapeDtypeStruct(into[0].shape, into[0].dtype)
    return pl.pallas_call(
        body, name=name, grid=(S // ts,), in_specs=in_specs, out_specs=out_specs, out_shape=out_shape,
        input_output_aliases={} if into is None else {ni - 1: 0},
        compiler_params=_params(("arbitrary",) if reds else ("parallel",)),
    )(*rows, *mats, *([] if into is None else [into[0]]))


def _shift_down(v, s):
    if s == 0:
        return v
    rows = lax.broadcasted_iota(I32, v.shape, 0)
    return jnp.where(rows >= s, pltpu.roll(v, s, 0), 0.0)


def _shift_up(v, s):
    if s == 0:
        return v
    n = v.shape[0]
    rows = lax.broadcasted_iota(I32, v.shape, 0)
    return jnp.where(rows < n - s, pltpu.roll(v, n - s, 0), 0.0)


def _conv(x, w, b):
    K = w.shape[0]
    y = jnp.broadcast_to(b, x.shape)
    for k in range(K):
        y = y + w[k:k + 1, :] * _shift_down(x, K - 1 - k)
    return y


def _conv_back(x, w, dc):
    K = w.shape[0]
    dx = jnp.zeros_like(x)
    dw = []
    for k in range(K):
        up = _shift_up(dc, K - 1 - k)
        dx = dx + w[k:k + 1, :] * up
        dw.append(_colsum(up * x))
    return dx, jnp.concatenate(dw, axis=0), _colsum(dc)


def _colwise(name, fn, cols, vecs, outs, pouts, tc):
    cols, widths, blocks = zip(*[_window(a) for a in cols])
    S, C = cols[0].shape[0], widths[0]
    firsts = [b * (C // tc) for b in blocks]
    nc_, nv, no = len(cols), len(vecs), len(outs)

    def body(*refs):
        res = fn(*[r[...] for r in refs[:nc_ + nv]])
        res = res if isinstance(res, (tuple, list)) else (res,)
        for r, v in zip(refs[nc_ + nv:], res):
            r[...] = v.astype(r.dtype)

    in_specs = [pl.BlockSpec((S, tc), lambda j, f=f: (0, f + j)) for f in firsts]
    in_specs += [pl.BlockSpec((v.shape[0], tc), lambda j: (0, j)) for v in vecs]
    out_specs = [pl.BlockSpec((S, tc), lambda j: (0, j)) for _ in outs] + [pl.BlockSpec((k, tc), lambda j: (0, j)) for k in pouts]
    out_shape = [jax.ShapeDtypeStruct((S, C), dt) for dt in outs] + [jax.ShapeDtypeStruct((k, C), F32) for k in pouts]
    return pl.pallas_call(
        body, name=name, grid=(C // tc,), in_specs=in_specs, out_specs=out_specs, out_shape=out_shape,
        compiler_params=_params(("parallel",)),
    )(*cols, *vecs)


_G0, _G1 = math.sqrt(2.0 / math.pi), 0.044715


def _gelu(g):
    th = jnp.tanh(_G0 * (g + _G1 * g * g * g))
    return 0.5 * g * (1.0 + th), th


def _ffn_act(gate_pre, up, w, b):
    act, _ = _gelu(_conv(gate_pre, w, b))
    return act * up


def _ffn_act_back(dact, gate_pre, up, w, b):
    g = _conv(gate_pre, w, b)
    ge, th = _gelu(g)
    dge = 0.5 * (1.0 + th) + 0.5 * g * (1.0 - th * th) * _G0 * (1.0 + 3.0 * _G1 * g * g)
    dup = dact * ge
    dgate_pre, dw, db = _conv_back(gate_pre, w, dact * up * dge)
    return dgate_pre, dup, dw, db


def _ssm_act(xbc, w, b):
    c = _conv(xbc, w, b)
    return c * _sigmoid(c)


def _ssm_act_back(dxc, xbc, w, b):
    c = _conv(xbc, w, b)
    sg = _sigmoid(c)
    return _conv_back(xbc, w, dxc * sg * (1.0 + c * (1.0 - sg)))


def _rope_tables(S):
    inv = 1.0 / (ROPE_THETA ** (jnp.arange(0, ROPE, 2, dtype=F32) / ROPE))
    ang = jnp.arange(S, dtype=F32)[:, None] * inv[None, :]
    cos, sin = jnp.cos(ang), jnp.sin(ang)
    return jnp.tile(cos, (1, 4)), jnp.tile(jnp.concatenate([-sin, sin], axis=1), (1, 2))


def _swap_halves(x):
    lane = lax.broadcasted_iota(I32, x.shape, 1)
    w = x.shape[1]
    return jnp.where((lane % ROPE) < ROPE // 2, pltpu.roll(x, w - ROPE // 2, 1), pltpu.roll(x, ROPE // 2, 1))


def _rot(x, cos2, sin2):
    return x * cos2 + _swap_halves(x) * sin2


def _rot_back(dy, cos2, sin2):
    return dy * cos2 + _swap_halves(dy * sin2)


def _mla_pack(cfg, q, kv, kr, cos2, sin2):
    S, H = cfg.S, cfg.H
    ts = _pick(S, 256, 8)
    kr, _, kr_block = _window(kr)

    def body(q_ref, kv_ref, kr_ref, c_ref, s_ref, Q_ref, K_ref, V_ref):
        c2, s2 = c_ref[...], s_ref[...]
        krr = _rot(kr_ref[...], c2, s2)
        kr_half = (krr.astype(BF16), pltpu.roll(krr, ROPE, 1).astype(BF16))
        for j in range(H // 2):
            qr = _rot(q_ref[:, (H + j) * LANE:(H + j + 1) * LANE], c2, s2).astype(BF16)
            for h in (2 * j, 2 * j + 1):
                Q_ref[h, :, 0:LANE] = q_ref[:, h * LANE:(h + 1) * LANE].astype(BF16)
                Q_ref[h, :, LANE:] = qr
                K_ref[h, :, 0:LANE] = kv_ref[:, h * LANE:(h + 1) * LANE].astype(BF16)
                K_ref[h, :, LANE:] = kr_half[h % 2]
                V_ref[h] = kv_ref[:, (H + h) * LANE:(H + h + 1) * LANE].astype(BF16)

    tab = pl.BlockSpec((ts, LANE), lambda i: (i, 0))
    heads = lambda w: pl.BlockSpec((H, ts, w), lambda i: (0, i, 0))
    return pl.pallas_call(
        body, name="mla_pack", grid=(S // ts,),
        in_specs=[pl.BlockSpec((ts, cfg.QW), lambda i: (i, 0)), pl.BlockSpec((ts, cfg.KVW), lambda i: (i, 0)),
                  pl.BlockSpec((ts, LANE), lambda i: (i, kr_block)), tab, tab],
        out_specs=[heads(2 * LANE), heads(2 * LANE), heads(LANE)],
        out_shape=[jax.ShapeDtypeStruct((H, S, 2 * LANE), BF16), jax.ShapeDtypeStruct((H, S, 2 * LANE), BF16),
                   jax.ShapeDtypeStruct((H, S, LANE), BF16)],
        compiler_params=_params(("parallel",)),
    )(q, kv, kr, cos2, sin2)


def _mla_unpack(cfg, dQ, dK, dV, cos2, sin2):
    S, H = cfg.S, cfg.H
    ts = _pick(S, 256, 8)

    def body(dQ_ref, dK_ref, dV_ref, c_ref, s_ref, dq_ref, dkv_ref, dkr_ref):
        c2, s2 = c_ref[...], s_ref[...]
        lo = lax.broadcasted_iota(I32, (ts, LANE), 1) < ROPE
        tk = jnp.zeros((ts, LANE), F32)
        for h in range(H):
            dq_ref[:, h * LANE:(h + 1) * LANE] = dQ_ref[h, :, 0:LANE].astype(BF16)
            dkv_ref[:, h * LANE:(h + 1) * LANE] = dK_ref[h, :, 0:LANE].astype(BF16)
            dkv_ref[:, (H + h) * LANE:(H + h + 1) * LANE] = dV_ref[h].astype(BF16)
            own = lo if h % 2 == 0 else jnp.logical_not(lo)
            tk = tk + jnp.where(own, dK_ref[h, :, LANE:], 0.0)
        for j in range(H // 2):
            dr = dQ_ref[2 * j, :, LANE:] + dQ_ref[2 * j + 1, :, LANE:]
            dq_ref[:, (H + j) * LANE:(H + j + 1) * LANE] = _rot_back(dr, c2, s2).astype(BF16)
        dkr_rot = jnp.where(lo, tk + pltpu.roll(tk, ROPE, 1), 0.0)
        dkr_ref[...] = _rot_back(dkr_rot, c2, s2).astype(BF16)

    tab = pl.BlockSpec((ts, LANE), lambda i: (i, 0))
    return pl.pallas_call(
        body, name="mla_unpack", grid=(S // ts,),
        in_specs=[pl.BlockSpec((H, ts, 2 * LANE), lambda i: (0, i, 0)), pl.BlockSpec((H, ts, 2 * LANE), lambda i: (0, i, 0)),
                  pl.BlockSpec((H, ts, LANE), lambda i: (0, i, 0)), tab, tab],
        out_specs=[pl.BlockSpec((ts, cfg.QW), lambda i: (i, 0)), pl.BlockSpec((ts, cfg.KVW), lambda i: (i, 0)), tab],
        out_shape=[jax.ShapeDtypeStruct((S, cfg.QW), BF16), jax.ShapeDtypeStruct((S, cfg.KVW), BF16),
                   jax.ShapeDtypeStruct((S, LANE), BF16)],
        compiler_params=_params(("parallel",)),
    )(dQ, dK, dV, cos2, sin2)


_ATT_T = 256
_ATT_HB = 8
_ATT_SCALE = (NOPE + ROPE) ** -0.5


def _diag_mask(transposed=False):
    r = lax.broadcasted_iota(I32, (_ATT_T, _ATT_T), 0) // CHUNK
    c = lax.broadcasted_iota(I32, (_ATT_T, _ATT_T), 1) // CHUNK
    return r <= c if transposed else c <= r


def _row_form(col):
    return jnp.broadcast_to(col, (col.shape[0], LANE)).T[0:8, :]


def _attn_fwd(cfg, Q, K, V):
    S, H, T, HB = cfg.S, cfg.H, _ATT_T, min(cfg.H, _ATT_HB)

    def body(q_ref, k_ref, v_ref, o_ref, lse_t_ref, ob_ref):
        qi = pl.program_id(1)

        def head_step(b, kb, carry, mask):
            m, l, acc = carry
            ks = pl.multiple_of(kb * T, T)
            s = _dot(q_ref[b], k_ref[b, pl.ds(ks, T), :], NT) * _ATT_SCALE
            if mask is not None:
                s = jnp.where(mask, s, -1e30)
            m_new = jnp.maximum(m, jnp.max(s, axis=1, keepdims=True))
            p = jnp.exp(s - m_new)
            alpha = jnp.exp(m - m_new)
            l = alpha * l + jnp.sum(p, axis=1, keepdims=True)
            acc = alpha * acc + _dot(p.astype(BF16), v_ref[b, pl.ds(ks, T), :])
            return m_new, l, acc

        def step(kb, carry, mask=None):
            return tuple(head_step(b, kb, carry[b], mask) for b in range(HB))

        init = (jnp.full((T, 1), -1e30, F32), jnp.zeros((T, 1), F32), jnp.zeros((T, VH), F32))
        done = step(qi, lax.fori_loop(0, qi, step, (init,) * HB), _diag_mask())
        for b, (m, l, acc) in enumerate(done):
            o_ref[:, b * LANE:(b + 1) * LANE] = acc / l
            ob_ref[:, b * LANE:(b + 1) * LANE] = (acc / l).astype(BF16)
            lse_t_ref[b] = _row_form(m + jnp.log(l))

    return pl.pallas_call(
        body, name="attn_fwd", grid=(H // HB, S // T),
        in_specs=[pl.BlockSpec((HB, T, 2 * LANE), lambda h, i: (h, i, 0)), pl.BlockSpec((HB, S, 2 * LANE), lambda h, i: (h, 0, 0)),
                  pl.BlockSpec((HB, S, LANE), lambda h, i: (h, 0, 0))],
        out_specs=[pl.BlockSpec((T, HB * LANE), lambda h, i: (i, h)), pl.BlockSpec((HB, 8, T), lambda h, i: (h, 0, i)),
                   pl.BlockSpec((T, HB * LANE), lambda h, i: (i, h))],
        out_shape=[jax.ShapeDtypeStruct((S, H * LANE), F32), jax.ShapeDtypeStruct((H, 8, S), F32),
                   jax.ShapeDtypeStruct((S, cfg.MLAW + cfg.INNER), BF16)],
        compiler_params=_params(("parallel", "parallel")),
    )(Q, K, V)


def _attn_delta(cfg, do, o, after):
    S, H, T = cfg.S, cfg.H, _ATT_T

    def body(do_ref, o_ref, after_ref, dl_t_ref):
        for h in range(H):
            sl = slice(h * LANE, (h + 1) * LANE)
            dl_t_ref[h] = _row_form(jnp.sum(do_ref[:, sl] * o_ref[:, sl], axis=1, keepdims=True))

    wide = pl.BlockSpec((T, H * LANE), lambda i: (i, 0))
    return pl.pallas_call(
        body, name="attn_delta", grid=(S // T,), in_specs=[wide, wide, _ANY],
        out_specs=pl.BlockSpec((H, 8, T), lambda i: (0, 0, i)), out_shape=jax.ShapeDtypeStruct((H, 8, S), F32),
        compiler_params=_params(("parallel",)),
    )(do, o, after)


_ATT_HB_BWD = 4


def _attn_bwd(cfg, Q, K, V, do, lse_t, delta_t):
    S, H, T, HB = cfg.S, cfg.H, _ATT_T, min(cfg.H, _ATT_HB_BWD)
    nq = S // T

    def body(q_ref, k_ref, v_ref, do_ref, lse_ref, dl_ref, dq_ref, dk_ref, dv_ref):
        kb = pl.program_id(1)

        @pl.when(kb == 0)
        def _():
            dq_ref[...] = jnp.zeros_like(dq_ref)

        def head_step(b, qi, carry, mask):
            dk, dv = carry
            qs = pl.multiple_of(qi * T, T)
            q = q_ref[b, pl.ds(qs, T), :]
            k = k_ref[b]
            dob = do_ref[pl.ds(qs, T), b * LANE:(b + 1) * LANE].astype(BF16)
            s = _dot(k, q, NT) * _ATT_SCALE
            if mask is not None:
                s = jnp.where(mask, s, -1e30)
            p = jnp.exp(s - lse_ref[b, 0:1, pl.ds(qs, T)])
            dv = dv + _dot(p.astype(BF16), dob)
            dp = _dot(v_ref[b], dob, NT)
            ds = (p * (dp - dl_ref[b, 0:1, pl.ds(qs, T)]) * _ATT_SCALE).astype(BF16)
            dk = dk + _dot(ds, q)
            dq_ref[b, pl.ds(qs, T), :] += _dot(ds, k, TN)
            return dk, dv

        def step(qi, carry, mask=None):
            return tuple(head_step(b, qi, carry[b], mask) for b in range(HB))

        zero = (jnp.zeros((T, 2 * LANE), F32), jnp.zeros((T, VH), F32))
        done = lax.fori_loop(kb + 1, nq, step, step(kb, (zero,) * HB, _diag_mask(transposed=True)))
        for b, (dk, dv) in enumerate(done):
            dk_ref[b] = dk
            dv_ref[b] = dv

    row = pl.BlockSpec((HB, 8, S), lambda h, j: (h, 0, 0))
    whole = pl.BlockSpec((HB, S, 2 * LANE), lambda h, j: (h, 0, 0))
    return pl.pallas_call(
        body, name="attn_bwd", grid=(H // HB, S // T),
        in_specs=[whole, pl.BlockSpec((HB, T, 2 * LANE), lambda h, j: (h, j, 0)), pl.BlockSpec((HB, T, LANE), lambda h, j: (h, j, 0)),
                  pl.BlockSpec((S, HB * LANE), lambda h, j: (0, h)), row, row],
        out_specs=[whole, pl.BlockSpec((HB, T, 2 * LANE), lambda h, j: (h, j, 0)), pl.BlockSpec((HB, T, LANE), lambda h, j: (h, j, 0))],
        out_shape=[jax.ShapeDtypeStruct((H, S, 2 * LANE), F32), jax.ShapeDtypeStruct((H, S, 2 * LANE), F32),
                   jax.ShapeDtypeStruct((H, S, LANE), F32)],
        compiler_params=_params(("parallel", "arbitrary")),
    )(Q, K, V, do, lse_t, delta_t)


def _expand_matrix(cfg):
    r = lax.broadcasted_iota(I32, (LANE, cfg.INNER), 0)
    c = lax.broadcasted_iota(I32, (LANE, cfg.INNER), 1)
    return (r == c // HP).astype(F32)


def _softplus(x):
    return jnp.maximum(x, 0.0) + jnp.log(1.0 + jnp.exp(-jnp.abs(x)))


def _ssd_prep(cfg, dt_raw, dt_bias_pad, a_log_pad, expand):
    HS = cfg.HS

    def fn(raw, bias, alog, E):
        heads = lax.broadcasted_iota(I32, raw.shape, 1) < HS
        dt = jnp.where(heads, _softplus(raw + bias), 0.0)
        a = dt * jnp.where(heads[0:1], -jnp.exp(alog), 0.0)
        return dt, a, _dot(dt, E, precision=HI)

    return _rowwise("ssd_prep", fn, [dt_raw], [dt_bias_pad, a_log_pad, expand],
                    [(LANE, F32), (LANE, F32), (cfg.INNER, F32)], [], _pick(cfg.S, 512, 8))


def _tril(T):
    return lax.broadcasted_iota(I32, (T, T), 0) >= lax.broadcasted_iota(I32, (T, T), 1)


def _ssd_fwd(cfg, xc, dt_exp, a_small, dskip_exp, expand):
    S, T, INNER, G, NPAIR = cfg.S, cfg.T, cfg.INNER, cfg.G, cfg.NPAIR
    NC = S // T

    def body(xc_ref, dte_ref, as_ref, dsk_ref, e_ref, y_ref, hin_ref, ht_ref):
        @pl.when(pl.program_id(0) == 0)
        def _():
            ht_ref[...] = jnp.zeros_like(ht_ref)

        tril = _tril(T)
        tri = tril.astype(F32)
        acs_s = _dot(tri, as_ref[...], precision=HI)
        acs_e = _dot(acs_s, e_ref[...], precision=HI)
        acs_t = acs_s.T
        lo = lax.broadcasted_iota(I32, (T, LANE), 1) < HP
        for g in range(G):
            Bb = xc_ref[:, INNER + g * NST:INNER + (g + 1) * NST].astype(BF16)
            Cb = xc_ref[:, INNER + (G + g) * NST:INNER + (G + g + 1) * NST].astype(BF16)
            Gm = _dot(Cb, Bb, NT)
            for j in range(g * NPAIR // G, (g + 1) * NPAIR // G):
                sl = slice(j * LANE, (j + 1) * LANE)
                Xp = xc_ref[:, sl]
                Xdt = Xp * dte_ref[:, sl]
                Xb = Xdt.astype(BF16)
                acs_p = acs_e[:, sl]
                last = acs_p[T - 1:T, :]
                Hin = ht_ref[j]
                hin_ref[0, j] = Hin
                yd = []
                for e in (0, 1):
                    h = 2 * j + e
                    Lm = jnp.exp(jnp.where(tril, acs_s[:, h:h + 1] - acs_t[h:h + 1, :], -1e30))
                    yd.append(_dot((Gm * Lm).astype(BF16), Xb))
                y_off = _dot(Cb, Hin.astype(BF16)) * jnp.exp(acs_p)
                y_ref[:, sl] = jnp.where(lo, yd[0], yd[1]) + y_off + Xp * dsk_ref[:, sl]
                st = _dot(Bb, (Xdt * jnp.exp(last - acs_p)).astype(BF16), TN)
                ht_ref[j] = jnp.exp(last) * Hin + st

    rows = lambda w: pl.BlockSpec((T, w), lambda c: (c, 0))
    return pl.pallas_call(
        body, name="ssd_fwd", grid=(NC,),
        in_specs=[rows(cfg.CONVCH), rows(INNER), rows(LANE), pl.BlockSpec((1, INNER), lambda c: (0, 0)),
                  pl.BlockSpec((LANE, INNER), lambda c: (0, 0))],
        out_specs=[rows(INNER), pl.BlockSpec((1, NPAIR, NST, LANE), lambda c: (c, 0, 0, 0))],
        out_shape=[jax.ShapeDtypeStruct((S, INNER), F32), jax.ShapeDtypeStruct((NC, NPAIR, NST, LANE), F32)],
        scratch_shapes=[pltpu.VMEM((NPAIR, NST, LANE), F32)],
        compiler_params=_params(("arbitrary",)),
    )(xc, dt_exp, a_small, dskip_exp, expand)


def _ssd_bwd(cfg, dy, xc, dt_exp, a_small, dskip_exp, hin, dt_raw, dt_bias_pad, a_log_pad, expand):
    S, T, INNER, G, NPAIR, HS = cfg.S, cfg.T, cfg.INNER, cfg.G, cfg.NPAIR, cfg.HS
    NC = S // T

    def body(dy_ref, xc_ref, dte_ref, as_ref, dsk_ref, hin_ref, raw_ref, bias_ref, alog_ref, e_ref,
             dxc_ref, draw_ref, dbias_ref, dalog_ref, dskip_ref, dht_ref, cols_ref, rows_ref, dacs_ref, ddt_ref):
        first = pl.program_id(0) == 0

        @pl.when(first)
        def _():
            dht_ref[...] = jnp.zeros_like(dht_ref)

        tril = _tril(T)
        tri = tril.astype(F32)
        a_s = as_ref[...]
        acs_s = _dot(tri, a_s, precision=HI)
        acs_e = _dot(acs_s, e_ref[...], precision=HI)
        acs_t = acs_s.T
        lo = lax.broadcasted_iota(I32, (T, LANE), 1) < HP
        last_row = lax.broadcasted_iota(I32, (T, LANE), 0) == T - 1
        cols_ref[...] = jnp.zeros_like(cols_ref)
        rows_ref[...] = jnp.zeros_like(rows_ref)
        dsk_parts = []
        for g in range(G):
            bsl = slice(INNER + g * NST, INNER + (g + 1) * NST)
            csl = slice(INNER + (G + g) * NST, INNER + (G + g + 1) * NST)
            Bb = xc_ref[:, bsl].astype(BF16)
            Cb = xc_ref[:, csl].astype(BF16)
            Gm = _dot(Cb, Bb, NT)
            dG = jnp.zeros((T, T), F32)
            dB = jnp.zeros((T, NST), F32)
            dC = jnp.zeros((T, NST), F32)
            for j in range(g * NPAIR // G, (g + 1) * NPAIR // G):
                sl = slice(j * LANE, (j + 1) * LANE)
                Xp = xc_ref[:, sl]
                dtp = dte_ref[:, sl]
                Xdt = Xp * dtp
                Xb = Xdt.astype(BF16)
                acs_p = acs_e[:, sl]
                last = acs_p[T - 1:T, :]
                e_p, dec, cd = jnp.exp(acs_p), jnp.exp(last - acs_p), jnp.exp(last)
                Hin = hin_ref[0, j]
                Hb = Hin.astype(BF16)
                dHn = dht_ref[j]
                dHb = dHn.astype(BF16)
                dYp = dy_ref[:, sl]
                z = _dot(Cb, Hb)
                dz = (dYp * e_p).astype(BF16)
                dacs_p = dYp * z * e_p
                dC = dC + _dot(dz, Hb, NT)
                dHin = _dot(Cb, dz, TN) + cd * dHn
                dlast = _colsum(dHn * Hin) * cd
                qv = _dot(Bb, dHb)
                dXdt = qv * dec
                ddec = qv * Xdt * dec
                dacs_p = dacs_p - ddec
                dlast = dlast + _colsum(ddec)
                dB = dB + _dot((Xdt * dec).astype(BF16), dHb, NT)
                for e in (0, 1):
                    h = 2 * j + e
                    Lm = jnp.exp(jnp.where(tril, acs_s[:, h:h + 1] - acs_t[h:h + 1, :], -1e30))
                    Mh = Gm * Lm
                    dYe = jnp.where(lo if e == 0 else jnp.logical_not(lo), dYp, 0.0).astype(BF16)
                    dM = _dot(dYe, Xb, NT)
                    dXdt = dXdt + _dot(Mh.astype(BF16), dYe, TN)
                    W = dM * Mh
                    cols_ref[:, h:h + 1] = jnp.sum(W, axis=1, keepdims=True)
                    rows_ref[h:h + 1, :] = _colsum(W)
                    dG = dG + dM * Lm
                dacs_ref[:, sl] = dacs_p + jnp.where(last_row, dlast, 0.0)
                ddt_ref[:, sl] = dXdt * Xp
                dxc_ref[:, sl] = dXdt * dtp + dYp * dsk_ref[:, sl]
                dsk_parts.append(_colsum(dYp * Xp))
                dht_ref[j] = dHin
            dGb = dG.astype(BF16)
            dxc_ref[:, bsl] = dB + _dot(dGb, Cb, TN)
            dxc_ref[:, csl] = dC + _dot(dGb, Bb)
        E = e_ref[...]
        dacs_s = cols_ref[...] - rows_ref[...].T + _dot(dacs_ref[...], E, NT, precision=HI)
        da = _dot(tri, dacs_s, TN, precision=HI)
        heads = lax.broadcasted_iota(I32, (1, LANE), 1) < HS
        A = jnp.where(heads, -jnp.exp(alog_ref[...]), 0.0)
        ddt = _dot(ddt_ref[...], E, NT, precision=HI) + da * A
        draw = jnp.where(heads, ddt * _sigmoid(raw_ref[...] + bias_ref[...]), 0.0)
        draw_ref[...] = draw
        dsk = _dot(jnp.broadcast_to(jnp.concatenate(dsk_parts, axis=1), (8, INNER)), E, NT, precision=HI)[0:1]
        for ref, val in ((dbias_ref, _colsum(draw)), (dalog_ref, _colsum(da * a_s)), (dskip_ref, dsk)):
            @pl.when(first)
            def _():
                ref[...] = val

            @pl.when(jnp.logical_not(first))
            def _():
                ref[...] += val

    dt_raw, _, raw_block = _window(dt_raw)
    rows = lambda w, b=0: pl.BlockSpec((T, w), lambda c: (NC - 1 - c, b))
    vec = lambda w: pl.BlockSpec((1, w), lambda c: (0, 0))
    return pl.pallas_call(
        body, name="ssd_bwd", grid=(NC,),
        in_specs=[rows(INNER), rows(cfg.CONVCH), rows(INNER), rows(LANE), vec(INNER),
                  pl.BlockSpec((1, NPAIR, NST, LANE), lambda c: (NC - 1 - c, 0, 0, 0)), rows(LANE, raw_block), vec(LANE), vec(LANE),
                  pl.BlockSpec((LANE, INNER), lambda c: (0, 0))],
        out_specs=[rows(cfg.CONVCH), rows(LANE), vec(LANE), vec(LANE), vec(LANE)],
        out_shape=[jax.ShapeDtypeStruct((S, cfg.CONVCH), F32), jax.ShapeDtypeStruct((S, LANE), F32)]
        + [jax.ShapeDtypeStruct((1, LANE), F32)] * 3,
        scratch_shapes=[pltpu.VMEM((NPAIR, NST, LANE), F32), pltpu.VMEM((T, LANE), F32), pltpu.VMEM((LANE, T), F32),
                        pltpu.VMEM((T, INNER), F32), pltpu.VMEM((T, INNER), F32)],
        compiler_params=_params(("arbitrary",)),
    )(dy, xc, dt_exp, a_small, dskip_exp, hin, dt_raw, dt_bias_pad, a_log_pad, expand)


def _ssd_post(cfg, y, z, norm_g, into):
    W = cfg.INNER // cfg.G

    def fn(y, z, g):
        yz = y * z * _sigmoid(z)
        return jnp.concatenate([yz[:, i * W:(i + 1) * W] * _rs(yz[:, i * W:(i + 1) * W]) for i in range(cfg.G)], axis=1) * g

    return _rowwise("ssd_post", fn, [y, z], [norm_g], [(cfg.INNER, BF16)], [], _pick(cfg.S, 256, 8), (into, cfg.MLAW // cfg.INNER))[0]


def _ssd_post_bwd(cfg, db, y, z, norm_g):
    W = cfg.INNER // cfg.G

    def fn(db, y, z, g):
        sg = _sigmoid(z)
        yz = y * z * sg
        dn = db * g
        dyz, nh = [], []
        for i in range(cfg.G):
            seg = yz[:, i * W:(i + 1) * W]
            r = _rs(seg)
            nh.append(seg * r)
            dyz.append(_rms_back(nh[-1], r, dn[:, i * W:(i + 1) * W]))
        dyz = jnp.concatenate(dyz, axis=1)
        return dyz * z * sg, dyz * y * sg * (1.0 + z * (1.0 - sg)), _colsum(db * jnp.concatenate(nh, axis=1))

    return _rowwise("ssd_post_bwd", fn, [db, y, z], [norm_g], [(cfg.INNER, F32), (cfg.INNER, BF16)], [(1, cfg.INNER)],
                    _pick(cfg.S, 256, 8))


def _rms_pre(cfg, x, g):
    return _rowwise("rms_pre", lambda x, g: x * _rs(x) * g, [x], [g], [(cfg.D, BF16)], [], _pick(cfg.S, 256, 8))[0]


def _local_grads(cfg, x, tgt, W, sp, mla_weights=None, out_weight=None, ffn_weights=None, down_weight=None,
                 ffn_grads_ready=None, early_grads_ready=None, in_grad_ready=None, xn=None, after_in=None):
    S, D, H, INNER = cfg.S, cfg.D, cfg.H, cfg.INNER
    ts = _pick(S, 256, 8)
    tc = _CONV_COLS

    if xn is None:
        xn = _rms_pre(cfg, x, sp["mix_pre_g"])
    u = _matmul("mm_in", xn, W["w_in"], "nt", F32, after=after_in)
    c_q, c_kv, kr, z, xbc, dt_raw = [(u, cfg.window(n)) for n in ("c_q", "c_kv", "kr", "z", "xbc", "dt")]

    if mla_weights is not None:
        sp = dict(sp, q_norm_g=sp["q_norm_g"] + mla_weights.pass_on(u)[0, 0])
    cqn = _rowwise("rms_q", lambda x, g: x * _rs(x) * g, [c_q], [sp["q_norm_g"]], [(cfg.QL, BF16)], [], ts)[0]
    ckvn = _rowwise("rms_kv", lambda x, g: x * _rs(x) * g, [c_kv], [sp["kv_norm_g"]], [(cfg.KVL, BF16)], [], ts)[0]
    if mla_weights is not None:
        W = dict(W, **mla_weights.arrived(ckvn))
    q = _matmul("mm_uq", cqn, W["w_uq"], "nn", F32)
    kv = _matmul("mm_ukv", ckvn, W["w_ukv"], "nn", F32)
    cos2, sin2 = _rope_tables(S)
    Qh, Kh, Vh = _mla_pack(cfg, q, kv, kr, cos2, sin2)
    a_out, lse_t, ab_out = _attn_fwd(cfg, Qh, Kh, Vh)
    if out_weight is not None:
        sp = dict(sp, ssm_conv_b=sp["ssm_conv_b"] + out_weight.pass_on(a_out)[0, 0])

    pad = lambda v: jnp.pad(v, ((0, 0), (0, LANE - v.shape[1])))
    expand = _expand_matrix(cfg)
    dt_bias_pad, a_log_pad = pad(sp["dt_bias"]), pad(sp["a_log"])
    dskip_exp = jnp.repeat(sp["d_skip"], HP, axis=1)
    xc = _colwise("ssm_act", _ssm_act, [xbc], [sp["ssm_conv_w"], sp["ssm_conv_b"]], [F32], [], tc)[0]
    dt_s, a_s, dt_exp = _ssd_prep(cfg, dt_raw, dt_bias_pad, a_log_pad, expand)
    y_ssd, hin = _ssd_fwd(cfg, xc, dt_exp, a_s, dskip_exp, expand)
    ab_out = _ssd_post(cfg, y_ssd, z, sp["ssm_norm_g"], ab_out)

    if out_weight is not None:
        W = dict(W, **out_weight.arrived(ab_out))
    if ffn_weights is not None:
        sp = dict(sp, mix_post_g=sp["mix_post_g"] + ffn_weights.pass_on(ab_out)[0, 0])
    mix = _matmul("mm_out", ab_out, W["w_out"], "nn", F32)

    def mid(x, mix, g_mp, g_fp):
        x1 = x + mix * _rs(mix) * g_mp
        return x1, x1 * _rs(x1) * g_fp

    x1, h2 = _rowwise("fwd_mid", mid, [x, mix], [sp["mix_post_g"], sp["ffn_pre_g"]], [(D, F32), (D, BF16)], [], ts)
    if ffn_weights is not None:
        W = dict(W, **ffn_weights.arrived(h2))
    gate_pre, up = _matmul_twin("mm_gate_up", h2, W["w_gate"], W["w_up"], "nn", F32)
    if down_weight is not None:
        sp = dict(sp, ffn_conv_b=sp["ffn_conv_b"] + down_weight.pass_on(gate_pre)[0, 0])
    act = _colwise("ffn_act", _ffn_act, [gate_pre, up], [sp["ffn_conv_w"], sp["ffn_conv_b"]], [BF16], [], tc)[0]
    if down_weight is not None:
        W = dict(W, **down_weight.arrived(act))
    f = _matmul("mm_down", act, W["w_down"], "nn", F32)

    def final(x1, f, t, g):
        r = _rs(f)
        fh = f * r
        err = x1 + fh * g - t
        loss = 0.5 * jnp.sum(jnp.mean(err * err, axis=-1, keepdims=True), axis=0, keepdims=True)
        dy = err * (1.0 / D)
        return dy, _rms_back(fh, r, dy * g), _colsum(dy * fh), loss

    dy, df, g_ffn_post, loss = _rowwise("final", final, [x1, f, tgt], [sp["ffn_post_g"]], [(D, F32), (D, BF16)],
                                        [(1, D), (1, LANE)], ts)
    gW = {}
    dact = _matmul("mm_down_dx", df, W["w_down"], "nt", F32)
    gW["w_down"] = _matmul("mm_down_dw", act, df, "tn", BF16)
    dgate, dup, g_ffn_conv_w, g_ffn_conv_b = _colwise(
        "ffn_act_bwd", _ffn_act_back, [dact, gate_pre, up], [sp["ffn_conv_w"], sp["ffn_conv_b"]], [BF16, BF16], [FFN_K, 1], tc)
    gW["w_gate"], gW["w_up"] = _matmul_twin("mm_gate_up_dw", h2, dgate, dup, "tn", BF16)
    if ffn_grads_ready is not None:
        sp = dict(sp, ffn_pre_g=sp["ffn_pre_g"] + ffn_grads_ready({n: gW[n] for n in ("w_down", "w_gate", "w_up")})[0, 0])
    dh2 = _matmul("mm_gu_dx", dgate, W["w_gate"], "nt", F32, dup, W["w_up"], chips=True)

    def mid_back(dy, dh2, x1, mix, g_mp, g_fp):
        r2 = _rs(x1)
        xh = x1 * r2
        dx1 = dy + _rms_back(xh, r2, dh2 * g_fp)
        r1 = _rs(mix)
        mh = mix * r1
        return dx1, _rms_back(mh, r1, dx1 * g_mp), _colsum(dh2 * xh), _colsum(dx1 * mh)

    dx1, dmix, g_ffn_pre, g_mix_post = _rowwise("bwd_mid", mid_back, [dy, dh2, x1, mix], [sp["mix_post_g"], sp["ffn_pre_g"]],
                                                [(D, F32), (D, BF16)], [(1, D), (1, D)], ts)
    dab_out = _matmul("mm_out_dx", dmix, W["w_out"], "nt", F32)
    db_out = (dab_out, (INNER, cfg.MLAW // INNER))
    gW["w_out"] = _matmul("mm_out_dw", ab_out, dmix, "tn", BF16)
    early_token = jnp.zeros((8, LANE), F32)
    if early_grads_ready is not None:
        early_token = early_grads_ready({n: gW[n] for n in ("w_down", "w_gate", "w_up", "w_out")})
        sp = dict(sp, ssm_norm_g=sp["ssm_norm_g"] + early_token[0, 0])

    dy_ssd, dz, g_ssm_norm = _ssd_post_bwd(cfg, db_out, y_ssd, z, sp["ssm_norm_g"])
    dxc, ddt_raw, g_dt_bias, g_a_log, g_d_skip = _ssd_bwd(cfg, dy_ssd, xc, dt_exp, a_s, dskip_exp, hin, dt_raw,
                                                          dt_bias_pad, a_log_pad, expand)
    dxbc, g_ssm_conv_w, g_ssm_conv_b = _colwise("ssm_act_bwd", _ssm_act_back, [dxc, xbc], [sp["ssm_conv_w"], sp["ssm_conv_b"]],
                                                [BF16], [SSM_K, 1], tc)

    delta_t = _attn_delta(cfg, dab_out, a_out, early_token)
    dQ, dK, dV = _attn_bwd(cfg, Qh, Kh, Vh, dab_out, lse_t, delta_t)
    dq, dkv, dkr = _mla_unpack(cfg, dQ, dK, dV, cos2, sin2)
    dcqn = _matmul("mm_uq_dx", dq, W["w_uq"], "nt", F32)
    dckvn = _matmul("mm_ukv_dx", dkv, W["w_ukv"], "nt", F32)
    gW["w_uq"] = _matmul("mm_uq_dw", cqn, dq, "tn", BF16)
    gW["w_ukv"] = _matmul("mm_ukv_dw", ckvn, dkv, "tn", BF16)

    def rms_back(x, dy, g):
        r = _rs(x)
        xh = x * r
        return _rms_back(xh, r, dy * g), _colsum(dy * xh)

    dc_q, g_q_norm = _rowwise("rms_q_bwd", rms_back, [c_q, dcqn], [sp["q_norm_g"]], [(cfg.QL, BF16)], [(1, cfg.QL)], ts)
    dc_kv, g_kv_norm = _rowwise("rms_kv_bwd", rms_back, [c_kv, dckvn], [sp["kv_norm_g"]], [(cfg.KVL, BF16)], [(1, cfg.KVL)], ts)

    du = dict(c_q=dc_q, c_kv=dc_kv, kr=dkr, z=dz, xbc=dxbc, dt=ddt_raw.astype(BF16))
    du = jnp.concatenate([du[n] for n in sorted(du, key=lambda n: cfg.seg[n][0])], axis=1)
    assert du.shape[1] == cfg.EXT, "the layout of u has gaps"
    gW["w_in"] = _matmul("mm_in_dw", du, xn, "tn", BF16)
    if in_grad_ready is None:
        dxn = _matmul("mm_in_dx", du, W["w_in"], "nn", F32)
    else:
        token = in_grad_ready({n: gW[n] for n in ("w_in", "w_uq", "w_ukv")}, None)
        dxn = _matmul("mm_in_dx", du, W["w_in"], "nn", F32, after=token)
        sp = dict(sp, mix_pre_g=sp["mix_pre_g"] + in_grad_ready(None, dxn)[0, 0])

    def first_back(dx1, dxn, x, g):
        r = _rs(x)
        xh = x * r
        return dx1 + _rms_back(xh, r, dxn * g), _colsum(dxn * xh)

    grad_x, g_mix_pre = _rowwise("bwd_first", first_back, [dx1, dxn, x], [sp["mix_pre_g"]], [(D, F32)], [(1, D)], ts)

    gs = dict(mix_pre_g=g_mix_pre, q_norm_g=g_q_norm, kv_norm_g=g_kv_norm, ssm_conv_w=g_ssm_conv_w, ssm_conv_b=g_ssm_conv_b,
              dt_bias=g_dt_bias[:, :cfg.HS], a_log=g_a_log[:, :cfg.HS], d_skip=g_d_skip[:, :cfg.HS], ssm_norm_g=g_ssm_norm,
              mix_post_g=g_mix_post, ffn_pre_g=g_ffn_pre, ffn_conv_w=g_ffn_conv_w, ffn_conv_b=g_ffn_conv_b,
              ffn_post_g=g_ffn_post)
    return loss, grad_x, gW, gs


def _to_kernel_layout(cfg, name, w):
    if name == "w_in":
        parts, at = [], 0
        for off, width, n_off, n_width in sorted(cfg.seg.values()):
            parts += [jnp.zeros((off - at, w.shape[1]), w.dtype), w[n_off:n_off + n_width],
                      jnp.zeros((width - n_width, w.shape[1]), w.dtype)]
            at = off + width
        parts.append(jnp.zeros((cfg.EXT - at, w.shape[1]), w.dtype))
        return jnp.concatenate([p for p in parts if p.shape[0]], axis=0)
    if name in ("w_uq", "w_ukv"):
        per = NOPE + (ROPE if name == "w_uq" else VH)
        return jnp.concatenate([w[:, h * per:h * per + NOPE] for h in range(cfg.H)]
                               + [w[:, h * per + NOPE:(h + 1) * per] for h in range(cfg.H)], axis=1)
    return w


def _from_kernel_layout(cfg, name, g):
    if name == "w_in":
        return jnp.concatenate([g[off:off + n_width] for off, _, _, n_width in sorted(cfg.seg.values(), key=lambda s: s[2])], axis=0)
    if name in ("w_uq", "w_ukv"):
        second = ROPE if name == "w_uq" else VH
        base = cfg.H * NOPE
        parts = []
        for h in range(cfg.H):
            parts += [g[:, h * NOPE:(h + 1) * NOPE], g[:, base + h * second:base + (h + 1) * second]]
        return jnp.concatenate(parts, axis=1)
    return g


_CHIP_MAJOR = ("w_gate", "w_up")
_RELAYOUT = ("w_uq", "w_ukv")
_LAYOUT_ROWS = 256
_CONV_COLS = 256


def _w_in_layout(cfg, wg):
    _, rs, d = wg.shape
    tc = _pick(d, _LAYOUT_ROWS, LANE)

    def body(w_ref, o_ref):
        o_ref[...] = _to_kernel_layout(cfg, "w_in", jnp.concatenate([w_ref[k] for k in range(N_CHIPS)], axis=0))

    return pl.pallas_call(
        body, name="layout_w_in", grid=(d // tc,),
        in_specs=[pl.BlockSpec((N_CHIPS, rs, tc), lambda j: (0, 0, j))], out_specs=pl.BlockSpec((cfg.EXT, tc), lambda j: (0, j)),
        out_shape=jax.ShapeDtypeStruct((cfg.EXT, d), wg.dtype), compiler_params=_params(("parallel",)),
    )(wg)


def _w_in_grad_to_chips(cfg, g):
    _, d = g.shape
    rs = cfg.IN_COLS // N_CHIPS
    tc = _pick(d, _LAYOUT_ROWS, LANE)

    def body(g_ref, o_ref):
        nat = _from_kernel_layout(cfg, "w_in", g_ref[...])
        for k in range(N_CHIPS):
            o_ref[k] = nat[k * rs:(k + 1) * rs]

    return pl.pallas_call(
        body, name="layout_grad_w_in", grid=(d // tc,),
        in_specs=[pl.BlockSpec((cfg.EXT, tc), lambda j: (0, j))], out_specs=pl.BlockSpec((N_CHIPS, rs, tc), lambda j: (0, 0, j)),
        out_shape=jax.ShapeDtypeStruct((N_CHIPS, rs, d), g.dtype), compiler_params=_params(("parallel",)),
    )(g)


def _gathered_to_kernel(cfg, name, wg):
    if name in _CHIP_MAJOR:
        return wg
    if name == "w_in":
        return _w_in_layout(cfg, wg)
    if name not in _RELAYOUT:
        return wg.reshape(wg.shape[0] * wg.shape[1], wg.shape[2])
    _, rows, cs = wg.shape
    tr = _pick(rows, _LAYOUT_ROWS, 16)

    def body(w_ref, o_ref):
        o_ref[...] = _to_kernel_layout(cfg, name, jnp.concatenate([w_ref[k] for k in range(N_CHIPS)], axis=1))

    wide = jax.eval_shape(lambda w: _to_kernel_layout(cfg, name, w), jax.ShapeDtypeStruct((rows, N_CHIPS * cs), wg.dtype)).shape[1]
    return pl.pallas_call(
        body, name="layout_" + name, grid=(rows // tr,),
        in_specs=[pl.BlockSpec((N_CHIPS, tr, cs), lambda i: (0, i, 0))], out_specs=pl.BlockSpec((tr, wide), lambda i: (i, 0)),
        out_shape=jax.ShapeDtypeStruct((rows, wide), wg.dtype), compiler_params=_params(("parallel",)),
    )(wg)


def _grad_to_chips(cfg, name, g):
    if name in _CHIP_MAJOR:
        return g
    if name == "w_in":
        return _w_in_grad_to_chips(cfg, g)
    if name not in _RELAYOUT:
        return g.reshape(N_CHIPS, g.shape[0] // N_CHIPS, g.shape[1])
    rows, wide = g.shape
    tr = _pick(rows, _LAYOUT_ROWS, 16)
    cs = jax.eval_shape(lambda v: _from_kernel_layout(cfg, name, v), g).shape[1] // N_CHIPS

    def body(g_ref, o_ref):
        nat = _from_kernel_layout(cfg, name, g_ref[...])
        for k in range(N_CHIPS):
            o_ref[k] = nat[:, k * cs:(k + 1) * cs]

    return pl.pallas_call(
        body, name="layout_grad_" + name, grid=(rows // tr,),
        in_specs=[pl.BlockSpec((tr, wide), lambda i: (i, 0))], out_specs=pl.BlockSpec((N_CHIPS, tr, cs), lambda i: (0, i, 0)),
        out_shape=jax.ShapeDtypeStruct((N_CHIPS, rows, cs), g.dtype), compiler_params=_params(("parallel",)),
    )(g)


def _me():
    return lax.axis_index("x"), lax.axis_index("y"), lax.axis_index("c")


def _other_chips(x, y):
    return [(1 - x, y), (x, 1 - y), (1 - x, 1 - y)]


_ANY = pl.BlockSpec(memory_space=pl.ANY)


BLOCK_ELEMS = 1 << 19
BLOCK_ELEMS_FEW = 1 << 20


def _row_block(rows, cols, mult, elems=BLOCK_ELEMS):
    return _pick(rows, max(mult, elems // cols // mult * mult), mult)


def _scalar(v):
    return v.astype(I32).reshape(1)


def _blocks2d(r, c, mult, elems=BLOCK_ELEMS):
    if r % mult == 0:
        tr = _row_block(r, c, mult, elems)
        return (tr, c), r // tr, lambda i: (i, 0)
    tc = _pick(c, max(LANE, elems // r // LANE * LANE), LANE)
    return (r, tc), c // tc, lambda i: (0, i)


def _by_rows(rows):
    return rows % 32 == 0


def _half_shape(rows, cols):
    return (rows // 2, cols) if _by_rows(rows) else (rows, cols // 2)


def _half_blocks(rows, cols, mult, elems=BLOCK_ELEMS):
    hr, hc = _half_shape(rows, cols)
    block, n, part = _blocks2d(hr, hc, mult, elems)
    assert (hr % mult == 0) == _by_rows(rows), (rows, cols, mult)
    full = (lambda h, i: (h * n + i, 0)) if _by_rows(rows) else (lambda h, i: (0, h * n + i))
    return block, n, full, part


def _half(ref, k, half):
    hr, hc = _half_shape(ref.shape[1], ref.shape[2])
    if _by_rows(ref.shape[1]):
        return ref.at[k, pl.ds(pl.multiple_of(half * hr, 16), hr), :]
    return ref.at[k, :, pl.ds(pl.multiple_of(half * hc, LANE), hc)]


def _shard_blocks(w, br, bc):
    if w.shape[0] == 1:
        def write(ref, v):
            ref[...] = v
        return (lambda f: pl.BlockSpec((None, br, bc), lambda *a: (0, *f(*a)))), (lambda ref: ref[...]), write
    assert w.shape[1] == 1 and br == w.shape[0], w.shape

    def write_rows(ref, v):
        ref[:, 0, :] = v
    return (lambda f: pl.BlockSpec((br, 1, bc), lambda *a: (0, 0, f(*a)[1]))), (lambda ref: ref[:, 0, :]), write_rows


def _stage_shard(name, w, chip, after=None):
    rs, cs = w.shape[0] * w.shape[1], w.shape[2]
    (br, bc), n, idx = _blocks2d(rs, cs, 16, BLOCK_ELEMS_FEW)
    spec, get, _ = _shard_blocks(w, br, bc)

    def body(chip_ref, w_ref, *refs):
        refs[-1][...] = get(w_ref).astype(BF16)

    return pl.pallas_call(
        body, name="stage_" + name,
        grid_spec=pltpu.PrefetchScalarGridSpec(
            num_scalar_prefetch=1, grid=(n,),
            in_specs=[spec(lambda i, chip_ref: idx(i))] + ([] if after is None else [_ANY]),
            out_specs=pl.BlockSpec((None, br, bc), lambda i, chip_ref: (chip_ref[0], *idx(i)))),
        out_shape=jax.ShapeDtypeStruct((N_CHIPS, rs, cs), BF16),
        compiler_params=_params(("parallel",)),
    )(_scalar(chip), w, *([] if after is None else [after]))


_HBM = pl.BlockSpec(memory_space=pltpu.HBM)
_SEM = pl.BlockSpec(memory_space=pltpu.SEMAPHORE)
_EFFECT = pltpu.SideEffectType.DATAFLOW_SIDE_EFFECTING


def _split_starts(name, groups, n_copies, copies, after):
    sizes = [len(g) for g in groups]
    bufs = [b for g in groups for b in g]
    n, k = len(bufs), len(groups)
    starts = [sum(sizes[:j]) for j in range(k)]

    def body(*refs):
        sems = refs[n + 1:n + 1 + 2 * k]
        for j, (at, size) in enumerate(zip(starts, sizes)):
            for cp in copies(refs[at:at + size], sems[2 * j], sems[2 * j + 1]):
                cp.start()
        refs[-1][...] = jnp.zeros_like(refs[-1])

    res = pl.pallas_call(
        body, name=name,
        out_shape=(*[pltpu.SemaphoreType.DMA((c,)) for c in n_copies for _ in range(2)],
                   *[pltpu.HBM(b.shape, b.dtype) for b in bufs], jax.ShapeDtypeStruct((8, LANE), F32)),
        in_specs=[_HBM] * n + [_ANY], out_specs=(*[_SEM] * (2 * k), *[_HBM] * n, pl.BlockSpec(memory_space=pltpu.VMEM)),
        input_output_aliases={i: 2 * k + i for i in range(n)},
        compiler_params=pltpu.CompilerParams(has_side_effects=_EFFECT),
    )(*[pltpu.with_memory_space_constraint(b, pltpu.HBM) for b in bufs], after)
    return [(res[2 * j], res[2 * j + 1], list(res[2 * k + at:2 * k + at + size]))
            for j, (at, size) in enumerate(zip(starts, sizes))], res[-1]


def _split_start(name, bufs, n_copies, copies, after):
    [(send_sems, recv_sems, bufs)], token = _split_starts(name, [bufs], [n_copies], copies, after)
    return send_sems, recv_sems, bufs, token


def _split_wait(name, send_sems, recv_sems, bufs, after, copies):
    n = len(bufs)

    def body(*refs):
        for cp in copies(refs[:n], refs[n], refs[n + 1]):
            cp.wait_send()
            cp.wait_recv()

    return list(pl.pallas_call(
        body, name=name, out_shape=[pltpu.HBM(b.shape, b.dtype) for b in bufs],
        in_specs=[_HBM] * n + [_SEM, _SEM, _ANY], out_specs=[_HBM] * n,
        input_output_aliases={i: i for i in range(n)},
        compiler_params=pltpu.CompilerParams(has_side_effects=_EFFECT),
    )(*bufs, send_sems, recv_sems, after))


def _gather_to_chips(bufs, send_sems, recv_sems):
    x, y, c = _me()
    return [pltpu.make_async_remote_copy(src_ref=_half(b, 2 * x + y, c), dst_ref=_half(b, 2 * x + y, c),
                                         send_sem=send_sems.at[3 * w + j], recv_sem=recv_sems.at[3 * w + j],
                                         device_id=(cx, cy, c), device_id_type=MESH_ID)
            for w, b in enumerate(bufs) for j, (cx, cy) in enumerate(_other_chips(x, y))]


def _gather_to_sibling(bufs, send_sems, recv_sems):
    x, y, c = _me()
    return [pltpu.make_async_remote_copy(src_ref=_half(b, 2 * cx + cy, c), dst_ref=_half(b, 2 * cx + cy, c),
                                         send_sem=send_sems.at[3 * w + j], recv_sem=recv_sems.at[3 * w + j],
                                         device_id=(x, y, 1 - c), device_id_type=MESH_ID)
            for w, b in enumerate(bufs) for j, (cx, cy) in enumerate(_other_chips(x, y))]


def _pair_copies(grads, lands, send_sems, recv_sems):
    x, y, c = _me()
    return [pltpu.make_async_remote_copy(src_ref=_half(g_ref, slice(None), 1 - c), dst_ref=l_ref, send_sem=send_sems.at[w],
                                         recv_sem=recv_sems.at[w], device_id=(x, y, 1 - c), device_id_type=MESH_ID)
            for w, (g_ref, l_ref) in enumerate(zip(grads, lands))]


def _pair_exchange_start(name, grads):
    n = len(grads)
    lands = [lax.empty((g.shape[0], *_half_shape(g.shape[1], g.shape[2])), g.dtype) for g in grads]
    send_sems, recv_sems, bufs, token = _split_start(
        "pair_exchange_start_" + name, [*grads, *lands], n, lambda refs, ss, rs: _pair_copies(refs[:n], refs[n:], ss, rs),
        jnp.zeros((8, LANE), F32))
    return (send_sems, recv_sems, bufs), token


def _pair_exchange_wait(name, state, after):
    send_sems, recv_sems, bufs = state
    n = len(bufs) // 2
    bufs = _split_wait("pair_exchange_wait_" + name, send_sems, recv_sems, bufs, after,
                       lambda refs, ss, rs: _pair_copies(refs[:n], refs[n:], ss, rs))
    return bufs[:n], bufs[n:]


def _pair_sum(name, g, theirs, c):
    (br, bc), nb, full, part = _half_blocks(g.shape[1], g.shape[2], 16, 2 * BLOCK_ELEMS_FEW)

    def body(c_ref, a_ref, b_ref, o_ref):
        o_ref[...] = (a_ref[...].astype(F32) + b_ref[...].astype(F32)).astype(o_ref.dtype)

    return pl.pallas_call(
        body, name="pair_sum_" + name,
        grid_spec=pltpu.PrefetchScalarGridSpec(
            num_scalar_prefetch=1, grid=(N_CHIPS, nb),
            in_specs=[pl.BlockSpec((None, br, bc), lambda k, i, c_ref: (k, *full(c_ref[0], i))),
                      pl.BlockSpec((None, br, bc), lambda k, i, c_ref: (k, *part(i)))],
            out_specs=pl.BlockSpec((None, br, bc), lambda k, i, c_ref: (k, *part(i)))),
        out_shape=jax.ShapeDtypeStruct(theirs.shape, BF16),
        compiler_params=_params(("parallel", "parallel")),
    )(_scalar(c), g, theirs)


def _chip_copies(srcs, lands, send_sems, recv_sems):
    x, y, c = _me()
    return [pltpu.make_async_remote_copy(src_ref=s_ref.at[2 * cx + cy], dst_ref=l_ref.at[j], send_sem=send_sems.at[3 * w + j],
                                         recv_sem=recv_sems.at[3 * w + j], device_id=(cx, cy, c), device_id_type=MESH_ID)
            for w, (s_ref, l_ref) in enumerate(zip(srcs, lands)) for j, (cx, cy) in enumerate(_other_chips(x, y))]


def _chip_exchange_start(name, sums):
    n = len(sums)
    lands = [lax.empty((3,) + s.shape[1:], s.dtype) for s in sums]
    send_sems, recv_sems, bufs, token = _split_start(
        "chip_exchange_start_" + name, [*sums, *lands], 3 * n, lambda refs, ss, rs: _chip_copies(refs[:n], refs[n:], ss, rs),
        jnp.zeros((8, LANE), F32))
    return send_sems, recv_sems, bufs[:n], bufs[n:], token


def _chip_exchange_wait(name, send_sems, recv_sems, sums, lands, after):
    n = len(sums)
    bufs = _split_wait("chip_exchange_wait_" + name, send_sems, recv_sems, [*sums, *lands], after,
                       lambda refs, ss, rs: _chip_copies(refs[:n], refs[n:], ss, rs))
    return bufs[:n], bufs[n:]


def _chip_sum(name, sums, theirs, chip):
    _, h, cs = sums.shape
    (br, bc), nb, idx = _blocks2d(h, cs, 16, BLOCK_ELEMS_FEW)

    def body(chip_ref, s_ref, t_ref, o_ref):
        acc = s_ref[...].astype(F32)
        for k in range(3):
            acc = acc + t_ref[k].astype(F32)
        o_ref[...] = acc

    return pl.pallas_call(
        body, name="chip_sum_" + name,
        grid_spec=pltpu.PrefetchScalarGridSpec(
            num_scalar_prefetch=1, grid=(nb,),
            in_specs=[pl.BlockSpec((None, br, bc), lambda i, chip_ref: (chip_ref[0], *idx(i))),
                      pl.BlockSpec((3, br, bc), lambda i, chip_ref: (0, *idx(i)))],
            out_specs=pl.BlockSpec((br, bc), lambda i, chip_ref: idx(i))),
        out_shape=jax.ShapeDtypeStruct((h, cs), F32),
        compiler_params=_params(("parallel",)),
    )(_scalar(chip), sums, theirs)


def _sibling_copies(halves, lands, send_sems, recv_sems):
    x, y, c = _me()
    return [pltpu.make_async_remote_copy(src_ref=h_ref, dst_ref=l_ref, send_sem=send_sems.at[w], recv_sem=recv_sems.at[w],
                                         device_id=(x, y, 1 - c), device_id_type=MESH_ID)
            for w, (h_ref, l_ref) in enumerate(zip(halves, lands))]


def _sibling_exchange_start(name, halves, after):
    n = len(halves)
    lands = [lax.empty(h.shape, h.dtype) for h in halves]
    send_sems, recv_sems, bufs, token = _split_start(
        "sibling_exchange_start_" + name, [*halves, *lands], n, lambda refs, ss, rs: _sibling_copies(refs[:n], refs[n:], ss, rs),
        after)
    return (send_sems, recv_sems, bufs), token


def _sibling_exchange_wait(name, state, after):
    send_sems, recv_sems, bufs = state
    n = len(bufs) // 2
    bufs = _split_wait("sibling_exchange_wait_" + name, send_sems, recv_sems, bufs, after,
                       lambda refs, ss, rs: _sibling_copies(refs[:n], refs[n:], ss, rs))
    return bufs[:n], bufs[n:]


def _sibling_exchange(name, halves):
    n = len(halves)

    def body(*refs):
        ins, outs, send_sems, recv_sems = refs[:n], refs[n:2 * n], refs[2 * n], refs[2 * n + 1]
        x, y, c = _me()
        cps = []
        for w, (h_ref, o_ref) in enumerate(zip(ins, outs)):
            cps.append(pltpu.make_async_remote_copy(src_ref=h_ref, dst_ref=o_ref, send_sem=send_sems.at[w], recv_sem=recv_sems.at[w],
                                                    device_id=(x, y, 1 - c), device_id_type=MESH_ID))
            cps[-1].start()
        for cp in cps:
            cp.wait()

    return pl.pallas_call(
        body, name="sibling_exchange_" + name, in_specs=[_ANY] * n, out_specs=[_ANY] * n,
        out_shape=[jax.ShapeDtypeStruct(h.shape, h.dtype) for h in halves],
        scratch_shapes=[pltpu.SemaphoreType.DMA((n,)), pltpu.SemaphoreType.DMA((n,))],
    )(*halves)


N_DEV = 8


def _peer_copies(bufs, send_sems, recv_sems):
    vec, land = bufs
    x, y, c = _me()
    return [pltpu.make_async_remote_copy(src_ref=vec, dst_ref=land.at[4 * x + 2 * y + c], send_sem=send_sems.at[p - 1],
                                         recv_sem=recv_sems.at[p - 1], device_id=(x ^ (p >> 2), y ^ ((p >> 1) & 1), c ^ (p & 1)),
                                         device_id_type=MESH_ID) for p in range(1, N_DEV)]


def _allreduce_small_start(vec, after):
    land = jnp.zeros((N_DEV,) + vec.shape, F32)
    send_sems, recv_sems, bufs, _ = _split_start("allreduce_small_start", [vec, land], N_DEV - 1, _peer_copies, after)
    return send_sems, recv_sems, bufs


def _allreduce_small_wait(state, chip, core, after):
    send_sems, recv_sems, bufs = state
    vec, land = _split_wait("allreduce_small_wait", send_sems, recv_sems, bufs, after, _peer_copies)

    def body(me_ref, v_ref, l_ref, o_ref):
        acc = None
        for k in range(N_DEV):
            term = jnp.where(me_ref[0] == k, v_ref[...], l_ref[k])
            acc = term if acc is None else acc + term
        o_ref[...] = acc

    return pl.pallas_call(
        body, name="allreduce_small_sum",
        grid_spec=pltpu.PrefetchScalarGridSpec(
            num_scalar_prefetch=1, grid=(1,),
            in_specs=[pl.BlockSpec(vec.shape, lambda i, me_ref: (0, 0)), pl.BlockSpec(land.shape, lambda i, me_ref: (0, 0, 0))],
            out_specs=pl.BlockSpec(vec.shape, lambda i, me_ref: (0, 0))),
        out_shape=jax.ShapeDtypeStruct(vec.shape, F32), compiler_params=_params(("arbitrary",)),
    )(_scalar(2 * chip + core), vec, land)


def _adam_math(w, g, m, v):
    m = ADAM_B1 * m + (1.0 - ADAM_B1) * g
    v = ADAM_B2 * v + (1.0 - ADAM_B2) * (g * g)
    m_hat = m / (1.0 - ADAM_B1 ** ADAM_STEP)
    v_hat = v / (1.0 - ADAM_B2 ** ADAM_STEP)
    return -ADAM_LR * (m_hat / (jnp.sqrt(v_hat) + ADAM_EPS) + ADAM_WD * w), m, v


def _adamw(name, w, g, m, v):
    R, C = w.shape
    tr = _row_block(R, C, 8)

    def body(w_ref, g_ref, m_ref, v_ref, d_ref, nm_ref, nv_ref):
        d_ref[...], nm_ref[...], nv_ref[...] = _adam_math(w_ref[...], g_ref[...], m_ref[...], v_ref[...])

    blk = pl.BlockSpec((tr, C), lambda i: (i, 0))
    return pl.pallas_call(
        body, name=name, grid=(R // tr,), in_specs=[blk] * 4, out_specs=[blk] * 3,
        out_shape=[jax.ShapeDtypeStruct((R, C), F32)] * 3, compiler_params=_params(("parallel",)),
    )(w, g, m, v)


def _adamw_halves(name, w, mine, theirs, m, v, c):
    rs, cs = w.shape[0] * w.shape[1], w.shape[2]
    (br, bc), nb, whole, half = _half_blocks(rs, cs, 8)
    spec, get, put = _shard_blocks(w, br, bc)

    def body(c_ref, w_ref, a_ref, b_ref, m_ref, v_ref, g_ref, d_ref, nm_ref, nv_ref):
        g = jnp.where(pl.program_id(0) == c_ref[0], a_ref[...], b_ref[...])
        put(g_ref, g)
        for ref, val in zip((d_ref, nm_ref, nv_ref), _adam_math(get(w_ref), g, get(m_ref), get(v_ref))):
            put(ref, val)

    full = spec(lambda s, i, c_ref: whole(s, i))
    part = pl.BlockSpec((br, bc), lambda s, i, c_ref: half(i))
    return pl.pallas_call(
        body, name=name,
        grid_spec=pltpu.PrefetchScalarGridSpec(num_scalar_prefetch=1, grid=(2, nb), in_specs=[full, part, part, full, full],
                                               out_specs=[full] * 4),
        out_shape=[jax.ShapeDtypeStruct(w.shape, F32)] * 4, compiler_params=_params(("parallel", "parallel")),
    )(_scalar(c), w, mine, theirs, m, v)


def _pack_small(arrs, lanes=LANE):
    flat = jnp.concatenate([a.reshape(-1) for a in arrs])
    n = -(-flat.shape[0] // (8 * lanes)) * 8 * lanes
    return jnp.pad(flat, (0, n - flat.shape[0])).reshape(8, n // 8)


def _unpack_small(vec, shapes):
    flat, out, off = vec.reshape(-1), [], 0
    for s in shapes:
        out.append(flat[off:off + s[0] * s[1]].reshape(s))
        off += s[0] * s[1]
    return out


class _LateWeights:
    def __init__(self, cfg, tag, names, started, token):
        self.cfg, self.tag, self.names, self.k, self.token = cfg, tag, names, 3 * len(names), token
        self.send, self.recv, self.bufs = started

    @staticmethod
    def start(cfg, groups, staged, after):
        started, token = _split_starts("gather_" + "_".join(groups) + "_chips_start", [[staged[n] for n in names] for names in groups.values()],
                                       [3 * len(names) for names in groups.values()], _gather_to_chips, after)
        return [_LateWeights(cfg, tag, names, s, token) for (tag, names), s in zip(groups.items(), started)]

    def pass_on(self, after):
        bufs = _split_wait(f"gather_{self.tag}_chips_wait", self.send, self.recv, self.bufs, after, _gather_to_chips)
        self.send, self.recv, self.bufs, token = _split_start(f"gather_{self.tag}_sibling_start", bufs, self.k, _gather_to_sibling,
                                                               self.token)
        return token

    def arrived(self, after):
        bufs = _split_wait(f"gather_{self.tag}_sibling_wait", self.send, self.recv, self.bufs, after, _gather_to_sibling)
        return {n: _gathered_to_kernel(self.cfg, n, b) for n, b in zip(self.names, bufs)}


def _step(cfg, a):
    chip = 2 * lax.axis_index("x") + lax.axis_index("y")
    core = lax.axis_index("c")
    big = BIG

    ffn = ("w_gate", "w_up", "w_down")
    first = ("w_in", "w_uq", "w_ukv")
    sp = {n: a[n] for n in SMALL}
    sharded = _pack_small([a[n] for n in SMALL_SHARDED], 2 * LANE)
    slabs = jnp.where(lax.broadcasted_iota(I32, (N_CHIPS,) + sharded.shape, 0) == chip, sharded[None], 0.0)
    staged = {"w_in": _stage_shard("w_in", a["w_in"], chip), "sharded_small": slabs}
    [in_weight] = _LateWeights.start(cfg, {"in": ("w_in", "sharded_small")}, staged, jnp.zeros((8, LANE), F32))
    behind = xn_early = _rms_pre(cfg, a["x"], sp["mix_pre_g"] + in_weight.token[0, 0])
    for n in big[1:]:
        behind = staged[n] = _stage_shard(n, a[n], chip, behind)
    mla_weights, out_weight, ffn_weights, down_weight = _LateWeights.start(
        cfg, {"mla": first[1:], "out": ("w_out",), "ffn": ffn[:2], "down": ffn[2:]}, staged, in_weight.pass_on(behind))
    W = in_weight.arrived(down_weight.token)
    allp = W.pop("sharded_small").reshape((N_CHIPS,) + sharded.shape)
    per_chip = [_unpack_small(allp[ch], [a[n].shape for n in SMALL_SHARDED]) for ch in range(N_CHIPS)]
    for k, n in enumerate(SMALL_SHARDED):
        sp[n] = jnp.concatenate([per_chip[ch][k] for ch in range(N_CHIPS)], axis=1)

    state = {}

    def ffn_grads_ready(grads):
        state["ffn_pairs"], token = _pair_exchange_start("ffn", [_grad_to_chips(cfg, n, grads[n]) for n in ffn_grads])
        return token

    def pair_sums(names, grads, theirs):
        return [_pair_sum(n, g, t, core) for n, g, t in zip(names, grads, theirs)]

    def early_grads_ready(grads):
        out_pairs, token = _pair_exchange_start("out", [_grad_to_chips(cfg, "w_out", grads["w_out"])])
        sums = pair_sums(ffn_grads, *_pair_exchange_wait("ffn", state["ffn_pairs"], token))
        sums += pair_sums(["w_out"], *_pair_exchange_wait("out", out_pairs, sums[-1]))
        state["early"] = _chip_exchange_start("early", sums)
        return state["early"][-1]

    def reduced_halves(tag, names, after):
        send_sems, recv_sems, s_bufs, l_bufs, _ = state[tag]
        s_bufs, l_bufs = _chip_exchange_wait(tag, send_sems, recv_sems, s_bufs, l_bufs, after)
        return [_chip_sum(n, s, t, chip) for n, s, t in zip(names, s_bufs, l_bufs)]

    def in_grad_ready(grads, after):
        if grads is not None:
            state["rest_pairs"], token = _pair_exchange_start("rest", [_grad_to_chips(cfg, n, grads[n]) for n in first])
            return token
        state["rest"] = _chip_exchange_start("rest", pair_sums(first, *_pair_exchange_wait("rest", state["rest_pairs"], after)))
        return state["rest"][-1]

    ffn_grads = ("w_down", "w_gate", "w_up")
    early = ffn_grads + ("w_out",)
    loss, grad_x, gW, gs = _local_grads(cfg, a["x"], a["loss_target"], W, sp, mla_weights, out_weight, ffn_weights, down_weight,
                                        ffn_grads_ready, early_grads_ready, in_grad_ready, xn_early, down_weight.token)
    out = {"grad_x": grad_x}

    def adamw(names, mine, theirs):
        for n, gm, gt in zip(names, mine, theirs):
            out["grad_" + n], out["delta_" + n], out["new_m_" + n], out["new_v_" + n] = _adamw_halves(
                "adamw_" + n, a[n], gm, gt, a["m_" + n], a["v_" + n], core)

    mine = reduced_halves("early", early, grad_x)
    theirs = _sibling_exchange("early", mine[:1])
    later, _ = _sibling_exchange_start("early", mine[1:], theirs[0])
    adamw(early[:1], mine[:1], theirs)
    e_mine, e_theirs = _sibling_exchange_wait("early", later, out["new_v_" + early[0]])
    adamw(early[3:], e_mine[2:], e_theirs[2:])
    mine = reduced_halves("rest", first, out["new_v_" + early[-1]])
    rest, token = _sibling_exchange_start("rest", mine, mine[0])
    small = _allreduce_small_start(_pack_small([gs[n] for n in SMALL] + [loss]), token)
    adamw(early[1:3], e_mine[:2], e_theirs[:2])
    adamw(first, *_sibling_exchange_wait("rest", rest, out["new_v_" + early[2]]))
    shapes = [gs[n].shape for n in SMALL] + [(1, LANE)]
    red = _unpack_small(_allreduce_small_wait(small, chip, core, out["new_v_" + first[-1]]), shapes)
    g_small = dict(zip(SMALL, red[:-1]))
    for n in SMALL_SHARDED:
        cs = a[n].shape[1]
        g_small[n] = lax.dynamic_slice_in_dim(g_small[n], chip * cs, cs, axis=1)
    out["loss"] = red[-1][0, 0]
    sshapes = [a[n].shape for n in SMALL]
    d, nm, nv = _adamw("adamw_small", _pack_small([a[n] for n in SMALL]), _pack_small([g_small[n] for n in SMALL]),
                       _pack_small([a["m_" + n] for n in SMALL]), _pack_small([a["v_" + n] for n in SMALL]))
    for n, dd, mm, vv in zip(SMALL, _unpack_small(d, sshapes), _unpack_small(nm, sshapes), _unpack_small(nv, sshapes)):
        out["grad_" + n], out["delta_" + n], out["new_m_" + n], out["new_v_" + n] = g_small[n], dd, mm, vv
    return out


def kernel(x, mix_pre_g, w_in, q_norm_g, w_uq, kv_norm_g, w_ukv, ssm_conv_w, ssm_conv_b, dt_bias, a_log, d_skip, ssm_norm_g, w_out, mix_post_g, ffn_pre_g, w_gate, w_up, ffn_conv_w, ffn_conv_b, w_down, ffn_post_g, loss_target, m_mix_pre_g, m_w_in, m_q_norm_g, m_w_uq, m_kv_norm_g, m_w_ukv, m_ssm_conv_w, m_ssm_conv_b, m_dt_bias, m_a_log, m_d_skip, m_ssm_norm_g, m_w_out, m_mix_post_g, m_ffn_pre_g, m_w_gate, m_w_up, m_ffn_conv_w, m_ffn_conv_b, m_w_down, m_ffn_post_g, v_mix_pre_g, v_w_in, v_q_norm_g, v_w_uq, v_kv_norm_g, v_w_ukv, v_ssm_conv_w, v_ssm_conv_b, v_dt_bias, v_a_log, v_d_skip, v_ssm_norm_g, v_w_out, v_mix_post_g, v_ffn_pre_g, v_w_gate, v_w_up, v_ffn_conv_w, v_ffn_conv_b, v_w_down, v_ffn_post_g):
    args = dict(locals())
    def given(k, v):
        if k in ("w_in", "m_w_in", "v_w_in"):
            return jnp.transpose(v, (2, 0, 1))
        return v if k.removeprefix("m_").removeprefix("v_") in BIG or v.ndim < 3 else v[0]

    out = _step(_FULL, {k: given(k, v) for k, v in args.items()})
    res = [out["loss"], out["grad_x"][None]]
    for pre in ("grad_", "delta_", "new_m_", "new_v_"):
        for n in WEIGHTS:
            o = out[pre + n]
            res.append(jnp.transpose(o, (1, 2, 0)) if n == "w_in" else o if n in BIG or args[n].ndim < 3 else o[None])
    return tuple(res)
```
